```python
import math
import numpy as np
import jax
import jax.numpy as jnp
from jax import lax


D_MODEL = 2048
BATCH = 8
SEQ = 4096
DEPTH = 1

D_MIX = D_MODEL
SSD_WIDTH = D_MIX // 2
ATT_WIDTH = D_MIX - SSD_WIDTH

SSD_HEAD_DIM = 64
SSD_HEADS = SSD_WIDTH // SSD_HEAD_DIM
SSD_GROUPS = 2
SSD_STATE = 128
SSD_CHUNK = 128
CONV_WIDTH = 4
CONV_CH = SSD_WIDTH + 2 * SSD_GROUPS * SSD_STATE

ATT_HEAD_DIM = 64
ATT_HEADS = ATT_WIDTH // ATT_HEAD_DIM
ATT_KV_HEADS = 4
ATT_GROUP = ATT_HEADS // ATT_KV_HEADS
KV_WIDTH = ATT_KV_HEADS * ATT_HEAD_DIM
CMP_BLOCK = 32
CMP_STRIDE = 16
CMP_HIDDEN = 256
SEL_BLOCK = 64
N_SELECT = 16
WINDOW = 512
Q_BLOCK = 128
N_BRANCH = 3

ROPE_THETA = 500000.0
ROPE_DIM = ATT_HEAD_DIM // 4

D_FF = -((-8 * D_MODEL) // (3 * 256)) * 256

NORM_EPS = 1e-6
NEG_INF = -1e30
FORCE_SCORE = 1e4

IN_WIDTHS = (SSD_WIDTH, CONV_CH, SSD_HEADS, ATT_WIDTH, KV_WIDTH, KV_WIDTH, KV_WIDTH, KV_WIDTH, KV_WIDTH, KV_WIDTH, ATT_HEADS * N_BRANCH)
D_IN = sum(IN_WIDTHS)

kernel_name = 'hymba_ssd_nsa_hybrid_block'


def rmsnorm(x, w):
    xf = x.astype(jnp.float32)
    xf = xf * lax.rsqrt(jnp.mean(xf * xf, axis=-1, keepdims=True) + NORM_EPS)
    return xf.astype(x.dtype) * w


def rope_tables(seq, dtype):
    inv = 1.0 / (ROPE_THETA ** (jnp.arange(0, ROPE_DIM, 2, dtype=jnp.float32) / ROPE_DIM))
    ang = jnp.arange(seq, dtype=jnp.float32)[:, None] * inv[None, :]
    return jnp.cos(ang).astype(dtype), jnp.sin(ang).astype(dtype)


def partial_rope(x, cos, sin):
    half = ROPE_DIM // 2
    c = cos[None, :, None, :]
    s = sin[None, :, None, :]
    x1 = x[..., :half]
    x2 = x[..., half:ROPE_DIM]
    return jnp.concatenate([x1 * c - x2 * s, x2 * c + x1 * s, x[..., ROPE_DIM:]], axis=-1)


def causal_depthwise_conv(u, w, b):
    y = lax.conv_general_dilated(u, w[:, None, :].astype(u.dtype), window_strides=(1,), padding=((CONV_WIDTH - 1, 0),), dimension_numbers=('NWC', 'WIO', 'NWC'), feature_group_count=u.shape[-1])
    return y + b


def ssd_chunked(xdt, adt, bm, cm):
    bsz, seq = xdt.shape[:2]
    nc = seq // SSD_CHUNK
    hg = SSD_HEADS // SSD_GROUPS
    x = xdt.reshape(bsz, nc, SSD_CHUNK, SSD_GROUPS, hg, SSD_HEAD_DIM)
    a = adt.reshape(bsz, nc, SSD_CHUNK, SSD_GROUPS, hg)
    b = bm.reshape(bsz, nc, SSD_CHUNK, SSD_GROUPS, SSD_STATE)
    c = cm.reshape(bsz, nc, SSD_CHUNK, SSD_GROUPS, SSD_STATE)
    a_cum = jnp.cumsum(a, axis=2)
    causal = jnp.tril(jnp.ones((SSD_CHUNK, SSD_CHUNK), dtype=bool))[:, :, None, None]
    seg = a_cum[:, :, :, None] - a_cum[:, :, None, :]
    decay = jnp.exp(jnp.where(causal, seg, -jnp.inf))
    cb = jnp.einsum('bclgn,bcsgn->bclsg', c, b)
    y_diag = jnp.einsum('bclsgh,bcsghp->bclghp', cb[..., None] * decay, x)
    decay_states = jnp.exp(a_cum[:, :, -1:] - a_cum)
    states = jnp.einsum('bclgn,bclghp->bcghpn', b, x * decay_states[..., None])
    chunk_decay = jnp.exp(a_cum[:, :, -1])

    def step(h, inp):
        st, dec = inp
        return h * dec[..., None, None] + st, h

    h0 = jnp.zeros_like(states[:, 0])
    _, prev = lax.scan(step, h0, (jnp.moveaxis(states, 1, 0), jnp.moveaxis(chunk_decay, 1, 0)))
    prev = jnp.moveaxis(prev, 0, 1)
    y_off = jnp.einsum('bclgn,bcghpn->bclghp', c, prev) * jnp.exp(a_cum)[..., None]
    return (y_diag + y_off).reshape(bsz, seq, SSD_HEADS, SSD_HEAD_DIM)


def ssd_mixer(z, xbc, dt_raw, conv_w, conv_b, dt_bias, a_log, d_skip, norm_w):
    bsz, seq = z.shape[:2]
    xbc = jax.nn.silu(causal_depthwise_conv(xbc, conv_w, conv_b))
    xs, bm, cm = jnp.split(xbc, [SSD_WIDTH, SSD_WIDTH + SSD_GROUPS * SSD_STATE], axis=-1)
    xh = xs.reshape(bsz, seq, SSD_HEADS, SSD_HEAD_DIM)
    dt = jax.nn.softplus(dt_raw.astype(jnp.float32) + dt_bias.astype(jnp.float32))
    a = -jnp.exp(a_log.astype(jnp.float32))
    y = ssd_chunked(xh * dt[..., None], a * dt, bm.reshape(bsz, seq, SSD_GROUPS, SSD_STATE), cm.reshape(bsz, seq, SSD_GROUPS, SSD_STATE))
    y = y + d_skip[:, None] * xh
    y = y.reshape(bsz, seq, SSD_WIDTH).astype(z.dtype)
    return rmsnorm(y * jax.nn.silu(z), norm_w)


def selection_overlap(n_cmp, n_sel):
    cs = np.arange(n_cmp)[:, None] * CMP_STRIDE
    ce = cs + CMP_BLOCK
    ss = np.arange(n_sel)[None, :] * SEL_BLOCK
    se = ss + SEL_BLOCK
    ov = np.clip(np.minimum(ce, se) - np.maximum(cs, ss), 0, None)
    return (ov / CMP_BLOCK).astype(np.float32)


def compress_tokens(kv, w1, w2, pe):
    bsz, seq = kv.shape[:2]
    n_cmp = (seq - CMP_BLOCK) // CMP_STRIDE + 1
    idx = np.arange(n_cmp)[:, None] * CMP_STRIDE + np.arange(CMP_BLOCK)[None, :]
    blocks = kv[:, idx] + pe[None, None, :, None, :]
    flat = jnp.swapaxes(blocks, 2, 3).reshape(bsz, n_cmp, ATT_KV_HEADS, CMP_BLOCK * ATT_HEAD_DIM)
    return jax.nn.silu(flat @ w1) @ w2


def nsa_mixer(q, kc, vc, ks, vs, kw, vw, gate_raw, cmp_w1_k, cmp_w2_k, cmp_w1_v, cmp_w2_v, cmp_pe_k, cmp_pe_v, cos, sin):
    bsz, seq = q.shape[:2]
    hd = ATT_HEAD_DIM
    scale = hd ** -0.5
    q = q.reshape(bsz, seq, ATT_HEADS, hd)
    kc, vc, ks, vs, kw, vw = [t.reshape(bsz, seq, ATT_KV_HEADS, hd) for t in (kc, vc, ks, vs, kw, vw)]
    q_grp = q.reshape(bsz, seq, ATT_KV_HEADS, ATT_GROUP, hd)
    q_rot = partial_rope(q, cos, sin).reshape(bsz, seq, ATT_KV_HEADS, ATT_GROUP, hd)
    ks = partial_rope(ks, cos, sin)
    kw = partial_rope(kw, cos, sin)
    t_pos = jnp.arange(seq)

    k_cmp = compress_tokens(kc, cmp_w1_k, cmp_w2_k, cmp_pe_k)
    v_cmp = compress_tokens(vc, cmp_w1_v, cmp_w2_v, cmp_pe_v)
    n_cmp = k_cmp.shape[1]
    cmp_end = jnp.arange(n_cmp) * CMP_STRIDE + CMP_BLOCK - 1
    cmp_mask = cmp_end[None, :] <= t_pos[:, None]
    s_cmp = jnp.einsum('bshgd,bihd->bhgsi', q_grp, k_cmp).astype(jnp.float32) * scale
    p_cmp = jax.nn.softmax(jnp.where(cmp_mask, s_cmp, NEG_INF), axis=-1) * cmp_mask
    o_cmp = jnp.einsum('bhgsi,bihd->bshgd', p_cmp.astype(v_cmp.dtype), v_cmp)

    n_sel = seq // SEL_BLOCK
    k_top = min(N_SELECT, n_sel)
    overlap = jnp.asarray(selection_overlap(n_cmp, n_sel))
    imp = jnp.einsum('bhgsi,ij->bhsj', p_cmp, overlap)
    blk = jnp.arange(n_sel)[None, :]
    cur = (t_pos // SEL_BLOCK)[:, None]
    imp = jnp.where((blk == 0) | (blk == cur) | (blk == cur - 1), FORCE_SCORE, imp)
    imp = jnp.where(blk <= cur, imp, -1.0)
    top_val, top_idx = lax.top_k(imp, k_top)
    top_valid = top_val >= 0.0

    ks_blocks = ks.reshape(bsz, n_sel, SEL_BLOCK, ATT_KV_HEADS, hd).transpose(0, 3, 1, 2, 4)
    vs_blocks = vs.reshape(bsz, n_sel, SEL_BLOCK, ATT_KV_HEADS, hd).transpose(0, 3, 1, 2, 4)
    kw_pad = jnp.pad(kw, ((0, 0), (WINDOW, 0), (0, 0), (0, 0)))
    vw_pad = jnp.pad(vw, ((0, 0), (WINDOW, 0), (0, 0), (0, 0)))
    gather_blocks = jax.vmap(jax.vmap(lambda blocks, ix: blocks[ix]))
    q_offs = jnp.arange(Q_BLOCK)
    w_offs = jnp.arange(WINDOW + Q_BLOCK)
    s_offs = jnp.arange(SEL_BLOCK)

    def query_block(qb):
        s0 = qb * Q_BLOCK
        tq = s0 + q_offs
        qblk = lax.dynamic_slice_in_dim(q_rot, s0, Q_BLOCK, axis=1)
        ib = lax.dynamic_slice_in_dim(top_idx, s0, Q_BLOCK, axis=2)
        vb = lax.dynamic_slice_in_dim(top_valid, s0, Q_BLOCK, axis=2)
        kg = gather_blocks(ks_blocks, ib)
        vg = gather_blocks(vs_blocks, ib)
        kpos = ib[..., None] * SEL_BLOCK + s_offs
        m_sel = (kpos <= tq[:, None, None]) & vb[..., None]
        s_sel = jnp.einsum('bqhgd,bhqjkd->bhgqjk', qblk, kg).astype(jnp.float32) * scale
        s_sel = jnp.where(m_sel[:, :, None], s_sel, NEG_INF).reshape(bsz, ATT_KV_HEADS, ATT_GROUP, Q_BLOCK, k_top * SEL_BLOCK)
        p_sel = jax.nn.softmax(s_sel, axis=-1).astype(vg.dtype)
        o_sel = jnp.einsum('bhgqn,bhqnd->bqhgd', p_sel, vg.reshape(bsz, ATT_KV_HEADS, Q_BLOCK, k_top * SEL_BLOCK, hd))
        kwb = lax.dynamic_slice_in_dim(kw_pad, s0, WINDOW + Q_BLOCK, axis=1)
        vwb = lax.dynamic_slice_in_dim(vw_pad, s0, WINDOW + Q_BLOCK, axis=1)
        kp = s0 - WINDOW + w_offs
        dist = tq[:, None] - kp[None, :]
        m_win = (dist >= 0) & (dist < WINDOW) & (kp[None, :] >= 0)
        s_win = jnp.einsum('bqhgd,bkhd->bhgqk', qblk, kwb).astype(jnp.float32) * scale
        p_win = jax.nn.softmax(jnp.where(m_win, s_win, NEG_INF), axis=-1).astype(vwb.dtype)
        o_win = jnp.einsum('bhgqk,bkhd->bqhgd', p_win, vwb)
        return o_sel, o_win

    o_sel, o_win = lax.map(query_block, jnp.arange(seq // Q_BLOCK))
    o_sel = jnp.moveaxis(o_sel, 0, 1).reshape(bsz, seq, ATT_KV_HEADS, ATT_GROUP, hd)
    o_win = jnp.moveaxis(o_win, 0, 1).reshape(bsz, seq, ATT_KV_HEADS, ATT_GROUP, hd)
    g = jax.nn.sigmoid(gate_raw.astype(jnp.float32)).reshape(bsz, seq, ATT_KV_HEADS, ATT_GROUP, N_BRANCH, 1).astype(q.dtype)
    o = g[..., 0, :] * o_cmp + g[..., 1, :] * o_sel + g[..., 2, :] * o_win
    return o.reshape(bsz, seq, ATT_WIDTH)


def _fwd_setup_inputs(seed: int = 0) -> dict:
    key = jax.random.key(seed)
    k = jax.random.split(key, 24)
    f32 = jnp.float32
    L = DEPTH

    def dense(kk, shape, fan_in):
        return jax.random.normal(kk, shape, f32) * fan_in ** -0.5

    def gain(kk, shape):
        return 1.0 + 0.05 * jax.random.normal(kk, shape, f32)

    dt = jnp.exp(jax.random.uniform(k[5], (L, SSD_HEADS), f32, math.log(1e-3), math.log(1e-1)))
    return {
        'x': jax.random.normal(k[0], (BATCH, SEQ, D_MODEL), f32),
        'attn_norm_w': gain(k[1], (L, D_MODEL)),
        'w_in': dense(k[2], (L, D_MODEL, D_IN), D_MODEL),
        'conv_w': dense(k[3], (L, CONV_WIDTH, CONV_CH), CONV_WIDTH),
        'conv_b': 0.01 * jax.random.normal(k[4], (L, CONV_CH), f32),
        'dt_bias': dt + jnp.log(-jnp.expm1(-dt)),
        'a_log': jnp.log(jax.random.uniform(k[6], (L, SSD_HEADS), f32, 1.0, 16.0)),
        'd_skip': 1.0 + 0.1 * jax.random.normal(k[7], (L, SSD_HEADS), f32),
        'ssd_norm_w': gain(k[8], (L, SSD_WIDTH)),
        'cmp_w1_k': dense(k[9], (L, CMP_BLOCK * ATT_HEAD_DIM, CMP_HIDDEN), CMP_BLOCK * ATT_HEAD_DIM),
        'cmp_w2_k': dense(k[10], (L, CMP_HIDDEN, ATT_HEAD_DIM), CMP_HIDDEN),
        'cmp_w1_v': dense(k[11], (L, CMP_BLOCK * ATT_HEAD_DIM, CMP_HIDDEN), CMP_BLOCK * ATT_HEAD_DIM),
        'cmp_w2_v': dense(k[12], (L, CMP_HIDDEN, ATT_HEAD_DIM), CMP_HIDDEN),
        'cmp_pe_k': 0.1 * jax.random.normal(k[13], (L, CMP_BLOCK, ATT_HEAD_DIM), f32),
        'cmp_pe_v': 0.1 * jax.random.normal(k[14], (L, CMP_BLOCK, ATT_HEAD_DIM), f32),
        'w_out': dense(k[15], (L, D_MIX, D_MODEL), D_MIX),
        'ffn_norm_w': gain(k[16], (L, D_MODEL)),
        'w_gate': dense(k[17], (L, D_MODEL, D_FF), D_MODEL),
        'w_up': dense(k[18], (L, D_MODEL, D_FF), D_MODEL),
        'w_down': dense(k[19], (L, D_FF, D_MODEL), D_FF),
        'final_norm_w': gain(k[20], (D_MODEL,)),
    }


def _fwd_reference(x, attn_norm_w, w_in, conv_w, conv_b, dt_bias, a_log, d_skip, ssd_norm_w, cmp_w1_k, cmp_w2_k, cmp_w1_v, cmp_w2_v, cmp_pe_k, cmp_pe_v, w_out, ffn_norm_w, w_gate, w_up, w_down, final_norm_w):
    cos, sin = rope_tables(x.shape[1], x.dtype)
    split_at = np.cumsum(IN_WIDTHS)[:-1].tolist()
    h = x
    for l in range(DEPTH):
        u = rmsnorm(h, attn_norm_w[l])
        z, xbc, dt_raw, q, kc, vc, ks, vs, kw, vw, gate_raw = jnp.split(u @ w_in[l], split_at, axis=-1)
        y_ssd = ssd_mixer(z, xbc, dt_raw, conv_w[l], conv_b[l], dt_bias[l], a_log[l], d_skip[l], ssd_norm_w[l])
        y_att = nsa_mixer(q, kc, vc, ks, vs, kw, vw, gate_raw, cmp_w1_k[l], cmp_w2_k[l], cmp_w1_v[l], cmp_w2_v[l], cmp_pe_k[l], cmp_pe_v[l], cos, sin)
        mixed = jnp.concatenate([y_ssd.astype(h.dtype), y_att.astype(h.dtype)], axis=-1)
        h = h + mixed @ w_out[l]
        v = rmsnorm(h, ffn_norm_w[l])
        h = h + (jax.nn.silu(v @ w_gate[l]) * (v @ w_up[l])) @ w_down[l]
    return rmsnorm(h, final_norm_w)


import jax as _jax
import jax.numpy as _jnp

TWIN_FORMAT = 'train_step'
FWD_PARAMS = ['x', 'attn_norm_w', 'w_in', 'conv_w', 'conv_b', 'dt_bias', 'a_log', 'd_skip', 'ssd_norm_w', 'cmp_w1_k', 'cmp_w2_k', 'cmp_w1_v', 'cmp_w2_v', 'cmp_pe_k', 'cmp_pe_v', 'w_out', 'ffn_norm_w', 'w_gate', 'w_up', 'w_down', 'final_norm_w']
TWIN_WEIGHTS = ['attn_norm_w', 'w_in', 'conv_w', 'conv_b', 'dt_bias', 'a_log', 'd_skip', 'ssd_norm_w', 'cmp_w1_k', 'cmp_w2_k', 'cmp_w1_v', 'cmp_w2_v', 'cmp_pe_k', 'cmp_pe_v', 'w_out', 'ffn_norm_w', 'w_gate', 'w_up', 'w_down', 'final_norm_w']
TWIN_DIFF_INPUT = 'x'
TWIN_INPUTS = ['x', 'attn_norm_w', 'w_in', 'conv_w', 'conv_b', 'dt_bias', 'a_log', 'd_skip', 'ssd_norm_w', 'cmp_w1_k', 'cmp_w2_k', 'cmp_w1_v', 'cmp_w2_v', 'cmp_pe_k', 'cmp_pe_v', 'w_out', 'ffn_norm_w', 'w_gate', 'w_up', 'w_down', 'final_norm_w', 'loss_target', 'm_attn_norm_w', 'm_w_in', 'm_conv_w', 'm_conv_b', 'm_dt_bias', 'm_a_log', 'm_d_skip', 'm_ssd_norm_w', 'm_cmp_w1_k', 'm_cmp_w2_k', 'm_cmp_w1_v', 'm_cmp_w2_v', 'm_cmp_pe_k', 'm_cmp_pe_v', 'm_w_out', 'm_ffn_norm_w', 'm_w_gate', 'm_w_up', 'm_w_down', 'm_final_norm_w', 'v_attn_norm_w', 'v_w_in', 'v_conv_w', 'v_conv_b', 'v_dt_bias', 'v_a_log', 'v_d_skip', 'v_ssd_norm_w', 'v_cmp_w1_k', 'v_cmp_w2_k', 'v_cmp_w1_v', 'v_cmp_w2_v', 'v_cmp_pe_k', 'v_cmp_pe_v', 'v_w_out', 'v_ffn_norm_w', 'v_w_gate', 'v_w_up', 'v_w_down', 'v_final_norm_w']
TWIN_OUTPUTS = ['loss', 'grad_x', 'grad_attn_norm_w', 'grad_w_in', 'grad_conv_w', 'grad_conv_b', 'grad_dt_bias', 'grad_a_log', 'grad_d_skip', 'grad_ssd_norm_w', 'grad_cmp_w1_k', 'grad_cmp_w2_k', 'grad_cmp_w1_v', 'grad_cmp_w2_v', 'grad_cmp_pe_k', 'grad_cmp_pe_v', 'grad_w_out', 'grad_ffn_norm_w', 'grad_w_gate', 'grad_w_up', 'grad_w_down', 'grad_final_norm_w', 'delta_attn_norm_w', 'delta_w_in', 'delta_conv_w', 'delta_conv_b', 'delta_dt_bias', 'delta_a_log', 'delta_d_skip', 'delta_ssd_norm_w', 'delta_cmp_w1_k', 'delta_cmp_w2_k', 'delta_cmp_w1_v', 'delta_cmp_w2_v', 'delta_cmp_pe_k', 'delta_cmp_pe_v', 'delta_w_out', 'delta_ffn_norm_w', 'delta_w_gate', 'delta_w_up', 'delta_w_down', 'delta_final_norm_w', 'new_m_attn_norm_w', 'new_m_w_in', 'new_m_conv_w', 'new_m_conv_b', 'new_m_dt_bias', 'new_m_a_log', 'new_m_d_skip', 'new_m_ssd_norm_w', 'new_m_cmp_w1_k', 'new_m_cmp_w2_k', 'new_m_cmp_w1_v', 'new_m_cmp_w2_v', 'new_m_cmp_pe_k', 'new_m_cmp_pe_v', 'new_m_w_out', 'new_m_ffn_norm_w', 'new_m_w_gate', 'new_m_w_up', 'new_m_w_down', 'new_m_final_norm_w', 'new_v_attn_norm_w', 'new_v_w_in', 'new_v_conv_w', 'new_v_conv_b', 'new_v_dt_bias', 'new_v_a_log', 'new_v_d_skip', 'new_v_ssd_norm_w', 'new_v_cmp_w1_k', 'new_v_cmp_w2_k', 'new_v_cmp_w1_v', 'new_v_cmp_w2_v', 'new_v_cmp_pe_k', 'new_v_cmp_pe_v', 'new_v_w_out', 'new_v_ffn_norm_w', 'new_v_w_gate', 'new_v_w_up', 'new_v_w_down', 'new_v_final_norm_w']
TWIN_LEAF_KINDS = {'loss': 'loss', 'grad_x': 'grad_x', 'grad_attn_norm_w': 'grad_w', 'grad_w_in': 'grad_w', 'grad_conv_w': 'grad_w', 'grad_conv_b': 'grad_w', 'grad_dt_bias': 'grad_w', 'grad_a_log': 'grad_w', 'grad_d_skip': 'grad_w', 'grad_ssd_norm_w': 'grad_w', 'grad_cmp_w1_k': 'grad_w', 'grad_cmp_w2_k': 'grad_w', 'grad_cmp_w1_v': 'grad_w', 'grad_cmp_w2_v': 'grad_w', 'grad_cmp_pe_k': 'grad_w', 'grad_cmp_pe_v': 'grad_w', 'grad_w_out': 'grad_w', 'grad_ffn_norm_w': 'grad_w', 'grad_w_gate': 'grad_w', 'grad_w_up': 'grad_w', 'grad_w_down': 'grad_w', 'grad_final_norm_w': 'grad_w', 'delta_attn_norm_w': 'delta_w', 'delta_w_in': 'delta_w', 'delta_conv_w': 'delta_w', 'delta_conv_b': 'delta_w', 'delta_dt_bias': 'delta_w', 'delta_a_log': 'delta_w', 'delta_d_skip': 'delta_w', 'delta_ssd_norm_w': 'delta_w', 'delta_cmp_w1_k': 'delta_w', 'delta_cmp_w2_k': 'delta_w', 'delta_cmp_w1_v': 'delta_w', 'delta_cmp_w2_v': 'delta_w', 'delta_cmp_pe_k': 'delta_w', 'delta_cmp_pe_v': 'delta_w', 'delta_w_out': 'delta_w', 'delta_ffn_norm_w': 'delta_w', 'delta_w_gate': 'delta_w', 'delta_w_up': 'delta_w', 'delta_w_down': 'delta_w', 'delta_final_norm_w': 'delta_w', 'new_m_attn_norm_w': 'new_m', 'new_m_w_in': 'new_m', 'new_m_conv_w': 'new_m', 'new_m_conv_b': 'new_m', 'new_m_dt_bias': 'new_m', 'new_m_a_log': 'new_m', 'new_m_d_skip': 'new_m', 'new_m_ssd_norm_w': 'new_m', 'new_m_cmp_w1_k': 'new_m', 'new_m_cmp_w2_k': 'new_m', 'new_m_cmp_w1_v': 'new_m', 'new_m_cmp_w2_v': 'new_m', 'new_m_cmp_pe_k': 'new_m', 'new_m_cmp_pe_v': 'new_m', 'new_m_w_out': 'new_m', 'new_m_ffn_norm_w': 'new_m', 'new_m_w_gate': 'new_m', 'new_m_w_up': 'new_m', 'new_m_w_down': 'new_m', 'new_m_final_norm_w': 'new_m', 'new_v_attn_norm_w': 'new_v', 'new_v_w_in': 'new_v', 'new_v_conv_w': 'new_v', 'new_v_conv_b': 'new_v', 'new_v_dt_bias': 'new_v', 'new_v_a_log': 'new_v', 'new_v_d_skip': 'new_v', 'new_v_ssd_norm_w': 'new_v', 'new_v_cmp_w1_k': 'new_v', 'new_v_cmp_w2_k': 'new_v', 'new_v_cmp_w1_v': 'new_v', 'new_v_cmp_w2_v': 'new_v', 'new_v_cmp_pe_k': 'new_v', 'new_v_cmp_pe_v': 'new_v', 'new_v_w_out': 'new_v', 'new_v_ffn_norm_w': 'new_v', 'new_v_w_gate': 'new_v', 'new_v_w_up': 'new_v', 'new_v_w_down': 'new_v', 'new_v_final_norm_w': 'new_v'}


def _forward(args):
    return _fwd_reference(*[args[k] for k in FWD_PARAMS])


def _output_shape():
    def fwd():
        inp = _fwd_setup_inputs(0)
        return _fwd_reference(*[inp[k] for k in FWD_PARAMS])
    out = _jax.eval_shape(fwd)
    return out.shape, out.dtype

N_MICROBATCH = 1
ADAM_LR = 0.001
ADAM_B1 = 0.9
ADAM_B2 = 0.999
ADAM_EPS = 1e-08
ADAM_WD = 0.01
ADAM_STEP = 10
PER_EXAMPLE_BATCH_AXIS = {'x': 0, 'loss_target': 0}
SHARED_INPUTS = []
_WEIGHT_DTYPES = {'attn_norm_w': _jnp.float32, 'w_in': _jnp.float32, 'conv_w': _jnp.float32, 'conv_b': _jnp.float32, 'dt_bias': _jnp.float32, 'a_log': _jnp.float32, 'd_skip': _jnp.float32, 'ssd_norm_w': _jnp.float32, 'cmp_w1_k': _jnp.float32, 'cmp_w2_k': _jnp.float32, 'cmp_w1_v': _jnp.float32, 'cmp_w2_v': _jnp.float32, 'cmp_pe_k': _jnp.float32, 'cmp_pe_v': _jnp.float32, 'w_out': _jnp.float32, 'ffn_norm_w': _jnp.float32, 'w_gate': _jnp.float32, 'w_up': _jnp.float32, 'w_down': _jnp.float32, 'final_norm_w': _jnp.float32}
MOMENT_SCALE = {'attn_norm_w': 9.031122e-02, 'w_in': 5.564865e-02, 'conv_w': 7.208101e-02, 'conv_b': 8.826989e-02, 'dt_bias': 2.059200e-01, 'a_log': 2.450868e-01, 'd_skip': 4.566443e-01, 'ssd_norm_w': 8.660619e-02, 'cmp_w1_k': 9.545100e-03, 'cmp_w2_k': 1.742377e-02, 'cmp_w1_v': 2.552285e-02, 'cmp_w2_v': 6.013136e-02, 'cmp_pe_k': 1.916338e-03, 'cmp_pe_v': 2.402709e-02, 'w_out': 5.915298e-02, 'ffn_norm_w': 5.812467e-02, 'w_gate': 2.447083e-02, 'w_up': 2.387378e-02, 'w_down': 3.970431e-02, 'final_norm_w': 1.605189e+01}


def _to_microbatches(a, axis):
    t = _jnp.moveaxis(a, axis, 0)
    t = t.reshape((N_MICROBATCH, t.shape[0] // N_MICROBATCH) + t.shape[1:])
    return _jnp.moveaxis(t, 1, axis + 1)


def setup_inputs(seed: int = 0) -> dict:
    inp = _fwd_setup_inputs(seed)
    key = _jax.random.fold_in(_jax.random.key(seed), 7919)
    shape, _ = _output_shape()
    out = dict(inp)
    out["loss_target"] = _jax.random.normal(_jax.random.fold_in(key, 0), shape, _jnp.float32)
    for i, name in enumerate(TWIN_WEIGHTS):
        w = inp[name].astype(_jnp.float32)
        if MOMENT_SCALE is None:
            s = _jnp.sqrt(_jnp.mean(_jnp.square(w)) + 1e-30)
        else:
            s = MOMENT_SCALE[name]
        km, kv = _jax.random.split(_jax.random.fold_in(key, i + 1))
        out[name] = w
        out["m_" + name] = s * _jax.random.normal(km, w.shape, _jnp.float32)
        out["v_" + name] = (s * s) * _jax.random.uniform(kv, w.shape, _jnp.float32, 0.5, 1.5)
    if N_MICROBATCH > 1:
        for name, axis in PER_EXAMPLE_BATCH_AXIS.items():
            out[name] = _to_microbatches(out[name], axis)
    return {'x': out['x'], 'attn_norm_w': out['attn_norm_w'], 'w_in': out['w_in'], 'conv_w': out['conv_w'], 'conv_b': out['conv_b'], 'dt_bias': out['dt_bias'], 'a_log': out['a_log'], 'd_skip': out['d_skip'], 'ssd_norm_w': out['ssd_norm_w'], 'cmp_w1_k': out['cmp_w1_k'], 'cmp_w2_k': out['cmp_w2_k'], 'cmp_w1_v': out['cmp_w1_v'], 'cmp_w2_v': out['cmp_w2_v'], 'cmp_pe_k': out['cmp_pe_k'], 'cmp_pe_v': out['cmp_pe_v'], 'w_out': out['w_out'], 'ffn_norm_w': out['ffn_norm_w'], 'w_gate': out['w_gate'], 'w_up': out['w_up'], 'w_down': out['w_down'], 'final_norm_w': out['final_norm_w'], 'loss_target': out['loss_target'], 'm_attn_norm_w': out['m_attn_norm_w'], 'm_w_in': out['m_w_in'], 'm_conv_w': out['m_conv_w'], 'm_conv_b': out['m_conv_b'], 'm_dt_bias': out['m_dt_bias'], 'm_a_log': out['m_a_log'], 'm_d_skip': out['m_d_skip'], 'm_ssd_norm_w': out['m_ssd_norm_w'], 'm_cmp_w1_k': out['m_cmp_w1_k'], 'm_cmp_w2_k': out['m_cmp_w2_k'], 'm_cmp_w1_v': out['m_cmp_w1_v'], 'm_cmp_w2_v': out['m_cmp_w2_v'], 'm_cmp_pe_k': out['m_cmp_pe_k'], 'm_cmp_pe_v': out['m_cmp_pe_v'], 'm_w_out': out['m_w_out'], 'm_ffn_norm_w': out['m_ffn_norm_w'], 'm_w_gate': out['m_w_gate'], 'm_w_up': out['m_w_up'], 'm_w_down': out['m_w_down'], 'm_final_norm_w': out['m_final_norm_w'], 'v_attn_norm_w': out['v_attn_norm_w'], 'v_w_in': out['v_w_in'], 'v_conv_w': out['v_conv_w'], 'v_conv_b': out['v_conv_b'], 'v_dt_bias': out['v_dt_bias'], 'v_a_log': out['v_a_log'], 'v_d_skip': out['v_d_skip'], 'v_ssd_norm_w': out['v_ssd_norm_w'], 'v_cmp_w1_k': out['v_cmp_w1_k'], 'v_cmp_w2_k': out['v_cmp_w2_k'], 'v_cmp_w1_v': out['v_cmp_w1_v'], 'v_cmp_w2_v': out['v_cmp_w2_v'], 'v_cmp_pe_k': out['v_cmp_pe_k'], 'v_cmp_pe_v': out['v_cmp_pe_v'], 'v_w_out': out['v_w_out'], 'v_ffn_norm_w': out['v_ffn_norm_w'], 'v_w_gate': out['v_w_gate'], 'v_w_up': out['v_w_up'], 'v_w_down': out['v_w_down'], 'v_final_norm_w': out['v_final_norm_w']}


def _loss(weights, diff, rest, loss_target):
    with _jax.named_scope("forward"):
        args = {**rest, TWIN_DIFF_INPUT: diff, **{k: w.astype(_WEIGHT_DTYPES[k]) for k, w in weights.items()}}
        y = _forward(args)
    with _jax.named_scope("loss_head"):
        err = _jnp.square(y.astype(_jnp.float32) - loss_target)
        return 0.5 * _jnp.sum(_jnp.mean(err, axis=-1)) if err.ndim else 0.5 * err


def _adamw(w, g, m, v):
    m = ADAM_B1 * m + (1.0 - ADAM_B1) * g
    v = ADAM_B2 * v + (1.0 - ADAM_B2) * _jnp.square(g)
    m_hat = m / (1.0 - ADAM_B1 ** ADAM_STEP)
    v_hat = v / (1.0 - ADAM_B2 ** ADAM_STEP)
    delta = -ADAM_LR * (m_hat / (_jnp.sqrt(v_hat) + ADAM_EPS) + ADAM_WD * w)
    return delta, m, v


def reference(x, attn_norm_w, w_in, conv_w, conv_b, dt_bias, a_log, d_skip, ssd_norm_w, cmp_w1_k, cmp_w2_k, cmp_w1_v, cmp_w2_v, cmp_pe_k, cmp_pe_v, w_out, ffn_norm_w, w_gate, w_up, w_down, final_norm_w, loss_target, m_attn_norm_w, m_w_in, m_conv_w, m_conv_b, m_dt_bias, m_a_log, m_d_skip, m_ssd_norm_w, m_cmp_w1_k, m_cmp_w2_k, m_cmp_w1_v, m_cmp_w2_v, m_cmp_pe_k, m_cmp_pe_v, m_w_out, m_ffn_norm_w, m_w_gate, m_w_up, m_w_down, m_final_norm_w, v_attn_norm_w, v_w_in, v_conv_w, v_conv_b, v_dt_bias, v_a_log, v_d_skip, v_ssd_norm_w, v_cmp_w1_k, v_cmp_w2_k, v_cmp_w1_v, v_cmp_w2_v, v_cmp_pe_k, v_cmp_pe_v, v_w_out, v_ffn_norm_w, v_w_gate, v_w_up, v_w_down, v_final_norm_w):
    given = dict(x=x, attn_norm_w=attn_norm_w, w_in=w_in, conv_w=conv_w, conv_b=conv_b, dt_bias=dt_bias, a_log=a_log, d_skip=d_skip, ssd_norm_w=ssd_norm_w, cmp_w1_k=cmp_w1_k, cmp_w2_k=cmp_w2_k, cmp_w1_v=cmp_w1_v, cmp_w2_v=cmp_w2_v, cmp_pe_k=cmp_pe_k, cmp_pe_v=cmp_pe_v, w_out=w_out, ffn_norm_w=ffn_norm_w, w_gate=w_gate, w_up=w_up, w_down=w_down, final_norm_w=final_norm_w, loss_target=loss_target, m_attn_norm_w=m_attn_norm_w, m_w_in=m_w_in, m_conv_w=m_conv_w, m_conv_b=m_conv_b, m_dt_bias=m_dt_bias, m_a_log=m_a_log, m_d_skip=m_d_skip, m_ssd_norm_w=m_ssd_norm_w, m_cmp_w1_k=m_cmp_w1_k, m_cmp_w2_k=m_cmp_w2_k, m_cmp_w1_v=m_cmp_w1_v, m_cmp_w2_v=m_cmp_w2_v, m_cmp_pe_k=m_cmp_pe_k, m_cmp_pe_v=m_cmp_pe_v, m_w_out=m_w_out, m_ffn_norm_w=m_ffn_norm_w, m_w_gate=m_w_gate, m_w_up=m_w_up, m_w_down=m_w_down, m_final_norm_w=m_final_norm_w, v_attn_norm_w=v_attn_norm_w, v_w_in=v_w_in, v_conv_w=v_conv_w, v_conv_b=v_conv_b, v_dt_bias=v_dt_bias, v_a_log=v_a_log, v_d_skip=v_d_skip, v_ssd_norm_w=v_ssd_norm_w, v_cmp_w1_k=v_cmp_w1_k, v_cmp_w2_k=v_cmp_w2_k, v_cmp_w1_v=v_cmp_w1_v, v_cmp_w2_v=v_cmp_w2_v, v_cmp_pe_k=v_cmp_pe_k, v_cmp_pe_v=v_cmp_pe_v, v_w_out=v_w_out, v_ffn_norm_w=v_ffn_norm_w, v_w_gate=v_w_gate, v_w_up=v_w_up, v_w_down=v_w_down, v_final_norm_w=v_final_norm_w)
    weights = {n: given[n] for n in TWIN_WEIGHTS}
    shared = {n: given[n] for n in SHARED_INPUTS}
    per_example = {n: given[n] for n in ['x']}
    grad_fn = _jax.value_and_grad(_loss, argnums=(0, 1))

    def one_microbatch(ex, loss_target):
        ex = dict(ex)
        diff = ex.pop(TWIN_DIFF_INPUT)
        return grad_fn(weights, diff, {**shared, **ex}, loss_target)

    if N_MICROBATCH == 1:
        loss, (grad_w, grad_x) = one_microbatch(per_example, given["loss_target"])
    else:
        def body(carry, xs):
            loss_sum, grad_sum = carry
            l_k, (gw_k, gx_k) = one_microbatch(xs[0], xs[1])
            with _jax.named_scope("update"):
                return (loss_sum + l_k, _jax.tree.map(_jnp.add, grad_sum, gw_k)), gx_k

        init = (_jnp.zeros((), _jnp.float32), _jax.tree.map(_jnp.zeros_like, weights))
        (loss, grad_w), grad_x = _jax.lax.scan(body, init, (per_example, given["loss_target"]))
    with _jax.named_scope("update"):
        delta_w, new_m, new_v = {}, {}, {}
        for n in TWIN_WEIGHTS:
            delta_w[n], new_m[n], new_v[n] = _adamw(weights[n], grad_w[n], given["m_" + n], given["v_" + n])
    return (loss, grad_x, *[grad_w[n] for n in TWIN_WEIGHTS], *[delta_w[n] for n in TWIN_WEIGHTS],
            *[new_m[n] for n in TWIN_WEIGHTS], *[new_v[n] for n in TWIN_WEIGHTS])
```

```python
import functools
import math

import numpy as np
import jax
import jax.numpy as jnp
from jax import lax
from jax.experimental import pallas as pl
from jax.experimental.pallas import tpu as pltpu

F32 = jnp.float32
_MXU = jnp.bfloat16
_HI = lax.Precision.HIGHEST

N_DEV = 8
D_MODEL = 2048
SSD_WIDTH = 1024
ATT_WIDTH = 1024
SSD_HEADS = 16
SSD_P = 64
SSD_N = 128
SSD_L = 128
SSD_G = 2
CONV_CH = 1536
CONV_K = 4
HD = 64
N_HEADS = 16
N_KV = 4
GRP = 4
CMP_HID = 256
SEL_BLOCK = 64
N_SELECT = 16
WINDOW = 512
ROPE_DIM = 16
ROPE_THETA = 500000.0
D_FF = 5632
EPS = 1e-6
NEG = -1e30
FORCE = 1e4
SCALE = HD ** -0.5
D_IN = 5184
W_MAIN = 5120
W_SMALL = 128
VMEM_LIMIT = 52 * 1024 * 1024

ADAM_LR, ADAM_B1, ADAM_B2, ADAM_EPS, ADAM_WD, ADAM_STEP = 0.001, 0.9, 0.999, 1e-08, 0.01, 10


def _pick(n, cands):
    for c in cands:
        if n % c == 0:
            return c
    return n


def _cp(sem=None):
    return pltpu.CompilerParams(dimension_semantics=sem, vmem_limit_bytes=VMEM_LIMIT)


def _sigmoid(x):
    return 1.0 / (1.0 + jnp.exp(-x))


def _dot(a, b, dims, hi=False):
    dn = {"nn": (((1,), (0,)), ((), ())), "nt": (((1,), (1,)), ((), ())), "tn": (((0,), (0,)), ((), ()))}[dims]
    if hi:
        return lax.dot_general(a.astype(F32), b.astype(F32), dn, precision=_HI, preferred_element_type=F32)
    return lax.dot_general(a.astype(_MXU), b.astype(_MXU), dn, preferred_element_type=F32)


def _mm(a, b, mode, out_dtype, name, res=None):
    if mode == "nn":
        (M, K), N = a.shape, b.shape[1]
    elif mode == "nt":
        (M, K), N = a.shape, b.shape[0]
    else:
        (K, M), N = a.shape, b.shape[1]
    tm, tn, tk = _pick(M, (1024, 512, 256, 128)), _pick(N, (1024, 512, 256, 128)), _pick(K, (512, 256, 128))
    nk = K // tk
    a_spec = pl.BlockSpec((tk, tm), lambda i, j, k: (k, i)) if mode == "tn" else pl.BlockSpec((tm, tk), lambda i, j, k: (i, k))
    b_spec = pl.BlockSpec((tn, tk), lambda i, j, k: (j, k)) if mode == "nt" else pl.BlockSpec((tk, tn), lambda i, j, k: (k, j))
    o_spec = pl.BlockSpec((tm, tn), lambda i, j, k: (i, j))

    def body(*refs):
        if res is None:
            a_ref, b_ref, o_ref, acc = refs
        else:
            a_ref, b_ref, r_ref, o_ref, acc = refs
        k = pl.program_id(2)

        @pl.when(k == 0)
        def _():
            acc[...] = jnp.zeros_like(acc)

        acc[...] += _dot(a_ref[...], b_ref[...], mode)

        @pl.when(k == nk - 1)
        def _():
            r = acc[...]
            if res is not None:
                r = r + r_ref[...].astype(F32)
            o_ref[...] = r.astype(out_dtype)

    ins, specs = [a, b], [a_spec, b_spec]
    if res is not None:
        ins.append(res)
        specs.append(o_spec)
    return pl.pallas_call(
        body, name=name, grid=(M // tm, N // tn, nk), in_specs=specs, out_specs=o_spec,
        out_shape=jax.ShapeDtypeStruct((M, N), out_dtype), scratch_shapes=[pltpu.VMEM((tm, tn), F32)],
        compiler_params=_cp(("parallel", "parallel", "arbitrary")))(*ins)


def _ffn_up(v, w_gate, w_up):
    S, D = v.shape
    F = w_gate.shape[1]
    tm, tn, tk = _pick(S, (1024, 512, 256, 128)), _pick(F, (512, 256, 128)), _pick(D, (512, 256, 128))
    nk = D // tk

    def body(v_ref, wg_ref, wu_ref, gt_ref, up_ref, act_ref, accg, accu):
        k = pl.program_id(2)

        @pl.when(k == 0)
        def _():
            accg[...] = jnp.zeros_like(accg)
            accu[...] = jnp.zeros_like(accu)

        vv = v_ref[...]
        accg[...] += _dot(vv, wg_ref[...], "nn")
        accu[...] += _dot(vv, wu_ref[...], "nn")

        @pl.when(k == nk - 1)
        def _():
            g, u = accg[...], accu[...]
            gt_ref[...] = g
            up_ref[...] = u
            act_ref[...] = (g * _sigmoid(g) * u).astype(act_ref.dtype)

    o_spec = pl.BlockSpec((tm, tn), lambda i, j, k: (i, j))
    w_spec = pl.BlockSpec((tk, tn), lambda i, j, k: (k, j))
    return pl.pallas_call(
        body, name="ffn_up", grid=(S // tm, F // tn, nk),
        in_specs=[pl.BlockSpec((tm, tk), lambda i, j, k: (i, k)), w_spec, w_spec], out_specs=[o_spec, o_spec, o_spec],
        out_shape=[jax.ShapeDtypeStruct((S, F), F32), jax.ShapeDtypeStruct((S, F), F32), jax.ShapeDtypeStruct((S, F), _MXU)],
        scratch_shapes=[pltpu.VMEM((tm, tn), F32), pltpu.VMEM((tm, tn), F32)],
        compiler_params=_cp(("parallel", "parallel", "arbitrary")))(v, w_gate, w_up)


def _ffn_dact(dh2, w_down, gt, up):
    S, D = dh2.shape
    F = w_down.shape[0]
    tm, tn, tk = _pick(S, (1024, 512, 256, 128)), _pick(F, (512, 256, 128)), _pick(D, (512, 256, 128))
    nk = D // tk

    def body(d_ref, w_ref, gt_ref, up_ref, dg_ref, du_ref, acc):
        k = pl.program_id(2)

        @pl.when(k == 0)
        def _():
            acc[...] = jnp.zeros_like(acc)

        acc[...] += _dot(d_ref[...], w_ref[...], "nt")

        @pl.when(k == nk - 1)
        def _():
            da, g, u = acc[...], gt_ref[...], up_ref[...]
            s = _sigmoid(g)
            dg_ref[...] = (da * u * (s * (1.0 + g * (1.0 - s)))).astype(dg_ref.dtype)
            du_ref[...] = (da * (g * s)).astype(du_ref.dtype)

    o_spec = pl.BlockSpec((tm, tn), lambda i, j, k: (i, j))
    return pl.pallas_call(
        body, name="ffn_dact", grid=(S // tm, F // tn, nk),
        in_specs=[pl.BlockSpec((tm, tk), lambda i, j, k: (i, k)), pl.BlockSpec((tn, tk), lambda i, j, k: (j, k)), o_spec, o_spec],
        out_specs=[o_spec, o_spec],
        out_shape=[jax.ShapeDtypeStruct((S, F), _MXU), jax.ShapeDtypeStruct((S, F), _MXU)],
        scratch_shapes=[pltpu.VMEM((tm, tn), F32)],
        compiler_params=_cp(("parallel", "parallel", "arbitrary")))(dh2, w_down, gt, up)


def _rms_fwd(x, w, name):
    S, D = x.shape
    tr = _pick(S, (256, 128))

    def body(x_ref, w_ref, xn_ref, rs_ref):
        xv = x_ref[...]
        rs = lax.rsqrt(jnp.mean(xv * xv, axis=-1, keepdims=True) + EPS)
        xn_ref[...] = ((xv * rs) * w_ref[...]).astype(xn_ref.dtype)
        rs_ref[...] = rs

    return pl.pallas_call(
        body, name=name, grid=(S // tr,),
        in_specs=[pl.BlockSpec((tr, D), lambda i: (i, 0)), pl.BlockSpec((1, D), lambda i: (0, 0))],
        out_specs=[pl.BlockSpec((tr, D), lambda i: (i, 0)), pl.BlockSpec((tr, 1), lambda i: (i, 0))],
        out_shape=[jax.ShapeDtypeStruct((S, D), _MXU), jax.ShapeDtypeStruct((S, 1), F32)],
        compiler_params=_cp(("parallel",)))(x, w)


def _rms_bwd(dyn, x, rs, w, res, name):
    S, D = x.shape
    tr = _pick(S, (256, 128))

    def body(dy_ref, x_ref, rs_ref, w_ref, res_ref, dx_ref, dw_ref):
        @pl.when(pl.program_id(0) == 0)
        def _():
            dw_ref[...] = jnp.zeros_like(dw_ref)

        dy, r = dy_ref[...].astype(F32), rs_ref[...]
        xhat = x_ref[...] * r
        dw_ref[...] += jnp.sum(dy * xhat, axis=0, keepdims=True)
        dxhat = dy * w_ref[...]
        dx_ref[...] = res_ref[...] + r * (dxhat - xhat * jnp.mean(dxhat * xhat, axis=-1, keepdims=True))

    row = pl.BlockSpec((tr, D), lambda i: (i, 0))
    vec = pl.BlockSpec((1, D), lambda i: (0, 0))
    return pl.pallas_call(
        body, name=name, grid=(S // tr,),
        in_specs=[row, row, pl.BlockSpec((tr, 1), lambda i: (i, 0)), vec, row], out_specs=[row, vec],
        out_shape=[jax.ShapeDtypeStruct((S, D), F32), jax.ShapeDtypeStruct((1, D), F32)],
        compiler_params=_cp(("arbitrary",)))(dyn, x, rs, w, res)


def _final_loss(h2, w, tgt):
    S, D = h2.shape
    tr = _pick(S, (256, 128))

    def body(h_ref, w_ref, t_ref, loss_ref, dh_ref, dw_ref):
        @pl.when(pl.program_id(0) == 0)
        def _():
            dw_ref[...] = jnp.zeros_like(dw_ref)
            loss_ref[...] = jnp.zeros_like(loss_ref)

        hv, wv = h_ref[...], w_ref[...]
        rs = lax.rsqrt(jnp.mean(hv * hv, axis=-1, keepdims=True) + EPS)
        xhat = hv * rs
        err = xhat * wv - t_ref[...]
        row = jnp.mean(err * err, axis=-1, keepdims=True)
        loss_ref[...] += 0.5 * jnp.sum(row, axis=0, keepdims=True)
        dy = err * (1.0 / D)
        dw_ref[...] += jnp.sum(dy * xhat, axis=0, keepdims=True)
        dxhat = dy * wv
        dh_ref[...] = rs * (dxhat - xhat * jnp.mean(dxhat * xhat, axis=-1, keepdims=True))

    row = pl.BlockSpec((tr, D), lambda i: (i, 0))
    vec = pl.BlockSpec((1, D), lambda i: (0, 0))
    return pl.pallas_call(
        body, name="final_loss", grid=(S // tr,), in_specs=[row, vec, row],
        out_specs=[pl.BlockSpec((1, 1), lambda i: (0, 0)), row, vec],
        out_shape=[jax.ShapeDtypeStruct((1, 1), F32), jax.ShapeDtypeStruct((S, D), F32), jax.ShapeDtypeStruct((1, D), F32)],
        compiler_params=_cp(("arbitrary",)))(h2, w, tgt)


def _shift_rows(x, k, rows):
    if k == 0:
        return x
    S = x.shape[0]
    r = pltpu.roll(x, k % S, axis=0)
    ok = (rows >= k) if k > 0 else (rows < S + k)
    return jnp.where(ok, r, 0.0)


XBC_COL0 = SSD_WIDTH // 128


def _conv_fwd(proj, conv_w, conv_b):
    S = proj.shape[0]
    nct = CONV_CH // 128

    def body(x_ref, w_ref, b_ref, o_ref):
        x = x_ref[...]
        rows = lax.broadcasted_iota(jnp.int32, x.shape, 0)
        c = b_ref[...] + w_ref[3:4, :] * x
        for k in range(1, CONV_K):
            c = c + w_ref[3 - k:4 - k, :] * _shift_rows(x, k, rows)
        o_ref[...] = c * _sigmoid(c)

    return pl.pallas_call(
        body, name="conv_fwd", grid=(nct,),
        in_specs=[pl.BlockSpec((S, 128), lambda j: (0, XBC_COL0 + j)), pl.BlockSpec((CONV_K, 128), lambda j: (0, j)),
                  pl.BlockSpec((1, 128), lambda j: (0, j))],
        out_specs=pl.BlockSpec((S, 128), lambda j: (0, j)),
        out_shape=jax.ShapeDtypeStruct((S, CONV_CH), F32), compiler_params=_cp(("parallel",)))(proj, conv_w, conv_b)


def _conv_bwd(proj, conv_w, conv_b, dxa):
    S = proj.shape[0]
    nct = CONV_CH // 128

    def body(x_ref, w_ref, b_ref, d_ref, dx_ref, dw_ref, db_ref):
        x = x_ref[...]
        rows = lax.broadcasted_iota(jnp.int32, x.shape, 0)
        xs = [_shift_rows(x, k, rows) for k in range(CONV_K)]
        c = b_ref[...] + w_ref[3:4, :] * x
        for k in range(1, CONV_K):
            c = c + w_ref[3 - k:4 - k, :] * xs[k]
        s = _sigmoid(c)
        dc = d_ref[...] * (s * (1.0 + c * (1.0 - s)))
        dx = w_ref[3:4, :] * dc
        for k in range(1, CONV_K):
            dx = dx + w_ref[3 - k:4 - k, :] * _shift_rows(dc, -k, rows)
        dx_ref[...] = dx.astype(dx_ref.dtype)
        for k in range(CONV_K):
            dw_ref[3 - k:4 - k, :] = jnp.sum(dc * xs[k], axis=0, keepdims=True)
        db_ref[...] = jnp.sum(dc, axis=0, keepdims=True)

    col = pl.BlockSpec((S, 128), lambda j: (0, j))
    return pl.pallas_call(
        body, name="conv_bwd", grid=(nct,),
        in_specs=[pl.BlockSpec((S, 128), lambda j: (0, XBC_COL0 + j)), pl.BlockSpec((CONV_K, 128), lambda j: (0, j)),
                  pl.BlockSpec((1, 128), lambda j: (0, j)), col],
        out_specs=[col, pl.BlockSpec((CONV_K, 128), lambda j: (0, j)), pl.BlockSpec((1, 128), lambda j: (0, j))],
        out_shape=[jax.ShapeDtypeStruct((S, CONV_CH), _MXU), jax.ShapeDtypeStruct((CONV_K, CONV_CH), F32),
                   jax.ShapeDtypeStruct((1, CONV_CH), F32)],
        compiler_params=_cp(("parallel",)))(proj, conv_w, conv_b, dxa)


def _ssd_consts():
    L = SSD_L
    r = lax.broadcasted_iota(jnp.int32, (L, L), 0)
    c = lax.broadcasted_iota(jnp.int32, (L, L), 1)
    causal = r >= c
    upper = (r <= c).astype(F32)
    hr = lax.broadcasted_iota(jnp.int32, (SSD_HEADS, SSD_WIDTH), 0)
    hc = lax.broadcasted_iota(jnp.int32, (SSD_HEADS, SSD_WIDTH), 1)
    expand = (lax.shift_right_logical(hc, 6) == hr).astype(F32)
    return causal, causal.astype(F32), upper, expand


def _softplus(x):
    return jnp.maximum(x, 0.0) + jnp.log(1.0 + jnp.exp(-jnp.abs(x)))


def _ssd_scalars(dtr, dt_bias, a_log, tri, upper, expand):
    dt = _softplus(dtr + dt_bias)
    A = -jnp.exp(a_log)
    adt = dt * A
    acum = _dot(tri, adt, "nn", hi=True)
    acum_t = _dot(adt, upper, "tn", hi=True)
    alast = acum[SSD_L - 1:SSD_L, :]
    e = jnp.exp(acum)
    wdec = jnp.exp(alast - acum)
    gam = jnp.exp(alast)
    ex = lambda t: _dot(t, expand, "nn", hi=True)
    gam8 = jnp.broadcast_to(gam, (8, SSD_HEADS))
    return dt, A, acum, acum_t, e, wdec, gam, ex(dt), ex(e), ex(wdec), ex(gam8)[0:1, :]


def _ssd_fwd(proj, proj_small, xa, dt_bias, a_log, d_skip, norm_w):
    S = proj.shape[0]
    L, N, W = SSD_L, SSD_N, SSD_WIDTH
    nc = S // L

    def body(z_ref, xa_ref, dtr_ref, dtb_ref, al_ref, dsk_ref, nw_ref, yo_ref, y_ref, rs_ref, hs_ref, h_scr, y_scr):
        @pl.when(pl.program_id(0) == 0)
        def _():
            h_scr[...] = jnp.zeros_like(h_scr)

        causal, tri, upper, expand = _ssd_consts()
        dt, A, acum, acum_t, e, wdec, gam, dtE, eE, wE, gamE = _ssd_scalars(dtr_ref[:, 0:SSD_HEADS], dtb_ref[...], al_ref[...], tri, upper, expand)
        xs = xa_ref[:, 0:W]
        X = xs * dtE
        XW = X * wE
        hs_ref[0] = h_scr[...]
        for g in range(SSD_G):
            gs = slice(g * 512, (g + 1) * 512)
            Bg = xa_ref[:, W + g * N:W + (g + 1) * N]
            Cg = xa_ref[:, W + SSD_G * N + g * N:W + SSD_G * N + (g + 1) * N]
            Hg = h_scr[:, gs]
            CB = _dot(Cg, Bg, "nt")
            yoff = _dot(Cg, Hg, "nn") * eE[:, gs]
            st = _dot(Bg, XW[:, gs], "tn")
            for j in range(8):
                h = g * 8 + j
                hsl = slice(h * SSD_P, (h + 1) * SSD_P)
                lam = jnp.exp(jnp.where(causal, acum[:, h:h + 1] - acum_t[h:h + 1, :], -jnp.inf))
                y_scr[:, hsl] = _dot(CB * lam, X[:, hsl], "nn") + yoff[:, j * SSD_P:(j + 1) * SSD_P]
            h_scr[:, gs] = gamE[:, gs] * Hg + st
        dskE = _dot(jnp.broadcast_to(dsk_ref[...], (8, SSD_HEADS)), expand, "nn", hi=True)[0:1, :]
        y = y_scr[...] + dskE * xs
        y_ref[...] = y
        zv = z_ref[...]
        yg = y * (zv * _sigmoid(zv))
        rs = lax.rsqrt(jnp.mean(yg * yg, axis=-1, keepdims=True) + EPS)
        rs_ref[...] = rs
        yo_ref[...] = ((yg * rs) * nw_ref[...]).astype(yo_ref.dtype)

    p16 = pl.BlockSpec((1, SSD_HEADS), lambda c: (0, 0))
    return pl.pallas_call(
        body, name="ssd_fwd", grid=(nc,),
        in_specs=[pl.BlockSpec((L, W), lambda c: (c, 0)), pl.BlockSpec((L, CONV_CH), lambda c: (c, 0)),
                  pl.BlockSpec((L, W_SMALL), lambda c: (c, 0)), p16, p16, p16, pl.BlockSpec((1, W), lambda c: (0, 0))],
        out_specs=[pl.BlockSpec((L, W), lambda c: (c, 0)), pl.BlockSpec((L, W), lambda c: (c, 0)),
                   pl.BlockSpec((L, 1), lambda c: (c, 0)), pl.BlockSpec((1, N, W), lambda c: (c, 0, 0))],
        out_shape=[jax.ShapeDtypeStruct((S, W), _MXU), jax.ShapeDtypeStruct((S, W), F32), jax.ShapeDtypeStruct((S, 1), F32),
                   jax.ShapeDtypeStruct((nc, N, W), F32)],
        scratch_shapes=[pltpu.VMEM((N, W), F32), pltpu.VMEM((L, W), F32)],
        compiler_params=_cp(("arbitrary",)))(proj, xa, proj_small, dt_bias, a_log, d_skip, norm_w)


def _ssd_bwd(dmixed, proj, proj_small, xa, y, rs2, hs, dt_bias, a_log, d_skip, norm_w):
    S = proj.shape[0]
    L, N, W, H = SSD_L, SSD_N, SSD_WIDTH, SSD_HEADS
    nc = S // L

    def body(dyo_ref, z_ref, xa_ref, dtr_ref, y_ref, rs_ref, hs_ref, dtb_ref, al_ref, dsk_ref, nw_ref,
             dz_ref, dxa_ref, ddtr_ref, ddtb_ref, dal_ref, ddsk_ref, dnw_ref, dh_scr, dx_scr):
        @pl.when(pl.program_id(0) == 0)
        def _():
            dh_scr[...] = jnp.zeros_like(dh_scr)
            ddtb_ref[...] = jnp.zeros_like(ddtb_ref)
            dal_ref[...] = jnp.zeros_like(dal_ref)
            ddsk_ref[...] = jnp.zeros_like(ddsk_ref)
            dnw_ref[...] = jnp.zeros_like(dnw_ref)

        causal, tri, upper, expand = _ssd_consts()
        heads = lambda t: _dot(t, expand, "nt", hi=True)
        onehot = lambda h: (lax.broadcasted_iota(jnp.int32, (1, H), 1) == h).astype(F32)

        zv, yv, rs = z_ref[...], y_ref[...], rs_ref[...]
        sz = _sigmoid(zv)
        zs = zv * sz
        xhat = (yv * zs) * rs
        dyo = dyo_ref[...].astype(F32)
        dnw_ref[...] += jnp.sum(dyo * xhat, axis=0, keepdims=True)
        dxhat = dyo * nw_ref[...]
        dyg = rs * (dxhat - xhat * jnp.mean(dxhat * xhat, axis=-1, keepdims=True))
        dz_ref[...] = (dyg * yv * (sz * (1.0 + zv * (1.0 - sz)))).astype(dz_ref.dtype)
        dy = dyg * zs

        dtr = dtr_ref[:, 0:H]
        dt, A, acum, acum_t, e, wdec, gam, dtE, eE, wE, gamE = _ssd_scalars(dtr, dtb_ref[...], al_ref[...], tri, upper, expand)
        xs = xa_ref[:, 0:W]
        X = xs * dtE
        XW = X * wE
        dskE = _dot(jnp.broadcast_to(dsk_ref[...], (8, H)), expand, "nn", hi=True)[0:1, :]
        ddsk_ref[...] += heads(jnp.broadcast_to(jnp.sum(dy * xs, axis=0, keepdims=True), (8, W)))[0:1, :]

        dYe = dy * eE
        dacum = jnp.zeros((L, H), F32)
        de_full = []
        dw_full = []
        dgam_full = []
        for g in range(SSD_G):
            gs = slice(g * 512, (g + 1) * 512)
            Bg = xa_ref[:, W + g * N:W + (g + 1) * N]
            Cg = xa_ref[:, W + SSD_G * N + g * N:W + SSD_G * N + (g + 1) * N]
            Hg = hs_ref[0, :, gs]
            dHn = dh_scr[:, gs]
            CH = _dot(Cg, Hg, "nn")
            de_full.append(dy[:, gs] * CH)
            dC = _dot(dYe[:, gs], Hg, "nt")
            dHs = gamE[:, gs] * dHn + _dot(Cg, dYe[:, gs], "tn")
            dgam_full.append(jnp.sum(dHn * Hg, axis=0, keepdims=True))
            BdS = _dot(Bg, dHn, "nn")
            dB = _dot(XW[:, gs], dHn, "nt")
            dx_scr[:, gs] = BdS * wE[:, gs]
            dw_full.append(BdS * X[:, gs])
            CB = _dot(Cg, Bg, "nt")
            dCB = jnp.zeros((L, L), F32)
            for j in range(8):
                h = g * 8 + j
                hsl = slice(h * SSD_P, (h + 1) * SSD_P)
                lam = jnp.exp(jnp.where(causal, acum[:, h:h + 1] - acum_t[h:h + 1, :], -jnp.inf))
                M = CB * lam
                dM = _dot(dy[:, hsl], X[:, hsl], "nt")
                dx_scr[:, hsl] += _dot(M, dy[:, hsl], "tn")
                dCB = dCB + dM * lam
                Q = dM * M
                rowsum = jnp.sum(Q, axis=1, keepdims=True)
                colsum = _dot(Q, jnp.ones((L, 8), F32), "tn", hi=True)[:, 0:1]
                dacum = dacum + (rowsum - colsum) * onehot(h)
            dC = dC + _dot(dCB, Bg, "nn")
            dB = dB + _dot(dCB, Cg, "tn")
            dxa_ref[:, W + g * N:W + (g + 1) * N] = dB
            dxa_ref[:, W + SSD_G * N + g * N:W + SSD_G * N + (g + 1) * N] = dC
            dh_scr[:, gs] = dHs

        de16 = heads(jnp.concatenate(de_full, axis=1))
        dw16 = heads(jnp.concatenate(dw_full, axis=1))
        dgam16 = heads(jnp.broadcast_to(jnp.concatenate(dgam_full, axis=1), (8, W)))[0:1, :]
        dacum = dacum + de16 * e - dw16 * wdec
        dlast = jnp.sum(dw16 * wdec, axis=0, keepdims=True) + dgam16 * gam
        lastrow = (lax.broadcasted_iota(jnp.int32, (L, 1), 0) == L - 1).astype(F32)
        dacum = dacum + lastrow * dlast
        da = _dot(tri, dacum, "tn", hi=True)
        dX = dx_scr[...]
        ddt = da * A + heads(dX * xs)
        dA = jnp.sum(da * dt, axis=0, keepdims=True)
        dal_ref[...] += dA * A
        ddtr = ddt * _sigmoid(dtr + dtb_ref[...])
        ddtb_ref[...] += jnp.sum(ddtr, axis=0, keepdims=True)
        ddtr_ref[...] = ddtr
        dxa_ref[:, 0:W] = dX * dtE + dy * dskE

    p16 = pl.BlockSpec((1, H), lambda c: (0, 0))
    rev = lambda c: (nc - 1 - c, 0)
    return pl.pallas_call(
        body, name="ssd_bwd", grid=(nc,),
        in_specs=[pl.BlockSpec((L, W), rev), pl.BlockSpec((L, W), rev), pl.BlockSpec((L, CONV_CH), rev),
                  pl.BlockSpec((L, W_SMALL), rev), pl.BlockSpec((L, W), rev), pl.BlockSpec((L, 1), rev),
                  pl.BlockSpec((1, N, W), lambda c: (nc - 1 - c, 0, 0)), p16, p16, p16, pl.BlockSpec((1, W), lambda c: (0, 0))],
        out_specs=[pl.BlockSpec((L, W), rev), pl.BlockSpec((L, CONV_CH), rev), pl.BlockSpec((L, H), rev),
                   p16, p16, p16, pl.BlockSpec((1, W), lambda c: (0, 0))],
        out_shape=[jax.ShapeDtypeStruct((S, W), _MXU), jax.ShapeDtypeStruct((S, CONV_CH), F32), jax.ShapeDtypeStruct((S, H), F32),
                   jax.ShapeDtypeStruct((1, H), F32), jax.ShapeDtypeStruct((1, H), F32), jax.ShapeDtypeStruct((1, H), F32),
                   jax.ShapeDtypeStruct((1, W), F32)],
        scratch_shapes=[pltpu.VMEM((N, W), F32), pltpu.VMEM((L, W), F32)],
        compiler_params=_cp(("arbitrary",)))(dmixed, proj, xa, proj_small, y, rs2, hs, dt_bias, a_log, d_skip, norm_w)


def _rope_tables(S):
    inv = 1.0 / (ROPE_THETA ** (jnp.arange(0, ROPE_DIM, 2, dtype=F32) / ROPE_DIM))
    ang = jnp.arange(S, dtype=F32)[:, None] * inv[None, :]
    cos, sin = jnp.cos(ang), jnp.sin(ang)
    half = ROPE_DIM // 2
    c64 = jnp.concatenate([cos, cos, jnp.ones((S, HD - ROPE_DIM), F32)], axis=1)
    s64 = jnp.concatenate([sin, sin, jnp.zeros((S, HD - ROPE_DIM), F32)], axis=1)
    del half
    return jnp.concatenate([c64, c64], axis=1), jnp.concatenate([s64, s64], axis=1)


def _rope(xs, blk0, width, cos, sin, sign, out_dtype, name, extra=None):
    S = xs[0].shape[0]
    tr = _pick(S, (512, 256, 128))
    nx = len(xs)

    def body(*refs):
        x_refs, c_ref, s_ref = refs[:nx], refs[nx], refs[nx + 1]
        e_ref = refs[nx + 2] if extra is not None else None
        o_ref = refs[-1]
        cv, sv = c_ref[...], s_ref[...] * sign
        lane = lax.broadcasted_iota(jnp.int32, (tr, 128), 1)
        first = (lane & (HD - 1)) < (ROPE_DIM // 2)
        for j in range(2):
            cs = slice(j * 128, (j + 1) * 128)
            xv = x_refs[0][:, cs].astype(F32)
            for r in x_refs[1:]:
                xv = xv + r[:, cs].astype(F32)
            rot = jnp.where(first, -pltpu.roll(xv, 128 - ROPE_DIM // 2, axis=1), pltpu.roll(xv, ROPE_DIM // 2, axis=1))
            out = xv * cv + rot * sv
            if extra is not None:
                out = out + e_ref[:, cs].astype(F32)
            o_ref[:, cs] = out.astype(out_dtype)

    t128 = pl.BlockSpec((tr, 128), lambda i, j: (i, 0))
    oblk = pl.BlockSpec((tr, 256), lambda i, j: (i, j))
    specs = [pl.BlockSpec((tr, 256), lambda i, j: (i, blk0 + j))] * nx + [t128, t128]
    ins = list(xs) + [cos, sin]
    if extra is not None:
        ins.append(extra[0])
        eb = extra[1]
        specs.append(pl.BlockSpec((tr, 256), lambda i, j: (i, eb + j)))
    return pl.pallas_call(
        body, name=name, grid=(S // tr, width // 256), in_specs=specs, out_specs=oblk,
        out_shape=jax.ShapeDtypeStruct((S, width), out_dtype), compiler_params=_cp(("parallel", "parallel")))(*ins)


def _compress_fwd(R, pe, w1, w2):
    NC = R.shape[1]
    half = 16 * HD

    def body(r_ref, pe_ref, w1_ref, w2_ref, o_ref, hid_ref):
        r = r_ref[0]
        a = _dot(r + pe_ref[:, 0:half], w1_ref[0:half, :], "nn")
        b = _dot(r + pe_ref[:, half:2 * half], w1_ref[half:2 * half, :], "nn")
        hid = a + pltpu.roll(b, NC - 1, axis=0)
        hid_ref[0] = hid
        out = _dot(hid * _sigmoid(hid), w2_ref[...], "nn")
        rows = lax.broadcasted_iota(jnp.int32, out.shape, 0)
        o_ref[0] = jnp.where(rows < NC - 1, out, 0.0).astype(o_ref.dtype)

    return pl.pallas_call(
        body, name="compress_fwd", grid=(N_KV,),
        in_specs=[pl.BlockSpec((1, NC, half), lambda h: (h, 0, 0)), pl.BlockSpec((1, 2 * half), lambda h: (0, 0)),
                  pl.BlockSpec((2 * half, CMP_HID), lambda h: (0, 0)), pl.BlockSpec((CMP_HID, HD), lambda h: (0, 0))],
        out_specs=[pl.BlockSpec((1, NC, HD), lambda h: (h, 0, 0)), pl.BlockSpec((1, NC, CMP_HID), lambda h: (h, 0, 0))],
        out_shape=[jax.ShapeDtypeStruct((N_KV, NC, HD), _MXU), jax.ShapeDtypeStruct((N_KV, NC, CMP_HID), F32)],
        compiler_params=_cp(("parallel",)))(R, pe, w1, w2)


def _compress_bwd(R, pe, w1, w2, hid, dout):
    NC = R.shape[1]
    half = 16 * HD

    def body(r_ref, pe_ref, w1_ref, w2_ref, hid_ref, do_ref, dr_ref, dw1_ref, dw2_ref, dpe_ref):
        @pl.when(pl.program_id(0) == 0)
        def _():
            dw1_ref[...] = jnp.zeros_like(dw1_ref)
            dw2_ref[...] = jnp.zeros_like(dw2_ref)
            dpe_ref[...] = jnp.zeros_like(dpe_ref)

        r, hv, do = r_ref[0], hid_ref[0], do_ref[0]
        s = _sigmoid(hv)
        dw2_ref[...] += _dot(hv * s, do, "tn")
        dhid = _dot(do, w2_ref[...], "nt") * (s * (1.0 + hv * (1.0 - s)))
        rows = lax.broadcasted_iota(jnp.int32, dhid.shape, 0)
        dhid = jnp.where(rows < NC - 1, dhid, 0.0)
        dhid_dn = pltpu.roll(dhid, 1, axis=0)
        dw1_ref[0:half, :] += _dot(r + pe_ref[:, 0:half], dhid, "tn")
        dw1_ref[half:2 * half, :] += _dot(r + pe_ref[:, half:2 * half], dhid_dn, "tn")
        dxt = _dot(dhid, w1_ref[0:half, :], "nt")
        dxb = _dot(dhid_dn, w1_ref[half:2 * half, :], "nt")
        dr_ref[0] = dxt + dxb
        dpe_ref[:, 0:half] += jnp.sum(dxt, axis=0, keepdims=True)
        dpe_ref[:, half:2 * half] += jnp.sum(dxb, axis=0, keepdims=True)

    return pl.pallas_call(
        body, name="compress_bwd", grid=(N_KV,),
        in_specs=[pl.BlockSpec((1, NC, half), lambda h: (h, 0, 0)), pl.BlockSpec((1, 2 * half), lambda h: (0, 0)),
                  pl.BlockSpec((2 * half, CMP_HID), lambda h: (0, 0)), pl.BlockSpec((CMP_HID, HD), lambda h: (0, 0)),
                  pl.BlockSpec((1, NC, CMP_HID), lambda h: (h, 0, 0)), pl.BlockSpec((1, NC, HD), lambda h: (h, 0, 0))],
        out_specs=[pl.BlockSpec((1, NC, half), lambda h: (h, 0, 0)), pl.BlockSpec((2 * half, CMP_HID), lambda h: (0, 0)),
                   pl.BlockSpec((CMP_HID, HD), lambda h: (0, 0)), pl.BlockSpec((1, 2 * half), lambda h: (0, 0))],
        out_shape=[jax.ShapeDtypeStruct((N_KV, NC, half), F32), jax.ShapeDtypeStruct((2 * half, CMP_HID), F32),
                   jax.ShapeDtypeStruct((CMP_HID, HD), F32), jax.ShapeDtypeStruct((1, 2 * half), F32)],
        compiler_params=_cp(("arbitrary",)))(R, pe, w1, w2, hid, dout)


def _attn_cfg(S, Sk, mode):
    tq = _pick(S, (256, 128))
    tk = Sk if mode == "cmp" else _pick(Sk, (256, 128))
    return tq, tk


def _kb_range(mode, q0, tq, tk):
    if mode == "cmp":
        return 0, 1
    hi = (q0 + tq - 1) // tk + 1
    if mode == "sel":
        return 0, hi
    return jnp.maximum(q0 - (WINDOW - 1), 0) // tk, hi


def _attn_bias(mode, q0, k0, tq, tk, sel):
    t = q0 + lax.broadcasted_iota(jnp.int32, (tq, tk), 0)
    k = k0 + lax.broadcasted_iota(jnp.int32, (tq, tk), 1)
    if mode == "cmp":
        ok = (k * 16 + 31) <= t
    elif mode == "win":
        ok = (k <= t) & ((t - k) < WINDOW)
    else:
        nb = sel.shape[1]
        er = lax.broadcasted_iota(jnp.int32, (nb, tk), 0)
        ec = k0 + lax.broadcasted_iota(jnp.int32, (nb, tk), 1)
        expand = (lax.shift_right_logical(ec, 6) == er).astype(_MXU)
        chosen = _dot(sel, expand, "nn") > 0.5
        ok = (k <= t) & chosen
    return jnp.where(ok, 0.0, NEG), ok


def _stack_heads(ref, tq):
    return jnp.concatenate([ref[:, g * HD:(g + 1) * HD] for g in range(GRP)], axis=0)


def _attn_fwd(q, qcol0, k, v, mode, sel, name):
    S, Sk = q.shape[0], k.shape[1]
    tq, tk = _attn_cfg(S, Sk, mode)
    R = GRP * tq

    def body(*refs):
        if mode == "sel":
            q_ref, k_ref, v_ref, sel_ref, o_ref, lse_ref, m_scr, l_scr, acc = refs
        else:
            q_ref, k_ref, v_ref, o_ref, lse_ref, m_scr, l_scr, acc = refs
        q0 = pl.program_id(1) * tq
        qs = _stack_heads(q_ref, tq).astype(_MXU)
        m_scr[...] = jnp.full_like(m_scr, NEG)
        l_scr[...] = jnp.zeros_like(l_scr)
        acc[...] = jnp.zeros_like(acc)
        selv = sel_ref[0].astype(_MXU) if mode == "sel" else None

        def step(kb, carry):
            k0 = pl.multiple_of(kb * tk, tk)
            kv, vv = k_ref[0, pl.ds(k0, tk), :], v_ref[0, pl.ds(k0, tk), :]
            bias, ok = _attn_bias(mode, q0, k0, tq, tk, selv)
            s = (_dot(qs, kv, "nt") * SCALE).reshape(GRP, tq, tk) + bias[None]
            m_old = m_scr[...].reshape(GRP, tq, 1)
            m_new = jnp.maximum(m_old, jnp.max(s, axis=-1, keepdims=True))
            p = jnp.exp(s - m_new)
            if mode == "cmp":
                p = p * ok.astype(F32)[None]
            alpha = jnp.exp(m_old - m_new)
            l_scr[...] = (alpha * l_scr[...].reshape(GRP, tq, 1) + jnp.sum(p, axis=-1, keepdims=True)).reshape(R, 1)
            acc[...] = alpha.reshape(R, 1) * acc[...] + _dot(p.reshape(R, tk), vv, "nn")
            m_scr[...] = m_new.reshape(R, 1)
            return carry

        lo, hi = _kb_range(mode, q0, tq, tk)
        lax.fori_loop(lo, hi, step, 0)
        l = l_scr[...]
        good = l > 0.0
        o = acc[...] * jnp.where(good, 1.0 / jnp.where(good, l, 1.0), 0.0)
        lse = jnp.where(good, m_scr[...] + jnp.log(jnp.where(good, l, 1.0)), -NEG)
        for g in range(GRP):
            o_ref[:, g * HD:(g + 1) * HD] = o[g * tq:(g + 1) * tq, :]
            lse_ref[0, :, g:g + 1] = lse[g * tq:(g + 1) * tq, :]

    kv_spec = pl.BlockSpec((1, Sk, HD), lambda h, i: (h, 0, 0))
    ins, specs = [q, k, v], [pl.BlockSpec((tq, GRP * HD), lambda h, i: (i, qcol0 + h)), kv_spec, kv_spec]
    if mode == "sel":
        ins.append(sel)
        specs.append(pl.BlockSpec((1, tq, sel.shape[2]), lambda h, i: (h, i, 0)))
    return pl.pallas_call(
        body, name=name, grid=(N_KV, S // tq), in_specs=specs,
        out_specs=[pl.BlockSpec((tq, GRP * HD), lambda h, i: (i, h)), pl.BlockSpec((1, tq, GRP), lambda h, i: (h, i, 0))],
        out_shape=[jax.ShapeDtypeStruct((S, ATT_WIDTH), F32), jax.ShapeDtypeStruct((N_KV, S, GRP), F32)],
        scratch_shapes=[pltpu.VMEM((R, 1), F32), pltpu.VMEM((R, 1), F32), pltpu.VMEM((R, HD), F32)],
        compiler_params=_cp(("parallel", "arbitrary")))(*ins)


def _attn_bwd(q, qcol0, k, v, o, lse, do, mode, sel, name):
    S, Sk = q.shape[0], k.shape[1]
    tq, tk = _attn_cfg(S, Sk, mode)
    R = GRP * tq

    def body(*refs):
        if mode == "sel":
            q_ref, k_ref, v_ref, o_ref, lse_ref, do_ref, sel_ref, dq_ref, dk_ref, dv_ref, dq_scr = refs
        else:
            q_ref, k_ref, v_ref, o_ref, lse_ref, do_ref, dq_ref, dk_ref, dv_ref, dq_scr = refs

        @pl.when(pl.program_id(1) == 0)
        def _():
            dk_ref[...] = jnp.zeros_like(dk_ref)
            dv_ref[...] = jnp.zeros_like(dv_ref)

        q0 = pl.program_id(1) * tq
        qs = _stack_heads(q_ref, tq).astype(_MXU)
        dos = _stack_heads(do_ref, tq)
        delta = jnp.sum(dos * _stack_heads(o_ref, tq), axis=-1, keepdims=True).reshape(GRP, tq, 1)
        lsev = jnp.concatenate([lse_ref[0, :, g:g + 1] for g in range(GRP)], axis=0).reshape(GRP, tq, 1)
        dos = dos.astype(_MXU)
        dq_scr[...] = jnp.zeros_like(dq_scr)
        selv = sel_ref[0].astype(_MXU) if mode == "sel" else None

        def step(kb, carry):
            k0 = pl.multiple_of(kb * tk, tk)
            kv, vv = k_ref[0, pl.ds(k0, tk), :], v_ref[0, pl.ds(k0, tk), :]
            bias, ok = _attn_bias(mode, q0, k0, tq, tk, selv)
            s = (_dot(qs, kv, "nt") * SCALE).reshape(GRP, tq, tk) + bias[None]
            p = jnp.exp(s - lsev)
            if mode == "cmp":
                p = p * ok.astype(F32)[None]
            dp = _dot(dos, vv, "nt").reshape(GRP, tq, tk)
            ds = (p * (dp - delta) * SCALE).reshape(R, tk)
            p2 = p.reshape(R, tk)
            dq_scr[...] += _dot(ds, kv, "nn")
            dk_ref[0, pl.ds(k0, tk), :] += _dot(ds, qs, "tn")
            dv_ref[0, pl.ds(k0, tk), :] += _dot(p2, dos, "tn")
            return carry

        lo, hi = _kb_range(mode, q0, tq, tk)
        lax.fori_loop(lo, hi, step, 0)
        for g in range(GRP):
            dq_ref[:, g * HD:(g + 1) * HD] = dq_scr[g * tq:(g + 1) * tq, :]

    kv_spec = pl.BlockSpec((1, Sk, HD), lambda h, i: (h, 0, 0))
    qo_spec = pl.BlockSpec((tq, GRP * HD), lambda h, i: (i, h))
    ins = [q, k, v, o, lse, do]
    specs = [pl.BlockSpec((tq, GRP * HD), lambda h, i: (i, qcol0 + h)), kv_spec, kv_spec, qo_spec,
             pl.BlockSpec((1, tq, GRP), lambda h, i: (h, i, 0)), qo_spec]
    if mode == "sel":
        ins.append(sel)
        specs.append(pl.BlockSpec((1, tq, sel.shape[2]), lambda h, i: (h, i, 0)))
    return pl.pallas_call(
        body, name=name, grid=(N_KV, S // tq), in_specs=specs, out_specs=[qo_spec, kv_spec, kv_spec],
        out_shape=[jax.ShapeDtypeStruct((S, ATT_WIDTH), F32), jax.ShapeDtypeStruct((N_KV, Sk, HD), F32),
                   jax.ShapeDtypeStruct((N_KV, Sk, HD), F32)],
        scratch_shapes=[pltpu.VMEM((R, HD), F32)],
        compiler_params=_cp(("parallel", "arbitrary")))(*ins)


def _select(q, qcol0, k_cmp, lse):
    S, NC = q.shape[0], k_cmp.shape[1]
    NB = S // SEL_BLOCK
    tq = _pick(S, (256, 128))
    R = GRP * tq
    ci = np.arange(NC)[:, None] * 16
    sj = np.arange(NB)[None, :] * SEL_BLOCK
    ov = np.clip(np.minimum(ci + 32, sj + SEL_BLOCK) - np.maximum(ci, sj), 0, None) / 32.0
    ov[NC - 1, :] = 0.0
    ov = jnp.asarray(ov, F32)

    def body(q_ref, k_ref, lse_ref, ov_ref, sel_ref):
        q0 = pl.program_id(1) * tq
        qs = _stack_heads(q_ref, tq)
        bias, ok = _attn_bias("cmp", q0, 0, tq, NC, None)
        s = (_dot(qs, k_ref[0], "nt") * SCALE).reshape(GRP, tq, NC) + bias[None]
        lsev = jnp.concatenate([lse_ref[0, :, g:g + 1] for g in range(GRP)], axis=0).reshape(GRP, tq, 1)
        p = jnp.exp(s - lsev) * ok.astype(F32)[None]
        imp4 = _dot(p.reshape(R, NC), ov_ref[...], "nn").reshape(GRP, tq, NB)
        imp = imp4[0] + imp4[1] + imp4[2] + imp4[3]
        blk = lax.broadcasted_iota(jnp.int32, (tq, NB), 1)
        cur = lax.shift_right_logical(q0 + lax.broadcasted_iota(jnp.int32, (tq, NB), 0), 6)
        imp = jnp.where((blk == 0) | (blk == cur) | (blk == cur - 1), FORCE, imp)
        imp = jnp.where(blk <= cur, imp, -1.0)
        rank = jnp.zeros((tq, NB), F32)
        for j in range(NB):
            col = imp[:, j:j + 1]
            ahead = (col > imp) | ((col == imp) & (blk > j))
            rank = rank + ahead.astype(F32)
        sel_ref[0] = ((rank < float(N_SELECT)) & (imp >= 0.0)).astype(F32)

    return pl.pallas_call(
        body, name="select_blocks", grid=(N_KV, S // tq),
        in_specs=[pl.BlockSpec((tq, GRP * HD), lambda h, i: (i, qcol0 + h)), pl.BlockSpec((1, NC, HD), lambda h, i: (h, 0, 0)),
                  pl.BlockSpec((1, tq, GRP), lambda h, i: (h, i, 0)), pl.BlockSpec((NC, NB), lambda h, i: (0, 0))],
        out_specs=pl.BlockSpec((1, tq, NB), lambda h, i: (h, i, 0)),
        out_shape=jax.ShapeDtypeStruct((N_KV, S, NB), F32), compiler_params=_cp(("parallel", "parallel")))(q, k_cmp, lse, ov)


GATE_COL0 = SSD_HEADS


def _combine_fwd(o_cmp, o_sel, o_win, proj_small):
    S = o_cmp.shape[0]
    tr = _pick(S, (256, 128))

    def body(oc_ref, os_ref, ow_ref, g_ref, y_ref):
        gate = _sigmoid(g_ref[...])
        for h in range(N_HEADS):
            hs = slice(h * HD, (h + 1) * HD)
            c = GATE_COL0 + 3 * h
            y = gate[:, c:c + 1] * oc_ref[:, hs] + gate[:, c + 1:c + 2] * os_ref[:, hs] + gate[:, c + 2:c + 3] * ow_ref[:, hs]
            y_ref[:, hs] = y.astype(y_ref.dtype)

    row = pl.BlockSpec((tr, ATT_WIDTH), lambda i: (i, 0))
    return pl.pallas_call(
        body, name="combine_fwd", grid=(S // tr,), in_specs=[row, row, row, pl.BlockSpec((tr, W_SMALL), lambda i: (i, 0))],
        out_specs=row, out_shape=jax.ShapeDtypeStruct((S, ATT_WIDTH), _MXU), compiler_params=_cp(("parallel",)))(
            o_cmp, o_sel, o_win, proj_small)


def _combine_bwd(dmixed, o_cmp, o_sel, o_win, proj_small):
    S = o_cmp.shape[0]
    tr = _pick(S, (256, 128))

    def body(dy_ref, oc_ref, os_ref, ow_ref, g_ref, dc_ref, ds_ref, dw_ref, dg_ref):
        gate = _sigmoid(g_ref[...])
        lane = lax.broadcasted_iota(jnp.int32, (1, W_SMALL), 1)
        dg = jnp.zeros((tr, W_SMALL), F32)
        for h in range(N_HEADS):
            hs = slice(h * HD, (h + 1) * HD)
            dy = dy_ref[:, hs].astype(F32)
            for b, (o_ref, d_ref) in enumerate(((oc_ref, dc_ref), (os_ref, ds_ref), (ow_ref, dw_ref))):
                c = GATE_COL0 + 3 * h + b
                gv = gate[:, c:c + 1]
                d_ref[:, hs] = gv * dy
                dgate = jnp.sum(dy * o_ref[:, hs], axis=-1, keepdims=True) * (gv * (1.0 - gv))
                dg = dg + dgate * (lane == c).astype(F32)
        dg_ref[...] = dg

    row = pl.BlockSpec((tr, ATT_WIDTH), lambda i: (i, 0))
    small = pl.BlockSpec((tr, W_SMALL), lambda i: (i, 0))
    return pl.pallas_call(
        body, name="combine_bwd", grid=(S // tr,),
        in_specs=[pl.BlockSpec((tr, ATT_WIDTH), lambda i: (i, 1)), row, row, row, small], out_specs=[row, row, row, small],
        out_shape=[jax.ShapeDtypeStruct((S, ATT_WIDTH), F32)] * 3 + [jax.ShapeDtypeStruct((S, W_SMALL), F32)],
        compiler_params=_cp(("parallel",)))(dmixed, o_cmp, o_sel, o_win, proj_small)


def _heads_major(x):
    S = x.shape[0]
    return x.reshape(S, N_KV, HD).transpose(1, 0, 2)


def _tokens_major(x):
    return x.transpose(1, 0, 2).reshape(x.shape[1], N_KV * HD)


def _to_rows16(x):
    S = x.shape[0]
    return x.reshape(S // 16, 16, N_KV, HD).transpose(2, 0, 1, 3).reshape(N_KV, S // 16, 16 * HD)


def _from_rows16(r):
    NC = r.shape[1]
    return r.reshape(N_KV, NC, 16, HD).transpose(1, 2, 0, 3).reshape(NC * 16, N_KV * HD)


DT_COL0 = SSD_WIDTH + CONV_CH
GATE_IN_COL0 = D_IN - 3 * N_HEADS


def _split_w_in(w):
    main = jnp.concatenate([w[:, :DT_COL0], w[:, DT_COL0 + SSD_HEADS:GATE_IN_COL0]], axis=1)
    small = jnp.concatenate([w[:, DT_COL0:DT_COL0 + SSD_HEADS], w[:, GATE_IN_COL0:],
                             jnp.zeros((w.shape[0], W_SMALL - SSD_HEADS - 3 * N_HEADS), w.dtype)], axis=1)
    return main, small


def _merge_w_in(main, small):
    return jnp.concatenate([main[:, :DT_COL0], small[:, :SSD_HEADS].astype(main.dtype), main[:, DT_COL0:],
                            small[:, SSD_HEADS:SSD_HEADS + 3 * N_HEADS].astype(main.dtype)], axis=1)


QB, KCB, VCB, KSB, VSB, KWB, VWB = 10, 14, 15, 16, 17, 18, 19


def _col256(a, b):
    return a[:, b * 256:(b + 1) * 256]


def _local_step(x, tgt, p):
    S = x.shape[0]
    cos, sin = _rope_tables(S)

    u, rs1 = _rms_fwd(x, p["attn_norm_w"], "attn_norm")
    proj = _mm(u, p["w_main"], "nn", F32, "in_proj")
    proj_small = _mm(u, p["w_small"], "nn", F32, "in_proj_small")
    xa = _conv_fwd(proj, p["conv_w"], p["conv_b"])
    y_ssd, y_pre, rs_ssd, hs = _ssd_fwd(proj, proj_small, xa, p["dt_bias"], p["a_log"], p["d_skip"], p["ssd_norm_w"])

    q_rot = _rope([proj], QB, ATT_WIDTH, cos, sin, 1.0, _MXU, "rope_q")
    ks_rot = _heads_major(_rope([proj], KSB, 256, cos, sin, 1.0, _MXU, "rope_ks"))
    kw_rot = _heads_major(_rope([proj], KWB, 256, cos, sin, 1.0, _MXU, "rope_kw"))
    vs = _heads_major(_col256(proj, VSB).astype(_MXU))
    vw = _heads_major(_col256(proj, VWB).astype(_MXU))
    rk, rv = _to_rows16(_col256(proj, KCB)), _to_rows16(_col256(proj, VCB))
    k_cmp, hid_k = _compress_fwd(rk, p["cmp_pe_k"], p["cmp_w1_k"], p["cmp_w2_k"])
    v_cmp, hid_v = _compress_fwd(rv, p["cmp_pe_v"], p["cmp_w1_v"], p["cmp_w2_v"])

    o_cmp, lse_cmp = _attn_fwd(proj, QB, k_cmp, v_cmp, "cmp", None, "attn_cmp_fwd")
    sel = _select(proj, QB, k_cmp, lse_cmp)
    o_sel, lse_sel = _attn_fwd(q_rot, 0, ks_rot, vs, "sel", sel, "attn_sel_fwd")
    o_win, lse_win = _attn_fwd(q_rot, 0, kw_rot, vw, "win", None, "attn_win_fwd")
    y_att = _combine_fwd(o_cmp, o_sel, o_win, proj_small)

    mixed = jnp.concatenate([y_ssd, y_att], axis=1)
    h1 = _mm(mixed, p["w_out"], "nn", F32, "out_proj", res=x)
    v, rs_ffn = _rms_fwd(h1, p["ffn_norm_w"], "ffn_norm")
    gt, up, act = _ffn_up(v, p["w_gate"], p["w_up"])
    h2 = _mm(act, p["w_down"], "nn", F32, "ffn_down", res=h1)
    loss, dh2, d_final_w = _final_loss(h2, p["final_norm_w"], tgt)

    g = {"final_norm_w": d_final_w}
    g["w_down"] = _mm(act, dh2, "tn", _MXU, "dw_down")
    dgt, dup = _ffn_dact(dh2, p["w_down"], gt, up)
    g["w_gate"] = _mm(v, dgt, "tn", _MXU, "dw_gate")
    g["w_up"] = _mm(v, dup, "tn", _MXU, "dw_up")
    dv = _mm(dgt, p["w_gate"], "nt", F32, "dv_gate")
    dv = _mm(dup, p["w_up"], "nt", F32, "dv_up", res=dv)
    dh1, g["ffn_norm_w"] = _rms_bwd(dv, h1, rs_ffn, p["ffn_norm_w"], dh2, "ffn_norm_bwd")
    g["w_out"] = _mm(mixed, dh1, "tn", _MXU, "dw_out")
    dmixed = _mm(dh1, p["w_out"], "nt", F32, "dmixed")

    dz, dxa, ddtr, g["dt_bias"], g["a_log"], g["d_skip"], g["ssd_norm_w"] = _ssd_bwd(
        dmixed, proj, proj_small, xa, y_pre, rs_ssd, hs, p["dt_bias"], p["a_log"], p["d_skip"], p["ssd_norm_w"])
    dxbc, g["conv_w"], g["conv_b"] = _conv_bwd(proj, p["conv_w"], p["conv_b"], dxa)

    do_cmp, do_sel, do_win, dgate = _combine_bwd(dmixed, o_cmp, o_sel, o_win, proj_small)
    dq_cmp, dk_cmp, dv_cmp = _attn_bwd(proj, QB, k_cmp, v_cmp, o_cmp, lse_cmp, do_cmp, "cmp", None, "attn_cmp_bwd")
    dq_sel, dks, dvs = _attn_bwd(q_rot, 0, ks_rot, vs, o_sel, lse_sel, do_sel, "sel", sel, "attn_sel_bwd")
    dq_win, dkw, dvw = _attn_bwd(q_rot, 0, kw_rot, vw, o_win, lse_win, do_win, "win", None, "attn_win_bwd")
    drk, g["cmp_w1_k"], g["cmp_w2_k"], g["cmp_pe_k"] = _compress_bwd(rk, p["cmp_pe_k"], p["cmp_w1_k"], p["cmp_w2_k"], hid_k, dk_cmp)
    drv, g["cmp_w1_v"], g["cmp_w2_v"], g["cmp_pe_v"] = _compress_bwd(rv, p["cmp_pe_v"], p["cmp_w1_v"], p["cmp_w2_v"], hid_v, dv_cmp)
    dq = _rope([dq_sel, dq_win], 0, ATT_WIDTH, cos, sin, -1.0, _MXU, "rope_dq", extra=(dq_cmp, 0))
    dks_t = _rope([_tokens_major(dks)], 0, 256, cos, sin, -1.0, _MXU, "rope_dks")
    dkw_t = _rope([_tokens_major(dkw)], 0, 256, cos, sin, -1.0, _MXU, "rope_dkw")
    dproj = jnp.concatenate(
        [dz, dxbc, dq] + [t.astype(_MXU) for t in (_from_rows16(drk), _from_rows16(drv))]
        + [dks_t, _tokens_major(dvs).astype(_MXU), dkw_t, _tokens_major(dvw).astype(_MXU)], axis=1)
    dsmall = jnp.concatenate([ddtr, dgate[:, GATE_COL0:GATE_COL0 + 3 * N_HEADS],
                              jnp.zeros((S, W_SMALL - SSD_HEADS - 3 * N_HEADS), F32)], axis=1).astype(_MXU)
    g["w_main"] = _mm(u, dproj, "tn", _MXU, "dw_in")
    g["w_small"] = _mm(u, dsmall, "tn", F32, "dw_in_small")
    du = _mm(dproj, p["w_main"], "nt", F32, "du_main")
    du = _mm(dsmall, p["w_small"], "nt", F32, "du_small", res=du)
    grad_x, g["attn_norm_w"] = _rms_bwd(du, x, rs1, p["attn_norm_w"], dh1, "attn_norm_bwd")
    return loss, grad_x, g


MESH_ID = pl.DeviceIdType.MESH


def _my_coords():
    return lax.axis_index("x"), lax.axis_index("y"), lax.axis_index("c")


def _flat_id(px, py, pc):
    return 4 * px + 2 * py + pc


def _peer(k):
    mx, my, mc = _my_coords()
    return (1 - mx if k & 4 else mx, 1 - my if k & 2 else my, 1 - mc if k & 1 else mc)


def _exchange(arrs, scatter, name):
    n = len(arrs)

    def body(*refs):
        ins, outs = refs[:n], refs[n:2 * n]
        send_sems, recv_sems, local_sems = refs[2 * n:]
        me = _flat_id(*_my_coords())
        copies = []
        for i in range(n):
            src_me = ins[i].at[me] if scatter else ins[i]
            local = pltpu.make_async_copy(src_me, outs[i].at[me], local_sems.at[i])
            local.start()
            copies.append(local)
        for k in range(1, N_DEV):
            peer = _peer(k)
            for i in range(n):
                src = ins[i].at[_flat_id(*peer)] if scatter else ins[i]
                cp = pltpu.make_async_remote_copy(src_ref=src, dst_ref=outs[i].at[me], send_sem=send_sems.at[i * 7 + k - 1],
                                                  recv_sem=recv_sems.at[i * 7 + k - 1], device_id=peer, device_id_type=MESH_ID)
                cp.start()
                copies.append(cp)
        for cp in copies:
            cp.wait()

    any_spec = pl.BlockSpec(memory_space=pl.ANY)
    out_shape = [jax.ShapeDtypeStruct(a.shape if scatter else (N_DEV,) + a.shape, a.dtype) for a in arrs]
    return pl.pallas_call(
        body, name=name, in_specs=[any_spec] * n, out_specs=[any_spec] * n, out_shape=out_shape,
        scratch_shapes=[pltpu.SemaphoreType.DMA((n * 7,)), pltpu.SemaphoreType.DMA((n * 7,)), pltpu.SemaphoreType.DMA((n,))],
        compiler_params=pltpu.CompilerParams(has_side_effects=True))(*arrs)


def _sum_parts(parts):
    P, R, C = parts.shape
    tr = _pick(R, (136, 8))

    def body(p_ref, o_ref):
        acc = p_ref[0]
        for j in range(1, P):
            acc = acc + p_ref[j]
        o_ref[...] = acc

    return pl.pallas_call(
        body, name="sum_small_grads", grid=(R // tr,), in_specs=[pl.BlockSpec((P, tr, C), lambda i: (0, i, 0))],
        out_specs=pl.BlockSpec((tr, C), lambda i: (i, 0)), out_shape=jax.ShapeDtypeStruct((R, C), F32),
        compiler_params=_cp(("parallel",)))(parts)


def _adam_sum(parts, w, m, v, name):
    P, R, C = parts.shape
    tr = _pick(R, (256, 128, 64, 32, 8)) if C <= 1024 else _pick(R, (128, 64, 32, 8))

    def body(p_ref, w_ref, m_ref, v_ref, g_ref, d_ref, nm_ref, nv_ref):
        g = p_ref[0].astype(F32)
        for j in range(1, P):
            g = g + p_ref[j].astype(F32)
        g_ref[...] = g
        nm = ADAM_B1 * m_ref[...] + (1.0 - ADAM_B1) * g
        nv = ADAM_B2 * v_ref[...] + (1.0 - ADAM_B2) * (g * g)
        nm_ref[...] = nm
        nv_ref[...] = nv
        m_hat = nm / (1.0 - ADAM_B1 ** ADAM_STEP)
        v_hat = nv / (1.0 - ADAM_B2 ** ADAM_STEP)
        d_ref[...] = -ADAM_LR * (m_hat / (jnp.sqrt(v_hat) + ADAM_EPS) + ADAM_WD * w_ref[...])

    blk = pl.BlockSpec((tr, C), lambda i: (i, 0))
    return pl.pallas_call(
        body, name=name, grid=(R // tr,), in_specs=[pl.BlockSpec((P, tr, C), lambda i: (0, i, 0)), blk, blk, blk],
        out_specs=[blk] * 4, out_shape=[jax.ShapeDtypeStruct((R, C), F32)] * 4, compiler_params=_cp(("parallel",)))(parts, w, m, v)


def _pack(arrs):
    rows = []
    for a in arrs:
        f = a.reshape(-1).astype(F32)
        f = jnp.pad(f, (0, (-f.shape[0]) % 128))
        rows.append(f.reshape(-1, 128))
    out = jnp.concatenate(rows, axis=0)
    return jnp.pad(out, ((0, (-out.shape[0]) % 8), (0, 0)))


def _unpack(pack, shapes):
    out, r = [], 0
    for s in shapes:
        n = int(np.prod(s))
        nr = -(-n // 128)
        out.append(pack[r:r + nr].reshape(-1)[:n].reshape(s))
        r += nr
    return out


_WEIGHTS = ["attn_norm_w", "w_in", "conv_w", "conv_b", "dt_bias", "a_log", "d_skip", "ssd_norm_w", "cmp_w1_k", "cmp_w2_k",
            "cmp_w1_v", "cmp_w2_v", "cmp_pe_k", "cmp_pe_v", "w_out", "ffn_norm_w", "w_gate", "w_up", "w_down", "final_norm_w"]
_BIG = ["w_in", "w_gate", "w_up", "w_down", "w_out", "cmp_w1_k", "cmp_w1_v"]
_COL_SHARDED = ("w_in", "w_gate", "w_up")
_REPLICATED = ["attn_norm_w", "conv_b", "dt_bias", "a_log", "d_skip", "ssd_norm_w", "cmp_pe_k", "cmp_pe_v", "ffn_norm_w",
               "final_norm_w"]
_SMALL_SHARDED = ["conv_w", "cmp_w2_k", "cmp_w2_v"]
_SMALL_FULL_SHAPES = {"attn_norm_w": (1, D_MODEL), "conv_b": (1, CONV_CH), "dt_bias": (1, SSD_HEADS), "a_log": (1, SSD_HEADS),
                      "d_skip": (1, SSD_HEADS), "ssd_norm_w": (1, SSD_WIDTH), "cmp_pe_k": (1, 32 * HD), "cmp_pe_v": (1, 32 * HD),
                      "ffn_norm_w": (1, D_MODEL), "final_norm_w": (1, D_MODEL), "conv_w": (CONV_K, CONV_CH),
                      "cmp_w2_k": (CMP_HID, HD), "cmp_w2_v": (CMP_HID, HD)}


def _cols_to_slabs(g):
    R = g.shape[0]
    return g.reshape(R, N_DEV, -1).transpose(1, 0, 2)


def _slabs_to_cols(s):
    return s.transpose(1, 0, 2).reshape(s.shape[1], -1)


def kernel(x, attn_norm_w, w_in, conv_w, conv_b, dt_bias, a_log, d_skip, ssd_norm_w, cmp_w1_k, cmp_w2_k, cmp_w1_v, cmp_w2_v, cmp_pe_k, cmp_pe_v, w_out, ffn_norm_w, w_gate, w_up, w_down, final_norm_w, loss_target, m_attn_norm_w, m_w_in, m_conv_w, m_conv_b, m_dt_bias, m_a_log, m_d_skip, m_ssd_norm_w, m_cmp_w1_k, m_cmp_w2_k, m_cmp_w1_v, m_cmp_w2_v, m_cmp_pe_k, m_cmp_pe_v, m_w_out, m_ffn_norm_w, m_w_gate, m_w_up, m_w_down, m_final_norm_w, v_attn_norm_w, v_w_in, v_conv_w, v_conv_b, v_dt_bias, v_a_log, v_d_skip, v_ssd_norm_w, v_cmp_w1_k, v_cmp_w2_k, v_cmp_w1_v, v_cmp_w2_v, v_cmp_pe_k, v_cmp_pe_v, v_w_out, v_ffn_norm_w, v_w_gate, v_w_up, v_w_down, v_final_norm_w):
    a = dict(locals())
    me = _flat_id(*_my_coords())

    small_in = _pack([cmp_w2_k[0], cmp_w2_v[0], conv_w[0]])
    gathered = _exchange([a[n][0].astype(_MXU) for n in _BIG] + [small_in], False, "gather_weights")
    full = {}
    for n, t in zip(_BIG, gathered[:-1]):
        full[n] = _slabs_to_cols(t) if n in _COL_SHARDED else t.reshape(-1, t.shape[-1])
    w2k, w2v, cw = [], [], []
    for d in range(N_DEV):
        parts = _unpack(gathered[-1][d], [cmp_w2_k.shape[1:], cmp_w2_v.shape[1:], conv_w.shape[1:]])
        w2k.append(parts[0]); w2v.append(parts[1]); cw.append(parts[2])
    w_main, w_small = _split_w_in(full["w_in"])
    p = dict(attn_norm_w=attn_norm_w, w_main=w_main, w_small=w_small, conv_w=jnp.concatenate(cw, axis=1), conv_b=conv_b,
             dt_bias=dt_bias, a_log=a_log, d_skip=d_skip, ssd_norm_w=ssd_norm_w, cmp_w1_k=full["cmp_w1_k"],
             cmp_w2_k=jnp.concatenate(w2k, axis=0).astype(_MXU), cmp_w1_v=full["cmp_w1_v"],
             cmp_w2_v=jnp.concatenate(w2v, axis=0).astype(_MXU), cmp_pe_k=cmp_pe_k.reshape(1, -1), cmp_pe_v=cmp_pe_v.reshape(1, -1),
             w_out=full["w_out"], ffn_norm_w=ffn_norm_w, w_gate=full["w_gate"], w_up=full["w_up"], w_down=full["w_down"],
             final_norm_w=final_norm_w.reshape(1, -1))

    loss_part, grad_x, g = _local_step(x[0], loss_target[0], p)
    loss = lax.psum(loss_part[0, 0], ("x", "y", "c"))
    g["w_in"] = _merge_w_in(g.pop("w_main"), g.pop("w_small"))

    slabs = [_cols_to_slabs(g[n]) if n in _COL_SHARDED else g[n].reshape(N_DEV, -1, g[n].shape[-1]) for n in _BIG]
    received = _exchange([s.astype(_MXU) for s in slabs], True, "scatter_grads")
    out = {}
    for n, parts in zip(_BIG, received):
        out[n] = _adam_sum(parts, a[n][0], a["m_" + n][0], a["v_" + n][0], "adam_" + n)

    small_names = _REPLICATED + _SMALL_SHARDED
    g_small = _pack([g[n] for n in small_names])
    g_sum = _sum_parts(_exchange([g_small], False, "gather_small_grads")[0])
    gs = dict(zip(small_names, _unpack(g_sum, [_SMALL_FULL_SHAPES[n] for n in small_names])))
    gs["conv_w"] = lax.dynamic_slice_in_dim(gs["conv_w"], me * conv_w.shape[2], conv_w.shape[2], axis=1)
    gs["cmp_w2_k"] = lax.dynamic_slice_in_dim(gs["cmp_w2_k"], me * cmp_w2_k.shape[1], cmp_w2_k.shape[1], axis=0)
    gs["cmp_w2_v"] = lax.dynamic_slice_in_dim(gs["cmp_w2_v"], me * cmp_w2_v.shape[1], cmp_w2_v.shape[1], axis=0)
    packs = [_pack([t[n] for n in small_names]) for t in
             (gs, a, {n: a["m_" + n] for n in small_names}, {n: a["v_" + n] for n in small_names})]
    res_small = _adam_sum(packs[0][None], packs[1], packs[2], packs[3], "adam_small")
    shapes = [a[n].shape for n in small_names]
    unpacked = [dict(zip(small_names, _unpack(r, shapes))) for r in res_small]
    for n in small_names:
        out[n] = tuple(u[n] for u in unpacked)

    outs = [loss, grad_x[None]]
    for j in range(4):
        for n in _WEIGHTS:
            outs.append(out[n][j].reshape(a[n].shape))
    return tuple(outs)
```

```python
import functools
import math

import numpy as np
import jax
import jax.numpy as jnp
from jax import lax
from jax.experimental import pallas as pl
from jax.experimental.pallas import tpu as pltpu

F32 = jnp.float32
_MXU = jnp.bfloat16
_HI = lax.Precision.HIGHEST

N_DEV = 8
D_MODEL = 2048
SSD_WIDTH = 1024
ATT_WIDTH = 1024
SSD_HEADS = 16
SSD_P = 64
SSD_N = 128
SSD_L = 128
SSD_G = 2
CONV_CH = 1536
CONV_K = 4
HD = 64
N_HEADS = 16
N_KV = 4
GRP = 4
CMP_HID = 256
SEL_BLOCK = 64
N_SELECT = 16
WINDOW = 512
ROPE_DIM = 16
ROPE_THETA = 500000.0
D_FF = 5632
EPS = 1e-6
NEG = -1e30
FORCE = 1e4
SCALE = HD ** -0.5
D_IN = 5184
W_MAIN = 5120
W_SMALL = 128
VMEM_LIMIT = 52 * 1024 * 1024

ADAM_LR, ADAM_B1, ADAM_B2, ADAM_EPS, ADAM_WD, ADAM_STEP = 0.001, 0.9, 0.999, 1e-08, 0.01, 10


def _pick(n, cands):
    for c in cands:
        if n % c == 0:
            return c
    return n


def _cp(sem=None):
    return pltpu.CompilerParams(dimension_semantics=sem, vmem_limit_bytes=VMEM_LIMIT)


def _sigmoid(x):
    return 1.0 / (1.0 + jnp.exp(-x))


def _dot(a, b, dims, hi=False):
    dn = {"nn": (((1,), (0,)), ((), ())), "nt": (((1,), (1,)), ((), ())), "tn": (((0,), (0,)), ((), ()))}[dims]
    if hi:
        return lax.dot_general(a.astype(F32), b.astype(F32), dn, precision=_HI, preferred_element_type=F32)
    return lax.dot_general(a.astype(_MXU), b.astype(_MXU), dn, preferred_element_type=F32)


def _mm(a, b, mode, out_dtype, name, res=None):
    if mode == "nn":
        (M, K), N = a.shape, b.shape[1]
    elif mode == "nt":
        (M, K), N = a.shape, b.shape[0]
    else:
        (K, M), N = a.shape, b.shape[1]
    tm, tn, tk = _pick(M, (1024, 512, 256, 128)), _pick(N, (1024, 512, 256, 128)), _pick(K, (512, 256, 128))
    nk = K // tk
    a_spec = pl.BlockSpec((tk, tm), lambda i, j, k: (k, i)) if mode == "tn" else pl.BlockSpec((tm, tk), lambda i, j, k: (i, k))
    b_spec = pl.BlockSpec((tn, tk), lambda i, j, k: (j, k)) if mode == "nt" else pl.BlockSpec((tk, tn), lambda i, j, k: (k, j))
    o_spec = pl.BlockSpec((tm, tn), lambda i, j, k: (i, j))

    def body(*refs):
        if res is None:
            a_ref, b_ref, o_ref, acc = refs
        else:
            a_ref, b_ref, r_ref, o_ref, acc = refs
        k = pl.program_id(2)

        @pl.when(k == 0)
        def _():
            acc[...] = jnp.zeros_like(acc)

        acc[...] += _dot(a_ref[...], b_ref[...], mode)

        @pl.when(k == nk - 1)
        def _():
            r = acc[...]
            if res is not None:
                r = r + r_ref[...].astype(F32)
            o_ref[...] = r.astype(out_dtype)

    ins, specs = [a, b], [a_spec, b_spec]
    if res is not None:
        ins.append(res)
        specs.append(o_spec)
    return pl.pallas_call(
        body, name=name, grid=(M // tm, N // tn, nk), in_specs=specs, out_specs=o_spec,
        out_shape=jax.ShapeDtypeStruct((M, N), out_dtype), scratch_shapes=[pltpu.VMEM((tm, tn), F32)],
        compiler_params=_cp(("parallel", "parallel", "arbitrary")))(*ins)


def _ffn_up(v, w_gate, w_up):
    S, D = v.shape
    F = w_gate.shape[1]
    tm, tn, tk = _pick(S, (1024, 512, 256, 128)), _pick(F, (512, 256, 128)), _pick(D, (512, 256, 128))
    nk = D // tk

    def body(v_ref, wg_ref, wu_ref, gt_ref, up_ref, act_ref, accg, accu):
        k = pl.program_id(2)

        @pl.when(k == 0)
        def _():
            accg[...] = jnp.zeros_like(accg)
            accu[...] = jnp.zeros_like(accu)

        vv = v_ref[...]
        accg[...] += _dot(vv, wg_ref[...], "nn")
        accu[...] += _dot(vv, wu_ref[...], "nn")

        @pl.when(k == nk - 1)
        def _():
            g, u = accg[...], accu[...]
            gt_ref[...] = g
            up_ref[...] = u
            act_ref[...] = (g * _sigmoid(g) * u).astype(act_ref.dtype)

    o_spec = pl.BlockSpec((tm, tn), lambda i, j, k: (i, j))
    w_spec = pl.BlockSpec((tk, tn), lambda i, j, k: (k, j))
    return pl.pallas_call(
        body, name="ffn_up", grid=(S // tm, F // tn, nk),
        in_specs=[pl.BlockSpec((tm, tk), lambda i, j, k: (i, k)), w_spec, w_spec], out_specs=[o_spec, o_spec, o_spec],
        out_shape=[jax.ShapeDtypeStruct((S, F), F32), jax.ShapeDtypeStruct((S, F), F32), jax.ShapeDtypeStruct((S, F), _MXU)],
        scratch_shapes=[pltpu.VMEM((tm, tn), F32), pltpu.VMEM((tm, tn), F32)],
        compiler_params=_cp(("parallel", "parallel", "arbitrary")))(v, w_gate, w_up)


def _ffn_dact(dh2, w_down, gt, up):
    S, D = dh2.shape
    F = w_down.shape[0]
    tm, tn, tk = _pick(S, (1024, 512, 256, 128)), _pick(F, (512, 256, 128)), _pick(D, (512, 256, 128))
    nk = D // tk

    def body(d_ref, w_ref, gt_ref, up_ref, dg_ref, du_ref, acc):
        k = pl.program_id(2)

        @pl.when(k == 0)
        def _():
            acc[...] = jnp.zeros_like(acc)

        acc[...] += _dot(d_ref[...], w_ref[...], "nt")

        @pl.when(k == nk - 1)
        def _():
            da, g, u = acc[...], gt_ref[...], up_ref[...]
            s = _sigmoid(g)
            dg_ref[...] = (da * u * (s * (1.0 + g * (1.0 - s)))).astype(dg_ref.dtype)
            du_ref[...] = (da * (g * s)).astype(du_ref.dtype)

    o_spec = pl.BlockSpec((tm, tn), lambda i, j, k: (i, j))
    return pl.pallas_call(
        body, name="ffn_dact", grid=(S // tm, F // tn, nk),
        in_specs=[pl.BlockSpec((tm, tk), lambda i, j, k: (i, k)), pl.BlockSpec((tn, tk), lambda i, j, k: (j, k)), o_spec, o_spec],
        out_specs=[o_spec, o_spec],
        out_shape=[jax.ShapeDtypeStruct((S, F), _MXU), jax.ShapeDtypeStruct((S, F), _MXU)],
        scratch_shapes=[pltpu.VMEM((tm, tn), F32)],
        compiler_params=_cp(("parallel", "parallel", "arbitrary")))(dh2, w_down, gt, up)


def _rms_fwd(x, w, name):
    S, D = x.shape
    tr = _pick(S, (256, 128))

    def body(x_ref, w_ref, xn_ref, rs_ref):
        xv = x_ref[...]
        rs = lax.rsqrt(jnp.mean(xv * xv, axis=-1, keepdims=True) + EPS)
        xn_ref[...] = ((xv * rs) * w_ref[...]).astype(xn_ref.dtype)
        rs_ref[...] = rs

    return pl.pallas_call(
        body, name=name, grid=(S // tr,),
        in_specs=[pl.BlockSpec((tr, D), lambda i: (i, 0)), pl.BlockSpec((1, D), lambda i: (0, 0))],
        out_specs=[pl.BlockSpec((tr, D), lambda i: (i, 0)), pl.BlockSpec((tr, 1), lambda i: (i, 0))],
        out_shape=[jax.ShapeDtypeStruct((S, D), _MXU), jax.ShapeDtypeStruct((S, 1), F32)],
        compiler_params=_cp(("parallel",)))(x, w)


def _rms_bwd(dyn, x, rs, w, res, name):
    S, D = x.shape
    tr = _pick(S, (256, 128))

    def body(dy_ref, x_ref, rs_ref, w_ref, res_ref, dx_ref, dw_ref):
        @pl.when(pl.program_id(0) == 0)
        def _():
            dw_ref[...] = jnp.zeros_like(dw_ref)

        dy, r = dy_ref[...].astype(F32), rs_ref[...]
        xhat = x_ref[...] * r
        dw_ref[...] += jnp.sum(dy * xhat, axis=0, keepdims=True)
        dxhat = dy * w_ref[...]
        dx_ref[...] = res_ref[...] + r * (dxhat - xhat * jnp.mean(dxhat * xhat, axis=-1, keepdims=True))

    row = pl.BlockSpec((tr, D), lambda i: (i, 0))
    vec = pl.BlockSpec((1, D), lambda i: (0, 0))
    return pl.pallas_call(
        body, name=name, grid=(S // tr,),
        in_specs=[row, row, pl.BlockSpec((tr, 1), lambda i: (i, 0)), vec, row], out_specs=[row, vec],
        out_shape=[jax.ShapeDtypeStruct((S, D), F32), jax.ShapeDtypeStruct((1, D), F32)],
        compiler_params=_cp(("arbitrary",)))(dyn, x, rs, w, res)


def _final_loss(h2, w, tgt):
    S, D = h2.shape
    tr = _pick(S, (256, 128))

    def body(h_ref, w_ref, t_ref, loss_ref, dh_ref, dw_ref):
        @pl.when(pl.program_id(0) == 0)
        def _():
            dw_ref[...] = jnp.zeros_like(dw_ref)
            loss_ref[...] = jnp.zeros_like(loss_ref)

        hv, wv = h_ref[...], w_ref[...]
        rs = lax.rsqrt(jnp.mean(hv * hv, axis=-1, keepdims=True) + EPS)
        xhat = hv * rs
        err = xhat * wv - t_ref[...]
        row = jnp.mean(err * err, axis=-1, keepdims=True)
        loss_ref[...] += 0.5 * jnp.sum(row, axis=0, keepdims=True)
        dy = err * (1.0 / D)
        dw_ref[...] += jnp.sum(dy * xhat, axis=0, keepdims=True)
        dxhat = dy * wv
        dh_ref[...] = rs * (dxhat - xhat * jnp.mean(dxhat * xhat, axis=-1, keepdims=True))

    row = pl.BlockSpec((tr, D), lambda i: (i, 0))
    vec = pl.BlockSpec((1, D), lambda i: (0, 0))
    return pl.pallas_call(
        body, name="final_loss", grid=(S // tr,), in_specs=[row, vec, row],
        out_specs=[pl.BlockSpec((1, 1), lambda i: (0, 0)), row, vec],
        out_shape=[jax.ShapeDtypeStruct((1, 1), F32), jax.ShapeDtypeStruct((S, D), F32), jax.ShapeDtypeStruct((1, D), F32)],
        compiler_params=_cp(("arbitrary",)))(h2, w, tgt)


def _shift_rows(x, k, rows):
    if k == 0:
        return x
    S = x.shape[0]
    r = pltpu.roll(x, k % S, axis=0)
    ok = (rows >= k) if k > 0 else (rows < S + k)
    return jnp.where(ok, r, 0.0)


XBC_COL0 = SSD_WIDTH // 128


def _conv_fwd(proj, conv_w, conv_b):
    S = proj.shape[0]
    nct = CONV_CH // 128

    def body(x_ref, w_ref, b_ref, o_ref):
        x = x_ref[...]
        rows = lax.broadcasted_iota(jnp.int32, x.shape, 0)
        c = b_ref[...] + w_ref[3:4, :] * x
        for k in range(1, CONV_K):
            c = c + w_ref[3 - k:4 - k, :] * _shift_rows(x, k, rows)
        o_ref[...] = c * _sigmoid(c)

    return pl.pallas_call(
        body, name="conv_fwd", grid=(nct,),
        in_specs=[pl.BlockSpec((S, 128), lambda j: (0, XBC_COL0 + j)), pl.BlockSpec((CONV_K, 128), lambda j: (0, j)),
                  pl.BlockSpec((1, 128), lambda j: (0, j))],
        out_specs=pl.BlockSpec((S, 128), lambda j: (0, j)),
        out_shape=jax.ShapeDtypeStruct((S, CONV_CH), F32), compiler_params=_cp(("parallel",)))(proj, conv_w, conv_b)


def _conv_bwd(proj, conv_w, conv_b, dxa):
    S = proj.shape[0]
    nct = CONV_CH // 128

    def body(x_ref, w_ref, b_ref, d_ref, dx_ref, dw_ref, db_ref):
        x = x_ref[...]
        rows = lax.broadcasted_iota(jnp.int32, x.shape, 0)
        xs = [_shift_rows(x, k, rows) for k in range(CONV_K)]
        c = b_ref[...] + w_ref[3:4, :] * x
        for k in range(1, CONV_K):
            c = c + w_ref[3 - k:4 - k, :] * xs[k]
        s = _sigmoid(c)
        dc = d_ref[...] * (s * (1.0 + c * (1.0 - s)))
        dx = w_ref[3:4, :] * dc
        for k in range(1, CONV_K):
            dx = dx + w_ref[3 - k:4 - k, :] * _shift_rows(dc, -k, rows)
        dx_ref[...] = dx.astype(dx_ref.dtype)
        for k in range(CONV_K):
            dw_ref[3 - k:4 - k, :] = jnp.sum(dc * xs[k], axis=0, keepdims=True)
        db_ref[...] = jnp.sum(dc, axis=0, keepdims=True)

    col = pl.BlockSpec((S, 128), lambda j: (0, j))
    return pl.pallas_call(
        body, name="conv_bwd", grid=(nct,),
        in_specs=[pl.BlockSpec((S, 128), lambda j: (0, XBC_COL0 + j)), pl.BlockSpec((CONV_K, 128), lambda j: (0, j)),
                  pl.BlockSpec((1, 128), lambda j: (0, j)), col],
        out_specs=[col, pl.BlockSpec((CONV_K, 128), lambda j: (0, j)), pl.BlockSpec((1, 128), lambda j: (0, j))],
        out_shape=[jax.ShapeDtypeStruct((S, CONV_CH), _MXU), jax.ShapeDtypeStruct((CONV_K, CONV_CH), F32),
                   jax.ShapeDtypeStruct((1, CONV_CH), F32)],
        compiler_params=_cp(("parallel",)))(proj, conv_w, conv_b, dxa)


def _ssd_consts():
    L = SSD_L
    r = lax.broadcasted_iota(jnp.int32, (L, L), 0)
    c = lax.broadcasted_iota(jnp.int32, (L, L), 1)
    causal = r >= c
    upper = (r <= c).astype(F32)
    hr = lax.broadcasted_iota(jnp.int32, (SSD_HEADS, SSD_WIDTH), 0)
    hc = lax.broadcasted_iota(jnp.int32, (SSD_HEADS, SSD_WIDTH), 1)
    expand = (lax.shift_right_logical(hc, 6) == hr).astype(F32)
    return causal, causal.astype(F32), upper, expand


def _softplus(x):
    return jnp.maximum(x, 0.0) + jnp.log(1.0 + jnp.exp(-jnp.abs(x)))


def _ssd_scalars(dtr, dt_bias, a_log, tri, upper, expand):
    dt = _softplus(dtr + dt_bias)
    A = -jnp.exp(a_log)
    adt = dt * A
    acum = _dot(tri, adt, "nn", hi=True)
    acum_t = _dot(adt, upper, "tn", hi=True)
    alast = acum[SSD_L - 1:SSD_L, :]
    e = jnp.exp(acum)
    wdec = jnp.exp(alast - acum)
    gam = jnp.exp(alast)
    ex = lambda t: _dot(t, expand, "nn", hi=True)
    gam8 = jnp.broadcast_to(gam, (8, SSD_HEADS))
    return dt, A, acum, acum_t, e, wdec, gam, ex(dt), ex(e), ex(wdec), ex(gam8)[0:1, :]


def _ssd_fwd(proj, proj_small, xa, dt_bias, a_log, d_skip, norm_w):
    S = proj.shape[0]
    L, N, W = SSD_L, SSD_N, SSD_WIDTH
    nc = S // L

    def body(z_ref, xa_ref, dtr_ref, dtb_ref, al_ref, dsk_ref, nw_ref, yo_ref, y_ref, rs_ref, hs_ref, h_scr, y_scr):
        @pl.when(pl.program_id(0) == 0)
        def _():
            h_scr[...] = jnp.zeros_like(h_scr)

        causal, tri, upper, expand = _ssd_consts()
        dt, A, acum, acum_t, e, wdec, gam, dtE, eE, wE, gamE = _ssd_scalars(dtr_ref[:, 0:SSD_HEADS], dtb_ref[...], al_ref[...], tri, upper, expand)
        xs = xa_ref[:, 0:W]
        X = xs * dtE
        XW = X * wE
        hs_ref[0] = h_scr[...]
        for g in range(SSD_G):
            gs = slice(g * 512, (g + 1) * 512)
            Bg = xa_ref[:, W + g * N:W + (g + 1) * N]
            Cg = xa_ref[:, W + SSD_G * N + g * N:W + SSD_G * N + (g + 1) * N]
            Hg = h_scr[:, gs]
            CB = _dot(Cg, Bg, "nt")
            yoff = _dot(Cg, Hg, "nn") * eE[:, gs]
            st = _dot(Bg, XW[:, gs], "tn")
            for j in range(8):
                h = g * 8 + j
                hsl = slice(h * SSD_P, (h + 1) * SSD_P)
                lam = jnp.exp(jnp.where(causal, acum[:, h:h + 1] - acum_t[h:h + 1, :], -jnp.inf))
                y_scr[:, hsl] = _dot(CB * lam, X[:, hsl], "nn") + yoff[:, j * SSD_P:(j + 1) * SSD_P]
            h_scr[:, gs] = gamE[:, gs] * Hg + st
        dskE = _dot(jnp.broadcast_to(dsk_ref[...], (8, SSD_HEADS)), expand, "nn", hi=True)[0:1, :]
        y = y_scr[...] + dskE * xs
        y_ref[...] = y
        zv = z_ref[...]
        yg = y * (zv * _sigmoid(zv))
        rs = lax.rsqrt(jnp.mean(yg * yg, axis=-1, keepdims=True) + EPS)
        rs_ref[...] = rs
        yo_ref[...] = ((yg * rs) * nw_ref[...]).astype(yo_ref.dtype)

    p16 = pl.BlockSpec((1, SSD_HEADS), lambda c: (0, 0))
    return pl.pallas_call(
        body, name="ssd_fwd", grid=(nc,),
        in_specs=[pl.BlockSpec((L, W), lambda c: (c, 0)), pl.BlockSpec((L, CONV_CH), lambda c: (c, 0)),
                  pl.BlockSpec((L, W_SMALL), lambda c: (c, 0)), p16, p16, p16, pl.BlockSpec((1, W), lambda c: (0, 0))],
        out_specs=[pl.BlockSpec((L, W), lambda c: (c, 0)), pl.BlockSpec((L, W), lambda c: (c, 0)),
                   pl.BlockSpec((L, 1), lambda c: (c, 0)), pl.BlockSpec((1, N, W), lambda c: (c, 0, 0))],
        out_shape=[jax.ShapeDtypeStruct((S, W), _MXU), jax.ShapeDtypeStruct((S, W), F32), jax.ShapeDtypeStruct((S, 1), F32),
                   jax.ShapeDtypeStruct((nc, N, W), F32)],
        scratch_shapes=[pltpu.VMEM((N, W), F32), pltpu.VMEM((L, W), F32)],
        compiler_params=_cp(("arbitrary",)))(proj, xa, proj_small, dt_bias, a_log, d_skip, norm_w)


def _ssd_bwd(dmixed, proj, proj_small, xa, y, rs2, hs, dt_bias, a_log, d_skip, norm_w):
    S = proj.shape[0]
    L, N, W, H = SSD_L, SSD_N, SSD_WIDTH, SSD_HEADS
    nc = S // L

    def body(dyo_ref, z_ref, xa_ref, dtr_ref, y_ref, rs_ref, hs_ref, dtb_ref, al_ref, dsk_ref, nw_ref,
             dz_ref, dxa_ref, ddtr_ref, ddtb_ref, dal_ref, ddsk_ref, dnw_ref, dh_scr, dx_scr):
        @pl.when(pl.program_id(0) == 0)
        def _():
            dh_scr[...] = jnp.zeros_like(dh_scr)
            ddtb_ref[...] = jnp.zeros_like(ddtb_ref)
            dal_ref[...] = jnp.zeros_like(dal_ref)
            ddsk_ref[...] = jnp.zeros_like(ddsk_ref)
            dnw_ref[...] = jnp.zeros_like(dnw_ref)

        causal, tri, upper, expand = _ssd_consts()
        heads = lambda t: _dot(t, expand, "nt", hi=True)
        onehot = lambda h: (lax.broadcasted_iota(jnp.int32, (1, H), 1) == h).astype(F32)

        zv, yv, rs = z_ref[...], y_ref[...], rs_ref[...]
        sz = _sigmoid(zv)
        zs = zv * sz
        xhat = (yv * zs) * rs
        dyo = dyo_ref[...].astype(F32)
        dnw_ref[...] += jnp.sum(dyo * xhat, axis=0, keepdims=True)
        dxhat = dyo * nw_ref[...]
        dyg = rs * (dxhat - xhat * jnp.mean(dxhat * xhat, axis=-1, keepdims=True))
        dz_ref[...] = (dyg * yv * (sz * (1.0 + zv * (1.0 - sz)))).astype(dz_ref.dtype)
        dy = dyg * zs

        dtr = dtr_ref[:, 0:H]
        dt, A, acum, acum_t, e, wdec, gam, dtE, eE, wE, gamE = _ssd_scalars(dtr, dtb_ref[...], al_ref[...], tri, upper, expand)
        xs = xa_ref[:, 0:W]
        X = xs * dtE
        XW = X * wE
        dskE = _dot(jnp.broadcast_to(dsk_ref[...], (8, H)), expand, "nn", hi=True)[0:1, :]
        ddsk_ref[...] += heads(jnp.broadcast_to(jnp.sum(dy * xs, axis=0, keepdims=True), (8, W)))[0:1, :]

        dYe = dy * eE
        dacum = jnp.zeros((L, H), F32)
        de_full = []
        dw_full = []
        dgam_full = []
        for g in range(SSD_G):
            gs = slice(g * 512, (g + 1) * 512)
            Bg = xa_ref[:, W + g * N:W + (g + 1) * N]
            Cg = xa_ref[:, W + SSD_G * N + g * N:W + SSD_G * N + (g + 1) * N]
            Hg = hs_ref[0, :, gs]
            dHn = dh_scr[:, gs]
            CH = _dot(Cg, Hg, "nn")
            de_full.append(dy[:, gs] * CH)
            dC = _dot(dYe[:, gs], Hg, "nt")
            dHs = gamE[:, gs] * dHn + _dot(Cg, dYe[:, gs], "tn")
            dgam_full.append(jnp.sum(dHn * Hg, axis=0, keepdims=True))
            BdS = _dot(Bg, dHn, "nn")
            dB = _dot(XW[:, gs], dHn, "nt")
            dx_scr[:, gs] = BdS * wE[:, gs]
            dw_full.append(BdS * X[:, gs])
            CB = _dot(Cg, Bg, "nt")
            dCB = jnp.zeros((L, L), F32)
            for j in range(8):
                h = g * 8 + j
                hsl = slice(h * SSD_P, (h + 1) * SSD_P)
                lam = jnp.exp(jnp.where(causal, acum[:, h:h + 1] - acum_t[h:h + 1, :], -jnp.inf))
                M = CB * lam
                dM = _dot(dy[:, hsl], X[:, hsl], "nt")
                dx_scr[:, hsl] += _dot(M, dy[:, hsl], "tn")
                dCB = dCB + dM * lam
                Q = dM * M
                rowsum = jnp.sum(Q, axis=1, keepdims=True)
                colsum = _dot(Q, jnp.ones((L, 8), F32), "tn", hi=True)[:, 0:1]
                dacum = dacum + (rowsum - colsum) * onehot(h)
            dC = dC + _dot(dCB, Bg, "nn")
            dB = dB + _dot(dCB, Cg, "tn")
            dxa_ref[:, W + g * N:W + (g + 1) * N] = dB
            dxa_ref[:, W + SSD_G * N + g * N:W + SSD_G * N + (g + 1) * N] = dC
            dh_scr[:, gs] = dHs

        de16 = heads(jnp.concatenate(de_full, axis=1))
        dw16 = heads(jnp.concatenate(dw_full, axis=1))
        dgam16 = heads(jnp.broadcast_to(jnp.concatenate(dgam_full, axis=1), (8, W)))[0:1, :]
        dacum = dacum + de16 * e - dw16 * wdec
        dlast = jnp.sum(dw16 * wdec, axis=0, keepdims=True) + dgam16 * gam
        lastrow = (lax.broadcasted_iota(jnp.int32, (L, 1), 0) == L - 1).astype(F32)
        dacum = dacum + lastrow * dlast
        da = _dot(tri, dacum, "tn", hi=True)
        dX = dx_scr[...]
        ddt = da * A + heads(dX * xs)
        dA = jnp.sum(da * dt, axis=0, keepdims=True)
        dal_ref[...] += dA * A
        ddtr = ddt * _sigmoid(dtr + dtb_ref[...])
        ddtb_ref[...] += jnp.sum(ddtr, axis=0, keepdims=True)
        ddtr_ref[...] = ddtr
        dxa_ref[:, 0:W] = dX * dtE + dy * dskE

    p16 = pl.BlockSpec((1, H), lambda c: (0, 0))
    rev = lambda c: (nc - 1 - c, 0)
    return pl.pallas_call(
        body, name="ssd_bwd", grid=(nc,),
        in_specs=[pl.BlockSpec((L, W), rev), pl.BlockSpec((L, W), rev), pl.BlockSpec((L, CONV_CH), rev),
                  pl.BlockSpec((L, W_SMALL), rev), pl.BlockSpec((L, W), rev), pl.BlockSpec((L, 1), rev),
                  pl.BlockSpec((1, N, W), lambda c: (nc - 1 - c, 0, 0)), p16, p16, p16, pl.BlockSpec((1, W), lambda c: (0, 0))],
        out_specs=[pl.BlockSpec((L, W), rev), pl.BlockSpec((L, CONV_CH), rev), pl.BlockSpec((L, H), rev),
                   p16, p16, p16, pl.BlockSpec((1, W), lambda c: (0, 0))],
        out_shape=[jax.ShapeDtypeStruct((S, W), _MXU), jax.ShapeDtypeStruct((S, CONV_CH), F32), jax.ShapeDtypeStruct((S, H), F32),
                   jax.ShapeDtypeStruct((1, H), F32), jax.ShapeDtypeStruct((1, H), F32), jax.ShapeDtypeStruct((1, H), F32),
                   jax.ShapeDtypeStruct((1, W), F32)],
        scratch_shapes=[pltpu.VMEM((N, W), F32), pltpu.VMEM((L, W), F32)],
        compiler_params=_cp(("arbitrary",)))(dmixed, proj, xa, proj_small, y, rs2, hs, dt_bias, a_log, d_skip, norm_w)


def _rope_tables(S):
    inv = 1.0 / (ROPE_THETA ** (jnp.arange(0, ROPE_DIM, 2, dtype=F32) / ROPE_DIM))
    ang = jnp.arange(S, dtype=F32)[:, None] * inv[None, :]
    cos, sin = jnp.cos(ang), jnp.sin(ang)
    half = ROPE_DIM // 2
    c64 = jnp.concatenate([cos, cos, jnp.ones((S, HD - ROPE_DIM), F32)], axis=1)
    s64 = jnp.concatenate([sin, sin, jnp.zeros((S, HD - ROPE_DIM), F32)], axis=1)
    del half
    return jnp.concatenate([c64, c64], axis=1), jnp.concatenate([s64, s64], axis=1)


def _rope(xs, blk0, width, cos, sin, sign, out_dtype, name, extra=None):
    S = xs[0].shape[0]
    tr = _pick(S, (512, 256, 128))
    nx = len(xs)

    def body(*refs):
        x_refs, c_ref, s_ref = refs[:nx], refs[nx], refs[nx + 1]
        e_ref = refs[nx + 2] if extra is not None else None
        o_ref = refs[-1]
        cv, sv = c_ref[...], s_ref[...] * sign
        lane = lax.broadcasted_iota(jnp.int32, (tr, 128), 1)
        first = (lane & (HD - 1)) < (ROPE_DIM // 2)
        for j in range(2):
            cs = slice(j * 128, (j + 1) * 128)
            xv = x_refs[0][:, cs].astype(F32)
            for r in x_refs[1:]:
                xv = xv + r[:, cs].astype(F32)
            rot = jnp.where(first, -pltpu.roll(xv, 128 - ROPE_DIM // 2, axis=1), pltpu.roll(xv, ROPE_DIM // 2, axis=1))
            out = xv * cv + rot * sv
            if extra is not None:
                out = out + e_ref[:, cs].astype(F32)
            o_ref[:, cs] = out.astype(out_dtype)

    t128 = pl.BlockSpec((tr, 128), lambda i, j: (i, 0))
    oblk = pl.BlockSpec((tr, 256), lambda i, j: (i, j))
    specs = [pl.BlockSpec((tr, 256), lambda i, j: (i, blk0 + j))] * nx + [t128, t128]
    ins = list(xs) + [cos, sin]
    if extra is not None:
        ins.append(extra[0])
        eb = extra[1]
        specs.append(pl.BlockSpec((tr, 256), lambda i, j: (i, eb + j)))
    return pl.pallas_call(
        body, name=name, grid=(S // tr, width // 256), in_specs=specs, out_specs=oblk,
        out_shape=jax.ShapeDtypeStruct((S, width), out_dtype), compiler_params=_cp(("parallel", "parallel")))(*ins)


def _compress_fwd(R, pe, w1, w2):
    NC = R.shape[1]
    half = 16 * HD

    def body(r_ref, pe_ref, w1_ref, w2_ref, o_ref, hid_ref):
        r = r_ref[0]
        a = _dot(r + pe_ref[:, 0:half], w1_ref[0:half, :], "nn")
        b = _dot(r + pe_ref[:, half:2 * half], w1_ref[half:2 * half, :], "nn")
        hid = a + pltpu.roll(b, NC - 1, axis=0)
        hid_ref[0] = hid
        out = _dot(hid * _sigmoid(hid), w2_ref[...], "nn")
        rows = lax.broadcasted_iota(jnp.int32, out.shape, 0)
        o_ref[0] = jnp.where(rows < NC - 1, out, 0.0).astype(o_ref.dtype)

    return pl.pallas_call(
        body, name="compress_fwd", grid=(N_KV,),
        in_specs=[pl.BlockSpec((1, NC, half), lambda h: (h, 0, 0)), pl.BlockSpec((1, 2 * half), lambda h: (0, 0)),
                  pl.BlockSpec((2 * half, CMP_HID), lambda h: (0, 0)), pl.BlockSpec((CMP_HID, HD), lambda h: (0, 0))],
        out_specs=[pl.BlockSpec((1, NC, HD), lambda h: (h, 0, 0)), pl.BlockSpec((1, NC, CMP_HID), lambda h: (h, 0, 0))],
        out_shape=[jax.ShapeDtypeStruct((N_KV, NC, HD), _MXU), jax.ShapeDtypeStruct((N_KV, NC, CMP_HID), F32)],
        compiler_params=_cp(("parallel",)))(R, pe, w1, w2)


def _compress_bwd(R, pe, w1, w2, hid, dout):
    NC = R.shape[1]
    half = 16 * HD

    def body(r_ref, pe_ref, w1_ref, w2_ref, hid_ref, do_ref, dr_ref, dw1_ref, dw2_ref, dpe_ref):
        @pl.when(pl.program_id(0) == 0)
        def _():
            dw1_ref[...] = jnp.zeros_like(dw1_ref)
            dw2_ref[...] = jnp.zeros_like(dw2_ref)
            dpe_ref[...] = jnp.zeros_like(dpe_ref)

        r, hv, do = r_ref[0], hid_ref[0], do_ref[0]
        s = _sigmoid(hv)
        dw2_ref[...] += _dot(hv * s, do, "tn")
        dhid = _dot(do, w2_ref[...], "nt") * (s * (1.0 + hv * (1.0 - s)))
        rows = lax.broadcasted_iota(jnp.int32, dhid.shape, 0)
        dhid = jnp.where(rows < NC - 1, dhid, 0.0)
        dhid_dn = pltpu.roll(dhid, 1, axis=0)
        dw1_ref[0:half, :] += _dot(r + pe_ref[:, 0:half], dhid, "tn")
        dw1_ref[half:2 * half, :] += _dot(r + pe_ref[:, half:2 * half], dhid_dn, "tn")
        dxt = _dot(dhid, w1_ref[0:half, :], "nt")
        dxb = _dot(dhid_dn, w1_ref[half:2 * half, :], "nt")
        dr_ref[0] = dxt + dxb
        dpe_ref[:, 0:half] += jnp.sum(dxt, axis=0, keepdims=True)
        dpe_ref[:, half:2 * half] += jnp.sum(dxb, axis=0, keepdims=True)

    return pl.pallas_call(
        body, name="compress_bwd", grid=(N_KV,),
        in_specs=[pl.BlockSpec((1, NC, half), lambda h: (h, 0, 0)), pl.BlockSpec((1, 2 * half), lambda h: (0, 0)),
                  pl.BlockSpec((2 * half, CMP_HID), lambda h: (0, 0)), pl.BlockSpec((CMP_HID, HD), lambda h: (0, 0)),
                  pl.BlockSpec((1, NC, CMP_HID), lambda h: (h, 0, 0)), pl.BlockSpec((1, NC, HD), lambda h: (h, 0, 0))],
        out_specs=[pl.BlockSpec((1, NC, half), lambda h: (h, 0, 0)), pl.BlockSpec((2 * half, CMP_HID), lambda h: (0, 0)),
                   pl.BlockSpec((CMP_HID, HD), lambda h: (0, 0)), pl.BlockSpec((1, 2 * half), lambda h: (0, 0))],
        out_shape=[jax.ShapeDtypeStruct((N_KV, NC, half), F32), jax.ShapeDtypeStruct((2 * half, CMP_HID), F32),
                   jax.ShapeDtypeStruct((CMP_HID, HD), F32), jax.ShapeDtypeStruct((1, 2 * half), F32)],
        compiler_params=_cp(("arbitrary",)))(R, pe, w1, w2, hid, dout)


def _attn_cfg(S, Sk, mode):
    tq = _pick(S, (256, 128))
    tk = Sk if mode == "cmp" else _pick(Sk, (256, 128))
    return tq, tk


def _kb_range(mode, q0, tq, tk):
    if mode == "cmp":
        return 0, 1
    hi = (q0 + tq - 1) // tk + 1
    if mode == "sel":
        return 0, hi
    return jnp.maximum(q0 - (WINDOW - 1), 0) // tk, hi


def _attn_bias(mode, q0, k0, tq, tk, sel):
    t = q0 + lax.broadcasted_iota(jnp.int32, (tq, tk), 0)
    k = k0 + lax.broadcasted_iota(jnp.int32, (tq, tk), 1)
    if mode == "cmp":
        ok = (k * 16 + 31) <= t
    elif mode == "win":
        ok = (k <= t) & ((t - k) < WINDOW)
    else:
        nb = sel.shape[1]
        er = lax.broadcasted_iota(jnp.int32, (nb, tk), 0)
        ec = k0 + lax.broadcasted_iota(jnp.int32, (nb, tk), 1)
        expand = (lax.shift_right_logical(ec, 6) == er).astype(_MXU)
        chosen = _dot(sel, expand, "nn") > 0.5
        ok = (k <= t) & chosen
    return jnp.where(ok, 0.0, NEG), ok


def _stack_heads(ref, tq):
    return jnp.concatenate([ref[:, g * HD:(g + 1) * HD] for g in range(GRP)], axis=0)


def _attn_fwd(q, qcol0, k, v, mode, sel, name):
    S, Sk = q.shape[0], k.shape[1]
    tq, tk = _attn_cfg(S, Sk, mode)
    R = GRP * tq

    def body(*refs):
        if mode == "sel":
            q_ref, k_ref, v_ref, sel_ref, o_ref, lse_ref, m_scr, l_scr, acc = refs
        else:
            q_ref, k_ref, v_ref, o_ref, lse_ref, m_scr, l_scr, acc = refs
        q0 = pl.program_id(1) * tq
        qs = _stack_heads(q_ref, tq).astype(_MXU)
        m_scr[...] = jnp.full_like(m_scr, NEG)
        l_scr[...] = jnp.zeros_like(l_scr)
        acc[...] = jnp.zeros_like(acc)
        selv = sel_ref[0].astype(_MXU) if mode == "sel" else None

        def step(kb, carry):
            k0 = pl.multiple_of(kb * tk, tk)
            kv, vv = k_ref[0, pl.ds(k0, tk), :], v_ref[0, pl.ds(k0, tk), :]
            bias, ok = _attn_bias(mode, q0, k0, tq, tk, selv)
            s = (_dot(qs, kv, "nt") * SCALE).reshape(GRP, tq, tk) + bias[None]
            m_old = m_scr[...].reshape(GRP, tq, 1)
            m_new = jnp.maximum(m_old, jnp.max(s, axis=-1, keepdims=True))
            p = jnp.exp(s - m_new)
            if mode == "cmp":
                p = p * ok.astype(F32)[None]
            alpha = jnp.exp(m_old - m_new)
            l_scr[...] = (alpha * l_scr[...].reshape(GRP, tq, 1) + jnp.sum(p, axis=-1, keepdims=True)).reshape(R, 1)
            acc[...] = alpha.reshape(R, 1) * acc[...] + _dot(p.reshape(R, tk), vv, "nn")
            m_scr[...] = m_new.reshape(R, 1)
            return carry

        lo, hi = _kb_range(mode, q0, tq, tk)
        lax.fori_loop(lo, hi, step, 0)
        l = l_scr[...]
        good = l > 0.0
        o = acc[...] * jnp.where(good, 1.0 / jnp.where(good, l, 1.0), 0.0)
        lse = jnp.where(good, m_scr[...] + jnp.log(jnp.where(good, l, 1.0)), -NEG)
        for g in range(GRP):
            o_ref[:, g * HD:(g + 1) * HD] = o[g * tq:(g + 1) * tq, :]
            lse_ref[0, :, g:g + 1] = lse[g * tq:(g + 1) * tq, :]

    kv_spec = pl.BlockSpec((1, Sk, HD), lambda h, i: (h, 0, 0))
    ins, specs = [q, k, v], [pl.BlockSpec((tq, GRP * HD), lambda h, i: (i, qcol0 + h)), kv_spec, kv_spec]
    if mode == "sel":
        ins.append(sel)
        specs.append(pl.BlockSpec((1, tq, sel.shape[2]), lambda h, i: (h, i, 0)))
    return pl.pallas_call(
        body, name=name, grid=(N_KV, S // tq), in_specs=specs,
        out_specs=[pl.BlockSpec((tq, GRP * HD), lambda h, i: (i, h)), pl.BlockSpec((1, tq, GRP), lambda h, i: (h, i, 0))],
        out_shape=[jax.ShapeDtypeStruct((S, ATT_WIDTH), F32), jax.ShapeDtypeStruct((N_KV, S, GRP), F32)],
        scratch_shapes=[pltpu.VMEM((R, 1), F32), pltpu.VMEM((R, 1), F32), pltpu.VMEM((R, HD), F32)],
        compiler_params=_cp(("parallel", "arbitrary")))(*ins)


def _attn_bwd(q, qcol0, k, v, o, lse, do, mode, sel, name):
    S, Sk = q.shape[0], k.shape[1]
    tq, tk = _attn_cfg(S, Sk, mode)
    R = GRP * tq

    def body(*refs):
        if mode == "sel":
            q_ref, k_ref, v_ref, o_ref, lse_ref, do_ref, sel_ref, dq_ref, dk_ref, dv_ref, dq_scr = refs
        else:
            q_ref, k_ref, v_ref, o_ref, lse_ref, do_ref, dq_ref, dk_ref, dv_ref, dq_scr = refs

        @pl.when(pl.program_id(1) == 0)
        def _():
            dk_ref[...] = jnp.zeros_like(dk_ref)
            dv_ref[...] = jnp.zeros_like(dv_ref)

        q0 = pl.program_id(1) * tq
        qs = _stack_heads(q_ref, tq).astype(_MXU)
        dos = _stack_heads(do_ref, tq)
        delta = jnp.sum(dos * _stack_heads(o_ref, tq), axis=-1, keepdims=True).reshape(GRP, tq, 1)
        lsev = jnp.concatenate([lse_ref[0, :, g:g + 1] for g in range(GRP)], axis=0).reshape(GRP, tq, 1)
        dos = dos.astype(_MXU)
        dq_scr[...] = jnp.zeros_like(dq_scr)
        selv = sel_ref[0].astype(_MXU) if mode == "sel" else None

        def step(kb, carry):
            k0 = pl.multiple_of(kb * tk, tk)
            kv, vv = k_ref[0, pl.ds(k0, tk), :], v_ref[0, pl.ds(k0, tk), :]
            bias, ok = _attn_bias(mode, q0, k0, tq, tk, selv)
            s = (_dot(qs, kv, "nt") * SCALE).reshape(GRP, tq, tk) + bias[None]
            p = jnp.exp(s - lsev)
            if mode == "cmp":
                p = p * ok.astype(F32)[None]
            dp = _dot(dos, vv, "nt").reshape(GRP, tq, tk)
            ds = (p * (dp - delta) * SCALE).reshape(R, tk)
            p2 = p.reshape(R, tk)
            dq_scr[...] += _dot(ds, kv, "nn")
            dk_ref[0, pl.ds(k0, tk), :] += _dot(ds, qs, "tn")
            dv_ref[0, pl.ds(k0, tk), :] += _dot(p2, dos, "tn")
            return carry

        lo, hi = _kb_range(mode, q0, tq, tk)
        lax.fori_loop(lo, hi, step, 0)
        for g in range(GRP):
            dq_ref[:, g * HD:(g + 1) * HD] = dq_scr[g * tq:(g + 1) * tq, :]

    kv_spec = pl.BlockSpec((1, Sk, HD), lambda h, i: (h, 0, 0))
    qo_spec = pl.BlockSpec((tq, GRP * HD), lambda h, i: (i, h))
    ins = [q, k, v, o, lse, do]
    specs = [pl.BlockSpec((tq, GRP * HD), lambda h, i: (i, qcol0 + h)), kv_spec, kv_spec, qo_spec,
             pl.BlockSpec((1, tq, GRP), lambda h, i: (h, i, 0)), qo_spec]
    if mode == "sel":
        ins.append(sel)
        specs.append(pl.BlockSpec((1, tq, sel.shape[2]), lambda h, i: (h, i, 0)))
    return pl.pallas_call(
        body, name=name, grid=(N_KV, S // tq), in_specs=specs, out_specs=[qo_spec, kv_spec, kv_spec],
        out_shape=[jax.ShapeDtypeStruct((S, ATT_WIDTH), F32), jax.ShapeDtypeStruct((N_KV, Sk, HD), F32),
                   jax.ShapeDtypeStruct((N_KV, Sk, HD), F32)],
        scratch_shapes=[pltpu.VMEM((R, HD), F32)],
        compiler_params=_cp(("parallel", "arbitrary")))(*ins)


def _select(q, qcol0, k_cmp, lse):
    S, NC = q.shape[0], k_cmp.shape[1]
    NB = S // SEL_BLOCK
    tq = _pick(S, (256, 128))
    R = GRP * tq
    ci = np.arange(NC)[:, None] * 16
    sj = np.arange(NB)[None, :] * SEL_BLOCK
    ov = np.clip(np.minimum(ci + 32, sj + SEL_BLOCK) - np.maximum(ci, sj), 0, None) / 32.0
    ov[NC - 1, :] = 0.0
    ov = jnp.asarray(ov, F32)

    def body(q_ref, k_ref, lse_ref, ov_ref, sel_ref):
        q0 = pl.program_id(1) * tq
        qs = _stack_heads(q_ref, tq)
        bias, ok = _attn_bias("cmp", q0, 0, tq, NC, None)
        s = (_dot(qs, k_ref[0], "nt") * SCALE).reshape(GRP, tq, NC) + bias[None]
        lsev = jnp.concatenate([lse_ref[0, :, g:g + 1] for g in range(GRP)], axis=0).reshape(GRP, tq, 1)
        p = jnp.exp(s - lsev) * ok.astype(F32)[None]
        imp4 = _dot(p.reshape(R, NC), ov_ref[...], "nn").reshape(GRP, tq, NB)
        imp = imp4[0] + imp4[1] + imp4[2] + imp4[3]
        blk = lax.broadcasted_iota(jnp.int32, (tq, NB), 1)
        cur = lax.shift_right_logical(q0 + lax.broadcasted_iota(jnp.int32, (tq, NB), 0), 6)
        imp = jnp.where((blk == 0) | (blk == cur) | (blk == cur - 1), FORCE, imp)
        imp = jnp.where(blk <= cur, imp, -1.0)
        rank = jnp.zeros((tq, NB), F32)
        for j in range(NB):
            col = imp[:, j:j + 1]
            ahead = (col > imp) | ((col == imp) & (blk > j))
            rank = rank + ahead.astype(F32)
        sel_ref[0] = ((rank < float(N_SELECT)) & (imp >= 0.0)).astype(F32)

    return pl.pallas_call(
        body, name="select_blocks", grid=(N_KV, S // tq),
        in_specs=[pl.BlockSpec((tq, GRP * HD), lambda h, i: (i, qcol0 + h)), pl.BlockSpec((1, NC, HD), lambda h, i: (h, 0, 0)),
                  pl.BlockSpec((1, tq, GRP), lambda h, i: (h, i, 0)), pl.BlockSpec((NC, NB), lambda h, i: (0, 0))],
        out_specs=pl.BlockSpec((1, tq, NB), lambda h, i: (h, i, 0)),
        out_shape=jax.ShapeDtypeStruct((N_KV, S, NB), F32), compiler_params=_cp(("parallel", "parallel")))(q, k_cmp, lse, ov)


GATE_COL0 = SSD_HEADS


def _combine_fwd(o_cmp, o_sel, o_win, proj_small):
    S = o_cmp.shape[0]
    tr = _pick(S, (256, 128))

    def body(oc_ref, os_ref, ow_ref, g_ref, y_ref):
        gate = _sigmoid(g_ref[...])
        for h in range(N_HEADS):
            hs = slice(h * HD, (h + 1) * HD)
            c = GATE_COL0 + 3 * h
            y = gate[:, c:c + 1] * oc_ref[:, hs] + gate[:, c + 1:c + 2] * os_ref[:, hs] + gate[:, c + 2:c + 3] * ow_ref[:, hs]
            y_ref[:, hs] = y.astype(y_ref.dtype)

    row = pl.BlockSpec((tr, ATT_WIDTH), lambda i: (i, 0))
    return pl.pallas_call(
        body, name="combine_fwd", grid=(S // tr,), in_specs=[row, row, row, pl.BlockSpec((tr, W_SMALL), lambda i: (i, 0))],
        out_specs=row, out_shape=jax.ShapeDtypeStruct((S, ATT_WIDTH), _MXU), compiler_params=_cp(("parallel",)))(
            o_cmp, o_sel, o_win, proj_small)


def _combine_bwd(dmixed, o_cmp, o_sel, o_win, proj_small):
    S = o_cmp.shape[0]
    tr = _pick(S, (256, 128))

    def body(dy_ref, oc_ref, os_ref, ow_ref, g_ref, dc_ref, ds_ref, dw_ref, dg_ref):
        gate = _sigmoid(g_ref[...])
        lane = lax.broadcasted_iota(jnp.int32, (1, W_SMALL), 1)
        dg = jnp.zeros((tr, W_SMALL), F32)
        for h in range(N_HEADS):
            hs = slice(h * HD, (h + 1) * HD)
            dy = dy_ref[:, hs].astype(F32)
            for b, (o_ref, d_ref) in enumerate(((oc_ref, dc_ref), (os_ref, ds_ref), (ow_ref, dw_ref))):
                c = GATE_COL0 + 3 * h + b
                gv = gate[:, c:c + 1]
                d_ref[:, hs] = gv * dy
                dgate = jnp.sum(dy * o_ref[:, hs], axis=-1, keepdims=True) * (gv * (1.0 - gv))
                dg = dg + dgate * (lane == c).astype(F32)
        dg_ref[...] = dg

    row = pl.BlockSpec((tr, ATT_WIDTH), lambda i: (i, 0))
    small = pl.BlockSpec((tr, W_SMALL), lambda i: (i, 0))
    return pl.pallas_call(
        body, name="combine_bwd", grid=(S // tr,),
        in_specs=[pl.BlockSpec((tr, ATT_WIDTH), lambda i: (i, 1)), row, row, row, small], out_specs=[row, row, row, small],
        out_shape=[jax.ShapeDtypeStruct((S, ATT_WIDTH), F32)] * 3 + [jax.ShapeDtypeStruct((S, W_SMALL), F32)],
        compiler_params=_cp(("parallel",)))(dmixed, o_cmp, o_sel, o_win, proj_small)


def _heads_major(x):
    S = x.shape[0]
    return x.reshape(S, N_KV, HD).transpose(1, 0, 2)


def _tokens_major(x):
    return x.transpose(1, 0, 2).reshape(x.shape[1], N_KV * HD)


def _to_rows16(x):
    S = x.shape[0]
    return x.reshape(S // 16, 16, N_KV, HD).transpose(2, 0, 1, 3).reshape(N_KV, S // 16, 16 * HD)


def _from_rows16(r):
    NC = r.shape[1]
    return r.reshape(N_KV, NC, 16, HD).transpose(1, 2, 0, 3).reshape(NC * 16, N_KV * HD)


DT_COL0 = SSD_WIDTH + CONV_CH
GATE_IN_COL0 = D_IN - 3 * N_HEADS


def _split_w_in(w):
    main = jnp.concatenate([w[:, :DT_COL0], w[:, DT_COL0 + SSD_HEADS:GATE_IN_COL0]], axis=1)
    small = jnp.concatenate([w[:, DT_COL0:DT_COL0 + SSD_HEADS], w[:, GATE_IN_COL0:],
                             jnp.zeros((w.shape[0], W_SMALL - SSD_HEADS - 3 * N_HEADS), w.dtype)], axis=1)
    return main, small


def _merge_w_in(main, small):
    return jnp.concatenate([main[:, :DT_COL0], small[:, :SSD_HEADS].astype(main.dtype), main[:, DT_COL0:],
                            small[:, SSD_HEADS:SSD_HEADS + 3 * N_HEADS].astype(main.dtype)], axis=1)


QB, KCB, VCB, KSB, VSB, KWB, VWB = 10, 14, 15, 16, 17, 18, 19


def _col256(a, b):
    return a[:, b * 256:(b + 1) * 256]


def _local_step(x, tgt, p, late_weights=None, ffn_grads_ready=None):
    S = x.shape[0]
    cos, sin = _rope_tables(S)

    u, rs1 = _rms_fwd(x, p["attn_norm_w"], "attn_norm")
    proj = _mm(u, p["w_main"], "nn", F32, "in_proj")
    proj_small = _mm(u, p["w_small"], "nn", F32, "in_proj_small")
    xa = _conv_fwd(proj, p["conv_w"], p["conv_b"])
    y_ssd, y_pre, rs_ssd, hs = _ssd_fwd(proj, proj_small, xa, p["dt_bias"], p["a_log"], p["d_skip"], p["ssd_norm_w"])

    q_rot = _rope([proj], QB, ATT_WIDTH, cos, sin, 1.0, _MXU, "rope_q")
    ks_rot = _heads_major(_rope([proj], KSB, 256, cos, sin, 1.0, _MXU, "rope_ks"))
    kw_rot = _heads_major(_rope([proj], KWB, 256, cos, sin, 1.0, _MXU, "rope_kw"))
    vs = _heads_major(_col256(proj, VSB).astype(_MXU))
    vw = _heads_major(_col256(proj, VWB).astype(_MXU))
    rk, rv = _to_rows16(_col256(proj, KCB)), _to_rows16(_col256(proj, VCB))
    k_cmp, hid_k = _compress_fwd(rk, p["cmp_pe_k"], p["cmp_w1_k"], p["cmp_w2_k"])
    v_cmp, hid_v = _compress_fwd(rv, p["cmp_pe_v"], p["cmp_w1_v"], p["cmp_w2_v"])

    o_cmp, lse_cmp = _attn_fwd(proj, QB, k_cmp, v_cmp, "cmp", None, "attn_cmp_fwd")
    sel = _select(proj, QB, k_cmp, lse_cmp)
    o_sel, lse_sel = _attn_fwd(q_rot, 0, ks_rot, vs, "sel", sel, "attn_sel_fwd")
    o_win, lse_win = _attn_fwd(q_rot, 0, kw_rot, vw, "win", None, "attn_win_fwd")
    y_att = _combine_fwd(o_cmp, o_sel, o_win, proj_small)

    if late_weights is not None:
        p = {**p, **late_weights(y_att)}
    mixed = jnp.concatenate([y_ssd, y_att], axis=1)
    h1 = _mm(mixed, p["w_out"], "nn", F32, "out_proj", res=x)
    v, rs_ffn = _rms_fwd(h1, p["ffn_norm_w"], "ffn_norm")
    gt, up, act = _ffn_up(v, p["w_gate"], p["w_up"])
    h2 = _mm(act, p["w_down"], "nn", F32, "ffn_down", res=h1)
    loss, dh2, d_final_w = _final_loss(h2, p["final_norm_w"], tgt)

    g = {"final_norm_w": d_final_w}
    g["w_down"] = _mm(act, dh2, "tn", _MXU, "dw_down")
    dgt, dup = _ffn_dact(dh2, p["w_down"], gt, up)
    g["w_gate"] = _mm(v, dgt, "tn", _MXU, "dw_gate")
    g["w_up"] = _mm(v, dup, "tn", _MXU, "dw_up")
    ffn_norm_w = p["ffn_norm_w"] if ffn_grads_ready is None else p["ffn_norm_w"] + ffn_grads_ready(g)
    dv = _mm(dgt, p["w_gate"], "nt", F32, "dv_gate")
    dv = _mm(dup, p["w_up"], "nt", F32, "dv_up", res=dv)
    dh1, g["ffn_norm_w"] = _rms_bwd(dv, h1, rs_ffn, ffn_norm_w, dh2, "ffn_norm_bwd")
    g["w_out"] = _mm(mixed, dh1, "tn", _MXU, "dw_out")
    dmixed = _mm(dh1, p["w_out"], "nt", F32, "dmixed")

    dz, dxa, ddtr, g["dt_bias"], g["a_log"], g["d_skip"], g["ssd_norm_w"] = _ssd_bwd(
        dmixed, proj, proj_small, xa, y_pre, rs_ssd, hs, p["dt_bias"], p["a_log"], p["d_skip"], p["ssd_norm_w"])
    dxbc, g["conv_w"], g["conv_b"] = _conv_bwd(proj, p["conv_w"], p["conv_b"], dxa)

    do_cmp, do_sel, do_win, dgate = _combine_bwd(dmixed, o_cmp, o_sel, o_win, proj_small)
    dq_cmp, dk_cmp, dv_cmp = _attn_bwd(proj, QB, k_cmp, v_cmp, o_cmp, lse_cmp, do_cmp, "cmp", None, "attn_cmp_bwd")
    dq_sel, dks, dvs = _attn_bwd(q_rot, 0, ks_rot, vs, o_sel, lse_sel, do_sel, "sel", sel, "attn_sel_bwd")
    dq_win, dkw, dvw = _attn_bwd(q_rot, 0, kw_rot, vw, o_win, lse_win, do_win, "win", None, "attn_win_bwd")
    drk, g["cmp_w1_k"], g["cmp_w2_k"], g["cmp_pe_k"] = _compress_bwd(rk, p["cmp_pe_k"], p["cmp_w1_k"], p["cmp_w2_k"], hid_k, dk_cmp)
    drv, g["cmp_w1_v"], g["cmp_w2_v"], g["cmp_pe_v"] = _compress_bwd(rv, p["cmp_pe_v"], p["cmp_w1_v"], p["cmp_w2_v"], hid_v, dv_cmp)
    dq = _rope([dq_sel, dq_win], 0, ATT_WIDTH, cos, sin, -1.0, _MXU, "rope_dq", extra=(dq_cmp, 0))
    dks_t = _rope([_tokens_major(dks)], 0, 256, cos, sin, -1.0, _MXU, "rope_dks")
    dkw_t = _rope([_tokens_major(dkw)], 0, 256, cos, sin, -1.0, _MXU, "rope_dkw")
    dproj = jnp.concatenate(
        [dz, dxbc, dq] + [t.astype(_MXU) for t in (_from_rows16(drk), _from_rows16(drv))]
        + [dks_t, _tokens_major(dvs).astype(_MXU), dkw_t, _tokens_major(dvw).astype(_MXU)], axis=1)
    dsmall = jnp.concatenate([ddtr, dgate[:, GATE_COL0:GATE_COL0 + 3 * N_HEADS],
                              jnp.zeros((S, W_SMALL - SSD_HEADS - 3 * N_HEADS), F32)], axis=1).astype(_MXU)
    g["w_main"] = _mm(u, dproj, "tn", _MXU, "dw_in")
    g["w_small"] = _mm(u, dsmall, "tn", F32, "dw_in_small")
    du = _mm(dproj, p["w_main"], "nt", F32, "du_main")
    du = _mm(dsmall, p["w_small"], "nt", F32, "du_small", res=du)
    grad_x, g["attn_norm_w"] = _rms_bwd(du, x, rs1, p["attn_norm_w"], dh1, "attn_norm_bwd")
    return loss, grad_x, g


MESH_ID = pl.DeviceIdType.MESH


def _my_coords():
    return lax.axis_index("x"), lax.axis_index("y"), lax.axis_index("c")


def _flat_id(px, py, pc):
    return 4 * px + 2 * py + pc


def _peer(k):
    mx, my, mc = _my_coords()
    return (1 - mx if k & 4 else mx, 1 - my if k & 2 else my, 1 - mc if k & 1 else mc)


def _exchange(arrs, scatter, name):
    n = len(arrs)

    def body(*refs):
        ins, outs = refs[:n], refs[n:2 * n]
        send_sems, recv_sems, local_sems = refs[2 * n:]
        me = _flat_id(*_my_coords())
        copies = []
        for i in range(n):
            src_me = ins[i].at[me] if scatter else ins[i]
            local = pltpu.make_async_copy(src_me, outs[i].at[me], local_sems.at[i])
            local.start()
            copies.append(local)
        for k in range(1, N_DEV):
            peer = _peer(k)
            for i in range(n):
                src = ins[i].at[_flat_id(*peer)] if scatter else ins[i]
                cp = pltpu.make_async_remote_copy(src_ref=src, dst_ref=outs[i].at[me], send_sem=send_sems.at[i * 7 + k - 1],
                                                  recv_sem=recv_sems.at[i * 7 + k - 1], device_id=peer, device_id_type=MESH_ID)
                cp.start()
                copies.append(cp)
        for cp in copies:
            cp.wait()

    any_spec = pl.BlockSpec(memory_space=pl.ANY)
    out_shape = [jax.ShapeDtypeStruct(a.shape if scatter else (N_DEV,) + a.shape, a.dtype) for a in arrs]
    return pl.pallas_call(
        body, name=name, in_specs=[any_spec] * n, out_specs=[any_spec] * n, out_shape=out_shape,
        scratch_shapes=[pltpu.SemaphoreType.DMA((n * 7,)), pltpu.SemaphoreType.DMA((n * 7,)), pltpu.SemaphoreType.DMA((n,))],
        compiler_params=pltpu.CompilerParams(has_side_effects=True))(*arrs)


_HBM = pl.BlockSpec(memory_space=pltpu.HBM)
_SEM = pl.BlockSpec(memory_space=pltpu.SEMAPHORE)
_EFFECT = pltpu.SideEffectType.DATAFLOW_SIDE_EFFECTING


def _split_copies(ins, lands, send_sems, recv_sems, scatter):
    me = _flat_id(*_my_coords())
    out = []
    for k in range(1, N_DEV):
        peer = _peer(k)
        for i in range(len(ins)):
            src = ins[i].at[_flat_id(*peer)] if scatter else ins[i]
            out.append(pltpu.make_async_remote_copy(src_ref=src, dst_ref=lands[i].at[me], send_sem=send_sems.at[i * 7 + k - 1],
                                                    recv_sem=recv_sems.at[i * 7 + k - 1], device_id=peer, device_id_type=MESH_ID))
    return out


def _split_start(arrs, scatter, name):
    n = len(arrs)

    def body(*refs):
        for cp in _split_copies(refs[:n], refs[n:2 * n], refs[2 * n], refs[2 * n + 1], scatter):
            cp.start()
        refs[-1][...] = jnp.zeros_like(refs[-1])

    land_shapes = [a.shape if scatter else (N_DEV,) + a.shape for a in arrs]
    out_shape = ((pltpu.SemaphoreType.DMA((n * 7,)), pltpu.SemaphoreType.DMA((n * 7,)))
                 + tuple(pltpu.HBM(a.shape, a.dtype) for a in arrs) + tuple(pltpu.HBM(s, a.dtype) for s, a in zip(land_shapes, arrs))
                 + (jax.ShapeDtypeStruct((8, 128), F32),))
    operands = ([pltpu.with_memory_space_constraint(a, pltpu.HBM) for a in arrs]
                + [pltpu.with_memory_space_constraint(lax.empty(s, a.dtype), pltpu.HBM) for s, a in zip(land_shapes, arrs)])
    res = pl.pallas_call(
        body, name=name, out_shape=out_shape, in_specs=[_HBM] * (2 * n),
        out_specs=(_SEM, _SEM) + (_HBM,) * (2 * n) + (pl.BlockSpec(memory_space=pltpu.VMEM),),
        input_output_aliases={i: 2 + i for i in range(2 * n)},
        compiler_params=pltpu.CompilerParams(has_side_effects=_EFFECT))(*operands)
    return dict(send=res[0], recv=res[1], ins=list(res[2:2 + n]), lands=list(res[2 + n:2 + 2 * n]), token=res[-1])


def _split_wait(st, scatter, after, name):
    n = len(st["ins"])

    def body(*refs):
        for cp in _split_copies(refs[:n], refs[n:2 * n], refs[2 * n], refs[2 * n + 1], scatter):
            cp.wait_send()
            cp.wait_recv()

    arrs = st["ins"] + st["lands"]
    res = pl.pallas_call(
        body, name=name, out_shape=tuple(pltpu.HBM(a.shape, a.dtype) for a in arrs),
        in_specs=[_HBM] * (2 * n) + [_SEM, _SEM, pl.BlockSpec(memory_space=pl.ANY)], out_specs=(_HBM,) * (2 * n),
        input_output_aliases={i: i for i in range(2 * n)},
        compiler_params=pltpu.CompilerParams(has_side_effects=_EFFECT))(*arrs, st["send"], st["recv"], after)
    me = _flat_id(*_my_coords())
    out = []
    for src, land in zip(res[:n], res[n:]):
        own = lax.dynamic_index_in_dim(src, me, 0, keepdims=True) if scatter else src[None]
        out.append(lax.dynamic_update_slice_in_dim(land, own, me, 0))
    return out


def _sum_parts(parts):
    P, R, C = parts.shape
    tr = _pick(R, (136, 8))

    def body(p_ref, o_ref):
        acc = p_ref[0]
        for j in range(1, P):
            acc = acc + p_ref[j]
        o_ref[...] = acc

    return pl.pallas_call(
        body, name="sum_small_grads", grid=(R // tr,), in_specs=[pl.BlockSpec((P, tr, C), lambda i: (0, i, 0))],
        out_specs=pl.BlockSpec((tr, C), lambda i: (i, 0)), out_shape=jax.ShapeDtypeStruct((R, C), F32),
        compiler_params=_cp(("parallel",)))(parts)


def _adam_sum(parts, w, m, v, name):
    P, R, C = parts.shape
    tr = _pick(R, (256, 128, 64, 32, 8)) if C <= 1024 else _pick(R, (128, 64, 32, 8))

    def body(p_ref, w_ref, m_ref, v_ref, g_ref, d_ref, nm_ref, nv_ref):
        g = p_ref[0].astype(F32)
        for j in range(1, P):
            g = g + p_ref[j].astype(F32)
        g_ref[...] = g
        nm = ADAM_B1 * m_ref[...] + (1.0 - ADAM_B1) * g
        nv = ADAM_B2 * v_ref[...] + (1.0 - ADAM_B2) * (g * g)
        nm_ref[...] = nm
        nv_ref[...] = nv
        m_hat = nm / (1.0 - ADAM_B1 ** ADAM_STEP)
        v_hat = nv / (1.0 - ADAM_B2 ** ADAM_STEP)
        d_ref[...] = -ADAM_LR * (m_hat / (jnp.sqrt(v_hat) + ADAM_EPS) + ADAM_WD * w_ref[...])

    blk = pl.BlockSpec((tr, C), lambda i: (i, 0))
    return pl.pallas_call(
        body, name=name, grid=(R // tr,), in_specs=[pl.BlockSpec((P, tr, C), lambda i: (0, i, 0)), blk, blk, blk],
        out_specs=[blk] * 4, out_shape=[jax.ShapeDtypeStruct((R, C), F32)] * 4, compiler_params=_cp(("parallel",)))(parts, w, m, v)


def _pack(arrs):
    rows = []
    for a in arrs:
        f = a.reshape(-1).astype(F32)
        f = jnp.pad(f, (0, (-f.shape[0]) % 128))
        rows.append(f.reshape(-1, 128))
    out = jnp.concatenate(rows, axis=0)
    return jnp.pad(out, ((0, (-out.shape[0]) % 8), (0, 0)))


def _unpack(pack, shapes):
    out, r = [], 0
    for s in shapes:
        n = int(np.prod(s))
        nr = -(-n // 128)
        out.append(pack[r:r + nr].reshape(-1)[:n].reshape(s))
        r += nr
    return out


_WEIGHTS = ["attn_norm_w", "w_in", "conv_w", "conv_b", "dt_bias", "a_log", "d_skip", "ssd_norm_w", "cmp_w1_k", "cmp_w2_k",
            "cmp_w1_v", "cmp_w2_v", "cmp_pe_k", "cmp_pe_v", "w_out", "ffn_norm_w", "w_gate", "w_up", "w_down", "final_norm_w"]
_BIG = ["w_in", "w_gate", "w_up", "w_down", "w_out", "cmp_w1_k", "cmp_w1_v"]
_EARLY = ["w_in", "cmp_w1_k", "cmp_w1_v"]
_LATE = ["w_out", "w_gate", "w_up", "w_down"]
_FFN = ["w_down", "w_gate", "w_up"]
_REST = ["w_in", "w_out", "cmp_w1_k", "cmp_w1_v"]
_COL_SHARDED = ("w_in", "w_gate", "w_up")
_REPLICATED = ["attn_norm_w", "conv_b", "dt_bias", "a_log", "d_skip", "ssd_norm_w", "cmp_pe_k", "cmp_pe_v", "ffn_norm_w",
               "final_norm_w"]
_SMALL_SHARDED = ["conv_w", "cmp_w2_k", "cmp_w2_v"]
_SMALL_FULL_SHAPES = {"attn_norm_w": (1, D_MODEL), "conv_b": (1, CONV_CH), "dt_bias": (1, SSD_HEADS), "a_log": (1, SSD_HEADS),
                      "d_skip": (1, SSD_HEADS), "ssd_norm_w": (1, SSD_WIDTH), "cmp_pe_k": (1, 32 * HD), "cmp_pe_v": (1, 32 * HD),
                      "ffn_norm_w": (1, D_MODEL), "final_norm_w": (1, D_MODEL), "conv_w": (CONV_K, CONV_CH),
                      "cmp_w2_k": (CMP_HID, HD), "cmp_w2_v": (CMP_HID, HD)}


def _cols_to_slabs(g):
    R = g.shape[0]
    return g.reshape(R, N_DEV, -1).transpose(1, 0, 2)


def _slabs_to_cols(s):
    return s.transpose(1, 0, 2).reshape(s.shape[1], -1)


def kernel(x, attn_norm_w, w_in, conv_w, conv_b, dt_bias, a_log, d_skip, ssd_norm_w, cmp_w1_k, cmp_w2_k, cmp_w1_v, cmp_w2_v, cmp_pe_k, cmp_pe_v, w_out, ffn_norm_w, w_gate, w_up, w_down, final_norm_w, loss_target, m_attn_norm_w, m_w_in, m_conv_w, m_conv_b, m_dt_bias, m_a_log, m_d_skip, m_ssd_norm_w, m_cmp_w1_k, m_cmp_w2_k, m_cmp_w1_v, m_cmp_w2_v, m_cmp_pe_k, m_cmp_pe_v, m_w_out, m_ffn_norm_w, m_w_gate, m_w_up, m_w_down, m_final_norm_w, v_attn_norm_w, v_w_in, v_conv_w, v_conv_b, v_dt_bias, v_a_log, v_d_skip, v_ssd_norm_w, v_cmp_w1_k, v_cmp_w2_k, v_cmp_w1_v, v_cmp_w2_v, v_cmp_pe_k, v_cmp_pe_v, v_w_out, v_ffn_norm_w, v_w_gate, v_w_up, v_w_down, v_final_norm_w):
    a = dict(locals())
    me = _flat_id(*_my_coords())

    small_in = _pack([cmp_w2_k[0], cmp_w2_v[0], conv_w[0]])
    shard = {n: a[n][0].astype(_MXU) for n in _BIG}
    st_early = _split_start([shard[n] for n in _EARLY] + [small_in], False, "gather_early_start")
    zero = st_early["token"][0, 0].astype(_MXU)
    st_late = _split_start([shard[_LATE[0]] + zero] + [shard[n] for n in _LATE[1:]], False, "gather_late_start")
    got = _split_wait(st_early, False, st_late["token"], "gather_early_wait")

    def assemble(n, t):
        return _slabs_to_cols(t) if n in _COL_SHARDED else t.reshape(-1, t.shape[-1])

    full = {n: assemble(n, t) for n, t in zip(_EARLY, got[:-1])}
    w2k, w2v, cw = [], [], []
    for d in range(N_DEV):
        parts = _unpack(got[-1][d], [cmp_w2_k.shape[1:], cmp_w2_v.shape[1:], conv_w.shape[1:]])
        w2k.append(parts[0]); w2v.append(parts[1]); cw.append(parts[2])
    w_main, w_small = _split_w_in(full["w_in"])
    p = dict(attn_norm_w=attn_norm_w, w_main=w_main, w_small=w_small, conv_w=jnp.concatenate(cw, axis=1), conv_b=conv_b,
             dt_bias=dt_bias, a_log=a_log, d_skip=d_skip, ssd_norm_w=ssd_norm_w, cmp_w1_k=full["cmp_w1_k"],
             cmp_w2_k=jnp.concatenate(w2k, axis=0).astype(_MXU), cmp_w1_v=full["cmp_w1_v"],
             cmp_w2_v=jnp.concatenate(w2v, axis=0).astype(_MXU), cmp_pe_k=cmp_pe_k.reshape(1, -1), cmp_pe_v=cmp_pe_v.reshape(1, -1),
             ffn_norm_w=ffn_norm_w, final_norm_w=final_norm_w.reshape(1, -1))

    def late_weights(after):
        got_late = _split_wait(st_late, False, after, "gather_late_wait")
        return {n: assemble(n, t) for n, t in zip(_LATE, got_late)}

    def slabs_of(g, names):
        return [(_cols_to_slabs(g[n]) if n in _COL_SHARDED else g[n].reshape(N_DEV, -1, g[n].shape[-1])).astype(_MXU) for n in names]

    pending = {}

    def ffn_grads_ready(g):
        pending["ffn"] = _split_start(slabs_of(g, _FFN), True, "scatter_ffn_grads_start")
        return pending["ffn"]["token"][0:1, 0:1]

    loss_part, grad_x, g = _local_step(x[0], loss_target[0], p, late_weights, ffn_grads_ready)
    loss = lax.psum(loss_part[0, 0], ("x", "y", "c"))
    g["w_in"] = _merge_w_in(g.pop("w_main"), g.pop("w_small"))

    st_rest = _split_start(slabs_of(g, _REST), True, "scatter_rest_grads_start")
    recv_ffn = _split_wait(pending["ffn"], True, st_rest["token"], "scatter_ffn_grads_wait")
    recv_rest = _split_wait(st_rest, True, recv_ffn[0], "scatter_rest_grads_wait")
    out = {}
    for n, parts in zip(_FFN + _REST, recv_ffn + recv_rest):
        out[n] = _adam_sum(parts, a[n][0], a["m_" + n][0], a["v_" + n][0], "adam_" + n)

    small_names = _REPLICATED + _SMALL_SHARDED
    g_small = _pack([g[n] for n in small_names])
    g_sum = _sum_parts(_exchange([g_small], False, "gather_small_grads")[0])
    gs = dict(zip(small_names, _unpack(g_sum, [_SMALL_FULL_SHAPES[n] for n in small_names])))
    gs["conv_w"] = lax.dynamic_slice_in_dim(gs["conv_w"], me * conv_w.shape[2], conv_w.shape[2], axis=1)
    gs["cmp_w2_k"] = lax.dynamic_slice_in_dim(gs["cmp_w2_k"], me * cmp_w2_k.shape[1], cmp_w2_k.shape[1], axis=0)
    gs["cmp_w2_v"] = lax.dynamic_slice_in_dim(gs["cmp_w2_v"], me * cmp_w2_v.shape[1], cmp_w2_v.shape[1], axis=0)
    packs = [_pack([t[n] for n in small_names]) for t in
             (gs, a, {n: a["m_" + n] for n in small_names}, {n: a["v_" + n] for n in small_names})]
    res_small = _adam_sum(packs[0][None], packs[1], packs[2], packs[3], "adam_small")
    shapes = [a[n].shape for n in small_names]
    unpacked = [dict(zip(small_names, _unpack(r, shapes))) for r in res_small]
    for n in small_names:
        out[n] = tuple(u[n] for u in unpacked)

    outs = [loss, grad_x[None]]
    for j in range(4):
        for n in _WEIGHTS:
            outs.append(out[n][j].reshape(a[n].shape))
    return tuple(outs)
```

```python
import functools
import math

import numpy as np
import jax
import jax.numpy as jnp
from jax import lax
from jax.experimental import pallas as pl
from jax.experimental.pallas import tpu as pltpu

F32 = jnp.float32
_MXU = jnp.bfloat16
_HI = lax.Precision.HIGHEST

N_DEV = 8
D_MODEL = 2048
SSD_WIDTH = 1024
ATT_WIDTH = 1024
SSD_HEADS = 16
SSD_P = 64
SSD_N = 128
SSD_L = 128
SSD_G = 2
CONV_CH = 1536
CONV_K = 4
HD = 64
N_HEADS = 16
N_KV = 4
GRP = 4
CMP_HID = 256
SEL_BLOCK = 64
N_SELECT = 16
WINDOW = 512
ROPE_DIM = 16
ROPE_THETA = 500000.0
D_FF = 5632
EPS = 1e-6
NEG = -1e30
FORCE = 1e4
SCALE = HD ** -0.5
D_IN = 5184
W_MAIN = 5120
W_SMALL = 128
VMEM_LIMIT = 52 * 1024 * 1024

ADAM_LR, ADAM_B1, ADAM_B2, ADAM_EPS, ADAM_WD, ADAM_STEP = 0.001, 0.9, 0.999, 1e-08, 0.01, 10


def _pick(n, cands):
    for c in cands:
        if n % c == 0:
            return c
    return n


def _cp(sem=None):
    return pltpu.CompilerParams(dimension_semantics=sem, vmem_limit_bytes=VMEM_LIMIT)


def _sigmoid(x):
    return 1.0 / (1.0 + jnp.exp(-x))


def _dot(a, b, dims, hi=False):
    dn = {"nn": (((1,), (0,)), ((), ())), "nt": (((1,), (1,)), ((), ())), "tn": (((0,), (0,)), ((), ()))}[dims]
    if hi:
        return lax.dot_general(a.astype(F32), b.astype(F32), dn, precision=_HI, preferred_element_type=F32)
    return lax.dot_general(a.astype(_MXU), b.astype(_MXU), dn, preferred_element_type=F32)


def _mm(a, b, mode, out_dtype, name, res=None):
    if mode == "nn":
        (M, K), N = a.shape, b.shape[1]
    elif mode == "nt":
        (M, K), N = a.shape, b.shape[0]
    else:
        (K, M), N = a.shape, b.shape[1]
    tm, tn, tk = _pick(M, (1024, 512, 256, 128)), _pick(N, (1024, 512, 256, 128)), _pick(K, (512, 256, 128))
    nk = K // tk
    a_spec = pl.BlockSpec((tk, tm), lambda i, j, k: (k, i)) if mode == "tn" else pl.BlockSpec((tm, tk), lambda i, j, k: (i, k))
    b_spec = pl.BlockSpec((tn, tk), lambda i, j, k: (j, k)) if mode == "nt" else pl.BlockSpec((tk, tn), lambda i, j, k: (k, j))
    o_spec = pl.BlockSpec((tm, tn), lambda i, j, k: (i, j))

    def body(*refs):
        if res is None:
            a_ref, b_ref, o_ref, acc = refs
        else:
            a_ref, b_ref, r_ref, o_ref, acc = refs
        k = pl.program_id(2)

        @pl.when(k == 0)
        def _():
            acc[...] = jnp.zeros_like(acc)

        acc[...] += _dot(a_ref[...], b_ref[...], mode)

        @pl.when(k == nk - 1)
        def _():
            r = acc[...]
            if res is not None:
                r = r + r_ref[...].astype(F32)
            o_ref[...] = r.astype(out_dtype)

    ins, specs = [a, b], [a_spec, b_spec]
    if res is not None:
        ins.append(res)
        specs.append(o_spec)
    return pl.pallas_call(
        body, name=name, grid=(M // tm, N // tn, nk), in_specs=specs, out_specs=o_spec,
        out_shape=jax.ShapeDtypeStruct((M, N), out_dtype), scratch_shapes=[pltpu.VMEM((tm, tn), F32)],
        compiler_params=_cp(("parallel", "parallel", "arbitrary")))(*ins)


def _ffn_up(v, w_gate, w_up):
    S, D = v.shape
    F = w_gate.shape[1]
    tm, tn, tk = _pick(S, (1024, 512, 256, 128)), _pick(F, (512, 256, 128)), _pick(D, (512, 256, 128))
    nk = D // tk

    def body(v_ref, wg_ref, wu_ref, gt_ref, up_ref, act_ref, accg, accu):
        k = pl.program_id(2)

        @pl.when(k == 0)
        def _():
            accg[...] = jnp.zeros_like(accg)
            accu[...] = jnp.zeros_like(accu)

        vv = v_ref[...]
        accg[...] += _dot(vv, wg_ref[...], "nn")
        accu[...] += _dot(vv, wu_ref[...], "nn")

        @pl.when(k == nk - 1)
        def _():
            g, u = accg[...], accu[...]
            gt_ref[...] = g
            up_ref[...] = u
            act_ref[...] = (g * _sigmoid(g) * u).astype(act_ref.dtype)

    o_spec = pl.BlockSpec((tm, tn), lambda i, j, k: (i, j))
    w_spec = pl.BlockSpec((tk, tn), lambda i, j, k: (k, j))
    return pl.pallas_call(
        body, name="ffn_up", grid=(S // tm, F // tn, nk),
        in_specs=[pl.BlockSpec((tm, tk), lambda i, j, k: (i, k)), w_spec, w_spec], out_specs=[o_spec, o_spec, o_spec],
        out_shape=[jax.ShapeDtypeStruct((S, F), F32), jax.ShapeDtypeStruct((S, F), F32), jax.ShapeDtypeStruct((S, F), _MXU)],
        scratch_shapes=[pltpu.VMEM((tm, tn), F32), pltpu.VMEM((tm, tn), F32)],
        compiler_params=_cp(("parallel", "parallel", "arbitrary")))(v, w_gate, w_up)


def _ffn_dact(dh2, w_down, gt, up):
    S, D = dh2.shape
    F = w_down.shape[0]
    tm, tn, tk = _pick(S, (1024, 512, 256, 128)), _pick(F, (512, 256, 128)), _pick(D, (512, 256, 128))
    nk = D // tk

    def body(d_ref, w_ref, gt_ref, up_ref, dg_ref, du_ref, acc):
        k = pl.program_id(2)

        @pl.when(k == 0)
        def _():
            acc[...] = jnp.zeros_like(acc)

        acc[...] += _dot(d_ref[...], w_ref[...], "nt")

        @pl.when(k == nk - 1)
        def _():
            da, g, u = acc[...], gt_ref[...], up_ref[...]
            s = _sigmoid(g)
            dg_ref[...] = (da * u * (s * (1.0 + g * (1.0 - s)))).astype(dg_ref.dtype)
            du_ref[...] = (da * (g * s)).astype(du_ref.dtype)

    o_spec = pl.BlockSpec((tm, tn), lambda i, j, k: (i, j))
    return pl.pallas_call(
        body, name="ffn_dact", grid=(S // tm, F // tn, nk),
        in_specs=[pl.BlockSpec((tm, tk), lambda i, j, k: (i, k)), pl.BlockSpec((tn, tk), lambda i, j, k: (j, k)), o_spec, o_spec],
        out_specs=[o_spec, o_spec],
        out_shape=[jax.ShapeDtypeStruct((S, F), _MXU), jax.ShapeDtypeStruct((S, F), _MXU)],
        scratch_shapes=[pltpu.VMEM((tm, tn), F32)],
        compiler_params=_cp(("parallel", "parallel", "arbitrary")))(dh2, w_down, gt, up)


def _rms_fwd(x, w, name):
    S, D = x.shape
    tr = _pick(S, (256, 128))

    def body(x_ref, w_ref, xn_ref, rs_ref):
        xv = x_ref[...]
        rs = lax.rsqrt(jnp.mean(xv * xv, axis=-1, keepdims=True) + EPS)
        xn_ref[...] = ((xv * rs) * w_ref[...]).astype(xn_ref.dtype)
        rs_ref[...] = rs

    return pl.pallas_call(
        body, name=name, grid=(S // tr,),
        in_specs=[pl.BlockSpec((tr, D), lambda i: (i, 0)), pl.BlockSpec((1, D), lambda i: (0, 0))],
        out_specs=[pl.BlockSpec((tr, D), lambda i: (i, 0)), pl.BlockSpec((tr, 1), lambda i: (i, 0))],
        out_shape=[jax.ShapeDtypeStruct((S, D), _MXU), jax.ShapeDtypeStruct((S, 1), F32)],
        compiler_params=_cp(("parallel",)))(x, w)


def _rms_bwd(dyn, x, rs, w, res, name):
    S, D = x.shape
    tr = _pick(S, (256, 128))

    def body(dy_ref, x_ref, rs_ref, w_ref, res_ref, dx_ref, dw_ref):
        @pl.when(pl.program_id(0) == 0)
        def _():
            dw_ref[...] = jnp.zeros_like(dw_ref)

        dy, r = dy_ref[...].astype(F32), rs_ref[...]
        xhat = x_ref[...] * r
        dw_ref[...] += jnp.sum(dy * xhat, axis=0, keepdims=True)
        dxhat = dy * w_ref[...]
        dx_ref[...] = res_ref[...] + r * (dxhat - xhat * jnp.mean(dxhat * xhat, axis=-1, keepdims=True))

    row = pl.BlockSpec((tr, D), lambda i: (i, 0))
    vec = pl.BlockSpec((1, D), lambda i: (0, 0))
    return pl.pallas_call(
        body, name=name, grid=(S // tr,),
        in_specs=[row, row, pl.BlockSpec((tr, 1), lambda i: (i, 0)), vec, row], out_specs=[row, vec],
        out_shape=[jax.ShapeDtypeStruct((S, D), F32), jax.ShapeDtypeStruct((1, D), F32)],
        compiler_params=_cp(("arbitrary",)))(dyn, x, rs, w, res)


def _final_loss(h2, w, tgt):
    S, D = h2.shape
    tr = _pick(S, (256, 128))

    def body(h_ref, w_ref, t_ref, loss_ref, dh_ref, dw_ref):
        @pl.when(pl.program_id(0) == 0)
        def _():
            dw_ref[...] = jnp.zeros_like(dw_ref)
            loss_ref[...] = jnp.zeros_like(loss_ref)

        hv, wv = h_ref[...], w_ref[...]
        rs = lax.rsqrt(jnp.mean(hv * hv, axis=-1, keepdims=True) + EPS)
        xhat = hv * rs
        err = xhat * wv - t_ref[...]
        row = jnp.mean(err * err, axis=-1, keepdims=True)
        loss_ref[...] += 0.5 * jnp.sum(row, axis=0, keepdims=True)
        dy = err * (1.0 / D)
        dw_ref[...] += jnp.sum(dy * xhat, axis=0, keepdims=True)
        dxhat = dy * wv
        dh_ref[...] = rs * (dxhat - xhat * jnp.mean(dxhat * xhat, axis=-1, keepdims=True))

    row = pl.BlockSpec((tr, D), lambda i: (i, 0))
    vec = pl.BlockSpec((1, D), lambda i: (0, 0))
    return pl.pallas_call(
        body, name="final_loss", grid=(S // tr,), in_specs=[row, vec, row],
        out_specs=[pl.BlockSpec((1, 1), lambda i: (0, 0)), row, vec],
        out_shape=[jax.ShapeDtypeStruct((1, 1), F32), jax.ShapeDtypeStruct((S, D), F32), jax.ShapeDtypeStruct((1, D), F32)],
        compiler_params=_cp(("arbitrary",)))(h2, w, tgt)


def _shift_rows(x, k, rows):
    if k == 0:
        return x
    S = x.shape[0]
    r = pltpu.roll(x, k % S, axis=0)
    ok = (rows >= k) if k > 0 else (rows < S + k)
    return jnp.where(ok, r, 0.0)


XBC_COL0 = SSD_WIDTH // 128


def _conv_fwd(proj, conv_w, conv_b):
    S = proj.shape[0]
    nct = CONV_CH // 128

    def body(x_ref, w_ref, b_ref, o_ref):
        x = x_ref[...]
        rows = lax.broadcasted_iota(jnp.int32, x.shape, 0)
        c = b_ref[...] + w_ref[3:4, :] * x
        for k in range(1, CONV_K):
            c = c + w_ref[3 - k:4 - k, :] * _shift_rows(x, k, rows)
        o_ref[...] = c * _sigmoid(c)

    return pl.pallas_call(
        body, name="conv_fwd", grid=(nct,),
        in_specs=[pl.BlockSpec((S, 128), lambda j: (0, XBC_COL0 + j)), pl.BlockSpec((CONV_K, 128), lambda j: (0, j)),
                  pl.BlockSpec((1, 128), lambda j: (0, j))],
        out_specs=pl.BlockSpec((S, 128), lambda j: (0, j)),
        out_shape=jax.ShapeDtypeStruct((S, CONV_CH), F32), compiler_params=_cp(("parallel",)))(proj, conv_w, conv_b)


def _conv_bwd(proj, conv_w, conv_b, dxa):
    S = proj.shape[0]
    nct = CONV_CH // 128

    def body(x_ref, w_ref, b_ref, d_ref, dx_ref, dw_ref, db_ref):
        x = x_ref[...]
        rows = lax.broadcasted_iota(jnp.int32, x.shape, 0)
        xs = [_shift_rows(x, k, rows) for k in range(CONV_K)]
        c = b_ref[...] + w_ref[3:4, :] * x
        for k in range(1, CONV_K):
            c = c + w_ref[3 - k:4 - k, :] * xs[k]
        s = _sigmoid(c)
        dc = d_ref[...] * (s * (1.0 + c * (1.0 - s)))
        dx = w_ref[3:4, :] * dc
        for k in range(1, CONV_K):
            dx = dx + w_ref[3 - k:4 - k, :] * _shift_rows(dc, -k, rows)
        dx_ref[...] = dx.astype(dx_ref.dtype)
        for k in range(CONV_K):
            dw_ref[3 - k:4 - k, :] = jnp.sum(dc * xs[k], axis=0, keepdims=True)
        db_ref[...] = jnp.sum(dc, axis=0, keepdims=True)

    col = pl.BlockSpec((S, 128), lambda j: (0, j))
    return pl.pallas_call(
        body, name="conv_bwd", grid=(nct,),
        in_specs=[pl.BlockSpec((S, 128), lambda j: (0, XBC_COL0 + j)), pl.BlockSpec((CONV_K, 128), lambda j: (0, j)),
                  pl.BlockSpec((1, 128), lambda j: (0, j)), col],
        out_specs=[col, pl.BlockSpec((CONV_K, 128), lambda j: (0, j)), pl.BlockSpec((1, 128), lambda j: (0, j))],
        out_shape=[jax.ShapeDtypeStruct((S, CONV_CH), _MXU), jax.ShapeDtypeStruct((CONV_K, CONV_CH), F32),
                   jax.ShapeDtypeStruct((1, CONV_CH), F32)],
        compiler_params=_cp(("parallel",)))(proj, conv_w, conv_b, dxa)


def _ssd_consts():
    L = SSD_L
    r = lax.broadcasted_iota(jnp.int32, (L, L), 0)
    c = lax.broadcasted_iota(jnp.int32, (L, L), 1)
    causal = r >= c
    upper = (r <= c).astype(F32)
    hr = lax.broadcasted_iota(jnp.int32, (SSD_HEADS, SSD_WIDTH), 0)
    hc = lax.broadcasted_iota(jnp.int32, (SSD_HEADS, SSD_WIDTH), 1)
    expand = (lax.shift_right_logical(hc, 6) == hr).astype(F32)
    return causal, causal.astype(F32), upper, expand


def _softplus(x):
    return jnp.maximum(x, 0.0) + jnp.log(1.0 + jnp.exp(-jnp.abs(x)))


def _ssd_scalars(dtr, dt_bias, a_log, tri, upper, expand):
    dt = _softplus(dtr + dt_bias)
    A = -jnp.exp(a_log)
    adt = dt * A
    acum = _dot(tri, adt, "nn", hi=True)
    acum_t = _dot(adt, upper, "tn", hi=True)
    alast = acum[SSD_L - 1:SSD_L, :]
    e = jnp.exp(acum)
    wdec = jnp.exp(alast - acum)
    gam = jnp.exp(alast)
    ex = lambda t: _dot(t, expand, "nn", hi=True)
    gam8 = jnp.broadcast_to(gam, (8, SSD_HEADS))
    return dt, A, acum, acum_t, e, wdec, gam, ex(dt), ex(e), ex(wdec), ex(gam8)[0:1, :]


def _ssd_fwd(proj, proj_small, xa, dt_bias, a_log, d_skip, norm_w):
    S = proj.shape[0]
    L, N, W = SSD_L, SSD_N, SSD_WIDTH
    nc = S // L

    def body(z_ref, xa_ref, dtr_ref, dtb_ref, al_ref, dsk_ref, nw_ref, yo_ref, y_ref, rs_ref, hs_ref, h_scr, y_scr):
        @pl.when(pl.program_id(0) == 0)
        def _():
            h_scr[...] = jnp.zeros_like(h_scr)

        causal, tri, upper, expand = _ssd_consts()
        dt, A, acum, acum_t, e, wdec, gam, dtE, eE, wE, gamE = _ssd_scalars(dtr_ref[:, 0:SSD_HEADS], dtb_ref[...], al_ref[...], tri, upper, expand)
        xs = xa_ref[:, 0:W]
        X = xs * dtE
        XW = X * wE
        hs_ref[0] = h_scr[...]
        for g in range(SSD_G):
            gs = slice(g * 512, (g + 1) * 512)
            Bg = xa_ref[:, W + g * N:W + (g + 1) * N]
            Cg = xa_ref[:, W + SSD_G * N + g * N:W + SSD_G * N + (g + 1) * N]
            Hg = h_scr[:, gs]
            CB = _dot(Cg, Bg, "nt")
            yoff = _dot(Cg, Hg, "nn") * eE[:, gs]
            st = _dot(Bg, XW[:, gs], "tn")
            for j in range(8):
                h = g * 8 + j
                hsl = slice(h * SSD_P, (h + 1) * SSD_P)
                lam = jnp.exp(jnp.where(causal, acum[:, h:h + 1] - acum_t[h:h + 1, :], -jnp.inf))
                y_scr[:, hsl] = _dot(CB * lam, X[:, hsl], "nn") + yoff[:, j * SSD_P:(j + 1) * SSD_P]
            h_scr[:, gs] = gamE[:, gs] * Hg + st
        dskE = _dot(jnp.broadcast_to(dsk_ref[...], (8, SSD_HEADS)), expand, "nn", hi=True)[0:1, :]
        y = y_scr[...] + dskE * xs
        y_ref[...] = y
        zv = z_ref[...]
        yg = y * (zv * _sigmoid(zv))
        rs = lax.rsqrt(jnp.mean(yg * yg, axis=-1, keepdims=True) + EPS)
        rs_ref[...] = rs
        yo_ref[...] = ((yg * rs) * nw_ref[...]).astype(yo_ref.dtype)

    p16 = pl.BlockSpec((1, SSD_HEADS), lambda c: (0, 0))
    return pl.pallas_call(
        body, name="ssd_fwd", grid=(nc,),
        in_specs=[pl.BlockSpec((L, W), lambda c: (c, 0)), pl.BlockSpec((L, CONV_CH), lambda c: (c, 0)),
                  pl.BlockSpec((L, W_SMALL), lambda c: (c, 0)), p16, p16, p16, pl.BlockSpec((1, W), lambda c: (0, 0))],
        out_specs=[pl.BlockSpec((L, W), lambda c: (c, 0)), pl.BlockSpec((L, W), lambda c: (c, 0)),
                   pl.BlockSpec((L, 1), lambda c: (c, 0)), pl.BlockSpec((1, N, W), lambda c: (c, 0, 0))],
        out_shape=[jax.ShapeDtypeStruct((S, W), _MXU), jax.ShapeDtypeStruct((S, W), F32), jax.ShapeDtypeStruct((S, 1), F32),
                   jax.ShapeDtypeStruct((nc, N, W), F32)],
        scratch_shapes=[pltpu.VMEM((N, W), F32), pltpu.VMEM((L, W), F32)],
        compiler_params=_cp(("arbitrary",)))(proj, xa, proj_small, dt_bias, a_log, d_skip, norm_w)


def _ssd_bwd(dmixed, proj, proj_small, xa, y, rs2, hs, dt_bias, a_log, d_skip, norm_w):
    S = proj.shape[0]
    L, N, W, H = SSD_L, SSD_N, SSD_WIDTH, SSD_HEADS
    nc = S // L

    def body(dyo_ref, z_ref, xa_ref, dtr_ref, y_ref, rs_ref, hs_ref, dtb_ref, al_ref, dsk_ref, nw_ref,
             dz_ref, dxa_ref, ddtr_ref, ddtb_ref, dal_ref, ddsk_ref, dnw_ref, dh_scr, dx_scr):
        @pl.when(pl.program_id(0) == 0)
        def _():
            dh_scr[...] = jnp.zeros_like(dh_scr)
            ddtb_ref[...] = jnp.zeros_like(ddtb_ref)
            dal_ref[...] = jnp.zeros_like(dal_ref)
            ddsk_ref[...] = jnp.zeros_like(ddsk_ref)
            dnw_ref[...] = jnp.zeros_like(dnw_ref)

        causal, tri, upper, expand = _ssd_consts()
        heads = lambda t: _dot(t, expand, "nt", hi=True)
        onehot = lambda h: (lax.broadcasted_iota(jnp.int32, (1, H), 1) == h).astype(F32)

        zv, yv, rs = z_ref[...], y_ref[...], rs_ref[...]
        sz = _sigmoid(zv)
        zs = zv * sz
        xhat = (yv * zs) * rs
        dyo = dyo_ref[...].astype(F32)
        dnw_ref[...] += jnp.sum(dyo * xhat, axis=0, keepdims=True)
        dxhat = dyo * nw_ref[...]
        dyg = rs * (dxhat - xhat * jnp.mean(dxhat * xhat, axis=-1, keepdims=True))
        dz_ref[...] = (dyg * yv * (sz * (1.0 + zv * (1.0 - sz)))).astype(dz_ref.dtype)
        dy = dyg * zs

        dtr = dtr_ref[:, 0:H]
        dt, A, acum, acum_t, e, wdec, gam, dtE, eE, wE, gamE = _ssd_scalars(dtr, dtb_ref[...], al_ref[...], tri, upper, expand)
        xs = xa_ref[:, 0:W]
        X = xs * dtE
        XW = X * wE
        dskE = _dot(jnp.broadcast_to(dsk_ref[...], (8, H)), expand, "nn", hi=True)[0:1, :]
        ddsk_ref[...] += heads(jnp.broadcast_to(jnp.sum(dy * xs, axis=0, keepdims=True), (8, W)))[0:1, :]

        dYe = dy * eE
        dacum = jnp.zeros((L, H), F32)
        de_full = []
        dw_full = []
        dgam_full = []
        for g in range(SSD_G):
            gs = slice(g * 512, (g + 1) * 512)
            Bg = xa_ref[:, W + g * N:W + (g + 1) * N]
            Cg = xa_ref[:, W + SSD_G * N + g * N:W + SSD_G * N + (g + 1) * N]
            Hg = hs_ref[0, :, gs]
            dHn = dh_scr[:, gs]
            CH = _dot(Cg, Hg, "nn")
            de_full.append(dy[:, gs] * CH)
            dC = _dot(dYe[:, gs], Hg, "nt")
            dHs = gamE[:, gs] * dHn + _dot(Cg, dYe[:, gs], "tn")
            dgam_full.append(jnp.sum(dHn * Hg, axis=0, keepdims=True))
            BdS = _dot(Bg, dHn, "nn")
            dB = _dot(XW[:, gs], dHn, "nt")
            dx_scr[:, gs] = BdS * wE[:, gs]
            dw_full.append(BdS * X[:, gs])
            CB = _dot(Cg, Bg, "nt")
            dCB = jnp.zeros((L, L), F32)
            for j in range(8):
                h = g * 8 + j
                hsl = slice(h * SSD_P, (h + 1) * SSD_P)
                lam = jnp.exp(jnp.where(causal, acum[:, h:h + 1] - acum_t[h:h + 1, :], -jnp.inf))
                M = CB * lam
                dM = _dot(dy[:, hsl], X[:, hsl], "nt")
                dx_scr[:, hsl] += _dot(M, dy[:, hsl], "tn")
                dCB = dCB + dM * lam
                Q = dM * M
                rowsum = jnp.sum(Q, axis=1, keepdims=True)
                colsum = _dot(Q, jnp.ones((L, 8), F32), "tn", hi=True)[:, 0:1]
                dacum = dacum + (rowsum - colsum) * onehot(h)
            dC = dC + _dot(dCB, Bg, "nn")
            dB = dB + _dot(dCB, Cg, "tn")
            dxa_ref[:, W + g * N:W + (g + 1) * N] = dB
            dxa_ref[:, W + SSD_G * N + g * N:W + SSD_G * N + (g + 1) * N] = dC
            dh_scr[:, gs] = dHs

        de16 = heads(jnp.concatenate(de_full, axis=1))
        dw16 = heads(jnp.concatenate(dw_full, axis=1))
        dgam16 = heads(jnp.broadcast_to(jnp.concatenate(dgam_full, axis=1), (8, W)))[0:1, :]
        dacum = dacum + de16 * e - dw16 * wdec
        dlast = jnp.sum(dw16 * wdec, axis=0, keepdims=True) + dgam16 * gam
        lastrow = (lax.broadcasted_iota(jnp.int32, (L, 1), 0) == L - 1).astype(F32)
        dacum = dacum + lastrow * dlast
        da = _dot(tri, dacum, "tn", hi=True)
        dX = dx_scr[...]
        ddt = da * A + heads(dX * xs)
        dA = jnp.sum(da * dt, axis=0, keepdims=True)
        dal_ref[...] += dA * A
        ddtr = ddt * _sigmoid(dtr + dtb_ref[...])
        ddtb_ref[...] += jnp.sum(ddtr, axis=0, keepdims=True)
        ddtr_ref[...] = ddtr
        dxa_ref[:, 0:W] = dX * dtE + dy * dskE

    p16 = pl.BlockSpec((1, H), lambda c: (0, 0))
    rev = lambda c: (nc - 1 - c, 0)
    return pl.pallas_call(
        body, name="ssd_bwd", grid=(nc,),
        in_specs=[pl.BlockSpec((L, W), rev), pl.BlockSpec((L, W), rev), pl.BlockSpec((L, CONV_CH), rev),
                  pl.BlockSpec((L, W_SMALL), rev), pl.BlockSpec((L, W), rev), pl.BlockSpec((L, 1), rev),
                  pl.BlockSpec((1, N, W), lambda c: (nc - 1 - c, 0, 0)), p16, p16, p16, pl.BlockSpec((1, W), lambda c: (0, 0))],
        out_specs=[pl.BlockSpec((L, W), rev), pl.BlockSpec((L, CONV_CH), rev), pl.BlockSpec((L, H), rev),
                   p16, p16, p16, pl.BlockSpec((1, W), lambda c: (0, 0))],
        out_shape=[jax.ShapeDtypeStruct((S, W), _MXU), jax.ShapeDtypeStruct((S, CONV_CH), F32), jax.ShapeDtypeStruct((S, H), F32),
                   jax.ShapeDtypeStruct((1, H), F32), jax.ShapeDtypeStruct((1, H), F32), jax.ShapeDtypeStruct((1, H), F32),
                   jax.ShapeDtypeStruct((1, W), F32)],
        scratch_shapes=[pltpu.VMEM((N, W), F32), pltpu.VMEM((L, W), F32)],
        compiler_params=_cp(("arbitrary",)))(dmixed, proj, xa, proj_small, y, rs2, hs, dt_bias, a_log, d_skip, norm_w)


def _rope_tables(S):
    inv = 1.0 / (ROPE_THETA ** (jnp.arange(0, ROPE_DIM, 2, dtype=F32) / ROPE_DIM))
    ang = jnp.arange(S, dtype=F32)[:, None] * inv[None, :]
    cos, sin = jnp.cos(ang), jnp.sin(ang)
    half = ROPE_DIM // 2
    c64 = jnp.concatenate([cos, cos, jnp.ones((S, HD - ROPE_DIM), F32)], axis=1)
    s64 = jnp.concatenate([sin, sin, jnp.zeros((S, HD - ROPE_DIM), F32)], axis=1)
    del half
    return jnp.concatenate([c64, c64], axis=1), jnp.concatenate([s64, s64], axis=1)


def _rope(xs, blk0, width, cos, sin, sign, out_dtype, name, extra=None):
    S = xs[0].shape[0]
    tr = _pick(S, (512, 256, 128))
    nx = len(xs)

    def body(*refs):
        x_refs, c_ref, s_ref = refs[:nx], refs[nx], refs[nx + 1]
        e_ref = refs[nx + 2] if extra is not None else None
        o_ref = refs[-1]
        cv, sv = c_ref[...], s_ref[...] * sign
        lane = lax.broadcasted_iota(jnp.int32, (tr, 128), 1)
        first = (lane & (HD - 1)) < (ROPE_DIM // 2)
        for j in range(2):
            cs = slice(j * 128, (j + 1) * 128)
            xv = x_refs[0][:, cs].astype(F32)
            for r in x_refs[1:]:
                xv = xv + r[:, cs].astype(F32)
            rot = jnp.where(first, -pltpu.roll(xv, 128 - ROPE_DIM // 2, axis=1), pltpu.roll(xv, ROPE_DIM // 2, axis=1))
            out = xv * cv + rot * sv
            if extra is not None:
                out = out + e_ref[:, cs].astype(F32)
            o_ref[:, cs] = out.astype(out_dtype)

    t128 = pl.BlockSpec((tr, 128), lambda i, j: (i, 0))
    oblk = pl.BlockSpec((tr, 256), lambda i, j: (i, j))
    specs = [pl.BlockSpec((tr, 256), lambda i, j: (i, blk0 + j))] * nx + [t128, t128]
    ins = list(xs) + [cos, sin]
    if extra is not None:
        ins.append(extra[0])
        eb = extra[1]
        specs.append(pl.BlockSpec((tr, 256), lambda i, j: (i, eb + j)))
    return pl.pallas_call(
        body, name=name, grid=(S // tr, width // 256), in_specs=specs, out_specs=oblk,
        out_shape=jax.ShapeDtypeStruct((S, width), out_dtype), compiler_params=_cp(("parallel", "parallel")))(*ins)


def _compress_fwd(R, pe, w1, w2):
    NC = R.shape[1]
    half = 16 * HD

    def body(r_ref, pe_ref, w1_ref, w2_ref, o_ref, hid_ref):
        r = r_ref[0]
        a = _dot(r + pe_ref[:, 0:half], w1_ref[0:half, :], "nn")
        b = _dot(r + pe_ref[:, half:2 * half], w1_ref[half:2 * half, :], "nn")
        hid = a + pltpu.roll(b, NC - 1, axis=0)
        hid_ref[0] = hid
        out = _dot(hid * _sigmoid(hid), w2_ref[...], "nn")
        rows = lax.broadcasted_iota(jnp.int32, out.shape, 0)
        o_ref[0] = jnp.where(rows < NC - 1, out, 0.0).astype(o_ref.dtype)

    return pl.pallas_call(
        body, name="compress_fwd", grid=(N_KV,),
        in_specs=[pl.BlockSpec((1, NC, half), lambda h: (h, 0, 0)), pl.BlockSpec((1, 2 * half), lambda h: (0, 0)),
                  pl.BlockSpec((2 * half, CMP_HID), lambda h: (0, 0)), pl.BlockSpec((CMP_HID, HD), lambda h: (0, 0))],
        out_specs=[pl.BlockSpec((1, NC, HD), lambda h: (h, 0, 0)), pl.BlockSpec((1, NC, CMP_HID), lambda h: (h, 0, 0))],
        out_shape=[jax.ShapeDtypeStruct((N_KV, NC, HD), _MXU), jax.ShapeDtypeStruct((N_KV, NC, CMP_HID), F32)],
        compiler_params=_cp(("parallel",)))(R, pe, w1, w2)


def _compress_bwd(R, pe, w1, w2, hid, dout):
    NC = R.shape[1]
    half = 16 * HD

    def body(r_ref, pe_ref, w1_ref, w2_ref, hid_ref, do_ref, dr_ref, dw1_ref, dw2_ref, dpe_ref):
        @pl.when(pl.program_id(0) == 0)
        def _():
            dw1_ref[...] = jnp.zeros_like(dw1_ref)
            dw2_ref[...] = jnp.zeros_like(dw2_ref)
            dpe_ref[...] = jnp.zeros_like(dpe_ref)

        r, hv, do = r_ref[0], hid_ref[0], do_ref[0]
        s = _sigmoid(hv)
        dw2_ref[...] += _dot(hv * s, do, "tn")
        dhid = _dot(do, w2_ref[...], "nt") * (s * (1.0 + hv * (1.0 - s)))
        rows = lax.broadcasted_iota(jnp.int32, dhid.shape, 0)
        dhid = jnp.where(rows < NC - 1, dhid, 0.0)
        dhid_dn = pltpu.roll(dhid, 1, axis=0)
        dw1_ref[0:half, :] += _dot(r + pe_ref[:, 0:half], dhid, "tn")
        dw1_ref[half:2 * half, :] += _dot(r + pe_ref[:, half:2 * half], dhid_dn, "tn")
        dxt = _dot(dhid, w1_ref[0:half, :], "nt")
        dxb = _dot(dhid_dn, w1_ref[half:2 * half, :], "nt")
        dr_ref[0] = dxt + dxb
        dpe_ref[:, 0:half] += jnp.sum(dxt, axis=0, keepdims=True)
        dpe_ref[:, half:2 * half] += jnp.sum(dxb, axis=0, keepdims=True)

    return pl.pallas_call(
        body, name="compress_bwd", grid=(N_KV,),
        in_specs=[pl.BlockSpec((1, NC, half), lambda h: (h, 0, 0)), pl.BlockSpec((1, 2 * half), lambda h: (0, 0)),
                  pl.BlockSpec((2 * half, CMP_HID), lambda h: (0, 0)), pl.BlockSpec((CMP_HID, HD), lambda h: (0, 0)),
                  pl.BlockSpec((1, NC, CMP_HID), lambda h: (h, 0, 0)), pl.BlockSpec((1, NC, HD), lambda h: (h, 0, 0))],
        out_specs=[pl.BlockSpec((1, NC, half), lambda h: (h, 0, 0)), pl.BlockSpec((2 * half, CMP_HID), lambda h: (0, 0)),
                   pl.BlockSpec((CMP_HID, HD), lambda h: (0, 0)), pl.BlockSpec((1, 2 * half), lambda h: (0, 0))],
        out_shape=[jax.ShapeDtypeStruct((N_KV, NC, half), F32), jax.ShapeDtypeStruct((2 * half, CMP_HID), F32),
                   jax.ShapeDtypeStruct((CMP_HID, HD), F32), jax.ShapeDtypeStruct((1, 2 * half), F32)],
        compiler_params=_cp(("arbitrary",)))(R, pe, w1, w2, hid, dout)


def _attn_cfg(S, Sk, mode):
    tq = _pick(S, (256, 128))
    tk = Sk if mode == "cmp" else _pick(Sk, (256, 128))
    return tq, tk


def _kb_range(mode, q0, tq, tk):
    if mode == "cmp":
        return 0, 1
    hi = (q0 + tq - 1) // tk + 1
    if mode == "sel":
        return 0, hi
    return jnp.maximum(q0 - (WINDOW - 1), 0) // tk, hi


def _attn_bias(mode, q0, k0, tq, tk, sel_t):
    k = k0 + lax.broadcasted_iota(jnp.int32, (tk, tq), 0)
    t = q0 + lax.broadcasted_iota(jnp.int32, (tk, tq), 1)
    if mode == "cmp":
        ok = (k * 16 + 31) <= t
    elif mode == "win":
        ok = (k <= t) & ((t - k) < WINDOW)
    else:
        nb = sel_t.shape[0]
        ek = k0 + lax.broadcasted_iota(jnp.int32, (tk, nb), 0)
        eb = lax.broadcasted_iota(jnp.int32, (tk, nb), 1)
        expand = (lax.shift_right_logical(ek, 6) == eb).astype(_MXU)
        chosen = _dot(expand, sel_t, "nn") > 0.5
        ok = (k <= t) & chosen
    bias = jnp.where(ok, 0.0, NEG)
    return jnp.concatenate([bias] * GRP, axis=1), jnp.concatenate([ok.astype(F32)] * GRP, axis=1)


def _stack_heads(ref, tq):
    return jnp.concatenate([ref[:, g * HD:(g + 1) * HD] for g in range(GRP)], axis=0)


def _scaled_queries(q_ref, tq):
    return (_stack_heads(q_ref, tq).astype(F32) * SCALE).astype(_MXU)


def _blocked_t(x, tk):
    n, Sk, d = x.shape
    return x.reshape(n, Sk // tk, tk, d).transpose(0, 1, 3, 2)


def _attn_fwd(q, qcol0, k, v, mode, sel_t, name):
    S, Sk = q.shape[0], k.shape[1]
    tq, tk = _attn_cfg(S, Sk, mode)
    R = GRP * tq
    vt = _blocked_t(v, tk)

    def body(*refs):
        if mode == "sel":
            q_ref, k_ref, vt_ref, sel_ref, o_ref, lse_ref, m_scr, l_scr, acc = refs
        else:
            q_ref, k_ref, vt_ref, o_ref, lse_ref, m_scr, l_scr, acc = refs
        q0 = pl.program_id(1) * tq
        qs = _scaled_queries(q_ref, tq)
        m_scr[...] = jnp.full_like(m_scr, NEG)
        l_scr[...] = jnp.zeros_like(l_scr)
        acc[...] = jnp.zeros_like(acc)
        selv = sel_ref[0].astype(_MXU) if mode == "sel" else None

        def step(kb, carry):
            k0 = pl.multiple_of(kb * tk, tk)
            bias, okf = _attn_bias(mode, q0, k0, tq, tk, selv)
            s = _dot(k_ref[0, pl.ds(k0, tk), :], qs, "nt") + bias
            m_old = m_scr[...]
            m_new = jnp.maximum(m_old, jnp.max(s, axis=0, keepdims=True))
            p = jnp.exp(s - m_new)
            if mode == "cmp":
                p = p * okf
            alpha = jnp.exp(m_old - m_new)
            l_scr[...] = alpha * l_scr[...] + jnp.sum(p, axis=0, keepdims=True)
            acc[...] = alpha * acc[...] + _dot(vt_ref[0, kb], p, "nn")
            m_scr[...] = m_new
            return carry

        lo, hi = _kb_range(mode, q0, tq, tk)
        lax.fori_loop(lo, hi, step, 0)
        l = l_scr[...]
        good = l > 0.0
        o_t = acc[...] * jnp.where(good, 1.0 / jnp.where(good, l, 1.0), 0.0)
        lse = jnp.where(good, m_scr[...] + jnp.log(jnp.where(good, l, 1.0)), -NEG)
        for g in range(GRP):
            o_ref[:, g * HD:(g + 1) * HD] = o_t[:, g * tq:(g + 1) * tq].T
            lse_ref[0, g:g + 1, :] = lse[:, g * tq:(g + 1) * tq]

    ins = [q, k, vt]
    specs = [pl.BlockSpec((tq, GRP * HD), lambda h, i: (i, qcol0 + h)), pl.BlockSpec((1, Sk, HD), lambda h, i: (h, 0, 0)),
             pl.BlockSpec((1, Sk // tk, HD, tk), lambda h, i: (h, 0, 0, 0))]
    if mode == "sel":
        ins.append(sel_t)
        specs.append(pl.BlockSpec((1, sel_t.shape[1], tq), lambda h, i: (h, 0, i)))
    return pl.pallas_call(
        body, name=name, grid=(N_KV, S // tq), in_specs=specs,
        out_specs=[pl.BlockSpec((tq, GRP * HD), lambda h, i: (i, h)), pl.BlockSpec((1, GRP, tq), lambda h, i: (h, 0, i))],
        out_shape=[jax.ShapeDtypeStruct((S, ATT_WIDTH), F32), jax.ShapeDtypeStruct((N_KV, GRP, S), F32)],
        scratch_shapes=[pltpu.VMEM((1, R), F32), pltpu.VMEM((1, R), F32), pltpu.VMEM((HD, R), F32)],
        compiler_params=_cp(("parallel", "arbitrary")))(*ins)


def _attn_bwd(q, qcol0, k, v, o, lse, do, mode, sel_t, name):
    S, Sk = q.shape[0], k.shape[1]
    tq, tk = _attn_cfg(S, Sk, mode)
    R = GRP * tq
    kt = _blocked_t(k, tk)

    def body(*refs):
        if mode == "sel":
            q_ref, k_ref, kt_ref, v_ref, o_ref, lse_ref, do_ref, sel_ref, dq_ref, dk_ref, dv_ref, dq_scr = refs
        else:
            q_ref, k_ref, kt_ref, v_ref, o_ref, lse_ref, do_ref, dq_ref, dk_ref, dv_ref, dq_scr = refs

        @pl.when(pl.program_id(1) == 0)
        def _():
            dk_ref[...] = jnp.zeros_like(dk_ref)
            dv_ref[...] = jnp.zeros_like(dv_ref)

        q0 = pl.program_id(1) * tq
        qs = _scaled_queries(q_ref, tq)
        dos = _stack_heads(do_ref, tq)
        delta = _dot(jnp.ones((8, HD), F32), dos * _stack_heads(o_ref, tq), "nt", hi=True)[0:1, :]
        lsev = jnp.concatenate([lse_ref[0, g:g + 1, :] for g in range(GRP)], axis=1)
        dos = dos.astype(_MXU)
        dq_scr[...] = jnp.zeros_like(dq_scr)
        selv = sel_ref[0].astype(_MXU) if mode == "sel" else None

        def step(kb, carry):
            k0 = pl.multiple_of(kb * tk, tk)
            kv = k_ref[0, pl.ds(k0, tk), :]
            bias, okf = _attn_bias(mode, q0, k0, tq, tk, selv)
            p = jnp.exp(_dot(kv, qs, "nt") + bias - lsev)
            if mode == "cmp":
                p = p * okf
            dp = _dot(v_ref[0, pl.ds(k0, tk), :], dos, "nt")
            ds = p * (dp - delta)
            dq_scr[...] += _dot(kt_ref[0, kb], ds, "nn")
            dk_ref[0, pl.ds(k0, tk), :] += _dot(ds, qs, "nn")
            dv_ref[0, pl.ds(k0, tk), :] += _dot(p, dos, "nn")
            return carry

        lo, hi = _kb_range(mode, q0, tq, tk)
        lax.fori_loop(lo, hi, step, 0)
        for g in range(GRP):
            dq_ref[:, g * HD:(g + 1) * HD] = (dq_scr[:, g * tq:(g + 1) * tq] * SCALE).T

    kv_spec = pl.BlockSpec((1, Sk, HD), lambda h, i: (h, 0, 0))
    qo_spec = pl.BlockSpec((tq, GRP * HD), lambda h, i: (i, h))
    ins = [q, k, kt, v, o, lse, do]
    specs = [pl.BlockSpec((tq, GRP * HD), lambda h, i: (i, qcol0 + h)), kv_spec,
             pl.BlockSpec((1, Sk // tk, HD, tk), lambda h, i: (h, 0, 0, 0)), kv_spec, qo_spec,
             pl.BlockSpec((1, GRP, tq), lambda h, i: (h, 0, i)), qo_spec]
    if mode == "sel":
        ins.append(sel_t)
        specs.append(pl.BlockSpec((1, sel_t.shape[1], tq), lambda h, i: (h, 0, i)))
    return pl.pallas_call(
        body, name=name, grid=(N_KV, S // tq), in_specs=specs, out_specs=[qo_spec, kv_spec, kv_spec],
        out_shape=[jax.ShapeDtypeStruct((S, ATT_WIDTH), F32), jax.ShapeDtypeStruct((N_KV, Sk, HD), F32),
                   jax.ShapeDtypeStruct((N_KV, Sk, HD), F32)],
        scratch_shapes=[pltpu.VMEM((HD, R), F32)],
        compiler_params=_cp(("parallel", "arbitrary")))(*ins)


def _select(q, qcol0, k_cmp, lse):
    S, NC = q.shape[0], k_cmp.shape[1]
    NB = S // SEL_BLOCK
    tq = _pick(S, (256, 128))
    ci = np.arange(NC)[None, :] * 16
    sj = np.arange(NB)[:, None] * SEL_BLOCK
    ov_t = np.clip(np.minimum(ci + 32, sj + SEL_BLOCK) - np.maximum(ci, sj), 0, None) / 32.0
    ov_t[:, NC - 1] = 0.0
    ov_t = jnp.asarray(ov_t, F32)

    def body(q_ref, k_ref, lse_ref, ov_ref, sel_ref):
        q0 = pl.program_id(1) * tq
        bias, okf = _attn_bias("cmp", q0, 0, tq, NC, None)
        lsev = jnp.concatenate([lse_ref[0, g:g + 1, :] for g in range(GRP)], axis=1)
        p = jnp.exp(_dot(k_ref[0], _scaled_queries(q_ref, tq), "nt") + bias - lsev) * okf
        imp4 = _dot(ov_ref[...], p, "nn")
        imp = imp4[:, 0:tq] + imp4[:, tq:2 * tq] + imp4[:, 2 * tq:3 * tq] + imp4[:, 3 * tq:4 * tq]
        blk = lax.broadcasted_iota(jnp.int32, (NB, tq), 0)
        cur = lax.shift_right_logical(q0 + lax.broadcasted_iota(jnp.int32, (NB, tq), 1), 6)
        imp = jnp.where((blk == 0) | (blk == cur) | (blk == cur - 1), FORCE, imp)
        imp = jnp.where(blk <= cur, imp, -1.0)
        rank = jnp.zeros((NB, tq), F32)
        for j in range(NB):
            row = imp[j:j + 1, :]
            ahead = (row > imp) | ((row == imp) & (blk > j))
            rank = rank + ahead.astype(F32)
        sel_ref[0] = ((rank < float(N_SELECT)) & (imp >= 0.0)).astype(F32)

    return pl.pallas_call(
        body, name="select_blocks", grid=(N_KV, S // tq),
        in_specs=[pl.BlockSpec((tq, GRP * HD), lambda h, i: (i, qcol0 + h)), pl.BlockSpec((1, NC, HD), lambda h, i: (h, 0, 0)),
                  pl.BlockSpec((1, GRP, tq), lambda h, i: (h, 0, i)), pl.BlockSpec((NB, NC), lambda h, i: (0, 0))],
        out_specs=pl.BlockSpec((1, NB, tq), lambda h, i: (h, 0, i)),
        out_shape=jax.ShapeDtypeStruct((N_KV, NB, S), F32), compiler_params=_cp(("parallel", "parallel")))(q, k_cmp, lse, ov_t)


GATE_COL0 = SSD_HEADS


def _combine_fwd(o_cmp, o_sel, o_win, proj_small):
    S = o_cmp.shape[0]
    tr = _pick(S, (256, 128))

    def body(oc_ref, os_ref, ow_ref, g_ref, y_ref):
        gate = _sigmoid(g_ref[...])
        for h in range(N_HEADS):
            hs = slice(h * HD, (h + 1) * HD)
            c = GATE_COL0 + 3 * h
            y = gate[:, c:c + 1] * oc_ref[:, hs] + gate[:, c + 1:c + 2] * os_ref[:, hs] + gate[:, c + 2:c + 3] * ow_ref[:, hs]
            y_ref[:, hs] = y.astype(y_ref.dtype)

    row = pl.BlockSpec((tr, ATT_WIDTH), lambda i: (i, 0))
    return pl.pallas_call(
        body, name="combine_fwd", grid=(S // tr,), in_specs=[row, row, row, pl.BlockSpec((tr, W_SMALL), lambda i: (i, 0))],
        out_specs=row, out_shape=jax.ShapeDtypeStruct((S, ATT_WIDTH), _MXU), compiler_params=_cp(("parallel",)))(
            o_cmp, o_sel, o_win, proj_small)


def _combine_bwd(dmixed, o_cmp, o_sel, o_win, proj_small):
    S = o_cmp.shape[0]
    tr = _pick(S, (256, 128))

    def body(dy_ref, oc_ref, os_ref, ow_ref, g_ref, dc_ref, ds_ref, dw_ref, dg_ref):
        gate = _sigmoid(g_ref[...])
        lane = lax.broadcasted_iota(jnp.int32, (1, W_SMALL), 1)
        dg = jnp.zeros((tr, W_SMALL), F32)
        for h in range(N_HEADS):
            hs = slice(h * HD, (h + 1) * HD)
            dy = dy_ref[:, hs].astype(F32)
            for b, (o_ref, d_ref) in enumerate(((oc_ref, dc_ref), (os_ref, ds_ref), (ow_ref, dw_ref))):
                c = GATE_COL0 + 3 * h + b
                gv = gate[:, c:c + 1]
                d_ref[:, hs] = gv * dy
                dgate = jnp.sum(dy * o_ref[:, hs], axis=-1, keepdims=True) * (gv * (1.0 - gv))
                dg = dg + dgate * (lane == c).astype(F32)
        dg_ref[...] = dg

    row = pl.BlockSpec((tr, ATT_WIDTH), lambda i: (i, 0))
    small = pl.BlockSpec((tr, W_SMALL), lambda i: (i, 0))
    return pl.pallas_call(
        body, name="combine_bwd", grid=(S // tr,),
        in_specs=[pl.BlockSpec((tr, ATT_WIDTH), lambda i: (i, 1)), row, row, row, small], out_specs=[row, row, row, small],
        out_shape=[jax.ShapeDtypeStruct((S, ATT_WIDTH), F32)] * 3 + [jax.ShapeDtypeStruct((S, W_SMALL), F32)],
        compiler_params=_cp(("parallel",)))(dmixed, o_cmp, o_sel, o_win, proj_small)


def _heads_major(x):
    S = x.shape[0]
    return x.reshape(S, N_KV, HD).transpose(1, 0, 2)


def _tokens_major(x):
    return x.transpose(1, 0, 2).reshape(x.shape[1], N_KV * HD)


def _to_rows16(x):
    S = x.shape[0]
    return x.reshape(S // 16, 16, N_KV, HD).transpose(2, 0, 1, 3).reshape(N_KV, S // 16, 16 * HD)


def _from_rows16(r):
    NC = r.shape[1]
    return r.reshape(N_KV, NC, 16, HD).transpose(1, 2, 0, 3).reshape(NC * 16, N_KV * HD)


DT_COL0 = SSD_WIDTH + CONV_CH
GATE_IN_COL0 = D_IN - 3 * N_HEADS


def _split_w_in(w):
    main = jnp.concatenate([w[:, :DT_COL0], w[:, DT_COL0 + SSD_HEADS:GATE_IN_COL0]], axis=1)
    small = jnp.concatenate([w[:, DT_COL0:DT_COL0 + SSD_HEADS], w[:, GATE_IN_COL0:],
                             jnp.zeros((w.shape[0], W_SMALL - SSD_HEADS - 3 * N_HEADS), w.dtype)], axis=1)
    return main, small


def _merge_w_in(main, small):
    return jnp.concatenate([main[:, :DT_COL0], small[:, :SSD_HEADS].astype(main.dtype), main[:, DT_COL0:],
                            small[:, SSD_HEADS:SSD_HEADS + 3 * N_HEADS].astype(main.dtype)], axis=1)


QB, KCB, VCB, KSB, VSB, KWB, VWB = 10, 14, 15, 16, 17, 18, 19


def _col256(a, b):
    return a[:, b * 256:(b + 1) * 256]


def _local_step(x, tgt, p, late_weights=None, ffn_grads_ready=None):
    S = x.shape[0]
    cos, sin = _rope_tables(S)

    u, rs1 = _rms_fwd(x, p["attn_norm_w"], "attn_norm")
    proj = _mm(u, p["w_main"], "nn", F32, "in_proj")
    proj_small = _mm(u, p["w_small"], "nn", F32, "in_proj_small")
    xa = _conv_fwd(proj, p["conv_w"], p["conv_b"])
    y_ssd, y_pre, rs_ssd, hs = _ssd_fwd(proj, proj_small, xa, p["dt_bias"], p["a_log"], p["d_skip"], p["ssd_norm_w"])

    q_rot = _rope([proj], QB, ATT_WIDTH, cos, sin, 1.0, _MXU, "rope_q")
    ks_rot = _heads_major(_rope([proj], KSB, 256, cos, sin, 1.0, _MXU, "rope_ks"))
    kw_rot = _heads_major(_rope([proj], KWB, 256, cos, sin, 1.0, _MXU, "rope_kw"))
    vs = _heads_major(_col256(proj, VSB).astype(_MXU))
    vw = _heads_major(_col256(proj, VWB).astype(_MXU))
    rk, rv = _to_rows16(_col256(proj, KCB)), _to_rows16(_col256(proj, VCB))
    k_cmp, hid_k = _compress_fwd(rk, p["cmp_pe_k"], p["cmp_w1_k"], p["cmp_w2_k"])
    v_cmp, hid_v = _compress_fwd(rv, p["cmp_pe_v"], p["cmp_w1_v"], p["cmp_w2_v"])

    o_cmp, lse_cmp = _attn_fwd(proj, QB, k_cmp, v_cmp, "cmp", None, "attn_cmp_fwd")
    sel = _select(proj, QB, k_cmp, lse_cmp)
    o_sel, lse_sel = _attn_fwd(q_rot, 0, ks_rot, vs, "sel", sel, "attn_sel_fwd")
    o_win, lse_win = _attn_fwd(q_rot, 0, kw_rot, vw, "win", None, "attn_win_fwd")
    y_att = _combine_fwd(o_cmp, o_sel, o_win, proj_small)

    if late_weights is not None:
        p = {**p, **late_weights(y_att)}
    mixed = jnp.concatenate([y_ssd, y_att], axis=1)
    h1 = _mm(mixed, p["w_out"], "nn", F32, "out_proj", res=x)
    v, rs_ffn = _rms_fwd(h1, p["ffn_norm_w"], "ffn_norm")
    gt, up, act = _ffn_up(v, p["w_gate"], p["w_up"])
    h2 = _mm(act, p["w_down"], "nn", F32, "ffn_down", res=h1)
    loss, dh2, d_final_w = _final_loss(h2, p["final_norm_w"], tgt)

    g = {"final_norm_w": d_final_w}
    g["w_down"] = _mm(act, dh2, "tn", _MXU, "dw_down")
    dgt, dup = _ffn_dact(dh2, p["w_down"], gt, up)
    g["w_gate"] = _mm(v, dgt, "tn", _MXU, "dw_gate")
    g["w_up"] = _mm(v, dup, "tn", _MXU, "dw_up")
    ffn_norm_w = p["ffn_norm_w"] if ffn_grads_ready is None else p["ffn_norm_w"] + ffn_grads_ready(g)
    dv = _mm(dgt, p["w_gate"], "nt", F32, "dv_gate")
    dv = _mm(dup, p["w_up"], "nt", F32, "dv_up", res=dv)
    dh1, g["ffn_norm_w"] = _rms_bwd(dv, h1, rs_ffn, ffn_norm_w, dh2, "ffn_norm_bwd")
    g["w_out"] = _mm(mixed, dh1, "tn", _MXU, "dw_out")
    dmixed = _mm(dh1, p["w_out"], "nt", F32, "dmixed")

    dz, dxa, ddtr, g["dt_bias"], g["a_log"], g["d_skip"], g["ssd_norm_w"] = _ssd_bwd(
        dmixed, proj, proj_small, xa, y_pre, rs_ssd, hs, p["dt_bias"], p["a_log"], p["d_skip"], p["ssd_norm_w"])
    dxbc, g["conv_w"], g["conv_b"] = _conv_bwd(proj, p["conv_w"], p["conv_b"], dxa)

    do_cmp, do_sel, do_win, dgate = _combine_bwd(dmixed, o_cmp, o_sel, o_win, proj_small)
    dq_cmp, dk_cmp, dv_cmp = _attn_bwd(proj, QB, k_cmp, v_cmp, o_cmp, lse_cmp, do_cmp, "cmp", None, "attn_cmp_bwd")
    dq_sel, dks, dvs = _attn_bwd(q_rot, 0, ks_rot, vs, o_sel, lse_sel, do_sel, "sel", sel, "attn_sel_bwd")
    dq_win, dkw, dvw = _attn_bwd(q_rot, 0, kw_rot, vw, o_win, lse_win, do_win, "win", None, "attn_win_bwd")
    drk, g["cmp_w1_k"], g["cmp_w2_k"], g["cmp_pe_k"] = _compress_bwd(rk, p["cmp_pe_k"], p["cmp_w1_k"], p["cmp_w2_k"], hid_k, dk_cmp)
    drv, g["cmp_w1_v"], g["cmp_w2_v"], g["cmp_pe_v"] = _compress_bwd(rv, p["cmp_pe_v"], p["cmp_w1_v"], p["cmp_w2_v"], hid_v, dv_cmp)
    dq = _rope([dq_sel, dq_win], 0, ATT_WIDTH, cos, sin, -1.0, _MXU, "rope_dq", extra=(dq_cmp, 0))
    dks_t = _rope([_tokens_major(dks)], 0, 256, cos, sin, -1.0, _MXU, "rope_dks")
    dkw_t = _rope([_tokens_major(dkw)], 0, 256, cos, sin, -1.0, _MXU, "rope_dkw")
    dproj = jnp.concatenate(
        [dz, dxbc, dq] + [t.astype(_MXU) for t in (_from_rows16(drk), _from_rows16(drv))]
        + [dks_t, _tokens_major(dvs).astype(_MXU), dkw_t, _tokens_major(dvw).astype(_MXU)], axis=1)
    dsmall = jnp.concatenate([ddtr, dgate[:, GATE_COL0:GATE_COL0 + 3 * N_HEADS],
                              jnp.zeros((S, W_SMALL - SSD_HEADS - 3 * N_HEADS), F32)], axis=1).astype(_MXU)
    g["w_main"] = _mm(u, dproj, "tn", _MXU, "dw_in")
    g["w_small"] = _mm(u, dsmall, "tn", F32, "dw_in_small")
    du = _mm(dproj, p["w_main"], "nt", F32, "du_main")
    du = _mm(dsmall, p["w_small"], "nt", F32, "du_small", res=du)
    grad_x, g["attn_norm_w"] = _rms_bwd(du, x, rs1, p["attn_norm_w"], dh1, "attn_norm_bwd")
    return loss, grad_x, g


MESH_ID = pl.DeviceIdType.MESH


def _my_coords():
    return lax.axis_index("x"), lax.axis_index("y"), lax.axis_index("c")


def _flat_id(px, py, pc):
    return 4 * px + 2 * py + pc


def _peer(k):
    mx, my, mc = _my_coords()
    return (1 - mx if k & 4 else mx, 1 - my if k & 2 else my, 1 - mc if k & 1 else mc)


def _exchange(arrs, scatter, name):
    n = len(arrs)

    def body(*refs):
        ins, outs = refs[:n], refs[n:2 * n]
        send_sems, recv_sems, local_sems = refs[2 * n:]
        me = _flat_id(*_my_coords())
        copies = []
        for i in range(n):
            src_me = ins[i].at[me] if scatter else ins[i]
            local = pltpu.make_async_copy(src_me, outs[i].at[me], local_sems.at[i])
            local.start()
            copies.append(local)
        for k in range(1, N_DEV):
            peer = _peer(k)
            for i in range(n):
                src = ins[i].at[_flat_id(*peer)] if scatter else ins[i]
                cp = pltpu.make_async_remote_copy(src_ref=src, dst_ref=outs[i].at[me], send_sem=send_sems.at[i * 7 + k - 1],
                                                  recv_sem=recv_sems.at[i * 7 + k - 1], device_id=peer, device_id_type=MESH_ID)
                cp.start()
                copies.append(cp)
        for cp in copies:
            cp.wait()

    any_spec = pl.BlockSpec(memory_space=pl.ANY)
    out_shape = [jax.ShapeDtypeStruct(a.shape if scatter else (N_DEV,) + a.shape, a.dtype) for a in arrs]
    return pl.pallas_call(
        body, name=name, in_specs=[any_spec] * n, out_specs=[any_spec] * n, out_shape=out_shape,
        scratch_shapes=[pltpu.SemaphoreType.DMA((n * 7,)), pltpu.SemaphoreType.DMA((n * 7,)), pltpu.SemaphoreType.DMA((n,))],
        compiler_params=pltpu.CompilerParams(has_side_effects=True))(*arrs)


_HBM = pl.BlockSpec(memory_space=pltpu.HBM)
_SEM = pl.BlockSpec(memory_space=pltpu.SEMAPHORE)
_EFFECT = pltpu.SideEffectType.DATAFLOW_SIDE_EFFECTING


def _split_copies(ins, lands, send_sems, recv_sems, scatter):
    me = _flat_id(*_my_coords())
    out = []
    for k in range(1, N_DEV):
        peer = _peer(k)
        for i in range(len(ins)):
            src = ins[i].at[_flat_id(*peer)] if scatter else ins[i]
            out.append(pltpu.make_async_remote_copy(src_ref=src, dst_ref=lands[i].at[me], send_sem=send_sems.at[i * 7 + k - 1],
                                                    recv_sem=recv_sems.at[i * 7 + k - 1], device_id=peer, device_id_type=MESH_ID))
    return out


def _split_start(arrs, scatter, name):
    n = len(arrs)

    def body(*refs):
        for cp in _split_copies(refs[:n], refs[n:2 * n], refs[2 * n], refs[2 * n + 1], scatter):
            cp.start()
        refs[-1][...] = jnp.zeros_like(refs[-1])

    land_shapes = [a.shape if scatter else (N_DEV,) + a.shape for a in arrs]
    out_shape = ((pltpu.SemaphoreType.DMA((n * 7,)), pltpu.SemaphoreType.DMA((n * 7,)))
                 + tuple(pltpu.HBM(a.shape, a.dtype) for a in arrs) + tuple(pltpu.HBM(s, a.dtype) for s, a in zip(land_shapes, arrs))
                 + (jax.ShapeDtypeStruct((8, 128), F32),))
    operands = ([pltpu.with_memory_space_constraint(a, pltpu.HBM) for a in arrs]
                + [pltpu.with_memory_space_constraint(lax.empty(s, a.dtype), pltpu.HBM) for s, a in zip(land_shapes, arrs)])
    res = pl.pallas_call(
        body, name=name, out_shape=out_shape, in_specs=[_HBM] * (2 * n),
        out_specs=(_SEM, _SEM) + (_HBM,) * (2 * n) + (pl.BlockSpec(memory_space=pltpu.VMEM),),
        input_output_aliases={i: 2 + i for i in range(2 * n)},
        compiler_params=pltpu.CompilerParams(has_side_effects=_EFFECT))(*operands)
    return dict(send=res[0], recv=res[1], ins=list(res[2:2 + n]), lands=list(res[2 + n:2 + 2 * n]), token=res[-1])


def _split_wait(st, scatter, after, name):
    n = len(st["ins"])

    def body(*refs):
        for cp in _split_copies(refs[:n], refs[n:2 * n], refs[2 * n], refs[2 * n + 1], scatter):
            cp.wait_send()
            cp.wait_recv()

    arrs = st["ins"] + st["lands"]
    res = pl.pallas_call(
        body, name=name, out_shape=tuple(pltpu.HBM(a.shape, a.dtype) for a in arrs),
        in_specs=[_HBM] * (2 * n) + [_SEM, _SEM, pl.BlockSpec(memory_space=pl.ANY)], out_specs=(_HBM,) * (2 * n),
        input_output_aliases={i: i for i in range(2 * n)},
        compiler_params=pltpu.CompilerParams(has_side_effects=_EFFECT))(*arrs, st["send"], st["recv"], after)
    me = _flat_id(*_my_coords())
    out = []
    for src, land in zip(res[:n], res[n:]):
        own = lax.dynamic_index_in_dim(src, me, 0, keepdims=True) if scatter else src[None]
        out.append(lax.dynamic_update_slice_in_dim(land, own, me, 0))
    return out


def _sum_parts(parts):
    P, R, C = parts.shape
    tr = _pick(R, (136, 8))

    def body(p_ref, o_ref):
        acc = p_ref[0]
        for j in range(1, P):
            acc = acc + p_ref[j]
        o_ref[...] = acc

    return pl.pallas_call(
        body, name="sum_small_grads", grid=(R // tr,), in_specs=[pl.BlockSpec((P, tr, C), lambda i: (0, i, 0))],
        out_specs=pl.BlockSpec((tr, C), lambda i: (i, 0)), out_shape=jax.ShapeDtypeStruct((R, C), F32),
        compiler_params=_cp(("parallel",)))(parts)


def _adam_sum(parts, w, m, v, name):
    P, R, C = parts.shape
    tr = _pick(R, (256, 128, 64, 32, 8)) if C <= 1024 else _pick(R, (128, 64, 32, 8))

    def body(p_ref, w_ref, m_ref, v_ref, g_ref, d_ref, nm_ref, nv_ref):
        g = p_ref[0].astype(F32)
        for j in range(1, P):
            g = g + p_ref[j].astype(F32)
        g_ref[...] = g
        nm = ADAM_B1 * m_ref[...] + (1.0 - ADAM_B1) * g
        nv = ADAM_B2 * v_ref[...] + (1.0 - ADAM_B2) * (g * g)
        nm_ref[...] = nm
        nv_ref[...] = nv
        m_hat = nm / (1.0 - ADAM_B1 ** ADAM_STEP)
        v_hat = nv / (1.0 - ADAM_B2 ** ADAM_STEP)
        d_ref[...] = -ADAM_LR * (m_hat / (jnp.sqrt(v_hat) + ADAM_EPS) + ADAM_WD * w_ref[...])

    blk = pl.BlockSpec((tr, C), lambda i: (i, 0))
    return pl.pallas_call(
        body, name=name, grid=(R // tr,), in_specs=[pl.BlockSpec((P, tr, C), lambda i: (0, i, 0)), blk, blk, blk],
        out_specs=[blk] * 4, out_shape=[jax.ShapeDtypeStruct((R, C), F32)] * 4, compiler_params=_cp(("parallel",)))(parts, w, m, v)


def _pack(arrs):
    rows = []
    for a in arrs:
        f = a.reshape(-1).astype(F32)
        f = jnp.pad(f, (0, (-f.shape[0]) % 128))
        rows.append(f.reshape(-1, 128))
    out = jnp.concatenate(rows, axis=0)
    return jnp.pad(out, ((0, (-out.shape[0]) % 8), (0, 0)))


def _unpack(pack, shapes):
    out, r = [], 0
    for s in shapes:
        n = int(np.prod(s))
        nr = -(-n // 128)
        out.append(pack[r:r + nr].reshape(-1)[:n].reshape(s))
        r += nr
    return out


_WEIGHTS = ["attn_norm_w", "w_in", "conv_w", "conv_b", "dt_bias", "a_log", "d_skip", "ssd_norm_w", "cmp_w1_k", "cmp_w2_k",
            "cmp_w1_v", "cmp_w2_v", "cmp_pe_k", "cmp_pe_v", "w_out", "ffn_norm_w", "w_gate", "w_up", "w_down", "final_norm_w"]
_BIG = ["w_in", "w_gate", "w_up", "w_down", "w_out", "cmp_w1_k", "cmp_w1_v"]
_EARLY = ["w_in", "cmp_w1_k", "cmp_w1_v"]
_LATE = ["w_out", "w_gate", "w_up", "w_down"]
_FFN = ["w_down", "w_gate", "w_up"]
_REST = ["w_in", "w_out", "cmp_w1_k", "cmp_w1_v"]
_COL_SHARDED = ("w_in", "w_gate", "w_up")
_REPLICATED = ["attn_norm_w", "conv_b", "dt_bias", "a_log", "d_skip", "ssd_norm_w", "cmp_pe_k", "cmp_pe_v", "ffn_norm_w",
               "final_norm_w"]
_SMALL_SHARDED = ["conv_w", "cmp_w2_k", "cmp_w2_v"]
_SMALL_FULL_SHAPES = {"attn_norm_w": (1, D_MODEL), "conv_b": (1, CONV_CH), "dt_bias": (1, SSD_HEADS), "a_log": (1, SSD_HEADS),
                      "d_skip": (1, SSD_HEADS), "ssd_norm_w": (1, SSD_WIDTH), "cmp_pe_k": (1, 32 * HD), "cmp_pe_v": (1, 32 * HD),
                      "ffn_norm_w": (1, D_MODEL), "final_norm_w": (1, D_MODEL), "conv_w": (CONV_K, CONV_CH),
                      "cmp_w2_k": (CMP_HID, HD), "cmp_w2_v": (CMP_HID, HD)}


def _cols_to_slabs(g):
    R = g.shape[0]
    return g.reshape(R, N_DEV, -1).transpose(1, 0, 2)


def _slabs_to_cols(s):
    return s.transpose(1, 0, 2).reshape(s.shape[1], -1)


def kernel(x, attn_norm_w, w_in, conv_w, conv_b, dt_bias, a_log, d_skip, ssd_norm_w, cmp_w1_k, cmp_w2_k, cmp_w1_v, cmp_w2_v, cmp_pe_k, cmp_pe_v, w_out, ffn_norm_w, w_gate, w_up, w_down, final_norm_w, loss_target, m_attn_norm_w, m_w_in, m_conv_w, m_conv_b, m_dt_bias, m_a_log, m_d_skip, m_ssd_norm_w, m_cmp_w1_k, m_cmp_w2_k, m_cmp_w1_v, m_cmp_w2_v, m_cmp_pe_k, m_cmp_pe_v, m_w_out, m_ffn_norm_w, m_w_gate, m_w_up, m_w_down, m_final_norm_w, v_attn_norm_w, v_w_in, v_conv_w, v_conv_b, v_dt_bias, v_a_log, v_d_skip, v_ssd_norm_w, v_cmp_w1_k, v_cmp_w2_k, v_cmp_w1_v, v_cmp_w2_v, v_cmp_pe_k, v_cmp_pe_v, v_w_out, v_ffn_norm_w, v_w_gate, v_w_up, v_w_down, v_final_norm_w):
    a = dict(locals())
    me = _flat_id(*_my_coords())

    small_in = _pack([cmp_w2_k[0], cmp_w2_v[0], conv_w[0]])
    shard = {n: a[n][0].astype(_MXU) for n in _BIG}
    st_early = _split_start([shard[n] for n in _EARLY] + [small_in], False, "gather_early_start")
    zero = st_early["token"][0, 0].astype(_MXU)
    st_late = _split_start([shard[_LATE[0]] + zero] + [shard[n] for n in _LATE[1:]], False, "gather_late_start")
    got = _split_wait(st_early, False, st_late["token"], "gather_early_wait")

    def assemble(n, t):
        return _slabs_to_cols(t) if n in _COL_SHARDED else t.reshape(-1, t.shape[-1])

    full = {n: assemble(n, t) for n, t in zip(_EARLY, got[:-1])}
    w2k, w2v, cw = [], [], []
    for d in range(N_DEV):
        parts = _unpack(got[-1][d], [cmp_w2_k.shape[1:], cmp_w2_v.shape[1:], conv_w.shape[1:]])
        w2k.append(parts[0]); w2v.append(parts[1]); cw.append(parts[2])
    w_main, w_small = _split_w_in(full["w_in"])
    p = dict(attn_norm_w=attn_norm_w, w_main=w_main, w_small=w_small, conv_w=jnp.concatenate(cw, axis=1), conv_b=conv_b,
             dt_bias=dt_bias, a_log=a_log, d_skip=d_skip, ssd_norm_w=ssd_norm_w, cmp_w1_k=full["cmp_w1_k"],
             cmp_w2_k=jnp.concatenate(w2k, axis=0).astype(_MXU), cmp_w1_v=full["cmp_w1_v"],
             cmp_w2_v=jnp.concatenate(w2v, axis=0).astype(_MXU), cmp_pe_k=cmp_pe_k.reshape(1, -1), cmp_pe_v=cmp_pe_v.reshape(1, -1),
             ffn_norm_w=ffn_norm_w, final_norm_w=final_norm_w.reshape(1, -1))

    def late_weights(after):
        got_late = _split_wait(st_late, False, after, "gather_late_wait")
        return {n: assemble(n, t) for n, t in zip(_LATE, got_late)}

    def slabs_of(g, names):
        return [(_cols_to_slabs(g[n]) if n in _COL_SHARDED else g[n].reshape(N_DEV, -1, g[n].shape[-1])).astype(_MXU) for n in names]

    pending = {}

    def ffn_grads_ready(g):
        pending["ffn"] = _split_start(slabs_of(g, _FFN), True, "scatter_ffn_grads_start")
        return pending["ffn"]["token"][0:1, 0:1]

    loss_part, grad_x, g = _local_step(x[0], loss_target[0], p, late_weights, ffn_grads_ready)
    loss = lax.psum(loss_part[0, 0], ("x", "y", "c"))
    g["w_in"] = _merge_w_in(g.pop("w_main"), g.pop("w_small"))

    st_rest = _split_start(slabs_of(g, _REST), True, "scatter_rest_grads_start")
    recv_ffn = _split_wait(pending["ffn"], True, st_rest["token"], "scatter_ffn_grads_wait")
    recv_rest = _split_wait(st_rest, True, recv_ffn[0], "scatter_rest_grads_wait")
    out = {}
    for n, parts in zip(_FFN + _REST, recv_ffn + recv_rest):
        out[n] = _adam_sum(parts, a[n][0], a["m_" + n][0], a["v_" + n][0], "adam_" + n)

    small_names = _REPLICATED + _SMALL_SHARDED
    g_small = _pack([g[n] for n in small_names])
    g_sum = _sum_parts(_exchange([g_small], False, "gather_small_grads")[0])
    gs = dict(zip(small_names, _unpack(g_sum, [_SMALL_FULL_SHAPES[n] for n in small_names])))
    gs["conv_w"] = lax.dynamic_slice_in_dim(gs["conv_w"], me * conv_w.shape[2], conv_w.shape[2], axis=1)
    gs["cmp_w2_k"] = lax.dynamic_slice_in_dim(gs["cmp_w2_k"], me * cmp_w2_k.shape[1], cmp_w2_k.shape[1], axis=0)
    gs["cmp_w2_v"] = lax.dynamic_slice_in_dim(gs["cmp_w2_v"], me * cmp_w2_v.shape[1], cmp_w2_v.shape[1], axis=0)
    packs = [_pack([t[n] for n in small_names]) for t in
             (gs, a, {n: a["m_" + n] for n in small_names}, {n: a["v_" + n] for n in small_names})]
    res_small = _adam_sum(packs[0][None], packs[1], packs[2], packs[3], "adam_small")
    shapes = [a[n].shape for n in small_names]
    unpacked = [dict(zip(small_names, _unpack(r, shapes))) for r in res_small]
    for n in small_names:
        out[n] = tuple(u[n] for u in unpacked)

    outs = [loss, grad_x[None]]
    for j in range(4):
        for n in _WEIGHTS:
            outs.append(out[n][j].reshape(a[n].shape))
    return tuple(outs)
```

```python
import functools
import math

import numpy as np
import jax
import jax.numpy as jnp
from jax import lax
from jax.experimental import pallas as pl
from jax.experimental.pallas import tpu as pltpu

F32 = jnp.float32
_MXU = jnp.bfloat16
_HI = lax.Precision.HIGHEST

N_DEV = 8
D_MODEL = 2048
SSD_WIDTH = 1024
ATT_WIDTH = 1024
SSD_HEADS = 16
SSD_P = 64
SSD_N = 128
SSD_L = 128
SSD_G = 2
CONV_CH = 1536
CONV_K = 4
HD = 64
N_HEADS = 16
N_KV = 4
GRP = 4
CMP_HID = 256
SEL_BLOCK = 64
N_SELECT = 16
WINDOW = 512
ROPE_DIM = 16
ROPE_THETA = 500000.0
D_FF = 5632
EPS = 1e-6
NEG = -1e30
FORCE = 1e4
SCALE = HD ** -0.5
D_IN = 5184
W_MAIN = 5120
W_SMALL = 128
VMEM_LIMIT = 52 * 1024 * 1024

ADAM_LR, ADAM_B1, ADAM_B2, ADAM_EPS, ADAM_WD, ADAM_STEP = 0.001, 0.9, 0.999, 1e-08, 0.01, 10


def _pick(n, cands):
    for c in cands:
        if n % c == 0:
            return c
    return n


def _cp(sem=None):
    return pltpu.CompilerParams(dimension_semantics=sem, vmem_limit_bytes=VMEM_LIMIT)


def _sigmoid(x):
    return 1.0 / (1.0 + jnp.exp(-x))


def _dot(a, b, dims, hi=False):
    dn = {"nn": (((1,), (0,)), ((), ())), "nt": (((1,), (1,)), ((), ())), "tn": (((0,), (0,)), ((), ()))}[dims]
    if hi:
        return lax.dot_general(a.astype(F32), b.astype(F32), dn, precision=_HI, preferred_element_type=F32)
    return lax.dot_general(a.astype(_MXU), b.astype(_MXU), dn, preferred_element_type=F32)


LANE = 128
MM_TILE = 1024
MM_K_WHOLE = 2048
MM_K_STEP = 1536
TN_ACC_ELEMS = 3 * 2 ** 20
TN_K_STEP = 512


def _largest_tile(n, cap):
    if n <= cap:
        return n
    best = LANE
    for t in range(LANE, cap + 1, LANE):
        if n % t == 0:
            best = t
    return best


def _mm_tiles(mode, M, N, K):
    if mode == "tn":
        tm = _largest_tile(M, 2 * MM_TILE)
        return tm, _largest_tile(N, TN_ACC_ELEMS // tm), _largest_tile(K, TN_K_STEP)
    tk = K if K <= MM_K_WHOLE else _largest_tile(K, MM_K_STEP)
    return _largest_tile(M, MM_TILE), _largest_tile(N, MM_TILE), tk


def _mm(a, b, mode, out_dtype, name, res=None):
    if mode == "nn":
        (M, K), N = a.shape, b.shape[1]
    elif mode == "nt":
        (M, K), N = a.shape, b.shape[0]
    else:
        (K, M), N = a.shape, b.shape[1]
    tm, tn, tk = _mm_tiles(mode, M, N, K)
    nk = K // tk
    a_spec = pl.BlockSpec((tk, tm), lambda i, j, k: (k, i)) if mode == "tn" else pl.BlockSpec((tm, tk), lambda i, j, k: (i, k))
    b_spec = pl.BlockSpec((tn, tk), lambda i, j, k: (j, k)) if mode == "nt" else pl.BlockSpec((tk, tn), lambda i, j, k: (k, j))
    o_spec = pl.BlockSpec((tm, tn), lambda i, j, k: (i, j))

    def finish(r, r_ref, o_ref):
        if res is not None:
            r = r + r_ref[...].astype(F32)
        o_ref[...] = r.astype(out_dtype)

    def body_one_step(*refs):
        a_ref, b_ref, o_ref = refs[0], refs[1], refs[-1]
        finish(_dot(a_ref[...], b_ref[...], mode), refs[2], o_ref)

    def body(*refs):
        a_ref, b_ref, o_ref, acc = refs[0], refs[1], refs[-2], refs[-1]
        k = pl.program_id(2)

        @pl.when(k == 0)
        def _():
            acc[...] = jnp.zeros_like(acc)

        acc[...] += _dot(a_ref[...], b_ref[...], mode)

        @pl.when(k == nk - 1)
        def _():
            finish(acc[...], refs[2], o_ref)

    ins, specs = [a, b], [a_spec, b_spec]
    if res is not None:
        ins.append(res)
        specs.append(o_spec)
    return pl.pallas_call(
        body_one_step if nk == 1 else body, name=name, grid=(M // tm, N // tn, nk), in_specs=specs, out_specs=o_spec,
        out_shape=jax.ShapeDtypeStruct((M, N), out_dtype), scratch_shapes=[] if nk == 1 else [pltpu.VMEM((tm, tn), F32)],
        compiler_params=_cp(("parallel", "parallel", "arbitrary")))(*ins)


def _ffn_up(v, w_gate, w_up):
    S, D = v.shape
    F = w_gate.shape[1]
    tm, tn = _largest_tile(S, MM_TILE), _largest_tile(F, MM_TILE // 2)

    def body(v_ref, wg_ref, wu_ref, gt_ref, up_ref, act_ref):
        vv = v_ref[...]
        g = _dot(vv, wg_ref[...], "nn")
        u = _dot(vv, wu_ref[...], "nn")
        gt_ref[...] = g
        up_ref[...] = u
        act_ref[...] = (g * _sigmoid(g) * u).astype(act_ref.dtype)

    o_spec = pl.BlockSpec((tm, tn), lambda i, j: (i, j))
    w_spec = pl.BlockSpec((D, tn), lambda i, j: (0, j))
    return pl.pallas_call(
        body, name="ffn_up", grid=(S // tm, F // tn),
        in_specs=[pl.BlockSpec((tm, D), lambda i, j: (i, 0)), w_spec, w_spec], out_specs=[o_spec, o_spec, o_spec],
        out_shape=[jax.ShapeDtypeStruct((S, F), F32), jax.ShapeDtypeStruct((S, F), F32), jax.ShapeDtypeStruct((S, F), _MXU)],
        compiler_params=_cp(("parallel", "parallel")))(v, w_gate, w_up)


def _ffn_dact(dh2, w_down, gt, up):
    S, D = dh2.shape
    F = w_down.shape[0]
    tm, tn = _largest_tile(S, MM_TILE), _largest_tile(F, MM_TILE // 2)

    def body(d_ref, w_ref, gt_ref, up_ref, dg_ref, du_ref):
        da, g, u = _dot(d_ref[...], w_ref[...], "nt"), gt_ref[...], up_ref[...]
        s = _sigmoid(g)
        dg_ref[...] = (da * u * (s * (1.0 + g * (1.0 - s)))).astype(dg_ref.dtype)
        du_ref[...] = (da * (g * s)).astype(du_ref.dtype)

    o_spec = pl.BlockSpec((tm, tn), lambda i, j: (i, j))
    return pl.pallas_call(
        body, name="ffn_dact", grid=(S // tm, F // tn),
        in_specs=[pl.BlockSpec((tm, D), lambda i, j: (i, 0)), pl.BlockSpec((tn, D), lambda i, j: (j, 0)), o_spec, o_spec],
        out_specs=[o_spec, o_spec],
        out_shape=[jax.ShapeDtypeStruct((S, F), _MXU), jax.ShapeDtypeStruct((S, F), _MXU)],
        compiler_params=_cp(("parallel", "parallel")))(dh2, w_down, gt, up)


def _rms_fwd(x, w, name):
    S, D = x.shape
    tr = _pick(S, (256, 128))

    def body(x_ref, w_ref, xn_ref, rs_ref):
        xv = x_ref[...]
        rs = lax.rsqrt(jnp.mean(xv * xv, axis=-1, keepdims=True) + EPS)
        xn_ref[...] = ((xv * rs) * w_ref[...]).astype(xn_ref.dtype)
        rs_ref[...] = rs

    return pl.pallas_call(
        body, name=name, grid=(S // tr,),
        in_specs=[pl.BlockSpec((tr, D), lambda i: (i, 0)), pl.BlockSpec((1, D), lambda i: (0, 0))],
        out_specs=[pl.BlockSpec((tr, D), lambda i: (i, 0)), pl.BlockSpec((tr, 1), lambda i: (i, 0))],
        out_shape=[jax.ShapeDtypeStruct((S, D), _MXU), jax.ShapeDtypeStruct((S, 1), F32)],
        compiler_params=_cp(("parallel",)))(x, w)


def _rms_bwd(dyn, x, rs, w, res, name):
    S, D = x.shape
    tr = _pick(S, (256, 128))

    def body(dy_ref, x_ref, rs_ref, w_ref, res_ref, dx_ref, dxb_ref, dw_ref):
        @pl.when(pl.program_id(0) == 0)
        def _():
            dw_ref[...] = jnp.zeros_like(dw_ref)

        dy, r = dy_ref[...].astype(F32), rs_ref[...]
        xhat = x_ref[...] * r
        dw_ref[...] += jnp.sum(dy * xhat, axis=0, keepdims=True)
        dxhat = dy * w_ref[...]
        dx = res_ref[...] + r * (dxhat - xhat * jnp.mean(dxhat * xhat, axis=-1, keepdims=True))
        dx_ref[...] = dx
        dxb_ref[...] = dx.astype(dxb_ref.dtype)

    row = pl.BlockSpec((tr, D), lambda i: (i, 0))
    vec = pl.BlockSpec((1, D), lambda i: (0, 0))
    return pl.pallas_call(
        body, name=name, grid=(S // tr,),
        in_specs=[row, row, pl.BlockSpec((tr, 1), lambda i: (i, 0)), vec, row], out_specs=[row, row, vec],
        out_shape=[jax.ShapeDtypeStruct((S, D), F32), jax.ShapeDtypeStruct((S, D), _MXU), jax.ShapeDtypeStruct((1, D), F32)],
        compiler_params=_cp(("arbitrary",)))(dyn, x, rs, w, res)


def _final_loss(h2, w, tgt):
    S, D = h2.shape
    tr = _pick(S, (256, 128))

    def body(h_ref, w_ref, t_ref, loss_ref, dh_ref, dhb_ref, dw_ref):
        @pl.when(pl.program_id(0) == 0)
        def _():
            dw_ref[...] = jnp.zeros_like(dw_ref)
            loss_ref[...] = jnp.zeros_like(loss_ref)

        hv, wv = h_ref[...], w_ref[...]
        rs = lax.rsqrt(jnp.mean(hv * hv, axis=-1, keepdims=True) + EPS)
        xhat = hv * rs
        err = xhat * wv - t_ref[...]
        row = jnp.mean(err * err, axis=-1, keepdims=True)
        loss_ref[...] += 0.5 * jnp.sum(row, axis=0, keepdims=True)
        dy = err * (1.0 / D)
        dw_ref[...] += jnp.sum(dy * xhat, axis=0, keepdims=True)
        dxhat = dy * wv
        dh = rs * (dxhat - xhat * jnp.mean(dxhat * xhat, axis=-1, keepdims=True))
        dh_ref[...] = dh
        dhb_ref[...] = dh.astype(dhb_ref.dtype)

    row = pl.BlockSpec((tr, D), lambda i: (i, 0))
    vec = pl.BlockSpec((1, D), lambda i: (0, 0))
    return pl.pallas_call(
        body, name="final_loss", grid=(S // tr,), in_specs=[row, vec, row],
        out_specs=[pl.BlockSpec((1, 1), lambda i: (0, 0)), row, row, vec],
        out_shape=[jax.ShapeDtypeStruct((1, 1), F32), jax.ShapeDtypeStruct((S, D), F32), jax.ShapeDtypeStruct((S, D), _MXU),
                   jax.ShapeDtypeStruct((1, D), F32)],
        compiler_params=_cp(("arbitrary",)))(h2, w, tgt)


def _shift_rows(x, k, rows):
    if k == 0:
        return x
    S = x.shape[0]
    r = pltpu.roll(x, k % S, axis=0)
    ok = (rows >= k) if k > 0 else (rows < S + k)
    return jnp.where(ok, r, 0.0)


XBC_COL0 = SSD_WIDTH // 128


def _conv_fwd(proj, conv_w, conv_b):
    S = proj.shape[0]
    nct = CONV_CH // 128

    def body(x_ref, w_ref, b_ref, o_ref):
        x = x_ref[...]
        rows = lax.broadcasted_iota(jnp.int32, x.shape, 0)
        c = b_ref[...] + w_ref[3:4, :] * x
        for k in range(1, CONV_K):
            c = c + w_ref[3 - k:4 - k, :] * _shift_rows(x, k, rows)
        o_ref[...] = c * _sigmoid(c)

    return pl.pallas_call(
        body, name="conv_fwd", grid=(nct,),
        in_specs=[pl.BlockSpec((S, 128), lambda j: (0, XBC_COL0 + j)), pl.BlockSpec((CONV_K, 128), lambda j: (0, j)),
                  pl.BlockSpec((1, 128), lambda j: (0, j))],
        out_specs=pl.BlockSpec((S, 128), lambda j: (0, j)),
        out_shape=jax.ShapeDtypeStruct((S, CONV_CH), F32), compiler_params=_cp(("parallel",)))(proj, conv_w, conv_b)


def _conv_bwd(proj, conv_w, conv_b, dxa):
    S = proj.shape[0]
    nct = CONV_CH // 128

    def body(x_ref, w_ref, b_ref, d_ref, dx_ref, dw_ref, db_ref):
        x = x_ref[...]
        rows = lax.broadcasted_iota(jnp.int32, x.shape, 0)
        xs = [_shift_rows(x, k, rows) for k in range(CONV_K)]
        c = b_ref[...] + w_ref[3:4, :] * x
        for k in range(1, CONV_K):
            c = c + w_ref[3 - k:4 - k, :] * xs[k]
        s = _sigmoid(c)
        dc = d_ref[...] * (s * (1.0 + c * (1.0 - s)))
        dx = w_ref[3:4, :] * dc
        for k in range(1, CONV_K):
            dx = dx + w_ref[3 - k:4 - k, :] * _shift_rows(dc, -k, rows)
        dx_ref[...] = dx.astype(dx_ref.dtype)
        for k in range(CONV_K):
            dw_ref[3 - k:4 - k, :] = jnp.sum(dc * xs[k], axis=0, keepdims=True)
        db_ref[...] = jnp.sum(dc, axis=0, keepdims=True)

    col = pl.BlockSpec((S, 128), lambda j: (0, j))
    return pl.pallas_call(
        body, name="conv_bwd", grid=(nct,),
        in_specs=[pl.BlockSpec((S, 128), lambda j: (0, XBC_COL0 + j)), pl.BlockSpec((CONV_K, 128), lambda j: (0, j)),
                  pl.BlockSpec((1, 128), lambda j: (0, j)), col],
        out_specs=[col, pl.BlockSpec((CONV_K, 128), lambda j: (0, j)), pl.BlockSpec((1, 128), lambda j: (0, j))],
        out_shape=[jax.ShapeDtypeStruct((S, CONV_CH), _MXU), jax.ShapeDtypeStruct((CONV_K, CONV_CH), F32),
                   jax.ShapeDtypeStruct((1, CONV_CH), F32)],
        compiler_params=_cp(("parallel",)))(proj, conv_w, conv_b, dxa)


def _ssd_consts():
    L = SSD_L
    r = lax.broadcasted_iota(jnp.int32, (L, L), 0)
    c = lax.broadcasted_iota(jnp.int32, (L, L), 1)
    causal = r >= c
    upper = (r <= c).astype(F32)
    hr = lax.broadcasted_iota(jnp.int32, (SSD_HEADS, SSD_WIDTH), 0)
    hc = lax.broadcasted_iota(jnp.int32, (SSD_HEADS, SSD_WIDTH), 1)
    expand = (lax.shift_right_logical(hc, 6) == hr).astype(F32)
    return causal, causal.astype(F32), upper, expand


def _softplus(x):
    return jnp.maximum(x, 0.0) + jnp.log(1.0 + jnp.exp(-jnp.abs(x)))


def _ssd_scalars(dtr, dt_bias, a_log, tri, upper, expand):
    dt = _softplus(dtr + dt_bias)
    A = -jnp.exp(a_log)
    adt = dt * A
    acum = _dot(tri, adt, "nn", hi=True)
    acum_t = _dot(adt, upper, "tn", hi=True)
    alast = acum[SSD_L - 1:SSD_L, :]
    e = jnp.exp(acum)
    wdec = jnp.exp(alast - acum)
    gam = jnp.exp(alast)
    ex = lambda t: _dot(t, expand, "nn", hi=True)
    gam8 = jnp.broadcast_to(gam, (8, SSD_HEADS))
    return dt, A, acum, acum_t, e, wdec, gam, ex(dt), ex(e), ex(wdec), ex(gam8)[0:1, :]


def _ssd_fwd(proj, proj_small, xa, dt_bias, a_log, d_skip, norm_w):
    S = proj.shape[0]
    L, N, W = SSD_L, SSD_N, SSD_WIDTH
    nc = S // L

    def body(z_ref, xa_ref, dtr_ref, dtb_ref, al_ref, dsk_ref, nw_ref, yo_ref, y_ref, rs_ref, hs_ref, h_scr, y_scr):
        @pl.when(pl.program_id(0) == 0)
        def _():
            h_scr[...] = jnp.zeros_like(h_scr)

        causal, tri, upper, expand = _ssd_consts()
        dt, A, acum, acum_t, e, wdec, gam, dtE, eE, wE, gamE = _ssd_scalars(dtr_ref[:, 0:SSD_HEADS], dtb_ref[...], al_ref[...], tri, upper, expand)
        xs = xa_ref[:, 0:W]
        X = xs * dtE
        XW = X * wE
        hs_ref[0] = h_scr[...]
        for g in range(SSD_G):
            gs = slice(g * 512, (g + 1) * 512)
            Bg = xa_ref[:, W + g * N:W + (g + 1) * N]
            Cg = xa_ref[:, W + SSD_G * N + g * N:W + SSD_G * N + (g + 1) * N]
            Hg = h_scr[:, gs]
            CB = _dot(Cg, Bg, "nt")
            yoff = _dot(Cg, Hg, "nn") * eE[:, gs]
            st = _dot(Bg, XW[:, gs], "tn")
            for j in range(8):
                h = g * 8 + j
                hsl = slice(h * SSD_P, (h + 1) * SSD_P)
                lam = jnp.exp(jnp.where(causal, acum[:, h:h + 1] - acum_t[h:h + 1, :], -jnp.inf))
                y_scr[:, hsl] = _dot(CB * lam, X[:, hsl], "nn") + yoff[:, j * SSD_P:(j + 1) * SSD_P]
            h_scr[:, gs] = gamE[:, gs] * Hg + st
        dskE = _dot(jnp.broadcast_to(dsk_ref[...], (8, SSD_HEADS)), expand, "nn", hi=True)[0:1, :]
        y = y_scr[...] + dskE * xs
        y_ref[...] = y
        zv = z_ref[...]
        yg = y * (zv * _sigmoid(zv))
        rs = lax.rsqrt(jnp.mean(yg * yg, axis=-1, keepdims=True) + EPS)
        rs_ref[...] = rs
        yo_ref[...] = ((yg * rs) * nw_ref[...]).astype(yo_ref.dtype)

    p16 = pl.BlockSpec((1, SSD_HEADS), lambda c: (0, 0))
    return pl.pallas_call(
        body, name="ssd_fwd", grid=(nc,),
        in_specs=[pl.BlockSpec((L, W), lambda c: (c, 0)), pl.BlockSpec((L, CONV_CH), lambda c: (c, 0)),
                  pl.BlockSpec((L, W_SMALL), lambda c: (c, 0)), p16, p16, p16, pl.BlockSpec((1, W), lambda c: (0, 0))],
        out_specs=[pl.BlockSpec((L, W), lambda c: (c, 0)), pl.BlockSpec((L, W), lambda c: (c, 0)),
                   pl.BlockSpec((L, 1), lambda c: (c, 0)), pl.BlockSpec((1, N, W), lambda c: (c, 0, 0))],
        out_shape=[jax.ShapeDtypeStruct((S, W), _MXU), jax.ShapeDtypeStruct((S, W), F32), jax.ShapeDtypeStruct((S, 1), F32),
                   jax.ShapeDtypeStruct((nc, N, W), F32)],
        scratch_shapes=[pltpu.VMEM((N, W), F32), pltpu.VMEM((L, W), F32)],
        compiler_params=_cp(("arbitrary",)))(proj, xa, proj_small, dt_bias, a_log, d_skip, norm_w)


def _ssd_bwd(dmixed, proj, proj_small, xa, y, rs2, hs, dt_bias, a_log, d_skip, norm_w):
    S = proj.shape[0]
    L, N, W, H = SSD_L, SSD_N, SSD_WIDTH, SSD_HEADS
    nc = S // L

    def body(dyo_ref, z_ref, xa_ref, dtr_ref, y_ref, rs_ref, hs_ref, dtb_ref, al_ref, dsk_ref, nw_ref,
             dz_ref, dxa_ref, ddtr_ref, ddtb_ref, dal_ref, ddsk_ref, dnw_ref, dh_scr, dx_scr):
        @pl.when(pl.program_id(0) == 0)
        def _():
            dh_scr[...] = jnp.zeros_like(dh_scr)
            ddtb_ref[...] = jnp.zeros_like(ddtb_ref)
            dal_ref[...] = jnp.zeros_like(dal_ref)
            ddsk_ref[...] = jnp.zeros_like(ddsk_ref)
            dnw_ref[...] = jnp.zeros_like(dnw_ref)

        causal, tri, upper, expand = _ssd_consts()
        heads = lambda t: _dot(t, expand, "nt", hi=True)
        onehot = lambda h: (lax.broadcasted_iota(jnp.int32, (1, H), 1) == h).astype(F32)

        zv, yv, rs = z_ref[...], y_ref[...], rs_ref[...]
        sz = _sigmoid(zv)
        zs = zv * sz
        xhat = (yv * zs) * rs
        dyo = dyo_ref[...].astype(F32)
        dnw_ref[...] += jnp.sum(dyo * xhat, axis=0, keepdims=True)
        dxhat = dyo * nw_ref[...]
        dyg = rs * (dxhat - xhat * jnp.mean(dxhat * xhat, axis=-1, keepdims=True))
        dz_ref[...] = (dyg * yv * (sz * (1.0 + zv * (1.0 - sz)))).astype(dz_ref.dtype)
        dy = dyg * zs

        dtr = dtr_ref[:, 0:H]
        dt, A, acum, acum_t, e, wdec, gam, dtE, eE, wE, gamE = _ssd_scalars(dtr, dtb_ref[...], al_ref[...], tri, upper, expand)
        xs = xa_ref[:, 0:W]
        X = xs * dtE
        XW = X * wE
        dskE = _dot(jnp.broadcast_to(dsk_ref[...], (8, H)), expand, "nn", hi=True)[0:1, :]
        ddsk_ref[...] += heads(jnp.broadcast_to(jnp.sum(dy * xs, axis=0, keepdims=True), (8, W)))[0:1, :]

        dYe = dy * eE
        dacum = jnp.zeros((L, H), F32)
        de_full = []
        dw_full = []
        dgam_full = []
        for g in range(SSD_G):
            gs = slice(g * 512, (g + 1) * 512)
            Bg = xa_ref[:, W + g * N:W + (g + 1) * N]
            Cg = xa_ref[:, W + SSD_G * N + g * N:W + SSD_G * N + (g + 1) * N]
            Hg = hs_ref[0, :, gs]
            dHn = dh_scr[:, gs]
            CH = _dot(Cg, Hg, "nn")
            de_full.append(dy[:, gs] * CH)
            dC = _dot(dYe[:, gs], Hg, "nt")
            dHs = gamE[:, gs] * dHn + _dot(Cg, dYe[:, gs], "tn")
            dgam_full.append(jnp.sum(dHn * Hg, axis=0, keepdims=True))
            BdS = _dot(Bg, dHn, "nn")
            dB = _dot(XW[:, gs], dHn, "nt")
            dx_scr[:, gs] = BdS * wE[:, gs]
            dw_full.append(BdS * X[:, gs])
            CB = _dot(Cg, Bg, "nt")
            dCB = jnp.zeros((L, L), F32)
            for j in range(8):
                h = g * 8 + j
                hsl = slice(h * SSD_P, (h + 1) * SSD_P)
                lam = jnp.exp(jnp.where(causal, acum[:, h:h + 1] - acum_t[h:h + 1, :], -jnp.inf))
                M = CB * lam
                dM = _dot(dy[:, hsl], X[:, hsl], "nt")
                dx_scr[:, hsl] += _dot(M, dy[:, hsl], "tn")
                dCB = dCB + dM * lam
                Q = dM * M
                rowsum = jnp.sum(Q, axis=1, keepdims=True)
                colsum = _dot(Q, jnp.ones((L, 8), F32), "tn", hi=True)[:, 0:1]
                dacum = dacum + (rowsum - colsum) * onehot(h)
            dC = dC + _dot(dCB, Bg, "nn")
            dB = dB + _dot(dCB, Cg, "tn")
            dxa_ref[:, W + g * N:W + (g + 1) * N] = dB
            dxa_ref[:, W + SSD_G * N + g * N:W + SSD_G * N + (g + 1) * N] = dC
            dh_scr[:, gs] = dHs

        de16 = heads(jnp.concatenate(de_full, axis=1))
        dw16 = heads(jnp.concatenate(dw_full, axis=1))
        dgam16 = heads(jnp.broadcast_to(jnp.concatenate(dgam_full, axis=1), (8, W)))[0:1, :]
        dacum = dacum + de16 * e - dw16 * wdec
        dlast = jnp.sum(dw16 * wdec, axis=0, keepdims=True) + dgam16 * gam
        lastrow = (lax.broadcasted_iota(jnp.int32, (L, 1), 0) == L - 1).astype(F32)
        dacum = dacum + lastrow * dlast
        da = _dot(tri, dacum, "tn", hi=True)
        dX = dx_scr[...]
        ddt = da * A + heads(dX * xs)
        dA = jnp.sum(da * dt, axis=0, keepdims=True)
        dal_ref[...] += dA * A
        ddtr = ddt * _sigmoid(dtr + dtb_ref[...])
        ddtb_ref[...] += jnp.sum(ddtr, axis=0, keepdims=True)
        ddtr_ref[...] = ddtr
        dxa_ref[:, 0:W] = dX * dtE + dy * dskE

    p16 = pl.BlockSpec((1, H), lambda c: (0, 0))
    rev = lambda c: (nc - 1 - c, 0)
    return pl.pallas_call(
        body, name="ssd_bwd", grid=(nc,),
        in_specs=[pl.BlockSpec((L, W), rev), pl.BlockSpec((L, W), rev), pl.BlockSpec((L, CONV_CH), rev),
                  pl.BlockSpec((L, W_SMALL), rev), pl.BlockSpec((L, W), rev), pl.BlockSpec((L, 1), rev),
                  pl.BlockSpec((1, N, W), lambda c: (nc - 1 - c, 0, 0)), p16, p16, p16, pl.BlockSpec((1, W), lambda c: (0, 0))],
        out_specs=[pl.BlockSpec((L, W), rev), pl.BlockSpec((L, CONV_CH), rev), pl.BlockSpec((L, H), rev),
                   p16, p16, p16, pl.BlockSpec((1, W), lambda c: (0, 0))],
        out_shape=[jax.ShapeDtypeStruct((S, W), _MXU), jax.ShapeDtypeStruct((S, CONV_CH), F32), jax.ShapeDtypeStruct((S, H), F32),
                   jax.ShapeDtypeStruct((1, H), F32), jax.ShapeDtypeStruct((1, H), F32), jax.ShapeDtypeStruct((1, H), F32),
                   jax.ShapeDtypeStruct((1, W), F32)],
        scratch_shapes=[pltpu.VMEM((N, W), F32), pltpu.VMEM((L, W), F32)],
        compiler_params=_cp(("arbitrary",)))(dmixed, proj, xa, proj_small, y, rs2, hs, dt_bias, a_log, d_skip, norm_w)


def _rope_tables(S):
    inv = 1.0 / (ROPE_THETA ** (jnp.arange(0, ROPE_DIM, 2, dtype=F32) / ROPE_DIM))
    ang = jnp.arange(S, dtype=F32)[:, None] * inv[None, :]
    cos, sin = jnp.cos(ang), jnp.sin(ang)
    half = ROPE_DIM // 2
    c64 = jnp.concatenate([cos, cos, jnp.ones((S, HD - ROPE_DIM), F32)], axis=1)
    s64 = jnp.concatenate([sin, sin, jnp.zeros((S, HD - ROPE_DIM), F32)], axis=1)
    del half
    return jnp.concatenate([c64, c64], axis=1), jnp.concatenate([s64, s64], axis=1)


def _rope(xs, blk0, width, cos, sin, sign, out_dtype, name, extra=None):
    S = xs[0].shape[0]
    tr = _pick(S, (512, 256, 128))
    nx = len(xs)

    def body(*refs):
        x_refs, c_ref, s_ref = refs[:nx], refs[nx], refs[nx + 1]
        e_ref = refs[nx + 2] if extra is not None else None
        o_ref = refs[-1]
        cv, sv = c_ref[...], s_ref[...] * sign
        lane = lax.broadcasted_iota(jnp.int32, (tr, 128), 1)
        first = (lane & (HD - 1)) < (ROPE_DIM // 2)
        for j in range(2):
            cs = slice(j * 128, (j + 1) * 128)
            xv = x_refs[0][:, cs].astype(F32)
            for r in x_refs[1:]:
                xv = xv + r[:, cs].astype(F32)
            rot = jnp.where(first, -pltpu.roll(xv, 128 - ROPE_DIM // 2, axis=1), pltpu.roll(xv, ROPE_DIM // 2, axis=1))
            out = xv * cv + rot * sv
            if extra is not None:
                out = out + e_ref[:, cs].astype(F32)
            o_ref[:, cs] = out.astype(out_dtype)

    t128 = pl.BlockSpec((tr, 128), lambda i, j: (i, 0))
    oblk = pl.BlockSpec((tr, 256), lambda i, j: (i, j))
    specs = [pl.BlockSpec((tr, 256), lambda i, j: (i, blk0 + j))] * nx + [t128, t128]
    ins = list(xs) + [cos, sin]
    if extra is not None:
        ins.append(extra[0])
        eb = extra[1]
        specs.append(pl.BlockSpec((tr, 256), lambda i, j: (i, eb + j)))
    return pl.pallas_call(
        body, name=name, grid=(S // tr, width // 256), in_specs=specs, out_specs=oblk,
        out_shape=jax.ShapeDtypeStruct((S, width), out_dtype), compiler_params=_cp(("parallel", "parallel")))(*ins)


def _compress_fwd(R, pe, w1, w2):
    NC = R.shape[1]
    half = 16 * HD

    def body(r_ref, pe_ref, w1_ref, w2_ref, o_ref, hid_ref):
        r = r_ref[0]
        a = _dot(r + pe_ref[:, 0:half], w1_ref[0:half, :], "nn")
        b = _dot(r + pe_ref[:, half:2 * half], w1_ref[half:2 * half, :], "nn")
        hid = a + pltpu.roll(b, NC - 1, axis=0)
        hid_ref[0] = hid
        out = _dot(hid * _sigmoid(hid), w2_ref[...], "nn")
        rows = lax.broadcasted_iota(jnp.int32, out.shape, 0)
        o_ref[0] = jnp.where(rows < NC - 1, out, 0.0).astype(o_ref.dtype)

    return pl.pallas_call(
        body, name="compress_fwd", grid=(N_KV,),
        in_specs=[pl.BlockSpec((1, NC, half), lambda h: (h, 0, 0)), pl.BlockSpec((1, 2 * half), lambda h: (0, 0)),
                  pl.BlockSpec((2 * half, CMP_HID), lambda h: (0, 0)), pl.BlockSpec((CMP_HID, HD), lambda h: (0, 0))],
        out_specs=[pl.BlockSpec((1, NC, HD), lambda h: (h, 0, 0)), pl.BlockSpec((1, NC, CMP_HID), lambda h: (h, 0, 0))],
        out_shape=[jax.ShapeDtypeStruct((N_KV, NC, HD), _MXU), jax.ShapeDtypeStruct((N_KV, NC, CMP_HID), F32)],
        compiler_params=_cp(("parallel",)))(R, pe, w1, w2)


def _compress_bwd(R, pe, w1, w2, hid, dout):
    NC = R.shape[1]
    half = 16 * HD

    def body(r_ref, pe_ref, w1_ref, w2_ref, hid_ref, do_ref, dr_ref, dw1_ref, dw2_ref, dpe_ref):
        @pl.when(pl.program_id(0) == 0)
        def _():
            dw1_ref[...] = jnp.zeros_like(dw1_ref)
            dw2_ref[...] = jnp.zeros_like(dw2_ref)
            dpe_ref[...] = jnp.zeros_like(dpe_ref)

        r, hv, do = r_ref[0], hid_ref[0], do_ref[0]
        s = _sigmoid(hv)
        dw2_ref[...] += _dot(hv * s, do, "tn")
        dhid = _dot(do, w2_ref[...], "nt") * (s * (1.0 + hv * (1.0 - s)))
        rows = lax.broadcasted_iota(jnp.int32, dhid.shape, 0)
        dhid = jnp.where(rows < NC - 1, dhid, 0.0)
        dhid_dn = pltpu.roll(dhid, 1, axis=0)
        dw1_ref[0:half, :] += _dot(r + pe_ref[:, 0:half], dhid, "tn")
        dw1_ref[half:2 * half, :] += _dot(r + pe_ref[:, half:2 * half], dhid_dn, "tn")
        dxt = _dot(dhid, w1_ref[0:half, :], "nt")
        dxb = _dot(dhid_dn, w1_ref[half:2 * half, :], "nt")
        dr_ref[0] = dxt + dxb
        dpe_ref[:, 0:half] += jnp.sum(dxt, axis=0, keepdims=True)
        dpe_ref[:, half:2 * half] += jnp.sum(dxb, axis=0, keepdims=True)

    return pl.pallas_call(
        body, name="compress_bwd", grid=(N_KV,),
        in_specs=[pl.BlockSpec((1, NC, half), lambda h: (h, 0, 0)), pl.BlockSpec((1, 2 * half), lambda h: (0, 0)),
                  pl.BlockSpec((2 * half, CMP_HID), lambda h: (0, 0)), pl.BlockSpec((CMP_HID, HD), lambda h: (0, 0)),
                  pl.BlockSpec((1, NC, CMP_HID), lambda h: (h, 0, 0)), pl.BlockSpec((1, NC, HD), lambda h: (h, 0, 0))],
        out_specs=[pl.BlockSpec((1, NC, half), lambda h: (h, 0, 0)), pl.BlockSpec((2 * half, CMP_HID), lambda h: (0, 0)),
                   pl.BlockSpec((CMP_HID, HD), lambda h: (0, 0)), pl.BlockSpec((1, 2 * half), lambda h: (0, 0))],
        out_shape=[jax.ShapeDtypeStruct((N_KV, NC, half), F32), jax.ShapeDtypeStruct((2 * half, CMP_HID), F32),
                   jax.ShapeDtypeStruct((CMP_HID, HD), F32), jax.ShapeDtypeStruct((1, 2 * half), F32)],
        compiler_params=_cp(("arbitrary",)))(R, pe, w1, w2, hid, dout)


def _attn_cfg(S, Sk, mode):
    tq = _pick(S, (256, 128))
    tk = Sk if mode == "cmp" else _pick(Sk, (256, 128))
    return tq, tk


def _kb_range(mode, q0, tq, tk):
    if mode == "cmp":
        return 0, 1
    hi = (q0 + tq - 1) // tk + 1
    if mode == "sel":
        return 0, hi
    return jnp.maximum(q0 - (WINDOW - 1), 0) // tk, hi


def _attn_bias(mode, q0, k0, tq, tk, sel_t):
    k = k0 + lax.broadcasted_iota(jnp.int32, (tk, tq), 0)
    t = q0 + lax.broadcasted_iota(jnp.int32, (tk, tq), 1)
    if mode == "cmp":
        ok = (k * 16 + 31) <= t
    elif mode == "win":
        ok = (k <= t) & ((t - k) < WINDOW)
    else:
        nb = sel_t.shape[0]
        ek = k0 + lax.broadcasted_iota(jnp.int32, (tk, nb), 0)
        eb = lax.broadcasted_iota(jnp.int32, (tk, nb), 1)
        expand = (lax.shift_right_logical(ek, 6) == eb).astype(_MXU)
        chosen = _dot(expand, sel_t, "nn") > 0.5
        ok = (k <= t) & chosen
    bias = jnp.where(ok, 0.0, NEG)
    return jnp.concatenate([bias] * GRP, axis=1), jnp.concatenate([ok.astype(F32)] * GRP, axis=1)


def _stack_heads(ref, tq):
    return jnp.concatenate([ref[:, g * HD:(g + 1) * HD] for g in range(GRP)], axis=0)


def _scaled_queries(q_ref, tq):
    return (_stack_heads(q_ref, tq).astype(F32) * SCALE).astype(_MXU)


def _blocked_t(x, tk):
    n, Sk, d = x.shape
    return x.reshape(n, Sk // tk, tk, d).transpose(0, 1, 3, 2)


def _attn_fwd(q, qcol0, k, v, mode, sel_t, name):
    S, Sk = q.shape[0], k.shape[1]
    tq, tk = _attn_cfg(S, Sk, mode)
    R = GRP * tq
    vt = _blocked_t(v, tk)

    def body(*refs):
        if mode == "sel":
            q_ref, k_ref, vt_ref, sel_ref, o_ref, lse_ref, m_scr, l_scr, acc = refs
        else:
            q_ref, k_ref, vt_ref, o_ref, lse_ref, m_scr, l_scr, acc = refs
        q0 = pl.program_id(1) * tq
        qs = _scaled_queries(q_ref, tq)
        m_scr[...] = jnp.full_like(m_scr, NEG)
        l_scr[...] = jnp.zeros_like(l_scr)
        acc[...] = jnp.zeros_like(acc)
        selv = sel_ref[0].astype(_MXU) if mode == "sel" else None

        def step(kb, carry):
            k0 = pl.multiple_of(kb * tk, tk)
            bias, okf = _attn_bias(mode, q0, k0, tq, tk, selv)
            s = _dot(k_ref[0, pl.ds(k0, tk), :], qs, "nt") + bias
            m_old = m_scr[...]
            m_new = jnp.maximum(m_old, jnp.max(s, axis=0, keepdims=True))
            p = jnp.exp(s - m_new)
            if mode == "cmp":
                p = p * okf
            alpha = jnp.exp(m_old - m_new)
            l_scr[...] = alpha * l_scr[...] + jnp.sum(p, axis=0, keepdims=True)
            acc[...] = alpha * acc[...] + _dot(vt_ref[0, kb], p, "nn")
            m_scr[...] = m_new
            return carry

        lo, hi = _kb_range(mode, q0, tq, tk)
        lax.fori_loop(lo, hi, step, 0)
        l = l_scr[...]
        good = l > 0.0
        o_t = acc[...] * jnp.where(good, 1.0 / jnp.where(good, l, 1.0), 0.0)
        lse = jnp.where(good, m_scr[...] + jnp.log(jnp.where(good, l, 1.0)), -NEG)
        for g in range(GRP):
            o_ref[:, g * HD:(g + 1) * HD] = o_t[:, g * tq:(g + 1) * tq].T
            lse_ref[0, g:g + 1, :] = lse[:, g * tq:(g + 1) * tq]

    ins = [q, k, vt]
    specs = [pl.BlockSpec((tq, GRP * HD), lambda h, i: (i, qcol0 + h)), pl.BlockSpec((1, Sk, HD), lambda h, i: (h, 0, 0)),
             pl.BlockSpec((1, Sk // tk, HD, tk), lambda h, i: (h, 0, 0, 0))]
    if mode == "sel":
        ins.append(sel_t)
        specs.append(pl.BlockSpec((1, sel_t.shape[1], tq), lambda h, i: (h, 0, i)))
    return pl.pallas_call(
        body, name=name, grid=(N_KV, S // tq), in_specs=specs,
        out_specs=[pl.BlockSpec((tq, GRP * HD), lambda h, i: (i, h)), pl.BlockSpec((1, GRP, tq), lambda h, i: (h, 0, i))],
        out_shape=[jax.ShapeDtypeStruct((S, ATT_WIDTH), F32), jax.ShapeDtypeStruct((N_KV, GRP, S), F32)],
        scratch_shapes=[pltpu.VMEM((1, R), F32), pltpu.VMEM((1, R), F32), pltpu.VMEM((HD, R), F32)],
        compiler_params=_cp(("parallel", "arbitrary")))(*ins)


def _attn_bwd(q, qcol0, k, v, o, lse, do, mode, sel_t, name):
    S, Sk = q.shape[0], k.shape[1]
    tq, tk = _attn_cfg(S, Sk, mode)
    R = GRP * tq
    kt = _blocked_t(k, tk)

    def body(*refs):
        if mode == "sel":
            q_ref, k_ref, kt_ref, v_ref, o_ref, lse_ref, do_ref, sel_ref, dq_ref, dk_ref, dv_ref, dq_scr = refs
        else:
            q_ref, k_ref, kt_ref, v_ref, o_ref, lse_ref, do_ref, dq_ref, dk_ref, dv_ref, dq_scr = refs

        @pl.when(pl.program_id(1) == 0)
        def _():
            dk_ref[...] = jnp.zeros_like(dk_ref)
            dv_ref[...] = jnp.zeros_like(dv_ref)

        q0 = pl.program_id(1) * tq
        qs = _scaled_queries(q_ref, tq)
        dos = _stack_heads(do_ref, tq)
        delta = _dot(jnp.ones((8, HD), F32), dos * _stack_heads(o_ref, tq), "nt", hi=True)[0:1, :]
        lsev = jnp.concatenate([lse_ref[0, g:g + 1, :] for g in range(GRP)], axis=1)
        dos = dos.astype(_MXU)
        dq_scr[...] = jnp.zeros_like(dq_scr)
        selv = sel_ref[0].astype(_MXU) if mode == "sel" else None

        def step(kb, carry):
            k0 = pl.multiple_of(kb * tk, tk)
            kv = k_ref[0, pl.ds(k0, tk), :]
            bias, okf = _attn_bias(mode, q0, k0, tq, tk, selv)
            p = jnp.exp(_dot(kv, qs, "nt") + bias - lsev)
            if mode == "cmp":
                p = p * okf
            dp = _dot(v_ref[0, pl.ds(k0, tk), :], dos, "nt")
            ds = p * (dp - delta)
            dq_scr[...] += _dot(kt_ref[0, kb], ds, "nn")
            dk_ref[0, pl.ds(k0, tk), :] += _dot(ds, qs, "nn")
            dv_ref[0, pl.ds(k0, tk), :] += _dot(p, dos, "nn")
            return carry

        lo, hi = _kb_range(mode, q0, tq, tk)
        lax.fori_loop(lo, hi, step, 0)
        for g in range(GRP):
            dq_ref[:, g * HD:(g + 1) * HD] = (dq_scr[:, g * tq:(g + 1) * tq] * SCALE).T

    kv_spec = pl.BlockSpec((1, Sk, HD), lambda h, i: (h, 0, 0))
    qo_spec = pl.BlockSpec((tq, GRP * HD), lambda h, i: (i, h))
    ins = [q, k, kt, v, o, lse, do]
    specs = [pl.BlockSpec((tq, GRP * HD), lambda h, i: (i, qcol0 + h)), kv_spec,
             pl.BlockSpec((1, Sk // tk, HD, tk), lambda h, i: (h, 0, 0, 0)), kv_spec, qo_spec,
             pl.BlockSpec((1, GRP, tq), lambda h, i: (h, 0, i)), qo_spec]
    if mode == "sel":
        ins.append(sel_t)
        specs.append(pl.BlockSpec((1, sel_t.shape[1], tq), lambda h, i: (h, 0, i)))
    return pl.pallas_call(
        body, name=name, grid=(N_KV, S // tq), in_specs=specs, out_specs=[qo_spec, kv_spec, kv_spec],
        out_shape=[jax.ShapeDtypeStruct((S, ATT_WIDTH), F32), jax.ShapeDtypeStruct((N_KV, Sk, HD), F32),
                   jax.ShapeDtypeStruct((N_KV, Sk, HD), F32)],
        scratch_shapes=[pltpu.VMEM((HD, R), F32)],
        compiler_params=_cp(("parallel", "arbitrary")))(*ins)


def _select(q, qcol0, k_cmp, lse):
    S, NC = q.shape[0], k_cmp.shape[1]
    NB = S // SEL_BLOCK
    tq = _pick(S, (256, 128))
    ci = np.arange(NC)[None, :] * 16
    sj = np.arange(NB)[:, None] * SEL_BLOCK
    ov_t = np.clip(np.minimum(ci + 32, sj + SEL_BLOCK) - np.maximum(ci, sj), 0, None) / 32.0
    ov_t[:, NC - 1] = 0.0
    ov_t = jnp.asarray(ov_t, F32)

    def body(q_ref, k_ref, lse_ref, ov_ref, sel_ref):
        q0 = pl.program_id(1) * tq
        bias, okf = _attn_bias("cmp", q0, 0, tq, NC, None)
        lsev = jnp.concatenate([lse_ref[0, g:g + 1, :] for g in range(GRP)], axis=1)
        p = jnp.exp(_dot(k_ref[0], _scaled_queries(q_ref, tq), "nt") + bias - lsev) * okf
        imp4 = _dot(ov_ref[...], p, "nn")
        imp = imp4[:, 0:tq] + imp4[:, tq:2 * tq] + imp4[:, 2 * tq:3 * tq] + imp4[:, 3 * tq:4 * tq]
        blk = lax.broadcasted_iota(jnp.int32, (NB, tq), 0)
        cur = lax.shift_right_logical(q0 + lax.broadcasted_iota(jnp.int32, (NB, tq), 1), 6)
        imp = jnp.where((blk == 0) | (blk == cur) | (blk == cur - 1), FORCE, imp)
        imp = jnp.where(blk <= cur, imp, -1.0)
        rank = jnp.zeros((NB, tq), F32)
        for j in range(NB):
            row = imp[j:j + 1, :]
            ahead = (row > imp) | ((row == imp) & (blk > j))
            rank = rank + ahead.astype(F32)
        sel_ref[0] = ((rank < float(N_SELECT)) & (imp >= 0.0)).astype(F32)

    return pl.pallas_call(
        body, name="select_blocks", grid=(N_KV, S // tq),
        in_specs=[pl.BlockSpec((tq, GRP * HD), lambda h, i: (i, qcol0 + h)), pl.BlockSpec((1, NC, HD), lambda h, i: (h, 0, 0)),
                  pl.BlockSpec((1, GRP, tq), lambda h, i: (h, 0, i)), pl.BlockSpec((NB, NC), lambda h, i: (0, 0))],
        out_specs=pl.BlockSpec((1, NB, tq), lambda h, i: (h, 0, i)),
        out_shape=jax.ShapeDtypeStruct((N_KV, NB, S), F32), compiler_params=_cp(("parallel", "parallel")))(q, k_cmp, lse, ov_t)


GATE_COL0 = SSD_HEADS


def _combine_fwd(o_cmp, o_sel, o_win, proj_small):
    S = o_cmp.shape[0]
    tr = _pick(S, (256, 128))

    def body(oc_ref, os_ref, ow_ref, g_ref, y_ref):
        gate = _sigmoid(g_ref[...])
        for h in range(N_HEADS):
            hs = slice(h * HD, (h + 1) * HD)
            c = GATE_COL0 + 3 * h
            y = gate[:, c:c + 1] * oc_ref[:, hs] + gate[:, c + 1:c + 2] * os_ref[:, hs] + gate[:, c + 2:c + 3] * ow_ref[:, hs]
            y_ref[:, hs] = y.astype(y_ref.dtype)

    row = pl.BlockSpec((tr, ATT_WIDTH), lambda i: (i, 0))
    return pl.pallas_call(
        body, name="combine_fwd", grid=(S // tr,), in_specs=[row, row, row, pl.BlockSpec((tr, W_SMALL), lambda i: (i, 0))],
        out_specs=row, out_shape=jax.ShapeDtypeStruct((S, ATT_WIDTH), _MXU), compiler_params=_cp(("parallel",)))(
            o_cmp, o_sel, o_win, proj_small)


def _combine_bwd(dmixed, o_cmp, o_sel, o_win, proj_small):
    S = o_cmp.shape[0]
    tr = _pick(S, (256, 128))

    def body(dy_ref, oc_ref, os_ref, ow_ref, g_ref, dc_ref, ds_ref, dw_ref, dg_ref):
        gate = _sigmoid(g_ref[...])
        lane = lax.broadcasted_iota(jnp.int32, (1, W_SMALL), 1)
        dg = jnp.zeros((tr, W_SMALL), F32)
        for h in range(N_HEADS):
            hs = slice(h * HD, (h + 1) * HD)
            dy = dy_ref[:, hs].astype(F32)
            for b, (o_ref, d_ref) in enumerate(((oc_ref, dc_ref), (os_ref, ds_ref), (ow_ref, dw_ref))):
                c = GATE_COL0 + 3 * h + b
                gv = gate[:, c:c + 1]
                d_ref[:, hs] = gv * dy
                dgate = jnp.sum(dy * o_ref[:, hs], axis=-1, keepdims=True) * (gv * (1.0 - gv))
                dg = dg + dgate * (lane == c).astype(F32)
        dg_ref[...] = dg

    row = pl.BlockSpec((tr, ATT_WIDTH), lambda i: (i, 0))
    small = pl.BlockSpec((tr, W_SMALL), lambda i: (i, 0))
    return pl.pallas_call(
        body, name="combine_bwd", grid=(S // tr,),
        in_specs=[pl.BlockSpec((tr, ATT_WIDTH), lambda i: (i, 1)), row, row, row, small], out_specs=[row, row, row, small],
        out_shape=[jax.ShapeDtypeStruct((S, ATT_WIDTH), F32)] * 3 + [jax.ShapeDtypeStruct((S, W_SMALL), F32)],
        compiler_params=_cp(("parallel",)))(dmixed, o_cmp, o_sel, o_win, proj_small)


def _heads_major(x):
    S = x.shape[0]
    return x.reshape(S, N_KV, HD).transpose(1, 0, 2)


def _tokens_major(x):
    return x.transpose(1, 0, 2).reshape(x.shape[1], N_KV * HD)


def _to_rows16(x):
    S = x.shape[0]
    return x.reshape(S // 16, 16, N_KV, HD).transpose(2, 0, 1, 3).reshape(N_KV, S // 16, 16 * HD)


def _from_rows16(r):
    NC = r.shape[1]
    return r.reshape(N_KV, NC, 16, HD).transpose(1, 2, 0, 3).reshape(NC * 16, N_KV * HD)


DT_COL0 = SSD_WIDTH + CONV_CH
GATE_IN_COL0 = D_IN - 3 * N_HEADS


def _split_w_in(w):
    main = jnp.concatenate([w[:, :DT_COL0], w[:, DT_COL0 + SSD_HEADS:GATE_IN_COL0]], axis=1)
    small = jnp.concatenate([w[:, DT_COL0:DT_COL0 + SSD_HEADS], w[:, GATE_IN_COL0:],
                             jnp.zeros((w.shape[0], W_SMALL - SSD_HEADS - 3 * N_HEADS), w.dtype)], axis=1)
    return main, small


def _merge_w_in(main, small):
    return jnp.concatenate([main[:, :DT_COL0], small[:, :SSD_HEADS].astype(main.dtype), main[:, DT_COL0:],
                            small[:, SSD_HEADS:SSD_HEADS + 3 * N_HEADS].astype(main.dtype)], axis=1)


QB, KCB, VCB, KSB, VSB, KWB, VWB = 10, 14, 15, 16, 17, 18, 19


def _col256(a, b):
    return a[:, b * 256:(b + 1) * 256]


_EARLY = ["w_in", "cmp_w1_k", "cmp_w1_v"]
_LATE = ["w_out", "w_gate", "w_up", "w_down"]
_FFN = ["w_down", "w_gate", "w_up"]
_MID = ["w_out"]
_LAST = ["cmp_w1_k", "cmp_w1_v", "w_in"]


def _local_step(x, tgt, p, late_weights=None, grads_ready=None):
    S = x.shape[0]
    cos, sin = _rope_tables(S)

    u, rs1 = _rms_fwd(x, p["attn_norm_w"], "attn_norm")
    proj = _mm(u, p["w_main"], "nn", F32, "in_proj")
    proj_small = _mm(u, p["w_small"], "nn", F32, "in_proj_small")
    xa = _conv_fwd(proj, p["conv_w"], p["conv_b"])
    y_ssd, y_pre, rs_ssd, hs = _ssd_fwd(proj, proj_small, xa, p["dt_bias"], p["a_log"], p["d_skip"], p["ssd_norm_w"])

    q_rot = _rope([proj], QB, ATT_WIDTH, cos, sin, 1.0, _MXU, "rope_q")
    ks_rot = _heads_major(_rope([proj], KSB, 256, cos, sin, 1.0, _MXU, "rope_ks"))
    kw_rot = _heads_major(_rope([proj], KWB, 256, cos, sin, 1.0, _MXU, "rope_kw"))
    vs = _heads_major(_col256(proj, VSB).astype(_MXU))
    vw = _heads_major(_col256(proj, VWB).astype(_MXU))
    rk, rv = _to_rows16(_col256(proj, KCB)), _to_rows16(_col256(proj, VCB))
    k_cmp, hid_k = _compress_fwd(rk, p["cmp_pe_k"], p["cmp_w1_k"], p["cmp_w2_k"])
    v_cmp, hid_v = _compress_fwd(rv, p["cmp_pe_v"], p["cmp_w1_v"], p["cmp_w2_v"])

    o_cmp, lse_cmp = _attn_fwd(proj, QB, k_cmp, v_cmp, "cmp", None, "attn_cmp_fwd")
    sel = _select(proj, QB, k_cmp, lse_cmp)
    o_sel, lse_sel = _attn_fwd(q_rot, 0, ks_rot, vs, "sel", sel, "attn_sel_fwd")
    o_win, lse_win = _attn_fwd(q_rot, 0, kw_rot, vw, "win", None, "attn_win_fwd")
    y_att = _combine_fwd(o_cmp, o_sel, o_win, proj_small)

    if late_weights is not None:
        p = {**p, **late_weights(y_att)}
    mixed = jnp.concatenate([y_ssd, y_att], axis=1)
    h1 = _mm(mixed, p["w_out"], "nn", F32, "out_proj", res=x)
    v, rs_ffn = _rms_fwd(h1, p["ffn_norm_w"], "ffn_norm")
    gt, up, act = _ffn_up(v, p["w_gate"], p["w_up"])
    h2 = _mm(act, p["w_down"], "nn", F32, "ffn_down", res=h1)
    loss, dh2, dh2b, d_final_w = _final_loss(h2, p["final_norm_w"], tgt)

    def ready(names):
        return 0.0 if grads_ready is None else grads_ready(names, g)

    g = {"final_norm_w": d_final_w}
    g["w_down"] = _mm(act, dh2b, "tn", _MXU, "dw_down")
    dgt, dup = _ffn_dact(dh2b, p["w_down"], gt, up)
    g["w_gate"] = _mm(v, dgt, "tn", _MXU, "dw_gate")
    g["w_up"] = _mm(v, dup, "tn", _MXU, "dw_up")
    ffn_norm_w = p["ffn_norm_w"] + ready(_FFN)
    dv = _mm(dgt, p["w_gate"], "nt", F32, "dv_gate")
    dv = _mm(dup, p["w_up"], "nt", F32, "dv_up", res=dv)
    dh1, dh1b, g["ffn_norm_w"] = _rms_bwd(dv, h1, rs_ffn, ffn_norm_w, dh2, "ffn_norm_bwd")
    g["w_out"] = _mm(mixed, dh1b, "tn", _MXU, "dw_out")
    dt_bias = p["dt_bias"] + ready(_MID)
    dmixed = _mm(dh1b, p["w_out"], "nt", F32, "dmixed")

    dz, dxa, ddtr, g["dt_bias"], g["a_log"], g["d_skip"], g["ssd_norm_w"] = _ssd_bwd(
        dmixed, proj, proj_small, xa, y_pre, rs_ssd, hs, dt_bias, p["a_log"], p["d_skip"], p["ssd_norm_w"])
    dxbc, g["conv_w"], g["conv_b"] = _conv_bwd(proj, p["conv_w"], p["conv_b"], dxa)

    do_cmp, do_sel, do_win, dgate = _combine_bwd(dmixed, o_cmp, o_sel, o_win, proj_small)
    dq_cmp, dk_cmp, dv_cmp = _attn_bwd(proj, QB, k_cmp, v_cmp, o_cmp, lse_cmp, do_cmp, "cmp", None, "attn_cmp_bwd")
    dq_sel, dks, dvs = _attn_bwd(q_rot, 0, ks_rot, vs, o_sel, lse_sel, do_sel, "sel", sel, "attn_sel_bwd")
    dq_win, dkw, dvw = _attn_bwd(q_rot, 0, kw_rot, vw, o_win, lse_win, do_win, "win", None, "attn_win_bwd")
    drk, g["cmp_w1_k"], g["cmp_w2_k"], g["cmp_pe_k"] = _compress_bwd(rk, p["cmp_pe_k"], p["cmp_w1_k"], p["cmp_w2_k"], hid_k, dk_cmp)
    drv, g["cmp_w1_v"], g["cmp_w2_v"], g["cmp_pe_v"] = _compress_bwd(rv, p["cmp_pe_v"], p["cmp_w1_v"], p["cmp_w2_v"], hid_v, dv_cmp)
    dq = _rope([dq_sel, dq_win], 0, ATT_WIDTH, cos, sin, -1.0, _MXU, "rope_dq", extra=(dq_cmp, 0))
    dks_t = _rope([_tokens_major(dks)], 0, 256, cos, sin, -1.0, _MXU, "rope_dks")
    dkw_t = _rope([_tokens_major(dkw)], 0, 256, cos, sin, -1.0, _MXU, "rope_dkw")
    dproj = jnp.concatenate(
        [dz, dxbc, dq] + [t.astype(_MXU) for t in (_from_rows16(drk), _from_rows16(drv))]
        + [dks_t, _tokens_major(dvs).astype(_MXU), dkw_t, _tokens_major(dvw).astype(_MXU)], axis=1)
    dsmall = jnp.concatenate([ddtr, dgate[:, GATE_COL0:GATE_COL0 + 3 * N_HEADS],
                              jnp.zeros((S, W_SMALL - SSD_HEADS - 3 * N_HEADS), F32)], axis=1).astype(_MXU)
    g["w_main"] = _mm(u, dproj, "tn", _MXU, "dw_in")
    g["w_small"] = _mm(u, dsmall, "tn", F32, "dw_in_small")
    attn_norm_w = p["attn_norm_w"] + ready(_LAST)
    du = _mm(dproj, p["w_main"], "nt", F32, "du_main")
    du = _mm(dsmall, p["w_small"], "nt", F32, "du_small", res=du)
    grad_x, _, g["attn_norm_w"] = _rms_bwd(du, x, rs1, attn_norm_w, dh1, "attn_norm_bwd")
    return loss, grad_x, g


MESH_ID = pl.DeviceIdType.MESH


def _my_coords():
    return lax.axis_index("x"), lax.axis_index("y"), lax.axis_index("c")


def _flat_id(px, py, pc):
    return 4 * px + 2 * py + pc


def _peer(k):
    mx, my, mc = _my_coords()
    return (1 - mx if k & 4 else mx, 1 - my if k & 2 else my, 1 - mc if k & 1 else mc)


def _exchange(arrs, scatter, name):
    n = len(arrs)

    def body(*refs):
        ins, outs = refs[:n], refs[n:2 * n]
        send_sems, recv_sems, local_sems = refs[2 * n:]
        me = _flat_id(*_my_coords())
        copies = []
        for i in range(n):
            src_me = ins[i].at[me] if scatter else ins[i]
            local = pltpu.make_async_copy(src_me, outs[i].at[me], local_sems.at[i])
            local.start()
            copies.append(local)
        for k in range(1, N_DEV):
            peer = _peer(k)
            for i in range(n):
                src = ins[i].at[_flat_id(*peer)] if scatter else ins[i]
                cp = pltpu.make_async_remote_copy(src_ref=src, dst_ref=outs[i].at[me], send_sem=send_sems.at[i * 7 + k - 1],
                                                  recv_sem=recv_sems.at[i * 7 + k - 1], device_id=peer, device_id_type=MESH_ID)
                cp.start()
                copies.append(cp)
        for cp in copies:
            cp.wait()

    any_spec = pl.BlockSpec(memory_space=pl.ANY)
    out_shape = [jax.ShapeDtypeStruct(a.shape if scatter else (N_DEV,) + a.shape, a.dtype) for a in arrs]
    return pl.pallas_call(
        body, name=name, in_specs=[any_spec] * n, out_specs=[any_spec] * n, out_shape=out_shape,
        scratch_shapes=[pltpu.SemaphoreType.DMA((n * 7,)), pltpu.SemaphoreType.DMA((n * 7,)), pltpu.SemaphoreType.DMA((n,))],
        compiler_params=pltpu.CompilerParams(has_side_effects=True))(*arrs)


_HBM = pl.BlockSpec(memory_space=pltpu.HBM)
_SEM = pl.BlockSpec(memory_space=pltpu.SEMAPHORE)
_EFFECT = pltpu.SideEffectType.DATAFLOW_SIDE_EFFECTING


def _split_copies(ins, lands, send_sems, recv_sems, scatter):
    me = _flat_id(*_my_coords())
    out = []
    for k in range(1, N_DEV):
        peer = _peer(k)
        for i in range(len(ins)):
            src = ins[i].at[_flat_id(*peer)] if scatter else ins[i]
            out.append(pltpu.make_async_remote_copy(src_ref=src, dst_ref=lands[i].at[me], send_sem=send_sems.at[i * 7 + k - 1],
                                                    recv_sem=recv_sems.at[i * 7 + k - 1], device_id=peer, device_id_type=MESH_ID))
    return out


def _split_start(arrs, scatter, name):
    n = len(arrs)

    def body(*refs):
        for cp in _split_copies(refs[:n], refs[n:2 * n], refs[2 * n], refs[2 * n + 1], scatter):
            cp.start()
        refs[-1][...] = jnp.zeros_like(refs[-1])

    land_shapes = [a.shape if scatter else (N_DEV,) + a.shape for a in arrs]
    out_shape = ((pltpu.SemaphoreType.DMA((n * 7,)), pltpu.SemaphoreType.DMA((n * 7,)))
                 + tuple(pltpu.HBM(a.shape, a.dtype) for a in arrs) + tuple(pltpu.HBM(s, a.dtype) for s, a in zip(land_shapes, arrs))
                 + (jax.ShapeDtypeStruct((8, 128), F32),))
    operands = ([pltpu.with_memory_space_constraint(a, pltpu.HBM) for a in arrs]
                + [pltpu.with_memory_space_constraint(lax.empty(s, a.dtype), pltpu.HBM) for s, a in zip(land_shapes, arrs)])
    res = pl.pallas_call(
        body, name=name, out_shape=out_shape, in_specs=[_HBM] * (2 * n),
        out_specs=(_SEM, _SEM) + (_HBM,) * (2 * n) + (pl.BlockSpec(memory_space=pltpu.VMEM),),
        input_output_aliases={i: 2 + i for i in range(2 * n)},
        compiler_params=pltpu.CompilerParams(has_side_effects=_EFFECT))(*operands)
    return dict(send=res[0], recv=res[1], ins=list(res[2:2 + n]), lands=list(res[2 + n:2 + 2 * n]), token=res[-1])


def _split_wait(st, scatter, after, name):
    n = len(st["ins"])

    def body(*refs):
        for cp in _split_copies(refs[:n], refs[n:2 * n], refs[2 * n], refs[2 * n + 1], scatter):
            cp.wait_send()
            cp.wait_recv()

    arrs = st["ins"] + st["lands"]
    res = pl.pallas_call(
        body, name=name, out_shape=tuple(pltpu.HBM(a.shape, a.dtype) for a in arrs),
        in_specs=[_HBM] * (2 * n) + [_SEM, _SEM, pl.BlockSpec(memory_space=pl.ANY)], out_specs=(_HBM,) * (2 * n),
        input_output_aliases={i: i for i in range(2 * n)},
        compiler_params=pltpu.CompilerParams(has_side_effects=_EFFECT))(*arrs, st["send"], st["recv"], after)
    me = _flat_id(*_my_coords())
    out = []
    for src, land in zip(res[:n], res[n:]):
        own = lax.dynamic_index_in_dim(src, me, 0, keepdims=True) if scatter else src[None]
        out.append(lax.dynamic_update_slice_in_dim(land, own, me, 0))
    return out


def _sum_parts(parts):
    P, R, C = parts.shape
    tr = _pick(R, (136, 8))

    def body(p_ref, o_ref):
        acc = p_ref[0]
        for j in range(1, P):
            acc = acc + p_ref[j]
        o_ref[...] = acc

    return pl.pallas_call(
        body, name="sum_small_grads", grid=(R // tr,), in_specs=[pl.BlockSpec((P, tr, C), lambda i: (0, i, 0))],
        out_specs=pl.BlockSpec((tr, C), lambda i: (i, 0)), out_shape=jax.ShapeDtypeStruct((R, C), F32),
        compiler_params=_cp(("parallel",)))(parts)


def _adam_sum(parts, w, m, v, name):
    P, R, C = parts.shape
    tr = _pick(R, (256, 128, 64, 32, 8)) if C <= 1024 else _pick(R, (128, 64, 32, 8))

    def body(p_ref, w_ref, m_ref, v_ref, g_ref, d_ref, nm_ref, nv_ref):
        g = p_ref[0].astype(F32)
        for j in range(1, P):
            g = g + p_ref[j].astype(F32)
        g_ref[...] = g
        nm = ADAM_B1 * m_ref[...] + (1.0 - ADAM_B1) * g
        nv = ADAM_B2 * v_ref[...] + (1.0 - ADAM_B2) * (g * g)
        nm_ref[...] = nm
        nv_ref[...] = nv
        m_hat = nm / (1.0 - ADAM_B1 ** ADAM_STEP)
        v_hat = nv / (1.0 - ADAM_B2 ** ADAM_STEP)
        d_ref[...] = -ADAM_LR * (m_hat / (jnp.sqrt(v_hat) + ADAM_EPS) + ADAM_WD * w_ref[...])

    blk = pl.BlockSpec((tr, C), lambda i: (i, 0))
    return pl.pallas_call(
        body, name=name, grid=(R // tr,), in_specs=[pl.BlockSpec((P, tr, C), lambda i: (0, i, 0)), blk, blk, blk],
        out_specs=[blk] * 4, out_shape=[jax.ShapeDtypeStruct((R, C), F32)] * 4, compiler_params=_cp(("parallel",)))(parts, w, m, v)


def _pack(arrs):
    rows = []
    for a in arrs:
        f = a.reshape(-1).astype(F32)
        f = jnp.pad(f, (0, (-f.shape[0]) % 128))
        rows.append(f.reshape(-1, 128))
    out = jnp.concatenate(rows, axis=0)
    return jnp.pad(out, ((0, (-out.shape[0]) % 8), (0, 0)))


def _unpack(pack, shapes):
    out, r = [], 0
    for s in shapes:
        n = int(np.prod(s))
        nr = -(-n // 128)
        out.append(pack[r:r + nr].reshape(-1)[:n].reshape(s))
        r += nr
    return out


_WEIGHTS = ["attn_norm_w", "w_in", "conv_w", "conv_b", "dt_bias", "a_log", "d_skip", "ssd_norm_w", "cmp_w1_k", "cmp_w2_k",
            "cmp_w1_v", "cmp_w2_v", "cmp_pe_k", "cmp_pe_v", "w_out", "ffn_norm_w", "w_gate", "w_up", "w_down", "final_norm_w"]
_BIG = ["w_in", "w_gate", "w_up", "w_down", "w_out", "cmp_w1_k", "cmp_w1_v"]
_COL_SHARDED = ("w_in", "w_gate", "w_up")
_REPLICATED = ["attn_norm_w", "conv_b", "dt_bias", "a_log", "d_skip", "ssd_norm_w", "cmp_pe_k", "cmp_pe_v", "ffn_norm_w",
               "final_norm_w"]
_SMALL_SHARDED = ["conv_w", "cmp_w2_k", "cmp_w2_v"]
_SMALL_FULL_SHAPES = {"attn_norm_w": (1, D_MODEL), "conv_b": (1, CONV_CH), "dt_bias": (1, SSD_HEADS), "a_log": (1, SSD_HEADS),
                      "d_skip": (1, SSD_HEADS), "ssd_norm_w": (1, SSD_WIDTH), "cmp_pe_k": (1, 32 * HD), "cmp_pe_v": (1, 32 * HD),
                      "ffn_norm_w": (1, D_MODEL), "final_norm_w": (1, D_MODEL), "conv_w": (CONV_K, CONV_CH),
                      "cmp_w2_k": (CMP_HID, HD), "cmp_w2_v": (CMP_HID, HD)}


def _cols_to_slabs(g):
    R = g.shape[0]
    return g.reshape(R, N_DEV, -1).transpose(1, 0, 2)


def _slabs_to_cols(s):
    return s.transpose(1, 0, 2).reshape(s.shape[1], -1)


def kernel(x, attn_norm_w, w_in, conv_w, conv_b, dt_bias, a_log, d_skip, ssd_norm_w, cmp_w1_k, cmp_w2_k, cmp_w1_v, cmp_w2_v, cmp_pe_k, cmp_pe_v, w_out, ffn_norm_w, w_gate, w_up, w_down, final_norm_w, loss_target, m_attn_norm_w, m_w_in, m_conv_w, m_conv_b, m_dt_bias, m_a_log, m_d_skip, m_ssd_norm_w, m_cmp_w1_k, m_cmp_w2_k, m_cmp_w1_v, m_cmp_w2_v, m_cmp_pe_k, m_cmp_pe_v, m_w_out, m_ffn_norm_w, m_w_gate, m_w_up, m_w_down, m_final_norm_w, v_attn_norm_w, v_w_in, v_conv_w, v_conv_b, v_dt_bias, v_a_log, v_d_skip, v_ssd_norm_w, v_cmp_w1_k, v_cmp_w2_k, v_cmp_w1_v, v_cmp_w2_v, v_cmp_pe_k, v_cmp_pe_v, v_w_out, v_ffn_norm_w, v_w_gate, v_w_up, v_w_down, v_final_norm_w):
    a = dict(locals())
    me = _flat_id(*_my_coords())

    small_in = _pack([cmp_w2_k[0], cmp_w2_v[0], conv_w[0]])
    shard = {n: a[n][0].astype(_MXU) for n in _BIG}
    st_early = _split_start([shard[n] for n in _EARLY] + [small_in], False, "gather_early_start")
    zero = st_early["token"][0, 0].astype(_MXU)
    st_late = _split_start([shard[_LATE[0]] + zero] + [shard[n] for n in _LATE[1:]], False, "gather_late_start")
    got = _split_wait(st_early, False, st_late["token"], "gather_early_wait")

    def assemble(n, t):
        return _slabs_to_cols(t) if n in _COL_SHARDED else t.reshape(-1, t.shape[-1])

    full = {n: assemble(n, t) for n, t in zip(_EARLY, got[:-1])}
    w2k, w2v, cw = [], [], []
    for d in range(N_DEV):
        parts = _unpack(got[-1][d], [cmp_w2_k.shape[1:], cmp_w2_v.shape[1:], conv_w.shape[1:]])
        w2k.append(parts[0]); w2v.append(parts[1]); cw.append(parts[2])
    w_main, w_small = _split_w_in(full["w_in"])
    p = dict(attn_norm_w=attn_norm_w, w_main=w_main, w_small=w_small, conv_w=jnp.concatenate(cw, axis=1), conv_b=conv_b,
             dt_bias=dt_bias, a_log=a_log, d_skip=d_skip, ssd_norm_w=ssd_norm_w, cmp_w1_k=full["cmp_w1_k"],
             cmp_w2_k=jnp.concatenate(w2k, axis=0).astype(_MXU), cmp_w1_v=full["cmp_w1_v"],
             cmp_w2_v=jnp.concatenate(w2v, axis=0).astype(_MXU), cmp_pe_k=cmp_pe_k.reshape(1, -1), cmp_pe_v=cmp_pe_v.reshape(1, -1),
             ffn_norm_w=ffn_norm_w, final_norm_w=final_norm_w.reshape(1, -1))

    def late_weights(after):
        got_late = _split_wait(st_late, False, after, "gather_late_wait")
        return {n: assemble(n, t) for n, t in zip(_LATE, got_late)}

    def slabs_of(g, names):
        return [(_cols_to_slabs(g[n]) if n in _COL_SHARDED else g[n].reshape(N_DEV, -1, g[n].shape[-1])).astype(_MXU) for n in names]

    started = []

    def grads_ready(names, g):
        if "w_in" in names:
            g = {**g, "w_in": _merge_w_in(g["w_main"], g["w_small"])}
        started.append((names, _split_start(slabs_of(g, names), True, "scatter_grads_start_%d" % len(started))))
        return started[-1][1]["token"][0:1, 0:1]

    loss_part, grad_x, g = _local_step(x[0], loss_target[0], p, late_weights, grads_ready)
    loss = lax.psum(loss_part[0, 0], ("x", "y", "c"))

    out, after = {}, g["w_main"]
    for i, (names, st) in enumerate(started):
        if i == len(started) - 1:
            after = grad_x[0:1, 0:LANE] + after[0:1, 0:LANE]
        received = _split_wait(st, True, after, "scatter_grads_wait_%d" % i)
        for n, parts in zip(names, received):
            out[n] = _adam_sum(parts, a[n][0], a["m_" + n][0], a["v_" + n][0], "adam_" + n)
        after = out[names[-1]][0]

    small_names = _REPLICATED + _SMALL_SHARDED
    g_small = _pack([g[n] for n in small_names])
    g_sum = _sum_parts(_exchange([g_small], False, "gather_small_grads")[0])
    gs = dict(zip(small_names, _unpack(g_sum, [_SMALL_FULL_SHAPES[n] for n in small_names])))
    gs["conv_w"] = lax.dynamic_slice_in_dim(gs["conv_w"], me * conv_w.shape[2], conv_w.shape[2], axis=1)
    gs["cmp_w2_k"] = lax.dynamic_slice_in_dim(gs["cmp_w2_k"], me * cmp_w2_k.shape[1], cmp_w2_k.shape[1], axis=0)
    gs["cmp_w2_v"] = lax.dynamic_slice_in_dim(gs["cmp_w2_v"], me * cmp_w2_v.shape[1], cmp_w2_v.shape[1], axis=0)
    packs = [_pack([t[n] for n in small_names]) for t in
             (gs, a, {n: a["m_" + n] for n in small_names}, {n: a["v_" + n] for n in small_names})]
    res_small = _adam_sum(packs[0][None], packs[1], packs[2], packs[3], "adam_small")
    shapes = [a[n].shape for n in small_names]
    unpacked = [dict(zip(small_names, _unpack(r, shapes))) for r in res_small]
    for n in small_names:
        out[n] = tuple(u[n] for u in unpacked)

    outs = [loss, grad_x[None]]
    for j in range(4):
        for n in _WEIGHTS:
            outs.append(out[n][j].reshape(a[n].shape))
    return tuple(outs)
```

```python
import functools
import math

import numpy as np
import jax
import jax.numpy as jnp
from jax import lax
from jax.experimental import pallas as pl
from jax.experimental.pallas import tpu as pltpu

F32 = jnp.float32
_MXU = jnp.bfloat16
_HI = lax.Precision.HIGHEST

N_DEV = 8
D_MODEL = 2048
SSD_WIDTH = 1024
ATT_WIDTH = 1024
SSD_HEADS = 16
SSD_P = 64
SSD_N = 128
SSD_L = 128
SSD_G = 2
CONV_CH = 1536
CONV_K = 4
HD = 64
N_HEADS = 16
N_KV = 4
GRP = 4
CMP_HID = 256
SEL_BLOCK = 64
N_SELECT = 16
WINDOW = 512
ROPE_DIM = 16
ROPE_THETA = 500000.0
D_FF = 5632
EPS = 1e-6
NEG = -1e30
FORCE = 1e4
SCALE = HD ** -0.5
D_IN = 5184
W_MAIN = 5120
W_SMALL = 128
VMEM_LIMIT = 52 * 1024 * 1024

ADAM_LR, ADAM_B1, ADAM_B2, ADAM_EPS, ADAM_WD, ADAM_STEP = 0.001, 0.9, 0.999, 1e-08, 0.01, 10


def _pick(n, cands):
    for c in cands:
        if n % c == 0:
            return c
    return n


def _cp(sem=None):
    return pltpu.CompilerParams(dimension_semantics=sem, vmem_limit_bytes=VMEM_LIMIT)


def _sigmoid(x):
    return 1.0 / (1.0 + jnp.exp(-x))


def _dot(a, b, dims, hi=False):
    dn = {"nn": (((1,), (0,)), ((), ())), "nt": (((1,), (1,)), ((), ())), "tn": (((0,), (0,)), ((), ()))}[dims]
    if hi:
        return lax.dot_general(a.astype(F32), b.astype(F32), dn, precision=_HI, preferred_element_type=F32)
    return lax.dot_general(a.astype(_MXU), b.astype(_MXU), dn, preferred_element_type=F32)


LANE = 128
MM_TILE = 1024
MM_K_WHOLE = 2048
MM_K_STEP = 1536
TN_ACC_ELEMS = 3 * 2 ** 20
TN_K_STEP = 512


def _largest_tile(n, cap):
    if n <= cap:
        return n
    best = LANE
    for t in range(LANE, cap + 1, LANE):
        if n % t == 0:
            best = t
    return best


def _mm_tiles(mode, M, N, K):
    if mode == "tn":
        tm = _largest_tile(M, 2 * MM_TILE)
        return tm, _largest_tile(N, TN_ACC_ELEMS // tm), _largest_tile(K, TN_K_STEP)
    tk = K if K <= MM_K_WHOLE else _largest_tile(K, MM_K_STEP)
    return _largest_tile(M, MM_TILE), _largest_tile(N, MM_TILE), tk


def _mm(a, b, mode, out_dtype, name, res=None):
    if mode == "nn":
        (M, K), N = a.shape, b.shape[1]
    elif mode == "nt":
        (M, K), N = a.shape, b.shape[0]
    else:
        (K, M), N = a.shape, b.shape[1]
    tm, tn, tk = _mm_tiles(mode, M, N, K)
    nk = K // tk
    a_spec = pl.BlockSpec((tk, tm), lambda i, j, k: (k, i)) if mode == "tn" else pl.BlockSpec((tm, tk), lambda i, j, k: (i, k))
    b_spec = pl.BlockSpec((tn, tk), lambda i, j, k: (j, k)) if mode == "nt" else pl.BlockSpec((tk, tn), lambda i, j, k: (k, j))
    o_spec = pl.BlockSpec((tm, tn), lambda i, j, k: (i, j))

    def finish(r, r_ref, o_ref):
        if res is not None:
            r = r + r_ref[...].astype(F32)
        o_ref[...] = r.astype(out_dtype)

    def body_one_step(*refs):
        a_ref, b_ref, o_ref = refs[0], refs[1], refs[-1]
        finish(_dot(a_ref[...], b_ref[...], mode), refs[2], o_ref)

    def body(*refs):
        a_ref, b_ref, o_ref, acc = refs[0], refs[1], refs[-2], refs[-1]
        k = pl.program_id(2)

        @pl.when(k == 0)
        def _():
            acc[...] = jnp.zeros_like(acc)

        acc[...] += _dot(a_ref[...], b_ref[...], mode)

        @pl.when(k == nk - 1)
        def _():
            finish(acc[...], refs[2], o_ref)

    ins, specs = [a, b], [a_spec, b_spec]
    if res is not None:
        ins.append(res)
        specs.append(o_spec)
    return pl.pallas_call(
        body_one_step if nk == 1 else body, name=name, grid=(M // tm, N // tn, nk), in_specs=specs, out_specs=o_spec,
        out_shape=jax.ShapeDtypeStruct((M, N), out_dtype), scratch_shapes=[] if nk == 1 else [pltpu.VMEM((tm, tn), F32)],
        compiler_params=_cp(("parallel", "parallel", "arbitrary")))(*ins)


def _ffn_up(v, w_gate, w_up):
    S, D = v.shape
    F = w_gate.shape[1]
    tm, tn = _largest_tile(S, MM_TILE), _largest_tile(F, MM_TILE // 2)

    def body(v_ref, wg_ref, wu_ref, gt_ref, up_ref, act_ref):
        vv = v_ref[...]
        g = _dot(vv, wg_ref[...], "nn")
        u = _dot(vv, wu_ref[...], "nn")
        gt_ref[...] = g
        up_ref[...] = u
        act_ref[...] = (g * _sigmoid(g) * u).astype(act_ref.dtype)

    o_spec = pl.BlockSpec((tm, tn), lambda i, j: (i, j))
    w_spec = pl.BlockSpec((D, tn), lambda i, j: (0, j))
    return pl.pallas_call(
        body, name="ffn_up", grid=(S // tm, F // tn),
        in_specs=[pl.BlockSpec((tm, D), lambda i, j: (i, 0)), w_spec, w_spec], out_specs=[o_spec, o_spec, o_spec],
        out_shape=[jax.ShapeDtypeStruct((S, F), F32), jax.ShapeDtypeStruct((S, F), F32), jax.ShapeDtypeStruct((S, F), _MXU)],
        compiler_params=_cp(("parallel", "parallel")))(v, w_gate, w_up)


def _ffn_dact(dh2, w_down, gt, up):
    S, D = dh2.shape
    F = w_down.shape[0]
    tm, tn = _largest_tile(S, MM_TILE), _largest_tile(F, MM_TILE // 2)

    def body(d_ref, w_ref, gt_ref, up_ref, dg_ref, du_ref):
        da, g, u = _dot(d_ref[...], w_ref[...], "nt"), gt_ref[...], up_ref[...]
        s = _sigmoid(g)
        dg_ref[...] = (da * u * (s * (1.0 + g * (1.0 - s)))).astype(dg_ref.dtype)
        du_ref[...] = (da * (g * s)).astype(du_ref.dtype)

    o_spec = pl.BlockSpec((tm, tn), lambda i, j: (i, j))
    return pl.pallas_call(
        body, name="ffn_dact", grid=(S // tm, F // tn),
        in_specs=[pl.BlockSpec((tm, D), lambda i, j: (i, 0)), pl.BlockSpec((tn, D), lambda i, j: (j, 0)), o_spec, o_spec],
        out_specs=[o_spec, o_spec],
        out_shape=[jax.ShapeDtypeStruct((S, F), _MXU), jax.ShapeDtypeStruct((S, F), _MXU)],
        compiler_params=_cp(("parallel", "parallel")))(dh2, w_down, gt, up)


def _rms_fwd(x, w, name):
    S, D = x.shape
    tr = _pick(S, (256, 128))

    def body(x_ref, w_ref, xn_ref, rs_ref):
        xv = x_ref[...]
        rs = lax.rsqrt(jnp.mean(xv * xv, axis=-1, keepdims=True) + EPS)
        xn_ref[...] = ((xv * rs) * w_ref[...]).astype(xn_ref.dtype)
        rs_ref[...] = rs

    return pl.pallas_call(
        body, name=name, grid=(S // tr,),
        in_specs=[pl.BlockSpec((tr, D), lambda i: (i, 0)), pl.BlockSpec((1, D), lambda i: (0, 0))],
        out_specs=[pl.BlockSpec((tr, D), lambda i: (i, 0)), pl.BlockSpec((tr, 1), lambda i: (i, 0))],
        out_shape=[jax.ShapeDtypeStruct((S, D), _MXU), jax.ShapeDtypeStruct((S, 1), F32)],
        compiler_params=_cp(("parallel",)))(x, w)


def _rms_bwd(dyn, x, rs, w, res, name):
    S, D = x.shape
    tr = _pick(S, (256, 128))

    def body(dy_ref, x_ref, rs_ref, w_ref, res_ref, dx_ref, dxb_ref, dw_ref):
        @pl.when(pl.program_id(0) == 0)
        def _():
            dw_ref[...] = jnp.zeros_like(dw_ref)

        dy, r = dy_ref[...].astype(F32), rs_ref[...]
        xhat = x_ref[...] * r
        dw_ref[...] += jnp.sum(dy * xhat, axis=0, keepdims=True)
        dxhat = dy * w_ref[...]
        dx = res_ref[...] + r * (dxhat - xhat * jnp.mean(dxhat * xhat, axis=-1, keepdims=True))
        dx_ref[...] = dx
        dxb_ref[...] = dx.astype(dxb_ref.dtype)

    row = pl.BlockSpec((tr, D), lambda i: (i, 0))
    vec = pl.BlockSpec((1, D), lambda i: (0, 0))
    return pl.pallas_call(
        body, name=name, grid=(S // tr,),
        in_specs=[row, row, pl.BlockSpec((tr, 1), lambda i: (i, 0)), vec, row], out_specs=[row, row, vec],
        out_shape=[jax.ShapeDtypeStruct((S, D), F32), jax.ShapeDtypeStruct((S, D), _MXU), jax.ShapeDtypeStruct((1, D), F32)],
        compiler_params=_cp(("arbitrary",)))(dyn, x, rs, w, res)


def _final_loss(h2, w, tgt):
    S, D = h2.shape
    tr = _pick(S, (256, 128))

    def body(h_ref, w_ref, t_ref, loss_ref, dh_ref, dhb_ref, dw_ref):
        @pl.when(pl.program_id(0) == 0)
        def _():
            dw_ref[...] = jnp.zeros_like(dw_ref)
            loss_ref[...] = jnp.zeros_like(loss_ref)

        hv, wv = h_ref[...], w_ref[...]
        rs = lax.rsqrt(jnp.mean(hv * hv, axis=-1, keepdims=True) + EPS)
        xhat = hv * rs
        err = xhat * wv - t_ref[...]
        row = jnp.mean(err * err, axis=-1, keepdims=True)
        loss_ref[...] += 0.5 * jnp.sum(row, axis=0, keepdims=True)
        dy = err * (1.0 / D)
        dw_ref[...] += jnp.sum(dy * xhat, axis=0, keepdims=True)
        dxhat = dy * wv
        dh = rs * (dxhat - xhat * jnp.mean(dxhat * xhat, axis=-1, keepdims=True))
        dh_ref[...] = dh
        dhb_ref[...] = dh.astype(dhb_ref.dtype)

    row = pl.BlockSpec((tr, D), lambda i: (i, 0))
    vec = pl.BlockSpec((1, D), lambda i: (0, 0))
    return pl.pallas_call(
        body, name="final_loss", grid=(S // tr,), in_specs=[row, vec, row],
        out_specs=[pl.BlockSpec((1, 1), lambda i: (0, 0)), row, row, vec],
        out_shape=[jax.ShapeDtypeStruct((1, 1), F32), jax.ShapeDtypeStruct((S, D), F32), jax.ShapeDtypeStruct((S, D), _MXU),
                   jax.ShapeDtypeStruct((1, D), F32)],
        compiler_params=_cp(("arbitrary",)))(h2, w, tgt)


def _shift_rows(x, k, rows):
    if k == 0:
        return x
    S = x.shape[0]
    r = pltpu.roll(x, k % S, axis=0)
    ok = (rows >= k) if k > 0 else (rows < S + k)
    return jnp.where(ok, r, 0.0)


XBC_COL0 = SSD_WIDTH // 128


def _conv_fwd(proj, conv_w, conv_b):
    S = proj.shape[0]
    nct = CONV_CH // 128

    def body(x_ref, w_ref, b_ref, o_ref):
        x = x_ref[...]
        rows = lax.broadcasted_iota(jnp.int32, x.shape, 0)
        c = b_ref[...] + w_ref[3:4, :] * x
        for k in range(1, CONV_K):
            c = c + w_ref[3 - k:4 - k, :] * _shift_rows(x, k, rows)
        o_ref[...] = c * _sigmoid(c)

    return pl.pallas_call(
        body, name="conv_fwd", grid=(nct,),
        in_specs=[pl.BlockSpec((S, 128), lambda j: (0, XBC_COL0 + j)), pl.BlockSpec((CONV_K, 128), lambda j: (0, j)),
                  pl.BlockSpec((1, 128), lambda j: (0, j))],
        out_specs=pl.BlockSpec((S, 128), lambda j: (0, j)),
        out_shape=jax.ShapeDtypeStruct((S, CONV_CH), F32), compiler_params=_cp(("parallel",)))(proj, conv_w, conv_b)


def _conv_bwd(proj, conv_w, conv_b, dxa):
    S = proj.shape[0]
    nct = CONV_CH // 128

    def body(x_ref, w_ref, b_ref, d_ref, dx_ref, dw_ref, db_ref):
        x = x_ref[...]
        rows = lax.broadcasted_iota(jnp.int32, x.shape, 0)
        xs = [_shift_rows(x, k, rows) for k in range(CONV_K)]
        c = b_ref[...] + w_ref[3:4, :] * x
        for k in range(1, CONV_K):
            c = c + w_ref[3 - k:4 - k, :] * xs[k]
        s = _sigmoid(c)
        dc = d_ref[...] * (s * (1.0 + c * (1.0 - s)))
        dx = w_ref[3:4, :] * dc
        for k in range(1, CONV_K):
            dx = dx + w_ref[3 - k:4 - k, :] * _shift_rows(dc, -k, rows)
        dx_ref[...] = dx.astype(dx_ref.dtype)
        for k in range(CONV_K):
            dw_ref[3 - k:4 - k, :] = jnp.sum(dc * xs[k], axis=0, keepdims=True)
        db_ref[...] = jnp.sum(dc, axis=0, keepdims=True)

    col = pl.BlockSpec((S, 128), lambda j: (0, j))
    return pl.pallas_call(
        body, name="conv_bwd", grid=(nct,),
        in_specs=[pl.BlockSpec((S, 128), lambda j: (0, XBC_COL0 + j)), pl.BlockSpec((CONV_K, 128), lambda j: (0, j)),
                  pl.BlockSpec((1, 128), lambda j: (0, j)), col],
        out_specs=[col, pl.BlockSpec((CONV_K, 128), lambda j: (0, j)), pl.BlockSpec((1, 128), lambda j: (0, j))],
        out_shape=[jax.ShapeDtypeStruct((S, CONV_CH), _MXU), jax.ShapeDtypeStruct((CONV_K, CONV_CH), F32),
                   jax.ShapeDtypeStruct((1, CONV_CH), F32)],
        compiler_params=_cp(("parallel",)))(proj, conv_w, conv_b, dxa)


def _ssd_consts():
    L = SSD_L
    r = lax.broadcasted_iota(jnp.int32, (L, L), 0)
    c = lax.broadcasted_iota(jnp.int32, (L, L), 1)
    causal = r >= c
    upper = (r <= c).astype(F32)
    hr = lax.broadcasted_iota(jnp.int32, (SSD_HEADS, SSD_WIDTH), 0)
    hc = lax.broadcasted_iota(jnp.int32, (SSD_HEADS, SSD_WIDTH), 1)
    expand = (lax.shift_right_logical(hc, 6) == hr).astype(F32)
    return causal, causal.astype(F32), upper, expand


def _softplus(x):
    return jnp.maximum(x, 0.0) + jnp.log(1.0 + jnp.exp(-jnp.abs(x)))


def _ssd_scalars(dtr, dt_bias, a_log, tri, upper, expand):
    dt = _softplus(dtr + dt_bias)
    A = -jnp.exp(a_log)
    adt = dt * A
    acum = _dot(tri, adt, "nn", hi=True)
    acum_t = _dot(adt, upper, "tn", hi=True)
    alast = acum[SSD_L - 1:SSD_L, :]
    e = jnp.exp(acum)
    wdec = jnp.exp(alast - acum)
    gam = jnp.exp(alast)
    ex = lambda t: _dot(t, expand, "nn", hi=True)
    gam8 = jnp.broadcast_to(gam, (8, SSD_HEADS))
    return dt, A, acum, acum_t, e, wdec, gam, ex(dt), ex(e), ex(wdec), ex(gam8)[0:1, :]


def _ssd_fwd(proj, proj_small, xa, dt_bias, a_log, d_skip, norm_w):
    S = proj.shape[0]
    L, N, W = SSD_L, SSD_N, SSD_WIDTH
    nc = S // L

    def body(z_ref, xa_ref, dtr_ref, dtb_ref, al_ref, dsk_ref, nw_ref, yo_ref, y_ref, rs_ref, hs_ref, h_scr, y_scr):
        @pl.when(pl.program_id(0) == 0)
        def _():
            h_scr[...] = jnp.zeros_like(h_scr)

        causal, tri, upper, expand = _ssd_consts()
        dt, A, acum, acum_t, e, wdec, gam, dtE, eE, wE, gamE = _ssd_scalars(dtr_ref[:, 0:SSD_HEADS], dtb_ref[...], al_ref[...], tri, upper, expand)
        xs = xa_ref[:, 0:W]
        X = xs * dtE
        XW = X * wE
        hs_ref[0] = h_scr[...]
        for g in range(SSD_G):
            gs = slice(g * 512, (g + 1) * 512)
            Bg = xa_ref[:, W + g * N:W + (g + 1) * N]
            Cg = xa_ref[:, W + SSD_G * N + g * N:W + SSD_G * N + (g + 1) * N]
            Hg = h_scr[:, gs]
            CB = _dot(Cg, Bg, "nt")
            yoff = _dot(Cg, Hg, "nn") * eE[:, gs]
            st = _dot(Bg, XW[:, gs], "tn")
            for j in range(8):
                h = g * 8 + j
                hsl = slice(h * SSD_P, (h + 1) * SSD_P)
                lam = jnp.exp(jnp.where(causal, acum[:, h:h + 1] - acum_t[h:h + 1, :], -jnp.inf))
                y_scr[:, hsl] = _dot(CB * lam, X[:, hsl], "nn") + yoff[:, j * SSD_P:(j + 1) * SSD_P]
            h_scr[:, gs] = gamE[:, gs] * Hg + st
        dskE = _dot(jnp.broadcast_to(dsk_ref[...], (8, SSD_HEADS)), expand, "nn", hi=True)[0:1, :]
        y = y_scr[...] + dskE * xs
        y_ref[...] = y
        zv = z_ref[...]
        yg = y * (zv * _sigmoid(zv))
        rs = lax.rsqrt(jnp.mean(yg * yg, axis=-1, keepdims=True) + EPS)
        rs_ref[...] = rs
        yo_ref[...] = ((yg * rs) * nw_ref[...]).astype(yo_ref.dtype)

    p16 = pl.BlockSpec((1, SSD_HEADS), lambda c: (0, 0))
    return pl.pallas_call(
        body, name="ssd_fwd", grid=(nc,),
        in_specs=[pl.BlockSpec((L, W), lambda c: (c, 0)), pl.BlockSpec((L, CONV_CH), lambda c: (c, 0)),
                  pl.BlockSpec((L, W_SMALL), lambda c: (c, 0)), p16, p16, p16, pl.BlockSpec((1, W), lambda c: (0, 0))],
        out_specs=[pl.BlockSpec((L, W), lambda c: (c, 0)), pl.BlockSpec((L, W), lambda c: (c, 0)),
                   pl.BlockSpec((L, 1), lambda c: (c, 0)), pl.BlockSpec((1, N, W), lambda c: (c, 0, 0))],
        out_shape=[jax.ShapeDtypeStruct((S, W), _MXU), jax.ShapeDtypeStruct((S, W), F32), jax.ShapeDtypeStruct((S, 1), F32),
                   jax.ShapeDtypeStruct((nc, N, W), F32)],
        scratch_shapes=[pltpu.VMEM((N, W), F32), pltpu.VMEM((L, W), F32)],
        compiler_params=_cp(("arbitrary",)))(proj, xa, proj_small, dt_bias, a_log, d_skip, norm_w)


def _ssd_bwd(dmixed, proj, proj_small, xa, y, rs2, hs, dt_bias, a_log, d_skip, norm_w):
    S = proj.shape[0]
    L, N, W, H = SSD_L, SSD_N, SSD_WIDTH, SSD_HEADS
    nc = S // L

    def body(dyo_ref, z_ref, xa_ref, dtr_ref, y_ref, rs_ref, hs_ref, dtb_ref, al_ref, dsk_ref, nw_ref,
             dz_ref, dxa_ref, ddtr_ref, ddtb_ref, dal_ref, ddsk_ref, dnw_ref, dh_scr, dx_scr):
        @pl.when(pl.program_id(0) == 0)
        def _():
            dh_scr[...] = jnp.zeros_like(dh_scr)
            ddtb_ref[...] = jnp.zeros_like(ddtb_ref)
            dal_ref[...] = jnp.zeros_like(dal_ref)
            ddsk_ref[...] = jnp.zeros_like(ddsk_ref)
            dnw_ref[...] = jnp.zeros_like(dnw_ref)

        causal, tri, upper, expand = _ssd_consts()
        heads = lambda t: _dot(t, expand, "nt", hi=True)
        onehot = lambda h: (lax.broadcasted_iota(jnp.int32, (1, H), 1) == h).astype(F32)

        zv, yv, rs = z_ref[...], y_ref[...], rs_ref[...]
        sz = _sigmoid(zv)
        zs = zv * sz
        xhat = (yv * zs) * rs
        dyo = dyo_ref[...].astype(F32)
        dnw_ref[...] += jnp.sum(dyo * xhat, axis=0, keepdims=True)
        dxhat = dyo * nw_ref[...]
        dyg = rs * (dxhat - xhat * jnp.mean(dxhat * xhat, axis=-1, keepdims=True))
        dz_ref[...] = (dyg * yv * (sz * (1.0 + zv * (1.0 - sz)))).astype(dz_ref.dtype)
        dy = dyg * zs

        dtr = dtr_ref[:, 0:H]
        dt, A, acum, acum_t, e, wdec, gam, dtE, eE, wE, gamE = _ssd_scalars(dtr, dtb_ref[...], al_ref[...], tri, upper, expand)
        xs = xa_ref[:, 0:W]
        X = xs * dtE
        XW = X * wE
        dskE = _dot(jnp.broadcast_to(dsk_ref[...], (8, H)), expand, "nn", hi=True)[0:1, :]
        ddsk_ref[...] += heads(jnp.broadcast_to(jnp.sum(dy * xs, axis=0, keepdims=True), (8, W)))[0:1, :]

        dYe = dy * eE
        dacum = jnp.zeros((L, H), F32)
        de_full = []
        dw_full = []
        dgam_full = []
        for g in range(SSD_G):
            gs = slice(g * 512, (g + 1) * 512)
            Bg = xa_ref[:, W + g * N:W + (g + 1) * N]
            Cg = xa_ref[:, W + SSD_G * N + g * N:W + SSD_G * N + (g + 1) * N]
            Hg = hs_ref[0, :, gs]
            dHn = dh_scr[:, gs]
            CH = _dot(Cg, Hg, "nn")
            de_full.append(dy[:, gs] * CH)
            dC = _dot(dYe[:, gs], Hg, "nt")
            dHs = gamE[:, gs] * dHn + _dot(Cg, dYe[:, gs], "tn")
            dgam_full.append(jnp.sum(dHn * Hg, axis=0, keepdims=True))
            BdS = _dot(Bg, dHn, "nn")
            dB = _dot(XW[:, gs], dHn, "nt")
            dx_scr[:, gs] = BdS * wE[:, gs]
            dw_full.append(BdS * X[:, gs])
            CB = _dot(Cg, Bg, "nt")
            dCB = jnp.zeros((L, L), F32)
            for j in range(8):
                h = g * 8 + j
                hsl = slice(h * SSD_P, (h + 1) * SSD_P)
                lam = jnp.exp(jnp.where(causal, acum[:, h:h + 1] - acum_t[h:h + 1, :], -jnp.inf))
                M = CB * lam
                dM = _dot(dy[:, hsl], X[:, hsl], "nt")
                dx_scr[:, hsl] += _dot(M, dy[:, hsl], "tn")
                dCB = dCB + dM * lam
                Q = dM * M
                rowsum = jnp.sum(Q, axis=1, keepdims=True)
                colsum = _dot(Q, jnp.ones((L, 8), F32), "tn", hi=True)[:, 0:1]
                dacum = dacum + (rowsum - colsum) * onehot(h)
            dC = dC + _dot(dCB, Bg, "nn")
            dB = dB + _dot(dCB, Cg, "tn")
            dxa_ref[:, W + g * N:W + (g + 1) * N] = dB
            dxa_ref[:, W + SSD_G * N + g * N:W + SSD_G * N + (g + 1) * N] = dC
            dh_scr[:, gs] = dHs

        de16 = heads(jnp.concatenate(de_full, axis=1))
        dw16 = heads(jnp.concatenate(dw_full, axis=1))
        dgam16 = heads(jnp.broadcast_to(jnp.concatenate(dgam_full, axis=1), (8, W)))[0:1, :]
        dacum = dacum + de16 * e - dw16 * wdec
        dlast = jnp.sum(dw16 * wdec, axis=0, keepdims=True) + dgam16 * gam
        lastrow = (lax.broadcasted_iota(jnp.int32, (L, 1), 0) == L - 1).astype(F32)
        dacum = dacum + lastrow * dlast
        da = _dot(tri, dacum, "tn", hi=True)
        dX = dx_scr[...]
        ddt = da * A + heads(dX * xs)
        dA = jnp.sum(da * dt, axis=0, keepdims=True)
        dal_ref[...] += dA * A
        ddtr = ddt * _sigmoid(dtr + dtb_ref[...])
        ddtb_ref[...] += jnp.sum(ddtr, axis=0, keepdims=True)
        ddtr_ref[...] = ddtr
        dxa_ref[:, 0:W] = dX * dtE + dy * dskE

    p16 = pl.BlockSpec((1, H), lambda c: (0, 0))
    rev = lambda c: (nc - 1 - c, 0)
    return pl.pallas_call(
        body, name="ssd_bwd", grid=(nc,),
        in_specs=[pl.BlockSpec((L, W), rev), pl.BlockSpec((L, W), rev), pl.BlockSpec((L, CONV_CH), rev),
                  pl.BlockSpec((L, W_SMALL), rev), pl.BlockSpec((L, W), rev), pl.BlockSpec((L, 1), rev),
                  pl.BlockSpec((1, N, W), lambda c: (nc - 1 - c, 0, 0)), p16, p16, p16, pl.BlockSpec((1, W), lambda c: (0, 0))],
        out_specs=[pl.BlockSpec((L, W), rev), pl.BlockSpec((L, CONV_CH), rev), pl.BlockSpec((L, H), rev),
                   p16, p16, p16, pl.BlockSpec((1, W), lambda c: (0, 0))],
        out_shape=[jax.ShapeDtypeStruct((S, W), _MXU), jax.ShapeDtypeStruct((S, CONV_CH), F32), jax.ShapeDtypeStruct((S, H), F32),
                   jax.ShapeDtypeStruct((1, H), F32), jax.ShapeDtypeStruct((1, H), F32), jax.ShapeDtypeStruct((1, H), F32),
                   jax.ShapeDtypeStruct((1, W), F32)],
        scratch_shapes=[pltpu.VMEM((N, W), F32), pltpu.VMEM((L, W), F32)],
        compiler_params=_cp(("arbitrary",)))(dmixed, proj, xa, proj_small, y, rs2, hs, dt_bias, a_log, d_skip, norm_w)


def _rope_tables(S):
    inv = 1.0 / (ROPE_THETA ** (jnp.arange(0, ROPE_DIM, 2, dtype=F32) / ROPE_DIM))
    ang = jnp.arange(S, dtype=F32)[:, None] * inv[None, :]
    cos, sin = jnp.cos(ang), jnp.sin(ang)
    half = ROPE_DIM // 2
    c64 = jnp.concatenate([cos, cos, jnp.ones((S, HD - ROPE_DIM), F32)], axis=1)
    s64 = jnp.concatenate([sin, sin, jnp.zeros((S, HD - ROPE_DIM), F32)], axis=1)
    del half
    return jnp.concatenate([c64, c64], axis=1), jnp.concatenate([s64, s64], axis=1)


def _rope(xs, blk0, width, cos, sin, sign, out_dtype, name, extra=None):
    S = xs[0].shape[0]
    tr = _pick(S, (512, 256, 128))
    nx = len(xs)

    def body(*refs):
        x_refs, c_ref, s_ref = refs[:nx], refs[nx], refs[nx + 1]
        e_ref = refs[nx + 2] if extra is not None else None
        o_ref = refs[-1]
        cv, sv = c_ref[...], s_ref[...] * sign
        lane = lax.broadcasted_iota(jnp.int32, (tr, 128), 1)
        first = (lane & (HD - 1)) < (ROPE_DIM // 2)
        for j in range(2):
            cs = slice(j * 128, (j + 1) * 128)
            xv = x_refs[0][:, cs].astype(F32)
            for r in x_refs[1:]:
                xv = xv + r[:, cs].astype(F32)
            rot = jnp.where(first, -pltpu.roll(xv, 128 - ROPE_DIM // 2, axis=1), pltpu.roll(xv, ROPE_DIM // 2, axis=1))
            out = xv * cv + rot * sv
            if extra is not None:
                out = out + e_ref[:, cs].astype(F32)
            o_ref[:, cs] = out.astype(out_dtype)

    t128 = pl.BlockSpec((tr, 128), lambda i, j: (i, 0))
    oblk = pl.BlockSpec((tr, 256), lambda i, j: (i, j))
    specs = [pl.BlockSpec((tr, 256), lambda i, j: (i, blk0 + j))] * nx + [t128, t128]
    ins = list(xs) + [cos, sin]
    if extra is not None:
        ins.append(extra[0])
        eb = extra[1]
        specs.append(pl.BlockSpec((tr, 256), lambda i, j: (i, eb + j)))
    return pl.pallas_call(
        body, name=name, grid=(S // tr, width // 256), in_specs=specs, out_specs=oblk,
        out_shape=jax.ShapeDtypeStruct((S, width), out_dtype), compiler_params=_cp(("parallel", "parallel")))(*ins)


def _compress_fwd(R, pe, w1, w2):
    NC = R.shape[1]
    half = 16 * HD

    def body(r_ref, pe_ref, w1_ref, w2_ref, o_ref, hid_ref):
        r = r_ref[0]
        a = _dot(r + pe_ref[:, 0:half], w1_ref[0:half, :], "nn")
        b = _dot(r + pe_ref[:, half:2 * half], w1_ref[half:2 * half, :], "nn")
        hid = a + pltpu.roll(b, NC - 1, axis=0)
        hid_ref[0] = hid
        out = _dot(hid * _sigmoid(hid), w2_ref[...], "nn")
        rows = lax.broadcasted_iota(jnp.int32, out.shape, 0)
        o_ref[0] = jnp.where(rows < NC - 1, out, 0.0).astype(o_ref.dtype)

    return pl.pallas_call(
        body, name="compress_fwd", grid=(N_KV,),
        in_specs=[pl.BlockSpec((1, NC, half), lambda h: (h, 0, 0)), pl.BlockSpec((1, 2 * half), lambda h: (0, 0)),
                  pl.BlockSpec((2 * half, CMP_HID), lambda h: (0, 0)), pl.BlockSpec((CMP_HID, HD), lambda h: (0, 0))],
        out_specs=[pl.BlockSpec((1, NC, HD), lambda h: (h, 0, 0)), pl.BlockSpec((1, NC, CMP_HID), lambda h: (h, 0, 0))],
        out_shape=[jax.ShapeDtypeStruct((N_KV, NC, HD), _MXU), jax.ShapeDtypeStruct((N_KV, NC, CMP_HID), F32)],
        compiler_params=_cp(("parallel",)))(R, pe, w1, w2)


def _compress_bwd(R, pe, w1, w2, hid, dout):
    NC = R.shape[1]
    half = 16 * HD

    def body(r_ref, pe_ref, w1_ref, w2_ref, hid_ref, do_ref, dr_ref, dw1_ref, dw2_ref, dpe_ref):
        @pl.when(pl.program_id(0) == 0)
        def _():
            dw1_ref[...] = jnp.zeros_like(dw1_ref)
            dw2_ref[...] = jnp.zeros_like(dw2_ref)
            dpe_ref[...] = jnp.zeros_like(dpe_ref)

        r, hv, do = r_ref[0], hid_ref[0], do_ref[0]
        s = _sigmoid(hv)
        dw2_ref[...] += _dot(hv * s, do, "tn")
        dhid = _dot(do, w2_ref[...], "nt") * (s * (1.0 + hv * (1.0 - s)))
        rows = lax.broadcasted_iota(jnp.int32, dhid.shape, 0)
        dhid = jnp.where(rows < NC - 1, dhid, 0.0)
        dhid_dn = pltpu.roll(dhid, 1, axis=0)
        dw1_ref[0:half, :] += _dot(r + pe_ref[:, 0:half], dhid, "tn")
        dw1_ref[half:2 * half, :] += _dot(r + pe_ref[:, half:2 * half], dhid_dn, "tn")
        dxt = _dot(dhid, w1_ref[0:half, :], "nt")
        dxb = _dot(dhid_dn, w1_ref[half:2 * half, :], "nt")
        dr_ref[0] = dxt + dxb
        dpe_ref[:, 0:half] += jnp.sum(dxt, axis=0, keepdims=True)
        dpe_ref[:, half:2 * half] += jnp.sum(dxb, axis=0, keepdims=True)

    return pl.pallas_call(
        body, name="compress_bwd", grid=(N_KV,),
        in_specs=[pl.BlockSpec((1, NC, half), lambda h: (h, 0, 0)), pl.BlockSpec((1, 2 * half), lambda h: (0, 0)),
                  pl.BlockSpec((2 * half, CMP_HID), lambda h: (0, 0)), pl.BlockSpec((CMP_HID, HD), lambda h: (0, 0)),
                  pl.BlockSpec((1, NC, CMP_HID), lambda h: (h, 0, 0)), pl.BlockSpec((1, NC, HD), lambda h: (h, 0, 0))],
        out_specs=[pl.BlockSpec((1, NC, half), lambda h: (h, 0, 0)), pl.BlockSpec((2 * half, CMP_HID), lambda h: (0, 0)),
                   pl.BlockSpec((CMP_HID, HD), lambda h: (0, 0)), pl.BlockSpec((1, 2 * half), lambda h: (0, 0))],
        out_shape=[jax.ShapeDtypeStruct((N_KV, NC, half), F32), jax.ShapeDtypeStruct((2 * half, CMP_HID), F32),
                   jax.ShapeDtypeStruct((CMP_HID, HD), F32), jax.ShapeDtypeStruct((1, 2 * half), F32)],
        compiler_params=_cp(("arbitrary",)))(R, pe, w1, w2, hid, dout)


def _attn_cfg(S, Sk, mode):
    tq = _pick(S, (256, 128))
    tk = Sk if mode == "cmp" else _pick(Sk, (256, 128))
    return tq, tk


def _kb_range(mode, q0, tq, tk):
    if mode == "cmp":
        return 0, 1
    hi = (q0 + tq - 1) // tk + 1
    if mode == "sel":
        return 0, hi
    return jnp.maximum(q0 - (WINDOW - 1), 0) // tk, hi


def _attn_bias(mode, q0, k0, tq, tk, sel_t):
    k = k0 + lax.broadcasted_iota(jnp.int32, (tk, tq), 0)
    t = q0 + lax.broadcasted_iota(jnp.int32, (tk, tq), 1)
    if mode == "cmp":
        ok = (k * 16 + 31) <= t
    elif mode == "win":
        ok = (k <= t) & ((t - k) < WINDOW)
    else:
        nb = sel_t.shape[0]
        ek = k0 + lax.broadcasted_iota(jnp.int32, (tk, nb), 0)
        eb = lax.broadcasted_iota(jnp.int32, (tk, nb), 1)
        expand = (lax.shift_right_logical(ek, 6) == eb).astype(_MXU)
        chosen = _dot(expand, sel_t, "nn") > 0.5
        ok = (k <= t) & chosen
    bias = jnp.where(ok, 0.0, NEG)
    return jnp.concatenate([bias] * GRP, axis=1), jnp.concatenate([ok.astype(F32)] * GRP, axis=1)


def _stack_heads(ref, tq):
    return jnp.concatenate([ref[:, g * HD:(g + 1) * HD] for g in range(GRP)], axis=0)


def _scaled_queries(q_ref, tq):
    return (_stack_heads(q_ref, tq).astype(F32) * SCALE).astype(_MXU)


def _blocked_t(x, tk):
    n, Sk, d = x.shape
    return x.reshape(n, Sk // tk, tk, d).transpose(0, 1, 3, 2)


def _attn_fwd(q, qcol0, k, v, mode, sel_t, name):
    S, Sk = q.shape[0], k.shape[1]
    tq, tk = _attn_cfg(S, Sk, mode)
    R = GRP * tq
    vt = _blocked_t(v, tk)

    def body(*refs):
        if mode == "sel":
            q_ref, k_ref, vt_ref, sel_ref, o_ref, lse_ref, m_scr, l_scr, acc = refs
        else:
            q_ref, k_ref, vt_ref, o_ref, lse_ref, m_scr, l_scr, acc = refs
        q0 = pl.program_id(1) * tq
        qs = _scaled_queries(q_ref, tq)
        m_scr[...] = jnp.full_like(m_scr, NEG)
        l_scr[...] = jnp.zeros_like(l_scr)
        acc[...] = jnp.zeros_like(acc)
        selv = sel_ref[0].astype(_MXU) if mode == "sel" else None

        def step(kb, carry):
            k0 = pl.multiple_of(kb * tk, tk)
            bias, okf = _attn_bias(mode, q0, k0, tq, tk, selv)
            s = _dot(k_ref[0, pl.ds(k0, tk), :], qs, "nt") + bias
            m_old = m_scr[...]
            m_new = jnp.maximum(m_old, jnp.max(s, axis=0, keepdims=True))
            p = jnp.exp(s - m_new)
            if mode == "cmp":
                p = p * okf
            alpha = jnp.exp(m_old - m_new)
            l_scr[...] = alpha * l_scr[...] + jnp.sum(p, axis=0, keepdims=True)
            acc[...] = alpha * acc[...] + _dot(vt_ref[0, kb], p, "nn")
            m_scr[...] = m_new
            return carry

        lo, hi = _kb_range(mode, q0, tq, tk)
        lax.fori_loop(lo, hi, step, 0)
        l = l_scr[...]
        good = l > 0.0
        o_t = acc[...] * jnp.where(good, 1.0 / jnp.where(good, l, 1.0), 0.0)
        lse = jnp.where(good, m_scr[...] + jnp.log(jnp.where(good, l, 1.0)), -NEG)
        for g in range(GRP):
            o_ref[:, g * HD:(g + 1) * HD] = o_t[:, g * tq:(g + 1) * tq].T
            lse_ref[0, g:g + 1, :] = lse[:, g * tq:(g + 1) * tq]

    ins = [q, k, vt]
    specs = [pl.BlockSpec((tq, GRP * HD), lambda h, i: (i, qcol0 + h)), pl.BlockSpec((1, Sk, HD), lambda h, i: (h, 0, 0)),
             pl.BlockSpec((1, Sk // tk, HD, tk), lambda h, i: (h, 0, 0, 0))]
    if mode == "sel":
        ins.append(sel_t)
        specs.append(pl.BlockSpec((1, sel_t.shape[1], tq), lambda h, i: (h, 0, i)))
    return pl.pallas_call(
        body, name=name, grid=(N_KV, S // tq), in_specs=specs,
        out_specs=[pl.BlockSpec((tq, GRP * HD), lambda h, i: (i, h)), pl.BlockSpec((1, GRP, tq), lambda h, i: (h, 0, i))],
        out_shape=[jax.ShapeDtypeStruct((S, ATT_WIDTH), F32), jax.ShapeDtypeStruct((N_KV, GRP, S), F32)],
        scratch_shapes=[pltpu.VMEM((1, R), F32), pltpu.VMEM((1, R), F32), pltpu.VMEM((HD, R), F32)],
        compiler_params=_cp(("parallel", "arbitrary")))(*ins)


def _attn_bwd(q, qcol0, k, v, o, lse, do, mode, sel_t, name):
    S, Sk = q.shape[0], k.shape[1]
    tq, tk = _attn_cfg(S, Sk, mode)
    R = GRP * tq
    kt = _blocked_t(k, tk)

    def body(*refs):
        if mode == "sel":
            q_ref, k_ref, kt_ref, v_ref, o_ref, lse_ref, do_ref, sel_ref, dq_ref, dk_ref, dv_ref, dq_scr = refs
        else:
            q_ref, k_ref, kt_ref, v_ref, o_ref, lse_ref, do_ref, dq_ref, dk_ref, dv_ref, dq_scr = refs

        @pl.when(pl.program_id(1) == 0)
        def _():
            dk_ref[...] = jnp.zeros_like(dk_ref)
            dv_ref[...] = jnp.zeros_like(dv_ref)

        q0 = pl.program_id(1) * tq
        qs = _scaled_queries(q_ref, tq)
        dos = _stack_heads(do_ref, tq)
        delta = _dot(jnp.ones((8, HD), F32), dos * _stack_heads(o_ref, tq), "nt", hi=True)[0:1, :]
        lsev = jnp.concatenate([lse_ref[0, g:g + 1, :] for g in range(GRP)], axis=1)
        dos = dos.astype(_MXU)
        dq_scr[...] = jnp.zeros_like(dq_scr)
        selv = sel_ref[0].astype(_MXU) if mode == "sel" else None

        def step(kb, carry):
            k0 = pl.multiple_of(kb * tk, tk)
            kv = k_ref[0, pl.ds(k0, tk), :]
            bias, okf = _attn_bias(mode, q0, k0, tq, tk, selv)
            p = jnp.exp(_dot(kv, qs, "nt") + bias - lsev)
            if mode == "cmp":
                p = p * okf
            dp = _dot(v_ref[0, pl.ds(k0, tk), :], dos, "nt")
            ds = p * (dp - delta)
            dq_scr[...] += _dot(kt_ref[0, kb], ds, "nn")
            dk_ref[0, pl.ds(k0, tk), :] += _dot(ds, qs, "nn")
            dv_ref[0, pl.ds(k0, tk), :] += _dot(p, dos, "nn")
            return carry

        lo, hi = _kb_range(mode, q0, tq, tk)
        lax.fori_loop(lo, hi, step, 0)
        for g in range(GRP):
            dq_ref[:, g * HD:(g + 1) * HD] = (dq_scr[:, g * tq:(g + 1) * tq] * SCALE).T

    kv_spec = pl.BlockSpec((1, Sk, HD), lambda h, i: (h, 0, 0))
    qo_spec = pl.BlockSpec((tq, GRP * HD), lambda h, i: (i, h))
    ins = [q, k, kt, v, o, lse, do]
    specs = [pl.BlockSpec((tq, GRP * HD), lambda h, i: (i, qcol0 + h)), kv_spec,
             pl.BlockSpec((1, Sk // tk, HD, tk), lambda h, i: (h, 0, 0, 0)), kv_spec, qo_spec,
             pl.BlockSpec((1, GRP, tq), lambda h, i: (h, 0, i)), qo_spec]
    if mode == "sel":
        ins.append(sel_t)
        specs.append(pl.BlockSpec((1, sel_t.shape[1], tq), lambda h, i: (h, 0, i)))
    return pl.pallas_call(
        body, name=name, grid=(N_KV, S // tq), in_specs=specs, out_specs=[qo_spec, kv_spec, kv_spec],
        out_shape=[jax.ShapeDtypeStruct((S, ATT_WIDTH), F32), jax.ShapeDtypeStruct((N_KV, Sk, HD), F32),
                   jax.ShapeDtypeStruct((N_KV, Sk, HD), F32)],
        scratch_shapes=[pltpu.VMEM((HD, R), F32)],
        compiler_params=_cp(("parallel", "arbitrary")))(*ins)


def _select(q, qcol0, k_cmp, lse):
    S, NC = q.shape[0], k_cmp.shape[1]
    NB = S // SEL_BLOCK
    tq = _pick(S, (256, 128))
    ci = np.arange(NC)[None, :] * 16
    sj = np.arange(NB)[:, None] * SEL_BLOCK
    ov_t = np.clip(np.minimum(ci + 32, sj + SEL_BLOCK) - np.maximum(ci, sj), 0, None) / 32.0
    ov_t[:, NC - 1] = 0.0
    ov_t = jnp.asarray(ov_t, F32)

    def body(q_ref, k_ref, lse_ref, ov_ref, sel_ref):
        q0 = pl.program_id(1) * tq
        bias, okf = _attn_bias("cmp", q0, 0, tq, NC, None)
        lsev = jnp.concatenate([lse_ref[0, g:g + 1, :] for g in range(GRP)], axis=1)
        p = jnp.exp(_dot(k_ref[0], _scaled_queries(q_ref, tq), "nt") + bias - lsev) * okf
        imp4 = _dot(ov_ref[...], p, "nn")
        imp = imp4[:, 0:tq] + imp4[:, tq:2 * tq] + imp4[:, 2 * tq:3 * tq] + imp4[:, 3 * tq:4 * tq]
        blk = lax.broadcasted_iota(jnp.int32, (NB, tq), 0)
        cur = lax.shift_right_logical(q0 + lax.broadcasted_iota(jnp.int32, (NB, tq), 1), 6)
        imp = jnp.where((blk == 0) | (blk == cur) | (blk == cur - 1), FORCE, imp)
        imp = jnp.where(blk <= cur, imp, -1.0)
        rank = jnp.zeros((NB, tq), F32)
        for j in range(NB):
            row = imp[j:j + 1, :]
            ahead = (row > imp) | ((row == imp) & (blk > j))
            rank = rank + ahead.astype(F32)
        sel_ref[0] = ((rank < float(N_SELECT)) & (imp >= 0.0)).astype(F32)

    return pl.pallas_call(
        body, name="select_blocks", grid=(N_KV, S // tq),
        in_specs=[pl.BlockSpec((tq, GRP * HD), lambda h, i: (i, qcol0 + h)), pl.BlockSpec((1, NC, HD), lambda h, i: (h, 0, 0)),
                  pl.BlockSpec((1, GRP, tq), lambda h, i: (h, 0, i)), pl.BlockSpec((NB, NC), lambda h, i: (0, 0))],
        out_specs=pl.BlockSpec((1, NB, tq), lambda h, i: (h, 0, i)),
        out_shape=jax.ShapeDtypeStruct((N_KV, NB, S), F32), compiler_params=_cp(("parallel", "parallel")))(q, k_cmp, lse, ov_t)


GATE_COL0 = SSD_HEADS


def _combine_fwd(o_cmp, o_sel, o_win, proj_small):
    S = o_cmp.shape[0]
    tr = _pick(S, (256, 128))

    def body(oc_ref, os_ref, ow_ref, g_ref, y_ref):
        gate = _sigmoid(g_ref[...])
        for h in range(N_HEADS):
            hs = slice(h * HD, (h + 1) * HD)
            c = GATE_COL0 + 3 * h
            y = gate[:, c:c + 1] * oc_ref[:, hs] + gate[:, c + 1:c + 2] * os_ref[:, hs] + gate[:, c + 2:c + 3] * ow_ref[:, hs]
            y_ref[:, hs] = y.astype(y_ref.dtype)

    row = pl.BlockSpec((tr, ATT_WIDTH), lambda i: (i, 0))
    return pl.pallas_call(
        body, name="combine_fwd", grid=(S // tr,), in_specs=[row, row, row, pl.BlockSpec((tr, W_SMALL), lambda i: (i, 0))],
        out_specs=row, out_shape=jax.ShapeDtypeStruct((S, ATT_WIDTH), _MXU), compiler_params=_cp(("parallel",)))(
            o_cmp, o_sel, o_win, proj_small)


def _combine_bwd(dmixed, o_cmp, o_sel, o_win, proj_small):
    S = o_cmp.shape[0]
    tr = _pick(S, (256, 128))

    def body(dy_ref, oc_ref, os_ref, ow_ref, g_ref, dc_ref, ds_ref, dw_ref, dg_ref):
        gate = _sigmoid(g_ref[...])
        lane = lax.broadcasted_iota(jnp.int32, (1, W_SMALL), 1)
        dg = jnp.zeros((tr, W_SMALL), F32)
        for h in range(N_HEADS):
            hs = slice(h * HD, (h + 1) * HD)
            dy = dy_ref[:, hs].astype(F32)
            for b, (o_ref, d_ref) in enumerate(((oc_ref, dc_ref), (os_ref, ds_ref), (ow_ref, dw_ref))):
                c = GATE_COL0 + 3 * h + b
                gv = gate[:, c:c + 1]
                d_ref[:, hs] = gv * dy
                dgate = jnp.sum(dy * o_ref[:, hs], axis=-1, keepdims=True) * (gv * (1.0 - gv))
                dg = dg + dgate * (lane == c).astype(F32)
        dg_ref[...] = dg

    row = pl.BlockSpec((tr, ATT_WIDTH), lambda i: (i, 0))
    small = pl.BlockSpec((tr, W_SMALL), lambda i: (i, 0))
    return pl.pallas_call(
        body, name="combine_bwd", grid=(S // tr,),
        in_specs=[pl.BlockSpec((tr, ATT_WIDTH), lambda i: (i, 1)), row, row, row, small], out_specs=[row, row, row, small],
        out_shape=[jax.ShapeDtypeStruct((S, ATT_WIDTH), F32)] * 3 + [jax.ShapeDtypeStruct((S, W_SMALL), F32)],
        compiler_params=_cp(("parallel",)))(dmixed, o_cmp, o_sel, o_win, proj_small)


def _heads_major(x):
    S = x.shape[0]
    return x.reshape(S, N_KV, HD).transpose(1, 0, 2)


def _tokens_major(x):
    return x.transpose(1, 0, 2).reshape(x.shape[1], N_KV * HD)


def _to_rows16(x):
    S = x.shape[0]
    return x.reshape(S // 16, 16, N_KV, HD).transpose(2, 0, 1, 3).reshape(N_KV, S // 16, 16 * HD)


def _from_rows16(r):
    NC = r.shape[1]
    return r.reshape(N_KV, NC, 16, HD).transpose(1, 2, 0, 3).reshape(NC * 16, N_KV * HD)


DT_COL0 = SSD_WIDTH + CONV_CH
GATE_IN_COL0 = D_IN - 3 * N_HEADS


def _split_w_in(w):
    main = jnp.concatenate([w[:, :DT_COL0], w[:, DT_COL0 + SSD_HEADS:GATE_IN_COL0]], axis=1)
    small = jnp.concatenate([w[:, DT_COL0:DT_COL0 + SSD_HEADS], w[:, GATE_IN_COL0:],
                             jnp.zeros((w.shape[0], W_SMALL - SSD_HEADS - 3 * N_HEADS), w.dtype)], axis=1)
    return main, small


def _merge_w_in(main, small):
    return jnp.concatenate([main[:, :DT_COL0], small[:, :SSD_HEADS].astype(main.dtype), main[:, DT_COL0:],
                            small[:, SSD_HEADS:SSD_HEADS + 3 * N_HEADS].astype(main.dtype)], axis=1)


QB, KCB, VCB, KSB, VSB, KWB, VWB = 10, 14, 15, 16, 17, 18, 19


def _col256(a, b):
    return a[:, b * 256:(b + 1) * 256]


_EARLY = ["w_in", "cmp_w1_k", "cmp_w1_v"]
_LATE = ["w_out", "w_gate", "w_up", "w_down"]
_FFN = ["w_down", "w_gate", "w_up"]
_MID = ["w_out"]
_LAST = ["cmp_w1_k", "cmp_w1_v", "w_in"]


def _local_step(x, tgt, p, early_weights=None, late_weights=None, grads_ready=None):
    S = x.shape[0]
    cos, sin = _rope_tables(S)

    u, rs1 = _rms_fwd(x, p["attn_norm_w"], "attn_norm")
    if early_weights is not None:
        p = {**p, **early_weights(u)}
    proj = _mm(u, p["w_main"], "nn", F32, "in_proj")
    proj_small = _mm(u, p["w_small"], "nn", F32, "in_proj_small")
    xa = _conv_fwd(proj, p["conv_w"], p["conv_b"])
    y_ssd, y_pre, rs_ssd, hs = _ssd_fwd(proj, proj_small, xa, p["dt_bias"], p["a_log"], p["d_skip"], p["ssd_norm_w"])

    q_rot = _rope([proj], QB, ATT_WIDTH, cos, sin, 1.0, _MXU, "rope_q")
    ks_rot = _heads_major(_rope([proj], KSB, 256, cos, sin, 1.0, _MXU, "rope_ks"))
    kw_rot = _heads_major(_rope([proj], KWB, 256, cos, sin, 1.0, _MXU, "rope_kw"))
    vs = _heads_major(_col256(proj, VSB).astype(_MXU))
    vw = _heads_major(_col256(proj, VWB).astype(_MXU))
    rk, rv = _to_rows16(_col256(proj, KCB)), _to_rows16(_col256(proj, VCB))
    k_cmp, hid_k = _compress_fwd(rk, p["cmp_pe_k"], p["cmp_w1_k"], p["cmp_w2_k"])
    v_cmp, hid_v = _compress_fwd(rv, p["cmp_pe_v"], p["cmp_w1_v"], p["cmp_w2_v"])

    o_cmp, lse_cmp = _attn_fwd(proj, QB, k_cmp, v_cmp, "cmp", None, "attn_cmp_fwd")
    sel = _select(proj, QB, k_cmp, lse_cmp)
    o_sel, lse_sel = _attn_fwd(q_rot, 0, ks_rot, vs, "sel", sel, "attn_sel_fwd")
    o_win, lse_win = _attn_fwd(q_rot, 0, kw_rot, vw, "win", None, "attn_win_fwd")
    y_att = _combine_fwd(o_cmp, o_sel, o_win, proj_small)

    if late_weights is not None:
        p = {**p, **late_weights(y_att)}
    mixed = jnp.concatenate([y_ssd, y_att], axis=1)
    h1 = _mm(mixed, p["w_out"], "nn", F32, "out_proj", res=x)
    v, rs_ffn = _rms_fwd(h1, p["ffn_norm_w"], "ffn_norm")
    gt, up, act = _ffn_up(v, p["w_gate"], p["w_up"])
    h2 = _mm(act, p["w_down"], "nn", F32, "ffn_down", res=h1)
    loss, dh2, dh2b, d_final_w = _final_loss(h2, p["final_norm_w"], tgt)

    def ready(names, nxt):
        if grads_ready is None:
            return nxt
        return lax.optimization_barrier((nxt, grads_ready(names, g)))[0]

    g = {"final_norm_w": d_final_w}
    g["w_down"] = _mm(act, dh2b, "tn", _MXU, "dw_down")
    dgt, dup = _ffn_dact(dh2b, p["w_down"], gt, up)
    g["w_gate"] = _mm(v, dgt, "tn", _MXU, "dw_gate")
    g["w_up"] = _mm(v, dup, "tn", _MXU, "dw_up")
    dgt = ready(_FFN, dgt)
    dv = _mm(dgt, p["w_gate"], "nt", F32, "dv_gate")
    dv = _mm(dup, p["w_up"], "nt", F32, "dv_up", res=dv)
    dh1, dh1b, g["ffn_norm_w"] = _rms_bwd(dv, h1, rs_ffn, p["ffn_norm_w"], dh2, "ffn_norm_bwd")
    g["w_out"] = _mm(mixed, dh1b, "tn", _MXU, "dw_out")
    dmixed = _mm(ready(_MID, dh1b), p["w_out"], "nt", F32, "dmixed")

    dz, dxa, ddtr, g["dt_bias"], g["a_log"], g["d_skip"], g["ssd_norm_w"] = _ssd_bwd(
        dmixed, proj, proj_small, xa, y_pre, rs_ssd, hs, p["dt_bias"], p["a_log"], p["d_skip"], p["ssd_norm_w"])
    dxbc, g["conv_w"], g["conv_b"] = _conv_bwd(proj, p["conv_w"], p["conv_b"], dxa)

    do_cmp, do_sel, do_win, dgate = _combine_bwd(dmixed, o_cmp, o_sel, o_win, proj_small)
    dq_cmp, dk_cmp, dv_cmp = _attn_bwd(proj, QB, k_cmp, v_cmp, o_cmp, lse_cmp, do_cmp, "cmp", None, "attn_cmp_bwd")
    dq_sel, dks, dvs = _attn_bwd(q_rot, 0, ks_rot, vs, o_sel, lse_sel, do_sel, "sel", sel, "attn_sel_bwd")
    dq_win, dkw, dvw = _attn_bwd(q_rot, 0, kw_rot, vw, o_win, lse_win, do_win, "win", None, "attn_win_bwd")
    drk, g["cmp_w1_k"], g["cmp_w2_k"], g["cmp_pe_k"] = _compress_bwd(rk, p["cmp_pe_k"], p["cmp_w1_k"], p["cmp_w2_k"], hid_k, dk_cmp)
    drv, g["cmp_w1_v"], g["cmp_w2_v"], g["cmp_pe_v"] = _compress_bwd(rv, p["cmp_pe_v"], p["cmp_w1_v"], p["cmp_w2_v"], hid_v, dv_cmp)
    dq = _rope([dq_sel, dq_win], 0, ATT_WIDTH, cos, sin, -1.0, _MXU, "rope_dq", extra=(dq_cmp, 0))
    dks_t = _rope([_tokens_major(dks)], 0, 256, cos, sin, -1.0, _MXU, "rope_dks")
    dkw_t = _rope([_tokens_major(dkw)], 0, 256, cos, sin, -1.0, _MXU, "rope_dkw")
    dproj = jnp.concatenate(
        [dz, dxbc, dq] + [t.astype(_MXU) for t in (_from_rows16(drk), _from_rows16(drv))]
        + [dks_t, _tokens_major(dvs).astype(_MXU), dkw_t, _tokens_major(dvw).astype(_MXU)], axis=1)
    dsmall = jnp.concatenate([ddtr, dgate[:, GATE_COL0:GATE_COL0 + 3 * N_HEADS],
                              jnp.zeros((S, W_SMALL - SSD_HEADS - 3 * N_HEADS), F32)], axis=1).astype(_MXU)
    g["w_main"] = _mm(u, dproj, "tn", _MXU, "dw_in")
    g["w_small"] = _mm(u, dsmall, "tn", F32, "dw_in_small")
    du = _mm(ready(_LAST, dproj), p["w_main"], "nt", F32, "du_main")
    du = _mm(dsmall, p["w_small"], "nt", F32, "du_small", res=du)
    grad_x, _, g["attn_norm_w"] = _rms_bwd(du, x, rs1, p["attn_norm_w"], dh1, "attn_norm_bwd")
    return loss, grad_x, g


MESH_ID = pl.DeviceIdType.MESH


def _my_coords():
    return lax.axis_index("x"), lax.axis_index("y"), lax.axis_index("c")


def _flat_id(px, py, pc):
    return 4 * px + 2 * py + pc


def _peer(k):
    mx, my, mc = _my_coords()
    return (1 - mx if k & 4 else mx, 1 - my if k & 2 else my, 1 - mc if k & 1 else mc)


def _exchange(arrs, scatter, name):
    n = len(arrs)

    def body(*refs):
        ins, outs = refs[:n], refs[n:2 * n]
        send_sems, recv_sems, local_sems = refs[2 * n:]
        me = _flat_id(*_my_coords())
        copies = []
        for i in range(n):
            src_me = ins[i].at[me] if scatter else ins[i]
            local = pltpu.make_async_copy(src_me, outs[i].at[me], local_sems.at[i])
            local.start()
            copies.append(local)
        for k in range(1, N_DEV):
            peer = _peer(k)
            for i in range(n):
                src = ins[i].at[_flat_id(*peer)] if scatter else ins[i]
                cp = pltpu.make_async_remote_copy(src_ref=src, dst_ref=outs[i].at[me], send_sem=send_sems.at[i * 7 + k - 1],
                                                  recv_sem=recv_sems.at[i * 7 + k - 1], device_id=peer, device_id_type=MESH_ID)
                cp.start()
                copies.append(cp)
        for cp in copies:
            cp.wait()

    any_spec = pl.BlockSpec(memory_space=pl.ANY)
    out_shape = [jax.ShapeDtypeStruct(a.shape if scatter else (N_DEV,) + a.shape, a.dtype) for a in arrs]
    return pl.pallas_call(
        body, name=name, in_specs=[any_spec] * n, out_specs=[any_spec] * n, out_shape=out_shape,
        scratch_shapes=[pltpu.SemaphoreType.DMA((n * 7,)), pltpu.SemaphoreType.DMA((n * 7,)), pltpu.SemaphoreType.DMA((n,))],
        compiler_params=pltpu.CompilerParams(has_side_effects=True))(*arrs)


_HBM = pl.BlockSpec(memory_space=pltpu.HBM)
_SEM = pl.BlockSpec(memory_space=pltpu.SEMAPHORE)
_EFFECT = pltpu.SideEffectType.DATAFLOW_SIDE_EFFECTING


def _split_copies(ins, lands, send_sems, recv_sems, scatter):
    me = _flat_id(*_my_coords())
    out = []
    for k in range(1, N_DEV):
        peer = _peer(k)
        for i in range(len(ins)):
            src = ins[i].at[_flat_id(*peer)] if scatter else ins[i]
            out.append(pltpu.make_async_remote_copy(src_ref=src, dst_ref=lands[i].at[me], send_sem=send_sems.at[i * 7 + k - 1],
                                                    recv_sem=recv_sems.at[i * 7 + k - 1], device_id=peer, device_id_type=MESH_ID))
    return out


def _split_start(arrs, scatter, name):
    n = len(arrs)

    def body(*refs):
        for cp in _split_copies(refs[:n], refs[n:2 * n], refs[2 * n], refs[2 * n + 1], scatter):
            cp.start()
        refs[-1][...] = jnp.zeros_like(refs[-1])

    land_shapes = [a.shape if scatter else (N_DEV,) + a.shape for a in arrs]
    out_shape = ((pltpu.SemaphoreType.DMA((n * 7,)), pltpu.SemaphoreType.DMA((n * 7,)))
                 + tuple(pltpu.HBM(a.shape, a.dtype) for a in arrs) + tuple(pltpu.HBM(s, a.dtype) for s, a in zip(land_shapes, arrs))
                 + (jax.ShapeDtypeStruct((8, 128), F32),))
    operands = ([pltpu.with_memory_space_constraint(a, pltpu.HBM) for a in arrs]
                + [pltpu.with_memory_space_constraint(lax.empty(s, a.dtype), pltpu.HBM) for s, a in zip(land_shapes, arrs)])
    res = pl.pallas_call(
        body, name=name, out_shape=out_shape, in_specs=[_HBM] * (2 * n),
        out_specs=(_SEM, _SEM) + (_HBM,) * (2 * n) + (pl.BlockSpec(memory_space=pltpu.VMEM),),
        input_output_aliases={i: 2 + i for i in range(2 * n)},
        compiler_params=pltpu.CompilerParams(has_side_effects=_EFFECT))(*operands)
    return dict(send=res[0], recv=res[1], ins=list(res[2:2 + n]), lands=list(res[2 + n:2 + 2 * n]), token=res[-1])


def _split_wait(st, scatter, after, name):
    n = len(st["ins"])

    def body(*refs):
        for cp in _split_copies(refs[:n], refs[n:2 * n], refs[2 * n], refs[2 * n + 1], scatter):
            cp.wait_send()
            cp.wait_recv()

    arrs = st["ins"] + st["lands"]
    res = pl.pallas_call(
        body, name=name, out_shape=tuple(pltpu.HBM(a.shape, a.dtype) for a in arrs),
        in_specs=[_HBM] * (2 * n) + [_SEM, _SEM, pl.BlockSpec(memory_space=pl.ANY)], out_specs=(_HBM,) * (2 * n),
        input_output_aliases={i: i for i in range(2 * n)},
        compiler_params=pltpu.CompilerParams(has_side_effects=_EFFECT))(*arrs, st["send"], st["recv"], after)
    me = _flat_id(*_my_coords())
    out = []
    for src, land in zip(res[:n], res[n:]):
        own = lax.dynamic_index_in_dim(src, me, 0, keepdims=True) if scatter else src[None]
        out.append(lax.dynamic_update_slice_in_dim(land, own, me, 0))
    return out


def _sum_parts(parts):
    P, R, C = parts.shape
    tr = _pick(R, (136, 8))

    def body(p_ref, o_ref):
        acc = p_ref[0]
        for j in range(1, P):
            acc = acc + p_ref[j]
        o_ref[...] = acc

    return pl.pallas_call(
        body, name="sum_small_grads", grid=(R // tr,), in_specs=[pl.BlockSpec((P, tr, C), lambda i: (0, i, 0))],
        out_specs=pl.BlockSpec((tr, C), lambda i: (i, 0)), out_shape=jax.ShapeDtypeStruct((R, C), F32),
        compiler_params=_cp(("parallel",)))(parts)


def _adam_sum(parts, w, m, v, name):
    P, R, C = parts.shape
    tr = _pick(R, (256, 128, 64, 32, 8)) if C <= 1024 else _pick(R, (128, 64, 32, 8))

    def body(p_ref, w_ref, m_ref, v_ref, g_ref, d_ref, nm_ref, nv_ref):
        g = p_ref[0].astype(F32)
        for j in range(1, P):
            g = g + p_ref[j].astype(F32)
        g_ref[...] = g
        nm = ADAM_B1 * m_ref[...] + (1.0 - ADAM_B1) * g
        nv = ADAM_B2 * v_ref[...] + (1.0 - ADAM_B2) * (g * g)
        nm_ref[...] = nm
        nv_ref[...] = nv
        m_hat = nm / (1.0 - ADAM_B1 ** ADAM_STEP)
        v_hat = nv / (1.0 - ADAM_B2 ** ADAM_STEP)
        d_ref[...] = -ADAM_LR * (m_hat / (jnp.sqrt(v_hat) + ADAM_EPS) + ADAM_WD * w_ref[...])

    blk = pl.BlockSpec((tr, C), lambda i: (i, 0))
    return pl.pallas_call(
        body, name=name, grid=(R // tr,), in_specs=[pl.BlockSpec((P, tr, C), lambda i: (0, i, 0)), blk, blk, blk],
        out_specs=[blk] * 4, out_shape=[jax.ShapeDtypeStruct((R, C), F32)] * 4, compiler_params=_cp(("parallel",)))(parts, w, m, v)


def _pack(arrs):
    rows = []
    for a in arrs:
        f = a.reshape(-1).astype(F32)
        f = jnp.pad(f, (0, (-f.shape[0]) % 128))
        rows.append(f.reshape(-1, 128))
    out = jnp.concatenate(rows, axis=0)
    return jnp.pad(out, ((0, (-out.shape[0]) % 8), (0, 0)))


def _unpack(pack, shapes):
    out, r = [], 0
    for s in shapes:
        n = int(np.prod(s))
        nr = -(-n // 128)
        out.append(pack[r:r + nr].reshape(-1)[:n].reshape(s))
        r += nr
    return out


_WEIGHTS = ["attn_norm_w", "w_in", "conv_w", "conv_b", "dt_bias", "a_log", "d_skip", "ssd_norm_w", "cmp_w1_k", "cmp_w2_k",
            "cmp_w1_v", "cmp_w2_v", "cmp_pe_k", "cmp_pe_v", "w_out", "ffn_norm_w", "w_gate", "w_up", "w_down", "final_norm_w"]
_BIG = ["w_in", "w_gate", "w_up", "w_down", "w_out", "cmp_w1_k", "cmp_w1_v"]
_COL_SHARDED = ("w_in", "w_gate", "w_up")
_REPLICATED = ["attn_norm_w", "conv_b", "dt_bias", "a_log", "d_skip", "ssd_norm_w", "cmp_pe_k", "cmp_pe_v", "ffn_norm_w",
               "final_norm_w"]
_SMALL_SHARDED = ["conv_w", "cmp_w2_k", "cmp_w2_v"]
_SMALL_FULL_SHAPES = {"attn_norm_w": (1, D_MODEL), "conv_b": (1, CONV_CH), "dt_bias": (1, SSD_HEADS), "a_log": (1, SSD_HEADS),
                      "d_skip": (1, SSD_HEADS), "ssd_norm_w": (1, SSD_WIDTH), "cmp_pe_k": (1, 32 * HD), "cmp_pe_v": (1, 32 * HD),
                      "ffn_norm_w": (1, D_MODEL), "final_norm_w": (1, D_MODEL), "conv_w": (CONV_K, CONV_CH),
                      "cmp_w2_k": (CMP_HID, HD), "cmp_w2_v": (CMP_HID, HD)}


def _cols_to_slabs(g):
    R = g.shape[0]
    return g.reshape(R, N_DEV, -1).transpose(1, 0, 2)


def _slabs_to_cols(s):
    return s.transpose(1, 0, 2).reshape(s.shape[1], -1)


def kernel(x, attn_norm_w, w_in, conv_w, conv_b, dt_bias, a_log, d_skip, ssd_norm_w, cmp_w1_k, cmp_w2_k, cmp_w1_v, cmp_w2_v, cmp_pe_k, cmp_pe_v, w_out, ffn_norm_w, w_gate, w_up, w_down, final_norm_w, loss_target, m_attn_norm_w, m_w_in, m_conv_w, m_conv_b, m_dt_bias, m_a_log, m_d_skip, m_ssd_norm_w, m_cmp_w1_k, m_cmp_w2_k, m_cmp_w1_v, m_cmp_w2_v, m_cmp_pe_k, m_cmp_pe_v, m_w_out, m_ffn_norm_w, m_w_gate, m_w_up, m_w_down, m_final_norm_w, v_attn_norm_w, v_w_in, v_conv_w, v_conv_b, v_dt_bias, v_a_log, v_d_skip, v_ssd_norm_w, v_cmp_w1_k, v_cmp_w2_k, v_cmp_w1_v, v_cmp_w2_v, v_cmp_pe_k, v_cmp_pe_v, v_w_out, v_ffn_norm_w, v_w_gate, v_w_up, v_w_down, v_final_norm_w):
    a = dict(locals())
    me = _flat_id(*_my_coords())

    shard = {n: a[n][0].astype(_MXU) for n in _BIG}
    early_small = [cmp_w2_k[0], cmp_w2_v[0], conv_w[0]]
    st_early = _split_start([shard[n] for n in _EARLY] + early_small, False, "gather_early_start")
    zero = st_early["token"][0, 0].astype(_MXU)
    st_late = _split_start([shard[_LATE[0]] + zero] + [shard[n] for n in _LATE[1:]], False, "gather_late_start")

    def assemble(n, t):
        return _slabs_to_cols(t) if n in _COL_SHARDED else t.reshape(-1, t.shape[-1])

    p = dict(attn_norm_w=attn_norm_w, conv_b=conv_b, dt_bias=dt_bias, a_log=a_log, d_skip=d_skip, ssd_norm_w=ssd_norm_w,
             cmp_pe_k=cmp_pe_k.reshape(1, -1), cmp_pe_v=cmp_pe_v.reshape(1, -1), ffn_norm_w=ffn_norm_w,
             final_norm_w=final_norm_w.reshape(1, -1))

    def early_weights(after):
        after = lax.optimization_barrier((after, st_late["token"]))[0]
        got = _split_wait(st_early, False, after, "gather_early_wait")
        w_main, w_small = _split_w_in(assemble("w_in", got[0]))
        return dict(w_main=w_main, w_small=w_small, cmp_w1_k=assemble("cmp_w1_k", got[1]), cmp_w1_v=assemble("cmp_w1_v", got[2]),
                    cmp_w2_k=assemble("cmp_w2_k", got[3]).astype(_MXU), cmp_w2_v=assemble("cmp_w2_v", got[4]).astype(_MXU),
                    conv_w=_slabs_to_cols(got[5]))

    def late_weights(after):
        got_late = _split_wait(st_late, False, after, "gather_late_wait")
        return {n: assemble(n, t) for n, t in zip(_LATE, got_late)}

    def slabs_of(g, names):
        return [(_cols_to_slabs(g[n]) if n in _COL_SHARDED else g[n].reshape(N_DEV, -1, g[n].shape[-1])).astype(_MXU) for n in names]

    started = []

    def grads_ready(names, g):
        if "w_in" in names:
            g = {**g, "w_in": _merge_w_in(g["w_main"], g["w_small"])}
        started.append((names, _split_start(slabs_of(g, names), True, "scatter_grads_start_%d" % len(started))))
        return started[-1][1]["token"][0:1, 0:1]

    loss_part, grad_x, g = _local_step(x[0], loss_target[0], p, early_weights, late_weights, grads_ready)
    loss = lax.psum(loss_part[0, 0], ("x", "y", "c"))

    out, after = {}, g["w_main"]
    for i, (names, st) in enumerate(started):
        if i == len(started) - 1:
            after = grad_x[0:1, 0:LANE] + after[0:1, 0:LANE]
        received = _split_wait(st, True, after, "scatter_grads_wait_%d" % i)
        for n, parts in zip(names, received):
            out[n] = _adam_sum(parts, a[n][0], a["m_" + n][0], a["v_" + n][0], "adam_" + n)
        after = out[names[-1]][0]

    small_names = _REPLICATED + _SMALL_SHARDED
    g_small = _pack([g[n] for n in small_names])
    g_sum = _sum_parts(_exchange([g_small], False, "gather_small_grads")[0])
    gs = dict(zip(small_names, _unpack(g_sum, [_SMALL_FULL_SHAPES[n] for n in small_names])))
    gs["conv_w"] = lax.dynamic_slice_in_dim(gs["conv_w"], me * conv_w.shape[2], conv_w.shape[2], axis=1)
    gs["cmp_w2_k"] = lax.dynamic_slice_in_dim(gs["cmp_w2_k"], me * cmp_w2_k.shape[1], cmp_w2_k.shape[1], axis=0)
    gs["cmp_w2_v"] = lax.dynamic_slice_in_dim(gs["cmp_w2_v"], me * cmp_w2_v.shape[1], cmp_w2_v.shape[1], axis=0)
    packs = [_pack([t[n] for n in small_names]) for t in
             (gs, a, {n: a["m_" + n] for n in small_names}, {n: a["v_" + n] for n in small_names})]
    res_small = _adam_sum(packs[0][None], packs[1], packs[2], packs[3], "adam_small")
    shapes = [a[n].shape for n in small_names]
    unpacked = [dict(zip(small_names, _unpack(r, shapes))) for r in res_small]
    for n in small_names:
        out[n] = tuple(u[n] for u in unpacked)

    outs = [loss, grad_x[None]]
    for j in range(4):
        for n in _WEIGHTS:
            outs.append(out[n][j].reshape(a[n].shape))
    return tuple(outs)
```

```python
import functools
import math

import numpy as np
import jax
import jax.numpy as jnp
from jax import lax
from jax.experimental import pallas as pl
from jax.experimental.pallas import tpu as pltpu

F32 = jnp.float32
_MXU = jnp.bfloat16
_HI = lax.Precision.HIGHEST

N_DEV = 8
D_MODEL = 2048
SSD_WIDTH = 1024
ATT_WIDTH = 1024
SSD_HEADS = 16
SSD_P = 64
SSD_N = 128
SSD_L = 128
SSD_G = 2
CONV_CH = 1536
CONV_K = 4
HD = 64
N_HEADS = 16
N_KV = 4
GRP = 4
CMP_HID = 256
SEL_BLOCK = 64
N_SELECT = 16
WINDOW = 512
ROPE_DIM = 16
ROPE_THETA = 500000.0
D_FF = 5632
EPS = 1e-6
NEG = -1e30
FORCE = 1e4
SCALE = HD ** -0.5
D_IN = 5184
W_MAIN = 5120
W_SMALL = 128
VMEM_LIMIT = 52 * 1024 * 1024

ADAM_LR, ADAM_B1, ADAM_B2, ADAM_EPS, ADAM_WD, ADAM_STEP = 0.001, 0.9, 0.999, 1e-08, 0.01, 10


def _pick(n, cands):
    for c in cands:
        if n % c == 0:
            return c
    return n


def _cp(sem=None):
    return pltpu.CompilerParams(dimension_semantics=sem, vmem_limit_bytes=VMEM_LIMIT)


def _sigmoid(x):
    return 1.0 / (1.0 + jnp.exp(-x))


def _dot(a, b, dims, hi=False):
    dn = {"nn": (((1,), (0,)), ((), ())), "nt": (((1,), (1,)), ((), ())), "tn": (((0,), (0,)), ((), ()))}[dims]
    if hi:
        return lax.dot_general(a.astype(F32), b.astype(F32), dn, precision=_HI, preferred_element_type=F32)
    return lax.dot_general(a.astype(_MXU), b.astype(_MXU), dn, preferred_element_type=F32)


LANE = 128
MM_TILE = 1024
MM_K_WHOLE = 2048
MM_K_STEP = 1536
TN_ACC_ELEMS = 3 * 2 ** 20
TN_K_STEP = 512


def _largest_tile(n, cap):
    if n <= cap:
        return n
    best = LANE
    for t in range(LANE, cap + 1, LANE):
        if n % t == 0:
            best = t
    return best


def _mm_tiles(mode, M, N, K):
    if mode == "tn":
        tm = _largest_tile(M, 2 * MM_TILE)
        return tm, _largest_tile(N, TN_ACC_ELEMS // tm), _largest_tile(K, TN_K_STEP)
    tk = K if K <= MM_K_WHOLE else _largest_tile(K, MM_K_STEP)
    return _largest_tile(M, MM_TILE), _largest_tile(N, MM_TILE), tk


def _mm(a, b, mode, out_dtype, name, res=None, after=None):
    if mode == "nn":
        (M, K), N = a.shape, b.shape[1]
    elif mode == "nt":
        (M, K), N = a.shape, b.shape[0]
    else:
        (K, M), N = a.shape, b.shape[1]
    tm, tn, tk = _mm_tiles(mode, M, N, K)
    nk = K // tk
    a_spec = pl.BlockSpec((tk, tm), lambda i, j, k: (k, i)) if mode == "tn" else pl.BlockSpec((tm, tk), lambda i, j, k: (i, k))
    b_spec = pl.BlockSpec((tn, tk), lambda i, j, k: (j, k)) if mode == "nt" else pl.BlockSpec((tk, tn), lambda i, j, k: (k, j))
    o_spec = pl.BlockSpec((tm, tn), lambda i, j, k: (i, j))

    def finish(r, r_ref, o_ref):
        if res is not None:
            r = r + r_ref[...].astype(F32)
        o_ref[...] = r.astype(out_dtype)

    def body_one_step(*refs):
        a_ref, b_ref, o_ref = refs[0], refs[1], refs[-1]
        finish(_dot(a_ref[...], b_ref[...], mode), refs[2], o_ref)

    def body(*refs):
        a_ref, b_ref, o_ref, acc = refs[0], refs[1], refs[-2], refs[-1]
        k = pl.program_id(2)

        @pl.when(k == 0)
        def _():
            acc[...] = jnp.zeros_like(acc)

        acc[...] += _dot(a_ref[...], b_ref[...], mode)

        @pl.when(k == nk - 1)
        def _():
            finish(acc[...], refs[2], o_ref)

    ins, specs = [a, b], [a_spec, b_spec]
    if res is not None:
        ins.append(res)
        specs.append(o_spec)
    if after is not None:
        ins.append(after)
        specs.append(pl.BlockSpec(memory_space=pl.ANY))
    return pl.pallas_call(
        body_one_step if nk == 1 else body, name=name, grid=(M // tm, N // tn, nk), in_specs=specs, out_specs=o_spec,
        out_shape=jax.ShapeDtypeStruct((M, N), out_dtype), scratch_shapes=[] if nk == 1 else [pltpu.VMEM((tm, tn), F32)],
        compiler_params=_cp(("parallel", "parallel", "arbitrary")))(*ins)


def _ffn_up(v, w_gate, w_up):
    S, D = v.shape
    F = w_gate.shape[1]
    tm, tn = _largest_tile(S, MM_TILE), _largest_tile(F, MM_TILE // 2)

    def body(v_ref, wg_ref, wu_ref, gt_ref, up_ref, act_ref):
        vv = v_ref[...]
        g = _dot(vv, wg_ref[...], "nn")
        u = _dot(vv, wu_ref[...], "nn")
        gt_ref[...] = g
        up_ref[...] = u
        act_ref[...] = (g * _sigmoid(g) * u).astype(act_ref.dtype)

    o_spec = pl.BlockSpec((tm, tn), lambda i, j: (i, j))
    w_spec = pl.BlockSpec((D, tn), lambda i, j: (0, j))
    return pl.pallas_call(
        body, name="ffn_up", grid=(S // tm, F // tn),
        in_specs=[pl.BlockSpec((tm, D), lambda i, j: (i, 0)), w_spec, w_spec], out_specs=[o_spec, o_spec, o_spec],
        out_shape=[jax.ShapeDtypeStruct((S, F), F32), jax.ShapeDtypeStruct((S, F), F32), jax.ShapeDtypeStruct((S, F), _MXU)],
        compiler_params=_cp(("parallel", "parallel")))(v, w_gate, w_up)


def _ffn_dact(dh2, w_down, gt, up):
    S, D = dh2.shape
    F = w_down.shape[0]
    tm, tn = _largest_tile(S, MM_TILE), _largest_tile(F, MM_TILE // 2)

    def body(d_ref, w_ref, gt_ref, up_ref, dg_ref, du_ref):
        da, g, u = _dot(d_ref[...], w_ref[...], "nt"), gt_ref[...], up_ref[...]
        s = _sigmoid(g)
        dg_ref[...] = (da * u * (s * (1.0 + g * (1.0 - s)))).astype(dg_ref.dtype)
        du_ref[...] = (da * (g * s)).astype(du_ref.dtype)

    o_spec = pl.BlockSpec((tm, tn), lambda i, j: (i, j))
    return pl.pallas_call(
        body, name="ffn_dact", grid=(S // tm, F // tn),
        in_specs=[pl.BlockSpec((tm, D), lambda i, j: (i, 0)), pl.BlockSpec((tn, D), lambda i, j: (j, 0)), o_spec, o_spec],
        out_specs=[o_spec, o_spec],
        out_shape=[jax.ShapeDtypeStruct((S, F), _MXU), jax.ShapeDtypeStruct((S, F), _MXU)],
        compiler_params=_cp(("parallel", "parallel")))(dh2, w_down, gt, up)


def _rms_fwd(x, w, name):
    S, D = x.shape
    tr = _pick(S, (256, 128))

    def body(x_ref, w_ref, xn_ref, rs_ref):
        xv = x_ref[...]
        rs = lax.rsqrt(jnp.mean(xv * xv, axis=-1, keepdims=True) + EPS)
        xn_ref[...] = ((xv * rs) * w_ref[...]).astype(xn_ref.dtype)
        rs_ref[...] = rs

    return pl.pallas_call(
        body, name=name, grid=(S // tr,),
        in_specs=[pl.BlockSpec((tr, D), lambda i: (i, 0)), pl.BlockSpec((1, D), lambda i: (0, 0))],
        out_specs=[pl.BlockSpec((tr, D), lambda i: (i, 0)), pl.BlockSpec((tr, 1), lambda i: (i, 0))],
        out_shape=[jax.ShapeDtypeStruct((S, D), _MXU), jax.ShapeDtypeStruct((S, 1), F32)],
        compiler_params=_cp(("parallel",)))(x, w)


def _rms_bwd(dyn, x, rs, w, res, name):
    S, D = x.shape
    tr = _pick(S, (256, 128))

    def body(dy_ref, x_ref, rs_ref, w_ref, res_ref, dx_ref, dxb_ref, dw_ref):
        @pl.when(pl.program_id(0) == 0)
        def _():
            dw_ref[...] = jnp.zeros_like(dw_ref)

        dy, r = dy_ref[...].astype(F32), rs_ref[...]
        xhat = x_ref[...] * r
        dw_ref[...] += jnp.sum(dy * xhat, axis=0, keepdims=True)
        dxhat = dy * w_ref[...]
        dx = res_ref[...] + r * (dxhat - xhat * jnp.mean(dxhat * xhat, axis=-1, keepdims=True))
        dx_ref[...] = dx
        dxb_ref[...] = dx.astype(dxb_ref.dtype)

    row = pl.BlockSpec((tr, D), lambda i: (i, 0))
    vec = pl.BlockSpec((1, D), lambda i: (0, 0))
    return pl.pallas_call(
        body, name=name, grid=(S // tr,),
        in_specs=[row, row, pl.BlockSpec((tr, 1), lambda i: (i, 0)), vec, row], out_specs=[row, row, vec],
        out_shape=[jax.ShapeDtypeStruct((S, D), F32), jax.ShapeDtypeStruct((S, D), _MXU), jax.ShapeDtypeStruct((1, D), F32)],
        compiler_params=_cp(("arbitrary",)))(dyn, x, rs, w, res)


def _final_loss(h2, w, tgt):
    S, D = h2.shape
    tr = _pick(S, (256, 128))

    def body(h_ref, w_ref, t_ref, loss_ref, dh_ref, dhb_ref, dw_ref):
        @pl.when(pl.program_id(0) == 0)
        def _():
            dw_ref[...] = jnp.zeros_like(dw_ref)
            loss_ref[...] = jnp.zeros_like(loss_ref)

        hv, wv = h_ref[...], w_ref[...]
        rs = lax.rsqrt(jnp.mean(hv * hv, axis=-1, keepdims=True) + EPS)
        xhat = hv * rs
        err = xhat * wv - t_ref[...]
        row = jnp.mean(err * err, axis=-1, keepdims=True)
        loss_ref[...] += 0.5 * jnp.sum(row, axis=0, keepdims=True)
        dy = err * (1.0 / D)
        dw_ref[...] += jnp.sum(dy * xhat, axis=0, keepdims=True)
        dxhat = dy * wv
        dh = rs * (dxhat - xhat * jnp.mean(dxhat * xhat, axis=-1, keepdims=True))
        dh_ref[...] = dh
        dhb_ref[...] = dh.astype(dhb_ref.dtype)

    row = pl.BlockSpec((tr, D), lambda i: (i, 0))
    vec = pl.BlockSpec((1, D), lambda i: (0, 0))
    return pl.pallas_call(
        body, name="final_loss", grid=(S // tr,), in_specs=[row, vec, row],
        out_specs=[pl.BlockSpec((1, 1), lambda i: (0, 0)), row, row, vec],
        out_shape=[jax.ShapeDtypeStruct((1, 1), F32), jax.ShapeDtypeStruct((S, D), F32), jax.ShapeDtypeStruct((S, D), _MXU),
                   jax.ShapeDtypeStruct((1, D), F32)],
        compiler_params=_cp(("arbitrary",)))(h2, w, tgt)


def _shift_rows(x, k, rows):
    if k == 0:
        return x
    S = x.shape[0]
    r = pltpu.roll(x, k % S, axis=0)
    ok = (rows >= k) if k > 0 else (rows < S + k)
    return jnp.where(ok, r, 0.0)


XBC_COL0 = SSD_WIDTH // 128


def _conv_fwd(proj, conv_w, conv_b):
    S = proj.shape[0]
    nct = CONV_CH // 128

    def body(x_ref, w_ref, b_ref, o_ref):
        x = x_ref[...]
        rows = lax.broadcasted_iota(jnp.int32, x.shape, 0)
        c = b_ref[...] + w_ref[3:4, :] * x
        for k in range(1, CONV_K):
            c = c + w_ref[3 - k:4 - k, :] * _shift_rows(x, k, rows)
        o_ref[...] = c * _sigmoid(c)

    return pl.pallas_call(
        body, name="conv_fwd", grid=(nct,),
        in_specs=[pl.BlockSpec((S, 128), lambda j: (0, XBC_COL0 + j)), pl.BlockSpec((CONV_K, 128), lambda j: (0, j)),
                  pl.BlockSpec((1, 128), lambda j: (0, j))],
        out_specs=pl.BlockSpec((S, 128), lambda j: (0, j)),
        out_shape=jax.ShapeDtypeStruct((S, CONV_CH), F32), compiler_params=_cp(("parallel",)))(proj, conv_w, conv_b)


def _conv_bwd(proj, conv_w, conv_b, dxa):
    S = proj.shape[0]
    nct = CONV_CH // 128

    def body(x_ref, w_ref, b_ref, d_ref, dx_ref, dw_ref, db_ref):
        x = x_ref[...]
        rows = lax.broadcasted_iota(jnp.int32, x.shape, 0)
        xs = [_shift_rows(x, k, rows) for k in range(CONV_K)]
        c = b_ref[...] + w_ref[3:4, :] * x
        for k in range(1, CONV_K):
            c = c + w_ref[3 - k:4 - k, :] * xs[k]
        s = _sigmoid(c)
        dc = d_ref[...] * (s * (1.0 + c * (1.0 - s)))
        dx = w_ref[3:4, :] * dc
        for k in range(1, CONV_K):
            dx = dx + w_ref[3 - k:4 - k, :] * _shift_rows(dc, -k, rows)
        dx_ref[...] = dx.astype(dx_ref.dtype)
        for k in range(CONV_K):
            dw_ref[3 - k:4 - k, :] = jnp.sum(dc * xs[k], axis=0, keepdims=True)
        db_ref[...] = jnp.sum(dc, axis=0, keepdims=True)

    col = pl.BlockSpec((S, 128), lambda j: (0, j))
    return pl.pallas_call(
        body, name="conv_bwd", grid=(nct,),
        in_specs=[pl.BlockSpec((S, 128), lambda j: (0, XBC_COL0 + j)), pl.BlockSpec((CONV_K, 128), lambda j: (0, j)),
                  pl.BlockSpec((1, 128), lambda j: (0, j)), col],
        out_specs=[col, pl.BlockSpec((CONV_K, 128), lambda j: (0, j)), pl.BlockSpec((1, 128), lambda j: (0, j))],
        out_shape=[jax.ShapeDtypeStruct((S, CONV_CH), _MXU), jax.ShapeDtypeStruct((CONV_K, CONV_CH), F32),
                   jax.ShapeDtypeStruct((1, CONV_CH), F32)],
        compiler_params=_cp(("parallel",)))(proj, conv_w, conv_b, dxa)


def _ssd_consts():
    L = SSD_L
    r = lax.broadcasted_iota(jnp.int32, (L, L), 0)
    c = lax.broadcasted_iota(jnp.int32, (L, L), 1)
    causal = r >= c
    upper = (r <= c).astype(F32)
    hr = lax.broadcasted_iota(jnp.int32, (SSD_HEADS, SSD_WIDTH), 0)
    hc = lax.broadcasted_iota(jnp.int32, (SSD_HEADS, SSD_WIDTH), 1)
    expand = (lax.shift_right_logical(hc, 6) == hr).astype(F32)
    return causal, causal.astype(F32), upper, expand


def _softplus(x):
    return jnp.maximum(x, 0.0) + jnp.log(1.0 + jnp.exp(-jnp.abs(x)))


def _ssd_scalars(dtr, dt_bias, a_log, tri, upper, expand):
    dt = _softplus(dtr + dt_bias)
    A = -jnp.exp(a_log)
    adt = dt * A
    acum = _dot(tri, adt, "nn", hi=True)
    acum_t = _dot(adt, upper, "tn", hi=True)
    alast = acum[SSD_L - 1:SSD_L, :]
    e = jnp.exp(acum)
    wdec = jnp.exp(alast - acum)
    gam = jnp.exp(alast)
    ex = lambda t: _dot(t, expand, "nn", hi=True)
    gam8 = jnp.broadcast_to(gam, (8, SSD_HEADS))
    return dt, A, acum, acum_t, e, wdec, gam, ex(dt), ex(e), ex(wdec), ex(gam8)[0:1, :]


def _ssd_fwd(proj, proj_small, xa, dt_bias, a_log, d_skip, norm_w):
    S = proj.shape[0]
    L, N, W = SSD_L, SSD_N, SSD_WIDTH
    nc = S // L

    def body(z_ref, xa_ref, dtr_ref, dtb_ref, al_ref, dsk_ref, nw_ref, yo_ref, y_ref, rs_ref, hs_ref, h_scr, y_scr):
        @pl.when(pl.program_id(0) == 0)
        def _():
            h_scr[...] = jnp.zeros_like(h_scr)

        causal, tri, upper, expand = _ssd_consts()
        dt, A, acum, acum_t, e, wdec, gam, dtE, eE, wE, gamE = _ssd_scalars(dtr_ref[:, 0:SSD_HEADS], dtb_ref[...], al_ref[...], tri, upper, expand)
        xs = xa_ref[:, 0:W]
        X = xs * dtE
        XW = X * wE
        hs_ref[0] = h_scr[...]
        for g in range(SSD_G):
            gs = slice(g * 512, (g + 1) * 512)
            Bg = xa_ref[:, W + g * N:W + (g + 1) * N]
            Cg = xa_ref[:, W + SSD_G * N + g * N:W + SSD_G * N + (g + 1) * N]
            Hg = h_scr[:, gs]
            CB = _dot(Cg, Bg, "nt")
            yoff = _dot(Cg, Hg, "nn") * eE[:, gs]
            st = _dot(Bg, XW[:, gs], "tn")
            for j in range(8):
                h = g * 8 + j
                hsl = slice(h * SSD_P, (h + 1) * SSD_P)
                lam = jnp.exp(jnp.where(causal, acum[:, h:h + 1] - acum_t[h:h + 1, :], -jnp.inf))
                y_scr[:, hsl] = _dot(CB * lam, X[:, hsl], "nn") + yoff[:, j * SSD_P:(j + 1) * SSD_P]
            h_scr[:, gs] = gamE[:, gs] * Hg + st
        dskE = _dot(jnp.broadcast_to(dsk_ref[...], (8, SSD_HEADS)), expand, "nn", hi=True)[0:1, :]
        y = y_scr[...] + dskE * xs
        y_ref[...] = y
        zv = z_ref[...]
        yg = y * (zv * _sigmoid(zv))
        rs = lax.rsqrt(jnp.mean(yg * yg, axis=-1, keepdims=True) + EPS)
        rs_ref[...] = rs
        yo_ref[...] = ((yg * rs) * nw_ref[...]).astype(yo_ref.dtype)

    p16 = pl.BlockSpec((1, SSD_HEADS), lambda c: (0, 0))
    return pl.pallas_call(
        body, name="ssd_fwd", grid=(nc,),
        in_specs=[pl.BlockSpec((L, W), lambda c: (c, 0)), pl.BlockSpec((L, CONV_CH), lambda c: (c, 0)),
                  pl.BlockSpec((L, W_SMALL), lambda c: (c, 0)), p16, p16, p16, pl.BlockSpec((1, W), lambda c: (0, 0))],
        out_specs=[pl.BlockSpec((L, W), lambda c: (c, 0)), pl.BlockSpec((L, W), lambda c: (c, 0)),
                   pl.BlockSpec((L, 1), lambda c: (c, 0)), pl.BlockSpec((1, N, W), lambda c: (c, 0, 0))],
        out_shape=[jax.ShapeDtypeStruct((S, W), _MXU), jax.ShapeDtypeStruct((S, W), F32), jax.ShapeDtypeStruct((S, 1), F32),
                   jax.ShapeDtypeStruct((nc, N, W), F32)],
        scratch_shapes=[pltpu.VMEM((N, W), F32), pltpu.VMEM((L, W), F32)],
        compiler_params=_cp(("arbitrary",)))(proj, xa, proj_small, dt_bias, a_log, d_skip, norm_w)


def _ssd_bwd(dmixed, proj, proj_small, xa, y, rs2, hs, dt_bias, a_log, d_skip, norm_w):
    S = proj.shape[0]
    L, N, W, H = SSD_L, SSD_N, SSD_WIDTH, SSD_HEADS
    nc = S // L

    def body(dyo_ref, z_ref, xa_ref, dtr_ref, y_ref, rs_ref, hs_ref, dtb_ref, al_ref, dsk_ref, nw_ref,
             dz_ref, dxa_ref, ddtr_ref, ddtb_ref, dal_ref, ddsk_ref, dnw_ref, dh_scr, dx_scr):
        @pl.when(pl.program_id(0) == 0)
        def _():
            dh_scr[...] = jnp.zeros_like(dh_scr)
            ddtb_ref[...] = jnp.zeros_like(ddtb_ref)
            dal_ref[...] = jnp.zeros_like(dal_ref)
            ddsk_ref[...] = jnp.zeros_like(ddsk_ref)
            dnw_ref[...] = jnp.zeros_like(dnw_ref)

        causal, tri, upper, expand = _ssd_consts()
        heads = lambda t: _dot(t, expand, "nt", hi=True)
        onehot = lambda h: (lax.broadcasted_iota(jnp.int32, (1, H), 1) == h).astype(F32)

        zv, yv, rs = z_ref[...], y_ref[...], rs_ref[...]
        sz = _sigmoid(zv)
        zs = zv * sz
        xhat = (yv * zs) * rs
        dyo = dyo_ref[...].astype(F32)
        dnw_ref[...] += jnp.sum(dyo * xhat, axis=0, keepdims=True)
        dxhat = dyo * nw_ref[...]
        dyg = rs * (dxhat - xhat * jnp.mean(dxhat * xhat, axis=-1, keepdims=True))
        dz_ref[...] = (dyg * yv * (sz * (1.0 + zv * (1.0 - sz)))).astype(dz_ref.dtype)
        dy = dyg * zs

        dtr = dtr_ref[:, 0:H]
        dt, A, acum, acum_t, e, wdec, gam, dtE, eE, wE, gamE = _ssd_scalars(dtr, dtb_ref[...], al_ref[...], tri, upper, expand)
        xs = xa_ref[:, 0:W]
        X = xs * dtE
        XW = X * wE
        dskE = _dot(jnp.broadcast_to(dsk_ref[...], (8, H)), expand, "nn", hi=True)[0:1, :]
        ddsk_ref[...] += heads(jnp.broadcast_to(jnp.sum(dy * xs, axis=0, keepdims=True), (8, W)))[0:1, :]

        dYe = dy * eE
        dacum = jnp.zeros((L, H), F32)
        de_full = []
        dw_full = []
        dgam_full = []
        for g in range(SSD_G):
            gs = slice(g * 512, (g + 1) * 512)
            Bg = xa_ref[:, W + g * N:W + (g + 1) * N]
            Cg = xa_ref[:, W + SSD_G * N + g * N:W + SSD_G * N + (g + 1) * N]
            Hg = hs_ref[0, :, gs]
            dHn = dh_scr[:, gs]
            CH = _dot(Cg, Hg, "nn")
            de_full.append(dy[:, gs] * CH)
            dC = _dot(dYe[:, gs], Hg, "nt")
            dHs = gamE[:, gs] * dHn + _dot(Cg, dYe[:, gs], "tn")
            dgam_full.append(jnp.sum(dHn * Hg, axis=0, keepdims=True))
            BdS = _dot(Bg, dHn, "nn")
            dB = _dot(XW[:, gs], dHn, "nt")
            dx_scr[:, gs] = BdS * wE[:, gs]
            dw_full.append(BdS * X[:, gs])
            CB = _dot(Cg, Bg, "nt")
            dCB = jnp.zeros((L, L), F32)
            for j in range(8):
                h = g * 8 + j
                hsl = slice(h * SSD_P, (h + 1) * SSD_P)
                lam = jnp.exp(jnp.where(causal, acum[:, h:h + 1] - acum_t[h:h + 1, :], -jnp.inf))
                M = CB * lam
                dM = _dot(dy[:, hsl], X[:, hsl], "nt")
                dx_scr[:, hsl] += _dot(M, dy[:, hsl], "tn")
                dCB = dCB + dM * lam
                Q = dM * M
                rowsum = jnp.sum(Q, axis=1, keepdims=True)
                colsum = _dot(Q, jnp.ones((L, 8), F32), "tn", hi=True)[:, 0:1]
                dacum = dacum + (rowsum - colsum) * onehot(h)
            dC = dC + _dot(dCB, Bg, "nn")
            dB = dB + _dot(dCB, Cg, "tn")
            dxa_ref[:, W + g * N:W + (g + 1) * N] = dB
            dxa_ref[:, W + SSD_G * N + g * N:W + SSD_G * N + (g + 1) * N] = dC
            dh_scr[:, gs] = dHs

        de16 = heads(jnp.concatenate(de_full, axis=1))
        dw16 = heads(jnp.concatenate(dw_full, axis=1))
        dgam16 = heads(jnp.broadcast_to(jnp.concatenate(dgam_full, axis=1), (8, W)))[0:1, :]
        dacum = dacum + de16 * e - dw16 * wdec
        dlast = jnp.sum(dw16 * wdec, axis=0, keepdims=True) + dgam16 * gam
        lastrow = (lax.broadcasted_iota(jnp.int32, (L, 1), 0) == L - 1).astype(F32)
        dacum = dacum + lastrow * dlast
        da = _dot(tri, dacum, "tn", hi=True)
        dX = dx_scr[...]
        ddt = da * A + heads(dX * xs)
        dA = jnp.sum(da * dt, axis=0, keepdims=True)
        dal_ref[...] += dA * A
        ddtr = ddt * _sigmoid(dtr + dtb_ref[...])
        ddtb_ref[...] += jnp.sum(ddtr, axis=0, keepdims=True)
        ddtr_ref[...] = ddtr
        dxa_ref[:, 0:W] = dX * dtE + dy * dskE

    p16 = pl.BlockSpec((1, H), lambda c: (0, 0))
    rev = lambda c: (nc - 1 - c, 0)
    return pl.pallas_call(
        body, name="ssd_bwd", grid=(nc,),
        in_specs=[pl.BlockSpec((L, W), rev), pl.BlockSpec((L, W), rev), pl.BlockSpec((L, CONV_CH), rev),
                  pl.BlockSpec((L, W_SMALL), rev), pl.BlockSpec((L, W), rev), pl.BlockSpec((L, 1), rev),
                  pl.BlockSpec((1, N, W), lambda c: (nc - 1 - c, 0, 0)), p16, p16, p16, pl.BlockSpec((1, W), lambda c: (0, 0))],
        out_specs=[pl.BlockSpec((L, W), rev), pl.BlockSpec((L, CONV_CH), rev), pl.BlockSpec((L, H), rev),
                   p16, p16, p16, pl.BlockSpec((1, W), lambda c: (0, 0))],
        out_shape=[jax.ShapeDtypeStruct((S, W), _MXU), jax.ShapeDtypeStruct((S, CONV_CH), F32), jax.ShapeDtypeStruct((S, H), F32),
                   jax.ShapeDtypeStruct((1, H), F32), jax.ShapeDtypeStruct((1, H), F32), jax.ShapeDtypeStruct((1, H), F32),
                   jax.ShapeDtypeStruct((1, W), F32)],
        scratch_shapes=[pltpu.VMEM((N, W), F32), pltpu.VMEM((L, W), F32)],
        compiler_params=_cp(("arbitrary",)))(dmixed, proj, xa, proj_small, y, rs2, hs, dt_bias, a_log, d_skip, norm_w)


def _rope_tables(S):
    inv = 1.0 / (ROPE_THETA ** (jnp.arange(0, ROPE_DIM, 2, dtype=F32) / ROPE_DIM))
    ang = jnp.arange(S, dtype=F32)[:, None] * inv[None, :]
    cos, sin = jnp.cos(ang), jnp.sin(ang)
    half = ROPE_DIM // 2
    c64 = jnp.concatenate([cos, cos, jnp.ones((S, HD - ROPE_DIM), F32)], axis=1)
    s64 = jnp.concatenate([sin, sin, jnp.zeros((S, HD - ROPE_DIM), F32)], axis=1)
    del half
    return jnp.concatenate([c64, c64], axis=1), jnp.concatenate([s64, s64], axis=1)


def _rope(xs, blk0, width, cos, sin, sign, out_dtype, name, extra=None):
    S = xs[0].shape[0]
    tr = _pick(S, (512, 256, 128))
    nx = len(xs)

    def body(*refs):
        x_refs, c_ref, s_ref = refs[:nx], refs[nx], refs[nx + 1]
        e_ref = refs[nx + 2] if extra is not None else None
        o_ref = refs[-1]
        cv, sv = c_ref[...], s_ref[...] * sign
        lane = lax.broadcasted_iota(jnp.int32, (tr, 128), 1)
        first = (lane & (HD - 1)) < (ROPE_DIM // 2)
        for j in range(2):
            cs = slice(j * 128, (j + 1) * 128)
            xv = x_refs[0][:, cs].astype(F32)
            for r in x_refs[1:]:
                xv = xv + r[:, cs].astype(F32)
            rot = jnp.where(first, -pltpu.roll(xv, 128 - ROPE_DIM // 2, axis=1), pltpu.roll(xv, ROPE_DIM // 2, axis=1))
            out = xv * cv + rot * sv
            if extra is not None:
                out = out + e_ref[:, cs].astype(F32)
            o_ref[:, cs] = out.astype(out_dtype)

    t128 = pl.BlockSpec((tr, 128), lambda i, j: (i, 0))
    oblk = pl.BlockSpec((tr, 256), lambda i, j: (i, j))
    specs = [pl.BlockSpec((tr, 256), lambda i, j: (i, blk0 + j))] * nx + [t128, t128]
    ins = list(xs) + [cos, sin]
    if extra is not None:
        ins.append(extra[0])
        eb = extra[1]
        specs.append(pl.BlockSpec((tr, 256), lambda i, j: (i, eb + j)))
    return pl.pallas_call(
        body, name=name, grid=(S // tr, width // 256), in_specs=specs, out_specs=oblk,
        out_shape=jax.ShapeDtypeStruct((S, width), out_dtype), compiler_params=_cp(("parallel", "parallel")))(*ins)


def _compress_fwd(R, pe, w1, w2):
    NC = R.shape[1]
    half = 16 * HD

    def body(r_ref, pe_ref, w1_ref, w2_ref, o_ref, hid_ref):
        r = r_ref[0]
        a = _dot(r + pe_ref[:, 0:half], w1_ref[0:half, :], "nn")
        b = _dot(r + pe_ref[:, half:2 * half], w1_ref[half:2 * half, :], "nn")
        hid = a + pltpu.roll(b, NC - 1, axis=0)
        hid_ref[0] = hid
        out = _dot(hid * _sigmoid(hid), w2_ref[...], "nn")
        rows = lax.broadcasted_iota(jnp.int32, out.shape, 0)
        o_ref[0] = jnp.where(rows < NC - 1, out, 0.0).astype(o_ref.dtype)

    return pl.pallas_call(
        body, name="compress_fwd", grid=(N_KV,),
        in_specs=[pl.BlockSpec((1, NC, half), lambda h: (h, 0, 0)), pl.BlockSpec((1, 2 * half), lambda h: (0, 0)),
                  pl.BlockSpec((2 * half, CMP_HID), lambda h: (0, 0)), pl.BlockSpec((CMP_HID, HD), lambda h: (0, 0))],
        out_specs=[pl.BlockSpec((1, NC, HD), lambda h: (h, 0, 0)), pl.BlockSpec((1, NC, CMP_HID), lambda h: (h, 0, 0))],
        out_shape=[jax.ShapeDtypeStruct((N_KV, NC, HD), _MXU), jax.ShapeDtypeStruct((N_KV, NC, CMP_HID), F32)],
        compiler_params=_cp(("parallel",)))(R, pe, w1, w2)


def _compress_bwd(R, pe, w1, w2, hid, dout):
    NC = R.shape[1]
    half = 16 * HD

    def body(r_ref, pe_ref, w1_ref, w2_ref, hid_ref, do_ref, dr_ref, dw1_ref, dw2_ref, dpe_ref):
        @pl.when(pl.program_id(0) == 0)
        def _():
            dw1_ref[...] = jnp.zeros_like(dw1_ref)
            dw2_ref[...] = jnp.zeros_like(dw2_ref)
            dpe_ref[...] = jnp.zeros_like(dpe_ref)

        r, hv, do = r_ref[0], hid_ref[0], do_ref[0]
        s = _sigmoid(hv)
        dw2_ref[...] += _dot(hv * s, do, "tn")
        dhid = _dot(do, w2_ref[...], "nt") * (s * (1.0 + hv * (1.0 - s)))
        rows = lax.broadcasted_iota(jnp.int32, dhid.shape, 0)
        dhid = jnp.where(rows < NC - 1, dhid, 0.0)
        dhid_dn = pltpu.roll(dhid, 1, axis=0)
        dw1_ref[0:half, :] += _dot(r + pe_ref[:, 0:half], dhid, "tn")
        dw1_ref[half:2 * half, :] += _dot(r + pe_ref[:, half:2 * half], dhid_dn, "tn")
        dxt = _dot(dhid, w1_ref[0:half, :], "nt")
        dxb = _dot(dhid_dn, w1_ref[half:2 * half, :], "nt")
        dr_ref[0] = dxt + dxb
        dpe_ref[:, 0:half] += jnp.sum(dxt, axis=0, keepdims=True)
        dpe_ref[:, half:2 * half] += jnp.sum(dxb, axis=0, keepdims=True)

    return pl.pallas_call(
        body, name="compress_bwd", grid=(N_KV,),
        in_specs=[pl.BlockSpec((1, NC, half), lambda h: (h, 0, 0)), pl.BlockSpec((1, 2 * half), lambda h: (0, 0)),
                  pl.BlockSpec((2 * half, CMP_HID), lambda h: (0, 0)), pl.BlockSpec((CMP_HID, HD), lambda h: (0, 0)),
                  pl.BlockSpec((1, NC, CMP_HID), lambda h: (h, 0, 0)), pl.BlockSpec((1, NC, HD), lambda h: (h, 0, 0))],
        out_specs=[pl.BlockSpec((1, NC, half), lambda h: (h, 0, 0)), pl.BlockSpec((2 * half, CMP_HID), lambda h: (0, 0)),
                   pl.BlockSpec((CMP_HID, HD), lambda h: (0, 0)), pl.BlockSpec((1, 2 * half), lambda h: (0, 0))],
        out_shape=[jax.ShapeDtypeStruct((N_KV, NC, half), F32), jax.ShapeDtypeStruct((2 * half, CMP_HID), F32),
                   jax.ShapeDtypeStruct((CMP_HID, HD), F32), jax.ShapeDtypeStruct((1, 2 * half), F32)],
        compiler_params=_cp(("arbitrary",)))(R, pe, w1, w2, hid, dout)


def _attn_cfg(S, Sk, mode):
    tq = _pick(S, (256, 128))
    tk = Sk if mode == "cmp" else _pick(Sk, (256, 128))
    return tq, tk


def _kb_range(mode, q0, tq, tk):
    if mode == "cmp":
        return 0, 1
    hi = (q0 + tq - 1) // tk + 1
    if mode == "sel":
        return 0, hi
    return jnp.maximum(q0 - (WINDOW - 1), 0) // tk, hi


def _attn_bias(mode, q0, k0, tq, tk, sel_t):
    k = k0 + lax.broadcasted_iota(jnp.int32, (tk, tq), 0)
    t = q0 + lax.broadcasted_iota(jnp.int32, (tk, tq), 1)
    if mode == "cmp":
        ok = (k * 16 + 31) <= t
    elif mode == "win":
        ok = (k <= t) & ((t - k) < WINDOW)
    else:
        nb = sel_t.shape[0]
        ek = k0 + lax.broadcasted_iota(jnp.int32, (tk, nb), 0)
        eb = lax.broadcasted_iota(jnp.int32, (tk, nb), 1)
        expand = (lax.shift_right_logical(ek, 6) == eb).astype(_MXU)
        chosen = _dot(expand, sel_t, "nn") > 0.5
        ok = (k <= t) & chosen
    bias = jnp.where(ok, 0.0, NEG)
    return jnp.concatenate([bias] * GRP, axis=1), jnp.concatenate([ok.astype(F32)] * GRP, axis=1)


def _stack_heads(ref, tq):
    return jnp.concatenate([ref[:, g * HD:(g + 1) * HD] for g in range(GRP)], axis=0)


def _scaled_queries(q_ref, tq):
    return (_stack_heads(q_ref, tq).astype(F32) * SCALE).astype(_MXU)


def _blocked_t(x, tk):
    n, Sk, d = x.shape
    return x.reshape(n, Sk // tk, tk, d).transpose(0, 1, 3, 2)


def _attn_fwd(q, qcol0, k, v, mode, sel_t, name):
    S, Sk = q.shape[0], k.shape[1]
    tq, tk = _attn_cfg(S, Sk, mode)
    R = GRP * tq
    vt = _blocked_t(v, tk)

    def body(*refs):
        if mode == "sel":
            q_ref, k_ref, vt_ref, sel_ref, o_ref, lse_ref, m_scr, l_scr, acc = refs
        else:
            q_ref, k_ref, vt_ref, o_ref, lse_ref, m_scr, l_scr, acc = refs
        q0 = pl.program_id(1) * tq
        qs = _scaled_queries(q_ref, tq)
        m_scr[...] = jnp.full_like(m_scr, NEG)
        l_scr[...] = jnp.zeros_like(l_scr)
        acc[...] = jnp.zeros_like(acc)
        selv = sel_ref[0].astype(_MXU) if mode == "sel" else None

        def step(kb, carry):
            k0 = pl.multiple_of(kb * tk, tk)
            bias, okf = _attn_bias(mode, q0, k0, tq, tk, selv)
            s = _dot(k_ref[0, pl.ds(k0, tk), :], qs, "nt") + bias
            m_old = m_scr[...]
            m_new = jnp.maximum(m_old, jnp.max(s, axis=0, keepdims=True))
            p = jnp.exp(s - m_new)
            if mode == "cmp":
                p = p * okf
            alpha = jnp.exp(m_old - m_new)
            l_scr[...] = alpha * l_scr[...] + jnp.sum(p, axis=0, keepdims=True)
            acc[...] = alpha * acc[...] + _dot(vt_ref[0, kb], p, "nn")
            m_scr[...] = m_new
            return carry

        lo, hi = _kb_range(mode, q0, tq, tk)
        lax.fori_loop(lo, hi, step, 0)
        l = l_scr[...]
        good = l > 0.0
        o_t = acc[...] * jnp.where(good, 1.0 / jnp.where(good, l, 1.0), 0.0)
        lse = jnp.where(good, m_scr[...] + jnp.log(jnp.where(good, l, 1.0)), -NEG)
        for g in range(GRP):
            o_ref[:, g * HD:(g + 1) * HD] = o_t[:, g * tq:(g + 1) * tq].T
            lse_ref[0, g:g + 1, :] = lse[:, g * tq:(g + 1) * tq]

    ins = [q, k, vt]
    specs = [pl.BlockSpec((tq, GRP * HD), lambda h, i: (i, qcol0 + h)), pl.BlockSpec((1, Sk, HD), lambda h, i: (h, 0, 0)),
             pl.BlockSpec((1, Sk // tk, HD, tk), lambda h, i: (h, 0, 0, 0))]
    if mode == "sel":
        ins.append(sel_t)
        specs.append(pl.BlockSpec((1, sel_t.shape[1], tq), lambda h, i: (h, 0, i)))
    return pl.pallas_call(
        body, name=name, grid=(N_KV, S // tq), in_specs=specs,
        out_specs=[pl.BlockSpec((tq, GRP * HD), lambda h, i: (i, h)), pl.BlockSpec((1, GRP, tq), lambda h, i: (h, 0, i))],
        out_shape=[jax.ShapeDtypeStruct((S, ATT_WIDTH), F32), jax.ShapeDtypeStruct((N_KV, GRP, S), F32)],
        scratch_shapes=[pltpu.VMEM((1, R), F32), pltpu.VMEM((1, R), F32), pltpu.VMEM((HD, R), F32)],
        compiler_params=_cp(("parallel", "arbitrary")))(*ins)


def _attn_bwd(q, qcol0, k, v, o, lse, do, mode, sel_t, name):
    S, Sk = q.shape[0], k.shape[1]
    tq, tk = _attn_cfg(S, Sk, mode)
    R = GRP * tq
    kt = _blocked_t(k, tk)

    def body(*refs):
        if mode == "sel":
            q_ref, k_ref, kt_ref, v_ref, o_ref, lse_ref, do_ref, sel_ref, dq_ref, dk_ref, dv_ref, dq_scr = refs
        else:
            q_ref, k_ref, kt_ref, v_ref, o_ref, lse_ref, do_ref, dq_ref, dk_ref, dv_ref, dq_scr = refs

        @pl.when(pl.program_id(1) == 0)
        def _():
            dk_ref[...] = jnp.zeros_like(dk_ref)
            dv_ref[...] = jnp.zeros_like(dv_ref)

        q0 = pl.program_id(1) * tq
        qs = _scaled_queries(q_ref, tq)
        dos = _stack_heads(do_ref, tq)
        delta = _dot(jnp.ones((8, HD), F32), dos * _stack_heads(o_ref, tq), "nt", hi=True)[0:1, :]
        lsev = jnp.concatenate([lse_ref[0, g:g + 1, :] for g in range(GRP)], axis=1)
        dos = dos.astype(_MXU)
        dq_scr[...] = jnp.zeros_like(dq_scr)
        selv = sel_ref[0].astype(_MXU) if mode == "sel" else None

        def step(kb, carry):
            k0 = pl.multiple_of(kb * tk, tk)
            kv = k_ref[0, pl.ds(k0, tk), :]
            bias, okf = _attn_bias(mode, q0, k0, tq, tk, selv)
            p = jnp.exp(_dot(kv, qs, "nt") + bias - lsev)
            if mode == "cmp":
                p = p * okf
            dp = _dot(v_ref[0, pl.ds(k0, tk), :], dos, "nt")
            ds = p * (dp - delta)
            dq_scr[...] += _dot(kt_ref[0, kb], ds, "nn")
            dk_ref[0, pl.ds(k0, tk), :] += _dot(ds, qs, "nn")
            dv_ref[0, pl.ds(k0, tk), :] += _dot(p, dos, "nn")
            return carry

        lo, hi = _kb_range(mode, q0, tq, tk)
        lax.fori_loop(lo, hi, step, 0)
        for g in range(GRP):
            dq_ref[:, g * HD:(g + 1) * HD] = (dq_scr[:, g * tq:(g + 1) * tq] * SCALE).T

    kv_spec = pl.BlockSpec((1, Sk, HD), lambda h, i: (h, 0, 0))
    qo_spec = pl.BlockSpec((tq, GRP * HD), lambda h, i: (i, h))
    ins = [q, k, kt, v, o, lse, do]
    specs = [pl.BlockSpec((tq, GRP * HD), lambda h, i: (i, qcol0 + h)), kv_spec,
             pl.BlockSpec((1, Sk // tk, HD, tk), lambda h, i: (h, 0, 0, 0)), kv_spec, qo_spec,
             pl.BlockSpec((1, GRP, tq), lambda h, i: (h, 0, i)), qo_spec]
    if mode == "sel":
        ins.append(sel_t)
        specs.append(pl.BlockSpec((1, sel_t.shape[1], tq), lambda h, i: (h, 0, i)))
    return pl.pallas_call(
        body, name=name, grid=(N_KV, S // tq), in_specs=specs, out_specs=[qo_spec, kv_spec, kv_spec],
        out_shape=[jax.ShapeDtypeStruct((S, ATT_WIDTH), F32), jax.ShapeDtypeStruct((N_KV, Sk, HD), F32),
                   jax.ShapeDtypeStruct((N_KV, Sk, HD), F32)],
        scratch_shapes=[pltpu.VMEM((HD, R), F32)],
        compiler_params=_cp(("parallel", "arbitrary")))(*ins)


def _select(q, qcol0, k_cmp, lse):
    S, NC = q.shape[0], k_cmp.shape[1]
    NB = S // SEL_BLOCK
    tq = _pick(S, (256, 128))
    ci = np.arange(NC)[None, :] * 16
    sj = np.arange(NB)[:, None] * SEL_BLOCK
    ov_t = np.clip(np.minimum(ci + 32, sj + SEL_BLOCK) - np.maximum(ci, sj), 0, None) / 32.0
    ov_t[:, NC - 1] = 0.0
    ov_t = jnp.asarray(ov_t, F32)

    def body(q_ref, k_ref, lse_ref, ov_ref, sel_ref):
        q0 = pl.program_id(1) * tq
        bias, okf = _attn_bias("cmp", q0, 0, tq, NC, None)
        lsev = jnp.concatenate([lse_ref[0, g:g + 1, :] for g in range(GRP)], axis=1)
        p = jnp.exp(_dot(k_ref[0], _scaled_queries(q_ref, tq), "nt") + bias - lsev) * okf
        imp4 = _dot(ov_ref[...], p, "nn")
        imp = imp4[:, 0:tq] + imp4[:, tq:2 * tq] + imp4[:, 2 * tq:3 * tq] + imp4[:, 3 * tq:4 * tq]
        blk = lax.broadcasted_iota(jnp.int32, (NB, tq), 0)
        cur = lax.shift_right_logical(q0 + lax.broadcasted_iota(jnp.int32, (NB, tq), 1), 6)
        imp = jnp.where((blk == 0) | (blk == cur) | (blk == cur - 1), FORCE, imp)
        imp = jnp.where(blk <= cur, imp, -1.0)
        rank = jnp.zeros((NB, tq), F32)
        for j in range(NB):
            row = imp[j:j + 1, :]
            ahead = (row > imp) | ((row == imp) & (blk > j))
            rank = rank + ahead.astype(F32)
        sel_ref[0] = ((rank < float(N_SELECT)) & (imp >= 0.0)).astype(F32)

    return pl.pallas_call(
        body, name="select_blocks", grid=(N_KV, S // tq),
        in_specs=[pl.BlockSpec((tq, GRP * HD), lambda h, i: (i, qcol0 + h)), pl.BlockSpec((1, NC, HD), lambda h, i: (h, 0, 0)),
                  pl.BlockSpec((1, GRP, tq), lambda h, i: (h, 0, i)), pl.BlockSpec((NB, NC), lambda h, i: (0, 0))],
        out_specs=pl.BlockSpec((1, NB, tq), lambda h, i: (h, 0, i)),
        out_shape=jax.ShapeDtypeStruct((N_KV, NB, S), F32), compiler_params=_cp(("parallel", "parallel")))(q, k_cmp, lse, ov_t)


GATE_COL0 = SSD_HEADS


def _combine_fwd(o_cmp, o_sel, o_win, proj_small):
    S = o_cmp.shape[0]
    tr = _pick(S, (256, 128))

    def body(oc_ref, os_ref, ow_ref, g_ref, y_ref):
        gate = _sigmoid(g_ref[...])
        for h in range(N_HEADS):
            hs = slice(h * HD, (h + 1) * HD)
            c = GATE_COL0 + 3 * h
            y = gate[:, c:c + 1] * oc_ref[:, hs] + gate[:, c + 1:c + 2] * os_ref[:, hs] + gate[:, c + 2:c + 3] * ow_ref[:, hs]
            y_ref[:, hs] = y.astype(y_ref.dtype)

    row = pl.BlockSpec((tr, ATT_WIDTH), lambda i: (i, 0))
    return pl.pallas_call(
        body, name="combine_fwd", grid=(S // tr,), in_specs=[row, row, row, pl.BlockSpec((tr, W_SMALL), lambda i: (i, 0))],
        out_specs=row, out_shape=jax.ShapeDtypeStruct((S, ATT_WIDTH), _MXU), compiler_params=_cp(("parallel",)))(
            o_cmp, o_sel, o_win, proj_small)


def _combine_bwd(dmixed, o_cmp, o_sel, o_win, proj_small):
    S = o_cmp.shape[0]
    tr = _pick(S, (256, 128))

    def body(dy_ref, oc_ref, os_ref, ow_ref, g_ref, dc_ref, ds_ref, dw_ref, dg_ref):
        gate = _sigmoid(g_ref[...])
        lane = lax.broadcasted_iota(jnp.int32, (1, W_SMALL), 1)
        dg = jnp.zeros((tr, W_SMALL), F32)
        for h in range(N_HEADS):
            hs = slice(h * HD, (h + 1) * HD)
            dy = dy_ref[:, hs].astype(F32)
            for b, (o_ref, d_ref) in enumerate(((oc_ref, dc_ref), (os_ref, ds_ref), (ow_ref, dw_ref))):
                c = GATE_COL0 + 3 * h + b
                gv = gate[:, c:c + 1]
                d_ref[:, hs] = gv * dy
                dgate = jnp.sum(dy * o_ref[:, hs], axis=-1, keepdims=True) * (gv * (1.0 - gv))
                dg = dg + dgate * (lane == c).astype(F32)
        dg_ref[...] = dg

    row = pl.BlockSpec((tr, ATT_WIDTH), lambda i: (i, 0))
    small = pl.BlockSpec((tr, W_SMALL), lambda i: (i, 0))
    return pl.pallas_call(
        body, name="combine_bwd", grid=(S // tr,),
        in_specs=[pl.BlockSpec((tr, ATT_WIDTH), lambda i: (i, 1)), row, row, row, small], out_specs=[row, row, row, small],
        out_shape=[jax.ShapeDtypeStruct((S, ATT_WIDTH), F32)] * 3 + [jax.ShapeDtypeStruct((S, W_SMALL), F32)],
        compiler_params=_cp(("parallel",)))(dmixed, o_cmp, o_sel, o_win, proj_small)


def _heads_major(x):
    S = x.shape[0]
    return x.reshape(S, N_KV, HD).transpose(1, 0, 2)


def _tokens_major(x):
    return x.transpose(1, 0, 2).reshape(x.shape[1], N_KV * HD)


def _to_rows16(x):
    S = x.shape[0]
    return x.reshape(S // 16, 16, N_KV, HD).transpose(2, 0, 1, 3).reshape(N_KV, S // 16, 16 * HD)


def _from_rows16(r):
    NC = r.shape[1]
    return r.reshape(N_KV, NC, 16, HD).transpose(1, 2, 0, 3).reshape(NC * 16, N_KV * HD)


DT_COL0 = SSD_WIDTH + CONV_CH
GATE_IN_COL0 = D_IN - 3 * N_HEADS


def _split_w_in(w):
    main = jnp.concatenate([w[:, :DT_COL0], w[:, DT_COL0 + SSD_HEADS:GATE_IN_COL0]], axis=1)
    small = jnp.concatenate([w[:, DT_COL0:DT_COL0 + SSD_HEADS], w[:, GATE_IN_COL0:],
                             jnp.zeros((w.shape[0], W_SMALL - SSD_HEADS - 3 * N_HEADS), w.dtype)], axis=1)
    return main, small


def _merge_w_in(main, small):
    return jnp.concatenate([main[:, :DT_COL0], small[:, :SSD_HEADS].astype(main.dtype), main[:, DT_COL0:],
                            small[:, SSD_HEADS:SSD_HEADS + 3 * N_HEADS].astype(main.dtype)], axis=1)


QB, KCB, VCB, KSB, VSB, KWB, VWB = 10, 14, 15, 16, 17, 18, 19


def _col256(a, b):
    return a[:, b * 256:(b + 1) * 256]


_EARLY = ["w_in", "cmp_w1_k", "cmp_w1_v"]
_LATE = ["w_out", "w_gate", "w_up", "w_down"]
_FFN = ["w_down", "w_gate", "w_up"]
_MID = ["w_out"]
_LAST = ["cmp_w1_k", "cmp_w1_v", "w_in"]


def _local_step(x, tgt, p, early_weights=None, late_weights=None, grads_ready=None):
    S = x.shape[0]
    cos, sin = _rope_tables(S)

    u, rs1 = _rms_fwd(x, p["attn_norm_w"], "attn_norm")
    if early_weights is not None:
        p = {**p, **early_weights(u)}
    proj = _mm(u, p["w_main"], "nn", F32, "in_proj")
    proj_small = _mm(u, p["w_small"], "nn", F32, "in_proj_small")
    xa = _conv_fwd(proj, p["conv_w"], p["conv_b"])
    y_ssd, y_pre, rs_ssd, hs = _ssd_fwd(proj, proj_small, xa, p["dt_bias"], p["a_log"], p["d_skip"], p["ssd_norm_w"])

    q_rot = _rope([proj], QB, ATT_WIDTH, cos, sin, 1.0, _MXU, "rope_q")
    ks_rot = _heads_major(_rope([proj], KSB, 256, cos, sin, 1.0, _MXU, "rope_ks"))
    kw_rot = _heads_major(_rope([proj], KWB, 256, cos, sin, 1.0, _MXU, "rope_kw"))
    vs = _heads_major(_col256(proj, VSB).astype(_MXU))
    vw = _heads_major(_col256(proj, VWB).astype(_MXU))
    rk, rv = _to_rows16(_col256(proj, KCB)), _to_rows16(_col256(proj, VCB))
    k_cmp, hid_k = _compress_fwd(rk, p["cmp_pe_k"], p["cmp_w1_k"], p["cmp_w2_k"])
    v_cmp, hid_v = _compress_fwd(rv, p["cmp_pe_v"], p["cmp_w1_v"], p["cmp_w2_v"])

    o_cmp, lse_cmp = _attn_fwd(proj, QB, k_cmp, v_cmp, "cmp", None, "attn_cmp_fwd")
    sel = _select(proj, QB, k_cmp, lse_cmp)
    o_sel, lse_sel = _attn_fwd(q_rot, 0, ks_rot, vs, "sel", sel, "attn_sel_fwd")
    o_win, lse_win = _attn_fwd(q_rot, 0, kw_rot, vw, "win", None, "attn_win_fwd")
    y_att = _combine_fwd(o_cmp, o_sel, o_win, proj_small)

    if late_weights is not None:
        p = {**p, **late_weights(y_att)}
    mixed = jnp.concatenate([y_ssd, y_att], axis=1)
    h1 = _mm(mixed, p["w_out"], "nn", F32, "out_proj", res=x)
    v, rs_ffn = _rms_fwd(h1, p["ffn_norm_w"], "ffn_norm")
    gt, up, act = _ffn_up(v, p["w_gate"], p["w_up"])
    h2 = _mm(act, p["w_down"], "nn", F32, "ffn_down", res=h1)
    loss, dh2, dh2b, d_final_w = _final_loss(h2, p["final_norm_w"], tgt)

    def ready(names):
        return None if grads_ready is None else grads_ready(names, g)

    g = {"final_norm_w": d_final_w}
    g["w_down"] = _mm(act, dh2b, "tn", _MXU, "dw_down")
    dgt, dup = _ffn_dact(dh2b, p["w_down"], gt, up)
    g["w_gate"] = _mm(v, dgt, "tn", _MXU, "dw_gate")
    g["w_up"] = _mm(v, dup, "tn", _MXU, "dw_up")
    dv = _mm(dgt, p["w_gate"], "nt", F32, "dv_gate", after=ready(_FFN))
    dv = _mm(dup, p["w_up"], "nt", F32, "dv_up", res=dv)
    dh1, dh1b, g["ffn_norm_w"] = _rms_bwd(dv, h1, rs_ffn, p["ffn_norm_w"], dh2, "ffn_norm_bwd")
    g["w_out"] = _mm(mixed, dh1b, "tn", _MXU, "dw_out")
    dmixed = _mm(dh1b, p["w_out"], "nt", F32, "dmixed", after=ready(_MID))

    dz, dxa, ddtr, g["dt_bias"], g["a_log"], g["d_skip"], g["ssd_norm_w"] = _ssd_bwd(
        dmixed, proj, proj_small, xa, y_pre, rs_ssd, hs, p["dt_bias"], p["a_log"], p["d_skip"], p["ssd_norm_w"])
    dxbc, g["conv_w"], g["conv_b"] = _conv_bwd(proj, p["conv_w"], p["conv_b"], dxa)

    do_cmp, do_sel, do_win, dgate = _combine_bwd(dmixed, o_cmp, o_sel, o_win, proj_small)
    dq_cmp, dk_cmp, dv_cmp = _attn_bwd(proj, QB, k_cmp, v_cmp, o_cmp, lse_cmp, do_cmp, "cmp", None, "attn_cmp_bwd")
    dq_sel, dks, dvs = _attn_bwd(q_rot, 0, ks_rot, vs, o_sel, lse_sel, do_sel, "sel", sel, "attn_sel_bwd")
    dq_win, dkw, dvw = _attn_bwd(q_rot, 0, kw_rot, vw, o_win, lse_win, do_win, "win", None, "attn_win_bwd")
    drk, g["cmp_w1_k"], g["cmp_w2_k"], g["cmp_pe_k"] = _compress_bwd(rk, p["cmp_pe_k"], p["cmp_w1_k"], p["cmp_w2_k"], hid_k, dk_cmp)
    drv, g["cmp_w1_v"], g["cmp_w2_v"], g["cmp_pe_v"] = _compress_bwd(rv, p["cmp_pe_v"], p["cmp_w1_v"], p["cmp_w2_v"], hid_v, dv_cmp)
    dq = _rope([dq_sel, dq_win], 0, ATT_WIDTH, cos, sin, -1.0, _MXU, "rope_dq", extra=(dq_cmp, 0))
    dks_t = _rope([_tokens_major(dks)], 0, 256, cos, sin, -1.0, _MXU, "rope_dks")
    dkw_t = _rope([_tokens_major(dkw)], 0, 256, cos, sin, -1.0, _MXU, "rope_dkw")
    dproj = jnp.concatenate(
        [dz, dxbc, dq] + [t.astype(_MXU) for t in (_from_rows16(drk), _from_rows16(drv))]
        + [dks_t, _tokens_major(dvs).astype(_MXU), dkw_t, _tokens_major(dvw).astype(_MXU)], axis=1)
    dsmall = jnp.concatenate([ddtr, dgate[:, GATE_COL0:GATE_COL0 + 3 * N_HEADS],
                              jnp.zeros((S, W_SMALL - SSD_HEADS - 3 * N_HEADS), F32)], axis=1).astype(_MXU)
    g["w_main"] = _mm(u, dproj, "tn", _MXU, "dw_in")
    g["w_small"] = _mm(u, dsmall, "tn", F32, "dw_in_small")
    du = _mm(dproj, p["w_main"], "nt", F32, "du_main", after=ready(_LAST))
    du = _mm(dsmall, p["w_small"], "nt", F32, "du_small", res=du)
    grad_x, _, g["attn_norm_w"] = _rms_bwd(du, x, rs1, p["attn_norm_w"], dh1, "attn_norm_bwd")
    return loss, grad_x, g


MESH_ID = pl.DeviceIdType.MESH


def _my_coords():
    return lax.axis_index("x"), lax.axis_index("y"), lax.axis_index("c")


def _flat_id(px, py, pc):
    return 4 * px + 2 * py + pc


def _peer(k):
    mx, my, mc = _my_coords()
    return (1 - mx if k & 4 else mx, 1 - my if k & 2 else my, 1 - mc if k & 1 else mc)


def _exchange(arrs, scatter, name, after=()):
    n, na = len(arrs), len(after)

    def body(*refs):
        ins, outs = refs[:n], refs[n + na:2 * n + na]
        send_sems, recv_sems, local_sems = refs[2 * n + na:]
        me = _flat_id(*_my_coords())
        copies = []
        for i in range(n):
            src_me = ins[i].at[me] if scatter else ins[i]
            local = pltpu.make_async_copy(src_me, outs[i].at[me], local_sems.at[i])
            local.start()
            copies.append(local)
        for k in range(1, N_DEV):
            peer = _peer(k)
            for i in range(n):
                src = ins[i].at[_flat_id(*peer)] if scatter else ins[i]
                cp = pltpu.make_async_remote_copy(src_ref=src, dst_ref=outs[i].at[me], send_sem=send_sems.at[i * 7 + k - 1],
                                                  recv_sem=recv_sems.at[i * 7 + k - 1], device_id=peer, device_id_type=MESH_ID)
                cp.start()
                copies.append(cp)
        for cp in copies:
            cp.wait()

    any_spec = pl.BlockSpec(memory_space=pl.ANY)
    out_shape = [jax.ShapeDtypeStruct(a.shape if scatter else (N_DEV,) + a.shape, a.dtype) for a in arrs]
    return pl.pallas_call(
        body, name=name, in_specs=[any_spec] * (n + na), out_specs=[any_spec] * n, out_shape=out_shape,
        scratch_shapes=[pltpu.SemaphoreType.DMA((n * 7,)), pltpu.SemaphoreType.DMA((n * 7,)), pltpu.SemaphoreType.DMA((n,))],
        compiler_params=pltpu.CompilerParams(has_side_effects=True))(*arrs, *after)


_HBM = pl.BlockSpec(memory_space=pltpu.HBM)
_SEM = pl.BlockSpec(memory_space=pltpu.SEMAPHORE)
_EFFECT = pltpu.SideEffectType.DATAFLOW_SIDE_EFFECTING


def _split_copies(ins, lands, send_sems, recv_sems, scatter):
    me = _flat_id(*_my_coords())
    out = []
    for k in range(1, N_DEV):
        peer = _peer(k)
        for i in range(len(ins)):
            src = ins[i].at[_flat_id(*peer)] if scatter else ins[i]
            out.append(pltpu.make_async_remote_copy(src_ref=src, dst_ref=lands[i].at[me], send_sem=send_sems.at[i * 7 + k - 1],
                                                    recv_sem=recv_sems.at[i * 7 + k - 1], device_id=peer, device_id_type=MESH_ID))
    return out


def _split_start(arrs, scatter, name):
    n = len(arrs)

    def body(*refs):
        for cp in _split_copies(refs[:n], refs[n:2 * n], refs[2 * n], refs[2 * n + 1], scatter):
            cp.start()
        refs[-1][...] = jnp.zeros_like(refs[-1])

    land_shapes = [a.shape if scatter else (N_DEV,) + a.shape for a in arrs]
    out_shape = ((pltpu.SemaphoreType.DMA((n * 7,)), pltpu.SemaphoreType.DMA((n * 7,)))
                 + tuple(pltpu.HBM(a.shape, a.dtype) for a in arrs) + tuple(pltpu.HBM(s, a.dtype) for s, a in zip(land_shapes, arrs))
                 + (jax.ShapeDtypeStruct((8, 128), F32),))
    operands = ([pltpu.with_memory_space_constraint(a, pltpu.HBM) for a in arrs]
                + [pltpu.with_memory_space_constraint(lax.empty(s, a.dtype), pltpu.HBM) for s, a in zip(land_shapes, arrs)])
    res = pl.pallas_call(
        body, name=name, out_shape=out_shape, in_specs=[_HBM] * (2 * n),
        out_specs=(_SEM, _SEM) + (_HBM,) * (2 * n) + (pl.BlockSpec(memory_space=pltpu.VMEM),),
        input_output_aliases={i: 2 + i for i in range(2 * n)},
        compiler_params=pltpu.CompilerParams(has_side_effects=_EFFECT))(*operands)
    return dict(send=res[0], recv=res[1], ins=list(res[2:2 + n]), lands=list(res[2 + n:2 + 2 * n]), token=res[-1])


def _split_wait(st, scatter, after, name):
    n = len(st["ins"])

    def body(*refs):
        for cp in _split_copies(refs[:n], refs[n:2 * n], refs[2 * n], refs[2 * n + 1], scatter):
            cp.wait_send()
            cp.wait_recv()

    arrs = st["ins"] + st["lands"]
    res = pl.pallas_call(
        body, name=name, out_shape=tuple(pltpu.HBM(a.shape, a.dtype) for a in arrs),
        in_specs=[_HBM] * (2 * n) + [_SEM, _SEM] + [pl.BlockSpec(memory_space=pl.ANY)] * len(after), out_specs=(_HBM,) * (2 * n),
        input_output_aliases={i: i for i in range(2 * n)},
        compiler_params=pltpu.CompilerParams(has_side_effects=_EFFECT))(*arrs, st["send"], st["recv"], *after)
    me = _flat_id(*_my_coords())
    out = []
    for src, land in zip(res[:n], res[n:]):
        own = lax.dynamic_index_in_dim(src, me, 0, keepdims=True) if scatter else src[None]
        out.append(lax.dynamic_update_slice_in_dim(land, own, me, 0))
    return out


def _sum_parts(parts):
    P, R, C = parts.shape
    tr = _pick(R, (136, 8))

    def body(p_ref, o_ref):
        acc = p_ref[0]
        for j in range(1, P):
            acc = acc + p_ref[j]
        o_ref[...] = acc

    return pl.pallas_call(
        body, name="sum_small_grads", grid=(R // tr,), in_specs=[pl.BlockSpec((P, tr, C), lambda i: (0, i, 0))],
        out_specs=pl.BlockSpec((tr, C), lambda i: (i, 0)), out_shape=jax.ShapeDtypeStruct((R, C), F32),
        compiler_params=_cp(("parallel",)))(parts)


def _adam_sum(parts, w, m, v, name):
    P, R, C = parts.shape
    tr = _pick(R, (256, 128, 64, 32, 8)) if C <= 1024 else _pick(R, (128, 64, 32, 8))

    def body(p_ref, w_ref, m_ref, v_ref, g_ref, d_ref, nm_ref, nv_ref):
        g = p_ref[0].astype(F32)
        for j in range(1, P):
            g = g + p_ref[j].astype(F32)
        g_ref[...] = g
        nm = ADAM_B1 * m_ref[...] + (1.0 - ADAM_B1) * g
        nv = ADAM_B2 * v_ref[...] + (1.0 - ADAM_B2) * (g * g)
        nm_ref[...] = nm
        nv_ref[...] = nv
        m_hat = nm / (1.0 - ADAM_B1 ** ADAM_STEP)
        v_hat = nv / (1.0 - ADAM_B2 ** ADAM_STEP)
        d_ref[...] = -ADAM_LR * (m_hat / (jnp.sqrt(v_hat) + ADAM_EPS) + ADAM_WD * w_ref[...])

    blk = pl.BlockSpec((tr, C), lambda i: (i, 0))
    return pl.pallas_call(
        body, name=name, grid=(R // tr,), in_specs=[pl.BlockSpec((P, tr, C), lambda i: (0, i, 0)), blk, blk, blk],
        out_specs=[blk] * 4, out_shape=[jax.ShapeDtypeStruct((R, C), F32)] * 4, compiler_params=_cp(("parallel",)))(parts, w, m, v)


def _pack(arrs):
    rows = []
    for a in arrs:
        f = a.reshape(-1).astype(F32)
        f = jnp.pad(f, (0, (-f.shape[0]) % 128))
        rows.append(f.reshape(-1, 128))
    out = jnp.concatenate(rows, axis=0)
    return jnp.pad(out, ((0, (-out.shape[0]) % 8), (0, 0)))


def _unpack(pack, shapes):
    out, r = [], 0
    for s in shapes:
        n = int(np.prod(s))
        nr = -(-n // 128)
        out.append(pack[r:r + nr].reshape(-1)[:n].reshape(s))
        r += nr
    return out


_WEIGHTS = ["attn_norm_w", "w_in", "conv_w", "conv_b", "dt_bias", "a_log", "d_skip", "ssd_norm_w", "cmp_w1_k", "cmp_w2_k",
            "cmp_w1_v", "cmp_w2_v", "cmp_pe_k", "cmp_pe_v", "w_out", "ffn_norm_w", "w_gate", "w_up", "w_down", "final_norm_w"]
_BIG = ["w_in", "w_gate", "w_up", "w_down", "w_out", "cmp_w1_k", "cmp_w1_v"]
_COL_SHARDED = ("w_in", "w_gate", "w_up")
_REPLICATED = ["attn_norm_w", "conv_b", "dt_bias", "a_log", "d_skip", "ssd_norm_w", "cmp_pe_k", "cmp_pe_v", "ffn_norm_w",
               "final_norm_w"]
_SMALL_SHARDED = ["conv_w", "cmp_w2_k", "cmp_w2_v"]
_SMALL_FULL_SHAPES = {"attn_norm_w": (1, D_MODEL), "conv_b": (1, CONV_CH), "dt_bias": (1, SSD_HEADS), "a_log": (1, SSD_HEADS),
                      "d_skip": (1, SSD_HEADS), "ssd_norm_w": (1, SSD_WIDTH), "cmp_pe_k": (1, 32 * HD), "cmp_pe_v": (1, 32 * HD),
                      "ffn_norm_w": (1, D_MODEL), "final_norm_w": (1, D_MODEL), "conv_w": (CONV_K, CONV_CH),
                      "cmp_w2_k": (CMP_HID, HD), "cmp_w2_v": (CMP_HID, HD)}


def _cols_to_slabs(g):
    R = g.shape[0]
    return g.reshape(R, N_DEV, -1).transpose(1, 0, 2)


def _slabs_to_cols(s):
    return s.transpose(1, 0, 2).reshape(s.shape[1], -1)


def kernel(x, attn_norm_w, w_in, conv_w, conv_b, dt_bias, a_log, d_skip, ssd_norm_w, cmp_w1_k, cmp_w2_k, cmp_w1_v, cmp_w2_v, cmp_pe_k, cmp_pe_v, w_out, ffn_norm_w, w_gate, w_up, w_down, final_norm_w, loss_target, m_attn_norm_w, m_w_in, m_conv_w, m_conv_b, m_dt_bias, m_a_log, m_d_skip, m_ssd_norm_w, m_cmp_w1_k, m_cmp_w2_k, m_cmp_w1_v, m_cmp_w2_v, m_cmp_pe_k, m_cmp_pe_v, m_w_out, m_ffn_norm_w, m_w_gate, m_w_up, m_w_down, m_final_norm_w, v_attn_norm_w, v_w_in, v_conv_w, v_conv_b, v_dt_bias, v_a_log, v_d_skip, v_ssd_norm_w, v_cmp_w1_k, v_cmp_w2_k, v_cmp_w1_v, v_cmp_w2_v, v_cmp_pe_k, v_cmp_pe_v, v_w_out, v_ffn_norm_w, v_w_gate, v_w_up, v_w_down, v_final_norm_w):
    a = dict(locals())
    me = _flat_id(*_my_coords())

    shard = {n: a[n][0].astype(_MXU) for n in _BIG}
    early_small = [cmp_w2_k[0], cmp_w2_v[0], conv_w[0]]
    st_early = _split_start([shard[n] for n in _EARLY] + early_small, False, "gather_early_start")
    zero = st_early["token"][0, 0].astype(_MXU)
    st_late = _split_start([shard[_LATE[0]] + zero] + [shard[n] for n in _LATE[1:]], False, "gather_late_start")

    def assemble(n, t):
        return _slabs_to_cols(t) if n in _COL_SHARDED else t.reshape(-1, t.shape[-1])

    p = dict(attn_norm_w=attn_norm_w, conv_b=conv_b, dt_bias=dt_bias, a_log=a_log, d_skip=d_skip, ssd_norm_w=ssd_norm_w,
             cmp_pe_k=cmp_pe_k.reshape(1, -1), cmp_pe_v=cmp_pe_v.reshape(1, -1), ffn_norm_w=ffn_norm_w,
             final_norm_w=final_norm_w.reshape(1, -1))

    def early_weights(after):
        got = _split_wait(st_early, False, (after, st_late["token"]), "gather_early_wait")
        w_main, w_small = _split_w_in(assemble("w_in", got[0]))
        return dict(w_main=w_main, w_small=w_small, cmp_w1_k=assemble("cmp_w1_k", got[1]), cmp_w1_v=assemble("cmp_w1_v", got[2]),
                    cmp_w2_k=assemble("cmp_w2_k", got[3]).astype(_MXU), cmp_w2_v=assemble("cmp_w2_v", got[4]).astype(_MXU),
                    conv_w=_slabs_to_cols(got[5]))

    def late_weights(after):
        got_late = _split_wait(st_late, False, (after,), "gather_late_wait")
        return {n: assemble(n, t) for n, t in zip(_LATE, got_late)}

    def slabs_of(g, names):
        return [(_cols_to_slabs(g[n]) if n in _COL_SHARDED else g[n].reshape(N_DEV, -1, g[n].shape[-1])).astype(_MXU) for n in names]

    started = []

    def grads_ready(names, g):
        if "w_in" in names:
            g = {**g, "w_in": _merge_w_in(g["w_main"], g["w_small"])}
        started.append((names, _split_start(slabs_of(g, names), True, "scatter_grads_start_%d" % len(started))))
        return started[-1][1]["token"][0:1, 0:1]

    loss_part, grad_x, g = _local_step(x[0], loss_target[0], p, early_weights, late_weights, grads_ready)
    loss = lax.psum(loss_part[0, 0], ("x", "y", "c"))

    out, after = {}, (started[-1][1]["token"],)
    for i, (names, st) in enumerate(started):
        if i == len(started) - 1:
            after = after + (grad_x,)
        received = _split_wait(st, True, after, "scatter_grads_wait_%d" % i)
        for n, parts in zip(names, received):
            out[n] = _adam_sum(parts, a[n][0], a["m_" + n][0], a["v_" + n][0], "adam_" + n)
        after = (out[names[-1]][0],)

    small_names = _REPLICATED + _SMALL_SHARDED
    g_small = _pack([g[n] for n in small_names])
    g_sum = _sum_parts(_exchange([g_small], False, "gather_small_grads", after=(received[0],))[0])
    gs = dict(zip(small_names, _unpack(g_sum, [_SMALL_FULL_SHAPES[n] for n in small_names])))
    gs["conv_w"] = lax.dynamic_slice_in_dim(gs["conv_w"], me * conv_w.shape[2], conv_w.shape[2], axis=1)
    gs["cmp_w2_k"] = lax.dynamic_slice_in_dim(gs["cmp_w2_k"], me * cmp_w2_k.shape[1], cmp_w2_k.shape[1], axis=0)
    gs["cmp_w2_v"] = lax.dynamic_slice_in_dim(gs["cmp_w2_v"], me * cmp_w2_v.shape[1], cmp_w2_v.shape[1], axis=0)
    packs = [_pack([t[n] for n in small_names]) for t in
             (gs, a, {n: a["m_" + n] for n in small_names}, {n: a["v_" + n] for n in small_names})]
    res_small = _adam_sum(packs[0][None], packs[1], packs[2], packs[3], "adam_small")
    shapes = [a[n].shape for n in small_names]
    unpacked = [dict(zip(small_names, _unpack(r, shapes))) for r in res_small]
    for n in small_names:
        out[n] = tuple(u[n] for u in unpacked)

    outs = [loss, grad_x[None]]
    for j in range(4):
        for n in _WEIGHTS:
            outs.append(out[n][j].reshape(a[n].shape))
    return tuple(outs)
```

```python
import functools

import numpy as np
import jax
import jax.numpy as jnp
from jax import lax
from jax.experimental import pallas as pl
from jax.experimental.pallas import tpu as pltpu

F32 = jnp.float32
_MXU = jnp.bfloat16
_HI = lax.Precision.HIGHEST

N_DEV = 8
D_MODEL = 2048
SSD_WIDTH = 1024
ATT_WIDTH = 1024
SSD_HEADS = 16
SSD_P = 64
SSD_N = 128
SSD_L = 128
SSD_G = 2
CONV_CH = 1536
CONV_K = 4
HD = 64
N_HEADS = 16
N_KV = 4
GRP = 4
CMP_HID = 256
SEL_BLOCK = 64
N_SELECT = 16
WINDOW = 512
ROPE_DIM = 16
ROPE_THETA = 500000.0
D_FF = 5632
EPS = 1e-6
NEG = -1e30
FORCE = 1e4
SCALE = HD ** -0.5
D_IN = 5184
W_MAIN = 5120
W_SMALL = 128
VMEM_LIMIT = 52 * 1024 * 1024

ADAM_LR, ADAM_B1, ADAM_B2, ADAM_EPS, ADAM_WD, ADAM_STEP = 0.001, 0.9, 0.999, 1e-08, 0.01, 10


def _pick(n, cands):
    for c in cands:
        if n % c == 0:
            return c
    return n


def _cp(sem=None):
    return pltpu.CompilerParams(dimension_semantics=sem, vmem_limit_bytes=VMEM_LIMIT)


def _sigmoid(x):
    return 1.0 / (1.0 + jnp.exp(-x))


def _dot(a, b, dims, hi=False):
    dn = {"nn": (((1,), (0,)), ((), ())), "nt": (((1,), (1,)), ((), ())), "tn": (((0,), (0,)), ((), ()))}[dims]
    if hi:
        return lax.dot_general(a.astype(F32), b.astype(F32), dn, precision=_HI, preferred_element_type=F32)
    return lax.dot_general(a.astype(_MXU), b.astype(_MXU), dn, preferred_element_type=F32)


LANE = 128
MM_TILE = 1024
MM_K_WHOLE = 2048
MM_K_STEP = 1536
TN_ACC_ELEMS = 3 * 2 ** 20
TN_K_STEP = 512


def _largest_tile(n, cap):
    if n <= cap:
        return n
    best = LANE
    for t in range(LANE, cap + 1, LANE):
        if n % t == 0:
            best = t
    return best


def _mm_tiles(mode, M, N, K):
    if mode == "tn":
        tm = _largest_tile(M, 2 * MM_TILE)
        return tm, _largest_tile(N, TN_ACC_ELEMS // tm), _largest_tile(K, TN_K_STEP)
    tk = K if K <= MM_K_WHOLE else _largest_tile(K, MM_K_STEP)
    return _largest_tile(M, MM_TILE), _largest_tile(N, MM_TILE), tk


def _mm(a, b, mode, out_dtype, name, res=None, after=None):
    if mode == "nn":
        (M, K), N = a.shape, b.shape[1]
    elif mode == "nt":
        (M, K), N = a.shape, b.shape[0]
    else:
        (K, M), N = a.shape, b.shape[1]
    tm, tn, tk = _mm_tiles(mode, M, N, K)
    nk = K // tk
    a_spec = pl.BlockSpec((tk, tm), lambda i, j, k: (k, i)) if mode == "tn" else pl.BlockSpec((tm, tk), lambda i, j, k: (i, k))
    b_spec = pl.BlockSpec((tn, tk), lambda i, j, k: (j, k)) if mode == "nt" else pl.BlockSpec((tk, tn), lambda i, j, k: (k, j))
    o_spec = pl.BlockSpec((tm, tn), lambda i, j, k: (i, j))

    def finish(r, r_ref, o_ref):
        if res is not None:
            r = r + r_ref[...].astype(F32)
        o_ref[...] = r.astype(out_dtype)

    def body_one_step(*refs):
        a_ref, b_ref, o_ref = refs[0], refs[1], refs[-1]
        finish(_dot(a_ref[...], b_ref[...], mode), refs[2], o_ref)

    def body(*refs):
        a_ref, b_ref, o_ref, acc = refs[0], refs[1], refs[-2], refs[-1]
        k = pl.program_id(2)

        @pl.when(k == 0)
        def _():
            acc[...] = jnp.zeros_like(acc)

        acc[...] += _dot(a_ref[...], b_ref[...], mode)

        @pl.when(k == nk - 1)
        def _():
            finish(acc[...], refs[2], o_ref)

    ins, specs = [a, b], [a_spec, b_spec]
    if res is not None:
        ins.append(res)
        specs.append(o_spec)
    if after is not None:
        ins.append(after)
        specs.append(pl.BlockSpec(memory_space=pl.ANY))
    return pl.pallas_call(
        body_one_step if nk == 1 else body, name=name, grid=(M // tm, N // tn, nk), in_specs=specs, out_specs=o_spec,
        out_shape=jax.ShapeDtypeStruct((M, N), out_dtype), scratch_shapes=[] if nk == 1 else [pltpu.VMEM((tm, tn), F32)],
        compiler_params=_cp(("parallel", "parallel", "arbitrary")))(*ins)


def _ffn_up(v, w_gate, w_up):
    S, D = v.shape
    F = w_gate.shape[1]
    tm, tn = _largest_tile(S, MM_TILE), _largest_tile(F, MM_TILE // 2)

    def body(v_ref, wg_ref, wu_ref, gt_ref, up_ref, act_ref):
        vv = v_ref[...]
        g = _dot(vv, wg_ref[...], "nn")
        u = _dot(vv, wu_ref[...], "nn")
        gt_ref[...] = g
        up_ref[...] = u
        act_ref[...] = (g * _sigmoid(g) * u).astype(act_ref.dtype)

    o_spec = pl.BlockSpec((tm, tn), lambda i, j: (i, j))
    w_spec = pl.BlockSpec((D, tn), lambda i, j: (0, j))
    return pl.pallas_call(
        body, name="ffn_up", grid=(S // tm, F // tn),
        in_specs=[pl.BlockSpec((tm, D), lambda i, j: (i, 0)), w_spec, w_spec], out_specs=[o_spec, o_spec, o_spec],
        out_shape=[jax.ShapeDtypeStruct((S, F), F32), jax.ShapeDtypeStruct((S, F), F32), jax.ShapeDtypeStruct((S, F), _MXU)],
        compiler_params=_cp(("parallel", "parallel")))(v, w_gate, w_up)


def _ffn_dact(dh2, w_down, gt, up):
    S, D = dh2.shape
    F = w_down.shape[0]
    tm, tn = _largest_tile(S, MM_TILE), _largest_tile(F, MM_TILE // 2)

    def body(d_ref, w_ref, gt_ref, up_ref, dg_ref, du_ref):
        da, g, u = _dot(d_ref[...], w_ref[...], "nt"), gt_ref[...], up_ref[...]
        s = _sigmoid(g)
        dg_ref[...] = (da * u * (s * (1.0 + g * (1.0 - s)))).astype(dg_ref.dtype)
        du_ref[...] = (da * (g * s)).astype(du_ref.dtype)

    o_spec = pl.BlockSpec((tm, tn), lambda i, j: (i, j))
    return pl.pallas_call(
        body, name="ffn_dact", grid=(S // tm, F // tn),
        in_specs=[pl.BlockSpec((tm, D), lambda i, j: (i, 0)), pl.BlockSpec((tn, D), lambda i, j: (j, 0)), o_spec, o_spec],
        out_specs=[o_spec, o_spec],
        out_shape=[jax.ShapeDtypeStruct((S, F), _MXU), jax.ShapeDtypeStruct((S, F), _MXU)],
        compiler_params=_cp(("parallel", "parallel")))(dh2, w_down, gt, up)


def _rms_fwd(x, w, name):
    S, D = x.shape
    tr = _pick(S, (256, 128))

    def body(x_ref, w_ref, xn_ref, rs_ref):
        xv = x_ref[...]
        rs = lax.rsqrt(jnp.mean(xv * xv, axis=-1, keepdims=True) + EPS)
        xn_ref[...] = ((xv * rs) * w_ref[...]).astype(xn_ref.dtype)
        rs_ref[...] = rs

    return pl.pallas_call(
        body, name=name, grid=(S // tr,),
        in_specs=[pl.BlockSpec((tr, D), lambda i: (i, 0)), pl.BlockSpec((1, D), lambda i: (0, 0))],
        out_specs=[pl.BlockSpec((tr, D), lambda i: (i, 0)), pl.BlockSpec((tr, 1), lambda i: (i, 0))],
        out_shape=[jax.ShapeDtypeStruct((S, D), _MXU), jax.ShapeDtypeStruct((S, 1), F32)],
        compiler_params=_cp(("parallel",)))(x, w)


def _rms_bwd(dyn, x, rs, w, res, name):
    S, D = x.shape
    tr = _pick(S, (256, 128))

    def body(dy_ref, x_ref, rs_ref, w_ref, res_ref, dx_ref, dxb_ref, dw_ref):
        @pl.when(pl.program_id(0) == 0)
        def _():
            dw_ref[...] = jnp.zeros_like(dw_ref)

        dy, r = dy_ref[...].astype(F32), rs_ref[...]
        xhat = x_ref[...] * r
        dw_ref[...] += jnp.sum(dy * xhat, axis=0, keepdims=True)
        dxhat = dy * w_ref[...]
        dx = res_ref[...] + r * (dxhat - xhat * jnp.mean(dxhat * xhat, axis=-1, keepdims=True))
        dx_ref[...] = dx
        dxb_ref[...] = dx.astype(dxb_ref.dtype)

    row = pl.BlockSpec((tr, D), lambda i: (i, 0))
    vec = pl.BlockSpec((1, D), lambda i: (0, 0))
    return pl.pallas_call(
        body, name=name, grid=(S // tr,),
        in_specs=[row, row, pl.BlockSpec((tr, 1), lambda i: (i, 0)), vec, row], out_specs=[row, row, vec],
        out_shape=[jax.ShapeDtypeStruct((S, D), F32), jax.ShapeDtypeStruct((S, D), _MXU), jax.ShapeDtypeStruct((1, D), F32)],
        compiler_params=_cp(("arbitrary",)))(dyn, x, rs, w, res)


def _final_loss(h2, w, tgt):
    S, D = h2.shape
    tr = _pick(S, (256, 128))

    def body(h_ref, w_ref, t_ref, loss_ref, dh_ref, dhb_ref, dw_ref):
        @pl.when(pl.program_id(0) == 0)
        def _():
            dw_ref[...] = jnp.zeros_like(dw_ref)
            loss_ref[...] = jnp.zeros_like(loss_ref)

        hv, wv = h_ref[...], w_ref[...]
        rs = lax.rsqrt(jnp.mean(hv * hv, axis=-1, keepdims=True) + EPS)
        xhat = hv * rs
        err = xhat * wv - t_ref[...]
        row = jnp.mean(err * err, axis=-1, keepdims=True)
        loss_ref[...] += 0.5 * jnp.sum(row, axis=0, keepdims=True)
        dy = err * (1.0 / D)
        dw_ref[...] += jnp.sum(dy * xhat, axis=0, keepdims=True)
        dxhat = dy * wv
        dh = rs * (dxhat - xhat * jnp.mean(dxhat * xhat, axis=-1, keepdims=True))
        dh_ref[...] = dh
        dhb_ref[...] = dh.astype(dhb_ref.dtype)

    row = pl.BlockSpec((tr, D), lambda i: (i, 0))
    vec = pl.BlockSpec((1, D), lambda i: (0, 0))
    return pl.pallas_call(
        body, name="final_loss", grid=(S // tr,), in_specs=[row, vec, row],
        out_specs=[pl.BlockSpec((1, 1), lambda i: (0, 0)), row, row, vec],
        out_shape=[jax.ShapeDtypeStruct((1, 1), F32), jax.ShapeDtypeStruct((S, D), F32), jax.ShapeDtypeStruct((S, D), _MXU),
                   jax.ShapeDtypeStruct((1, D), F32)],
        compiler_params=_cp(("arbitrary",)))(h2, w, tgt)


def _shift_rows(x, k, rows):
    if k == 0:
        return x
    S = x.shape[0]
    r = pltpu.roll(x, k % S, axis=0)
    ok = (rows >= k) if k > 0 else (rows < S + k)
    return jnp.where(ok, r, 0.0)


XBC_COL0 = SSD_WIDTH // 128


def _conv_fwd(proj, conv_w, conv_b):
    S = proj.shape[0]
    nct = CONV_CH // 128

    def body(x_ref, w_ref, b_ref, o_ref):
        x = x_ref[...]
        rows = lax.broadcasted_iota(jnp.int32, x.shape, 0)
        c = b_ref[...] + w_ref[3:4, :] * x
        for k in range(1, CONV_K):
            c = c + w_ref[3 - k:4 - k, :] * _shift_rows(x, k, rows)
        o_ref[...] = c * _sigmoid(c)

    return pl.pallas_call(
        body, name="conv_fwd", grid=(nct,),
        in_specs=[pl.BlockSpec((S, 128), lambda j: (0, XBC_COL0 + j)), pl.BlockSpec((CONV_K, 128), lambda j: (0, j)),
                  pl.BlockSpec((1, 128), lambda j: (0, j))],
        out_specs=pl.BlockSpec((S, 128), lambda j: (0, j)),
        out_shape=jax.ShapeDtypeStruct((S, CONV_CH), F32), compiler_params=_cp(("parallel",)))(proj, conv_w, conv_b)


def _conv_bwd(proj, conv_w, conv_b, dxa):
    S = proj.shape[0]
    nct = CONV_CH // 128

    def body(x_ref, w_ref, b_ref, d_ref, dx_ref, dw_ref, db_ref):
        x = x_ref[...]
        rows = lax.broadcasted_iota(jnp.int32, x.shape, 0)
        xs = [_shift_rows(x, k, rows) for k in range(CONV_K)]
        c = b_ref[...] + w_ref[3:4, :] * x
        for k in range(1, CONV_K):
            c = c + w_ref[3 - k:4 - k, :] * xs[k]
        s = _sigmoid(c)
        dc = d_ref[...] * (s * (1.0 + c * (1.0 - s)))
        dx = w_ref[3:4, :] * dc
        for k in range(1, CONV_K):
            dx = dx + w_ref[3 - k:4 - k, :] * _shift_rows(dc, -k, rows)
        dx_ref[...] = dx.astype(dx_ref.dtype)
        for k in range(CONV_K):
            dw_ref[3 - k:4 - k, :] = jnp.sum(dc * xs[k], axis=0, keepdims=True)
        db_ref[...] = jnp.sum(dc, axis=0, keepdims=True)

    col = pl.BlockSpec((S, 128), lambda j: (0, j))
    return pl.pallas_call(
        body, name="conv_bwd", grid=(nct,),
        in_specs=[pl.BlockSpec((S, 128), lambda j: (0, XBC_COL0 + j)), pl.BlockSpec((CONV_K, 128), lambda j: (0, j)),
                  pl.BlockSpec((1, 128), lambda j: (0, j)), col],
        out_specs=[col, pl.BlockSpec((CONV_K, 128), lambda j: (0, j)), pl.BlockSpec((1, 128), lambda j: (0, j))],
        out_shape=[jax.ShapeDtypeStruct((S, CONV_CH), _MXU), jax.ShapeDtypeStruct((CONV_K, CONV_CH), F32),
                   jax.ShapeDtypeStruct((1, CONV_CH), F32)],
        compiler_params=_cp(("parallel",)))(proj, conv_w, conv_b, dxa)


def _ssd_consts():
    L = SSD_L
    r = lax.broadcasted_iota(jnp.int32, (L, L), 0)
    c = lax.broadcasted_iota(jnp.int32, (L, L), 1)
    causal = r >= c
    upper = (r <= c).astype(F32)
    hr = lax.broadcasted_iota(jnp.int32, (SSD_HEADS, SSD_WIDTH), 0)
    hc = lax.broadcasted_iota(jnp.int32, (SSD_HEADS, SSD_WIDTH), 1)
    expand = (lax.shift_right_logical(hc, 6) == hr).astype(F32)
    return causal, causal.astype(F32), upper, expand


def _softplus(x):
    return jnp.maximum(x, 0.0) + jnp.log(1.0 + jnp.exp(-jnp.abs(x)))


def _ssd_scalars(dtr, dt_bias, a_log, tri, upper, expand):
    dt = _softplus(dtr + dt_bias)
    A = -jnp.exp(a_log)
    adt = dt * A
    acum = _dot(tri, adt, "nn", hi=True)
    acum_t = _dot(adt, upper, "tn", hi=True)
    alast = acum[SSD_L - 1:SSD_L, :]
    e = jnp.exp(acum)
    wdec = jnp.exp(alast - acum)
    gam = jnp.exp(alast)
    ex = lambda t: _dot(t, expand, "nn", hi=True)
    gam8 = jnp.broadcast_to(gam, (8, SSD_HEADS))
    return dt, A, acum, acum_t, e, wdec, gam, ex(dt), ex(e), ex(wdec), ex(gam8)[0:1, :]


def _ssd_fwd(proj, proj_small, xa, dt_bias, a_log, d_skip, norm_w):
    S = proj.shape[0]
    L, N, W = SSD_L, SSD_N, SSD_WIDTH
    nc = S // L

    def body(z_ref, xa_ref, dtr_ref, dtb_ref, al_ref, dsk_ref, nw_ref, yo_ref, y_ref, rs_ref, hs_ref, h_scr, y_scr):
        @pl.when(pl.program_id(0) == 0)
        def _():
            h_scr[...] = jnp.zeros_like(h_scr)

        causal, tri, upper, expand = _ssd_consts()
        dt, A, acum, acum_t, e, wdec, gam, dtE, eE, wE, gamE = _ssd_scalars(dtr_ref[:, 0:SSD_HEADS], dtb_ref[...], al_ref[...], tri, upper, expand)
        xs = xa_ref[:, 0:W]
        X = xs * dtE
        XW = X * wE
        hs_ref[0] = h_scr[...]
        for g in range(SSD_G):
            gs = slice(g * 512, (g + 1) * 512)
            Bg = xa_ref[:, W + g * N:W + (g + 1) * N]
            Cg = xa_ref[:, W + SSD_G * N + g * N:W + SSD_G * N + (g + 1) * N]
            Hg = h_scr[:, gs]
            CB = _dot(Cg, Bg, "nt")
            yoff = _dot(Cg, Hg, "nn") * eE[:, gs]
            st = _dot(Bg, XW[:, gs], "tn")
            for j in range(8):
                h = g * 8 + j
                hsl = slice(h * SSD_P, (h + 1) * SSD_P)
                lam = jnp.exp(jnp.where(causal, acum[:, h:h + 1] - acum_t[h:h + 1, :], -jnp.inf))
                y_scr[:, hsl] = _dot(CB * lam, X[:, hsl], "nn") + yoff[:, j * SSD_P:(j + 1) * SSD_P]
            h_scr[:, gs] = gamE[:, gs] * Hg + st
        dskE = _dot(jnp.broadcast_to(dsk_ref[...], (8, SSD_HEADS)), expand, "nn", hi=True)[0:1, :]
        y = y_scr[...] + dskE * xs
        y_ref[...] = y
        zv = z_ref[...]
        yg = y * (zv * _sigmoid(zv))
        rs = lax.rsqrt(jnp.mean(yg * yg, axis=-1, keepdims=True) + EPS)
        rs_ref[...] = rs
        yo_ref[...] = ((yg * rs) * nw_ref[...]).astype(yo_ref.dtype)

    p16 = pl.BlockSpec((1, SSD_HEADS), lambda c: (0, 0))
    return pl.pallas_call(
        body, name="ssd_fwd", grid=(nc,),
        in_specs=[pl.BlockSpec((L, W), lambda c: (c, 0)), pl.BlockSpec((L, CONV_CH), lambda c: (c, 0)),
                  pl.BlockSpec((L, W_SMALL), lambda c: (c, 0)), p16, p16, p16, pl.BlockSpec((1, W), lambda c: (0, 0))],
        out_specs=[pl.BlockSpec((L, W), lambda c: (c, 0)), pl.BlockSpec((L, W), lambda c: (c, 0)),
                   pl.BlockSpec((L, 1), lambda c: (c, 0)), pl.BlockSpec((1, N, W), lambda c: (c, 0, 0))],
        out_shape=[jax.ShapeDtypeStruct((S, W), _MXU), jax.ShapeDtypeStruct((S, W), F32), jax.ShapeDtypeStruct((S, 1), F32),
                   jax.ShapeDtypeStruct((nc, N, W), F32)],
        scratch_shapes=[pltpu.VMEM((N, W), F32), pltpu.VMEM((L, W), F32)],
        compiler_params=_cp(("arbitrary",)))(proj, xa, proj_small, dt_bias, a_log, d_skip, norm_w)


def _ssd_bwd(dmixed, proj, proj_small, xa, y, rs2, hs, dt_bias, a_log, d_skip, norm_w):
    S = proj.shape[0]
    L, N, W, H = SSD_L, SSD_N, SSD_WIDTH, SSD_HEADS
    nc = S // L

    def body(dyo_ref, z_ref, xa_ref, dtr_ref, y_ref, rs_ref, hs_ref, dtb_ref, al_ref, dsk_ref, nw_ref,
             dz_ref, dxa_ref, ddtr_ref, ddtb_ref, dal_ref, ddsk_ref, dnw_ref, dh_scr, dx_scr):
        @pl.when(pl.program_id(0) == 0)
        def _():
            dh_scr[...] = jnp.zeros_like(dh_scr)
            ddtb_ref[...] = jnp.zeros_like(ddtb_ref)
            dal_ref[...] = jnp.zeros_like(dal_ref)
            ddsk_ref[...] = jnp.zeros_like(ddsk_ref)
            dnw_ref[...] = jnp.zeros_like(dnw_ref)

        causal, tri, upper, expand = _ssd_consts()
        heads = lambda t: _dot(t, expand, "nt", hi=True)
        onehot = lambda h: (lax.broadcasted_iota(jnp.int32, (1, H), 1) == h).astype(F32)

        zv, yv, rs = z_ref[...], y_ref[...], rs_ref[...]
        sz = _sigmoid(zv)
        zs = zv * sz
        xhat = (yv * zs) * rs
        dyo = dyo_ref[...].astype(F32)
        dnw_ref[...] += jnp.sum(dyo * xhat, axis=0, keepdims=True)
        dxhat = dyo * nw_ref[...]
        dyg = rs * (dxhat - xhat * jnp.mean(dxhat * xhat, axis=-1, keepdims=True))
        dz_ref[...] = (dyg * yv * (sz * (1.0 + zv * (1.0 - sz)))).astype(dz_ref.dtype)
        dy = dyg * zs

        dtr = dtr_ref[:, 0:H]
        dt, A, acum, acum_t, e, wdec, gam, dtE, eE, wE, gamE = _ssd_scalars(dtr, dtb_ref[...], al_ref[...], tri, upper, expand)
        xs = xa_ref[:, 0:W]
        X = xs * dtE
        XW = X * wE
        dskE = _dot(jnp.broadcast_to(dsk_ref[...], (8, H)), expand, "nn", hi=True)[0:1, :]
        ddsk_ref[...] += heads(jnp.broadcast_to(jnp.sum(dy * xs, axis=0, keepdims=True), (8, W)))[0:1, :]

        dYe = dy * eE
        dacum = jnp.zeros((L, H), F32)
        de_full = []
        dw_full = []
        dgam_full = []
        for g in range(SSD_G):
            gs = slice(g * 512, (g + 1) * 512)
            Bg = xa_ref[:, W + g * N:W + (g + 1) * N]
            Cg = xa_ref[:, W + SSD_G * N + g * N:W + SSD_G * N + (g + 1) * N]
            Hg = hs_ref[0, :, gs]
            dHn = dh_scr[:, gs]
            CH = _dot(Cg, Hg, "nn")
            de_full.append(dy[:, gs] * CH)
            dC = _dot(dYe[:, gs], Hg, "nt")
            dHs = gamE[:, gs] * dHn + _dot(Cg, dYe[:, gs], "tn")
            dgam_full.append(jnp.sum(dHn * Hg, axis=0, keepdims=True))
            BdS = _dot(Bg, dHn, "nn")
            dB = _dot(XW[:, gs], dHn, "nt")
            dx_scr[:, gs] = BdS * wE[:, gs]
            dw_full.append(BdS * X[:, gs])
            CB = _dot(Cg, Bg, "nt")
            dCB = jnp.zeros((L, L), F32)
            for j in range(8):
                h = g * 8 + j
                hsl = slice(h * SSD_P, (h + 1) * SSD_P)
                lam = jnp.exp(jnp.where(causal, acum[:, h:h + 1] - acum_t[h:h + 1, :], -jnp.inf))
                M = CB * lam
                dM = _dot(dy[:, hsl], X[:, hsl], "nt")
                dx_scr[:, hsl] += _dot(M, dy[:, hsl], "tn")
                dCB = dCB + dM * lam
                Q = dM * M
                rowsum = jnp.sum(Q, axis=1, keepdims=True)
                colsum = _dot(Q, jnp.ones((L, 8), F32), "tn", hi=True)[:, 0:1]
                dacum = dacum + (rowsum - colsum) * onehot(h)
            dC = dC + _dot(dCB, Bg, "nn")
            dB = dB + _dot(dCB, Cg, "tn")
            dxa_ref[:, W + g * N:W + (g + 1) * N] = dB
            dxa_ref[:, W + SSD_G * N + g * N:W + SSD_G * N + (g + 1) * N] = dC
            dh_scr[:, gs] = dHs

        de16 = heads(jnp.concatenate(de_full, axis=1))
        dw16 = heads(jnp.concatenate(dw_full, axis=1))
        dgam16 = heads(jnp.broadcast_to(jnp.concatenate(dgam_full, axis=1), (8, W)))[0:1, :]
        dacum = dacum + de16 * e - dw16 * wdec
        dlast = jnp.sum(dw16 * wdec, axis=0, keepdims=True) + dgam16 * gam
        lastrow = (lax.broadcasted_iota(jnp.int32, (L, 1), 0) == L - 1).astype(F32)
        dacum = dacum + lastrow * dlast
        da = _dot(tri, dacum, "tn", hi=True)
        dX = dx_scr[...]
        ddt = da * A + heads(dX * xs)
        dA = jnp.sum(da * dt, axis=0, keepdims=True)
        dal_ref[...] += dA * A
        ddtr = ddt * _sigmoid(dtr + dtb_ref[...])
        ddtb_ref[...] += jnp.sum(ddtr, axis=0, keepdims=True)
        ddtr_ref[...] = ddtr
        dxa_ref[:, 0:W] = dX * dtE + dy * dskE

    p16 = pl.BlockSpec((1, H), lambda c: (0, 0))
    rev = lambda c: (nc - 1 - c, 0)
    return pl.pallas_call(
        body, name="ssd_bwd", grid=(nc,),
        in_specs=[pl.BlockSpec((L, W), rev), pl.BlockSpec((L, W), rev), pl.BlockSpec((L, CONV_CH), rev),
                  pl.BlockSpec((L, W_SMALL), rev), pl.BlockSpec((L, W), rev), pl.BlockSpec((L, 1), rev),
                  pl.BlockSpec((1, N, W), lambda c: (nc - 1 - c, 0, 0)), p16, p16, p16, pl.BlockSpec((1, W), lambda c: (0, 0))],
        out_specs=[pl.BlockSpec((L, W), rev), pl.BlockSpec((L, CONV_CH), rev), pl.BlockSpec((L, H), rev),
                   p16, p16, p16, pl.BlockSpec((1, W), lambda c: (0, 0))],
        out_shape=[jax.ShapeDtypeStruct((S, W), _MXU), jax.ShapeDtypeStruct((S, CONV_CH), F32), jax.ShapeDtypeStruct((S, H), F32),
                   jax.ShapeDtypeStruct((1, H), F32), jax.ShapeDtypeStruct((1, H), F32), jax.ShapeDtypeStruct((1, H), F32),
                   jax.ShapeDtypeStruct((1, W), F32)],
        scratch_shapes=[pltpu.VMEM((N, W), F32), pltpu.VMEM((L, W), F32)],
        compiler_params=_cp(("arbitrary",)))(dmixed, proj, xa, proj_small, y, rs2, hs, dt_bias, a_log, d_skip, norm_w)


def _rope_tables(S):
    inv = 1.0 / (ROPE_THETA ** (jnp.arange(0, ROPE_DIM, 2, dtype=F32) / ROPE_DIM))
    ang = jnp.arange(S, dtype=F32)[:, None] * inv[None, :]
    cos, sin = jnp.cos(ang), jnp.sin(ang)
    half = ROPE_DIM // 2
    c64 = jnp.concatenate([cos, cos, jnp.ones((S, HD - ROPE_DIM), F32)], axis=1)
    s64 = jnp.concatenate([sin, sin, jnp.zeros((S, HD - ROPE_DIM), F32)], axis=1)
    del half
    return jnp.concatenate([c64, c64], axis=1), jnp.concatenate([s64, s64], axis=1)


def _rope(xs, blk0, width, cos, sin, sign, out_dtype, name, extra=None):
    S = xs[0].shape[0]
    tr = _pick(S, (512, 256, 128))
    nx = len(xs)

    def body(*refs):
        x_refs, c_ref, s_ref = refs[:nx], refs[nx], refs[nx + 1]
        e_ref = refs[nx + 2] if extra is not None else None
        o_ref = refs[-1]
        cv, sv = c_ref[...], s_ref[...] * sign
        lane = lax.broadcasted_iota(jnp.int32, (tr, 128), 1)
        first = (lane & (HD - 1)) < (ROPE_DIM // 2)
        for j in range(2):
            cs = slice(j * 128, (j + 1) * 128)
            xv = x_refs[0][:, cs].astype(F32)
            for r in x_refs[1:]:
                xv = xv + r[:, cs].astype(F32)
            rot = jnp.where(first, -pltpu.roll(xv, 128 - ROPE_DIM // 2, axis=1), pltpu.roll(xv, ROPE_DIM // 2, axis=1))
            out = xv * cv + rot * sv
            if extra is not None:
                out = out + e_ref[:, cs].astype(F32)
            o_ref[:, cs] = out.astype(out_dtype)

    t128 = pl.BlockSpec((tr, 128), lambda i, j: (i, 0))
    oblk = pl.BlockSpec((tr, 256), lambda i, j: (i, j))
    specs = [pl.BlockSpec((tr, 256), lambda i, j: (i, blk0 + j))] * nx + [t128, t128]
    ins = list(xs) + [cos, sin]
    if extra is not None:
        ins.append(extra[0])
        eb = extra[1]
        specs.append(pl.BlockSpec((tr, 256), lambda i, j: (i, eb + j)))
    return pl.pallas_call(
        body, name=name, grid=(S // tr, width // 256), in_specs=specs, out_specs=oblk,
        out_shape=jax.ShapeDtypeStruct((S, width), out_dtype), compiler_params=_cp(("parallel", "parallel")))(*ins)


def _rotate128(xv, cv, sv, first):
    rot = jnp.where(first, -pltpu.roll(xv, 128 - ROPE_DIM // 2, axis=1), pltpu.roll(xv, ROPE_DIM // 2, axis=1))
    return xv * cv + rot * sv


def _kv_prep(proj, cos, sin, tk):
    S = proj.shape[0]

    def body(ks_ref, vs_ref, kw_ref, vw_ref, c_ref, s_ref, *outs):
        cv, sv = c_ref[...], s_ref[...]
        lane = lax.broadcasted_iota(jnp.int32, (tk, 128), 1)
        first = (lane & (HD - 1)) < (ROPE_DIM // 2)
        for j, (ref, rotated) in enumerate(((ks_ref, True), (vs_ref, False), (kw_ref, True), (vw_ref, False))):
            nat, blk = outs[2 * j], outs[2 * j + 1]
            for half in range(2):
                xv = ref[:, half * 128:(half + 1) * 128]
                if rotated:
                    xv = _rotate128(xv, cv, sv, first)
                for e in range(2):
                    h = 2 * half + e
                    piece = xv[:, e * HD:(e + 1) * HD]
                    nat[h] = piece.astype(nat.dtype)
                    blk[h, 0] = piece.T.astype(blk.dtype)

    col = lambda b: pl.BlockSpec((tk, 256), lambda i: (i, b))
    t128 = pl.BlockSpec((tk, 128), lambda i: (i, 0))
    nat_spec = pl.BlockSpec((N_KV, tk, HD), lambda i: (0, i, 0))
    blk_spec = pl.BlockSpec((N_KV, 1, HD, tk), lambda i: (0, i, 0, 0))
    nat_shape = jax.ShapeDtypeStruct((N_KV, S, HD), _MXU)
    blk_shape = jax.ShapeDtypeStruct((N_KV, S // tk, HD, tk), _MXU)
    res = pl.pallas_call(
        body, name="kv_prep", grid=(S // tk,), in_specs=[col(KSB), col(VSB), col(KWB), col(VWB), t128, t128],
        out_specs=[nat_spec, blk_spec] * 4, out_shape=[nat_shape, blk_shape] * 4,
        compiler_params=_cp(("parallel",)))(proj, proj, proj, proj, cos, sin)
    return dict(ks=res[0], ks_t=res[1], vs=res[2], vs_t=res[3], kw=res[4], kw_t=res[5], vw=res[6], vw_t=res[7])


def _dkv_post(dks, dvs, dkw, dvw, cos, sin):
    S = dks.shape[1]
    tr = _pick(S, (512, 256, 128))

    def body(dks_ref, dvs_ref, dkw_ref, dvw_ref, c_ref, s_ref, o_ref):
        cv, sv = c_ref[...], -s_ref[...]
        lane = lax.broadcasted_iota(jnp.int32, (tr, 128), 1)
        first = (lane & (HD - 1)) < (ROPE_DIM // 2)
        for j, (ref, rotated) in enumerate(((dks_ref, True), (dvs_ref, False), (dkw_ref, True), (dvw_ref, False))):
            for half in range(2):
                xv = jnp.concatenate([ref[2 * half], ref[2 * half + 1]], axis=1)
                if rotated:
                    xv = _rotate128(xv, cv, sv, first)
                o_ref[:, j * 256 + half * 128:j * 256 + (half + 1) * 128] = xv.astype(o_ref.dtype)

    hm = pl.BlockSpec((N_KV, tr, HD), lambda i: (0, i, 0))
    t128 = pl.BlockSpec((tr, 128), lambda i: (i, 0))
    return pl.pallas_call(
        body, name="dkv_post", grid=(S // tr,), in_specs=[hm, hm, hm, hm, t128, t128],
        out_specs=pl.BlockSpec((tr, 4 * 256), lambda i: (i, 0)), out_shape=jax.ShapeDtypeStruct((S, 4 * 256), _MXU),
        compiler_params=_cp(("parallel",)))(dks, dvs, dkw, dvw, cos, sin)


def _compress_fwd(R, pe, w1, w2):
    NC = R.shape[1]
    half = 16 * HD

    def body(r_ref, pe_ref, w1_ref, w2_ref, o_ref, hid_ref):
        r = r_ref[0]
        a = _dot(r + pe_ref[:, 0:half], w1_ref[0:half, :], "nn")
        b = _dot(r + pe_ref[:, half:2 * half], w1_ref[half:2 * half, :], "nn")
        hid = a + pltpu.roll(b, NC - 1, axis=0)
        hid_ref[0] = hid
        out = _dot(hid * _sigmoid(hid), w2_ref[...], "nn")
        rows = lax.broadcasted_iota(jnp.int32, out.shape, 0)
        o_ref[0] = jnp.where(rows < NC - 1, out, 0.0).astype(o_ref.dtype)

    return pl.pallas_call(
        body, name="compress_fwd", grid=(N_KV,),
        in_specs=[pl.BlockSpec((1, NC, half), lambda h: (h, 0, 0)), pl.BlockSpec((1, 2 * half), lambda h: (0, 0)),
                  pl.BlockSpec((2 * half, CMP_HID), lambda h: (0, 0)), pl.BlockSpec((CMP_HID, HD), lambda h: (0, 0))],
        out_specs=[pl.BlockSpec((1, NC, HD), lambda h: (h, 0, 0)), pl.BlockSpec((1, NC, CMP_HID), lambda h: (h, 0, 0))],
        out_shape=[jax.ShapeDtypeStruct((N_KV, NC, HD), _MXU), jax.ShapeDtypeStruct((N_KV, NC, CMP_HID), F32)],
        compiler_params=_cp(("parallel",)))(R, pe, w1, w2)


def _compress_bwd(R, pe, w1, w2, hid, dout):
    NC = R.shape[1]
    half = 16 * HD

    def body(r_ref, pe_ref, w1_ref, w2_ref, hid_ref, do_ref, dr_ref, dw1_ref, dw2_ref, dpe_ref):
        @pl.when(pl.program_id(0) == 0)
        def _():
            dw1_ref[...] = jnp.zeros_like(dw1_ref)
            dw2_ref[...] = jnp.zeros_like(dw2_ref)
            dpe_ref[...] = jnp.zeros_like(dpe_ref)

        r, hv, do = r_ref[0], hid_ref[0], do_ref[0]
        s = _sigmoid(hv)
        dw2_ref[...] += _dot(hv * s, do, "tn")
        dhid = _dot(do, w2_ref[...], "nt") * (s * (1.0 + hv * (1.0 - s)))
        rows = lax.broadcasted_iota(jnp.int32, dhid.shape, 0)
        dhid = jnp.where(rows < NC - 1, dhid, 0.0)
        dhid_dn = pltpu.roll(dhid, 1, axis=0)
        dw1_ref[0:half, :] += _dot(r + pe_ref[:, 0:half], dhid, "tn")
        dw1_ref[half:2 * half, :] += _dot(r + pe_ref[:, half:2 * half], dhid_dn, "tn")
        dxt = _dot(dhid, w1_ref[0:half, :], "nt")
        dxb = _dot(dhid_dn, w1_ref[half:2 * half, :], "nt")
        dr_ref[0] = dxt + dxb
        dpe_ref[:, 0:half] += jnp.sum(dxt, axis=0, keepdims=True)
        dpe_ref[:, half:2 * half] += jnp.sum(dxb, axis=0, keepdims=True)

    return pl.pallas_call(
        body, name="compress_bwd", grid=(N_KV,),
        in_specs=[pl.BlockSpec((1, NC, half), lambda h: (h, 0, 0)), pl.BlockSpec((1, 2 * half), lambda h: (0, 0)),
                  pl.BlockSpec((2 * half, CMP_HID), lambda h: (0, 0)), pl.BlockSpec((CMP_HID, HD), lambda h: (0, 0)),
                  pl.BlockSpec((1, NC, CMP_HID), lambda h: (h, 0, 0)), pl.BlockSpec((1, NC, HD), lambda h: (h, 0, 0))],
        out_specs=[pl.BlockSpec((1, NC, half), lambda h: (h, 0, 0)), pl.BlockSpec((2 * half, CMP_HID), lambda h: (0, 0)),
                   pl.BlockSpec((CMP_HID, HD), lambda h: (0, 0)), pl.BlockSpec((1, 2 * half), lambda h: (0, 0))],
        out_shape=[jax.ShapeDtypeStruct((N_KV, NC, half), F32), jax.ShapeDtypeStruct((2 * half, CMP_HID), F32),
                   jax.ShapeDtypeStruct((CMP_HID, HD), F32), jax.ShapeDtypeStruct((1, 2 * half), F32)],
        compiler_params=_cp(("arbitrary",)))(R, pe, w1, w2, hid, dout)


def _attn_cfg(S, Sk, mode):
    tq = _pick(S, (256, 128))
    tk = Sk if mode == "cmp" else _pick(Sk, (256, 128))
    return tq, tk


def _kb_range(mode, q0, tq, tk):
    if mode == "cmp":
        return 0, 1
    hi = (q0 + tq - 1) // tk + 1
    if mode == "sel":
        return 0, hi
    return jnp.maximum(q0 - (WINDOW - 1), 0) // tk, hi


def _attn_bias(mode, q0, k0, tq, tk, sel_t):
    k = k0 + lax.broadcasted_iota(jnp.int32, (tk, tq), 0)
    t = q0 + lax.broadcasted_iota(jnp.int32, (tk, tq), 1)
    if mode == "cmp":
        ok = (k * 16 + 31) <= t
    elif mode == "win":
        ok = (k <= t) & ((t - k) < WINDOW)
    else:
        nb = sel_t.shape[0]
        ek = k0 + lax.broadcasted_iota(jnp.int32, (tk, nb), 0)
        eb = lax.broadcasted_iota(jnp.int32, (tk, nb), 1)
        expand = (lax.shift_right_logical(ek, 6) == eb).astype(_MXU)
        chosen = _dot(expand, sel_t, "nn") > 0.5
        ok = (k <= t) & chosen
    bias = jnp.where(ok, 0.0, NEG)
    return jnp.concatenate([bias] * GRP, axis=1), jnp.concatenate([ok.astype(F32)] * GRP, axis=1)


def _stack_heads(ref, tq):
    return jnp.concatenate([ref[:, g * HD:(g + 1) * HD] for g in range(GRP)], axis=0)


def _scaled_queries(q_ref, tq):
    return (_stack_heads(q_ref, tq).astype(F32) * SCALE).astype(_MXU)


def _blocked_t(x, tk):
    n, Sk, d = x.shape
    return x.reshape(n, Sk // tk, tk, d).transpose(0, 1, 3, 2)


def _attn_fwd(q, qcol0, k, vt, mode, sel_t, name):
    S, Sk = q.shape[0], k.shape[1]
    tq, tk = _attn_cfg(S, Sk, mode)
    R = GRP * tq

    def body(*refs):
        if mode == "sel":
            q_ref, k_ref, vt_ref, sel_ref, o_ref, lse_ref, m_scr, l_scr, acc = refs
        else:
            q_ref, k_ref, vt_ref, o_ref, lse_ref, m_scr, l_scr, acc = refs
        q0 = pl.program_id(1) * tq
        qs = _scaled_queries(q_ref, tq)
        m_scr[...] = jnp.full_like(m_scr, NEG)
        l_scr[...] = jnp.zeros_like(l_scr)
        acc[...] = jnp.zeros_like(acc)
        selv = sel_ref[0].astype(_MXU) if mode == "sel" else None

        def step(kb, carry):
            k0 = pl.multiple_of(kb * tk, tk)
            bias, okf = _attn_bias(mode, q0, k0, tq, tk, selv)
            s = _dot(k_ref[0, pl.ds(k0, tk), :], qs, "nt") + bias
            m_old = m_scr[...]
            m_new = jnp.maximum(m_old, jnp.max(s, axis=0, keepdims=True))
            p = jnp.exp(s - m_new)
            if mode == "cmp":
                p = p * okf
            alpha = jnp.exp(m_old - m_new)
            l_scr[...] = alpha * l_scr[...] + jnp.sum(p, axis=0, keepdims=True)
            acc[...] = alpha * acc[...] + _dot(vt_ref[0, kb], p, "nn")
            m_scr[...] = m_new
            return carry

        lo, hi = _kb_range(mode, q0, tq, tk)
        lax.fori_loop(lo, hi, step, 0)
        l = l_scr[...]
        good = l > 0.0
        o_t = acc[...] * jnp.where(good, 1.0 / jnp.where(good, l, 1.0), 0.0)
        lse = jnp.where(good, m_scr[...] + jnp.log(jnp.where(good, l, 1.0)), -NEG)
        for g in range(GRP):
            o_ref[:, g * HD:(g + 1) * HD] = o_t[:, g * tq:(g + 1) * tq].T
            lse_ref[0, g:g + 1, :] = lse[:, g * tq:(g + 1) * tq]

    ins = [q, k, vt]
    specs = [pl.BlockSpec((tq, GRP * HD), lambda h, i: (i, qcol0 + h)), pl.BlockSpec((1, Sk, HD), lambda h, i: (h, 0, 0)),
             pl.BlockSpec((1, Sk // tk, HD, tk), lambda h, i: (h, 0, 0, 0))]
    if mode == "sel":
        ins.append(sel_t)
        specs.append(pl.BlockSpec((1, sel_t.shape[1], tq), lambda h, i: (h, 0, i)))
    return pl.pallas_call(
        body, name=name, grid=(N_KV, S // tq), in_specs=specs,
        out_specs=[pl.BlockSpec((tq, GRP * HD), lambda h, i: (i, h)), pl.BlockSpec((1, GRP, tq), lambda h, i: (h, 0, i))],
        out_shape=[jax.ShapeDtypeStruct((S, ATT_WIDTH), F32), jax.ShapeDtypeStruct((N_KV, GRP, S), F32)],
        scratch_shapes=[pltpu.VMEM((1, R), F32), pltpu.VMEM((1, R), F32), pltpu.VMEM((HD, R), F32)],
        compiler_params=_cp(("parallel", "arbitrary")))(*ins)


def _attn_bwd(q, qcol0, k, kt, v, o, lse, do, mode, sel_t, name):
    S, Sk = q.shape[0], k.shape[1]
    tq, tk = _attn_cfg(S, Sk, mode)
    R = GRP * tq

    def body(*refs):
        if mode == "sel":
            q_ref, k_ref, kt_ref, v_ref, o_ref, lse_ref, do_ref, sel_ref, dq_ref, dk_ref, dv_ref, dq_scr = refs
        else:
            q_ref, k_ref, kt_ref, v_ref, o_ref, lse_ref, do_ref, dq_ref, dk_ref, dv_ref, dq_scr = refs

        @pl.when(pl.program_id(1) == 0)
        def _():
            dk_ref[...] = jnp.zeros_like(dk_ref)
            dv_ref[...] = jnp.zeros_like(dv_ref)

        q0 = pl.program_id(1) * tq
        qs = _scaled_queries(q_ref, tq)
        dos = _stack_heads(do_ref, tq)
        delta = _dot(jnp.ones((8, HD), F32), dos * _stack_heads(o_ref, tq), "nt", hi=True)[0:1, :]
        lsev = jnp.concatenate([lse_ref[0, g:g + 1, :] for g in range(GRP)], axis=1)
        dos = dos.astype(_MXU)
        dq_scr[...] = jnp.zeros_like(dq_scr)
        selv = sel_ref[0].astype(_MXU) if mode == "sel" else None

        def step(kb, carry):
            k0 = pl.multiple_of(kb * tk, tk)
            kv = k_ref[0, pl.ds(k0, tk), :]
            bias, okf = _attn_bias(mode, q0, k0, tq, tk, selv)
            p = jnp.exp(_dot(kv, qs, "nt") + bias - lsev)
            if mode == "cmp":
                p = p * okf
            dp = _dot(v_ref[0, pl.ds(k0, tk), :], dos, "nt")
            ds = p * (dp - delta)
            dq_scr[...] += _dot(kt_ref[0, kb], ds, "nn")
            dk_ref[0, pl.ds(k0, tk), :] += _dot(ds, qs, "nn")
            dv_ref[0, pl.ds(k0, tk), :] += _dot(p, dos, "nn")
            return carry

        lo, hi = _kb_range(mode, q0, tq, tk)
        lax.fori_loop(lo, hi, step, 0)
        for g in range(GRP):
            dq_ref[:, g * HD:(g + 1) * HD] = (dq_scr[:, g * tq:(g + 1) * tq] * SCALE).T

    kv_spec = pl.BlockSpec((1, Sk, HD), lambda h, i: (h, 0, 0))
    qo_spec = pl.BlockSpec((tq, GRP * HD), lambda h, i: (i, h))
    ins = [q, k, kt, v, o, lse, do]
    specs = [pl.BlockSpec((tq, GRP * HD), lambda h, i: (i, qcol0 + h)), kv_spec,
             pl.BlockSpec((1, Sk // tk, HD, tk), lambda h, i: (h, 0, 0, 0)), kv_spec, qo_spec,
             pl.BlockSpec((1, GRP, tq), lambda h, i: (h, 0, i)), qo_spec]
    if mode == "sel":
        ins.append(sel_t)
        specs.append(pl.BlockSpec((1, sel_t.shape[1], tq), lambda h, i: (h, 0, i)))
    return pl.pallas_call(
        body, name=name, grid=(N_KV, S // tq), in_specs=specs, out_specs=[qo_spec, kv_spec, kv_spec],
        out_shape=[jax.ShapeDtypeStruct((S, ATT_WIDTH), F32), jax.ShapeDtypeStruct((N_KV, Sk, HD), F32),
                   jax.ShapeDtypeStruct((N_KV, Sk, HD), F32)],
        scratch_shapes=[pltpu.VMEM((HD, R), F32)],
        compiler_params=_cp(("parallel", "arbitrary")))(*ins)


def _select(q, qcol0, k_cmp, lse):
    S, NC = q.shape[0], k_cmp.shape[1]
    NB = S // SEL_BLOCK
    tq = _pick(S, (256, 128))
    ci = np.arange(NC)[None, :] * 16
    sj = np.arange(NB)[:, None] * SEL_BLOCK
    ov_t = np.clip(np.minimum(ci + 32, sj + SEL_BLOCK) - np.maximum(ci, sj), 0, None) / 32.0
    ov_t[:, NC - 1] = 0.0
    ov_t = jnp.asarray(ov_t, F32)

    def body(q_ref, k_ref, lse_ref, ov_ref, sel_ref):
        q0 = pl.program_id(1) * tq
        bias, okf = _attn_bias("cmp", q0, 0, tq, NC, None)
        lsev = jnp.concatenate([lse_ref[0, g:g + 1, :] for g in range(GRP)], axis=1)
        p = jnp.exp(_dot(k_ref[0], _scaled_queries(q_ref, tq), "nt") + bias - lsev) * okf
        imp4 = _dot(ov_ref[...], p, "nn")
        imp = imp4[:, 0:tq] + imp4[:, tq:2 * tq] + imp4[:, 2 * tq:3 * tq] + imp4[:, 3 * tq:4 * tq]
        blk = lax.broadcasted_iota(jnp.int32, (NB, tq), 0)
        cur = lax.shift_right_logical(q0 + lax.broadcasted_iota(jnp.int32, (NB, tq), 1), 6)
        imp = jnp.where((blk == 0) | (blk == cur) | (blk == cur - 1), FORCE, imp)
        imp = jnp.where(blk <= cur, imp, -1.0)
        rank = jnp.zeros((NB, tq), F32)
        for j in range(NB):
            row = imp[j:j + 1, :]
            ahead = (row > imp) | ((row == imp) & (blk > j))
            rank = rank + ahead.astype(F32)
        sel_ref[0] = ((rank < float(N_SELECT)) & (imp >= 0.0)).astype(F32)

    return pl.pallas_call(
        body, name="select_blocks", grid=(N_KV, S // tq),
        in_specs=[pl.BlockSpec((tq, GRP * HD), lambda h, i: (i, qcol0 + h)), pl.BlockSpec((1, NC, HD), lambda h, i: (h, 0, 0)),
                  pl.BlockSpec((1, GRP, tq), lambda h, i: (h, 0, i)), pl.BlockSpec((NB, NC), lambda h, i: (0, 0))],
        out_specs=pl.BlockSpec((1, NB, tq), lambda h, i: (h, 0, i)),
        out_shape=jax.ShapeDtypeStruct((N_KV, NB, S), F32), compiler_params=_cp(("parallel", "parallel")))(q, k_cmp, lse, ov_t)


GATE_COL0 = SSD_HEADS


def _combine_fwd(o_cmp, o_sel, o_win, proj_small):
    S = o_cmp.shape[0]
    tr = _pick(S, (256, 128))

    def body(oc_ref, os_ref, ow_ref, g_ref, y_ref):
        gate = _sigmoid(g_ref[...])
        for h in range(N_HEADS):
            hs = slice(h * HD, (h + 1) * HD)
            c = GATE_COL0 + 3 * h
            y = gate[:, c:c + 1] * oc_ref[:, hs] + gate[:, c + 1:c + 2] * os_ref[:, hs] + gate[:, c + 2:c + 3] * ow_ref[:, hs]
            y_ref[:, hs] = y.astype(y_ref.dtype)

    row = pl.BlockSpec((tr, ATT_WIDTH), lambda i: (i, 0))
    return pl.pallas_call(
        body, name="combine_fwd", grid=(S // tr,), in_specs=[row, row, row, pl.BlockSpec((tr, W_SMALL), lambda i: (i, 0))],
        out_specs=row, out_shape=jax.ShapeDtypeStruct((S, ATT_WIDTH), _MXU), compiler_params=_cp(("parallel",)))(
            o_cmp, o_sel, o_win, proj_small)


def _combine_bwd(dmixed, o_cmp, o_sel, o_win, proj_small):
    S = o_cmp.shape[0]
    tr = _pick(S, (256, 128))

    def body(dy_ref, oc_ref, os_ref, ow_ref, g_ref, dc_ref, ds_ref, dw_ref, dg_ref):
        gate = _sigmoid(g_ref[...])
        lane = lax.broadcasted_iota(jnp.int32, (1, W_SMALL), 1)
        dg = jnp.zeros((tr, W_SMALL), F32)
        for h in range(N_HEADS):
            hs = slice(h * HD, (h + 1) * HD)
            dy = dy_ref[:, hs].astype(F32)
            for b, (o_ref, d_ref) in enumerate(((oc_ref, dc_ref), (os_ref, ds_ref), (ow_ref, dw_ref))):
                c = GATE_COL0 + 3 * h + b
                gv = gate[:, c:c + 1]
                d_ref[:, hs] = gv * dy
                dgate = jnp.sum(dy * o_ref[:, hs], axis=-1, keepdims=True) * (gv * (1.0 - gv))
                dg = dg + dgate * (lane == c).astype(F32)
        dg_ref[...] = dg

    row = pl.BlockSpec((tr, ATT_WIDTH), lambda i: (i, 0))
    small = pl.BlockSpec((tr, W_SMALL), lambda i: (i, 0))
    return pl.pallas_call(
        body, name="combine_bwd", grid=(S // tr,),
        in_specs=[pl.BlockSpec((tr, ATT_WIDTH), lambda i: (i, 1)), row, row, row, small], out_specs=[row, row, row, small],
        out_shape=[jax.ShapeDtypeStruct((S, ATT_WIDTH), F32)] * 3 + [jax.ShapeDtypeStruct((S, W_SMALL), F32)],
        compiler_params=_cp(("parallel",)))(dmixed, o_cmp, o_sel, o_win, proj_small)


def _to_rows16(x):
    S = x.shape[0]
    return x.reshape(S // 16, 16, N_KV, HD).transpose(2, 0, 1, 3).reshape(N_KV, S // 16, 16 * HD)


def _from_rows16(r):
    NC = r.shape[1]
    return r.reshape(N_KV, NC, 16, HD).transpose(1, 2, 0, 3).reshape(NC * 16, N_KV * HD)


DT_COL0 = SSD_WIDTH + CONV_CH
GATE_IN_COL0 = D_IN - 3 * N_HEADS


def _split_w_in(w):
    main = jnp.concatenate([w[:, :DT_COL0], w[:, DT_COL0 + SSD_HEADS:GATE_IN_COL0]], axis=1)
    small = jnp.concatenate([w[:, DT_COL0:DT_COL0 + SSD_HEADS], w[:, GATE_IN_COL0:],
                             jnp.zeros((w.shape[0], W_SMALL - SSD_HEADS - 3 * N_HEADS), w.dtype)], axis=1)
    return main, small


def _merge_w_in(main, small):
    return jnp.concatenate([main[:, :DT_COL0], small[:, :SSD_HEADS].astype(main.dtype), main[:, DT_COL0:],
                            small[:, SSD_HEADS:SSD_HEADS + 3 * N_HEADS].astype(main.dtype)], axis=1)


QB, KCB, VCB, KSB, VSB, KWB, VWB = 10, 14, 15, 16, 17, 18, 19


def _col256(a, b):
    return a[:, b * 256:(b + 1) * 256]


_EARLY = ["w_in", "cmp_w1_k", "cmp_w1_v"]
_LATE = ["w_out", "w_gate", "w_up", "w_down"]
_FFN = ["w_down", "w_gate", "w_up"]
_MID = ["w_out"]
_LAST = ["cmp_w1_k", "cmp_w1_v", "w_in"]


def _local_step(x, tgt, p, early_weights=None, late_weights=None, grads_ready=None):
    S = x.shape[0]
    cos, sin = _rope_tables(S)

    u, rs1 = _rms_fwd(x, p["attn_norm_w"], "attn_norm")
    if early_weights is not None:
        p = {**p, **early_weights(u)}
    proj = _mm(u, p["w_main"], "nn", F32, "in_proj")
    proj_small = _mm(u, p["w_small"], "nn", F32, "in_proj_small")
    xa = _conv_fwd(proj, p["conv_w"], p["conv_b"])
    y_ssd, y_pre, rs_ssd, hs = _ssd_fwd(proj, proj_small, xa, p["dt_bias"], p["a_log"], p["d_skip"], p["ssd_norm_w"])

    q_rot = _rope([proj], QB, ATT_WIDTH, cos, sin, 1.0, _MXU, "rope_q")
    kv = _kv_prep(proj, cos, sin, _attn_cfg(S, S, "sel")[1])
    rk, rv = _to_rows16(_col256(proj, KCB)), _to_rows16(_col256(proj, VCB))
    k_cmp, hid_k = _compress_fwd(rk, p["cmp_pe_k"], p["cmp_w1_k"], p["cmp_w2_k"])
    v_cmp, hid_v = _compress_fwd(rv, p["cmp_pe_v"], p["cmp_w1_v"], p["cmp_w2_v"])
    n_cmp = k_cmp.shape[1]

    o_cmp, lse_cmp = _attn_fwd(proj, QB, k_cmp, _blocked_t(v_cmp, n_cmp), "cmp", None, "attn_cmp_fwd")
    sel = _select(proj, QB, k_cmp, lse_cmp)
    o_sel, lse_sel = _attn_fwd(q_rot, 0, kv["ks"], kv["vs_t"], "sel", sel, "attn_sel_fwd")
    o_win, lse_win = _attn_fwd(q_rot, 0, kv["kw"], kv["vw_t"], "win", None, "attn_win_fwd")
    y_att = _combine_fwd(o_cmp, o_sel, o_win, proj_small)

    if late_weights is not None:
        p = {**p, **late_weights(y_att)}
    mixed = jnp.concatenate([y_ssd, y_att], axis=1)
    h1 = _mm(mixed, p["w_out"], "nn", F32, "out_proj", res=x)
    v, rs_ffn = _rms_fwd(h1, p["ffn_norm_w"], "ffn_norm")
    gt, up, act = _ffn_up(v, p["w_gate"], p["w_up"])
    h2 = _mm(act, p["w_down"], "nn", F32, "ffn_down", res=h1)
    loss, dh2, dh2b, d_final_w = _final_loss(h2, p["final_norm_w"], tgt)

    def ready(names):
        return None if grads_ready is None else grads_ready(names, g)

    g = {"final_norm_w": d_final_w}
    g["w_down"] = _mm(act, dh2b, "tn", _MXU, "dw_down")
    dgt, dup = _ffn_dact(dh2b, p["w_down"], gt, up)
    g["w_gate"] = _mm(v, dgt, "tn", _MXU, "dw_gate")
    g["w_up"] = _mm(v, dup, "tn", _MXU, "dw_up")
    dv = _mm(dgt, p["w_gate"], "nt", F32, "dv_gate", after=ready(_FFN))
    dv = _mm(dup, p["w_up"], "nt", F32, "dv_up", res=dv)
    dh1, dh1b, g["ffn_norm_w"] = _rms_bwd(dv, h1, rs_ffn, p["ffn_norm_w"], dh2, "ffn_norm_bwd")
    g["w_out"] = _mm(mixed, dh1b, "tn", _MXU, "dw_out")
    dmixed = _mm(dh1b, p["w_out"], "nt", F32, "dmixed", after=ready(_MID))

    dz, dxa, ddtr, g["dt_bias"], g["a_log"], g["d_skip"], g["ssd_norm_w"] = _ssd_bwd(
        dmixed, proj, proj_small, xa, y_pre, rs_ssd, hs, p["dt_bias"], p["a_log"], p["d_skip"], p["ssd_norm_w"])
    dxbc, g["conv_w"], g["conv_b"] = _conv_bwd(proj, p["conv_w"], p["conv_b"], dxa)

    do_cmp, do_sel, do_win, dgate = _combine_bwd(dmixed, o_cmp, o_sel, o_win, proj_small)
    dq_cmp, dk_cmp, dv_cmp = _attn_bwd(proj, QB, k_cmp, _blocked_t(k_cmp, n_cmp), v_cmp, o_cmp, lse_cmp, do_cmp, "cmp", None,
                                       "attn_cmp_bwd")
    dq_sel, dks, dvs = _attn_bwd(q_rot, 0, kv["ks"], kv["ks_t"], kv["vs"], o_sel, lse_sel, do_sel, "sel", sel, "attn_sel_bwd")
    dq_win, dkw, dvw = _attn_bwd(q_rot, 0, kv["kw"], kv["kw_t"], kv["vw"], o_win, lse_win, do_win, "win", None, "attn_win_bwd")
    drk, g["cmp_w1_k"], g["cmp_w2_k"], g["cmp_pe_k"] = _compress_bwd(rk, p["cmp_pe_k"], p["cmp_w1_k"], p["cmp_w2_k"], hid_k, dk_cmp)
    drv, g["cmp_w1_v"], g["cmp_w2_v"], g["cmp_pe_v"] = _compress_bwd(rv, p["cmp_pe_v"], p["cmp_w1_v"], p["cmp_w2_v"], hid_v, dv_cmp)
    dq = _rope([dq_sel, dq_win], 0, ATT_WIDTH, cos, sin, -1.0, _MXU, "rope_dq", extra=(dq_cmp, 0))
    dkv = _dkv_post(dks, dvs, dkw, dvw, cos, sin)
    dproj = jnp.concatenate([dz, dxbc, dq] + [t.astype(_MXU) for t in (_from_rows16(drk), _from_rows16(drv))] + [dkv], axis=1)
    dsmall = jnp.concatenate([ddtr, dgate[:, GATE_COL0:GATE_COL0 + 3 * N_HEADS],
                              jnp.zeros((S, W_SMALL - SSD_HEADS - 3 * N_HEADS), F32)], axis=1).astype(_MXU)
    g["w_main"] = _mm(u, dproj, "tn", _MXU, "dw_in")
    g["w_small"] = _mm(u, dsmall, "tn", F32, "dw_in_small")
    du = _mm(dproj, p["w_main"], "nt", F32, "du_main", after=ready(_LAST))
    du = _mm(dsmall, p["w_small"], "nt", F32, "du_small", res=du)
    grad_x, _, g["attn_norm_w"] = _rms_bwd(du, x, rs1, p["attn_norm_w"], dh1, "attn_norm_bwd")
    return loss, grad_x, g


MESH_ID = pl.DeviceIdType.MESH


def _my_coords():
    return lax.axis_index("x"), lax.axis_index("y"), lax.axis_index("c")


def _flat_id(px, py, pc):
    return 4 * px + 2 * py + pc


def _peer(k):
    mx, my, mc = _my_coords()
    return (1 - mx if k & 4 else mx, 1 - my if k & 2 else my, 1 - mc if k & 1 else mc)


def _exchange(arrs, scatter, name, after=()):
    n, na = len(arrs), len(after)
    scatter = [scatter] * n if isinstance(scatter, bool) else list(scatter)

    def body(*refs):
        ins, outs = refs[:n], refs[n + na:2 * n + na]
        send_sems, recv_sems, local_sems = refs[2 * n + na:]
        me = _flat_id(*_my_coords())
        copies = []
        for i in range(n):
            src_me = ins[i].at[me] if scatter[i] else ins[i]
            local = pltpu.make_async_copy(src_me, outs[i].at[me], local_sems.at[i])
            local.start()
            copies.append(local)
        for k in range(1, N_DEV):
            peer = _peer(k)
            for i in range(n):
                src = ins[i].at[_flat_id(*peer)] if scatter[i] else ins[i]
                cp = pltpu.make_async_remote_copy(src_ref=src, dst_ref=outs[i].at[me], send_sem=send_sems.at[i * 7 + k - 1],
                                                  recv_sem=recv_sems.at[i * 7 + k - 1], device_id=peer, device_id_type=MESH_ID)
                cp.start()
                copies.append(cp)
        for cp in copies:
            cp.wait()

    any_spec = pl.BlockSpec(memory_space=pl.ANY)
    out_shape = [jax.ShapeDtypeStruct(a.shape if sc else (N_DEV,) + a.shape, a.dtype) for a, sc in zip(arrs, scatter)]
    return pl.pallas_call(
        body, name=name, in_specs=[any_spec] * (n + na), out_specs=[any_spec] * n, out_shape=out_shape,
        scratch_shapes=[pltpu.SemaphoreType.DMA((n * 7,)), pltpu.SemaphoreType.DMA((n * 7,)), pltpu.SemaphoreType.DMA((n,))],
        compiler_params=pltpu.CompilerParams(has_side_effects=True))(*arrs, *after)


_HBM = pl.BlockSpec(memory_space=pltpu.HBM)
_SEM = pl.BlockSpec(memory_space=pltpu.SEMAPHORE)
_EFFECT = pltpu.SideEffectType.DATAFLOW_SIDE_EFFECTING


def _split_copies(ins, lands, send_sems, recv_sems, scatter):
    me = _flat_id(*_my_coords())
    out = []
    for k in range(1, N_DEV):
        peer = _peer(k)
        for i in range(len(ins)):
            src = ins[i].at[_flat_id(*peer)] if scatter else ins[i]
            out.append(pltpu.make_async_remote_copy(src_ref=src, dst_ref=lands[i].at[me], send_sem=send_sems.at[i * 7 + k - 1],
                                                    recv_sem=recv_sems.at[i * 7 + k - 1], device_id=peer, device_id_type=MESH_ID))
    return out


def _split_start(arrs, scatter, name):
    n = len(arrs)

    def body(*refs):
        for cp in _split_copies(refs[:n], refs[n:2 * n], refs[2 * n], refs[2 * n + 1], scatter):
            cp.start()
        refs[-1][...] = jnp.zeros_like(refs[-1])

    land_shapes = [a.shape if scatter else (N_DEV,) + a.shape for a in arrs]
    out_shape = ((pltpu.SemaphoreType.DMA((n * 7,)), pltpu.SemaphoreType.DMA((n * 7,)))
                 + tuple(pltpu.HBM(a.shape, a.dtype) for a in arrs) + tuple(pltpu.HBM(s, a.dtype) for s, a in zip(land_shapes, arrs))
                 + (jax.ShapeDtypeStruct((8, 128), F32),))
    operands = ([pltpu.with_memory_space_constraint(a, pltpu.HBM) for a in arrs]
                + [pltpu.with_memory_space_constraint(lax.empty(s, a.dtype), pltpu.HBM) for s, a in zip(land_shapes, arrs)])
    res = pl.pallas_call(
        body, name=name, out_shape=out_shape, in_specs=[_HBM] * (2 * n),
        out_specs=(_SEM, _SEM) + (_HBM,) * (2 * n) + (pl.BlockSpec(memory_space=pltpu.VMEM),),
        input_output_aliases={i: 2 + i for i in range(2 * n)},
        compiler_params=pltpu.CompilerParams(has_side_effects=_EFFECT))(*operands)
    return dict(send=res[0], recv=res[1], ins=list(res[2:2 + n]), lands=list(res[2 + n:2 + 2 * n]), token=res[-1])


def _split_wait(st, scatter, after, name):
    n = len(st["ins"])

    def body(*refs):
        for cp in _split_copies(refs[:n], refs[n:2 * n], refs[2 * n], refs[2 * n + 1], scatter):
            cp.wait_send()
            cp.wait_recv()

    arrs = st["ins"] + st["lands"]
    res = pl.pallas_call(
        body, name=name, out_shape=tuple(pltpu.HBM(a.shape, a.dtype) for a in arrs),
        in_specs=[_HBM] * (2 * n) + [_SEM, _SEM] + [pl.BlockSpec(memory_space=pl.ANY)] * len(after), out_specs=(_HBM,) * (2 * n),
        input_output_aliases={i: i for i in range(2 * n)},
        compiler_params=pltpu.CompilerParams(has_side_effects=_EFFECT))(*arrs, st["send"], st["recv"], *after)
    me = _flat_id(*_my_coords())
    out = []
    for src, land in zip(res[:n], res[n:]):
        own = lax.dynamic_index_in_dim(src, me, 0, keepdims=True) if scatter else src[None]
        out.append(lax.dynamic_update_slice_in_dim(land, own, me, 0))
    return out


def _adam_step(p_ref, w_ref, m_ref, v_ref, g_ref, d_ref, nm_ref, nv_ref):
    g = p_ref[0].astype(F32)
    for j in range(1, p_ref.shape[0]):
        g = g + p_ref[j].astype(F32)
    g_ref[...] = g
    nm = ADAM_B1 * m_ref[...] + (1.0 - ADAM_B1) * g
    nv = ADAM_B2 * v_ref[...] + (1.0 - ADAM_B2) * (g * g)
    nm_ref[...] = nm
    nv_ref[...] = nv
    m_hat = nm / (1.0 - ADAM_B1 ** ADAM_STEP)
    v_hat = nv / (1.0 - ADAM_B2 ** ADAM_STEP)
    d_ref[...] = -ADAM_LR * (m_hat / (jnp.sqrt(v_hat) + ADAM_EPS) + ADAM_WD * w_ref[...])


def _adam_sum(parts, w, m, v, name):
    P, R, C = parts.shape
    tr = _pick(R, (256, 128, 64, 32, 8)) if C <= 1024 else _pick(R, (128, 64, 32, 8))
    blk = pl.BlockSpec((tr, C), lambda i: (i, 0))
    return pl.pallas_call(
        functools.partial(_adam_step), name=name, grid=(R // tr,),
        in_specs=[pl.BlockSpec((P, tr, C), lambda i: (0, i, 0)), blk, blk, blk],
        out_specs=[blk] * 4, out_shape=[jax.ShapeDtypeStruct((R, C), F32)] * 4, compiler_params=_cp(("parallel",)))(parts, w, m, v)


def _adam_small(parts, ws, ms, vs):
    n = len(parts)

    def body(*refs):
        ins, outs = refs[:4 * n], refs[4 * n:]
        for i in range(n):
            _adam_step(ins[i], ins[n + i], ins[2 * n + i], ins[3 * n + i], *outs[4 * i:4 * i + 4])

    out_shape = [jax.ShapeDtypeStruct(w.shape, F32) for w in ws for _ in range(4)]
    res = pl.pallas_call(body, name="adam_small", out_shape=out_shape)(*parts, *ws, *ms, *vs)
    return [tuple(res[4 * i:4 * i + 4]) for i in range(n)]


_WEIGHTS = ["attn_norm_w", "w_in", "conv_w", "conv_b", "dt_bias", "a_log", "d_skip", "ssd_norm_w", "cmp_w1_k", "cmp_w2_k",
            "cmp_w1_v", "cmp_w2_v", "cmp_pe_k", "cmp_pe_v", "w_out", "ffn_norm_w", "w_gate", "w_up", "w_down", "final_norm_w"]
_BIG = ["w_in", "w_gate", "w_up", "w_down", "w_out", "cmp_w1_k", "cmp_w1_v"]
_COL_SHARDED = ("w_in", "w_gate", "w_up")
_REPLICATED = ["attn_norm_w", "conv_b", "dt_bias", "a_log", "d_skip", "ssd_norm_w", "cmp_pe_k", "cmp_pe_v", "ffn_norm_w",
               "final_norm_w"]
_SMALL_SHARDED = ["conv_w", "cmp_w2_k", "cmp_w2_v"]


def _cols_to_slabs(g):
    R = g.shape[0]
    return g.reshape(R, N_DEV, -1).transpose(1, 0, 2)


def _slabs_to_cols(s):
    return s.transpose(1, 0, 2).reshape(s.shape[1], -1)


def kernel(x, attn_norm_w, w_in, conv_w, conv_b, dt_bias, a_log, d_skip, ssd_norm_w, cmp_w1_k, cmp_w2_k, cmp_w1_v, cmp_w2_v, cmp_pe_k, cmp_pe_v, w_out, ffn_norm_w, w_gate, w_up, w_down, final_norm_w, loss_target, m_attn_norm_w, m_w_in, m_conv_w, m_conv_b, m_dt_bias, m_a_log, m_d_skip, m_ssd_norm_w, m_cmp_w1_k, m_cmp_w2_k, m_cmp_w1_v, m_cmp_w2_v, m_cmp_pe_k, m_cmp_pe_v, m_w_out, m_ffn_norm_w, m_w_gate, m_w_up, m_w_down, m_final_norm_w, v_attn_norm_w, v_w_in, v_conv_w, v_conv_b, v_dt_bias, v_a_log, v_d_skip, v_ssd_norm_w, v_cmp_w1_k, v_cmp_w2_k, v_cmp_w1_v, v_cmp_w2_v, v_cmp_pe_k, v_cmp_pe_v, v_w_out, v_ffn_norm_w, v_w_gate, v_w_up, v_w_down, v_final_norm_w):
    a = dict(locals())

    shard = {n: a[n][0].astype(_MXU) for n in _BIG}
    early_small = [cmp_w2_k[0], cmp_w2_v[0], conv_w[0]]
    st_early = _split_start([shard[n] for n in _EARLY] + early_small, False, "gather_early_start")
    zero = st_early["token"][0, 0].astype(_MXU)
    st_late = _split_start([shard[_LATE[0]] + zero] + [shard[n] for n in _LATE[1:]], False, "gather_late_start")

    def assemble(n, t):
        return _slabs_to_cols(t) if n in _COL_SHARDED else t.reshape(-1, t.shape[-1])

    p = dict(attn_norm_w=attn_norm_w, conv_b=conv_b, dt_bias=dt_bias, a_log=a_log, d_skip=d_skip, ssd_norm_w=ssd_norm_w,
             cmp_pe_k=cmp_pe_k.reshape(1, -1), cmp_pe_v=cmp_pe_v.reshape(1, -1), ffn_norm_w=ffn_norm_w,
             final_norm_w=final_norm_w.reshape(1, -1))

    def early_weights(after):
        got = _split_wait(st_early, False, (after, st_late["token"]), "gather_early_wait")
        w_main, w_small = _split_w_in(assemble("w_in", got[0]))
        return dict(w_main=w_main, w_small=w_small, cmp_w1_k=assemble("cmp_w1_k", got[1]), cmp_w1_v=assemble("cmp_w1_v", got[2]),
                    cmp_w2_k=assemble("cmp_w2_k", got[3]).astype(_MXU), cmp_w2_v=assemble("cmp_w2_v", got[4]).astype(_MXU),
                    conv_w=_slabs_to_cols(got[5]))

    def late_weights(after):
        got_late = _split_wait(st_late, False, (after,), "gather_late_wait")
        return {n: assemble(n, t) for n, t in zip(_LATE, got_late)}

    def slabs_of(g, names):
        return [(_cols_to_slabs(g[n]) if n in _COL_SHARDED else g[n].reshape(N_DEV, -1, g[n].shape[-1])).astype(_MXU) for n in names]

    started = []

    def grads_ready(names, g):
        if "w_in" in names:
            g = {**g, "w_in": _merge_w_in(g["w_main"], g["w_small"])}
        started.append((names, _split_start(slabs_of(g, names), True, "scatter_grads_start_%d" % len(started))))
        return started[-1][1]["token"][0:1, 0:1]

    loss_part, grad_x, g = _local_step(x[0], loss_target[0], p, early_weights, late_weights, grads_ready)
    loss = lax.psum(loss_part[0, 0], ("x", "y", "c"))

    out, after = {}, (started[-1][1]["token"],)
    for i, (names, st) in enumerate(started):
        if i == len(started) - 1:
            after = after + (grad_x,)
        received = _split_wait(st, True, after, "scatter_grads_wait_%d" % i)
        for n, parts in zip(names, received):
            out[n] = _adam_sum(parts, a[n][0], a["m_" + n][0], a["v_" + n][0], "adam_" + n)
        after = (out[names[-1]][0],)

    small_names = _REPLICATED + _SMALL_SHARDED
    partials = [g[n] for n in _REPLICATED] + [_cols_to_slabs(g["conv_w"])] + [
        g[n].reshape(N_DEV, -1, g[n].shape[-1]) for n in ("cmp_w2_k", "cmp_w2_v")]
    gathered = _exchange(partials, [False] * len(_REPLICATED) + [True] * len(_SMALL_SHARDED), "exchange_small_grads",
                         after=(received[0],))
    shapes2d = [t.shape[1:] for t in gathered]
    res_small = _adam_small(gathered, *[[a[pre + n].reshape(s) for n, s in zip(small_names, shapes2d)] for pre in ("", "m_", "v_")])
    for n, r in zip(small_names, res_small):
        out[n] = r

    outs = [loss, grad_x[None]]
    for j in range(4):
        for n in _WEIGHTS:
            outs.append(out[n][j].reshape(a[n].shape))
    return tuple(outs)
```

```python
import functools

import numpy as np
import jax
import jax.numpy as jnp
from jax import lax
from jax.experimental import pallas as pl
from jax.experimental.pallas import tpu as pltpu

F32 = jnp.float32
_MXU = jnp.bfloat16
_HI = lax.Precision.HIGHEST

N_DEV = 8
D_MODEL = 2048
SSD_WIDTH = 1024
ATT_WIDTH = 1024
SSD_HEADS = 16
SSD_P = 64
SSD_N = 128
SSD_L = 128
SSD_G = 2
CONV_CH = 1536
CONV_K = 4
HD = 64
N_HEADS = 16
N_KV = 4
GRP = 4
CMP_HID = 256
SEL_BLOCK = 64
N_SELECT = 16
WINDOW = 512
ROPE_DIM = 16
ROPE_THETA = 500000.0
D_FF = 5632
EPS = 1e-6
NEG = -1e30
FORCE = 1e4
SCALE = HD ** -0.5
D_IN = 5184
W_MAIN = 5120
W_SMALL = 128
VMEM_LIMIT = 52 * 1024 * 1024

ADAM_LR, ADAM_B1, ADAM_B2, ADAM_EPS, ADAM_WD, ADAM_STEP = 0.001, 0.9, 0.999, 1e-08, 0.01, 10


def _pick(n, cands):
    for c in cands:
        if n % c == 0:
            return c
    return n


def _cp(sem=None):
    return pltpu.CompilerParams(dimension_semantics=sem, vmem_limit_bytes=VMEM_LIMIT)


def _sigmoid(x):
    return 1.0 / (1.0 + jnp.exp(-x))


def _dot(a, b, dims, hi=False):
    dn = {"nn": (((1,), (0,)), ((), ())), "nt": (((1,), (1,)), ((), ())), "tn": (((0,), (0,)), ((), ()))}[dims]
    if hi:
        return lax.dot_general(a.astype(F32), b.astype(F32), dn, precision=_HI, preferred_element_type=F32)
    return lax.dot_general(a.astype(_MXU), b.astype(_MXU), dn, preferred_element_type=F32)


LANE = 128
MM_TILE = 1024
MM_K_WHOLE = 2048
MM_K_STEP = 1536
TN_ACC_ELEMS = 3 * 2 ** 20
TN_K_STEP = 512


def _largest_tile(n, cap):
    if n <= cap:
        return n
    best = LANE
    for t in range(LANE, cap + 1, LANE):
        if n % t == 0:
            best = t
    return best


def _mm_tiles(mode, M, N, K):
    if mode == "tn":
        tm = _largest_tile(M, 2 * MM_TILE)
        return tm, _largest_tile(N, TN_ACC_ELEMS // tm), _largest_tile(K, TN_K_STEP)
    tk = K if K <= MM_K_WHOLE else _largest_tile(K, MM_K_STEP)
    return _largest_tile(M, MM_TILE), _largest_tile(N, MM_TILE), tk


def _mm(a, b, mode, out_dtype, name, res=None, after=None):
    if mode == "nn":
        (M, K), N = a.shape, b.shape[1]
    elif mode == "nt":
        (M, K), N = a.shape, b.shape[0]
    else:
        (K, M), N = a.shape, b.shape[1]
    tm, tn, tk = _mm_tiles(mode, M, N, K)
    nk = K // tk
    a_spec = pl.BlockSpec((tk, tm), lambda i, j, k: (k, i)) if mode == "tn" else pl.BlockSpec((tm, tk), lambda i, j, k: (i, k))
    b_spec = pl.BlockSpec((tn, tk), lambda i, j, k: (j, k)) if mode == "nt" else pl.BlockSpec((tk, tn), lambda i, j, k: (k, j))
    o_spec = pl.BlockSpec((tm, tn), lambda i, j, k: (i, j))

    def finish(r, r_ref, o_ref):
        if res is not None:
            r = r + r_ref[...].astype(F32)
        o_ref[...] = r.astype(out_dtype)

    def body_one_step(*refs):
        a_ref, b_ref, o_ref = refs[0], refs[1], refs[-1]
        finish(_dot(a_ref[...], b_ref[...], mode), refs[2], o_ref)

    def body(*refs):
        a_ref, b_ref, o_ref, acc = refs[0], refs[1], refs[-2], refs[-1]
        k = pl.program_id(2)

        @pl.when(k == 0)
        def _():
            acc[...] = jnp.zeros_like(acc)

        acc[...] += _dot(a_ref[...], b_ref[...], mode)

        @pl.when(k == nk - 1)
        def _():
            finish(acc[...], refs[2], o_ref)

    ins, specs = [a, b], [a_spec, b_spec]
    if res is not None:
        ins.append(res)
        specs.append(o_spec)
    if after is not None:
        ins.append(after)
        specs.append(pl.BlockSpec(memory_space=pl.ANY))
    return pl.pallas_call(
        body_one_step if nk == 1 else body, name=name, grid=(M // tm, N // tn, nk), in_specs=specs, out_specs=o_spec,
        out_shape=jax.ShapeDtypeStruct((M, N), out_dtype), scratch_shapes=[] if nk == 1 else [pltpu.VMEM((tm, tn), F32)],
        compiler_params=_cp(("parallel", "parallel", "arbitrary")))(*ins)


def _ffn_up(v, w_gate, w_up):
    S, D = v.shape
    F = w_gate.shape[1]
    tm, tn = _largest_tile(S, MM_TILE), _largest_tile(F, MM_TILE // 2)

    def body(v_ref, wg_ref, wu_ref, gt_ref, up_ref, act_ref):
        vv = v_ref[...]
        g = _dot(vv, wg_ref[...], "nn")
        u = _dot(vv, wu_ref[...], "nn")
        gt_ref[...] = g
        up_ref[...] = u
        act_ref[...] = (g * _sigmoid(g) * u).astype(act_ref.dtype)

    o_spec = pl.BlockSpec((tm, tn), lambda i, j: (i, j))
    w_spec = pl.BlockSpec((D, tn), lambda i, j: (0, j))
    return pl.pallas_call(
        body, name="ffn_up", grid=(S // tm, F // tn),
        in_specs=[pl.BlockSpec((tm, D), lambda i, j: (i, 0)), w_spec, w_spec], out_specs=[o_spec, o_spec, o_spec],
        out_shape=[jax.ShapeDtypeStruct((S, F), F32), jax.ShapeDtypeStruct((S, F), F32), jax.ShapeDtypeStruct((S, F), _MXU)],
        compiler_params=_cp(("parallel", "parallel")))(v, w_gate, w_up)


def _ffn_dact(dh2, w_down, gt, up):
    S, D = dh2.shape
    F = w_down.shape[0]
    tm, tn = _largest_tile(S, MM_TILE), _largest_tile(F, MM_TILE // 2)

    def body(d_ref, w_ref, gt_ref, up_ref, dg_ref, du_ref):
        da, g, u = _dot(d_ref[...], w_ref[...], "nt"), gt_ref[...], up_ref[...]
        s = _sigmoid(g)
        dg_ref[...] = (da * u * (s * (1.0 + g * (1.0 - s)))).astype(dg_ref.dtype)
        du_ref[...] = (da * (g * s)).astype(du_ref.dtype)

    o_spec = pl.BlockSpec((tm, tn), lambda i, j: (i, j))
    return pl.pallas_call(
        body, name="ffn_dact", grid=(S // tm, F // tn),
        in_specs=[pl.BlockSpec((tm, D), lambda i, j: (i, 0)), pl.BlockSpec((tn, D), lambda i, j: (j, 0)), o_spec, o_spec],
        out_specs=[o_spec, o_spec],
        out_shape=[jax.ShapeDtypeStruct((S, F), _MXU), jax.ShapeDtypeStruct((S, F), _MXU)],
        compiler_params=_cp(("parallel", "parallel")))(dh2, w_down, gt, up)


def _rms_fwd(x, w, name):
    S, D = x.shape
    tr = _pick(S, (256, 128))

    def body(x_ref, w_ref, xn_ref, rs_ref):
        xv = x_ref[...]
        rs = lax.rsqrt(jnp.mean(xv * xv, axis=-1, keepdims=True) + EPS)
        xn_ref[...] = ((xv * rs) * w_ref[...]).astype(xn_ref.dtype)
        rs_ref[...] = rs

    return pl.pallas_call(
        body, name=name, grid=(S // tr,),
        in_specs=[pl.BlockSpec((tr, D), lambda i: (i, 0)), pl.BlockSpec((1, D), lambda i: (0, 0))],
        out_specs=[pl.BlockSpec((tr, D), lambda i: (i, 0)), pl.BlockSpec((tr, 1), lambda i: (i, 0))],
        out_shape=[jax.ShapeDtypeStruct((S, D), _MXU), jax.ShapeDtypeStruct((S, 1), F32)],
        compiler_params=_cp(("parallel",)))(x, w)


def _rms_bwd(dyn, x, rs, w, res, name):
    S, D = x.shape
    tr = _pick(S, (256, 128))

    def body(dy_ref, x_ref, rs_ref, w_ref, res_ref, dx_ref, dxb_ref, dw_ref):
        @pl.when(pl.program_id(0) == 0)
        def _():
            dw_ref[...] = jnp.zeros_like(dw_ref)

        dy, r = dy_ref[...].astype(F32), rs_ref[...]
        xhat = x_ref[...] * r
        dw_ref[...] += jnp.sum(dy * xhat, axis=0, keepdims=True)
        dxhat = dy * w_ref[...]
        dx = res_ref[...] + r * (dxhat - xhat * jnp.mean(dxhat * xhat, axis=-1, keepdims=True))
        dx_ref[...] = dx
        dxb_ref[...] = dx.astype(dxb_ref.dtype)

    row = pl.BlockSpec((tr, D), lambda i: (i, 0))
    vec = pl.BlockSpec((1, D), lambda i: (0, 0))
    return pl.pallas_call(
        body, name=name, grid=(S // tr,),
        in_specs=[row, row, pl.BlockSpec((tr, 1), lambda i: (i, 0)), vec, row], out_specs=[row, row, vec],
        out_shape=[jax.ShapeDtypeStruct((S, D), F32), jax.ShapeDtypeStruct((S, D), _MXU), jax.ShapeDtypeStruct((1, D), F32)],
        compiler_params=_cp(("arbitrary",)))(dyn, x, rs, w, res)


def _final_loss(h2, w, tgt):
    S, D = h2.shape
    tr = _pick(S, (256, 128))

    def body(h_ref, w_ref, t_ref, loss_ref, dh_ref, dhb_ref, dw_ref):
        @pl.when(pl.program_id(0) == 0)
        def _():
            dw_ref[...] = jnp.zeros_like(dw_ref)
            loss_ref[...] = jnp.zeros_like(loss_ref)

        hv, wv = h_ref[...], w_ref[...]
        rs = lax.rsqrt(jnp.mean(hv * hv, axis=-1, keepdims=True) + EPS)
        xhat = hv * rs
        err = xhat * wv - t_ref[...]
        row = jnp.mean(err * err, axis=-1, keepdims=True)
        loss_ref[...] += jnp.broadcast_to(0.5 * jnp.sum(row, axis=0, keepdims=True), loss_ref.shape)
        dy = err * (1.0 / D)
        dw_ref[...] += jnp.sum(dy * xhat, axis=0, keepdims=True)
        dxhat = dy * wv
        dh = rs * (dxhat - xhat * jnp.mean(dxhat * xhat, axis=-1, keepdims=True))
        dh_ref[...] = dh
        dhb_ref[...] = dh.astype(dhb_ref.dtype)

    row = pl.BlockSpec((tr, D), lambda i: (i, 0))
    vec = pl.BlockSpec((1, D), lambda i: (0, 0))
    return pl.pallas_call(
        body, name="final_loss", grid=(S // tr,), in_specs=[row, vec, row],
        out_specs=[pl.BlockSpec((1, LANE), lambda i: (0, 0)), row, row, vec],
        out_shape=[jax.ShapeDtypeStruct((1, LANE), F32), jax.ShapeDtypeStruct((S, D), F32), jax.ShapeDtypeStruct((S, D), _MXU),
                   jax.ShapeDtypeStruct((1, D), F32)],
        compiler_params=_cp(("arbitrary",)))(h2, w, tgt)


def _shift_rows(x, k, rows):
    if k == 0:
        return x
    S = x.shape[0]
    r = pltpu.roll(x, k % S, axis=0)
    ok = (rows >= k) if k > 0 else (rows < S + k)
    return jnp.where(ok, r, 0.0)


XBC_COL0 = SSD_WIDTH // 128


def _conv_fwd(proj, conv_w, conv_b):
    S = proj.shape[0]
    nct = CONV_CH // 128

    def body(x_ref, w_ref, b_ref, o_ref):
        x = x_ref[...]
        rows = lax.broadcasted_iota(jnp.int32, x.shape, 0)
        c = b_ref[...] + w_ref[3:4, :] * x
        for k in range(1, CONV_K):
            c = c + w_ref[3 - k:4 - k, :] * _shift_rows(x, k, rows)
        o_ref[...] = c * _sigmoid(c)

    return pl.pallas_call(
        body, name="conv_fwd", grid=(nct,),
        in_specs=[pl.BlockSpec((S, 128), lambda j: (0, XBC_COL0 + j)), pl.BlockSpec((CONV_K, 128), lambda j: (0, j)),
                  pl.BlockSpec((1, 128), lambda j: (0, j))],
        out_specs=pl.BlockSpec((S, 128), lambda j: (0, j)),
        out_shape=jax.ShapeDtypeStruct((S, CONV_CH), F32), compiler_params=_cp(("parallel",)))(proj, conv_w, conv_b)


def _conv_bwd(proj, conv_w, conv_b, dxa):
    S = proj.shape[0]
    nct = CONV_CH // 128

    def body(x_ref, w_ref, b_ref, d_ref, dx_ref, dw_ref, db_ref):
        x = x_ref[...]
        rows = lax.broadcasted_iota(jnp.int32, x.shape, 0)
        xs = [_shift_rows(x, k, rows) for k in range(CONV_K)]
        c = b_ref[...] + w_ref[3:4, :] * x
        for k in range(1, CONV_K):
            c = c + w_ref[3 - k:4 - k, :] * xs[k]
        s = _sigmoid(c)
        dc = d_ref[...] * (s * (1.0 + c * (1.0 - s)))
        dx = w_ref[3:4, :] * dc
        for k in range(1, CONV_K):
            dx = dx + w_ref[3 - k:4 - k, :] * _shift_rows(dc, -k, rows)
        dx_ref[...] = dx.astype(dx_ref.dtype)
        for k in range(CONV_K):
            dw_ref[3 - k:4 - k, :] = jnp.sum(dc * xs[k], axis=0, keepdims=True)
        db_ref[...] = jnp.sum(dc, axis=0, keepdims=True)

    col = pl.BlockSpec((S, 128), lambda j: (0, j))
    return pl.pallas_call(
        body, name="conv_bwd", grid=(nct,),
        in_specs=[pl.BlockSpec((S, 128), lambda j: (0, XBC_COL0 + j)), pl.BlockSpec((CONV_K, 128), lambda j: (0, j)),
                  pl.BlockSpec((1, 128), lambda j: (0, j)), col],
        out_specs=[col, pl.BlockSpec((CONV_K, 128), lambda j: (0, j)), pl.BlockSpec((1, 128), lambda j: (0, j))],
        out_shape=[jax.ShapeDtypeStruct((S, CONV_CH), _MXU), jax.ShapeDtypeStruct((CONV_K, CONV_CH), F32),
                   jax.ShapeDtypeStruct((1, CONV_CH), F32)],
        compiler_params=_cp(("parallel",)))(proj, conv_w, conv_b, dxa)


def _ssd_consts():
    L = SSD_L
    r = lax.broadcasted_iota(jnp.int32, (L, L), 0)
    c = lax.broadcasted_iota(jnp.int32, (L, L), 1)
    causal = r >= c
    upper = (r <= c).astype(F32)
    hr = lax.broadcasted_iota(jnp.int32, (SSD_HEADS, SSD_WIDTH), 0)
    hc = lax.broadcasted_iota(jnp.int32, (SSD_HEADS, SSD_WIDTH), 1)
    expand = (lax.shift_right_logical(hc, 6) == hr).astype(F32)
    return causal, causal.astype(F32), upper, expand


def _softplus(x):
    return jnp.maximum(x, 0.0) + jnp.log(1.0 + jnp.exp(-jnp.abs(x)))


def _ssd_scalars(dtr, dt_bias, a_log, tri, upper, expand):
    dt = _softplus(dtr + dt_bias)
    A = -jnp.exp(a_log)
    adt = dt * A
    acum = _dot(tri, adt, "nn", hi=True)
    acum_t = _dot(adt, upper, "tn", hi=True)
    alast = acum[SSD_L - 1:SSD_L, :]
    e = jnp.exp(acum)
    wdec = jnp.exp(alast - acum)
    gam = jnp.exp(alast)
    ex = lambda t: _dot(t, expand, "nn", hi=True)
    gam8 = jnp.broadcast_to(gam, (8, SSD_HEADS))
    return dt, A, acum, acum_t, e, wdec, gam, ex(dt), ex(e), ex(wdec), ex(gam8)[0:1, :]


def _ssd_fwd(proj, proj_small, xa, dt_bias, a_log, d_skip, norm_w):
    S = proj.shape[0]
    L, N, W = SSD_L, SSD_N, SSD_WIDTH
    nc = S // L

    def body(z_ref, xa_ref, dtr_ref, dtb_ref, al_ref, dsk_ref, nw_ref, yo_ref, y_ref, rs_ref, hs_ref, h_scr, y_scr):
        @pl.when(pl.program_id(0) == 0)
        def _():
            h_scr[...] = jnp.zeros_like(h_scr)

        causal, tri, upper, expand = _ssd_consts()
        dt, A, acum, acum_t, e, wdec, gam, dtE, eE, wE, gamE = _ssd_scalars(dtr_ref[:, 0:SSD_HEADS], dtb_ref[...], al_ref[...], tri, upper, expand)
        xs = xa_ref[:, 0:W]
        X = xs * dtE
        XW = X * wE
        hs_ref[0] = h_scr[...]
        for g in range(SSD_G):
            gs = slice(g * 512, (g + 1) * 512)
            Bg = xa_ref[:, W + g * N:W + (g + 1) * N]
            Cg = xa_ref[:, W + SSD_G * N + g * N:W + SSD_G * N + (g + 1) * N]
            Hg = h_scr[:, gs]
            CB = _dot(Cg, Bg, "nt")
            yoff = _dot(Cg, Hg, "nn") * eE[:, gs]
            st = _dot(Bg, XW[:, gs], "tn")
            for j in range(8):
                h = g * 8 + j
                hsl = slice(h * SSD_P, (h + 1) * SSD_P)
                lam = jnp.exp(jnp.where(causal, acum[:, h:h + 1] - acum_t[h:h + 1, :], -jnp.inf))
                y_scr[:, hsl] = _dot(CB * lam, X[:, hsl], "nn") + yoff[:, j * SSD_P:(j + 1) * SSD_P]
            h_scr[:, gs] = gamE[:, gs] * Hg + st
        dskE = _dot(jnp.broadcast_to(dsk_ref[...], (8, SSD_HEADS)), expand, "nn", hi=True)[0:1, :]
        y = y_scr[...] + dskE * xs
        y_ref[...] = y
        zv = z_ref[...]
        yg = y * (zv * _sigmoid(zv))
        rs = lax.rsqrt(jnp.mean(yg * yg, axis=-1, keepdims=True) + EPS)
        rs_ref[...] = rs
        yo_ref[...] = ((yg * rs) * nw_ref[...]).astype(yo_ref.dtype)

    p16 = pl.BlockSpec((1, SSD_HEADS), lambda c: (0, 0))
    return pl.pallas_call(
        body, name="ssd_fwd", grid=(nc,),
        in_specs=[pl.BlockSpec((L, W), lambda c: (c, 0)), pl.BlockSpec((L, CONV_CH), lambda c: (c, 0)),
                  pl.BlockSpec((L, W_SMALL), lambda c: (c, 0)), p16, p16, p16, pl.BlockSpec((1, W), lambda c: (0, 0))],
        out_specs=[pl.BlockSpec((L, W), lambda c: (c, 0)), pl.BlockSpec((L, W), lambda c: (c, 0)),
                   pl.BlockSpec((L, 1), lambda c: (c, 0)), pl.BlockSpec((1, N, W), lambda c: (c, 0, 0))],
        out_shape=[jax.ShapeDtypeStruct((S, W), _MXU), jax.ShapeDtypeStruct((S, W), F32), jax.ShapeDtypeStruct((S, 1), F32),
                   jax.ShapeDtypeStruct((nc, N, W), F32)],
        scratch_shapes=[pltpu.VMEM((N, W), F32), pltpu.VMEM((L, W), F32)],
        compiler_params=_cp(("arbitrary",)))(proj, xa, proj_small, dt_bias, a_log, d_skip, norm_w)


def _ssd_bwd(dmixed, proj, proj_small, xa, y, rs2, hs, dt_bias, a_log, d_skip, norm_w):
    S = proj.shape[0]
    L, N, W, H = SSD_L, SSD_N, SSD_WIDTH, SSD_HEADS
    nc = S // L

    def body(dyo_ref, z_ref, xa_ref, dtr_ref, y_ref, rs_ref, hs_ref, dtb_ref, al_ref, dsk_ref, nw_ref,
             dz_ref, dxa_ref, ddtr_ref, ddtb_ref, dal_ref, ddsk_ref, dnw_ref, dh_scr, dx_scr):
        @pl.when(pl.program_id(0) == 0)
        def _():
            dh_scr[...] = jnp.zeros_like(dh_scr)
            ddtb_ref[...] = jnp.zeros_like(ddtb_ref)
            dal_ref[...] = jnp.zeros_like(dal_ref)
            ddsk_ref[...] = jnp.zeros_like(ddsk_ref)
            dnw_ref[...] = jnp.zeros_like(dnw_ref)

        causal, tri, upper, expand = _ssd_consts()
        heads = lambda t: _dot(t, expand, "nt", hi=True)
        onehot = lambda h: (lax.broadcasted_iota(jnp.int32, (1, H), 1) == h).astype(F32)

        zv, yv, rs = z_ref[...], y_ref[...], rs_ref[...]
        sz = _sigmoid(zv)
        zs = zv * sz
        xhat = (yv * zs) * rs
        dyo = dyo_ref[...].astype(F32)
        dnw_ref[...] += jnp.sum(dyo * xhat, axis=0, keepdims=True)
        dxhat = dyo * nw_ref[...]
        dyg = rs * (dxhat - xhat * jnp.mean(dxhat * xhat, axis=-1, keepdims=True))
        dz_ref[...] = (dyg * yv * (sz * (1.0 + zv * (1.0 - sz)))).astype(dz_ref.dtype)
        dy = dyg * zs

        dtr = dtr_ref[:, 0:H]
        dt, A, acum, acum_t, e, wdec, gam, dtE, eE, wE, gamE = _ssd_scalars(dtr, dtb_ref[...], al_ref[...], tri, upper, expand)
        xs = xa_ref[:, 0:W]
        X = xs * dtE
        XW = X * wE
        dskE = _dot(jnp.broadcast_to(dsk_ref[...], (8, H)), expand, "nn", hi=True)[0:1, :]
        ddsk_ref[...] += heads(jnp.broadcast_to(jnp.sum(dy * xs, axis=0, keepdims=True), (8, W)))[0:1, :]

        dYe = dy * eE
        dacum = jnp.zeros((L, H), F32)
        de_full = []
        dw_full = []
        dgam_full = []
        for g in range(SSD_G):
            gs = slice(g * 512, (g + 1) * 512)
            Bg = xa_ref[:, W + g * N:W + (g + 1) * N]
            Cg = xa_ref[:, W + SSD_G * N + g * N:W + SSD_G * N + (g + 1) * N]
            Hg = hs_ref[0, :, gs]
            dHn = dh_scr[:, gs]
            CH = _dot(Cg, Hg, "nn")
            de_full.append(dy[:, gs] * CH)
            dC = _dot(dYe[:, gs], Hg, "nt")
            dHs = gamE[:, gs] * dHn + _dot(Cg, dYe[:, gs], "tn")
            dgam_full.append(jnp.sum(dHn * Hg, axis=0, keepdims=True))
            BdS = _dot(Bg, dHn, "nn")
            dB = _dot(XW[:, gs], dHn, "nt")
            dx_scr[:, gs] = BdS * wE[:, gs]
            dw_full.append(BdS * X[:, gs])
            CB = _dot(Cg, Bg, "nt")
            dCB = jnp.zeros((L, L), F32)
            for j in range(8):
                h = g * 8 + j
                hsl = slice(h * SSD_P, (h + 1) * SSD_P)
                lam = jnp.exp(jnp.where(causal, acum[:, h:h + 1] - acum_t[h:h + 1, :], -jnp.inf))
                M = CB * lam
                dM = _dot(dy[:, hsl], X[:, hsl], "nt")
                dx_scr[:, hsl] += _dot(M, dy[:, hsl], "tn")
                dCB = dCB + dM * lam
                Q = dM * M
                rowsum = jnp.sum(Q, axis=1, keepdims=True)
                colsum = _dot(Q, jnp.ones((L, 8), F32), "tn", hi=True)[:, 0:1]
                dacum = dacum + (rowsum - colsum) * onehot(h)
            dC = dC + _dot(dCB, Bg, "nn")
            dB = dB + _dot(dCB, Cg, "tn")
            dxa_ref[:, W + g * N:W + (g + 1) * N] = dB
            dxa_ref[:, W + SSD_G * N + g * N:W + SSD_G * N + (g + 1) * N] = dC
            dh_scr[:, gs] = dHs

        de16 = heads(jnp.concatenate(de_full, axis=1))
        dw16 = heads(jnp.concatenate(dw_full, axis=1))
        dgam16 = heads(jnp.broadcast_to(jnp.concatenate(dgam_full, axis=1), (8, W)))[0:1, :]
        dacum = dacum + de16 * e - dw16 * wdec
        dlast = jnp.sum(dw16 * wdec, axis=0, keepdims=True) + dgam16 * gam
        lastrow = (lax.broadcasted_iota(jnp.int32, (L, 1), 0) == L - 1).astype(F32)
        dacum = dacum + lastrow * dlast
        da = _dot(tri, dacum, "tn", hi=True)
        dX = dx_scr[...]
        ddt = da * A + heads(dX * xs)
        dA = jnp.sum(da * dt, axis=0, keepdims=True)
        dal_ref[...] += dA * A
        ddtr = ddt * _sigmoid(dtr + dtb_ref[...])
        ddtb_ref[...] += jnp.sum(ddtr, axis=0, keepdims=True)
        ddtr_ref[...] = ddtr
        dxa_ref[:, 0:W] = dX * dtE + dy * dskE

    p16 = pl.BlockSpec((1, H), lambda c: (0, 0))
    rev = lambda c: (nc - 1 - c, 0)
    return pl.pallas_call(
        body, name="ssd_bwd", grid=(nc,),
        in_specs=[pl.BlockSpec((L, W), rev), pl.BlockSpec((L, W), rev), pl.BlockSpec((L, CONV_CH), rev),
                  pl.BlockSpec((L, W_SMALL), rev), pl.BlockSpec((L, W), rev), pl.BlockSpec((L, 1), rev),
                  pl.BlockSpec((1, N, W), lambda c: (nc - 1 - c, 0, 0)), p16, p16, p16, pl.BlockSpec((1, W), lambda c: (0, 0))],
        out_specs=[pl.BlockSpec((L, W), rev), pl.BlockSpec((L, CONV_CH), rev), pl.BlockSpec((L, H), rev),
                   p16, p16, p16, pl.BlockSpec((1, W), lambda c: (0, 0))],
        out_shape=[jax.ShapeDtypeStruct((S, W), _MXU), jax.ShapeDtypeStruct((S, CONV_CH), F32), jax.ShapeDtypeStruct((S, H), F32),
                   jax.ShapeDtypeStruct((1, H), F32), jax.ShapeDtypeStruct((1, H), F32), jax.ShapeDtypeStruct((1, H), F32),
                   jax.ShapeDtypeStruct((1, W), F32)],
        scratch_shapes=[pltpu.VMEM((N, W), F32), pltpu.VMEM((L, W), F32)],
        compiler_params=_cp(("arbitrary",)))(dmixed, proj, xa, proj_small, y, rs2, hs, dt_bias, a_log, d_skip, norm_w)


def _rope_tables(S):
    inv = 1.0 / (ROPE_THETA ** (jnp.arange(0, ROPE_DIM, 2, dtype=F32) / ROPE_DIM))
    ang = jnp.arange(S, dtype=F32)[:, None] * inv[None, :]
    cos, sin = jnp.cos(ang), jnp.sin(ang)
    half = ROPE_DIM // 2
    c64 = jnp.concatenate([cos, cos, jnp.ones((S, HD - ROPE_DIM), F32)], axis=1)
    s64 = jnp.concatenate([sin, sin, jnp.zeros((S, HD - ROPE_DIM), F32)], axis=1)
    del half
    return jnp.concatenate([c64, c64], axis=1), jnp.concatenate([s64, s64], axis=1)


def _rope(xs, blk0, width, cos, sin, sign, out_dtype, name, extra=None):
    S = xs[0].shape[0]
    tr = _pick(S, (512, 256, 128))
    nx = len(xs)

    def body(*refs):
        x_refs, c_ref, s_ref = refs[:nx], refs[nx], refs[nx + 1]
        e_ref = refs[nx + 2] if extra is not None else None
        o_ref = refs[-1]
        cv, sv = c_ref[...], s_ref[...] * sign
        lane = lax.broadcasted_iota(jnp.int32, (tr, 128), 1)
        first = (lane & (HD - 1)) < (ROPE_DIM // 2)
        for j in range(2):
            cs = slice(j * 128, (j + 1) * 128)
            xv = x_refs[0][:, cs].astype(F32)
            for r in x_refs[1:]:
                xv = xv + r[:, cs].astype(F32)
            rot = jnp.where(first, -pltpu.roll(xv, 128 - ROPE_DIM // 2, axis=1), pltpu.roll(xv, ROPE_DIM // 2, axis=1))
            out = xv * cv + rot * sv
            if extra is not None:
                out = out + e_ref[:, cs].astype(F32)
            o_ref[:, cs] = out.astype(out_dtype)

    t128 = pl.BlockSpec((tr, 128), lambda i, j: (i, 0))
    oblk = pl.BlockSpec((tr, 256), lambda i, j: (i, j))
    specs = [pl.BlockSpec((tr, 256), lambda i, j: (i, blk0 + j))] * nx + [t128, t128]
    ins = list(xs) + [cos, sin]
    if extra is not None:
        ins.append(extra[0])
        eb = extra[1]
        specs.append(pl.BlockSpec((tr, 256), lambda i, j: (i, eb + j)))
    return pl.pallas_call(
        body, name=name, grid=(S // tr, width // 256), in_specs=specs, out_specs=oblk,
        out_shape=jax.ShapeDtypeStruct((S, width), out_dtype), compiler_params=_cp(("parallel", "parallel")))(*ins)


def _rotate128(xv, cv, sv, first):
    rot = jnp.where(first, -pltpu.roll(xv, 128 - ROPE_DIM // 2, axis=1), pltpu.roll(xv, ROPE_DIM // 2, axis=1))
    return xv * cv + rot * sv


def _kv_prep(proj, cos, sin, tk):
    S = proj.shape[0]

    def body(ks_ref, vs_ref, kw_ref, vw_ref, c_ref, s_ref, *outs):
        cv, sv = c_ref[...], s_ref[...]
        lane = lax.broadcasted_iota(jnp.int32, (tk, 128), 1)
        first = (lane & (HD - 1)) < (ROPE_DIM // 2)
        for j, (ref, rotated) in enumerate(((ks_ref, True), (vs_ref, False), (kw_ref, True), (vw_ref, False))):
            nat, blk = outs[2 * j], outs[2 * j + 1]
            for half in range(2):
                xv = ref[:, half * 128:(half + 1) * 128]
                if rotated:
                    xv = _rotate128(xv, cv, sv, first)
                for e in range(2):
                    h = 2 * half + e
                    piece = xv[:, e * HD:(e + 1) * HD]
                    nat[h] = piece.astype(nat.dtype)
                    blk[h, 0] = piece.T.astype(blk.dtype)

    col = lambda b: pl.BlockSpec((tk, 256), lambda i: (i, b))
    t128 = pl.BlockSpec((tk, 128), lambda i: (i, 0))
    nat_spec = pl.BlockSpec((N_KV, tk, HD), lambda i: (0, i, 0))
    blk_spec = pl.BlockSpec((N_KV, 1, HD, tk), lambda i: (0, i, 0, 0))
    nat_shape = jax.ShapeDtypeStruct((N_KV, S, HD), _MXU)
    blk_shape = jax.ShapeDtypeStruct((N_KV, S // tk, HD, tk), _MXU)
    res = pl.pallas_call(
        body, name="kv_prep", grid=(S // tk,), in_specs=[col(KSB), col(VSB), col(KWB), col(VWB), t128, t128],
        out_specs=[nat_spec, blk_spec] * 4, out_shape=[nat_shape, blk_shape] * 4,
        compiler_params=_cp(("parallel",)))(proj, proj, proj, proj, cos, sin)
    return dict(ks=res[0], ks_t=res[1], vs=res[2], vs_t=res[3], kw=res[4], kw_t=res[5], vw=res[6], vw_t=res[7])


def _dkv_post(dks, dvs, dkw, dvw, cos, sin):
    S = dks.shape[1]
    tr = _pick(S, (512, 256, 128))

    def body(dks_ref, dvs_ref, dkw_ref, dvw_ref, c_ref, s_ref, o_ref):
        cv, sv = c_ref[...], -s_ref[...]
        lane = lax.broadcasted_iota(jnp.int32, (tr, 128), 1)
        first = (lane & (HD - 1)) < (ROPE_DIM // 2)
        for j, (ref, rotated) in enumerate(((dks_ref, True), (dvs_ref, False), (dkw_ref, True), (dvw_ref, False))):
            for half in range(2):
                xv = jnp.concatenate([ref[2 * half], ref[2 * half + 1]], axis=1)
                if rotated:
                    xv = _rotate128(xv, cv, sv, first)
                o_ref[:, j * 256 + half * 128:j * 256 + (half + 1) * 128] = xv.astype(o_ref.dtype)

    hm = pl.BlockSpec((N_KV, tr, HD), lambda i: (0, i, 0))
    t128 = pl.BlockSpec((tr, 128), lambda i: (i, 0))
    return pl.pallas_call(
        body, name="dkv_post", grid=(S // tr,), in_specs=[hm, hm, hm, hm, t128, t128],
        out_specs=pl.BlockSpec((tr, 4 * 256), lambda i: (i, 0)), out_shape=jax.ShapeDtypeStruct((S, 4 * 256), _MXU),
        compiler_params=_cp(("parallel",)))(dks, dvs, dkw, dvw, cos, sin)


def _compress_fwd(R, pe, w1, w2):
    NC = R.shape[1]
    half = 16 * HD

    def body(r_ref, pe_ref, w1_ref, w2_ref, o_ref, hid_ref):
        r = r_ref[0]
        a = _dot(r + pe_ref[:, 0:half], w1_ref[0:half, :], "nn")
        b = _dot(r + pe_ref[:, half:2 * half], w1_ref[half:2 * half, :], "nn")
        hid = a + pltpu.roll(b, NC - 1, axis=0)
        hid_ref[0] = hid
        out = _dot(hid * _sigmoid(hid), w2_ref[...], "nn")
        rows = lax.broadcasted_iota(jnp.int32, out.shape, 0)
        o_ref[0] = jnp.where(rows < NC - 1, out, 0.0).astype(o_ref.dtype)

    return pl.pallas_call(
        body, name="compress_fwd", grid=(N_KV,),
        in_specs=[pl.BlockSpec((1, NC, half), lambda h: (h, 0, 0)), pl.BlockSpec((1, 2 * half), lambda h: (0, 0)),
                  pl.BlockSpec((2 * half, CMP_HID), lambda h: (0, 0)), pl.BlockSpec((CMP_HID, HD), lambda h: (0, 0))],
        out_specs=[pl.BlockSpec((1, NC, HD), lambda h: (h, 0, 0)), pl.BlockSpec((1, NC, CMP_HID), lambda h: (h, 0, 0))],
        out_shape=[jax.ShapeDtypeStruct((N_KV, NC, HD), _MXU), jax.ShapeDtypeStruct((N_KV, NC, CMP_HID), F32)],
        compiler_params=_cp(("parallel",)))(R, pe, w1, w2)


def _compress_bwd(R, pe, w1, w2, hid, dout):
    NC = R.shape[1]
    half = 16 * HD

    def body(r_ref, pe_ref, w1_ref, w2_ref, hid_ref, do_ref, dr_ref, dw1_ref, dw2_ref, dpe_ref):
        @pl.when(pl.program_id(0) == 0)
        def _():
            dw1_ref[...] = jnp.zeros_like(dw1_ref)
            dw2_ref[...] = jnp.zeros_like(dw2_ref)
            dpe_ref[...] = jnp.zeros_like(dpe_ref)

        r, hv, do = r_ref[0], hid_ref[0], do_ref[0]
        s = _sigmoid(hv)
        dw2_ref[...] += _dot(hv * s, do, "tn")
        dhid = _dot(do, w2_ref[...], "nt") * (s * (1.0 + hv * (1.0 - s)))
        rows = lax.broadcasted_iota(jnp.int32, dhid.shape, 0)
        dhid = jnp.where(rows < NC - 1, dhid, 0.0)
        dhid_dn = pltpu.roll(dhid, 1, axis=0)
        dw1_ref[0:half, :] += _dot(r + pe_ref[:, 0:half], dhid, "tn")
        dw1_ref[half:2 * half, :] += _dot(r + pe_ref[:, half:2 * half], dhid_dn, "tn")
        dxt = _dot(dhid, w1_ref[0:half, :], "nt")
        dxb = _dot(dhid_dn, w1_ref[half:2 * half, :], "nt")
        dr_ref[0] = dxt + dxb
        dpe_ref[:, 0:half] += jnp.sum(dxt, axis=0, keepdims=True)
        dpe_ref[:, half:2 * half] += jnp.sum(dxb, axis=0, keepdims=True)

    return pl.pallas_call(
        body, name="compress_bwd", grid=(N_KV,),
        in_specs=[pl.BlockSpec((1, NC, half), lambda h: (h, 0, 0)), pl.BlockSpec((1, 2 * half), lambda h: (0, 0)),
                  pl.BlockSpec((2 * half, CMP_HID), lambda h: (0, 0)), pl.BlockSpec((CMP_HID, HD), lambda h: (0, 0)),
                  pl.BlockSpec((1, NC, CMP_HID), lambda h: (h, 0, 0)), pl.BlockSpec((1, NC, HD), lambda h: (h, 0, 0))],
        out_specs=[pl.BlockSpec((1, NC, half), lambda h: (h, 0, 0)), pl.BlockSpec((2 * half, CMP_HID), lambda h: (0, 0)),
                   pl.BlockSpec((CMP_HID, HD), lambda h: (0, 0)), pl.BlockSpec((1, 2 * half), lambda h: (0, 0))],
        out_shape=[jax.ShapeDtypeStruct((N_KV, NC, half), F32), jax.ShapeDtypeStruct((2 * half, CMP_HID), F32),
                   jax.ShapeDtypeStruct((CMP_HID, HD), F32), jax.ShapeDtypeStruct((1, 2 * half), F32)],
        compiler_params=_cp(("arbitrary",)))(R, pe, w1, w2, hid, dout)


def _attn_cfg(S, Sk, mode):
    tq = _pick(S, (256, 128))
    tk = Sk if mode == "cmp" else _pick(Sk, (256, 128))
    return tq, tk


def _kb_range(mode, q0, tq, tk):
    if mode == "cmp":
        return 0, 1
    hi = (q0 + tq - 1) // tk + 1
    if mode == "sel":
        return 0, hi
    return jnp.maximum(q0 - (WINDOW - 1), 0) // tk, hi


def _attn_bias(mode, q0, k0, tq, tk, sel_t):
    k = k0 + lax.broadcasted_iota(jnp.int32, (tk, tq), 0)
    t = q0 + lax.broadcasted_iota(jnp.int32, (tk, tq), 1)
    if mode == "cmp":
        ok = (k * 16 + 31) <= t
    elif mode == "win":
        ok = (k <= t) & ((t - k) < WINDOW)
    else:
        nb = sel_t.shape[0]
        ek = k0 + lax.broadcasted_iota(jnp.int32, (tk, nb), 0)
        eb = lax.broadcasted_iota(jnp.int32, (tk, nb), 1)
        expand = (lax.shift_right_logical(ek, 6) == eb).astype(_MXU)
        chosen = _dot(expand, sel_t, "nn") > 0.5
        ok = (k <= t) & chosen
    bias = jnp.where(ok, 0.0, NEG)
    return jnp.concatenate([bias] * GRP, axis=1), jnp.concatenate([ok.astype(F32)] * GRP, axis=1)


def _stack_heads(ref, tq):
    return jnp.concatenate([ref[:, g * HD:(g + 1) * HD] for g in range(GRP)], axis=0)


def _scaled_queries(q_ref, tq):
    return (_stack_heads(q_ref, tq).astype(F32) * SCALE).astype(_MXU)


def _blocked_t(x, tk):
    n, Sk, d = x.shape
    return x.reshape(n, Sk // tk, tk, d).transpose(0, 1, 3, 2)


def _attn_fwd(q, qcol0, k, vt, mode, sel_t, name):
    S, Sk = q.shape[0], k.shape[1]
    tq, tk = _attn_cfg(S, Sk, mode)
    R = GRP * tq

    def body(*refs):
        if mode == "sel":
            q_ref, k_ref, vt_ref, sel_ref, o_ref, lse_ref, m_scr, l_scr, acc = refs
        else:
            q_ref, k_ref, vt_ref, o_ref, lse_ref, m_scr, l_scr, acc = refs
        q0 = pl.program_id(1) * tq
        qs = _scaled_queries(q_ref, tq)
        m_scr[...] = jnp.full_like(m_scr, NEG)
        l_scr[...] = jnp.zeros_like(l_scr)
        acc[...] = jnp.zeros_like(acc)
        selv = sel_ref[0].astype(_MXU) if mode == "sel" else None

        def step(kb, carry):
            k0 = pl.multiple_of(kb * tk, tk)
            bias, okf = _attn_bias(mode, q0, k0, tq, tk, selv)
            s = _dot(k_ref[0, pl.ds(k0, tk), :], qs, "nt") + bias
            m_old = m_scr[...]
            m_new = jnp.maximum(m_old, jnp.max(s, axis=0, keepdims=True))
            p = jnp.exp(s - m_new)
            if mode == "cmp":
                p = p * okf
            alpha = jnp.exp(m_old - m_new)
            l_scr[...] = alpha * l_scr[...] + jnp.sum(p, axis=0, keepdims=True)
            acc[...] = alpha * acc[...] + _dot(vt_ref[0, kb], p, "nn")
            m_scr[...] = m_new
            return carry

        lo, hi = _kb_range(mode, q0, tq, tk)
        lax.fori_loop(lo, hi, step, 0)
        l = l_scr[...]
        good = l > 0.0
        o_t = acc[...] * jnp.where(good, 1.0 / jnp.where(good, l, 1.0), 0.0)
        lse = jnp.where(good, m_scr[...] + jnp.log(jnp.where(good, l, 1.0)), -NEG)
        for g in range(GRP):
            o_ref[:, g * HD:(g + 1) * HD] = o_t[:, g * tq:(g + 1) * tq].T
            lse_ref[0, g:g + 1, :] = lse[:, g * tq:(g + 1) * tq]

    ins = [q, k, vt]
    specs = [pl.BlockSpec((tq, GRP * HD), lambda h, i: (i, qcol0 + h)), pl.BlockSpec((1, Sk, HD), lambda h, i: (h, 0, 0)),
             pl.BlockSpec((1, Sk // tk, HD, tk), lambda h, i: (h, 0, 0, 0))]
    if mode == "sel":
        ins.append(sel_t)
        specs.append(pl.BlockSpec((1, sel_t.shape[1], tq), lambda h, i: (h, 0, i)))
    return pl.pallas_call(
        body, name=name, grid=(N_KV, S // tq), in_specs=specs,
        out_specs=[pl.BlockSpec((tq, GRP * HD), lambda h, i: (i, h)), pl.BlockSpec((1, GRP, tq), lambda h, i: (h, 0, i))],
        out_shape=[jax.ShapeDtypeStruct((S, ATT_WIDTH), F32), jax.ShapeDtypeStruct((N_KV, GRP, S), F32)],
        scratch_shapes=[pltpu.VMEM((1, R), F32), pltpu.VMEM((1, R), F32), pltpu.VMEM((HD, R), F32)],
        compiler_params=_cp(("parallel", "arbitrary")))(*ins)


def _attn_bwd(q, qcol0, k, kt, v, o, lse, do, mode, sel_t, name):
    S, Sk = q.shape[0], k.shape[1]
    tq, tk = _attn_cfg(S, Sk, mode)
    R = GRP * tq

    def body(*refs):
        if mode == "sel":
            q_ref, k_ref, kt_ref, v_ref, o_ref, lse_ref, do_ref, sel_ref, dq_ref, dk_ref, dv_ref, dq_scr = refs
        else:
            q_ref, k_ref, kt_ref, v_ref, o_ref, lse_ref, do_ref, dq_ref, dk_ref, dv_ref, dq_scr = refs

        @pl.when(pl.program_id(1) == 0)
        def _():
            dk_ref[...] = jnp.zeros_like(dk_ref)
            dv_ref[...] = jnp.zeros_like(dv_ref)

        q0 = pl.program_id(1) * tq
        qs = _scaled_queries(q_ref, tq)
        dos = _stack_heads(do_ref, tq)
        delta = _dot(jnp.ones((8, HD), F32), dos * _stack_heads(o_ref, tq), "nt", hi=True)[0:1, :]
        lsev = jnp.concatenate([lse_ref[0, g:g + 1, :] for g in range(GRP)], axis=1)
        dos = dos.astype(_MXU)
        dq_scr[...] = jnp.zeros_like(dq_scr)
        selv = sel_ref[0].astype(_MXU) if mode == "sel" else None

        def step(kb, carry):
            k0 = pl.multiple_of(kb * tk, tk)
            kv = k_ref[0, pl.ds(k0, tk), :]
            bias, okf = _attn_bias(mode, q0, k0, tq, tk, selv)
            p = jnp.exp(_dot(kv, qs, "nt") + bias - lsev)
            if mode == "cmp":
                p = p * okf
            dp = _dot(v_ref[0, pl.ds(k0, tk), :], dos, "nt")
            ds = p * (dp - delta)
            dq_scr[...] += _dot(kt_ref[0, kb], ds, "nn")
            dk_ref[0, pl.ds(k0, tk), :] += _dot(ds, qs, "nn")
            dv_ref[0, pl.ds(k0, tk), :] += _dot(p, dos, "nn")
            return carry

        lo, hi = _kb_range(mode, q0, tq, tk)
        lax.fori_loop(lo, hi, step, 0)
        for g in range(GRP):
            dq_ref[:, g * HD:(g + 1) * HD] = (dq_scr[:, g * tq:(g + 1) * tq] * SCALE).T

    kv_spec = pl.BlockSpec((1, Sk, HD), lambda h, i: (h, 0, 0))
    qo_spec = pl.BlockSpec((tq, GRP * HD), lambda h, i: (i, h))
    ins = [q, k, kt, v, o, lse, do]
    specs = [pl.BlockSpec((tq, GRP * HD), lambda h, i: (i, qcol0 + h)), kv_spec,
             pl.BlockSpec((1, Sk // tk, HD, tk), lambda h, i: (h, 0, 0, 0)), kv_spec, qo_spec,
             pl.BlockSpec((1, GRP, tq), lambda h, i: (h, 0, i)), qo_spec]
    if mode == "sel":
        ins.append(sel_t)
        specs.append(pl.BlockSpec((1, sel_t.shape[1], tq), lambda h, i: (h, 0, i)))
    return pl.pallas_call(
        body, name=name, grid=(N_KV, S // tq), in_specs=specs, out_specs=[qo_spec, kv_spec, kv_spec],
        out_shape=[jax.ShapeDtypeStruct((S, ATT_WIDTH), F32), jax.ShapeDtypeStruct((N_KV, Sk, HD), F32),
                   jax.ShapeDtypeStruct((N_KV, Sk, HD), F32)],
        scratch_shapes=[pltpu.VMEM((HD, R), F32)],
        compiler_params=_cp(("parallel", "arbitrary")))(*ins)


def _select(q, qcol0, k_cmp, lse):
    S, NC = q.shape[0], k_cmp.shape[1]
    NB = S // SEL_BLOCK
    tq = _pick(S, (256, 128))
    ci = np.arange(NC)[None, :] * 16
    sj = np.arange(NB)[:, None] * SEL_BLOCK
    ov_t = np.clip(np.minimum(ci + 32, sj + SEL_BLOCK) - np.maximum(ci, sj), 0, None) / 32.0
    ov_t[:, NC - 1] = 0.0
    ov_t = jnp.asarray(ov_t, F32)

    def body(q_ref, k_ref, lse_ref, ov_ref, sel_ref):
        q0 = pl.program_id(1) * tq
        bias, okf = _attn_bias("cmp", q0, 0, tq, NC, None)
        lsev = jnp.concatenate([lse_ref[0, g:g + 1, :] for g in range(GRP)], axis=1)
        p = jnp.exp(_dot(k_ref[0], _scaled_queries(q_ref, tq), "nt") + bias - lsev) * okf
        imp4 = _dot(ov_ref[...], p, "nn")
        imp = imp4[:, 0:tq] + imp4[:, tq:2 * tq] + imp4[:, 2 * tq:3 * tq] + imp4[:, 3 * tq:4 * tq]
        blk = lax.broadcasted_iota(jnp.int32, (NB, tq), 0)
        cur = lax.shift_right_logical(q0 + lax.broadcasted_iota(jnp.int32, (NB, tq), 1), 6)
        imp = jnp.where((blk == 0) | (blk == cur) | (blk == cur - 1), FORCE, imp)
        imp = jnp.where(blk <= cur, imp, -1.0)
        rank = jnp.zeros((NB, tq), F32)
        for j in range(NB):
            row = imp[j:j + 1, :]
            ahead = (row > imp) | ((row == imp) & (blk > j))
            rank = rank + ahead.astype(F32)
        sel_ref[0] = ((rank < float(N_SELECT)) & (imp >= 0.0)).astype(F32)

    return pl.pallas_call(
        body, name="select_blocks", grid=(N_KV, S // tq),
        in_specs=[pl.BlockSpec((tq, GRP * HD), lambda h, i: (i, qcol0 + h)), pl.BlockSpec((1, NC, HD), lambda h, i: (h, 0, 0)),
                  pl.BlockSpec((1, GRP, tq), lambda h, i: (h, 0, i)), pl.BlockSpec((NB, NC), lambda h, i: (0, 0))],
        out_specs=pl.BlockSpec((1, NB, tq), lambda h, i: (h, 0, i)),
        out_shape=jax.ShapeDtypeStruct((N_KV, NB, S), F32), compiler_params=_cp(("parallel", "parallel")))(q, k_cmp, lse, ov_t)


GATE_COL0 = SSD_HEADS


def _combine_fwd(o_cmp, o_sel, o_win, proj_small):
    S = o_cmp.shape[0]
    tr = _pick(S, (256, 128))

    def body(oc_ref, os_ref, ow_ref, g_ref, y_ref):
        gate = _sigmoid(g_ref[...])
        for h in range(N_HEADS):
            hs = slice(h * HD, (h + 1) * HD)
            c = GATE_COL0 + 3 * h
            y = gate[:, c:c + 1] * oc_ref[:, hs] + gate[:, c + 1:c + 2] * os_ref[:, hs] + gate[:, c + 2:c + 3] * ow_ref[:, hs]
            y_ref[:, hs] = y.astype(y_ref.dtype)

    row = pl.BlockSpec((tr, ATT_WIDTH), lambda i: (i, 0))
    return pl.pallas_call(
        body, name="combine_fwd", grid=(S // tr,), in_specs=[row, row, row, pl.BlockSpec((tr, W_SMALL), lambda i: (i, 0))],
        out_specs=row, out_shape=jax.ShapeDtypeStruct((S, ATT_WIDTH), _MXU), compiler_params=_cp(("parallel",)))(
            o_cmp, o_sel, o_win, proj_small)


def _combine_bwd(dmixed, o_cmp, o_sel, o_win, proj_small):
    S = o_cmp.shape[0]
    tr = _pick(S, (256, 128))

    def body(dy_ref, oc_ref, os_ref, ow_ref, g_ref, dc_ref, ds_ref, dw_ref, dg_ref):
        gate = _sigmoid(g_ref[...])
        lane = lax.broadcasted_iota(jnp.int32, (1, W_SMALL), 1)
        dg = jnp.zeros((tr, W_SMALL), F32)
        for h in range(N_HEADS):
            hs = slice(h * HD, (h + 1) * HD)
            dy = dy_ref[:, hs].astype(F32)
            for b, (o_ref, d_ref) in enumerate(((oc_ref, dc_ref), (os_ref, ds_ref), (ow_ref, dw_ref))):
                c = GATE_COL0 + 3 * h + b
                gv = gate[:, c:c + 1]
                d_ref[:, hs] = gv * dy
                dgate = jnp.sum(dy * o_ref[:, hs], axis=-1, keepdims=True) * (gv * (1.0 - gv))
                dg = dg + dgate * (lane == c).astype(F32)
        dg_ref[...] = dg

    row = pl.BlockSpec((tr, ATT_WIDTH), lambda i: (i, 0))
    small = pl.BlockSpec((tr, W_SMALL), lambda i: (i, 0))
    return pl.pallas_call(
        body, name="combine_bwd", grid=(S // tr,),
        in_specs=[pl.BlockSpec((tr, ATT_WIDTH), lambda i: (i, 1)), row, row, row, small], out_specs=[row, row, row, small],
        out_shape=[jax.ShapeDtypeStruct((S, ATT_WIDTH), F32)] * 3 + [jax.ShapeDtypeStruct((S, W_SMALL), F32)],
        compiler_params=_cp(("parallel",)))(dmixed, o_cmp, o_sel, o_win, proj_small)


def _to_rows16(x):
    S = x.shape[0]
    return x.reshape(S // 16, 16, N_KV, HD).transpose(2, 0, 1, 3).reshape(N_KV, S // 16, 16 * HD)


def _from_rows16(r):
    NC = r.shape[1]
    return r.reshape(N_KV, NC, 16, HD).transpose(1, 2, 0, 3).reshape(NC * 16, N_KV * HD)


DT_COL0 = SSD_WIDTH + CONV_CH
GATE_IN_COL0 = D_IN - 3 * N_HEADS


SHARD_IN = D_IN // N_DEV


def _orig_cols(ref, c0, width):
    pieces, c = [], c0
    while c < c0 + width:
        d, off = divmod(c, SHARD_IN)
        w = min(SHARD_IN - off, c0 + width - c)
        pieces.append(ref[d, :, off:off + w])
        c += w
    return pieces[0] if len(pieces) == 1 else jnp.concatenate(pieces, axis=1)


def _w_in_from_slabs(slabs):
    D = slabs.shape[1]
    tr = _pick(D, (256, 128))

    def body(s_ref, main_ref, small_ref):
        for t in range(W_MAIN // LANE):
            c = t * LANE
            main_ref[:, c:c + LANE] = _orig_cols(s_ref, c if c < DT_COL0 else c + SSD_HEADS, LANE)
        small_ref[...] = jnp.concatenate(
            [_orig_cols(s_ref, DT_COL0, SSD_HEADS), _orig_cols(s_ref, GATE_IN_COL0, 3 * N_HEADS),
             jnp.zeros((tr, W_SMALL - SSD_HEADS - 3 * N_HEADS), small_ref.dtype)], axis=1)

    return pl.pallas_call(
        body, name="w_in_layout", grid=(D // tr,), in_specs=[pl.BlockSpec((N_DEV, tr, SHARD_IN), lambda i: (0, i, 0))],
        out_specs=[pl.BlockSpec((tr, W_MAIN), lambda i: (i, 0)), pl.BlockSpec((tr, W_SMALL), lambda i: (i, 0))],
        out_shape=[jax.ShapeDtypeStruct((D, W_MAIN), slabs.dtype), jax.ShapeDtypeStruct((D, W_SMALL), slabs.dtype)],
        compiler_params=_cp(("parallel",)))(slabs)


def _w_in_to_slabs(main, small):
    D = main.shape[0]
    tr = _pick(D, (256, 128))
    ranges = [(0, DT_COL0, 0, 0), (DT_COL0, DT_COL0 + SSD_HEADS, 1, 0), (DT_COL0 + SSD_HEADS, GATE_IN_COL0, 0, DT_COL0),
              (GATE_IN_COL0, D_IN, 1, SSD_HEADS)]

    def body(main_ref, small_ref, o_ref):
        srcs = (main_ref, small_ref)
        for d in range(N_DEV):
            lo, hi = d * SHARD_IN, (d + 1) * SHARD_IN
            pieces = []
            for start, stop, which, s0 in ranges:
                a, b = max(lo, start), min(hi, stop)
                if a < b:
                    pieces.append(srcs[which][:, s0 + a - start:s0 + b - start].astype(o_ref.dtype))
            o_ref[d] = pieces[0] if len(pieces) == 1 else jnp.concatenate(pieces, axis=1)

    return pl.pallas_call(
        body, name="w_in_grad_layout", grid=(D // tr,),
        in_specs=[pl.BlockSpec((tr, W_MAIN), lambda i: (i, 0)), pl.BlockSpec((tr, W_SMALL), lambda i: (i, 0))],
        out_specs=pl.BlockSpec((N_DEV, tr, SHARD_IN), lambda i: (0, i, 0)),
        out_shape=jax.ShapeDtypeStruct((N_DEV, D, SHARD_IN), main.dtype), compiler_params=_cp(("parallel",)))(main, small)


QB, KCB, VCB, KSB, VSB, KWB, VWB = 10, 14, 15, 16, 17, 18, 19


def _col256(a, b):
    return a[:, b * 256:(b + 1) * 256]


_EARLY = ["w_in", "cmp_w1_k", "cmp_w1_v"]
_LATE = ["w_out", "w_gate", "w_up", "w_down"]
_FFN = ["w_down", "w_gate", "w_up"]
_MID = ["w_out"]
_LAST = ["cmp_w1_k", "cmp_w1_v", "w_in"]


def _local_step(x, tgt, p, early_weights=None, late_weights=None, grads_ready=None):
    S = x.shape[0]
    cos, sin = _rope_tables(S)

    u, rs1 = _rms_fwd(x, p["attn_norm_w"], "attn_norm")
    if early_weights is not None:
        p = {**p, **early_weights((u, cos, sin))}
    proj = _mm(u, p["w_main"], "nn", F32, "in_proj")
    proj_small = _mm(u, p["w_small"], "nn", F32, "in_proj_small")
    xa = _conv_fwd(proj, p["conv_w"], p["conv_b"])
    y_ssd, y_pre, rs_ssd, hs = _ssd_fwd(proj, proj_small, xa, p["dt_bias"], p["a_log"], p["d_skip"], p["ssd_norm_w"])

    q_rot = _rope([proj], QB, ATT_WIDTH, cos, sin, 1.0, _MXU, "rope_q")
    kv = _kv_prep(proj, cos, sin, _attn_cfg(S, S, "sel")[1])
    rk, rv = _to_rows16(_col256(proj, KCB)), _to_rows16(_col256(proj, VCB))
    k_cmp, hid_k = _compress_fwd(rk, p["cmp_pe_k"], p["cmp_w1_k"], p["cmp_w2_k"])
    v_cmp, hid_v = _compress_fwd(rv, p["cmp_pe_v"], p["cmp_w1_v"], p["cmp_w2_v"])
    n_cmp = k_cmp.shape[1]

    o_cmp, lse_cmp = _attn_fwd(proj, QB, k_cmp, _blocked_t(v_cmp, n_cmp), "cmp", None, "attn_cmp_fwd")
    sel = _select(proj, QB, k_cmp, lse_cmp)
    o_sel, lse_sel = _attn_fwd(q_rot, 0, kv["ks"], kv["vs_t"], "sel", sel, "attn_sel_fwd")
    o_win, lse_win = _attn_fwd(q_rot, 0, kv["kw"], kv["vw_t"], "win", None, "attn_win_fwd")
    y_att = _combine_fwd(o_cmp, o_sel, o_win, proj_small)

    if late_weights is not None:
        p = {**p, **late_weights(y_att)}
    mixed = jnp.concatenate([y_ssd, y_att], axis=1)
    h1 = _mm(mixed, p["w_out"], "nn", F32, "out_proj", res=x)
    v, rs_ffn = _rms_fwd(h1, p["ffn_norm_w"], "ffn_norm")
    gt, up, act = _ffn_up(v, p["w_gate"], p["w_up"])
    h2 = _mm(act, p["w_down"], "nn", F32, "ffn_down", res=h1)
    loss, dh2, dh2b, d_final_w = _final_loss(h2, p["final_norm_w"], tgt)

    def ready(names):
        return None if grads_ready is None else grads_ready(names, g)

    g = {"final_norm_w": d_final_w}
    g["w_down"] = _mm(act, dh2b, "tn", _MXU, "dw_down")
    dgt, dup = _ffn_dact(dh2b, p["w_down"], gt, up)
    g["w_gate"] = _mm(v, dgt, "tn", _MXU, "dw_gate")
    g["w_up"] = _mm(v, dup, "tn", _MXU, "dw_up")
    dv = _mm(dgt, p["w_gate"], "nt", F32, "dv_gate", after=ready(_FFN))
    dv = _mm(dup, p["w_up"], "nt", F32, "dv_up", res=dv)
    dh1, dh1b, g["ffn_norm_w"] = _rms_bwd(dv, h1, rs_ffn, p["ffn_norm_w"], dh2, "ffn_norm_bwd")
    g["w_out"] = _mm(mixed, dh1b, "tn", _MXU, "dw_out")
    dmixed = _mm(dh1b, p["w_out"], "nt", F32, "dmixed", after=ready(_MID))

    dz, dxa, ddtr, g["dt_bias"], g["a_log"], g["d_skip"], g["ssd_norm_w"] = _ssd_bwd(
        dmixed, proj, proj_small, xa, y_pre, rs_ssd, hs, p["dt_bias"], p["a_log"], p["d_skip"], p["ssd_norm_w"])
    dxbc, g["conv_w"], g["conv_b"] = _conv_bwd(proj, p["conv_w"], p["conv_b"], dxa)

    do_cmp, do_sel, do_win, dgate = _combine_bwd(dmixed, o_cmp, o_sel, o_win, proj_small)
    dq_cmp, dk_cmp, dv_cmp = _attn_bwd(proj, QB, k_cmp, _blocked_t(k_cmp, n_cmp), v_cmp, o_cmp, lse_cmp, do_cmp, "cmp", None,
                                       "attn_cmp_bwd")
    dq_sel, dks, dvs = _attn_bwd(q_rot, 0, kv["ks"], kv["ks_t"], kv["vs"], o_sel, lse_sel, do_sel, "sel", sel, "attn_sel_bwd")
    dq_win, dkw, dvw = _attn_bwd(q_rot, 0, kv["kw"], kv["kw_t"], kv["vw"], o_win, lse_win, do_win, "win", None, "attn_win_bwd")
    drk, g["cmp_w1_k"], g["cmp_w2_k"], g["cmp_pe_k"] = _compress_bwd(rk, p["cmp_pe_k"], p["cmp_w1_k"], p["cmp_w2_k"], hid_k, dk_cmp)
    drv, g["cmp_w1_v"], g["cmp_w2_v"], g["cmp_pe_v"] = _compress_bwd(rv, p["cmp_pe_v"], p["cmp_w1_v"], p["cmp_w2_v"], hid_v, dv_cmp)
    dq = _rope([dq_sel, dq_win], 0, ATT_WIDTH, cos, sin, -1.0, _MXU, "rope_dq", extra=(dq_cmp, 0))
    dkv = _dkv_post(dks, dvs, dkw, dvw, cos, sin)
    dproj = jnp.concatenate([dz, dxbc, dq] + [t.astype(_MXU) for t in (_from_rows16(drk), _from_rows16(drv))] + [dkv], axis=1)
    dsmall = jnp.concatenate([ddtr, dgate[:, GATE_COL0:GATE_COL0 + 3 * N_HEADS],
                              jnp.zeros((S, W_SMALL - SSD_HEADS - 3 * N_HEADS), F32)], axis=1).astype(_MXU)
    g["w_main"] = _mm(u, dproj, "tn", _MXU, "dw_in")
    g["w_small"] = _mm(u, dsmall, "tn", F32, "dw_in_small")
    du = _mm(dproj, p["w_main"], "nt", F32, "du_main", after=ready(_LAST))
    du = _mm(dsmall, p["w_small"], "nt", F32, "du_small", res=du)
    grad_x, _, g["attn_norm_w"] = _rms_bwd(du, x, rs1, p["attn_norm_w"], dh1, "attn_norm_bwd")
    return loss, grad_x, g


MESH_ID = pl.DeviceIdType.MESH


def _my_coords():
    return lax.axis_index("x"), lax.axis_index("y"), lax.axis_index("c")


def _flat_id(px, py, pc):
    return 4 * px + 2 * py + pc


def _peer(k):
    mx, my, mc = _my_coords()
    return (1 - mx if k & 4 else mx, 1 - my if k & 2 else my, 1 - mc if k & 1 else mc)


def _exchange(arrs, scatter, name, after=()):
    n, na = len(arrs), len(after)
    scatter = [scatter] * n if isinstance(scatter, bool) else list(scatter)

    def body(*refs):
        ins, outs = refs[:n], refs[n + na:2 * n + na]
        send_sems, recv_sems, local_sems = refs[2 * n + na:]
        me = _flat_id(*_my_coords())
        copies = []
        for i in range(n):
            src_me = ins[i].at[me] if scatter[i] else ins[i]
            local = pltpu.make_async_copy(src_me, outs[i].at[me], local_sems.at[i])
            local.start()
            copies.append(local)
        for k in range(1, N_DEV):
            peer = _peer(k)
            for i in range(n):
                src = ins[i].at[_flat_id(*peer)] if scatter[i] else ins[i]
                cp = pltpu.make_async_remote_copy(src_ref=src, dst_ref=outs[i].at[me], send_sem=send_sems.at[i * 7 + k - 1],
                                                  recv_sem=recv_sems.at[i * 7 + k - 1], device_id=peer, device_id_type=MESH_ID)
                cp.start()
                copies.append(cp)
        for cp in copies:
            cp.wait()

    any_spec = pl.BlockSpec(memory_space=pl.ANY)
    out_shape = [jax.ShapeDtypeStruct(a.shape if sc else (N_DEV,) + a.shape, a.dtype) for a, sc in zip(arrs, scatter)]
    return pl.pallas_call(
        body, name=name, in_specs=[any_spec] * (n + na), out_specs=[any_spec] * n, out_shape=out_shape,
        scratch_shapes=[pltpu.SemaphoreType.DMA((n * 7,)), pltpu.SemaphoreType.DMA((n * 7,)), pltpu.SemaphoreType.DMA((n,))],
        compiler_params=pltpu.CompilerParams(has_side_effects=True))(*arrs, *after)


_HBM = pl.BlockSpec(memory_space=pltpu.HBM)
_SEM = pl.BlockSpec(memory_space=pltpu.SEMAPHORE)
_EFFECT = pltpu.SideEffectType.DATAFLOW_SIDE_EFFECTING


def _split_copies(ins, lands, send_sems, recv_sems, scatter):
    me = _flat_id(*_my_coords())
    out = []
    for k in range(1, N_DEV):
        peer = _peer(k)
        for i in range(len(ins)):
            src = ins[i].at[_flat_id(*peer)] if scatter else ins[i]
            out.append(pltpu.make_async_remote_copy(src_ref=src, dst_ref=lands[i].at[me], send_sem=send_sems.at[i * 7 + k - 1],
                                                    recv_sem=recv_sems.at[i * 7 + k - 1], device_id=peer, device_id_type=MESH_ID))
    return out


def _split_start(arrs, scatter, name):
    n = len(arrs)

    def body(*refs):
        for cp in _split_copies(refs[:n], refs[n:2 * n], refs[2 * n], refs[2 * n + 1], scatter):
            cp.start()
        refs[-1][...] = jnp.zeros_like(refs[-1])

    land_shapes = [a.shape if scatter else (N_DEV,) + a.shape for a in arrs]
    out_shape = ((pltpu.SemaphoreType.DMA((n * 7,)), pltpu.SemaphoreType.DMA((n * 7,)))
                 + tuple(pltpu.HBM(a.shape, a.dtype) for a in arrs) + tuple(pltpu.HBM(s, a.dtype) for s, a in zip(land_shapes, arrs))
                 + (jax.ShapeDtypeStruct((8, 128), F32),))
    operands = ([pltpu.with_memory_space_constraint(a, pltpu.HBM) for a in arrs]
                + [pltpu.with_memory_space_constraint(lax.empty(s, a.dtype), pltpu.HBM) for s, a in zip(land_shapes, arrs)])
    res = pl.pallas_call(
        body, name=name, out_shape=out_shape, in_specs=[_HBM] * (2 * n),
        out_specs=(_SEM, _SEM) + (_HBM,) * (2 * n) + (pl.BlockSpec(memory_space=pltpu.VMEM),),
        input_output_aliases={i: 2 + i for i in range(2 * n)},
        compiler_params=pltpu.CompilerParams(has_side_effects=_EFFECT))(*operands)
    return dict(send=res[0], recv=res[1], ins=list(res[2:2 + n]), lands=list(res[2 + n:2 + 2 * n]), token=res[-1])


def _split_wait(st, scatter, after, name):
    n = len(st["ins"])

    def body(*refs):
        for cp in _split_copies(refs[:n], refs[n:2 * n], refs[2 * n], refs[2 * n + 1], scatter):
            cp.wait_send()
            cp.wait_recv()

    arrs = st["ins"] + st["lands"]
    res = pl.pallas_call(
        body, name=name, out_shape=tuple(pltpu.HBM(a.shape, a.dtype) for a in arrs),
        in_specs=[_HBM] * (2 * n) + [_SEM, _SEM] + [pl.BlockSpec(memory_space=pl.ANY)] * len(after), out_specs=(_HBM,) * (2 * n),
        input_output_aliases={i: i for i in range(2 * n)},
        compiler_params=pltpu.CompilerParams(has_side_effects=_EFFECT))(*arrs, st["send"], st["recv"], *after)
    me = _flat_id(*_my_coords())
    out = []
    for src, land in zip(res[:n], res[n:]):
        own = lax.dynamic_index_in_dim(src, me, 0, keepdims=True) if scatter else src[None]
        out.append(lax.dynamic_update_slice_in_dim(land, own, me, 0))
    return out


def _adam_step(p_ref, w_ref, m_ref, v_ref, g_ref, d_ref, nm_ref, nv_ref):
    g = p_ref[0].astype(F32)
    for j in range(1, p_ref.shape[0]):
        g = g + p_ref[j].astype(F32)
    g_ref[...] = g
    nm = ADAM_B1 * m_ref[...] + (1.0 - ADAM_B1) * g
    nv = ADAM_B2 * v_ref[...] + (1.0 - ADAM_B2) * (g * g)
    nm_ref[...] = nm
    nv_ref[...] = nv
    m_hat = nm / (1.0 - ADAM_B1 ** ADAM_STEP)
    v_hat = nv / (1.0 - ADAM_B2 ** ADAM_STEP)
    d_ref[...] = -ADAM_LR * (m_hat / (jnp.sqrt(v_hat) + ADAM_EPS) + ADAM_WD * w_ref[...])


def _adam_sum(parts, w, m, v, name):
    P, R, C = parts.shape
    tr = _pick(R, (256, 128, 64, 32, 8)) if C <= 1024 else _pick(R, (128, 64, 32, 8))
    blk = pl.BlockSpec((tr, C), lambda i: (i, 0))
    return pl.pallas_call(
        functools.partial(_adam_step), name=name, grid=(R // tr,),
        in_specs=[pl.BlockSpec((P, tr, C), lambda i: (0, i, 0)), blk, blk, blk],
        out_specs=[blk] * 4, out_shape=[jax.ShapeDtypeStruct((R, C), F32)] * 4, compiler_params=_cp(("parallel",)))(parts, w, m, v)


def _adam_small(loss_parts, parts, ws, ms, vs):
    n = len(parts)

    def body(*refs):
        loss_ref, ins, outs, total_ref = refs[0], refs[1:4 * n + 1], refs[4 * n + 1:-1], refs[-1]
        for i in range(n):
            _adam_step(ins[i], ins[n + i], ins[2 * n + i], ins[3 * n + i], *outs[4 * i:4 * i + 4])
        total = loss_ref[0]
        for d in range(1, N_DEV):
            total = total + loss_ref[d]
        total_ref[...] = total

    out_shape = [jax.ShapeDtypeStruct(w.shape, F32) for w in ws for _ in range(4)] + [jax.ShapeDtypeStruct(loss_parts.shape[1:], F32)]
    res = pl.pallas_call(body, name="adam_small", out_shape=out_shape)(loss_parts, *parts, *ws, *ms, *vs)
    return res[-1], [tuple(res[4 * i:4 * i + 4]) for i in range(n)]


_WEIGHTS = ["attn_norm_w", "w_in", "conv_w", "conv_b", "dt_bias", "a_log", "d_skip", "ssd_norm_w", "cmp_w1_k", "cmp_w2_k",
            "cmp_w1_v", "cmp_w2_v", "cmp_pe_k", "cmp_pe_v", "w_out", "ffn_norm_w", "w_gate", "w_up", "w_down", "final_norm_w"]
_BIG = ["w_in", "w_gate", "w_up", "w_down", "w_out", "cmp_w1_k", "cmp_w1_v"]
_COL_SHARDED = ("w_in", "w_gate", "w_up")
_REPLICATED = ["attn_norm_w", "conv_b", "dt_bias", "a_log", "d_skip", "ssd_norm_w", "cmp_pe_k", "cmp_pe_v", "ffn_norm_w",
               "final_norm_w"]
_SMALL_SHARDED = ["conv_w", "cmp_w2_k", "cmp_w2_v"]


def _cols_to_slabs(g):
    R = g.shape[0]
    return g.reshape(R, N_DEV, -1).transpose(1, 0, 2)


def _slabs_to_cols(s):
    return s.transpose(1, 0, 2).reshape(s.shape[1], -1)


def kernel(x, attn_norm_w, w_in, conv_w, conv_b, dt_bias, a_log, d_skip, ssd_norm_w, cmp_w1_k, cmp_w2_k, cmp_w1_v, cmp_w2_v, cmp_pe_k, cmp_pe_v, w_out, ffn_norm_w, w_gate, w_up, w_down, final_norm_w, loss_target, m_attn_norm_w, m_w_in, m_conv_w, m_conv_b, m_dt_bias, m_a_log, m_d_skip, m_ssd_norm_w, m_cmp_w1_k, m_cmp_w2_k, m_cmp_w1_v, m_cmp_w2_v, m_cmp_pe_k, m_cmp_pe_v, m_w_out, m_ffn_norm_w, m_w_gate, m_w_up, m_w_down, m_final_norm_w, v_attn_norm_w, v_w_in, v_conv_w, v_conv_b, v_dt_bias, v_a_log, v_d_skip, v_ssd_norm_w, v_cmp_w1_k, v_cmp_w2_k, v_cmp_w1_v, v_cmp_w2_v, v_cmp_pe_k, v_cmp_pe_v, v_w_out, v_ffn_norm_w, v_w_gate, v_w_up, v_w_down, v_final_norm_w):
    a = dict(locals())

    shard = {n: a[n][0].astype(_MXU) for n in _BIG}
    early_small = [cmp_w2_k[0], cmp_w2_v[0], conv_w[0]]
    st_early = _split_start([shard[n] for n in _EARLY] + early_small, False, "gather_early_start")
    zero = st_early["token"][0, 0].astype(_MXU)
    st_late = _split_start([shard[_LATE[0]] + zero] + [shard[n] for n in _LATE[1:]], False, "gather_late_start")

    def assemble(n, t):
        return _slabs_to_cols(t) if n in _COL_SHARDED else t.reshape(-1, t.shape[-1])

    p = dict(attn_norm_w=attn_norm_w, conv_b=conv_b, dt_bias=dt_bias, a_log=a_log, d_skip=d_skip, ssd_norm_w=ssd_norm_w,
             cmp_pe_k=cmp_pe_k.reshape(1, -1), cmp_pe_v=cmp_pe_v.reshape(1, -1), ffn_norm_w=ffn_norm_w,
             final_norm_w=final_norm_w.reshape(1, -1))

    def early_weights(after):
        got = _split_wait(st_early, False, tuple(after) + (st_late["token"],), "gather_early_wait")
        w_main, w_small = _w_in_from_slabs(got[0])
        return dict(w_main=w_main, w_small=w_small, cmp_w1_k=assemble("cmp_w1_k", got[1]), cmp_w1_v=assemble("cmp_w1_v", got[2]),
                    cmp_w2_k=assemble("cmp_w2_k", got[3]).astype(_MXU), cmp_w2_v=assemble("cmp_w2_v", got[4]).astype(_MXU),
                    conv_w=_slabs_to_cols(got[5]))

    def late_weights(after):
        got_late = _split_wait(st_late, False, (after,), "gather_late_wait")
        return {n: assemble(n, t) for n, t in zip(_LATE, got_late)}

    def slabs_of(g, n):
        if n == "w_in":
            return _w_in_to_slabs(g["w_main"], g["w_small"])
        return (_cols_to_slabs(g[n]) if n in _COL_SHARDED else g[n].reshape(N_DEV, -1, g[n].shape[-1])).astype(_MXU)

    started = []

    def grads_ready(names, g):
        started.append((names, _split_start([slabs_of(g, n) for n in names], True, "scatter_grads_start_%d" % len(started))))
        return started[-1][1]["token"]

    loss_part, grad_x, g = _local_step(x[0], loss_target[0], p, early_weights, late_weights, grads_ready)

    out, after = {}, (started[-1][1]["token"],)
    for i, (names, st) in enumerate(started):
        if i == len(started) - 1:
            after = after + (grad_x,)
        received = _split_wait(st, True, after, "scatter_grads_wait_%d" % i)
        for n, parts in zip(names, received):
            out[n] = _adam_sum(parts, a[n][0], a["m_" + n][0], a["v_" + n][0], "adam_" + n)
        after = (out[names[-1]][0],)

    small_names = _REPLICATED + _SMALL_SHARDED
    partials = [g[n] for n in _REPLICATED] + [_cols_to_slabs(g["conv_w"])] + [
        g[n].reshape(N_DEV, -1, g[n].shape[-1]) for n in ("cmp_w2_k", "cmp_w2_v")]
    gathered = _exchange([loss_part] + partials, [False] * (1 + len(_REPLICATED)) + [True] * len(_SMALL_SHARDED),
                         "exchange_small_grads", after=(received[0],))
    shapes2d = [t.shape[1:] for t in gathered[1:]]
    loss, res_small = _adam_small(gathered[0], gathered[1:],
                                  *[[a[pre + n].reshape(s) for n, s in zip(small_names, shapes2d)] for pre in ("", "m_", "v_")])
    for n, r in zip(small_names, res_small):
        out[n] = r

    outs = [loss[0, 0], grad_x[None]]
    for j in range(4):
        for n in _WEIGHTS:
            outs.append(out[n][j].reshape(a[n].shape))
    return tuple(outs)
```

```python
import functools

import numpy as np
import jax
import jax.numpy as jnp
from jax import lax
from jax.experimental import pallas as pl
from jax.experimental.pallas import tpu as pltpu

F32 = jnp.float32
_MXU = jnp.bfloat16
_HI = lax.Precision.HIGHEST

N_DEV = 8
D_MODEL = 2048
SSD_WIDTH = 1024
ATT_WIDTH = 1024
SSD_HEADS = 16
SSD_P = 64
SSD_N = 128
SSD_L = 128
SSD_G = 2
CONV_CH = 1536
CONV_K = 4
HD = 64
N_HEADS = 16
N_KV = 4
GRP = 4
CMP_HID = 256
SEL_BLOCK = 64
N_SELECT = 16
WINDOW = 512
ROPE_DIM = 16
ROPE_THETA = 500000.0
D_FF = 5632
EPS = 1e-6
NEG = -1e30
FORCE = 1e4
SCALE = HD ** -0.5
D_IN = 5184
W_MAIN = 5120
W_SMALL = 128
VMEM_LIMIT = 52 * 1024 * 1024

ADAM_LR, ADAM_B1, ADAM_B2, ADAM_EPS, ADAM_WD, ADAM_STEP = 0.001, 0.9, 0.999, 1e-08, 0.01, 10


def _pick(n, cands):
    for c in cands:
        if n % c == 0:
            return c
    return n


def _cp(sem=None):
    return pltpu.CompilerParams(dimension_semantics=sem, vmem_limit_bytes=VMEM_LIMIT)


def _sigmoid(x):
    return 1.0 / (1.0 + jnp.exp(-x))


def _dot(a, b, dims, hi=False):
    dn = {"nn": (((1,), (0,)), ((), ())), "nt": (((1,), (1,)), ((), ())), "tn": (((0,), (0,)), ((), ()))}[dims]
    if hi:
        return lax.dot_general(a.astype(F32), b.astype(F32), dn, precision=_HI, preferred_element_type=F32)
    return lax.dot_general(a.astype(_MXU), b.astype(_MXU), dn, preferred_element_type=F32)


LANE = 128
MM_TILE = 1024
MM_K_WHOLE = 2048
MM_K_STEP = 1536
TN_ACC_ELEMS = 3 * 2 ** 20
TN_K_STEP = 512


def _largest_tile(n, cap):
    if n <= cap:
        return n
    best = LANE
    for t in range(LANE, cap + 1, LANE):
        if n % t == 0:
            best = t
    return best


def _mm_tiles(mode, M, N, K):
    if mode == "tn":
        tm = _largest_tile(M, 2 * MM_TILE)
        return tm, _largest_tile(N, TN_ACC_ELEMS // tm), _largest_tile(K, TN_K_STEP)
    tk = K if K <= MM_K_WHOLE else _largest_tile(K, MM_K_STEP)
    return _largest_tile(M, MM_TILE), _largest_tile(N, MM_TILE), tk


def _mm(a, b, mode, out_dtype, name, res=None, after=None):
    if mode == "nn":
        (M, K), N = a.shape, b.shape[1]
    elif mode == "nt":
        (M, K), N = a.shape, b.shape[0]
    else:
        (K, M), N = a.shape, b.shape[1]
    tm, tn, tk = _mm_tiles(mode, M, N, K)
    nk = K // tk
    a_spec = pl.BlockSpec((tk, tm), lambda i, j, k: (k, i)) if mode == "tn" else pl.BlockSpec((tm, tk), lambda i, j, k: (i, k))
    b_spec = pl.BlockSpec((tn, tk), lambda i, j, k: (j, k)) if mode == "nt" else pl.BlockSpec((tk, tn), lambda i, j, k: (k, j))
    o_spec = pl.BlockSpec((tm, tn), lambda i, j, k: (i, j))

    def finish(r, r_ref, o_ref):
        if res is not None:
            r = r + r_ref[...].astype(F32)
        o_ref[...] = r.astype(out_dtype)

    def body_one_step(*refs):
        a_ref, b_ref, o_ref = refs[0], refs[1], refs[-1]
        finish(_dot(a_ref[...], b_ref[...], mode), refs[2], o_ref)

    def body(*refs):
        a_ref, b_ref, o_ref, acc = refs[0], refs[1], refs[-2], refs[-1]
        k = pl.program_id(2)

        @pl.when(k == 0)
        def _():
            acc[...] = jnp.zeros_like(acc)

        acc[...] += _dot(a_ref[...], b_ref[...], mode)

        @pl.when(k == nk - 1)
        def _():
            finish(acc[...], refs[2], o_ref)

    ins, specs = [a, b], [a_spec, b_spec]
    if res is not None:
        ins.append(res)
        specs.append(o_spec)
    if after is not None:
        ins.append(after)
        specs.append(pl.BlockSpec(memory_space=pl.ANY))
    return pl.pallas_call(
        body_one_step if nk == 1 else body, name=name, grid=(M // tm, N // tn, nk), in_specs=specs, out_specs=o_spec,
        out_shape=jax.ShapeDtypeStruct((M, N), out_dtype), scratch_shapes=[] if nk == 1 else [pltpu.VMEM((tm, tn), F32)],
        compiler_params=_cp(("parallel", "parallel", "arbitrary")))(*ins)


def _ffn_up(v, w_gate, w_up):
    S, D = v.shape
    F = w_gate.shape[1]
    tm, tn = _largest_tile(S, MM_TILE), _largest_tile(F, MM_TILE // 2)

    def body(v_ref, wg_ref, wu_ref, gt_ref, up_ref, act_ref):
        vv = v_ref[...]
        g = _dot(vv, wg_ref[...], "nn")
        u = _dot(vv, wu_ref[...], "nn")
        gt_ref[...] = g
        up_ref[...] = u
        act_ref[...] = (g * _sigmoid(g) * u).astype(act_ref.dtype)

    o_spec = pl.BlockSpec((tm, tn), lambda i, j: (i, j))
    w_spec = pl.BlockSpec((D, tn), lambda i, j: (0, j))
    return pl.pallas_call(
        body, name="ffn_up", grid=(S // tm, F // tn),
        in_specs=[pl.BlockSpec((tm, D), lambda i, j: (i, 0)), w_spec, w_spec], out_specs=[o_spec, o_spec, o_spec],
        out_shape=[jax.ShapeDtypeStruct((S, F), F32), jax.ShapeDtypeStruct((S, F), F32), jax.ShapeDtypeStruct((S, F), _MXU)],
        compiler_params=_cp(("parallel", "parallel")))(v, w_gate, w_up)


def _ffn_dact(dh2, w_down, gt, up):
    S, D = dh2.shape
    F = w_down.shape[0]
    tm, tn = _largest_tile(S, MM_TILE), _largest_tile(F, MM_TILE // 2)

    def body(d_ref, w_ref, gt_ref, up_ref, dg_ref, du_ref):
        da, g, u = _dot(d_ref[...], w_ref[...], "nt"), gt_ref[...], up_ref[...]
        s = _sigmoid(g)
        dg_ref[...] = (da * u * (s * (1.0 + g * (1.0 - s)))).astype(dg_ref.dtype)
        du_ref[...] = (da * (g * s)).astype(du_ref.dtype)

    o_spec = pl.BlockSpec((tm, tn), lambda i, j: (i, j))
    return pl.pallas_call(
        body, name="ffn_dact", grid=(S // tm, F // tn),
        in_specs=[pl.BlockSpec((tm, D), lambda i, j: (i, 0)), pl.BlockSpec((tn, D), lambda i, j: (j, 0)), o_spec, o_spec],
        out_specs=[o_spec, o_spec],
        out_shape=[jax.ShapeDtypeStruct((S, F), _MXU), jax.ShapeDtypeStruct((S, F), _MXU)],
        compiler_params=_cp(("parallel", "parallel")))(dh2, w_down, gt, up)


def _rms_fwd(x, w, name):
    S, D = x.shape
    tr = _pick(S, (256, 128))

    def body(x_ref, w_ref, xn_ref, rs_ref):
        xv = x_ref[...]
        rs = lax.rsqrt(jnp.mean(xv * xv, axis=-1, keepdims=True) + EPS)
        xn_ref[...] = ((xv * rs) * w_ref[...]).astype(xn_ref.dtype)
        rs_ref[...] = rs

    return pl.pallas_call(
        body, name=name, grid=(S // tr,),
        in_specs=[pl.BlockSpec((tr, D), lambda i: (i, 0)), pl.BlockSpec((1, D), lambda i: (0, 0))],
        out_specs=[pl.BlockSpec((tr, D), lambda i: (i, 0)), pl.BlockSpec((tr, 1), lambda i: (i, 0))],
        out_shape=[jax.ShapeDtypeStruct((S, D), _MXU), jax.ShapeDtypeStruct((S, 1), F32)],
        compiler_params=_cp(("parallel",)))(x, w)


def _rms_bwd(dyn, x, rs, w, res, name):
    S, D = x.shape
    tr = _pick(S, (256, 128))

    def body(dy_ref, x_ref, rs_ref, w_ref, res_ref, dx_ref, dxb_ref, dw_ref):
        @pl.when(pl.program_id(0) == 0)
        def _():
            dw_ref[...] = jnp.zeros_like(dw_ref)

        dy, r = dy_ref[...].astype(F32), rs_ref[...]
        xhat = x_ref[...] * r
        dw_ref[...] += jnp.sum(dy * xhat, axis=0, keepdims=True)
        dxhat = dy * w_ref[...]
        dx = res_ref[...] + r * (dxhat - xhat * jnp.mean(dxhat * xhat, axis=-1, keepdims=True))
        dx_ref[...] = dx
        dxb_ref[...] = dx.astype(dxb_ref.dtype)

    row = pl.BlockSpec((tr, D), lambda i: (i, 0))
    vec = pl.BlockSpec((1, D), lambda i: (0, 0))
    return pl.pallas_call(
        body, name=name, grid=(S // tr,),
        in_specs=[row, row, pl.BlockSpec((tr, 1), lambda i: (i, 0)), vec, row], out_specs=[row, row, vec],
        out_shape=[jax.ShapeDtypeStruct((S, D), F32), jax.ShapeDtypeStruct((S, D), _MXU), jax.ShapeDtypeStruct((1, D), F32)],
        compiler_params=_cp(("arbitrary",)))(dyn, x, rs, w, res)


def _final_loss(h2, w, tgt):
    S, D = h2.shape
    tr = _pick(S, (256, 128))

    def body(h_ref, w_ref, t_ref, loss_ref, dh_ref, dhb_ref, dw_ref):
        @pl.when(pl.program_id(0) == 0)
        def _():
            dw_ref[...] = jnp.zeros_like(dw_ref)
            loss_ref[...] = jnp.zeros_like(loss_ref)

        hv, wv = h_ref[...], w_ref[...]
        rs = lax.rsqrt(jnp.mean(hv * hv, axis=-1, keepdims=True) + EPS)
        xhat = hv * rs
        err = xhat * wv - t_ref[...]
        row = jnp.mean(err * err, axis=-1, keepdims=True)
        loss_ref[...] += jnp.broadcast_to(0.5 * jnp.sum(row, axis=0, keepdims=True), loss_ref.shape)
        dy = err * (1.0 / D)
        dw_ref[...] += jnp.sum(dy * xhat, axis=0, keepdims=True)
        dxhat = dy * wv
        dh = rs * (dxhat - xhat * jnp.mean(dxhat * xhat, axis=-1, keepdims=True))
        dh_ref[...] = dh
        dhb_ref[...] = dh.astype(dhb_ref.dtype)

    row = pl.BlockSpec((tr, D), lambda i: (i, 0))
    vec = pl.BlockSpec((1, D), lambda i: (0, 0))
    return pl.pallas_call(
        body, name="final_loss", grid=(S // tr,), in_specs=[row, vec, row],
        out_specs=[pl.BlockSpec((1, LANE), lambda i: (0, 0)), row, row, vec],
        out_shape=[jax.ShapeDtypeStruct((1, LANE), F32), jax.ShapeDtypeStruct((S, D), F32), jax.ShapeDtypeStruct((S, D), _MXU),
                   jax.ShapeDtypeStruct((1, D), F32)],
        compiler_params=_cp(("arbitrary",)))(h2, w, tgt)


def _shift_rows(x, k, rows):
    if k == 0:
        return x
    S = x.shape[0]
    r = pltpu.roll(x, k % S, axis=0)
    ok = (rows >= k) if k > 0 else (rows < S + k)
    return jnp.where(ok, r, 0.0)


XBC_COL0 = SSD_WIDTH // 128


def _conv_fwd(proj, conv_w, conv_b):
    S = proj.shape[0]
    nct = CONV_CH // 128

    def body(x_ref, w_ref, b_ref, o_ref):
        x = x_ref[...]
        rows = lax.broadcasted_iota(jnp.int32, x.shape, 0)
        c = b_ref[...] + w_ref[3:4, :] * x
        for k in range(1, CONV_K):
            c = c + w_ref[3 - k:4 - k, :] * _shift_rows(x, k, rows)
        o_ref[...] = c * _sigmoid(c)

    return pl.pallas_call(
        body, name="conv_fwd", grid=(nct,),
        in_specs=[pl.BlockSpec((S, 128), lambda j: (0, XBC_COL0 + j)), pl.BlockSpec((CONV_K, 128), lambda j: (0, j)),
                  pl.BlockSpec((1, 128), lambda j: (0, j))],
        out_specs=pl.BlockSpec((S, 128), lambda j: (0, j)),
        out_shape=jax.ShapeDtypeStruct((S, CONV_CH), F32), compiler_params=_cp(("parallel",)))(proj, conv_w, conv_b)


def _conv_bwd(proj, conv_w, conv_b, dxa):
    S = proj.shape[0]
    nct = CONV_CH // 128

    def body(x_ref, w_ref, b_ref, d_ref, dx_ref, dw_ref, db_ref):
        x = x_ref[...]
        rows = lax.broadcasted_iota(jnp.int32, x.shape, 0)
        xs = [_shift_rows(x, k, rows) for k in range(CONV_K)]
        c = b_ref[...] + w_ref[3:4, :] * x
        for k in range(1, CONV_K):
            c = c + w_ref[3 - k:4 - k, :] * xs[k]
        s = _sigmoid(c)
        dc = d_ref[...] * (s * (1.0 + c * (1.0 - s)))
        dx = w_ref[3:4, :] * dc
        for k in range(1, CONV_K):
            dx = dx + w_ref[3 - k:4 - k, :] * _shift_rows(dc, -k, rows)
        dx_ref[...] = dx.astype(dx_ref.dtype)
        for k in range(CONV_K):
            dw_ref[3 - k:4 - k, :] = jnp.sum(dc * xs[k], axis=0, keepdims=True)
        db_ref[...] = jnp.sum(dc, axis=0, keepdims=True)

    col = pl.BlockSpec((S, 128), lambda j: (0, j))
    return pl.pallas_call(
        body, name="conv_bwd", grid=(nct,),
        in_specs=[pl.BlockSpec((S, 128), lambda j: (0, XBC_COL0 + j)), pl.BlockSpec((CONV_K, 128), lambda j: (0, j)),
                  pl.BlockSpec((1, 128), lambda j: (0, j)), col],
        out_specs=[col, pl.BlockSpec((CONV_K, 128), lambda j: (0, j)), pl.BlockSpec((1, 128), lambda j: (0, j))],
        out_shape=[jax.ShapeDtypeStruct((S, CONV_CH), _MXU), jax.ShapeDtypeStruct((CONV_K, CONV_CH), F32),
                   jax.ShapeDtypeStruct((1, CONV_CH), F32)],
        compiler_params=_cp(("parallel",)))(proj, conv_w, conv_b, dxa)


def _ssd_consts():
    L = SSD_L
    r = lax.broadcasted_iota(jnp.int32, (L, L), 0)
    c = lax.broadcasted_iota(jnp.int32, (L, L), 1)
    causal = r >= c
    upper = (r <= c).astype(F32)
    hr = lax.broadcasted_iota(jnp.int32, (SSD_HEADS, SSD_WIDTH), 0)
    hc = lax.broadcasted_iota(jnp.int32, (SSD_HEADS, SSD_WIDTH), 1)
    expand = (lax.shift_right_logical(hc, 6) == hr).astype(F32)
    return causal, causal.astype(F32), upper, expand


def _softplus(x):
    return jnp.maximum(x, 0.0) + jnp.log(1.0 + jnp.exp(-jnp.abs(x)))


def _ssd_scalars(dtr, dt_bias, a_log, tri, upper, expand):
    dt = _softplus(dtr + dt_bias)
    A = -jnp.exp(a_log)
    adt = dt * A
    acum = _dot(tri, adt, "nn", hi=True)
    acum_t = _dot(adt, upper, "tn", hi=True)
    alast = acum[SSD_L - 1:SSD_L, :]
    e = jnp.exp(acum)
    wdec = jnp.exp(alast - acum)
    gam = jnp.exp(alast)
    ex = lambda t: _dot(t, expand, "nn", hi=True)
    gam8 = jnp.broadcast_to(gam, (8, SSD_HEADS))
    return dt, A, acum, acum_t, e, wdec, gam, ex(dt), ex(e), ex(wdec), ex(gam8)[0:1, :]


def _ssd_fwd(proj, proj_small, xa, dt_bias, a_log, d_skip, norm_w):
    S = proj.shape[0]
    L, N, W = SSD_L, SSD_N, SSD_WIDTH
    nc = S // L

    def body(z_ref, xa_ref, dtr_ref, dtb_ref, al_ref, dsk_ref, nw_ref, yo_ref, y_ref, rs_ref, hs_ref, h_scr, y_scr):
        @pl.when(pl.program_id(0) == 0)
        def _():
            h_scr[...] = jnp.zeros_like(h_scr)

        causal, tri, upper, expand = _ssd_consts()
        dt, A, acum, acum_t, e, wdec, gam, dtE, eE, wE, gamE = _ssd_scalars(dtr_ref[:, 0:SSD_HEADS], dtb_ref[...], al_ref[...], tri, upper, expand)
        xs = xa_ref[:, 0:W]
        X = xs * dtE
        XW = X * wE
        hs_ref[0] = h_scr[...]
        for g in range(SSD_G):
            gs = slice(g * 512, (g + 1) * 512)
            Bg = xa_ref[:, W + g * N:W + (g + 1) * N]
            Cg = xa_ref[:, W + SSD_G * N + g * N:W + SSD_G * N + (g + 1) * N]
            Hg = h_scr[:, gs]
            CB = _dot(Cg, Bg, "nt")
            yoff = _dot(Cg, Hg, "nn") * eE[:, gs]
            st = _dot(Bg, XW[:, gs], "tn")
            for j in range(8):
                h = g * 8 + j
                hsl = slice(h * SSD_P, (h + 1) * SSD_P)
                lam = jnp.exp(jnp.where(causal, acum[:, h:h + 1] - acum_t[h:h + 1, :], -jnp.inf))
                y_scr[:, hsl] = _dot(CB * lam, X[:, hsl], "nn") + yoff[:, j * SSD_P:(j + 1) * SSD_P]
            h_scr[:, gs] = gamE[:, gs] * Hg + st
        dskE = _dot(jnp.broadcast_to(dsk_ref[...], (8, SSD_HEADS)), expand, "nn", hi=True)[0:1, :]
        y = y_scr[...] + dskE * xs
        y_ref[...] = y
        zv = z_ref[...]
        yg = y * (zv * _sigmoid(zv))
        rs = lax.rsqrt(jnp.mean(yg * yg, axis=-1, keepdims=True) + EPS)
        rs_ref[...] = rs
        yo_ref[...] = ((yg * rs) * nw_ref[...]).astype(yo_ref.dtype)

    p16 = pl.BlockSpec((1, SSD_HEADS), lambda c: (0, 0))
    return pl.pallas_call(
        body, name="ssd_fwd", grid=(nc,),
        in_specs=[pl.BlockSpec((L, W), lambda c: (c, 0)), pl.BlockSpec((L, CONV_CH), lambda c: (c, 0)),
                  pl.BlockSpec((L, W_SMALL), lambda c: (c, 0)), p16, p16, p16, pl.BlockSpec((1, W), lambda c: (0, 0))],
        out_specs=[pl.BlockSpec((L, W), lambda c: (c, 0)), pl.BlockSpec((L, W), lambda c: (c, 0)),
                   pl.BlockSpec((L, 1), lambda c: (c, 0)), pl.BlockSpec((1, N, W), lambda c: (c, 0, 0))],
        out_shape=[jax.ShapeDtypeStruct((S, W), _MXU), jax.ShapeDtypeStruct((S, W), F32), jax.ShapeDtypeStruct((S, 1), F32),
                   jax.ShapeDtypeStruct((nc, N, W), F32)],
        scratch_shapes=[pltpu.VMEM((N, W), F32), pltpu.VMEM((L, W), F32)],
        compiler_params=_cp(("arbitrary",)))(proj, xa, proj_small, dt_bias, a_log, d_skip, norm_w)


def _ssd_bwd(dmixed, proj, proj_small, xa, y, rs2, hs, dt_bias, a_log, d_skip, norm_w):
    S = proj.shape[0]
    L, N, W, H = SSD_L, SSD_N, SSD_WIDTH, SSD_HEADS
    nc = S // L

    def body(dyo_ref, z_ref, xa_ref, dtr_ref, y_ref, rs_ref, hs_ref, dtb_ref, al_ref, dsk_ref, nw_ref,
             dz_ref, dxa_ref, ddtr_ref, ddtb_ref, dal_ref, ddsk_ref, dnw_ref, dh_scr, dx_scr):
        @pl.when(pl.program_id(0) == 0)
        def _():
            dh_scr[...] = jnp.zeros_like(dh_scr)
            ddtb_ref[...] = jnp.zeros_like(ddtb_ref)
            dal_ref[...] = jnp.zeros_like(dal_ref)
            ddsk_ref[...] = jnp.zeros_like(ddsk_ref)
            dnw_ref[...] = jnp.zeros_like(dnw_ref)

        causal, tri, upper, expand = _ssd_consts()
        heads = lambda t: _dot(t, expand, "nt", hi=True)
        onehot = lambda h: (lax.broadcasted_iota(jnp.int32, (1, H), 1) == h).astype(F32)

        zv, yv, rs = z_ref[...], y_ref[...], rs_ref[...]
        sz = _sigmoid(zv)
        zs = zv * sz
        xhat = (yv * zs) * rs
        dyo = dyo_ref[...].astype(F32)
        dnw_ref[...] += jnp.sum(dyo * xhat, axis=0, keepdims=True)
        dxhat = dyo * nw_ref[...]
        dyg = rs * (dxhat - xhat * jnp.mean(dxhat * xhat, axis=-1, keepdims=True))
        dz_ref[...] = (dyg * yv * (sz * (1.0 + zv * (1.0 - sz)))).astype(dz_ref.dtype)
        dy = dyg * zs

        dtr = dtr_ref[:, 0:H]
        dt, A, acum, acum_t, e, wdec, gam, dtE, eE, wE, gamE = _ssd_scalars(dtr, dtb_ref[...], al_ref[...], tri, upper, expand)
        xs = xa_ref[:, 0:W]
        X = xs * dtE
        XW = X * wE
        dskE = _dot(jnp.broadcast_to(dsk_ref[...], (8, H)), expand, "nn", hi=True)[0:1, :]
        ddsk_ref[...] += heads(jnp.broadcast_to(jnp.sum(dy * xs, axis=0, keepdims=True), (8, W)))[0:1, :]

        dYe = dy * eE
        dacum = jnp.zeros((L, H), F32)
        de_full = []
        dw_full = []
        dgam_full = []
        for g in range(SSD_G):
            gs = slice(g * 512, (g + 1) * 512)
            Bg = xa_ref[:, W + g * N:W + (g + 1) * N]
            Cg = xa_ref[:, W + SSD_G * N + g * N:W + SSD_G * N + (g + 1) * N]
            Hg = hs_ref[0, :, gs]
            dHn = dh_scr[:, gs]
            CH = _dot(Cg, Hg, "nn")
            de_full.append(dy[:, gs] * CH)
            dC = _dot(dYe[:, gs], Hg, "nt")
            dHs = gamE[:, gs] * dHn + _dot(Cg, dYe[:, gs], "tn")
            dgam_full.append(jnp.sum(dHn * Hg, axis=0, keepdims=True))
            BdS = _dot(Bg, dHn, "nn")
            dB = _dot(XW[:, gs], dHn, "nt")
            dx_scr[:, gs] = BdS * wE[:, gs]
            dw_full.append(BdS * X[:, gs])
            CB = _dot(Cg, Bg, "nt")
            dCB = jnp.zeros((L, L), F32)
            for j in range(8):
                h = g * 8 + j
                hsl = slice(h * SSD_P, (h + 1) * SSD_P)
                lam = jnp.exp(jnp.where(causal, acum[:, h:h + 1] - acum_t[h:h + 1, :], -jnp.inf))
                M = CB * lam
                dM = _dot(dy[:, hsl], X[:, hsl], "nt")
                dx_scr[:, hsl] += _dot(M, dy[:, hsl], "tn")
                dCB = dCB + dM * lam
                Q = dM * M
                rowsum = jnp.sum(Q, axis=1, keepdims=True)
                colsum = _dot(Q, jnp.ones((L, 8), F32), "tn", hi=True)[:, 0:1]
                dacum = dacum + (rowsum - colsum) * onehot(h)
            dC = dC + _dot(dCB, Bg, "nn")
            dB = dB + _dot(dCB, Cg, "tn")
            dxa_ref[:, W + g * N:W + (g + 1) * N] = dB
            dxa_ref[:, W + SSD_G * N + g * N:W + SSD_G * N + (g + 1) * N] = dC
            dh_scr[:, gs] = dHs

        de16 = heads(jnp.concatenate(de_full, axis=1))
        dw16 = heads(jnp.concatenate(dw_full, axis=1))
        dgam16 = heads(jnp.broadcast_to(jnp.concatenate(dgam_full, axis=1), (8, W)))[0:1, :]
        dacum = dacum + de16 * e - dw16 * wdec
        dlast = jnp.sum(dw16 * wdec, axis=0, keepdims=True) + dgam16 * gam
        lastrow = (lax.broadcasted_iota(jnp.int32, (L, 1), 0) == L - 1).astype(F32)
        dacum = dacum + lastrow * dlast
        da = _dot(tri, dacum, "tn", hi=True)
        dX = dx_scr[...]
        ddt = da * A + heads(dX * xs)
        dA = jnp.sum(da * dt, axis=0, keepdims=True)
        dal_ref[...] += dA * A
        ddtr = ddt * _sigmoid(dtr + dtb_ref[...])
        ddtb_ref[...] += jnp.sum(ddtr, axis=0, keepdims=True)
        ddtr_ref[...] = ddtr
        dxa_ref[:, 0:W] = dX * dtE + dy * dskE

    p16 = pl.BlockSpec((1, H), lambda c: (0, 0))
    rev = lambda c: (nc - 1 - c, 0)
    return pl.pallas_call(
        body, name="ssd_bwd", grid=(nc,),
        in_specs=[pl.BlockSpec((L, W), rev), pl.BlockSpec((L, W), rev), pl.BlockSpec((L, CONV_CH), rev),
                  pl.BlockSpec((L, W_SMALL), rev), pl.BlockSpec((L, W), rev), pl.BlockSpec((L, 1), rev),
                  pl.BlockSpec((1, N, W), lambda c: (nc - 1 - c, 0, 0)), p16, p16, p16, pl.BlockSpec((1, W), lambda c: (0, 0))],
        out_specs=[pl.BlockSpec((L, W), rev), pl.BlockSpec((L, CONV_CH), rev), pl.BlockSpec((L, H), rev),
                   p16, p16, p16, pl.BlockSpec((1, W), lambda c: (0, 0))],
        out_shape=[jax.ShapeDtypeStruct((S, W), _MXU), jax.ShapeDtypeStruct((S, CONV_CH), F32), jax.ShapeDtypeStruct((S, H), F32),
                   jax.ShapeDtypeStruct((1, H), F32), jax.ShapeDtypeStruct((1, H), F32), jax.ShapeDtypeStruct((1, H), F32),
                   jax.ShapeDtypeStruct((1, W), F32)],
        scratch_shapes=[pltpu.VMEM((N, W), F32), pltpu.VMEM((L, W), F32)],
        compiler_params=_cp(("arbitrary",)))(dmixed, proj, xa, proj_small, y, rs2, hs, dt_bias, a_log, d_skip, norm_w)


def _rope_tables(S):
    inv = 1.0 / (ROPE_THETA ** (jnp.arange(0, ROPE_DIM, 2, dtype=F32) / ROPE_DIM))
    ang = jnp.arange(S, dtype=F32)[:, None] * inv[None, :]
    cos, sin = jnp.cos(ang), jnp.sin(ang)
    half = ROPE_DIM // 2
    c64 = jnp.concatenate([cos, cos, jnp.ones((S, HD - ROPE_DIM), F32)], axis=1)
    s64 = jnp.concatenate([sin, sin, jnp.zeros((S, HD - ROPE_DIM), F32)], axis=1)
    del half
    return jnp.concatenate([c64, c64], axis=1), jnp.concatenate([s64, s64], axis=1)


def _rope(xs, blk0, width, cos, sin, sign, out_dtype, name, extra=None):
    S = xs[0].shape[0]
    tr = _pick(S, (512, 256, 128))
    nx = len(xs)

    def body(*refs):
        x_refs, c_ref, s_ref = refs[:nx], refs[nx], refs[nx + 1]
        e_ref = refs[nx + 2] if extra is not None else None
        o_ref = refs[-1]
        cv, sv = c_ref[...], s_ref[...] * sign
        lane = lax.broadcasted_iota(jnp.int32, (tr, 128), 1)
        first = (lane & (HD - 1)) < (ROPE_DIM // 2)
        for j in range(2):
            cs = slice(j * 128, (j + 1) * 128)
            xv = x_refs[0][:, cs].astype(F32)
            for r in x_refs[1:]:
                xv = xv + r[:, cs].astype(F32)
            rot = jnp.where(first, -pltpu.roll(xv, 128 - ROPE_DIM // 2, axis=1), pltpu.roll(xv, ROPE_DIM // 2, axis=1))
            out = xv * cv + rot * sv
            if extra is not None:
                out = out + e_ref[:, cs].astype(F32)
            o_ref[:, cs] = out.astype(out_dtype)

    t128 = pl.BlockSpec((tr, 128), lambda i, j: (i, 0))
    oblk = pl.BlockSpec((tr, 256), lambda i, j: (i, j))
    specs = [pl.BlockSpec((tr, 256), lambda i, j: (i, blk0 + j))] * nx + [t128, t128]
    ins = list(xs) + [cos, sin]
    if extra is not None:
        ins.append(extra[0])
        eb = extra[1]
        specs.append(pl.BlockSpec((tr, 256), lambda i, j: (i, eb + j)))
    return pl.pallas_call(
        body, name=name, grid=(S // tr, width // 256), in_specs=specs, out_specs=oblk,
        out_shape=jax.ShapeDtypeStruct((S, width), out_dtype), compiler_params=_cp(("parallel", "parallel")))(*ins)


def _rotate128(xv, cv, sv, first):
    rot = jnp.where(first, -pltpu.roll(xv, 128 - ROPE_DIM // 2, axis=1), pltpu.roll(xv, ROPE_DIM // 2, axis=1))
    return xv * cv + rot * sv


def _kv_prep(proj, cos, sin, tk):
    S = proj.shape[0]

    def body(ks_ref, vs_ref, kw_ref, vw_ref, c_ref, s_ref, *outs):
        cv, sv = c_ref[...], s_ref[...]
        lane = lax.broadcasted_iota(jnp.int32, (tk, 128), 1)
        first = (lane & (HD - 1)) < (ROPE_DIM // 2)
        for j, (ref, rotated) in enumerate(((ks_ref, True), (vs_ref, False), (kw_ref, True), (vw_ref, False))):
            nat, blk = outs[2 * j], outs[2 * j + 1]
            for half in range(2):
                xv = ref[:, half * 128:(half + 1) * 128]
                if rotated:
                    xv = _rotate128(xv, cv, sv, first)
                for e in range(2):
                    h = 2 * half + e
                    piece = xv[:, e * HD:(e + 1) * HD]
                    nat[h] = piece.astype(nat.dtype)
                    blk[h, 0] = piece.T.astype(blk.dtype)

    col = lambda b: pl.BlockSpec((tk, 256), lambda i: (i, b))
    t128 = pl.BlockSpec((tk, 128), lambda i: (i, 0))
    nat_spec = pl.BlockSpec((N_KV, tk, HD), lambda i: (0, i, 0))
    blk_spec = pl.BlockSpec((N_KV, 1, HD, tk), lambda i: (0, i, 0, 0))
    nat_shape = jax.ShapeDtypeStruct((N_KV, S, HD), _MXU)
    blk_shape = jax.ShapeDtypeStruct((N_KV, S // tk, HD, tk), _MXU)
    res = pl.pallas_call(
        body, name="kv_prep", grid=(S // tk,), in_specs=[col(KSB), col(VSB), col(KWB), col(VWB), t128, t128],
        out_specs=[nat_spec, blk_spec] * 4, out_shape=[nat_shape, blk_shape] * 4,
        compiler_params=_cp(("parallel",)))(proj, proj, proj, proj, cos, sin)
    return dict(ks=res[0], ks_t=res[1], vs=res[2], vs_t=res[3], kw=res[4], kw_t=res[5], vw=res[6], vw_t=res[7])


def _dkv_post(dks, dvs, dkw, dvw, cos, sin):
    S = dks.shape[1]
    tr = _pick(S, (512, 256, 128))

    def body(dks_ref, dvs_ref, dkw_ref, dvw_ref, c_ref, s_ref, o_ref):
        cv, sv = c_ref[...], -s_ref[...]
        lane = lax.broadcasted_iota(jnp.int32, (tr, 128), 1)
        first = (lane & (HD - 1)) < (ROPE_DIM // 2)
        for j, (ref, rotated) in enumerate(((dks_ref, True), (dvs_ref, False), (dkw_ref, True), (dvw_ref, False))):
            for half in range(2):
                xv = jnp.concatenate([ref[2 * half], ref[2 * half + 1]], axis=1)
                if rotated:
                    xv = _rotate128(xv, cv, sv, first)
                o_ref[:, j * 256 + half * 128:j * 256 + (half + 1) * 128] = xv.astype(o_ref.dtype)

    hm = pl.BlockSpec((N_KV, tr, HD), lambda i: (0, i, 0))
    t128 = pl.BlockSpec((tr, 128), lambda i: (i, 0))
    return pl.pallas_call(
        body, name="dkv_post", grid=(S // tr,), in_specs=[hm, hm, hm, hm, t128, t128],
        out_specs=pl.BlockSpec((tr, 4 * 256), lambda i: (i, 0)), out_shape=jax.ShapeDtypeStruct((S, 4 * 256), _MXU),
        compiler_params=_cp(("parallel",)))(dks, dvs, dkw, dvw, cos, sin)


def _compress_fwd(R, pe, w1, w2):
    NC = R.shape[1]
    half = 16 * HD

    def body(r_ref, pe_ref, w1_ref, w2_ref, o_ref, hid_ref):
        r = r_ref[0]
        a = _dot(r + pe_ref[:, 0:half], w1_ref[0:half, :], "nn")
        b = _dot(r + pe_ref[:, half:2 * half], w1_ref[half:2 * half, :], "nn")
        hid = a + pltpu.roll(b, NC - 1, axis=0)
        hid_ref[0] = hid
        out = _dot(hid * _sigmoid(hid), w2_ref[...], "nn")
        rows = lax.broadcasted_iota(jnp.int32, out.shape, 0)
        o_ref[0] = jnp.where(rows < NC - 1, out, 0.0).astype(o_ref.dtype)

    return pl.pallas_call(
        body, name="compress_fwd", grid=(N_KV,),
        in_specs=[pl.BlockSpec((1, NC, half), lambda h: (h, 0, 0)), pl.BlockSpec((1, 2 * half), lambda h: (0, 0)),
                  pl.BlockSpec((2 * half, CMP_HID), lambda h: (0, 0)), pl.BlockSpec((CMP_HID, HD), lambda h: (0, 0))],
        out_specs=[pl.BlockSpec((1, NC, HD), lambda h: (h, 0, 0)), pl.BlockSpec((1, NC, CMP_HID), lambda h: (h, 0, 0))],
        out_shape=[jax.ShapeDtypeStruct((N_KV, NC, HD), _MXU), jax.ShapeDtypeStruct((N_KV, NC, CMP_HID), F32)],
        compiler_params=_cp(("parallel",)))(R, pe, w1, w2)


def _compress_bwd(R, pe, w1, w2, hid, dout):
    NC = R.shape[1]
    half = 16 * HD

    def body(r_ref, pe_ref, w1_ref, w2_ref, hid_ref, do_ref, dr_ref, dw1_ref, dw2_ref, dpe_ref):
        @pl.when(pl.program_id(0) == 0)
        def _():
            dw1_ref[...] = jnp.zeros_like(dw1_ref)
            dw2_ref[...] = jnp.zeros_like(dw2_ref)
            dpe_ref[...] = jnp.zeros_like(dpe_ref)

        r, hv, do = r_ref[0], hid_ref[0], do_ref[0]
        s = _sigmoid(hv)
        dw2_ref[...] += _dot(hv * s, do, "tn")
        dhid = _dot(do, w2_ref[...], "nt") * (s * (1.0 + hv * (1.0 - s)))
        rows = lax.broadcasted_iota(jnp.int32, dhid.shape, 0)
        dhid = jnp.where(rows < NC - 1, dhid, 0.0)
        dhid_dn = pltpu.roll(dhid, 1, axis=0)
        dw1_ref[0:half, :] += _dot(r + pe_ref[:, 0:half], dhid, "tn")
        dw1_ref[half:2 * half, :] += _dot(r + pe_ref[:, half:2 * half], dhid_dn, "tn")
        dxt = _dot(dhid, w1_ref[0:half, :], "nt")
        dxb = _dot(dhid_dn, w1_ref[half:2 * half, :], "nt")
        dr_ref[0] = dxt + dxb
        dpe_ref[:, 0:half] += jnp.sum(dxt, axis=0, keepdims=True)
        dpe_ref[:, half:2 * half] += jnp.sum(dxb, axis=0, keepdims=True)

    return pl.pallas_call(
        body, name="compress_bwd", grid=(N_KV,),
        in_specs=[pl.BlockSpec((1, NC, half), lambda h: (h, 0, 0)), pl.BlockSpec((1, 2 * half), lambda h: (0, 0)),
                  pl.BlockSpec((2 * half, CMP_HID), lambda h: (0, 0)), pl.BlockSpec((CMP_HID, HD), lambda h: (0, 0)),
                  pl.BlockSpec((1, NC, CMP_HID), lambda h: (h, 0, 0)), pl.BlockSpec((1, NC, HD), lambda h: (h, 0, 0))],
        out_specs=[pl.BlockSpec((1, NC, half), lambda h: (h, 0, 0)), pl.BlockSpec((2 * half, CMP_HID), lambda h: (0, 0)),
                   pl.BlockSpec((CMP_HID, HD), lambda h: (0, 0)), pl.BlockSpec((1, 2 * half), lambda h: (0, 0))],
        out_shape=[jax.ShapeDtypeStruct((N_KV, NC, half), F32), jax.ShapeDtypeStruct((2 * half, CMP_HID), F32),
                   jax.ShapeDtypeStruct((CMP_HID, HD), F32), jax.ShapeDtypeStruct((1, 2 * half), F32)],
        compiler_params=_cp(("arbitrary",)))(R, pe, w1, w2, hid, dout)


def _attn_cfg(S, Sk, mode):
    tq = _pick(S, (256, 128))
    tk = Sk if mode == "cmp" else _pick(Sk, (256, 128))
    return tq, tk


def _kb_range(mode, q0, tq, tk):
    if mode == "cmp":
        return 0, 1
    hi = (q0 + tq - 1) // tk + 1
    if mode == "sel":
        return 0, hi
    return jnp.maximum(q0 - (WINDOW - 1), 0) // tk, hi


def _attn_bias(mode, q0, k0, tq, tk, sel_t):
    k = k0 + lax.broadcasted_iota(jnp.int32, (tk, tq), 0)
    t = q0 + lax.broadcasted_iota(jnp.int32, (tk, tq), 1)
    if mode == "cmp":
        ok = (k * 16 + 31) <= t
    elif mode == "win":
        ok = (k <= t) & ((t - k) < WINDOW)
    else:
        nb = sel_t.shape[0]
        ek = k0 + lax.broadcasted_iota(jnp.int32, (tk, nb), 0)
        eb = lax.broadcasted_iota(jnp.int32, (tk, nb), 1)
        expand = (lax.shift_right_logical(ek, 6) == eb).astype(_MXU)
        chosen = _dot(expand, sel_t, "nn") > 0.5
        ok = (k <= t) & chosen
    bias = jnp.where(ok, 0.0, NEG)
    return jnp.concatenate([bias] * GRP, axis=1), jnp.concatenate([ok.astype(F32)] * GRP, axis=1)


def _stack_heads(ref, tq):
    return jnp.concatenate([ref[:, g * HD:(g + 1) * HD] for g in range(GRP)], axis=0)


def _scaled_queries(q_ref, tq):
    return (_stack_heads(q_ref, tq).astype(F32) * SCALE).astype(_MXU)


def _blocked_t(x, tk):
    n, Sk, d = x.shape
    return x.reshape(n, Sk // tk, tk, d).transpose(0, 1, 3, 2)


def _head_rows(ref):
    return jnp.concatenate([ref[0, g:g + 1, :] for g in range(GRP)], axis=1)


def _attn_fwd(q, qcol0, k, vt, mode, sel_t, gate, y_prev, y_dtype, name):
    S, Sk = q.shape[0], k.shape[1]
    tq, tk = _attn_cfg(S, Sk, mode)
    R = GRP * tq

    def body(*refs):
        q_ref, k_ref, vt_ref = refs[:3]
        rest = list(refs[3:])
        sel_ref = rest.pop(0) if mode == "sel" else None
        gate_ref = rest.pop(0)
        yp_ref = rest.pop(0) if y_prev is not None else None
        o_ref, lse_ref, y_ref, m_scr, l_scr, acc = rest
        q0 = pl.program_id(1) * tq
        qs = _scaled_queries(q_ref, tq)
        m_scr[...] = jnp.full_like(m_scr, NEG)
        l_scr[...] = jnp.zeros_like(l_scr)
        acc[...] = jnp.zeros_like(acc)
        selv = sel_ref[0].astype(_MXU) if mode == "sel" else None

        def step(kb, carry):
            k0 = pl.multiple_of(kb * tk, tk)
            bias, okf = _attn_bias(mode, q0, k0, tq, tk, selv)
            s = _dot(k_ref[0, pl.ds(k0, tk), :], qs, "nt") + bias
            m_old = m_scr[...]
            m_new = jnp.maximum(m_old, jnp.max(s, axis=0, keepdims=True))
            p = jnp.exp(s - m_new)
            if mode == "cmp":
                p = p * okf
            alpha = jnp.exp(m_old - m_new)
            l_scr[...] = alpha * l_scr[...] + jnp.sum(p, axis=0, keepdims=True)
            acc[...] = alpha * acc[...] + _dot(vt_ref[0, kb], p, "nn")
            m_scr[...] = m_new
            return carry

        lo, hi = _kb_range(mode, q0, tq, tk)
        lax.fori_loop(lo, hi, step, 0)
        l = l_scr[...]
        good = l > 0.0
        o_t = acc[...] * jnp.where(good, 1.0 / jnp.where(good, l, 1.0), 0.0)
        lse = jnp.where(good, m_scr[...] + jnp.log(jnp.where(good, l, 1.0)), -NEG)
        y_t = o_t * _sigmoid(_head_rows(gate_ref))
        for g in range(GRP):
            hs, qs_ = slice(g * HD, (g + 1) * HD), slice(g * tq, (g + 1) * tq)
            o_ref[:, hs] = o_t[:, qs_].T
            lse_ref[0, g:g + 1, :] = lse[:, qs_]
            yg = y_t[:, qs_].T
            if y_prev is not None:
                yg = yg + yp_ref[:, hs]
            y_ref[:, hs] = yg.astype(y_ref.dtype)

    row_spec = pl.BlockSpec((1, GRP, tq), lambda h, i: (h, 0, i))
    qo_spec = pl.BlockSpec((tq, GRP * HD), lambda h, i: (i, h))
    ins = [q, k, vt]
    specs = [pl.BlockSpec((tq, GRP * HD), lambda h, i: (i, qcol0 + h)), pl.BlockSpec((1, Sk, HD), lambda h, i: (h, 0, 0)),
             pl.BlockSpec((1, Sk // tk, HD, tk), lambda h, i: (h, 0, 0, 0))]
    if mode == "sel":
        ins.append(sel_t)
        specs.append(pl.BlockSpec((1, sel_t.shape[1], tq), lambda h, i: (h, 0, i)))
    ins.append(gate)
    specs.append(row_spec)
    if y_prev is not None:
        ins.append(y_prev)
        specs.append(qo_spec)
    return pl.pallas_call(
        body, name=name, grid=(N_KV, S // tq), in_specs=specs, out_specs=[qo_spec, row_spec, qo_spec],
        out_shape=[jax.ShapeDtypeStruct((S, ATT_WIDTH), F32), jax.ShapeDtypeStruct((N_KV, GRP, S), F32),
                   jax.ShapeDtypeStruct((S, ATT_WIDTH), y_dtype)],
        scratch_shapes=[pltpu.VMEM((1, R), F32), pltpu.VMEM((1, R), F32), pltpu.VMEM((HD, R), F32)],
        compiler_params=_cp(("parallel", "arbitrary")))(*ins)


def _attn_bwd(q, qcol0, k, kt, v, o, lse, dy, dycol0, gate, mode, sel_t, name):
    S, Sk = q.shape[0], k.shape[1]
    tq, tk = _attn_cfg(S, Sk, mode)
    R = GRP * tq

    def body(*refs):
        if mode == "sel":
            q_ref, k_ref, kt_ref, v_ref, o_ref, lse_ref, dy_ref, gate_ref, sel_ref, dq_ref, dk_ref, dv_ref, dg_ref, dq_scr = refs
        else:
            q_ref, k_ref, kt_ref, v_ref, o_ref, lse_ref, dy_ref, gate_ref, dq_ref, dk_ref, dv_ref, dg_ref, dq_scr = refs

        @pl.when(pl.program_id(1) == 0)
        def _():
            dk_ref[...] = jnp.zeros_like(dk_ref)
            dv_ref[...] = jnp.zeros_like(dv_ref)

        q0 = pl.program_id(1) * tq
        qs = _scaled_queries(q_ref, tq)
        dys = _stack_heads(dy_ref, tq)
        gv = _sigmoid(_head_rows(gate_ref))
        dy_o = _dot(jnp.ones((8, HD), F32), dys * _stack_heads(o_ref, tq), "nt", hi=True)[0:1, :]
        delta = gv * dy_o
        dgate = dy_o * (gv * (1.0 - gv))
        for g in range(GRP):
            dg_ref[0, g:g + 1, :] = dgate[:, g * tq:(g + 1) * tq]
        lsev = _head_rows(lse_ref)
        dos = (dys * jnp.broadcast_to(gv, (8, R)).T[:, 0:1]).astype(_MXU)
        dq_scr[...] = jnp.zeros_like(dq_scr)
        selv = sel_ref[0].astype(_MXU) if mode == "sel" else None

        def step(kb, carry):
            k0 = pl.multiple_of(kb * tk, tk)
            kv = k_ref[0, pl.ds(k0, tk), :]
            bias, okf = _attn_bias(mode, q0, k0, tq, tk, selv)
            p = jnp.exp(_dot(kv, qs, "nt") + bias - lsev)
            if mode == "cmp":
                p = p * okf
            dp = _dot(v_ref[0, pl.ds(k0, tk), :], dos, "nt")
            ds = p * (dp - delta)
            dq_scr[...] += _dot(kt_ref[0, kb], ds, "nn")
            dk_ref[0, pl.ds(k0, tk), :] += _dot(ds, qs, "nn")
            dv_ref[0, pl.ds(k0, tk), :] += _dot(p, dos, "nn")
            return carry

        lo, hi = _kb_range(mode, q0, tq, tk)
        lax.fori_loop(lo, hi, step, 0)
        for g in range(GRP):
            dq_ref[:, g * HD:(g + 1) * HD] = (dq_scr[:, g * tq:(g + 1) * tq] * SCALE).T

    kv_spec = pl.BlockSpec((1, Sk, HD), lambda h, i: (h, 0, 0))
    qo_spec = pl.BlockSpec((tq, GRP * HD), lambda h, i: (i, h))
    row_spec = pl.BlockSpec((1, GRP, tq), lambda h, i: (h, 0, i))
    ins = [q, k, kt, v, o, lse, dy, gate]
    specs = [pl.BlockSpec((tq, GRP * HD), lambda h, i: (i, qcol0 + h)), kv_spec,
             pl.BlockSpec((1, Sk // tk, HD, tk), lambda h, i: (h, 0, 0, 0)), kv_spec, qo_spec, row_spec,
             pl.BlockSpec((tq, GRP * HD), lambda h, i: (i, dycol0 + h)), row_spec]
    if mode == "sel":
        ins.append(sel_t)
        specs.append(pl.BlockSpec((1, sel_t.shape[1], tq), lambda h, i: (h, 0, i)))
    return pl.pallas_call(
        body, name=name, grid=(N_KV, S // tq), in_specs=specs, out_specs=[qo_spec, kv_spec, kv_spec, row_spec],
        out_shape=[jax.ShapeDtypeStruct((S, ATT_WIDTH), F32), jax.ShapeDtypeStruct((N_KV, Sk, HD), F32),
                   jax.ShapeDtypeStruct((N_KV, Sk, HD), F32), jax.ShapeDtypeStruct((N_KV, GRP, S), F32)],
        scratch_shapes=[pltpu.VMEM((HD, R), F32)],
        compiler_params=_cp(("parallel", "arbitrary")))(*ins)


def _select(q, qcol0, k_cmp, lse):
    S, NC = q.shape[0], k_cmp.shape[1]
    NB = S // SEL_BLOCK
    tq = _pick(S, (256, 128))
    ci = np.arange(NC)[None, :] * 16
    sj = np.arange(NB)[:, None] * SEL_BLOCK
    ov_t = np.clip(np.minimum(ci + 32, sj + SEL_BLOCK) - np.maximum(ci, sj), 0, None) / 32.0
    ov_t[:, NC - 1] = 0.0
    ov_t = jnp.asarray(ov_t, F32)

    def body(q_ref, k_ref, lse_ref, ov_ref, sel_ref):
        q0 = pl.program_id(1) * tq
        bias, okf = _attn_bias("cmp", q0, 0, tq, NC, None)
        lsev = jnp.concatenate([lse_ref[0, g:g + 1, :] for g in range(GRP)], axis=1)
        p = jnp.exp(_dot(k_ref[0], _scaled_queries(q_ref, tq), "nt") + bias - lsev) * okf
        imp4 = _dot(ov_ref[...], p, "nn")
        imp = imp4[:, 0:tq] + imp4[:, tq:2 * tq] + imp4[:, 2 * tq:3 * tq] + imp4[:, 3 * tq:4 * tq]
        blk = lax.broadcasted_iota(jnp.int32, (NB, tq), 0)
        cur = lax.shift_right_logical(q0 + lax.broadcasted_iota(jnp.int32, (NB, tq), 1), 6)
        imp = jnp.where((blk == 0) | (blk == cur) | (blk == cur - 1), FORCE, imp)
        imp = jnp.where(blk <= cur, imp, -1.0)
        rank = jnp.zeros((NB, tq), F32)
        for j in range(NB):
            row = imp[j:j + 1, :]
            ahead = (row > imp) | ((row == imp) & (blk > j))
            rank = rank + ahead.astype(F32)
        sel_ref[0] = ((rank < float(N_SELECT)) & (imp >= 0.0)).astype(F32)

    return pl.pallas_call(
        body, name="select_blocks", grid=(N_KV, S // tq),
        in_specs=[pl.BlockSpec((tq, GRP * HD), lambda h, i: (i, qcol0 + h)), pl.BlockSpec((1, NC, HD), lambda h, i: (h, 0, 0)),
                  pl.BlockSpec((1, GRP, tq), lambda h, i: (h, 0, i)), pl.BlockSpec((NB, NC), lambda h, i: (0, 0))],
        out_specs=pl.BlockSpec((1, NB, tq), lambda h, i: (h, 0, i)),
        out_shape=jax.ShapeDtypeStruct((N_KV, NB, S), F32), compiler_params=_cp(("parallel", "parallel")))(q, k_cmp, lse, ov_t)


def _to_rows16(x):
    S = x.shape[0]
    return x.reshape(S // 16, 16, N_KV, HD).transpose(2, 0, 1, 3).reshape(N_KV, S // 16, 16 * HD)


def _from_rows16(r):
    NC = r.shape[1]
    return r.reshape(N_KV, NC, 16, HD).transpose(1, 2, 0, 3).reshape(NC * 16, N_KV * HD)


DT_COL0 = SSD_WIDTH + CONV_CH
GATE_IN_COL0 = D_IN - 3 * N_HEADS


SHARD_IN = D_IN // N_DEV


def _orig_cols(ref, c0, width):
    pieces, c = [], c0
    while c < c0 + width:
        d, off = divmod(c, SHARD_IN)
        w = min(SHARD_IN - off, c0 + width - c)
        pieces.append(ref[d, :, off:off + w])
        c += w
    return pieces[0] if len(pieces) == 1 else jnp.concatenate(pieces, axis=1)


def _cols_from_slabs(slabs):
    _, R, c = slabs.shape
    tr = _pick(R, (256, 128))

    def body(s_ref, o_ref):
        for t in range(N_DEV * c // LANE):
            pieces, col = [], t * LANE
            while col < (t + 1) * LANE:
                d, off = divmod(col, c)
                w = min(c - off, (t + 1) * LANE - col)
                pieces.append(s_ref[d, :, off:off + w])
                col += w
            o_ref[:, t * LANE:(t + 1) * LANE] = pieces[0] if len(pieces) == 1 else jnp.concatenate(pieces, axis=1)

    return pl.pallas_call(
        body, name="cols_from_slabs", grid=(R // tr,), in_specs=[pl.BlockSpec((N_DEV, tr, c), lambda i: (0, i, 0))],
        out_specs=pl.BlockSpec((tr, N_DEV * c), lambda i: (i, 0)), out_shape=jax.ShapeDtypeStruct((R, N_DEV * c), slabs.dtype),
        compiler_params=_cp(("parallel",)))(slabs)


def _slabs_from_cols(x):
    R, c = x.shape[0], x.shape[1] // N_DEV
    tr = _pick(R, (256, 128))

    def body(x_ref, o_ref):
        for d in range(N_DEV):
            o_ref[d] = x_ref[:, d * c:(d + 1) * c]

    return pl.pallas_call(
        body, name="slabs_from_cols", grid=(R // tr,), in_specs=[pl.BlockSpec((tr, N_DEV * c), lambda i: (i, 0))],
        out_specs=pl.BlockSpec((N_DEV, tr, c), lambda i: (0, i, 0)), out_shape=jax.ShapeDtypeStruct((N_DEV, R, c), x.dtype),
        compiler_params=_cp(("parallel",)))(x)


def _w_in_from_slabs(slabs):
    D = slabs.shape[1]
    tr = _pick(D, (256, 128))

    def body(s_ref, main_ref, small_ref):
        for t in range(W_MAIN // LANE):
            c = t * LANE
            main_ref[:, c:c + LANE] = _orig_cols(s_ref, c if c < DT_COL0 else c + SSD_HEADS, LANE)
        small_ref[...] = jnp.concatenate(
            [_orig_cols(s_ref, DT_COL0, SSD_HEADS), _orig_cols(s_ref, GATE_IN_COL0, 3 * N_HEADS),
             jnp.zeros((tr, W_SMALL - SSD_HEADS - 3 * N_HEADS), small_ref.dtype)], axis=1)

    return pl.pallas_call(
        body, name="w_in_layout", grid=(D // tr,), in_specs=[pl.BlockSpec((N_DEV, tr, SHARD_IN), lambda i: (0, i, 0))],
        out_specs=[pl.BlockSpec((tr, W_MAIN), lambda i: (i, 0)), pl.BlockSpec((tr, W_SMALL), lambda i: (i, 0))],
        out_shape=[jax.ShapeDtypeStruct((D, W_MAIN), slabs.dtype), jax.ShapeDtypeStruct((D, W_SMALL), slabs.dtype)],
        compiler_params=_cp(("parallel",)))(slabs)


def _w_in_to_slabs(main, small):
    D = main.shape[0]
    tr = _pick(D, (256, 128))
    ranges = [(0, DT_COL0, 0, 0), (DT_COL0, DT_COL0 + SSD_HEADS, 1, 0), (DT_COL0 + SSD_HEADS, GATE_IN_COL0, 0, DT_COL0),
              (GATE_IN_COL0, D_IN, 1, SSD_HEADS)]

    def body(main_ref, small_ref, o_ref):
        srcs = (main_ref, small_ref)
        for d in range(N_DEV):
            lo, hi = d * SHARD_IN, (d + 1) * SHARD_IN
            pieces = []
            for start, stop, which, s0 in ranges:
                a, b = max(lo, start), min(hi, stop)
                if a < b:
                    pieces.append(srcs[which][:, s0 + a - start:s0 + b - start].astype(o_ref.dtype))
            o_ref[d] = pieces[0] if len(pieces) == 1 else jnp.concatenate(pieces, axis=1)

    return pl.pallas_call(
        body, name="w_in_grad_layout", grid=(D // tr,),
        in_specs=[pl.BlockSpec((tr, W_MAIN), lambda i: (i, 0)), pl.BlockSpec((tr, W_SMALL), lambda i: (i, 0))],
        out_specs=pl.BlockSpec((N_DEV, tr, SHARD_IN), lambda i: (0, i, 0)),
        out_shape=jax.ShapeDtypeStruct((N_DEV, D, SHARD_IN), main.dtype), compiler_params=_cp(("parallel",)))(main, small)


QB, KCB, VCB, KSB, VSB, KWB, VWB = 10, 14, 15, 16, 17, 18, 19


def _col256(a, b):
    return a[:, b * 256:(b + 1) * 256]


_EARLY = ["w_in", "cmp_w1_k", "cmp_w1_v"]
_LATE = ["w_out", "w_gate", "w_up", "w_down"]
_FFN = ["w_down", "w_gate", "w_up"]
_MID = ["w_out"]
_LAST = ["cmp_w1_k", "cmp_w1_v", "w_in"]


def _local_step(x, tgt, p, early_weights=None, late_weights=None, grads_ready=None):
    S = x.shape[0]
    cos, sin = _rope_tables(S)

    u, rs1 = _rms_fwd(x, p["attn_norm_w"], "attn_norm")
    if early_weights is not None:
        p = {**p, **early_weights((u, cos, sin))}
    proj = _mm(u, p["w_main"], "nn", F32, "in_proj")
    proj_small = _mm(u, p["w_small"], "nn", F32, "in_proj_small")
    xa = _conv_fwd(proj, p["conv_w"], p["conv_b"])
    y_ssd, y_pre, rs_ssd, hs = _ssd_fwd(proj, proj_small, xa, p["dt_bias"], p["a_log"], p["d_skip"], p["ssd_norm_w"])

    q_rot = _rope([proj], QB, ATT_WIDTH, cos, sin, 1.0, _MXU, "rope_q")
    kv = _kv_prep(proj, cos, sin, _attn_cfg(S, S, "sel")[1])
    rk, rv = _to_rows16(_col256(proj, KCB)), _to_rows16(_col256(proj, VCB))
    k_cmp, hid_k = _compress_fwd(rk, p["cmp_pe_k"], p["cmp_w1_k"], p["cmp_w2_k"])
    v_cmp, hid_v = _compress_fwd(rv, p["cmp_pe_v"], p["cmp_w1_v"], p["cmp_w2_v"])
    n_cmp = k_cmp.shape[1]

    gates = proj_small[:, SSD_HEADS:SSD_HEADS + 3 * N_HEADS].reshape(S, N_KV, GRP, 3).transpose(3, 1, 2, 0)
    o_cmp, lse_cmp, y_att = _attn_fwd(proj, QB, k_cmp, _blocked_t(v_cmp, n_cmp), "cmp", None, gates[0], None, F32, "attn_cmp_fwd")
    sel = _select(proj, QB, k_cmp, lse_cmp)
    o_sel, lse_sel, y_att = _attn_fwd(q_rot, 0, kv["ks"], kv["vs_t"], "sel", sel, gates[1], y_att, F32, "attn_sel_fwd")
    o_win, lse_win, y_att = _attn_fwd(q_rot, 0, kv["kw"], kv["vw_t"], "win", None, gates[2], y_att, _MXU, "attn_win_fwd")

    if late_weights is not None:
        p = {**p, **late_weights(y_att)}
    mixed = jnp.concatenate([y_ssd, y_att], axis=1)
    h1 = _mm(mixed, p["w_out"], "nn", F32, "out_proj", res=x)
    v, rs_ffn = _rms_fwd(h1, p["ffn_norm_w"], "ffn_norm")
    gt, up, act = _ffn_up(v, p["w_gate"], p["w_up"])
    h2 = _mm(act, p["w_down"], "nn", F32, "ffn_down", res=h1)
    loss, dh2, dh2b, d_final_w = _final_loss(h2, p["final_norm_w"], tgt)

    def ready(names):
        return None if grads_ready is None else grads_ready(names, g)

    g = {"final_norm_w": d_final_w}
    g["w_down"] = _mm(act, dh2b, "tn", _MXU, "dw_down")
    dgt, dup = _ffn_dact(dh2b, p["w_down"], gt, up)
    g["w_gate"] = _mm(v, dgt, "tn", _MXU, "dw_gate")
    g["w_up"] = _mm(v, dup, "tn", _MXU, "dw_up")
    dv = _mm(dgt, p["w_gate"], "nt", F32, "dv_gate", after=ready(_FFN))
    dv = _mm(dup, p["w_up"], "nt", F32, "dv_up", res=dv)
    dh1, dh1b, g["ffn_norm_w"] = _rms_bwd(dv, h1, rs_ffn, p["ffn_norm_w"], dh2, "ffn_norm_bwd")
    g["w_out"] = _mm(mixed, dh1b, "tn", _MXU, "dw_out")
    dmixed = _mm(dh1b, p["w_out"], "nt", F32, "dmixed", after=ready(_MID))

    dz, dxa, ddtr, g["dt_bias"], g["a_log"], g["d_skip"], g["ssd_norm_w"] = _ssd_bwd(
        dmixed, proj, proj_small, xa, y_pre, rs_ssd, hs, p["dt_bias"], p["a_log"], p["d_skip"], p["ssd_norm_w"])
    dxbc, g["conv_w"], g["conv_b"] = _conv_bwd(proj, p["conv_w"], p["conv_b"], dxa)

    dyb = SSD_WIDTH // (GRP * HD)
    dq_cmp, dk_cmp, dv_cmp, dg_cmp = _attn_bwd(proj, QB, k_cmp, _blocked_t(k_cmp, n_cmp), v_cmp, o_cmp, lse_cmp, dmixed, dyb,
                                               gates[0], "cmp", None, "attn_cmp_bwd")
    dq_sel, dks, dvs, dg_sel = _attn_bwd(q_rot, 0, kv["ks"], kv["ks_t"], kv["vs"], o_sel, lse_sel, dmixed, dyb, gates[1], "sel", sel,
                                         "attn_sel_bwd")
    dq_win, dkw, dvw, dg_win = _attn_bwd(q_rot, 0, kv["kw"], kv["kw_t"], kv["vw"], o_win, lse_win, dmixed, dyb, gates[2], "win", None,
                                         "attn_win_bwd")
    dgate = jnp.stack([dg_cmp, dg_sel, dg_win]).transpose(3, 1, 2, 0).reshape(S, 3 * N_HEADS)
    drk, g["cmp_w1_k"], g["cmp_w2_k"], g["cmp_pe_k"] = _compress_bwd(rk, p["cmp_pe_k"], p["cmp_w1_k"], p["cmp_w2_k"], hid_k, dk_cmp)
    drv, g["cmp_w1_v"], g["cmp_w2_v"], g["cmp_pe_v"] = _compress_bwd(rv, p["cmp_pe_v"], p["cmp_w1_v"], p["cmp_w2_v"], hid_v, dv_cmp)
    dq = _rope([dq_sel, dq_win], 0, ATT_WIDTH, cos, sin, -1.0, _MXU, "rope_dq", extra=(dq_cmp, 0))
    dkv = _dkv_post(dks, dvs, dkw, dvw, cos, sin)
    dproj = jnp.concatenate([dz, dxbc, dq] + [t.astype(_MXU) for t in (_from_rows16(drk), _from_rows16(drv))] + [dkv], axis=1)
    dsmall = jnp.concatenate([ddtr, dgate, jnp.zeros((S, W_SMALL - SSD_HEADS - 3 * N_HEADS), F32)], axis=1).astype(_MXU)
    g["w_main"] = _mm(u, dproj, "tn", _MXU, "dw_in")
    g["w_small"] = _mm(u, dsmall, "tn", F32, "dw_in_small")
    du = _mm(dproj, p["w_main"], "nt", F32, "du_main", after=ready(_LAST))
    du = _mm(dsmall, p["w_small"], "nt", F32, "du_small", res=du)
    grad_x, _, g["attn_norm_w"] = _rms_bwd(du, x, rs1, p["attn_norm_w"], dh1, "attn_norm_bwd")
    return loss, grad_x, g


MESH_ID = pl.DeviceIdType.MESH


def _my_coords():
    return lax.axis_index("x"), lax.axis_index("y"), lax.axis_index("c")


def _flat_id(px, py, pc):
    return 4 * px + 2 * py + pc


def _peer(k):
    mx, my, mc = _my_coords()
    return (1 - mx if k & 4 else mx, 1 - my if k & 2 else my, 1 - mc if k & 1 else mc)


def _exchange(arrs, scatter, name, after=()):
    n, na = len(arrs), len(after)
    scatter = [scatter] * n if isinstance(scatter, bool) else list(scatter)

    def body(*refs):
        ins, outs = refs[:n], refs[n + na:2 * n + na]
        send_sems, recv_sems, local_sems = refs[2 * n + na:]
        me = _flat_id(*_my_coords())
        copies = []
        for i in range(n):
            src_me = ins[i].at[me] if scatter[i] else ins[i]
            local = pltpu.make_async_copy(src_me, outs[i].at[me], local_sems.at[i])
            local.start()
            copies.append(local)
        for k in range(1, N_DEV):
            peer = _peer(k)
            for i in range(n):
                src = ins[i].at[_flat_id(*peer)] if scatter[i] else ins[i]
                cp = pltpu.make_async_remote_copy(src_ref=src, dst_ref=outs[i].at[me], send_sem=send_sems.at[i * 7 + k - 1],
                                                  recv_sem=recv_sems.at[i * 7 + k - 1], device_id=peer, device_id_type=MESH_ID)
                cp.start()
                copies.append(cp)
        for cp in copies:
            cp.wait()

    any_spec = pl.BlockSpec(memory_space=pl.ANY)
    out_shape = [jax.ShapeDtypeStruct(a.shape if sc else (N_DEV,) + a.shape, a.dtype) for a, sc in zip(arrs, scatter)]
    return pl.pallas_call(
        body, name=name, in_specs=[any_spec] * (n + na), out_specs=[any_spec] * n, out_shape=out_shape,
        scratch_shapes=[pltpu.SemaphoreType.DMA((n * 7,)), pltpu.SemaphoreType.DMA((n * 7,)), pltpu.SemaphoreType.DMA((n,))],
        compiler_params=pltpu.CompilerParams(has_side_effects=True))(*arrs, *after)


_HBM = pl.BlockSpec(memory_space=pltpu.HBM)
_SEM = pl.BlockSpec(memory_space=pltpu.SEMAPHORE)
_EFFECT = pltpu.SideEffectType.DATAFLOW_SIDE_EFFECTING


def _split_copies(ins, lands, send_sems, recv_sems, scatter):
    me = _flat_id(*_my_coords())
    out = []
    for k in range(1, N_DEV):
        peer = _peer(k)
        for i in range(len(ins)):
            src = ins[i].at[_flat_id(*peer)] if scatter else ins[i]
            out.append(pltpu.make_async_remote_copy(src_ref=src, dst_ref=lands[i].at[me], send_sem=send_sems.at[i * 7 + k - 1],
                                                    recv_sem=recv_sems.at[i * 7 + k - 1], device_id=peer, device_id_type=MESH_ID))
    return out


def _split_start(arrs, scatter, name):
    n = len(arrs)

    def body(*refs):
        for cp in _split_copies(refs[:n], refs[n:2 * n], refs[2 * n], refs[2 * n + 1], scatter):
            cp.start()
        refs[-1][...] = jnp.zeros_like(refs[-1])

    land_shapes = [a.shape if scatter else (N_DEV,) + a.shape for a in arrs]
    out_shape = ((pltpu.SemaphoreType.DMA((n * 7,)), pltpu.SemaphoreType.DMA((n * 7,)))
                 + tuple(pltpu.HBM(a.shape, a.dtype) for a in arrs) + tuple(pltpu.HBM(s, a.dtype) for s, a in zip(land_shapes, arrs))
                 + (jax.ShapeDtypeStruct((8, 128), F32),))
    operands = ([pltpu.with_memory_space_constraint(a, pltpu.HBM) for a in arrs]
                + [pltpu.with_memory_space_constraint(lax.empty(s, a.dtype), pltpu.HBM) for s, a in zip(land_shapes, arrs)])
    res = pl.pallas_call(
        body, name=name, out_shape=out_shape, in_specs=[_HBM] * (2 * n),
        out_specs=(_SEM, _SEM) + (_HBM,) * (2 * n) + (pl.BlockSpec(memory_space=pltpu.VMEM),),
        input_output_aliases={i: 2 + i for i in range(2 * n)},
        compiler_params=pltpu.CompilerParams(has_side_effects=_EFFECT))(*operands)
    return dict(send=res[0], recv=res[1], ins=list(res[2:2 + n]), lands=list(res[2 + n:2 + 2 * n]), token=res[-1])


def _split_wait(st, scatter, after, name):
    n = len(st["ins"])

    def body(*refs):
        for cp in _split_copies(refs[:n], refs[n:2 * n], refs[2 * n], refs[2 * n + 1], scatter):
            cp.wait_send()
            cp.wait_recv()

    arrs = st["ins"] + st["lands"]
    res = pl.pallas_call(
        body, name=name, out_shape=tuple(pltpu.HBM(a.shape, a.dtype) for a in arrs),
        in_specs=[_HBM] * (2 * n) + [_SEM, _SEM] + [pl.BlockSpec(memory_space=pl.ANY)] * len(after), out_specs=(_HBM,) * (2 * n),
        input_output_aliases={i: i for i in range(2 * n)},
        compiler_params=pltpu.CompilerParams(has_side_effects=_EFFECT))(*arrs, st["send"], st["recv"], *after)
    me = _flat_id(*_my_coords())
    out = []
    for src, land in zip(res[:n], res[n:]):
        own = lax.dynamic_index_in_dim(src, me, 0, keepdims=True) if scatter else src[None]
        out.append(lax.dynamic_update_slice_in_dim(land, own, me, 0))
    return out


def _adam_step(p_ref, w_ref, m_ref, v_ref, g_ref, d_ref, nm_ref, nv_ref):
    g = p_ref[0].astype(F32)
    for j in range(1, p_ref.shape[0]):
        g = g + p_ref[j].astype(F32)
    g_ref[...] = g
    nm = ADAM_B1 * m_ref[...] + (1.0 - ADAM_B1) * g
    nv = ADAM_B2 * v_ref[...] + (1.0 - ADAM_B2) * (g * g)
    nm_ref[...] = nm
    nv_ref[...] = nv
    m_hat = nm / (1.0 - ADAM_B1 ** ADAM_STEP)
    v_hat = nv / (1.0 - ADAM_B2 ** ADAM_STEP)
    d_ref[...] = -ADAM_LR * (m_hat / (jnp.sqrt(v_hat) + ADAM_EPS) + ADAM_WD * w_ref[...])


def _adam_sum(parts, w, m, v, name):
    P, R, C = parts.shape
    tr = _pick(R, (256, 128, 64, 32, 8)) if C <= 1024 else _pick(R, (128, 64, 32, 8))
    blk = pl.BlockSpec((tr, C), lambda i: (i, 0))
    return pl.pallas_call(
        functools.partial(_adam_step), name=name, grid=(R // tr,),
        in_specs=[pl.BlockSpec((P, tr, C), lambda i: (0, i, 0)), blk, blk, blk],
        out_specs=[blk] * 4, out_shape=[jax.ShapeDtypeStruct((R, C), F32)] * 4, compiler_params=_cp(("parallel",)))(parts, w, m, v)


def _adam_small(loss_parts, parts, ws, ms, vs):
    n = len(parts)

    def body(*refs):
        loss_ref, ins, outs, total_ref = refs[0], refs[1:4 * n + 1], refs[4 * n + 1:-1], refs[-1]
        for i in range(n):
            _adam_step(ins[i], ins[n + i], ins[2 * n + i], ins[3 * n + i], *outs[4 * i:4 * i + 4])
        total = loss_ref[0]
        for d in range(1, N_DEV):
            total = total + loss_ref[d]
        total_ref[...] = total

    out_shape = [jax.ShapeDtypeStruct(w.shape, F32) for w in ws for _ in range(4)] + [jax.ShapeDtypeStruct(loss_parts.shape[1:], F32)]
    res = pl.pallas_call(body, name="adam_small", out_shape=out_shape)(loss_parts, *parts, *ws, *ms, *vs)
    return res[-1], [tuple(res[4 * i:4 * i + 4]) for i in range(n)]


_WEIGHTS = ["attn_norm_w", "w_in", "conv_w", "conv_b", "dt_bias", "a_log", "d_skip", "ssd_norm_w", "cmp_w1_k", "cmp_w2_k",
            "cmp_w1_v", "cmp_w2_v", "cmp_pe_k", "cmp_pe_v", "w_out", "ffn_norm_w", "w_gate", "w_up", "w_down", "final_norm_w"]
_BIG = ["w_in", "w_gate", "w_up", "w_down", "w_out", "cmp_w1_k", "cmp_w1_v"]
_COL_SHARDED = ("w_in", "w_gate", "w_up")
_REPLICATED = ["attn_norm_w", "conv_b", "dt_bias", "a_log", "d_skip", "ssd_norm_w", "cmp_pe_k", "cmp_pe_v", "ffn_norm_w",
               "final_norm_w"]
_SMALL_SHARDED = ["conv_w", "cmp_w2_k", "cmp_w2_v"]


def _cols_to_slabs(g):
    R = g.shape[0]
    return g.reshape(R, N_DEV, -1).transpose(1, 0, 2)


def _slabs_to_cols(s):
    return s.transpose(1, 0, 2).reshape(s.shape[1], -1)


def kernel(x, attn_norm_w, w_in, conv_w, conv_b, dt_bias, a_log, d_skip, ssd_norm_w, cmp_w1_k, cmp_w2_k, cmp_w1_v, cmp_w2_v, cmp_pe_k, cmp_pe_v, w_out, ffn_norm_w, w_gate, w_up, w_down, final_norm_w, loss_target, m_attn_norm_w, m_w_in, m_conv_w, m_conv_b, m_dt_bias, m_a_log, m_d_skip, m_ssd_norm_w, m_cmp_w1_k, m_cmp_w2_k, m_cmp_w1_v, m_cmp_w2_v, m_cmp_pe_k, m_cmp_pe_v, m_w_out, m_ffn_norm_w, m_w_gate, m_w_up, m_w_down, m_final_norm_w, v_attn_norm_w, v_w_in, v_conv_w, v_conv_b, v_dt_bias, v_a_log, v_d_skip, v_ssd_norm_w, v_cmp_w1_k, v_cmp_w2_k, v_cmp_w1_v, v_cmp_w2_v, v_cmp_pe_k, v_cmp_pe_v, v_w_out, v_ffn_norm_w, v_w_gate, v_w_up, v_w_down, v_final_norm_w):
    a = dict(locals())

    shard = {n: a[n][0].astype(_MXU) for n in _BIG}
    early_small = [cmp_w2_k[0], cmp_w2_v[0], conv_w[0]]
    st_early = _split_start([shard[n] for n in _EARLY] + early_small, False, "gather_early_start")
    zero = st_early["token"][0, 0].astype(_MXU)
    st_late = _split_start([shard[_LATE[0]] + zero] + [shard[n] for n in _LATE[1:]], False, "gather_late_start")

    def assemble(n, t):
        return _cols_from_slabs(t) if n in _COL_SHARDED else t.reshape(-1, t.shape[-1])

    p = dict(attn_norm_w=attn_norm_w, conv_b=conv_b, dt_bias=dt_bias, a_log=a_log, d_skip=d_skip, ssd_norm_w=ssd_norm_w,
             cmp_pe_k=cmp_pe_k.reshape(1, -1), cmp_pe_v=cmp_pe_v.reshape(1, -1), ffn_norm_w=ffn_norm_w,
             final_norm_w=final_norm_w.reshape(1, -1))

    def early_weights(after):
        got = _split_wait(st_early, False, tuple(after) + (st_late["token"],), "gather_early_wait")
        w_main, w_small = _w_in_from_slabs(got[0])
        return dict(w_main=w_main, w_small=w_small, cmp_w1_k=assemble("cmp_w1_k", got[1]), cmp_w1_v=assemble("cmp_w1_v", got[2]),
                    cmp_w2_k=assemble("cmp_w2_k", got[3]).astype(_MXU), cmp_w2_v=assemble("cmp_w2_v", got[4]).astype(_MXU),
                    conv_w=_slabs_to_cols(got[5]))

    def late_weights(after):
        got_late = _split_wait(st_late, False, (after,), "gather_late_wait")
        return {n: assemble(n, t) for n, t in zip(_LATE, got_late)}

    def slabs_of(g, n):
        if n == "w_in":
            return _w_in_to_slabs(g["w_main"], g["w_small"])
        return _slabs_from_cols(g[n]) if n in _COL_SHARDED else g[n].reshape(N_DEV, -1, g[n].shape[-1])

    started = []

    def grads_ready(names, g):
        started.append((names, _split_start([slabs_of(g, n) for n in names], True, "scatter_grads_start_%d" % len(started))))
        return started[-1][1]["token"]

    loss_part, grad_x, g = _local_step(x[0], loss_target[0], p, early_weights, late_weights, grads_ready)

    out, after = {}, (started[-1][1]["token"],)
    for i, (names, st) in enumerate(started):
        if i == len(started) - 1:
            after = after + (grad_x,)
        received = _split_wait(st, True, after, "scatter_grads_wait_%d" % i)
        for n, parts in zip(names, received):
            out[n] = _adam_sum(parts, a[n][0], a["m_" + n][0], a["v_" + n][0], "adam_" + n)
        after = (out[names[-1]][0],)

    small_names = _REPLICATED + _SMALL_SHARDED
    partials = [g[n] for n in _REPLICATED] + [_cols_to_slabs(g["conv_w"])] + [
        g[n].reshape(N_DEV, -1, g[n].shape[-1]) for n in ("cmp_w2_k", "cmp_w2_v")]
    gathered = _exchange([loss_part] + partials, [False] * (1 + len(_REPLICATED)) + [True] * len(_SMALL_SHARDED),
                         "exchange_small_grads", after=(received[0],))
    shapes2d = [t.shape[1:] for t in gathered[1:]]
    loss, res_small = _adam_small(gathered[0], gathered[1:],
                                  *[[a[pre + n].reshape(s) for n, s in zip(small_names, shapes2d)] for pre in ("", "m_", "v_")])
    for n, r in zip(small_names, res_small):
        out[n] = r

    outs = [loss[0, 0], grad_x[None]]
    for j in range(4):
        for n in _WEIGHTS:
            outs.append(out[n][j].reshape(a[n].shape))
    return tuple(outs)
```

```python
import functools

import numpy as np
import jax
import jax.numpy as jnp
from jax import lax
from jax.experimental import pallas as pl
from jax.experimental.pallas import tpu as pltpu

F32 = jnp.float32
_MXU = jnp.bfloat16
_HI = lax.Precision.HIGHEST

N_DEV = 8
D_MODEL = 2048
SSD_WIDTH = 1024
ATT_WIDTH = 1024
SSD_HEADS = 16
SSD_P = 64
SSD_N = 128
SSD_L = 128
SSD_G = 2
CONV_CH = 1536
CONV_K = 4
HD = 64
N_HEADS = 16
N_KV = 4
GRP = 4
CMP_HID = 256
SEL_BLOCK = 64
N_SELECT = 16
WINDOW = 512
ROPE_DIM = 16
ROPE_THETA = 500000.0
D_FF = 5632
EPS = 1e-6
NEG = -1e30
FORCE = 1e4
SCALE = HD ** -0.5
D_IN = 5184
W_MAIN = 5120
W_SMALL = 128
VMEM_LIMIT = 52 * 1024 * 1024

ADAM_LR, ADAM_B1, ADAM_B2, ADAM_EPS, ADAM_WD, ADAM_STEP = 0.001, 0.9, 0.999, 1e-08, 0.01, 10


def _pick(n, cands):
    for c in cands:
        if n % c == 0:
            return c
    return n


def _cp(sem=None):
    return pltpu.CompilerParams(dimension_semantics=sem, vmem_limit_bytes=VMEM_LIMIT)


def _sigmoid(x):
    return 1.0 / (1.0 + jnp.exp(-x))


def _dot(a, b, dims, hi=False):
    dn = {"nn": (((1,), (0,)), ((), ())), "nt": (((1,), (1,)), ((), ())), "tn": (((0,), (0,)), ((), ()))}[dims]
    if hi:
        return lax.dot_general(a.astype(F32), b.astype(F32), dn, precision=_HI, preferred_element_type=F32)
    return lax.dot_general(a.astype(_MXU), b.astype(_MXU), dn, preferred_element_type=F32)


LANE = 128
MM_TILE = 1024
MM_K_WHOLE = 2048
MM_K_STEP = 1536
TN_ACC_ELEMS = 3 * 2 ** 20
TN_K_STEP = 512


def _largest_tile(n, cap):
    if n <= cap:
        return n
    best = LANE
    for t in range(LANE, cap + 1, LANE):
        if n % t == 0:
            best = t
    return best


def _mm_tiles(mode, M, N, K):
    if mode == "tn":
        tm = _largest_tile(M, 2 * MM_TILE)
        return tm, _largest_tile(N, TN_ACC_ELEMS // tm), _largest_tile(K, TN_K_STEP)
    tk = K if K <= MM_K_WHOLE else _largest_tile(K, MM_K_STEP)
    return _largest_tile(M, MM_TILE), _largest_tile(N, MM_TILE), tk


def _mm(a, b, mode, out_dtype, name, res=None, after=None):
    if mode == "nn":
        (M, K), N = a.shape, b.shape[1]
    elif mode == "nt":
        (M, K), N = a.shape, b.shape[0]
    else:
        (K, M), N = a.shape, b.shape[1]
    tm, tn, tk = _mm_tiles(mode, M, N, K)
    nk = K // tk
    a_spec = pl.BlockSpec((tk, tm), lambda i, j, k: (k, i)) if mode == "tn" else pl.BlockSpec((tm, tk), lambda i, j, k: (i, k))
    b_spec = pl.BlockSpec((tn, tk), lambda i, j, k: (j, k)) if mode == "nt" else pl.BlockSpec((tk, tn), lambda i, j, k: (k, j))
    o_spec = pl.BlockSpec((tm, tn), lambda i, j, k: (i, j))

    def finish(r, r_ref, o_ref):
        if res is not None:
            r = r + r_ref[...].astype(F32)
        o_ref[...] = r.astype(out_dtype)

    def body_one_step(*refs):
        a_ref, b_ref, o_ref = refs[0], refs[1], refs[-1]
        finish(_dot(a_ref[...], b_ref[...], mode), refs[2], o_ref)

    def body(*refs):
        a_ref, b_ref, o_ref, acc = refs[0], refs[1], refs[-2], refs[-1]
        k = pl.program_id(2)

        @pl.when(k == 0)
        def _():
            acc[...] = jnp.zeros_like(acc)

        acc[...] += _dot(a_ref[...], b_ref[...], mode)

        @pl.when(k == nk - 1)
        def _():
            finish(acc[...], refs[2], o_ref)

    ins, specs = [a, b], [a_spec, b_spec]
    if res is not None:
        ins.append(res)
        specs.append(o_spec)
    if after is not None:
        ins.append(after)
        specs.append(pl.BlockSpec(memory_space=pl.ANY))
    return pl.pallas_call(
        body_one_step if nk == 1 else body, name=name, grid=(M // tm, N // tn, nk), in_specs=specs, out_specs=o_spec,
        out_shape=jax.ShapeDtypeStruct((M, N), out_dtype), scratch_shapes=[] if nk == 1 else [pltpu.VMEM((tm, tn), F32)],
        compiler_params=_cp(("parallel", "parallel", "arbitrary")))(*ins)


def _ffn_up(v, w_gate, w_up):
    S, D = v.shape
    F = w_gate.shape[1]
    tm, tn = _largest_tile(S, MM_TILE), _largest_tile(F, MM_TILE // 2)

    def body(v_ref, wg_ref, wu_ref, gt_ref, up_ref, act_ref):
        vv = v_ref[...]
        g = _dot(vv, wg_ref[...], "nn")
        u = _dot(vv, wu_ref[...], "nn")
        gt_ref[...] = g
        up_ref[...] = u
        act_ref[...] = (g * _sigmoid(g) * u).astype(act_ref.dtype)

    o_spec = pl.BlockSpec((tm, tn), lambda i, j: (i, j))
    w_spec = pl.BlockSpec((D, tn), lambda i, j: (0, j))
    return pl.pallas_call(
        body, name="ffn_up", grid=(S // tm, F // tn),
        in_specs=[pl.BlockSpec((tm, D), lambda i, j: (i, 0)), w_spec, w_spec], out_specs=[o_spec, o_spec, o_spec],
        out_shape=[jax.ShapeDtypeStruct((S, F), F32), jax.ShapeDtypeStruct((S, F), F32), jax.ShapeDtypeStruct((S, F), _MXU)],
        compiler_params=_cp(("parallel", "parallel")))(v, w_gate, w_up)


def _ffn_dact(dh2, w_down, gt, up):
    S, D = dh2.shape
    F = w_down.shape[0]
    tm, tn = _largest_tile(S, MM_TILE), _largest_tile(F, MM_TILE // 2)

    def body(d_ref, w_ref, gt_ref, up_ref, dg_ref, du_ref):
        da, g, u = _dot(d_ref[...], w_ref[...], "nt"), gt_ref[...], up_ref[...]
        s = _sigmoid(g)
        dg_ref[...] = (da * u * (s * (1.0 + g * (1.0 - s)))).astype(dg_ref.dtype)
        du_ref[...] = (da * (g * s)).astype(du_ref.dtype)

    o_spec = pl.BlockSpec((tm, tn), lambda i, j: (i, j))
    return pl.pallas_call(
        body, name="ffn_dact", grid=(S // tm, F // tn),
        in_specs=[pl.BlockSpec((tm, D), lambda i, j: (i, 0)), pl.BlockSpec((tn, D), lambda i, j: (j, 0)), o_spec, o_spec],
        out_specs=[o_spec, o_spec],
        out_shape=[jax.ShapeDtypeStruct((S, F), _MXU), jax.ShapeDtypeStruct((S, F), _MXU)],
        compiler_params=_cp(("parallel", "parallel")))(dh2, w_down, gt, up)


def _rms_fwd(x, w, name):
    S, D = x.shape
    tr = _pick(S, (256, 128))

    def body(x_ref, w_ref, xn_ref, rs_ref):
        xv = x_ref[...]
        rs = lax.rsqrt(jnp.mean(xv * xv, axis=-1, keepdims=True) + EPS)
        xn_ref[...] = ((xv * rs) * w_ref[...]).astype(xn_ref.dtype)
        rs_ref[...] = rs

    return pl.pallas_call(
        body, name=name, grid=(S // tr,),
        in_specs=[pl.BlockSpec((tr, D), lambda i: (i, 0)), pl.BlockSpec((1, D), lambda i: (0, 0))],
        out_specs=[pl.BlockSpec((tr, D), lambda i: (i, 0)), pl.BlockSpec((tr, 1), lambda i: (i, 0))],
        out_shape=[jax.ShapeDtypeStruct((S, D), _MXU), jax.ShapeDtypeStruct((S, 1), F32)],
        compiler_params=_cp(("parallel",)))(x, w)


def _rms_bwd(dyn, x, rs, w, res, name):
    S, D = x.shape
    tr = _pick(S, (256, 128))

    def body(dy_ref, x_ref, rs_ref, w_ref, res_ref, dx_ref, dxb_ref, dw_ref):
        @pl.when(pl.program_id(0) == 0)
        def _():
            dw_ref[...] = jnp.zeros_like(dw_ref)

        dy, r = dy_ref[...].astype(F32), rs_ref[...]
        xhat = x_ref[...] * r
        dw_ref[...] += jnp.sum(dy * xhat, axis=0, keepdims=True)
        dxhat = dy * w_ref[...]
        dx = res_ref[...] + r * (dxhat - xhat * jnp.mean(dxhat * xhat, axis=-1, keepdims=True))
        dx_ref[...] = dx
        dxb_ref[...] = dx.astype(dxb_ref.dtype)

    row = pl.BlockSpec((tr, D), lambda i: (i, 0))
    vec = pl.BlockSpec((1, D), lambda i: (0, 0))
    return pl.pallas_call(
        body, name=name, grid=(S // tr,),
        in_specs=[row, row, pl.BlockSpec((tr, 1), lambda i: (i, 0)), vec, row], out_specs=[row, row, vec],
        out_shape=[jax.ShapeDtypeStruct((S, D), F32), jax.ShapeDtypeStruct((S, D), _MXU), jax.ShapeDtypeStruct((1, D), F32)],
        compiler_params=_cp(("arbitrary",)))(dyn, x, rs, w, res)


def _final_loss(h2, w, tgt):
    S, D = h2.shape
    tr = _pick(S, (256, 128))

    def body(h_ref, w_ref, t_ref, loss_ref, dh_ref, dhb_ref, dw_ref):
        @pl.when(pl.program_id(0) == 0)
        def _():
            dw_ref[...] = jnp.zeros_like(dw_ref)
            loss_ref[...] = jnp.zeros_like(loss_ref)

        hv, wv = h_ref[...], w_ref[...]
        rs = lax.rsqrt(jnp.mean(hv * hv, axis=-1, keepdims=True) + EPS)
        xhat = hv * rs
        err = xhat * wv - t_ref[...]
        row = jnp.mean(err * err, axis=-1, keepdims=True)
        loss_ref[...] += jnp.broadcast_to(0.5 * jnp.sum(row, axis=0, keepdims=True), loss_ref.shape)
        dy = err * (1.0 / D)
        dw_ref[...] += jnp.sum(dy * xhat, axis=0, keepdims=True)
        dxhat = dy * wv
        dh = rs * (dxhat - xhat * jnp.mean(dxhat * xhat, axis=-1, keepdims=True))
        dh_ref[...] = dh
        dhb_ref[...] = dh.astype(dhb_ref.dtype)

    row = pl.BlockSpec((tr, D), lambda i: (i, 0))
    vec = pl.BlockSpec((1, D), lambda i: (0, 0))
    return pl.pallas_call(
        body, name="final_loss", grid=(S // tr,), in_specs=[row, vec, row],
        out_specs=[pl.BlockSpec((1, LANE), lambda i: (0, 0)), row, row, vec],
        out_shape=[jax.ShapeDtypeStruct((1, LANE), F32), jax.ShapeDtypeStruct((S, D), F32), jax.ShapeDtypeStruct((S, D), _MXU),
                   jax.ShapeDtypeStruct((1, D), F32)],
        compiler_params=_cp(("arbitrary",)))(h2, w, tgt)


def _shift_rows(x, k, rows):
    if k == 0:
        return x
    S = x.shape[0]
    r = pltpu.roll(x, k % S, axis=0)
    ok = (rows >= k) if k > 0 else (rows < S + k)
    return jnp.where(ok, r, 0.0)


XBC_COL0 = SSD_WIDTH // 128


def _conv_fwd(proj, conv_w, conv_b):
    S = proj.shape[0]
    nct = CONV_CH // 128

    def body(x_ref, w_ref, b_ref, o_ref):
        x = x_ref[...]
        rows = lax.broadcasted_iota(jnp.int32, x.shape, 0)
        c = b_ref[...] + w_ref[3:4, :] * x
        for k in range(1, CONV_K):
            c = c + w_ref[3 - k:4 - k, :] * _shift_rows(x, k, rows)
        o_ref[...] = c * _sigmoid(c)

    return pl.pallas_call(
        body, name="conv_fwd", grid=(nct,),
        in_specs=[pl.BlockSpec((S, 128), lambda j: (0, XBC_COL0 + j)), pl.BlockSpec((CONV_K, 128), lambda j: (0, j)),
                  pl.BlockSpec((1, 128), lambda j: (0, j))],
        out_specs=pl.BlockSpec((S, 128), lambda j: (0, j)),
        out_shape=jax.ShapeDtypeStruct((S, CONV_CH), F32), compiler_params=_cp(("parallel",)))(proj, conv_w, conv_b)


def _conv_bwd(proj, conv_w, conv_b, dxa):
    S = proj.shape[0]
    nct = CONV_CH // 128

    def body(x_ref, w_ref, b_ref, d_ref, dx_ref, dw_ref, db_ref):
        x = x_ref[...]
        rows = lax.broadcasted_iota(jnp.int32, x.shape, 0)
        xs = [_shift_rows(x, k, rows) for k in range(CONV_K)]
        c = b_ref[...] + w_ref[3:4, :] * x
        for k in range(1, CONV_K):
            c = c + w_ref[3 - k:4 - k, :] * xs[k]
        s = _sigmoid(c)
        dc = d_ref[...] * (s * (1.0 + c * (1.0 - s)))
        dx = w_ref[3:4, :] * dc
        for k in range(1, CONV_K):
            dx = dx + w_ref[3 - k:4 - k, :] * _shift_rows(dc, -k, rows)
        dx_ref[...] = dx.astype(dx_ref.dtype)
        for k in range(CONV_K):
            dw_ref[3 - k:4 - k, :] = jnp.sum(dc * xs[k], axis=0, keepdims=True)
        db_ref[...] = jnp.sum(dc, axis=0, keepdims=True)

    col = pl.BlockSpec((S, 128), lambda j: (0, j))
    return pl.pallas_call(
        body, name="conv_bwd", grid=(nct,),
        in_specs=[pl.BlockSpec((S, 128), lambda j: (0, XBC_COL0 + j)), pl.BlockSpec((CONV_K, 128), lambda j: (0, j)),
                  pl.BlockSpec((1, 128), lambda j: (0, j)), col],
        out_specs=[col, pl.BlockSpec((CONV_K, 128), lambda j: (0, j)), pl.BlockSpec((1, 128), lambda j: (0, j))],
        out_shape=[jax.ShapeDtypeStruct((S, CONV_CH), _MXU), jax.ShapeDtypeStruct((CONV_K, CONV_CH), F32),
                   jax.ShapeDtypeStruct((1, CONV_CH), F32)],
        compiler_params=_cp(("parallel",)))(proj, conv_w, conv_b, dxa)


def _ssd_consts():
    L = SSD_L
    r = lax.broadcasted_iota(jnp.int32, (L, L), 0)
    c = lax.broadcasted_iota(jnp.int32, (L, L), 1)
    causal = r >= c
    upper = (r <= c).astype(F32)
    hr = lax.broadcasted_iota(jnp.int32, (SSD_HEADS, SSD_WIDTH), 0)
    hc = lax.broadcasted_iota(jnp.int32, (SSD_HEADS, SSD_WIDTH), 1)
    expand = (lax.shift_right_logical(hc, 6) == hr).astype(F32)
    return causal, causal.astype(F32), upper, expand


def _softplus(x):
    return jnp.maximum(x, 0.0) + jnp.log(1.0 + jnp.exp(-jnp.abs(x)))


def _ssd_scalars(dtr, dt_bias, a_log, tri, upper, expand):
    dt = _softplus(dtr + dt_bias)
    A = -jnp.exp(a_log)
    adt = dt * A
    acum = _dot(tri, adt, "nn", hi=True)
    acum_t = _dot(adt, upper, "tn", hi=True)
    alast = acum[SSD_L - 1:SSD_L, :]
    e = jnp.exp(acum)
    wdec = jnp.exp(alast - acum)
    gam = jnp.exp(alast)
    ex = lambda t: _dot(t, expand, "nn", hi=True)
    gam8 = jnp.broadcast_to(gam, (8, SSD_HEADS))
    return dt, A, acum, acum_t, e, wdec, gam, ex(dt), ex(e), ex(wdec), ex(gam8)[0:1, :]


def _ssd_fwd(proj, proj_small, xa, dt_bias, a_log, d_skip, norm_w):
    S = proj.shape[0]
    L, N, W = SSD_L, SSD_N, SSD_WIDTH
    nc = S // L

    def body(z_ref, xa_ref, dtr_ref, dtb_ref, al_ref, dsk_ref, nw_ref, yo_ref, y_ref, rs_ref, hs_ref, h_scr, y_scr):
        @pl.when(pl.program_id(0) == 0)
        def _():
            h_scr[...] = jnp.zeros_like(h_scr)

        causal, tri, upper, expand = _ssd_consts()
        dt, A, acum, acum_t, e, wdec, gam, dtE, eE, wE, gamE = _ssd_scalars(dtr_ref[:, 0:SSD_HEADS], dtb_ref[...], al_ref[...], tri, upper, expand)
        xs = xa_ref[:, 0:W]
        X = xs * dtE
        XW = X * wE
        hs_ref[0] = h_scr[...]
        for g in range(SSD_G):
            gs = slice(g * 512, (g + 1) * 512)
            Bg = xa_ref[:, W + g * N:W + (g + 1) * N]
            Cg = xa_ref[:, W + SSD_G * N + g * N:W + SSD_G * N + (g + 1) * N]
            Hg = h_scr[:, gs]
            CB = _dot(Cg, Bg, "nt")
            yoff = _dot(Cg, Hg, "nn") * eE[:, gs]
            st = _dot(Bg, XW[:, gs], "tn")
            for j in range(8):
                h = g * 8 + j
                hsl = slice(h * SSD_P, (h + 1) * SSD_P)
                lam = jnp.exp(jnp.where(causal, acum[:, h:h + 1] - acum_t[h:h + 1, :], -jnp.inf))
                y_scr[:, hsl] = _dot(CB * lam, X[:, hsl], "nn") + yoff[:, j * SSD_P:(j + 1) * SSD_P]
            h_scr[:, gs] = gamE[:, gs] * Hg + st
        dskE = _dot(jnp.broadcast_to(dsk_ref[...], (8, SSD_HEADS)), expand, "nn", hi=True)[0:1, :]
        y = y_scr[...] + dskE * xs
        y_ref[...] = y
        zv = z_ref[...]
        yg = y * (zv * _sigmoid(zv))
        rs = lax.rsqrt(jnp.mean(yg * yg, axis=-1, keepdims=True) + EPS)
        rs_ref[...] = rs
        yo_ref[...] = ((yg * rs) * nw_ref[...]).astype(yo_ref.dtype)

    p16 = pl.BlockSpec((1, SSD_HEADS), lambda c: (0, 0))
    return pl.pallas_call(
        body, name="ssd_fwd", grid=(nc,),
        in_specs=[pl.BlockSpec((L, W), lambda c: (c, 0)), pl.BlockSpec((L, CONV_CH), lambda c: (c, 0)),
                  pl.BlockSpec((L, W_SMALL), lambda c: (c, 0)), p16, p16, p16, pl.BlockSpec((1, W), lambda c: (0, 0))],
        out_specs=[pl.BlockSpec((L, W), lambda c: (c, 0)), pl.BlockSpec((L, W), lambda c: (c, 0)),
                   pl.BlockSpec((L, 1), lambda c: (c, 0)), pl.BlockSpec((1, N, W), lambda c: (c, 0, 0))],
        out_shape=[jax.ShapeDtypeStruct((S, W), _MXU), jax.ShapeDtypeStruct((S, W), F32), jax.ShapeDtypeStruct((S, 1), F32),
                   jax.ShapeDtypeStruct((nc, N, W), F32)],
        scratch_shapes=[pltpu.VMEM((N, W), F32), pltpu.VMEM((L, W), F32)],
        compiler_params=_cp(("arbitrary",)))(proj, xa, proj_small, dt_bias, a_log, d_skip, norm_w)


def _ssd_bwd(dmixed, proj, proj_small, xa, y, rs2, hs, dt_bias, a_log, d_skip, norm_w):
    S = proj.shape[0]
    L, N, W, H = SSD_L, SSD_N, SSD_WIDTH, SSD_HEADS
    nc = S // L

    def body(dyo_ref, z_ref, xa_ref, dtr_ref, y_ref, rs_ref, hs_ref, dtb_ref, al_ref, dsk_ref, nw_ref,
             dz_ref, dxa_ref, ddtr_ref, ddtb_ref, dal_ref, ddsk_ref, dnw_ref, dh_scr, dx_scr):
        @pl.when(pl.program_id(0) == 0)
        def _():
            dh_scr[...] = jnp.zeros_like(dh_scr)
            ddtb_ref[...] = jnp.zeros_like(ddtb_ref)
            dal_ref[...] = jnp.zeros_like(dal_ref)
            ddsk_ref[...] = jnp.zeros_like(ddsk_ref)
            dnw_ref[...] = jnp.zeros_like(dnw_ref)

        causal, tri, upper, expand = _ssd_consts()
        heads = lambda t: _dot(t, expand, "nt", hi=True)
        onehot = lambda h: (lax.broadcasted_iota(jnp.int32, (1, H), 1) == h).astype(F32)

        zv, yv, rs = z_ref[...], y_ref[...], rs_ref[...]
        sz = _sigmoid(zv)
        zs = zv * sz
        xhat = (yv * zs) * rs
        dyo = dyo_ref[...].astype(F32)
        dnw_ref[...] += jnp.sum(dyo * xhat, axis=0, keepdims=True)
        dxhat = dyo * nw_ref[...]
        dyg = rs * (dxhat - xhat * jnp.mean(dxhat * xhat, axis=-1, keepdims=True))
        dz_ref[...] = (dyg * yv * (sz * (1.0 + zv * (1.0 - sz)))).astype(dz_ref.dtype)
        dy = dyg * zs

        dtr = dtr_ref[:, 0:H]
        dt, A, acum, acum_t, e, wdec, gam, dtE, eE, wE, gamE = _ssd_scalars(dtr, dtb_ref[...], al_ref[...], tri, upper, expand)
        xs = xa_ref[:, 0:W]
        X = xs * dtE
        XW = X * wE
        dskE = _dot(jnp.broadcast_to(dsk_ref[...], (8, H)), expand, "nn", hi=True)[0:1, :]
        ddsk_ref[...] += heads(jnp.broadcast_to(jnp.sum(dy * xs, axis=0, keepdims=True), (8, W)))[0:1, :]

        dYe = dy * eE
        dacum = jnp.zeros((L, H), F32)
        de_full = []
        dw_full = []
        dgam_full = []
        for g in range(SSD_G):
            gs = slice(g * 512, (g + 1) * 512)
            Bg = xa_ref[:, W + g * N:W + (g + 1) * N]
            Cg = xa_ref[:, W + SSD_G * N + g * N:W + SSD_G * N + (g + 1) * N]
            Hg = hs_ref[0, :, gs]
            dHn = dh_scr[:, gs]
            CH = _dot(Cg, Hg, "nn")
            de_full.append(dy[:, gs] * CH)
            dC = _dot(dYe[:, gs], Hg, "nt")
            dHs = gamE[:, gs] * dHn + _dot(Cg, dYe[:, gs], "tn")
            dgam_full.append(jnp.sum(dHn * Hg, axis=0, keepdims=True))
            BdS = _dot(Bg, dHn, "nn")
            dB = _dot(XW[:, gs], dHn, "nt")
            dx_scr[:, gs] = BdS * wE[:, gs]
            dw_full.append(BdS * X[:, gs])
            CB = _dot(Cg, Bg, "nt")
            dCB = jnp.zeros((L, L), F32)
            for j in range(8):
                h = g * 8 + j
                hsl = slice(h * SSD_P, (h + 1) * SSD_P)
                lam = jnp.exp(jnp.where(causal, acum[:, h:h + 1] - acum_t[h:h + 1, :], -jnp.inf))
                M = CB * lam
                dM = _dot(dy[:, hsl], X[:, hsl], "nt")
                dx_scr[:, hsl] += _dot(M, dy[:, hsl], "tn")
                dCB = dCB + dM * lam
                Q = dM * M
                rowsum = jnp.sum(Q, axis=1, keepdims=True)
                colsum = _dot(Q, jnp.ones((L, 8), F32), "tn", hi=True)[:, 0:1]
                dacum = dacum + (rowsum - colsum) * onehot(h)
            dC = dC + _dot(dCB, Bg, "nn")
            dB = dB + _dot(dCB, Cg, "tn")
            dxa_ref[:, W + g * N:W + (g + 1) * N] = dB
            dxa_ref[:, W + SSD_G * N + g * N:W + SSD_G * N + (g + 1) * N] = dC
            dh_scr[:, gs] = dHs

        de16 = heads(jnp.concatenate(de_full, axis=1))
        dw16 = heads(jnp.concatenate(dw_full, axis=1))
        dgam16 = heads(jnp.broadcast_to(jnp.concatenate(dgam_full, axis=1), (8, W)))[0:1, :]
        dacum = dacum + de16 * e - dw16 * wdec
        dlast = jnp.sum(dw16 * wdec, axis=0, keepdims=True) + dgam16 * gam
        lastrow = (lax.broadcasted_iota(jnp.int32, (L, 1), 0) == L - 1).astype(F32)
        dacum = dacum + lastrow * dlast
        da = _dot(tri, dacum, "tn", hi=True)
        dX = dx_scr[...]
        ddt = da * A + heads(dX * xs)
        dA = jnp.sum(da * dt, axis=0, keepdims=True)
        dal_ref[...] += dA * A
        ddtr = ddt * _sigmoid(dtr + dtb_ref[...])
        ddtb_ref[...] += jnp.sum(ddtr, axis=0, keepdims=True)
        ddtr_ref[...] = ddtr
        dxa_ref[:, 0:W] = dX * dtE + dy * dskE

    p16 = pl.BlockSpec((1, H), lambda c: (0, 0))
    rev = lambda c: (nc - 1 - c, 0)
    return pl.pallas_call(
        body, name="ssd_bwd", grid=(nc,),
        in_specs=[pl.BlockSpec((L, W), rev), pl.BlockSpec((L, W), rev), pl.BlockSpec((L, CONV_CH), rev),
                  pl.BlockSpec((L, W_SMALL), rev), pl.BlockSpec((L, W), rev), pl.BlockSpec((L, 1), rev),
                  pl.BlockSpec((1, N, W), lambda c: (nc - 1 - c, 0, 0)), p16, p16, p16, pl.BlockSpec((1, W), lambda c: (0, 0))],
        out_specs=[pl.BlockSpec((L, W), rev), pl.BlockSpec((L, CONV_CH), rev), pl.BlockSpec((L, H), rev),
                   p16, p16, p16, pl.BlockSpec((1, W), lambda c: (0, 0))],
        out_shape=[jax.ShapeDtypeStruct((S, W), _MXU), jax.ShapeDtypeStruct((S, CONV_CH), F32), jax.ShapeDtypeStruct((S, H), F32),
                   jax.ShapeDtypeStruct((1, H), F32), jax.ShapeDtypeStruct((1, H), F32), jax.ShapeDtypeStruct((1, H), F32),
                   jax.ShapeDtypeStruct((1, W), F32)],
        scratch_shapes=[pltpu.VMEM((N, W), F32), pltpu.VMEM((L, W), F32)],
        compiler_params=_cp(("arbitrary",)))(dmixed, proj, xa, proj_small, y, rs2, hs, dt_bias, a_log, d_skip, norm_w)


def _rope_tables(S):
    inv = 1.0 / (ROPE_THETA ** (jnp.arange(0, ROPE_DIM, 2, dtype=F32) / ROPE_DIM))
    ang = jnp.arange(S, dtype=F32)[:, None] * inv[None, :]
    cos, sin = jnp.cos(ang), jnp.sin(ang)
    half = ROPE_DIM // 2
    c64 = jnp.concatenate([cos, cos, jnp.ones((S, HD - ROPE_DIM), F32)], axis=1)
    s64 = jnp.concatenate([sin, sin, jnp.zeros((S, HD - ROPE_DIM), F32)], axis=1)
    del half
    return jnp.concatenate([c64, c64], axis=1), jnp.concatenate([s64, s64], axis=1)


def _rope(xs, blk0, width, cos, sin, sign, out_dtype, name, extra=None):
    S = xs[0].shape[0]
    tr = _pick(S, (512, 256, 128))
    nx = len(xs)

    def body(*refs):
        x_refs, c_ref, s_ref = refs[:nx], refs[nx], refs[nx + 1]
        e_ref = refs[nx + 2] if extra is not None else None
        o_ref = refs[-1]
        cv, sv = c_ref[...], s_ref[...] * sign
        lane = lax.broadcasted_iota(jnp.int32, (tr, 128), 1)
        first = (lane & (HD - 1)) < (ROPE_DIM // 2)
        for j in range(2):
            cs = slice(j * 128, (j + 1) * 128)
            xv = x_refs[0][:, cs].astype(F32)
            for r in x_refs[1:]:
                xv = xv + r[:, cs].astype(F32)
            rot = jnp.where(first, -pltpu.roll(xv, 128 - ROPE_DIM // 2, axis=1), pltpu.roll(xv, ROPE_DIM // 2, axis=1))
            out = xv * cv + rot * sv
            if extra is not None:
                out = out + e_ref[:, cs].astype(F32)
            o_ref[:, cs] = out.astype(out_dtype)

    t128 = pl.BlockSpec((tr, 128), lambda i, j: (i, 0))
    oblk = pl.BlockSpec((tr, 256), lambda i, j: (i, j))
    specs = [pl.BlockSpec((tr, 256), lambda i, j: (i, blk0 + j))] * nx + [t128, t128]
    ins = list(xs) + [cos, sin]
    if extra is not None:
        ins.append(extra[0])
        eb = extra[1]
        specs.append(pl.BlockSpec((tr, 256), lambda i, j: (i, eb + j)))
    return pl.pallas_call(
        body, name=name, grid=(S // tr, width // 256), in_specs=specs, out_specs=oblk,
        out_shape=jax.ShapeDtypeStruct((S, width), out_dtype), compiler_params=_cp(("parallel", "parallel")))(*ins)


def _rotate128(xv, cv, sv, first):
    rot = jnp.where(first, -pltpu.roll(xv, 128 - ROPE_DIM // 2, axis=1), pltpu.roll(xv, ROPE_DIM // 2, axis=1))
    return xv * cv + rot * sv


def _kv_prep(proj, cos, sin, tk):
    S = proj.shape[0]
    NB = S // SEL_BLOCK

    def body(ks_ref, vs_ref, kw_ref, vw_ref, c_ref, s_ref, *outs):
        cv, sv = c_ref[...], s_ref[...]
        lane = lax.broadcasted_iota(jnp.int32, (tk, 128), 1)
        first = (lane & (HD - 1)) < (ROPE_DIM // 2)
        key = pl.program_id(0) * tk + lax.broadcasted_iota(jnp.int32, (tk, NB), 0)
        onehot = (lax.shift_right_logical(key, 6) == lax.broadcasted_iota(jnp.int32, (tk, NB), 1)).astype(F32)
        for j, (ref, rotated) in enumerate(((ks_ref, True), (vs_ref, False), (kw_ref, True), (vw_ref, False))):
            nat, blk = outs[2 * j], outs[2 * j + 1]
            for half in range(2):
                xv = ref[:, half * 128:(half + 1) * 128]
                if rotated:
                    xv = _rotate128(xv, cv, sv, first)
                for e in range(2):
                    h = 2 * half + e
                    piece = xv[:, e * HD:(e + 1) * HD]
                    nat[h] = (jnp.concatenate([piece, onehot], axis=1) if j == 0 else piece).astype(nat.dtype)
                    blk[h, 0] = piece.T.astype(blk.dtype)

    col = lambda b: pl.BlockSpec((tk, 256), lambda i: (i, b))
    t128 = pl.BlockSpec((tk, 128), lambda i: (i, 0))
    nat_spec = lambda w: pl.BlockSpec((N_KV, tk, w), lambda i: (0, i, 0))
    blk_spec = pl.BlockSpec((N_KV, 1, HD, tk), lambda i: (0, i, 0, 0))
    nat_shape = lambda w: jax.ShapeDtypeStruct((N_KV, S, w), _MXU)
    blk_shape = jax.ShapeDtypeStruct((N_KV, S // tk, HD, tk), _MXU)
    widths = (HD + NB, HD, HD, HD)
    res = pl.pallas_call(
        body, name="kv_prep", grid=(S // tk,), in_specs=[col(KSB), col(VSB), col(KWB), col(VWB), t128, t128],
        out_specs=[s for w in widths for s in (nat_spec(w), blk_spec)],
        out_shape=[s for w in widths for s in (nat_shape(w), blk_shape)],
        compiler_params=_cp(("parallel",)))(proj, proj, proj, proj, cos, sin)
    return dict(ks_ext=res[0], ks_t=res[1], vs=res[2], vs_t=res[3], kw=res[4], kw_t=res[5], vw=res[6], vw_t=res[7])


def _dkv_post(dks, dvs, dkw, dvw, cos, sin):
    S = dks.shape[1]
    tr = _pick(S, (512, 256, 128))

    def body(dks_ref, dvs_ref, dkw_ref, dvw_ref, c_ref, s_ref, o_ref):
        cv, sv = c_ref[...], -s_ref[...]
        lane = lax.broadcasted_iota(jnp.int32, (tr, 128), 1)
        first = (lane & (HD - 1)) < (ROPE_DIM // 2)
        for j, (ref, rotated) in enumerate(((dks_ref, True), (dvs_ref, False), (dkw_ref, True), (dvw_ref, False))):
            for half in range(2):
                xv = jnp.concatenate([ref[2 * half], ref[2 * half + 1]], axis=1)
                if rotated:
                    xv = _rotate128(xv, cv, sv, first)
                o_ref[:, j * 256 + half * 128:j * 256 + (half + 1) * 128] = xv.astype(o_ref.dtype)

    hm = pl.BlockSpec((N_KV, tr, HD), lambda i: (0, i, 0))
    t128 = pl.BlockSpec((tr, 128), lambda i: (i, 0))
    return pl.pallas_call(
        body, name="dkv_post", grid=(S // tr,), in_specs=[hm, hm, hm, hm, t128, t128],
        out_specs=pl.BlockSpec((tr, 4 * 256), lambda i: (i, 0)), out_shape=jax.ShapeDtypeStruct((S, 4 * 256), _MXU),
        compiler_params=_cp(("parallel",)))(dks, dvs, dkw, dvw, cos, sin)


def _compress_fwd(R, pe, w1, w2):
    NC = R.shape[1]
    half = 16 * HD

    def body(r_ref, pe_ref, w1_ref, w2_ref, o_ref, hid_ref):
        r = r_ref[0]
        a = _dot(r + pe_ref[:, 0:half], w1_ref[0:half, :], "nn")
        b = _dot(r + pe_ref[:, half:2 * half], w1_ref[half:2 * half, :], "nn")
        hid = a + pltpu.roll(b, NC - 1, axis=0)
        hid_ref[0] = hid
        out = _dot(hid * _sigmoid(hid), w2_ref[...], "nn")
        rows = lax.broadcasted_iota(jnp.int32, out.shape, 0)
        o_ref[0] = jnp.where(rows < NC - 1, out, 0.0).astype(o_ref.dtype)

    return pl.pallas_call(
        body, name="compress_fwd", grid=(N_KV,),
        in_specs=[pl.BlockSpec((1, NC, half), lambda h: (h, 0, 0)), pl.BlockSpec((1, 2 * half), lambda h: (0, 0)),
                  pl.BlockSpec((2 * half, CMP_HID), lambda h: (0, 0)), pl.BlockSpec((CMP_HID, HD), lambda h: (0, 0))],
        out_specs=[pl.BlockSpec((1, NC, HD), lambda h: (h, 0, 0)), pl.BlockSpec((1, NC, CMP_HID), lambda h: (h, 0, 0))],
        out_shape=[jax.ShapeDtypeStruct((N_KV, NC, HD), _MXU), jax.ShapeDtypeStruct((N_KV, NC, CMP_HID), F32)],
        compiler_params=_cp(("parallel",)))(R, pe, w1, w2)


def _compress_bwd(R, pe, w1, w2, hid, dout):
    NC = R.shape[1]
    half = 16 * HD

    def body(r_ref, pe_ref, w1_ref, w2_ref, hid_ref, do_ref, dr_ref, dw1_ref, dw2_ref, dpe_ref):
        @pl.when(pl.program_id(0) == 0)
        def _():
            dw1_ref[...] = jnp.zeros_like(dw1_ref)
            dw2_ref[...] = jnp.zeros_like(dw2_ref)
            dpe_ref[...] = jnp.zeros_like(dpe_ref)

        r, hv, do = r_ref[0], hid_ref[0], do_ref[0]
        s = _sigmoid(hv)
        dw2_ref[...] += _dot(hv * s, do, "tn")
        dhid = _dot(do, w2_ref[...], "nt") * (s * (1.0 + hv * (1.0 - s)))
        rows = lax.broadcasted_iota(jnp.int32, dhid.shape, 0)
        dhid = jnp.where(rows < NC - 1, dhid, 0.0)
        dhid_dn = pltpu.roll(dhid, 1, axis=0)
        dw1_ref[0:half, :] += _dot(r + pe_ref[:, 0:half], dhid, "tn")
        dw1_ref[half:2 * half, :] += _dot(r + pe_ref[:, half:2 * half], dhid_dn, "tn")
        dxt = _dot(dhid, w1_ref[0:half, :], "nt")
        dxb = _dot(dhid_dn, w1_ref[half:2 * half, :], "nt")
        dr_ref[0] = dxt + dxb
        dpe_ref[:, 0:half] += jnp.sum(dxt, axis=0, keepdims=True)
        dpe_ref[:, half:2 * half] += jnp.sum(dxb, axis=0, keepdims=True)

    return pl.pallas_call(
        body, name="compress_bwd", grid=(N_KV,),
        in_specs=[pl.BlockSpec((1, NC, half), lambda h: (h, 0, 0)), pl.BlockSpec((1, 2 * half), lambda h: (0, 0)),
                  pl.BlockSpec((2 * half, CMP_HID), lambda h: (0, 0)), pl.BlockSpec((CMP_HID, HD), lambda h: (0, 0)),
                  pl.BlockSpec((1, NC, CMP_HID), lambda h: (h, 0, 0)), pl.BlockSpec((1, NC, HD), lambda h: (h, 0, 0))],
        out_specs=[pl.BlockSpec((1, NC, half), lambda h: (h, 0, 0)), pl.BlockSpec((2 * half, CMP_HID), lambda h: (0, 0)),
                   pl.BlockSpec((CMP_HID, HD), lambda h: (0, 0)), pl.BlockSpec((1, 2 * half), lambda h: (0, 0))],
        out_shape=[jax.ShapeDtypeStruct((N_KV, NC, half), F32), jax.ShapeDtypeStruct((2 * half, CMP_HID), F32),
                   jax.ShapeDtypeStruct((CMP_HID, HD), F32), jax.ShapeDtypeStruct((1, 2 * half), F32)],
        compiler_params=_cp(("arbitrary",)))(R, pe, w1, w2, hid, dout)


def _attn_cfg(S, Sk, mode):
    tq = _pick(S, (256, 128))
    tk = Sk if mode == "cmp" else _pick(Sk, (256, 128))
    return tq, tk


def _block_start(kb, tk):
    return kb * tk if isinstance(kb, int) else pl.multiple_of(kb * tk, tk)


def _pipelined_key_blocks(mode, q0, tq, tk, produce, consume):
    if mode == "cmp":
        produce(0, True, 0)
        consume(0, 0)
        return
    if mode == "sel":
        first, n_plain, plain_masked = 0, q0 // tk, False
    else:
        first = jnp.maximum(q0 - (WINDOW - 1), 0) // tk
        n_plain, plain_masked = (q0 + tq - 1) // tk - first, True
    last = first + n_plain
    pairs = jnp.maximum(n_plain - 1, 0) // 2

    @pl.when(n_plain >= 1)
    def _():
        produce(first, plain_masked, 0)

    def two(j, carry):
        kb = first + 2 * j
        produce(kb + 1, plain_masked, 1)
        consume(kb, 0)
        produce(kb + 2, plain_masked, 0)
        consume(kb + 1, 1)
        return carry

    lax.fori_loop(0, pairs, two, 0)
    kb = first + 2 * pairs
    left = n_plain - 2 * pairs

    @pl.when(left == 2)
    def _():
        produce(kb + 1, plain_masked, 1)
        consume(kb, 0)
        produce(last, True, 0)
        consume(kb + 1, 1)
        consume(last, 0)

    @pl.when(left == 1)
    def _():
        produce(last, True, 1)
        consume(kb, 0)
        consume(last, 1)

    @pl.when(left == 0)
    def _():
        produce(last, True, 0)
        consume(last, 0)


def _for_key_blocks(mode, q0, tq, tk, step):
    def loop(lo, hi, masked):
        def it(kb, carry):
            step(kb, masked)
            return carry
        lax.fori_loop(lo, hi, it, 0)

    if mode == "cmp":
        step(0, True)
    elif mode == "sel":
        diag = q0 // tk
        loop(0, diag, False)
        step(diag, True)
    else:
        loop(jnp.maximum(q0 - (WINDOW - 1), 0) // tk, (q0 + tq - 1) // tk + 1, True)


def _attn_bias(mode, q0, k0, tq, tk):
    k = k0 + lax.broadcasted_iota(jnp.int32, (tk, tq), 0)
    t = q0 + lax.broadcasted_iota(jnp.int32, (tk, tq), 1)
    if mode == "cmp":
        ok = (k * 16 + 31) <= t
    elif mode == "win":
        ok = (k <= t) & ((t - k) < WINDOW)
    else:
        ok = k <= t
    bias = jnp.where(ok, 0.0, NEG)
    return jnp.concatenate([bias] * GRP, axis=1), jnp.concatenate([ok.astype(F32)] * GRP, axis=1)


def _sel_operands(qs, selneg_ref):
    return jnp.concatenate([qs, jnp.concatenate([selneg_ref[0]] * GRP, axis=0)], axis=1)


def _stack_heads(ref, tq):
    return jnp.concatenate([ref[:, g * HD:(g + 1) * HD] for g in range(GRP)], axis=0)


def _scaled_queries(q_ref, tq):
    return (_stack_heads(q_ref, tq).astype(F32) * SCALE).astype(_MXU)


def _blocked_t(x, tk):
    n, Sk, d = x.shape
    return x.reshape(n, Sk // tk, tk, d).transpose(0, 1, 3, 2)


def _head_rows(ref):
    return jnp.concatenate([ref[0, g:g + 1, :] for g in range(GRP)], axis=1)


def _attn_fwd(q, qcol0, k, vt, mode, selneg, gate, y_prev, y_dtype, name):
    S, Sk = q.shape[0], k.shape[1]
    tq, tk = _attn_cfg(S, Sk, mode)
    R = GRP * tq

    def body(*refs):
        q_ref, k_ref, vt_ref = refs[:3]
        rest = list(refs[3:])
        sel_ref = rest.pop(0) if mode == "sel" else None
        gate_ref = rest.pop(0)
        yp_ref = rest.pop(0) if y_prev is not None else None
        o_ref, lse_ref, y_ref, m_scr, l_scr, acc, s_scr = rest
        q0 = pl.program_id(1) * tq
        qs = _scaled_queries(q_ref, tq)
        m_scr[...] = jnp.full_like(m_scr, NEG)
        l_scr[...] = jnp.zeros_like(l_scr)
        acc[...] = jnp.zeros_like(acc)
        qk = _sel_operands(qs, sel_ref) if mode == "sel" else qs

        def produce(kb, masked, slot):
            k0 = _block_start(kb, tk)
            s = _dot(k_ref[0, pl.ds(k0, tk), :], qk, "nt")
            if masked:
                s = s + _attn_bias(mode, q0, k0, tq, tk)[0]
            s_scr[slot] = s

        def consume(kb, slot):
            s = s_scr[slot]
            m_old = m_scr[...]
            m_new = jnp.maximum(m_old, jnp.max(s, axis=0, keepdims=True))
            p = jnp.exp(s - m_new)
            if mode == "cmp":
                p = p * _attn_bias(mode, q0, 0, tq, tk)[1]
            alpha = jnp.exp(m_old - m_new)
            l_scr[...] = alpha * l_scr[...] + jnp.sum(p, axis=0, keepdims=True)
            acc[...] = alpha * acc[...] + _dot(vt_ref[0, kb], p, "nn")
            m_scr[...] = m_new

        _pipelined_key_blocks(mode, q0, tq, tk, produce, consume)
        l = l_scr[...]
        good = l > 0.0
        o_t = acc[...] * jnp.where(good, 1.0 / jnp.where(good, l, 1.0), 0.0)
        lse = jnp.where(good, m_scr[...] + jnp.log(jnp.where(good, l, 1.0)), -NEG)
        y_t = o_t * _sigmoid(_head_rows(gate_ref))
        for g in range(GRP):
            hs, qs_ = slice(g * HD, (g + 1) * HD), slice(g * tq, (g + 1) * tq)
            o_ref[:, hs] = o_t[:, qs_].T
            lse_ref[0, g:g + 1, :] = lse[:, qs_]
            yg = y_t[:, qs_].T
            if y_prev is not None:
                yg = yg + yp_ref[:, hs]
            y_ref[:, hs] = yg.astype(y_ref.dtype)

    row_spec = pl.BlockSpec((1, GRP, tq), lambda h, i: (h, 0, i))
    qo_spec = pl.BlockSpec((tq, GRP * HD), lambda h, i: (i, h))
    ins = [q, k, vt]
    specs = [pl.BlockSpec((tq, GRP * HD), lambda h, i: (i, qcol0 + h)), pl.BlockSpec((1, Sk, k.shape[2]), lambda h, i: (h, 0, 0)),
             pl.BlockSpec((1, Sk // tk, HD, tk), lambda h, i: (h, 0, 0, 0))]
    if mode == "sel":
        assert tq == tk
        ins.append(selneg)
        specs.append(pl.BlockSpec((1, tq, selneg.shape[2]), lambda h, i: (h, i, 0)))
    ins.append(gate)
    specs.append(row_spec)
    if y_prev is not None:
        ins.append(y_prev)
        specs.append(qo_spec)
    return pl.pallas_call(
        body, name=name, grid=(N_KV, S // tq), in_specs=specs, out_specs=[qo_spec, row_spec, qo_spec],
        out_shape=[jax.ShapeDtypeStruct((S, ATT_WIDTH), F32), jax.ShapeDtypeStruct((N_KV, GRP, S), F32),
                   jax.ShapeDtypeStruct((S, ATT_WIDTH), y_dtype)],
        scratch_shapes=[pltpu.VMEM((1, R), F32), pltpu.VMEM((1, R), F32), pltpu.VMEM((HD, R), F32), pltpu.VMEM((2, tk, R), F32)],
        compiler_params=_cp(("parallel", "arbitrary")))(*ins)


def _attn_bwd(q, qcol0, k, kt, v, o, lse, dy, dycol0, gate, mode, selneg, name):
    S, Sk = q.shape[0], k.shape[1]
    tq, tk = _attn_cfg(S, Sk, mode)
    R = GRP * tq

    def body(*refs):
        if mode == "sel":
            q_ref, k_ref, kt_ref, v_ref, o_ref, lse_ref, dy_ref, gate_ref, sel_ref, dq_ref, dk_ref, dv_ref, dg_ref, dq_scr = refs
        else:
            q_ref, k_ref, kt_ref, v_ref, o_ref, lse_ref, dy_ref, gate_ref, dq_ref, dk_ref, dv_ref, dg_ref, dq_scr = refs

        @pl.when(pl.program_id(1) == 0)
        def _():
            dk_ref[...] = jnp.zeros_like(dk_ref)
            dv_ref[...] = jnp.zeros_like(dv_ref)

        q0 = pl.program_id(1) * tq
        qs = _scaled_queries(q_ref, tq)
        dys = _stack_heads(dy_ref, tq)
        gv = _sigmoid(_head_rows(gate_ref))
        dy_o = _dot(jnp.ones((8, HD), F32), dys * _stack_heads(o_ref, tq), "nt", hi=True)[0:1, :]
        delta = gv * dy_o
        dgate = dy_o * (gv * (1.0 - gv))
        for g in range(GRP):
            dg_ref[0, g:g + 1, :] = dgate[:, g * tq:(g + 1) * tq]
        lsev = _head_rows(lse_ref)
        dos = (dys * jnp.broadcast_to(gv, (8, R)).T[:, 0:1]).astype(_MXU)
        dq_scr[...] = jnp.zeros_like(dq_scr)
        qk = _sel_operands(qs, sel_ref) if mode == "sel" else qs

        def step(kb, masked):
            k0 = _block_start(kb, tk)
            s = _dot(k_ref[0, pl.ds(k0, tk), :], qk, "nt")
            if masked:
                bias, okf = _attn_bias(mode, q0, k0, tq, tk)
                s = s + bias
            p = jnp.exp(s - lsev)
            if mode == "cmp":
                p = p * okf
            dp = _dot(v_ref[0, pl.ds(k0, tk), :], dos, "nt")
            ds = p * (dp - delta)
            dq_scr[...] += _dot(kt_ref[0, kb], ds, "nn")
            dk_ref[0, pl.ds(k0, tk), :] += _dot(ds, qs, "nn")
            dv_ref[0, pl.ds(k0, tk), :] += _dot(p, dos, "nn")

        _for_key_blocks(mode, q0, tq, tk, step)
        for g in range(GRP):
            dq_ref[:, g * HD:(g + 1) * HD] = (dq_scr[:, g * tq:(g + 1) * tq] * SCALE).T

    kv_spec = pl.BlockSpec((1, Sk, HD), lambda h, i: (h, 0, 0))
    qo_spec = pl.BlockSpec((tq, GRP * HD), lambda h, i: (i, h))
    row_spec = pl.BlockSpec((1, GRP, tq), lambda h, i: (h, 0, i))
    ins = [q, k, kt, v, o, lse, dy, gate]
    specs = [pl.BlockSpec((tq, GRP * HD), lambda h, i: (i, qcol0 + h)), pl.BlockSpec((1, Sk, k.shape[2]), lambda h, i: (h, 0, 0)),
             pl.BlockSpec((1, Sk // tk, HD, tk), lambda h, i: (h, 0, 0, 0)), kv_spec, qo_spec, row_spec,
             pl.BlockSpec((tq, GRP * HD), lambda h, i: (i, dycol0 + h)), row_spec]
    if mode == "sel":
        assert tq == tk
        ins.append(selneg)
        specs.append(pl.BlockSpec((1, tq, selneg.shape[2]), lambda h, i: (h, i, 0)))
    return pl.pallas_call(
        body, name=name, grid=(N_KV, S // tq), in_specs=specs, out_specs=[qo_spec, kv_spec, kv_spec, row_spec],
        out_shape=[jax.ShapeDtypeStruct((S, ATT_WIDTH), F32), jax.ShapeDtypeStruct((N_KV, Sk, HD), F32),
                   jax.ShapeDtypeStruct((N_KV, Sk, HD), F32), jax.ShapeDtypeStruct((N_KV, GRP, S), F32)],
        scratch_shapes=[pltpu.VMEM((HD, R), F32)],
        compiler_params=_cp(("parallel", "arbitrary")))(*ins)


def _select(q, qcol0, k_cmp, lse):
    S, NC = q.shape[0], k_cmp.shape[1]
    NB = S // SEL_BLOCK
    tq = _pick(S, (256, 128))
    ci = np.arange(NC)[None, :] * 16
    sj = np.arange(NB)[:, None] * SEL_BLOCK
    ov_t = np.clip(np.minimum(ci + 32, sj + SEL_BLOCK) - np.maximum(ci, sj), 0, None) / 32.0
    ov_t[:, NC - 1] = 0.0
    ov_t = jnp.asarray(ov_t, F32)

    def body(q_ref, k_ref, lse_ref, ov_ref, sel_ref):
        q0 = pl.program_id(1) * tq
        bias, okf = _attn_bias("cmp", q0, 0, tq, NC)
        lsev = _head_rows(lse_ref)
        p = jnp.exp(_dot(k_ref[0], _scaled_queries(q_ref, tq), "nt") + bias - lsev) * okf
        imp4 = _dot(ov_ref[...], p, "nn")
        imp = imp4[:, 0:tq] + imp4[:, tq:2 * tq] + imp4[:, 2 * tq:3 * tq] + imp4[:, 3 * tq:4 * tq]
        blk = lax.broadcasted_iota(jnp.int32, (NB, tq), 0)
        cur = lax.shift_right_logical(q0 + lax.broadcasted_iota(jnp.int32, (NB, tq), 1), 6)
        imp = jnp.where((blk == 0) | (blk == cur) | (blk == cur - 1), FORCE, imp)
        imp = jnp.where(blk <= cur, imp, -1.0)
        rank = jnp.zeros((NB, tq), F32)
        for j in range(NB):
            row = imp[j:j + 1, :]
            ahead = (row > imp) | ((row == imp) & (blk > j))
            rank = rank + ahead.astype(F32)
        chosen = (rank < float(N_SELECT)) & (imp >= 0.0)
        sel_ref[0] = jnp.where(chosen, 0.0, NEG).T.astype(sel_ref.dtype)

    return pl.pallas_call(
        body, name="select_blocks", grid=(N_KV, S // tq),
        in_specs=[pl.BlockSpec((tq, GRP * HD), lambda h, i: (i, qcol0 + h)), pl.BlockSpec((1, NC, HD), lambda h, i: (h, 0, 0)),
                  pl.BlockSpec((1, GRP, tq), lambda h, i: (h, 0, i)), pl.BlockSpec((NB, NC), lambda h, i: (0, 0))],
        out_specs=pl.BlockSpec((1, tq, NB), lambda h, i: (h, i, 0)),
        out_shape=jax.ShapeDtypeStruct((N_KV, S, NB), _MXU), compiler_params=_cp(("parallel", "parallel")))(q, k_cmp, lse, ov_t)


def _to_rows16(x):
    S = x.shape[0]
    return x.reshape(S // 16, 16, N_KV, HD).transpose(2, 0, 1, 3).reshape(N_KV, S // 16, 16 * HD)


def _from_rows16(r):
    NC = r.shape[1]
    return r.reshape(N_KV, NC, 16, HD).transpose(1, 2, 0, 3).reshape(NC * 16, N_KV * HD)


DT_COL0 = SSD_WIDTH + CONV_CH
GATE_IN_COL0 = D_IN - 3 * N_HEADS


SHARD_IN = D_IN // N_DEV


def _orig_cols(ref, c0, width):
    pieces, c = [], c0
    while c < c0 + width:
        d, off = divmod(c, SHARD_IN)
        w = min(SHARD_IN - off, c0 + width - c)
        pieces.append(ref[d, :, off:off + w])
        c += w
    return pieces[0] if len(pieces) == 1 else jnp.concatenate(pieces, axis=1)


def _cols_from_slabs(slabs):
    _, R, c = slabs.shape
    tr = _pick(R, (256, 128))

    def body(s_ref, o_ref):
        for t in range(N_DEV * c // LANE):
            pieces, col = [], t * LANE
            while col < (t + 1) * LANE:
                d, off = divmod(col, c)
                w = min(c - off, (t + 1) * LANE - col)
                pieces.append(s_ref[d, :, off:off + w])
                col += w
            o_ref[:, t * LANE:(t + 1) * LANE] = pieces[0] if len(pieces) == 1 else jnp.concatenate(pieces, axis=1)

    return pl.pallas_call(
        body, name="cols_from_slabs", grid=(R // tr,), in_specs=[pl.BlockSpec((N_DEV, tr, c), lambda i: (0, i, 0))],
        out_specs=pl.BlockSpec((tr, N_DEV * c), lambda i: (i, 0)), out_shape=jax.ShapeDtypeStruct((R, N_DEV * c), slabs.dtype),
        compiler_params=_cp(("parallel",)))(slabs)


def _slabs_from_cols(x):
    R, c = x.shape[0], x.shape[1] // N_DEV
    tr = _pick(R, (256, 128))

    def body(x_ref, o_ref):
        for d in range(N_DEV):
            o_ref[d] = x_ref[:, d * c:(d + 1) * c]

    return pl.pallas_call(
        body, name="slabs_from_cols", grid=(R // tr,), in_specs=[pl.BlockSpec((tr, N_DEV * c), lambda i: (i, 0))],
        out_specs=pl.BlockSpec((N_DEV, tr, c), lambda i: (0, i, 0)), out_shape=jax.ShapeDtypeStruct((N_DEV, R, c), x.dtype),
        compiler_params=_cp(("parallel",)))(x)


def _w_in_from_slabs(slabs):
    D = slabs.shape[1]
    tr = _pick(D, (256, 128))

    def body(s_ref, main_ref, small_ref):
        for t in range(W_MAIN // LANE):
            c = t * LANE
            main_ref[:, c:c + LANE] = _orig_cols(s_ref, c if c < DT_COL0 else c + SSD_HEADS, LANE)
        small_ref[...] = jnp.concatenate(
            [_orig_cols(s_ref, DT_COL0, SSD_HEADS), _orig_cols(s_ref, GATE_IN_COL0, 3 * N_HEADS),
             jnp.zeros((tr, W_SMALL - SSD_HEADS - 3 * N_HEADS), small_ref.dtype)], axis=1)

    return pl.pallas_call(
        body, name="w_in_layout", grid=(D // tr,), in_specs=[pl.BlockSpec((N_DEV, tr, SHARD_IN), lambda i: (0, i, 0))],
        out_specs=[pl.BlockSpec((tr, W_MAIN), lambda i: (i, 0)), pl.BlockSpec((tr, W_SMALL), lambda i: (i, 0))],
        out_shape=[jax.ShapeDtypeStruct((D, W_MAIN), slabs.dtype), jax.ShapeDtypeStruct((D, W_SMALL), slabs.dtype)],
        compiler_params=_cp(("parallel",)))(slabs)


def _w_in_to_slabs(main, small):
    D = main.shape[0]
    tr = _pick(D, (256, 128))
    ranges = [(0, DT_COL0, 0, 0), (DT_COL0, DT_COL0 + SSD_HEADS, 1, 0), (DT_COL0 + SSD_HEADS, GATE_IN_COL0, 0, DT_COL0),
              (GATE_IN_COL0, D_IN, 1, SSD_HEADS)]

    def body(main_ref, small_ref, o_ref):
        srcs = (main_ref, small_ref)
        for d in range(N_DEV):
            lo, hi = d * SHARD_IN, (d + 1) * SHARD_IN
            pieces = []
            for start, stop, which, s0 in ranges:
                a, b = max(lo, start), min(hi, stop)
                if a < b:
                    pieces.append(srcs[which][:, s0 + a - start:s0 + b - start].astype(o_ref.dtype))
            o_ref[d] = pieces[0] if len(pieces) == 1 else jnp.concatenate(pieces, axis=1)

    return pl.pallas_call(
        body, name="w_in_grad_layout", grid=(D // tr,),
        in_specs=[pl.BlockSpec((tr, W_MAIN), lambda i: (i, 0)), pl.BlockSpec((tr, W_SMALL), lambda i: (i, 0))],
        out_specs=pl.BlockSpec((N_DEV, tr, SHARD_IN), lambda i: (0, i, 0)),
        out_shape=jax.ShapeDtypeStruct((N_DEV, D, SHARD_IN), main.dtype), compiler_params=_cp(("parallel",)))(main, small)


QB, KCB, VCB, KSB, VSB, KWB, VWB = 10, 14, 15, 16, 17, 18, 19


def _col256(a, b):
    return a[:, b * 256:(b + 1) * 256]


_EARLY = ["w_in", "cmp_w1_k", "cmp_w1_v"]
_LATE = ["w_out", "w_gate", "w_up", "w_down"]
_FFN = ["w_down", "w_gate", "w_up"]
_MID = ["w_out"]
_LAST = ["cmp_w1_k", "cmp_w1_v", "w_in"]


def _local_step(x, tgt, p, early_weights=None, late_weights=None, grads_ready=None):
    S = x.shape[0]
    cos, sin = _rope_tables(S)

    u, rs1 = _rms_fwd(x, p["attn_norm_w"], "attn_norm")
    if early_weights is not None:
        p = {**p, **early_weights((u, cos, sin))}
    proj = _mm(u, p["w_main"], "nn", F32, "in_proj")
    proj_small = _mm(u, p["w_small"], "nn", F32, "in_proj_small")
    xa = _conv_fwd(proj, p["conv_w"], p["conv_b"])
    y_ssd, y_pre, rs_ssd, hs = _ssd_fwd(proj, proj_small, xa, p["dt_bias"], p["a_log"], p["d_skip"], p["ssd_norm_w"])

    q_rot = _rope([proj], QB, ATT_WIDTH, cos, sin, 1.0, _MXU, "rope_q")
    kv = _kv_prep(proj, cos, sin, _attn_cfg(S, S, "sel")[1])
    rk, rv = _to_rows16(_col256(proj, KCB)), _to_rows16(_col256(proj, VCB))
    k_cmp, hid_k = _compress_fwd(rk, p["cmp_pe_k"], p["cmp_w1_k"], p["cmp_w2_k"])
    v_cmp, hid_v = _compress_fwd(rv, p["cmp_pe_v"], p["cmp_w1_v"], p["cmp_w2_v"])
    n_cmp = k_cmp.shape[1]

    gates = proj_small[:, SSD_HEADS:SSD_HEADS + 3 * N_HEADS].reshape(S, N_KV, GRP, 3).transpose(3, 1, 2, 0)
    o_cmp, lse_cmp, y_att = _attn_fwd(proj, QB, k_cmp, _blocked_t(v_cmp, n_cmp), "cmp", None, gates[0], None, F32, "attn_cmp_fwd")
    sel = _select(proj, QB, k_cmp, lse_cmp)
    o_sel, lse_sel, y_att = _attn_fwd(q_rot, 0, kv["ks_ext"], kv["vs_t"], "sel", sel, gates[1], y_att, F32, "attn_sel_fwd")
    o_win, lse_win, y_att = _attn_fwd(q_rot, 0, kv["kw"], kv["vw_t"], "win", None, gates[2], y_att, _MXU, "attn_win_fwd")

    if late_weights is not None:
        p = {**p, **late_weights(y_att)}
    mixed = jnp.concatenate([y_ssd, y_att], axis=1)
    h1 = _mm(mixed, p["w_out"], "nn", F32, "out_proj", res=x)
    v, rs_ffn = _rms_fwd(h1, p["ffn_norm_w"], "ffn_norm")
    gt, up, act = _ffn_up(v, p["w_gate"], p["w_up"])
    h2 = _mm(act, p["w_down"], "nn", F32, "ffn_down", res=h1)
    loss, dh2, dh2b, d_final_w = _final_loss(h2, p["final_norm_w"], tgt)

    def ready(names):
        return None if grads_ready is None else grads_ready(names, g)

    g = {"final_norm_w": d_final_w}
    g["w_down"] = _mm(act, dh2b, "tn", _MXU, "dw_down")
    dgt, dup = _ffn_dact(dh2b, p["w_down"], gt, up)
    g["w_gate"] = _mm(v, dgt, "tn", _MXU, "dw_gate")
    g["w_up"] = _mm(v, dup, "tn", _MXU, "dw_up")
    dv = _mm(dgt, p["w_gate"], "nt", F32, "dv_gate", after=ready(_FFN))
    dv = _mm(dup, p["w_up"], "nt", F32, "dv_up", res=dv)
    dh1, dh1b, g["ffn_norm_w"] = _rms_bwd(dv, h1, rs_ffn, p["ffn_norm_w"], dh2, "ffn_norm_bwd")
    g["w_out"] = _mm(mixed, dh1b, "tn", _MXU, "dw_out")
    dmixed = _mm(dh1b, p["w_out"], "nt", F32, "dmixed", after=ready(_MID))

    dz, dxa, ddtr, g["dt_bias"], g["a_log"], g["d_skip"], g["ssd_norm_w"] = _ssd_bwd(
        dmixed, proj, proj_small, xa, y_pre, rs_ssd, hs, p["dt_bias"], p["a_log"], p["d_skip"], p["ssd_norm_w"])
    dxbc, g["conv_w"], g["conv_b"] = _conv_bwd(proj, p["conv_w"], p["conv_b"], dxa)

    dyb = SSD_WIDTH // (GRP * HD)
    dq_cmp, dk_cmp, dv_cmp, dg_cmp = _attn_bwd(proj, QB, k_cmp, _blocked_t(k_cmp, n_cmp), v_cmp, o_cmp, lse_cmp, dmixed, dyb,
                                               gates[0], "cmp", None, "attn_cmp_bwd")
    dq_sel, dks, dvs, dg_sel = _attn_bwd(q_rot, 0, kv["ks_ext"], kv["ks_t"], kv["vs"], o_sel, lse_sel, dmixed, dyb, gates[1], "sel",
                                         sel, "attn_sel_bwd")
    dq_win, dkw, dvw, dg_win = _attn_bwd(q_rot, 0, kv["kw"], kv["kw_t"], kv["vw"], o_win, lse_win, dmixed, dyb, gates[2], "win", None,
                                         "attn_win_bwd")
    dgate = jnp.stack([dg_cmp, dg_sel, dg_win]).transpose(3, 1, 2, 0).reshape(S, 3 * N_HEADS)
    drk, g["cmp_w1_k"], g["cmp_w2_k"], g["cmp_pe_k"] = _compress_bwd(rk, p["cmp_pe_k"], p["cmp_w1_k"], p["cmp_w2_k"], hid_k, dk_cmp)
    drv, g["cmp_w1_v"], g["cmp_w2_v"], g["cmp_pe_v"] = _compress_bwd(rv, p["cmp_pe_v"], p["cmp_w1_v"], p["cmp_w2_v"], hid_v, dv_cmp)
    dq = _rope([dq_sel, dq_win], 0, ATT_WIDTH, cos, sin, -1.0, _MXU, "rope_dq", extra=(dq_cmp, 0))
    dkv = _dkv_post(dks, dvs, dkw, dvw, cos, sin)
    dproj = jnp.concatenate([dz, dxbc, dq] + [t.astype(_MXU) for t in (_from_rows16(drk), _from_rows16(drv))] + [dkv], axis=1)
    dsmall = jnp.concatenate([ddtr, dgate, jnp.zeros((S, W_SMALL - SSD_HEADS - 3 * N_HEADS), F32)], axis=1).astype(_MXU)
    g["w_main"] = _mm(u, dproj, "tn", _MXU, "dw_in")
    g["w_small"] = _mm(u, dsmall, "tn", F32, "dw_in_small")
    du = _mm(dproj, p["w_main"], "nt", F32, "du_main", after=ready(_LAST))
    du = _mm(dsmall, p["w_small"], "nt", F32, "du_small", res=du)
    grad_x, _, g["attn_norm_w"] = _rms_bwd(du, x, rs1, p["attn_norm_w"], dh1, "attn_norm_bwd")
    return loss, grad_x, g


MESH_ID = pl.DeviceIdType.MESH


def _my_coords():
    return lax.axis_index("x"), lax.axis_index("y"), lax.axis_index("c")


def _flat_id(px, py, pc):
    return 4 * px + 2 * py + pc


def _peer(k):
    mx, my, mc = _my_coords()
    return (1 - mx if k & 4 else mx, 1 - my if k & 2 else my, 1 - mc if k & 1 else mc)


def _exchange(arrs, scatter, name, after=()):
    n, na = len(arrs), len(after)
    scatter = [scatter] * n if isinstance(scatter, bool) else list(scatter)

    def body(*refs):
        ins, outs = refs[:n], refs[n + na:2 * n + na]
        send_sems, recv_sems, local_sems = refs[2 * n + na:]
        me = _flat_id(*_my_coords())
        copies = []
        for i in range(n):
            src_me = ins[i].at[me] if scatter[i] else ins[i]
            local = pltpu.make_async_copy(src_me, outs[i].at[me], local_sems.at[i])
            local.start()
            copies.append(local)
        for k in range(1, N_DEV):
            peer = _peer(k)
            for i in range(n):
                src = ins[i].at[_flat_id(*peer)] if scatter[i] else ins[i]
                cp = pltpu.make_async_remote_copy(src_ref=src, dst_ref=outs[i].at[me], send_sem=send_sems.at[i * 7 + k - 1],
                                                  recv_sem=recv_sems.at[i * 7 + k - 1], device_id=peer, device_id_type=MESH_ID)
                cp.start()
                copies.append(cp)
        for cp in copies:
            cp.wait()

    any_spec = pl.BlockSpec(memory_space=pl.ANY)
    out_shape = [jax.ShapeDtypeStruct(a.shape if sc else (N_DEV,) + a.shape, a.dtype) for a, sc in zip(arrs, scatter)]
    return pl.pallas_call(
        body, name=name, in_specs=[any_spec] * (n + na), out_specs=[any_spec] * n, out_shape=out_shape,
        scratch_shapes=[pltpu.SemaphoreType.DMA((n * 7,)), pltpu.SemaphoreType.DMA((n * 7,)), pltpu.SemaphoreType.DMA((n,))],
        compiler_params=pltpu.CompilerParams(has_side_effects=True))(*arrs, *after)


_HBM = pl.BlockSpec(memory_space=pltpu.HBM)
_SEM = pl.BlockSpec(memory_space=pltpu.SEMAPHORE)
_EFFECT = pltpu.SideEffectType.DATAFLOW_SIDE_EFFECTING


def _split_copies(ins, lands, send_sems, recv_sems, scatter):
    me = _flat_id(*_my_coords())
    out = []
    for k in range(1, N_DEV):
        peer = _peer(k)
        for i in range(len(ins)):
            src = ins[i].at[_flat_id(*peer)] if scatter else ins[i]
            out.append(pltpu.make_async_remote_copy(src_ref=src, dst_ref=lands[i].at[me], send_sem=send_sems.at[i * 7 + k - 1],
                                                    recv_sem=recv_sems.at[i * 7 + k - 1], device_id=peer, device_id_type=MESH_ID))
    return out


def _split_start(arrs, scatter, name):
    n = len(arrs)

    def body(*refs):
        for cp in _split_copies(refs[:n], refs[n:2 * n], refs[2 * n], refs[2 * n + 1], scatter):
            cp.start()
        refs[-1][...] = jnp.zeros_like(refs[-1])

    land_shapes = [a.shape if scatter else (N_DEV,) + a.shape for a in arrs]
    out_shape = ((pltpu.SemaphoreType.DMA((n * 7,)), pltpu.SemaphoreType.DMA((n * 7,)))
                 + tuple(pltpu.HBM(a.shape, a.dtype) for a in arrs) + tuple(pltpu.HBM(s, a.dtype) for s, a in zip(land_shapes, arrs))
                 + (jax.ShapeDtypeStruct((8, 128), F32),))
    operands = ([pltpu.with_memory_space_constraint(a, pltpu.HBM) for a in arrs]
                + [pltpu.with_memory_space_constraint(lax.empty(s, a.dtype), pltpu.HBM) for s, a in zip(land_shapes, arrs)])
    res = pl.pallas_call(
        body, name=name, out_shape=out_shape, in_specs=[_HBM] * (2 * n),
        out_specs=(_SEM, _SEM) + (_HBM,) * (2 * n) + (pl.BlockSpec(memory_space=pltpu.VMEM),),
        input_output_aliases={i: 2 + i for i in range(2 * n)},
        compiler_params=pltpu.CompilerParams(has_side_effects=_EFFECT))(*operands)
    return dict(send=res[0], recv=res[1], ins=list(res[2:2 + n]), lands=list(res[2 + n:2 + 2 * n]), token=res[-1])


def _split_wait(st, scatter, after, name):
    n = len(st["ins"])

    def body(*refs):
        for cp in _split_copies(refs[:n], refs[n:2 * n], refs[2 * n], refs[2 * n + 1], scatter):
            cp.wait_send()
            cp.wait_recv()

    arrs = st["ins"] + st["lands"]
    res = pl.pallas_call(
        body, name=name, out_shape=tuple(pltpu.HBM(a.shape, a.dtype) for a in arrs),
        in_specs=[_HBM] * (2 * n) + [_SEM, _SEM] + [pl.BlockSpec(memory_space=pl.ANY)] * len(after), out_specs=(_HBM,) * (2 * n),
        input_output_aliases={i: i for i in range(2 * n)},
        compiler_params=pltpu.CompilerParams(has_side_effects=_EFFECT))(*arrs, st["send"], st["recv"], *after)
    me = _flat_id(*_my_coords())
    out = []
    for src, land in zip(res[:n], res[n:]):
        own = lax.dynamic_index_in_dim(src, me, 0, keepdims=True) if scatter else src[None]
        out.append(lax.dynamic_update_slice_in_dim(land, own, me, 0))
    return out


def _adam_step(p_ref, w_ref, m_ref, v_ref, g_ref, d_ref, nm_ref, nv_ref):
    g = p_ref[0].astype(F32)
    for j in range(1, p_ref.shape[0]):
        g = g + p_ref[j].astype(F32)
    g_ref[...] = g
    nm = ADAM_B1 * m_ref[...] + (1.0 - ADAM_B1) * g
    nv = ADAM_B2 * v_ref[...] + (1.0 - ADAM_B2) * (g * g)
    nm_ref[...] = nm
    nv_ref[...] = nv
    m_hat = nm / (1.0 - ADAM_B1 ** ADAM_STEP)
    v_hat = nv / (1.0 - ADAM_B2 ** ADAM_STEP)
    d_ref[...] = -ADAM_LR * (m_hat / (jnp.sqrt(v_hat) + ADAM_EPS) + ADAM_WD * w_ref[...])


def _adam_sum(parts, w, m, v, name):
    P, R, C = parts.shape
    tr = _pick(R, (256, 128, 64, 32, 8)) if C <= 1024 else _pick(R, (128, 64, 32, 8))
    blk = pl.BlockSpec((tr, C), lambda i: (i, 0))
    return pl.pallas_call(
        functools.partial(_adam_step), name=name, grid=(R // tr,),
        in_specs=[pl.BlockSpec((P, tr, C), lambda i: (0, i, 0)), blk, blk, blk],
        out_specs=[blk] * 4, out_shape=[jax.ShapeDtypeStruct((R, C), F32)] * 4, compiler_params=_cp(("parallel",)))(parts, w, m, v)


def _adam_small(loss_parts, parts, ws, ms, vs):
    n = len(parts)

    def body(*refs):
        loss_ref, ins, outs, total_ref = refs[0], refs[1:4 * n + 1], refs[4 * n + 1:-1], refs[-1]
        for i in range(n):
            _adam_step(ins[i], ins[n + i], ins[2 * n + i], ins[3 * n + i], *outs[4 * i:4 * i + 4])
        total = loss_ref[0]
        for d in range(1, N_DEV):
            total = total + loss_ref[d]
        total_ref[...] = total

    out_shape = [jax.ShapeDtypeStruct(w.shape, F32) for w in ws for _ in range(4)] + [jax.ShapeDtypeStruct(loss_parts.shape[1:], F32)]
    res = pl.pallas_call(body, name="adam_small", out_shape=out_shape)(loss_parts, *parts, *ws, *ms, *vs)
    return res[-1], [tuple(res[4 * i:4 * i + 4]) for i in range(n)]


_WEIGHTS = ["attn_norm_w", "w_in", "conv_w", "conv_b", "dt_bias", "a_log", "d_skip", "ssd_norm_w", "cmp_w1_k", "cmp_w2_k",
            "cmp_w1_v", "cmp_w2_v", "cmp_pe_k", "cmp_pe_v", "w_out", "ffn_norm_w", "w_gate", "w_up", "w_down", "final_norm_w"]
_BIG = ["w_in", "w_gate", "w_up", "w_down", "w_out", "cmp_w1_k", "cmp_w1_v"]
_COL_SHARDED = ("w_in", "w_gate", "w_up")
_REPLICATED = ["attn_norm_w", "conv_b", "dt_bias", "a_log", "d_skip", "ssd_norm_w", "cmp_pe_k", "cmp_pe_v", "ffn_norm_w",
               "final_norm_w"]
_SMALL_SHARDED = ["conv_w", "cmp_w2_k", "cmp_w2_v"]


def _cols_to_slabs(g):
    R = g.shape[0]
    return g.reshape(R, N_DEV, -1).transpose(1, 0, 2)


def _slabs_to_cols(s):
    return s.transpose(1, 0, 2).reshape(s.shape[1], -1)


def kernel(x, attn_norm_w, w_in, conv_w, conv_b, dt_bias, a_log, d_skip, ssd_norm_w, cmp_w1_k, cmp_w2_k, cmp_w1_v, cmp_w2_v, cmp_pe_k, cmp_pe_v, w_out, ffn_norm_w, w_gate, w_up, w_down, final_norm_w, loss_target, m_attn_norm_w, m_w_in, m_conv_w, m_conv_b, m_dt_bias, m_a_log, m_d_skip, m_ssd_norm_w, m_cmp_w1_k, m_cmp_w2_k, m_cmp_w1_v, m_cmp_w2_v, m_cmp_pe_k, m_cmp_pe_v, m_w_out, m_ffn_norm_w, m_w_gate, m_w_up, m_w_down, m_final_norm_w, v_attn_norm_w, v_w_in, v_conv_w, v_conv_b, v_dt_bias, v_a_log, v_d_skip, v_ssd_norm_w, v_cmp_w1_k, v_cmp_w2_k, v_cmp_w1_v, v_cmp_w2_v, v_cmp_pe_k, v_cmp_pe_v, v_w_out, v_ffn_norm_w, v_w_gate, v_w_up, v_w_down, v_final_norm_w):
    a = dict(locals())

    shard = {n: a[n][0].astype(_MXU) for n in _BIG}
    early_small = [cmp_w2_k[0], cmp_w2_v[0], conv_w[0]]
    st_early = _split_start([shard[n] for n in _EARLY] + early_small, False, "gather_early_start")
    zero = st_early["token"][0, 0].astype(_MXU)
    st_late = _split_start([shard[_LATE[0]] + zero] + [shard[n] for n in _LATE[1:]], False, "gather_late_start")

    def assemble(n, t):
        return _cols_from_slabs(t) if n in _COL_SHARDED else t.reshape(-1, t.shape[-1])

    p = dict(attn_norm_w=attn_norm_w, conv_b=conv_b, dt_bias=dt_bias, a_log=a_log, d_skip=d_skip, ssd_norm_w=ssd_norm_w,
             cmp_pe_k=cmp_pe_k.reshape(1, -1), cmp_pe_v=cmp_pe_v.reshape(1, -1), ffn_norm_w=ffn_norm_w,
             final_norm_w=final_norm_w.reshape(1, -1))

    def early_weights(after):
        got = _split_wait(st_early, False, tuple(after) + (st_late["token"],), "gather_early_wait")
        w_main, w_small = _w_in_from_slabs(got[0])
        return dict(w_main=w_main, w_small=w_small, cmp_w1_k=assemble("cmp_w1_k", got[1]), cmp_w1_v=assemble("cmp_w1_v", got[2]),
                    cmp_w2_k=assemble("cmp_w2_k", got[3]).astype(_MXU), cmp_w2_v=assemble("cmp_w2_v", got[4]).astype(_MXU),
                    conv_w=_slabs_to_cols(got[5]))

    def late_weights(after):
        got_late = _split_wait(st_late, False, (after,), "gather_late_wait")
        return {n: assemble(n, t) for n, t in zip(_LATE, got_late)}

    def slabs_of(g, n):
        if n == "w_in":
            return _w_in_to_slabs(g["w_main"], g["w_small"])
        return _slabs_from_cols(g[n]) if n in _COL_SHARDED else g[n].reshape(N_DEV, -1, g[n].shape[-1])

    started = []

    def grads_ready(names, g):
        started.append((names, _split_start([slabs_of(g, n) for n in names], True, "scatter_grads_start_%d" % len(started))))
        return started[-1][1]["token"]

    loss_part, grad_x, g = _local_step(x[0], loss_target[0], p, early_weights, late_weights, grads_ready)

    out, after = {}, (started[-1][1]["token"],)
    for i, (names, st) in enumerate(started):
        if i == len(started) - 1:
            after = after + (grad_x,)
        received = _split_wait(st, True, after, "scatter_grads_wait_%d" % i)
        for n, parts in zip(names, received):
            out[n] = _adam_sum(parts, a[n][0], a["m_" + n][0], a["v_" + n][0], "adam_" + n)
        after = (out[names[-1]][0],)

    small_names = _REPLICATED + _SMALL_SHARDED
    partials = [g[n] for n in _REPLICATED] + [_cols_to_slabs(g["conv_w"])] + [
        g[n].reshape(N_DEV, -1, g[n].shape[-1]) for n in ("cmp_w2_k", "cmp_w2_v")]
    gathered = _exchange([loss_part] + partials, [False] * (1 + len(_REPLICATED)) + [True] * len(_SMALL_SHARDED),
                         "exchange_small_grads", after=(received[0],))
    shapes2d = [t.shape[1:] for t in gathered[1:]]
    loss, res_small = _adam_small(gathered[0], gathered[1:],
                                  *[[a[pre + n].reshape(s) for n, s in zip(small_names, shapes2d)] for pre in ("", "m_", "v_")])
    for n, r in zip(small_names, res_small):
        out[n] = r

    outs = [loss[0, 0], grad_x[None]]
    for j in range(4):
        for n in _WEIGHTS:
            outs.append(out[n][j].reshape(a[n].shape))
    return tuple(outs)
```

```python
import functools

import numpy as np
import jax
import jax.numpy as jnp
from jax import lax
from jax.experimental import pallas as pl
from jax.experimental.pallas import tpu as pltpu

F32 = jnp.float32
_MXU = jnp.bfloat16
_HI = lax.Precision.HIGHEST

N_DEV = 8
D_MODEL = 2048
SSD_WIDTH = 1024
ATT_WIDTH = 1024
SSD_HEADS = 16
SSD_P = 64
SSD_N = 128
SSD_L = 128
SSD_G = 2
CONV_CH = 1536
CONV_K = 4
HD = 64
N_HEADS = 16
N_KV = 4
GRP = 4
CMP_HID = 256
SEL_BLOCK = 64
N_SELECT = 16
WINDOW = 512
ROPE_DIM = 16
ROPE_THETA = 500000.0
D_FF = 5632
EPS = 1e-6
NEG = -1e30
FORCE = 1e4
SCALE = HD ** -0.5
D_IN = 5184
W_MAIN = 5120
W_SMALL = 128
VMEM_LIMIT = 52 * 1024 * 1024

ADAM_LR, ADAM_B1, ADAM_B2, ADAM_EPS, ADAM_WD, ADAM_STEP = 0.001, 0.9, 0.999, 1e-08, 0.01, 10


def _pick(n, cands):
    for c in cands:
        if n % c == 0:
            return c
    return n


def _cp(sem=None):
    return pltpu.CompilerParams(dimension_semantics=sem, vmem_limit_bytes=VMEM_LIMIT)


def _sigmoid(x):
    return 1.0 / (1.0 + jnp.exp(-x))


def _dot(a, b, dims, hi=False):
    dn = {"nn": (((1,), (0,)), ((), ())), "nt": (((1,), (1,)), ((), ())), "tn": (((0,), (0,)), ((), ()))}[dims]
    if hi:
        return lax.dot_general(a.astype(F32), b.astype(F32), dn, precision=_HI, preferred_element_type=F32)
    return lax.dot_general(a.astype(_MXU), b.astype(_MXU), dn, preferred_element_type=F32)


LANE = 128
MM_TILE = 1024
MM_K_WHOLE = 2048
MM_K_STEP = 1536
TN_ACC_ELEMS = 3 * 2 ** 20
TN_K_STEP = 512


def _largest_tile(n, cap):
    if n <= cap:
        return n
    best = LANE
    for t in range(LANE, cap + 1, LANE):
        if n % t == 0:
            best = t
    return best


def _mm_tiles(mode, M, N, K):
    if mode == "tn":
        tm = _largest_tile(M, 2 * MM_TILE)
        return tm, _largest_tile(N, TN_ACC_ELEMS // tm), _largest_tile(K, TN_K_STEP)
    tk = K if K <= MM_K_WHOLE else _largest_tile(K, MM_K_STEP)
    return _largest_tile(M, MM_TILE), _largest_tile(N, MM_TILE), tk


def _mm(a, b, mode, out_dtype, name, res=None, after=None):
    if mode == "nn":
        (M, K), N = a.shape, b.shape[1]
    elif mode == "nt":
        (M, K), N = a.shape, b.shape[0]
    else:
        (K, M), N = a.shape, b.shape[1]
    tm, tn, tk = _mm_tiles(mode, M, N, K)
    nk = K // tk
    a_spec = pl.BlockSpec((tk, tm), lambda i, j, k: (k, i)) if mode == "tn" else pl.BlockSpec((tm, tk), lambda i, j, k: (i, k))
    b_spec = pl.BlockSpec((tn, tk), lambda i, j, k: (j, k)) if mode == "nt" else pl.BlockSpec((tk, tn), lambda i, j, k: (k, j))
    o_spec = pl.BlockSpec((tm, tn), lambda i, j, k: (i, j))

    def finish(r, r_ref, o_ref):
        if res is not None:
            r = r + r_ref[...].astype(F32)
        o_ref[...] = r.astype(out_dtype)

    def body_one_step(*refs):
        a_ref, b_ref, o_ref = refs[0], refs[1], refs[-1]
        finish(_dot(a_ref[...], b_ref[...], mode), refs[2], o_ref)

    def body(*refs):
        a_ref, b_ref, o_ref, acc = refs[0], refs[1], refs[-2], refs[-1]
        k = pl.program_id(2)

        @pl.when(k == 0)
        def _():
            acc[...] = jnp.zeros_like(acc)

        acc[...] += _dot(a_ref[...], b_ref[...], mode)

        @pl.when(k == nk - 1)
        def _():
            finish(acc[...], refs[2], o_ref)

    ins, specs = [a, b], [a_spec, b_spec]
    if res is not None:
        ins.append(res)
        specs.append(o_spec)
    if after is not None:
        ins.append(after)
        specs.append(pl.BlockSpec(memory_space=pl.ANY))
    return pl.pallas_call(
        body_one_step if nk == 1 else body, name=name, grid=(M // tm, N // tn, nk), in_specs=specs, out_specs=o_spec,
        out_shape=jax.ShapeDtypeStruct((M, N), out_dtype), scratch_shapes=[] if nk == 1 else [pltpu.VMEM((tm, tn), F32)],
        compiler_params=_cp(("parallel", "parallel", "arbitrary")))(*ins)


def _ffn_up(v, w_gate, w_up):
    S, D = v.shape
    F = w_gate.shape[1]
    tm, tn = _largest_tile(S, MM_TILE), _largest_tile(F, MM_TILE // 2)

    def body(v_ref, wg_ref, wu_ref, gt_ref, up_ref, act_ref):
        vv = v_ref[...]
        g = _dot(vv, wg_ref[...], "nn")
        u = _dot(vv, wu_ref[...], "nn")
        gt_ref[...] = g
        up_ref[...] = u
        act_ref[...] = (g * _sigmoid(g) * u).astype(act_ref.dtype)

    o_spec = pl.BlockSpec((tm, tn), lambda i, j: (i, j))
    w_spec = pl.BlockSpec((D, tn), lambda i, j: (0, j))
    return pl.pallas_call(
        body, name="ffn_up", grid=(S // tm, F // tn),
        in_specs=[pl.BlockSpec((tm, D), lambda i, j: (i, 0)), w_spec, w_spec], out_specs=[o_spec, o_spec, o_spec],
        out_shape=[jax.ShapeDtypeStruct((S, F), F32), jax.ShapeDtypeStruct((S, F), F32), jax.ShapeDtypeStruct((S, F), _MXU)],
        compiler_params=_cp(("parallel", "parallel")))(v, w_gate, w_up)


def _ffn_dact(dh2, w_down, gt, up):
    S, D = dh2.shape
    F = w_down.shape[0]
    tm, tn = _largest_tile(S, MM_TILE), _largest_tile(F, MM_TILE // 2)

    def body(d_ref, w_ref, gt_ref, up_ref, dg_ref, du_ref):
        da, g, u = _dot(d_ref[...], w_ref[...], "nt"), gt_ref[...], up_ref[...]
        s = _sigmoid(g)
        dg_ref[...] = (da * u * (s * (1.0 + g * (1.0 - s)))).astype(dg_ref.dtype)
        du_ref[...] = (da * (g * s)).astype(du_ref.dtype)

    o_spec = pl.BlockSpec((tm, tn), lambda i, j: (i, j))
    return pl.pallas_call(
        body, name="ffn_dact", grid=(S // tm, F // tn),
        in_specs=[pl.BlockSpec((tm, D), lambda i, j: (i, 0)), pl.BlockSpec((tn, D), lambda i, j: (j, 0)), o_spec, o_spec],
        out_specs=[o_spec, o_spec],
        out_shape=[jax.ShapeDtypeStruct((S, F), _MXU), jax.ShapeDtypeStruct((S, F), _MXU)],
        compiler_params=_cp(("parallel", "parallel")))(dh2, w_down, gt, up)


def _rms_fwd(x, w, name):
    S, D = x.shape
    tr = _pick(S, (256, 128))

    def body(x_ref, w_ref, xn_ref, rs_ref):
        xv = x_ref[...]
        rs = lax.rsqrt(jnp.mean(xv * xv, axis=-1, keepdims=True) + EPS)
        xn_ref[...] = ((xv * rs) * w_ref[...]).astype(xn_ref.dtype)
        rs_ref[...] = rs

    return pl.pallas_call(
        body, name=name, grid=(S // tr,),
        in_specs=[pl.BlockSpec((tr, D), lambda i: (i, 0)), pl.BlockSpec((1, D), lambda i: (0, 0))],
        out_specs=[pl.BlockSpec((tr, D), lambda i: (i, 0)), pl.BlockSpec((tr, 1), lambda i: (i, 0))],
        out_shape=[jax.ShapeDtypeStruct((S, D), _MXU), jax.ShapeDtypeStruct((S, 1), F32)],
        compiler_params=_cp(("parallel",)))(x, w)


def _rms_bwd(dyn, x, rs, w, res, name):
    S, D = x.shape
    tr = _pick(S, (256, 128))

    def body(dy_ref, x_ref, rs_ref, w_ref, res_ref, dx_ref, dxb_ref, dw_ref):
        @pl.when(pl.program_id(0) == 0)
        def _():
            dw_ref[...] = jnp.zeros_like(dw_ref)

        dy, r = dy_ref[...].astype(F32), rs_ref[...]
        xhat = x_ref[...] * r
        dw_ref[...] += jnp.sum(dy * xhat, axis=0, keepdims=True)
        dxhat = dy * w_ref[...]
        dx = res_ref[...] + r * (dxhat - xhat * jnp.mean(dxhat * xhat, axis=-1, keepdims=True))
        dx_ref[...] = dx
        dxb_ref[...] = dx.astype(dxb_ref.dtype)

    row = pl.BlockSpec((tr, D), lambda i: (i, 0))
    vec = pl.BlockSpec((1, D), lambda i: (0, 0))
    return pl.pallas_call(
        body, name=name, grid=(S // tr,),
        in_specs=[row, row, pl.BlockSpec((tr, 1), lambda i: (i, 0)), vec, row], out_specs=[row, row, vec],
        out_shape=[jax.ShapeDtypeStruct((S, D), F32), jax.ShapeDtypeStruct((S, D), _MXU), jax.ShapeDtypeStruct((1, D), F32)],
        compiler_params=_cp(("arbitrary",)))(dyn, x, rs, w, res)


def _final_loss(h2, w, tgt):
    S, D = h2.shape
    tr = _pick(S, (256, 128))

    def body(h_ref, w_ref, t_ref, loss_ref, dh_ref, dhb_ref, dw_ref):
        @pl.when(pl.program_id(0) == 0)
        def _():
            dw_ref[...] = jnp.zeros_like(dw_ref)
            loss_ref[...] = jnp.zeros_like(loss_ref)

        hv, wv = h_ref[...], w_ref[...]
        rs = lax.rsqrt(jnp.mean(hv * hv, axis=-1, keepdims=True) + EPS)
        xhat = hv * rs
        err = xhat * wv - t_ref[...]
        row = jnp.mean(err * err, axis=-1, keepdims=True)
        loss_ref[...] += jnp.broadcast_to(0.5 * jnp.sum(row, axis=0, keepdims=True), loss_ref.shape)
        dy = err * (1.0 / D)
        dw_ref[...] += jnp.sum(dy * xhat, axis=0, keepdims=True)
        dxhat = dy * wv
        dh = rs * (dxhat - xhat * jnp.mean(dxhat * xhat, axis=-1, keepdims=True))
        dh_ref[...] = dh
        dhb_ref[...] = dh.astype(dhb_ref.dtype)

    row = pl.BlockSpec((tr, D), lambda i: (i, 0))
    vec = pl.BlockSpec((1, D), lambda i: (0, 0))
    return pl.pallas_call(
        body, name="final_loss", grid=(S // tr,), in_specs=[row, vec, row],
        out_specs=[pl.BlockSpec((1, LANE), lambda i: (0, 0)), row, row, vec],
        out_shape=[jax.ShapeDtypeStruct((1, LANE), F32), jax.ShapeDtypeStruct((S, D), F32), jax.ShapeDtypeStruct((S, D), _MXU),
                   jax.ShapeDtypeStruct((1, D), F32)],
        compiler_params=_cp(("arbitrary",)))(h2, w, tgt)


def _shift_rows(x, k, rows):
    if k == 0:
        return x
    S = x.shape[0]
    r = pltpu.roll(x, k % S, axis=0)
    ok = (rows >= k) if k > 0 else (rows < S + k)
    return jnp.where(ok, r, 0.0)


XBC_COL0 = SSD_WIDTH // 128


def _conv_fwd(proj, conv_w, conv_b):
    S = proj.shape[0]
    nct = CONV_CH // 128

    def body(x_ref, w_ref, b_ref, o_ref):
        x = x_ref[...]
        rows = lax.broadcasted_iota(jnp.int32, x.shape, 0)
        c = b_ref[...] + w_ref[3:4, :] * x
        for k in range(1, CONV_K):
            c = c + w_ref[3 - k:4 - k, :] * _shift_rows(x, k, rows)
        o_ref[...] = c * _sigmoid(c)

    return pl.pallas_call(
        body, name="conv_fwd", grid=(nct,),
        in_specs=[pl.BlockSpec((S, 128), lambda j: (0, XBC_COL0 + j)), pl.BlockSpec((CONV_K, 128), lambda j: (0, j)),
                  pl.BlockSpec((1, 128), lambda j: (0, j))],
        out_specs=pl.BlockSpec((S, 128), lambda j: (0, j)),
        out_shape=jax.ShapeDtypeStruct((S, CONV_CH), F32), compiler_params=_cp(("parallel",)))(proj, conv_w, conv_b)


def _conv_bwd(proj, conv_w, conv_b, dxa):
    S = proj.shape[0]
    nct = CONV_CH // 128

    def body(x_ref, w_ref, b_ref, d_ref, dx_ref, dw_ref, db_ref):
        x = x_ref[...]
        rows = lax.broadcasted_iota(jnp.int32, x.shape, 0)
        xs = [_shift_rows(x, k, rows) for k in range(CONV_K)]
        c = b_ref[...] + w_ref[3:4, :] * x
        for k in range(1, CONV_K):
            c = c + w_ref[3 - k:4 - k, :] * xs[k]
        s = _sigmoid(c)
        dc = d_ref[...] * (s * (1.0 + c * (1.0 - s)))
        dx = w_ref[3:4, :] * dc
        for k in range(1, CONV_K):
            dx = dx + w_ref[3 - k:4 - k, :] * _shift_rows(dc, -k, rows)
        dx_ref[...] = dx.astype(dx_ref.dtype)
        for k in range(CONV_K):
            dw_ref[3 - k:4 - k, :] = jnp.sum(dc * xs[k], axis=0, keepdims=True)
        db_ref[...] = jnp.sum(dc, axis=0, keepdims=True)

    col = pl.BlockSpec((S, 128), lambda j: (0, j))
    return pl.pallas_call(
        body, name="conv_bwd", grid=(nct,),
        in_specs=[pl.BlockSpec((S, 128), lambda j: (0, XBC_COL0 + j)), pl.BlockSpec((CONV_K, 128), lambda j: (0, j)),
                  pl.BlockSpec((1, 128), lambda j: (0, j)), col],
        out_specs=[col, pl.BlockSpec((CONV_K, 128), lambda j: (0, j)), pl.BlockSpec((1, 128), lambda j: (0, j))],
        out_shape=[jax.ShapeDtypeStruct((S, CONV_CH), _MXU), jax.ShapeDtypeStruct((CONV_K, CONV_CH), F32),
                   jax.ShapeDtypeStruct((1, CONV_CH), F32)],
        compiler_params=_cp(("parallel",)))(proj, conv_w, conv_b, dxa)


def _ssd_consts():
    L = SSD_L
    r = lax.broadcasted_iota(jnp.int32, (L, L), 0)
    c = lax.broadcasted_iota(jnp.int32, (L, L), 1)
    causal = r >= c
    upper = (r <= c).astype(F32)
    hr = lax.broadcasted_iota(jnp.int32, (SSD_HEADS, SSD_WIDTH), 0)
    hc = lax.broadcasted_iota(jnp.int32, (SSD_HEADS, SSD_WIDTH), 1)
    expand = (lax.shift_right_logical(hc, 6) == hr).astype(F32)
    return causal, causal.astype(F32), upper, expand


def _softplus(x):
    return jnp.maximum(x, 0.0) + jnp.log(1.0 + jnp.exp(-jnp.abs(x)))


def _ssd_scalars(dtr, dt_bias, a_log, tri, upper, expand):
    dt = _softplus(dtr + dt_bias)
    A = -jnp.exp(a_log)
    adt = dt * A
    acum = _dot(tri, adt, "nn", hi=True)
    acum_t = _dot(adt, upper, "tn", hi=True)
    alast = acum[SSD_L - 1:SSD_L, :]
    e = jnp.exp(acum)
    wdec = jnp.exp(alast - acum)
    gam = jnp.exp(alast)
    ex = lambda t: _dot(t, expand, "nn", hi=True)
    gam8 = jnp.broadcast_to(gam, (8, SSD_HEADS))
    return dt, A, acum, acum_t, e, wdec, gam, ex(dt), ex(e), ex(wdec), ex(gam8)[0:1, :]


def _ssd_fwd(proj, proj_small, xa, dt_bias, a_log, d_skip, norm_w):
    S = proj.shape[0]
    L, N, W = SSD_L, SSD_N, SSD_WIDTH
    nc = S // L

    def body(z_ref, xa_ref, dtr_ref, dtb_ref, al_ref, dsk_ref, nw_ref, yo_ref, y_ref, rs_ref, hs_ref, h_scr, y_scr):
        @pl.when(pl.program_id(0) == 0)
        def _():
            h_scr[...] = jnp.zeros_like(h_scr)

        causal, tri, upper, expand = _ssd_consts()
        dt, A, acum, acum_t, e, wdec, gam, dtE, eE, wE, gamE = _ssd_scalars(dtr_ref[:, 0:SSD_HEADS], dtb_ref[...], al_ref[...], tri, upper, expand)
        xs = xa_ref[:, 0:W]
        X = xs * dtE
        XW = X * wE
        hs_ref[0] = h_scr[...]
        for g in range(SSD_G):
            gs = slice(g * 512, (g + 1) * 512)
            Bg = xa_ref[:, W + g * N:W + (g + 1) * N]
            Cg = xa_ref[:, W + SSD_G * N + g * N:W + SSD_G * N + (g + 1) * N]
            Hg = h_scr[:, gs]
            CB = _dot(Cg, Bg, "nt")
            yoff = _dot(Cg, Hg, "nn") * eE[:, gs]
            st = _dot(Bg, XW[:, gs], "tn")
            for j in range(8):
                h = g * 8 + j
                hsl = slice(h * SSD_P, (h + 1) * SSD_P)
                lam = jnp.exp(jnp.where(causal, acum[:, h:h + 1] - acum_t[h:h + 1, :], -jnp.inf))
                y_scr[:, hsl] = _dot(CB * lam, X[:, hsl], "nn") + yoff[:, j * SSD_P:(j + 1) * SSD_P]
            h_scr[:, gs] = gamE[:, gs] * Hg + st
        dskE = _dot(jnp.broadcast_to(dsk_ref[...], (8, SSD_HEADS)), expand, "nn", hi=True)[0:1, :]
        y = y_scr[...] + dskE * xs
        y_ref[...] = y
        zv = z_ref[...]
        yg = y * (zv * _sigmoid(zv))
        rs = lax.rsqrt(jnp.mean(yg * yg, axis=-1, keepdims=True) + EPS)
        rs_ref[...] = rs
        yo_ref[...] = ((yg * rs) * nw_ref[...]).astype(yo_ref.dtype)

    p16 = pl.BlockSpec((1, SSD_HEADS), lambda c: (0, 0))
    return pl.pallas_call(
        body, name="ssd_fwd", grid=(nc,),
        in_specs=[pl.BlockSpec((L, W), lambda c: (c, 0)), pl.BlockSpec((L, CONV_CH), lambda c: (c, 0)),
                  pl.BlockSpec((L, W_SMALL), lambda c: (c, 0)), p16, p16, p16, pl.BlockSpec((1, W), lambda c: (0, 0))],
        out_specs=[pl.BlockSpec((L, W), lambda c: (c, 0)), pl.BlockSpec((L, W), lambda c: (c, 0)),
                   pl.BlockSpec((L, 1), lambda c: (c, 0)), pl.BlockSpec((1, N, W), lambda c: (c, 0, 0))],
        out_shape=[jax.ShapeDtypeStruct((S, W), _MXU), jax.ShapeDtypeStruct((S, W), F32), jax.ShapeDtypeStruct((S, 1), F32),
                   jax.ShapeDtypeStruct((nc, N, W), F32)],
        scratch_shapes=[pltpu.VMEM((N, W), F32), pltpu.VMEM((L, W), F32)],
        compiler_params=_cp(("arbitrary",)))(proj, xa, proj_small, dt_bias, a_log, d_skip, norm_w)


def _ssd_bwd(dmixed, proj, proj_small, xa, y, rs2, hs, dt_bias, a_log, d_skip, norm_w):
    S = proj.shape[0]
    L, N, W, H = SSD_L, SSD_N, SSD_WIDTH, SSD_HEADS
    nc = S // L

    def body(dyo_ref, z_ref, xa_ref, dtr_ref, y_ref, rs_ref, hs_ref, dtb_ref, al_ref, dsk_ref, nw_ref,
             dz_ref, dxa_ref, ddtr_ref, ddtb_ref, dal_ref, ddsk_ref, dnw_ref, dh_scr, dx_scr):
        @pl.when(pl.program_id(0) == 0)
        def _():
            dh_scr[...] = jnp.zeros_like(dh_scr)
            ddtb_ref[...] = jnp.zeros_like(ddtb_ref)
            dal_ref[...] = jnp.zeros_like(dal_ref)
            ddsk_ref[...] = jnp.zeros_like(ddsk_ref)
            dnw_ref[...] = jnp.zeros_like(dnw_ref)

        causal, tri, upper, expand = _ssd_consts()
        heads = lambda t: _dot(t, expand, "nt", hi=True)
        onehot = lambda h: (lax.broadcasted_iota(jnp.int32, (1, H), 1) == h).astype(F32)

        zv, yv, rs = z_ref[...], y_ref[...], rs_ref[...]
        sz = _sigmoid(zv)
        zs = zv * sz
        xhat = (yv * zs) * rs
        dyo = dyo_ref[...].astype(F32)
        dnw_ref[...] += jnp.sum(dyo * xhat, axis=0, keepdims=True)
        dxhat = dyo * nw_ref[...]
        dyg = rs * (dxhat - xhat * jnp.mean(dxhat * xhat, axis=-1, keepdims=True))
        dz_ref[...] = (dyg * yv * (sz * (1.0 + zv * (1.0 - sz)))).astype(dz_ref.dtype)
        dy = dyg * zs

        dtr = dtr_ref[:, 0:H]
        dt, A, acum, acum_t, e, wdec, gam, dtE, eE, wE, gamE = _ssd_scalars(dtr, dtb_ref[...], al_ref[...], tri, upper, expand)
        xs = xa_ref[:, 0:W]
        X = xs * dtE
        XW = X * wE
        dskE = _dot(jnp.broadcast_to(dsk_ref[...], (8, H)), expand, "nn", hi=True)[0:1, :]
        ddsk_ref[...] += heads(jnp.broadcast_to(jnp.sum(dy * xs, axis=0, keepdims=True), (8, W)))[0:1, :]

        dYe = dy * eE
        dacum = jnp.zeros((L, H), F32)
        de_full = []
        dw_full = []
        dgam_full = []
        for g in range(SSD_G):
            gs = slice(g * 512, (g + 1) * 512)
            Bg = xa_ref[:, W + g * N:W + (g + 1) * N]
            Cg = xa_ref[:, W + SSD_G * N + g * N:W + SSD_G * N + (g + 1) * N]
            Hg = hs_ref[0, :, gs]
            dHn = dh_scr[:, gs]
            CH = _dot(Cg, Hg, "nn")
            de_full.append(dy[:, gs] * CH)
            dC = _dot(dYe[:, gs], Hg, "nt")
            dHs = gamE[:, gs] * dHn + _dot(Cg, dYe[:, gs], "tn")
            dgam_full.append(jnp.sum(dHn * Hg, axis=0, keepdims=True))
            BdS = _dot(Bg, dHn, "nn")
            dB = _dot(XW[:, gs], dHn, "nt")
            dx_scr[:, gs] = BdS * wE[:, gs]
            dw_full.append(BdS * X[:, gs])
            CB = _dot(Cg, Bg, "nt")
            dCB = jnp.zeros((L, L), F32)
            for j in range(8):
                h = g * 8 + j
                hsl = slice(h * SSD_P, (h + 1) * SSD_P)
                lam = jnp.exp(jnp.where(causal, acum[:, h:h + 1] - acum_t[h:h + 1, :], -jnp.inf))
                M = CB * lam
                dM = _dot(dy[:, hsl], X[:, hsl], "nt")
                dx_scr[:, hsl] += _dot(M, dy[:, hsl], "tn")
                dCB = dCB + dM * lam
                Q = dM * M
                rowsum = jnp.sum(Q, axis=1, keepdims=True)
                colsum = _dot(Q, jnp.ones((L, 8), F32), "tn", hi=True)[:, 0:1]
                dacum = dacum + (rowsum - colsum) * onehot(h)
            dC = dC + _dot(dCB, Bg, "nn")
            dB = dB + _dot(dCB, Cg, "tn")
            dxa_ref[:, W + g * N:W + (g + 1) * N] = dB
            dxa_ref[:, W + SSD_G * N + g * N:W + SSD_G * N + (g + 1) * N] = dC
            dh_scr[:, gs] = dHs

        de16 = heads(jnp.concatenate(de_full, axis=1))
        dw16 = heads(jnp.concatenate(dw_full, axis=1))
        dgam16 = heads(jnp.broadcast_to(jnp.concatenate(dgam_full, axis=1), (8, W)))[0:1, :]
        dacum = dacum + de16 * e - dw16 * wdec
        dlast = jnp.sum(dw16 * wdec, axis=0, keepdims=True) + dgam16 * gam
        lastrow = (lax.broadcasted_iota(jnp.int32, (L, 1), 0) == L - 1).astype(F32)
        dacum = dacum + lastrow * dlast
        da = _dot(tri, dacum, "tn", hi=True)
        dX = dx_scr[...]
        ddt = da * A + heads(dX * xs)
        dA = jnp.sum(da * dt, axis=0, keepdims=True)
        dal_ref[...] += dA * A
        ddtr = ddt * _sigmoid(dtr + dtb_ref[...])
        ddtb_ref[...] += jnp.sum(ddtr, axis=0, keepdims=True)
        ddtr_ref[...] = ddtr
        dxa_ref[:, 0:W] = dX * dtE + dy * dskE

    p16 = pl.BlockSpec((1, H), lambda c: (0, 0))
    rev = lambda c: (nc - 1 - c, 0)
    return pl.pallas_call(
        body, name="ssd_bwd", grid=(nc,),
        in_specs=[pl.BlockSpec((L, W), rev), pl.BlockSpec((L, W), rev), pl.BlockSpec((L, CONV_CH), rev),
                  pl.BlockSpec((L, W_SMALL), rev), pl.BlockSpec((L, W), rev), pl.BlockSpec((L, 1), rev),
                  pl.BlockSpec((1, N, W), lambda c: (nc - 1 - c, 0, 0)), p16, p16, p16, pl.BlockSpec((1, W), lambda c: (0, 0))],
        out_specs=[pl.BlockSpec((L, W), rev), pl.BlockSpec((L, CONV_CH), rev), pl.BlockSpec((L, H), rev),
                   p16, p16, p16, pl.BlockSpec((1, W), lambda c: (0, 0))],
        out_shape=[jax.ShapeDtypeStruct((S, W), _MXU), jax.ShapeDtypeStruct((S, CONV_CH), F32), jax.ShapeDtypeStruct((S, H), F32),
                   jax.ShapeDtypeStruct((1, H), F32), jax.ShapeDtypeStruct((1, H), F32), jax.ShapeDtypeStruct((1, H), F32),
                   jax.ShapeDtypeStruct((1, W), F32)],
        scratch_shapes=[pltpu.VMEM((N, W), F32), pltpu.VMEM((L, W), F32)],
        compiler_params=_cp(("arbitrary",)))(dmixed, proj, xa, proj_small, y, rs2, hs, dt_bias, a_log, d_skip, norm_w)


def _rope_tables(S):
    inv = 1.0 / (ROPE_THETA ** (jnp.arange(0, ROPE_DIM, 2, dtype=F32) / ROPE_DIM))
    ang = jnp.arange(S, dtype=F32)[:, None] * inv[None, :]
    cos, sin = jnp.cos(ang), jnp.sin(ang)
    half = ROPE_DIM // 2
    c64 = jnp.concatenate([cos, cos, jnp.ones((S, HD - ROPE_DIM), F32)], axis=1)
    s64 = jnp.concatenate([sin, sin, jnp.zeros((S, HD - ROPE_DIM), F32)], axis=1)
    del half
    return jnp.concatenate([c64, c64], axis=1), jnp.concatenate([s64, s64], axis=1)


def _rope(xs, blk0, width, cos, sin, sign, out_dtype, name, extra=None):
    S = xs[0].shape[0]
    tr = _pick(S, (512, 256, 128))
    nx = len(xs)

    def body(*refs):
        x_refs, c_ref, s_ref = refs[:nx], refs[nx], refs[nx + 1]
        e_ref = refs[nx + 2] if extra is not None else None
        o_ref = refs[-1]
        cv, sv = c_ref[...], s_ref[...] * sign
        lane = lax.broadcasted_iota(jnp.int32, (tr, 128), 1)
        first = (lane & (HD - 1)) < (ROPE_DIM // 2)
        for j in range(2):
            cs = slice(j * 128, (j + 1) * 128)
            xv = x_refs[0][:, cs].astype(F32)
            for r in x_refs[1:]:
                xv = xv + r[:, cs].astype(F32)
            rot = jnp.where(first, -pltpu.roll(xv, 128 - ROPE_DIM // 2, axis=1), pltpu.roll(xv, ROPE_DIM // 2, axis=1))
            out = xv * cv + rot * sv
            if extra is not None:
                out = out + e_ref[:, cs].astype(F32)
            o_ref[:, cs] = out.astype(out_dtype)

    t128 = pl.BlockSpec((tr, 128), lambda i, j: (i, 0))
    oblk = pl.BlockSpec((tr, 256), lambda i, j: (i, j))
    specs = [pl.BlockSpec((tr, 256), lambda i, j: (i, blk0 + j))] * nx + [t128, t128]
    ins = list(xs) + [cos, sin]
    if extra is not None:
        ins.append(extra[0])
        eb = extra[1]
        specs.append(pl.BlockSpec((tr, 256), lambda i, j: (i, eb + j)))
    return pl.pallas_call(
        body, name=name, grid=(S // tr, width // 256), in_specs=specs, out_specs=oblk,
        out_shape=jax.ShapeDtypeStruct((S, width), out_dtype), compiler_params=_cp(("parallel", "parallel")))(*ins)


def _rotate128(xv, cv, sv, first):
    rot = jnp.where(first, -pltpu.roll(xv, 128 - ROPE_DIM // 2, axis=1), pltpu.roll(xv, ROPE_DIM // 2, axis=1))
    return xv * cv + rot * sv


def _kv_prep(proj, cos, sin, tk):
    S = proj.shape[0]
    NB = S // SEL_BLOCK

    def body(ks_ref, vs_ref, kw_ref, vw_ref, c_ref, s_ref, *outs):
        cv, sv = c_ref[...], s_ref[...]
        lane = lax.broadcasted_iota(jnp.int32, (tk, 128), 1)
        first = (lane & (HD - 1)) < (ROPE_DIM // 2)
        key = pl.program_id(0) * tk + lax.broadcasted_iota(jnp.int32, (tk, NB), 0)
        onehot = (lax.shift_right_logical(key, 6) == lax.broadcasted_iota(jnp.int32, (tk, NB), 1)).astype(F32)
        for j, (ref, rotated) in enumerate(((ks_ref, True), (vs_ref, False), (kw_ref, True), (vw_ref, False))):
            nat, blk = outs[2 * j], outs[2 * j + 1]
            for half in range(2):
                xv = ref[:, half * 128:(half + 1) * 128]
                if rotated:
                    xv = _rotate128(xv, cv, sv, first)
                for e in range(2):
                    h = 2 * half + e
                    piece = xv[:, e * HD:(e + 1) * HD]
                    nat[h] = (jnp.concatenate([piece, onehot], axis=1) if j == 0 else piece).astype(nat.dtype)
                    blk[h, 0] = piece.T.astype(blk.dtype)

    col = lambda b: pl.BlockSpec((tk, 256), lambda i: (i, b))
    t128 = pl.BlockSpec((tk, 128), lambda i: (i, 0))
    nat_spec = lambda w: pl.BlockSpec((N_KV, tk, w), lambda i: (0, i, 0))
    blk_spec = pl.BlockSpec((N_KV, 1, HD, tk), lambda i: (0, i, 0, 0))
    nat_shape = lambda w: jax.ShapeDtypeStruct((N_KV, S, w), _MXU)
    blk_shape = jax.ShapeDtypeStruct((N_KV, S // tk, HD, tk), _MXU)
    widths = (HD + NB, HD, HD, HD)
    res = pl.pallas_call(
        body, name="kv_prep", grid=(S // tk,), in_specs=[col(KSB), col(VSB), col(KWB), col(VWB), t128, t128],
        out_specs=[s for w in widths for s in (nat_spec(w), blk_spec)],
        out_shape=[s for w in widths for s in (nat_shape(w), blk_shape)],
        compiler_params=_cp(("parallel",)))(proj, proj, proj, proj, cos, sin)
    return dict(ks_ext=res[0], ks_t=res[1], vs=res[2], vs_t=res[3], kw=res[4], kw_t=res[5], vw=res[6], vw_t=res[7])


def _dkv_post(dks, dvs, dkw, dvw, cos, sin):
    S = dks.shape[1]
    tr = _pick(S, (512, 256, 128))

    def body(dks_ref, dvs_ref, dkw_ref, dvw_ref, c_ref, s_ref, o_ref):
        cv, sv = c_ref[...], -s_ref[...]
        lane = lax.broadcasted_iota(jnp.int32, (tr, 128), 1)
        first = (lane & (HD - 1)) < (ROPE_DIM // 2)
        for j, (ref, rotated) in enumerate(((dks_ref, True), (dvs_ref, False), (dkw_ref, True), (dvw_ref, False))):
            for half in range(2):
                xv = jnp.concatenate([ref[2 * half], ref[2 * half + 1]], axis=1)
                if rotated:
                    xv = _rotate128(xv, cv, sv, first)
                o_ref[:, j * 256 + half * 128:j * 256 + (half + 1) * 128] = xv.astype(o_ref.dtype)

    hm = pl.BlockSpec((N_KV, tr, HD), lambda i: (0, i, 0))
    t128 = pl.BlockSpec((tr, 128), lambda i: (i, 0))
    return pl.pallas_call(
        body, name="dkv_post", grid=(S // tr,), in_specs=[hm, hm, hm, hm, t128, t128],
        out_specs=pl.BlockSpec((tr, 4 * 256), lambda i: (i, 0)), out_shape=jax.ShapeDtypeStruct((S, 4 * 256), _MXU),
        compiler_params=_cp(("parallel",)))(dks, dvs, dkw, dvw, cos, sin)


def _compress_fwd(R, pe, w1, w2):
    NC = R.shape[1]
    half = 16 * HD

    def body(r_ref, pe_ref, w1_ref, w2_ref, o_ref, hid_ref):
        r = r_ref[0]
        a = _dot(r + pe_ref[:, 0:half], w1_ref[0:half, :], "nn")
        b = _dot(r + pe_ref[:, half:2 * half], w1_ref[half:2 * half, :], "nn")
        hid = a + pltpu.roll(b, NC - 1, axis=0)
        hid_ref[0] = hid
        out = _dot(hid * _sigmoid(hid), w2_ref[...], "nn")
        rows = lax.broadcasted_iota(jnp.int32, out.shape, 0)
        o_ref[0] = jnp.where(rows < NC - 1, out, 0.0).astype(o_ref.dtype)

    return pl.pallas_call(
        body, name="compress_fwd", grid=(N_KV,),
        in_specs=[pl.BlockSpec((1, NC, half), lambda h: (h, 0, 0)), pl.BlockSpec((1, 2 * half), lambda h: (0, 0)),
                  pl.BlockSpec((2 * half, CMP_HID), lambda h: (0, 0)), pl.BlockSpec((CMP_HID, HD), lambda h: (0, 0))],
        out_specs=[pl.BlockSpec((1, NC, HD), lambda h: (h, 0, 0)), pl.BlockSpec((1, NC, CMP_HID), lambda h: (h, 0, 0))],
        out_shape=[jax.ShapeDtypeStruct((N_KV, NC, HD), _MXU), jax.ShapeDtypeStruct((N_KV, NC, CMP_HID), F32)],
        compiler_params=_cp(("parallel",)))(R, pe, w1, w2)


def _compress_bwd(R, pe, w1, w2, hid, dout):
    NC = R.shape[1]
    half = 16 * HD

    def body(r_ref, pe_ref, w1_ref, w2_ref, hid_ref, do_ref, dr_ref, dw1_ref, dw2_ref, dpe_ref):
        @pl.when(pl.program_id(0) == 0)
        def _():
            dw1_ref[...] = jnp.zeros_like(dw1_ref)
            dw2_ref[...] = jnp.zeros_like(dw2_ref)
            dpe_ref[...] = jnp.zeros_like(dpe_ref)

        r, hv, do = r_ref[0], hid_ref[0], do_ref[0]
        s = _sigmoid(hv)
        dw2_ref[...] += _dot(hv * s, do, "tn")
        dhid = _dot(do, w2_ref[...], "nt") * (s * (1.0 + hv * (1.0 - s)))
        rows = lax.broadcasted_iota(jnp.int32, dhid.shape, 0)
        dhid = jnp.where(rows < NC - 1, dhid, 0.0)
        dhid_dn = pltpu.roll(dhid, 1, axis=0)
        dw1_ref[0:half, :] += _dot(r + pe_ref[:, 0:half], dhid, "tn")
        dw1_ref[half:2 * half, :] += _dot(r + pe_ref[:, half:2 * half], dhid_dn, "tn")
        dxt = _dot(dhid, w1_ref[0:half, :], "nt")
        dxb = _dot(dhid_dn, w1_ref[half:2 * half, :], "nt")
        dr_ref[0] = dxt + dxb
        dpe_ref[:, 0:half] += jnp.sum(dxt, axis=0, keepdims=True)
        dpe_ref[:, half:2 * half] += jnp.sum(dxb, axis=0, keepdims=True)

    return pl.pallas_call(
        body, name="compress_bwd", grid=(N_KV,),
        in_specs=[pl.BlockSpec((1, NC, half), lambda h: (h, 0, 0)), pl.BlockSpec((1, 2 * half), lambda h: (0, 0)),
                  pl.BlockSpec((2 * half, CMP_HID), lambda h: (0, 0)), pl.BlockSpec((CMP_HID, HD), lambda h: (0, 0)),
                  pl.BlockSpec((1, NC, CMP_HID), lambda h: (h, 0, 0)), pl.BlockSpec((1, NC, HD), lambda h: (h, 0, 0))],
        out_specs=[pl.BlockSpec((1, NC, half), lambda h: (h, 0, 0)), pl.BlockSpec((2 * half, CMP_HID), lambda h: (0, 0)),
                   pl.BlockSpec((CMP_HID, HD), lambda h: (0, 0)), pl.BlockSpec((1, 2 * half), lambda h: (0, 0))],
        out_shape=[jax.ShapeDtypeStruct((N_KV, NC, half), F32), jax.ShapeDtypeStruct((2 * half, CMP_HID), F32),
                   jax.ShapeDtypeStruct((CMP_HID, HD), F32), jax.ShapeDtypeStruct((1, 2 * half), F32)],
        compiler_params=_cp(("arbitrary",)))(R, pe, w1, w2, hid, dout)


def _attn_cfg(S, Sk, mode):
    tq = _pick(S, (256, 128))
    tk = Sk if mode == "cmp" else _pick(Sk, (256, 128))
    return tq, tk


def _block_start(kb, tk):
    return kb * tk if isinstance(kb, int) else pl.multiple_of(kb * tk, tk)


def _pipelined_key_blocks(mode, q0, tq, tk, produce, consume):
    if mode == "cmp":
        produce(0, True, 0)
        consume(0, 0)
        return
    if mode == "sel":
        first, n_plain, plain_masked = 0, q0 // tk, False
    else:
        first = jnp.maximum(q0 - (WINDOW - 1), 0) // tk
        n_plain, plain_masked = (q0 + tq - 1) // tk - first, True
    last = first + n_plain
    pairs = jnp.maximum(n_plain - 1, 0) // 2

    @pl.when(n_plain >= 1)
    def _():
        produce(first, plain_masked, 0)

    def two(j, carry):
        kb = first + 2 * j
        produce(kb + 1, plain_masked, 1)
        consume(kb, 0)
        produce(kb + 2, plain_masked, 0)
        consume(kb + 1, 1)
        return carry

    lax.fori_loop(0, pairs, two, 0)
    kb = first + 2 * pairs
    left = n_plain - 2 * pairs

    @pl.when(left == 2)
    def _():
        produce(kb + 1, plain_masked, 1)
        consume(kb, 0)
        produce(last, True, 0)
        consume(kb + 1, 1)
        consume(last, 0)

    @pl.when(left == 1)
    def _():
        produce(last, True, 1)
        consume(kb, 0)
        consume(last, 1)

    @pl.when(left == 0)
    def _():
        produce(last, True, 0)
        consume(last, 0)


def _attn_bias(mode, q0, k0, tq, tk):
    k = k0 + lax.broadcasted_iota(jnp.int32, (tk, tq), 0)
    t = q0 + lax.broadcasted_iota(jnp.int32, (tk, tq), 1)
    if mode == "cmp":
        ok = (k * 16 + 31) <= t
    elif mode == "win":
        ok = (k <= t) & ((t - k) < WINDOW)
    else:
        ok = k <= t
    bias = jnp.where(ok, 0.0, NEG)
    return jnp.concatenate([bias] * GRP, axis=1), jnp.concatenate([ok.astype(F32)] * GRP, axis=1)


def _sel_operands(qs, selneg_ref):
    return jnp.concatenate([qs, jnp.concatenate([selneg_ref[0]] * GRP, axis=0)], axis=1)


def _stack_heads(ref, tq):
    return jnp.concatenate([ref[:, g * HD:(g + 1) * HD] for g in range(GRP)], axis=0)


def _scaled_queries(q_ref, tq):
    return (_stack_heads(q_ref, tq).astype(F32) * SCALE).astype(_MXU)


def _blocked_t(x, tk):
    n, Sk, d = x.shape
    return x.reshape(n, Sk // tk, tk, d).transpose(0, 1, 3, 2)


def _head_rows(ref):
    return jnp.concatenate([ref[0, g:g + 1, :] for g in range(GRP)], axis=1)


def _attn_fwd(q, qcol0, k, vt, mode, selneg, gate, y_prev, y_dtype, name):
    S, Sk = q.shape[0], k.shape[1]
    tq, tk = _attn_cfg(S, Sk, mode)
    R = GRP * tq

    def body(*refs):
        q_ref, k_ref, vt_ref = refs[:3]
        rest = list(refs[3:])
        sel_ref = rest.pop(0) if mode == "sel" else None
        gate_ref = rest.pop(0)
        yp_ref = rest.pop(0) if y_prev is not None else None
        o_ref, lse_ref, y_ref, m_scr, l_scr, acc, s_scr = rest
        q0 = pl.program_id(1) * tq
        qs = _scaled_queries(q_ref, tq)
        m_scr[...] = jnp.full_like(m_scr, NEG)
        l_scr[...] = jnp.zeros_like(l_scr)
        acc[...] = jnp.zeros_like(acc)
        qk = _sel_operands(qs, sel_ref) if mode == "sel" else qs

        def produce(kb, masked, slot):
            k0 = _block_start(kb, tk)
            s = _dot(k_ref[0, pl.ds(k0, tk), :], qk, "nt")
            if masked:
                s = s + _attn_bias(mode, q0, k0, tq, tk)[0]
            s_scr[slot] = s

        def consume(kb, slot):
            s = s_scr[slot]
            m_old = m_scr[...]
            m_new = jnp.maximum(m_old, jnp.max(s, axis=0, keepdims=True))
            p = jnp.exp(s - m_new)
            if mode == "cmp":
                p = p * _attn_bias(mode, q0, 0, tq, tk)[1]
            alpha = jnp.exp(m_old - m_new)
            l_scr[...] = alpha * l_scr[...] + jnp.sum(p, axis=0, keepdims=True)
            acc[...] = alpha * acc[...] + _dot(vt_ref[0, kb], p, "nn")
            m_scr[...] = m_new

        _pipelined_key_blocks(mode, q0, tq, tk, produce, consume)
        l = l_scr[...]
        good = l > 0.0
        o_t = acc[...] * jnp.where(good, 1.0 / jnp.where(good, l, 1.0), 0.0)
        lse = jnp.where(good, m_scr[...] + jnp.log(jnp.where(good, l, 1.0)), -NEG)
        y_t = o_t * _sigmoid(_head_rows(gate_ref))
        for g in range(GRP):
            hs, qs_ = slice(g * HD, (g + 1) * HD), slice(g * tq, (g + 1) * tq)
            o_ref[:, hs] = o_t[:, qs_].T
            lse_ref[0, g:g + 1, :] = lse[:, qs_]
            yg = y_t[:, qs_].T
            if y_prev is not None:
                yg = yg + yp_ref[:, hs]
            y_ref[:, hs] = yg.astype(y_ref.dtype)

    row_spec = pl.BlockSpec((1, GRP, tq), lambda h, i: (h, 0, i))
    qo_spec = pl.BlockSpec((tq, GRP * HD), lambda h, i: (i, h))
    ins = [q, k, vt]
    specs = [pl.BlockSpec((tq, GRP * HD), lambda h, i: (i, qcol0 + h)), pl.BlockSpec((1, Sk, k.shape[2]), lambda h, i: (h, 0, 0)),
             pl.BlockSpec((1, Sk // tk, HD, tk), lambda h, i: (h, 0, 0, 0))]
    if mode == "sel":
        assert tq == tk
        ins.append(selneg)
        specs.append(pl.BlockSpec((1, tq, selneg.shape[2]), lambda h, i: (h, i, 0)))
    ins.append(gate)
    specs.append(row_spec)
    if y_prev is not None:
        ins.append(y_prev)
        specs.append(qo_spec)
    return pl.pallas_call(
        body, name=name, grid=(N_KV, S // tq), in_specs=specs, out_specs=[qo_spec, row_spec, qo_spec],
        out_shape=[jax.ShapeDtypeStruct((S, ATT_WIDTH), F32), jax.ShapeDtypeStruct((N_KV, GRP, S), F32),
                   jax.ShapeDtypeStruct((S, ATT_WIDTH), y_dtype)],
        scratch_shapes=[pltpu.VMEM((1, R), F32), pltpu.VMEM((1, R), F32), pltpu.VMEM((HD, R), F32), pltpu.VMEM((2, tk, R), F32)],
        compiler_params=_cp(("parallel", "arbitrary")))(*ins)


def _attn_bwd(q, qcol0, k, kt, v, o, lse, dy, dycol0, gate, mode, selneg, name):
    S, Sk = q.shape[0], k.shape[1]
    tq, tk = _attn_cfg(S, Sk, mode)
    R = GRP * tq

    def body(*refs):
        if mode == "sel":
            (q_ref, k_ref, kt_ref, v_ref, o_ref, lse_ref, dy_ref, gate_ref, sel_ref, dq_ref, dk_ref, dv_ref, dg_ref, dq_scr, s_scr,
             dp_scr) = refs
        else:
            q_ref, k_ref, kt_ref, v_ref, o_ref, lse_ref, dy_ref, gate_ref, dq_ref, dk_ref, dv_ref, dg_ref, dq_scr, s_scr, dp_scr = refs

        @pl.when(pl.program_id(1) == 0)
        def _():
            dk_ref[...] = jnp.zeros_like(dk_ref)
            dv_ref[...] = jnp.zeros_like(dv_ref)

        q0 = pl.program_id(1) * tq
        qs = _scaled_queries(q_ref, tq)
        dys = _stack_heads(dy_ref, tq)
        gv = _sigmoid(_head_rows(gate_ref))
        dy_o = _dot(jnp.ones((8, HD), F32), dys * _stack_heads(o_ref, tq), "nt", hi=True)[0:1, :]
        delta = gv * dy_o
        dgate = dy_o * (gv * (1.0 - gv))
        for g in range(GRP):
            dg_ref[0, g:g + 1, :] = dgate[:, g * tq:(g + 1) * tq]
        lsev = _head_rows(lse_ref)
        dos = (dys * jnp.broadcast_to(gv, (8, R)).T[:, 0:1]).astype(_MXU)
        dq_scr[...] = jnp.zeros_like(dq_scr)
        qk = _sel_operands(qs, sel_ref) if mode == "sel" else qs

        def produce(kb, masked, slot):
            k0 = _block_start(kb, tk)
            s = _dot(k_ref[0, pl.ds(k0, tk), :], qk, "nt")
            if masked:
                s = s + _attn_bias(mode, q0, k0, tq, tk)[0]
            s_scr[slot] = s
            dp_scr[slot] = _dot(v_ref[0, pl.ds(k0, tk), :], dos, "nt")

        def consume(kb, slot):
            k0 = _block_start(kb, tk)
            p = jnp.exp(s_scr[slot] - lsev)
            if mode == "cmp":
                p = p * _attn_bias(mode, q0, 0, tq, tk)[1]
            ds = p * (dp_scr[slot] - delta)
            dq_scr[...] += _dot(kt_ref[0, kb], ds, "nn")
            dk_ref[0, pl.ds(k0, tk), :] += _dot(ds, qs, "nn")
            dv_ref[0, pl.ds(k0, tk), :] += _dot(p, dos, "nn")

        _pipelined_key_blocks(mode, q0, tq, tk, produce, consume)
        for g in range(GRP):
            dq_ref[:, g * HD:(g + 1) * HD] = (dq_scr[:, g * tq:(g + 1) * tq] * SCALE).T

    kv_spec = pl.BlockSpec((1, Sk, HD), lambda h, i: (h, 0, 0))
    qo_spec = pl.BlockSpec((tq, GRP * HD), lambda h, i: (i, h))
    row_spec = pl.BlockSpec((1, GRP, tq), lambda h, i: (h, 0, i))
    ins = [q, k, kt, v, o, lse, dy, gate]
    specs = [pl.BlockSpec((tq, GRP * HD), lambda h, i: (i, qcol0 + h)), pl.BlockSpec((1, Sk, k.shape[2]), lambda h, i: (h, 0, 0)),
             pl.BlockSpec((1, Sk // tk, HD, tk), lambda h, i: (h, 0, 0, 0)), kv_spec, qo_spec, row_spec,
             pl.BlockSpec((tq, GRP * HD), lambda h, i: (i, dycol0 + h)), row_spec]
    if mode == "sel":
        assert tq == tk
        ins.append(selneg)
        specs.append(pl.BlockSpec((1, tq, selneg.shape[2]), lambda h, i: (h, i, 0)))
    return pl.pallas_call(
        body, name=name, grid=(N_KV, S // tq), in_specs=specs, out_specs=[qo_spec, kv_spec, kv_spec, row_spec],
        out_shape=[jax.ShapeDtypeStruct((S, ATT_WIDTH), F32), jax.ShapeDtypeStruct((N_KV, Sk, HD), F32),
                   jax.ShapeDtypeStruct((N_KV, Sk, HD), F32), jax.ShapeDtypeStruct((N_KV, GRP, S), F32)],
        scratch_shapes=[pltpu.VMEM((HD, R), F32), pltpu.VMEM((2, tk, R), F32), pltpu.VMEM((2, tk, R), F32)],
        compiler_params=_cp(("parallel", "arbitrary")))(*ins)


def _select(q, qcol0, k_cmp, lse):
    S, NC = q.shape[0], k_cmp.shape[1]
    NB = S // SEL_BLOCK
    tq = _pick(S, (256, 128))
    ci = np.arange(NC)[None, :] * 16
    sj = np.arange(NB)[:, None] * SEL_BLOCK
    ov_t = np.clip(np.minimum(ci + 32, sj + SEL_BLOCK) - np.maximum(ci, sj), 0, None) / 32.0
    ov_t[:, NC - 1] = 0.0
    ov_t = jnp.asarray(ov_t, F32)

    def body(q_ref, k_ref, lse_ref, ov_ref, sel_ref):
        q0 = pl.program_id(1) * tq
        bias, okf = _attn_bias("cmp", q0, 0, tq, NC)
        lsev = _head_rows(lse_ref)
        p = jnp.exp(_dot(k_ref[0], _scaled_queries(q_ref, tq), "nt") + bias - lsev) * okf
        imp4 = _dot(ov_ref[...], p, "nn")
        imp = imp4[:, 0:tq] + imp4[:, tq:2 * tq] + imp4[:, 2 * tq:3 * tq] + imp4[:, 3 * tq:4 * tq]
        blk = lax.broadcasted_iota(jnp.int32, (NB, tq), 0)
        cur = lax.shift_right_logical(q0 + lax.broadcasted_iota(jnp.int32, (NB, tq), 1), 6)
        imp = jnp.where((blk == 0) | (blk == cur) | (blk == cur - 1), FORCE, imp)
        imp = jnp.where(blk <= cur, imp, -1.0)
        rank = jnp.zeros((NB, tq), F32)
        for j in range(NB):
            row = imp[j:j + 1, :]
            ahead = (row > imp) | ((row == imp) & (blk > j))
            rank = rank + ahead.astype(F32)
        chosen = (rank < float(N_SELECT)) & (imp >= 0.0)
        sel_ref[0] = jnp.where(chosen, 0.0, NEG).T.astype(sel_ref.dtype)

    return pl.pallas_call(
        body, name="select_blocks", grid=(N_KV, S // tq),
        in_specs=[pl.BlockSpec((tq, GRP * HD), lambda h, i: (i, qcol0 + h)), pl.BlockSpec((1, NC, HD), lambda h, i: (h, 0, 0)),
                  pl.BlockSpec((1, GRP, tq), lambda h, i: (h, 0, i)), pl.BlockSpec((NB, NC), lambda h, i: (0, 0))],
        out_specs=pl.BlockSpec((1, tq, NB), lambda h, i: (h, i, 0)),
        out_shape=jax.ShapeDtypeStruct((N_KV, S, NB), _MXU), compiler_params=_cp(("parallel", "parallel")))(q, k_cmp, lse, ov_t)


def _to_rows16(x):
    S = x.shape[0]
    return x.reshape(S // 16, 16, N_KV, HD).transpose(2, 0, 1, 3).reshape(N_KV, S // 16, 16 * HD)


def _from_rows16(r):
    NC = r.shape[1]
    return r.reshape(N_KV, NC, 16, HD).transpose(1, 2, 0, 3).reshape(NC * 16, N_KV * HD)


DT_COL0 = SSD_WIDTH + CONV_CH
GATE_IN_COL0 = D_IN - 3 * N_HEADS


SHARD_IN = D_IN // N_DEV


def _orig_cols(ref, c0, width):
    pieces, c = [], c0
    while c < c0 + width:
        d, off = divmod(c, SHARD_IN)
        w = min(SHARD_IN - off, c0 + width - c)
        pieces.append(ref[d, :, off:off + w])
        c += w
    return pieces[0] if len(pieces) == 1 else jnp.concatenate(pieces, axis=1)


def _cols_from_slabs(slabs):
    _, R, c = slabs.shape
    tr = _pick(R, (256, 128))

    def body(s_ref, o_ref):
        for t in range(N_DEV * c // LANE):
            pieces, col = [], t * LANE
            while col < (t + 1) * LANE:
                d, off = divmod(col, c)
                w = min(c - off, (t + 1) * LANE - col)
                pieces.append(s_ref[d, :, off:off + w])
                col += w
            o_ref[:, t * LANE:(t + 1) * LANE] = pieces[0] if len(pieces) == 1 else jnp.concatenate(pieces, axis=1)

    return pl.pallas_call(
        body, name="cols_from_slabs", grid=(R // tr,), in_specs=[pl.BlockSpec((N_DEV, tr, c), lambda i: (0, i, 0))],
        out_specs=pl.BlockSpec((tr, N_DEV * c), lambda i: (i, 0)), out_shape=jax.ShapeDtypeStruct((R, N_DEV * c), slabs.dtype),
        compiler_params=_cp(("parallel",)))(slabs)


def _slabs_from_cols(x):
    R, c = x.shape[0], x.shape[1] // N_DEV
    tr = _pick(R, (256, 128))

    def body(x_ref, o_ref):
        for d in range(N_DEV):
            o_ref[d] = x_ref[:, d * c:(d + 1) * c]

    return pl.pallas_call(
        body, name="slabs_from_cols", grid=(R // tr,), in_specs=[pl.BlockSpec((tr, N_DEV * c), lambda i: (i, 0))],
        out_specs=pl.BlockSpec((N_DEV, tr, c), lambda i: (0, i, 0)), out_shape=jax.ShapeDtypeStruct((N_DEV, R, c), x.dtype),
        compiler_params=_cp(("parallel",)))(x)


def _w_in_from_slabs(slabs):
    D = slabs.shape[1]
    tr = _pick(D, (256, 128))

    def body(s_ref, main_ref, small_ref):
        for t in range(W_MAIN // LANE):
            c = t * LANE
            main_ref[:, c:c + LANE] = _orig_cols(s_ref, c if c < DT_COL0 else c + SSD_HEADS, LANE)
        small_ref[...] = jnp.concatenate(
            [_orig_cols(s_ref, DT_COL0, SSD_HEADS), _orig_cols(s_ref, GATE_IN_COL0, 3 * N_HEADS),
             jnp.zeros((tr, W_SMALL - SSD_HEADS - 3 * N_HEADS), small_ref.dtype)], axis=1)

    return pl.pallas_call(
        body, name="w_in_layout", grid=(D // tr,), in_specs=[pl.BlockSpec((N_DEV, tr, SHARD_IN), lambda i: (0, i, 0))],
        out_specs=[pl.BlockSpec((tr, W_MAIN), lambda i: (i, 0)), pl.BlockSpec((tr, W_SMALL), lambda i: (i, 0))],
        out_shape=[jax.ShapeDtypeStruct((D, W_MAIN), slabs.dtype), jax.ShapeDtypeStruct((D, W_SMALL), slabs.dtype)],
        compiler_params=_cp(("parallel",)))(slabs)


def _w_in_to_slabs(main, small):
    D = main.shape[0]
    tr = _pick(D, (256, 128))
    ranges = [(0, DT_COL0, 0, 0), (DT_COL0, DT_COL0 + SSD_HEADS, 1, 0), (DT_COL0 + SSD_HEADS, GATE_IN_COL0, 0, DT_COL0),
              (GATE_IN_COL0, D_IN, 1, SSD_HEADS)]

    def body(main_ref, small_ref, o_ref):
        srcs = (main_ref, small_ref)
        for d in range(N_DEV):
            lo, hi = d * SHARD_IN, (d + 1) * SHARD_IN
            pieces = []
            for start, stop, which, s0 in ranges:
                a, b = max(lo, start), min(hi, stop)
                if a < b:
                    pieces.append(srcs[which][:, s0 + a - start:s0 + b - start].astype(o_ref.dtype))
            o_ref[d] = pieces[0] if len(pieces) == 1 else jnp.concatenate(pieces, axis=1)

    return pl.pallas_call(
        body, name="w_in_grad_layout", grid=(D // tr,),
        in_specs=[pl.BlockSpec((tr, W_MAIN), lambda i: (i, 0)), pl.BlockSpec((tr, W_SMALL), lambda i: (i, 0))],
        out_specs=pl.BlockSpec((N_DEV, tr, SHARD_IN), lambda i: (0, i, 0)),
        out_shape=jax.ShapeDtypeStruct((N_DEV, D, SHARD_IN), main.dtype), compiler_params=_cp(("parallel",)))(main, small)


QB, KCB, VCB, KSB, VSB, KWB, VWB = 10, 14, 15, 16, 17, 18, 19


def _col256(a, b):
    return a[:, b * 256:(b + 1) * 256]


_EARLY = ["w_in", "cmp_w1_k", "cmp_w1_v"]
_LATE = ["w_out", "w_gate", "w_up", "w_down"]
_FFN = ["w_down", "w_gate", "w_up"]
_MID = ["w_out"]
_LAST = ["cmp_w1_k", "cmp_w1_v", "w_in"]


def _local_step(x, tgt, p, early_weights=None, late_weights=None, grads_ready=None):
    S = x.shape[0]
    cos, sin = _rope_tables(S)

    u, rs1 = _rms_fwd(x, p["attn_norm_w"], "attn_norm")
    if early_weights is not None:
        p = {**p, **early_weights((u, cos, sin))}
    proj = _mm(u, p["w_main"], "nn", F32, "in_proj")
    proj_small = _mm(u, p["w_small"], "nn", F32, "in_proj_small")
    xa = _conv_fwd(proj, p["conv_w"], p["conv_b"])
    y_ssd, y_pre, rs_ssd, hs = _ssd_fwd(proj, proj_small, xa, p["dt_bias"], p["a_log"], p["d_skip"], p["ssd_norm_w"])

    q_rot = _rope([proj], QB, ATT_WIDTH, cos, sin, 1.0, _MXU, "rope_q")
    kv = _kv_prep(proj, cos, sin, _attn_cfg(S, S, "sel")[1])
    rk, rv = _to_rows16(_col256(proj, KCB)), _to_rows16(_col256(proj, VCB))
    k_cmp, hid_k = _compress_fwd(rk, p["cmp_pe_k"], p["cmp_w1_k"], p["cmp_w2_k"])
    v_cmp, hid_v = _compress_fwd(rv, p["cmp_pe_v"], p["cmp_w1_v"], p["cmp_w2_v"])
    n_cmp = k_cmp.shape[1]

    gates = proj_small[:, SSD_HEADS:SSD_HEADS + 3 * N_HEADS].reshape(S, N_KV, GRP, 3).transpose(3, 1, 2, 0)
    o_cmp, lse_cmp, y_att = _attn_fwd(proj, QB, k_cmp, _blocked_t(v_cmp, n_cmp), "cmp", None, gates[0], None, F32, "attn_cmp_fwd")
    sel = _select(proj, QB, k_cmp, lse_cmp)
    o_sel, lse_sel, y_att = _attn_fwd(q_rot, 0, kv["ks_ext"], kv["vs_t"], "sel", sel, gates[1], y_att, F32, "attn_sel_fwd")
    o_win, lse_win, y_att = _attn_fwd(q_rot, 0, kv["kw"], kv["vw_t"], "win", None, gates[2], y_att, _MXU, "attn_win_fwd")

    if late_weights is not None:
        p = {**p, **late_weights(y_att)}
    mixed = jnp.concatenate([y_ssd, y_att], axis=1)
    h1 = _mm(mixed, p["w_out"], "nn", F32, "out_proj", res=x)
    v, rs_ffn = _rms_fwd(h1, p["ffn_norm_w"], "ffn_norm")
    gt, up, act = _ffn_up(v, p["w_gate"], p["w_up"])
    h2 = _mm(act, p["w_down"], "nn", F32, "ffn_down", res=h1)
    loss, dh2, dh2b, d_final_w = _final_loss(h2, p["final_norm_w"], tgt)

    def ready(names):
        return None if grads_ready is None else grads_ready(names, g)

    g = {"final_norm_w": d_final_w}
    g["w_down"] = _mm(act, dh2b, "tn", _MXU, "dw_down")
    dgt, dup = _ffn_dact(dh2b, p["w_down"], gt, up)
    g["w_gate"] = _mm(v, dgt, "tn", _MXU, "dw_gate")
    g["w_up"] = _mm(v, dup, "tn", _MXU, "dw_up")
    dv = _mm(dgt, p["w_gate"], "nt", F32, "dv_gate", after=ready(_FFN))
    dv = _mm(dup, p["w_up"], "nt", F32, "dv_up", res=dv)
    dh1, dh1b, g["ffn_norm_w"] = _rms_bwd(dv, h1, rs_ffn, p["ffn_norm_w"], dh2, "ffn_norm_bwd")
    g["w_out"] = _mm(mixed, dh1b, "tn", _MXU, "dw_out")
    dmixed = _mm(dh1b, p["w_out"], "nt", F32, "dmixed", after=ready(_MID))

    dz, dxa, ddtr, g["dt_bias"], g["a_log"], g["d_skip"], g["ssd_norm_w"] = _ssd_bwd(
        dmixed, proj, proj_small, xa, y_pre, rs_ssd, hs, p["dt_bias"], p["a_log"], p["d_skip"], p["ssd_norm_w"])
    dxbc, g["conv_w"], g["conv_b"] = _conv_bwd(proj, p["conv_w"], p["conv_b"], dxa)

    dyb = SSD_WIDTH // (GRP * HD)
    dq_cmp, dk_cmp, dv_cmp, dg_cmp = _attn_bwd(proj, QB, k_cmp, _blocked_t(k_cmp, n_cmp), v_cmp, o_cmp, lse_cmp, dmixed, dyb,
                                               gates[0], "cmp", None, "attn_cmp_bwd")
    dq_sel, dks, dvs, dg_sel = _attn_bwd(q_rot, 0, kv["ks_ext"], kv["ks_t"], kv["vs"], o_sel, lse_sel, dmixed, dyb, gates[1], "sel",
                                         sel, "attn_sel_bwd")
    dq_win, dkw, dvw, dg_win = _attn_bwd(q_rot, 0, kv["kw"], kv["kw_t"], kv["vw"], o_win, lse_win, dmixed, dyb, gates[2], "win", None,
                                         "attn_win_bwd")
    dgate = jnp.stack([dg_cmp, dg_sel, dg_win]).transpose(3, 1, 2, 0).reshape(S, 3 * N_HEADS)
    drk, g["cmp_w1_k"], g["cmp_w2_k"], g["cmp_pe_k"] = _compress_bwd(rk, p["cmp_pe_k"], p["cmp_w1_k"], p["cmp_w2_k"], hid_k, dk_cmp)
    drv, g["cmp_w1_v"], g["cmp_w2_v"], g["cmp_pe_v"] = _compress_bwd(rv, p["cmp_pe_v"], p["cmp_w1_v"], p["cmp_w2_v"], hid_v, dv_cmp)
    dq = _rope([dq_sel, dq_win], 0, ATT_WIDTH, cos, sin, -1.0, _MXU, "rope_dq", extra=(dq_cmp, 0))
    dkv = _dkv_post(dks, dvs, dkw, dvw, cos, sin)
    dproj = jnp.concatenate([dz, dxbc, dq] + [t.astype(_MXU) for t in (_from_rows16(drk), _from_rows16(drv))] + [dkv], axis=1)
    dsmall = jnp.concatenate([ddtr, dgate, jnp.zeros((S, W_SMALL - SSD_HEADS - 3 * N_HEADS), F32)], axis=1).astype(_MXU)
    g["w_main"] = _mm(u, dproj, "tn", _MXU, "dw_in")
    g["w_small"] = _mm(u, dsmall, "tn", F32, "dw_in_small")
    du = _mm(dproj, p["w_main"], "nt", F32, "du_main", after=ready(_LAST))
    du = _mm(dsmall, p["w_small"], "nt", F32, "du_small", res=du)
    grad_x, _, g["attn_norm_w"] = _rms_bwd(du, x, rs1, p["attn_norm_w"], dh1, "attn_norm_bwd")
    return loss, grad_x, g


MESH_ID = pl.DeviceIdType.MESH


def _my_coords():
    return lax.axis_index("x"), lax.axis_index("y"), lax.axis_index("c")


def _flat_id(px, py, pc):
    return 4 * px + 2 * py + pc


def _peer(k):
    mx, my, mc = _my_coords()
    return (1 - mx if k & 4 else mx, 1 - my if k & 2 else my, 1 - mc if k & 1 else mc)


def _exchange(arrs, scatter, name, after=()):
    n, na = len(arrs), len(after)
    scatter = [scatter] * n if isinstance(scatter, bool) else list(scatter)

    def body(*refs):
        ins, outs = refs[:n], refs[n + na:2 * n + na]
        send_sems, recv_sems, local_sems = refs[2 * n + na:]
        me = _flat_id(*_my_coords())
        copies = []
        for i in range(n):
            src_me = ins[i].at[me] if scatter[i] else ins[i]
            local = pltpu.make_async_copy(src_me, outs[i].at[me], local_sems.at[i])
            local.start()
            copies.append(local)
        for k in range(1, N_DEV):
            peer = _peer(k)
            for i in range(n):
                src = ins[i].at[_flat_id(*peer)] if scatter[i] else ins[i]
                cp = pltpu.make_async_remote_copy(src_ref=src, dst_ref=outs[i].at[me], send_sem=send_sems.at[i * 7 + k - 1],
                                                  recv_sem=recv_sems.at[i * 7 + k - 1], device_id=peer, device_id_type=MESH_ID)
                cp.start()
                copies.append(cp)
        for cp in copies:
            cp.wait()

    any_spec = pl.BlockSpec(memory_space=pl.ANY)
    out_shape = [jax.ShapeDtypeStruct(a.shape if sc else (N_DEV,) + a.shape, a.dtype) for a, sc in zip(arrs, scatter)]
    return pl.pallas_call(
        body, name=name, in_specs=[any_spec] * (n + na), out_specs=[any_spec] * n, out_shape=out_shape,
        scratch_shapes=[pltpu.SemaphoreType.DMA((n * 7,)), pltpu.SemaphoreType.DMA((n * 7,)), pltpu.SemaphoreType.DMA((n,))],
        compiler_params=pltpu.CompilerParams(has_side_effects=True))(*arrs, *after)


_HBM = pl.BlockSpec(memory_space=pltpu.HBM)
_SEM = pl.BlockSpec(memory_space=pltpu.SEMAPHORE)
_EFFECT = pltpu.SideEffectType.DATAFLOW_SIDE_EFFECTING


def _split_copies(ins, lands, send_sems, recv_sems, scatter):
    me = _flat_id(*_my_coords())
    out = []
    for k in range(1, N_DEV):
        peer = _peer(k)
        for i in range(len(ins)):
            src = ins[i].at[_flat_id(*peer)] if scatter else ins[i]
            out.append(pltpu.make_async_remote_copy(src_ref=src, dst_ref=lands[i].at[me], send_sem=send_sems.at[i * 7 + k - 1],
                                                    recv_sem=recv_sems.at[i * 7 + k - 1], device_id=peer, device_id_type=MESH_ID))
    return out


def _split_start(arrs, scatter, name):
    n = len(arrs)

    def body(*refs):
        for cp in _split_copies(refs[:n], refs[n:2 * n], refs[2 * n], refs[2 * n + 1], scatter):
            cp.start()
        refs[-1][...] = jnp.zeros_like(refs[-1])

    land_shapes = [a.shape if scatter else (N_DEV,) + a.shape for a in arrs]
    out_shape = ((pltpu.SemaphoreType.DMA((n * 7,)), pltpu.SemaphoreType.DMA((n * 7,)))
                 + tuple(pltpu.HBM(a.shape, a.dtype) for a in arrs) + tuple(pltpu.HBM(s, a.dtype) for s, a in zip(land_shapes, arrs))
                 + (jax.ShapeDtypeStruct((8, 128), F32),))
    operands = ([pltpu.with_memory_space_constraint(a, pltpu.HBM) for a in arrs]
                + [pltpu.with_memory_space_constraint(lax.empty(s, a.dtype), pltpu.HBM) for s, a in zip(land_shapes, arrs)])
    res = pl.pallas_call(
        body, name=name, out_shape=out_shape, in_specs=[_HBM] * (2 * n),
        out_specs=(_SEM, _SEM) + (_HBM,) * (2 * n) + (pl.BlockSpec(memory_space=pltpu.VMEM),),
        input_output_aliases={i: 2 + i for i in range(2 * n)},
        compiler_params=pltpu.CompilerParams(has_side_effects=_EFFECT))(*operands)
    return dict(send=res[0], recv=res[1], ins=list(res[2:2 + n]), lands=list(res[2 + n:2 + 2 * n]), token=res[-1])


def _split_wait(st, scatter, after, name):
    n = len(st["ins"])

    def body(*refs):
        for cp in _split_copies(refs[:n], refs[n:2 * n], refs[2 * n], refs[2 * n + 1], scatter):
            cp.wait_send()
            cp.wait_recv()

    arrs = st["ins"] + st["lands"]
    res = pl.pallas_call(
        body, name=name, out_shape=tuple(pltpu.HBM(a.shape, a.dtype) for a in arrs),
        in_specs=[_HBM] * (2 * n) + [_SEM, _SEM] + [pl.BlockSpec(memory_space=pl.ANY)] * len(after), out_specs=(_HBM,) * (2 * n),
        input_output_aliases={i: i for i in range(2 * n)},
        compiler_params=pltpu.CompilerParams(has_side_effects=_EFFECT))(*arrs, st["send"], st["recv"], *after)
    me = _flat_id(*_my_coords())
    out = []
    for src, land in zip(res[:n], res[n:]):
        own = lax.dynamic_index_in_dim(src, me, 0, keepdims=True) if scatter else src[None]
        out.append(lax.dynamic_update_slice_in_dim(land, own, me, 0))
    return out


def _adam_step(p_ref, w_ref, m_ref, v_ref, g_ref, d_ref, nm_ref, nv_ref):
    g = p_ref[0].astype(F32)
    for j in range(1, p_ref.shape[0]):
        g = g + p_ref[j].astype(F32)
    g_ref[...] = g
    nm = ADAM_B1 * m_ref[...] + (1.0 - ADAM_B1) * g
    nv = ADAM_B2 * v_ref[...] + (1.0 - ADAM_B2) * (g * g)
    nm_ref[...] = nm
    nv_ref[...] = nv
    m_hat = nm / (1.0 - ADAM_B1 ** ADAM_STEP)
    v_hat = nv / (1.0 - ADAM_B2 ** ADAM_STEP)
    d_ref[...] = -ADAM_LR * (m_hat / (jnp.sqrt(v_hat) + ADAM_EPS) + ADAM_WD * w_ref[...])


def _adam_sum(parts, w, m, v, name):
    P, R, C = parts.shape
    tr = _pick(R, (256, 128, 64, 32, 8)) if C <= 1024 else _pick(R, (128, 64, 32, 8))
    blk = pl.BlockSpec((tr, C), lambda i: (i, 0))
    return pl.pallas_call(
        functools.partial(_adam_step), name=name, grid=(R // tr,),
        in_specs=[pl.BlockSpec((P, tr, C), lambda i: (0, i, 0)), blk, blk, blk],
        out_specs=[blk] * 4, out_shape=[jax.ShapeDtypeStruct((R, C), F32)] * 4, compiler_params=_cp(("parallel",)))(parts, w, m, v)


def _adam_small(loss_parts, parts, ws, ms, vs):
    n = len(parts)

    def body(*refs):
        loss_ref, ins, outs, total_ref = refs[0], refs[1:4 * n + 1], refs[4 * n + 1:-1], refs[-1]
        for i in range(n):
            _adam_step(ins[i], ins[n + i], ins[2 * n + i], ins[3 * n + i], *outs[4 * i:4 * i + 4])
        total = loss_ref[0]
        for d in range(1, N_DEV):
            total = total + loss_ref[d]
        total_ref[...] = total

    out_shape = [jax.ShapeDtypeStruct(w.shape, F32) for w in ws for _ in range(4)] + [jax.ShapeDtypeStruct(loss_parts.shape[1:], F32)]
    res = pl.pallas_call(body, name="adam_small", out_shape=out_shape)(loss_parts, *parts, *ws, *ms, *vs)
    return res[-1], [tuple(res[4 * i:4 * i + 4]) for i in range(n)]


_WEIGHTS = ["attn_norm_w", "w_in", "conv_w", "conv_b", "dt_bias", "a_log", "d_skip", "ssd_norm_w", "cmp_w1_k", "cmp_w2_k",
            "cmp_w1_v", "cmp_w2_v", "cmp_pe_k", "cmp_pe_v", "w_out", "ffn_norm_w", "w_gate", "w_up", "w_down", "final_norm_w"]
_BIG = ["w_in", "w_gate", "w_up", "w_down", "w_out", "cmp_w1_k", "cmp_w1_v"]
_COL_SHARDED = ("w_in", "w_gate", "w_up")
_REPLICATED = ["attn_norm_w", "conv_b", "dt_bias", "a_log", "d_skip", "ssd_norm_w", "cmp_pe_k", "cmp_pe_v", "ffn_norm_w",
               "final_norm_w"]
_SMALL_SHARDED = ["conv_w", "cmp_w2_k", "cmp_w2_v"]


def _cols_to_slabs(g):
    R = g.shape[0]
    return g.reshape(R, N_DEV, -1).transpose(1, 0, 2)


def _slabs_to_cols(s):
    return s.transpose(1, 0, 2).reshape(s.shape[1], -1)


def kernel(x, attn_norm_w, w_in, conv_w, conv_b, dt_bias, a_log, d_skip, ssd_norm_w, cmp_w1_k, cmp_w2_k, cmp_w1_v, cmp_w2_v, cmp_pe_k, cmp_pe_v, w_out, ffn_norm_w, w_gate, w_up, w_down, final_norm_w, loss_target, m_attn_norm_w, m_w_in, m_conv_w, m_conv_b, m_dt_bias, m_a_log, m_d_skip, m_ssd_norm_w, m_cmp_w1_k, m_cmp_w2_k, m_cmp_w1_v, m_cmp_w2_v, m_cmp_pe_k, m_cmp_pe_v, m_w_out, m_ffn_norm_w, m_w_gate, m_w_up, m_w_down, m_final_norm_w, v_attn_norm_w, v_w_in, v_conv_w, v_conv_b, v_dt_bias, v_a_log, v_d_skip, v_ssd_norm_w, v_cmp_w1_k, v_cmp_w2_k, v_cmp_w1_v, v_cmp_w2_v, v_cmp_pe_k, v_cmp_pe_v, v_w_out, v_ffn_norm_w, v_w_gate, v_w_up, v_w_down, v_final_norm_w):
    a = dict(locals())

    shard = {n: a[n][0].astype(_MXU) for n in _BIG}
    early_small = [cmp_w2_k[0], cmp_w2_v[0], conv_w[0]]
    st_early = _split_start([shard[n] for n in _EARLY] + early_small, False, "gather_early_start")
    zero = st_early["token"][0, 0].astype(_MXU)
    st_late = _split_start([shard[_LATE[0]] + zero] + [shard[n] for n in _LATE[1:]], False, "gather_late_start")

    def assemble(n, t):
        return _cols_from_slabs(t) if n in _COL_SHARDED else t.reshape(-1, t.shape[-1])

    p = dict(attn_norm_w=attn_norm_w, conv_b=conv_b, dt_bias=dt_bias, a_log=a_log, d_skip=d_skip, ssd_norm_w=ssd_norm_w,
             cmp_pe_k=cmp_pe_k.reshape(1, -1), cmp_pe_v=cmp_pe_v.reshape(1, -1), ffn_norm_w=ffn_norm_w,
             final_norm_w=final_norm_w.reshape(1, -1))

    def early_weights(after):
        got = _split_wait(st_early, False, tuple(after) + (st_late["token"],), "gather_early_wait")
        w_main, w_small = _w_in_from_slabs(got[0])
        return dict(w_main=w_main, w_small=w_small, cmp_w1_k=assemble("cmp_w1_k", got[1]), cmp_w1_v=assemble("cmp_w1_v", got[2]),
                    cmp_w2_k=assemble("cmp_w2_k", got[3]).astype(_MXU), cmp_w2_v=assemble("cmp_w2_v", got[4]).astype(_MXU),
                    conv_w=_slabs_to_cols(got[5]))

    def late_weights(after):
        got_late = _split_wait(st_late, False, (after,), "gather_late_wait")
        return {n: assemble(n, t) for n, t in zip(_LATE, got_late)}

    def slabs_of(g, n):
        if n == "w_in":
            return _w_in_to_slabs(g["w_main"], g["w_small"])
        return _slabs_from_cols(g[n]) if n in _COL_SHARDED else g[n].reshape(N_DEV, -1, g[n].shape[-1])

    started = []

    def grads_ready(names, g):
        started.append((names, _split_start([slabs_of(g, n) for n in names], True, "scatter_grads_start_%d" % len(started))))
        return started[-1][1]["token"]

    loss_part, grad_x, g = _local_step(x[0], loss_target[0], p, early_weights, late_weights, grads_ready)

    out, after = {}, (started[-1][1]["token"],)
    for i, (names, st) in enumerate(started):
        if i == len(started) - 1:
            after = after + (grad_x,)
        received = _split_wait(st, True, after, "scatter_grads_wait_%d" % i)
        for n, parts in zip(names, received):
            out[n] = _adam_sum(parts, a[n][0], a["m_" + n][0], a["v_" + n][0], "adam_" + n)
        after = (out[names[-1]][0],)

    small_names = _REPLICATED + _SMALL_SHARDED
    partials = [g[n] for n in _REPLICATED] + [_cols_to_slabs(g["conv_w"])] + [
        g[n].reshape(N_DEV, -1, g[n].shape[-1]) for n in ("cmp_w2_k", "cmp_w2_v")]
    gathered = _exchange([loss_part] + partials, [False] * (1 + len(_REPLICATED)) + [True] * len(_SMALL_SHARDED),
                         "exchange_small_grads", after=(received[0],))
    shapes2d = [t.shape[1:] for t in gathered[1:]]
    loss, res_small = _adam_small(gathered[0], gathered[1:],
                                  *[[a[pre + n].reshape(s) for n, s in zip(small_names, shapes2d)] for pre in ("", "m_", "v_")])
    for n, r in zip(small_names, res_small):
        out[n] = r

    outs = [loss[0, 0], grad_x[None]]
    for j in range(4):
        for n in _WEIGHTS:
            outs.append(out[n][j].reshape(a[n].shape))
    return tuple(outs)
```

```python
import functools

import numpy as np
import jax
import jax.numpy as jnp
from jax import lax
from jax.experimental import pallas as pl
from jax.experimental.pallas import tpu as pltpu

F32 = jnp.float32
_MXU = jnp.bfloat16

N_DEV = 8
D_MODEL = 2048
SSD_WIDTH = 1024
ATT_WIDTH = 1024
SSD_HEADS = 16
SSD_P = 64
SSD_N = 128
SSD_L = 128
SSD_G = 2
CONV_CH = 1536
CONV_K = 4
HD = 64
N_HEADS = 16
N_KV = 4
GRP = 4
CMP_HID = 256
SEL_BLOCK = 64
N_SELECT = 16
WINDOW = 512
ROPE_DIM = 16
ROPE_THETA = 500000.0
D_FF = 5632
EPS = 1e-6
NEG = -1e30
FORCE = 1e4
SCALE = HD ** -0.5
D_IN = 5184
W_MAIN = 5120
W_SMALL = 128
VMEM_LIMIT = 52 * 1024 * 1024

ADAM_LR, ADAM_B1, ADAM_B2, ADAM_EPS, ADAM_WD, ADAM_STEP = 0.001, 0.9, 0.999, 1e-08, 0.01, 10


def _pick(n, cands):
    for c in cands:
        if n % c == 0:
            return c
    return n


def _cp(sem=None):
    return pltpu.CompilerParams(dimension_semantics=sem, vmem_limit_bytes=VMEM_LIMIT)


def _sigmoid(x):
    return 1.0 / (1.0 + jnp.exp(-x))


def _dot(a, b, dims, split=None):
    dn = {"nn": (((1,), (0,)), ((), ())), "nt": (((1,), (1,)), ((), ())), "tn": (((0,), (0,)), ((), ()))}[dims]
    mm = lambda x, y: lax.dot_general(x.astype(_MXU), y.astype(_MXU), dn, preferred_element_type=F32)
    if split is None:
        return mm(a, b)
    x = (a if split == "a" else b).astype(F32)
    hi = x.astype(_MXU)
    lo = x - hi.astype(F32)
    return mm(hi, b) + mm(lo, b) if split == "a" else mm(a, hi) + mm(a, lo)


LANE = 128
MM_TILE = 1024
MM_K_WHOLE = 2048
MM_K_STEP = 1536
TN_ACC_ELEMS = 3 * 2 ** 20
TN_K_STEP = 512


def _largest_tile(n, cap):
    if n <= cap:
        return n
    best = LANE
    for t in range(LANE, cap + 1, LANE):
        if n % t == 0:
            best = t
    return best


def _mm_tiles(mode, M, N, K):
    if mode == "tn":
        tm = _largest_tile(M, 2 * MM_TILE)
        return tm, _largest_tile(N, TN_ACC_ELEMS // tm), _largest_tile(K, TN_K_STEP)
    tk = K if K <= MM_K_WHOLE else _largest_tile(K, MM_K_STEP)
    return _largest_tile(M, MM_TILE), _largest_tile(N, MM_TILE), tk


def _mm(a, b, mode, out_dtype, name, res=None, after=None):
    if mode == "nn":
        (M, K), N = a.shape, b.shape[1]
    elif mode == "nt":
        (M, K), N = a.shape, b.shape[0]
    else:
        (K, M), N = a.shape, b.shape[1]
    tm, tn, tk = _mm_tiles(mode, M, N, K)
    nk = K // tk
    a_spec = pl.BlockSpec((tk, tm), lambda i, j, k: (k, i)) if mode == "tn" else pl.BlockSpec((tm, tk), lambda i, j, k: (i, k))
    b_spec = pl.BlockSpec((tn, tk), lambda i, j, k: (j, k)) if mode == "nt" else pl.BlockSpec((tk, tn), lambda i, j, k: (k, j))
    o_spec = pl.BlockSpec((tm, tn), lambda i, j, k: (i, j))

    def finish(r, r_ref, o_ref):
        if res is not None:
            r = r + r_ref[...].astype(F32)
        o_ref[...] = r.astype(out_dtype)

    def body_one_step(*refs):
        a_ref, b_ref, o_ref = refs[0], refs[1], refs[-1]
        finish(_dot(a_ref[...], b_ref[...], mode), refs[2], o_ref)

    def body(*refs):
        a_ref, b_ref, o_ref, acc = refs[0], refs[1], refs[-2], refs[-1]
        k = pl.program_id(2)

        @pl.when(k == 0)
        def _():
            acc[...] = jnp.zeros_like(acc)

        acc[...] += _dot(a_ref[...], b_ref[...], mode)

        @pl.when(k == nk - 1)
        def _():
            finish(acc[...], refs[2], o_ref)

    ins, specs = [a, b], [a_spec, b_spec]
    if res is not None:
        ins.append(res)
        specs.append(o_spec)
    if after is not None:
        ins.append(after)
        specs.append(pl.BlockSpec(memory_space=pl.ANY))
    return pl.pallas_call(
        body_one_step if nk == 1 else body, name=name, grid=(M // tm, N // tn, nk), in_specs=specs, out_specs=o_spec,
        out_shape=jax.ShapeDtypeStruct((M, N), out_dtype), scratch_shapes=[] if nk == 1 else [pltpu.VMEM((tm, tn), F32)],
        compiler_params=_cp(("parallel", "parallel", "arbitrary")))(*ins)


def _ffn_up(v, w_gate, w_up):
    S, D = v.shape
    F = w_gate.shape[1]
    tm, tn = _largest_tile(S, MM_TILE), _largest_tile(F, MM_TILE // 2)

    def body(v_ref, wg_ref, wu_ref, gt_ref, up_ref, act_ref):
        vv = v_ref[...]
        g = _dot(vv, wg_ref[...], "nn")
        u = _dot(vv, wu_ref[...], "nn")
        gt_ref[...] = g
        up_ref[...] = u
        act_ref[...] = (g * _sigmoid(g) * u).astype(act_ref.dtype)

    o_spec = pl.BlockSpec((tm, tn), lambda i, j: (i, j))
    w_spec = pl.BlockSpec((D, tn), lambda i, j: (0, j))
    return pl.pallas_call(
        body, name="ffn_up", grid=(S // tm, F // tn),
        in_specs=[pl.BlockSpec((tm, D), lambda i, j: (i, 0)), w_spec, w_spec], out_specs=[o_spec, o_spec, o_spec],
        out_shape=[jax.ShapeDtypeStruct((S, F), F32), jax.ShapeDtypeStruct((S, F), F32), jax.ShapeDtypeStruct((S, F), _MXU)],
        compiler_params=_cp(("parallel", "parallel")))(v, w_gate, w_up)


def _ffn_dv(dgt, dup, w_gate, w_up, after):
    S, F = dgt.shape
    D = w_gate.shape[0]
    tm, tn, tk = _mm_tiles("nt", S, D, F)
    nk = F // tk

    def body(g_ref, u_ref, wg_ref, wu_ref, *rest):
        o_ref, acc = rest[-2], rest[-1]
        k = pl.program_id(2)

        @pl.when(k == 0)
        def _():
            acc[...] = jnp.zeros_like(acc)

        acc[...] += _dot(g_ref[...], wg_ref[...], "nt") + _dot(u_ref[...], wu_ref[...], "nt")

        @pl.when(k == nk - 1)
        def _():
            o_ref[...] = acc[...]

    a_spec = pl.BlockSpec((tm, tk), lambda i, j, k: (i, k))
    w_spec = pl.BlockSpec((tn, tk), lambda i, j, k: (j, k))
    ins, specs = [dgt, dup, w_gate, w_up], [a_spec, a_spec, w_spec, w_spec]
    if after is not None:
        ins.append(after)
        specs.append(pl.BlockSpec(memory_space=pl.ANY))
    return pl.pallas_call(
        body, name="ffn_dv", grid=(S // tm, D // tn, nk), in_specs=specs, out_specs=pl.BlockSpec((tm, tn), lambda i, j, k: (i, j)),
        out_shape=jax.ShapeDtypeStruct((S, D), F32), scratch_shapes=[pltpu.VMEM((tm, tn), F32)],
        compiler_params=_cp(("parallel", "parallel", "arbitrary")))(*ins)


def _ffn_dact(dh2, w_down, gt, up):
    S, D = dh2.shape
    F = w_down.shape[0]
    tm, tn = _largest_tile(S, MM_TILE), _largest_tile(F, MM_TILE // 2)

    def body(d_ref, w_ref, gt_ref, up_ref, dg_ref, du_ref):
        da, g, u = _dot(d_ref[...], w_ref[...], "nt"), gt_ref[...], up_ref[...]
        s = _sigmoid(g)
        dg_ref[...] = (da * u * (s * (1.0 + g * (1.0 - s)))).astype(dg_ref.dtype)
        du_ref[...] = (da * (g * s)).astype(du_ref.dtype)

    o_spec = pl.BlockSpec((tm, tn), lambda i, j: (i, j))
    return pl.pallas_call(
        body, name="ffn_dact", grid=(S // tm, F // tn),
        in_specs=[pl.BlockSpec((tm, D), lambda i, j: (i, 0)), pl.BlockSpec((tn, D), lambda i, j: (j, 0)), o_spec, o_spec],
        out_specs=[o_spec, o_spec],
        out_shape=[jax.ShapeDtypeStruct((S, F), _MXU), jax.ShapeDtypeStruct((S, F), _MXU)],
        compiler_params=_cp(("parallel", "parallel")))(dh2, w_down, gt, up)


def _rms_fwd(x, w, name):
    S, D = x.shape
    tr = _pick(S, (256, 128))

    def body(x_ref, w_ref, xn_ref, rs_ref):
        xv = x_ref[...]
        rs = lax.rsqrt(jnp.mean(xv * xv, axis=-1, keepdims=True) + EPS)
        xn_ref[...] = ((xv * rs) * w_ref[...]).astype(xn_ref.dtype)
        rs_ref[...] = rs

    return pl.pallas_call(
        body, name=name, grid=(S // tr,),
        in_specs=[pl.BlockSpec((tr, D), lambda i: (i, 0)), pl.BlockSpec((1, D), lambda i: (0, 0))],
        out_specs=[pl.BlockSpec((tr, D), lambda i: (i, 0)), pl.BlockSpec((tr, 1), lambda i: (i, 0))],
        out_shape=[jax.ShapeDtypeStruct((S, D), _MXU), jax.ShapeDtypeStruct((S, 1), F32)],
        compiler_params=_cp(("parallel",)))(x, w)


def _rms_bwd(dyn, x, rs, w, res, name):
    S, D = x.shape
    tr = _pick(S, (256, 128))

    def body(dy_ref, x_ref, rs_ref, w_ref, res_ref, dx_ref, dxb_ref, dw_ref):
        @pl.when(pl.program_id(0) == 0)
        def _():
            dw_ref[...] = jnp.zeros_like(dw_ref)

        dy, r = dy_ref[...].astype(F32), rs_ref[...]
        xhat = x_ref[...] * r
        dw_ref[...] += jnp.sum(dy * xhat, axis=0, keepdims=True)
        dxhat = dy * w_ref[...]
        dx = res_ref[...] + r * (dxhat - xhat * jnp.mean(dxhat * xhat, axis=-1, keepdims=True))
        dx_ref[...] = dx
        dxb_ref[...] = dx.astype(dxb_ref.dtype)

    row = pl.BlockSpec((tr, D), lambda i: (i, 0))
    vec = pl.BlockSpec((1, D), lambda i: (0, 0))
    return pl.pallas_call(
        body, name=name, grid=(S // tr,),
        in_specs=[row, row, pl.BlockSpec((tr, 1), lambda i: (i, 0)), vec, row], out_specs=[row, row, vec],
        out_shape=[jax.ShapeDtypeStruct((S, D), F32), jax.ShapeDtypeStruct((S, D), _MXU), jax.ShapeDtypeStruct((1, D), F32)],
        compiler_params=_cp(("arbitrary",)))(dyn, x, rs, w, res)


def _final_loss(h2, w, tgt):
    S, D = h2.shape
    tr = _pick(S, (256, 128))

    def body(h_ref, w_ref, t_ref, loss_ref, dh_ref, dhb_ref, dw_ref):
        @pl.when(pl.program_id(0) == 0)
        def _():
            dw_ref[...] = jnp.zeros_like(dw_ref)
            loss_ref[...] = jnp.zeros_like(loss_ref)

        hv, wv = h_ref[...], w_ref[...]
        rs = lax.rsqrt(jnp.mean(hv * hv, axis=-1, keepdims=True) + EPS)
        xhat = hv * rs
        err = xhat * wv - t_ref[...]
        row = jnp.mean(err * err, axis=-1, keepdims=True)
        loss_ref[...] += jnp.broadcast_to(0.5 * jnp.sum(row, axis=0, keepdims=True), loss_ref.shape)
        dy = err * (1.0 / D)
        dw_ref[...] += jnp.sum(dy * xhat, axis=0, keepdims=True)
        dxhat = dy * wv
        dh = rs * (dxhat - xhat * jnp.mean(dxhat * xhat, axis=-1, keepdims=True))
        dh_ref[...] = dh
        dhb_ref[...] = dh.astype(dhb_ref.dtype)

    row = pl.BlockSpec((tr, D), lambda i: (i, 0))
    vec = pl.BlockSpec((1, D), lambda i: (0, 0))
    return pl.pallas_call(
        body, name="final_loss", grid=(S // tr,), in_specs=[row, vec, row],
        out_specs=[pl.BlockSpec((1, LANE), lambda i: (0, 0)), row, row, vec],
        out_shape=[jax.ShapeDtypeStruct((1, LANE), F32), jax.ShapeDtypeStruct((S, D), F32), jax.ShapeDtypeStruct((S, D), _MXU),
                   jax.ShapeDtypeStruct((1, D), F32)],
        compiler_params=_cp(("arbitrary",)))(h2, w, tgt)


def _shift_rows(x, k, rows):
    if k == 0:
        return x
    S = x.shape[0]
    r = pltpu.roll(x, k % S, axis=0)
    ok = (rows >= k) if k > 0 else (rows < S + k)
    return jnp.where(ok, r, 0.0)


XBC_COL0 = SSD_WIDTH // 128


def _conv_fwd(proj, conv_w, conv_b):
    S = proj.shape[0]
    nct = CONV_CH // 128

    def body(x_ref, w_ref, b_ref, o_ref):
        x = x_ref[...]
        rows = lax.broadcasted_iota(jnp.int32, x.shape, 0)
        c = b_ref[...] + w_ref[3:4, :] * x
        for k in range(1, CONV_K):
            c = c + w_ref[3 - k:4 - k, :] * _shift_rows(x, k, rows)
        o_ref[...] = c * _sigmoid(c)

    return pl.pallas_call(
        body, name="conv_fwd", grid=(nct,),
        in_specs=[pl.BlockSpec((S, 128), lambda j: (0, XBC_COL0 + j)), pl.BlockSpec((CONV_K, 128), lambda j: (0, j)),
                  pl.BlockSpec((1, 128), lambda j: (0, j))],
        out_specs=pl.BlockSpec((S, 128), lambda j: (0, j)),
        out_shape=jax.ShapeDtypeStruct((S, CONV_CH), F32), compiler_params=_cp(("parallel",)))(proj, conv_w, conv_b)


def _conv_bwd(proj, conv_w, conv_b, dxa):
    S = proj.shape[0]
    nct = CONV_CH // 128

    def body(x_ref, w_ref, b_ref, d_ref, dx_ref, dw_ref, db_ref):
        x = x_ref[...]
        rows = lax.broadcasted_iota(jnp.int32, x.shape, 0)
        xs = [_shift_rows(x, k, rows) for k in range(CONV_K)]
        c = b_ref[...] + w_ref[3:4, :] * x
        for k in range(1, CONV_K):
            c = c + w_ref[3 - k:4 - k, :] * xs[k]
        s = _sigmoid(c)
        dc = d_ref[...] * (s * (1.0 + c * (1.0 - s)))
        dx = w_ref[3:4, :] * dc
        for k in range(1, CONV_K):
            dx = dx + w_ref[3 - k:4 - k, :] * _shift_rows(dc, -k, rows)
        dx_ref[...] = dx.astype(dx_ref.dtype)
        for k in range(CONV_K):
            dw_ref[3 - k:4 - k, :] = jnp.sum(dc * xs[k], axis=0, keepdims=True)
        db_ref[...] = jnp.sum(dc, axis=0, keepdims=True)

    col = pl.BlockSpec((S, 128), lambda j: (0, j))
    return pl.pallas_call(
        body, name="conv_bwd", grid=(nct,),
        in_specs=[pl.BlockSpec((S, 128), lambda j: (0, XBC_COL0 + j)), pl.BlockSpec((CONV_K, 128), lambda j: (0, j)),
                  pl.BlockSpec((1, 128), lambda j: (0, j)), col],
        out_specs=[col, pl.BlockSpec((CONV_K, 128), lambda j: (0, j)), pl.BlockSpec((1, 128), lambda j: (0, j))],
        out_shape=[jax.ShapeDtypeStruct((S, CONV_CH), _MXU), jax.ShapeDtypeStruct((CONV_K, CONV_CH), F32),
                   jax.ShapeDtypeStruct((1, CONV_CH), F32)],
        compiler_params=_cp(("parallel",)))(proj, conv_w, conv_b, dxa)


def _ssd_consts():
    L = SSD_L
    r = lax.broadcasted_iota(jnp.int32, (L, L), 0)
    c = lax.broadcasted_iota(jnp.int32, (L, L), 1)
    causal = r >= c
    upper = (r <= c).astype(F32)
    hr = lax.broadcasted_iota(jnp.int32, (SSD_HEADS, SSD_WIDTH), 0)
    hc = lax.broadcasted_iota(jnp.int32, (SSD_HEADS, SSD_WIDTH), 1)
    expand = (lax.shift_right_logical(hc, 6) == hr).astype(F32)
    return causal, causal.astype(F32), upper, expand


def _softplus(x):
    return jnp.maximum(x, 0.0) + jnp.log(1.0 + jnp.exp(-jnp.abs(x)))


def _ssd_scalars(dtr, dt_bias, a_log, tri, upper, expand):
    dt = _softplus(dtr + dt_bias)
    A = -jnp.exp(a_log)
    adt = dt * A
    acum = _dot(tri, adt, "nn", split="b")
    acum_t = _dot(adt, upper, "tn", split="a")
    alast = acum[SSD_L - 1:SSD_L, :]
    e = jnp.exp(acum)
    wdec = jnp.exp(alast - acum)
    gam = jnp.exp(alast)
    ex = lambda t: _dot(t, expand, "nn", split="a")
    gam8 = jnp.broadcast_to(gam, (8, SSD_HEADS))
    return dt, A, acum, acum_t, e, wdec, gam, ex(dt), ex(e), ex(wdec), ex(gam8)[0:1, :]


def _ssd_fwd(proj, proj_small, xa, dt_bias, a_log, d_skip, norm_w):
    S = proj.shape[0]
    L, N, W = SSD_L, SSD_N, SSD_WIDTH
    nc = S // L

    def body(z_ref, xa_ref, dtr_ref, dtb_ref, al_ref, dsk_ref, nw_ref, yo_ref, y_ref, rs_ref, hs_ref, h_scr, y_scr):
        @pl.when(pl.program_id(0) == 0)
        def _():
            h_scr[...] = jnp.zeros_like(h_scr)

        causal, tri, upper, expand = _ssd_consts()
        dt, A, acum, acum_t, e, wdec, gam, dtE, eE, wE, gamE = _ssd_scalars(dtr_ref[:, 0:SSD_HEADS], dtb_ref[...], al_ref[...], tri, upper, expand)
        xs = xa_ref[:, 0:W]
        X = xs * dtE
        XW = X * wE
        hs_ref[0] = h_scr[...]
        for g in range(SSD_G):
            gs = slice(g * 512, (g + 1) * 512)
            Bg = xa_ref[:, W + g * N:W + (g + 1) * N]
            Cg = xa_ref[:, W + SSD_G * N + g * N:W + SSD_G * N + (g + 1) * N]
            Hg = h_scr[:, gs]
            CB = _dot(Cg, Bg, "nt")
            yoff = _dot(Cg, Hg, "nn") * eE[:, gs]
            st = _dot(Bg, XW[:, gs], "tn")
            for j in range(8):
                h = g * 8 + j
                hsl = slice(h * SSD_P, (h + 1) * SSD_P)
                lam = jnp.exp(jnp.where(causal, acum[:, h:h + 1] - acum_t[h:h + 1, :], -jnp.inf))
                y_scr[:, hsl] = _dot(CB * lam, X[:, hsl], "nn") + yoff[:, j * SSD_P:(j + 1) * SSD_P]
            h_scr[:, gs] = gamE[:, gs] * Hg + st
        dskE = _dot(jnp.broadcast_to(dsk_ref[...], (8, SSD_HEADS)), expand, "nn", split="a")[0:1, :]
        y = y_scr[...] + dskE * xs
        y_ref[...] = y
        zv = z_ref[...]
        yg = y * (zv * _sigmoid(zv))
        rs = lax.rsqrt(jnp.mean(yg * yg, axis=-1, keepdims=True) + EPS)
        rs_ref[...] = rs
        yo_ref[...] = ((yg * rs) * nw_ref[...]).astype(yo_ref.dtype)

    p16 = pl.BlockSpec((1, SSD_HEADS), lambda c: (0, 0))
    return pl.pallas_call(
        body, name="ssd_fwd", grid=(nc,),
        in_specs=[pl.BlockSpec((L, W), lambda c: (c, 0)), pl.BlockSpec((L, CONV_CH), lambda c: (c, 0)),
                  pl.BlockSpec((L, W_SMALL), lambda c: (c, 0)), p16, p16, p16, pl.BlockSpec((1, W), lambda c: (0, 0))],
        out_specs=[pl.BlockSpec((L, W), lambda c: (c, 0)), pl.BlockSpec((L, W), lambda c: (c, 0)),
                   pl.BlockSpec((L, 1), lambda c: (c, 0)), pl.BlockSpec((1, N, W), lambda c: (c, 0, 0))],
        out_shape=[jax.ShapeDtypeStruct((S, W), _MXU), jax.ShapeDtypeStruct((S, W), F32), jax.ShapeDtypeStruct((S, 1), F32),
                   jax.ShapeDtypeStruct((nc, N, W), F32)],
        scratch_shapes=[pltpu.VMEM((N, W), F32), pltpu.VMEM((L, W), F32)],
        compiler_params=_cp(("arbitrary",)))(proj, xa, proj_small, dt_bias, a_log, d_skip, norm_w)


def _ssd_bwd(dmixed, proj, proj_small, xa, y, rs2, hs, dt_bias, a_log, d_skip, norm_w):
    S = proj.shape[0]
    L, N, W, H = SSD_L, SSD_N, SSD_WIDTH, SSD_HEADS
    nc = S // L

    def body(dyo_ref, z_ref, xa_ref, dtr_ref, y_ref, rs_ref, hs_ref, dtb_ref, al_ref, dsk_ref, nw_ref,
             dz_ref, dxa_ref, ddtr_ref, ddtb_ref, dal_ref, ddsk_ref, dnw_ref, dh_scr, dx_scr):
        @pl.when(pl.program_id(0) == 0)
        def _():
            dh_scr[...] = jnp.zeros_like(dh_scr)
            ddtb_ref[...] = jnp.zeros_like(ddtb_ref)
            dal_ref[...] = jnp.zeros_like(dal_ref)
            ddsk_ref[...] = jnp.zeros_like(ddsk_ref)
            dnw_ref[...] = jnp.zeros_like(dnw_ref)

        causal, tri, upper, expand = _ssd_consts()
        heads = lambda t: _dot(t, expand, "nt", split="a")
        onehot = lambda h: (lax.broadcasted_iota(jnp.int32, (1, H), 1) == h).astype(F32)

        zv, yv, rs = z_ref[...], y_ref[...], rs_ref[...]
        sz = _sigmoid(zv)
        zs = zv * sz
        xhat = (yv * zs) * rs
        dyo = dyo_ref[...].astype(F32)
        dnw_ref[...] += jnp.sum(dyo * xhat, axis=0, keepdims=True)
        dxhat = dyo * nw_ref[...]
        dyg = rs * (dxhat - xhat * jnp.mean(dxhat * xhat, axis=-1, keepdims=True))
        dz_ref[...] = (dyg * yv * (sz * (1.0 + zv * (1.0 - sz)))).astype(dz_ref.dtype)
        dy = dyg * zs

        dtr = dtr_ref[:, 0:H]
        dt, A, acum, acum_t, e, wdec, gam, dtE, eE, wE, gamE = _ssd_scalars(dtr, dtb_ref[...], al_ref[...], tri, upper, expand)
        xs = xa_ref[:, 0:W]
        X = xs * dtE
        XW = X * wE
        dskE = _dot(jnp.broadcast_to(dsk_ref[...], (8, H)), expand, "nn", split="a")[0:1, :]
        ddsk_ref[...] += heads(jnp.broadcast_to(jnp.sum(dy * xs, axis=0, keepdims=True), (8, W)))[0:1, :]

        dYe = dy * eE
        dacum = jnp.zeros((L, H), F32)
        de_full = []
        dw_full = []
        dgam_full = []
        for g in range(SSD_G):
            gs = slice(g * 512, (g + 1) * 512)
            Bg = xa_ref[:, W + g * N:W + (g + 1) * N]
            Cg = xa_ref[:, W + SSD_G * N + g * N:W + SSD_G * N + (g + 1) * N]
            Hg = hs_ref[0, :, gs]
            dHn = dh_scr[:, gs]
            CH = _dot(Cg, Hg, "nn")
            de_full.append(dy[:, gs] * CH)
            dC = _dot(dYe[:, gs], Hg, "nt")
            dHs = gamE[:, gs] * dHn + _dot(Cg, dYe[:, gs], "tn")
            dgam_full.append(jnp.sum(dHn * Hg, axis=0, keepdims=True))
            BdS = _dot(Bg, dHn, "nn")
            dB = _dot(XW[:, gs], dHn, "nt")
            dx_scr[:, gs] = BdS * wE[:, gs]
            dw_full.append(BdS * X[:, gs])
            CB = _dot(Cg, Bg, "nt")
            dCB = jnp.zeros((L, L), F32)
            for j in range(8):
                h = g * 8 + j
                hsl = slice(h * SSD_P, (h + 1) * SSD_P)
                lam = jnp.exp(jnp.where(causal, acum[:, h:h + 1] - acum_t[h:h + 1, :], -jnp.inf))
                M = CB * lam
                dM = _dot(dy[:, hsl], X[:, hsl], "nt")
                dx_scr[:, hsl] += _dot(M, dy[:, hsl], "tn")
                dCB = dCB + dM * lam
                Q = dM * M
                rowsum = jnp.sum(Q, axis=1, keepdims=True)
                colsum = _dot(Q, jnp.ones((L, 8), F32), "tn", split="a")[:, 0:1]
                dacum = dacum + (rowsum - colsum) * onehot(h)
            dC = dC + _dot(dCB, Bg, "nn")
            dB = dB + _dot(dCB, Cg, "tn")
            dxa_ref[:, W + g * N:W + (g + 1) * N] = dB
            dxa_ref[:, W + SSD_G * N + g * N:W + SSD_G * N + (g + 1) * N] = dC
            dh_scr[:, gs] = dHs

        de16 = heads(jnp.concatenate(de_full, axis=1))
        dw16 = heads(jnp.concatenate(dw_full, axis=1))
        dgam16 = heads(jnp.broadcast_to(jnp.concatenate(dgam_full, axis=1), (8, W)))[0:1, :]
        dacum = dacum + de16 * e - dw16 * wdec
        dlast = jnp.sum(dw16 * wdec, axis=0, keepdims=True) + dgam16 * gam
        lastrow = (lax.broadcasted_iota(jnp.int32, (L, 1), 0) == L - 1).astype(F32)
        dacum = dacum + lastrow * dlast
        da = _dot(tri, dacum, "tn", split="b")
        dX = dx_scr[...]
        ddt = da * A + heads(dX * xs)
        dA = jnp.sum(da * dt, axis=0, keepdims=True)
        dal_ref[...] += dA * A
        ddtr = ddt * _sigmoid(dtr + dtb_ref[...])
        ddtb_ref[...] += jnp.sum(ddtr, axis=0, keepdims=True)
        ddtr_ref[...] = ddtr
        dxa_ref[:, 0:W] = dX * dtE + dy * dskE

    p16 = pl.BlockSpec((1, H), lambda c: (0, 0))
    rev = lambda c: (nc - 1 - c, 0)
    return pl.pallas_call(
        body, name="ssd_bwd", grid=(nc,),
        in_specs=[pl.BlockSpec((L, W), rev), pl.BlockSpec((L, W), rev), pl.BlockSpec((L, CONV_CH), rev),
                  pl.BlockSpec((L, W_SMALL), rev), pl.BlockSpec((L, W), rev), pl.BlockSpec((L, 1), rev),
                  pl.BlockSpec((1, N, W), lambda c: (nc - 1 - c, 0, 0)), p16, p16, p16, pl.BlockSpec((1, W), lambda c: (0, 0))],
        out_specs=[pl.BlockSpec((L, W), rev), pl.BlockSpec((L, CONV_CH), rev), pl.BlockSpec((L, H), rev),
                   p16, p16, p16, pl.BlockSpec((1, W), lambda c: (0, 0))],
        out_shape=[jax.ShapeDtypeStruct((S, W), _MXU), jax.ShapeDtypeStruct((S, CONV_CH), F32), jax.ShapeDtypeStruct((S, H), F32),
                   jax.ShapeDtypeStruct((1, H), F32), jax.ShapeDtypeStruct((1, H), F32), jax.ShapeDtypeStruct((1, H), F32),
                   jax.ShapeDtypeStruct((1, W), F32)],
        scratch_shapes=[pltpu.VMEM((N, W), F32), pltpu.VMEM((L, W), F32)],
        compiler_params=_cp(("arbitrary",)))(dmixed, proj, xa, proj_small, y, rs2, hs, dt_bias, a_log, d_skip, norm_w)


def _rope_tables(S):
    inv = 1.0 / (ROPE_THETA ** (jnp.arange(0, ROPE_DIM, 2, dtype=F32) / ROPE_DIM))
    ang = jnp.arange(S, dtype=F32)[:, None] * inv[None, :]
    cos, sin = jnp.cos(ang), jnp.sin(ang)
    half = ROPE_DIM // 2
    c64 = jnp.concatenate([cos, cos, jnp.ones((S, HD - ROPE_DIM), F32)], axis=1)
    s64 = jnp.concatenate([sin, sin, jnp.zeros((S, HD - ROPE_DIM), F32)], axis=1)
    del half
    return jnp.concatenate([c64, c64], axis=1), jnp.concatenate([s64, s64], axis=1)


def _rope(xs, blk0, width, cos, sin, sign, out_dtype, name, extra=None):
    S = xs[0].shape[0]
    tr = _pick(S, (512, 256, 128))
    nx = len(xs)

    def body(*refs):
        x_refs, c_ref, s_ref = refs[:nx], refs[nx], refs[nx + 1]
        e_ref = refs[nx + 2] if extra is not None else None
        o_ref = refs[-1]
        cv, sv = c_ref[...], s_ref[...] * sign
        lane = lax.broadcasted_iota(jnp.int32, (tr, 128), 1)
        first = (lane & (HD - 1)) < (ROPE_DIM // 2)
        for j in range(2):
            cs = slice(j * 128, (j + 1) * 128)
            xv = x_refs[0][:, cs].astype(F32)
            for r in x_refs[1:]:
                xv = xv + r[:, cs].astype(F32)
            rot = jnp.where(first, -pltpu.roll(xv, 128 - ROPE_DIM // 2, axis=1), pltpu.roll(xv, ROPE_DIM // 2, axis=1))
            out = xv * cv + rot * sv
            if extra is not None:
                out = out + e_ref[:, cs].astype(F32)
            o_ref[:, cs] = out.astype(out_dtype)

    t128 = pl.BlockSpec((tr, 128), lambda i, j: (i, 0))
    oblk = pl.BlockSpec((tr, 256), lambda i, j: (i, j))
    specs = [pl.BlockSpec((tr, 256), lambda i, j: (i, blk0 + j))] * nx + [t128, t128]
    ins = list(xs) + [cos, sin]
    if extra is not None:
        ins.append(extra[0])
        eb = extra[1]
        specs.append(pl.BlockSpec((tr, 256), lambda i, j: (i, eb + j)))
    return pl.pallas_call(
        body, name=name, grid=(S // tr, width // 256), in_specs=specs, out_specs=oblk,
        out_shape=jax.ShapeDtypeStruct((S, width), out_dtype), compiler_params=_cp(("parallel", "parallel")))(*ins)


def _rotate128(xv, cv, sv, first):
    rot = jnp.where(first, -pltpu.roll(xv, 128 - ROPE_DIM // 2, axis=1), pltpu.roll(xv, ROPE_DIM // 2, axis=1))
    return xv * cv + rot * sv


def _kv_prep(proj, cos, sin, tk):
    S = proj.shape[0]
    NB = S // SEL_BLOCK

    def body(ks_ref, vs_ref, kw_ref, vw_ref, c_ref, s_ref, *outs):
        cv, sv = c_ref[...], s_ref[...]
        lane = lax.broadcasted_iota(jnp.int32, (tk, 128), 1)
        first = (lane & (HD - 1)) < (ROPE_DIM // 2)
        key = pl.program_id(0) * tk + lax.broadcasted_iota(jnp.int32, (tk, NB), 0)
        onehot = (lax.shift_right_logical(key, 6) == lax.broadcasted_iota(jnp.int32, (tk, NB), 1)).astype(F32)
        for j, (ref, rotated) in enumerate(((ks_ref, True), (vs_ref, False), (kw_ref, True), (vw_ref, False))):
            nat, blk = outs[2 * j], outs[2 * j + 1]
            for half in range(2):
                xv = ref[:, half * 128:(half + 1) * 128]
                if rotated:
                    xv = _rotate128(xv, cv, sv, first)
                for e in range(2):
                    h = 2 * half + e
                    piece = xv[:, e * HD:(e + 1) * HD]
                    nat[h] = (jnp.concatenate([piece, onehot], axis=1) if j == 0 else piece).astype(nat.dtype)
                    blk[h, 0] = piece.T.astype(blk.dtype)

    col = lambda b: pl.BlockSpec((tk, 256), lambda i: (i, b))
    t128 = pl.BlockSpec((tk, 128), lambda i: (i, 0))
    nat_spec = lambda w: pl.BlockSpec((N_KV, tk, w), lambda i: (0, i, 0))
    blk_spec = pl.BlockSpec((N_KV, 1, HD, tk), lambda i: (0, i, 0, 0))
    nat_shape = lambda w: jax.ShapeDtypeStruct((N_KV, S, w), _MXU)
    blk_shape = jax.ShapeDtypeStruct((N_KV, S // tk, HD, tk), _MXU)
    widths = (HD + NB, HD, HD, HD)
    res = pl.pallas_call(
        body, name="kv_prep", grid=(S // tk,), in_specs=[col(KSB), col(VSB), col(KWB), col(VWB), t128, t128],
        out_specs=[s for w in widths for s in (nat_spec(w), blk_spec)],
        out_shape=[s for w in widths for s in (nat_shape(w), blk_shape)],
        compiler_params=_cp(("parallel",)))(proj, proj, proj, proj, cos, sin)
    return dict(ks_ext=res[0], ks_t=res[1], vs=res[2], vs_t=res[3], kw=res[4], kw_t=res[5], vw=res[6], vw_t=res[7])


def _dkv_post(dks, dvs, dkw, dvw, cos, sin):
    S = dks.shape[1]
    tr = _pick(S, (512, 256, 128))

    def body(dks_ref, dvs_ref, dkw_ref, dvw_ref, c_ref, s_ref, o_ref):
        cv, sv = c_ref[...], -s_ref[...]
        lane = lax.broadcasted_iota(jnp.int32, (tr, 128), 1)
        first = (lane & (HD - 1)) < (ROPE_DIM // 2)
        for j, (ref, rotated) in enumerate(((dks_ref, True), (dvs_ref, False), (dkw_ref, True), (dvw_ref, False))):
            for half in range(2):
                xv = jnp.concatenate([ref[2 * half], ref[2 * half + 1]], axis=1)
                if rotated:
                    xv = _rotate128(xv, cv, sv, first)
                o_ref[:, j * 256 + half * 128:j * 256 + (half + 1) * 128] = xv.astype(o_ref.dtype)

    hm = pl.BlockSpec((N_KV, tr, HD), lambda i: (0, i, 0))
    t128 = pl.BlockSpec((tr, 128), lambda i: (i, 0))
    return pl.pallas_call(
        body, name="dkv_post", grid=(S // tr,), in_specs=[hm, hm, hm, hm, t128, t128],
        out_specs=pl.BlockSpec((tr, 4 * 256), lambda i: (i, 0)), out_shape=jax.ShapeDtypeStruct((S, 4 * 256), _MXU),
        compiler_params=_cp(("parallel",)))(dks, dvs, dkw, dvw, cos, sin)


def _compress_fwd(R, pe, w1, w2):
    NC = R.shape[1]
    half = 16 * HD

    def body(r_ref, pe_ref, w1_ref, w2_ref, o_ref, hid_ref):
        r = r_ref[0]
        a = _dot(r + pe_ref[:, 0:half], w1_ref[0:half, :], "nn")
        b = _dot(r + pe_ref[:, half:2 * half], w1_ref[half:2 * half, :], "nn")
        hid = a + pltpu.roll(b, NC - 1, axis=0)
        hid_ref[0] = hid
        out = _dot(hid * _sigmoid(hid), w2_ref[...], "nn")
        rows = lax.broadcasted_iota(jnp.int32, out.shape, 0)
        o_ref[0] = jnp.where(rows < NC - 1, out, 0.0).astype(o_ref.dtype)

    return pl.pallas_call(
        body, name="compress_fwd", grid=(N_KV,),
        in_specs=[pl.BlockSpec((1, NC, half), lambda h: (h, 0, 0)), pl.BlockSpec((1, 2 * half), lambda h: (0, 0)),
                  pl.BlockSpec((2 * half, CMP_HID), lambda h: (0, 0)), pl.BlockSpec((CMP_HID, HD), lambda h: (0, 0))],
        out_specs=[pl.BlockSpec((1, NC, HD), lambda h: (h, 0, 0)), pl.BlockSpec((1, NC, CMP_HID), lambda h: (h, 0, 0))],
        out_shape=[jax.ShapeDtypeStruct((N_KV, NC, HD), _MXU), jax.ShapeDtypeStruct((N_KV, NC, CMP_HID), F32)],
        compiler_params=_cp(("parallel",)))(R, pe, w1, w2)


def _compress_bwd(R, pe, w1, w2, hid, dout):
    NC = R.shape[1]
    half = 16 * HD

    def body(r_ref, pe_ref, w1_ref, w2_ref, hid_ref, do_ref, dr_ref, dw1_ref, dw2_ref, dpe_ref):
        @pl.when(pl.program_id(0) == 0)
        def _():
            dw1_ref[...] = jnp.zeros_like(dw1_ref)
            dw2_ref[...] = jnp.zeros_like(dw2_ref)
            dpe_ref[...] = jnp.zeros_like(dpe_ref)

        r, hv, do = r_ref[0], hid_ref[0], do_ref[0]
        s = _sigmoid(hv)
        dw2_ref[...] += _dot(hv * s, do, "tn")
        dhid = _dot(do, w2_ref[...], "nt") * (s * (1.0 + hv * (1.0 - s)))
        rows = lax.broadcasted_iota(jnp.int32, dhid.shape, 0)
        dhid = jnp.where(rows < NC - 1, dhid, 0.0)
        dhid_dn = pltpu.roll(dhid, 1, axis=0)
        dw1_ref[0:half, :] += _dot(r + pe_ref[:, 0:half], dhid, "tn")
        dw1_ref[half:2 * half, :] += _dot(r + pe_ref[:, half:2 * half], dhid_dn, "tn")
        dxt = _dot(dhid, w1_ref[0:half, :], "nt")
        dxb = _dot(dhid_dn, w1_ref[half:2 * half, :], "nt")
        dr_ref[0] = dxt + dxb
        dpe_ref[:, 0:half] += jnp.sum(dxt, axis=0, keepdims=True)
        dpe_ref[:, half:2 * half] += jnp.sum(dxb, axis=0, keepdims=True)

    return pl.pallas_call(
        body, name="compress_bwd", grid=(N_KV,),
        in_specs=[pl.BlockSpec((1, NC, half), lambda h: (h, 0, 0)), pl.BlockSpec((1, 2 * half), lambda h: (0, 0)),
                  pl.BlockSpec((2 * half, CMP_HID), lambda h: (0, 0)), pl.BlockSpec((CMP_HID, HD), lambda h: (0, 0)),
                  pl.BlockSpec((1, NC, CMP_HID), lambda h: (h, 0, 0)), pl.BlockSpec((1, NC, HD), lambda h: (h, 0, 0))],
        out_specs=[pl.BlockSpec((1, NC, half), lambda h: (h, 0, 0)), pl.BlockSpec((2 * half, CMP_HID), lambda h: (0, 0)),
                   pl.BlockSpec((CMP_HID, HD), lambda h: (0, 0)), pl.BlockSpec((1, 2 * half), lambda h: (0, 0))],
        out_shape=[jax.ShapeDtypeStruct((N_KV, NC, half), F32), jax.ShapeDtypeStruct((2 * half, CMP_HID), F32),
                   jax.ShapeDtypeStruct((CMP_HID, HD), F32), jax.ShapeDtypeStruct((1, 2 * half), F32)],
        compiler_params=_cp(("arbitrary",)))(R, pe, w1, w2, hid, dout)


def _attn_cfg(S, Sk, mode):
    tq = _pick(S, (256, 128))
    tk = Sk if mode == "cmp" else _pick(Sk, (256, 128))
    return tq, tk


def _block_start(kb, tk):
    return kb * tk if isinstance(kb, int) else pl.multiple_of(kb * tk, tk)


def _pipelined_key_blocks(mode, q0, tq, tk, produce, consume):
    if mode == "cmp":
        produce(0, True, 0)
        consume(0, 0)
        return
    if mode == "sel":
        first, n_plain, plain_masked = 0, q0 // tk, False
    else:
        first = jnp.maximum(q0 - (WINDOW - 1), 0) // tk
        n_plain, plain_masked = (q0 + tq - 1) // tk - first, True
    last = first + n_plain
    pairs = jnp.maximum(n_plain - 1, 0) // 2

    @pl.when(n_plain >= 1)
    def _():
        produce(first, plain_masked, 0)

    def two(j, carry):
        kb = first + 2 * j
        produce(kb + 1, plain_masked, 1)
        consume(kb, 0)
        produce(kb + 2, plain_masked, 0)
        consume(kb + 1, 1)
        return carry

    lax.fori_loop(0, pairs, two, 0)
    kb = first + 2 * pairs
    left = n_plain - 2 * pairs

    @pl.when(left == 2)
    def _():
        produce(kb + 1, plain_masked, 1)
        consume(kb, 0)
        produce(last, True, 0)
        consume(kb + 1, 1)
        consume(last, 0)

    @pl.when(left == 1)
    def _():
        produce(last, True, 1)
        consume(kb, 0)
        consume(last, 1)

    @pl.when(left == 0)
    def _():
        produce(last, True, 0)
        consume(last, 0)


def _attn_bias(mode, q0, k0, tq, tk):
    k = k0 + lax.broadcasted_iota(jnp.int32, (tk, tq), 0)
    t = q0 + lax.broadcasted_iota(jnp.int32, (tk, tq), 1)
    if mode == "cmp":
        ok = (k * 16 + 31) <= t
    elif mode == "win":
        ok = (k <= t) & ((t - k) < WINDOW)
    else:
        ok = k <= t
    bias = jnp.where(ok, 0.0, NEG)
    return jnp.concatenate([bias] * GRP, axis=1), jnp.concatenate([ok.astype(F32)] * GRP, axis=1)


def _sel_operands(qs, selneg_ref):
    return jnp.concatenate([qs, jnp.concatenate([selneg_ref[0]] * GRP, axis=0)], axis=1)


def _stack_heads(ref, tq):
    return jnp.concatenate([ref[:, g * HD:(g + 1) * HD] for g in range(GRP)], axis=0)


def _scaled_queries(q_ref, tq):
    return (_stack_heads(q_ref, tq).astype(F32) * SCALE).astype(_MXU)


def _blocked_t(x, tk):
    n, Sk, d = x.shape
    return x.reshape(n, Sk // tk, tk, d).transpose(0, 1, 3, 2)


def _head_rows(ref):
    return jnp.concatenate([ref[0, g:g + 1, :] for g in range(GRP)], axis=1)


def _attn_fwd(q, qcol0, k, vt, mode, selneg, gate, y_prev, y_dtype, name):
    S, Sk = q.shape[0], k.shape[1]
    tq, tk = _attn_cfg(S, Sk, mode)
    R = GRP * tq

    def body(*refs):
        q_ref, k_ref, vt_ref = refs[:3]
        rest = list(refs[3:])
        sel_ref = rest.pop(0) if mode == "sel" else None
        gate_ref = rest.pop(0)
        yp_ref = rest.pop(0) if y_prev is not None else None
        o_ref, lse_ref, y_ref, m_scr, l_scr, acc, s_scr = rest
        q0 = pl.program_id(1) * tq
        qs = _scaled_queries(q_ref, tq)
        m_scr[...] = jnp.full_like(m_scr, NEG)
        l_scr[...] = jnp.zeros_like(l_scr)
        acc[...] = jnp.zeros_like(acc)
        qk = _sel_operands(qs, sel_ref) if mode == "sel" else qs

        def produce(kb, masked, slot):
            k0 = _block_start(kb, tk)
            s = _dot(k_ref[0, pl.ds(k0, tk), :], qk, "nt")
            if masked:
                s = s + _attn_bias(mode, q0, k0, tq, tk)[0]
            s_scr[slot] = s

        def consume(kb, slot):
            s = s_scr[slot]
            m_old = m_scr[...]
            m_new = jnp.maximum(m_old, jnp.max(s, axis=0, keepdims=True))
            p = jnp.exp(s - m_new)
            if mode == "cmp":
                p = p * _attn_bias(mode, q0, 0, tq, tk)[1]
            alpha = jnp.exp(m_old - m_new)
            l_scr[...] = alpha * l_scr[...] + jnp.sum(p, axis=0, keepdims=True)
            acc[...] = alpha * acc[...] + _dot(vt_ref[0, kb], p, "nn")
            m_scr[...] = m_new

        _pipelined_key_blocks(mode, q0, tq, tk, produce, consume)
        l = l_scr[...]
        good = l > 0.0
        o_t = acc[...] * jnp.where(good, 1.0 / jnp.where(good, l, 1.0), 0.0)
        lse = jnp.where(good, m_scr[...] + jnp.log(jnp.where(good, l, 1.0)), -NEG)
        y_t = o_t * _sigmoid(_head_rows(gate_ref))
        for g in range(GRP):
            hs, qs_ = slice(g * HD, (g + 1) * HD), slice(g * tq, (g + 1) * tq)
            o_ref[:, hs] = o_t[:, qs_].T
            lse_ref[0, g:g + 1, :] = lse[:, qs_]
            yg = y_t[:, qs_].T
            if y_prev is not None:
                yg = yg + yp_ref[:, hs]
            y_ref[:, hs] = yg.astype(y_ref.dtype)

    row_spec = pl.BlockSpec((1, GRP, tq), lambda h, i: (h, 0, i))
    qo_spec = pl.BlockSpec((tq, GRP * HD), lambda h, i: (i, h))
    ins = [q, k, vt]
    specs = [pl.BlockSpec((tq, GRP * HD), lambda h, i: (i, qcol0 + h)), pl.BlockSpec((1, Sk, k.shape[2]), lambda h, i: (h, 0, 0)),
             pl.BlockSpec((1, Sk // tk, HD, tk), lambda h, i: (h, 0, 0, 0))]
    if mode == "sel":
        assert tq == tk
        ins.append(selneg)
        specs.append(pl.BlockSpec((1, tq, selneg.shape[2]), lambda h, i: (h, i, 0)))
    ins.append(gate)
    specs.append(row_spec)
    if y_prev is not None:
        ins.append(y_prev)
        specs.append(qo_spec)
    return pl.pallas_call(
        body, name=name, grid=(N_KV, S // tq), in_specs=specs, out_specs=[qo_spec, row_spec, qo_spec],
        out_shape=[jax.ShapeDtypeStruct((S, ATT_WIDTH), F32), jax.ShapeDtypeStruct((N_KV, GRP, S), F32),
                   jax.ShapeDtypeStruct((S, ATT_WIDTH), y_dtype)],
        scratch_shapes=[pltpu.VMEM((1, R), F32), pltpu.VMEM((1, R), F32), pltpu.VMEM((HD, R), F32), pltpu.VMEM((2, tk, R), F32)],
        compiler_params=_cp(("parallel", "arbitrary")))(*ins)


def _attn_bwd(q, qcol0, k, kt, v, o, lse, dy, dycol0, gate, mode, selneg, name):
    S, Sk = q.shape[0], k.shape[1]
    tq, tk = _attn_cfg(S, Sk, mode)
    R = GRP * tq

    def body(*refs):
        if mode == "sel":
            (q_ref, k_ref, kt_ref, v_ref, o_ref, lse_ref, dy_ref, gate_ref, sel_ref, dq_ref, dk_ref, dv_ref, dg_ref, dq_scr, s_scr,
             dp_scr) = refs
        else:
            q_ref, k_ref, kt_ref, v_ref, o_ref, lse_ref, dy_ref, gate_ref, dq_ref, dk_ref, dv_ref, dg_ref, dq_scr, s_scr, dp_scr = refs

        @pl.when(pl.program_id(1) == 0)
        def _():
            dk_ref[...] = jnp.zeros_like(dk_ref)
            dv_ref[...] = jnp.zeros_like(dv_ref)

        q0 = pl.program_id(1) * tq
        qs = _scaled_queries(q_ref, tq)
        dys = _stack_heads(dy_ref, tq)
        gv = _sigmoid(_head_rows(gate_ref))
        dy_o = _dot(jnp.ones((8, HD), F32), dys * _stack_heads(o_ref, tq), "nt", split="b")[0:1, :]
        delta = gv * dy_o
        dgate = dy_o * (gv * (1.0 - gv))
        for g in range(GRP):
            dg_ref[0, g:g + 1, :] = dgate[:, g * tq:(g + 1) * tq]
        lsev = _head_rows(lse_ref)
        dos = (dys * jnp.broadcast_to(gv, (8, R)).T[:, 0:1]).astype(_MXU)
        dq_scr[...] = jnp.zeros_like(dq_scr)
        qk = _sel_operands(qs, sel_ref) if mode == "sel" else qs

        def produce(kb, masked, slot):
            k0 = _block_start(kb, tk)
            s = _dot(k_ref[0, pl.ds(k0, tk), :], qk, "nt")
            if masked:
                s = s + _attn_bias(mode, q0, k0, tq, tk)[0]
            s_scr[slot] = s
            dp_scr[slot] = _dot(v_ref[0, pl.ds(k0, tk), :], dos, "nt")

        def consume(kb, slot):
            k0 = _block_start(kb, tk)
            p = jnp.exp(s_scr[slot] - lsev)
            if mode == "cmp":
                p = p * _attn_bias(mode, q0, 0, tq, tk)[1]
            ds = p * (dp_scr[slot] - delta)
            dq_scr[...] += _dot(kt_ref[0, kb], ds, "nn")
            dk_ref[0, pl.ds(k0, tk), :] += _dot(ds, qs, "nn")
            dv_ref[0, pl.ds(k0, tk), :] += _dot(p, dos, "nn")

        _pipelined_key_blocks(mode, q0, tq, tk, produce, consume)
        for g in range(GRP):
            dq_ref[:, g * HD:(g + 1) * HD] = (dq_scr[:, g * tq:(g + 1) * tq] * SCALE).T

    kv_spec = pl.BlockSpec((1, Sk, HD), lambda h, i: (h, 0, 0))
    qo_spec = pl.BlockSpec((tq, GRP * HD), lambda h, i: (i, h))
    row_spec = pl.BlockSpec((1, GRP, tq), lambda h, i: (h, 0, i))
    ins = [q, k, kt, v, o, lse, dy, gate]
    specs = [pl.BlockSpec((tq, GRP * HD), lambda h, i: (i, qcol0 + h)), pl.BlockSpec((1, Sk, k.shape[2]), lambda h, i: (h, 0, 0)),
             pl.BlockSpec((1, Sk // tk, HD, tk), lambda h, i: (h, 0, 0, 0)), kv_spec, qo_spec, row_spec,
             pl.BlockSpec((tq, GRP * HD), lambda h, i: (i, dycol0 + h)), row_spec]
    if mode == "sel":
        assert tq == tk
        ins.append(selneg)
        specs.append(pl.BlockSpec((1, tq, selneg.shape[2]), lambda h, i: (h, i, 0)))
    return pl.pallas_call(
        body, name=name, grid=(N_KV, S // tq), in_specs=specs, out_specs=[qo_spec, kv_spec, kv_spec, row_spec],
        out_shape=[jax.ShapeDtypeStruct((S, ATT_WIDTH), F32), jax.ShapeDtypeStruct((N_KV, Sk, HD), F32),
                   jax.ShapeDtypeStruct((N_KV, Sk, HD), F32), jax.ShapeDtypeStruct((N_KV, GRP, S), F32)],
        scratch_shapes=[pltpu.VMEM((HD, R), F32), pltpu.VMEM((2, tk, R), F32), pltpu.VMEM((2, tk, R), F32)],
        compiler_params=_cp(("parallel", "arbitrary")))(*ins)


def _select(q, qcol0, k_cmp, lse):
    S, NC = q.shape[0], k_cmp.shape[1]
    NB = S // SEL_BLOCK
    tq = _pick(S, (256, 128))
    ci = np.arange(NC)[None, :] * 16
    sj = np.arange(NB)[:, None] * SEL_BLOCK
    ov_t = np.clip(np.minimum(ci + 32, sj + SEL_BLOCK) - np.maximum(ci, sj), 0, None) / 32.0
    ov_t[:, NC - 1] = 0.0
    ov_t = jnp.asarray(ov_t, F32)

    def body(q_ref, k_ref, lse_ref, ov_ref, sel_ref):
        q0 = pl.program_id(1) * tq
        bias, okf = _attn_bias("cmp", q0, 0, tq, NC)
        lsev = _head_rows(lse_ref)
        p = jnp.exp(_dot(k_ref[0], _scaled_queries(q_ref, tq), "nt") + bias - lsev) * okf
        imp4 = _dot(ov_ref[...], p, "nn")
        imp = imp4[:, 0:tq] + imp4[:, tq:2 * tq] + imp4[:, 2 * tq:3 * tq] + imp4[:, 3 * tq:4 * tq]
        blk = lax.broadcasted_iota(jnp.int32, (NB, tq), 0)
        cur = lax.shift_right_logical(q0 + lax.broadcasted_iota(jnp.int32, (NB, tq), 1), 6)
        imp = jnp.where((blk == 0) | (blk == cur) | (blk == cur - 1), FORCE, imp)
        imp = jnp.where(blk <= cur, imp, -1.0)
        rank = jnp.zeros((NB, tq), F32)
        for j in range(NB):
            row = imp[j:j + 1, :]
            ahead = (row > imp) | ((row == imp) & (blk > j))
            rank = rank + ahead.astype(F32)
        chosen = (rank < float(N_SELECT)) & (imp >= 0.0)
        sel_ref[0] = jnp.where(chosen, 0.0, NEG).T.astype(sel_ref.dtype)

    return pl.pallas_call(
        body, name="select_blocks", grid=(N_KV, S // tq),
        in_specs=[pl.BlockSpec((tq, GRP * HD), lambda h, i: (i, qcol0 + h)), pl.BlockSpec((1, NC, HD), lambda h, i: (h, 0, 0)),
                  pl.BlockSpec((1, GRP, tq), lambda h, i: (h, 0, i)), pl.BlockSpec((NB, NC), lambda h, i: (0, 0))],
        out_specs=pl.BlockSpec((1, tq, NB), lambda h, i: (h, i, 0)),
        out_shape=jax.ShapeDtypeStruct((N_KV, S, NB), _MXU), compiler_params=_cp(("parallel", "parallel")))(q, k_cmp, lse, ov_t)


def _to_rows16(x):
    S = x.shape[0]
    return x.reshape(S // 16, 16, N_KV, HD).transpose(2, 0, 1, 3).reshape(N_KV, S // 16, 16 * HD)


def _from_rows16(r):
    NC = r.shape[1]
    return r.reshape(N_KV, NC, 16, HD).transpose(1, 2, 0, 3).reshape(NC * 16, N_KV * HD)


DT_COL0 = SSD_WIDTH + CONV_CH
GATE_IN_COL0 = D_IN - 3 * N_HEADS


SHARD_IN = D_IN // N_DEV


def _orig_cols(ref, c0, width):
    pieces, c = [], c0
    while c < c0 + width:
        d, off = divmod(c, SHARD_IN)
        w = min(SHARD_IN - off, c0 + width - c)
        pieces.append(ref[d, :, off:off + w])
        c += w
    return pieces[0] if len(pieces) == 1 else jnp.concatenate(pieces, axis=1)


def _cols_from_slabs(slabs):
    _, R, c = slabs.shape
    tr = _pick(R, (256, 128))

    def body(s_ref, o_ref):
        for t in range(N_DEV * c // LANE):
            pieces, col = [], t * LANE
            while col < (t + 1) * LANE:
                d, off = divmod(col, c)
                w = min(c - off, (t + 1) * LANE - col)
                pieces.append(s_ref[d, :, off:off + w])
                col += w
            o_ref[:, t * LANE:(t + 1) * LANE] = pieces[0] if len(pieces) == 1 else jnp.concatenate(pieces, axis=1)

    return pl.pallas_call(
        body, name="cols_from_slabs", grid=(R // tr,), in_specs=[pl.BlockSpec((N_DEV, tr, c), lambda i: (0, i, 0))],
        out_specs=pl.BlockSpec((tr, N_DEV * c), lambda i: (i, 0)), out_shape=jax.ShapeDtypeStruct((R, N_DEV * c), slabs.dtype),
        compiler_params=_cp(("parallel",)))(slabs)


def _slabs_from_cols(x):
    R, c = x.shape[0], x.shape[1] // N_DEV
    tr = _pick(R, (256, 128))

    def body(x_ref, o_ref):
        for d in range(N_DEV):
            o_ref[d] = x_ref[:, d * c:(d + 1) * c]

    return pl.pallas_call(
        body, name="slabs_from_cols", grid=(R // tr,), in_specs=[pl.BlockSpec((tr, N_DEV * c), lambda i: (i, 0))],
        out_specs=pl.BlockSpec((N_DEV, tr, c), lambda i: (0, i, 0)), out_shape=jax.ShapeDtypeStruct((N_DEV, R, c), x.dtype),
        compiler_params=_cp(("parallel",)))(x)


def _w_in_from_slabs(slabs):
    D = slabs.shape[1]
    tr = _pick(D, (256, 128))

    def body(s_ref, main_ref, small_ref):
        for t in range(W_MAIN // LANE):
            c = t * LANE
            main_ref[:, c:c + LANE] = _orig_cols(s_ref, c if c < DT_COL0 else c + SSD_HEADS, LANE)
        small_ref[...] = jnp.concatenate(
            [_orig_cols(s_ref, DT_COL0, SSD_HEADS), _orig_cols(s_ref, GATE_IN_COL0, 3 * N_HEADS),
             jnp.zeros((tr, W_SMALL - SSD_HEADS - 3 * N_HEADS), small_ref.dtype)], axis=1)

    return pl.pallas_call(
        body, name="w_in_layout", grid=(D // tr,), in_specs=[pl.BlockSpec((N_DEV, tr, SHARD_IN), lambda i: (0, i, 0))],
        out_specs=[pl.BlockSpec((tr, W_MAIN), lambda i: (i, 0)), pl.BlockSpec((tr, W_SMALL), lambda i: (i, 0))],
        out_shape=[jax.ShapeDtypeStruct((D, W_MAIN), slabs.dtype), jax.ShapeDtypeStruct((D, W_SMALL), slabs.dtype)],
        compiler_params=_cp(("parallel",)))(slabs)


def _w_in_to_slabs(main, small):
    D = main.shape[0]
    tr = _pick(D, (256, 128))
    ranges = [(0, DT_COL0, 0, 0), (DT_COL0, DT_COL0 + SSD_HEADS, 1, 0), (DT_COL0 + SSD_HEADS, GATE_IN_COL0, 0, DT_COL0),
              (GATE_IN_COL0, D_IN, 1, SSD_HEADS)]

    def body(main_ref, small_ref, o_ref):
        srcs = (main_ref, small_ref)
        for d in range(N_DEV):
            lo, hi = d * SHARD_IN, (d + 1) * SHARD_IN
            pieces = []
            for start, stop, which, s0 in ranges:
                a, b = max(lo, start), min(hi, stop)
                if a < b:
                    pieces.append(srcs[which][:, s0 + a - start:s0 + b - start].astype(o_ref.dtype))
            o_ref[d] = pieces[0] if len(pieces) == 1 else jnp.concatenate(pieces, axis=1)

    return pl.pallas_call(
        body, name="w_in_grad_layout", grid=(D // tr,),
        in_specs=[pl.BlockSpec((tr, W_MAIN), lambda i: (i, 0)), pl.BlockSpec((tr, W_SMALL), lambda i: (i, 0))],
        out_specs=pl.BlockSpec((N_DEV, tr, SHARD_IN), lambda i: (0, i, 0)),
        out_shape=jax.ShapeDtypeStruct((N_DEV, D, SHARD_IN), main.dtype), compiler_params=_cp(("parallel",)))(main, small)


QB, KCB, VCB, KSB, VSB, KWB, VWB = 10, 14, 15, 16, 17, 18, 19


def _col256(a, b):
    return a[:, b * 256:(b + 1) * 256]


_EARLY = ["w_in", "cmp_w1_k", "cmp_w1_v"]
_LATE = ["w_out", "w_gate", "w_up", "w_down"]
_FFN = ["w_down", "w_gate", "w_up"]
_MID = ["w_out"]
_LAST = ["cmp_w1_k", "cmp_w1_v", "w_in"]


def _local_step(x, tgt, p, early_weights=None, late_weights=None, grads_ready=None):
    S = x.shape[0]
    cos, sin = _rope_tables(S)

    u, rs1 = _rms_fwd(x, p["attn_norm_w"], "attn_norm")
    if early_weights is not None:
        p = {**p, **early_weights((u, cos, sin))}
    proj = _mm(u, p["w_main"], "nn", F32, "in_proj")
    proj_small = _mm(u, p["w_small"], "nn", F32, "in_proj_small")
    xa = _conv_fwd(proj, p["conv_w"], p["conv_b"])
    y_ssd, y_pre, rs_ssd, hs = _ssd_fwd(proj, proj_small, xa, p["dt_bias"], p["a_log"], p["d_skip"], p["ssd_norm_w"])

    q_rot = _rope([proj], QB, ATT_WIDTH, cos, sin, 1.0, _MXU, "rope_q")
    kv = _kv_prep(proj, cos, sin, _attn_cfg(S, S, "sel")[1])
    rk, rv = _to_rows16(_col256(proj, KCB)), _to_rows16(_col256(proj, VCB))
    k_cmp, hid_k = _compress_fwd(rk, p["cmp_pe_k"], p["cmp_w1_k"], p["cmp_w2_k"])
    v_cmp, hid_v = _compress_fwd(rv, p["cmp_pe_v"], p["cmp_w1_v"], p["cmp_w2_v"])
    n_cmp = k_cmp.shape[1]

    gates = proj_small[:, SSD_HEADS:SSD_HEADS + 3 * N_HEADS].reshape(S, N_KV, GRP, 3).transpose(3, 1, 2, 0)
    o_cmp, lse_cmp, y_att = _attn_fwd(proj, QB, k_cmp, _blocked_t(v_cmp, n_cmp), "cmp", None, gates[0], None, F32, "attn_cmp_fwd")
    sel = _select(proj, QB, k_cmp, lse_cmp)
    o_sel, lse_sel, y_att = _attn_fwd(q_rot, 0, kv["ks_ext"], kv["vs_t"], "sel", sel, gates[1], y_att, F32, "attn_sel_fwd")
    o_win, lse_win, y_att = _attn_fwd(q_rot, 0, kv["kw"], kv["vw_t"], "win", None, gates[2], y_att, _MXU, "attn_win_fwd")

    if late_weights is not None:
        p = {**p, **late_weights(y_att)}
    mixed = jnp.concatenate([y_ssd, y_att], axis=1)
    h1 = _mm(mixed, p["w_out"], "nn", F32, "out_proj", res=x)
    v, rs_ffn = _rms_fwd(h1, p["ffn_norm_w"], "ffn_norm")
    gt, up, act = _ffn_up(v, p["w_gate"], p["w_up"])
    h2 = _mm(act, p["w_down"], "nn", F32, "ffn_down", res=h1)
    loss, dh2, dh2b, d_final_w = _final_loss(h2, p["final_norm_w"], tgt)

    def ready(names):
        return None if grads_ready is None else grads_ready(names, g)

    g = {"final_norm_w": d_final_w}
    g["w_down"] = _mm(act, dh2b, "tn", _MXU, "dw_down")
    dgt, dup = _ffn_dact(dh2b, p["w_down"], gt, up)
    g["w_gate"] = _mm(v, dgt, "tn", _MXU, "dw_gate")
    g["w_up"] = _mm(v, dup, "tn", _MXU, "dw_up")
    dv = _ffn_dv(dgt, dup, p["w_gate"], p["w_up"], ready(_FFN))
    dh1, dh1b, g["ffn_norm_w"] = _rms_bwd(dv, h1, rs_ffn, p["ffn_norm_w"], dh2, "ffn_norm_bwd")
    g["w_out"] = _mm(mixed, dh1b, "tn", _MXU, "dw_out")
    dmixed = _mm(dh1b, p["w_out"], "nt", F32, "dmixed", after=ready(_MID))

    dz, dxa, ddtr, g["dt_bias"], g["a_log"], g["d_skip"], g["ssd_norm_w"] = _ssd_bwd(
        dmixed, proj, proj_small, xa, y_pre, rs_ssd, hs, p["dt_bias"], p["a_log"], p["d_skip"], p["ssd_norm_w"])
    dxbc, g["conv_w"], g["conv_b"] = _conv_bwd(proj, p["conv_w"], p["conv_b"], dxa)

    dyb = SSD_WIDTH // (GRP * HD)
    dq_cmp, dk_cmp, dv_cmp, dg_cmp = _attn_bwd(proj, QB, k_cmp, _blocked_t(k_cmp, n_cmp), v_cmp, o_cmp, lse_cmp, dmixed, dyb,
                                               gates[0], "cmp", None, "attn_cmp_bwd")
    dq_sel, dks, dvs, dg_sel = _attn_bwd(q_rot, 0, kv["ks_ext"], kv["ks_t"], kv["vs"], o_sel, lse_sel, dmixed, dyb, gates[1], "sel",
                                         sel, "attn_sel_bwd")
    dq_win, dkw, dvw, dg_win = _attn_bwd(q_rot, 0, kv["kw"], kv["kw_t"], kv["vw"], o_win, lse_win, dmixed, dyb, gates[2], "win", None,
                                         "attn_win_bwd")
    dgate = jnp.stack([dg_cmp, dg_sel, dg_win]).transpose(3, 1, 2, 0).reshape(S, 3 * N_HEADS)
    drk, g["cmp_w1_k"], g["cmp_w2_k"], g["cmp_pe_k"] = _compress_bwd(rk, p["cmp_pe_k"], p["cmp_w1_k"], p["cmp_w2_k"], hid_k, dk_cmp)
    drv, g["cmp_w1_v"], g["cmp_w2_v"], g["cmp_pe_v"] = _compress_bwd(rv, p["cmp_pe_v"], p["cmp_w1_v"], p["cmp_w2_v"], hid_v, dv_cmp)
    dq = _rope([dq_sel, dq_win], 0, ATT_WIDTH, cos, sin, -1.0, _MXU, "rope_dq", extra=(dq_cmp, 0))
    dkv = _dkv_post(dks, dvs, dkw, dvw, cos, sin)
    dproj = jnp.concatenate([dz, dxbc, dq] + [t.astype(_MXU) for t in (_from_rows16(drk), _from_rows16(drv))] + [dkv], axis=1)
    dsmall = jnp.concatenate([ddtr, dgate, jnp.zeros((S, W_SMALL - SSD_HEADS - 3 * N_HEADS), F32)], axis=1).astype(_MXU)
    g["w_main"] = _mm(u, dproj, "tn", _MXU, "dw_in")
    g["w_small"] = _mm(u, dsmall, "tn", F32, "dw_in_small")
    du = _mm(dproj, p["w_main"], "nt", F32, "du_main", after=ready(_LAST))
    du = _mm(dsmall, p["w_small"], "nt", F32, "du_small", res=du)
    grad_x, _, g["attn_norm_w"] = _rms_bwd(du, x, rs1, p["attn_norm_w"], dh1, "attn_norm_bwd")
    return loss, grad_x, g


MESH_ID = pl.DeviceIdType.MESH


def _my_coords():
    return lax.axis_index("x"), lax.axis_index("y"), lax.axis_index("c")


def _flat_id(px, py, pc):
    return 4 * px + 2 * py + pc


def _peer(k):
    mx, my, mc = _my_coords()
    return (1 - mx if k & 4 else mx, 1 - my if k & 2 else my, 1 - mc if k & 1 else mc)


def _exchange(arrs, scatter, name, after=()):
    n, na = len(arrs), len(after)
    scatter = [scatter] * n if isinstance(scatter, bool) else list(scatter)

    def body(*refs):
        ins, outs = refs[:n], refs[n + na:2 * n + na]
        send_sems, recv_sems, local_sems = refs[2 * n + na:]
        me = _flat_id(*_my_coords())
        copies = []
        for i in range(n):
            src_me = ins[i].at[me] if scatter[i] else ins[i]
            local = pltpu.make_async_copy(src_me, outs[i].at[me], local_sems.at[i])
            local.start()
            copies.append(local)
        for k in range(1, N_DEV):
            peer = _peer(k)
            for i in range(n):
                src = ins[i].at[_flat_id(*peer)] if scatter[i] else ins[i]
                cp = pltpu.make_async_remote_copy(src_ref=src, dst_ref=outs[i].at[me], send_sem=send_sems.at[i * 7 + k - 1],
                                                  recv_sem=recv_sems.at[i * 7 + k - 1], device_id=peer, device_id_type=MESH_ID)
                cp.start()
                copies.append(cp)
        for cp in copies:
            cp.wait()

    any_spec = pl.BlockSpec(memory_space=pl.ANY)
    out_shape = [jax.ShapeDtypeStruct(a.shape if sc else (N_DEV,) + a.shape, a.dtype) for a, sc in zip(arrs, scatter)]
    return pl.pallas_call(
        body, name=name, in_specs=[any_spec] * (n + na), out_specs=[any_spec] * n, out_shape=out_shape,
        scratch_shapes=[pltpu.SemaphoreType.DMA((n * 7,)), pltpu.SemaphoreType.DMA((n * 7,)), pltpu.SemaphoreType.DMA((n,))],
        compiler_params=pltpu.CompilerParams(has_side_effects=True))(*arrs, *after)


_HBM = pl.BlockSpec(memory_space=pltpu.HBM)
_SEM = pl.BlockSpec(memory_space=pltpu.SEMAPHORE)
_EFFECT = pltpu.SideEffectType.DATAFLOW_SIDE_EFFECTING


def _split_copies(ins, lands, send_sems, recv_sems, scatter):
    me = _flat_id(*_my_coords())
    out = []
    for k in range(1, N_DEV):
        peer = _peer(k)
        for i in range(len(ins)):
            src = ins[i].at[_flat_id(*peer)] if scatter else ins[i]
            out.append(pltpu.make_async_remote_copy(src_ref=src, dst_ref=lands[i].at[me], send_sem=send_sems.at[i * 7 + k - 1],
                                                    recv_sem=recv_sems.at[i * 7 + k - 1], device_id=peer, device_id_type=MESH_ID))
    return out


def _split_start(arrs, scatter, name):
    n = len(arrs)

    def body(*refs):
        for cp in _split_copies(refs[:n], refs[n:2 * n], refs[2 * n], refs[2 * n + 1], scatter):
            cp.start()
        refs[-1][...] = jnp.zeros_like(refs[-1])

    land_shapes = [a.shape if scatter else (N_DEV,) + a.shape for a in arrs]
    out_shape = ((pltpu.SemaphoreType.DMA((n * 7,)), pltpu.SemaphoreType.DMA((n * 7,)))
                 + tuple(pltpu.HBM(a.shape, a.dtype) for a in arrs) + tuple(pltpu.HBM(s, a.dtype) for s, a in zip(land_shapes, arrs))
                 + (jax.ShapeDtypeStruct((8, 128), F32),))
    operands = ([pltpu.with_memory_space_constraint(a, pltpu.HBM) for a in arrs]
                + [pltpu.with_memory_space_constraint(lax.empty(s, a.dtype), pltpu.HBM) for s, a in zip(land_shapes, arrs)])
    res = pl.pallas_call(
        body, name=name, out_shape=out_shape, in_specs=[_HBM] * (2 * n),
        out_specs=(_SEM, _SEM) + (_HBM,) * (2 * n) + (pl.BlockSpec(memory_space=pltpu.VMEM),),
        input_output_aliases={i: 2 + i for i in range(2 * n)},
        compiler_params=pltpu.CompilerParams(has_side_effects=_EFFECT))(*operands)
    return dict(send=res[0], recv=res[1], ins=list(res[2:2 + n]), lands=list(res[2 + n:2 + 2 * n]), token=res[-1])


def _split_wait(st, scatter, after, name):
    n = len(st["ins"])

    def body(*refs):
        for cp in _split_copies(refs[:n], refs[n:2 * n], refs[2 * n], refs[2 * n + 1], scatter):
            cp.wait_send()
            cp.wait_recv()

    arrs = st["ins"] + st["lands"]
    res = pl.pallas_call(
        body, name=name, out_shape=tuple(pltpu.HBM(a.shape, a.dtype) for a in arrs),
        in_specs=[_HBM] * (2 * n) + [_SEM, _SEM] + [pl.BlockSpec(memory_space=pl.ANY)] * len(after), out_specs=(_HBM,) * (2 * n),
        input_output_aliases={i: i for i in range(2 * n)},
        compiler_params=pltpu.CompilerParams(has_side_effects=_EFFECT))(*arrs, st["send"], st["recv"], *after)
    me = _flat_id(*_my_coords())
    out = []
    for src, land in zip(res[:n], res[n:]):
        own = lax.dynamic_index_in_dim(src, me, 0, keepdims=True) if scatter else src[None]
        out.append(lax.dynamic_update_slice_in_dim(land, own, me, 0))
    return out


def _adam_step(p_ref, w_ref, m_ref, v_ref, g_ref, d_ref, nm_ref, nv_ref):
    g = p_ref[0].astype(F32)
    for j in range(1, p_ref.shape[0]):
        g = g + p_ref[j].astype(F32)
    g_ref[...] = g
    nm = ADAM_B1 * m_ref[...] + (1.0 - ADAM_B1) * g
    nv = ADAM_B2 * v_ref[...] + (1.0 - ADAM_B2) * (g * g)
    nm_ref[...] = nm
    nv_ref[...] = nv
    m_hat = nm / (1.0 - ADAM_B1 ** ADAM_STEP)
    v_hat = nv / (1.0 - ADAM_B2 ** ADAM_STEP)
    d_ref[...] = -ADAM_LR * (m_hat / (jnp.sqrt(v_hat) + ADAM_EPS) + ADAM_WD * w_ref[...])


def _adam_sum(parts, w, m, v, name):
    P, R, C = parts.shape
    tr = _pick(R, (256, 128, 64, 32, 8)) if C <= 1024 else _pick(R, (128, 64, 32, 8))
    blk = pl.BlockSpec((tr, C), lambda i: (i, 0))
    return pl.pallas_call(
        functools.partial(_adam_step), name=name, grid=(R // tr,),
        in_specs=[pl.BlockSpec((P, tr, C), lambda i: (0, i, 0)), blk, blk, blk],
        out_specs=[blk] * 4, out_shape=[jax.ShapeDtypeStruct((R, C), F32)] * 4, compiler_params=_cp(("parallel",)))(parts, w, m, v)


def _adam_small(loss_parts, parts, ws, ms, vs):
    n = len(parts)

    def body(*refs):
        loss_ref, ins, outs, total_ref = refs[0], refs[1:4 * n + 1], refs[4 * n + 1:-1], refs[-1]
        for i in range(n):
            _adam_step(ins[i], ins[n + i], ins[2 * n + i], ins[3 * n + i], *outs[4 * i:4 * i + 4])
        total = loss_ref[0]
        for d in range(1, N_DEV):
            total = total + loss_ref[d]
        total_ref[...] = total

    out_shape = [jax.ShapeDtypeStruct(w.shape, F32) for w in ws for _ in range(4)] + [jax.ShapeDtypeStruct(loss_parts.shape[1:], F32)]
    res = pl.pallas_call(body, name="adam_small", out_shape=out_shape)(loss_parts, *parts, *ws, *ms, *vs)
    return res[-1], [tuple(res[4 * i:4 * i + 4]) for i in range(n)]


_WEIGHTS = ["attn_norm_w", "w_in", "conv_w", "conv_b", "dt_bias", "a_log", "d_skip", "ssd_norm_w", "cmp_w1_k", "cmp_w2_k",
            "cmp_w1_v", "cmp_w2_v", "cmp_pe_k", "cmp_pe_v", "w_out", "ffn_norm_w", "w_gate", "w_up", "w_down", "final_norm_w"]
_BIG = ["w_in", "w_gate", "w_up", "w_down", "w_out", "cmp_w1_k", "cmp_w1_v"]
_COL_SHARDED = ("w_in", "w_gate", "w_up")
_REPLICATED = ["attn_norm_w", "conv_b", "dt_bias", "a_log", "d_skip", "ssd_norm_w", "cmp_pe_k", "cmp_pe_v", "ffn_norm_w",
               "final_norm_w"]
_SMALL_SHARDED = ["conv_w", "cmp_w2_k", "cmp_w2_v"]


def _cols_to_slabs(g):
    R = g.shape[0]
    return g.reshape(R, N_DEV, -1).transpose(1, 0, 2)


def _slabs_to_cols(s):
    return s.transpose(1, 0, 2).reshape(s.shape[1], -1)


def kernel(x, attn_norm_w, w_in, conv_w, conv_b, dt_bias, a_log, d_skip, ssd_norm_w, cmp_w1_k, cmp_w2_k, cmp_w1_v, cmp_w2_v, cmp_pe_k, cmp_pe_v, w_out, ffn_norm_w, w_gate, w_up, w_down, final_norm_w, loss_target, m_attn_norm_w, m_w_in, m_conv_w, m_conv_b, m_dt_bias, m_a_log, m_d_skip, m_ssd_norm_w, m_cmp_w1_k, m_cmp_w2_k, m_cmp_w1_v, m_cmp_w2_v, m_cmp_pe_k, m_cmp_pe_v, m_w_out, m_ffn_norm_w, m_w_gate, m_w_up, m_w_down, m_final_norm_w, v_attn_norm_w, v_w_in, v_conv_w, v_conv_b, v_dt_bias, v_a_log, v_d_skip, v_ssd_norm_w, v_cmp_w1_k, v_cmp_w2_k, v_cmp_w1_v, v_cmp_w2_v, v_cmp_pe_k, v_cmp_pe_v, v_w_out, v_ffn_norm_w, v_w_gate, v_w_up, v_w_down, v_final_norm_w):
    a = dict(locals())

    shard = {n: a[n][0].astype(_MXU) for n in _BIG}
    early_small = [cmp_w2_k[0], cmp_w2_v[0], conv_w[0]]
    st_early = _split_start([shard[n] for n in _EARLY] + early_small, False, "gather_early_start")
    zero = st_early["token"][0, 0].astype(_MXU)
    st_late = _split_start([shard[_LATE[0]] + zero] + [shard[n] for n in _LATE[1:]], False, "gather_late_start")

    def assemble(n, t):
        return _cols_from_slabs(t) if n in _COL_SHARDED else t.reshape(-1, t.shape[-1])

    p = dict(attn_norm_w=attn_norm_w, conv_b=conv_b, dt_bias=dt_bias, a_log=a_log, d_skip=d_skip, ssd_norm_w=ssd_norm_w,
             cmp_pe_k=cmp_pe_k.reshape(1, -1), cmp_pe_v=cmp_pe_v.reshape(1, -1), ffn_norm_w=ffn_norm_w,
             final_norm_w=final_norm_w.reshape(1, -1))

    def early_weights(after):
        got = _split_wait(st_early, False, tuple(after) + (st_late["token"],), "gather_early_wait")
        w_main, w_small = _w_in_from_slabs(got[0])
        return dict(w_main=w_main, w_small=w_small, cmp_w1_k=assemble("cmp_w1_k", got[1]), cmp_w1_v=assemble("cmp_w1_v", got[2]),
                    cmp_w2_k=assemble("cmp_w2_k", got[3]).astype(_MXU), cmp_w2_v=assemble("cmp_w2_v", got[4]).astype(_MXU),
                    conv_w=_slabs_to_cols(got[5]))

    def late_weights(after):
        got_late = _split_wait(st_late, False, (after,), "gather_late_wait")
        return {n: assemble(n, t) for n, t in zip(_LATE, got_late)}

    def slabs_of(g, n):
        if n == "w_in":
            return _w_in_to_slabs(g["w_main"], g["w_small"])
        return _slabs_from_cols(g[n]) if n in _COL_SHARDED else g[n].reshape(N_DEV, -1, g[n].shape[-1])

    started = []

    def grads_ready(names, g):
        started.append((names, _split_start([slabs_of(g, n) for n in names], True, "scatter_grads_start_%d" % len(started))))
        return started[-1][1]["token"]

    loss_part, grad_x, g = _local_step(x[0], loss_target[0], p, early_weights, late_weights, grads_ready)

    out, after = {}, (started[-1][1]["token"],)
    for i, (names, st) in enumerate(started):
        if i == len(started) - 1:
            after = after + (grad_x,)
        received = _split_wait(st, True, after, "scatter_grads_wait_%d" % i)
        for n, parts in zip(names, received):
            out[n] = _adam_sum(parts, a[n][0], a["m_" + n][0], a["v_" + n][0], "adam_" + n)
        after = (out[names[-1]][0],)

    small_names = _REPLICATED + _SMALL_SHARDED
    partials = [g[n] for n in _REPLICATED] + [_cols_to_slabs(g["conv_w"])] + [
        g[n].reshape(N_DEV, -1, g[n].shape[-1]) for n in ("cmp_w2_k", "cmp_w2_v")]
    gathered = _exchange([loss_part] + partials, [False] * (1 + len(_REPLICATED)) + [True] * len(_SMALL_SHARDED),
                         "exchange_small_grads", after=(received[0],))
    shapes2d = [t.shape[1:] for t in gathered[1:]]
    loss, res_small = _adam_small(gathered[0], gathered[1:],
                                  *[[a[pre + n].reshape(s) for n, s in zip(small_names, shapes2d)] for pre in ("", "m_", "v_")])
    for n, r in zip(small_names, res_small):
        out[n] = r

    outs = [loss[0, 0], grad_x[None]]
    for j in range(4):
        for n in _WEIGHTS:
            outs.append(out[n][j].reshape(a[n].shape))
    return tuple(outs)
```

```python
import functools

import numpy as np
import jax
import jax.numpy as jnp
from jax import lax
from jax.experimental import pallas as pl
from jax.experimental.pallas import tpu as pltpu

F32 = jnp.float32
_MXU = jnp.bfloat16

N_DEV = 8
D_MODEL = 2048
SSD_WIDTH = 1024
ATT_WIDTH = 1024
SSD_HEADS = 16
SSD_P = 64
SSD_N = 128
SSD_L = 128
SSD_G = 2
CONV_CH = 1536
CONV_K = 4
HD = 64
N_HEADS = 16
N_KV = 4
GRP = 4
CMP_HID = 256
SEL_BLOCK = 64
N_SELECT = 16
WINDOW = 512
ROPE_DIM = 16
ROPE_THETA = 500000.0
D_FF = 5632
EPS = 1e-6
NEG = -1e30
FORCE = 1e4
SCALE = HD ** -0.5
D_IN = 5184
W_MAIN = 5120
W_SMALL = 128
VMEM_LIMIT = 52 * 1024 * 1024

ADAM_LR, ADAM_B1, ADAM_B2, ADAM_EPS, ADAM_WD, ADAM_STEP = 0.001, 0.9, 0.999, 1e-08, 0.01, 10


def _pick(n, cands):
    for c in cands:
        if n % c == 0:
            return c
    return n


def _cp(sem=None):
    return pltpu.CompilerParams(dimension_semantics=sem, vmem_limit_bytes=VMEM_LIMIT)


def _sigmoid(x):
    return 1.0 / (1.0 + jnp.exp(-x))


def _dot(a, b, dims, split=None):
    dn = {"nn": (((1,), (0,)), ((), ())), "nt": (((1,), (1,)), ((), ())), "tn": (((0,), (0,)), ((), ()))}[dims]
    mm = lambda x, y: lax.dot_general(x.astype(_MXU), y.astype(_MXU), dn, preferred_element_type=F32)
    if split is None:
        return mm(a, b)
    x = (a if split == "a" else b).astype(F32)
    hi = x.astype(_MXU)
    lo = x - hi.astype(F32)
    return mm(hi, b) + mm(lo, b) if split == "a" else mm(a, hi) + mm(a, lo)


LANE = 128
MM_TILE = 1024
MM_K_WHOLE = 2048
MM_K_STEP = 1536
TN_ACC_ELEMS = 3 * 2 ** 20
TN_K_STEP = 512


def _largest_tile(n, cap):
    if n <= cap:
        return n
    best = LANE
    for t in range(LANE, cap + 1, LANE):
        if n % t == 0:
            best = t
    return best


def _mm_tiles(mode, M, N, K):
    if mode == "tn":
        tm = _largest_tile(M, 2 * MM_TILE)
        return tm, _largest_tile(N, TN_ACC_ELEMS // tm), _largest_tile(K, TN_K_STEP)
    tk = K if K <= MM_K_WHOLE else _largest_tile(K, MM_K_STEP)
    return _largest_tile(M, MM_TILE), _largest_tile(N, MM_TILE), tk


def _mm(a, b, mode, out_dtype, name, res=None, after=None):
    if mode == "nn":
        (M, K), N = a.shape, b.shape[1]
    elif mode == "nt":
        (M, K), N = a.shape, b.shape[0]
    else:
        (K, M), N = a.shape, b.shape[1]
    tm, tn, tk = _mm_tiles(mode, M, N, K)
    nk = K // tk
    a_spec = pl.BlockSpec((tk, tm), lambda i, j, k: (k, i)) if mode == "tn" else pl.BlockSpec((tm, tk), lambda i, j, k: (i, k))
    b_spec = pl.BlockSpec((tn, tk), lambda i, j, k: (j, k)) if mode == "nt" else pl.BlockSpec((tk, tn), lambda i, j, k: (k, j))
    o_spec = pl.BlockSpec((tm, tn), lambda i, j, k: (i, j))

    def finish(r, r_ref, o_ref):
        if res is not None:
            r = r + r_ref[...].astype(F32)
        o_ref[...] = r.astype(out_dtype)

    def body_one_step(*refs):
        a_ref, b_ref, o_ref = refs[0], refs[1], refs[-1]
        finish(_dot(a_ref[...], b_ref[...], mode), refs[2], o_ref)

    def body(*refs):
        a_ref, b_ref, o_ref, acc = refs[0], refs[1], refs[-2], refs[-1]
        k = pl.program_id(2)

        @pl.when(k == 0)
        def _():
            acc[...] = jnp.zeros_like(acc)

        acc[...] += _dot(a_ref[...], b_ref[...], mode)

        @pl.when(k == nk - 1)
        def _():
            finish(acc[...], refs[2], o_ref)

    ins, specs = [a, b], [a_spec, b_spec]
    if res is not None:
        ins.append(res)
        specs.append(o_spec)
    if after is not None:
        ins.append(after)
        specs.append(pl.BlockSpec(memory_space=pl.ANY))
    return pl.pallas_call(
        body_one_step if nk == 1 else body, name=name, grid=(M // tm, N // tn, nk), in_specs=specs, out_specs=o_spec,
        out_shape=jax.ShapeDtypeStruct((M, N), out_dtype), scratch_shapes=[] if nk == 1 else [pltpu.VMEM((tm, tn), F32)],
        compiler_params=_cp(("parallel", "parallel", "arbitrary")))(*ins)


def _ffn_up(v, w_gate, w_up):
    S, D = v.shape
    F = w_gate.shape[1]
    tm, tn = _largest_tile(S, MM_TILE), _largest_tile(F, MM_TILE // 2)

    def body(v_ref, wg_ref, wu_ref, gt_ref, up_ref, act_ref):
        vv = v_ref[...]
        g = _dot(vv, wg_ref[...], "nn")
        u = _dot(vv, wu_ref[...], "nn")
        gt_ref[...] = g
        up_ref[...] = u
        act_ref[...] = (g * _sigmoid(g) * u).astype(act_ref.dtype)

    o_spec = pl.BlockSpec((tm, tn), lambda i, j: (i, j))
    w_spec = pl.BlockSpec((D, tn), lambda i, j: (0, j))
    return pl.pallas_call(
        body, name="ffn_up", grid=(S // tm, F // tn),
        in_specs=[pl.BlockSpec((tm, D), lambda i, j: (i, 0)), w_spec, w_spec], out_specs=[o_spec, o_spec, o_spec],
        out_shape=[jax.ShapeDtypeStruct((S, F), F32), jax.ShapeDtypeStruct((S, F), F32), jax.ShapeDtypeStruct((S, F), _MXU)],
        compiler_params=_cp(("parallel", "parallel")))(v, w_gate, w_up)


def _ffn_dv(dgt, dup, w_gate, w_up, after):
    S, F = dgt.shape
    D = w_gate.shape[0]
    tm, tn, tk = _mm_tiles("nt", S, D, F)
    nk = F // tk

    def body(g_ref, u_ref, wg_ref, wu_ref, *rest):
        o_ref, acc = rest[-2], rest[-1]
        k = pl.program_id(2)

        @pl.when(k == 0)
        def _():
            acc[...] = jnp.zeros_like(acc)

        acc[...] += _dot(g_ref[...], wg_ref[...], "nt") + _dot(u_ref[...], wu_ref[...], "nt")

        @pl.when(k == nk - 1)
        def _():
            o_ref[...] = acc[...]

    a_spec = pl.BlockSpec((tm, tk), lambda i, j, k: (i, k))
    w_spec = pl.BlockSpec((tn, tk), lambda i, j, k: (j, k))
    ins, specs = [dgt, dup, w_gate, w_up], [a_spec, a_spec, w_spec, w_spec]
    if after is not None:
        ins.append(after)
        specs.append(pl.BlockSpec(memory_space=pl.ANY))
    return pl.pallas_call(
        body, name="ffn_dv", grid=(S // tm, D // tn, nk), in_specs=specs, out_specs=pl.BlockSpec((tm, tn), lambda i, j, k: (i, j)),
        out_shape=jax.ShapeDtypeStruct((S, D), F32), scratch_shapes=[pltpu.VMEM((tm, tn), F32)],
        compiler_params=_cp(("parallel", "parallel", "arbitrary")))(*ins)


def _ffn_dact(dh2, w_down, gt, up):
    S, D = dh2.shape
    F = w_down.shape[0]
    tm, tn = _largest_tile(S, MM_TILE), _largest_tile(F, MM_TILE // 2)

    def body(d_ref, w_ref, gt_ref, up_ref, dg_ref, du_ref):
        da, g, u = _dot(d_ref[...], w_ref[...], "nt"), gt_ref[...], up_ref[...]
        s = _sigmoid(g)
        dg_ref[...] = (da * u * (s * (1.0 + g * (1.0 - s)))).astype(dg_ref.dtype)
        du_ref[...] = (da * (g * s)).astype(du_ref.dtype)

    o_spec = pl.BlockSpec((tm, tn), lambda i, j: (i, j))
    return pl.pallas_call(
        body, name="ffn_dact", grid=(S // tm, F // tn),
        in_specs=[pl.BlockSpec((tm, D), lambda i, j: (i, 0)), pl.BlockSpec((tn, D), lambda i, j: (j, 0)), o_spec, o_spec],
        out_specs=[o_spec, o_spec],
        out_shape=[jax.ShapeDtypeStruct((S, F), _MXU), jax.ShapeDtypeStruct((S, F), _MXU)],
        compiler_params=_cp(("parallel", "parallel")))(dh2, w_down, gt, up)


def _rms_fwd(x, w, name):
    S, D = x.shape
    tr = _pick(S, (256, 128))

    def body(x_ref, w_ref, xn_ref, rs_ref):
        xv = x_ref[...]
        rs = lax.rsqrt(jnp.mean(xv * xv, axis=-1, keepdims=True) + EPS)
        xn_ref[...] = ((xv * rs) * w_ref[...]).astype(xn_ref.dtype)
        rs_ref[...] = rs

    return pl.pallas_call(
        body, name=name, grid=(S // tr,),
        in_specs=[pl.BlockSpec((tr, D), lambda i: (i, 0)), pl.BlockSpec((1, D), lambda i: (0, 0))],
        out_specs=[pl.BlockSpec((tr, D), lambda i: (i, 0)), pl.BlockSpec((tr, 1), lambda i: (i, 0))],
        out_shape=[jax.ShapeDtypeStruct((S, D), _MXU), jax.ShapeDtypeStruct((S, 1), F32)],
        compiler_params=_cp(("parallel",)))(x, w)


def _rms_bwd(dyn, x, rs, w, res, name):
    S, D = x.shape
    tr = _pick(S, (256, 128))

    def body(dy_ref, x_ref, rs_ref, w_ref, res_ref, dx_ref, dxb_ref, dw_ref):
        @pl.when(pl.program_id(0) == 0)
        def _():
            dw_ref[...] = jnp.zeros_like(dw_ref)

        dy, r = dy_ref[...].astype(F32), rs_ref[...]
        xhat = x_ref[...] * r
        dw_ref[...] += jnp.sum(dy * xhat, axis=0, keepdims=True)
        dxhat = dy * w_ref[...]
        dx = res_ref[...] + r * (dxhat - xhat * jnp.mean(dxhat * xhat, axis=-1, keepdims=True))
        dx_ref[...] = dx
        dxb_ref[...] = dx.astype(dxb_ref.dtype)

    row = pl.BlockSpec((tr, D), lambda i: (i, 0))
    vec = pl.BlockSpec((1, D), lambda i: (0, 0))
    return pl.pallas_call(
        body, name=name, grid=(S // tr,),
        in_specs=[row, row, pl.BlockSpec((tr, 1), lambda i: (i, 0)), vec, row], out_specs=[row, row, vec],
        out_shape=[jax.ShapeDtypeStruct((S, D), F32), jax.ShapeDtypeStruct((S, D), _MXU), jax.ShapeDtypeStruct((1, D), F32)],
        compiler_params=_cp(("arbitrary",)))(dyn, x, rs, w, res)


def _final_loss(h2, w, tgt):
    S, D = h2.shape
    tr = _pick(S, (256, 128))

    def body(h_ref, w_ref, t_ref, loss_ref, dh_ref, dhb_ref, dw_ref):
        @pl.when(pl.program_id(0) == 0)
        def _():
            dw_ref[...] = jnp.zeros_like(dw_ref)
            loss_ref[...] = jnp.zeros_like(loss_ref)

        hv, wv = h_ref[...], w_ref[...]
        rs = lax.rsqrt(jnp.mean(hv * hv, axis=-1, keepdims=True) + EPS)
        xhat = hv * rs
        err = xhat * wv - t_ref[...]
        row = jnp.mean(err * err, axis=-1, keepdims=True)
        loss_ref[...] += jnp.broadcast_to(0.5 * jnp.sum(row, axis=0, keepdims=True), loss_ref.shape)
        dy = err * (1.0 / D)
        dw_ref[...] += jnp.sum(dy * xhat, axis=0, keepdims=True)
        dxhat = dy * wv
        dh = rs * (dxhat - xhat * jnp.mean(dxhat * xhat, axis=-1, keepdims=True))
        dh_ref[...] = dh
        dhb_ref[...] = dh.astype(dhb_ref.dtype)

    row = pl.BlockSpec((tr, D), lambda i: (i, 0))
    vec = pl.BlockSpec((1, D), lambda i: (0, 0))
    return pl.pallas_call(
        body, name="final_loss", grid=(S // tr,), in_specs=[row, vec, row],
        out_specs=[pl.BlockSpec((1, LANE), lambda i: (0, 0)), row, row, vec],
        out_shape=[jax.ShapeDtypeStruct((1, LANE), F32), jax.ShapeDtypeStruct((S, D), F32), jax.ShapeDtypeStruct((S, D), _MXU),
                   jax.ShapeDtypeStruct((1, D), F32)],
        compiler_params=_cp(("arbitrary",)))(h2, w, tgt)


def _shift_rows(x, k, rows):
    if k == 0:
        return x
    S = x.shape[0]
    r = pltpu.roll(x, k % S, axis=0)
    ok = (rows >= k) if k > 0 else (rows < S + k)
    return jnp.where(ok, r, 0.0)


XBC_COL0 = SSD_WIDTH // 128


def _conv_fwd(proj, conv_w, conv_b):
    S = proj.shape[0]
    nct = CONV_CH // 128

    def body(x_ref, w_ref, b_ref, o_ref):
        x = x_ref[...]
        rows = lax.broadcasted_iota(jnp.int32, x.shape, 0)
        c = b_ref[...] + w_ref[3:4, :] * x
        for k in range(1, CONV_K):
            c = c + w_ref[3 - k:4 - k, :] * _shift_rows(x, k, rows)
        o_ref[...] = c * _sigmoid(c)

    return pl.pallas_call(
        body, name="conv_fwd", grid=(nct,),
        in_specs=[pl.BlockSpec((S, 128), lambda j: (0, XBC_COL0 + j)), pl.BlockSpec((CONV_K, 128), lambda j: (0, j)),
                  pl.BlockSpec((1, 128), lambda j: (0, j))],
        out_specs=pl.BlockSpec((S, 128), lambda j: (0, j)),
        out_shape=jax.ShapeDtypeStruct((S, CONV_CH), F32), compiler_params=_cp(("parallel",)))(proj, conv_w, conv_b)


def _conv_bwd(proj, conv_w, conv_b, dxa):
    S = proj.shape[0]
    nct = CONV_CH // 128

    def body(x_ref, w_ref, b_ref, d_ref, dx_ref, dw_ref, db_ref):
        x = x_ref[...]
        rows = lax.broadcasted_iota(jnp.int32, x.shape, 0)
        xs = [_shift_rows(x, k, rows) for k in range(CONV_K)]
        c = b_ref[...] + w_ref[3:4, :] * x
        for k in range(1, CONV_K):
            c = c + w_ref[3 - k:4 - k, :] * xs[k]
        s = _sigmoid(c)
        dc = d_ref[...] * (s * (1.0 + c * (1.0 - s)))
        dx = w_ref[3:4, :] * dc
        for k in range(1, CONV_K):
            dx = dx + w_ref[3 - k:4 - k, :] * _shift_rows(dc, -k, rows)
        dx_ref[...] = dx.astype(dx_ref.dtype)
        for k in range(CONV_K):
            dw_ref[3 - k:4 - k, :] = jnp.sum(dc * xs[k], axis=0, keepdims=True)
        db_ref[...] = jnp.sum(dc, axis=0, keepdims=True)

    col = pl.BlockSpec((S, 128), lambda j: (0, j))
    return pl.pallas_call(
        body, name="conv_bwd", grid=(nct,),
        in_specs=[pl.BlockSpec((S, 128), lambda j: (0, XBC_COL0 + j)), pl.BlockSpec((CONV_K, 128), lambda j: (0, j)),
                  pl.BlockSpec((1, 128), lambda j: (0, j)), col],
        out_specs=[col, pl.BlockSpec((CONV_K, 128), lambda j: (0, j)), pl.BlockSpec((1, 128), lambda j: (0, j))],
        out_shape=[jax.ShapeDtypeStruct((S, CONV_CH), _MXU), jax.ShapeDtypeStruct((CONV_K, CONV_CH), F32),
                   jax.ShapeDtypeStruct((1, CONV_CH), F32)],
        compiler_params=_cp(("parallel",)))(proj, conv_w, conv_b, dxa)


def _ssd_consts():
    L = SSD_L
    r = lax.broadcasted_iota(jnp.int32, (L, L), 0)
    c = lax.broadcasted_iota(jnp.int32, (L, L), 1)
    causal = r >= c
    upper = (r <= c).astype(F32)
    hr = lax.broadcasted_iota(jnp.int32, (SSD_HEADS, SSD_WIDTH), 0)
    hc = lax.broadcasted_iota(jnp.int32, (SSD_HEADS, SSD_WIDTH), 1)
    expand = (lax.shift_right_logical(hc, 6) == hr).astype(F32)
    return causal, causal.astype(F32), upper, expand


def _softplus(x):
    return jnp.maximum(x, 0.0) + jnp.log(1.0 + jnp.exp(-jnp.abs(x)))


def _ssd_scalars(dtr, dt_bias, a_log, tri, upper, expand):
    dt = _softplus(dtr + dt_bias)
    A = -jnp.exp(a_log)
    adt = dt * A
    acum = _dot(tri, adt, "nn", split="b")
    acum_t = _dot(adt, upper, "tn", split="a")
    alast = acum[SSD_L - 1:SSD_L, :]
    e = jnp.exp(acum)
    wdec = jnp.exp(alast - acum)
    gam = jnp.exp(alast)
    ex = lambda t: _dot(t, expand, "nn", split="a")
    gam8 = jnp.broadcast_to(gam, (8, SSD_HEADS))
    return dt, A, acum, acum_t, e, wdec, gam, ex(dt), ex(e), ex(wdec), ex(gam8)[0:1, :]


def _ssd_fwd(proj, proj_small, xa, dt_bias, a_log, d_skip, norm_w):
    S = proj.shape[0]
    L, N, W = SSD_L, SSD_N, SSD_WIDTH
    nc = S // L

    def body(z_ref, xa_ref, dtr_ref, dtb_ref, al_ref, dsk_ref, nw_ref, yo_ref, y_ref, rs_ref, hs_ref, h_scr, y_scr):
        @pl.when(pl.program_id(0) == 0)
        def _():
            h_scr[...] = jnp.zeros_like(h_scr)

        causal, tri, upper, expand = _ssd_consts()
        dt, A, acum, acum_t, e, wdec, gam, dtE, eE, wE, gamE = _ssd_scalars(dtr_ref[:, 0:SSD_HEADS], dtb_ref[...], al_ref[...], tri, upper, expand)
        xs = xa_ref[:, 0:W]
        X = xs * dtE
        XW = X * wE
        hs_ref[0] = h_scr[...]
        for g in range(SSD_G):
            gs = slice(g * 512, (g + 1) * 512)
            Bg = xa_ref[:, W + g * N:W + (g + 1) * N]
            Cg = xa_ref[:, W + SSD_G * N + g * N:W + SSD_G * N + (g + 1) * N]
            Hg = h_scr[:, gs]
            CB = _dot(Cg, Bg, "nt")
            yoff = _dot(Cg, Hg, "nn") * eE[:, gs]
            st = _dot(Bg, XW[:, gs], "tn")
            for j in range(8):
                h = g * 8 + j
                hsl = slice(h * SSD_P, (h + 1) * SSD_P)
                lam = jnp.exp(jnp.where(causal, acum[:, h:h + 1] - acum_t[h:h + 1, :], -jnp.inf))
                y_scr[:, hsl] = _dot(CB * lam, X[:, hsl], "nn") + yoff[:, j * SSD_P:(j + 1) * SSD_P]
            h_scr[:, gs] = gamE[:, gs] * Hg + st
        dskE = _dot(jnp.broadcast_to(dsk_ref[...], (8, SSD_HEADS)), expand, "nn", split="a")[0:1, :]
        y = y_scr[...] + dskE * xs
        y_ref[...] = y
        zv = z_ref[...]
        yg = y * (zv * _sigmoid(zv))
        rs = lax.rsqrt(jnp.mean(yg * yg, axis=-1, keepdims=True) + EPS)
        rs_ref[...] = rs
        yo_ref[...] = ((yg * rs) * nw_ref[...]).astype(yo_ref.dtype)

    p16 = pl.BlockSpec((1, SSD_HEADS), lambda c: (0, 0))
    return pl.pallas_call(
        body, name="ssd_fwd", grid=(nc,),
        in_specs=[pl.BlockSpec((L, W), lambda c: (c, 0)), pl.BlockSpec((L, CONV_CH), lambda c: (c, 0)),
                  pl.BlockSpec((L, W_SMALL), lambda c: (c, 0)), p16, p16, p16, pl.BlockSpec((1, W), lambda c: (0, 0))],
        out_specs=[pl.BlockSpec((L, W), lambda c: (c, 0)), pl.BlockSpec((L, W), lambda c: (c, 0)),
                   pl.BlockSpec((L, 1), lambda c: (c, 0)), pl.BlockSpec((1, N, W), lambda c: (c, 0, 0))],
        out_shape=[jax.ShapeDtypeStruct((S, W), _MXU), jax.ShapeDtypeStruct((S, W), F32), jax.ShapeDtypeStruct((S, 1), F32),
                   jax.ShapeDtypeStruct((nc, N, W), F32)],
        scratch_shapes=[pltpu.VMEM((N, W), F32), pltpu.VMEM((L, W), F32)],
        compiler_params=_cp(("arbitrary",)))(proj, xa, proj_small, dt_bias, a_log, d_skip, norm_w)


def _ssd_bwd(dmixed, proj, proj_small, xa, y, rs2, hs, dt_bias, a_log, d_skip, norm_w):
    S = proj.shape[0]
    L, N, W, H = SSD_L, SSD_N, SSD_WIDTH, SSD_HEADS
    nc = S // L

    def body(dyo_ref, z_ref, xa_ref, dtr_ref, y_ref, rs_ref, hs_ref, dtb_ref, al_ref, dsk_ref, nw_ref,
             dz_ref, dxa_ref, ddtr_ref, ddtb_ref, dal_ref, ddsk_ref, dnw_ref, dh_scr, dx_scr):
        @pl.when(pl.program_id(0) == 0)
        def _():
            dh_scr[...] = jnp.zeros_like(dh_scr)
            ddtb_ref[...] = jnp.zeros_like(ddtb_ref)
            dal_ref[...] = jnp.zeros_like(dal_ref)
            ddsk_ref[...] = jnp.zeros_like(ddsk_ref)
            dnw_ref[...] = jnp.zeros_like(dnw_ref)

        causal, tri, upper, expand = _ssd_consts()
        heads = lambda t: _dot(t, expand, "nt", split="a")
        onehot = lambda h: (lax.broadcasted_iota(jnp.int32, (1, H), 1) == h).astype(F32)

        zv, yv, rs = z_ref[...], y_ref[...], rs_ref[...]
        sz = _sigmoid(zv)
        zs = zv * sz
        xhat = (yv * zs) * rs
        dyo = dyo_ref[...].astype(F32)
        dnw_ref[...] += jnp.sum(dyo * xhat, axis=0, keepdims=True)
        dxhat = dyo * nw_ref[...]
        dyg = rs * (dxhat - xhat * jnp.mean(dxhat * xhat, axis=-1, keepdims=True))
        dz_ref[...] = (dyg * yv * (sz * (1.0 + zv * (1.0 - sz)))).astype(dz_ref.dtype)
        dy = dyg * zs

        dtr = dtr_ref[:, 0:H]
        dt, A, acum, acum_t, e, wdec, gam, dtE, eE, wE, gamE = _ssd_scalars(dtr, dtb_ref[...], al_ref[...], tri, upper, expand)
        xs = xa_ref[:, 0:W]
        X = xs * dtE
        XW = X * wE
        dskE = _dot(jnp.broadcast_to(dsk_ref[...], (8, H)), expand, "nn", split="a")[0:1, :]
        ddsk_ref[...] += heads(jnp.broadcast_to(jnp.sum(dy * xs, axis=0, keepdims=True), (8, W)))[0:1, :]

        dYe = dy * eE
        dacum = jnp.zeros((L, H), F32)
        de_full = []
        dw_full = []
        dgam_full = []
        for g in range(SSD_G):
            gs = slice(g * 512, (g + 1) * 512)
            Bg = xa_ref[:, W + g * N:W + (g + 1) * N]
            Cg = xa_ref[:, W + SSD_G * N + g * N:W + SSD_G * N + (g + 1) * N]
            Hg = hs_ref[0, :, gs]
            dHn = dh_scr[:, gs]
            CH = _dot(Cg, Hg, "nn")
            de_full.append(dy[:, gs] * CH)
            dC = _dot(dYe[:, gs], Hg, "nt")
            dHs = gamE[:, gs] * dHn + _dot(Cg, dYe[:, gs], "tn")
            dgam_full.append(jnp.sum(dHn * Hg, axis=0, keepdims=True))
            BdS = _dot(Bg, dHn, "nn")
            dB = _dot(XW[:, gs], dHn, "nt")
            dx_scr[:, gs] = BdS * wE[:, gs]
            dw_full.append(BdS * X[:, gs])
            CB = _dot(Cg, Bg, "nt")
            dCB = jnp.zeros((L, L), F32)
            for j in range(8):
                h = g * 8 + j
                hsl = slice(h * SSD_P, (h + 1) * SSD_P)
                lam = jnp.exp(jnp.where(causal, acum[:, h:h + 1] - acum_t[h:h + 1, :], -jnp.inf))
                M = CB * lam
                dM = _dot(dy[:, hsl], X[:, hsl], "nt")
                dx_scr[:, hsl] += _dot(M, dy[:, hsl], "tn")
                dCB = dCB + dM * lam
                Q = dM * M
                rowsum = jnp.sum(Q, axis=1, keepdims=True)
                colsum = _dot(Q, jnp.ones((L, 8), F32), "tn", split="a")[:, 0:1]
                dacum = dacum + (rowsum - colsum) * onehot(h)
            dC = dC + _dot(dCB, Bg, "nn")
            dB = dB + _dot(dCB, Cg, "tn")
            dxa_ref[:, W + g * N:W + (g + 1) * N] = dB
            dxa_ref[:, W + SSD_G * N + g * N:W + SSD_G * N + (g + 1) * N] = dC
            dh_scr[:, gs] = dHs

        de16 = heads(jnp.concatenate(de_full, axis=1))
        dw16 = heads(jnp.concatenate(dw_full, axis=1))
        dgam16 = heads(jnp.broadcast_to(jnp.concatenate(dgam_full, axis=1), (8, W)))[0:1, :]
        dacum = dacum + de16 * e - dw16 * wdec
        dlast = jnp.sum(dw16 * wdec, axis=0, keepdims=True) + dgam16 * gam
        lastrow = (lax.broadcasted_iota(jnp.int32, (L, 1), 0) == L - 1).astype(F32)
        dacum = dacum + lastrow * dlast
        da = _dot(tri, dacum, "tn", split="b")
        dX = dx_scr[...]
        ddt = da * A + heads(dX * xs)
        dA = jnp.sum(da * dt, axis=0, keepdims=True)
        dal_ref[...] += dA * A
        ddtr = ddt * _sigmoid(dtr + dtb_ref[...])
        ddtb_ref[...] += jnp.sum(ddtr, axis=0, keepdims=True)
        ddtr_ref[...] = ddtr
        dxa_ref[:, 0:W] = dX * dtE + dy * dskE

    p16 = pl.BlockSpec((1, H), lambda c: (0, 0))
    rev = lambda c: (nc - 1 - c, 0)
    return pl.pallas_call(
        body, name="ssd_bwd", grid=(nc,),
        in_specs=[pl.BlockSpec((L, W), rev), pl.BlockSpec((L, W), rev), pl.BlockSpec((L, CONV_CH), rev),
                  pl.BlockSpec((L, W_SMALL), rev), pl.BlockSpec((L, W), rev), pl.BlockSpec((L, 1), rev),
                  pl.BlockSpec((1, N, W), lambda c: (nc - 1 - c, 0, 0)), p16, p16, p16, pl.BlockSpec((1, W), lambda c: (0, 0))],
        out_specs=[pl.BlockSpec((L, W), rev), pl.BlockSpec((L, CONV_CH), rev), pl.BlockSpec((L, H), rev),
                   p16, p16, p16, pl.BlockSpec((1, W), lambda c: (0, 0))],
        out_shape=[jax.ShapeDtypeStruct((S, W), _MXU), jax.ShapeDtypeStruct((S, CONV_CH), F32), jax.ShapeDtypeStruct((S, H), F32),
                   jax.ShapeDtypeStruct((1, H), F32), jax.ShapeDtypeStruct((1, H), F32), jax.ShapeDtypeStruct((1, H), F32),
                   jax.ShapeDtypeStruct((1, W), F32)],
        scratch_shapes=[pltpu.VMEM((N, W), F32), pltpu.VMEM((L, W), F32)],
        compiler_params=_cp(("arbitrary",)))(dmixed, proj, xa, proj_small, y, rs2, hs, dt_bias, a_log, d_skip, norm_w)


def _rope_tables(S):
    inv = 1.0 / (ROPE_THETA ** (jnp.arange(0, ROPE_DIM, 2, dtype=F32) / ROPE_DIM))
    ang = jnp.arange(S, dtype=F32)[:, None] * inv[None, :]
    cos, sin = jnp.cos(ang), jnp.sin(ang)
    half = ROPE_DIM // 2
    c64 = jnp.concatenate([cos, cos, jnp.ones((S, HD - ROPE_DIM), F32)], axis=1)
    s64 = jnp.concatenate([sin, sin, jnp.zeros((S, HD - ROPE_DIM), F32)], axis=1)
    del half
    return jnp.concatenate([c64, c64], axis=1), jnp.concatenate([s64, s64], axis=1)


def _rope(xs, blk0, width, cos, sin, sign, out_dtype, name, extra=None):
    S = xs[0].shape[0]
    tr = _pick(S, (512, 256, 128))
    nx = len(xs)

    def body(*refs):
        x_refs, c_ref, s_ref = refs[:nx], refs[nx], refs[nx + 1]
        e_ref = refs[nx + 2] if extra is not None else None
        o_ref = refs[-1]
        cv, sv = c_ref[...], s_ref[...] * sign
        lane = lax.broadcasted_iota(jnp.int32, (tr, 128), 1)
        first = (lane & (HD - 1)) < (ROPE_DIM // 2)
        for j in range(2):
            cs = slice(j * 128, (j + 1) * 128)
            xv = x_refs[0][:, cs].astype(F32)
            for r in x_refs[1:]:
                xv = xv + r[:, cs].astype(F32)
            rot = jnp.where(first, -pltpu.roll(xv, 128 - ROPE_DIM // 2, axis=1), pltpu.roll(xv, ROPE_DIM // 2, axis=1))
            out = xv * cv + rot * sv
            if extra is not None:
                out = out + e_ref[:, cs].astype(F32)
            o_ref[:, cs] = out.astype(out_dtype)

    t128 = pl.BlockSpec((tr, 128), lambda i, j: (i, 0))
    oblk = pl.BlockSpec((tr, 256), lambda i, j: (i, j))
    specs = [pl.BlockSpec((tr, 256), lambda i, j: (i, blk0 + j))] * nx + [t128, t128]
    ins = list(xs) + [cos, sin]
    if extra is not None:
        ins.append(extra[0])
        eb = extra[1]
        specs.append(pl.BlockSpec((tr, 256), lambda i, j: (i, eb + j)))
    return pl.pallas_call(
        body, name=name, grid=(S // tr, width // 256), in_specs=specs, out_specs=oblk,
        out_shape=jax.ShapeDtypeStruct((S, width), out_dtype), compiler_params=_cp(("parallel", "parallel")))(*ins)


def _rotate128(xv, cv, sv, first):
    rot = jnp.where(first, -pltpu.roll(xv, 128 - ROPE_DIM // 2, axis=1), pltpu.roll(xv, ROPE_DIM // 2, axis=1))
    return xv * cv + rot * sv


def _kv_prep(proj, cos, sin, tk):
    S = proj.shape[0]
    NB = S // SEL_BLOCK

    def body(ks_ref, vs_ref, kw_ref, vw_ref, c_ref, s_ref, *outs):
        cv, sv = c_ref[...], s_ref[...]
        lane = lax.broadcasted_iota(jnp.int32, (tk, 128), 1)
        first = (lane & (HD - 1)) < (ROPE_DIM // 2)
        key = pl.program_id(0) * tk + lax.broadcasted_iota(jnp.int32, (tk, NB), 0)
        onehot = (lax.shift_right_logical(key, 6) == lax.broadcasted_iota(jnp.int32, (tk, NB), 1)).astype(F32)
        for j, (ref, rotated) in enumerate(((ks_ref, True), (vs_ref, False), (kw_ref, True), (vw_ref, False))):
            nat, blk = outs[2 * j], outs[2 * j + 1]
            for half in range(2):
                xv = ref[:, half * 128:(half + 1) * 128]
                if rotated:
                    xv = _rotate128(xv, cv, sv, first)
                for e in range(2):
                    h = 2 * half + e
                    piece = xv[:, e * HD:(e + 1) * HD]
                    nat[h] = (jnp.concatenate([piece, onehot], axis=1) if j == 0 else piece).astype(nat.dtype)
                    blk[h, 0] = piece.T.astype(blk.dtype)

    col = lambda b: pl.BlockSpec((tk, 256), lambda i: (i, b))
    t128 = pl.BlockSpec((tk, 128), lambda i: (i, 0))
    nat_spec = lambda w: pl.BlockSpec((N_KV, tk, w), lambda i: (0, i, 0))
    blk_spec = pl.BlockSpec((N_KV, 1, HD, tk), lambda i: (0, i, 0, 0))
    nat_shape = lambda w: jax.ShapeDtypeStruct((N_KV, S, w), _MXU)
    blk_shape = jax.ShapeDtypeStruct((N_KV, S // tk, HD, tk), _MXU)
    widths = (HD + NB, HD, HD, HD)
    res = pl.pallas_call(
        body, name="kv_prep", grid=(S // tk,), in_specs=[col(KSB), col(VSB), col(KWB), col(VWB), t128, t128],
        out_specs=[s for w in widths for s in (nat_spec(w), blk_spec)],
        out_shape=[s for w in widths for s in (nat_shape(w), blk_shape)],
        compiler_params=_cp(("parallel",)))(proj, proj, proj, proj, cos, sin)
    return dict(ks_ext=res[0], ks_t=res[1], vs=res[2], vs_t=res[3], kw=res[4], kw_t=res[5], vw=res[6], vw_t=res[7])


def _dkv_post(dks, dvs, dkw, dvw, cos, sin):
    S = dks.shape[1]
    tr = _pick(S, (512, 256, 128))

    def body(dks_ref, dvs_ref, dkw_ref, dvw_ref, c_ref, s_ref, o_ref):
        cv, sv = c_ref[...], -s_ref[...]
        lane = lax.broadcasted_iota(jnp.int32, (tr, 128), 1)
        first = (lane & (HD - 1)) < (ROPE_DIM // 2)
        for j, (ref, rotated) in enumerate(((dks_ref, True), (dvs_ref, False), (dkw_ref, True), (dvw_ref, False))):
            for half in range(2):
                xv = jnp.concatenate([ref[2 * half], ref[2 * half + 1]], axis=1)
                if rotated:
                    xv = _rotate128(xv, cv, sv, first)
                o_ref[:, j * 256 + half * 128:j * 256 + (half + 1) * 128] = xv.astype(o_ref.dtype)

    hm = pl.BlockSpec((N_KV, tr, HD), lambda i: (0, i, 0))
    t128 = pl.BlockSpec((tr, 128), lambda i: (i, 0))
    return pl.pallas_call(
        body, name="dkv_post", grid=(S // tr,), in_specs=[hm, hm, hm, hm, t128, t128],
        out_specs=pl.BlockSpec((tr, 4 * 256), lambda i: (i, 0)), out_shape=jax.ShapeDtypeStruct((S, 4 * 256), _MXU),
        compiler_params=_cp(("parallel",)))(dks, dvs, dkw, dvw, cos, sin)


def _compress_fwd(R, pe, w1, w2):
    NC = R.shape[1]
    half = 16 * HD

    def body(r_ref, pe_ref, w1_ref, w2_ref, o_ref, hid_ref):
        r = r_ref[0]
        a = _dot(r + pe_ref[:, 0:half], w1_ref[0:half, :], "nn")
        b = _dot(r + pe_ref[:, half:2 * half], w1_ref[half:2 * half, :], "nn")
        hid = a + pltpu.roll(b, NC - 1, axis=0)
        hid_ref[0] = hid
        out = _dot(hid * _sigmoid(hid), w2_ref[...], "nn")
        rows = lax.broadcasted_iota(jnp.int32, out.shape, 0)
        o_ref[0] = jnp.where(rows < NC - 1, out, 0.0).astype(o_ref.dtype)

    return pl.pallas_call(
        body, name="compress_fwd", grid=(N_KV,),
        in_specs=[pl.BlockSpec((1, NC, half), lambda h: (h, 0, 0)), pl.BlockSpec((1, 2 * half), lambda h: (0, 0)),
                  pl.BlockSpec((2 * half, CMP_HID), lambda h: (0, 0)), pl.BlockSpec((CMP_HID, HD), lambda h: (0, 0))],
        out_specs=[pl.BlockSpec((1, NC, HD), lambda h: (h, 0, 0)), pl.BlockSpec((1, NC, CMP_HID), lambda h: (h, 0, 0))],
        out_shape=[jax.ShapeDtypeStruct((N_KV, NC, HD), _MXU), jax.ShapeDtypeStruct((N_KV, NC, CMP_HID), F32)],
        compiler_params=_cp(("parallel",)))(R, pe, w1, w2)


def _compress_bwd(R, pe, w1, w2, hid, dout):
    NC = R.shape[1]
    half = 16 * HD

    def body(r_ref, pe_ref, w1_ref, w2_ref, hid_ref, do_ref, dr_ref, dw1_ref, dw2_ref, dpe_ref):
        @pl.when(pl.program_id(0) == 0)
        def _():
            dw1_ref[...] = jnp.zeros_like(dw1_ref)
            dw2_ref[...] = jnp.zeros_like(dw2_ref)
            dpe_ref[...] = jnp.zeros_like(dpe_ref)

        r, hv, do = r_ref[0], hid_ref[0], do_ref[0]
        s = _sigmoid(hv)
        dw2_ref[...] += _dot(hv * s, do, "tn")
        dhid = _dot(do, w2_ref[...], "nt") * (s * (1.0 + hv * (1.0 - s)))
        rows = lax.broadcasted_iota(jnp.int32, dhid.shape, 0)
        dhid = jnp.where(rows < NC - 1, dhid, 0.0)
        dhid_dn = pltpu.roll(dhid, 1, axis=0)
        dw1_ref[0:half, :] += _dot(r + pe_ref[:, 0:half], dhid, "tn")
        dw1_ref[half:2 * half, :] += _dot(r + pe_ref[:, half:2 * half], dhid_dn, "tn")
        dxt = _dot(dhid, w1_ref[0:half, :], "nt")
        dxb = _dot(dhid_dn, w1_ref[half:2 * half, :], "nt")
        dr_ref[0] = dxt + dxb
        dpe_ref[:, 0:half] += jnp.sum(dxt, axis=0, keepdims=True)
        dpe_ref[:, half:2 * half] += jnp.sum(dxb, axis=0, keepdims=True)

    return pl.pallas_call(
        body, name="compress_bwd", grid=(N_KV,),
        in_specs=[pl.BlockSpec((1, NC, half), lambda h: (h, 0, 0)), pl.BlockSpec((1, 2 * half), lambda h: (0, 0)),
                  pl.BlockSpec((2 * half, CMP_HID), lambda h: (0, 0)), pl.BlockSpec((CMP_HID, HD), lambda h: (0, 0)),
                  pl.BlockSpec((1, NC, CMP_HID), lambda h: (h, 0, 0)), pl.BlockSpec((1, NC, HD), lambda h: (h, 0, 0))],
        out_specs=[pl.BlockSpec((1, NC, half), lambda h: (h, 0, 0)), pl.BlockSpec((2 * half, CMP_HID), lambda h: (0, 0)),
                   pl.BlockSpec((CMP_HID, HD), lambda h: (0, 0)), pl.BlockSpec((1, 2 * half), lambda h: (0, 0))],
        out_shape=[jax.ShapeDtypeStruct((N_KV, NC, half), F32), jax.ShapeDtypeStruct((2 * half, CMP_HID), F32),
                   jax.ShapeDtypeStruct((CMP_HID, HD), F32), jax.ShapeDtypeStruct((1, 2 * half), F32)],
        compiler_params=_cp(("arbitrary",)))(R, pe, w1, w2, hid, dout)


def _attn_cfg(S, Sk, mode):
    tq = _pick(S, (256, 128))
    tk = Sk if mode == "cmp" else _pick(Sk, (256, 128))
    return tq, tk


def _block_start(kb, tk):
    return kb * tk if isinstance(kb, int) else pl.multiple_of(kb * tk, tk)


def _pipelined_key_blocks(mode, q0, tq, tk, produce, consume):
    if mode == "cmp":
        produce(0, True, 0)
        consume(0, 0)
        return
    if mode == "sel":
        first, n_plain, plain_masked = 0, q0 // tk, False
    else:
        first = jnp.maximum(q0 - (WINDOW - 1), 0) // tk
        n_plain, plain_masked = (q0 + tq - 1) // tk - first, True
    last = first + n_plain
    pairs = jnp.maximum(n_plain - 1, 0) // 2

    @pl.when(n_plain >= 1)
    def _():
        produce(first, plain_masked, 0)

    def two(j, carry):
        kb = first + 2 * j
        produce(kb + 1, plain_masked, 1)
        consume(kb, 0)
        produce(kb + 2, plain_masked, 0)
        consume(kb + 1, 1)
        return carry

    lax.fori_loop(0, pairs, two, 0)
    kb = first + 2 * pairs
    left = n_plain - 2 * pairs

    @pl.when(left == 2)
    def _():
        produce(kb + 1, plain_masked, 1)
        consume(kb, 0)
        produce(last, True, 0)
        consume(kb + 1, 1)
        consume(last, 0)

    @pl.when(left == 1)
    def _():
        produce(last, True, 1)
        consume(kb, 0)
        consume(last, 1)

    @pl.when(left == 0)
    def _():
        produce(last, True, 0)
        consume(last, 0)


def _attn_bias(mode, q0, k0, tq, tk):
    k = k0 + lax.broadcasted_iota(jnp.int32, (tk, tq), 0)
    t = q0 + lax.broadcasted_iota(jnp.int32, (tk, tq), 1)
    if mode == "cmp":
        ok = (k * 16 + 31) <= t
    elif mode == "win":
        ok = (k <= t) & ((t - k) < WINDOW)
    else:
        ok = k <= t
    bias = jnp.where(ok, 0.0, NEG)
    return jnp.concatenate([bias] * GRP, axis=1), jnp.concatenate([ok.astype(F32)] * GRP, axis=1)


def _sel_operands(qs, selneg_ref):
    return jnp.concatenate([qs, jnp.concatenate([selneg_ref[0]] * GRP, axis=0)], axis=1)


def _stack_heads(ref, tq):
    return jnp.concatenate([ref[:, g * HD:(g + 1) * HD] for g in range(GRP)], axis=0)


def _scaled_queries(q_ref, tq):
    return (_stack_heads(q_ref, tq).astype(F32) * SCALE).astype(_MXU)


def _blocked_t(x, tk):
    n, Sk, d = x.shape
    return x.reshape(n, Sk // tk, tk, d).transpose(0, 1, 3, 2)


def _head_rows(ref):
    return jnp.concatenate([ref[0, g:g + 1, :] for g in range(GRP)], axis=1)


def _attn_fwd(q, qcol0, k, vt, mode, selneg, gate, y_prev, y_dtype, name):
    S, Sk = q.shape[0], k.shape[1]
    tq, tk = _attn_cfg(S, Sk, mode)
    R = GRP * tq

    def body(*refs):
        q_ref, k_ref, vt_ref = refs[:3]
        rest = list(refs[3:])
        sel_ref = rest.pop(0) if mode == "sel" else None
        gate_ref = rest.pop(0)
        yp_ref = rest.pop(0) if y_prev is not None else None
        o_ref, lse_ref, y_ref, m_scr, l_scr, acc, s_scr = rest
        q0 = pl.program_id(1) * tq
        qs = _scaled_queries(q_ref, tq)
        m_scr[...] = jnp.full_like(m_scr, NEG)
        l_scr[...] = jnp.zeros_like(l_scr)
        acc[...] = jnp.zeros_like(acc)
        qk = _sel_operands(qs, sel_ref) if mode == "sel" else qs

        def produce(kb, masked, slot):
            k0 = _block_start(kb, tk)
            s = _dot(k_ref[0, pl.ds(k0, tk), :], qk, "nt")
            if masked:
                s = s + _attn_bias(mode, q0, k0, tq, tk)[0]
            s_scr[slot] = s

        def consume(kb, slot):
            s = s_scr[slot]
            m_old = m_scr[...]
            m_new = jnp.maximum(m_old, jnp.max(s, axis=0, keepdims=True))
            p = jnp.exp(s - m_new)
            if mode == "cmp":
                p = p * _attn_bias(mode, q0, 0, tq, tk)[1]
            alpha = jnp.exp(m_old - m_new)
            l_scr[...] = alpha * l_scr[...] + jnp.sum(p, axis=0, keepdims=True)
            acc[...] = alpha * acc[...] + _dot(vt_ref[0, kb], p, "nn")
            m_scr[...] = m_new

        _pipelined_key_blocks(mode, q0, tq, tk, produce, consume)
        l = l_scr[...]
        good = l > 0.0
        o_t = acc[...] * jnp.where(good, 1.0 / jnp.where(good, l, 1.0), 0.0)
        lse = jnp.where(good, m_scr[...] + jnp.log(jnp.where(good, l, 1.0)), -NEG)
        y_t = o_t * _sigmoid(_head_rows(gate_ref))
        for g in range(GRP):
            hs, qs_ = slice(g * HD, (g + 1) * HD), slice(g * tq, (g + 1) * tq)
            o_ref[:, hs] = o_t[:, qs_].T
            lse_ref[0, g:g + 1, :] = lse[:, qs_]
            yg = y_t[:, qs_].T
            if y_prev is not None:
                yg = yg + yp_ref[:, hs]
            y_ref[:, hs] = yg.astype(y_ref.dtype)

    row_spec = pl.BlockSpec((1, GRP, tq), lambda h, i: (h, 0, i))
    qo_spec = pl.BlockSpec((tq, GRP * HD), lambda h, i: (i, h))
    ins = [q, k, vt]
    specs = [pl.BlockSpec((tq, GRP * HD), lambda h, i: (i, qcol0 + h)), pl.BlockSpec((1, Sk, k.shape[2]), lambda h, i: (h, 0, 0)),
             pl.BlockSpec((1, Sk // tk, HD, tk), lambda h, i: (h, 0, 0, 0))]
    if mode == "sel":
        assert tq == tk
        ins.append(selneg)
        specs.append(pl.BlockSpec((1, tq, selneg.shape[2]), lambda h, i: (h, i, 0)))
    ins.append(gate)
    specs.append(row_spec)
    if y_prev is not None:
        ins.append(y_prev)
        specs.append(qo_spec)
    return pl.pallas_call(
        body, name=name, grid=(N_KV, S // tq), in_specs=specs, out_specs=[qo_spec, row_spec, qo_spec],
        out_shape=[jax.ShapeDtypeStruct((S, ATT_WIDTH), F32), jax.ShapeDtypeStruct((N_KV, GRP, S), F32),
                   jax.ShapeDtypeStruct((S, ATT_WIDTH), y_dtype)],
        scratch_shapes=[pltpu.VMEM((1, R), F32), pltpu.VMEM((1, R), F32), pltpu.VMEM((HD, R), F32), pltpu.VMEM((2, tk, R), F32)],
        compiler_params=_cp(("parallel", "arbitrary")))(*ins)


def _attn_bwd(q, qcol0, k, kt, v, o, lse, dy, dycol0, gate, mode, selneg, name):
    S, Sk = q.shape[0], k.shape[1]
    tq, tk = _attn_cfg(S, Sk, mode)
    R = GRP * tq

    def body(*refs):
        if mode == "sel":
            (q_ref, k_ref, kt_ref, v_ref, o_ref, lse_ref, dy_ref, gate_ref, sel_ref, dq_ref, dk_ref, dv_ref, dg_ref, dq_scr, s_scr,
             dp_scr) = refs
        else:
            q_ref, k_ref, kt_ref, v_ref, o_ref, lse_ref, dy_ref, gate_ref, dq_ref, dk_ref, dv_ref, dg_ref, dq_scr, s_scr, dp_scr = refs

        @pl.when(pl.program_id(1) == 0)
        def _():
            dk_ref[...] = jnp.zeros_like(dk_ref)
            dv_ref[...] = jnp.zeros_like(dv_ref)

        q0 = pl.program_id(1) * tq
        qs = _scaled_queries(q_ref, tq)
        dys = _stack_heads(dy_ref, tq)
        gv = _sigmoid(_head_rows(gate_ref))
        dy_o = _dot(jnp.ones((8, HD), F32), dys * _stack_heads(o_ref, tq), "nt", split="b")[0:1, :]
        delta = gv * dy_o
        dgate = dy_o * (gv * (1.0 - gv))
        for g in range(GRP):
            dg_ref[0, g:g + 1, :] = dgate[:, g * tq:(g + 1) * tq]
        lsev = _head_rows(lse_ref)
        dos = (dys * jnp.broadcast_to(gv, (8, R)).T[:, 0:1]).astype(_MXU)
        dq_scr[...] = jnp.zeros_like(dq_scr)
        qk = _sel_operands(qs, sel_ref) if mode == "sel" else qs

        def produce(kb, masked, slot):
            k0 = _block_start(kb, tk)
            s = _dot(k_ref[0, pl.ds(k0, tk), :], qk, "nt")
            if masked:
                s = s + _attn_bias(mode, q0, k0, tq, tk)[0]
            s_scr[slot] = s
            dp_scr[slot] = _dot(v_ref[0, pl.ds(k0, tk), :], dos, "nt")

        def consume(kb, slot):
            k0 = _block_start(kb, tk)
            p = jnp.exp(s_scr[slot] - lsev)
            if mode == "cmp":
                p = p * _attn_bias(mode, q0, 0, tq, tk)[1]
            ds = p * (dp_scr[slot] - delta)
            dq_scr[...] += _dot(kt_ref[0, kb], ds, "nn")
            dk_ref[0, pl.ds(k0, tk), :] += _dot(ds, qs, "nn")
            dv_ref[0, pl.ds(k0, tk), :] += _dot(p, dos, "nn")

        _pipelined_key_blocks(mode, q0, tq, tk, produce, consume)
        for g in range(GRP):
            dq_ref[:, g * HD:(g + 1) * HD] = (dq_scr[:, g * tq:(g + 1) * tq] * SCALE).T

    kv_spec = pl.BlockSpec((1, Sk, HD), lambda h, i: (h, 0, 0))
    qo_spec = pl.BlockSpec((tq, GRP * HD), lambda h, i: (i, h))
    row_spec = pl.BlockSpec((1, GRP, tq), lambda h, i: (h, 0, i))
    ins = [q, k, kt, v, o, lse, dy, gate]
    specs = [pl.BlockSpec((tq, GRP * HD), lambda h, i: (i, qcol0 + h)), pl.BlockSpec((1, Sk, k.shape[2]), lambda h, i: (h, 0, 0)),
             pl.BlockSpec((1, Sk // tk, HD, tk), lambda h, i: (h, 0, 0, 0)), kv_spec, qo_spec, row_spec,
             pl.BlockSpec((tq, GRP * HD), lambda h, i: (i, dycol0 + h)), row_spec]
    if mode == "sel":
        assert tq == tk
        ins.append(selneg)
        specs.append(pl.BlockSpec((1, tq, selneg.shape[2]), lambda h, i: (h, i, 0)))
    return pl.pallas_call(
        body, name=name, grid=(N_KV, S // tq), in_specs=specs, out_specs=[qo_spec, kv_spec, kv_spec, row_spec],
        out_shape=[jax.ShapeDtypeStruct((S, ATT_WIDTH), F32), jax.ShapeDtypeStruct((N_KV, Sk, HD), F32),
                   jax.ShapeDtypeStruct((N_KV, Sk, HD), F32), jax.ShapeDtypeStruct((N_KV, GRP, S), F32)],
        scratch_shapes=[pltpu.VMEM((HD, R), F32), pltpu.VMEM((2, tk, R), F32), pltpu.VMEM((2, tk, R), F32)],
        compiler_params=_cp(("parallel", "arbitrary")))(*ins)


def _select(q, qcol0, k_cmp, lse):
    S, NC = q.shape[0], k_cmp.shape[1]
    NB = S // SEL_BLOCK
    tq = _pick(S, (256, 128))
    ci = np.arange(NC)[None, :] * 16
    sj = np.arange(NB)[:, None] * SEL_BLOCK
    ov_t = np.clip(np.minimum(ci + 32, sj + SEL_BLOCK) - np.maximum(ci, sj), 0, None) / 32.0
    ov_t[:, NC - 1] = 0.0
    ov_t = jnp.asarray(ov_t, F32)

    def body(q_ref, k_ref, lse_ref, ov_ref, sel_ref):
        q0 = pl.program_id(1) * tq
        bias, okf = _attn_bias("cmp", q0, 0, tq, NC)
        lsev = _head_rows(lse_ref)
        p = jnp.exp(_dot(k_ref[0], _scaled_queries(q_ref, tq), "nt") + bias - lsev) * okf
        imp4 = _dot(ov_ref[...], p, "nn")
        imp = imp4[:, 0:tq] + imp4[:, tq:2 * tq] + imp4[:, 2 * tq:3 * tq] + imp4[:, 3 * tq:4 * tq]
        blk = lax.broadcasted_iota(jnp.int32, (NB, tq), 0)
        cur = lax.shift_right_logical(q0 + lax.broadcasted_iota(jnp.int32, (NB, tq), 1), 6)
        imp = jnp.where((blk == 0) | (blk == cur) | (blk == cur - 1), FORCE, imp)
        imp = jnp.where(blk <= cur, imp, -1.0)
        rank = jnp.zeros((NB, tq), F32)
        for j in range(NB):
            row = imp[j:j + 1, :]
            ahead = (row > imp) | ((row == imp) & (blk > j))
            rank = rank + ahead.astype(F32)
        chosen = (rank < float(N_SELECT)) & (imp >= 0.0)
        sel_ref[0] = jnp.where(chosen, 0.0, NEG).T.astype(sel_ref.dtype)

    return pl.pallas_call(
        body, name="select_blocks", grid=(N_KV, S // tq),
        in_specs=[pl.BlockSpec((tq, GRP * HD), lambda h, i: (i, qcol0 + h)), pl.BlockSpec((1, NC, HD), lambda h, i: (h, 0, 0)),
                  pl.BlockSpec((1, GRP, tq), lambda h, i: (h, 0, i)), pl.BlockSpec((NB, NC), lambda h, i: (0, 0))],
        out_specs=pl.BlockSpec((1, tq, NB), lambda h, i: (h, i, 0)),
        out_shape=jax.ShapeDtypeStruct((N_KV, S, NB), _MXU), compiler_params=_cp(("parallel", "parallel")))(q, k_cmp, lse, ov_t)


def _to_rows16(x):
    S = x.shape[0]
    return x.reshape(S // 16, 16, N_KV, HD).transpose(2, 0, 1, 3).reshape(N_KV, S // 16, 16 * HD)


def _from_rows16(r):
    NC = r.shape[1]
    return r.reshape(N_KV, NC, 16, HD).transpose(1, 2, 0, 3).reshape(NC * 16, N_KV * HD)


DT_COL0 = SSD_WIDTH + CONV_CH
GATE_IN_COL0 = D_IN - 3 * N_HEADS


SHARD_IN = D_IN // N_DEV


def _orig_cols(ref, c0, width):
    pieces, c = [], c0
    while c < c0 + width:
        d, off = divmod(c, SHARD_IN)
        w = min(SHARD_IN - off, c0 + width - c)
        pieces.append(ref[d, :, off:off + w])
        c += w
    return pieces[0] if len(pieces) == 1 else jnp.concatenate(pieces, axis=1)


def _cols_from_slabs(slabs):
    _, R, c = slabs.shape
    tr = _pick(R, (256, 128))

    def body(s_ref, o_ref):
        for t in range(N_DEV * c // LANE):
            pieces, col = [], t * LANE
            while col < (t + 1) * LANE:
                d, off = divmod(col, c)
                w = min(c - off, (t + 1) * LANE - col)
                pieces.append(s_ref[d, :, off:off + w])
                col += w
            o_ref[:, t * LANE:(t + 1) * LANE] = pieces[0] if len(pieces) == 1 else jnp.concatenate(pieces, axis=1)

    return pl.pallas_call(
        body, name="cols_from_slabs", grid=(R // tr,), in_specs=[pl.BlockSpec((N_DEV, tr, c), lambda i: (0, i, 0))],
        out_specs=pl.BlockSpec((tr, N_DEV * c), lambda i: (i, 0)), out_shape=jax.ShapeDtypeStruct((R, N_DEV * c), slabs.dtype),
        compiler_params=_cp(("parallel",)))(slabs)


def _slabs_from_cols(x):
    R, c = x.shape[0], x.shape[1] // N_DEV
    tr = _pick(R, (256, 128))

    def body(x_ref, o_ref):
        for d in range(N_DEV):
            o_ref[d] = x_ref[:, d * c:(d + 1) * c]

    return pl.pallas_call(
        body, name="slabs_from_cols", grid=(R // tr,), in_specs=[pl.BlockSpec((tr, N_DEV * c), lambda i: (i, 0))],
        out_specs=pl.BlockSpec((N_DEV, tr, c), lambda i: (0, i, 0)), out_shape=jax.ShapeDtypeStruct((N_DEV, R, c), x.dtype),
        compiler_params=_cp(("parallel",)))(x)


def _w_in_from_slabs(slabs):
    D = slabs.shape[1]
    tr = _pick(D, (256, 128))

    def body(s_ref, main_ref, small_ref):
        for t in range(W_MAIN // LANE):
            c = t * LANE
            main_ref[:, c:c + LANE] = _orig_cols(s_ref, c if c < DT_COL0 else c + SSD_HEADS, LANE)
        small_ref[...] = jnp.concatenate(
            [_orig_cols(s_ref, DT_COL0, SSD_HEADS), _orig_cols(s_ref, GATE_IN_COL0, 3 * N_HEADS),
             jnp.zeros((tr, W_SMALL - SSD_HEADS - 3 * N_HEADS), small_ref.dtype)], axis=1)

    return pl.pallas_call(
        body, name="w_in_layout", grid=(D // tr,), in_specs=[pl.BlockSpec((N_DEV, tr, SHARD_IN), lambda i: (0, i, 0))],
        out_specs=[pl.BlockSpec((tr, W_MAIN), lambda i: (i, 0)), pl.BlockSpec((tr, W_SMALL), lambda i: (i, 0))],
        out_shape=[jax.ShapeDtypeStruct((D, W_MAIN), slabs.dtype), jax.ShapeDtypeStruct((D, W_SMALL), slabs.dtype)],
        compiler_params=_cp(("parallel",)))(slabs)


def _w_in_to_slabs(main, small):
    D = main.shape[0]
    tr = _pick(D, (256, 128))
    ranges = [(0, DT_COL0, 0, 0), (DT_COL0, DT_COL0 + SSD_HEADS, 1, 0), (DT_COL0 + SSD_HEADS, GATE_IN_COL0, 0, DT_COL0),
              (GATE_IN_COL0, D_IN, 1, SSD_HEADS)]

    def body(main_ref, small_ref, o_ref):
        srcs = (main_ref, small_ref)
        for d in range(N_DEV):
            lo, hi = d * SHARD_IN, (d + 1) * SHARD_IN
            pieces = []
            for start, stop, which, s0 in ranges:
                a, b = max(lo, start), min(hi, stop)
                if a < b:
                    pieces.append(srcs[which][:, s0 + a - start:s0 + b - start].astype(o_ref.dtype))
            o_ref[d] = pieces[0] if len(pieces) == 1 else jnp.concatenate(pieces, axis=1)

    return pl.pallas_call(
        body, name="w_in_grad_layout", grid=(D // tr,),
        in_specs=[pl.BlockSpec((tr, W_MAIN), lambda i: (i, 0)), pl.BlockSpec((tr, W_SMALL), lambda i: (i, 0))],
        out_specs=pl.BlockSpec((N_DEV, tr, SHARD_IN), lambda i: (0, i, 0)),
        out_shape=jax.ShapeDtypeStruct((N_DEV, D, SHARD_IN), main.dtype), compiler_params=_cp(("parallel",)))(main, small)


QB, KCB, VCB, KSB, VSB, KWB, VWB = 10, 14, 15, 16, 17, 18, 19


def _col256(a, b):
    return a[:, b * 256:(b + 1) * 256]


_EARLY = ["w_in", "cmp_w1_k", "cmp_w1_v"]
_LATE = ["w_out", "w_gate", "w_up", "w_down"]
_FFN = ["w_down", "w_gate", "w_up"]
_MID = ["w_out"]
_LAST = ["cmp_w1_k", "cmp_w1_v", "w_in"]


def _local_step(x, tgt, p, late_weights=None, grads_ready=None):
    S = x.shape[0]
    cos, sin = _rope_tables(S)

    u, rs1 = _rms_fwd(x, p["attn_norm_w"], "attn_norm")
    proj = _mm(u, p["w_main"], "nn", F32, "in_proj")
    proj_small = _mm(u, p["w_small"], "nn", F32, "in_proj_small")
    xa = _conv_fwd(proj, p["conv_w"], p["conv_b"])
    y_ssd, y_pre, rs_ssd, hs = _ssd_fwd(proj, proj_small, xa, p["dt_bias"], p["a_log"], p["d_skip"], p["ssd_norm_w"])

    q_rot = _rope([proj], QB, ATT_WIDTH, cos, sin, 1.0, _MXU, "rope_q")
    kv = _kv_prep(proj, cos, sin, _attn_cfg(S, S, "sel")[1])
    rk, rv = _to_rows16(_col256(proj, KCB)), _to_rows16(_col256(proj, VCB))
    k_cmp, hid_k = _compress_fwd(rk, p["cmp_pe_k"], p["cmp_w1_k"], p["cmp_w2_k"])
    v_cmp, hid_v = _compress_fwd(rv, p["cmp_pe_v"], p["cmp_w1_v"], p["cmp_w2_v"])
    n_cmp = k_cmp.shape[1]

    gates = proj_small[:, SSD_HEADS:SSD_HEADS + 3 * N_HEADS].reshape(S, N_KV, GRP, 3).transpose(3, 1, 2, 0)
    o_cmp, lse_cmp, y_att = _attn_fwd(proj, QB, k_cmp, _blocked_t(v_cmp, n_cmp), "cmp", None, gates[0], None, F32, "attn_cmp_fwd")
    sel = _select(proj, QB, k_cmp, lse_cmp)
    o_sel, lse_sel, y_att = _attn_fwd(q_rot, 0, kv["ks_ext"], kv["vs_t"], "sel", sel, gates[1], y_att, F32, "attn_sel_fwd")
    o_win, lse_win, y_att = _attn_fwd(q_rot, 0, kv["kw"], kv["vw_t"], "win", None, gates[2], y_att, _MXU, "attn_win_fwd")

    if late_weights is not None:
        p = {**p, **late_weights(y_att)}
    mixed = jnp.concatenate([y_ssd, y_att], axis=1)
    h1 = _mm(mixed, p["w_out"], "nn", F32, "out_proj", res=x)
    v, rs_ffn = _rms_fwd(h1, p["ffn_norm_w"], "ffn_norm")
    gt, up, act = _ffn_up(v, p["w_gate"], p["w_up"])
    h2 = _mm(act, p["w_down"], "nn", F32, "ffn_down", res=h1)
    loss, dh2, dh2b, d_final_w = _final_loss(h2, p["final_norm_w"], tgt)

    def ready(names):
        return None if grads_ready is None else grads_ready(names, g)

    g = {"final_norm_w": d_final_w}
    g["w_down"] = _mm(act, dh2b, "tn", _MXU, "dw_down")
    dgt, dup = _ffn_dact(dh2b, p["w_down"], gt, up)
    g["w_gate"] = _mm(v, dgt, "tn", _MXU, "dw_gate")
    g["w_up"] = _mm(v, dup, "tn", _MXU, "dw_up")
    dv = _ffn_dv(dgt, dup, p["w_gate"], p["w_up"], ready(_FFN))
    dh1, dh1b, g["ffn_norm_w"] = _rms_bwd(dv, h1, rs_ffn, p["ffn_norm_w"], dh2, "ffn_norm_bwd")
    g["w_out"] = _mm(mixed, dh1b, "tn", _MXU, "dw_out")
    dmixed = _mm(dh1b, p["w_out"], "nt", F32, "dmixed", after=ready(_MID))

    dz, dxa, ddtr, g["dt_bias"], g["a_log"], g["d_skip"], g["ssd_norm_w"] = _ssd_bwd(
        dmixed, proj, proj_small, xa, y_pre, rs_ssd, hs, p["dt_bias"], p["a_log"], p["d_skip"], p["ssd_norm_w"])
    dxbc, g["conv_w"], g["conv_b"] = _conv_bwd(proj, p["conv_w"], p["conv_b"], dxa)

    dyb = SSD_WIDTH // (GRP * HD)
    dq_cmp, dk_cmp, dv_cmp, dg_cmp = _attn_bwd(proj, QB, k_cmp, _blocked_t(k_cmp, n_cmp), v_cmp, o_cmp, lse_cmp, dmixed, dyb,
                                               gates[0], "cmp", None, "attn_cmp_bwd")
    dq_sel, dks, dvs, dg_sel = _attn_bwd(q_rot, 0, kv["ks_ext"], kv["ks_t"], kv["vs"], o_sel, lse_sel, dmixed, dyb, gates[1], "sel",
                                         sel, "attn_sel_bwd")
    dq_win, dkw, dvw, dg_win = _attn_bwd(q_rot, 0, kv["kw"], kv["kw_t"], kv["vw"], o_win, lse_win, dmixed, dyb, gates[2], "win", None,
                                         "attn_win_bwd")
    dgate = jnp.stack([dg_cmp, dg_sel, dg_win]).transpose(3, 1, 2, 0).reshape(S, 3 * N_HEADS)
    drk, g["cmp_w1_k"], g["cmp_w2_k"], g["cmp_pe_k"] = _compress_bwd(rk, p["cmp_pe_k"], p["cmp_w1_k"], p["cmp_w2_k"], hid_k, dk_cmp)
    drv, g["cmp_w1_v"], g["cmp_w2_v"], g["cmp_pe_v"] = _compress_bwd(rv, p["cmp_pe_v"], p["cmp_w1_v"], p["cmp_w2_v"], hid_v, dv_cmp)
    dq = _rope([dq_sel, dq_win], 0, ATT_WIDTH, cos, sin, -1.0, _MXU, "rope_dq", extra=(dq_cmp, 0))
    dkv = _dkv_post(dks, dvs, dkw, dvw, cos, sin)
    dproj = jnp.concatenate([dz, dxbc, dq] + [t.astype(_MXU) for t in (_from_rows16(drk), _from_rows16(drv))] + [dkv], axis=1)
    dsmall = jnp.concatenate([ddtr, dgate, jnp.zeros((S, W_SMALL - SSD_HEADS - 3 * N_HEADS), F32)], axis=1).astype(_MXU)
    g["w_main"] = _mm(u, dproj, "tn", _MXU, "dw_in")
    g["w_small"] = _mm(u, dsmall, "tn", F32, "dw_in_small")
    du = _mm(dproj, p["w_main"], "nt", F32, "du_main", after=ready(_LAST))
    du = _mm(dsmall, p["w_small"], "nt", F32, "du_small", res=du)
    grad_x, _, g["attn_norm_w"] = _rms_bwd(du, x, rs1, p["attn_norm_w"], dh1, "attn_norm_bwd")
    return loss, grad_x, g


MESH_ID = pl.DeviceIdType.MESH


def _my_coords():
    return lax.axis_index("x"), lax.axis_index("y"), lax.axis_index("c")


def _flat_id(px, py, pc):
    return 4 * px + 2 * py + pc


def _peer(k):
    mx, my, mc = _my_coords()
    return (1 - mx if k & 4 else mx, 1 - my if k & 2 else my, 1 - mc if k & 1 else mc)


def _exchange(arrs, scatter, name, after=()):
    n, na = len(arrs), len(after)
    scatter = [scatter] * n if isinstance(scatter, bool) else list(scatter)

    def body(*refs):
        ins, outs = refs[:n], refs[n + na:2 * n + na]
        send_sems, recv_sems, local_sems = refs[2 * n + na:]
        me = _flat_id(*_my_coords())
        copies = []
        for i in range(n):
            src_me = ins[i].at[me] if scatter[i] else ins[i]
            local = pltpu.make_async_copy(src_me, outs[i].at[me], local_sems.at[i])
            local.start()
            copies.append(local)
        for k in range(1, N_DEV):
            peer = _peer(k)
            for i in range(n):
                src = ins[i].at[_flat_id(*peer)] if scatter[i] else ins[i]
                cp = pltpu.make_async_remote_copy(src_ref=src, dst_ref=outs[i].at[me], send_sem=send_sems.at[i * 7 + k - 1],
                                                  recv_sem=recv_sems.at[i * 7 + k - 1], device_id=peer, device_id_type=MESH_ID)
                cp.start()
                copies.append(cp)
        for cp in copies:
            cp.wait()

    any_spec = pl.BlockSpec(memory_space=pl.ANY)
    out_shape = [jax.ShapeDtypeStruct(a.shape if sc else (N_DEV,) + a.shape, a.dtype) for a, sc in zip(arrs, scatter)]
    return pl.pallas_call(
        body, name=name, in_specs=[any_spec] * (n + na), out_specs=[any_spec] * n, out_shape=out_shape,
        scratch_shapes=[pltpu.SemaphoreType.DMA((n * 7,)), pltpu.SemaphoreType.DMA((n * 7,)), pltpu.SemaphoreType.DMA((n,))],
        compiler_params=pltpu.CompilerParams(has_side_effects=True))(*arrs, *after)


def _gather_two_level(arrs, name):
    n = len(arrs)

    def body(*refs):
        ins, outs = refs[:n], refs[n:2 * n]
        send_sems, recv_sems, local_sems = refs[2 * n:]
        x, y, c = _my_coords()
        me, sibling = (x, y, c), (x, y, 1 - c)
        chips = [(1 - x, y), (x, 1 - y), (1 - x, 1 - y)]

        def copy(i, k, block, to, src=None):
            slot = outs[i].at[_flat_id(*block)]
            return pltpu.make_async_remote_copy(src_ref=slot if src is None else src, dst_ref=slot, send_sem=send_sems.at[i * 7 + k],
                                                recv_sem=recv_sems.at[i * 7 + k], device_id=to, device_id_type=MESH_ID)

        mine = [pltpu.make_async_copy(ins[i], outs[i].at[_flat_id(*me)], local_sems.at[i]) for i in range(n)]
        for cp in mine:
            cp.start()
        first = []
        for j, chip in enumerate(chips):
            first += [copy(i, 1 + j, me, (*chip, c), src=ins[i]) for i in range(n)]
        first += [copy(i, 0, me, sibling, src=ins[i]) for i in range(n)]
        for cp in first:
            cp.start()
        passed = []
        for j, chip in enumerate(chips):
            for i in range(n):
                copy(i, 1 + j, (*chip, c), me).wait_recv()
                passed.append(copy(i, 4 + j, (*chip, c), sibling))
                passed[-1].start()
        for i in range(n):
            copy(i, 0, sibling, me).wait_recv()
        for j, chip in enumerate(chips):
            for i in range(n):
                copy(i, 4 + j, (*chip, 1 - c), me).wait_recv()
        for cp in first + passed:
            cp.wait_send()
        for cp in mine:
            cp.wait()

    any_spec = pl.BlockSpec(memory_space=pl.ANY)
    return pl.pallas_call(
        body, name=name, in_specs=[any_spec] * n, out_specs=[any_spec] * n,
        out_shape=[jax.ShapeDtypeStruct((N_DEV,) + a.shape, a.dtype) for a in arrs],
        scratch_shapes=[pltpu.SemaphoreType.DMA((n * 7,)), pltpu.SemaphoreType.DMA((n * 7,)), pltpu.SemaphoreType.DMA((n,))],
        compiler_params=pltpu.CompilerParams(has_side_effects=True))(*arrs)


_HBM = pl.BlockSpec(memory_space=pltpu.HBM)
_SEM = pl.BlockSpec(memory_space=pltpu.SEMAPHORE)
_EFFECT = pltpu.SideEffectType.DATAFLOW_SIDE_EFFECTING


def _split_copies(ins, lands, send_sems, recv_sems, scatter):
    me = _flat_id(*_my_coords())
    out = []
    for k in range(1, N_DEV):
        peer = _peer(k)
        for i in range(len(ins)):
            src = ins[i].at[_flat_id(*peer)] if scatter else ins[i]
            out.append(pltpu.make_async_remote_copy(src_ref=src, dst_ref=lands[i].at[me], send_sem=send_sems.at[i * 7 + k - 1],
                                                    recv_sem=recv_sems.at[i * 7 + k - 1], device_id=peer, device_id_type=MESH_ID))
    return out


def _split_start(arrs, scatter, name, after=()):
    n, na = len(arrs), len(after)

    def body(*refs):
        for cp in _split_copies(refs[:n], refs[n:2 * n], refs[2 * n + na], refs[2 * n + na + 1], scatter):
            cp.start()
        refs[-1][...] = jnp.zeros_like(refs[-1])

    land_shapes = [a.shape if scatter else (N_DEV,) + a.shape for a in arrs]
    out_shape = ((pltpu.SemaphoreType.DMA((n * 7,)), pltpu.SemaphoreType.DMA((n * 7,)))
                 + tuple(pltpu.HBM(a.shape, a.dtype) for a in arrs) + tuple(pltpu.HBM(s, a.dtype) for s, a in zip(land_shapes, arrs))
                 + (jax.ShapeDtypeStruct((8, 128), F32),))
    operands = ([pltpu.with_memory_space_constraint(a, pltpu.HBM) for a in arrs]
                + [pltpu.with_memory_space_constraint(lax.empty(s, a.dtype), pltpu.HBM) for s, a in zip(land_shapes, arrs)])
    res = pl.pallas_call(
        body, name=name, out_shape=out_shape, in_specs=[_HBM] * (2 * n) + [pl.BlockSpec(memory_space=pl.ANY)] * na,
        out_specs=(_SEM, _SEM) + (_HBM,) * (2 * n) + (pl.BlockSpec(memory_space=pltpu.VMEM),),
        input_output_aliases={i: 2 + i for i in range(2 * n)},
        compiler_params=pltpu.CompilerParams(has_side_effects=_EFFECT))(*operands, *after)
    return dict(send=res[0], recv=res[1], ins=list(res[2:2 + n]), lands=list(res[2 + n:2 + 2 * n]), token=res[-1])


def _split_wait(st, scatter, after, name):
    n = len(st["ins"])

    def body(*refs):
        for cp in _split_copies(refs[:n], refs[n:2 * n], refs[2 * n], refs[2 * n + 1], scatter):
            cp.wait_send()
            cp.wait_recv()

    arrs = st["ins"] + st["lands"]
    res = pl.pallas_call(
        body, name=name, out_shape=tuple(pltpu.HBM(a.shape, a.dtype) for a in arrs),
        in_specs=[_HBM] * (2 * n) + [_SEM, _SEM] + [pl.BlockSpec(memory_space=pl.ANY)] * len(after), out_specs=(_HBM,) * (2 * n),
        input_output_aliases={i: i for i in range(2 * n)},
        compiler_params=pltpu.CompilerParams(has_side_effects=_EFFECT))(*arrs, st["send"], st["recv"], *after)
    me = _flat_id(*_my_coords())
    out = []
    for src, land in zip(res[:n], res[n:]):
        own = lax.dynamic_index_in_dim(src, me, 0, keepdims=True) if scatter else src[None]
        out.append(lax.dynamic_update_slice_in_dim(land, own, me, 0))
    return out


def _adam_step(p_ref, w_ref, m_ref, v_ref, g_ref, d_ref, nm_ref, nv_ref):
    g = p_ref[0].astype(F32)
    for j in range(1, p_ref.shape[0]):
        g = g + p_ref[j].astype(F32)
    g_ref[...] = g
    nm = ADAM_B1 * m_ref[...] + (1.0 - ADAM_B1) * g
    nv = ADAM_B2 * v_ref[...] + (1.0 - ADAM_B2) * (g * g)
    nm_ref[...] = nm
    nv_ref[...] = nv
    m_hat = nm / (1.0 - ADAM_B1 ** ADAM_STEP)
    v_hat = nv / (1.0 - ADAM_B2 ** ADAM_STEP)
    d_ref[...] = -ADAM_LR * (m_hat / (jnp.sqrt(v_hat) + ADAM_EPS) + ADAM_WD * w_ref[...])


def _adam_sum(parts, w, m, v, name):
    P, R, C = parts.shape
    tr = _pick(R, (256, 128, 64, 32, 8)) if C <= 1024 else _pick(R, (128, 64, 32, 8))
    blk = pl.BlockSpec((tr, C), lambda i: (i, 0))
    return pl.pallas_call(
        functools.partial(_adam_step), name=name, grid=(R // tr,),
        in_specs=[pl.BlockSpec((P, tr, C), lambda i: (0, i, 0)), blk, blk, blk],
        out_specs=[blk] * 4, out_shape=[jax.ShapeDtypeStruct((R, C), F32)] * 4, compiler_params=_cp(("parallel",)))(parts, w, m, v)


def _adam_small(loss_parts, parts, ws, ms, vs):
    n = len(parts)

    def body(*refs):
        loss_ref, ins, outs, total_ref = refs[0], refs[1:4 * n + 1], refs[4 * n + 1:-1], refs[-1]
        for i in range(n):
            _adam_step(ins[i], ins[n + i], ins[2 * n + i], ins[3 * n + i], *outs[4 * i:4 * i + 4])
        total = loss_ref[0]
        for d in range(1, N_DEV):
            total = total + loss_ref[d]
        total_ref[...] = total

    out_shape = [jax.ShapeDtypeStruct(w.shape, F32) for w in ws for _ in range(4)] + [jax.ShapeDtypeStruct(loss_parts.shape[1:], F32)]
    res = pl.pallas_call(body, name="adam_small", out_shape=out_shape)(loss_parts, *parts, *ws, *ms, *vs)
    return res[-1], [tuple(res[4 * i:4 * i + 4]) for i in range(n)]


_WEIGHTS = ["attn_norm_w", "w_in", "conv_w", "conv_b", "dt_bias", "a_log", "d_skip", "ssd_norm_w", "cmp_w1_k", "cmp_w2_k",
            "cmp_w1_v", "cmp_w2_v", "cmp_pe_k", "cmp_pe_v", "w_out", "ffn_norm_w", "w_gate", "w_up", "w_down", "final_norm_w"]
_BIG = ["w_in", "w_gate", "w_up", "w_down", "w_out", "cmp_w1_k", "cmp_w1_v"]
_COL_SHARDED = ("w_in", "w_gate", "w_up")
_REPLICATED = ["attn_norm_w", "conv_b", "dt_bias", "a_log", "d_skip", "ssd_norm_w", "cmp_pe_k", "cmp_pe_v", "ffn_norm_w",
               "final_norm_w"]
_SMALL_SHARDED = ["conv_w", "cmp_w2_k", "cmp_w2_v"]


def _cols_to_slabs(g):
    R = g.shape[0]
    return g.reshape(R, N_DEV, -1).transpose(1, 0, 2)


def _slabs_to_cols(s):
    return s.transpose(1, 0, 2).reshape(s.shape[1], -1)


def kernel(x, attn_norm_w, w_in, conv_w, conv_b, dt_bias, a_log, d_skip, ssd_norm_w, cmp_w1_k, cmp_w2_k, cmp_w1_v, cmp_w2_v, cmp_pe_k, cmp_pe_v, w_out, ffn_norm_w, w_gate, w_up, w_down, final_norm_w, loss_target, m_attn_norm_w, m_w_in, m_conv_w, m_conv_b, m_dt_bias, m_a_log, m_d_skip, m_ssd_norm_w, m_cmp_w1_k, m_cmp_w2_k, m_cmp_w1_v, m_cmp_w2_v, m_cmp_pe_k, m_cmp_pe_v, m_w_out, m_ffn_norm_w, m_w_gate, m_w_up, m_w_down, m_final_norm_w, v_attn_norm_w, v_w_in, v_conv_w, v_conv_b, v_dt_bias, v_a_log, v_d_skip, v_ssd_norm_w, v_cmp_w1_k, v_cmp_w2_k, v_cmp_w1_v, v_cmp_w2_v, v_cmp_pe_k, v_cmp_pe_v, v_w_out, v_ffn_norm_w, v_w_gate, v_w_up, v_w_down, v_final_norm_w):
    a = dict(locals())

    shard = {n: a[n][0].astype(_MXU) for n in _BIG}
    got = _gather_two_level([shard[n] for n in _EARLY] + [cmp_w2_k[0], cmp_w2_v[0], conv_w[0]], "gather_early")
    st_late = _split_start([shard[n] for n in _LATE], False, "gather_late_start", after=(got[0],))

    def assemble(n, t):
        return _cols_from_slabs(t) if n in _COL_SHARDED else t.reshape(-1, t.shape[-1])

    p = dict(attn_norm_w=attn_norm_w, conv_b=conv_b, dt_bias=dt_bias, a_log=a_log, d_skip=d_skip, ssd_norm_w=ssd_norm_w,
             cmp_pe_k=cmp_pe_k.reshape(1, -1), cmp_pe_v=cmp_pe_v.reshape(1, -1), ffn_norm_w=ffn_norm_w,
             final_norm_w=final_norm_w.reshape(1, -1))

    w_main, w_small = _w_in_from_slabs(got[0])
    p.update(w_main=w_main, w_small=w_small, cmp_w1_k=assemble("cmp_w1_k", got[1]), cmp_w1_v=assemble("cmp_w1_v", got[2]),
             cmp_w2_k=assemble("cmp_w2_k", got[3]).astype(_MXU), cmp_w2_v=assemble("cmp_w2_v", got[4]).astype(_MXU),
             conv_w=_slabs_to_cols(got[5]))

    def late_weights(after):
        got_late = _split_wait(st_late, False, (after,), "gather_late_wait")
        return {n: assemble(n, t) for n, t in zip(_LATE, got_late)}

    def slabs_of(g, n):
        if n == "w_in":
            return _w_in_to_slabs(g["w_main"], g["w_small"])
        return _slabs_from_cols(g[n]) if n in _COL_SHARDED else g[n].reshape(N_DEV, -1, g[n].shape[-1])

    started = []

    def grads_ready(names, g):
        started.append((names, _split_start([slabs_of(g, n) for n in names], True, "scatter_grads_start_%d" % len(started))))
        return started[-1][1]["token"]

    loss_part, grad_x, g = _local_step(x[0], loss_target[0], p, late_weights, grads_ready)

    out, after = {}, (started[-1][1]["token"],)
    for i, (names, st) in enumerate(started):
        if i == len(started) - 1:
            after = after + (grad_x,)
        received = _split_wait(st, True, after, "scatter_grads_wait_%d" % i)
        for n, parts in zip(names, received):
            out[n] = _adam_sum(parts, a[n][0], a["m_" + n][0], a["v_" + n][0], "adam_" + n)
        after = (out[names[-1]][0],)

    small_names = _REPLICATED + _SMALL_SHARDED
    partials = [g[n] for n in _REPLICATED] + [_cols_to_slabs(g["conv_w"])] + [
        g[n].reshape(N_DEV, -1, g[n].shape[-1]) for n in ("cmp_w2_k", "cmp_w2_v")]
    gathered = _exchange([loss_part] + partials, [False] * (1 + len(_REPLICATED)) + [True] * len(_SMALL_SHARDED),
                         "exchange_small_grads", after=(received[0],))
    shapes2d = [t.shape[1:] for t in gathered[1:]]
    loss, res_small = _adam_small(gathered[0], gathered[1:],
                                  *[[a[pre + n].reshape(s) for n, s in zip(small_names, shapes2d)] for pre in ("", "m_", "v_")])
    for n, r in zip(small_names, res_small):
        out[n] = r

    outs = [loss[0, 0], grad_x[None]]
    for j in range(4):
        for n in _WEIGHTS:
            outs.append(out[n][j].reshape(a[n].shape))
    return tuple(outs)
```

```python
import functools

import numpy as np
import jax
import jax.numpy as jnp
from jax import lax
from jax.experimental import pallas as pl
from jax.experimental.pallas import tpu as pltpu

F32 = jnp.float32
_MXU = jnp.bfloat16

N_DEV = 8
D_MODEL = 2048
SSD_WIDTH = 1024
ATT_WIDTH = 1024
SSD_HEADS = 16
SSD_P = 64
SSD_N = 128
SSD_L = 128
SSD_G = 2
CONV_CH = 1536
CONV_K = 4
HD = 64
N_HEADS = 16
N_KV = 4
GRP = 4
CMP_HID = 256
SEL_BLOCK = 64
N_SELECT = 16
WINDOW = 512
ROPE_DIM = 16
ROPE_THETA = 500000.0
D_FF = 5632
EPS = 1e-6
NEG = -1e30
FORCE = 1e4
SCALE = HD ** -0.5
D_IN = 5184
W_MAIN = 5120
W_SMALL = 128
VMEM_LIMIT = 52 * 1024 * 1024

ADAM_LR, ADAM_B1, ADAM_B2, ADAM_EPS, ADAM_WD, ADAM_STEP = 0.001, 0.9, 0.999, 1e-08, 0.01, 10


def _pick(n, cands):
    for c in cands:
        if n % c == 0:
            return c
    return n


def _cp(sem=None):
    return pltpu.CompilerParams(dimension_semantics=sem, vmem_limit_bytes=VMEM_LIMIT)


def _sigmoid(x):
    return 1.0 / (1.0 + jnp.exp(-x))


def _dot(a, b, dims, split=None):
    dn = {"nn": (((1,), (0,)), ((), ())), "nt": (((1,), (1,)), ((), ())), "tn": (((0,), (0,)), ((), ()))}[dims]
    mm = lambda x, y: lax.dot_general(x.astype(_MXU), y.astype(_MXU), dn, preferred_element_type=F32)
    if split is None:
        return mm(a, b)
    x = (a if split == "a" else b).astype(F32)
    hi = x.astype(_MXU)
    lo = x - hi.astype(F32)
    return mm(hi, b) + mm(lo, b) if split == "a" else mm(a, hi) + mm(a, lo)


LANE = 128
MM_TILE = 1024
MM_K_WHOLE = 2048
MM_K_STEP = 1536
TN_ACC_ELEMS = 3 * 2 ** 20
TN_K_STEP = 512


def _largest_tile(n, cap):
    if n <= cap:
        return n
    best = LANE
    for t in range(LANE, cap + 1, LANE):
        if n % t == 0:
            best = t
    return best


def _mm_tiles(mode, M, N, K):
    if mode == "tn":
        tm = _largest_tile(M, 2 * MM_TILE)
        return tm, _largest_tile(N, TN_ACC_ELEMS // tm), _largest_tile(K, TN_K_STEP)
    tk = K if K <= MM_K_WHOLE else _largest_tile(K, MM_K_STEP)
    return _largest_tile(M, MM_TILE), _largest_tile(N, MM_TILE), tk


def _mm(a, b, mode, out_dtype, name, res=None, after=None):
    if mode == "nn":
        (M, K), N = a.shape, b.shape[1]
    elif mode == "nt":
        (M, K), N = a.shape, b.shape[0]
    else:
        (K, M), N = a.shape, b.shape[1]
    tm, tn, tk = _mm_tiles(mode, M, N, K)
    nk = K // tk
    a_spec = pl.BlockSpec((tk, tm), lambda i, j, k: (k, i)) if mode == "tn" else pl.BlockSpec((tm, tk), lambda i, j, k: (i, k))
    b_spec = pl.BlockSpec((tn, tk), lambda i, j, k: (j, k)) if mode == "nt" else pl.BlockSpec((tk, tn), lambda i, j, k: (k, j))
    o_spec = pl.BlockSpec((tm, tn), lambda i, j, k: (i, j))

    def finish(r, r_ref, o_ref):
        if res is not None:
            r = r + r_ref[...].astype(F32)
        o_ref[...] = r.astype(out_dtype)

    def body_one_step(*refs):
        a_ref, b_ref, o_ref = refs[0], refs[1], refs[-1]
        finish(_dot(a_ref[...], b_ref[...], mode), refs[2], o_ref)

    def body(*refs):
        a_ref, b_ref, o_ref, acc = refs[0], refs[1], refs[-2], refs[-1]
        k = pl.program_id(2)

        @pl.when(k == 0)
        def _():
            acc[...] = jnp.zeros_like(acc)

        acc[...] += _dot(a_ref[...], b_ref[...], mode)

        @pl.when(k == nk - 1)
        def _():
            finish(acc[...], refs[2], o_ref)

    ins, specs = [a, b], [a_spec, b_spec]
    if res is not None:
        ins.append(res)
        specs.append(o_spec)
    if after is not None:
        ins.append(after)
        specs.append(pl.BlockSpec(memory_space=pl.ANY))
    return pl.pallas_call(
        body_one_step if nk == 1 else body, name=name, grid=(M // tm, N // tn, nk), in_specs=specs, out_specs=o_spec,
        out_shape=jax.ShapeDtypeStruct((M, N), out_dtype), scratch_shapes=[] if nk == 1 else [pltpu.VMEM((tm, tn), F32)],
        compiler_params=_cp(("parallel", "parallel", "arbitrary")))(*ins)


def _ffn_up(v, w_gate, w_up):
    S, D = v.shape
    F = w_gate.shape[1]
    tm, tn = _largest_tile(S, MM_TILE), _largest_tile(F, MM_TILE // 2)

    def body(v_ref, wg_ref, wu_ref, gt_ref, up_ref, act_ref):
        vv = v_ref[...]
        g = _dot(vv, wg_ref[...], "nn")
        u = _dot(vv, wu_ref[...], "nn")
        gt_ref[...] = g
        up_ref[...] = u
        act_ref[...] = (g * _sigmoid(g) * u).astype(act_ref.dtype)

    o_spec = pl.BlockSpec((tm, tn), lambda i, j: (i, j))
    w_spec = pl.BlockSpec((D, tn), lambda i, j: (0, j))
    return pl.pallas_call(
        body, name="ffn_up", grid=(S // tm, F // tn),
        in_specs=[pl.BlockSpec((tm, D), lambda i, j: (i, 0)), w_spec, w_spec], out_specs=[o_spec, o_spec, o_spec],
        out_shape=[jax.ShapeDtypeStruct((S, F), F32), jax.ShapeDtypeStruct((S, F), F32), jax.ShapeDtypeStruct((S, F), _MXU)],
        compiler_params=_cp(("parallel", "parallel")))(v, w_gate, w_up)


def _ffn_dv(dgt, dup, w_gate, w_up, after):
    S, F = dgt.shape
    D = w_gate.shape[0]
    tm, tn, tk = _mm_tiles("nt", S, D, F)
    nk = F // tk

    def body(g_ref, u_ref, wg_ref, wu_ref, *rest):
        o_ref, acc = rest[-2], rest[-1]
        k = pl.program_id(2)

        @pl.when(k == 0)
        def _():
            acc[...] = jnp.zeros_like(acc)

        acc[...] += _dot(g_ref[...], wg_ref[...], "nt") + _dot(u_ref[...], wu_ref[...], "nt")

        @pl.when(k == nk - 1)
        def _():
            o_ref[...] = acc[...]

    a_spec = pl.BlockSpec((tm, tk), lambda i, j, k: (i, k))
    w_spec = pl.BlockSpec((tn, tk), lambda i, j, k: (j, k))
    ins, specs = [dgt, dup, w_gate, w_up], [a_spec, a_spec, w_spec, w_spec]
    if after is not None:
        ins.append(after)
        specs.append(pl.BlockSpec(memory_space=pl.ANY))
    return pl.pallas_call(
        body, name="ffn_dv", grid=(S // tm, D // tn, nk), in_specs=specs, out_specs=pl.BlockSpec((tm, tn), lambda i, j, k: (i, j)),
        out_shape=jax.ShapeDtypeStruct((S, D), F32), scratch_shapes=[pltpu.VMEM((tm, tn), F32)],
        compiler_params=_cp(("parallel", "parallel", "arbitrary")))(*ins)


def _ffn_dact(dh2, w_down, gt, up):
    S, D = dh2.shape
    F = w_down.shape[0]
    tm, tn = _largest_tile(S, MM_TILE), _largest_tile(F, MM_TILE // 2)

    def body(d_ref, w_ref, gt_ref, up_ref, dg_ref, du_ref):
        da, g, u = _dot(d_ref[...], w_ref[...], "nt"), gt_ref[...], up_ref[...]
        s = _sigmoid(g)
        dg_ref[...] = (da * u * (s * (1.0 + g * (1.0 - s)))).astype(dg_ref.dtype)
        du_ref[...] = (da * (g * s)).astype(du_ref.dtype)

    o_spec = pl.BlockSpec((tm, tn), lambda i, j: (i, j))
    return pl.pallas_call(
        body, name="ffn_dact", grid=(S // tm, F // tn),
        in_specs=[pl.BlockSpec((tm, D), lambda i, j: (i, 0)), pl.BlockSpec((tn, D), lambda i, j: (j, 0)), o_spec, o_spec],
        out_specs=[o_spec, o_spec],
        out_shape=[jax.ShapeDtypeStruct((S, F), _MXU), jax.ShapeDtypeStruct((S, F), _MXU)],
        compiler_params=_cp(("parallel", "parallel")))(dh2, w_down, gt, up)


def _rms_fwd(x, w, name):
    S, D = x.shape
    tr = _pick(S, (256, 128))

    def body(x_ref, w_ref, xn_ref, rs_ref):
        xv = x_ref[...]
        rs = lax.rsqrt(jnp.mean(xv * xv, axis=-1, keepdims=True) + EPS)
        xn_ref[...] = ((xv * rs) * w_ref[...]).astype(xn_ref.dtype)
        rs_ref[...] = rs

    return pl.pallas_call(
        body, name=name, grid=(S // tr,),
        in_specs=[pl.BlockSpec((tr, D), lambda i: (i, 0)), pl.BlockSpec((1, D), lambda i: (0, 0))],
        out_specs=[pl.BlockSpec((tr, D), lambda i: (i, 0)), pl.BlockSpec((tr, 1), lambda i: (i, 0))],
        out_shape=[jax.ShapeDtypeStruct((S, D), _MXU), jax.ShapeDtypeStruct((S, 1), F32)],
        compiler_params=_cp(("parallel",)))(x, w)


def _rms_bwd(dyn, x, rs, w, res, name):
    S, D = x.shape
    tr = _pick(S, (256, 128))

    def body(dy_ref, x_ref, rs_ref, w_ref, res_ref, dx_ref, dxb_ref, dw_ref):
        @pl.when(pl.program_id(0) == 0)
        def _():
            dw_ref[...] = jnp.zeros_like(dw_ref)

        dy, r = dy_ref[...].astype(F32), rs_ref[...]
        xhat = x_ref[...] * r
        dw_ref[...] += jnp.sum(dy * xhat, axis=0, keepdims=True)
        dxhat = dy * w_ref[...]
        dx = res_ref[...] + r * (dxhat - xhat * jnp.mean(dxhat * xhat, axis=-1, keepdims=True))
        dx_ref[...] = dx
        dxb_ref[...] = dx.astype(dxb_ref.dtype)

    row = pl.BlockSpec((tr, D), lambda i: (i, 0))
    vec = pl.BlockSpec((1, D), lambda i: (0, 0))
    return pl.pallas_call(
        body, name=name, grid=(S // tr,),
        in_specs=[row, row, pl.BlockSpec((tr, 1), lambda i: (i, 0)), vec, row], out_specs=[row, row, vec],
        out_shape=[jax.ShapeDtypeStruct((S, D), F32), jax.ShapeDtypeStruct((S, D), _MXU), jax.ShapeDtypeStruct((1, D), F32)],
        compiler_params=_cp(("arbitrary",)))(dyn, x, rs, w, res)


def _final_loss(h2, w, tgt):
    S, D = h2.shape
    tr = _pick(S, (256, 128))

    def body(h_ref, w_ref, t_ref, loss_ref, dh_ref, dhb_ref, dw_ref):
        @pl.when(pl.program_id(0) == 0)
        def _():
            dw_ref[...] = jnp.zeros_like(dw_ref)
            loss_ref[...] = jnp.zeros_like(loss_ref)

        hv, wv = h_ref[...], w_ref[...]
        rs = lax.rsqrt(jnp.mean(hv * hv, axis=-1, keepdims=True) + EPS)
        xhat = hv * rs
        err = xhat * wv - t_ref[...]
        row = jnp.mean(err * err, axis=-1, keepdims=True)
        loss_ref[...] += jnp.broadcast_to(0.5 * jnp.sum(row, axis=0, keepdims=True), loss_ref.shape)
        dy = err * (1.0 / D)
        dw_ref[...] += jnp.sum(dy * xhat, axis=0, keepdims=True)
        dxhat = dy * wv
        dh = rs * (dxhat - xhat * jnp.mean(dxhat * xhat, axis=-1, keepdims=True))
        dh_ref[...] = dh
        dhb_ref[...] = dh.astype(dhb_ref.dtype)

    row = pl.BlockSpec((tr, D), lambda i: (i, 0))
    vec = pl.BlockSpec((1, D), lambda i: (0, 0))
    return pl.pallas_call(
        body, name="final_loss", grid=(S // tr,), in_specs=[row, vec, row],
        out_specs=[pl.BlockSpec((1, LANE), lambda i: (0, 0)), row, row, vec],
        out_shape=[jax.ShapeDtypeStruct((1, LANE), F32), jax.ShapeDtypeStruct((S, D), F32), jax.ShapeDtypeStruct((S, D), _MXU),
                   jax.ShapeDtypeStruct((1, D), F32)],
        compiler_params=_cp(("arbitrary",)))(h2, w, tgt)


def _shift_rows(x, k, rows):
    if k == 0:
        return x
    S = x.shape[0]
    r = pltpu.roll(x, k % S, axis=0)
    ok = (rows >= k) if k > 0 else (rows < S + k)
    return jnp.where(ok, r, 0.0)


XBC_COL0 = SSD_WIDTH // 128


def _conv_fwd(proj, conv_w, conv_b):
    S = proj.shape[0]
    nct = CONV_CH // 128

    def body(x_ref, w_ref, b_ref, o_ref):
        x = x_ref[...]
        rows = lax.broadcasted_iota(jnp.int32, x.shape, 0)
        c = b_ref[...] + w_ref[3:4, :] * x
        for k in range(1, CONV_K):
            c = c + w_ref[3 - k:4 - k, :] * _shift_rows(x, k, rows)
        o_ref[...] = c * _sigmoid(c)

    return pl.pallas_call(
        body, name="conv_fwd", grid=(nct,),
        in_specs=[pl.BlockSpec((S, 128), lambda j: (0, XBC_COL0 + j)), pl.BlockSpec((CONV_K, 128), lambda j: (0, j)),
                  pl.BlockSpec((1, 128), lambda j: (0, j))],
        out_specs=pl.BlockSpec((S, 128), lambda j: (0, j)),
        out_shape=jax.ShapeDtypeStruct((S, CONV_CH), F32), compiler_params=_cp(("parallel",)))(proj, conv_w, conv_b)


def _conv_bwd(proj, conv_w, conv_b, dxa):
    S = proj.shape[0]
    nct = CONV_CH // 128

    def body(x_ref, w_ref, b_ref, d_ref, dx_ref, dw_ref, db_ref):
        x = x_ref[...]
        rows = lax.broadcasted_iota(jnp.int32, x.shape, 0)
        xs = [_shift_rows(x, k, rows) for k in range(CONV_K)]
        c = b_ref[...] + w_ref[3:4, :] * x
        for k in range(1, CONV_K):
            c = c + w_ref[3 - k:4 - k, :] * xs[k]
        s = _sigmoid(c)
        dc = d_ref[...] * (s * (1.0 + c * (1.0 - s)))
        dx = w_ref[3:4, :] * dc
        for k in range(1, CONV_K):
            dx = dx + w_ref[3 - k:4 - k, :] * _shift_rows(dc, -k, rows)
        dx_ref[...] = dx.astype(dx_ref.dtype)
        for k in range(CONV_K):
            dw_ref[3 - k:4 - k, :] = jnp.sum(dc * xs[k], axis=0, keepdims=True)
        db_ref[...] = jnp.sum(dc, axis=0, keepdims=True)

    col = pl.BlockSpec((S, 128), lambda j: (0, j))
    return pl.pallas_call(
        body, name="conv_bwd", grid=(nct,),
        in_specs=[pl.BlockSpec((S, 128), lambda j: (0, XBC_COL0 + j)), pl.BlockSpec((CONV_K, 128), lambda j: (0, j)),
                  pl.BlockSpec((1, 128), lambda j: (0, j)), col],
        out_specs=[col, pl.BlockSpec((CONV_K, 128), lambda j: (0, j)), pl.BlockSpec((1, 128), lambda j: (0, j))],
        out_shape=[jax.ShapeDtypeStruct((S, CONV_CH), _MXU), jax.ShapeDtypeStruct((CONV_K, CONV_CH), F32),
                   jax.ShapeDtypeStruct((1, CONV_CH), F32)],
        compiler_params=_cp(("parallel",)))(proj, conv_w, conv_b, dxa)


def _ssd_consts():
    L = SSD_L
    r = lax.broadcasted_iota(jnp.int32, (L, L), 0)
    c = lax.broadcasted_iota(jnp.int32, (L, L), 1)
    causal = r >= c
    upper = (r <= c).astype(F32)
    hr = lax.broadcasted_iota(jnp.int32, (SSD_HEADS, SSD_WIDTH), 0)
    hc = lax.broadcasted_iota(jnp.int32, (SSD_HEADS, SSD_WIDTH), 1)
    expand = (lax.shift_right_logical(hc, 6) == hr).astype(F32)
    return causal, causal.astype(F32), upper, expand


def _softplus(x):
    return jnp.maximum(x, 0.0) + jnp.log(1.0 + jnp.exp(-jnp.abs(x)))


def _ssd_scalars(dtr, dt_bias, a_log, tri, upper, expand):
    dt = _softplus(dtr + dt_bias)
    A = -jnp.exp(a_log)
    adt = dt * A
    acum = _dot(tri, adt, "nn", split="b")
    acum_t = _dot(adt, upper, "tn", split="a")
    alast = acum[SSD_L - 1:SSD_L, :]
    e = jnp.exp(acum)
    wdec = jnp.exp(alast - acum)
    gam = jnp.exp(alast)
    ex = lambda t: _dot(t, expand, "nn", split="a")
    gam8 = jnp.broadcast_to(gam, (8, SSD_HEADS))
    return dt, A, acum, acum_t, e, wdec, gam, ex(dt), ex(e), ex(wdec), ex(gam8)[0:1, :]


def _ssd_fwd(proj, proj_small, xa, dt_bias, a_log, d_skip, norm_w):
    S = proj.shape[0]
    L, N, W = SSD_L, SSD_N, SSD_WIDTH
    nc = S // L

    def body(z_ref, xa_ref, dtr_ref, dtb_ref, al_ref, dsk_ref, nw_ref, yo_ref, y_ref, rs_ref, hs_ref, h_scr, y_scr):
        @pl.when(pl.program_id(0) == 0)
        def _():
            h_scr[...] = jnp.zeros_like(h_scr)

        causal, tri, upper, expand = _ssd_consts()
        dt, A, acum, acum_t, e, wdec, gam, dtE, eE, wE, gamE = _ssd_scalars(dtr_ref[:, 0:SSD_HEADS], dtb_ref[...], al_ref[...], tri, upper, expand)
        xs = xa_ref[:, 0:W]
        X = xs * dtE
        XW = X * wE
        hs_ref[0] = h_scr[...]
        for g in range(SSD_G):
            gs = slice(g * 512, (g + 1) * 512)
            Bg = xa_ref[:, W + g * N:W + (g + 1) * N]
            Cg = xa_ref[:, W + SSD_G * N + g * N:W + SSD_G * N + (g + 1) * N]
            Hg = h_scr[:, gs]
            CB = _dot(Cg, Bg, "nt")
            yoff = _dot(Cg, Hg, "nn") * eE[:, gs]
            st = _dot(Bg, XW[:, gs], "tn")
            for j in range(8):
                h = g * 8 + j
                hsl = slice(h * SSD_P, (h + 1) * SSD_P)
                lam = jnp.exp(jnp.where(causal, acum[:, h:h + 1] - acum_t[h:h + 1, :], -jnp.inf))
                y_scr[:, hsl] = _dot(CB * lam, X[:, hsl], "nn") + yoff[:, j * SSD_P:(j + 1) * SSD_P]
            h_scr[:, gs] = gamE[:, gs] * Hg + st
        dskE = _dot(jnp.broadcast_to(dsk_ref[...], (8, SSD_HEADS)), expand, "nn", split="a")[0:1, :]
        y = y_scr[...] + dskE * xs
        y_ref[...] = y
        zv = z_ref[...]
        yg = y * (zv * _sigmoid(zv))
        rs = lax.rsqrt(jnp.mean(yg * yg, axis=-1, keepdims=True) + EPS)
        rs_ref[...] = rs
        yo_ref[...] = ((yg * rs) * nw_ref[...]).astype(yo_ref.dtype)

    p16 = pl.BlockSpec((1, SSD_HEADS), lambda c: (0, 0))
    return pl.pallas_call(
        body, name="ssd_fwd", grid=(nc,),
        in_specs=[pl.BlockSpec((L, W), lambda c: (c, 0)), pl.BlockSpec((L, CONV_CH), lambda c: (c, 0)),
                  pl.BlockSpec((L, W_SMALL), lambda c: (c, 0)), p16, p16, p16, pl.BlockSpec((1, W), lambda c: (0, 0))],
        out_specs=[pl.BlockSpec((L, W), lambda c: (c, 0)), pl.BlockSpec((L, W), lambda c: (c, 0)),
                   pl.BlockSpec((L, 1), lambda c: (c, 0)), pl.BlockSpec((1, N, W), lambda c: (c, 0, 0))],
        out_shape=[jax.ShapeDtypeStruct((S, W), _MXU), jax.ShapeDtypeStruct((S, W), F32), jax.ShapeDtypeStruct((S, 1), F32),
                   jax.ShapeDtypeStruct((nc, N, W), F32)],
        scratch_shapes=[pltpu.VMEM((N, W), F32), pltpu.VMEM((L, W), F32)],
        compiler_params=_cp(("arbitrary",)))(proj, xa, proj_small, dt_bias, a_log, d_skip, norm_w)


def _ssd_bwd(dmixed, proj, proj_small, xa, y, rs2, hs, dt_bias, a_log, d_skip, norm_w):
    S = proj.shape[0]
    L, N, W, H = SSD_L, SSD_N, SSD_WIDTH, SSD_HEADS
    nc = S // L

    def body(dyo_ref, z_ref, xa_ref, dtr_ref, y_ref, rs_ref, hs_ref, dtb_ref, al_ref, dsk_ref, nw_ref,
             dz_ref, dxa_ref, ddtr_ref, ddtb_ref, dal_ref, ddsk_ref, dnw_ref, dh_scr, dx_scr):
        @pl.when(pl.program_id(0) == 0)
        def _():
            dh_scr[...] = jnp.zeros_like(dh_scr)
            ddtb_ref[...] = jnp.zeros_like(ddtb_ref)
            dal_ref[...] = jnp.zeros_like(dal_ref)
            ddsk_ref[...] = jnp.zeros_like(ddsk_ref)
            dnw_ref[...] = jnp.zeros_like(dnw_ref)

        causal, tri, upper, expand = _ssd_consts()
        heads = lambda t: _dot(t, expand, "nt", split="a")
        onehot = lambda h: (lax.broadcasted_iota(jnp.int32, (1, H), 1) == h).astype(F32)

        zv, yv, rs = z_ref[...], y_ref[...], rs_ref[...]
        sz = _sigmoid(zv)
        zs = zv * sz
        xhat = (yv * zs) * rs
        dyo = dyo_ref[...].astype(F32)
        dnw_ref[...] += jnp.sum(dyo * xhat, axis=0, keepdims=True)
        dxhat = dyo * nw_ref[...]
        dyg = rs * (dxhat - xhat * jnp.mean(dxhat * xhat, axis=-1, keepdims=True))
        dz_ref[...] = (dyg * yv * (sz * (1.0 + zv * (1.0 - sz)))).astype(dz_ref.dtype)
        dy = dyg * zs

        dtr = dtr_ref[:, 0:H]
        dt, A, acum, acum_t, e, wdec, gam, dtE, eE, wE, gamE = _ssd_scalars(dtr, dtb_ref[...], al_ref[...], tri, upper, expand)
        xs = xa_ref[:, 0:W]
        X = xs * dtE
        XW = X * wE
        dskE = _dot(jnp.broadcast_to(dsk_ref[...], (8, H)), expand, "nn", split="a")[0:1, :]
        ddsk_ref[...] += heads(jnp.broadcast_to(jnp.sum(dy * xs, axis=0, keepdims=True), (8, W)))[0:1, :]

        dYe = dy * eE
        dacum = jnp.zeros((L, H), F32)
        de_full = []
        dw_full = []
        dgam_full = []
        for g in range(SSD_G):
            gs = slice(g * 512, (g + 1) * 512)
            Bg = xa_ref[:, W + g * N:W + (g + 1) * N]
            Cg = xa_ref[:, W + SSD_G * N + g * N:W + SSD_G * N + (g + 1) * N]
            Hg = hs_ref[0, :, gs]
            dHn = dh_scr[:, gs]
            CH = _dot(Cg, Hg, "nn")
            de_full.append(dy[:, gs] * CH)
            dC = _dot(dYe[:, gs], Hg, "nt")
            dHs = gamE[:, gs] * dHn + _dot(Cg, dYe[:, gs], "tn")
            dgam_full.append(jnp.sum(dHn * Hg, axis=0, keepdims=True))
            BdS = _dot(Bg, dHn, "nn")
            dB = _dot(XW[:, gs], dHn, "nt")
            dx_scr[:, gs] = BdS * wE[:, gs]
            dw_full.append(BdS * X[:, gs])
            CB = _dot(Cg, Bg, "nt")
            dCB = jnp.zeros((L, L), F32)
            for j in range(8):
                h = g * 8 + j
                hsl = slice(h * SSD_P, (h + 1) * SSD_P)
                lam = jnp.exp(jnp.where(causal, acum[:, h:h + 1] - acum_t[h:h + 1, :], -jnp.inf))
                M = CB * lam
                dM = _dot(dy[:, hsl], X[:, hsl], "nt")
                dx_scr[:, hsl] += _dot(M, dy[:, hsl], "tn")
                dCB = dCB + dM * lam
                Q = dM * M
                rowsum = jnp.sum(Q, axis=1, keepdims=True)
                colsum = _dot(Q, jnp.ones((L, 8), F32), "tn", split="a")[:, 0:1]
                dacum = dacum + (rowsum - colsum) * onehot(h)
            dC = dC + _dot(dCB, Bg, "nn")
            dB = dB + _dot(dCB, Cg, "tn")
            dxa_ref[:, W + g * N:W + (g + 1) * N] = dB
            dxa_ref[:, W + SSD_G * N + g * N:W + SSD_G * N + (g + 1) * N] = dC
            dh_scr[:, gs] = dHs

        de16 = heads(jnp.concatenate(de_full, axis=1))
        dw16 = heads(jnp.concatenate(dw_full, axis=1))
        dgam16 = heads(jnp.broadcast_to(jnp.concatenate(dgam_full, axis=1), (8, W)))[0:1, :]
        dacum = dacum + de16 * e - dw16 * wdec
        dlast = jnp.sum(dw16 * wdec, axis=0, keepdims=True) + dgam16 * gam
        lastrow = (lax.broadcasted_iota(jnp.int32, (L, 1), 0) == L - 1).astype(F32)
        dacum = dacum + lastrow * dlast
        da = _dot(tri, dacum, "tn", split="b")
        dX = dx_scr[...]
        ddt = da * A + heads(dX * xs)
        dA = jnp.sum(da * dt, axis=0, keepdims=True)
        dal_ref[...] += dA * A
        ddtr = ddt * _sigmoid(dtr + dtb_ref[...])
        ddtb_ref[...] += jnp.sum(ddtr, axis=0, keepdims=True)
        ddtr_ref[...] = ddtr
        dxa_ref[:, 0:W] = dX * dtE + dy * dskE

    p16 = pl.BlockSpec((1, H), lambda c: (0, 0))
    rev = lambda c: (nc - 1 - c, 0)
    return pl.pallas_call(
        body, name="ssd_bwd", grid=(nc,),
        in_specs=[pl.BlockSpec((L, W), rev), pl.BlockSpec((L, W), rev), pl.BlockSpec((L, CONV_CH), rev),
                  pl.BlockSpec((L, W_SMALL), rev), pl.BlockSpec((L, W), rev), pl.BlockSpec((L, 1), rev),
                  pl.BlockSpec((1, N, W), lambda c: (nc - 1 - c, 0, 0)), p16, p16, p16, pl.BlockSpec((1, W), lambda c: (0, 0))],
        out_specs=[pl.BlockSpec((L, W), rev), pl.BlockSpec((L, CONV_CH), rev), pl.BlockSpec((L, H), rev),
                   p16, p16, p16, pl.BlockSpec((1, W), lambda c: (0, 0))],
        out_shape=[jax.ShapeDtypeStruct((S, W), _MXU), jax.ShapeDtypeStruct((S, CONV_CH), F32), jax.ShapeDtypeStruct((S, H), F32),
                   jax.ShapeDtypeStruct((1, H), F32), jax.ShapeDtypeStruct((1, H), F32), jax.ShapeDtypeStruct((1, H), F32),
                   jax.ShapeDtypeStruct((1, W), F32)],
        scratch_shapes=[pltpu.VMEM((N, W), F32), pltpu.VMEM((L, W), F32)],
        compiler_params=_cp(("arbitrary",)))(dmixed, proj, xa, proj_small, y, rs2, hs, dt_bias, a_log, d_skip, norm_w)


def _rope_tables(S):
    inv = 1.0 / (ROPE_THETA ** (jnp.arange(0, ROPE_DIM, 2, dtype=F32) / ROPE_DIM))
    ang = jnp.arange(S, dtype=F32)[:, None] * inv[None, :]
    cos, sin = jnp.cos(ang), jnp.sin(ang)
    half = ROPE_DIM // 2
    c64 = jnp.concatenate([cos, cos, jnp.ones((S, HD - ROPE_DIM), F32)], axis=1)
    s64 = jnp.concatenate([sin, sin, jnp.zeros((S, HD - ROPE_DIM), F32)], axis=1)
    del half
    return jnp.concatenate([c64, c64], axis=1), jnp.concatenate([s64, s64], axis=1)


def _rope(xs, blk0, width, cos, sin, sign, out_dtype, name, extra=None):
    S = xs[0].shape[0]
    tr = _pick(S, (512, 256, 128))
    nx = len(xs)

    def body(*refs):
        x_refs, c_ref, s_ref = refs[:nx], refs[nx], refs[nx + 1]
        e_ref = refs[nx + 2] if extra is not None else None
        o_ref = refs[-1]
        cv, sv = c_ref[...], s_ref[...] * sign
        lane = lax.broadcasted_iota(jnp.int32, (tr, 128), 1)
        first = (lane & (HD - 1)) < (ROPE_DIM // 2)
        for j in range(2):
            cs = slice(j * 128, (j + 1) * 128)
            xv = x_refs[0][:, cs].astype(F32)
            for r in x_refs[1:]:
                xv = xv + r[:, cs].astype(F32)
            rot = jnp.where(first, -pltpu.roll(xv, 128 - ROPE_DIM // 2, axis=1), pltpu.roll(xv, ROPE_DIM // 2, axis=1))
            out = xv * cv + rot * sv
            if extra is not None:
                out = out + e_ref[:, cs].astype(F32)
            o_ref[:, cs] = out.astype(out_dtype)

    t128 = pl.BlockSpec((tr, 128), lambda i, j: (i, 0))
    oblk = pl.BlockSpec((tr, 256), lambda i, j: (i, j))
    specs = [pl.BlockSpec((tr, 256), lambda i, j: (i, blk0 + j))] * nx + [t128, t128]
    ins = list(xs) + [cos, sin]
    if extra is not None:
        ins.append(extra[0])
        eb = extra[1]
        specs.append(pl.BlockSpec((tr, 256), lambda i, j: (i, eb + j)))
    return pl.pallas_call(
        body, name=name, grid=(S // tr, width // 256), in_specs=specs, out_specs=oblk,
        out_shape=jax.ShapeDtypeStruct((S, width), out_dtype), compiler_params=_cp(("parallel", "parallel")))(*ins)


def _rotate128(xv, cv, sv, first):
    rot = jnp.where(first, -pltpu.roll(xv, 128 - ROPE_DIM // 2, axis=1), pltpu.roll(xv, ROPE_DIM // 2, axis=1))
    return xv * cv + rot * sv


def _kv_prep(proj, cos, sin, tk):
    S = proj.shape[0]
    NB = S // SEL_BLOCK

    def body(ks_ref, vs_ref, kw_ref, vw_ref, c_ref, s_ref, *outs):
        cv, sv = c_ref[...], s_ref[...]
        lane = lax.broadcasted_iota(jnp.int32, (tk, 128), 1)
        first = (lane & (HD - 1)) < (ROPE_DIM // 2)
        key = pl.program_id(0) * tk + lax.broadcasted_iota(jnp.int32, (tk, NB), 0)
        onehot = (lax.shift_right_logical(key, 6) == lax.broadcasted_iota(jnp.int32, (tk, NB), 1)).astype(F32)
        for j, (ref, rotated) in enumerate(((ks_ref, True), (vs_ref, False), (kw_ref, True), (vw_ref, False))):
            nat, blk = outs[2 * j], outs[2 * j + 1]
            for half in range(2):
                xv = ref[:, half * 128:(half + 1) * 128]
                if rotated:
                    xv = _rotate128(xv, cv, sv, first)
                for e in range(2):
                    h = 2 * half + e
                    piece = xv[:, e * HD:(e + 1) * HD]
                    nat[h] = (jnp.concatenate([piece, onehot], axis=1) if j == 0 else piece).astype(nat.dtype)
                    blk[h, 0] = piece.T.astype(blk.dtype)

    col = lambda b: pl.BlockSpec((tk, 256), lambda i: (i, b))
    t128 = pl.BlockSpec((tk, 128), lambda i: (i, 0))
    nat_spec = lambda w: pl.BlockSpec((N_KV, tk, w), lambda i: (0, i, 0))
    blk_spec = pl.BlockSpec((N_KV, 1, HD, tk), lambda i: (0, i, 0, 0))
    nat_shape = lambda w: jax.ShapeDtypeStruct((N_KV, S, w), _MXU)
    blk_shape = jax.ShapeDtypeStruct((N_KV, S // tk, HD, tk), _MXU)
    widths = (HD + NB, HD, HD, HD)
    res = pl.pallas_call(
        body, name="kv_prep", grid=(S // tk,), in_specs=[col(KSB), col(VSB), col(KWB), col(VWB), t128, t128],
        out_specs=[s for w in widths for s in (nat_spec(w), blk_spec)],
        out_shape=[s for w in widths for s in (nat_shape(w), blk_shape)],
        compiler_params=_cp(("parallel",)))(proj, proj, proj, proj, cos, sin)
    return dict(ks_ext=res[0], ks_t=res[1], vs=res[2], vs_t=res[3], kw=res[4], kw_t=res[5], vw=res[6], vw_t=res[7])


def _dkv_post(dks, dvs, dkw, dvw, cos, sin):
    S = dks.shape[1]
    tr = _pick(S, (512, 256, 128))

    def body(dks_ref, dvs_ref, dkw_ref, dvw_ref, c_ref, s_ref, o_ref):
        cv, sv = c_ref[...], -s_ref[...]
        lane = lax.broadcasted_iota(jnp.int32, (tr, 128), 1)
        first = (lane & (HD - 1)) < (ROPE_DIM // 2)
        for j, (ref, rotated) in enumerate(((dks_ref, True), (dvs_ref, False), (dkw_ref, True), (dvw_ref, False))):
            for half in range(2):
                xv = jnp.concatenate([ref[2 * half], ref[2 * half + 1]], axis=1)
                if rotated:
                    xv = _rotate128(xv, cv, sv, first)
                o_ref[:, j * 256 + half * 128:j * 256 + (half + 1) * 128] = xv.astype(o_ref.dtype)

    hm = pl.BlockSpec((N_KV, tr, HD), lambda i: (0, i, 0))
    t128 = pl.BlockSpec((tr, 128), lambda i: (i, 0))
    return pl.pallas_call(
        body, name="dkv_post", grid=(S // tr,), in_specs=[hm, hm, hm, hm, t128, t128],
        out_specs=pl.BlockSpec((tr, 4 * 256), lambda i: (i, 0)), out_shape=jax.ShapeDtypeStruct((S, 4 * 256), _MXU),
        compiler_params=_cp(("parallel",)))(dks, dvs, dkw, dvw, cos, sin)


def _compress_fwd(R, pe, w1, w2):
    NC = R.shape[1]
    half = 16 * HD

    def body(r_ref, pe_ref, w1_ref, w2_ref, o_ref, hid_ref):
        r = r_ref[0]
        a = _dot(r + pe_ref[:, 0:half], w1_ref[0:half, :], "nn")
        b = _dot(r + pe_ref[:, half:2 * half], w1_ref[half:2 * half, :], "nn")
        hid = a + pltpu.roll(b, NC - 1, axis=0)
        hid_ref[0] = hid
        out = _dot(hid * _sigmoid(hid), w2_ref[...], "nn")
        rows = lax.broadcasted_iota(jnp.int32, out.shape, 0)
        o_ref[0] = jnp.where(rows < NC - 1, out, 0.0).astype(o_ref.dtype)

    return pl.pallas_call(
        body, name="compress_fwd", grid=(N_KV,),
        in_specs=[pl.BlockSpec((1, NC, half), lambda h: (h, 0, 0)), pl.BlockSpec((1, 2 * half), lambda h: (0, 0)),
                  pl.BlockSpec((2 * half, CMP_HID), lambda h: (0, 0)), pl.BlockSpec((CMP_HID, HD), lambda h: (0, 0))],
        out_specs=[pl.BlockSpec((1, NC, HD), lambda h: (h, 0, 0)), pl.BlockSpec((1, NC, CMP_HID), lambda h: (h, 0, 0))],
        out_shape=[jax.ShapeDtypeStruct((N_KV, NC, HD), _MXU), jax.ShapeDtypeStruct((N_KV, NC, CMP_HID), F32)],
        compiler_params=_cp(("parallel",)))(R, pe, w1, w2)


def _compress_bwd(R, pe, w1, w2, hid, dout):
    NC = R.shape[1]
    half = 16 * HD

    def body(r_ref, pe_ref, w1_ref, w2_ref, hid_ref, do_ref, dr_ref, dw1_ref, dw2_ref, dpe_ref):
        @pl.when(pl.program_id(0) == 0)
        def _():
            dw1_ref[...] = jnp.zeros_like(dw1_ref)
            dw2_ref[...] = jnp.zeros_like(dw2_ref)
            dpe_ref[...] = jnp.zeros_like(dpe_ref)

        r, hv, do = r_ref[0], hid_ref[0], do_ref[0]
        s = _sigmoid(hv)
        dw2_ref[...] += _dot(hv * s, do, "tn")
        dhid = _dot(do, w2_ref[...], "nt") * (s * (1.0 + hv * (1.0 - s)))
        rows = lax.broadcasted_iota(jnp.int32, dhid.shape, 0)
        dhid = jnp.where(rows < NC - 1, dhid, 0.0)
        dhid_dn = pltpu.roll(dhid, 1, axis=0)
        dw1_ref[0:half, :] += _dot(r + pe_ref[:, 0:half], dhid, "tn")
        dw1_ref[half:2 * half, :] += _dot(r + pe_ref[:, half:2 * half], dhid_dn, "tn")
        dxt = _dot(dhid, w1_ref[0:half, :], "nt")
        dxb = _dot(dhid_dn, w1_ref[half:2 * half, :], "nt")
        dr_ref[0] = dxt + dxb
        dpe_ref[:, 0:half] += jnp.sum(dxt, axis=0, keepdims=True)
        dpe_ref[:, half:2 * half] += jnp.sum(dxb, axis=0, keepdims=True)

    return pl.pallas_call(
        body, name="compress_bwd", grid=(N_KV,),
        in_specs=[pl.BlockSpec((1, NC, half), lambda h: (h, 0, 0)), pl.BlockSpec((1, 2 * half), lambda h: (0, 0)),
                  pl.BlockSpec((2 * half, CMP_HID), lambda h: (0, 0)), pl.BlockSpec((CMP_HID, HD), lambda h: (0, 0)),
                  pl.BlockSpec((1, NC, CMP_HID), lambda h: (h, 0, 0)), pl.BlockSpec((1, NC, HD), lambda h: (h, 0, 0))],
        out_specs=[pl.BlockSpec((1, NC, half), lambda h: (h, 0, 0)), pl.BlockSpec((2 * half, CMP_HID), lambda h: (0, 0)),
                   pl.BlockSpec((CMP_HID, HD), lambda h: (0, 0)), pl.BlockSpec((1, 2 * half), lambda h: (0, 0))],
        out_shape=[jax.ShapeDtypeStruct((N_KV, NC, half), F32), jax.ShapeDtypeStruct((2 * half, CMP_HID), F32),
                   jax.ShapeDtypeStruct((CMP_HID, HD), F32), jax.ShapeDtypeStruct((1, 2 * half), F32)],
        compiler_params=_cp(("arbitrary",)))(R, pe, w1, w2, hid, dout)


def _attn_cfg(S, Sk, mode):
    tq = _pick(S, (256, 128))
    tk = Sk if mode == "cmp" else _pick(Sk, (256, 128))
    return tq, tk


def _block_start(kb, tk):
    return kb * tk if isinstance(kb, int) else pl.multiple_of(kb * tk, tk)


def _pipelined_key_blocks(mode, q0, tq, tk, produce, consume):
    if mode == "cmp":
        produce(0, True, 0)
        consume(0, 0)
        return
    if mode == "sel":
        first, n_plain, plain_masked = 0, q0 // tk, False
    else:
        first = jnp.maximum(q0 - (WINDOW - 1), 0) // tk
        n_plain, plain_masked = (q0 + tq - 1) // tk - first, True
    last = first + n_plain
    pairs = jnp.maximum(n_plain - 1, 0) // 2

    @pl.when(n_plain >= 1)
    def _():
        produce(first, plain_masked, 0)

    def two(j, carry):
        kb = first + 2 * j
        produce(kb + 1, plain_masked, 1)
        consume(kb, 0)
        produce(kb + 2, plain_masked, 0)
        consume(kb + 1, 1)
        return carry

    lax.fori_loop(0, pairs, two, 0)
    kb = first + 2 * pairs
    left = n_plain - 2 * pairs

    @pl.when(left == 2)
    def _():
        produce(kb + 1, plain_masked, 1)
        consume(kb, 0)
        produce(last, True, 0)
        consume(kb + 1, 1)
        consume(last, 0)

    @pl.when(left == 1)
    def _():
        produce(last, True, 1)
        consume(kb, 0)
        consume(last, 1)

    @pl.when(left == 0)
    def _():
        produce(last, True, 0)
        consume(last, 0)


def _attn_bias(mode, q0, k0, tq, tk):
    k = k0 + lax.broadcasted_iota(jnp.int32, (tk, tq), 0)
    t = q0 + lax.broadcasted_iota(jnp.int32, (tk, tq), 1)
    if mode == "cmp":
        ok = (k * 16 + 31) <= t
    elif mode == "win":
        ok = (k <= t) & ((t - k) < WINDOW)
    else:
        ok = k <= t
    bias = jnp.where(ok, 0.0, NEG)
    return jnp.concatenate([bias] * GRP, axis=1), jnp.concatenate([ok.astype(F32)] * GRP, axis=1)


def _sel_operands(qs, selneg_ref):
    return jnp.concatenate([qs, jnp.concatenate([selneg_ref[0]] * GRP, axis=0)], axis=1)


def _stack_heads(ref, tq):
    return jnp.concatenate([ref[:, g * HD:(g + 1) * HD] for g in range(GRP)], axis=0)


def _scaled_queries(q_ref, tq):
    return (_stack_heads(q_ref, tq).astype(F32) * SCALE).astype(_MXU)


def _blocked_t(x, tk):
    n, Sk, d = x.shape
    return x.reshape(n, Sk // tk, tk, d).transpose(0, 1, 3, 2)


def _head_rows(ref):
    return jnp.concatenate([ref[0, g:g + 1, :] for g in range(GRP)], axis=1)


def _attn_fwd(q, qcol0, k, vt, mode, selneg, gate, y_prev, y_dtype, name):
    S, Sk = q.shape[0], k.shape[1]
    tq, tk = _attn_cfg(S, Sk, mode)
    R = GRP * tq

    def body(*refs):
        q_ref, k_ref, vt_ref = refs[:3]
        rest = list(refs[3:])
        sel_ref = rest.pop(0) if mode == "sel" else None
        gate_ref = rest.pop(0)
        yp_ref = rest.pop(0) if y_prev is not None else None
        o_ref, lse_ref, y_ref, m_scr, l_scr, acc, s_scr = rest
        q0 = pl.program_id(1) * tq
        qs = _scaled_queries(q_ref, tq)
        m_scr[...] = jnp.full_like(m_scr, NEG)
        l_scr[...] = jnp.zeros_like(l_scr)
        acc[...] = jnp.zeros_like(acc)
        qk = _sel_operands(qs, sel_ref) if mode == "sel" else qs

        def produce(kb, masked, slot):
            k0 = _block_start(kb, tk)
            s = _dot(k_ref[0, pl.ds(k0, tk), :], qk, "nt")
            if masked:
                s = s + _attn_bias(mode, q0, k0, tq, tk)[0]
            s_scr[slot] = s

        def consume(kb, slot):
            s = s_scr[slot]
            m_old = m_scr[...]
            m_new = jnp.maximum(m_old, jnp.max(s, axis=0, keepdims=True))
            p = jnp.exp(s - m_new)
            if mode == "cmp":
                p = p * _attn_bias(mode, q0, 0, tq, tk)[1]
            alpha = jnp.exp(m_old - m_new)
            l_scr[...] = alpha * l_scr[...] + jnp.sum(p, axis=0, keepdims=True)
            acc[...] = alpha * acc[...] + _dot(vt_ref[0, kb], p, "nn")
            m_scr[...] = m_new

        _pipelined_key_blocks(mode, q0, tq, tk, produce, consume)
        l = l_scr[...]
        good = l > 0.0
        o_t = acc[...] * jnp.where(good, 1.0 / jnp.where(good, l, 1.0), 0.0)
        lse = jnp.where(good, m_scr[...] + jnp.log(jnp.where(good, l, 1.0)), -NEG)
        y_t = o_t * _sigmoid(_head_rows(gate_ref))
        for g in range(GRP):
            hs, qs_ = slice(g * HD, (g + 1) * HD), slice(g * tq, (g + 1) * tq)
            o_ref[:, hs] = o_t[:, qs_].T
            lse_ref[0, g:g + 1, :] = lse[:, qs_]
            yg = y_t[:, qs_].T
            if y_prev is not None:
                yg = yg + yp_ref[:, hs]
            y_ref[:, hs] = yg.astype(y_ref.dtype)

    row_spec = pl.BlockSpec((1, GRP, tq), lambda h, i: (h, 0, i))
    qo_spec = pl.BlockSpec((tq, GRP * HD), lambda h, i: (i, h))
    ins = [q, k, vt]
    specs = [pl.BlockSpec((tq, GRP * HD), lambda h, i: (i, qcol0 + h)), pl.BlockSpec((1, Sk, k.shape[2]), lambda h, i: (h, 0, 0)),
             pl.BlockSpec((1, Sk // tk, HD, tk), lambda h, i: (h, 0, 0, 0))]
    if mode == "sel":
        assert tq == tk
        ins.append(selneg)
        specs.append(pl.BlockSpec((1, tq, selneg.shape[2]), lambda h, i: (h, i, 0)))
    ins.append(gate)
    specs.append(row_spec)
    if y_prev is not None:
        ins.append(y_prev)
        specs.append(qo_spec)
    return pl.pallas_call(
        body, name=name, grid=(N_KV, S // tq), in_specs=specs, out_specs=[qo_spec, row_spec, qo_spec],
        out_shape=[jax.ShapeDtypeStruct((S, ATT_WIDTH), F32), jax.ShapeDtypeStruct((N_KV, GRP, S), F32),
                   jax.ShapeDtypeStruct((S, ATT_WIDTH), y_dtype)],
        scratch_shapes=[pltpu.VMEM((1, R), F32), pltpu.VMEM((1, R), F32), pltpu.VMEM((HD, R), F32), pltpu.VMEM((2, tk, R), F32)],
        compiler_params=_cp(("parallel", "arbitrary")))(*ins)


def _attn_bwd(q, qcol0, k, kt, v, o, lse, dy, dycol0, gate, mode, selneg, name):
    S, Sk = q.shape[0], k.shape[1]
    tq, tk = _attn_cfg(S, Sk, mode)
    R = GRP * tq

    def body(*refs):
        if mode == "sel":
            (q_ref, k_ref, kt_ref, v_ref, o_ref, lse_ref, dy_ref, gate_ref, sel_ref, dq_ref, dk_ref, dv_ref, dg_ref, dq_scr, s_scr,
             dp_scr) = refs
        else:
            q_ref, k_ref, kt_ref, v_ref, o_ref, lse_ref, dy_ref, gate_ref, dq_ref, dk_ref, dv_ref, dg_ref, dq_scr, s_scr, dp_scr = refs

        @pl.when(pl.program_id(1) == 0)
        def _():
            dk_ref[...] = jnp.zeros_like(dk_ref)
            dv_ref[...] = jnp.zeros_like(dv_ref)

        q0 = pl.program_id(1) * tq
        qs = _scaled_queries(q_ref, tq)
        dys = _stack_heads(dy_ref, tq)
        gv = _sigmoid(_head_rows(gate_ref))
        dy_o = _dot(jnp.ones((8, HD), F32), dys * _stack_heads(o_ref, tq), "nt", split="b")[0:1, :]
        delta = gv * dy_o
        dgate = dy_o * (gv * (1.0 - gv))
        for g in range(GRP):
            dg_ref[0, g:g + 1, :] = dgate[:, g * tq:(g + 1) * tq]
        lsev = _head_rows(lse_ref)
        dos = (dys * jnp.broadcast_to(gv, (8, R)).T[:, 0:1]).astype(_MXU)
        dq_scr[...] = jnp.zeros_like(dq_scr)
        qk = _sel_operands(qs, sel_ref) if mode == "sel" else qs

        def produce(kb, masked, slot):
            k0 = _block_start(kb, tk)
            s = _dot(k_ref[0, pl.ds(k0, tk), :], qk, "nt")
            if masked:
                s = s + _attn_bias(mode, q0, k0, tq, tk)[0]
            s_scr[slot] = s
            dp_scr[slot] = _dot(v_ref[0, pl.ds(k0, tk), :], dos, "nt")

        def consume(kb, slot):
            k0 = _block_start(kb, tk)
            p = jnp.exp(s_scr[slot] - lsev)
            if mode == "cmp":
                p = p * _attn_bias(mode, q0, 0, tq, tk)[1]
            ds = p * (dp_scr[slot] - delta)
            dq_scr[...] += _dot(kt_ref[0, kb], ds, "nn")
            dk_ref[0, pl.ds(k0, tk), :] += _dot(ds, qs, "nn")
            dv_ref[0, pl.ds(k0, tk), :] += _dot(p, dos, "nn")

        _pipelined_key_blocks(mode, q0, tq, tk, produce, consume)
        for g in range(GRP):
            dq_ref[:, g * HD:(g + 1) * HD] = (dq_scr[:, g * tq:(g + 1) * tq] * SCALE).T

    kv_spec = pl.BlockSpec((1, Sk, HD), lambda h, i: (h, 0, 0))
    qo_spec = pl.BlockSpec((tq, GRP * HD), lambda h, i: (i, h))
    row_spec = pl.BlockSpec((1, GRP, tq), lambda h, i: (h, 0, i))
    ins = [q, k, kt, v, o, lse, dy, gate]
    specs = [pl.BlockSpec((tq, GRP * HD), lambda h, i: (i, qcol0 + h)), pl.BlockSpec((1, Sk, k.shape[2]), lambda h, i: (h, 0, 0)),
             pl.BlockSpec((1, Sk // tk, HD, tk), lambda h, i: (h, 0, 0, 0)), kv_spec, qo_spec, row_spec,
             pl.BlockSpec((tq, GRP * HD), lambda h, i: (i, dycol0 + h)), row_spec]
    if mode == "sel":
        assert tq == tk
        ins.append(selneg)
        specs.append(pl.BlockSpec((1, tq, selneg.shape[2]), lambda h, i: (h, i, 0)))
    return pl.pallas_call(
        body, name=name, grid=(N_KV, S // tq), in_specs=specs, out_specs=[qo_spec, kv_spec, kv_spec, row_spec],
        out_shape=[jax.ShapeDtypeStruct((S, ATT_WIDTH), F32), jax.ShapeDtypeStruct((N_KV, Sk, HD), F32),
                   jax.ShapeDtypeStruct((N_KV, Sk, HD), F32), jax.ShapeDtypeStruct((N_KV, GRP, S), F32)],
        scratch_shapes=[pltpu.VMEM((HD, R), F32), pltpu.VMEM((2, tk, R), F32), pltpu.VMEM((2, tk, R), F32)],
        compiler_params=_cp(("parallel", "arbitrary")))(*ins)


def _select(q, qcol0, k_cmp, lse):
    S, NC = q.shape[0], k_cmp.shape[1]
    NB = S // SEL_BLOCK
    tq = _pick(S, (256, 128))
    ci = np.arange(NC)[None, :] * 16
    sj = np.arange(NB)[:, None] * SEL_BLOCK
    ov_t = np.clip(np.minimum(ci + 32, sj + SEL_BLOCK) - np.maximum(ci, sj), 0, None) / 32.0
    ov_t[:, NC - 1] = 0.0
    ov_t = jnp.asarray(ov_t, F32)

    def body(q_ref, k_ref, lse_ref, ov_ref, sel_ref):
        q0 = pl.program_id(1) * tq
        bias, okf = _attn_bias("cmp", q0, 0, tq, NC)
        lsev = _head_rows(lse_ref)
        p = jnp.exp(_dot(k_ref[0], _scaled_queries(q_ref, tq), "nt") + bias - lsev) * okf
        imp4 = _dot(ov_ref[...], p, "nn")
        imp = imp4[:, 0:tq] + imp4[:, tq:2 * tq] + imp4[:, 2 * tq:3 * tq] + imp4[:, 3 * tq:4 * tq]
        blk = lax.broadcasted_iota(jnp.int32, (NB, tq), 0)
        cur = lax.shift_right_logical(q0 + lax.broadcasted_iota(jnp.int32, (NB, tq), 1), 6)
        imp = jnp.where((blk == 0) | (blk == cur) | (blk == cur - 1), FORCE, imp)
        imp = jnp.where(blk <= cur, imp, -1.0)
        rank = jnp.zeros((NB, tq), F32)
        for j in range(NB):
            row = imp[j:j + 1, :]
            ahead = (row > imp) | ((row == imp) & (blk > j))
            rank = rank + ahead.astype(F32)
        chosen = (rank < float(N_SELECT)) & (imp >= 0.0)
        sel_ref[0] = jnp.where(chosen, 0.0, NEG).T.astype(sel_ref.dtype)

    return pl.pallas_call(
        body, name="select_blocks", grid=(N_KV, S // tq),
        in_specs=[pl.BlockSpec((tq, GRP * HD), lambda h, i: (i, qcol0 + h)), pl.BlockSpec((1, NC, HD), lambda h, i: (h, 0, 0)),
                  pl.BlockSpec((1, GRP, tq), lambda h, i: (h, 0, i)), pl.BlockSpec((NB, NC), lambda h, i: (0, 0))],
        out_specs=pl.BlockSpec((1, tq, NB), lambda h, i: (h, i, 0)),
        out_shape=jax.ShapeDtypeStruct((N_KV, S, NB), _MXU), compiler_params=_cp(("parallel", "parallel")))(q, k_cmp, lse, ov_t)


def _to_rows16(x):
    S = x.shape[0]
    return x.reshape(S // 16, 16, N_KV, HD).transpose(2, 0, 1, 3).reshape(N_KV, S // 16, 16 * HD)


def _from_rows16(r):
    NC = r.shape[1]
    return r.reshape(N_KV, NC, 16, HD).transpose(1, 2, 0, 3).reshape(NC * 16, N_KV * HD)


DT_COL0 = SSD_WIDTH + CONV_CH
GATE_IN_COL0 = D_IN - 3 * N_HEADS


SHARD_IN = D_IN // N_DEV


def _orig_cols(ref, c0, width):
    pieces, c = [], c0
    while c < c0 + width:
        d, off = divmod(c, SHARD_IN)
        w = min(SHARD_IN - off, c0 + width - c)
        pieces.append(ref[d, :, off:off + w])
        c += w
    return pieces[0] if len(pieces) == 1 else jnp.concatenate(pieces, axis=1)


def _cols_from_slabs(slabs):
    _, R, c = slabs.shape
    tr = _pick(R, (256, 128))

    def body(s_ref, o_ref):
        for t in range(N_DEV * c // LANE):
            pieces, col = [], t * LANE
            while col < (t + 1) * LANE:
                d, off = divmod(col, c)
                w = min(c - off, (t + 1) * LANE - col)
                pieces.append(s_ref[d, :, off:off + w])
                col += w
            o_ref[:, t * LANE:(t + 1) * LANE] = pieces[0] if len(pieces) == 1 else jnp.concatenate(pieces, axis=1)

    return pl.pallas_call(
        body, name="cols_from_slabs", grid=(R // tr,), in_specs=[pl.BlockSpec((N_DEV, tr, c), lambda i: (0, i, 0))],
        out_specs=pl.BlockSpec((tr, N_DEV * c), lambda i: (i, 0)), out_shape=jax.ShapeDtypeStruct((R, N_DEV * c), slabs.dtype),
        compiler_params=_cp(("parallel",)))(slabs)


def _slabs_from_cols(x):
    R, c = x.shape[0], x.shape[1] // N_DEV
    tr = _pick(R, (256, 128))

    def body(x_ref, o_ref):
        for d in range(N_DEV):
            o_ref[d] = x_ref[:, d * c:(d + 1) * c]

    return pl.pallas_call(
        body, name="slabs_from_cols", grid=(R // tr,), in_specs=[pl.BlockSpec((tr, N_DEV * c), lambda i: (i, 0))],
        out_specs=pl.BlockSpec((N_DEV, tr, c), lambda i: (0, i, 0)), out_shape=jax.ShapeDtypeStruct((N_DEV, R, c), x.dtype),
        compiler_params=_cp(("parallel",)))(x)


def _w_in_from_slabs(slabs):
    D = slabs.shape[1]
    tr = _pick(D, (256, 128))

    def body(s_ref, main_ref, small_ref):
        for t in range(W_MAIN // LANE):
            c = t * LANE
            main_ref[:, c:c + LANE] = _orig_cols(s_ref, c if c < DT_COL0 else c + SSD_HEADS, LANE)
        small_ref[...] = jnp.concatenate(
            [_orig_cols(s_ref, DT_COL0, SSD_HEADS), _orig_cols(s_ref, GATE_IN_COL0, 3 * N_HEADS),
             jnp.zeros((tr, W_SMALL - SSD_HEADS - 3 * N_HEADS), small_ref.dtype)], axis=1)

    return pl.pallas_call(
        body, name="w_in_layout", grid=(D // tr,), in_specs=[pl.BlockSpec((N_DEV, tr, SHARD_IN), lambda i: (0, i, 0))],
        out_specs=[pl.BlockSpec((tr, W_MAIN), lambda i: (i, 0)), pl.BlockSpec((tr, W_SMALL), lambda i: (i, 0))],
        out_shape=[jax.ShapeDtypeStruct((D, W_MAIN), slabs.dtype), jax.ShapeDtypeStruct((D, W_SMALL), slabs.dtype)],
        compiler_params=_cp(("parallel",)))(slabs)


def _w_in_to_slabs(main, small):
    D = main.shape[0]
    tr = _pick(D, (256, 128))
    ranges = [(0, DT_COL0, 0, 0), (DT_COL0, DT_COL0 + SSD_HEADS, 1, 0), (DT_COL0 + SSD_HEADS, GATE_IN_COL0, 0, DT_COL0),
              (GATE_IN_COL0, D_IN, 1, SSD_HEADS)]

    def body(main_ref, small_ref, o_ref):
        srcs = (main_ref, small_ref)
        for d in range(N_DEV):
            lo, hi = d * SHARD_IN, (d + 1) * SHARD_IN
            pieces = []
            for start, stop, which, s0 in ranges:
                a, b = max(lo, start), min(hi, stop)
                if a < b:
                    pieces.append(srcs[which][:, s0 + a - start:s0 + b - start].astype(o_ref.dtype))
            o_ref[d] = pieces[0] if len(pieces) == 1 else jnp.concatenate(pieces, axis=1)

    return pl.pallas_call(
        body, name="w_in_grad_layout", grid=(D // tr,),
        in_specs=[pl.BlockSpec((tr, W_MAIN), lambda i: (i, 0)), pl.BlockSpec((tr, W_SMALL), lambda i: (i, 0))],
        out_specs=pl.BlockSpec((N_DEV, tr, SHARD_IN), lambda i: (0, i, 0)),
        out_shape=jax.ShapeDtypeStruct((N_DEV, D, SHARD_IN), main.dtype), compiler_params=_cp(("parallel",)))(main, small)


QB, KCB, VCB, KSB, VSB, KWB, VWB = 10, 14, 15, 16, 17, 18, 19


def _col256(a, b):
    return a[:, b * 256:(b + 1) * 256]


_EARLY = ["w_in", "cmp_w1_k", "cmp_w1_v"]
_LATE = ["w_out", "w_gate", "w_up", "w_down"]
_FFN = ["w_down", "w_gate", "w_up"]
_MID = ["w_out"]
_LAST = ["cmp_w1_k", "cmp_w1_v", "w_in"]


def _local_step(x, tgt, p, late_weights=None, grads_ready=None):
    S = x.shape[0]
    cos, sin = _rope_tables(S)

    u, rs1 = _rms_fwd(x, p["attn_norm_w"], "attn_norm")
    proj = _mm(u, p["w_main"], "nn", F32, "in_proj", after=p.get("before_in_proj"))
    proj_small = _mm(u, p["w_small"], "nn", F32, "in_proj_small")
    xa = _conv_fwd(proj, p["conv_w"], p["conv_b"])
    y_ssd, y_pre, rs_ssd, hs = _ssd_fwd(proj, proj_small, xa, p["dt_bias"], p["a_log"], p["d_skip"], p["ssd_norm_w"])

    q_rot = _rope([proj], QB, ATT_WIDTH, cos, sin, 1.0, _MXU, "rope_q")
    kv = _kv_prep(proj, cos, sin, _attn_cfg(S, S, "sel")[1])
    rk, rv = _to_rows16(_col256(proj, KCB)), _to_rows16(_col256(proj, VCB))
    k_cmp, hid_k = _compress_fwd(rk, p["cmp_pe_k"], p["cmp_w1_k"], p["cmp_w2_k"])
    v_cmp, hid_v = _compress_fwd(rv, p["cmp_pe_v"], p["cmp_w1_v"], p["cmp_w2_v"])
    n_cmp = k_cmp.shape[1]

    gates = proj_small[:, SSD_HEADS:SSD_HEADS + 3 * N_HEADS].reshape(S, N_KV, GRP, 3).transpose(3, 1, 2, 0)
    o_cmp, lse_cmp, y_att = _attn_fwd(proj, QB, k_cmp, _blocked_t(v_cmp, n_cmp), "cmp", None, gates[0], None, F32, "attn_cmp_fwd")
    sel = _select(proj, QB, k_cmp, lse_cmp)
    o_sel, lse_sel, y_att = _attn_fwd(q_rot, 0, kv["ks_ext"], kv["vs_t"], "sel", sel, gates[1], y_att, F32, "attn_sel_fwd")
    o_win, lse_win, y_att = _attn_fwd(q_rot, 0, kv["kw"], kv["vw_t"], "win", None, gates[2], y_att, _MXU, "attn_win_fwd")

    if late_weights is not None:
        p = {**p, **late_weights(y_att)}
    mixed = jnp.concatenate([y_ssd, y_att], axis=1)
    h1 = _mm(mixed, p["w_out"], "nn", F32, "out_proj", res=x)
    v, rs_ffn = _rms_fwd(h1, p["ffn_norm_w"], "ffn_norm")
    gt, up, act = _ffn_up(v, p["w_gate"], p["w_up"])
    h2 = _mm(act, p["w_down"], "nn", F32, "ffn_down", res=h1)
    loss, dh2, dh2b, d_final_w = _final_loss(h2, p["final_norm_w"], tgt)

    def ready(names):
        return None if grads_ready is None else grads_ready(names, g)

    g = {"final_norm_w": d_final_w}
    g["w_down"] = _mm(act, dh2b, "tn", _MXU, "dw_down")
    dgt, dup = _ffn_dact(dh2b, p["w_down"], gt, up)
    g["w_gate"] = _mm(v, dgt, "tn", _MXU, "dw_gate")
    g["w_up"] = _mm(v, dup, "tn", _MXU, "dw_up")
    dv = _ffn_dv(dgt, dup, p["w_gate"], p["w_up"], ready(_FFN))
    dh1, dh1b, g["ffn_norm_w"] = _rms_bwd(dv, h1, rs_ffn, p["ffn_norm_w"], dh2, "ffn_norm_bwd")
    g["w_out"] = _mm(mixed, dh1b, "tn", _MXU, "dw_out")
    dmixed = _mm(dh1b, p["w_out"], "nt", F32, "dmixed", after=ready(_MID))

    dz, dxa, ddtr, g["dt_bias"], g["a_log"], g["d_skip"], g["ssd_norm_w"] = _ssd_bwd(
        dmixed, proj, proj_small, xa, y_pre, rs_ssd, hs, p["dt_bias"], p["a_log"], p["d_skip"], p["ssd_norm_w"])
    dxbc, g["conv_w"], g["conv_b"] = _conv_bwd(proj, p["conv_w"], p["conv_b"], dxa)

    dyb = SSD_WIDTH // (GRP * HD)
    dq_cmp, dk_cmp, dv_cmp, dg_cmp = _attn_bwd(proj, QB, k_cmp, _blocked_t(k_cmp, n_cmp), v_cmp, o_cmp, lse_cmp, dmixed, dyb,
                                               gates[0], "cmp", None, "attn_cmp_bwd")
    dq_sel, dks, dvs, dg_sel = _attn_bwd(q_rot, 0, kv["ks_ext"], kv["ks_t"], kv["vs"], o_sel, lse_sel, dmixed, dyb, gates[1], "sel",
                                         sel, "attn_sel_bwd")
    dq_win, dkw, dvw, dg_win = _attn_bwd(q_rot, 0, kv["kw"], kv["kw_t"], kv["vw"], o_win, lse_win, dmixed, dyb, gates[2], "win", None,
                                         "attn_win_bwd")
    dgate = jnp.stack([dg_cmp, dg_sel, dg_win]).transpose(3, 1, 2, 0).reshape(S, 3 * N_HEADS)
    drk, g["cmp_w1_k"], g["cmp_w2_k"], g["cmp_pe_k"] = _compress_bwd(rk, p["cmp_pe_k"], p["cmp_w1_k"], p["cmp_w2_k"], hid_k, dk_cmp)
    drv, g["cmp_w1_v"], g["cmp_w2_v"], g["cmp_pe_v"] = _compress_bwd(rv, p["cmp_pe_v"], p["cmp_w1_v"], p["cmp_w2_v"], hid_v, dv_cmp)
    dq = _rope([dq_sel, dq_win], 0, ATT_WIDTH, cos, sin, -1.0, _MXU, "rope_dq", extra=(dq_cmp, 0))
    dkv = _dkv_post(dks, dvs, dkw, dvw, cos, sin)
    dproj = jnp.concatenate([dz, dxbc, dq] + [t.astype(_MXU) for t in (_from_rows16(drk), _from_rows16(drv))] + [dkv], axis=1)
    dsmall = jnp.concatenate([ddtr, dgate, jnp.zeros((S, W_SMALL - SSD_HEADS - 3 * N_HEADS), F32)], axis=1).astype(_MXU)
    g["w_main"] = _mm(u, dproj, "tn", _MXU, "dw_in")
    g["w_small"] = _mm(u, dsmall, "tn", F32, "dw_in_small")
    du = _mm(dproj, p["w_main"], "nt", F32, "du_main", after=ready(_LAST))
    du = _mm(dsmall, p["w_small"], "nt", F32, "du_small", res=du)
    grad_x, _, g["attn_norm_w"] = _rms_bwd(du, x, rs1, p["attn_norm_w"], dh1, "attn_norm_bwd")
    return loss, grad_x, g


MESH_ID = pl.DeviceIdType.MESH


def _my_coords():
    return lax.axis_index("x"), lax.axis_index("y"), lax.axis_index("c")


def _flat_id(px, py, pc):
    return 4 * px + 2 * py + pc


def _peer(k):
    mx, my, mc = _my_coords()
    return (1 - mx if k & 4 else mx, 1 - my if k & 2 else my, 1 - mc if k & 1 else mc)


def _exchange(arrs, scatter, name, after=()):
    n, na = len(arrs), len(after)
    scatter = [scatter] * n if isinstance(scatter, bool) else list(scatter)

    def body(*refs):
        ins, outs = refs[:n], refs[n + na:2 * n + na]
        send_sems, recv_sems, local_sems = refs[2 * n + na:]
        me = _flat_id(*_my_coords())
        copies = []
        for i in range(n):
            src_me = ins[i].at[me] if scatter[i] else ins[i]
            local = pltpu.make_async_copy(src_me, outs[i].at[me], local_sems.at[i])
            local.start()
            copies.append(local)
        for k in range(1, N_DEV):
            peer = _peer(k)
            for i in range(n):
                src = ins[i].at[_flat_id(*peer)] if scatter[i] else ins[i]
                cp = pltpu.make_async_remote_copy(src_ref=src, dst_ref=outs[i].at[me], send_sem=send_sems.at[i * 7 + k - 1],
                                                  recv_sem=recv_sems.at[i * 7 + k - 1], device_id=peer, device_id_type=MESH_ID)
                cp.start()
                copies.append(cp)
        for cp in copies:
            cp.wait()

    any_spec = pl.BlockSpec(memory_space=pl.ANY)
    out_shape = [jax.ShapeDtypeStruct(a.shape if sc else (N_DEV,) + a.shape, a.dtype) for a, sc in zip(arrs, scatter)]
    return pl.pallas_call(
        body, name=name, in_specs=[any_spec] * (n + na), out_specs=[any_spec] * n, out_shape=out_shape,
        scratch_shapes=[pltpu.SemaphoreType.DMA((n * 7,)), pltpu.SemaphoreType.DMA((n * 7,)), pltpu.SemaphoreType.DMA((n,))],
        compiler_params=pltpu.CompilerParams(has_side_effects=True))(*arrs, *after)


def _gather_two_level(arrs, name):
    n = len(arrs)

    def body(*refs):
        ins, outs = refs[:n], refs[n:2 * n]
        send_sems, recv_sems, local_sems = refs[2 * n:]
        x, y, c = _my_coords()
        me, sibling = (x, y, c), (x, y, 1 - c)
        chips = [(1 - x, y), (x, 1 - y), (1 - x, 1 - y)]

        def copy(i, k, block, to, src=None):
            slot = outs[i].at[_flat_id(*block)]
            return pltpu.make_async_remote_copy(src_ref=slot if src is None else src, dst_ref=slot, send_sem=send_sems.at[i * 7 + k],
                                                recv_sem=recv_sems.at[i * 7 + k], device_id=to, device_id_type=MESH_ID)

        mine = [pltpu.make_async_copy(ins[i], outs[i].at[_flat_id(*me)], local_sems.at[i]) for i in range(n)]
        for cp in mine:
            cp.start()
        first = []
        for j, chip in enumerate(chips):
            first += [copy(i, 1 + j, me, (*chip, c), src=ins[i]) for i in range(n)]
        first += [copy(i, 0, me, sibling, src=ins[i]) for i in range(n)]
        for cp in first:
            cp.start()
        passed = []
        for j, chip in enumerate(chips):
            for i in range(n):
                copy(i, 1 + j, (*chip, c), me).wait_recv()
                passed.append(copy(i, 4 + j, (*chip, c), sibling))
                passed[-1].start()
        for i in range(n):
            copy(i, 0, sibling, me).wait_recv()
        for j, chip in enumerate(chips):
            for i in range(n):
                copy(i, 4 + j, (*chip, 1 - c), me).wait_recv()
        for cp in first + passed:
            cp.wait_send()
        for cp in mine:
            cp.wait()

    any_spec = pl.BlockSpec(memory_space=pl.ANY)
    return pl.pallas_call(
        body, name=name, in_specs=[any_spec] * n, out_specs=[any_spec] * n,
        out_shape=[jax.ShapeDtypeStruct((N_DEV,) + a.shape, a.dtype) for a in arrs],
        scratch_shapes=[pltpu.SemaphoreType.DMA((n * 7,)), pltpu.SemaphoreType.DMA((n * 7,)), pltpu.SemaphoreType.DMA((n,))],
        compiler_params=pltpu.CompilerParams(has_side_effects=True))(*arrs)


_HBM = pl.BlockSpec(memory_space=pltpu.HBM)
_SEM = pl.BlockSpec(memory_space=pltpu.SEMAPHORE)
_EFFECT = pltpu.SideEffectType.DATAFLOW_SIDE_EFFECTING


def _split_copies(ins, lands, send_sems, recv_sems, scatter):
    me = _flat_id(*_my_coords())
    out = []
    for k in range(1, N_DEV):
        peer = _peer(k)
        for i in range(len(ins)):
            src = ins[i].at[_flat_id(*peer)] if scatter else ins[i]
            out.append(pltpu.make_async_remote_copy(src_ref=src, dst_ref=lands[i].at[me], send_sem=send_sems.at[i * 7 + k - 1],
                                                    recv_sem=recv_sems.at[i * 7 + k - 1], device_id=peer, device_id_type=MESH_ID))
    return out


def _split_start(arrs, scatter, name, after=()):
    n, na = len(arrs), len(after)

    def body(*refs):
        for cp in _split_copies(refs[:n], refs[n:2 * n], refs[2 * n + na], refs[2 * n + na + 1], scatter):
            cp.start()
        refs[-1][...] = jnp.zeros_like(refs[-1])

    land_shapes = [a.shape if scatter else (N_DEV,) + a.shape for a in arrs]
    out_shape = ((pltpu.SemaphoreType.DMA((n * 7,)), pltpu.SemaphoreType.DMA((n * 7,)))
                 + tuple(pltpu.HBM(a.shape, a.dtype) for a in arrs) + tuple(pltpu.HBM(s, a.dtype) for s, a in zip(land_shapes, arrs))
                 + (jax.ShapeDtypeStruct((8, 128), F32),))
    operands = ([pltpu.with_memory_space_constraint(a, pltpu.HBM) for a in arrs]
                + [pltpu.with_memory_space_constraint(lax.empty(s, a.dtype), pltpu.HBM) for s, a in zip(land_shapes, arrs)])
    res = pl.pallas_call(
        body, name=name, out_shape=out_shape, in_specs=[_HBM] * (2 * n) + [pl.BlockSpec(memory_space=pl.ANY)] * na,
        out_specs=(_SEM, _SEM) + (_HBM,) * (2 * n) + (pl.BlockSpec(memory_space=pltpu.VMEM),),
        input_output_aliases={i: 2 + i for i in range(2 * n)},
        compiler_params=pltpu.CompilerParams(has_side_effects=_EFFECT))(*operands, *after)
    return dict(send=res[0], recv=res[1], ins=list(res[2:2 + n]), lands=list(res[2 + n:2 + 2 * n]), token=res[-1])


def _split_wait(st, scatter, after, name):
    n = len(st["ins"])

    def body(*refs):
        for cp in _split_copies(refs[:n], refs[n:2 * n], refs[2 * n], refs[2 * n + 1], scatter):
            cp.wait_send()
            cp.wait_recv()

    arrs = st["ins"] + st["lands"]
    res = pl.pallas_call(
        body, name=name, out_shape=tuple(pltpu.HBM(a.shape, a.dtype) for a in arrs),
        in_specs=[_HBM] * (2 * n) + [_SEM, _SEM] + [pl.BlockSpec(memory_space=pl.ANY)] * len(after), out_specs=(_HBM,) * (2 * n),
        input_output_aliases={i: i for i in range(2 * n)},
        compiler_params=pltpu.CompilerParams(has_side_effects=_EFFECT))(*arrs, st["send"], st["recv"], *after)
    me = _flat_id(*_my_coords())
    out = []
    for src, land in zip(res[:n], res[n:]):
        own = lax.dynamic_index_in_dim(src, me, 0, keepdims=True) if scatter else src[None]
        out.append(lax.dynamic_update_slice_in_dim(land, own, me, 0))
    return out


def _adam_step(p_ref, w_ref, m_ref, v_ref, g_ref, d_ref, nm_ref, nv_ref):
    g = p_ref[0].astype(F32)
    for j in range(1, p_ref.shape[0]):
        g = g + p_ref[j].astype(F32)
    g_ref[...] = g
    nm = ADAM_B1 * m_ref[...] + (1.0 - ADAM_B1) * g
    nv = ADAM_B2 * v_ref[...] + (1.0 - ADAM_B2) * (g * g)
    nm_ref[...] = nm
    nv_ref[...] = nv
    m_hat = nm / (1.0 - ADAM_B1 ** ADAM_STEP)
    v_hat = nv / (1.0 - ADAM_B2 ** ADAM_STEP)
    d_ref[...] = -ADAM_LR * (m_hat / (jnp.sqrt(v_hat) + ADAM_EPS) + ADAM_WD * w_ref[...])


def _adam_sum(parts, w, m, v, name):
    P, R, C = parts.shape
    tr = _pick(R, (256, 128, 64, 32, 8)) if C <= 1024 else _pick(R, (128, 64, 32, 8))
    blk = pl.BlockSpec((tr, C), lambda i: (i, 0))
    return pl.pallas_call(
        functools.partial(_adam_step), name=name, grid=(R // tr,),
        in_specs=[pl.BlockSpec((P, tr, C), lambda i: (0, i, 0)), blk, blk, blk],
        out_specs=[blk] * 4, out_shape=[jax.ShapeDtypeStruct((R, C), F32)] * 4, compiler_params=_cp(("parallel",)))(parts, w, m, v)


def _adam_small(loss_parts, parts, ws, ms, vs):
    n = len(parts)

    def body(*refs):
        loss_ref, ins, outs, total_ref = refs[0], refs[1:4 * n + 1], refs[4 * n + 1:-1], refs[-1]
        for i in range(n):
            _adam_step(ins[i], ins[n + i], ins[2 * n + i], ins[3 * n + i], *outs[4 * i:4 * i + 4])
        total = loss_ref[0]
        for d in range(1, N_DEV):
            total = total + loss_ref[d]
        total_ref[...] = total

    out_shape = [jax.ShapeDtypeStruct(w.shape, F32) for w in ws for _ in range(4)] + [jax.ShapeDtypeStruct(loss_parts.shape[1:], F32)]
    res = pl.pallas_call(body, name="adam_small", out_shape=out_shape)(loss_parts, *parts, *ws, *ms, *vs)
    return res[-1], [tuple(res[4 * i:4 * i + 4]) for i in range(n)]


_WEIGHTS = ["attn_norm_w", "w_in", "conv_w", "conv_b", "dt_bias", "a_log", "d_skip", "ssd_norm_w", "cmp_w1_k", "cmp_w2_k",
            "cmp_w1_v", "cmp_w2_v", "cmp_pe_k", "cmp_pe_v", "w_out", "ffn_norm_w", "w_gate", "w_up", "w_down", "final_norm_w"]
_BIG = ["w_in", "w_gate", "w_up", "w_down", "w_out", "cmp_w1_k", "cmp_w1_v"]
_COL_SHARDED = ("w_in", "w_gate", "w_up")
_REPLICATED = ["attn_norm_w", "conv_b", "dt_bias", "a_log", "d_skip", "ssd_norm_w", "cmp_pe_k", "cmp_pe_v", "ffn_norm_w",
               "final_norm_w"]
_SMALL_SHARDED = ["conv_w", "cmp_w2_k", "cmp_w2_v"]


def _cols_to_slabs(g):
    R = g.shape[0]
    return g.reshape(R, N_DEV, -1).transpose(1, 0, 2)


def _slabs_to_cols(s):
    return s.transpose(1, 0, 2).reshape(s.shape[1], -1)


def kernel(x, attn_norm_w, w_in, conv_w, conv_b, dt_bias, a_log, d_skip, ssd_norm_w, cmp_w1_k, cmp_w2_k, cmp_w1_v, cmp_w2_v, cmp_pe_k, cmp_pe_v, w_out, ffn_norm_w, w_gate, w_up, w_down, final_norm_w, loss_target, m_attn_norm_w, m_w_in, m_conv_w, m_conv_b, m_dt_bias, m_a_log, m_d_skip, m_ssd_norm_w, m_cmp_w1_k, m_cmp_w2_k, m_cmp_w1_v, m_cmp_w2_v, m_cmp_pe_k, m_cmp_pe_v, m_w_out, m_ffn_norm_w, m_w_gate, m_w_up, m_w_down, m_final_norm_w, v_attn_norm_w, v_w_in, v_conv_w, v_conv_b, v_dt_bias, v_a_log, v_d_skip, v_ssd_norm_w, v_cmp_w1_k, v_cmp_w2_k, v_cmp_w1_v, v_cmp_w2_v, v_cmp_pe_k, v_cmp_pe_v, v_w_out, v_ffn_norm_w, v_w_gate, v_w_up, v_w_down, v_final_norm_w):
    a = dict(locals())

    shard = {n: a[n][0].astype(_MXU) for n in _BIG}
    got = _gather_two_level([shard[n] for n in _EARLY] + [cmp_w2_k[0], cmp_w2_v[0], conv_w[0]], "gather_early")
    st_late = _split_start([shard[n] for n in _LATE], False, "gather_late_start", after=(got[0],))

    def assemble(n, t):
        return _cols_from_slabs(t) if n in _COL_SHARDED else t.reshape(-1, t.shape[-1])

    p = dict(attn_norm_w=attn_norm_w, conv_b=conv_b, dt_bias=dt_bias, a_log=a_log, d_skip=d_skip, ssd_norm_w=ssd_norm_w,
             cmp_pe_k=cmp_pe_k.reshape(1, -1), cmp_pe_v=cmp_pe_v.reshape(1, -1), ffn_norm_w=ffn_norm_w,
             final_norm_w=final_norm_w.reshape(1, -1))

    w_main, w_small = _w_in_from_slabs(got[0])
    p.update(before_in_proj=st_late["token"],
             w_main=w_main, w_small=w_small, cmp_w1_k=assemble("cmp_w1_k", got[1]), cmp_w1_v=assemble("cmp_w1_v", got[2]),
             cmp_w2_k=assemble("cmp_w2_k", got[3]).astype(_MXU), cmp_w2_v=assemble("cmp_w2_v", got[4]).astype(_MXU),
             conv_w=_slabs_to_cols(got[5]))

    def late_weights(after):
        got_late = _split_wait(st_late, False, (after,), "gather_late_wait")
        return {n: assemble(n, t) for n, t in zip(_LATE, got_late)}

    def slabs_of(g, n):
        if n == "w_in":
            return _w_in_to_slabs(g["w_main"], g["w_small"])
        return _slabs_from_cols(g[n]) if n in _COL_SHARDED else g[n].reshape(N_DEV, -1, g[n].shape[-1])

    started = []

    def grads_ready(names, g):
        started.append((names, _split_start([slabs_of(g, n) for n in names], True, "scatter_grads_start_%d" % len(started))))
        return started[-1][1]["token"]

    loss_part, grad_x, g = _local_step(x[0], loss_target[0], p, late_weights, grads_ready)

    out, after = {}, (started[-1][1]["token"],)
    for i, (names, st) in enumerate(started):
        if i == len(started) - 1:
            after = after + (grad_x,)
        received = _split_wait(st, True, after, "scatter_grads_wait_%d" % i)
        for n, parts in zip(names, received):
            out[n] = _adam_sum(parts, a[n][0], a["m_" + n][0], a["v_" + n][0], "adam_" + n)
        after = (out[names[-1]][0],)

    small_names = _REPLICATED + _SMALL_SHARDED
    partials = [g[n] for n in _REPLICATED] + [_cols_to_slabs(g["conv_w"])] + [
        g[n].reshape(N_DEV, -1, g[n].shape[-1]) for n in ("cmp_w2_k", "cmp_w2_v")]
    gathered = _exchange([loss_part] + partials, [False] * (1 + len(_REPLICATED)) + [True] * len(_SMALL_SHARDED),
                         "exchange_small_grads", after=(received[0],))
    shapes2d = [t.shape[1:] for t in gathered[1:]]
    loss, res_small = _adam_small(gathered[0], gathered[1:],
                                  *[[a[pre + n].reshape(s) for n, s in zip(small_names, shapes2d)] for pre in ("", "m_", "v_")])
    for n, r in zip(small_names, res_small):
        out[n] = r

    outs = [loss[0, 0], grad_x[None]]
    for j in range(4):
        for n in _WEIGHTS:
            outs.append(out[n][j].reshape(a[n].shape))
    return tuple(outs)
```

```python
import functools

import numpy as np
import jax
import jax.numpy as jnp
from jax import lax
from jax.experimental import pallas as pl
from jax.experimental.pallas import tpu as pltpu

F32 = jnp.float32
_MXU = jnp.bfloat16

N_DEV = 8
D_MODEL = 2048
SSD_WIDTH = 1024
ATT_WIDTH = 1024
SSD_HEADS = 16
SSD_P = 64
SSD_N = 128
SSD_L = 128
SSD_G = 2
CONV_CH = 1536
CONV_K = 4
HD = 64
N_HEADS = 16
N_KV = 4
GRP = 4
CMP_HID = 256
SEL_BLOCK = 64
N_SELECT = 16
WINDOW = 512
ROPE_DIM = 16
ROPE_THETA = 500000.0
D_FF = 5632
EPS = 1e-6
NEG = -1e30
FORCE = 1e4
SCALE = HD ** -0.5
D_IN = 5184
W_MAIN = 5120
W_SMALL = 128
VMEM_LIMIT = 52 * 1024 * 1024

ADAM_LR, ADAM_B1, ADAM_B2, ADAM_EPS, ADAM_WD, ADAM_STEP = 0.001, 0.9, 0.999, 1e-08, 0.01, 10


def _pick(n, cands):
    for c in cands:
        if n % c == 0:
            return c
    return n


def _cp(sem=None):
    return pltpu.CompilerParams(dimension_semantics=sem, vmem_limit_bytes=VMEM_LIMIT)


def _sigmoid(x):
    return 1.0 / (1.0 + jnp.exp(-x))


def _dot(a, b, dims, split=None):
    dn = {"nn": (((1,), (0,)), ((), ())), "nt": (((1,), (1,)), ((), ())), "tn": (((0,), (0,)), ((), ()))}[dims]
    mm = lambda x, y: lax.dot_general(x.astype(_MXU), y.astype(_MXU), dn, preferred_element_type=F32)
    if split is None:
        return mm(a, b)
    x = (a if split == "a" else b).astype(F32)
    hi = x.astype(_MXU)
    lo = x - hi.astype(F32)
    return mm(hi, b) + mm(lo, b) if split == "a" else mm(a, hi) + mm(a, lo)


LANE = 128
MM_TILE = 1024
MM_K_WHOLE = 2048
MM_K_STEP = 2816
TN_ACC_ELEMS = 3 * 2 ** 20
TN_K_STEP = 512


def _largest_tile(n, cap):
    if n <= cap:
        return n
    best = LANE
    for t in range(LANE, cap + 1, LANE):
        if n % t == 0:
            best = t
    return best


def _mm_tiles(mode, M, N, K):
    if mode == "tn":
        tm = _largest_tile(M, 2 * MM_TILE)
        return tm, _largest_tile(N, TN_ACC_ELEMS // tm), _largest_tile(K, TN_K_STEP)
    tk = K if K <= MM_K_WHOLE else _largest_tile(K, MM_K_STEP)
    return _largest_tile(M, MM_TILE), _largest_tile(N, MM_TILE), tk


def _mm(a, b, mode, out_dtype, name, res=None, after=None):
    if mode == "nn":
        (M, K), N = a.shape, b.shape[1]
    elif mode == "nt":
        (M, K), N = a.shape, b.shape[0]
    else:
        (K, M), N = a.shape, b.shape[1]
    tm, tn, tk = _mm_tiles(mode, M, N, K)
    nk = K // tk
    a_spec = pl.BlockSpec((tk, tm), lambda i, j, k: (k, i)) if mode == "tn" else pl.BlockSpec((tm, tk), lambda i, j, k: (i, k))
    b_spec = pl.BlockSpec((tn, tk), lambda i, j, k: (j, k)) if mode == "nt" else pl.BlockSpec((tk, tn), lambda i, j, k: (k, j))
    o_spec = pl.BlockSpec((tm, tn), lambda i, j, k: (i, j))

    def finish(r, r_ref, o_ref):
        if res is not None:
            r = r + r_ref[...].astype(F32)
        o_ref[...] = r.astype(out_dtype)

    def body_one_step(*refs):
        a_ref, b_ref, o_ref = refs[0], refs[1], refs[-1]
        finish(_dot(a_ref[...], b_ref[...], mode), refs[2], o_ref)

    def body(*refs):
        a_ref, b_ref, o_ref, acc = refs[0], refs[1], refs[-2], refs[-1]
        k = pl.program_id(2)

        @pl.when(k == 0)
        def _():
            acc[...] = jnp.zeros_like(acc)

        acc[...] += _dot(a_ref[...], b_ref[...], mode)

        @pl.when(k == nk - 1)
        def _():
            finish(acc[...], refs[2], o_ref)

    ins, specs = [a, b], [a_spec, b_spec]
    if res is not None:
        ins.append(res)
        specs.append(o_spec)
    if after is not None:
        ins.append(after)
        specs.append(pl.BlockSpec(memory_space=pl.ANY))
    return pl.pallas_call(
        body_one_step if nk == 1 else body, name=name, grid=(M // tm, N // tn, nk), in_specs=specs, out_specs=o_spec,
        out_shape=jax.ShapeDtypeStruct((M, N), out_dtype), scratch_shapes=[] if nk == 1 else [pltpu.VMEM((tm, tn), F32)],
        compiler_params=_cp(("parallel", "parallel", "arbitrary")))(*ins)


def _ffn_up(v, w_gate, w_up):
    S, D = v.shape
    F = w_gate.shape[1]
    tm, tn = _largest_tile(S, MM_TILE), _largest_tile(F, MM_TILE // 2)

    def body(v_ref, wg_ref, wu_ref, gt_ref, up_ref, act_ref):
        vv = v_ref[...]
        g = _dot(vv, wg_ref[...], "nn")
        u = _dot(vv, wu_ref[...], "nn")
        gt_ref[...] = g
        up_ref[...] = u
        act_ref[...] = (g * _sigmoid(g) * u).astype(act_ref.dtype)

    o_spec = pl.BlockSpec((tm, tn), lambda i, j: (i, j))
    w_spec = pl.BlockSpec((D, tn), lambda i, j: (0, j))
    return pl.pallas_call(
        body, name="ffn_up", grid=(S // tm, F // tn),
        in_specs=[pl.BlockSpec((tm, D), lambda i, j: (i, 0)), w_spec, w_spec], out_specs=[o_spec, o_spec, o_spec],
        out_shape=[jax.ShapeDtypeStruct((S, F), F32), jax.ShapeDtypeStruct((S, F), F32), jax.ShapeDtypeStruct((S, F), _MXU)],
        compiler_params=_cp(("parallel", "parallel")))(v, w_gate, w_up)


def _ffn_dv(dgt, dup, w_gate, w_up, after):
    S, F = dgt.shape
    D = w_gate.shape[0]
    tm, tn, _ = _mm_tiles("nt", S, D, F)
    tk = _largest_tile(F, MM_K_STEP // 2)
    nk = F // tk

    def body(g_ref, u_ref, wg_ref, wu_ref, *rest):
        o_ref, acc = rest[-2], rest[-1]
        k = pl.program_id(2)

        @pl.when(k == 0)
        def _():
            acc[...] = jnp.zeros_like(acc)

        acc[...] += _dot(g_ref[...], wg_ref[...], "nt") + _dot(u_ref[...], wu_ref[...], "nt")

        @pl.when(k == nk - 1)
        def _():
            o_ref[...] = acc[...]

    a_spec = pl.BlockSpec((tm, tk), lambda i, j, k: (i, k))
    w_spec = pl.BlockSpec((tn, tk), lambda i, j, k: (j, k))
    ins, specs = [dgt, dup, w_gate, w_up], [a_spec, a_spec, w_spec, w_spec]
    if after is not None:
        ins.append(after)
        specs.append(pl.BlockSpec(memory_space=pl.ANY))
    return pl.pallas_call(
        body, name="ffn_dv", grid=(S // tm, D // tn, nk), in_specs=specs, out_specs=pl.BlockSpec((tm, tn), lambda i, j, k: (i, j)),
        out_shape=jax.ShapeDtypeStruct((S, D), F32), scratch_shapes=[pltpu.VMEM((tm, tn), F32)],
        compiler_params=_cp(("parallel", "parallel", "arbitrary")))(*ins)


def _ffn_dact(dh2, w_down, gt, up):
    S, D = dh2.shape
    F = w_down.shape[0]
    tm, tn = _largest_tile(S, MM_TILE), _largest_tile(F, MM_TILE // 2)

    def body(d_ref, w_ref, gt_ref, up_ref, dg_ref, du_ref):
        da, g, u = _dot(d_ref[...], w_ref[...], "nt"), gt_ref[...], up_ref[...]
        s = _sigmoid(g)
        dg_ref[...] = (da * u * (s * (1.0 + g * (1.0 - s)))).astype(dg_ref.dtype)
        du_ref[...] = (da * (g * s)).astype(du_ref.dtype)

    o_spec = pl.BlockSpec((tm, tn), lambda i, j: (i, j))
    return pl.pallas_call(
        body, name="ffn_dact", grid=(S // tm, F // tn),
        in_specs=[pl.BlockSpec((tm, D), lambda i, j: (i, 0)), pl.BlockSpec((tn, D), lambda i, j: (j, 0)), o_spec, o_spec],
        out_specs=[o_spec, o_spec],
        out_shape=[jax.ShapeDtypeStruct((S, F), _MXU), jax.ShapeDtypeStruct((S, F), _MXU)],
        compiler_params=_cp(("parallel", "parallel")))(dh2, w_down, gt, up)


def _rms_fwd(x, w, name):
    S, D = x.shape
    tr = _pick(S, (256, 128))

    def body(x_ref, w_ref, xn_ref, rs_ref):
        xv = x_ref[...]
        rs = lax.rsqrt(jnp.mean(xv * xv, axis=-1, keepdims=True) + EPS)
        xn_ref[...] = ((xv * rs) * w_ref[...]).astype(xn_ref.dtype)
        rs_ref[...] = rs

    return pl.pallas_call(
        body, name=name, grid=(S // tr,),
        in_specs=[pl.BlockSpec((tr, D), lambda i: (i, 0)), pl.BlockSpec((1, D), lambda i: (0, 0))],
        out_specs=[pl.BlockSpec((tr, D), lambda i: (i, 0)), pl.BlockSpec((tr, 1), lambda i: (i, 0))],
        out_shape=[jax.ShapeDtypeStruct((S, D), _MXU), jax.ShapeDtypeStruct((S, 1), F32)],
        compiler_params=_cp(("parallel",)))(x, w)


def _rms_bwd(dyn, x, rs, w, res, name):
    S, D = x.shape
    tr = _pick(S, (256, 128))

    def body(dy_ref, x_ref, rs_ref, w_ref, res_ref, dx_ref, dxb_ref, dw_ref):
        @pl.when(pl.program_id(0) == 0)
        def _():
            dw_ref[...] = jnp.zeros_like(dw_ref)

        dy, r = dy_ref[...].astype(F32), rs_ref[...]
        xhat = x_ref[...] * r
        dw_ref[...] += jnp.sum(dy * xhat, axis=0, keepdims=True)
        dxhat = dy * w_ref[...]
        dx = res_ref[...] + r * (dxhat - xhat * jnp.mean(dxhat * xhat, axis=-1, keepdims=True))
        dx_ref[...] = dx
        dxb_ref[...] = dx.astype(dxb_ref.dtype)

    row = pl.BlockSpec((tr, D), lambda i: (i, 0))
    vec = pl.BlockSpec((1, D), lambda i: (0, 0))
    return pl.pallas_call(
        body, name=name, grid=(S // tr,),
        in_specs=[row, row, pl.BlockSpec((tr, 1), lambda i: (i, 0)), vec, row], out_specs=[row, row, vec],
        out_shape=[jax.ShapeDtypeStruct((S, D), F32), jax.ShapeDtypeStruct((S, D), _MXU), jax.ShapeDtypeStruct((1, D), F32)],
        compiler_params=_cp(("arbitrary",)))(dyn, x, rs, w, res)


def _final_loss(h2, w, tgt):
    S, D = h2.shape
    tr = _pick(S, (256, 128))

    def body(h_ref, w_ref, t_ref, loss_ref, dh_ref, dhb_ref, dw_ref):
        @pl.when(pl.program_id(0) == 0)
        def _():
            dw_ref[...] = jnp.zeros_like(dw_ref)
            loss_ref[...] = jnp.zeros_like(loss_ref)

        hv, wv = h_ref[...], w_ref[...]
        rs = lax.rsqrt(jnp.mean(hv * hv, axis=-1, keepdims=True) + EPS)
        xhat = hv * rs
        err = xhat * wv - t_ref[...]
        row = jnp.mean(err * err, axis=-1, keepdims=True)
        loss_ref[...] += jnp.broadcast_to(0.5 * jnp.sum(row, axis=0, keepdims=True), loss_ref.shape)
        dy = err * (1.0 / D)
        dw_ref[...] += jnp.sum(dy * xhat, axis=0, keepdims=True)
        dxhat = dy * wv
        dh = rs * (dxhat - xhat * jnp.mean(dxhat * xhat, axis=-1, keepdims=True))
        dh_ref[...] = dh
        dhb_ref[...] = dh.astype(dhb_ref.dtype)

    row = pl.BlockSpec((tr, D), lambda i: (i, 0))
    vec = pl.BlockSpec((1, D), lambda i: (0, 0))
    return pl.pallas_call(
        body, name="final_loss", grid=(S // tr,), in_specs=[row, vec, row],
        out_specs=[pl.BlockSpec((1, LANE), lambda i: (0, 0)), row, row, vec],
        out_shape=[jax.ShapeDtypeStruct((1, LANE), F32), jax.ShapeDtypeStruct((S, D), F32), jax.ShapeDtypeStruct((S, D), _MXU),
                   jax.ShapeDtypeStruct((1, D), F32)],
        compiler_params=_cp(("arbitrary",)))(h2, w, tgt)


def _shift_rows(x, k, rows):
    if k == 0:
        return x
    S = x.shape[0]
    r = pltpu.roll(x, k % S, axis=0)
    ok = (rows >= k) if k > 0 else (rows < S + k)
    return jnp.where(ok, r, 0.0)


XBC_COL0 = SSD_WIDTH // 128


def _conv_fwd(proj, conv_w, conv_b):
    S = proj.shape[0]
    nct = CONV_CH // 128

    def body(x_ref, w_ref, b_ref, o_ref):
        x = x_ref[...]
        rows = lax.broadcasted_iota(jnp.int32, x.shape, 0)
        c = b_ref[...] + w_ref[3:4, :] * x
        for k in range(1, CONV_K):
            c = c + w_ref[3 - k:4 - k, :] * _shift_rows(x, k, rows)
        o_ref[...] = c * _sigmoid(c)

    return pl.pallas_call(
        body, name="conv_fwd", grid=(nct,),
        in_specs=[pl.BlockSpec((S, 128), lambda j: (0, XBC_COL0 + j)), pl.BlockSpec((CONV_K, 128), lambda j: (0, j)),
                  pl.BlockSpec((1, 128), lambda j: (0, j))],
        out_specs=pl.BlockSpec((S, 128), lambda j: (0, j)),
        out_shape=jax.ShapeDtypeStruct((S, CONV_CH), F32), compiler_params=_cp(("parallel",)))(proj, conv_w, conv_b)


def _conv_bwd(proj, conv_w, conv_b, dxa):
    S = proj.shape[0]
    nct = CONV_CH // 128

    def body(x_ref, w_ref, b_ref, d_ref, dx_ref, dw_ref, db_ref):
        x = x_ref[...]
        rows = lax.broadcasted_iota(jnp.int32, x.shape, 0)
        xs = [_shift_rows(x, k, rows) for k in range(CONV_K)]
        c = b_ref[...] + w_ref[3:4, :] * x
        for k in range(1, CONV_K):
            c = c + w_ref[3 - k:4 - k, :] * xs[k]
        s = _sigmoid(c)
        dc = d_ref[...] * (s * (1.0 + c * (1.0 - s)))
        dx = w_ref[3:4, :] * dc
        for k in range(1, CONV_K):
            dx = dx + w_ref[3 - k:4 - k, :] * _shift_rows(dc, -k, rows)
        dx_ref[...] = dx.astype(dx_ref.dtype)
        for k in range(CONV_K):
            dw_ref[3 - k:4 - k, :] = jnp.sum(dc * xs[k], axis=0, keepdims=True)
        db_ref[...] = jnp.sum(dc, axis=0, keepdims=True)

    col = pl.BlockSpec((S, 128), lambda j: (0, j))
    return pl.pallas_call(
        body, name="conv_bwd", grid=(nct,),
        in_specs=[pl.BlockSpec((S, 128), lambda j: (0, XBC_COL0 + j)), pl.BlockSpec((CONV_K, 128), lambda j: (0, j)),
                  pl.BlockSpec((1, 128), lambda j: (0, j)), col],
        out_specs=[col, pl.BlockSpec((CONV_K, 128), lambda j: (0, j)), pl.BlockSpec((1, 128), lambda j: (0, j))],
        out_shape=[jax.ShapeDtypeStruct((S, CONV_CH), _MXU), jax.ShapeDtypeStruct((CONV_K, CONV_CH), F32),
                   jax.ShapeDtypeStruct((1, CONV_CH), F32)],
        compiler_params=_cp(("parallel",)))(proj, conv_w, conv_b, dxa)


def _ssd_consts():
    L = SSD_L
    r = lax.broadcasted_iota(jnp.int32, (L, L), 0)
    c = lax.broadcasted_iota(jnp.int32, (L, L), 1)
    causal = r >= c
    upper = (r <= c).astype(F32)
    hr = lax.broadcasted_iota(jnp.int32, (SSD_HEADS, SSD_WIDTH), 0)
    hc = lax.broadcasted_iota(jnp.int32, (SSD_HEADS, SSD_WIDTH), 1)
    expand = (lax.shift_right_logical(hc, 6) == hr).astype(F32)
    return causal, causal.astype(F32), upper, expand


def _softplus(x):
    return jnp.maximum(x, 0.0) + jnp.log(1.0 + jnp.exp(-jnp.abs(x)))


def _ssd_scalars(dtr, dt_bias, a_log, tri, upper, expand):
    dt = _softplus(dtr + dt_bias)
    A = -jnp.exp(a_log)
    adt = dt * A
    acum = _dot(tri, adt, "nn", split="b")
    acum_t = _dot(adt, upper, "tn", split="a")
    alast = acum[SSD_L - 1:SSD_L, :]
    e = jnp.exp(acum)
    wdec = jnp.exp(alast - acum)
    gam = jnp.exp(alast)
    ex = lambda t: _dot(t, expand, "nn", split="a")
    gam8 = jnp.broadcast_to(gam, (8, SSD_HEADS))
    return dt, A, acum, acum_t, e, wdec, gam, ex(dt), ex(e), ex(wdec), ex(gam8)[0:1, :]


def _ssd_fwd(proj, proj_small, xa, dt_bias, a_log, d_skip, norm_w):
    S = proj.shape[0]
    L, N, W = SSD_L, SSD_N, SSD_WIDTH
    nc = S // L

    def body(z_ref, xa_ref, dtr_ref, dtb_ref, al_ref, dsk_ref, nw_ref, yo_ref, y_ref, rs_ref, hs_ref, h_scr, y_scr):
        @pl.when(pl.program_id(0) == 0)
        def _():
            h_scr[...] = jnp.zeros_like(h_scr)

        causal, tri, upper, expand = _ssd_consts()
        dt, A, acum, acum_t, e, wdec, gam, dtE, eE, wE, gamE = _ssd_scalars(dtr_ref[:, 0:SSD_HEADS], dtb_ref[...], al_ref[...], tri, upper, expand)
        xs = xa_ref[:, 0:W]
        X = xs * dtE
        XW = X * wE
        hs_ref[0] = h_scr[...]
        for g in range(SSD_G):
            gs = slice(g * 512, (g + 1) * 512)
            Bg = xa_ref[:, W + g * N:W + (g + 1) * N]
            Cg = xa_ref[:, W + SSD_G * N + g * N:W + SSD_G * N + (g + 1) * N]
            Hg = h_scr[:, gs]
            CB = _dot(Cg, Bg, "nt")
            yoff = _dot(Cg, Hg, "nn") * eE[:, gs]
            st = _dot(Bg, XW[:, gs], "tn")
            for j in range(8):
                h = g * 8 + j
                hsl = slice(h * SSD_P, (h + 1) * SSD_P)
                lam = jnp.exp(jnp.where(causal, acum[:, h:h + 1] - acum_t[h:h + 1, :], -jnp.inf))
                y_scr[:, hsl] = _dot(CB * lam, X[:, hsl], "nn") + yoff[:, j * SSD_P:(j + 1) * SSD_P]
            h_scr[:, gs] = gamE[:, gs] * Hg + st
        dskE = _dot(jnp.broadcast_to(dsk_ref[...], (8, SSD_HEADS)), expand, "nn", split="a")[0:1, :]
        y = y_scr[...] + dskE * xs
        y_ref[...] = y
        zv = z_ref[...]
        yg = y * (zv * _sigmoid(zv))
        rs = lax.rsqrt(jnp.mean(yg * yg, axis=-1, keepdims=True) + EPS)
        rs_ref[...] = rs
        yo_ref[...] = ((yg * rs) * nw_ref[...]).astype(yo_ref.dtype)

    p16 = pl.BlockSpec((1, SSD_HEADS), lambda c: (0, 0))
    return pl.pallas_call(
        body, name="ssd_fwd", grid=(nc,),
        in_specs=[pl.BlockSpec((L, W), lambda c: (c, 0)), pl.BlockSpec((L, CONV_CH), lambda c: (c, 0)),
                  pl.BlockSpec((L, W_SMALL), lambda c: (c, 0)), p16, p16, p16, pl.BlockSpec((1, W), lambda c: (0, 0))],
        out_specs=[pl.BlockSpec((L, W), lambda c: (c, 0)), pl.BlockSpec((L, W), lambda c: (c, 0)),
                   pl.BlockSpec((L, 1), lambda c: (c, 0)), pl.BlockSpec((1, N, W), lambda c: (c, 0, 0))],
        out_shape=[jax.ShapeDtypeStruct((S, W), _MXU), jax.ShapeDtypeStruct((S, W), F32), jax.ShapeDtypeStruct((S, 1), F32),
                   jax.ShapeDtypeStruct((nc, N, W), F32)],
        scratch_shapes=[pltpu.VMEM((N, W), F32), pltpu.VMEM((L, W), F32)],
        compiler_params=_cp(("arbitrary",)))(proj, xa, proj_small, dt_bias, a_log, d_skip, norm_w)


def _ssd_bwd(dmixed, proj, proj_small, xa, y, rs2, hs, dt_bias, a_log, d_skip, norm_w):
    S = proj.shape[0]
    L, N, W, H = SSD_L, SSD_N, SSD_WIDTH, SSD_HEADS
    nc = S // L

    def body(dyo_ref, z_ref, xa_ref, dtr_ref, y_ref, rs_ref, hs_ref, dtb_ref, al_ref, dsk_ref, nw_ref,
             dz_ref, dxa_ref, ddtr_ref, ddtb_ref, dal_ref, ddsk_ref, dnw_ref, dh_scr, dx_scr):
        @pl.when(pl.program_id(0) == 0)
        def _():
            dh_scr[...] = jnp.zeros_like(dh_scr)
            ddtb_ref[...] = jnp.zeros_like(ddtb_ref)
            dal_ref[...] = jnp.zeros_like(dal_ref)
            ddsk_ref[...] = jnp.zeros_like(ddsk_ref)
            dnw_ref[...] = jnp.zeros_like(dnw_ref)

        causal, tri, upper, expand = _ssd_consts()
        heads = lambda t: _dot(t, expand, "nt", split="a")
        onehot = lambda h: (lax.broadcasted_iota(jnp.int32, (1, H), 1) == h).astype(F32)

        zv, yv, rs = z_ref[...], y_ref[...], rs_ref[...]
        sz = _sigmoid(zv)
        zs = zv * sz
        xhat = (yv * zs) * rs
        dyo = dyo_ref[...].astype(F32)
        dnw_ref[...] += jnp.sum(dyo * xhat, axis=0, keepdims=True)
        dxhat = dyo * nw_ref[...]
        dyg = rs * (dxhat - xhat * jnp.mean(dxhat * xhat, axis=-1, keepdims=True))
        dz_ref[...] = (dyg * yv * (sz * (1.0 + zv * (1.0 - sz)))).astype(dz_ref.dtype)
        dy = dyg * zs

        dtr = dtr_ref[:, 0:H]
        dt, A, acum, acum_t, e, wdec, gam, dtE, eE, wE, gamE = _ssd_scalars(dtr, dtb_ref[...], al_ref[...], tri, upper, expand)
        xs = xa_ref[:, 0:W]
        X = xs * dtE
        XW = X * wE
        dskE = _dot(jnp.broadcast_to(dsk_ref[...], (8, H)), expand, "nn", split="a")[0:1, :]
        ddsk_ref[...] += heads(jnp.broadcast_to(jnp.sum(dy * xs, axis=0, keepdims=True), (8, W)))[0:1, :]

        dYe = dy * eE
        dacum = jnp.zeros((L, H), F32)
        de_full = []
        dw_full = []
        dgam_full = []
        for g in range(SSD_G):
            gs = slice(g * 512, (g + 1) * 512)
            Bg = xa_ref[:, W + g * N:W + (g + 1) * N]
            Cg = xa_ref[:, W + SSD_G * N + g * N:W + SSD_G * N + (g + 1) * N]
            Hg = hs_ref[0, :, gs]
            dHn = dh_scr[:, gs]
            CH = _dot(Cg, Hg, "nn")
            de_full.append(dy[:, gs] * CH)
            dC = _dot(dYe[:, gs], Hg, "nt")
            dHs = gamE[:, gs] * dHn + _dot(Cg, dYe[:, gs], "tn")
            dgam_full.append(jnp.sum(dHn * Hg, axis=0, keepdims=True))
            BdS = _dot(Bg, dHn, "nn")
            dB = _dot(XW[:, gs], dHn, "nt")
            dx_scr[:, gs] = BdS * wE[:, gs]
            dw_full.append(BdS * X[:, gs])
            CB = _dot(Cg, Bg, "nt")
            dCB = jnp.zeros((L, L), F32)
            for j in range(8):
                h = g * 8 + j
                hsl = slice(h * SSD_P, (h + 1) * SSD_P)
                lam = jnp.exp(jnp.where(causal, acum[:, h:h + 1] - acum_t[h:h + 1, :], -jnp.inf))
                M = CB * lam
                dM = _dot(dy[:, hsl], X[:, hsl], "nt")
                dx_scr[:, hsl] += _dot(M, dy[:, hsl], "tn")
                dCB = dCB + dM * lam
                Q = dM * M
                rowsum = jnp.sum(Q, axis=1, keepdims=True)
                colsum = _dot(Q, jnp.ones((L, 8), F32), "tn", split="a")[:, 0:1]
                dacum = dacum + (rowsum - colsum) * onehot(h)
            dC = dC + _dot(dCB, Bg, "nn")
            dB = dB + _dot(dCB, Cg, "tn")
            dxa_ref[:, W + g * N:W + (g + 1) * N] = dB
            dxa_ref[:, W + SSD_G * N + g * N:W + SSD_G * N + (g + 1) * N] = dC
            dh_scr[:, gs] = dHs

        de16 = heads(jnp.concatenate(de_full, axis=1))
        dw16 = heads(jnp.concatenate(dw_full, axis=1))
        dgam16 = heads(jnp.broadcast_to(jnp.concatenate(dgam_full, axis=1), (8, W)))[0:1, :]
        dacum = dacum + de16 * e - dw16 * wdec
        dlast = jnp.sum(dw16 * wdec, axis=0, keepdims=True) + dgam16 * gam
        lastrow = (lax.broadcasted_iota(jnp.int32, (L, 1), 0) == L - 1).astype(F32)
        dacum = dacum + lastrow * dlast
        da = _dot(tri, dacum, "tn", split="b")
        dX = dx_scr[...]
        ddt = da * A + heads(dX * xs)
        dA = jnp.sum(da * dt, axis=0, keepdims=True)
        dal_ref[...] += dA * A
        ddtr = ddt * _sigmoid(dtr + dtb_ref[...])
        ddtb_ref[...] += jnp.sum(ddtr, axis=0, keepdims=True)
        ddtr_ref[...] = ddtr
        dxa_ref[:, 0:W] = dX * dtE + dy * dskE

    p16 = pl.BlockSpec((1, H), lambda c: (0, 0))
    rev = lambda c: (nc - 1 - c, 0)
    return pl.pallas_call(
        body, name="ssd_bwd", grid=(nc,),
        in_specs=[pl.BlockSpec((L, W), rev), pl.BlockSpec((L, W), rev), pl.BlockSpec((L, CONV_CH), rev),
                  pl.BlockSpec((L, W_SMALL), rev), pl.BlockSpec((L, W), rev), pl.BlockSpec((L, 1), rev),
                  pl.BlockSpec((1, N, W), lambda c: (nc - 1 - c, 0, 0)), p16, p16, p16, pl.BlockSpec((1, W), lambda c: (0, 0))],
        out_specs=[pl.BlockSpec((L, W), rev), pl.BlockSpec((L, CONV_CH), rev), pl.BlockSpec((L, H), rev),
                   p16, p16, p16, pl.BlockSpec((1, W), lambda c: (0, 0))],
        out_shape=[jax.ShapeDtypeStruct((S, W), _MXU), jax.ShapeDtypeStruct((S, CONV_CH), F32), jax.ShapeDtypeStruct((S, H), F32),
                   jax.ShapeDtypeStruct((1, H), F32), jax.ShapeDtypeStruct((1, H), F32), jax.ShapeDtypeStruct((1, H), F32),
                   jax.ShapeDtypeStruct((1, W), F32)],
        scratch_shapes=[pltpu.VMEM((N, W), F32), pltpu.VMEM((L, W), F32)],
        compiler_params=_cp(("arbitrary",)))(dmixed, proj, xa, proj_small, y, rs2, hs, dt_bias, a_log, d_skip, norm_w)


def _rope_tables(S):
    inv = 1.0 / (ROPE_THETA ** (jnp.arange(0, ROPE_DIM, 2, dtype=F32) / ROPE_DIM))
    ang = jnp.arange(S, dtype=F32)[:, None] * inv[None, :]
    cos, sin = jnp.cos(ang), jnp.sin(ang)
    half = ROPE_DIM // 2
    c64 = jnp.concatenate([cos, cos, jnp.ones((S, HD - ROPE_DIM), F32)], axis=1)
    s64 = jnp.concatenate([sin, sin, jnp.zeros((S, HD - ROPE_DIM), F32)], axis=1)
    del half
    return jnp.concatenate([c64, c64], axis=1), jnp.concatenate([s64, s64], axis=1)


def _rope(xs, blk0, width, cos, sin, sign, out_dtype, name, extra=None):
    S = xs[0].shape[0]
    tr = _pick(S, (512, 256, 128))
    nx = len(xs)

    def body(*refs):
        x_refs, c_ref, s_ref = refs[:nx], refs[nx], refs[nx + 1]
        e_ref = refs[nx + 2] if extra is not None else None
        o_ref = refs[-1]
        cv, sv = c_ref[...], s_ref[...] * sign
        lane = lax.broadcasted_iota(jnp.int32, (tr, 128), 1)
        first = (lane & (HD - 1)) < (ROPE_DIM // 2)
        for j in range(2):
            cs = slice(j * 128, (j + 1) * 128)
            xv = x_refs[0][:, cs].astype(F32)
            for r in x_refs[1:]:
                xv = xv + r[:, cs].astype(F32)
            rot = jnp.where(first, -pltpu.roll(xv, 128 - ROPE_DIM // 2, axis=1), pltpu.roll(xv, ROPE_DIM // 2, axis=1))
            out = xv * cv + rot * sv
            if extra is not None:
                out = out + e_ref[:, cs].astype(F32)
            o_ref[:, cs] = out.astype(out_dtype)

    t128 = pl.BlockSpec((tr, 128), lambda i, j: (i, 0))
    oblk = pl.BlockSpec((tr, 256), lambda i, j: (i, j))
    specs = [pl.BlockSpec((tr, 256), lambda i, j: (i, blk0 + j))] * nx + [t128, t128]
    ins = list(xs) + [cos, sin]
    if extra is not None:
        ins.append(extra[0])
        eb = extra[1]
        specs.append(pl.BlockSpec((tr, 256), lambda i, j: (i, eb + j)))
    return pl.pallas_call(
        body, name=name, grid=(S // tr, width // 256), in_specs=specs, out_specs=oblk,
        out_shape=jax.ShapeDtypeStruct((S, width), out_dtype), compiler_params=_cp(("parallel", "parallel")))(*ins)


def _rotate128(xv, cv, sv, first):
    rot = jnp.where(first, -pltpu.roll(xv, 128 - ROPE_DIM // 2, axis=1), pltpu.roll(xv, ROPE_DIM // 2, axis=1))
    return xv * cv + rot * sv


def _kv_prep(proj, cos, sin, tk):
    S = proj.shape[0]
    NB = S // SEL_BLOCK

    def body(ks_ref, vs_ref, kw_ref, vw_ref, c_ref, s_ref, *outs):
        cv, sv = c_ref[...], s_ref[...]
        lane = lax.broadcasted_iota(jnp.int32, (tk, 128), 1)
        first = (lane & (HD - 1)) < (ROPE_DIM // 2)
        key = pl.program_id(0) * tk + lax.broadcasted_iota(jnp.int32, (tk, NB), 0)
        onehot = (lax.shift_right_logical(key, 6) == lax.broadcasted_iota(jnp.int32, (tk, NB), 1)).astype(F32)
        for j, (ref, rotated) in enumerate(((ks_ref, True), (vs_ref, False), (kw_ref, True), (vw_ref, False))):
            nat, blk = outs[2 * j], outs[2 * j + 1]
            for half in range(2):
                xv = ref[:, half * 128:(half + 1) * 128]
                if rotated:
                    xv = _rotate128(xv, cv, sv, first)
                for e in range(2):
                    h = 2 * half + e
                    piece = xv[:, e * HD:(e + 1) * HD]
                    nat[h] = (jnp.concatenate([piece, onehot], axis=1) if j == 0 else piece).astype(nat.dtype)
                    blk[h, 0] = piece.T.astype(blk.dtype)

    col = lambda b: pl.BlockSpec((tk, 256), lambda i: (i, b))
    t128 = pl.BlockSpec((tk, 128), lambda i: (i, 0))
    nat_spec = lambda w: pl.BlockSpec((N_KV, tk, w), lambda i: (0, i, 0))
    blk_spec = pl.BlockSpec((N_KV, 1, HD, tk), lambda i: (0, i, 0, 0))
    nat_shape = lambda w: jax.ShapeDtypeStruct((N_KV, S, w), _MXU)
    blk_shape = jax.ShapeDtypeStruct((N_KV, S // tk, HD, tk), _MXU)
    widths = (HD + NB, HD, HD, HD)
    res = pl.pallas_call(
        body, name="kv_prep", grid=(S // tk,), in_specs=[col(KSB), col(VSB), col(KWB), col(VWB), t128, t128],
        out_specs=[s for w in widths for s in (nat_spec(w), blk_spec)],
        out_shape=[s for w in widths for s in (nat_shape(w), blk_shape)],
        compiler_params=_cp(("parallel",)))(proj, proj, proj, proj, cos, sin)
    return dict(ks_ext=res[0], ks_t=res[1], vs=res[2], vs_t=res[3], kw=res[4], kw_t=res[5], vw=res[6], vw_t=res[7])


def _dkv_post(dks, dvs, dkw, dvw, cos, sin):
    S = dks.shape[1]
    tr = _pick(S, (512, 256, 128))

    def body(dks_ref, dvs_ref, dkw_ref, dvw_ref, c_ref, s_ref, o_ref):
        cv, sv = c_ref[...], -s_ref[...]
        lane = lax.broadcasted_iota(jnp.int32, (tr, 128), 1)
        first = (lane & (HD - 1)) < (ROPE_DIM // 2)
        for j, (ref, rotated) in enumerate(((dks_ref, True), (dvs_ref, False), (dkw_ref, True), (dvw_ref, False))):
            for half in range(2):
                xv = jnp.concatenate([ref[2 * half], ref[2 * half + 1]], axis=1)
                if rotated:
                    xv = _rotate128(xv, cv, sv, first)
                o_ref[:, j * 256 + half * 128:j * 256 + (half + 1) * 128] = xv.astype(o_ref.dtype)

    hm = pl.BlockSpec((N_KV, tr, HD), lambda i: (0, i, 0))
    t128 = pl.BlockSpec((tr, 128), lambda i: (i, 0))
    return pl.pallas_call(
        body, name="dkv_post", grid=(S // tr,), in_specs=[hm, hm, hm, hm, t128, t128],
        out_specs=pl.BlockSpec((tr, 4 * 256), lambda i: (i, 0)), out_shape=jax.ShapeDtypeStruct((S, 4 * 256), _MXU),
        compiler_params=_cp(("parallel",)))(dks, dvs, dkw, dvw, cos, sin)


def _compress_fwd(R, pe, w1, w2):
    NC = R.shape[1]
    half = 16 * HD

    def body(r_ref, pe_ref, w1_ref, w2_ref, o_ref, hid_ref):
        r = r_ref[0]
        a = _dot(r + pe_ref[:, 0:half], w1_ref[0:half, :], "nn")
        b = _dot(r + pe_ref[:, half:2 * half], w1_ref[half:2 * half, :], "nn")
        hid = a + pltpu.roll(b, NC - 1, axis=0)
        hid_ref[0] = hid
        out = _dot(hid * _sigmoid(hid), w2_ref[...], "nn")
        rows = lax.broadcasted_iota(jnp.int32, out.shape, 0)
        o_ref[0] = jnp.where(rows < NC - 1, out, 0.0).astype(o_ref.dtype)

    return pl.pallas_call(
        body, name="compress_fwd", grid=(N_KV,),
        in_specs=[pl.BlockSpec((1, NC, half), lambda h: (h, 0, 0)), pl.BlockSpec((1, 2 * half), lambda h: (0, 0)),
                  pl.BlockSpec((2 * half, CMP_HID), lambda h: (0, 0)), pl.BlockSpec((CMP_HID, HD), lambda h: (0, 0))],
        out_specs=[pl.BlockSpec((1, NC, HD), lambda h: (h, 0, 0)), pl.BlockSpec((1, NC, CMP_HID), lambda h: (h, 0, 0))],
        out_shape=[jax.ShapeDtypeStruct((N_KV, NC, HD), _MXU), jax.ShapeDtypeStruct((N_KV, NC, CMP_HID), F32)],
        compiler_params=_cp(("parallel",)))(R, pe, w1, w2)


def _compress_bwd(R, pe, w1, w2, hid, dout):
    NC = R.shape[1]
    half = 16 * HD

    def body(r_ref, pe_ref, w1_ref, w2_ref, hid_ref, do_ref, dr_ref, dw1_ref, dw2_ref, dpe_ref):
        @pl.when(pl.program_id(0) == 0)
        def _():
            dw1_ref[...] = jnp.zeros_like(dw1_ref)
            dw2_ref[...] = jnp.zeros_like(dw2_ref)
            dpe_ref[...] = jnp.zeros_like(dpe_ref)

        r, hv, do = r_ref[0], hid_ref[0], do_ref[0]
        s = _sigmoid(hv)
        dw2_ref[...] += _dot(hv * s, do, "tn")
        dhid = _dot(do, w2_ref[...], "nt") * (s * (1.0 + hv * (1.0 - s)))
        rows = lax.broadcasted_iota(jnp.int32, dhid.shape, 0)
        dhid = jnp.where(rows < NC - 1, dhid, 0.0)
        dhid_dn = pltpu.roll(dhid, 1, axis=0)
        dw1_ref[0:half, :] += _dot(r + pe_ref[:, 0:half], dhid, "tn")
        dw1_ref[half:2 * half, :] += _dot(r + pe_ref[:, half:2 * half], dhid_dn, "tn")
        dxt = _dot(dhid, w1_ref[0:half, :], "nt")
        dxb = _dot(dhid_dn, w1_ref[half:2 * half, :], "nt")
        dr_ref[0] = dxt + dxb
        dpe_ref[:, 0:half] += jnp.sum(dxt, axis=0, keepdims=True)
        dpe_ref[:, half:2 * half] += jnp.sum(dxb, axis=0, keepdims=True)

    return pl.pallas_call(
        body, name="compress_bwd", grid=(N_KV,),
        in_specs=[pl.BlockSpec((1, NC, half), lambda h: (h, 0, 0)), pl.BlockSpec((1, 2 * half), lambda h: (0, 0)),
                  pl.BlockSpec((2 * half, CMP_HID), lambda h: (0, 0)), pl.BlockSpec((CMP_HID, HD), lambda h: (0, 0)),
                  pl.BlockSpec((1, NC, CMP_HID), lambda h: (h, 0, 0)), pl.BlockSpec((1, NC, HD), lambda h: (h, 0, 0))],
        out_specs=[pl.BlockSpec((1, NC, half), lambda h: (h, 0, 0)), pl.BlockSpec((2 * half, CMP_HID), lambda h: (0, 0)),
                   pl.BlockSpec((CMP_HID, HD), lambda h: (0, 0)), pl.BlockSpec((1, 2 * half), lambda h: (0, 0))],
        out_shape=[jax.ShapeDtypeStruct((N_KV, NC, half), F32), jax.ShapeDtypeStruct((2 * half, CMP_HID), F32),
                   jax.ShapeDtypeStruct((CMP_HID, HD), F32), jax.ShapeDtypeStruct((1, 2 * half), F32)],
        compiler_params=_cp(("arbitrary",)))(R, pe, w1, w2, hid, dout)


def _attn_cfg(S, Sk, mode):
    tq = _pick(S, (256, 128))
    tk = Sk if mode == "cmp" else _pick(Sk, (256, 128))
    return tq, tk


def _block_start(kb, tk):
    return kb * tk if isinstance(kb, int) else pl.multiple_of(kb * tk, tk)


def _pipelined_key_blocks(mode, q0, tq, tk, produce, consume):
    if mode == "cmp":
        produce(0, True, 0)
        consume(0, 0)
        return
    if mode == "win":
        assert tq == tk and WINDOW == 2 * tk
        last = q0 // tk
        first = jnp.maximum(last - 2, 0)

        @pl.when(last == 0)
        def _():
            produce(last, True, 0)
            consume(last, 0)

        @pl.when(last == 1)
        def _():
            produce(first, True, 0)
            produce(last, True, 1)
            consume(first, 0)
            consume(last, 1)

        @pl.when(last >= 2)
        def _():
            produce(first, True, 0)
            produce(first + 1, False, 1)
            consume(first, 0)
            produce(last, True, 0)
            consume(first + 1, 1)
            consume(last, 0)

        return
    first, n_plain, plain_masked = 0, q0 // tk, False
    last = first + n_plain
    pairs = jnp.maximum(n_plain - 1, 0) // 2

    @pl.when(n_plain >= 1)
    def _():
        produce(first, plain_masked, 0)

    def two(j, carry):
        kb = first + 2 * j
        produce(kb + 1, plain_masked, 1)
        consume(kb, 0)
        produce(kb + 2, plain_masked, 0)
        consume(kb + 1, 1)
        return carry

    lax.fori_loop(0, pairs, two, 0)
    kb = first + 2 * pairs
    left = n_plain - 2 * pairs

    @pl.when(left == 2)
    def _():
        produce(kb + 1, plain_masked, 1)
        consume(kb, 0)
        produce(last, True, 0)
        consume(kb + 1, 1)
        consume(last, 0)

    @pl.when(left == 1)
    def _():
        produce(last, True, 1)
        consume(kb, 0)
        consume(last, 1)

    @pl.when(left == 0)
    def _():
        produce(last, True, 0)
        consume(last, 0)


def _attn_bias(mode, q0, k0, tq, tk):
    k = k0 + lax.broadcasted_iota(jnp.int32, (tk, tq), 0)
    t = q0 + lax.broadcasted_iota(jnp.int32, (tk, tq), 1)
    if mode == "cmp":
        ok = (k * 16 + 31) <= t
    elif mode == "win":
        ok = (k <= t) & ((t - k) < WINDOW)
    else:
        ok = k <= t
    bias = jnp.where(ok, 0.0, NEG)
    return jnp.concatenate([bias] * GRP, axis=1), jnp.concatenate([ok.astype(F32)] * GRP, axis=1)


def _sel_operands(qs, selneg_ref):
    return jnp.concatenate([qs, jnp.concatenate([selneg_ref[0]] * GRP, axis=0)], axis=1)


def _stack_heads(ref, tq):
    return jnp.concatenate([ref[:, g * HD:(g + 1) * HD] for g in range(GRP)], axis=0)


def _scaled_queries(q_ref, tq):
    return (_stack_heads(q_ref, tq).astype(F32) * SCALE).astype(_MXU)


def _blocked_t(x, tk):
    n, Sk, d = x.shape
    return x.reshape(n, Sk // tk, tk, d).transpose(0, 1, 3, 2)


def _head_rows(ref):
    return jnp.concatenate([ref[0, g:g + 1, :] for g in range(GRP)], axis=1)


def _attn_fwd(q, qcol0, k, vt, mode, selneg, gate, y_prev, y_dtype, name):
    S, Sk = q.shape[0], k.shape[1]
    tq, tk = _attn_cfg(S, Sk, mode)
    R = GRP * tq

    def body(*refs):
        q_ref, k_ref, vt_ref = refs[:3]
        rest = list(refs[3:])
        sel_ref = rest.pop(0) if mode == "sel" else None
        gate_ref = rest.pop(0)
        yp_ref = rest.pop(0) if y_prev is not None else None
        o_ref, lse_ref, y_ref, m_scr, l_scr, acc, s_scr = rest
        q0 = pl.program_id(1) * tq
        qs = _scaled_queries(q_ref, tq)
        m_scr[...] = jnp.full_like(m_scr, NEG)
        l_scr[...] = jnp.zeros_like(l_scr)
        acc[...] = jnp.zeros_like(acc)
        qk = _sel_operands(qs, sel_ref) if mode == "sel" else qs

        def produce(kb, masked, slot):
            k0 = _block_start(kb, tk)
            s = _dot(k_ref[0, pl.ds(k0, tk), :], qk, "nt")
            if masked:
                s = s + _attn_bias(mode, q0, k0, tq, tk)[0]
            s_scr[slot] = s

        def consume(kb, slot):
            s = s_scr[slot]
            m_old = m_scr[...]
            m_new = jnp.maximum(m_old, jnp.max(s, axis=0, keepdims=True))
            p = jnp.exp(s - m_new)
            if mode == "cmp":
                p = p * _attn_bias(mode, q0, 0, tq, tk)[1]
            alpha = jnp.exp(m_old - m_new)
            l_scr[...] = alpha * l_scr[...] + jnp.sum(p, axis=0, keepdims=True)
            acc[...] = alpha * acc[...] + _dot(vt_ref[0, kb], p, "nn")
            m_scr[...] = m_new

        _pipelined_key_blocks(mode, q0, tq, tk, produce, consume)
        l = l_scr[...]
        good = l > 0.0
        o_t = acc[...] * jnp.where(good, 1.0 / jnp.where(good, l, 1.0), 0.0)
        lse = jnp.where(good, m_scr[...] + jnp.log(jnp.where(good, l, 1.0)), -NEG)
        y_t = o_t * _sigmoid(_head_rows(gate_ref))
        for g in range(GRP):
            hs, qs_ = slice(g * HD, (g + 1) * HD), slice(g * tq, (g + 1) * tq)
            o_ref[:, hs] = o_t[:, qs_].T
            lse_ref[0, g:g + 1, :] = lse[:, qs_]
            yg = y_t[:, qs_].T
            if y_prev is not None:
                yg = yg + yp_ref[:, hs]
            y_ref[:, hs] = yg.astype(y_ref.dtype)

    row_spec = pl.BlockSpec((1, GRP, tq), lambda h, i: (h, 0, i))
    qo_spec = pl.BlockSpec((tq, GRP * HD), lambda h, i: (i, h))
    ins = [q, k, vt]
    specs = [pl.BlockSpec((tq, GRP * HD), lambda h, i: (i, qcol0 + h)), pl.BlockSpec((1, Sk, k.shape[2]), lambda h, i: (h, 0, 0)),
             pl.BlockSpec((1, Sk // tk, HD, tk), lambda h, i: (h, 0, 0, 0))]
    if mode == "sel":
        assert tq == tk
        ins.append(selneg)
        specs.append(pl.BlockSpec((1, tq, selneg.shape[2]), lambda h, i: (h, i, 0)))
    ins.append(gate)
    specs.append(row_spec)
    if y_prev is not None:
        ins.append(y_prev)
        specs.append(qo_spec)
    return pl.pallas_call(
        body, name=name, grid=(N_KV, S // tq), in_specs=specs, out_specs=[qo_spec, row_spec, qo_spec],
        out_shape=[jax.ShapeDtypeStruct((S, ATT_WIDTH), F32), jax.ShapeDtypeStruct((N_KV, GRP, S), F32),
                   jax.ShapeDtypeStruct((S, ATT_WIDTH), y_dtype)],
        scratch_shapes=[pltpu.VMEM((1, R), F32), pltpu.VMEM((1, R), F32), pltpu.VMEM((HD, R), F32), pltpu.VMEM((2, tk, R), F32)],
        compiler_params=_cp(("parallel", "arbitrary")))(*ins)


def _attn_bwd(q, qcol0, k, kt, v, o, lse, dy, dycol0, gate, mode, selneg, name):
    S, Sk = q.shape[0], k.shape[1]
    tq, tk = _attn_cfg(S, Sk, mode)
    R = GRP * tq

    def body(*refs):
        if mode == "sel":
            (q_ref, k_ref, kt_ref, v_ref, o_ref, lse_ref, dy_ref, gate_ref, sel_ref, dq_ref, dk_ref, dv_ref, dg_ref, dq_scr, s_scr,
             dp_scr) = refs
        else:
            q_ref, k_ref, kt_ref, v_ref, o_ref, lse_ref, dy_ref, gate_ref, dq_ref, dk_ref, dv_ref, dg_ref, dq_scr, s_scr, dp_scr = refs

        @pl.when(pl.program_id(1) == 0)
        def _():
            dk_ref[...] = jnp.zeros_like(dk_ref)
            dv_ref[...] = jnp.zeros_like(dv_ref)

        q0 = pl.program_id(1) * tq
        qs = _scaled_queries(q_ref, tq)
        dys = _stack_heads(dy_ref, tq)
        gv = _sigmoid(_head_rows(gate_ref))
        dy_o = _dot(jnp.ones((8, HD), F32), dys * _stack_heads(o_ref, tq), "nt", split="b")[0:1, :]
        delta = gv * dy_o
        dgate = dy_o * (gv * (1.0 - gv))
        for g in range(GRP):
            dg_ref[0, g:g + 1, :] = dgate[:, g * tq:(g + 1) * tq]
        lsev = _head_rows(lse_ref)
        dos = (dys * jnp.broadcast_to(gv, (8, R)).T[:, 0:1]).astype(_MXU)
        dq_scr[...] = jnp.zeros_like(dq_scr)
        qk = _sel_operands(qs, sel_ref) if mode == "sel" else qs

        def produce(kb, masked, slot):
            k0 = _block_start(kb, tk)
            s = _dot(k_ref[0, pl.ds(k0, tk), :], qk, "nt")
            if masked:
                s = s + _attn_bias(mode, q0, k0, tq, tk)[0]
            s_scr[slot] = s
            dp_scr[slot] = _dot(v_ref[0, pl.ds(k0, tk), :], dos, "nt")

        def consume(kb, slot):
            k0 = _block_start(kb, tk)
            p = jnp.exp(s_scr[slot] - lsev)
            if mode == "cmp":
                p = p * _attn_bias(mode, q0, 0, tq, tk)[1]
            ds = p * (dp_scr[slot] - delta)
            dq_scr[...] += _dot(kt_ref[0, kb], ds, "nn")
            dk_ref[0, pl.ds(k0, tk), :] += _dot(ds, qs, "nn")
            dv_ref[0, pl.ds(k0, tk), :] += _dot(p, dos, "nn")

        _pipelined_key_blocks(mode, q0, tq, tk, produce, consume)
        for g in range(GRP):
            dq_ref[:, g * HD:(g + 1) * HD] = (dq_scr[:, g * tq:(g + 1) * tq] * SCALE).T

    kv_spec = pl.BlockSpec((1, Sk, HD), lambda h, i: (h, 0, 0))
    qo_spec = pl.BlockSpec((tq, GRP * HD), lambda h, i: (i, h))
    row_spec = pl.BlockSpec((1, GRP, tq), lambda h, i: (h, 0, i))
    ins = [q, k, kt, v, o, lse, dy, gate]
    specs = [pl.BlockSpec((tq, GRP * HD), lambda h, i: (i, qcol0 + h)), pl.BlockSpec((1, Sk, k.shape[2]), lambda h, i: (h, 0, 0)),
             pl.BlockSpec((1, Sk // tk, HD, tk), lambda h, i: (h, 0, 0, 0)), kv_spec, qo_spec, row_spec,
             pl.BlockSpec((tq, GRP * HD), lambda h, i: (i, dycol0 + h)), row_spec]
    if mode == "sel":
        assert tq == tk
        ins.append(selneg)
        specs.append(pl.BlockSpec((1, tq, selneg.shape[2]), lambda h, i: (h, i, 0)))
    return pl.pallas_call(
        body, name=name, grid=(N_KV, S // tq), in_specs=specs, out_specs=[qo_spec, kv_spec, kv_spec, row_spec],
        out_shape=[jax.ShapeDtypeStruct((S, ATT_WIDTH), F32), jax.ShapeDtypeStruct((N_KV, Sk, HD), F32),
                   jax.ShapeDtypeStruct((N_KV, Sk, HD), F32), jax.ShapeDtypeStruct((N_KV, GRP, S), F32)],
        scratch_shapes=[pltpu.VMEM((HD, R), F32), pltpu.VMEM((2, tk, R), F32), pltpu.VMEM((2, tk, R), F32)],
        compiler_params=_cp(("parallel", "arbitrary")))(*ins)


def _select(q, qcol0, k_cmp, lse):
    S, NC = q.shape[0], k_cmp.shape[1]
    NB = S // SEL_BLOCK
    tq = _pick(S, (256, 128))
    ci = np.arange(NC)[None, :] * 16
    sj = np.arange(NB)[:, None] * SEL_BLOCK
    ov_t = np.clip(np.minimum(ci + 32, sj + SEL_BLOCK) - np.maximum(ci, sj), 0, None) / 32.0
    ov_t[:, NC - 1] = 0.0
    ov_t = jnp.asarray(ov_t, F32)

    def body(q_ref, k_ref, lse_ref, ov_ref, sel_ref):
        q0 = pl.program_id(1) * tq
        bias, okf = _attn_bias("cmp", q0, 0, tq, NC)
        lsev = _head_rows(lse_ref)
        p = jnp.exp(_dot(k_ref[0], _scaled_queries(q_ref, tq), "nt") + bias - lsev) * okf
        imp4 = _dot(ov_ref[...], p, "nn")
        imp = imp4[:, 0:tq] + imp4[:, tq:2 * tq] + imp4[:, 2 * tq:3 * tq] + imp4[:, 3 * tq:4 * tq]
        blk = lax.broadcasted_iota(jnp.int32, (NB, tq), 0)
        cur = lax.shift_right_logical(q0 + lax.broadcasted_iota(jnp.int32, (NB, tq), 1), 6)
        imp = jnp.where((blk == 0) | (blk == cur) | (blk == cur - 1), FORCE, imp)
        imp = jnp.where(blk <= cur, imp, -1.0)
        rank = jnp.zeros((NB, tq), F32)
        for j in range(NB):
            row = imp[j:j + 1, :]
            ahead = (row > imp) | ((row == imp) & (blk > j))
            rank = rank + ahead.astype(F32)
        chosen = (rank < float(N_SELECT)) & (imp >= 0.0)
        sel_ref[0] = jnp.where(chosen, 0.0, NEG).T.astype(sel_ref.dtype)

    return pl.pallas_call(
        body, name="select_blocks", grid=(N_KV, S // tq),
        in_specs=[pl.BlockSpec((tq, GRP * HD), lambda h, i: (i, qcol0 + h)), pl.BlockSpec((1, NC, HD), lambda h, i: (h, 0, 0)),
                  pl.BlockSpec((1, GRP, tq), lambda h, i: (h, 0, i)), pl.BlockSpec((NB, NC), lambda h, i: (0, 0))],
        out_specs=pl.BlockSpec((1, tq, NB), lambda h, i: (h, i, 0)),
        out_shape=jax.ShapeDtypeStruct((N_KV, S, NB), _MXU), compiler_params=_cp(("parallel", "parallel")))(q, k_cmp, lse, ov_t)


def _to_rows16(x):
    S = x.shape[0]
    return x.reshape(S // 16, 16, N_KV, HD).transpose(2, 0, 1, 3).reshape(N_KV, S // 16, 16 * HD)


def _from_rows16(r):
    NC = r.shape[1]
    return r.reshape(N_KV, NC, 16, HD).transpose(1, 2, 0, 3).reshape(NC * 16, N_KV * HD)


DT_COL0 = SSD_WIDTH + CONV_CH
GATE_IN_COL0 = D_IN - 3 * N_HEADS


SHARD_IN = D_IN // N_DEV


def _orig_cols(ref, c0, width):
    pieces, c = [], c0
    while c < c0 + width:
        d, off = divmod(c, SHARD_IN)
        w = min(SHARD_IN - off, c0 + width - c)
        pieces.append(ref[d, :, off:off + w])
        c += w
    return pieces[0] if len(pieces) == 1 else jnp.concatenate(pieces, axis=1)


def _cols_from_slabs(slabs):
    _, R, c = slabs.shape
    tr = _pick(R, (256, 128))

    def body(s_ref, o_ref):
        for t in range(N_DEV * c // LANE):
            pieces, col = [], t * LANE
            while col < (t + 1) * LANE:
                d, off = divmod(col, c)
                w = min(c - off, (t + 1) * LANE - col)
                pieces.append(s_ref[d, :, off:off + w])
                col += w
            o_ref[:, t * LANE:(t + 1) * LANE] = pieces[0] if len(pieces) == 1 else jnp.concatenate(pieces, axis=1)

    return pl.pallas_call(
        body, name="cols_from_slabs", grid=(R // tr,), in_specs=[pl.BlockSpec((N_DEV, tr, c), lambda i: (0, i, 0))],
        out_specs=pl.BlockSpec((tr, N_DEV * c), lambda i: (i, 0)), out_shape=jax.ShapeDtypeStruct((R, N_DEV * c), slabs.dtype),
        compiler_params=_cp(("parallel",)))(slabs)


def _slabs_from_cols(x):
    R, c = x.shape[0], x.shape[1] // N_DEV
    tr = _pick(R, (256, 128))

    def body(x_ref, o_ref):
        for d in range(N_DEV):
            o_ref[d] = x_ref[:, d * c:(d + 1) * c]

    return pl.pallas_call(
        body, name="slabs_from_cols", grid=(R // tr,), in_specs=[pl.BlockSpec((tr, N_DEV * c), lambda i: (i, 0))],
        out_specs=pl.BlockSpec((N_DEV, tr, c), lambda i: (0, i, 0)), out_shape=jax.ShapeDtypeStruct((N_DEV, R, c), x.dtype),
        compiler_params=_cp(("parallel",)))(x)


def _w_in_from_slabs(slabs):
    D = slabs.shape[1]
    tr = _pick(D, (256, 128))

    def body(s_ref, main_ref, small_ref):
        for t in range(W_MAIN // LANE):
            c = t * LANE
            main_ref[:, c:c + LANE] = _orig_cols(s_ref, c if c < DT_COL0 else c + SSD_HEADS, LANE)
        small_ref[...] = jnp.concatenate(
            [_orig_cols(s_ref, DT_COL0, SSD_HEADS), _orig_cols(s_ref, GATE_IN_COL0, 3 * N_HEADS),
             jnp.zeros((tr, W_SMALL - SSD_HEADS - 3 * N_HEADS), small_ref.dtype)], axis=1)

    return pl.pallas_call(
        body, name="w_in_layout", grid=(D // tr,), in_specs=[pl.BlockSpec((N_DEV, tr, SHARD_IN), lambda i: (0, i, 0))],
        out_specs=[pl.BlockSpec((tr, W_MAIN), lambda i: (i, 0)), pl.BlockSpec((tr, W_SMALL), lambda i: (i, 0))],
        out_shape=[jax.ShapeDtypeStruct((D, W_MAIN), slabs.dtype), jax.ShapeDtypeStruct((D, W_SMALL), slabs.dtype)],
        compiler_params=_cp(("parallel",)))(slabs)


def _w_in_to_slabs(main, small):
    D = main.shape[0]
    tr = _pick(D, (256, 128))
    ranges = [(0, DT_COL0, 0, 0), (DT_COL0, DT_COL0 + SSD_HEADS, 1, 0), (DT_COL0 + SSD_HEADS, GATE_IN_COL0, 0, DT_COL0),
              (GATE_IN_COL0, D_IN, 1, SSD_HEADS)]

    def body(main_ref, small_ref, o_ref):
        srcs = (main_ref, small_ref)
        for d in range(N_DEV):
            lo, hi = d * SHARD_IN, (d + 1) * SHARD_IN
            pieces = []
            for start, stop, which, s0 in ranges:
                a, b = max(lo, start), min(hi, stop)
                if a < b:
                    pieces.append(srcs[which][:, s0 + a - start:s0 + b - start].astype(o_ref.dtype))
            o_ref[d] = pieces[0] if len(pieces) == 1 else jnp.concatenate(pieces, axis=1)

    return pl.pallas_call(
        body, name="w_in_grad_layout", grid=(D // tr,),
        in_specs=[pl.BlockSpec((tr, W_MAIN), lambda i: (i, 0)), pl.BlockSpec((tr, W_SMALL), lambda i: (i, 0))],
        out_specs=pl.BlockSpec((N_DEV, tr, SHARD_IN), lambda i: (0, i, 0)),
        out_shape=jax.ShapeDtypeStruct((N_DEV, D, SHARD_IN), main.dtype), compiler_params=_cp(("parallel",)))(main, small)


QB, KCB, VCB, KSB, VSB, KWB, VWB = 10, 14, 15, 16, 17, 18, 19


def _col256(a, b):
    return a[:, b * 256:(b + 1) * 256]


_EARLY = ["w_in", "cmp_w1_k", "cmp_w1_v"]
_LATE = ["w_out", "w_gate", "w_up", "w_down"]
_FFN = ["w_down", "w_gate", "w_up"]
_MID = ["w_out"]
_LAST = ["cmp_w1_k", "cmp_w1_v", "w_in"]


def _local_step(x, tgt, p, late_weights=None, grads_ready=None):
    S = x.shape[0]
    cos, sin = _rope_tables(S)

    u, rs1 = _rms_fwd(x, p["attn_norm_w"], "attn_norm")
    proj = _mm(u, p["w_main"], "nn", F32, "in_proj", after=p.get("before_in_proj"))
    proj_small = _mm(u, p["w_small"], "nn", F32, "in_proj_small")
    xa = _conv_fwd(proj, p["conv_w"], p["conv_b"])
    y_ssd, y_pre, rs_ssd, hs = _ssd_fwd(proj, proj_small, xa, p["dt_bias"], p["a_log"], p["d_skip"], p["ssd_norm_w"])

    q_rot = _rope([proj], QB, ATT_WIDTH, cos, sin, 1.0, _MXU, "rope_q")
    kv = _kv_prep(proj, cos, sin, _attn_cfg(S, S, "sel")[1])
    rk, rv = _to_rows16(_col256(proj, KCB)), _to_rows16(_col256(proj, VCB))
    k_cmp, hid_k = _compress_fwd(rk, p["cmp_pe_k"], p["cmp_w1_k"], p["cmp_w2_k"])
    v_cmp, hid_v = _compress_fwd(rv, p["cmp_pe_v"], p["cmp_w1_v"], p["cmp_w2_v"])
    n_cmp = k_cmp.shape[1]

    gates = proj_small[:, SSD_HEADS:SSD_HEADS + 3 * N_HEADS].reshape(S, N_KV, GRP, 3).transpose(3, 1, 2, 0)
    o_cmp, lse_cmp, y_att = _attn_fwd(proj, QB, k_cmp, _blocked_t(v_cmp, n_cmp), "cmp", None, gates[0], None, F32, "attn_cmp_fwd")
    sel = _select(proj, QB, k_cmp, lse_cmp)
    o_sel, lse_sel, y_att = _attn_fwd(q_rot, 0, kv["ks_ext"], kv["vs_t"], "sel", sel, gates[1], y_att, F32, "attn_sel_fwd")
    o_win, lse_win, y_att = _attn_fwd(q_rot, 0, kv["kw"], kv["vw_t"], "win", None, gates[2], y_att, _MXU, "attn_win_fwd")

    if late_weights is not None:
        p = {**p, **late_weights(y_att)}
    mixed = jnp.concatenate([y_ssd, y_att], axis=1)
    h1 = _mm(mixed, p["w_out"], "nn", F32, "out_proj", res=x)
    v, rs_ffn = _rms_fwd(h1, p["ffn_norm_w"], "ffn_norm")
    gt, up, act = _ffn_up(v, p["w_gate"], p["w_up"])
    h2 = _mm(act, p["w_down"], "nn", F32, "ffn_down", res=h1)
    loss, dh2, dh2b, d_final_w = _final_loss(h2, p["final_norm_w"], tgt)

    def ready(names):
        return None if grads_ready is None else grads_ready(names, g)

    g = {"final_norm_w": d_final_w}
    g["w_down"] = _mm(act, dh2b, "tn", _MXU, "dw_down")
    dgt, dup = _ffn_dact(dh2b, p["w_down"], gt, up)
    g["w_gate"] = _mm(v, dgt, "tn", _MXU, "dw_gate")
    g["w_up"] = _mm(v, dup, "tn", _MXU, "dw_up")
    dv = _ffn_dv(dgt, dup, p["w_gate"], p["w_up"], ready(_FFN))
    dh1, dh1b, g["ffn_norm_w"] = _rms_bwd(dv, h1, rs_ffn, p["ffn_norm_w"], dh2, "ffn_norm_bwd")
    g["w_out"] = _mm(mixed, dh1b, "tn", _MXU, "dw_out")
    dmixed = _mm(dh1b, p["w_out"], "nt", F32, "dmixed", after=ready(_MID))

    dz, dxa, ddtr, g["dt_bias"], g["a_log"], g["d_skip"], g["ssd_norm_w"] = _ssd_bwd(
        dmixed, proj, proj_small, xa, y_pre, rs_ssd, hs, p["dt_bias"], p["a_log"], p["d_skip"], p["ssd_norm_w"])
    dxbc, g["conv_w"], g["conv_b"] = _conv_bwd(proj, p["conv_w"], p["conv_b"], dxa)

    dyb = SSD_WIDTH // (GRP * HD)
    dq_cmp, dk_cmp, dv_cmp, dg_cmp = _attn_bwd(proj, QB, k_cmp, _blocked_t(k_cmp, n_cmp), v_cmp, o_cmp, lse_cmp, dmixed, dyb,
                                               gates[0], "cmp", None, "attn_cmp_bwd")
    dq_sel, dks, dvs, dg_sel = _attn_bwd(q_rot, 0, kv["ks_ext"], kv["ks_t"], kv["vs"], o_sel, lse_sel, dmixed, dyb, gates[1], "sel",
                                         sel, "attn_sel_bwd")
    dq_win, dkw, dvw, dg_win = _attn_bwd(q_rot, 0, kv["kw"], kv["kw_t"], kv["vw"], o_win, lse_win, dmixed, dyb, gates[2], "win", None,
                                         "attn_win_bwd")
    dgate = jnp.stack([dg_cmp, dg_sel, dg_win]).transpose(3, 1, 2, 0).reshape(S, 3 * N_HEADS)
    drk, g["cmp_w1_k"], g["cmp_w2_k"], g["cmp_pe_k"] = _compress_bwd(rk, p["cmp_pe_k"], p["cmp_w1_k"], p["cmp_w2_k"], hid_k, dk_cmp)
    drv, g["cmp_w1_v"], g["cmp_w2_v"], g["cmp_pe_v"] = _compress_bwd(rv, p["cmp_pe_v"], p["cmp_w1_v"], p["cmp_w2_v"], hid_v, dv_cmp)
    dq = _rope([dq_sel, dq_win], 0, ATT_WIDTH, cos, sin, -1.0, _MXU, "rope_dq", extra=(dq_cmp, 0))
    dkv = _dkv_post(dks, dvs, dkw, dvw, cos, sin)
    dproj = jnp.concatenate([dz, dxbc, dq] + [t.astype(_MXU) for t in (_from_rows16(drk), _from_rows16(drv))] + [dkv], axis=1)
    dsmall = jnp.concatenate([ddtr, dgate, jnp.zeros((S, W_SMALL - SSD_HEADS - 3 * N_HEADS), F32)], axis=1).astype(_MXU)
    g["w_main"] = _mm(u, dproj, "tn", _MXU, "dw_in")
    g["w_small"] = _mm(u, dsmall, "tn", F32, "dw_in_small")
    du = _mm(dproj, p["w_main"], "nt", F32, "du_main", after=ready(_LAST))
    du = _mm(dsmall, p["w_small"], "nt", F32, "du_small", res=du)
    grad_x, _, g["attn_norm_w"] = _rms_bwd(du, x, rs1, p["attn_norm_w"], dh1, "attn_norm_bwd")
    return loss, grad_x, g


MESH_ID = pl.DeviceIdType.MESH


def _my_coords():
    return lax.axis_index("x"), lax.axis_index("y"), lax.axis_index("c")


def _flat_id(px, py, pc):
    return 4 * px + 2 * py + pc


def _peer(k):
    mx, my, mc = _my_coords()
    return (1 - mx if k & 4 else mx, 1 - my if k & 2 else my, 1 - mc if k & 1 else mc)


def _exchange(arrs, scatter, name, after=()):
    n, na = len(arrs), len(after)
    scatter = [scatter] * n if isinstance(scatter, bool) else list(scatter)

    def body(*refs):
        ins, outs = refs[:n], refs[n + na:2 * n + na]
        send_sems, recv_sems, local_sems = refs[2 * n + na:]
        me = _flat_id(*_my_coords())
        copies = []
        for i in range(n):
            src_me = ins[i].at[me] if scatter[i] else ins[i]
            local = pltpu.make_async_copy(src_me, outs[i].at[me], local_sems.at[i])
            local.start()
            copies.append(local)
        for k in range(1, N_DEV):
            peer = _peer(k)
            for i in range(n):
                src = ins[i].at[_flat_id(*peer)] if scatter[i] else ins[i]
                cp = pltpu.make_async_remote_copy(src_ref=src, dst_ref=outs[i].at[me], send_sem=send_sems.at[i * 7 + k - 1],
                                                  recv_sem=recv_sems.at[i * 7 + k - 1], device_id=peer, device_id_type=MESH_ID)
                cp.start()
                copies.append(cp)
        for cp in copies:
            cp.wait()

    any_spec = pl.BlockSpec(memory_space=pl.ANY)
    out_shape = [jax.ShapeDtypeStruct(a.shape if sc else (N_DEV,) + a.shape, a.dtype) for a, sc in zip(arrs, scatter)]
    return pl.pallas_call(
        body, name=name, in_specs=[any_spec] * (n + na), out_specs=[any_spec] * n, out_shape=out_shape,
        scratch_shapes=[pltpu.SemaphoreType.DMA((n * 7,)), pltpu.SemaphoreType.DMA((n * 7,)), pltpu.SemaphoreType.DMA((n,))],
        compiler_params=pltpu.CompilerParams(has_side_effects=True))(*arrs, *after)


def _gather_two_level(arrs, name):
    n = len(arrs)

    def body(*refs):
        ins, outs = refs[:n], refs[n:2 * n]
        send_sems, recv_sems, local_sems = refs[2 * n:]
        x, y, c = _my_coords()
        me, sibling = (x, y, c), (x, y, 1 - c)
        chips = [(1 - x, y), (x, 1 - y), (1 - x, 1 - y)]

        def copy(i, k, block, to, src=None):
            slot = outs[i].at[_flat_id(*block)]
            return pltpu.make_async_remote_copy(src_ref=slot if src is None else src, dst_ref=slot, send_sem=send_sems.at[i * 7 + k],
                                                recv_sem=recv_sems.at[i * 7 + k], device_id=to, device_id_type=MESH_ID)

        mine = [pltpu.make_async_copy(ins[i], outs[i].at[_flat_id(*me)], local_sems.at[i]) for i in range(n)]
        for cp in mine:
            cp.start()
        first = []
        for j, chip in enumerate(chips):
            first += [copy(i, 1 + j, me, (*chip, c), src=ins[i]) for i in range(n)]
        first += [copy(i, 0, me, sibling, src=ins[i]) for i in range(n)]
        for cp in first:
            cp.start()
        passed = []
        for j, chip in enumerate(chips):
            for i in range(n):
                copy(i, 1 + j, (*chip, c), me).wait_recv()
                passed.append(copy(i, 4 + j, (*chip, c), sibling))
                passed[-1].start()
        for i in range(n):
            copy(i, 0, sibling, me).wait_recv()
        for j, chip in enumerate(chips):
            for i in range(n):
                copy(i, 4 + j, (*chip, 1 - c), me).wait_recv()
        for cp in first + passed:
            cp.wait_send()
        for cp in mine:
            cp.wait()

    any_spec = pl.BlockSpec(memory_space=pl.ANY)
    return pl.pallas_call(
        body, name=name, in_specs=[any_spec] * n, out_specs=[any_spec] * n,
        out_shape=[jax.ShapeDtypeStruct((N_DEV,) + a.shape, a.dtype) for a in arrs],
        scratch_shapes=[pltpu.SemaphoreType.DMA((n * 7,)), pltpu.SemaphoreType.DMA((n * 7,)), pltpu.SemaphoreType.DMA((n,))],
        compiler_params=pltpu.CompilerParams(has_side_effects=True))(*arrs)


_HBM = pl.BlockSpec(memory_space=pltpu.HBM)
_SEM = pl.BlockSpec(memory_space=pltpu.SEMAPHORE)
_EFFECT = pltpu.SideEffectType.DATAFLOW_SIDE_EFFECTING


def _split_copies(ins, lands, send_sems, recv_sems, scatter):
    me = _flat_id(*_my_coords())
    out = []
    for k in range(1, N_DEV):
        peer = _peer(k)
        for i in range(len(ins)):
            src = ins[i].at[_flat_id(*peer)] if scatter else ins[i]
            out.append(pltpu.make_async_remote_copy(src_ref=src, dst_ref=lands[i].at[me], send_sem=send_sems.at[i * 7 + k - 1],
                                                    recv_sem=recv_sems.at[i * 7 + k - 1], device_id=peer, device_id_type=MESH_ID))
    return out


def _split_start(arrs, scatter, name, after=()):
    n, na = len(arrs), len(after)

    def body(*refs):
        for cp in _split_copies(refs[:n], refs[n:2 * n], refs[2 * n + na], refs[2 * n + na + 1], scatter):
            cp.start()
        refs[-1][...] = jnp.zeros_like(refs[-1])

    land_shapes = [a.shape if scatter else (N_DEV,) + a.shape for a in arrs]
    out_shape = ((pltpu.SemaphoreType.DMA((n * 7,)), pltpu.SemaphoreType.DMA((n * 7,)))
                 + tuple(pltpu.HBM(a.shape, a.dtype) for a in arrs) + tuple(pltpu.HBM(s, a.dtype) for s, a in zip(land_shapes, arrs))
                 + (jax.ShapeDtypeStruct((8, 128), F32),))
    operands = ([pltpu.with_memory_space_constraint(a, pltpu.HBM) for a in arrs]
                + [pltpu.with_memory_space_constraint(lax.empty(s, a.dtype), pltpu.HBM) for s, a in zip(land_shapes, arrs)])
    res = pl.pallas_call(
        body, name=name, out_shape=out_shape, in_specs=[_HBM] * (2 * n) + [pl.BlockSpec(memory_space=pl.ANY)] * na,
        out_specs=(_SEM, _SEM) + (_HBM,) * (2 * n) + (pl.BlockSpec(memory_space=pltpu.VMEM),),
        input_output_aliases={i: 2 + i for i in range(2 * n)},
        compiler_params=pltpu.CompilerParams(has_side_effects=_EFFECT))(*operands, *after)
    return dict(send=res[0], recv=res[1], ins=list(res[2:2 + n]), lands=list(res[2 + n:2 + 2 * n]), token=res[-1])


def _split_wait(st, scatter, after, name):
    n = len(st["ins"])

    def body(*refs):
        for cp in _split_copies(refs[:n], refs[n:2 * n], refs[2 * n], refs[2 * n + 1], scatter):
            cp.wait_send()
            cp.wait_recv()

    arrs = st["ins"] + st["lands"]
    res = pl.pallas_call(
        body, name=name, out_shape=tuple(pltpu.HBM(a.shape, a.dtype) for a in arrs),
        in_specs=[_HBM] * (2 * n) + [_SEM, _SEM] + [pl.BlockSpec(memory_space=pl.ANY)] * len(after), out_specs=(_HBM,) * (2 * n),
        input_output_aliases={i: i for i in range(2 * n)},
        compiler_params=pltpu.CompilerParams(has_side_effects=_EFFECT))(*arrs, st["send"], st["recv"], *after)
    me = _flat_id(*_my_coords())
    out = []
    for src, land in zip(res[:n], res[n:]):
        own = lax.dynamic_index_in_dim(src, me, 0, keepdims=True) if scatter else src[None]
        out.append(lax.dynamic_update_slice_in_dim(land, own, me, 0))
    return out


def _adam_step(p_ref, w_ref, m_ref, v_ref, g_ref, d_ref, nm_ref, nv_ref):
    g = p_ref[0].astype(F32)
    for j in range(1, p_ref.shape[0]):
        g = g + p_ref[j].astype(F32)
    g_ref[...] = g
    nm = ADAM_B1 * m_ref[...] + (1.0 - ADAM_B1) * g
    nv = ADAM_B2 * v_ref[...] + (1.0 - ADAM_B2) * (g * g)
    nm_ref[...] = nm
    nv_ref[...] = nv
    m_hat = nm / (1.0 - ADAM_B1 ** ADAM_STEP)
    v_hat = nv / (1.0 - ADAM_B2 ** ADAM_STEP)
    d_ref[...] = -ADAM_LR * (m_hat / (jnp.sqrt(v_hat) + ADAM_EPS) + ADAM_WD * w_ref[...])


def _adam_sum(parts, w, m, v, name):
    P, R, C = parts.shape
    tr = _pick(R, (256, 128, 64, 32, 8)) if C <= 1024 else _pick(R, (128, 64, 32, 8))
    blk = pl.BlockSpec((tr, C), lambda i: (i, 0))
    return pl.pallas_call(
        functools.partial(_adam_step), name=name, grid=(R // tr,),
        in_specs=[pl.BlockSpec((P, tr, C), lambda i: (0, i, 0)), blk, blk, blk],
        out_specs=[blk] * 4, out_shape=[jax.ShapeDtypeStruct((R, C), F32)] * 4, compiler_params=_cp(("parallel",)))(parts, w, m, v)


def _adam_small(loss_parts, parts, ws, ms, vs):
    n = len(parts)

    def body(*refs):
        loss_ref, ins, outs, total_ref = refs[0], refs[1:4 * n + 1], refs[4 * n + 1:-1], refs[-1]
        for i in range(n):
            _adam_step(ins[i], ins[n + i], ins[2 * n + i], ins[3 * n + i], *outs[4 * i:4 * i + 4])
        total = loss_ref[0]
        for d in range(1, N_DEV):
            total = total + loss_ref[d]
        total_ref[...] = total

    out_shape = [jax.ShapeDtypeStruct(w.shape, F32) for w in ws for _ in range(4)] + [jax.ShapeDtypeStruct(loss_parts.shape[1:], F32)]
    res = pl.pallas_call(body, name="adam_small", out_shape=out_shape)(loss_parts, *parts, *ws, *ms, *vs)
    return res[-1], [tuple(res[4 * i:4 * i + 4]) for i in range(n)]


_WEIGHTS = ["attn_norm_w", "w_in", "conv_w", "conv_b", "dt_bias", "a_log", "d_skip", "ssd_norm_w", "cmp_w1_k", "cmp_w2_k",
            "cmp_w1_v", "cmp_w2_v", "cmp_pe_k", "cmp_pe_v", "w_out", "ffn_norm_w", "w_gate", "w_up", "w_down", "final_norm_w"]
_BIG = ["w_in", "w_gate", "w_up", "w_down", "w_out", "cmp_w1_k", "cmp_w1_v"]
_COL_SHARDED = ("w_in", "w_gate", "w_up")
_REPLICATED = ["attn_norm_w", "conv_b", "dt_bias", "a_log", "d_skip", "ssd_norm_w", "cmp_pe_k", "cmp_pe_v", "ffn_norm_w",
               "final_norm_w"]
_SMALL_SHARDED = ["conv_w", "cmp_w2_k", "cmp_w2_v"]


def _cols_to_slabs(g):
    R = g.shape[0]
    return g.reshape(R, N_DEV, -1).transpose(1, 0, 2)


def _slabs_to_cols(s):
    return s.transpose(1, 0, 2).reshape(s.shape[1], -1)


def kernel(x, attn_norm_w, w_in, conv_w, conv_b, dt_bias, a_log, d_skip, ssd_norm_w, cmp_w1_k, cmp_w2_k, cmp_w1_v, cmp_w2_v, cmp_pe_k, cmp_pe_v, w_out, ffn_norm_w, w_gate, w_up, w_down, final_norm_w, loss_target, m_attn_norm_w, m_w_in, m_conv_w, m_conv_b, m_dt_bias, m_a_log, m_d_skip, m_ssd_norm_w, m_cmp_w1_k, m_cmp_w2_k, m_cmp_w1_v, m_cmp_w2_v, m_cmp_pe_k, m_cmp_pe_v, m_w_out, m_ffn_norm_w, m_w_gate, m_w_up, m_w_down, m_final_norm_w, v_attn_norm_w, v_w_in, v_conv_w, v_conv_b, v_dt_bias, v_a_log, v_d_skip, v_ssd_norm_w, v_cmp_w1_k, v_cmp_w2_k, v_cmp_w1_v, v_cmp_w2_v, v_cmp_pe_k, v_cmp_pe_v, v_w_out, v_ffn_norm_w, v_w_gate, v_w_up, v_w_down, v_final_norm_w):
    a = dict(locals())

    shard = {n: a[n][0].astype(_MXU) for n in _BIG}
    got = _gather_two_level([shard[n] for n in _EARLY] + [cmp_w2_k[0], cmp_w2_v[0], conv_w[0]], "gather_early")
    st_late = _split_start([shard[n] for n in _LATE], False, "gather_late_start", after=(got[0],))

    def assemble(n, t):
        return _cols_from_slabs(t) if n in _COL_SHARDED else t.reshape(-1, t.shape[-1])

    p = dict(attn_norm_w=attn_norm_w, conv_b=conv_b, dt_bias=dt_bias, a_log=a_log, d_skip=d_skip, ssd_norm_w=ssd_norm_w,
             cmp_pe_k=cmp_pe_k.reshape(1, -1), cmp_pe_v=cmp_pe_v.reshape(1, -1), ffn_norm_w=ffn_norm_w,
             final_norm_w=final_norm_w.reshape(1, -1))

    w_main, w_small = _w_in_from_slabs(got[0])
    p.update(before_in_proj=st_late["token"],
             w_main=w_main, w_small=w_small, cmp_w1_k=assemble("cmp_w1_k", got[1]), cmp_w1_v=assemble("cmp_w1_v", got[2]),
             cmp_w2_k=assemble("cmp_w2_k", got[3]).astype(_MXU), cmp_w2_v=assemble("cmp_w2_v", got[4]).astype(_MXU),
             conv_w=_slabs_to_cols(got[5]))

    def late_weights(after):
        got_late = _split_wait(st_late, False, (after,), "gather_late_wait")
        return {n: assemble(n, t) for n, t in zip(_LATE, got_late)}

    def slabs_of(g, n):
        if n == "w_in":
            return _w_in_to_slabs(g["w_main"], g["w_small"])
        return _slabs_from_cols(g[n]) if n in _COL_SHARDED else g[n].reshape(N_DEV, -1, g[n].shape[-1])

    started = []

    def grads_ready(names, g):
        started.append((names, _split_start([slabs_of(g, n) for n in names], True, "scatter_grads_start_%d" % len(started))))
        return started[-1][1]["token"]

    loss_part, grad_x, g = _local_step(x[0], loss_target[0], p, late_weights, grads_ready)

    out, after = {}, (started[-1][1]["token"],)
    for i, (names, st) in enumerate(started):
        if i == len(started) - 1:
            after = after + (grad_x,)
        received = _split_wait(st, True, after, "scatter_grads_wait_%d" % i)
        for n, parts in zip(names, received):
            out[n] = _adam_sum(parts, a[n][0], a["m_" + n][0], a["v_" + n][0], "adam_" + n)
        after = (out[names[-1]][0],)

    small_names = _REPLICATED + _SMALL_SHARDED
    partials = [g[n] for n in _REPLICATED] + [_cols_to_slabs(g["conv_w"])] + [
        g[n].reshape(N_DEV, -1, g[n].shape[-1]) for n in ("cmp_w2_k", "cmp_w2_v")]
    gathered = _exchange([loss_part] + partials, [False] * (1 + len(_REPLICATED)) + [True] * len(_SMALL_SHARDED),
                         "exchange_small_grads", after=(received[0],))
    shapes2d = [t.shape[1:] for t in gathered[1:]]
    loss, res_small = _adam_small(gathered[0], gathered[1:],
                                  *[[a[pre + n].reshape(s) for n, s in zip(small_names, shapes2d)] for pre in ("", "m_", "v_")])
    for n, r in zip(small_names, res_small):
        out[n] = r

    outs = [loss[0, 0], grad_x[None]]
    for j in range(4):
        for n in _WEIGHTS:
            outs.append(out[n][j].reshape(a[n].shape))
    return tuple(outs)
```

```python
import functools

import numpy as np
import jax
import jax.numpy as jnp
from jax import lax
from jax.experimental import pallas as pl
from jax.experimental.pallas import tpu as pltpu

F32 = jnp.float32
_MXU = jnp.bfloat16

N_DEV = 8
D_MODEL = 2048
SSD_WIDTH = 1024
ATT_WIDTH = 1024
SSD_HEADS = 16
SSD_P = 64
SSD_N = 128
SSD_L = 128
SSD_G = 2
CONV_CH = 1536
CONV_K = 4
HD = 64
N_HEADS = 16
N_KV = 4
GRP = 4
CMP_HID = 256
SEL_BLOCK = 64
N_SELECT = 16
WINDOW = 512
ROPE_DIM = 16
ROPE_THETA = 500000.0
D_FF = 5632
EPS = 1e-6
NEG = -1e30
FORCE = 1e4
SCALE = HD ** -0.5
D_IN = 5184
W_MAIN = 5120
W_SMALL = 128
VMEM_LIMIT = 52 * 1024 * 1024

ADAM_LR, ADAM_B1, ADAM_B2, ADAM_EPS, ADAM_WD, ADAM_STEP = 0.001, 0.9, 0.999, 1e-08, 0.01, 10


def _pick(n, cands):
    for c in cands:
        if n % c == 0:
            return c
    return n


def _cp(sem=None):
    return pltpu.CompilerParams(dimension_semantics=sem, vmem_limit_bytes=VMEM_LIMIT)


def _sigmoid(x):
    return 1.0 / (1.0 + jnp.exp(-x))


def _dot(a, b, dims, split=None):
    dn = {"nn": (((1,), (0,)), ((), ())), "nt": (((1,), (1,)), ((), ())), "tn": (((0,), (0,)), ((), ()))}[dims]
    mm = lambda x, y: lax.dot_general(x.astype(_MXU), y.astype(_MXU), dn, preferred_element_type=F32)
    if split is None:
        return mm(a, b)
    x = (a if split == "a" else b).astype(F32)
    hi = x.astype(_MXU)
    lo = x - hi.astype(F32)
    return mm(hi, b) + mm(lo, b) if split == "a" else mm(a, hi) + mm(a, lo)


LANE = 128
MM_TILE = 1024
MM_K_WHOLE = 2048
MM_K_STEP = 2816
TN_ACC_ELEMS = 3 * 2 ** 20
TN_K_STEP = 512


def _largest_tile(n, cap):
    if n <= cap:
        return n
    best = LANE
    for t in range(LANE, cap + 1, LANE):
        if n % t == 0:
            best = t
    return best


def _mm_tiles(mode, M, N, K):
    if mode == "tn":
        tm = _largest_tile(M, 2 * MM_TILE)
        return tm, _largest_tile(N, TN_ACC_ELEMS // tm), _largest_tile(K, TN_K_STEP)
    tk = K if K <= MM_K_WHOLE else _largest_tile(K, MM_K_STEP)
    return _largest_tile(M, MM_TILE), _largest_tile(N, MM_TILE), tk


def _mm(a, b, mode, out_dtype, name, res=None, after=None):
    if mode == "nn":
        (M, K), N = a.shape, b.shape[1]
    elif mode == "nt":
        (M, K), N = a.shape, b.shape[0]
    else:
        (K, M), N = a.shape, b.shape[1]
    tm, tn, tk = _mm_tiles(mode, M, N, K)
    nk = K // tk
    a_spec = pl.BlockSpec((tk, tm), lambda i, j, k: (k, i)) if mode == "tn" else pl.BlockSpec((tm, tk), lambda i, j, k: (i, k))
    b_spec = pl.BlockSpec((tn, tk), lambda i, j, k: (j, k)) if mode == "nt" else pl.BlockSpec((tk, tn), lambda i, j, k: (k, j))
    o_spec = pl.BlockSpec((tm, tn), lambda i, j, k: (i, j))

    def finish(r, r_ref, o_ref):
        if res is not None:
            r = r + r_ref[...].astype(F32)
        o_ref[...] = r.astype(out_dtype)

    def body_one_step(*refs):
        a_ref, b_ref, o_ref = refs[0], refs[1], refs[-1]
        finish(_dot(a_ref[...], b_ref[...], mode), refs[2], o_ref)

    def body(*refs):
        a_ref, b_ref, o_ref, acc = refs[0], refs[1], refs[-2], refs[-1]
        k = pl.program_id(2)

        @pl.when(k == 0)
        def _():
            acc[...] = jnp.zeros_like(acc)

        acc[...] += _dot(a_ref[...], b_ref[...], mode)

        @pl.when(k == nk - 1)
        def _():
            finish(acc[...], refs[2], o_ref)

    ins, specs = [a, b], [a_spec, b_spec]
    if res is not None:
        ins.append(res)
        specs.append(o_spec)
    if after is not None:
        ins.append(after)
        specs.append(pl.BlockSpec(memory_space=pl.ANY))
    return pl.pallas_call(
        body_one_step if nk == 1 else body, name=name, grid=(M // tm, N // tn, nk), in_specs=specs, out_specs=o_spec,
        out_shape=jax.ShapeDtypeStruct((M, N), out_dtype), scratch_shapes=[] if nk == 1 else [pltpu.VMEM((tm, tn), F32)],
        compiler_params=_cp(("parallel", "parallel", "arbitrary")))(*ins)


def _ffn_up(v, w_gate, w_up):
    S, D = v.shape
    F = w_gate.shape[1]
    tm, tn = _largest_tile(S, MM_TILE), _largest_tile(F, MM_TILE // 2)

    def body(v_ref, wg_ref, wu_ref, gt_ref, up_ref, act_ref):
        vv = v_ref[...]
        g = _dot(vv, wg_ref[...], "nn")
        u = _dot(vv, wu_ref[...], "nn")
        gt_ref[...] = g
        up_ref[...] = u
        act_ref[...] = (g * _sigmoid(g) * u).astype(act_ref.dtype)

    o_spec = pl.BlockSpec((tm, tn), lambda i, j: (i, j))
    w_spec = pl.BlockSpec((D, tn), lambda i, j: (0, j))
    return pl.pallas_call(
        body, name="ffn_up", grid=(S // tm, F // tn),
        in_specs=[pl.BlockSpec((tm, D), lambda i, j: (i, 0)), w_spec, w_spec], out_specs=[o_spec, o_spec, o_spec],
        out_shape=[jax.ShapeDtypeStruct((S, F), F32), jax.ShapeDtypeStruct((S, F), F32), jax.ShapeDtypeStruct((S, F), _MXU)],
        compiler_params=_cp(("parallel", "parallel")))(v, w_gate, w_up)


def _ffn_dv(dgt, dup, w_gate, w_up, after):
    S, F = dgt.shape
    D = w_gate.shape[0]
    tm, tn, _ = _mm_tiles("nt", S, D, F)
    tk = _largest_tile(F, MM_K_STEP // 2)
    nk = F // tk

    def body(g_ref, u_ref, wg_ref, wu_ref, *rest):
        o_ref, acc = rest[-2], rest[-1]
        k = pl.program_id(2)

        @pl.when(k == 0)
        def _():
            acc[...] = jnp.zeros_like(acc)

        acc[...] += _dot(g_ref[...], wg_ref[...], "nt") + _dot(u_ref[...], wu_ref[...], "nt")

        @pl.when(k == nk - 1)
        def _():
            o_ref[...] = acc[...]

    a_spec = pl.BlockSpec((tm, tk), lambda i, j, k: (i, k))
    w_spec = pl.BlockSpec((tn, tk), lambda i, j, k: (j, k))
    ins, specs = [dgt, dup, w_gate, w_up], [a_spec, a_spec, w_spec, w_spec]
    if after is not None:
        ins.append(after)
        specs.append(pl.BlockSpec(memory_space=pl.ANY))
    return pl.pallas_call(
        body, name="ffn_dv", grid=(S // tm, D // tn, nk), in_specs=specs, out_specs=pl.BlockSpec((tm, tn), lambda i, j, k: (i, j)),
        out_shape=jax.ShapeDtypeStruct((S, D), F32), scratch_shapes=[pltpu.VMEM((tm, tn), F32)],
        compiler_params=_cp(("parallel", "parallel", "arbitrary")))(*ins)


def _ffn_dact(dh2, w_down, gt, up):
    S, D = dh2.shape
    F = w_down.shape[0]
    tm, tn = _largest_tile(S, MM_TILE), _largest_tile(F, MM_TILE // 2)

    def body(d_ref, w_ref, gt_ref, up_ref, dg_ref, du_ref):
        da, g, u = _dot(d_ref[...], w_ref[...], "nt"), gt_ref[...], up_ref[...]
        s = _sigmoid(g)
        dg_ref[...] = (da * u * (s * (1.0 + g * (1.0 - s)))).astype(dg_ref.dtype)
        du_ref[...] = (da * (g * s)).astype(du_ref.dtype)

    o_spec = pl.BlockSpec((tm, tn), lambda i, j: (i, j))
    return pl.pallas_call(
        body, name="ffn_dact", grid=(S // tm, F // tn),
        in_specs=[pl.BlockSpec((tm, D), lambda i, j: (i, 0)), pl.BlockSpec((tn, D), lambda i, j: (j, 0)), o_spec, o_spec],
        out_specs=[o_spec, o_spec],
        out_shape=[jax.ShapeDtypeStruct((S, F), _MXU), jax.ShapeDtypeStruct((S, F), _MXU)],
        compiler_params=_cp(("parallel", "parallel")))(dh2, w_down, gt, up)


def _rms_fwd(x, w, name):
    S, D = x.shape
    tr = _pick(S, (256, 128))

    def body(x_ref, w_ref, xn_ref, rs_ref):
        xv = x_ref[...]
        rs = lax.rsqrt(jnp.mean(xv * xv, axis=-1, keepdims=True) + EPS)
        xn_ref[...] = ((xv * rs) * w_ref[...]).astype(xn_ref.dtype)
        rs_ref[...] = rs

    return pl.pallas_call(
        body, name=name, grid=(S // tr,),
        in_specs=[pl.BlockSpec((tr, D), lambda i: (i, 0)), pl.BlockSpec((1, D), lambda i: (0, 0))],
        out_specs=[pl.BlockSpec((tr, D), lambda i: (i, 0)), pl.BlockSpec((tr, 1), lambda i: (i, 0))],
        out_shape=[jax.ShapeDtypeStruct((S, D), _MXU), jax.ShapeDtypeStruct((S, 1), F32)],
        compiler_params=_cp(("parallel",)))(x, w)


def _rms_bwd(dyn, x, rs, w, res, name):
    S, D = x.shape
    tr = _pick(S, (256, 128))

    def body(dy_ref, x_ref, rs_ref, w_ref, res_ref, dx_ref, dxb_ref, dw_ref):
        @pl.when(pl.program_id(0) == 0)
        def _():
            dw_ref[...] = jnp.zeros_like(dw_ref)

        dy, r = dy_ref[...].astype(F32), rs_ref[...]
        xhat = x_ref[...] * r
        dw_ref[...] += jnp.sum(dy * xhat, axis=0, keepdims=True)
        dxhat = dy * w_ref[...]
        dx = res_ref[...] + r * (dxhat - xhat * jnp.mean(dxhat * xhat, axis=-1, keepdims=True))
        dx_ref[...] = dx
        dxb_ref[...] = dx.astype(dxb_ref.dtype)

    row = pl.BlockSpec((tr, D), lambda i: (i, 0))
    vec = pl.BlockSpec((1, D), lambda i: (0, 0))
    return pl.pallas_call(
        body, name=name, grid=(S // tr,),
        in_specs=[row, row, pl.BlockSpec((tr, 1), lambda i: (i, 0)), vec, row], out_specs=[row, row, vec],
        out_shape=[jax.ShapeDtypeStruct((S, D), F32), jax.ShapeDtypeStruct((S, D), _MXU), jax.ShapeDtypeStruct((1, D), F32)],
        compiler_params=_cp(("arbitrary",)))(dyn, x, rs, w, res)


ROWS_TILE = 512
ROWS_K_STEP = 512
ROWS_EPILOGUE = 128
ROWS_VMEM_LIMIT = 60 * 1024 * 1024


def _mm_rows(pairs, mode, name, rows, cols, vecs, epilogue, row_outs, acc_outs, after=None):
    (S, K), npairs = pairs[0][0].shape, len(pairs)
    N = pairs[0][1].shape[1] if mode == "nn" else pairs[0][1].shape[0]
    tm, tk = _largest_tile(S, ROWS_TILE), _largest_tile(K, ROWS_K_STEP)
    nk = K // tk
    n_in = 2 * npairs + len(rows) + len(cols) + len(vecs) + (after is not None)

    def body(*refs):
        ab = refs[:2 * npairs]
        rest = list(refs[2 * npairs:n_in - (after is not None)])
        row_refs = [rest.pop(0) for _ in rows]
        col_refs = [rest.pop(0) for _ in cols]
        vec_refs = [rest.pop(0) for _ in vecs]
        outs = list(refs[n_in:-1])
        ro_refs, ao_refs, acc = outs[:len(row_outs)], outs[len(row_outs):], refs[-1]
        i, k = pl.program_id(0), pl.program_id(1)

        @pl.when(k == 0)
        def _():
            acc[...] = jnp.zeros_like(acc)

        part = _dot(ab[0][...], ab[1][...], mode)
        for j in range(1, npairs):
            part = part + _dot(ab[2 * j][...], ab[2 * j + 1][...], mode)
        acc[...] += part

        @pl.when(k == nk - 1)
        def _():
            @pl.when(i == 0)
            def _():
                for r in ao_refs:
                    r[...] = jnp.zeros_like(r)

            for r0 in range(0, tm, ROWS_EPILOGUE):
                sl = slice(r0, r0 + ROWS_EPILOGUE)
                ro_vals, ao_incs = epilogue(acc[sl, :], [r[sl, :] for r in row_refs], [c[sl, :] for c in col_refs],
                                            [v[...] for v in vec_refs])
                for ref, val in zip(ro_refs, ro_vals):
                    ref[sl, :] = val.astype(ref.dtype)
                for ref, inc in zip(ao_refs, ao_incs):
                    ref[...] += inc

    a_spec = pl.BlockSpec((tm, tk), lambda i, k: (i, k))
    b_spec = pl.BlockSpec((tk, N), lambda i, k: (k, 0)) if mode == "nn" else pl.BlockSpec((N, tk), lambda i, k: (0, k))
    row_spec = pl.BlockSpec((tm, N), lambda i, k: (i, 0))
    ins = [t for pair in pairs for t in pair] + list(rows) + list(cols) + list(vecs)
    specs = ([a_spec, b_spec] * npairs + [row_spec] * len(rows) + [pl.BlockSpec((tm, 1), lambda i, k: (i, 0))] * len(cols)
             + [pl.BlockSpec((1, N), lambda i, k: (0, 0))] * len(vecs))
    if after is not None:
        ins.append(after)
        specs.append(pl.BlockSpec(memory_space=pl.ANY))
    return pl.pallas_call(
        body, name=name, grid=(S // tm, nk), in_specs=specs,
        out_specs=[row_spec] * len(row_outs) + [pl.BlockSpec(s, lambda i, k: (0, 0)) for s in acc_outs],
        out_shape=[jax.ShapeDtypeStruct((S, N), d) for d in row_outs] + [jax.ShapeDtypeStruct(s, F32) for s in acc_outs],
        scratch_shapes=[pltpu.VMEM((tm, N), F32)],
        compiler_params=pltpu.CompilerParams(dimension_semantics=("arbitrary", "arbitrary"), vmem_limit_bytes=ROWS_VMEM_LIMIT))(*ins)


def _norm_bwd_epilogue(dy, rows, cols, vecs):
    if len(rows) == 3:
        dy = dy + rows[2]
    r = cols[0]
    xhat = rows[0] * r
    dxhat = dy * vecs[0]
    dx = rows[1] + r * (dxhat - xhat * jnp.mean(dxhat * xhat, axis=-1, keepdims=True))
    return [dx, dx], [jnp.sum(dy * xhat, axis=0, keepdims=True)]


def _final_epilogue(acc, rows, cols, vecs):
    hv, wv = acc + rows[0], vecs[0]
    rs = lax.rsqrt(jnp.mean(hv * hv, axis=-1, keepdims=True) + EPS)
    xhat = hv * rs
    err = xhat * wv - rows[1]
    loss = jnp.broadcast_to(0.5 * jnp.sum(jnp.mean(err * err, axis=-1, keepdims=True), axis=0, keepdims=True), (1, LANE))
    dy = err * (1.0 / hv.shape[-1])
    dxhat = dy * wv
    dh = rs * (dxhat - xhat * jnp.mean(dxhat * xhat, axis=-1, keepdims=True))
    return [dh, dh], [loss, jnp.sum(dy * xhat, axis=0, keepdims=True)]


def _final_loss(h2, w, tgt):
    S, D = h2.shape
    tr = _pick(S, (256, 128))

    def body(h_ref, w_ref, t_ref, loss_ref, dh_ref, dhb_ref, dw_ref):
        @pl.when(pl.program_id(0) == 0)
        def _():
            dw_ref[...] = jnp.zeros_like(dw_ref)
            loss_ref[...] = jnp.zeros_like(loss_ref)

        hv, wv = h_ref[...], w_ref[...]
        rs = lax.rsqrt(jnp.mean(hv * hv, axis=-1, keepdims=True) + EPS)
        xhat = hv * rs
        err = xhat * wv - t_ref[...]
        row = jnp.mean(err * err, axis=-1, keepdims=True)
        loss_ref[...] += jnp.broadcast_to(0.5 * jnp.sum(row, axis=0, keepdims=True), loss_ref.shape)
        dy = err * (1.0 / D)
        dw_ref[...] += jnp.sum(dy * xhat, axis=0, keepdims=True)
        dxhat = dy * wv
        dh = rs * (dxhat - xhat * jnp.mean(dxhat * xhat, axis=-1, keepdims=True))
        dh_ref[...] = dh
        dhb_ref[...] = dh.astype(dhb_ref.dtype)

    row = pl.BlockSpec((tr, D), lambda i: (i, 0))
    vec = pl.BlockSpec((1, D), lambda i: (0, 0))
    return pl.pallas_call(
        body, name="final_loss", grid=(S // tr,), in_specs=[row, vec, row],
        out_specs=[pl.BlockSpec((1, LANE), lambda i: (0, 0)), row, row, vec],
        out_shape=[jax.ShapeDtypeStruct((1, LANE), F32), jax.ShapeDtypeStruct((S, D), F32), jax.ShapeDtypeStruct((S, D), _MXU),
                   jax.ShapeDtypeStruct((1, D), F32)],
        compiler_params=_cp(("arbitrary",)))(h2, w, tgt)


def _shift_rows(x, k, rows):
    if k == 0:
        return x
    S = x.shape[0]
    r = pltpu.roll(x, k % S, axis=0)
    ok = (rows >= k) if k > 0 else (rows < S + k)
    return jnp.where(ok, r, 0.0)


XBC_COL0 = SSD_WIDTH // 128


def _conv_fwd(proj, conv_w, conv_b):
    S = proj.shape[0]
    nct = CONV_CH // 128

    def body(x_ref, w_ref, b_ref, o_ref):
        x = x_ref[...]
        rows = lax.broadcasted_iota(jnp.int32, x.shape, 0)
        c = b_ref[...] + w_ref[3:4, :] * x
        for k in range(1, CONV_K):
            c = c + w_ref[3 - k:4 - k, :] * _shift_rows(x, k, rows)
        o_ref[...] = c * _sigmoid(c)

    return pl.pallas_call(
        body, name="conv_fwd", grid=(nct,),
        in_specs=[pl.BlockSpec((S, 128), lambda j: (0, XBC_COL0 + j)), pl.BlockSpec((CONV_K, 128), lambda j: (0, j)),
                  pl.BlockSpec((1, 128), lambda j: (0, j))],
        out_specs=pl.BlockSpec((S, 128), lambda j: (0, j)),
        out_shape=jax.ShapeDtypeStruct((S, CONV_CH), F32), compiler_params=_cp(("parallel",)))(proj, conv_w, conv_b)


def _conv_bwd(proj, conv_w, conv_b, dxa):
    S = proj.shape[0]
    nct = CONV_CH // 128

    def body(x_ref, w_ref, b_ref, d_ref, dx_ref, dw_ref, db_ref):
        x = x_ref[...]
        rows = lax.broadcasted_iota(jnp.int32, x.shape, 0)
        xs = [_shift_rows(x, k, rows) for k in range(CONV_K)]
        c = b_ref[...] + w_ref[3:4, :] * x
        for k in range(1, CONV_K):
            c = c + w_ref[3 - k:4 - k, :] * xs[k]
        s = _sigmoid(c)
        dc = d_ref[...] * (s * (1.0 + c * (1.0 - s)))
        dx = w_ref[3:4, :] * dc
        for k in range(1, CONV_K):
            dx = dx + w_ref[3 - k:4 - k, :] * _shift_rows(dc, -k, rows)
        dx_ref[...] = dx.astype(dx_ref.dtype)
        for k in range(CONV_K):
            dw_ref[3 - k:4 - k, :] = jnp.sum(dc * xs[k], axis=0, keepdims=True)
        db_ref[...] = jnp.sum(dc, axis=0, keepdims=True)

    col = pl.BlockSpec((S, 128), lambda j: (0, j))
    return pl.pallas_call(
        body, name="conv_bwd", grid=(nct,),
        in_specs=[pl.BlockSpec((S, 128), lambda j: (0, XBC_COL0 + j)), pl.BlockSpec((CONV_K, 128), lambda j: (0, j)),
                  pl.BlockSpec((1, 128), lambda j: (0, j)), col],
        out_specs=[col, pl.BlockSpec((CONV_K, 128), lambda j: (0, j)), pl.BlockSpec((1, 128), lambda j: (0, j))],
        out_shape=[jax.ShapeDtypeStruct((S, CONV_CH), _MXU), jax.ShapeDtypeStruct((CONV_K, CONV_CH), F32),
                   jax.ShapeDtypeStruct((1, CONV_CH), F32)],
        compiler_params=_cp(("parallel",)))(proj, conv_w, conv_b, dxa)


def _ssd_consts():
    L = SSD_L
    r = lax.broadcasted_iota(jnp.int32, (L, L), 0)
    c = lax.broadcasted_iota(jnp.int32, (L, L), 1)
    causal = r >= c
    upper = (r <= c).astype(F32)
    hr = lax.broadcasted_iota(jnp.int32, (SSD_HEADS, SSD_WIDTH), 0)
    hc = lax.broadcasted_iota(jnp.int32, (SSD_HEADS, SSD_WIDTH), 1)
    expand = (lax.shift_right_logical(hc, 6) == hr).astype(F32)
    return causal, causal.astype(F32), upper, expand


def _softplus(x):
    return jnp.maximum(x, 0.0) + jnp.log(1.0 + jnp.exp(-jnp.abs(x)))


def _ssd_scalars(dtr, dt_bias, a_log, tri, upper, expand):
    dt = _softplus(dtr + dt_bias)
    A = -jnp.exp(a_log)
    adt = dt * A
    acum = _dot(tri, adt, "nn", split="b")
    acum_t = _dot(adt, upper, "tn", split="a")
    alast = acum[SSD_L - 1:SSD_L, :]
    e = jnp.exp(acum)
    wdec = jnp.exp(alast - acum)
    gam = jnp.exp(alast)
    ex = lambda t: _dot(t, expand, "nn", split="a")
    gam8 = jnp.broadcast_to(gam, (8, SSD_HEADS))
    return dt, A, acum, acum_t, e, wdec, gam, ex(dt), ex(e), ex(wdec), ex(gam8)[0:1, :]


def _ssd_fwd(proj, proj_small, xa, dt_bias, a_log, d_skip, norm_w):
    S = proj.shape[0]
    L, N, W = SSD_L, SSD_N, SSD_WIDTH
    nc = S // L

    def body(z_ref, xa_ref, dtr_ref, dtb_ref, al_ref, dsk_ref, nw_ref, yo_ref, y_ref, rs_ref, hs_ref, h_scr, y_scr):
        @pl.when(pl.program_id(0) == 0)
        def _():
            h_scr[...] = jnp.zeros_like(h_scr)

        causal, tri, upper, expand = _ssd_consts()
        dt, A, acum, acum_t, e, wdec, gam, dtE, eE, wE, gamE = _ssd_scalars(dtr_ref[:, 0:SSD_HEADS], dtb_ref[...], al_ref[...], tri, upper, expand)
        xs = xa_ref[:, 0:W]
        X = xs * dtE
        XW = X * wE
        hs_ref[0] = h_scr[...]
        for g in range(SSD_G):
            gs = slice(g * 512, (g + 1) * 512)
            Bg = xa_ref[:, W + g * N:W + (g + 1) * N]
            Cg = xa_ref[:, W + SSD_G * N + g * N:W + SSD_G * N + (g + 1) * N]
            Hg = h_scr[:, gs]
            CB = _dot(Cg, Bg, "nt")
            yoff = _dot(Cg, Hg, "nn") * eE[:, gs]
            st = _dot(Bg, XW[:, gs], "tn")
            for j in range(8):
                h = g * 8 + j
                hsl = slice(h * SSD_P, (h + 1) * SSD_P)
                lam = jnp.exp(jnp.where(causal, acum[:, h:h + 1] - acum_t[h:h + 1, :], -jnp.inf))
                y_scr[:, hsl] = _dot(CB * lam, X[:, hsl], "nn") + yoff[:, j * SSD_P:(j + 1) * SSD_P]
            h_scr[:, gs] = gamE[:, gs] * Hg + st
        dskE = _dot(jnp.broadcast_to(dsk_ref[...], (8, SSD_HEADS)), expand, "nn", split="a")[0:1, :]
        y = y_scr[...] + dskE * xs
        y_ref[...] = y
        zv = z_ref[...]
        yg = y * (zv * _sigmoid(zv))
        rs = lax.rsqrt(jnp.mean(yg * yg, axis=-1, keepdims=True) + EPS)
        rs_ref[...] = rs
        yo_ref[...] = ((yg * rs) * nw_ref[...]).astype(yo_ref.dtype)

    p16 = pl.BlockSpec((1, SSD_HEADS), lambda c: (0, 0))
    return pl.pallas_call(
        body, name="ssd_fwd", grid=(nc,),
        in_specs=[pl.BlockSpec((L, W), lambda c: (c, 0)), pl.BlockSpec((L, CONV_CH), lambda c: (c, 0)),
                  pl.BlockSpec((L, W_SMALL), lambda c: (c, 0)), p16, p16, p16, pl.BlockSpec((1, W), lambda c: (0, 0))],
        out_specs=[pl.BlockSpec((L, W), lambda c: (c, 0)), pl.BlockSpec((L, W), lambda c: (c, 0)),
                   pl.BlockSpec((L, 1), lambda c: (c, 0)), pl.BlockSpec((1, N, W), lambda c: (c, 0, 0))],
        out_shape=[jax.ShapeDtypeStruct((S, W), _MXU), jax.ShapeDtypeStruct((S, W), F32), jax.ShapeDtypeStruct((S, 1), F32),
                   jax.ShapeDtypeStruct((nc, N, W), F32)],
        scratch_shapes=[pltpu.VMEM((N, W), F32), pltpu.VMEM((L, W), F32)],
        compiler_params=_cp(("arbitrary",)))(proj, xa, proj_small, dt_bias, a_log, d_skip, norm_w)


def _ssd_bwd(dmixed, proj, proj_small, xa, y, rs2, hs, dt_bias, a_log, d_skip, norm_w):
    S = proj.shape[0]
    L, N, W, H = SSD_L, SSD_N, SSD_WIDTH, SSD_HEADS
    nc = S // L

    def body(dyo_ref, z_ref, xa_ref, dtr_ref, y_ref, rs_ref, hs_ref, dtb_ref, al_ref, dsk_ref, nw_ref,
             dz_ref, dxa_ref, ddtr_ref, ddtb_ref, dal_ref, ddsk_ref, dnw_ref, dh_scr, dx_scr):
        @pl.when(pl.program_id(0) == 0)
        def _():
            dh_scr[...] = jnp.zeros_like(dh_scr)
            ddtb_ref[...] = jnp.zeros_like(ddtb_ref)
            dal_ref[...] = jnp.zeros_like(dal_ref)
            ddsk_ref[...] = jnp.zeros_like(ddsk_ref)
            dnw_ref[...] = jnp.zeros_like(dnw_ref)

        causal, tri, upper, expand = _ssd_consts()
        heads = lambda t: _dot(t, expand, "nt", split="a")
        onehot = lambda h: (lax.broadcasted_iota(jnp.int32, (1, H), 1) == h).astype(F32)

        zv, yv, rs = z_ref[...], y_ref[...], rs_ref[...]
        sz = _sigmoid(zv)
        zs = zv * sz
        xhat = (yv * zs) * rs
        dyo = dyo_ref[...].astype(F32)
        dnw_ref[...] += jnp.sum(dyo * xhat, axis=0, keepdims=True)
        dxhat = dyo * nw_ref[...]
        dyg = rs * (dxhat - xhat * jnp.mean(dxhat * xhat, axis=-1, keepdims=True))
        dz_ref[...] = (dyg * yv * (sz * (1.0 + zv * (1.0 - sz)))).astype(dz_ref.dtype)
        dy = dyg * zs

        dtr = dtr_ref[:, 0:H]
        dt, A, acum, acum_t, e, wdec, gam, dtE, eE, wE, gamE = _ssd_scalars(dtr, dtb_ref[...], al_ref[...], tri, upper, expand)
        xs = xa_ref[:, 0:W]
        X = xs * dtE
        XW = X * wE
        dskE = _dot(jnp.broadcast_to(dsk_ref[...], (8, H)), expand, "nn", split="a")[0:1, :]
        ddsk_ref[...] += heads(jnp.broadcast_to(jnp.sum(dy * xs, axis=0, keepdims=True), (8, W)))[0:1, :]

        dYe = dy * eE
        dacum = jnp.zeros((L, H), F32)
        de_full = []
        dw_full = []
        dgam_full = []
        for g in range(SSD_G):
            gs = slice(g * 512, (g + 1) * 512)
            Bg = xa_ref[:, W + g * N:W + (g + 1) * N]
            Cg = xa_ref[:, W + SSD_G * N + g * N:W + SSD_G * N + (g + 1) * N]
            Hg = hs_ref[0, :, gs]
            dHn = dh_scr[:, gs]
            CH = _dot(Cg, Hg, "nn")
            de_full.append(dy[:, gs] * CH)
            dC = _dot(dYe[:, gs], Hg, "nt")
            dHs = gamE[:, gs] * dHn + _dot(Cg, dYe[:, gs], "tn")
            dgam_full.append(jnp.sum(dHn * Hg, axis=0, keepdims=True))
            BdS = _dot(Bg, dHn, "nn")
            dB = _dot(XW[:, gs], dHn, "nt")
            dx_scr[:, gs] = BdS * wE[:, gs]
            dw_full.append(BdS * X[:, gs])
            CB = _dot(Cg, Bg, "nt")
            dCB = jnp.zeros((L, L), F32)
            for j in range(8):
                h = g * 8 + j
                hsl = slice(h * SSD_P, (h + 1) * SSD_P)
                lam = jnp.exp(jnp.where(causal, acum[:, h:h + 1] - acum_t[h:h + 1, :], -jnp.inf))
                M = CB * lam
                dM = _dot(dy[:, hsl], X[:, hsl], "nt")
                dx_scr[:, hsl] += _dot(M, dy[:, hsl], "tn")
                dCB = dCB + dM * lam
                Q = dM * M
                rowsum = jnp.sum(Q, axis=1, keepdims=True)
                colsum = _dot(Q, jnp.ones((L, 8), F32), "tn", split="a")[:, 0:1]
                dacum = dacum + (rowsum - colsum) * onehot(h)
            dC = dC + _dot(dCB, Bg, "nn")
            dB = dB + _dot(dCB, Cg, "tn")
            dxa_ref[:, W + g * N:W + (g + 1) * N] = dB
            dxa_ref[:, W + SSD_G * N + g * N:W + SSD_G * N + (g + 1) * N] = dC
            dh_scr[:, gs] = dHs

        de16 = heads(jnp.concatenate(de_full, axis=1))
        dw16 = heads(jnp.concatenate(dw_full, axis=1))
        dgam16 = heads(jnp.broadcast_to(jnp.concatenate(dgam_full, axis=1), (8, W)))[0:1, :]
        dacum = dacum + de16 * e - dw16 * wdec
        dlast = jnp.sum(dw16 * wdec, axis=0, keepdims=True) + dgam16 * gam
        lastrow = (lax.broadcasted_iota(jnp.int32, (L, 1), 0) == L - 1).astype(F32)
        dacum = dacum + lastrow * dlast
        da = _dot(tri, dacum, "tn", split="b")
        dX = dx_scr[...]
        ddt = da * A + heads(dX * xs)
        dA = jnp.sum(da * dt, axis=0, keepdims=True)
        dal_ref[...] += dA * A
        ddtr = ddt * _sigmoid(dtr + dtb_ref[...])
        ddtb_ref[...] += jnp.sum(ddtr, axis=0, keepdims=True)
        ddtr_ref[...] = ddtr
        dxa_ref[:, 0:W] = dX * dtE + dy * dskE

    p16 = pl.BlockSpec((1, H), lambda c: (0, 0))
    rev = lambda c: (nc - 1 - c, 0)
    return pl.pallas_call(
        body, name="ssd_bwd", grid=(nc,),
        in_specs=[pl.BlockSpec((L, W), rev), pl.BlockSpec((L, W), rev), pl.BlockSpec((L, CONV_CH), rev),
                  pl.BlockSpec((L, W_SMALL), rev), pl.BlockSpec((L, W), rev), pl.BlockSpec((L, 1), rev),
                  pl.BlockSpec((1, N, W), lambda c: (nc - 1 - c, 0, 0)), p16, p16, p16, pl.BlockSpec((1, W), lambda c: (0, 0))],
        out_specs=[pl.BlockSpec((L, W), rev), pl.BlockSpec((L, CONV_CH), rev), pl.BlockSpec((L, H), rev),
                   p16, p16, p16, pl.BlockSpec((1, W), lambda c: (0, 0))],
        out_shape=[jax.ShapeDtypeStruct((S, W), _MXU), jax.ShapeDtypeStruct((S, CONV_CH), F32), jax.ShapeDtypeStruct((S, H), F32),
                   jax.ShapeDtypeStruct((1, H), F32), jax.ShapeDtypeStruct((1, H), F32), jax.ShapeDtypeStruct((1, H), F32),
                   jax.ShapeDtypeStruct((1, W), F32)],
        scratch_shapes=[pltpu.VMEM((N, W), F32), pltpu.VMEM((L, W), F32)],
        compiler_params=_cp(("arbitrary",)))(dmixed, proj, xa, proj_small, y, rs2, hs, dt_bias, a_log, d_skip, norm_w)


def _rope_tables(S):
    inv = 1.0 / (ROPE_THETA ** (jnp.arange(0, ROPE_DIM, 2, dtype=F32) / ROPE_DIM))
    ang = jnp.arange(S, dtype=F32)[:, None] * inv[None, :]
    cos, sin = jnp.cos(ang), jnp.sin(ang)
    half = ROPE_DIM // 2
    c64 = jnp.concatenate([cos, cos, jnp.ones((S, HD - ROPE_DIM), F32)], axis=1)
    s64 = jnp.concatenate([sin, sin, jnp.zeros((S, HD - ROPE_DIM), F32)], axis=1)
    del half
    return jnp.concatenate([c64, c64], axis=1), jnp.concatenate([s64, s64], axis=1)


def _rope(xs, blk0, width, cos, sin, sign, out_dtype, name, extra=None):
    S = xs[0].shape[0]
    tr = _pick(S, (512, 256, 128))
    nx = len(xs)

    def body(*refs):
        x_refs, c_ref, s_ref = refs[:nx], refs[nx], refs[nx + 1]
        e_ref = refs[nx + 2] if extra is not None else None
        o_ref = refs[-1]
        cv, sv = c_ref[...], s_ref[...] * sign
        lane = lax.broadcasted_iota(jnp.int32, (tr, 128), 1)
        first = (lane & (HD - 1)) < (ROPE_DIM // 2)
        for j in range(2):
            cs = slice(j * 128, (j + 1) * 128)
            xv = x_refs[0][:, cs].astype(F32)
            for r in x_refs[1:]:
                xv = xv + r[:, cs].astype(F32)
            rot = jnp.where(first, -pltpu.roll(xv, 128 - ROPE_DIM // 2, axis=1), pltpu.roll(xv, ROPE_DIM // 2, axis=1))
            out = xv * cv + rot * sv
            if extra is not None:
                out = out + e_ref[:, cs].astype(F32)
            o_ref[:, cs] = out.astype(out_dtype)

    t128 = pl.BlockSpec((tr, 128), lambda i, j: (i, 0))
    oblk = pl.BlockSpec((tr, 256), lambda i, j: (i, j))
    specs = [pl.BlockSpec((tr, 256), lambda i, j: (i, blk0 + j))] * nx + [t128, t128]
    ins = list(xs) + [cos, sin]
    if extra is not None:
        ins.append(extra[0])
        eb = extra[1]
        specs.append(pl.BlockSpec((tr, 256), lambda i, j: (i, eb + j)))
    return pl.pallas_call(
        body, name=name, grid=(S // tr, width // 256), in_specs=specs, out_specs=oblk,
        out_shape=jax.ShapeDtypeStruct((S, width), out_dtype), compiler_params=_cp(("parallel", "parallel")))(*ins)


def _rotate128(xv, cv, sv, first):
    rot = jnp.where(first, -pltpu.roll(xv, 128 - ROPE_DIM // 2, axis=1), pltpu.roll(xv, ROPE_DIM // 2, axis=1))
    return xv * cv + rot * sv


def _kv_prep(proj, cos, sin, tk):
    S = proj.shape[0]
    NB = S // SEL_BLOCK

    def body(ks_ref, vs_ref, kw_ref, vw_ref, c_ref, s_ref, *outs):
        cv, sv = c_ref[...], s_ref[...]
        lane = lax.broadcasted_iota(jnp.int32, (tk, 128), 1)
        first = (lane & (HD - 1)) < (ROPE_DIM // 2)
        key = pl.program_id(0) * tk + lax.broadcasted_iota(jnp.int32, (tk, NB), 0)
        onehot = (lax.shift_right_logical(key, 6) == lax.broadcasted_iota(jnp.int32, (tk, NB), 1)).astype(F32)
        for j, (ref, rotated) in enumerate(((ks_ref, True), (vs_ref, False), (kw_ref, True), (vw_ref, False))):
            nat, blk = outs[2 * j], outs[2 * j + 1]
            for half in range(2):
                xv = ref[:, half * 128:(half + 1) * 128]
                if rotated:
                    xv = _rotate128(xv, cv, sv, first)
                for e in range(2):
                    h = 2 * half + e
                    piece = xv[:, e * HD:(e + 1) * HD]
                    nat[h] = (jnp.concatenate([piece, onehot], axis=1) if j == 0 else piece).astype(nat.dtype)
                    blk[h, 0] = piece.T.astype(blk.dtype)

    col = lambda b: pl.BlockSpec((tk, 256), lambda i: (i, b))
    t128 = pl.BlockSpec((tk, 128), lambda i: (i, 0))
    nat_spec = lambda w: pl.BlockSpec((N_KV, tk, w), lambda i: (0, i, 0))
    blk_spec = pl.BlockSpec((N_KV, 1, HD, tk), lambda i: (0, i, 0, 0))
    nat_shape = lambda w: jax.ShapeDtypeStruct((N_KV, S, w), _MXU)
    blk_shape = jax.ShapeDtypeStruct((N_KV, S // tk, HD, tk), _MXU)
    widths = (HD + NB, HD, HD, HD)
    res = pl.pallas_call(
        body, name="kv_prep", grid=(S // tk,), in_specs=[col(KSB), col(VSB), col(KWB), col(VWB), t128, t128],
        out_specs=[s for w in widths for s in (nat_spec(w), blk_spec)],
        out_shape=[s for w in widths for s in (nat_shape(w), blk_shape)],
        compiler_params=_cp(("parallel",)))(proj, proj, proj, proj, cos, sin)
    return dict(ks_ext=res[0], ks_t=res[1], vs=res[2], vs_t=res[3], kw=res[4], kw_t=res[5], vw=res[6], vw_t=res[7])


def _dkv_post(dks, dvs, dkw, dvw, cos, sin):
    S = dks.shape[1]
    tr = _pick(S, (512, 256, 128))

    def body(dks_ref, dvs_ref, dkw_ref, dvw_ref, c_ref, s_ref, o_ref):
        cv, sv = c_ref[...], -s_ref[...]
        lane = lax.broadcasted_iota(jnp.int32, (tr, 128), 1)
        first = (lane & (HD - 1)) < (ROPE_DIM // 2)
        for j, (ref, rotated) in enumerate(((dks_ref, True), (dvs_ref, False), (dkw_ref, True), (dvw_ref, False))):
            for half in range(2):
                xv = jnp.concatenate([ref[2 * half], ref[2 * half + 1]], axis=1)
                if rotated:
                    xv = _rotate128(xv, cv, sv, first)
                o_ref[:, j * 256 + half * 128:j * 256 + (half + 1) * 128] = xv.astype(o_ref.dtype)

    hm = pl.BlockSpec((N_KV, tr, HD), lambda i: (0, i, 0))
    t128 = pl.BlockSpec((tr, 128), lambda i: (i, 0))
    return pl.pallas_call(
        body, name="dkv_post", grid=(S // tr,), in_specs=[hm, hm, hm, hm, t128, t128],
        out_specs=pl.BlockSpec((tr, 4 * 256), lambda i: (i, 0)), out_shape=jax.ShapeDtypeStruct((S, 4 * 256), _MXU),
        compiler_params=_cp(("parallel",)))(dks, dvs, dkw, dvw, cos, sin)


def _compress_fwd(R, pe, w1, w2):
    NC = R.shape[1]
    half = 16 * HD

    def body(r_ref, pe_ref, w1_ref, w2_ref, o_ref, hid_ref):
        r = r_ref[0]
        a = _dot(r + pe_ref[:, 0:half], w1_ref[0:half, :], "nn")
        b = _dot(r + pe_ref[:, half:2 * half], w1_ref[half:2 * half, :], "nn")
        hid = a + pltpu.roll(b, NC - 1, axis=0)
        hid_ref[0] = hid
        out = _dot(hid * _sigmoid(hid), w2_ref[...], "nn")
        rows = lax.broadcasted_iota(jnp.int32, out.shape, 0)
        o_ref[0] = jnp.where(rows < NC - 1, out, 0.0).astype(o_ref.dtype)

    return pl.pallas_call(
        body, name="compress_fwd", grid=(N_KV,),
        in_specs=[pl.BlockSpec((1, NC, half), lambda h: (h, 0, 0)), pl.BlockSpec((1, 2 * half), lambda h: (0, 0)),
                  pl.BlockSpec((2 * half, CMP_HID), lambda h: (0, 0)), pl.BlockSpec((CMP_HID, HD), lambda h: (0, 0))],
        out_specs=[pl.BlockSpec((1, NC, HD), lambda h: (h, 0, 0)), pl.BlockSpec((1, NC, CMP_HID), lambda h: (h, 0, 0))],
        out_shape=[jax.ShapeDtypeStruct((N_KV, NC, HD), _MXU), jax.ShapeDtypeStruct((N_KV, NC, CMP_HID), F32)],
        compiler_params=_cp(("parallel",)))(R, pe, w1, w2)


def _compress_bwd(R, pe, w1, w2, hid, dout):
    NC = R.shape[1]
    half = 16 * HD

    def body(r_ref, pe_ref, w1_ref, w2_ref, hid_ref, do_ref, dr_ref, dw1_ref, dw2_ref, dpe_ref):
        @pl.when(pl.program_id(0) == 0)
        def _():
            dw1_ref[...] = jnp.zeros_like(dw1_ref)
            dw2_ref[...] = jnp.zeros_like(dw2_ref)
            dpe_ref[...] = jnp.zeros_like(dpe_ref)

        r, hv, do = r_ref[0], hid_ref[0], do_ref[0]
        s = _sigmoid(hv)
        dw2_ref[...] += _dot(hv * s, do, "tn")
        dhid = _dot(do, w2_ref[...], "nt") * (s * (1.0 + hv * (1.0 - s)))
        rows = lax.broadcasted_iota(jnp.int32, dhid.shape, 0)
        dhid = jnp.where(rows < NC - 1, dhid, 0.0)
        dhid_dn = pltpu.roll(dhid, 1, axis=0)
        dw1_ref[0:half, :] += _dot(r + pe_ref[:, 0:half], dhid, "tn")
        dw1_ref[half:2 * half, :] += _dot(r + pe_ref[:, half:2 * half], dhid_dn, "tn")
        dxt = _dot(dhid, w1_ref[0:half, :], "nt")
        dxb = _dot(dhid_dn, w1_ref[half:2 * half, :], "nt")
        dr_ref[0] = dxt + dxb
        dpe_ref[:, 0:half] += jnp.sum(dxt, axis=0, keepdims=True)
        dpe_ref[:, half:2 * half] += jnp.sum(dxb, axis=0, keepdims=True)

    return pl.pallas_call(
        body, name="compress_bwd", grid=(N_KV,),
        in_specs=[pl.BlockSpec((1, NC, half), lambda h: (h, 0, 0)), pl.BlockSpec((1, 2 * half), lambda h: (0, 0)),
                  pl.BlockSpec((2 * half, CMP_HID), lambda h: (0, 0)), pl.BlockSpec((CMP_HID, HD), lambda h: (0, 0)),
                  pl.BlockSpec((1, NC, CMP_HID), lambda h: (h, 0, 0)), pl.BlockSpec((1, NC, HD), lambda h: (h, 0, 0))],
        out_specs=[pl.BlockSpec((1, NC, half), lambda h: (h, 0, 0)), pl.BlockSpec((2 * half, CMP_HID), lambda h: (0, 0)),
                   pl.BlockSpec((CMP_HID, HD), lambda h: (0, 0)), pl.BlockSpec((1, 2 * half), lambda h: (0, 0))],
        out_shape=[jax.ShapeDtypeStruct((N_KV, NC, half), F32), jax.ShapeDtypeStruct((2 * half, CMP_HID), F32),
                   jax.ShapeDtypeStruct((CMP_HID, HD), F32), jax.ShapeDtypeStruct((1, 2 * half), F32)],
        compiler_params=_cp(("arbitrary",)))(R, pe, w1, w2, hid, dout)


def _attn_cfg(S, Sk, mode):
    tq = _pick(S, (256, 128))
    tk = Sk if mode == "cmp" else _pick(Sk, (256, 128))
    return tq, tk


def _block_start(kb, tk):
    return kb * tk if isinstance(kb, int) else pl.multiple_of(kb * tk, tk)


def _pipelined_key_blocks(mode, q0, tq, tk, produce, consume):
    if mode == "cmp":
        produce(0, True, 0)
        consume(0, 0)
        return
    if mode == "win":
        assert tq == tk and WINDOW == 2 * tk
        last = q0 // tk
        first = jnp.maximum(last - 2, 0)

        @pl.when(last == 0)
        def _():
            produce(last, True, 0)
            consume(last, 0)

        @pl.when(last == 1)
        def _():
            produce(first, True, 0)
            produce(last, True, 1)
            consume(first, 0)
            consume(last, 1)

        @pl.when(last >= 2)
        def _():
            produce(first, True, 0)
            produce(first + 1, False, 1)
            consume(first, 0)
            produce(last, True, 0)
            consume(first + 1, 1)
            consume(last, 0)

        return
    first, n_plain, plain_masked = 0, q0 // tk, False
    last = first + n_plain
    pairs = jnp.maximum(n_plain - 1, 0) // 2

    @pl.when(n_plain >= 1)
    def _():
        produce(first, plain_masked, 0)

    def two(j, carry):
        kb = first + 2 * j
        produce(kb + 1, plain_masked, 1)
        consume(kb, 0)
        produce(kb + 2, plain_masked, 0)
        consume(kb + 1, 1)
        return carry

    lax.fori_loop(0, pairs, two, 0)
    kb = first + 2 * pairs
    left = n_plain - 2 * pairs

    @pl.when(left == 2)
    def _():
        produce(kb + 1, plain_masked, 1)
        consume(kb, 0)
        produce(last, True, 0)
        consume(kb + 1, 1)
        consume(last, 0)

    @pl.when(left == 1)
    def _():
        produce(last, True, 1)
        consume(kb, 0)
        consume(last, 1)

    @pl.when(left == 0)
    def _():
        produce(last, True, 0)
        consume(last, 0)


def _attn_bias(mode, q0, k0, tq, tk):
    k = k0 + lax.broadcasted_iota(jnp.int32, (tk, tq), 0)
    t = q0 + lax.broadcasted_iota(jnp.int32, (tk, tq), 1)
    if mode == "cmp":
        ok = (k * 16 + 31) <= t
    elif mode == "win":
        ok = (k <= t) & ((t - k) < WINDOW)
    else:
        ok = k <= t
    bias = jnp.where(ok, 0.0, NEG)
    return jnp.concatenate([bias] * GRP, axis=1), jnp.concatenate([ok.astype(F32)] * GRP, axis=1)


def _sel_operands(qs, selneg_ref):
    return jnp.concatenate([qs, jnp.concatenate([selneg_ref[0]] * GRP, axis=0)], axis=1)


def _stack_heads(ref, tq):
    return jnp.concatenate([ref[:, g * HD:(g + 1) * HD] for g in range(GRP)], axis=0)


def _scaled_queries(q_ref, tq):
    return (_stack_heads(q_ref, tq).astype(F32) * SCALE).astype(_MXU)


def _blocked_t(x, tk):
    n, Sk, d = x.shape
    return x.reshape(n, Sk // tk, tk, d).transpose(0, 1, 3, 2)


def _head_rows(ref):
    return jnp.concatenate([ref[0, g:g + 1, :] for g in range(GRP)], axis=1)


def _attn_fwd(q, qcol0, k, vt, mode, selneg, gate, y_prev, y_dtype, name):
    S, Sk = q.shape[0], k.shape[1]
    tq, tk = _attn_cfg(S, Sk, mode)
    R = GRP * tq

    def body(*refs):
        q_ref, k_ref, vt_ref = refs[:3]
        rest = list(refs[3:])
        sel_ref = rest.pop(0) if mode == "sel" else None
        gate_ref = rest.pop(0)
        yp_ref = rest.pop(0) if y_prev is not None else None
        o_ref, lse_ref, y_ref, m_scr, l_scr, acc, s_scr = rest
        q0 = pl.program_id(1) * tq
        qs = _scaled_queries(q_ref, tq)
        m_scr[...] = jnp.full_like(m_scr, NEG)
        l_scr[...] = jnp.zeros_like(l_scr)
        acc[...] = jnp.zeros_like(acc)
        qk = _sel_operands(qs, sel_ref) if mode == "sel" else qs

        def produce(kb, masked, slot):
            k0 = _block_start(kb, tk)
            s = _dot(k_ref[0, pl.ds(k0, tk), :], qk, "nt")
            if masked:
                s = s + _attn_bias(mode, q0, k0, tq, tk)[0]
            s_scr[slot] = s

        def consume(kb, slot):
            s = s_scr[slot]
            m_old = m_scr[...]
            m_new = jnp.maximum(m_old, jnp.max(s, axis=0, keepdims=True))
            p = jnp.exp(s - m_new)
            if mode == "cmp":
                p = p * _attn_bias(mode, q0, 0, tq, tk)[1]
            alpha = jnp.exp(m_old - m_new)
            l_scr[...] = alpha * l_scr[...] + jnp.sum(p, axis=0, keepdims=True)
            acc[...] = alpha * acc[...] + _dot(vt_ref[0, kb], p, "nn")
            m_scr[...] = m_new

        _pipelined_key_blocks(mode, q0, tq, tk, produce, consume)
        l = l_scr[...]
        good = l > 0.0
        o_t = acc[...] * jnp.where(good, 1.0 / jnp.where(good, l, 1.0), 0.0)
        lse = jnp.where(good, m_scr[...] + jnp.log(jnp.where(good, l, 1.0)), -NEG)
        y_t = o_t * _sigmoid(_head_rows(gate_ref))
        for g in range(GRP):
            hs, qs_ = slice(g * HD, (g + 1) * HD), slice(g * tq, (g + 1) * tq)
            o_ref[:, hs] = o_t[:, qs_].T
            lse_ref[0, g:g + 1, :] = lse[:, qs_]
            yg = y_t[:, qs_].T
            if y_prev is not None:
                yg = yg + yp_ref[:, hs]
            y_ref[:, hs] = yg.astype(y_ref.dtype)

    row_spec = pl.BlockSpec((1, GRP, tq), lambda h, i: (h, 0, i))
    qo_spec = pl.BlockSpec((tq, GRP * HD), lambda h, i: (i, h))
    ins = [q, k, vt]
    specs = [pl.BlockSpec((tq, GRP * HD), lambda h, i: (i, qcol0 + h)), pl.BlockSpec((1, Sk, k.shape[2]), lambda h, i: (h, 0, 0)),
             pl.BlockSpec((1, Sk // tk, HD, tk), lambda h, i: (h, 0, 0, 0))]
    if mode == "sel":
        assert tq == tk
        ins.append(selneg)
        specs.append(pl.BlockSpec((1, tq, selneg.shape[2]), lambda h, i: (h, i, 0)))
    ins.append(gate)
    specs.append(row_spec)
    if y_prev is not None:
        ins.append(y_prev)
        specs.append(qo_spec)
    return pl.pallas_call(
        body, name=name, grid=(N_KV, S // tq), in_specs=specs, out_specs=[qo_spec, row_spec, qo_spec],
        out_shape=[jax.ShapeDtypeStruct((S, ATT_WIDTH), F32), jax.ShapeDtypeStruct((N_KV, GRP, S), F32),
                   jax.ShapeDtypeStruct((S, ATT_WIDTH), y_dtype)],
        scratch_shapes=[pltpu.VMEM((1, R), F32), pltpu.VMEM((1, R), F32), pltpu.VMEM((HD, R), F32), pltpu.VMEM((2, tk, R), F32)],
        compiler_params=_cp(("parallel", "arbitrary")))(*ins)


def _attn_bwd(q, qcol0, k, kt, v, o, lse, dy, dycol0, gate, mode, selneg, name):
    S, Sk = q.shape[0], k.shape[1]
    tq, tk = _attn_cfg(S, Sk, mode)
    R = GRP * tq

    def body(*refs):
        if mode == "sel":
            (q_ref, k_ref, kt_ref, v_ref, o_ref, lse_ref, dy_ref, gate_ref, sel_ref, dq_ref, dk_ref, dv_ref, dg_ref, dq_scr, s_scr,
             dp_scr) = refs
        else:
            q_ref, k_ref, kt_ref, v_ref, o_ref, lse_ref, dy_ref, gate_ref, dq_ref, dk_ref, dv_ref, dg_ref, dq_scr, s_scr, dp_scr = refs

        @pl.when(pl.program_id(1) == 0)
        def _():
            dk_ref[...] = jnp.zeros_like(dk_ref)
            dv_ref[...] = jnp.zeros_like(dv_ref)

        q0 = pl.program_id(1) * tq
        qs = _scaled_queries(q_ref, tq)
        dys = _stack_heads(dy_ref, tq)
        gv = _sigmoid(_head_rows(gate_ref))
        dy_o = _dot(jnp.ones((8, HD), F32), dys * _stack_heads(o_ref, tq), "nt", split="b")[0:1, :]
        delta = gv * dy_o
        dgate = dy_o * (gv * (1.0 - gv))
        for g in range(GRP):
            dg_ref[0, g:g + 1, :] = dgate[:, g * tq:(g + 1) * tq]
        lsev = _head_rows(lse_ref)
        dos = (dys * jnp.broadcast_to(gv, (8, R)).T[:, 0:1]).astype(_MXU)
        dq_scr[...] = jnp.zeros_like(dq_scr)
        qk = _sel_operands(qs, sel_ref) if mode == "sel" else qs

        def produce(kb, masked, slot):
            k0 = _block_start(kb, tk)
            s = _dot(k_ref[0, pl.ds(k0, tk), :], qk, "nt")
            if masked:
                s = s + _attn_bias(mode, q0, k0, tq, tk)[0]
            s_scr[slot] = s
            dp_scr[slot] = _dot(v_ref[0, pl.ds(k0, tk), :], dos, "nt")

        def consume(kb, slot):
            k0 = _block_start(kb, tk)
            p = jnp.exp(s_scr[slot] - lsev)
            if mode == "cmp":
                p = p * _attn_bias(mode, q0, 0, tq, tk)[1]
            ds = p * (dp_scr[slot] - delta)
            dq_scr[...] += _dot(kt_ref[0, kb], ds, "nn")
            dk_ref[0, pl.ds(k0, tk), :] += _dot(ds, qs, "nn")
            dv_ref[0, pl.ds(k0, tk), :] += _dot(p, dos, "nn")

        _pipelined_key_blocks(mode, q0, tq, tk, produce, consume)
        for g in range(GRP):
            dq_ref[:, g * HD:(g + 1) * HD] = (dq_scr[:, g * tq:(g + 1) * tq] * SCALE).T

    kv_spec = pl.BlockSpec((1, Sk, HD), lambda h, i: (h, 0, 0))
    qo_spec = pl.BlockSpec((tq, GRP * HD), lambda h, i: (i, h))
    row_spec = pl.BlockSpec((1, GRP, tq), lambda h, i: (h, 0, i))
    ins = [q, k, kt, v, o, lse, dy, gate]
    specs = [pl.BlockSpec((tq, GRP * HD), lambda h, i: (i, qcol0 + h)), pl.BlockSpec((1, Sk, k.shape[2]), lambda h, i: (h, 0, 0)),
             pl.BlockSpec((1, Sk // tk, HD, tk), lambda h, i: (h, 0, 0, 0)), kv_spec, qo_spec, row_spec,
             pl.BlockSpec((tq, GRP * HD), lambda h, i: (i, dycol0 + h)), row_spec]
    if mode == "sel":
        assert tq == tk
        ins.append(selneg)
        specs.append(pl.BlockSpec((1, tq, selneg.shape[2]), lambda h, i: (h, i, 0)))
    return pl.pallas_call(
        body, name=name, grid=(N_KV, S // tq), in_specs=specs, out_specs=[qo_spec, kv_spec, kv_spec, row_spec],
        out_shape=[jax.ShapeDtypeStruct((S, ATT_WIDTH), F32), jax.ShapeDtypeStruct((N_KV, Sk, HD), F32),
                   jax.ShapeDtypeStruct((N_KV, Sk, HD), F32), jax.ShapeDtypeStruct((N_KV, GRP, S), F32)],
        scratch_shapes=[pltpu.VMEM((HD, R), F32), pltpu.VMEM((2, tk, R), F32), pltpu.VMEM((2, tk, R), F32)],
        compiler_params=_cp(("parallel", "arbitrary")))(*ins)


def _select(q, qcol0, k_cmp, lse):
    S, NC = q.shape[0], k_cmp.shape[1]
    NB = S // SEL_BLOCK
    tq = _pick(S, (256, 128))
    ci = np.arange(NC)[None, :] * 16
    sj = np.arange(NB)[:, None] * SEL_BLOCK
    ov_t = np.clip(np.minimum(ci + 32, sj + SEL_BLOCK) - np.maximum(ci, sj), 0, None) / 32.0
    ov_t[:, NC - 1] = 0.0
    ov_t = jnp.asarray(ov_t, F32)

    def body(q_ref, k_ref, lse_ref, ov_ref, sel_ref):
        q0 = pl.program_id(1) * tq
        bias, okf = _attn_bias("cmp", q0, 0, tq, NC)
        lsev = _head_rows(lse_ref)
        p = jnp.exp(_dot(k_ref[0], _scaled_queries(q_ref, tq), "nt") + bias - lsev) * okf
        imp4 = _dot(ov_ref[...], p, "nn")
        imp = imp4[:, 0:tq] + imp4[:, tq:2 * tq] + imp4[:, 2 * tq:3 * tq] + imp4[:, 3 * tq:4 * tq]
        blk = lax.broadcasted_iota(jnp.int32, (NB, tq), 0)
        cur = lax.shift_right_logical(q0 + lax.broadcasted_iota(jnp.int32, (NB, tq), 1), 6)
        imp = jnp.where((blk == 0) | (blk == cur) | (blk == cur - 1), FORCE, imp)
        imp = jnp.where(blk <= cur, imp, -1.0)
        rank = jnp.zeros((NB, tq), F32)
        for j in range(NB):
            row = imp[j:j + 1, :]
            ahead = (row > imp) | ((row == imp) & (blk > j))
            rank = rank + ahead.astype(F32)
        chosen = (rank < float(N_SELECT)) & (imp >= 0.0)
        sel_ref[0] = jnp.where(chosen, 0.0, NEG).T.astype(sel_ref.dtype)

    return pl.pallas_call(
        body, name="select_blocks", grid=(N_KV, S // tq),
        in_specs=[pl.BlockSpec((tq, GRP * HD), lambda h, i: (i, qcol0 + h)), pl.BlockSpec((1, NC, HD), lambda h, i: (h, 0, 0)),
                  pl.BlockSpec((1, GRP, tq), lambda h, i: (h, 0, i)), pl.BlockSpec((NB, NC), lambda h, i: (0, 0))],
        out_specs=pl.BlockSpec((1, tq, NB), lambda h, i: (h, i, 0)),
        out_shape=jax.ShapeDtypeStruct((N_KV, S, NB), _MXU), compiler_params=_cp(("parallel", "parallel")))(q, k_cmp, lse, ov_t)


def _to_rows16(x):
    S = x.shape[0]
    return x.reshape(S // 16, 16, N_KV, HD).transpose(2, 0, 1, 3).reshape(N_KV, S // 16, 16 * HD)


def _from_rows16(r):
    NC = r.shape[1]
    return r.reshape(N_KV, NC, 16, HD).transpose(1, 2, 0, 3).reshape(NC * 16, N_KV * HD)


DT_COL0 = SSD_WIDTH + CONV_CH
GATE_IN_COL0 = D_IN - 3 * N_HEADS


SHARD_IN = D_IN // N_DEV


def _orig_cols(ref, c0, width):
    pieces, c = [], c0
    while c < c0 + width:
        d, off = divmod(c, SHARD_IN)
        w = min(SHARD_IN - off, c0 + width - c)
        pieces.append(ref[d, :, off:off + w])
        c += w
    return pieces[0] if len(pieces) == 1 else jnp.concatenate(pieces, axis=1)


def _cols_from_slabs(slabs):
    _, R, c = slabs.shape
    tr = _pick(R, (256, 128))

    def body(s_ref, o_ref):
        for t in range(N_DEV * c // LANE):
            pieces, col = [], t * LANE
            while col < (t + 1) * LANE:
                d, off = divmod(col, c)
                w = min(c - off, (t + 1) * LANE - col)
                pieces.append(s_ref[d, :, off:off + w])
                col += w
            o_ref[:, t * LANE:(t + 1) * LANE] = pieces[0] if len(pieces) == 1 else jnp.concatenate(pieces, axis=1)

    return pl.pallas_call(
        body, name="cols_from_slabs", grid=(R // tr,), in_specs=[pl.BlockSpec((N_DEV, tr, c), lambda i: (0, i, 0))],
        out_specs=pl.BlockSpec((tr, N_DEV * c), lambda i: (i, 0)), out_shape=jax.ShapeDtypeStruct((R, N_DEV * c), slabs.dtype),
        compiler_params=_cp(("parallel",)))(slabs)


def _slabs_from_cols(x):
    R, c = x.shape[0], x.shape[1] // N_DEV
    tr = _pick(R, (256, 128))

    def body(x_ref, o_ref):
        for d in range(N_DEV):
            o_ref[d] = x_ref[:, d * c:(d + 1) * c]

    return pl.pallas_call(
        body, name="slabs_from_cols", grid=(R // tr,), in_specs=[pl.BlockSpec((tr, N_DEV * c), lambda i: (i, 0))],
        out_specs=pl.BlockSpec((N_DEV, tr, c), lambda i: (0, i, 0)), out_shape=jax.ShapeDtypeStruct((N_DEV, R, c), x.dtype),
        compiler_params=_cp(("parallel",)))(x)


def _w_in_from_slabs(slabs):
    D = slabs.shape[1]
    tr = _pick(D, (256, 128))

    def body(s_ref, main_ref, small_ref):
        for t in range(W_MAIN // LANE):
            c = t * LANE
            main_ref[:, c:c + LANE] = _orig_cols(s_ref, c if c < DT_COL0 else c + SSD_HEADS, LANE)
        small_ref[...] = jnp.concatenate(
            [_orig_cols(s_ref, DT_COL0, SSD_HEADS), _orig_cols(s_ref, GATE_IN_COL0, 3 * N_HEADS),
             jnp.zeros((tr, W_SMALL - SSD_HEADS - 3 * N_HEADS), small_ref.dtype)], axis=1)

    return pl.pallas_call(
        body, name="w_in_layout", grid=(D // tr,), in_specs=[pl.BlockSpec((N_DEV, tr, SHARD_IN), lambda i: (0, i, 0))],
        out_specs=[pl.BlockSpec((tr, W_MAIN), lambda i: (i, 0)), pl.BlockSpec((tr, W_SMALL), lambda i: (i, 0))],
        out_shape=[jax.ShapeDtypeStruct((D, W_MAIN), slabs.dtype), jax.ShapeDtypeStruct((D, W_SMALL), slabs.dtype)],
        compiler_params=_cp(("parallel",)))(slabs)


def _w_in_to_slabs(main, small):
    D = main.shape[0]
    tr = _pick(D, (256, 128))
    ranges = [(0, DT_COL0, 0, 0), (DT_COL0, DT_COL0 + SSD_HEADS, 1, 0), (DT_COL0 + SSD_HEADS, GATE_IN_COL0, 0, DT_COL0),
              (GATE_IN_COL0, D_IN, 1, SSD_HEADS)]

    def body(main_ref, small_ref, o_ref):
        srcs = (main_ref, small_ref)
        for d in range(N_DEV):
            lo, hi = d * SHARD_IN, (d + 1) * SHARD_IN
            pieces = []
            for start, stop, which, s0 in ranges:
                a, b = max(lo, start), min(hi, stop)
                if a < b:
                    pieces.append(srcs[which][:, s0 + a - start:s0 + b - start].astype(o_ref.dtype))
            o_ref[d] = pieces[0] if len(pieces) == 1 else jnp.concatenate(pieces, axis=1)

    return pl.pallas_call(
        body, name="w_in_grad_layout", grid=(D // tr,),
        in_specs=[pl.BlockSpec((tr, W_MAIN), lambda i: (i, 0)), pl.BlockSpec((tr, W_SMALL), lambda i: (i, 0))],
        out_specs=pl.BlockSpec((N_DEV, tr, SHARD_IN), lambda i: (0, i, 0)),
        out_shape=jax.ShapeDtypeStruct((N_DEV, D, SHARD_IN), main.dtype), compiler_params=_cp(("parallel",)))(main, small)


QB, KCB, VCB, KSB, VSB, KWB, VWB = 10, 14, 15, 16, 17, 18, 19


def _col256(a, b):
    return a[:, b * 256:(b + 1) * 256]


_EARLY = ["w_in", "cmp_w1_k", "cmp_w1_v"]
_LATE = ["w_out", "w_gate", "w_up", "w_down"]
_FFN = ["w_down", "w_gate", "w_up"]
_MID = ["w_out"]
_LAST = ["cmp_w1_k", "cmp_w1_v", "w_in"]


def _local_step(x, tgt, p, late_weights=None, grads_ready=None):
    S = x.shape[0]
    cos, sin = _rope_tables(S)

    u, rs1 = _rms_fwd(x, p["attn_norm_w"], "attn_norm")
    proj = _mm(u, p["w_main"], "nn", F32, "in_proj", after=p.get("before_in_proj"))
    proj_small = _mm(u, p["w_small"], "nn", F32, "in_proj_small")
    xa = _conv_fwd(proj, p["conv_w"], p["conv_b"])
    y_ssd, y_pre, rs_ssd, hs = _ssd_fwd(proj, proj_small, xa, p["dt_bias"], p["a_log"], p["d_skip"], p["ssd_norm_w"])

    q_rot = _rope([proj], QB, ATT_WIDTH, cos, sin, 1.0, _MXU, "rope_q")
    kv = _kv_prep(proj, cos, sin, _attn_cfg(S, S, "sel")[1])
    rk, rv = _to_rows16(_col256(proj, KCB)), _to_rows16(_col256(proj, VCB))
    k_cmp, hid_k = _compress_fwd(rk, p["cmp_pe_k"], p["cmp_w1_k"], p["cmp_w2_k"])
    v_cmp, hid_v = _compress_fwd(rv, p["cmp_pe_v"], p["cmp_w1_v"], p["cmp_w2_v"])
    n_cmp = k_cmp.shape[1]

    gates = proj_small[:, SSD_HEADS:SSD_HEADS + 3 * N_HEADS].reshape(S, N_KV, GRP, 3).transpose(3, 1, 2, 0)
    o_cmp, lse_cmp, y_att = _attn_fwd(proj, QB, k_cmp, _blocked_t(v_cmp, n_cmp), "cmp", None, gates[0], None, F32, "attn_cmp_fwd")
    sel = _select(proj, QB, k_cmp, lse_cmp)
    o_sel, lse_sel, y_att = _attn_fwd(q_rot, 0, kv["ks_ext"], kv["vs_t"], "sel", sel, gates[1], y_att, F32, "attn_sel_fwd")
    o_win, lse_win, y_att = _attn_fwd(q_rot, 0, kv["kw"], kv["vw_t"], "win", None, gates[2], y_att, _MXU, "attn_win_fwd")

    if late_weights is not None:
        p = {**p, **late_weights(y_att)}
    mixed = jnp.concatenate([y_ssd, y_att], axis=1)
    h1 = _mm(mixed, p["w_out"], "nn", F32, "out_proj", res=x)
    v, rs_ffn = _rms_fwd(h1, p["ffn_norm_w"], "ffn_norm")
    gt, up, act = _ffn_up(v, p["w_gate"], p["w_up"])
    D = x.shape[1]
    dh2, dh2b, loss, d_final_w = _mm_rows([(act, p["w_down"])], "nn", "ffn_down_loss", [h1, tgt], [], [p["final_norm_w"]],
                                          _final_epilogue, [F32, _MXU], [(1, LANE), (1, D)])

    def ready(names):
        return None if grads_ready is None else grads_ready(names, g)

    g = {"final_norm_w": d_final_w}
    g["w_down"] = _mm(act, dh2b, "tn", _MXU, "dw_down")
    dgt, dup = _ffn_dact(dh2b, p["w_down"], gt, up)
    g["w_gate"] = _mm(v, dgt, "tn", _MXU, "dw_gate")
    g["w_up"] = _mm(v, dup, "tn", _MXU, "dw_up")
    dh1, dh1b, g["ffn_norm_w"] = _mm_rows([(dgt, p["w_gate"]), (dup, p["w_up"])], "nt", "ffn_dv_norm_bwd", [h1, dh2], [rs_ffn],
                                          [p["ffn_norm_w"]], _norm_bwd_epilogue, [F32, _MXU], [(1, D)], after=ready(_FFN))
    g["w_out"] = _mm(mixed, dh1b, "tn", _MXU, "dw_out")
    dmixed = _mm(dh1b, p["w_out"], "nt", F32, "dmixed", after=ready(_MID))

    dz, dxa, ddtr, g["dt_bias"], g["a_log"], g["d_skip"], g["ssd_norm_w"] = _ssd_bwd(
        dmixed, proj, proj_small, xa, y_pre, rs_ssd, hs, p["dt_bias"], p["a_log"], p["d_skip"], p["ssd_norm_w"])
    dxbc, g["conv_w"], g["conv_b"] = _conv_bwd(proj, p["conv_w"], p["conv_b"], dxa)

    dyb = SSD_WIDTH // (GRP * HD)
    dq_cmp, dk_cmp, dv_cmp, dg_cmp = _attn_bwd(proj, QB, k_cmp, _blocked_t(k_cmp, n_cmp), v_cmp, o_cmp, lse_cmp, dmixed, dyb,
                                               gates[0], "cmp", None, "attn_cmp_bwd")
    dq_sel, dks, dvs, dg_sel = _attn_bwd(q_rot, 0, kv["ks_ext"], kv["ks_t"], kv["vs"], o_sel, lse_sel, dmixed, dyb, gates[1], "sel",
                                         sel, "attn_sel_bwd")
    dq_win, dkw, dvw, dg_win = _attn_bwd(q_rot, 0, kv["kw"], kv["kw_t"], kv["vw"], o_win, lse_win, dmixed, dyb, gates[2], "win", None,
                                         "attn_win_bwd")
    dgate = jnp.stack([dg_cmp, dg_sel, dg_win]).transpose(3, 1, 2, 0).reshape(S, 3 * N_HEADS)
    drk, g["cmp_w1_k"], g["cmp_w2_k"], g["cmp_pe_k"] = _compress_bwd(rk, p["cmp_pe_k"], p["cmp_w1_k"], p["cmp_w2_k"], hid_k, dk_cmp)
    drv, g["cmp_w1_v"], g["cmp_w2_v"], g["cmp_pe_v"] = _compress_bwd(rv, p["cmp_pe_v"], p["cmp_w1_v"], p["cmp_w2_v"], hid_v, dv_cmp)
    dq = _rope([dq_sel, dq_win], 0, ATT_WIDTH, cos, sin, -1.0, _MXU, "rope_dq", extra=(dq_cmp, 0))
    dkv = _dkv_post(dks, dvs, dkw, dvw, cos, sin)
    dproj = jnp.concatenate([dz, dxbc, dq] + [t.astype(_MXU) for t in (_from_rows16(drk), _from_rows16(drv))] + [dkv], axis=1)
    dsmall = jnp.concatenate([ddtr, dgate, jnp.zeros((S, W_SMALL - SSD_HEADS - 3 * N_HEADS), F32)], axis=1).astype(_MXU)
    g["w_main"] = _mm(u, dproj, "tn", _MXU, "dw_in")
    g["w_small"] = _mm(u, dsmall, "tn", F32, "dw_in_small")
    du_small = _mm(dsmall, p["w_small"], "nt", F32, "du_small", after=ready(_LAST))
    grad_x, g["attn_norm_w"] = _mm_rows([(dproj, p["w_main"])], "nt", "du_norm_bwd", [x, dh1, du_small], [rs1], [p["attn_norm_w"]],
                                        _norm_bwd_epilogue, [F32], [(1, D)])
    return loss, grad_x, g


MESH_ID = pl.DeviceIdType.MESH


def _my_coords():
    return lax.axis_index("x"), lax.axis_index("y"), lax.axis_index("c")


def _flat_id(px, py, pc):
    return 4 * px + 2 * py + pc


def _peer(k):
    mx, my, mc = _my_coords()
    return (1 - mx if k & 4 else mx, 1 - my if k & 2 else my, 1 - mc if k & 1 else mc)


def _exchange(arrs, scatter, name, after=()):
    n, na = len(arrs), len(after)
    scatter = [scatter] * n if isinstance(scatter, bool) else list(scatter)

    def body(*refs):
        ins, outs = refs[:n], refs[n + na:2 * n + na]
        send_sems, recv_sems, local_sems = refs[2 * n + na:]
        me = _flat_id(*_my_coords())
        copies = []
        for i in range(n):
            src_me = ins[i].at[me] if scatter[i] else ins[i]
            local = pltpu.make_async_copy(src_me, outs[i].at[me], local_sems.at[i])
            local.start()
            copies.append(local)
        for k in range(1, N_DEV):
            peer = _peer(k)
            for i in range(n):
                src = ins[i].at[_flat_id(*peer)] if scatter[i] else ins[i]
                cp = pltpu.make_async_remote_copy(src_ref=src, dst_ref=outs[i].at[me], send_sem=send_sems.at[i * 7 + k - 1],
                                                  recv_sem=recv_sems.at[i * 7 + k - 1], device_id=peer, device_id_type=MESH_ID)
                cp.start()
                copies.append(cp)
        for cp in copies:
            cp.wait()

    any_spec = pl.BlockSpec(memory_space=pl.ANY)
    out_shape = [jax.ShapeDtypeStruct(a.shape if sc else (N_DEV,) + a.shape, a.dtype) for a, sc in zip(arrs, scatter)]
    return pl.pallas_call(
        body, name=name, in_specs=[any_spec] * (n + na), out_specs=[any_spec] * n, out_shape=out_shape,
        scratch_shapes=[pltpu.SemaphoreType.DMA((n * 7,)), pltpu.SemaphoreType.DMA((n * 7,)), pltpu.SemaphoreType.DMA((n,))],
        compiler_params=pltpu.CompilerParams(has_side_effects=True))(*arrs, *after)


def _gather_two_level(arrs, name):
    n = len(arrs)

    def body(*refs):
        ins, outs = refs[:n], refs[n:2 * n]
        send_sems, recv_sems, local_sems = refs[2 * n:]
        x, y, c = _my_coords()
        me, sibling = (x, y, c), (x, y, 1 - c)
        chips = [(1 - x, y), (x, 1 - y), (1 - x, 1 - y)]

        def copy(i, k, block, to, src=None):
            slot = outs[i].at[_flat_id(*block)]
            return pltpu.make_async_remote_copy(src_ref=slot if src is None else src, dst_ref=slot, send_sem=send_sems.at[i * 7 + k],
                                                recv_sem=recv_sems.at[i * 7 + k], device_id=to, device_id_type=MESH_ID)

        mine = [pltpu.make_async_copy(ins[i], outs[i].at[_flat_id(*me)], local_sems.at[i]) for i in range(n)]
        for cp in mine:
            cp.start()
        first = []
        for j, chip in enumerate(chips):
            first += [copy(i, 1 + j, me, (*chip, c), src=ins[i]) for i in range(n)]
        first += [copy(i, 0, me, sibling, src=ins[i]) for i in range(n)]
        for cp in first:
            cp.start()
        passed = []
        for j, chip in enumerate(chips):
            for i in range(n):
                copy(i, 1 + j, (*chip, c), me).wait_recv()
                passed.append(copy(i, 4 + j, (*chip, c), sibling))
                passed[-1].start()
        for i in range(n):
            copy(i, 0, sibling, me).wait_recv()
        for j, chip in enumerate(chips):
            for i in range(n):
                copy(i, 4 + j, (*chip, 1 - c), me).wait_recv()
        for cp in first + passed:
            cp.wait_send()
        for cp in mine:
            cp.wait()

    any_spec = pl.BlockSpec(memory_space=pl.ANY)
    return pl.pallas_call(
        body, name=name, in_specs=[any_spec] * n, out_specs=[any_spec] * n,
        out_shape=[jax.ShapeDtypeStruct((N_DEV,) + a.shape, a.dtype) for a in arrs],
        scratch_shapes=[pltpu.SemaphoreType.DMA((n * 7,)), pltpu.SemaphoreType.DMA((n * 7,)), pltpu.SemaphoreType.DMA((n,))],
        compiler_params=pltpu.CompilerParams(has_side_effects=True))(*arrs)


_HBM = pl.BlockSpec(memory_space=pltpu.HBM)
_SEM = pl.BlockSpec(memory_space=pltpu.SEMAPHORE)
_EFFECT = pltpu.SideEffectType.DATAFLOW_SIDE_EFFECTING


def _split_copies(ins, lands, send_sems, recv_sems, scatter):
    me = _flat_id(*_my_coords())
    out = []
    for k in range(1, N_DEV):
        peer = _peer(k)
        for i in range(len(ins)):
            src = ins[i].at[_flat_id(*peer)] if scatter else ins[i]
            out.append(pltpu.make_async_remote_copy(src_ref=src, dst_ref=lands[i].at[me], send_sem=send_sems.at[i * 7 + k - 1],
                                                    recv_sem=recv_sems.at[i * 7 + k - 1], device_id=peer, device_id_type=MESH_ID))
    return out


def _split_start(arrs, scatter, name, after=()):
    n, na = len(arrs), len(after)

    def body(*refs):
        for cp in _split_copies(refs[:n], refs[n:2 * n], refs[2 * n + na], refs[2 * n + na + 1], scatter):
            cp.start()
        refs[-1][...] = jnp.zeros_like(refs[-1])

    land_shapes = [a.shape if scatter else (N_DEV,) + a.shape for a in arrs]
    out_shape = ((pltpu.SemaphoreType.DMA((n * 7,)), pltpu.SemaphoreType.DMA((n * 7,)))
                 + tuple(pltpu.HBM(a.shape, a.dtype) for a in arrs) + tuple(pltpu.HBM(s, a.dtype) for s, a in zip(land_shapes, arrs))
                 + (jax.ShapeDtypeStruct((8, 128), F32),))
    operands = ([pltpu.with_memory_space_constraint(a, pltpu.HBM) for a in arrs]
                + [pltpu.with_memory_space_constraint(lax.empty(s, a.dtype), pltpu.HBM) for s, a in zip(land_shapes, arrs)])
    res = pl.pallas_call(
        body, name=name, out_shape=out_shape, in_specs=[_HBM] * (2 * n) + [pl.BlockSpec(memory_space=pl.ANY)] * na,
        out_specs=(_SEM, _SEM) + (_HBM,) * (2 * n) + (pl.BlockSpec(memory_space=pltpu.VMEM),),
        input_output_aliases={i: 2 + i for i in range(2 * n)},
        compiler_params=pltpu.CompilerParams(has_side_effects=_EFFECT))(*operands, *after)
    return dict(send=res[0], recv=res[1], ins=list(res[2:2 + n]), lands=list(res[2 + n:2 + 2 * n]), token=res[-1])


def _split_wait(st, scatter, after, name):
    n = len(st["ins"])

    def body(*refs):
        for cp in _split_copies(refs[:n], refs[n:2 * n], refs[2 * n], refs[2 * n + 1], scatter):
            cp.wait_send()
            cp.wait_recv()

    arrs = st["ins"] + st["lands"]
    res = pl.pallas_call(
        body, name=name, out_shape=tuple(pltpu.HBM(a.shape, a.dtype) for a in arrs),
        in_specs=[_HBM] * (2 * n) + [_SEM, _SEM] + [pl.BlockSpec(memory_space=pl.ANY)] * len(after), out_specs=(_HBM,) * (2 * n),
        input_output_aliases={i: i for i in range(2 * n)},
        compiler_params=pltpu.CompilerParams(has_side_effects=_EFFECT))(*arrs, st["send"], st["recv"], *after)
    me = _flat_id(*_my_coords())
    out = []
    for src, land in zip(res[:n], res[n:]):
        own = lax.dynamic_index_in_dim(src, me, 0, keepdims=True) if scatter else src[None]
        out.append(lax.dynamic_update_slice_in_dim(land, own, me, 0))
    return out


def _adam_step(p_ref, w_ref, m_ref, v_ref, g_ref, d_ref, nm_ref, nv_ref):
    g = p_ref[0].astype(F32)
    for j in range(1, p_ref.shape[0]):
        g = g + p_ref[j].astype(F32)
    g_ref[...] = g
    nm = ADAM_B1 * m_ref[...] + (1.0 - ADAM_B1) * g
    nv = ADAM_B2 * v_ref[...] + (1.0 - ADAM_B2) * (g * g)
    nm_ref[...] = nm
    nv_ref[...] = nv
    m_hat = nm / (1.0 - ADAM_B1 ** ADAM_STEP)
    v_hat = nv / (1.0 - ADAM_B2 ** ADAM_STEP)
    d_ref[...] = -ADAM_LR * (m_hat / (jnp.sqrt(v_hat) + ADAM_EPS) + ADAM_WD * w_ref[...])


def _adam_sum(parts, w, m, v, name):
    P, R, C = parts.shape
    tr = _pick(R, (256, 128, 64, 32, 8)) if C <= 1024 else _pick(R, (128, 64, 32, 8))
    blk = pl.BlockSpec((tr, C), lambda i: (i, 0))
    return pl.pallas_call(
        functools.partial(_adam_step), name=name, grid=(R // tr,),
        in_specs=[pl.BlockSpec((P, tr, C), lambda i: (0, i, 0)), blk, blk, blk],
        out_specs=[blk] * 4, out_shape=[jax.ShapeDtypeStruct((R, C), F32)] * 4, compiler_params=_cp(("parallel",)))(parts, w, m, v)


def _adam_small(loss_parts, parts, ws, ms, vs):
    n = len(parts)

    def body(*refs):
        loss_ref, ins, outs, total_ref = refs[0], refs[1:4 * n + 1], refs[4 * n + 1:-1], refs[-1]
        for i in range(n):
            _adam_step(ins[i], ins[n + i], ins[2 * n + i], ins[3 * n + i], *outs[4 * i:4 * i + 4])
        total = loss_ref[0]
        for d in range(1, N_DEV):
            total = total + loss_ref[d]
        total_ref[...] = total

    out_shape = [jax.ShapeDtypeStruct(w.shape, F32) for w in ws for _ in range(4)] + [jax.ShapeDtypeStruct(loss_parts.shape[1:], F32)]
    res = pl.pallas_call(body, name="adam_small", out_shape=out_shape)(loss_parts, *parts, *ws, *ms, *vs)
    return res[-1], [tuple(res[4 * i:4 * i + 4]) for i in range(n)]


_WEIGHTS = ["attn_norm_w", "w_in", "conv_w", "conv_b", "dt_bias", "a_log", "d_skip", "ssd_norm_w", "cmp_w1_k", "cmp_w2_k",
            "cmp_w1_v", "cmp_w2_v", "cmp_pe_k", "cmp_pe_v", "w_out", "ffn_norm_w", "w_gate", "w_up", "w_down", "final_norm_w"]
_BIG = ["w_in", "w_gate", "w_up", "w_down", "w_out", "cmp_w1_k", "cmp_w1_v"]
_COL_SHARDED = ("w_in", "w_gate", "w_up")
_REPLICATED = ["attn_norm_w", "conv_b", "dt_bias", "a_log", "d_skip", "ssd_norm_w", "cmp_pe_k", "cmp_pe_v", "ffn_norm_w",
               "final_norm_w"]
_SMALL_SHARDED = ["conv_w", "cmp_w2_k", "cmp_w2_v"]


def _cols_to_slabs(g):
    R = g.shape[0]
    return g.reshape(R, N_DEV, -1).transpose(1, 0, 2)


def _slabs_to_cols(s):
    return s.transpose(1, 0, 2).reshape(s.shape[1], -1)


def kernel(x, attn_norm_w, w_in, conv_w, conv_b, dt_bias, a_log, d_skip, ssd_norm_w, cmp_w1_k, cmp_w2_k, cmp_w1_v, cmp_w2_v, cmp_pe_k, cmp_pe_v, w_out, ffn_norm_w, w_gate, w_up, w_down, final_norm_w, loss_target, m_attn_norm_w, m_w_in, m_conv_w, m_conv_b, m_dt_bias, m_a_log, m_d_skip, m_ssd_norm_w, m_cmp_w1_k, m_cmp_w2_k, m_cmp_w1_v, m_cmp_w2_v, m_cmp_pe_k, m_cmp_pe_v, m_w_out, m_ffn_norm_w, m_w_gate, m_w_up, m_w_down, m_final_norm_w, v_attn_norm_w, v_w_in, v_conv_w, v_conv_b, v_dt_bias, v_a_log, v_d_skip, v_ssd_norm_w, v_cmp_w1_k, v_cmp_w2_k, v_cmp_w1_v, v_cmp_w2_v, v_cmp_pe_k, v_cmp_pe_v, v_w_out, v_ffn_norm_w, v_w_gate, v_w_up, v_w_down, v_final_norm_w):
    a = dict(locals())

    shard = {n: a[n][0].astype(_MXU) for n in _BIG}
    got = _gather_two_level([shard[n] for n in _EARLY] + [cmp_w2_k[0], cmp_w2_v[0], conv_w[0]], "gather_early")
    st_late = _split_start([shard[n] for n in _LATE], False, "gather_late_start", after=(got[0],))

    def assemble(n, t):
        return _cols_from_slabs(t) if n in _COL_SHARDED else t.reshape(-1, t.shape[-1])

    p = dict(attn_norm_w=attn_norm_w, conv_b=conv_b, dt_bias=dt_bias, a_log=a_log, d_skip=d_skip, ssd_norm_w=ssd_norm_w,
             cmp_pe_k=cmp_pe_k.reshape(1, -1), cmp_pe_v=cmp_pe_v.reshape(1, -1), ffn_norm_w=ffn_norm_w,
             final_norm_w=final_norm_w.reshape(1, -1))

    w_main, w_small = _w_in_from_slabs(got[0])
    p.update(before_in_proj=st_late["token"],
             w_main=w_main, w_small=w_small, cmp_w1_k=assemble("cmp_w1_k", got[1]), cmp_w1_v=assemble("cmp_w1_v", got[2]),
             cmp_w2_k=assemble("cmp_w2_k", got[3]).astype(_MXU), cmp_w2_v=assemble("cmp_w2_v", got[4]).astype(_MXU),
             conv_w=_slabs_to_cols(got[5]))

    def late_weights(after):
        got_late = _split_wait(st_late, False, (after,), "gather_late_wait")
        return {n: assemble(n, t) for n, t in zip(_LATE, got_late)}

    def slabs_of(g, n):
        if n == "w_in":
            return _w_in_to_slabs(g["w_main"], g["w_small"])
        return _slabs_from_cols(g[n]) if n in _COL_SHARDED else g[n].reshape(N_DEV, -1, g[n].shape[-1])

    started = []

    def grads_ready(names, g):
        started.append((names, _split_start([slabs_of(g, n) for n in names], True, "scatter_grads_start_%d" % len(started))))
        return started[-1][1]["token"]

    loss_part, grad_x, g = _local_step(x[0], loss_target[0], p, late_weights, grads_ready)

    out, after = {}, (started[-1][1]["token"],)
    for i, (names, st) in enumerate(started):
        if i == len(started) - 1:
            after = after + (grad_x,)
        received = _split_wait(st, True, after, "scatter_grads_wait_%d" % i)
        for n, parts in zip(names, received):
            out[n] = _adam_sum(parts, a[n][0], a["m_" + n][0], a["v_" + n][0], "adam_" + n)
        after = (out[names[-1]][0],)

    small_names = _REPLICATED + _SMALL_SHARDED
    partials = [g[n] for n in _REPLICATED] + [_cols_to_slabs(g["conv_w"])] + [
        g[n].reshape(N_DEV, -1, g[n].shape[-1]) for n in ("cmp_w2_k", "cmp_w2_v")]
    gathered = _exchange([loss_part] + partials, [False] * (1 + len(_REPLICATED)) + [True] * len(_SMALL_SHARDED),
                         "exchange_small_grads", after=(received[0],))
    shapes2d = [t.shape[1:] for t in gathered[1:]]
    loss, res_small = _adam_small(gathered[0], gathered[1:],
                                  *[[a[pre + n].reshape(s) for n, s in zip(small_names, shapes2d)] for pre in ("", "m_", "v_")])
    for n, r in zip(small_names, res_small):
        out[n] = r

    outs = [loss[0, 0], grad_x[None]]
    for j in range(4):
        for n in _WEIGHTS:
            outs.append(out[n][j].reshape(a[n].shape))
    return tuple(outs)
```

```python
import functools

import numpy as np
import jax
import jax.numpy as jnp
from jax import lax
from jax.experimental import pallas as pl
from jax.experimental.pallas import tpu as pltpu

F32 = jnp.float32
_MXU = jnp.bfloat16

N_DEV = 8
D_MODEL = 2048
SSD_WIDTH = 1024
ATT_WIDTH = 1024
SSD_HEADS = 16
SSD_P = 64
SSD_N = 128
SSD_L = 128
SSD_G = 2
CONV_CH = 1536
CONV_K = 4
HD = 64
N_HEADS = 16
N_KV = 4
GRP = 4
CMP_HID = 256
SEL_BLOCK = 64
N_SELECT = 16
WINDOW = 512
ROPE_DIM = 16
ROPE_THETA = 500000.0
D_FF = 5632
EPS = 1e-6
NEG = -1e30
FORCE = 1e4
SCALE = HD ** -0.5
D_IN = 5184
W_MAIN = 5120
W_SMALL = 128
VMEM_LIMIT = 52 * 1024 * 1024

ADAM_LR, ADAM_B1, ADAM_B2, ADAM_EPS, ADAM_WD, ADAM_STEP = 0.001, 0.9, 0.999, 1e-08, 0.01, 10


def _pick(n, cands):
    for c in cands:
        if n % c == 0:
            return c
    return n


def _cp(sem=None):
    return pltpu.CompilerParams(dimension_semantics=sem, vmem_limit_bytes=VMEM_LIMIT)


def _sigmoid(x):
    return 1.0 / (1.0 + jnp.exp(-x))


def _dot(a, b, dims, split=None):
    dn = {"nn": (((1,), (0,)), ((), ())), "nt": (((1,), (1,)), ((), ())), "tn": (((0,), (0,)), ((), ()))}[dims]
    mm = lambda x, y: lax.dot_general(x.astype(_MXU), y.astype(_MXU), dn, preferred_element_type=F32)
    if split is None:
        return mm(a, b)
    x = (a if split == "a" else b).astype(F32)
    hi = x.astype(_MXU)
    lo = x - hi.astype(F32)
    return mm(hi, b) + mm(lo, b) if split == "a" else mm(a, hi) + mm(a, lo)


LANE = 128
MM_TILE = 1024
MM_K_WHOLE = 2048
MM_K_STEP = 2816
TN_ACC_ELEMS = 3 * 2 ** 20
TN_K_STEP = 512


def _largest_tile(n, cap):
    if n <= cap:
        return n
    best = LANE
    for t in range(LANE, cap + 1, LANE):
        if n % t == 0:
            best = t
    return best


def _mm_tiles(mode, M, N, K):
    if mode == "tn":
        tm = _largest_tile(M, 2 * MM_TILE)
        return tm, _largest_tile(N, TN_ACC_ELEMS // tm), _largest_tile(K, TN_K_STEP)
    tk = K if K <= MM_K_WHOLE else _largest_tile(K, MM_K_STEP)
    return _largest_tile(M, MM_TILE), _largest_tile(N, MM_TILE), tk


def _mm(a, b, mode, out_dtype, name, res=None, after=None):
    if mode == "nn":
        (M, K), N = a.shape, b.shape[1]
    elif mode == "nt":
        (M, K), N = a.shape, b.shape[0]
    else:
        (K, M), N = a.shape, b.shape[1]
    tm, tn, tk = _mm_tiles(mode, M, N, K)
    nk = K // tk
    a_spec = pl.BlockSpec((tk, tm), lambda i, j, k: (k, i)) if mode == "tn" else pl.BlockSpec((tm, tk), lambda i, j, k: (i, k))
    b_spec = pl.BlockSpec((tn, tk), lambda i, j, k: (j, k)) if mode == "nt" else pl.BlockSpec((tk, tn), lambda i, j, k: (k, j))
    o_spec = pl.BlockSpec((tm, tn), lambda i, j, k: (i, j))

    def finish(r, r_ref, o_ref):
        if res is not None:
            r = r + r_ref[...].astype(F32)
        o_ref[...] = r.astype(out_dtype)

    def body_one_step(*refs):
        a_ref, b_ref, o_ref = refs[0], refs[1], refs[-1]
        finish(_dot(a_ref[...], b_ref[...], mode), refs[2], o_ref)

    def body(*refs):
        a_ref, b_ref, o_ref, acc = refs[0], refs[1], refs[-2], refs[-1]
        k = pl.program_id(2)

        @pl.when(k == 0)
        def _():
            acc[...] = jnp.zeros_like(acc)

        acc[...] += _dot(a_ref[...], b_ref[...], mode)

        @pl.when(k == nk - 1)
        def _():
            finish(acc[...], refs[2], o_ref)

    ins, specs = [a, b], [a_spec, b_spec]
    if res is not None:
        ins.append(res)
        specs.append(o_spec)
    if after is not None:
        ins.append(after)
        specs.append(pl.BlockSpec(memory_space=pl.ANY))
    return pl.pallas_call(
        body_one_step if nk == 1 else body, name=name, grid=(M // tm, N // tn, nk), in_specs=specs, out_specs=o_spec,
        out_shape=jax.ShapeDtypeStruct((M, N), out_dtype), scratch_shapes=[] if nk == 1 else [pltpu.VMEM((tm, tn), F32)],
        compiler_params=_cp(("parallel", "parallel", "arbitrary")))(*ins)


def _ffn_up(v, w_gate, w_up):
    S, D = v.shape
    F = w_gate.shape[1]
    tm, tn = _largest_tile(S, MM_TILE), _largest_tile(F, MM_TILE // 2)

    def body(v_ref, wg_ref, wu_ref, gt_ref, up_ref, act_ref):
        vv = v_ref[...]
        g = _dot(vv, wg_ref[...], "nn")
        u = _dot(vv, wu_ref[...], "nn")
        gt_ref[...] = g
        up_ref[...] = u
        act_ref[...] = (g * _sigmoid(g) * u).astype(act_ref.dtype)

    o_spec = pl.BlockSpec((tm, tn), lambda i, j: (i, j))
    w_spec = pl.BlockSpec((D, tn), lambda i, j: (0, j))
    return pl.pallas_call(
        body, name="ffn_up", grid=(S // tm, F // tn),
        in_specs=[pl.BlockSpec((tm, D), lambda i, j: (i, 0)), w_spec, w_spec], out_specs=[o_spec, o_spec, o_spec],
        out_shape=[jax.ShapeDtypeStruct((S, F), F32), jax.ShapeDtypeStruct((S, F), F32), jax.ShapeDtypeStruct((S, F), _MXU)],
        compiler_params=_cp(("parallel", "parallel")))(v, w_gate, w_up)


def _ffn_dv(dgt, dup, w_gate, w_up, after):
    S, F = dgt.shape
    D = w_gate.shape[0]
    tm, tn, _ = _mm_tiles("nt", S, D, F)
    tk = _largest_tile(F, MM_K_STEP // 2)
    nk = F // tk

    def body(g_ref, u_ref, wg_ref, wu_ref, *rest):
        o_ref, acc = rest[-2], rest[-1]
        k = pl.program_id(2)

        @pl.when(k == 0)
        def _():
            acc[...] = jnp.zeros_like(acc)

        acc[...] += _dot(g_ref[...], wg_ref[...], "nt") + _dot(u_ref[...], wu_ref[...], "nt")

        @pl.when(k == nk - 1)
        def _():
            o_ref[...] = acc[...]

    a_spec = pl.BlockSpec((tm, tk), lambda i, j, k: (i, k))
    w_spec = pl.BlockSpec((tn, tk), lambda i, j, k: (j, k))
    ins, specs = [dgt, dup, w_gate, w_up], [a_spec, a_spec, w_spec, w_spec]
    if after is not None:
        ins.append(after)
        specs.append(pl.BlockSpec(memory_space=pl.ANY))
    return pl.pallas_call(
        body, name="ffn_dv", grid=(S // tm, D // tn, nk), in_specs=specs, out_specs=pl.BlockSpec((tm, tn), lambda i, j, k: (i, j)),
        out_shape=jax.ShapeDtypeStruct((S, D), F32), scratch_shapes=[pltpu.VMEM((tm, tn), F32)],
        compiler_params=_cp(("parallel", "parallel", "arbitrary")))(*ins)


def _ffn_dact(dh2, w_down, gt, up):
    S, D = dh2.shape
    F = w_down.shape[0]
    tm, tn = _largest_tile(S, MM_TILE), _largest_tile(F, MM_TILE // 2)

    def body(d_ref, w_ref, gt_ref, up_ref, dg_ref, du_ref):
        da, g, u = _dot(d_ref[...], w_ref[...], "nt"), gt_ref[...], up_ref[...]
        s = _sigmoid(g)
        dg_ref[...] = (da * u * (s * (1.0 + g * (1.0 - s)))).astype(dg_ref.dtype)
        du_ref[...] = (da * (g * s)).astype(du_ref.dtype)

    o_spec = pl.BlockSpec((tm, tn), lambda i, j: (i, j))
    return pl.pallas_call(
        body, name="ffn_dact", grid=(S // tm, F // tn),
        in_specs=[pl.BlockSpec((tm, D), lambda i, j: (i, 0)), pl.BlockSpec((tn, D), lambda i, j: (j, 0)), o_spec, o_spec],
        out_specs=[o_spec, o_spec],
        out_shape=[jax.ShapeDtypeStruct((S, F), _MXU), jax.ShapeDtypeStruct((S, F), _MXU)],
        compiler_params=_cp(("parallel", "parallel")))(dh2, w_down, gt, up)


def _rms_fwd(x, w, name):
    S, D = x.shape
    tr = _pick(S, (256, 128))

    def body(x_ref, w_ref, xn_ref, rs_ref):
        xv = x_ref[...]
        rs = lax.rsqrt(jnp.mean(xv * xv, axis=-1, keepdims=True) + EPS)
        xn_ref[...] = ((xv * rs) * w_ref[...]).astype(xn_ref.dtype)
        rs_ref[...] = rs

    return pl.pallas_call(
        body, name=name, grid=(S // tr,),
        in_specs=[pl.BlockSpec((tr, D), lambda i: (i, 0)), pl.BlockSpec((1, D), lambda i: (0, 0))],
        out_specs=[pl.BlockSpec((tr, D), lambda i: (i, 0)), pl.BlockSpec((tr, 1), lambda i: (i, 0))],
        out_shape=[jax.ShapeDtypeStruct((S, D), _MXU), jax.ShapeDtypeStruct((S, 1), F32)],
        compiler_params=_cp(("parallel",)))(x, w)


def _rms_bwd(dyn, x, rs, w, res, name):
    S, D = x.shape
    tr = _pick(S, (256, 128))

    def body(dy_ref, x_ref, rs_ref, w_ref, res_ref, dx_ref, dxb_ref, dw_ref):
        @pl.when(pl.program_id(0) == 0)
        def _():
            dw_ref[...] = jnp.zeros_like(dw_ref)

        dy, r = dy_ref[...].astype(F32), rs_ref[...]
        xhat = x_ref[...] * r
        dw_ref[...] += jnp.sum(dy * xhat, axis=0, keepdims=True)
        dxhat = dy * w_ref[...]
        dx = res_ref[...] + r * (dxhat - xhat * jnp.mean(dxhat * xhat, axis=-1, keepdims=True))
        dx_ref[...] = dx
        dxb_ref[...] = dx.astype(dxb_ref.dtype)

    row = pl.BlockSpec((tr, D), lambda i: (i, 0))
    vec = pl.BlockSpec((1, D), lambda i: (0, 0))
    return pl.pallas_call(
        body, name=name, grid=(S // tr,),
        in_specs=[row, row, pl.BlockSpec((tr, 1), lambda i: (i, 0)), vec, row], out_specs=[row, row, vec],
        out_shape=[jax.ShapeDtypeStruct((S, D), F32), jax.ShapeDtypeStruct((S, D), _MXU), jax.ShapeDtypeStruct((1, D), F32)],
        compiler_params=_cp(("arbitrary",)))(dyn, x, rs, w, res)


def _final_loss(h2, w, tgt):
    S, D = h2.shape
    tr = _pick(S, (256, 128))

    def body(h_ref, w_ref, t_ref, loss_ref, dh_ref, dhb_ref, dw_ref):
        @pl.when(pl.program_id(0) == 0)
        def _():
            dw_ref[...] = jnp.zeros_like(dw_ref)
            loss_ref[...] = jnp.zeros_like(loss_ref)

        hv, wv = h_ref[...], w_ref[...]
        rs = lax.rsqrt(jnp.mean(hv * hv, axis=-1, keepdims=True) + EPS)
        xhat = hv * rs
        err = xhat * wv - t_ref[...]
        row = jnp.mean(err * err, axis=-1, keepdims=True)
        loss_ref[...] += jnp.broadcast_to(0.5 * jnp.sum(row, axis=0, keepdims=True), loss_ref.shape)
        dy = err * (1.0 / D)
        dw_ref[...] += jnp.sum(dy * xhat, axis=0, keepdims=True)
        dxhat = dy * wv
        dh = rs * (dxhat - xhat * jnp.mean(dxhat * xhat, axis=-1, keepdims=True))
        dh_ref[...] = dh
        dhb_ref[...] = dh.astype(dhb_ref.dtype)

    row = pl.BlockSpec((tr, D), lambda i: (i, 0))
    vec = pl.BlockSpec((1, D), lambda i: (0, 0))
    return pl.pallas_call(
        body, name="final_loss", grid=(S // tr,), in_specs=[row, vec, row],
        out_specs=[pl.BlockSpec((1, LANE), lambda i: (0, 0)), row, row, vec],
        out_shape=[jax.ShapeDtypeStruct((1, LANE), F32), jax.ShapeDtypeStruct((S, D), F32), jax.ShapeDtypeStruct((S, D), _MXU),
                   jax.ShapeDtypeStruct((1, D), F32)],
        compiler_params=_cp(("arbitrary",)))(h2, w, tgt)


def _shift_rows(x, k, rows):
    if k == 0:
        return x
    S = x.shape[0]
    r = pltpu.roll(x, k % S, axis=0)
    ok = (rows >= k) if k > 0 else (rows < S + k)
    return jnp.where(ok, r, 0.0)


XBC_COL0 = SSD_WIDTH // 128


def _conv_fwd(proj, conv_w, conv_b):
    S = proj.shape[0]
    nct = CONV_CH // 128

    def body(x_ref, w_ref, b_ref, o_ref):
        x = x_ref[...]
        rows = lax.broadcasted_iota(jnp.int32, x.shape, 0)
        c = b_ref[...] + w_ref[3:4, :] * x
        for k in range(1, CONV_K):
            c = c + w_ref[3 - k:4 - k, :] * _shift_rows(x, k, rows)
        o_ref[...] = c * _sigmoid(c)

    return pl.pallas_call(
        body, name="conv_fwd", grid=(nct,),
        in_specs=[pl.BlockSpec((S, 128), lambda j: (0, XBC_COL0 + j)), pl.BlockSpec((CONV_K, 128), lambda j: (0, j)),
                  pl.BlockSpec((1, 128), lambda j: (0, j))],
        out_specs=pl.BlockSpec((S, 128), lambda j: (0, j)),
        out_shape=jax.ShapeDtypeStruct((S, CONV_CH), F32), compiler_params=_cp(("parallel",)))(proj, conv_w, conv_b)


def _conv_bwd(proj, conv_w, conv_b, dxa):
    S = proj.shape[0]
    nct = CONV_CH // 128

    def body(x_ref, w_ref, b_ref, d_ref, dx_ref, dw_ref, db_ref):
        x = x_ref[...]
        rows = lax.broadcasted_iota(jnp.int32, x.shape, 0)
        xs = [_shift_rows(x, k, rows) for k in range(CONV_K)]
        c = b_ref[...] + w_ref[3:4, :] * x
        for k in range(1, CONV_K):
            c = c + w_ref[3 - k:4 - k, :] * xs[k]
        s = _sigmoid(c)
        dc = d_ref[...] * (s * (1.0 + c * (1.0 - s)))
        dx = w_ref[3:4, :] * dc
        for k in range(1, CONV_K):
            dx = dx + w_ref[3 - k:4 - k, :] * _shift_rows(dc, -k, rows)
        dx_ref[...] = dx.astype(dx_ref.dtype)
        for k in range(CONV_K):
            dw_ref[3 - k:4 - k, :] = jnp.sum(dc * xs[k], axis=0, keepdims=True)
        db_ref[...] = jnp.sum(dc, axis=0, keepdims=True)

    col = pl.BlockSpec((S, 128), lambda j: (0, j))
    return pl.pallas_call(
        body, name="conv_bwd", grid=(nct,),
        in_specs=[pl.BlockSpec((S, 128), lambda j: (0, XBC_COL0 + j)), pl.BlockSpec((CONV_K, 128), lambda j: (0, j)),
                  pl.BlockSpec((1, 128), lambda j: (0, j)), col],
        out_specs=[col, pl.BlockSpec((CONV_K, 128), lambda j: (0, j)), pl.BlockSpec((1, 128), lambda j: (0, j))],
        out_shape=[jax.ShapeDtypeStruct((S, CONV_CH), _MXU), jax.ShapeDtypeStruct((CONV_K, CONV_CH), F32),
                   jax.ShapeDtypeStruct((1, CONV_CH), F32)],
        compiler_params=_cp(("parallel",)))(proj, conv_w, conv_b, dxa)


def _ssd_consts():
    L = SSD_L
    r = lax.broadcasted_iota(jnp.int32, (L, L), 0)
    c = lax.broadcasted_iota(jnp.int32, (L, L), 1)
    causal = r >= c
    upper = (r <= c).astype(F32)
    hr = lax.broadcasted_iota(jnp.int32, (SSD_HEADS, SSD_WIDTH), 0)
    hc = lax.broadcasted_iota(jnp.int32, (SSD_HEADS, SSD_WIDTH), 1)
    expand = (lax.shift_right_logical(hc, 6) == hr).astype(F32)
    return causal, causal.astype(F32), upper, expand


def _softplus(x):
    return jnp.maximum(x, 0.0) + jnp.log(1.0 + jnp.exp(-jnp.abs(x)))


def _ssd_scalars(dtr, dt_bias, a_log, tri, upper, expand):
    dt = _softplus(dtr + dt_bias)
    A = -jnp.exp(a_log)
    adt = dt * A
    acum = _dot(tri, adt, "nn", split="b")
    acum_t = _dot(adt, upper, "tn", split="a")
    alast = acum[SSD_L - 1:SSD_L, :]
    e = jnp.exp(acum)
    wdec = jnp.exp(alast - acum)
    gam = jnp.exp(alast)
    ex = lambda t: _dot(t, expand, "nn", split="a")
    gam8 = jnp.broadcast_to(gam, (8, SSD_HEADS))
    return dt, A, acum, acum_t, e, wdec, gam, ex(dt), ex(e), ex(wdec), ex(gam8)[0:1, :]


def _ssd_fwd(proj, proj_small, xa, dt_bias, a_log, d_skip, norm_w):
    S = proj.shape[0]
    L, N, W = SSD_L, SSD_N, SSD_WIDTH
    nc = S // L

    def body(z_ref, xa_ref, dtr_ref, dtb_ref, al_ref, dsk_ref, nw_ref, yo_ref, y_ref, rs_ref, hs_ref, h_scr, y_scr):
        @pl.when(pl.program_id(0) == 0)
        def _():
            h_scr[...] = jnp.zeros_like(h_scr)

        causal, tri, upper, expand = _ssd_consts()
        dt, A, acum, acum_t, e, wdec, gam, dtE, eE, wE, gamE = _ssd_scalars(dtr_ref[:, 0:SSD_HEADS], dtb_ref[...], al_ref[...], tri, upper, expand)
        xs = xa_ref[:, 0:W]
        X = xs * dtE
        XW = X * wE
        hs_ref[0] = h_scr[...]
        for g in range(SSD_G):
            gs = slice(g * 512, (g + 1) * 512)
            Bg = xa_ref[:, W + g * N:W + (g + 1) * N]
            Cg = xa_ref[:, W + SSD_G * N + g * N:W + SSD_G * N + (g + 1) * N]
            Hg = h_scr[:, gs]
            CB = _dot(Cg, Bg, "nt")
            yoff = _dot(Cg, Hg, "nn") * eE[:, gs]
            st = _dot(Bg, XW[:, gs], "tn")
            for j in range(8):
                h = g * 8 + j
                hsl = slice(h * SSD_P, (h + 1) * SSD_P)
                lam = jnp.exp(jnp.where(causal, acum[:, h:h + 1] - acum_t[h:h + 1, :], -jnp.inf))
                y_scr[:, hsl] = _dot(CB * lam, X[:, hsl], "nn") + yoff[:, j * SSD_P:(j + 1) * SSD_P]
            h_scr[:, gs] = gamE[:, gs] * Hg + st
        dskE = _dot(jnp.broadcast_to(dsk_ref[...], (8, SSD_HEADS)), expand, "nn", split="a")[0:1, :]
        y = y_scr[...] + dskE * xs
        y_ref[...] = y
        zv = z_ref[...]
        yg = y * (zv * _sigmoid(zv))
        rs = lax.rsqrt(jnp.mean(yg * yg, axis=-1, keepdims=True) + EPS)
        rs_ref[...] = rs
        yo_ref[...] = ((yg * rs) * nw_ref[...]).astype(yo_ref.dtype)

    p16 = pl.BlockSpec((1, SSD_HEADS), lambda c: (0, 0))
    return pl.pallas_call(
        body, name="ssd_fwd", grid=(nc,),
        in_specs=[pl.BlockSpec((L, W), lambda c: (c, 0)), pl.BlockSpec((L, CONV_CH), lambda c: (c, 0)),
                  pl.BlockSpec((L, W_SMALL), lambda c: (c, 0)), p16, p16, p16, pl.BlockSpec((1, W), lambda c: (0, 0))],
        out_specs=[pl.BlockSpec((L, W), lambda c: (c, 0)), pl.BlockSpec((L, W), lambda c: (c, 0)),
                   pl.BlockSpec((L, 1), lambda c: (c, 0)), pl.BlockSpec((1, N, W), lambda c: (c, 0, 0))],
        out_shape=[jax.ShapeDtypeStruct((S, W), _MXU), jax.ShapeDtypeStruct((S, W), F32), jax.ShapeDtypeStruct((S, 1), F32),
                   jax.ShapeDtypeStruct((nc, N, W), F32)],
        scratch_shapes=[pltpu.VMEM((N, W), F32), pltpu.VMEM((L, W), F32)],
        compiler_params=_cp(("arbitrary",)))(proj, xa, proj_small, dt_bias, a_log, d_skip, norm_w)


def _ssd_bwd(dmixed, proj, proj_small, xa, y, rs2, hs, dt_bias, a_log, d_skip, norm_w):
    S = proj.shape[0]
    L, N, W, H = SSD_L, SSD_N, SSD_WIDTH, SSD_HEADS
    nc = S // L

    def body(dyo_ref, z_ref, xa_ref, dtr_ref, y_ref, rs_ref, hs_ref, dtb_ref, al_ref, dsk_ref, nw_ref,
             dz_ref, dxa_ref, ddtr_ref, ddtb_ref, dal_ref, ddsk_ref, dnw_ref, dh_scr, dx_scr):
        @pl.when(pl.program_id(0) == 0)
        def _():
            dh_scr[...] = jnp.zeros_like(dh_scr)
            ddtb_ref[...] = jnp.zeros_like(ddtb_ref)
            dal_ref[...] = jnp.zeros_like(dal_ref)
            ddsk_ref[...] = jnp.zeros_like(ddsk_ref)
            dnw_ref[...] = jnp.zeros_like(dnw_ref)

        causal, tri, upper, expand = _ssd_consts()
        heads = lambda t: _dot(t, expand, "nt", split="a")
        onehot = lambda h: (lax.broadcasted_iota(jnp.int32, (1, H), 1) == h).astype(F32)

        zv, yv, rs = z_ref[...], y_ref[...], rs_ref[...]
        sz = _sigmoid(zv)
        zs = zv * sz
        xhat = (yv * zs) * rs
        dyo = dyo_ref[...].astype(F32)
        dnw_ref[...] += jnp.sum(dyo * xhat, axis=0, keepdims=True)
        dxhat = dyo * nw_ref[...]
        dyg = rs * (dxhat - xhat * jnp.mean(dxhat * xhat, axis=-1, keepdims=True))
        dz_ref[...] = (dyg * yv * (sz * (1.0 + zv * (1.0 - sz)))).astype(dz_ref.dtype)
        dy = dyg * zs

        dtr = dtr_ref[:, 0:H]
        dt, A, acum, acum_t, e, wdec, gam, dtE, eE, wE, gamE = _ssd_scalars(dtr, dtb_ref[...], al_ref[...], tri, upper, expand)
        xs = xa_ref[:, 0:W]
        X = xs * dtE
        XW = X * wE
        dskE = _dot(jnp.broadcast_to(dsk_ref[...], (8, H)), expand, "nn", split="a")[0:1, :]
        ddsk_ref[...] += heads(jnp.broadcast_to(jnp.sum(dy * xs, axis=0, keepdims=True), (8, W)))[0:1, :]

        dYe = dy * eE
        dacum = jnp.zeros((L, H), F32)
        de_full = []
        dw_full = []
        dgam_full = []
        for g in range(SSD_G):
            gs = slice(g * 512, (g + 1) * 512)
            Bg = xa_ref[:, W + g * N:W + (g + 1) * N]
            Cg = xa_ref[:, W + SSD_G * N + g * N:W + SSD_G * N + (g + 1) * N]
            Hg = hs_ref[0, :, gs]
            dHn = dh_scr[:, gs]
            CH = _dot(Cg, Hg, "nn")
            de_full.append(dy[:, gs] * CH)
            dC = _dot(dYe[:, gs], Hg, "nt")
            dHs = gamE[:, gs] * dHn + _dot(Cg, dYe[:, gs], "tn")
            dgam_full.append(jnp.sum(dHn * Hg, axis=0, keepdims=True))
            BdS = _dot(Bg, dHn, "nn")
            dB = _dot(XW[:, gs], dHn, "nt")
            dx_scr[:, gs] = BdS * wE[:, gs]
            dw_full.append(BdS * X[:, gs])
            CB = _dot(Cg, Bg, "nt")
            dCB = jnp.zeros((L, L), F32)
            for j in range(8):
                h = g * 8 + j
                hsl = slice(h * SSD_P, (h + 1) * SSD_P)
                lam = jnp.exp(jnp.where(causal, acum[:, h:h + 1] - acum_t[h:h + 1, :], -jnp.inf))
                M = CB * lam
                dM = _dot(dy[:, hsl], X[:, hsl], "nt")
                dx_scr[:, hsl] += _dot(M, dy[:, hsl], "tn")
                dCB = dCB + dM * lam
                Q = dM * M
                rowsum = jnp.sum(Q, axis=1, keepdims=True)
                colsum = _dot(Q, jnp.ones((L, 8), F32), "tn", split="a")[:, 0:1]
                dacum = dacum + (rowsum - colsum) * onehot(h)
            dC = dC + _dot(dCB, Bg, "nn")
            dB = dB + _dot(dCB, Cg, "tn")
            dxa_ref[:, W + g * N:W + (g + 1) * N] = dB
            dxa_ref[:, W + SSD_G * N + g * N:W + SSD_G * N + (g + 1) * N] = dC
            dh_scr[:, gs] = dHs

        de16 = heads(jnp.concatenate(de_full, axis=1))
        dw16 = heads(jnp.concatenate(dw_full, axis=1))
        dgam16 = heads(jnp.broadcast_to(jnp.concatenate(dgam_full, axis=1), (8, W)))[0:1, :]
        dacum = dacum + de16 * e - dw16 * wdec
        dlast = jnp.sum(dw16 * wdec, axis=0, keepdims=True) + dgam16 * gam
        lastrow = (lax.broadcasted_iota(jnp.int32, (L, 1), 0) == L - 1).astype(F32)
        dacum = dacum + lastrow * dlast
        da = _dot(tri, dacum, "tn", split="b")
        dX = dx_scr[...]
        ddt = da * A + heads(dX * xs)
        dA = jnp.sum(da * dt, axis=0, keepdims=True)
        dal_ref[...] += dA * A
        ddtr = ddt * _sigmoid(dtr + dtb_ref[...])
        ddtb_ref[...] += jnp.sum(ddtr, axis=0, keepdims=True)
        ddtr_ref[...] = ddtr
        dxa_ref[:, 0:W] = dX * dtE + dy * dskE

    p16 = pl.BlockSpec((1, H), lambda c: (0, 0))
    rev = lambda c: (nc - 1 - c, 0)
    return pl.pallas_call(
        body, name="ssd_bwd", grid=(nc,),
        in_specs=[pl.BlockSpec((L, W), rev), pl.BlockSpec((L, W), rev), pl.BlockSpec((L, CONV_CH), rev),
                  pl.BlockSpec((L, W_SMALL), rev), pl.BlockSpec((L, W), rev), pl.BlockSpec((L, 1), rev),
                  pl.BlockSpec((1, N, W), lambda c: (nc - 1 - c, 0, 0)), p16, p16, p16, pl.BlockSpec((1, W), lambda c: (0, 0))],
        out_specs=[pl.BlockSpec((L, W), rev), pl.BlockSpec((L, CONV_CH), rev), pl.BlockSpec((L, H), rev),
                   p16, p16, p16, pl.BlockSpec((1, W), lambda c: (0, 0))],
        out_shape=[jax.ShapeDtypeStruct((S, W), _MXU), jax.ShapeDtypeStruct((S, CONV_CH), F32), jax.ShapeDtypeStruct((S, H), F32),
                   jax.ShapeDtypeStruct((1, H), F32), jax.ShapeDtypeStruct((1, H), F32), jax.ShapeDtypeStruct((1, H), F32),
                   jax.ShapeDtypeStruct((1, W), F32)],
        scratch_shapes=[pltpu.VMEM((N, W), F32), pltpu.VMEM((L, W), F32)],
        compiler_params=_cp(("arbitrary",)))(dmixed, proj, xa, proj_small, y, rs2, hs, dt_bias, a_log, d_skip, norm_w)


def _rope_tables(S):
    inv = 1.0 / (ROPE_THETA ** (jnp.arange(0, ROPE_DIM, 2, dtype=F32) / ROPE_DIM))
    ang = jnp.arange(S, dtype=F32)[:, None] * inv[None, :]
    cos, sin = jnp.cos(ang), jnp.sin(ang)
    half = ROPE_DIM // 2
    c64 = jnp.concatenate([cos, cos, jnp.ones((S, HD - ROPE_DIM), F32)], axis=1)
    s64 = jnp.concatenate([sin, sin, jnp.zeros((S, HD - ROPE_DIM), F32)], axis=1)
    del half
    return jnp.concatenate([c64, c64], axis=1), jnp.concatenate([s64, s64], axis=1)


def _rope(xs, blk0, width, cos, sin, sign, out_dtype, name, extra=None):
    S = xs[0].shape[0]
    tr = _pick(S, (512, 256, 128))
    nx = len(xs)

    def body(*refs):
        x_refs, c_ref, s_ref = refs[:nx], refs[nx], refs[nx + 1]
        e_ref = refs[nx + 2] if extra is not None else None
        o_ref = refs[-1]
        cv, sv = c_ref[...], s_ref[...] * sign
        lane = lax.broadcasted_iota(jnp.int32, (tr, 128), 1)
        first = (lane & (HD - 1)) < (ROPE_DIM // 2)
        for j in range(bw // 128):
            cs = slice(j * 128, (j + 1) * 128)
            xv = x_refs[0][:, cs].astype(F32)
            for r in x_refs[1:]:
                xv = xv + r[:, cs].astype(F32)
            out = _rotate128(xv, cv, sv, first)
            if extra is not None:
                out = out + e_ref[:, cs].astype(F32)
            o_ref[:, cs] = out.astype(out_dtype)

    bw = 512
    assert width % bw == 0 and (blk0 * 256) % bw == 0
    b0 = blk0 * 256 // bw
    t128 = pl.BlockSpec((tr, 128), lambda i, j: (i, 0))
    oblk = pl.BlockSpec((tr, bw), lambda i, j: (i, j))
    specs = [pl.BlockSpec((tr, bw), lambda i, j: (i, b0 + j))] * nx + [t128, t128]
    ins = list(xs) + [cos, sin]
    if extra is not None:
        assert (extra[1] * 256) % bw == 0
        ins.append(extra[0])
        eb = extra[1] * 256 // bw
        specs.append(pl.BlockSpec((tr, bw), lambda i, j: (i, eb + j)))
    return pl.pallas_call(
        body, name=name, grid=(S // tr, width // bw), in_specs=specs, out_specs=oblk,
        out_shape=jax.ShapeDtypeStruct((S, width), out_dtype), compiler_params=_cp(("parallel", "parallel")))(*ins)


def _rotate128(xv, cv, sv, first):
    rot = jnp.where(first, -pltpu.roll(xv, 128 - ROPE_DIM // 2, axis=1), pltpu.roll(xv, ROPE_DIM // 2, axis=1))
    return xv * cv + rot * sv


def _kv_prep(proj, cos, sin, tk):
    S = proj.shape[0]
    NB = S // SEL_BLOCK

    def body(ks_ref, vs_ref, kw_ref, vw_ref, c_ref, s_ref, *outs):
        cv, sv = c_ref[...], s_ref[...]
        lane = lax.broadcasted_iota(jnp.int32, (tk, 128), 1)
        first = (lane & (HD - 1)) < (ROPE_DIM // 2)
        key = pl.program_id(0) * tk + lax.broadcasted_iota(jnp.int32, (tk, NB), 0)
        onehot = (lax.shift_right_logical(key, 6) == lax.broadcasted_iota(jnp.int32, (tk, NB), 1)).astype(F32)
        for j, (ref, rotated) in enumerate(((ks_ref, True), (vs_ref, False), (kw_ref, True), (vw_ref, False))):
            nat, blk = outs[2 * j], outs[2 * j + 1]
            for half in range(2):
                xv = ref[:, half * 128:(half + 1) * 128]
                if rotated:
                    xv = _rotate128(xv, cv, sv, first)
                for e in range(2):
                    h = 2 * half + e
                    piece = xv[:, e * HD:(e + 1) * HD]
                    nat[h] = (jnp.concatenate([piece, onehot], axis=1) if j == 0 else piece).astype(nat.dtype)
                    blk[h, 0] = piece.T.astype(blk.dtype)

    col = lambda b: pl.BlockSpec((tk, 256), lambda i: (i, b))
    t128 = pl.BlockSpec((tk, 128), lambda i: (i, 0))
    nat_spec = lambda w: pl.BlockSpec((N_KV, tk, w), lambda i: (0, i, 0))
    blk_spec = pl.BlockSpec((N_KV, 1, HD, tk), lambda i: (0, i, 0, 0))
    nat_shape = lambda w: jax.ShapeDtypeStruct((N_KV, S, w), _MXU)
    blk_shape = jax.ShapeDtypeStruct((N_KV, S // tk, HD, tk), _MXU)
    widths = (HD + NB, HD, HD, HD)
    res = pl.pallas_call(
        body, name="kv_prep", grid=(S // tk,), in_specs=[col(KSB), col(VSB), col(KWB), col(VWB), t128, t128],
        out_specs=[s for w in widths for s in (nat_spec(w), blk_spec)],
        out_shape=[s for w in widths for s in (nat_shape(w), blk_shape)],
        compiler_params=_cp(("parallel",)))(proj, proj, proj, proj, cos, sin)
    return dict(ks_ext=res[0], ks_t=res[1], vs=res[2], vs_t=res[3], kw=res[4], kw_t=res[5], vw=res[6], vw_t=res[7])


def _dkv_post(dks, dvs, dkw, dvw, cos, sin):
    S = dks.shape[1]
    tr = _pick(S, (512, 256, 128))

    def body(dks_ref, dvs_ref, dkw_ref, dvw_ref, c_ref, s_ref, o_ref):
        cv, sv = c_ref[...], -s_ref[...]
        lane = lax.broadcasted_iota(jnp.int32, (tr, 128), 1)
        first = (lane & (HD - 1)) < (ROPE_DIM // 2)
        for j, (ref, rotated) in enumerate(((dks_ref, True), (dvs_ref, False), (dkw_ref, True), (dvw_ref, False))):
            for half in range(2):
                xv = jnp.concatenate([ref[2 * half], ref[2 * half + 1]], axis=1)
                if rotated:
                    xv = _rotate128(xv, cv, sv, first)
                o_ref[:, j * 256 + half * 128:j * 256 + (half + 1) * 128] = xv.astype(o_ref.dtype)

    hm = pl.BlockSpec((N_KV, tr, HD), lambda i: (0, i, 0))
    t128 = pl.BlockSpec((tr, 128), lambda i: (i, 0))
    return pl.pallas_call(
        body, name="dkv_post", grid=(S // tr,), in_specs=[hm, hm, hm, hm, t128, t128],
        out_specs=pl.BlockSpec((tr, 4 * 256), lambda i: (i, 0)), out_shape=jax.ShapeDtypeStruct((S, 4 * 256), _MXU),
        compiler_params=_cp(("parallel",)))(dks, dvs, dkw, dvw, cos, sin)


def _compress_fwd(R, pe, w1, w2):
    NC = R.shape[1]
    half = 16 * HD

    def body(r_ref, pe_ref, w1_ref, w2_ref, o_ref, hid_ref):
        r = r_ref[0]
        a = _dot(r + pe_ref[:, 0:half], w1_ref[0:half, :], "nn")
        b = _dot(r + pe_ref[:, half:2 * half], w1_ref[half:2 * half, :], "nn")
        hid = a + pltpu.roll(b, NC - 1, axis=0)
        hid_ref[0] = hid
        out = _dot(hid * _sigmoid(hid), w2_ref[...], "nn")
        rows = lax.broadcasted_iota(jnp.int32, out.shape, 0)
        o_ref[0] = jnp.where(rows < NC - 1, out, 0.0).astype(o_ref.dtype)

    return pl.pallas_call(
        body, name="compress_fwd", grid=(N_KV,),
        in_specs=[pl.BlockSpec((1, NC, half), lambda h: (h, 0, 0)), pl.BlockSpec((1, 2 * half), lambda h: (0, 0)),
                  pl.BlockSpec((2 * half, CMP_HID), lambda h: (0, 0)), pl.BlockSpec((CMP_HID, HD), lambda h: (0, 0))],
        out_specs=[pl.BlockSpec((1, NC, HD), lambda h: (h, 0, 0)), pl.BlockSpec((1, NC, CMP_HID), lambda h: (h, 0, 0))],
        out_shape=[jax.ShapeDtypeStruct((N_KV, NC, HD), _MXU), jax.ShapeDtypeStruct((N_KV, NC, CMP_HID), F32)],
        compiler_params=_cp(("parallel",)))(R, pe, w1, w2)


def _compress_bwd(R, pe, w1, w2, hid, dout):
    NC = R.shape[1]
    half = 16 * HD

    def body(r_ref, pe_ref, w1_ref, w2_ref, hid_ref, do_ref, dr_ref, dw1_ref, dw2_ref, dpe_ref):
        @pl.when(pl.program_id(0) == 0)
        def _():
            dw1_ref[...] = jnp.zeros_like(dw1_ref)
            dw2_ref[...] = jnp.zeros_like(dw2_ref)
            dpe_ref[...] = jnp.zeros_like(dpe_ref)

        r, hv, do = r_ref[0], hid_ref[0], do_ref[0]
        s = _sigmoid(hv)
        dw2_ref[...] += _dot(hv * s, do, "tn")
        dhid = _dot(do, w2_ref[...], "nt") * (s * (1.0 + hv * (1.0 - s)))
        rows = lax.broadcasted_iota(jnp.int32, dhid.shape, 0)
        dhid = jnp.where(rows < NC - 1, dhid, 0.0)
        dhid_dn = pltpu.roll(dhid, 1, axis=0)
        dw1_ref[0:half, :] += _dot(r + pe_ref[:, 0:half], dhid, "tn")
        dw1_ref[half:2 * half, :] += _dot(r + pe_ref[:, half:2 * half], dhid_dn, "tn")
        dxt = _dot(dhid, w1_ref[0:half, :], "nt")
        dxb = _dot(dhid_dn, w1_ref[half:2 * half, :], "nt")
        dr_ref[0] = dxt + dxb
        dpe_ref[:, 0:half] += jnp.sum(dxt, axis=0, keepdims=True)
        dpe_ref[:, half:2 * half] += jnp.sum(dxb, axis=0, keepdims=True)

    return pl.pallas_call(
        body, name="compress_bwd", grid=(N_KV,),
        in_specs=[pl.BlockSpec((1, NC, half), lambda h: (h, 0, 0)), pl.BlockSpec((1, 2 * half), lambda h: (0, 0)),
                  pl.BlockSpec((2 * half, CMP_HID), lambda h: (0, 0)), pl.BlockSpec((CMP_HID, HD), lambda h: (0, 0)),
                  pl.BlockSpec((1, NC, CMP_HID), lambda h: (h, 0, 0)), pl.BlockSpec((1, NC, HD), lambda h: (h, 0, 0))],
        out_specs=[pl.BlockSpec((1, NC, half), lambda h: (h, 0, 0)), pl.BlockSpec((2 * half, CMP_HID), lambda h: (0, 0)),
                   pl.BlockSpec((CMP_HID, HD), lambda h: (0, 0)), pl.BlockSpec((1, 2 * half), lambda h: (0, 0))],
        out_shape=[jax.ShapeDtypeStruct((N_KV, NC, half), F32), jax.ShapeDtypeStruct((2 * half, CMP_HID), F32),
                   jax.ShapeDtypeStruct((CMP_HID, HD), F32), jax.ShapeDtypeStruct((1, 2 * half), F32)],
        compiler_params=_cp(("arbitrary",)))(R, pe, w1, w2, hid, dout)


def _attn_cfg(S, Sk, mode):
    if mode == "cmp":
        return _pick(S, (512, 256, 128)), Sk
    return _pick(S, (256, 128)), _pick(Sk, (256, 128))


def _block_start(kb, tk):
    return kb * tk if isinstance(kb, int) else pl.multiple_of(kb * tk, tk)


def _pipelined_key_blocks(mode, q0, tq, tk, produce, consume):
    if mode == "cmp":
        produce(0, True, 0)
        consume(0, 0)
        return
    if mode == "win":
        assert tq == tk and WINDOW == 2 * tk
        last = q0 // tk
        first = jnp.maximum(last - 2, 0)

        @pl.when(last == 0)
        def _():
            produce(last, True, 0)
            consume(last, 0)

        @pl.when(last == 1)
        def _():
            produce(first, True, 0)
            produce(last, True, 1)
            consume(first, 0)
            consume(last, 1)

        @pl.when(last >= 2)
        def _():
            produce(first, True, 0)
            produce(first + 1, False, 1)
            consume(first, 0)
            produce(last, True, 0)
            consume(first + 1, 1)
            consume(last, 0)

        return
    first, n_plain, plain_masked = 0, q0 // tk, False
    last = first + n_plain
    pairs = jnp.maximum(n_plain - 1, 0) // 2

    @pl.when(n_plain >= 1)
    def _():
        produce(first, plain_masked, 0)

    def two(j, carry):
        kb = first + 2 * j
        produce(kb + 1, plain_masked, 1)
        consume(kb, 0)
        produce(kb + 2, plain_masked, 0)
        consume(kb + 1, 1)
        return carry

    lax.fori_loop(0, pairs, two, 0)
    kb = first + 2 * pairs
    left = n_plain - 2 * pairs

    @pl.when(left == 2)
    def _():
        produce(kb + 1, plain_masked, 1)
        consume(kb, 0)
        produce(last, True, 0)
        consume(kb + 1, 1)
        consume(last, 0)

    @pl.when(left == 1)
    def _():
        produce(last, True, 1)
        consume(kb, 0)
        consume(last, 1)

    @pl.when(left == 0)
    def _():
        produce(last, True, 0)
        consume(last, 0)


def _attn_bias(mode, q0, k0, tq, tk):
    k = k0 + lax.broadcasted_iota(jnp.int32, (tk, tq), 0)
    t = q0 + lax.broadcasted_iota(jnp.int32, (tk, tq), 1)
    if mode == "cmp":
        ok = (k * 16 + 31) <= t
    elif mode == "win":
        ok = (k <= t) & ((t - k) < WINDOW)
    else:
        ok = k <= t
    bias = jnp.where(ok, 0.0, NEG)
    return jnp.concatenate([bias] * GRP, axis=1), jnp.concatenate([ok.astype(F32)] * GRP, axis=1)


def _sel_operands(qs, selneg_ref):
    return jnp.concatenate([qs, jnp.concatenate([selneg_ref[0]] * GRP, axis=0)], axis=1)


def _stack_heads(ref, tq):
    return jnp.concatenate([ref[:, g * HD:(g + 1) * HD] for g in range(GRP)], axis=0)


def _scaled_queries(q_ref, tq):
    return (_stack_heads(q_ref, tq).astype(F32) * SCALE).astype(_MXU)


def _blocked_t(x, tk):
    n, Sk, d = x.shape
    return x.reshape(n, Sk // tk, tk, d).transpose(0, 1, 3, 2)


def _head_rows(ref):
    return jnp.concatenate([ref[0, g:g + 1, :] for g in range(GRP)], axis=1)


def _attn_fwd(q, qcol0, k, vt, mode, selneg, gate, y_prev, y_dtype, name):
    S, Sk = q.shape[0], k.shape[1]
    tq, tk = _attn_cfg(S, Sk, mode)
    R = GRP * tq

    def body(*refs):
        q_ref, k_ref, vt_ref = refs[:3]
        rest = list(refs[3:])
        sel_ref = rest.pop(0) if mode == "sel" else None
        gate_ref = rest.pop(0)
        yp_ref = rest.pop(0) if y_prev is not None else None
        o_ref, lse_ref, y_ref, m_scr, l_scr, acc, s_scr = rest
        q0 = pl.program_id(1) * tq
        qs = _scaled_queries(q_ref, tq)
        m_scr[...] = jnp.full_like(m_scr, NEG)
        l_scr[...] = jnp.zeros_like(l_scr)
        acc[...] = jnp.zeros_like(acc)
        qk = _sel_operands(qs, sel_ref) if mode == "sel" else qs

        def produce(kb, masked, slot):
            k0 = _block_start(kb, tk)
            s = _dot(k_ref[0, pl.ds(k0, tk), :], qk, "nt")
            if masked:
                s = s + _attn_bias(mode, q0, k0, tq, tk)[0]
            s_scr[slot] = s

        def consume(kb, slot):
            s = s_scr[slot]
            m_old = m_scr[...]
            m_new = jnp.maximum(m_old, jnp.max(s, axis=0, keepdims=True))
            p = jnp.exp(s - m_new)
            if mode == "cmp":
                p = p * _attn_bias(mode, q0, 0, tq, tk)[1]
            alpha = jnp.exp(m_old - m_new)
            l_scr[...] = alpha * l_scr[...] + jnp.sum(p, axis=0, keepdims=True)
            acc[...] = alpha * acc[...] + _dot(vt_ref[0, kb], p, "nn")
            m_scr[...] = m_new

        _pipelined_key_blocks(mode, q0, tq, tk, produce, consume)
        l = l_scr[...]
        good = l > 0.0
        o_t = acc[...] * jnp.where(good, 1.0 / jnp.where(good, l, 1.0), 0.0)
        lse = jnp.where(good, m_scr[...] + jnp.log(jnp.where(good, l, 1.0)), -NEG)
        y_t = o_t * _sigmoid(_head_rows(gate_ref))
        for g in range(GRP):
            hs, qs_ = slice(g * HD, (g + 1) * HD), slice(g * tq, (g + 1) * tq)
            o_ref[:, hs] = o_t[:, qs_].T
            lse_ref[0, g:g + 1, :] = lse[:, qs_]
            yg = y_t[:, qs_].T
            if y_prev is not None:
                yg = yg + yp_ref[:, hs]
            y_ref[:, hs] = yg.astype(y_ref.dtype)

    row_spec = pl.BlockSpec((1, GRP, tq), lambda h, i: (h, 0, i))
    qo_spec = pl.BlockSpec((tq, GRP * HD), lambda h, i: (i, h))
    ins = [q, k, vt]
    specs = [pl.BlockSpec((tq, GRP * HD), lambda h, i: (i, qcol0 + h)), pl.BlockSpec((1, Sk, k.shape[2]), lambda h, i: (h, 0, 0)),
             pl.BlockSpec((1, Sk // tk, HD, tk), lambda h, i: (h, 0, 0, 0))]
    if mode == "sel":
        assert tq == tk
        ins.append(selneg)
        specs.append(pl.BlockSpec((1, tq, selneg.shape[2]), lambda h, i: (h, i, 0)))
    ins.append(gate)
    specs.append(row_spec)
    if y_prev is not None:
        ins.append(y_prev)
        specs.append(qo_spec)
    return pl.pallas_call(
        body, name=name, grid=(N_KV, S // tq), in_specs=specs, out_specs=[qo_spec, row_spec, qo_spec],
        out_shape=[jax.ShapeDtypeStruct((S, ATT_WIDTH), F32), jax.ShapeDtypeStruct((N_KV, GRP, S), F32),
                   jax.ShapeDtypeStruct((S, ATT_WIDTH), y_dtype)],
        scratch_shapes=[pltpu.VMEM((1, R), F32), pltpu.VMEM((1, R), F32), pltpu.VMEM((HD, R), F32), pltpu.VMEM((2, tk, R), F32)],
        compiler_params=_cp(("parallel", "arbitrary")))(*ins)


def _attn_bwd(q, qcol0, k, kt, v, o, lse, dy, dycol0, gate, mode, selneg, name):
    S, Sk = q.shape[0], k.shape[1]
    tq, tk = _attn_cfg(S, Sk, mode)
    R = GRP * tq

    def body(*refs):
        if mode == "sel":
            (q_ref, k_ref, kt_ref, v_ref, o_ref, lse_ref, dy_ref, gate_ref, sel_ref, dq_ref, dk_ref, dv_ref, dg_ref, dq_scr, s_scr,
             dp_scr) = refs
        else:
            q_ref, k_ref, kt_ref, v_ref, o_ref, lse_ref, dy_ref, gate_ref, dq_ref, dk_ref, dv_ref, dg_ref, dq_scr, s_scr, dp_scr = refs

        @pl.when(pl.program_id(1) == 0)
        def _():
            dk_ref[...] = jnp.zeros_like(dk_ref)
            dv_ref[...] = jnp.zeros_like(dv_ref)

        q0 = pl.program_id(1) * tq
        qs = _scaled_queries(q_ref, tq)
        dys = _stack_heads(dy_ref, tq)
        gv = _sigmoid(_head_rows(gate_ref))
        dy_o = _dot(jnp.ones((8, HD), F32), dys * _stack_heads(o_ref, tq), "nt", split="b")[0:1, :]
        delta = gv * dy_o
        dgate = dy_o * (gv * (1.0 - gv))
        for g in range(GRP):
            dg_ref[0, g:g + 1, :] = dgate[:, g * tq:(g + 1) * tq]
        lsev = _head_rows(lse_ref)
        dos = (dys * jnp.broadcast_to(gv, (8, R)).T[:, 0:1]).astype(_MXU)
        dq_scr[...] = jnp.zeros_like(dq_scr)
        qk = _sel_operands(qs, sel_ref) if mode == "sel" else qs

        def produce(kb, masked, slot):
            k0 = _block_start(kb, tk)
            s = _dot(k_ref[0, pl.ds(k0, tk), :], qk, "nt")
            if masked:
                s = s + _attn_bias(mode, q0, k0, tq, tk)[0]
            s_scr[slot] = s
            dp_scr[slot] = _dot(v_ref[0, pl.ds(k0, tk), :], dos, "nt")

        def consume(kb, slot):
            k0 = _block_start(kb, tk)
            p = jnp.exp(s_scr[slot] - lsev)
            if mode == "cmp":
                p = p * _attn_bias(mode, q0, 0, tq, tk)[1]
            ds = p * (dp_scr[slot] - delta)
            dq_scr[...] += _dot(kt_ref[0, kb], ds, "nn")
            dk_ref[0, pl.ds(k0, tk), :] += _dot(ds, qs, "nn")
            dv_ref[0, pl.ds(k0, tk), :] += _dot(p, dos, "nn")

        _pipelined_key_blocks(mode, q0, tq, tk, produce, consume)
        for g in range(GRP):
            dq_ref[:, g * HD:(g + 1) * HD] = (dq_scr[:, g * tq:(g + 1) * tq] * SCALE).T

    kv_spec = pl.BlockSpec((1, Sk, HD), lambda h, i: (h, 0, 0))
    qo_spec = pl.BlockSpec((tq, GRP * HD), lambda h, i: (i, h))
    row_spec = pl.BlockSpec((1, GRP, tq), lambda h, i: (h, 0, i))
    ins = [q, k, kt, v, o, lse, dy, gate]
    specs = [pl.BlockSpec((tq, GRP * HD), lambda h, i: (i, qcol0 + h)), pl.BlockSpec((1, Sk, k.shape[2]), lambda h, i: (h, 0, 0)),
             pl.BlockSpec((1, Sk // tk, HD, tk), lambda h, i: (h, 0, 0, 0)), kv_spec, qo_spec, row_spec,
             pl.BlockSpec((tq, GRP * HD), lambda h, i: (i, dycol0 + h)), row_spec]
    if mode == "sel":
        assert tq == tk
        ins.append(selneg)
        specs.append(pl.BlockSpec((1, tq, selneg.shape[2]), lambda h, i: (h, i, 0)))
    return pl.pallas_call(
        body, name=name, grid=(N_KV, S // tq), in_specs=specs, out_specs=[qo_spec, kv_spec, kv_spec, row_spec],
        out_shape=[jax.ShapeDtypeStruct((S, ATT_WIDTH), F32), jax.ShapeDtypeStruct((N_KV, Sk, HD), F32),
                   jax.ShapeDtypeStruct((N_KV, Sk, HD), F32), jax.ShapeDtypeStruct((N_KV, GRP, S), F32)],
        scratch_shapes=[pltpu.VMEM((HD, R), F32), pltpu.VMEM((2, tk, R), F32), pltpu.VMEM((2, tk, R), F32)],
        compiler_params=_cp(("parallel", "arbitrary")))(*ins)


def _select(q, qcol0, k_cmp, lse):
    S, NC = q.shape[0], k_cmp.shape[1]
    NB = S // SEL_BLOCK
    tq = _attn_cfg(S, NC, "cmp")[0]
    ci = np.arange(NC)[None, :] * 16
    sj = np.arange(NB)[:, None] * SEL_BLOCK
    ov_t = np.clip(np.minimum(ci + 32, sj + SEL_BLOCK) - np.maximum(ci, sj), 0, None) / 32.0
    ov_t[:, NC - 1] = 0.0
    ov_t = jnp.asarray(ov_t, F32)

    def body(q_ref, k_ref, lse_ref, ov_ref, sel_ref):
        q0 = pl.program_id(1) * tq
        bias, okf = _attn_bias("cmp", q0, 0, tq, NC)
        lsev = _head_rows(lse_ref)
        p = jnp.exp(_dot(k_ref[0], _scaled_queries(q_ref, tq), "nt") + bias - lsev) * okf
        imp4 = _dot(ov_ref[...], p, "nn")
        imp = imp4[:, 0:tq] + imp4[:, tq:2 * tq] + imp4[:, 2 * tq:3 * tq] + imp4[:, 3 * tq:4 * tq]
        blk = lax.broadcasted_iota(jnp.int32, (NB, tq), 0)
        cur = lax.shift_right_logical(q0 + lax.broadcasted_iota(jnp.int32, (NB, tq), 1), 6)
        imp = jnp.where((blk == 0) | (blk == cur) | (blk == cur - 1), FORCE, imp)
        imp = jnp.where(blk <= cur, imp, -1.0)
        rank = jnp.zeros((NB, tq), F32)
        for j in range(NB):
            row = imp[j:j + 1, :]
            ahead = (row > imp) | ((row == imp) & (blk > j))
            rank = rank + ahead.astype(F32)
        chosen = (rank < float(N_SELECT)) & (imp >= 0.0)
        sel_ref[0] = jnp.where(chosen, 0.0, NEG).T.astype(sel_ref.dtype)

    return pl.pallas_call(
        body, name="select_blocks", grid=(N_KV, S // tq),
        in_specs=[pl.BlockSpec((tq, GRP * HD), lambda h, i: (i, qcol0 + h)), pl.BlockSpec((1, NC, HD), lambda h, i: (h, 0, 0)),
                  pl.BlockSpec((1, GRP, tq), lambda h, i: (h, 0, i)), pl.BlockSpec((NB, NC), lambda h, i: (0, 0))],
        out_specs=pl.BlockSpec((1, tq, NB), lambda h, i: (h, i, 0)),
        out_shape=jax.ShapeDtypeStruct((N_KV, S, NB), _MXU), compiler_params=_cp(("parallel", "parallel")))(q, k_cmp, lse, ov_t)


def _to_rows16(x):
    S = x.shape[0]
    return x.reshape(S // 16, 16, N_KV, HD).transpose(2, 0, 1, 3).reshape(N_KV, S // 16, 16 * HD)


def _from_rows16(r):
    NC = r.shape[1]
    return r.reshape(N_KV, NC, 16, HD).transpose(1, 2, 0, 3).reshape(NC * 16, N_KV * HD)


DT_COL0 = SSD_WIDTH + CONV_CH
GATE_IN_COL0 = D_IN - 3 * N_HEADS


SHARD_IN = D_IN // N_DEV


def _orig_cols(ref, c0, width):
    pieces, c = [], c0
    while c < c0 + width:
        d, off = divmod(c, SHARD_IN)
        w = min(SHARD_IN - off, c0 + width - c)
        pieces.append(ref[d, :, off:off + w])
        c += w
    return pieces[0] if len(pieces) == 1 else jnp.concatenate(pieces, axis=1)


def _cols_from_slabs(slabs):
    _, R, c = slabs.shape
    tr = _pick(R, (256, 128))

    def body(s_ref, o_ref):
        for t in range(N_DEV * c // LANE):
            pieces, col = [], t * LANE
            while col < (t + 1) * LANE:
                d, off = divmod(col, c)
                w = min(c - off, (t + 1) * LANE - col)
                pieces.append(s_ref[d, :, off:off + w])
                col += w
            o_ref[:, t * LANE:(t + 1) * LANE] = pieces[0] if len(pieces) == 1 else jnp.concatenate(pieces, axis=1)

    return pl.pallas_call(
        body, name="cols_from_slabs", grid=(R // tr,), in_specs=[pl.BlockSpec((N_DEV, tr, c), lambda i: (0, i, 0))],
        out_specs=pl.BlockSpec((tr, N_DEV * c), lambda i: (i, 0)), out_shape=jax.ShapeDtypeStruct((R, N_DEV * c), slabs.dtype),
        compiler_params=_cp(("parallel",)))(slabs)


def _slabs_from_cols(x):
    R, c = x.shape[0], x.shape[1] // N_DEV
    tr = _pick(R, (256, 128))

    def body(x_ref, o_ref):
        for d in range(N_DEV):
            o_ref[d] = x_ref[:, d * c:(d + 1) * c]

    return pl.pallas_call(
        body, name="slabs_from_cols", grid=(R // tr,), in_specs=[pl.BlockSpec((tr, N_DEV * c), lambda i: (i, 0))],
        out_specs=pl.BlockSpec((N_DEV, tr, c), lambda i: (0, i, 0)), out_shape=jax.ShapeDtypeStruct((N_DEV, R, c), x.dtype),
        compiler_params=_cp(("parallel",)))(x)


def _w_in_from_slabs(slabs):
    D = slabs.shape[1]
    tr = _pick(D, (256, 128))

    def body(s_ref, main_ref, small_ref):
        for t in range(W_MAIN // LANE):
            c = t * LANE
            main_ref[:, c:c + LANE] = _orig_cols(s_ref, c if c < DT_COL0 else c + SSD_HEADS, LANE)
        small_ref[...] = jnp.concatenate(
            [_orig_cols(s_ref, DT_COL0, SSD_HEADS), _orig_cols(s_ref, GATE_IN_COL0, 3 * N_HEADS),
             jnp.zeros((tr, W_SMALL - SSD_HEADS - 3 * N_HEADS), small_ref.dtype)], axis=1)

    return pl.pallas_call(
        body, name="w_in_layout", grid=(D // tr,), in_specs=[pl.BlockSpec((N_DEV, tr, SHARD_IN), lambda i: (0, i, 0))],
        out_specs=[pl.BlockSpec((tr, W_MAIN), lambda i: (i, 0)), pl.BlockSpec((tr, W_SMALL), lambda i: (i, 0))],
        out_shape=[jax.ShapeDtypeStruct((D, W_MAIN), slabs.dtype), jax.ShapeDtypeStruct((D, W_SMALL), slabs.dtype)],
        compiler_params=_cp(("parallel",)))(slabs)


def _w_in_to_slabs(main, small):
    D = main.shape[0]
    tr = _pick(D, (256, 128))
    ranges = [(0, DT_COL0, 0, 0), (DT_COL0, DT_COL0 + SSD_HEADS, 1, 0), (DT_COL0 + SSD_HEADS, GATE_IN_COL0, 0, DT_COL0),
              (GATE_IN_COL0, D_IN, 1, SSD_HEADS)]

    def body(main_ref, small_ref, o_ref):
        srcs = (main_ref, small_ref)
        for d in range(N_DEV):
            lo, hi = d * SHARD_IN, (d + 1) * SHARD_IN
            pieces = []
            for start, stop, which, s0 in ranges:
                a, b = max(lo, start), min(hi, stop)
                if a < b:
                    pieces.append(srcs[which][:, s0 + a - start:s0 + b - start].astype(o_ref.dtype))
            o_ref[d] = pieces[0] if len(pieces) == 1 else jnp.concatenate(pieces, axis=1)

    return pl.pallas_call(
        body, name="w_in_grad_layout", grid=(D // tr,),
        in_specs=[pl.BlockSpec((tr, W_MAIN), lambda i: (i, 0)), pl.BlockSpec((tr, W_SMALL), lambda i: (i, 0))],
        out_specs=pl.BlockSpec((N_DEV, tr, SHARD_IN), lambda i: (0, i, 0)),
        out_shape=jax.ShapeDtypeStruct((N_DEV, D, SHARD_IN), main.dtype), compiler_params=_cp(("parallel",)))(main, small)


QB, KCB, VCB, KSB, VSB, KWB, VWB = 10, 14, 15, 16, 17, 18, 19


def _col256(a, b):
    return a[:, b * 256:(b + 1) * 256]


_EARLY = ["w_in", "cmp_w1_k", "cmp_w1_v"]
_LATE = ["w_out", "w_gate", "w_up", "w_down"]
_FFN = ["w_down", "w_gate", "w_up"]
_MID = ["w_out"]
_LAST = ["cmp_w1_k", "cmp_w1_v", "w_in"]


def _local_step(x, tgt, p, late_weights=None, grads_ready=None):
    S = x.shape[0]
    cos, sin = _rope_tables(S)

    u, rs1 = _rms_fwd(x, p["attn_norm_w"], "attn_norm")
    proj = _mm(u, p["w_main"], "nn", F32, "in_proj", after=p.get("before_in_proj"))
    proj_small = _mm(u, p["w_small"], "nn", F32, "in_proj_small")
    xa = _conv_fwd(proj, p["conv_w"], p["conv_b"])
    y_ssd, y_pre, rs_ssd, hs = _ssd_fwd(proj, proj_small, xa, p["dt_bias"], p["a_log"], p["d_skip"], p["ssd_norm_w"])

    q_rot = _rope([proj], QB, ATT_WIDTH, cos, sin, 1.0, _MXU, "rope_q")
    kv = _kv_prep(proj, cos, sin, _attn_cfg(S, S, "sel")[1])
    rk, rv = _to_rows16(_col256(proj, KCB)), _to_rows16(_col256(proj, VCB))
    k_cmp, hid_k = _compress_fwd(rk, p["cmp_pe_k"], p["cmp_w1_k"], p["cmp_w2_k"])
    v_cmp, hid_v = _compress_fwd(rv, p["cmp_pe_v"], p["cmp_w1_v"], p["cmp_w2_v"])
    n_cmp = k_cmp.shape[1]

    gates = proj_small[:, SSD_HEADS:SSD_HEADS + 3 * N_HEADS].reshape(S, N_KV, GRP, 3).transpose(3, 1, 2, 0)
    o_cmp, lse_cmp, y_att = _attn_fwd(proj, QB, k_cmp, _blocked_t(v_cmp, n_cmp), "cmp", None, gates[0], None, F32, "attn_cmp_fwd")
    sel = _select(proj, QB, k_cmp, lse_cmp)
    o_sel, lse_sel, y_att = _attn_fwd(q_rot, 0, kv["ks_ext"], kv["vs_t"], "sel", sel, gates[1], y_att, F32, "attn_sel_fwd")
    o_win, lse_win, y_att = _attn_fwd(q_rot, 0, kv["kw"], kv["vw_t"], "win", None, gates[2], y_att, _MXU, "attn_win_fwd")

    if late_weights is not None:
        p = {**p, **late_weights(y_att)}
    mixed = jnp.concatenate([y_ssd, y_att], axis=1)
    h1 = _mm(mixed, p["w_out"], "nn", F32, "out_proj", res=x)
    v, rs_ffn = _rms_fwd(h1, p["ffn_norm_w"], "ffn_norm")
    gt, up, act = _ffn_up(v, p["w_gate"], p["w_up"])
    h2 = _mm(act, p["w_down"], "nn", F32, "ffn_down", res=h1)
    loss, dh2, dh2b, d_final_w = _final_loss(h2, p["final_norm_w"], tgt)

    def ready(names):
        return None if grads_ready is None else grads_ready(names, g)

    g = {"final_norm_w": d_final_w}
    g["w_down"] = _mm(act, dh2b, "tn", _MXU, "dw_down")
    dgt, dup = _ffn_dact(dh2b, p["w_down"], gt, up)
    g["w_gate"] = _mm(v, dgt, "tn", _MXU, "dw_gate")
    g["w_up"] = _mm(v, dup, "tn", _MXU, "dw_up")
    dv = _ffn_dv(dgt, dup, p["w_gate"], p["w_up"], ready(_FFN))
    dh1, dh1b, g["ffn_norm_w"] = _rms_bwd(dv, h1, rs_ffn, p["ffn_norm_w"], dh2, "ffn_norm_bwd")
    g["w_out"] = _mm(mixed, dh1b, "tn", _MXU, "dw_out")
    dmixed = _mm(dh1b, p["w_out"], "nt", F32, "dmixed", after=ready(_MID))

    dz, dxa, ddtr, g["dt_bias"], g["a_log"], g["d_skip"], g["ssd_norm_w"] = _ssd_bwd(
        dmixed, proj, proj_small, xa, y_pre, rs_ssd, hs, p["dt_bias"], p["a_log"], p["d_skip"], p["ssd_norm_w"])
    dxbc, g["conv_w"], g["conv_b"] = _conv_bwd(proj, p["conv_w"], p["conv_b"], dxa)

    dyb = SSD_WIDTH // (GRP * HD)
    dq_cmp, dk_cmp, dv_cmp, dg_cmp = _attn_bwd(proj, QB, k_cmp, _blocked_t(k_cmp, n_cmp), v_cmp, o_cmp, lse_cmp, dmixed, dyb,
                                               gates[0], "cmp", None, "attn_cmp_bwd")
    dq_sel, dks, dvs, dg_sel = _attn_bwd(q_rot, 0, kv["ks_ext"], kv["ks_t"], kv["vs"], o_sel, lse_sel, dmixed, dyb, gates[1], "sel",
                                         sel, "attn_sel_bwd")
    dq_win, dkw, dvw, dg_win = _attn_bwd(q_rot, 0, kv["kw"], kv["kw_t"], kv["vw"], o_win, lse_win, dmixed, dyb, gates[2], "win", None,
                                         "attn_win_bwd")
    dgate = jnp.stack([dg_cmp, dg_sel, dg_win]).transpose(3, 1, 2, 0).reshape(S, 3 * N_HEADS)
    drk, g["cmp_w1_k"], g["cmp_w2_k"], g["cmp_pe_k"] = _compress_bwd(rk, p["cmp_pe_k"], p["cmp_w1_k"], p["cmp_w2_k"], hid_k, dk_cmp)
    drv, g["cmp_w1_v"], g["cmp_w2_v"], g["cmp_pe_v"] = _compress_bwd(rv, p["cmp_pe_v"], p["cmp_w1_v"], p["cmp_w2_v"], hid_v, dv_cmp)
    dq = _rope([dq_sel, dq_win], 0, ATT_WIDTH, cos, sin, -1.0, _MXU, "rope_dq", extra=(dq_cmp, 0))
    dkv = _dkv_post(dks, dvs, dkw, dvw, cos, sin)
    dproj = jnp.concatenate([dz, dxbc, dq] + [t.astype(_MXU) for t in (_from_rows16(drk), _from_rows16(drv))] + [dkv], axis=1)
    dsmall = jnp.concatenate([ddtr, dgate, jnp.zeros((S, W_SMALL - SSD_HEADS - 3 * N_HEADS), F32)], axis=1).astype(_MXU)
    g["w_main"] = _mm(u, dproj, "tn", _MXU, "dw_in")
    g["w_small"] = _mm(u, dsmall, "tn", F32, "dw_in_small")
    du = _mm(dproj, p["w_main"], "nt", F32, "du_main", after=ready(_LAST))
    du = _mm(dsmall, p["w_small"], "nt", F32, "du_small", res=du)
    grad_x, _, g["attn_norm_w"] = _rms_bwd(du, x, rs1, p["attn_norm_w"], dh1, "attn_norm_bwd")
    return loss, grad_x, g


MESH_ID = pl.DeviceIdType.MESH


def _my_coords():
    return lax.axis_index("x"), lax.axis_index("y"), lax.axis_index("c")


def _flat_id(px, py, pc):
    return 4 * px + 2 * py + pc


def _peer(k):
    mx, my, mc = _my_coords()
    return (1 - mx if k & 4 else mx, 1 - my if k & 2 else my, 1 - mc if k & 1 else mc)


def _exchange(arrs, scatter, name, after=()):
    n, na = len(arrs), len(after)
    scatter = [scatter] * n if isinstance(scatter, bool) else list(scatter)

    def body(*refs):
        ins, outs = refs[:n], refs[n + na:2 * n + na]
        send_sems, recv_sems, local_sems = refs[2 * n + na:]
        me = _flat_id(*_my_coords())
        copies = []
        for i in range(n):
            src_me = ins[i].at[me] if scatter[i] else ins[i]
            local = pltpu.make_async_copy(src_me, outs[i].at[me], local_sems.at[i])
            local.start()
            copies.append(local)
        for k in range(1, N_DEV):
            peer = _peer(k)
            for i in range(n):
                src = ins[i].at[_flat_id(*peer)] if scatter[i] else ins[i]
                cp = pltpu.make_async_remote_copy(src_ref=src, dst_ref=outs[i].at[me], send_sem=send_sems.at[i * 7 + k - 1],
                                                  recv_sem=recv_sems.at[i * 7 + k - 1], device_id=peer, device_id_type=MESH_ID)
                cp.start()
                copies.append(cp)
        for cp in copies:
            cp.wait()

    any_spec = pl.BlockSpec(memory_space=pl.ANY)
    out_shape = [jax.ShapeDtypeStruct(a.shape if sc else (N_DEV,) + a.shape, a.dtype) for a, sc in zip(arrs, scatter)]
    return pl.pallas_call(
        body, name=name, in_specs=[any_spec] * (n + na), out_specs=[any_spec] * n, out_shape=out_shape,
        scratch_shapes=[pltpu.SemaphoreType.DMA((n * 7,)), pltpu.SemaphoreType.DMA((n * 7,)), pltpu.SemaphoreType.DMA((n,))],
        compiler_params=pltpu.CompilerParams(has_side_effects=True))(*arrs, *after)


def _gather_two_level(arrs, name):
    n = len(arrs)

    def body(*refs):
        ins, outs = refs[:n], refs[n:2 * n]
        send_sems, recv_sems, local_sems = refs[2 * n:]
        x, y, c = _my_coords()
        me, sibling = (x, y, c), (x, y, 1 - c)
        chips = [(1 - x, y), (x, 1 - y), (1 - x, 1 - y)]

        def copy(i, k, block, to, src=None):
            slot = outs[i].at[_flat_id(*block)]
            return pltpu.make_async_remote_copy(src_ref=slot if src is None else src, dst_ref=slot, send_sem=send_sems.at[i * 7 + k],
                                                recv_sem=recv_sems.at[i * 7 + k], device_id=to, device_id_type=MESH_ID)

        mine = [pltpu.make_async_copy(ins[i], outs[i].at[_flat_id(*me)], local_sems.at[i]) for i in range(n)]
        for cp in mine:
            cp.start()
        first = []
        for j, chip in enumerate(chips):
            first += [copy(i, 1 + j, me, (*chip, c), src=ins[i]) for i in range(n)]
        first += [copy(i, 0, me, sibling, src=ins[i]) for i in range(n)]
        for cp in first:
            cp.start()
        passed = []
        for j, chip in enumerate(chips):
            for i in range(n):
                copy(i, 1 + j, (*chip, c), me).wait_recv()
                passed.append(copy(i, 4 + j, (*chip, c), sibling))
                passed[-1].start()
        for i in range(n):
            copy(i, 0, sibling, me).wait_recv()
        for j, chip in enumerate(chips):
            for i in range(n):
                copy(i, 4 + j, (*chip, 1 - c), me).wait_recv()
        for cp in first + passed:
            cp.wait_send()
        for cp in mine:
            cp.wait()

    any_spec = pl.BlockSpec(memory_space=pl.ANY)
    return pl.pallas_call(
        body, name=name, in_specs=[any_spec] * n, out_specs=[any_spec] * n,
        out_shape=[jax.ShapeDtypeStruct((N_DEV,) + a.shape, a.dtype) for a in arrs],
        scratch_shapes=[pltpu.SemaphoreType.DMA((n * 7,)), pltpu.SemaphoreType.DMA((n * 7,)), pltpu.SemaphoreType.DMA((n,))],
        compiler_params=pltpu.CompilerParams(has_side_effects=True))(*arrs)


_HBM = pl.BlockSpec(memory_space=pltpu.HBM)
_SEM = pl.BlockSpec(memory_space=pltpu.SEMAPHORE)
_EFFECT = pltpu.SideEffectType.DATAFLOW_SIDE_EFFECTING


def _split_copies(ins, lands, send_sems, recv_sems, scatter):
    me = _flat_id(*_my_coords())
    out = []
    for k in range(1, N_DEV):
        peer = _peer(k)
        for i in range(len(ins)):
            src = ins[i].at[_flat_id(*peer)] if scatter else ins[i]
            out.append(pltpu.make_async_remote_copy(src_ref=src, dst_ref=lands[i].at[me], send_sem=send_sems.at[i * 7 + k - 1],
                                                    recv_sem=recv_sems.at[i * 7 + k - 1], device_id=peer, device_id_type=MESH_ID))
    return out


def _split_start(arrs, scatter, name, after=()):
    n, na = len(arrs), len(after)

    def body(*refs):
        for cp in _split_copies(refs[:n], refs[n:2 * n], refs[2 * n + na], refs[2 * n + na + 1], scatter):
            cp.start()
        refs[-1][...] = jnp.zeros_like(refs[-1])

    land_shapes = [a.shape if scatter else (N_DEV,) + a.shape for a in arrs]
    out_shape = ((pltpu.SemaphoreType.DMA((n * 7,)), pltpu.SemaphoreType.DMA((n * 7,)))
                 + tuple(pltpu.HBM(a.shape, a.dtype) for a in arrs) + tuple(pltpu.HBM(s, a.dtype) for s, a in zip(land_shapes, arrs))
                 + (jax.ShapeDtypeStruct((8, 128), F32),))
    operands = ([pltpu.with_memory_space_constraint(a, pltpu.HBM) for a in arrs]
                + [pltpu.with_memory_space_constraint(lax.empty(s, a.dtype), pltpu.HBM) for s, a in zip(land_shapes, arrs)])
    res = pl.pallas_call(
        body, name=name, out_shape=out_shape, in_specs=[_HBM] * (2 * n) + [pl.BlockSpec(memory_space=pl.ANY)] * na,
        out_specs=(_SEM, _SEM) + (_HBM,) * (2 * n) + (pl.BlockSpec(memory_space=pltpu.VMEM),),
        input_output_aliases={i: 2 + i for i in range(2 * n)},
        compiler_params=pltpu.CompilerParams(has_side_effects=_EFFECT))(*operands, *after)
    return dict(send=res[0], recv=res[1], ins=list(res[2:2 + n]), lands=list(res[2 + n:2 + 2 * n]), token=res[-1])


def _split_wait(st, scatter, after, name):
    n = len(st["ins"])

    def body(*refs):
        for cp in _split_copies(refs[:n], refs[n:2 * n], refs[2 * n], refs[2 * n + 1], scatter):
            cp.wait_send()
            cp.wait_recv()

    arrs = st["ins"] + st["lands"]
    res = pl.pallas_call(
        body, name=name, out_shape=tuple(pltpu.HBM(a.shape, a.dtype) for a in arrs),
        in_specs=[_HBM] * (2 * n) + [_SEM, _SEM] + [pl.BlockSpec(memory_space=pl.ANY)] * len(after), out_specs=(_HBM,) * (2 * n),
        input_output_aliases={i: i for i in range(2 * n)},
        compiler_params=pltpu.CompilerParams(has_side_effects=_EFFECT))(*arrs, st["send"], st["recv"], *after)
    me = _flat_id(*_my_coords())
    out = []
    for src, land in zip(res[:n], res[n:]):
        own = lax.dynamic_index_in_dim(src, me, 0, keepdims=True) if scatter else src[None]
        out.append(lax.dynamic_update_slice_in_dim(land, own, me, 0))
    return out


def _adam_step(p_ref, w_ref, m_ref, v_ref, g_ref, d_ref, nm_ref, nv_ref):
    g = p_ref[0].astype(F32)
    for j in range(1, p_ref.shape[0]):
        g = g + p_ref[j].astype(F32)
    g_ref[...] = g
    nm = ADAM_B1 * m_ref[...] + (1.0 - ADAM_B1) * g
    nv = ADAM_B2 * v_ref[...] + (1.0 - ADAM_B2) * (g * g)
    nm_ref[...] = nm
    nv_ref[...] = nv
    m_hat = nm / (1.0 - ADAM_B1 ** ADAM_STEP)
    v_hat = nv / (1.0 - ADAM_B2 ** ADAM_STEP)
    d_ref[...] = -ADAM_LR * (m_hat / (jnp.sqrt(v_hat) + ADAM_EPS) + ADAM_WD * w_ref[...])


def _adam_sum(parts, w, m, v, name):
    P, R, C = parts.shape
    tr = _pick(R, (256, 128, 64, 32, 8)) if C <= 1024 else _pick(R, (128, 64, 32, 8))
    blk = pl.BlockSpec((tr, C), lambda i: (i, 0))
    return pl.pallas_call(
        functools.partial(_adam_step), name=name, grid=(R // tr,),
        in_specs=[pl.BlockSpec((P, tr, C), lambda i: (0, i, 0)), blk, blk, blk],
        out_specs=[blk] * 4, out_shape=[jax.ShapeDtypeStruct((R, C), F32)] * 4, compiler_params=_cp(("parallel",)))(parts, w, m, v)


def _adam_small(loss_parts, parts, ws, ms, vs):
    n = len(parts)

    def body(*refs):
        loss_ref, ins, outs, total_ref = refs[0], refs[1:4 * n + 1], refs[4 * n + 1:-1], refs[-1]
        for i in range(n):
            _adam_step(ins[i], ins[n + i], ins[2 * n + i], ins[3 * n + i], *outs[4 * i:4 * i + 4])
        total = loss_ref[0]
        for d in range(1, N_DEV):
            total = total + loss_ref[d]
        total_ref[...] = total

    out_shape = [jax.ShapeDtypeStruct(w.shape, F32) for w in ws for _ in range(4)] + [jax.ShapeDtypeStruct(loss_parts.shape[1:], F32)]
    res = pl.pallas_call(body, name="adam_small", out_shape=out_shape)(loss_parts, *parts, *ws, *ms, *vs)
    return res[-1], [tuple(res[4 * i:4 * i + 4]) for i in range(n)]


_WEIGHTS = ["attn_norm_w", "w_in", "conv_w", "conv_b", "dt_bias", "a_log", "d_skip", "ssd_norm_w", "cmp_w1_k", "cmp_w2_k",
            "cmp_w1_v", "cmp_w2_v", "cmp_pe_k", "cmp_pe_v", "w_out", "ffn_norm_w", "w_gate", "w_up", "w_down", "final_norm_w"]
_BIG = ["w_in", "w_gate", "w_up", "w_down", "w_out", "cmp_w1_k", "cmp_w1_v"]
_COL_SHARDED = ("w_in", "w_gate", "w_up")
_REPLICATED = ["attn_norm_w", "conv_b", "dt_bias", "a_log", "d_skip", "ssd_norm_w", "cmp_pe_k", "cmp_pe_v", "ffn_norm_w",
               "final_norm_w"]
_SMALL_SHARDED = ["conv_w", "cmp_w2_k", "cmp_w2_v"]


def _cols_to_slabs(g):
    R = g.shape[0]
    return g.reshape(R, N_DEV, -1).transpose(1, 0, 2)


def _slabs_to_cols(s):
    return s.transpose(1, 0, 2).reshape(s.shape[1], -1)


def kernel(x, attn_norm_w, w_in, conv_w, conv_b, dt_bias, a_log, d_skip, ssd_norm_w, cmp_w1_k, cmp_w2_k, cmp_w1_v, cmp_w2_v, cmp_pe_k, cmp_pe_v, w_out, ffn_norm_w, w_gate, w_up, w_down, final_norm_w, loss_target, m_attn_norm_w, m_w_in, m_conv_w, m_conv_b, m_dt_bias, m_a_log, m_d_skip, m_ssd_norm_w, m_cmp_w1_k, m_cmp_w2_k, m_cmp_w1_v, m_cmp_w2_v, m_cmp_pe_k, m_cmp_pe_v, m_w_out, m_ffn_norm_w, m_w_gate, m_w_up, m_w_down, m_final_norm_w, v_attn_norm_w, v_w_in, v_conv_w, v_conv_b, v_dt_bias, v_a_log, v_d_skip, v_ssd_norm_w, v_cmp_w1_k, v_cmp_w2_k, v_cmp_w1_v, v_cmp_w2_v, v_cmp_pe_k, v_cmp_pe_v, v_w_out, v_ffn_norm_w, v_w_gate, v_w_up, v_w_down, v_final_norm_w):
    a = dict(locals())

    shard = {n: a[n][0].astype(_MXU) for n in _BIG}
    got = _gather_two_level([shard[n] for n in _EARLY] + [cmp_w2_k[0], cmp_w2_v[0], conv_w[0]], "gather_early")
    st_late = _split_start([shard[n] for n in _LATE], False, "gather_late_start", after=(got[0],))

    def assemble(n, t):
        return _cols_from_slabs(t) if n in _COL_SHARDED else t.reshape(-1, t.shape[-1])

    p = dict(attn_norm_w=attn_norm_w, conv_b=conv_b, dt_bias=dt_bias, a_log=a_log, d_skip=d_skip, ssd_norm_w=ssd_norm_w,
             cmp_pe_k=cmp_pe_k.reshape(1, -1), cmp_pe_v=cmp_pe_v.reshape(1, -1), ffn_norm_w=ffn_norm_w,
             final_norm_w=final_norm_w.reshape(1, -1))

    w_main, w_small = _w_in_from_slabs(got[0])
    p.update(before_in_proj=st_late["token"],
             w_main=w_main, w_small=w_small, cmp_w1_k=assemble("cmp_w1_k", got[1]), cmp_w1_v=assemble("cmp_w1_v", got[2]),
             cmp_w2_k=assemble("cmp_w2_k", got[3]).astype(_MXU), cmp_w2_v=assemble("cmp_w2_v", got[4]).astype(_MXU),
             conv_w=_slabs_to_cols(got[5]))

    def late_weights(after):
        got_late = _split_wait(st_late, False, (after,), "gather_late_wait")
        return {n: assemble(n, t) for n, t in zip(_LATE, got_late)}

    def slabs_of(g, n):
        if n == "w_in":
            return _w_in_to_slabs(g["w_main"], g["w_small"])
        return _slabs_from_cols(g[n]) if n in _COL_SHARDED else g[n].reshape(N_DEV, -1, g[n].shape[-1])

    started = []

    def grads_ready(names, g):
        started.append((names, _split_start([slabs_of(g, n) for n in names], True, "scatter_grads_start_%d" % len(started))))
        return started[-1][1]["token"]

    loss_part, grad_x, g = _local_step(x[0], loss_target[0], p, late_weights, grads_ready)

    out, after = {}, (started[-1][1]["token"],)
    for i, (names, st) in enumerate(started):
        if i == len(started) - 1:
            after = after + (grad_x,)
        received = _split_wait(st, True, after, "scatter_grads_wait_%d" % i)
        for n, parts in zip(names, received):
            out[n] = _adam_sum(parts, a[n][0], a["m_" + n][0], a["v_" + n][0], "adam_" + n)
        after = (out[names[-1]][0],)

    small_names = _REPLICATED + _SMALL_SHARDED
    partials = [g[n] for n in _REPLICATED] + [_cols_to_slabs(g["conv_w"])] + [
        g[n].reshape(N_DEV, -1, g[n].shape[-1]) for n in ("cmp_w2_k", "cmp_w2_v")]
    gathered = _exchange([loss_part] + partials, [False] * (1 + len(_REPLICATED)) + [True] * len(_SMALL_SHARDED),
                         "exchange_small_grads", after=(received[0],))
    shapes2d = [t.shape[1:] for t in gathered[1:]]
    loss, res_small = _adam_small(gathered[0], gathered[1:],
                                  *[[a[pre + n].reshape(s) for n, s in zip(small_names, shapes2d)] for pre in ("", "m_", "v_")])
    for n, r in zip(small_names, res_small):
        out[n] = r

    outs = [loss[0, 0], grad_x[None]]
    for j in range(4):
        for n in _WEIGHTS:
            outs.append(out[n][j].reshape(a[n].shape))
    return tuple(outs)
```

```python
import functools

import numpy as np
import jax
import jax.numpy as jnp
from jax import lax
from jax.experimental import pallas as pl
from jax.experimental.pallas import tpu as pltpu

F32 = jnp.float32
_MXU = jnp.bfloat16

N_DEV = 8
D_MODEL = 2048
SSD_WIDTH = 1024
ATT_WIDTH = 1024
SSD_HEADS = 16
SSD_P = 64
SSD_N = 128
SSD_L = 128
SSD_G = 2
CONV_CH = 1536
CONV_K = 4
HD = 64
N_HEADS = 16
N_KV = 4
GRP = 4
CMP_HID = 256
SEL_BLOCK = 64
N_SELECT = 16
WINDOW = 512
ROPE_DIM = 16
ROPE_THETA = 500000.0
D_FF = 5632
EPS = 1e-6
NEG = -1e30
FORCE = 1e4
SCALE = HD ** -0.5
D_IN = 5184
W_MAIN = 5120
W_SMALL = 128
VMEM_LIMIT = 52 * 1024 * 1024

ADAM_LR, ADAM_B1, ADAM_B2, ADAM_EPS, ADAM_WD, ADAM_STEP = 0.001, 0.9, 0.999, 1e-08, 0.01, 10


def _pick(n, cands):
    for c in cands:
        if n % c == 0:
            return c
    return n


def _cp(sem=None):
    return pltpu.CompilerParams(dimension_semantics=sem, vmem_limit_bytes=VMEM_LIMIT)


def _sigmoid(x):
    return 1.0 / (1.0 + jnp.exp(-x))


def _dot(a, b, dims, split=None):
    dn = {"nn": (((1,), (0,)), ((), ())), "nt": (((1,), (1,)), ((), ())), "tn": (((0,), (0,)), ((), ()))}[dims]
    mm = lambda x, y: lax.dot_general(x.astype(_MXU), y.astype(_MXU), dn, preferred_element_type=F32)
    if split is None:
        return mm(a, b)
    x = (a if split == "a" else b).astype(F32)
    hi = x.astype(_MXU)
    lo = x - hi.astype(F32)
    return mm(hi, b) + mm(lo, b) if split == "a" else mm(a, hi) + mm(a, lo)


LANE = 128
MM_TILE = 1024
MM_K_WHOLE = 2048
MM_K_STEP = 2816
TN_ACC_ELEMS = 3 * 2 ** 20
TN_K_STEP = 512


def _largest_tile(n, cap):
    if n <= cap:
        return n
    best = LANE
    for t in range(LANE, cap + 1, LANE):
        if n % t == 0:
            best = t
    return best


def _mm_tiles(mode, M, N, K):
    if mode == "tn":
        tm = _largest_tile(M, 2 * MM_TILE)
        return tm, _largest_tile(N, TN_ACC_ELEMS // tm), _largest_tile(K, TN_K_STEP)
    tk = K if K <= MM_K_WHOLE else _largest_tile(K, MM_K_STEP)
    return _largest_tile(M, MM_TILE), _largest_tile(N, MM_TILE), tk


def _mm(a, b, mode, out_dtype, name, res=None, after=None):
    if mode == "nn":
        (M, K), N = a.shape, b.shape[1]
    elif mode == "nt":
        (M, K), N = a.shape, b.shape[0]
    else:
        (K, M), N = a.shape, b.shape[1]
    tm, tn, tk = _mm_tiles(mode, M, N, K)
    nk = K // tk
    a_spec = pl.BlockSpec((tk, tm), lambda i, j, k: (k, i)) if mode == "tn" else pl.BlockSpec((tm, tk), lambda i, j, k: (i, k))
    b_spec = pl.BlockSpec((tn, tk), lambda i, j, k: (j, k)) if mode == "nt" else pl.BlockSpec((tk, tn), lambda i, j, k: (k, j))
    o_spec = pl.BlockSpec((tm, tn), lambda i, j, k: (i, j))

    def finish(r, r_ref, o_ref):
        if res is not None:
            r = r + r_ref[...].astype(F32)
        o_ref[...] = r.astype(out_dtype)

    def body_one_step(*refs):
        a_ref, b_ref, o_ref = refs[0], refs[1], refs[-1]
        finish(_dot(a_ref[...], b_ref[...], mode), refs[2], o_ref)

    def body(*refs):
        a_ref, b_ref, o_ref, acc = refs[0], refs[1], refs[-2], refs[-1]
        k = pl.program_id(2)

        @pl.when(k == 0)
        def _():
            acc[...] = jnp.zeros_like(acc)

        acc[...] += _dot(a_ref[...], b_ref[...], mode)

        @pl.when(k == nk - 1)
        def _():
            finish(acc[...], refs[2], o_ref)

    ins, specs = [a, b], [a_spec, b_spec]
    if res is not None:
        ins.append(res)
        specs.append(o_spec)
    if after is not None:
        ins.append(after)
        specs.append(pl.BlockSpec(memory_space=pl.ANY))
    return pl.pallas_call(
        body_one_step if nk == 1 else body, name=name, grid=(M // tm, N // tn, nk), in_specs=specs, out_specs=o_spec,
        out_shape=jax.ShapeDtypeStruct((M, N), out_dtype), scratch_shapes=[] if nk == 1 else [pltpu.VMEM((tm, tn), F32)],
        compiler_params=_cp(("parallel", "parallel", "arbitrary")))(*ins)


def _ffn_up(v, w_gate, w_up):
    S, D = v.shape
    F = w_gate.shape[1]
    tm, tn = _largest_tile(S, MM_TILE), _largest_tile(F, MM_TILE // 2)

    def body(v_ref, wg_ref, wu_ref, gt_ref, up_ref, act_ref):
        vv = v_ref[...]
        g = _dot(vv, wg_ref[...], "nn")
        u = _dot(vv, wu_ref[...], "nn")
        gt_ref[...] = g
        up_ref[...] = u
        act_ref[...] = (g * _sigmoid(g) * u).astype(act_ref.dtype)

    o_spec = pl.BlockSpec((tm, tn), lambda i, j: (i, j))
    w_spec = pl.BlockSpec((D, tn), lambda i, j: (0, j))
    return pl.pallas_call(
        body, name="ffn_up", grid=(S // tm, F // tn),
        in_specs=[pl.BlockSpec((tm, D), lambda i, j: (i, 0)), w_spec, w_spec], out_specs=[o_spec, o_spec, o_spec],
        out_shape=[jax.ShapeDtypeStruct((S, F), F32), jax.ShapeDtypeStruct((S, F), F32), jax.ShapeDtypeStruct((S, F), _MXU)],
        compiler_params=_cp(("parallel", "parallel")))(v, w_gate, w_up)


def _ffn_dv(dgt, dup, w_gate, w_up, after):
    S, F = dgt.shape
    D = w_gate.shape[0]
    tm, tn, _ = _mm_tiles("nt", S, D, F)
    tk = _largest_tile(F, MM_K_STEP // 2)
    nk = F // tk

    def body(g_ref, u_ref, wg_ref, wu_ref, *rest):
        o_ref, acc = rest[-2], rest[-1]
        k = pl.program_id(2)

        @pl.when(k == 0)
        def _():
            acc[...] = jnp.zeros_like(acc)

        acc[...] += _dot(g_ref[...], wg_ref[...], "nt") + _dot(u_ref[...], wu_ref[...], "nt")

        @pl.when(k == nk - 1)
        def _():
            o_ref[...] = acc[...]

    a_spec = pl.BlockSpec((tm, tk), lambda i, j, k: (i, k))
    w_spec = pl.BlockSpec((tn, tk), lambda i, j, k: (j, k))
    ins, specs = [dgt, dup, w_gate, w_up], [a_spec, a_spec, w_spec, w_spec]
    if after is not None:
        ins.append(after)
        specs.append(pl.BlockSpec(memory_space=pl.ANY))
    return pl.pallas_call(
        body, name="ffn_dv", grid=(S // tm, D // tn, nk), in_specs=specs, out_specs=pl.BlockSpec((tm, tn), lambda i, j, k: (i, j)),
        out_shape=jax.ShapeDtypeStruct((S, D), F32), scratch_shapes=[pltpu.VMEM((tm, tn), F32)],
        compiler_params=_cp(("parallel", "parallel", "arbitrary")))(*ins)


def _ffn_dact(dh2, w_down, gt, up):
    S, D = dh2.shape
    F = w_down.shape[0]
    tm, tn = _largest_tile(S, MM_TILE), _largest_tile(F, MM_TILE // 2)

    def body(d_ref, w_ref, gt_ref, up_ref, dg_ref, du_ref):
        da, g, u = _dot(d_ref[...], w_ref[...], "nt"), gt_ref[...], up_ref[...]
        s = _sigmoid(g)
        dg_ref[...] = (da * u * (s * (1.0 + g * (1.0 - s)))).astype(dg_ref.dtype)
        du_ref[...] = (da * (g * s)).astype(du_ref.dtype)

    o_spec = pl.BlockSpec((tm, tn), lambda i, j: (i, j))
    return pl.pallas_call(
        body, name="ffn_dact", grid=(S // tm, F // tn),
        in_specs=[pl.BlockSpec((tm, D), lambda i, j: (i, 0)), pl.BlockSpec((tn, D), lambda i, j: (j, 0)), o_spec, o_spec],
        out_specs=[o_spec, o_spec],
        out_shape=[jax.ShapeDtypeStruct((S, F), _MXU), jax.ShapeDtypeStruct((S, F), _MXU)],
        compiler_params=_cp(("parallel", "parallel")))(dh2, w_down, gt, up)


def _rms_fwd(x, w, name):
    S, D = x.shape
    tr = _pick(S, (256, 128))

    def body(x_ref, w_ref, xn_ref, rs_ref):
        xv = x_ref[...]
        rs = lax.rsqrt(jnp.mean(xv * xv, axis=-1, keepdims=True) + EPS)
        xn_ref[...] = ((xv * rs) * w_ref[...]).astype(xn_ref.dtype)
        rs_ref[...] = rs

    return pl.pallas_call(
        body, name=name, grid=(S // tr,),
        in_specs=[pl.BlockSpec((tr, D), lambda i: (i, 0)), pl.BlockSpec((1, D), lambda i: (0, 0))],
        out_specs=[pl.BlockSpec((tr, D), lambda i: (i, 0)), pl.BlockSpec((tr, 1), lambda i: (i, 0))],
        out_shape=[jax.ShapeDtypeStruct((S, D), _MXU), jax.ShapeDtypeStruct((S, 1), F32)],
        compiler_params=_cp(("parallel",)))(x, w)


def _rms_bwd(dyn, x, rs, w, res, name):
    S, D = x.shape
    tr = _pick(S, (256, 128))

    def body(dy_ref, x_ref, rs_ref, w_ref, res_ref, dx_ref, dxb_ref, dw_ref):
        @pl.when(pl.program_id(0) == 0)
        def _():
            dw_ref[...] = jnp.zeros_like(dw_ref)

        dy, r = dy_ref[...].astype(F32), rs_ref[...]
        xhat = x_ref[...] * r
        dw_ref[...] += jnp.sum(dy * xhat, axis=0, keepdims=True)
        dxhat = dy * w_ref[...]
        dx = res_ref[...] + r * (dxhat - xhat * jnp.mean(dxhat * xhat, axis=-1, keepdims=True))
        dx_ref[...] = dx
        dxb_ref[...] = dx.astype(dxb_ref.dtype)

    row = pl.BlockSpec((tr, D), lambda i: (i, 0))
    vec = pl.BlockSpec((1, D), lambda i: (0, 0))
    return pl.pallas_call(
        body, name=name, grid=(S // tr,),
        in_specs=[row, row, pl.BlockSpec((tr, 1), lambda i: (i, 0)), vec, row], out_specs=[row, row, vec],
        out_shape=[jax.ShapeDtypeStruct((S, D), F32), jax.ShapeDtypeStruct((S, D), _MXU), jax.ShapeDtypeStruct((1, D), F32)],
        compiler_params=_cp(("arbitrary",)))(dyn, x, rs, w, res)


def _final_loss(h2, w, tgt):
    S, D = h2.shape
    tr = _pick(S, (256, 128))

    def body(h_ref, w_ref, t_ref, loss_ref, dh_ref, dhb_ref, dw_ref):
        @pl.when(pl.program_id(0) == 0)
        def _():
            dw_ref[...] = jnp.zeros_like(dw_ref)
            loss_ref[...] = jnp.zeros_like(loss_ref)

        hv, wv = h_ref[...], w_ref[...]
        rs = lax.rsqrt(jnp.mean(hv * hv, axis=-1, keepdims=True) + EPS)
        xhat = hv * rs
        err = xhat * wv - t_ref[...]
        row = jnp.mean(err * err, axis=-1, keepdims=True)
        loss_ref[...] += jnp.broadcast_to(0.5 * jnp.sum(row, axis=0, keepdims=True), loss_ref.shape)
        dy = err * (1.0 / D)
        dw_ref[...] += jnp.sum(dy * xhat, axis=0, keepdims=True)
        dxhat = dy * wv
        dh = rs * (dxhat - xhat * jnp.mean(dxhat * xhat, axis=-1, keepdims=True))
        dh_ref[...] = dh
        dhb_ref[...] = dh.astype(dhb_ref.dtype)

    row = pl.BlockSpec((tr, D), lambda i: (i, 0))
    vec = pl.BlockSpec((1, D), lambda i: (0, 0))
    return pl.pallas_call(
        body, name="final_loss", grid=(S // tr,), in_specs=[row, vec, row],
        out_specs=[pl.BlockSpec((1, LANE), lambda i: (0, 0)), row, row, vec],
        out_shape=[jax.ShapeDtypeStruct((1, LANE), F32), jax.ShapeDtypeStruct((S, D), F32), jax.ShapeDtypeStruct((S, D), _MXU),
                   jax.ShapeDtypeStruct((1, D), F32)],
        compiler_params=_cp(("arbitrary",)))(h2, w, tgt)


def _shift_rows(x, k, rows):
    if k == 0:
        return x
    S = x.shape[0]
    r = pltpu.roll(x, k % S, axis=0)
    ok = (rows >= k) if k > 0 else (rows < S + k)
    return jnp.where(ok, r, 0.0)


XBC_COL0 = SSD_WIDTH // 128


def _conv_fwd(proj, conv_w, conv_b):
    S = proj.shape[0]
    nct = CONV_CH // 128

    def body(x_ref, w_ref, b_ref, o_ref):
        x = x_ref[...]
        rows = lax.broadcasted_iota(jnp.int32, x.shape, 0)
        c = b_ref[...] + w_ref[3:4, :] * x
        for k in range(1, CONV_K):
            c = c + w_ref[3 - k:4 - k, :] * _shift_rows(x, k, rows)
        o_ref[...] = c * _sigmoid(c)

    return pl.pallas_call(
        body, name="conv_fwd", grid=(nct,),
        in_specs=[pl.BlockSpec((S, 128), lambda j: (0, XBC_COL0 + j)), pl.BlockSpec((CONV_K, 128), lambda j: (0, j)),
                  pl.BlockSpec((1, 128), lambda j: (0, j))],
        out_specs=pl.BlockSpec((S, 128), lambda j: (0, j)),
        out_shape=jax.ShapeDtypeStruct((S, CONV_CH), F32), compiler_params=_cp(("parallel",)))(proj, conv_w, conv_b)


def _conv_bwd(proj, conv_w, conv_b, dxa):
    S = proj.shape[0]
    nct = CONV_CH // 128

    def body(x_ref, w_ref, b_ref, d_ref, dx_ref, dw_ref, db_ref):
        x = x_ref[...]
        rows = lax.broadcasted_iota(jnp.int32, x.shape, 0)
        xs = [_shift_rows(x, k, rows) for k in range(CONV_K)]
        c = b_ref[...] + w_ref[3:4, :] * x
        for k in range(1, CONV_K):
            c = c + w_ref[3 - k:4 - k, :] * xs[k]
        s = _sigmoid(c)
        dc = d_ref[...] * (s * (1.0 + c * (1.0 - s)))
        dx = w_ref[3:4, :] * dc
        for k in range(1, CONV_K):
            dx = dx + w_ref[3 - k:4 - k, :] * _shift_rows(dc, -k, rows)
        dx_ref[...] = dx.astype(dx_ref.dtype)
        for k in range(CONV_K):
            dw_ref[3 - k:4 - k, :] = jnp.sum(dc * xs[k], axis=0, keepdims=True)
        db_ref[...] = jnp.sum(dc, axis=0, keepdims=True)

    col = pl.BlockSpec((S, 128), lambda j: (0, j))
    return pl.pallas_call(
        body, name="conv_bwd", grid=(nct,),
        in_specs=[pl.BlockSpec((S, 128), lambda j: (0, XBC_COL0 + j)), pl.BlockSpec((CONV_K, 128), lambda j: (0, j)),
                  pl.BlockSpec((1, 128), lambda j: (0, j)), col],
        out_specs=[col, pl.BlockSpec((CONV_K, 128), lambda j: (0, j)), pl.BlockSpec((1, 128), lambda j: (0, j))],
        out_shape=[jax.ShapeDtypeStruct((S, CONV_CH), _MXU), jax.ShapeDtypeStruct((CONV_K, CONV_CH), F32),
                   jax.ShapeDtypeStruct((1, CONV_CH), F32)],
        compiler_params=_cp(("parallel",)))(proj, conv_w, conv_b, dxa)


def _ssd_consts():
    L = SSD_L
    r = lax.broadcasted_iota(jnp.int32, (L, L), 0)
    c = lax.broadcasted_iota(jnp.int32, (L, L), 1)
    causal = r >= c
    upper = (r <= c).astype(F32)
    hr = lax.broadcasted_iota(jnp.int32, (SSD_HEADS, SSD_WIDTH), 0)
    hc = lax.broadcasted_iota(jnp.int32, (SSD_HEADS, SSD_WIDTH), 1)
    expand = (lax.shift_right_logical(hc, 6) == hr).astype(F32)
    return causal, causal.astype(F32), upper, expand


def _softplus(x):
    return jnp.maximum(x, 0.0) + jnp.log(1.0 + jnp.exp(-jnp.abs(x)))


def _ssd_scalars(dtr, dt_bias, a_log, tri, upper, expand):
    dt = _softplus(dtr + dt_bias)
    A = -jnp.exp(a_log)
    adt = dt * A
    acum = _dot(tri, adt, "nn", split="b")
    acum_t = _dot(adt, upper, "tn", split="a")
    alast = acum[SSD_L - 1:SSD_L, :]
    e = jnp.exp(acum)
    wdec = jnp.exp(alast - acum)
    gam = jnp.exp(alast)
    ex = lambda t: _dot(t, expand, "nn", split="a")
    gam8 = jnp.broadcast_to(gam, (8, SSD_HEADS))
    return dt, A, acum, acum_t, e, wdec, gam, ex(dt), ex(e), ex(wdec), ex(gam8)[0:1, :]


def _ssd_fwd(proj, proj_small, xa, dt_bias, a_log, d_skip, norm_w):
    S = proj.shape[0]
    L, N, W = SSD_L, SSD_N, SSD_WIDTH
    nc = S // L

    def body(z_ref, xa_ref, dtr_ref, dtb_ref, al_ref, dsk_ref, nw_ref, yo_ref, y_ref, rs_ref, hs_ref, h_scr, y_scr):
        @pl.when(pl.program_id(0) == 0)
        def _():
            h_scr[...] = jnp.zeros_like(h_scr)

        causal, tri, upper, expand = _ssd_consts()
        dt, A, acum, acum_t, e, wdec, gam, dtE, eE, wE, gamE = _ssd_scalars(dtr_ref[:, 0:SSD_HEADS], dtb_ref[...], al_ref[...], tri, upper, expand)
        xs = xa_ref[:, 0:W]
        X = xs * dtE
        XW = X * wE
        hs_ref[0] = h_scr[...]
        for g in range(SSD_G):
            gs = slice(g * 512, (g + 1) * 512)
            Bg = xa_ref[:, W + g * N:W + (g + 1) * N]
            Cg = xa_ref[:, W + SSD_G * N + g * N:W + SSD_G * N + (g + 1) * N]
            Hg = h_scr[:, gs]
            CB = _dot(Cg, Bg, "nt")
            yoff = _dot(Cg, Hg, "nn") * eE[:, gs]
            st = _dot(Bg, XW[:, gs], "tn")
            for j in range(8):
                h = g * 8 + j
                hsl = slice(h * SSD_P, (h + 1) * SSD_P)
                lam = jnp.exp(jnp.where(causal, acum[:, h:h + 1] - acum_t[h:h + 1, :], -jnp.inf))
                y_scr[:, hsl] = _dot(CB * lam, X[:, hsl], "nn") + yoff[:, j * SSD_P:(j + 1) * SSD_P]
            h_scr[:, gs] = gamE[:, gs] * Hg + st
        dskE = _dot(jnp.broadcast_to(dsk_ref[...], (8, SSD_HEADS)), expand, "nn", split="a")[0:1, :]
        y = y_scr[...] + dskE * xs
        y_ref[...] = y
        zv = z_ref[...]
        yg = y * (zv * _sigmoid(zv))
        rs = lax.rsqrt(jnp.mean(yg * yg, axis=-1, keepdims=True) + EPS)
        rs_ref[...] = rs
        yo_ref[...] = ((yg * rs) * nw_ref[...]).astype(yo_ref.dtype)

    p16 = pl.BlockSpec((1, SSD_HEADS), lambda c: (0, 0))
    return pl.pallas_call(
        body, name="ssd_fwd", grid=(nc,),
        in_specs=[pl.BlockSpec((L, W), lambda c: (c, 0)), pl.BlockSpec((L, CONV_CH), lambda c: (c, 0)),
                  pl.BlockSpec((L, W_SMALL), lambda c: (c, 0)), p16, p16, p16, pl.BlockSpec((1, W), lambda c: (0, 0))],
        out_specs=[pl.BlockSpec((L, W), lambda c: (c, 0)), pl.BlockSpec((L, W), lambda c: (c, 0)),
                   pl.BlockSpec((L, 1), lambda c: (c, 0)), pl.BlockSpec((1, N, W), lambda c: (c, 0, 0))],
        out_shape=[jax.ShapeDtypeStruct((S, W), _MXU), jax.ShapeDtypeStruct((S, W), F32), jax.ShapeDtypeStruct((S, 1), F32),
                   jax.ShapeDtypeStruct((nc, N, W), F32)],
        scratch_shapes=[pltpu.VMEM((N, W), F32), pltpu.VMEM((L, W), F32)],
        compiler_params=_cp(("arbitrary",)))(proj, xa, proj_small, dt_bias, a_log, d_skip, norm_w)


def _ssd_bwd(dmixed, proj, proj_small, xa, y, rs2, hs, dt_bias, a_log, d_skip, norm_w):
    S = proj.shape[0]
    L, N, W, H = SSD_L, SSD_N, SSD_WIDTH, SSD_HEADS
    nc = S // L

    def body(dyo_ref, z_ref, xa_ref, dtr_ref, y_ref, rs_ref, hs_ref, dtb_ref, al_ref, dsk_ref, nw_ref,
             dz_ref, dxa_ref, ddtr_ref, ddtb_ref, dal_ref, ddsk_ref, dnw_ref, dh_scr, dx_scr):
        @pl.when(pl.program_id(0) == 0)
        def _():
            dh_scr[...] = jnp.zeros_like(dh_scr)
            ddtb_ref[...] = jnp.zeros_like(ddtb_ref)
            dal_ref[...] = jnp.zeros_like(dal_ref)
            ddsk_ref[...] = jnp.zeros_like(ddsk_ref)
            dnw_ref[...] = jnp.zeros_like(dnw_ref)

        causal, tri, upper, expand = _ssd_consts()
        heads = lambda t: _dot(t, expand, "nt", split="a")
        onehot = lambda h: (lax.broadcasted_iota(jnp.int32, (1, H), 1) == h).astype(F32)

        zv, yv, rs = z_ref[...], y_ref[...], rs_ref[...]
        sz = _sigmoid(zv)
        zs = zv * sz
        xhat = (yv * zs) * rs
        dyo = dyo_ref[...].astype(F32)
        dnw_ref[...] += jnp.sum(dyo * xhat, axis=0, keepdims=True)
        dxhat = dyo * nw_ref[...]
        dyg = rs * (dxhat - xhat * jnp.mean(dxhat * xhat, axis=-1, keepdims=True))
        dz_ref[...] = (dyg * yv * (sz * (1.0 + zv * (1.0 - sz)))).astype(dz_ref.dtype)
        dy = dyg * zs

        dtr = dtr_ref[:, 0:H]
        dt, A, acum, acum_t, e, wdec, gam, dtE, eE, wE, gamE = _ssd_scalars(dtr, dtb_ref[...], al_ref[...], tri, upper, expand)
        xs = xa_ref[:, 0:W]
        X = xs * dtE
        XW = X * wE
        dskE = _dot(jnp.broadcast_to(dsk_ref[...], (8, H)), expand, "nn", split="a")[0:1, :]
        ddsk_ref[...] += heads(jnp.broadcast_to(jnp.sum(dy * xs, axis=0, keepdims=True), (8, W)))[0:1, :]

        dYe = dy * eE
        dacum = jnp.zeros((L, H), F32)
        de_full = []
        dw_full = []
        dgam_full = []
        for g in range(SSD_G):
            gs = slice(g * 512, (g + 1) * 512)
            Bg = xa_ref[:, W + g * N:W + (g + 1) * N]
            Cg = xa_ref[:, W + SSD_G * N + g * N:W + SSD_G * N + (g + 1) * N]
            Hg = hs_ref[0, :, gs]
            dHn = dh_scr[:, gs]
            CH = _dot(Cg, Hg, "nn")
            de_full.append(dy[:, gs] * CH)
            dC = _dot(dYe[:, gs], Hg, "nt")
            dHs = gamE[:, gs] * dHn + _dot(Cg, dYe[:, gs], "tn")
            dgam_full.append(jnp.sum(dHn * Hg, axis=0, keepdims=True))
            BdS = _dot(Bg, dHn, "nn")
            dB = _dot(XW[:, gs], dHn, "nt")
            dx_scr[:, gs] = BdS * wE[:, gs]
            dw_full.append(BdS * X[:, gs])
            CB = _dot(Cg, Bg, "nt")
            dCB = jnp.zeros((L, L), F32)
            for j in range(8):
                h = g * 8 + j
                hsl = slice(h * SSD_P, (h + 1) * SSD_P)
                lam = jnp.exp(jnp.where(causal, acum[:, h:h + 1] - acum_t[h:h + 1, :], -jnp.inf))
                M = CB * lam
                dM = _dot(dy[:, hsl], X[:, hsl], "nt")
                dx_scr[:, hsl] += _dot(M, dy[:, hsl], "tn")
                dCB = dCB + dM * lam
                Q = dM * M
                rowsum = jnp.sum(Q, axis=1, keepdims=True)
                colsum = _dot(Q, jnp.ones((L, 8), F32), "tn", split="a")[:, 0:1]
                dacum = dacum + (rowsum - colsum) * onehot(h)
            dC = dC + _dot(dCB, Bg, "nn")
            dB = dB + _dot(dCB, Cg, "tn")
            dxa_ref[:, W + g * N:W + (g + 1) * N] = dB
            dxa_ref[:, W + SSD_G * N + g * N:W + SSD_G * N + (g + 1) * N] = dC
            dh_scr[:, gs] = dHs

        de16 = heads(jnp.concatenate(de_full, axis=1))
        dw16 = heads(jnp.concatenate(dw_full, axis=1))
        dgam16 = heads(jnp.broadcast_to(jnp.concatenate(dgam_full, axis=1), (8, W)))[0:1, :]
        dacum = dacum + de16 * e - dw16 * wdec
        dlast = jnp.sum(dw16 * wdec, axis=0, keepdims=True) + dgam16 * gam
        lastrow = (lax.broadcasted_iota(jnp.int32, (L, 1), 0) == L - 1).astype(F32)
        dacum = dacum + lastrow * dlast
        da = _dot(tri, dacum, "tn", split="b")
        dX = dx_scr[...]
        ddt = da * A + heads(dX * xs)
        dA = jnp.sum(da * dt, axis=0, keepdims=True)
        dal_ref[...] += dA * A
        ddtr = ddt * _sigmoid(dtr + dtb_ref[...])
        ddtb_ref[...] += jnp.sum(ddtr, axis=0, keepdims=True)
        ddtr_ref[...] = ddtr
        dxa_ref[:, 0:W] = dX * dtE + dy * dskE

    p16 = pl.BlockSpec((1, H), lambda c: (0, 0))
    rev = lambda c: (nc - 1 - c, 0)
    return pl.pallas_call(
        body, name="ssd_bwd", grid=(nc,),
        in_specs=[pl.BlockSpec((L, W), rev), pl.BlockSpec((L, W), rev), pl.BlockSpec((L, CONV_CH), rev),
                  pl.BlockSpec((L, W_SMALL), rev), pl.BlockSpec((L, W), rev), pl.BlockSpec((L, 1), rev),
                  pl.BlockSpec((1, N, W), lambda c: (nc - 1 - c, 0, 0)), p16, p16, p16, pl.BlockSpec((1, W), lambda c: (0, 0))],
        out_specs=[pl.BlockSpec((L, W), rev), pl.BlockSpec((L, CONV_CH), rev), pl.BlockSpec((L, H), rev),
                   p16, p16, p16, pl.BlockSpec((1, W), lambda c: (0, 0))],
        out_shape=[jax.ShapeDtypeStruct((S, W), _MXU), jax.ShapeDtypeStruct((S, CONV_CH), F32), jax.ShapeDtypeStruct((S, H), F32),
                   jax.ShapeDtypeStruct((1, H), F32), jax.ShapeDtypeStruct((1, H), F32), jax.ShapeDtypeStruct((1, H), F32),
                   jax.ShapeDtypeStruct((1, W), F32)],
        scratch_shapes=[pltpu.VMEM((N, W), F32), pltpu.VMEM((L, W), F32)],
        compiler_params=_cp(("arbitrary",)))(dmixed, proj, xa, proj_small, y, rs2, hs, dt_bias, a_log, d_skip, norm_w)


def _rope_tables(S):
    inv = 1.0 / (ROPE_THETA ** (jnp.arange(0, ROPE_DIM, 2, dtype=F32) / ROPE_DIM))
    ang = jnp.arange(S, dtype=F32)[:, None] * inv[None, :]
    cos, sin = jnp.cos(ang), jnp.sin(ang)
    half = ROPE_DIM // 2
    c64 = jnp.concatenate([cos, cos, jnp.ones((S, HD - ROPE_DIM), F32)], axis=1)
    s64 = jnp.concatenate([sin, sin, jnp.zeros((S, HD - ROPE_DIM), F32)], axis=1)
    del half
    return jnp.concatenate([c64, c64], axis=1), jnp.concatenate([s64, s64], axis=1)


def _rope(xs, blk0, width, cos, sin, sign, out_dtype, name, extra=None):
    S = xs[0].shape[0]
    tr = _pick(S, (512, 256, 128))
    nx = len(xs)

    def body(*refs):
        x_refs, c_ref, s_ref = refs[:nx], refs[nx], refs[nx + 1]
        e_ref = refs[nx + 2] if extra is not None else None
        o_ref = refs[-1]
        cv, sv = c_ref[...], s_ref[...] * sign
        lane = lax.broadcasted_iota(jnp.int32, (tr, 128), 1)
        first = (lane & (HD - 1)) < (ROPE_DIM // 2)
        for j in range(bw // 128):
            cs = slice(j * 128, (j + 1) * 128)
            xv = x_refs[0][:, cs].astype(F32)
            for r in x_refs[1:]:
                xv = xv + r[:, cs].astype(F32)
            out = _rotate128(xv, cv, sv, first)
            if extra is not None:
                out = out + e_ref[:, cs].astype(F32)
            o_ref[:, cs] = out.astype(out_dtype)

    bw = 512
    assert width % bw == 0 and (blk0 * 256) % bw == 0
    b0 = blk0 * 256 // bw
    t128 = pl.BlockSpec((tr, 128), lambda i, j: (i, 0))
    oblk = pl.BlockSpec((tr, bw), lambda i, j: (i, j))
    specs = [pl.BlockSpec((tr, bw), lambda i, j: (i, b0 + j))] * nx + [t128, t128]
    ins = list(xs) + [cos, sin]
    if extra is not None:
        assert (extra[1] * 256) % bw == 0
        ins.append(extra[0])
        eb = extra[1] * 256 // bw
        specs.append(pl.BlockSpec((tr, bw), lambda i, j: (i, eb + j)))
    return pl.pallas_call(
        body, name=name, grid=(S // tr, width // bw), in_specs=specs, out_specs=oblk,
        out_shape=jax.ShapeDtypeStruct((S, width), out_dtype), compiler_params=_cp(("parallel", "parallel")))(*ins)


def _rotate128(xv, cv, sv, first):
    rot = jnp.where(first, -pltpu.roll(xv, 128 - ROPE_DIM // 2, axis=1), pltpu.roll(xv, ROPE_DIM // 2, axis=1))
    return xv * cv + rot * sv


def _kv_prep(proj, cos, sin, tk):
    S = proj.shape[0]
    NB = S // SEL_BLOCK

    def body(ks_ref, vs_ref, kw_ref, vw_ref, c_ref, s_ref, *outs):
        cv, sv = c_ref[...], s_ref[...]
        lane = lax.broadcasted_iota(jnp.int32, (tk, 128), 1)
        first = (lane & (HD - 1)) < (ROPE_DIM // 2)
        key = pl.program_id(0) * tk + lax.broadcasted_iota(jnp.int32, (tk, NB), 0)
        onehot = (lax.shift_right_logical(key, 6) == lax.broadcasted_iota(jnp.int32, (tk, NB), 1)).astype(F32)
        for j, (ref, rotated) in enumerate(((ks_ref, True), (vs_ref, False), (kw_ref, True), (vw_ref, False))):
            nat, blk = outs[2 * j], outs[2 * j + 1]
            for half in range(2):
                xv = ref[:, half * 128:(half + 1) * 128]
                if rotated:
                    xv = _rotate128(xv, cv, sv, first)
                for e in range(2):
                    h = 2 * half + e
                    piece = xv[:, e * HD:(e + 1) * HD]
                    nat[h] = (jnp.concatenate([piece, onehot], axis=1) if j == 0 else piece).astype(nat.dtype)
                    blk[h, 0] = piece.T.astype(blk.dtype)

    col = lambda b: pl.BlockSpec((tk, 256), lambda i: (i, b))
    t128 = pl.BlockSpec((tk, 128), lambda i: (i, 0))
    nat_spec = lambda w: pl.BlockSpec((N_KV, tk, w), lambda i: (0, i, 0))
    blk_spec = pl.BlockSpec((N_KV, 1, HD, tk), lambda i: (0, i, 0, 0))
    nat_shape = lambda w: jax.ShapeDtypeStruct((N_KV, S, w), _MXU)
    blk_shape = jax.ShapeDtypeStruct((N_KV, S // tk, HD, tk), _MXU)
    widths = (HD + NB, HD, HD, HD)
    res = pl.pallas_call(
        body, name="kv_prep", grid=(S // tk,), in_specs=[col(KSB), col(VSB), col(KWB), col(VWB), t128, t128],
        out_specs=[s for w in widths for s in (nat_spec(w), blk_spec)],
        out_shape=[s for w in widths for s in (nat_shape(w), blk_shape)],
        compiler_params=_cp(("parallel",)))(proj, proj, proj, proj, cos, sin)
    return dict(ks_ext=res[0], ks_t=res[1], vs=res[2], vs_t=res[3], kw=res[4], kw_t=res[5], vw=res[6], vw_t=res[7])


def _dkv_post(dks, dvs, dkw, dvw, cos, sin):
    S = dks.shape[1]
    tr = _pick(S, (512, 256, 128))

    def body(dks_ref, dvs_ref, dkw_ref, dvw_ref, c_ref, s_ref, o_ref):
        cv, sv = c_ref[...], -s_ref[...]
        lane = lax.broadcasted_iota(jnp.int32, (tr, 128), 1)
        first = (lane & (HD - 1)) < (ROPE_DIM // 2)
        for j, (ref, rotated) in enumerate(((dks_ref, True), (dvs_ref, False), (dkw_ref, True), (dvw_ref, False))):
            for half in range(2):
                xv = jnp.concatenate([ref[2 * half], ref[2 * half + 1]], axis=1)
                if rotated:
                    xv = _rotate128(xv, cv, sv, first)
                o_ref[:, j * 256 + half * 128:j * 256 + (half + 1) * 128] = xv.astype(o_ref.dtype)

    hm = pl.BlockSpec((N_KV, tr, HD), lambda i: (0, i, 0))
    t128 = pl.BlockSpec((tr, 128), lambda i: (i, 0))
    return pl.pallas_call(
        body, name="dkv_post", grid=(S // tr,), in_specs=[hm, hm, hm, hm, t128, t128],
        out_specs=pl.BlockSpec((tr, 4 * 256), lambda i: (i, 0)), out_shape=jax.ShapeDtypeStruct((S, 4 * 256), _MXU),
        compiler_params=_cp(("parallel",)))(dks, dvs, dkw, dvw, cos, sin)


def _compress_fwd(R, pe, w1, w2):
    NC = R.shape[1]
    half = 16 * HD

    def body(r_ref, pe_ref, w1_ref, w2_ref, o_ref, hid_ref):
        r = r_ref[0]
        a = _dot(r + pe_ref[:, 0:half], w1_ref[0:half, :], "nn")
        b = _dot(r + pe_ref[:, half:2 * half], w1_ref[half:2 * half, :], "nn")
        hid = a + pltpu.roll(b, NC - 1, axis=0)
        hid_ref[0] = hid
        out = _dot(hid * _sigmoid(hid), w2_ref[...], "nn")
        rows = lax.broadcasted_iota(jnp.int32, out.shape, 0)
        o_ref[0] = jnp.where(rows < NC - 1, out, 0.0).astype(o_ref.dtype)

    return pl.pallas_call(
        body, name="compress_fwd", grid=(N_KV,),
        in_specs=[pl.BlockSpec((1, NC, half), lambda h: (h, 0, 0)), pl.BlockSpec((1, 2 * half), lambda h: (0, 0)),
                  pl.BlockSpec((2 * half, CMP_HID), lambda h: (0, 0)), pl.BlockSpec((CMP_HID, HD), lambda h: (0, 0))],
        out_specs=[pl.BlockSpec((1, NC, HD), lambda h: (h, 0, 0)), pl.BlockSpec((1, NC, CMP_HID), lambda h: (h, 0, 0))],
        out_shape=[jax.ShapeDtypeStruct((N_KV, NC, HD), _MXU), jax.ShapeDtypeStruct((N_KV, NC, CMP_HID), F32)],
        compiler_params=_cp(("parallel",)))(R, pe, w1, w2)


def _compress_bwd(R, pe, w1, w2, hid, dout):
    NC = R.shape[1]
    half = 16 * HD

    def body(r_ref, pe_ref, w1_ref, w2_ref, hid_ref, do_ref, dr_ref, dw1_ref, dw2_ref, dpe_ref):
        @pl.when(pl.program_id(0) == 0)
        def _():
            dw1_ref[...] = jnp.zeros_like(dw1_ref)
            dw2_ref[...] = jnp.zeros_like(dw2_ref)
            dpe_ref[...] = jnp.zeros_like(dpe_ref)

        r, hv, do = r_ref[0], hid_ref[0], do_ref[0]
        s = _sigmoid(hv)
        dw2_ref[...] += _dot(hv * s, do, "tn")
        dhid = _dot(do, w2_ref[...], "nt") * (s * (1.0 + hv * (1.0 - s)))
        rows = lax.broadcasted_iota(jnp.int32, dhid.shape, 0)
        dhid = jnp.where(rows < NC - 1, dhid, 0.0)
        dhid_dn = pltpu.roll(dhid, 1, axis=0)
        dw1_ref[0:half, :] += _dot(r + pe_ref[:, 0:half], dhid, "tn")
        dw1_ref[half:2 * half, :] += _dot(r + pe_ref[:, half:2 * half], dhid_dn, "tn")
        dxt = _dot(dhid, w1_ref[0:half, :], "nt")
        dxb = _dot(dhid_dn, w1_ref[half:2 * half, :], "nt")
        dr_ref[0] = dxt + dxb
        dpe_ref[:, 0:half] += jnp.sum(dxt, axis=0, keepdims=True)
        dpe_ref[:, half:2 * half] += jnp.sum(dxb, axis=0, keepdims=True)

    return pl.pallas_call(
        body, name="compress_bwd", grid=(N_KV,),
        in_specs=[pl.BlockSpec((1, NC, half), lambda h: (h, 0, 0)), pl.BlockSpec((1, 2 * half), lambda h: (0, 0)),
                  pl.BlockSpec((2 * half, CMP_HID), lambda h: (0, 0)), pl.BlockSpec((CMP_HID, HD), lambda h: (0, 0)),
                  pl.BlockSpec((1, NC, CMP_HID), lambda h: (h, 0, 0)), pl.BlockSpec((1, NC, HD), lambda h: (h, 0, 0))],
        out_specs=[pl.BlockSpec((1, NC, half), lambda h: (h, 0, 0)), pl.BlockSpec((2 * half, CMP_HID), lambda h: (0, 0)),
                   pl.BlockSpec((CMP_HID, HD), lambda h: (0, 0)), pl.BlockSpec((1, 2 * half), lambda h: (0, 0))],
        out_shape=[jax.ShapeDtypeStruct((N_KV, NC, half), F32), jax.ShapeDtypeStruct((2 * half, CMP_HID), F32),
                   jax.ShapeDtypeStruct((CMP_HID, HD), F32), jax.ShapeDtypeStruct((1, 2 * half), F32)],
        compiler_params=_cp(("arbitrary",)))(R, pe, w1, w2, hid, dout)


def _attn_cfg(S, Sk, mode):
    tk = _pick(Sk, (256, 128))
    if mode == "cmp":
        return _pick(S, (512, 256, 128)), Sk
    if mode == "sel" and S % (2 * tk) == 0:
        return 2 * tk, tk
    return tk, tk


def _block_start(kb, tk):
    return kb * tk if isinstance(kb, int) else pl.multiple_of(kb * tk, tk)


def _pipelined_key_blocks(mode, q0, tq, tk, produce, consume):
    if mode == "cmp":
        produce(0, True, 0)
        consume(0, 0)
        return
    if mode == "win":
        assert tq == tk and WINDOW == 2 * tk
        last = q0 // tk
        first = jnp.maximum(last - 2, 0)

        @pl.when(last == 0)
        def _():
            produce(last, True, 0)
            consume(last, 0)

        @pl.when(last == 1)
        def _():
            produce(first, True, 0)
            produce(last, True, 1)
            consume(first, 0)
            consume(last, 1)

        @pl.when(last >= 2)
        def _():
            produce(first, True, 0)
            produce(first + 1, False, 1)
            consume(first, 0)
            produce(last, True, 0)
            consume(first + 1, 1)
            consume(last, 0)

        return
    first, n_plain, plain_masked = 0, q0 // tk, False
    pairs = jnp.maximum(n_plain - 1, 0) // 2
    if tq == 2 * tk:
        @pl.when(n_plain >= 1)
        def _():
            produce(0, False, 0)

        def two_plain(j, carry):
            produce(2 * j + 1, False, 1)
            consume(2 * j, 0)
            produce(2 * j + 2, False, 0)
            consume(2 * j + 1, 1)
            return carry

        lax.fori_loop(0, pairs, two_plain, 0)
        kb = 2 * pairs

        @pl.when(n_plain >= 2)
        def _():
            produce(kb + 1, False, 1)
            consume(kb, 0)
            produce(n_plain, True, 0)
            consume(kb + 1, 1)
            produce(n_plain + 1, True, 1)
            consume(n_plain, 0)
            consume(n_plain + 1, 1)

        @pl.when(n_plain == 0)
        def _():
            produce(0, True, 0)
            produce(1, True, 1)
            consume(0, 0)
            consume(1, 1)

        return
    assert tq == tk
    last = first + n_plain

    @pl.when(n_plain >= 1)
    def _():
        produce(first, plain_masked, 0)

    def two(j, carry):
        kb = first + 2 * j
        produce(kb + 1, plain_masked, 1)
        consume(kb, 0)
        produce(kb + 2, plain_masked, 0)
        consume(kb + 1, 1)
        return carry

    lax.fori_loop(0, pairs, two, 0)
    kb = first + 2 * pairs
    left = n_plain - 2 * pairs

    @pl.when(left == 2)
    def _():
        produce(kb + 1, plain_masked, 1)
        consume(kb, 0)
        produce(last, True, 0)
        consume(kb + 1, 1)
        consume(last, 0)

    @pl.when(left == 1)
    def _():
        produce(last, True, 1)
        consume(kb, 0)
        consume(last, 1)

    @pl.when(left == 0)
    def _():
        produce(last, True, 0)
        consume(last, 0)


def _attn_bias(mode, q0, k0, tq, tk):
    k = k0 + lax.broadcasted_iota(jnp.int32, (tk, tq), 0)
    t = q0 + lax.broadcasted_iota(jnp.int32, (tk, tq), 1)
    if mode == "cmp":
        ok = (k * 16 + 31) <= t
    elif mode == "win":
        ok = (k <= t) & ((t - k) < WINDOW)
    else:
        ok = k <= t
    bias = jnp.where(ok, 0.0, NEG)
    return jnp.concatenate([bias] * GRP, axis=1), jnp.concatenate([ok.astype(F32)] * GRP, axis=1)


def _sel_operands(qs, selneg_ref):
    return jnp.concatenate([qs, jnp.concatenate([selneg_ref[0]] * GRP, axis=0)], axis=1)


def _stack_heads(ref, tq):
    return jnp.concatenate([ref[:, g * HD:(g + 1) * HD] for g in range(GRP)], axis=0)


def _scaled_queries(q_ref, tq):
    return (_stack_heads(q_ref, tq).astype(F32) * SCALE).astype(_MXU)


def _blocked_t(x, tk):
    n, Sk, d = x.shape
    return x.reshape(n, Sk // tk, tk, d).transpose(0, 1, 3, 2)


def _head_rows(ref):
    return jnp.concatenate([ref[0, g:g + 1, :] for g in range(GRP)], axis=1)


def _attn_fwd(q, qcol0, k, vt, mode, selneg, gate, y_prev, y_dtype, name):
    S, Sk = q.shape[0], k.shape[1]
    tq, tk = _attn_cfg(S, Sk, mode)
    R = GRP * tq

    def body(*refs):
        q_ref, k_ref, vt_ref = refs[:3]
        rest = list(refs[3:])
        sel_ref = rest.pop(0) if mode == "sel" else None
        gate_ref = rest.pop(0)
        yp_ref = rest.pop(0) if y_prev is not None else None
        o_ref, lse_ref, y_ref, m_scr, l_scr, acc, s_scr = rest
        q0 = pl.program_id(1) * tq
        qs = _scaled_queries(q_ref, tq)
        m_scr[...] = jnp.full_like(m_scr, NEG)
        l_scr[...] = jnp.zeros_like(l_scr)
        acc[...] = jnp.zeros_like(acc)
        qk = _sel_operands(qs, sel_ref) if mode == "sel" else qs

        def produce(kb, masked, slot):
            k0 = _block_start(kb, tk)
            s = _dot(k_ref[0, pl.ds(k0, tk), :], qk, "nt")
            if masked:
                s = s + _attn_bias(mode, q0, k0, tq, tk)[0]
            s_scr[slot] = s

        def consume(kb, slot):
            s = s_scr[slot]
            m_old = m_scr[...]
            m_new = jnp.maximum(m_old, jnp.max(s, axis=0, keepdims=True))
            p = jnp.exp(s - m_new)
            if mode == "cmp":
                p = p * _attn_bias(mode, q0, 0, tq, tk)[1]
            alpha = jnp.exp(m_old - m_new)
            l_scr[...] = alpha * l_scr[...] + jnp.sum(p, axis=0, keepdims=True)
            acc[...] = alpha * acc[...] + _dot(vt_ref[0, kb], p, "nn")
            m_scr[...] = m_new

        _pipelined_key_blocks(mode, q0, tq, tk, produce, consume)
        l = l_scr[...]
        good = l > 0.0
        o_t = acc[...] * jnp.where(good, 1.0 / jnp.where(good, l, 1.0), 0.0)
        lse = jnp.where(good, m_scr[...] + jnp.log(jnp.where(good, l, 1.0)), -NEG)
        y_t = o_t * _sigmoid(_head_rows(gate_ref))
        for g in range(GRP):
            hs, qs_ = slice(g * HD, (g + 1) * HD), slice(g * tq, (g + 1) * tq)
            o_ref[:, hs] = o_t[:, qs_].T
            lse_ref[0, g:g + 1, :] = lse[:, qs_]
            yg = y_t[:, qs_].T
            if y_prev is not None:
                yg = yg + yp_ref[:, hs]
            y_ref[:, hs] = yg.astype(y_ref.dtype)

    row_spec = pl.BlockSpec((1, GRP, tq), lambda h, i: (h, 0, i))
    qo_spec = pl.BlockSpec((tq, GRP * HD), lambda h, i: (i, h))
    ins = [q, k, vt]
    specs = [pl.BlockSpec((tq, GRP * HD), lambda h, i: (i, qcol0 + h)), pl.BlockSpec((1, Sk, k.shape[2]), lambda h, i: (h, 0, 0)),
             pl.BlockSpec((1, Sk // tk, HD, tk), lambda h, i: (h, 0, 0, 0))]
    if mode == "sel":
        ins.append(selneg)
        specs.append(pl.BlockSpec((1, tq, selneg.shape[2]), lambda h, i: (h, i, 0)))
    ins.append(gate)
    specs.append(row_spec)
    if y_prev is not None:
        ins.append(y_prev)
        specs.append(qo_spec)
    return pl.pallas_call(
        body, name=name, grid=(N_KV, S // tq), in_specs=specs, out_specs=[qo_spec, row_spec, qo_spec],
        out_shape=[jax.ShapeDtypeStruct((S, ATT_WIDTH), F32), jax.ShapeDtypeStruct((N_KV, GRP, S), F32),
                   jax.ShapeDtypeStruct((S, ATT_WIDTH), y_dtype)],
        scratch_shapes=[pltpu.VMEM((1, R), F32), pltpu.VMEM((1, R), F32), pltpu.VMEM((HD, R), F32), pltpu.VMEM((2, tk, R), F32)],
        compiler_params=_cp(("parallel", "arbitrary")))(*ins)


def _attn_bwd(q, qcol0, k, kt, v, o, lse, dy, dycol0, gate, mode, selneg, name):
    S, Sk = q.shape[0], k.shape[1]
    tq, tk = _attn_cfg(S, Sk, mode)
    R = GRP * tq

    def body(*refs):
        if mode == "sel":
            (q_ref, k_ref, kt_ref, v_ref, o_ref, lse_ref, dy_ref, gate_ref, sel_ref, dq_ref, dk_ref, dv_ref, dg_ref, dq_scr, s_scr,
             dp_scr) = refs
        else:
            q_ref, k_ref, kt_ref, v_ref, o_ref, lse_ref, dy_ref, gate_ref, dq_ref, dk_ref, dv_ref, dg_ref, dq_scr, s_scr, dp_scr = refs

        @pl.when(pl.program_id(1) == 0)
        def _():
            dk_ref[...] = jnp.zeros_like(dk_ref)
            dv_ref[...] = jnp.zeros_like(dv_ref)

        q0 = pl.program_id(1) * tq
        qs = _scaled_queries(q_ref, tq)
        dys = _stack_heads(dy_ref, tq)
        gv = _sigmoid(_head_rows(gate_ref))
        dy_o = _dot(jnp.ones((8, HD), F32), dys * _stack_heads(o_ref, tq), "nt", split="b")[0:1, :]
        delta = gv * dy_o
        dgate = dy_o * (gv * (1.0 - gv))
        for g in range(GRP):
            dg_ref[0, g:g + 1, :] = dgate[:, g * tq:(g + 1) * tq]
        lsev = _head_rows(lse_ref)
        dos = (dys * jnp.broadcast_to(gv, (8, R)).T[:, 0:1]).astype(_MXU)
        dq_scr[...] = jnp.zeros_like(dq_scr)
        qk = _sel_operands(qs, sel_ref) if mode == "sel" else qs

        def produce(kb, masked, slot):
            k0 = _block_start(kb, tk)
            s = _dot(k_ref[0, pl.ds(k0, tk), :], qk, "nt")
            if masked:
                s = s + _attn_bias(mode, q0, k0, tq, tk)[0]
            s_scr[slot] = s
            dp_scr[slot] = _dot(v_ref[0, pl.ds(k0, tk), :], dos, "nt")

        def consume(kb, slot):
            k0 = _block_start(kb, tk)
            p = jnp.exp(s_scr[slot] - lsev)
            if mode == "cmp":
                p = p * _attn_bias(mode, q0, 0, tq, tk)[1]
            ds = p * (dp_scr[slot] - delta)
            dq_scr[...] += _dot(kt_ref[0, kb], ds, "nn")
            dk_ref[0, pl.ds(k0, tk), :] += _dot(ds, qs, "nn")
            dv_ref[0, pl.ds(k0, tk), :] += _dot(p, dos, "nn")

        _pipelined_key_blocks(mode, q0, tq, tk, produce, consume)
        for g in range(GRP):
            dq_ref[:, g * HD:(g + 1) * HD] = (dq_scr[:, g * tq:(g + 1) * tq] * SCALE).T

    kv_spec = pl.BlockSpec((1, Sk, HD), lambda h, i: (h, 0, 0))
    qo_spec = pl.BlockSpec((tq, GRP * HD), lambda h, i: (i, h))
    row_spec = pl.BlockSpec((1, GRP, tq), lambda h, i: (h, 0, i))
    ins = [q, k, kt, v, o, lse, dy, gate]
    specs = [pl.BlockSpec((tq, GRP * HD), lambda h, i: (i, qcol0 + h)), pl.BlockSpec((1, Sk, k.shape[2]), lambda h, i: (h, 0, 0)),
             pl.BlockSpec((1, Sk // tk, HD, tk), lambda h, i: (h, 0, 0, 0)), kv_spec, qo_spec, row_spec,
             pl.BlockSpec((tq, GRP * HD), lambda h, i: (i, dycol0 + h)), row_spec]
    if mode == "sel":
        ins.append(selneg)
        specs.append(pl.BlockSpec((1, tq, selneg.shape[2]), lambda h, i: (h, i, 0)))
    return pl.pallas_call(
        body, name=name, grid=(N_KV, S // tq), in_specs=specs, out_specs=[qo_spec, kv_spec, kv_spec, row_spec],
        out_shape=[jax.ShapeDtypeStruct((S, ATT_WIDTH), F32), jax.ShapeDtypeStruct((N_KV, Sk, HD), F32),
                   jax.ShapeDtypeStruct((N_KV, Sk, HD), F32), jax.ShapeDtypeStruct((N_KV, GRP, S), F32)],
        scratch_shapes=[pltpu.VMEM((HD, R), F32), pltpu.VMEM((2, tk, R), F32), pltpu.VMEM((2, tk, R), F32)],
        compiler_params=_cp(("parallel", "arbitrary")))(*ins)


def _select(q, qcol0, k_cmp, lse):
    S, NC = q.shape[0], k_cmp.shape[1]
    NB = S // SEL_BLOCK
    tq = _attn_cfg(S, NC, "cmp")[0]
    ci = np.arange(NC)[None, :] * 16
    sj = np.arange(NB)[:, None] * SEL_BLOCK
    ov_t = np.clip(np.minimum(ci + 32, sj + SEL_BLOCK) - np.maximum(ci, sj), 0, None) / 32.0
    ov_t[:, NC - 1] = 0.0
    ov_t = jnp.asarray(ov_t, F32)

    def body(q_ref, k_ref, lse_ref, ov_ref, sel_ref):
        q0 = pl.program_id(1) * tq
        bias, okf = _attn_bias("cmp", q0, 0, tq, NC)
        lsev = _head_rows(lse_ref)
        p = jnp.exp(_dot(k_ref[0], _scaled_queries(q_ref, tq), "nt") + bias - lsev) * okf
        imp4 = _dot(ov_ref[...], p, "nn")
        imp = imp4[:, 0:tq] + imp4[:, tq:2 * tq] + imp4[:, 2 * tq:3 * tq] + imp4[:, 3 * tq:4 * tq]
        blk = lax.broadcasted_iota(jnp.int32, (NB, tq), 0)
        cur = lax.shift_right_logical(q0 + lax.broadcasted_iota(jnp.int32, (NB, tq), 1), 6)
        imp = jnp.where((blk == 0) | (blk == cur) | (blk == cur - 1), FORCE, imp)
        imp = jnp.where(blk <= cur, imp, -1.0)
        rank = jnp.zeros((NB, tq), F32)
        for j in range(NB):
            row = imp[j:j + 1, :]
            ahead = (row > imp) | ((row == imp) & (blk > j))
            rank = rank + ahead.astype(F32)
        chosen = (rank < float(N_SELECT)) & (imp >= 0.0)
        sel_ref[0] = jnp.where(chosen, 0.0, NEG).T.astype(sel_ref.dtype)

    return pl.pallas_call(
        body, name="select_blocks", grid=(N_KV, S // tq),
        in_specs=[pl.BlockSpec((tq, GRP * HD), lambda h, i: (i, qcol0 + h)), pl.BlockSpec((1, NC, HD), lambda h, i: (h, 0, 0)),
                  pl.BlockSpec((1, GRP, tq), lambda h, i: (h, 0, i)), pl.BlockSpec((NB, NC), lambda h, i: (0, 0))],
        out_specs=pl.BlockSpec((1, tq, NB), lambda h, i: (h, i, 0)),
        out_shape=jax.ShapeDtypeStruct((N_KV, S, NB), _MXU), compiler_params=_cp(("parallel", "parallel")))(q, k_cmp, lse, ov_t)


def _to_rows16(x):
    S = x.shape[0]
    return x.reshape(S // 16, 16, N_KV, HD).transpose(2, 0, 1, 3).reshape(N_KV, S // 16, 16 * HD)


def _from_rows16(r):
    NC = r.shape[1]
    return r.reshape(N_KV, NC, 16, HD).transpose(1, 2, 0, 3).reshape(NC * 16, N_KV * HD)


DT_COL0 = SSD_WIDTH + CONV_CH
GATE_IN_COL0 = D_IN - 3 * N_HEADS


SHARD_IN = D_IN // N_DEV


def _orig_cols(ref, c0, width):
    pieces, c = [], c0
    while c < c0 + width:
        d, off = divmod(c, SHARD_IN)
        w = min(SHARD_IN - off, c0 + width - c)
        pieces.append(ref[d, :, off:off + w])
        c += w
    return pieces[0] if len(pieces) == 1 else jnp.concatenate(pieces, axis=1)


def _cols_from_slabs(slabs):
    _, R, c = slabs.shape
    tr = _pick(R, (256, 128))

    def body(s_ref, o_ref):
        for t in range(N_DEV * c // LANE):
            pieces, col = [], t * LANE
            while col < (t + 1) * LANE:
                d, off = divmod(col, c)
                w = min(c - off, (t + 1) * LANE - col)
                pieces.append(s_ref[d, :, off:off + w])
                col += w
            o_ref[:, t * LANE:(t + 1) * LANE] = pieces[0] if len(pieces) == 1 else jnp.concatenate(pieces, axis=1)

    return pl.pallas_call(
        body, name="cols_from_slabs", grid=(R // tr,), in_specs=[pl.BlockSpec((N_DEV, tr, c), lambda i: (0, i, 0))],
        out_specs=pl.BlockSpec((tr, N_DEV * c), lambda i: (i, 0)), out_shape=jax.ShapeDtypeStruct((R, N_DEV * c), slabs.dtype),
        compiler_params=_cp(("parallel",)))(slabs)


def _slabs_from_cols(x):
    R, c = x.shape[0], x.shape[1] // N_DEV
    tr = _pick(R, (256, 128))

    def body(x_ref, o_ref):
        for d in range(N_DEV):
            o_ref[d] = x_ref[:, d * c:(d + 1) * c]

    return pl.pallas_call(
        body, name="slabs_from_cols", grid=(R // tr,), in_specs=[pl.BlockSpec((tr, N_DEV * c), lambda i: (i, 0))],
        out_specs=pl.BlockSpec((N_DEV, tr, c), lambda i: (0, i, 0)), out_shape=jax.ShapeDtypeStruct((N_DEV, R, c), x.dtype),
        compiler_params=_cp(("parallel",)))(x)


def _w_in_from_slabs(slabs):
    D = slabs.shape[1]
    tr = _pick(D, (256, 128))

    def body(s_ref, main_ref, small_ref):
        for t in range(W_MAIN // LANE):
            c = t * LANE
            main_ref[:, c:c + LANE] = _orig_cols(s_ref, c if c < DT_COL0 else c + SSD_HEADS, LANE)
        small_ref[...] = jnp.concatenate(
            [_orig_cols(s_ref, DT_COL0, SSD_HEADS), _orig_cols(s_ref, GATE_IN_COL0, 3 * N_HEADS),
             jnp.zeros((tr, W_SMALL - SSD_HEADS - 3 * N_HEADS), small_ref.dtype)], axis=1)

    return pl.pallas_call(
        body, name="w_in_layout", grid=(D // tr,), in_specs=[pl.BlockSpec((N_DEV, tr, SHARD_IN), lambda i: (0, i, 0))],
        out_specs=[pl.BlockSpec((tr, W_MAIN), lambda i: (i, 0)), pl.BlockSpec((tr, W_SMALL), lambda i: (i, 0))],
        out_shape=[jax.ShapeDtypeStruct((D, W_MAIN), slabs.dtype), jax.ShapeDtypeStruct((D, W_SMALL), slabs.dtype)],
        compiler_params=_cp(("parallel",)))(slabs)


def _w_in_to_slabs(main, small):
    D = main.shape[0]
    tr = _pick(D, (256, 128))
    ranges = [(0, DT_COL0, 0, 0), (DT_COL0, DT_COL0 + SSD_HEADS, 1, 0), (DT_COL0 + SSD_HEADS, GATE_IN_COL0, 0, DT_COL0),
              (GATE_IN_COL0, D_IN, 1, SSD_HEADS)]

    def body(main_ref, small_ref, o_ref):
        srcs = (main_ref, small_ref)
        for d in range(N_DEV):
            lo, hi = d * SHARD_IN, (d + 1) * SHARD_IN
            pieces = []
            for start, stop, which, s0 in ranges:
                a, b = max(lo, start), min(hi, stop)
                if a < b:
                    pieces.append(srcs[which][:, s0 + a - start:s0 + b - start].astype(o_ref.dtype))
            o_ref[d] = pieces[0] if len(pieces) == 1 else jnp.concatenate(pieces, axis=1)

    return pl.pallas_call(
        body, name="w_in_grad_layout", grid=(D // tr,),
        in_specs=[pl.BlockSpec((tr, W_MAIN), lambda i: (i, 0)), pl.BlockSpec((tr, W_SMALL), lambda i: (i, 0))],
        out_specs=pl.BlockSpec((N_DEV, tr, SHARD_IN), lambda i: (0, i, 0)),
        out_shape=jax.ShapeDtypeStruct((N_DEV, D, SHARD_IN), main.dtype), compiler_params=_cp(("parallel",)))(main, small)


QB, KCB, VCB, KSB, VSB, KWB, VWB = 10, 14, 15, 16, 17, 18, 19


def _col256(a, b):
    return a[:, b * 256:(b + 1) * 256]


_EARLY = ["w_in", "cmp_w1_k", "cmp_w1_v"]
_LATE = ["w_out", "w_gate", "w_up", "w_down"]
_FFN = ["w_down", "w_gate", "w_up"]
_MID = ["w_out"]
_LAST = ["cmp_w1_k", "cmp_w1_v", "w_in"]


def _local_step(x, tgt, p, late_weights=None, grads_ready=None):
    S = x.shape[0]
    cos, sin = _rope_tables(S)

    u, rs1 = _rms_fwd(x, p["attn_norm_w"], "attn_norm")
    proj = _mm(u, p["w_main"], "nn", F32, "in_proj", after=p.get("before_in_proj"))
    proj_small = _mm(u, p["w_small"], "nn", F32, "in_proj_small")
    xa = _conv_fwd(proj, p["conv_w"], p["conv_b"])
    y_ssd, y_pre, rs_ssd, hs = _ssd_fwd(proj, proj_small, xa, p["dt_bias"], p["a_log"], p["d_skip"], p["ssd_norm_w"])

    q_rot = _rope([proj], QB, ATT_WIDTH, cos, sin, 1.0, _MXU, "rope_q")
    kv = _kv_prep(proj, cos, sin, _attn_cfg(S, S, "sel")[1])
    rk, rv = _to_rows16(_col256(proj, KCB)), _to_rows16(_col256(proj, VCB))
    k_cmp, hid_k = _compress_fwd(rk, p["cmp_pe_k"], p["cmp_w1_k"], p["cmp_w2_k"])
    v_cmp, hid_v = _compress_fwd(rv, p["cmp_pe_v"], p["cmp_w1_v"], p["cmp_w2_v"])
    n_cmp = k_cmp.shape[1]

    gates = proj_small[:, SSD_HEADS:SSD_HEADS + 3 * N_HEADS].reshape(S, N_KV, GRP, 3).transpose(3, 1, 2, 0)
    o_cmp, lse_cmp, y_att = _attn_fwd(proj, QB, k_cmp, _blocked_t(v_cmp, n_cmp), "cmp", None, gates[0], None, F32, "attn_cmp_fwd")
    sel = _select(proj, QB, k_cmp, lse_cmp)
    o_sel, lse_sel, y_att = _attn_fwd(q_rot, 0, kv["ks_ext"], kv["vs_t"], "sel", sel, gates[1], y_att, F32, "attn_sel_fwd")
    o_win, lse_win, y_att = _attn_fwd(q_rot, 0, kv["kw"], kv["vw_t"], "win", None, gates[2], y_att, _MXU, "attn_win_fwd")

    if late_weights is not None:
        p = {**p, **late_weights(y_att)}
    mixed = jnp.concatenate([y_ssd, y_att], axis=1)
    h1 = _mm(mixed, p["w_out"], "nn", F32, "out_proj", res=x)
    v, rs_ffn = _rms_fwd(h1, p["ffn_norm_w"], "ffn_norm")
    gt, up, act = _ffn_up(v, p["w_gate"], p["w_up"])
    h2 = _mm(act, p["w_down"], "nn", F32, "ffn_down", res=h1)
    loss, dh2, dh2b, d_final_w = _final_loss(h2, p["final_norm_w"], tgt)

    def ready(names):
        return None if grads_ready is None else grads_ready(names, g)

    g = {"final_norm_w": d_final_w}
    g["w_down"] = _mm(act, dh2b, "tn", _MXU, "dw_down")
    dgt, dup = _ffn_dact(dh2b, p["w_down"], gt, up)
    g["w_gate"] = _mm(v, dgt, "tn", _MXU, "dw_gate")
    g["w_up"] = _mm(v, dup, "tn", _MXU, "dw_up")
    dv = _ffn_dv(dgt, dup, p["w_gate"], p["w_up"], ready(_FFN))
    dh1, dh1b, g["ffn_norm_w"] = _rms_bwd(dv, h1, rs_ffn, p["ffn_norm_w"], dh2, "ffn_norm_bwd")
    g["w_out"] = _mm(mixed, dh1b, "tn", _MXU, "dw_out")
    dmixed = _mm(dh1b, p["w_out"], "nt", F32, "dmixed", after=ready(_MID))

    dz, dxa, ddtr, g["dt_bias"], g["a_log"], g["d_skip"], g["ssd_norm_w"] = _ssd_bwd(
        dmixed, proj, proj_small, xa, y_pre, rs_ssd, hs, p["dt_bias"], p["a_log"], p["d_skip"], p["ssd_norm_w"])
    dxbc, g["conv_w"], g["conv_b"] = _conv_bwd(proj, p["conv_w"], p["conv_b"], dxa)

    dyb = SSD_WIDTH // (GRP * HD)
    dq_cmp, dk_cmp, dv_cmp, dg_cmp = _attn_bwd(proj, QB, k_cmp, _blocked_t(k_cmp, n_cmp), v_cmp, o_cmp, lse_cmp, dmixed, dyb,
                                               gates[0], "cmp", None, "attn_cmp_bwd")
    dq_sel, dks, dvs, dg_sel = _attn_bwd(q_rot, 0, kv["ks_ext"], kv["ks_t"], kv["vs"], o_sel, lse_sel, dmixed, dyb, gates[1], "sel",
                                         sel, "attn_sel_bwd")
    dq_win, dkw, dvw, dg_win = _attn_bwd(q_rot, 0, kv["kw"], kv["kw_t"], kv["vw"], o_win, lse_win, dmixed, dyb, gates[2], "win", None,
                                         "attn_win_bwd")
    dgate = jnp.stack([dg_cmp, dg_sel, dg_win]).transpose(3, 1, 2, 0).reshape(S, 3 * N_HEADS)
    drk, g["cmp_w1_k"], g["cmp_w2_k"], g["cmp_pe_k"] = _compress_bwd(rk, p["cmp_pe_k"], p["cmp_w1_k"], p["cmp_w2_k"], hid_k, dk_cmp)
    drv, g["cmp_w1_v"], g["cmp_w2_v"], g["cmp_pe_v"] = _compress_bwd(rv, p["cmp_pe_v"], p["cmp_w1_v"], p["cmp_w2_v"], hid_v, dv_cmp)
    dq = _rope([dq_sel, dq_win], 0, ATT_WIDTH, cos, sin, -1.0, _MXU, "rope_dq", extra=(dq_cmp, 0))
    dkv = _dkv_post(dks, dvs, dkw, dvw, cos, sin)
    dproj = jnp.concatenate([dz, dxbc, dq] + [t.astype(_MXU) for t in (_from_rows16(drk), _from_rows16(drv))] + [dkv], axis=1)
    dsmall = jnp.concatenate([ddtr, dgate, jnp.zeros((S, W_SMALL - SSD_HEADS - 3 * N_HEADS), F32)], axis=1).astype(_MXU)
    g["w_main"] = _mm(u, dproj, "tn", _MXU, "dw_in")
    g["w_small"] = _mm(u, dsmall, "tn", F32, "dw_in_small")
    du = _mm(dproj, p["w_main"], "nt", F32, "du_main", after=ready(_LAST))
    du = _mm(dsmall, p["w_small"], "nt", F32, "du_small", res=du)
    grad_x, _, g["attn_norm_w"] = _rms_bwd(du, x, rs1, p["attn_norm_w"], dh1, "attn_norm_bwd")
    return loss, grad_x, g


MESH_ID = pl.DeviceIdType.MESH


def _my_coords():
    return lax.axis_index("x"), lax.axis_index("y"), lax.axis_index("c")


def _flat_id(px, py, pc):
    return 4 * px + 2 * py + pc


def _peer(k):
    mx, my, mc = _my_coords()
    return (1 - mx if k & 4 else mx, 1 - my if k & 2 else my, 1 - mc if k & 1 else mc)


def _exchange(arrs, scatter, name, after=()):
    n, na = len(arrs), len(after)
    scatter = [scatter] * n if isinstance(scatter, bool) else list(scatter)

    def body(*refs):
        ins, outs = refs[:n], refs[n + na:2 * n + na]
        send_sems, recv_sems, local_sems = refs[2 * n + na:]
        me = _flat_id(*_my_coords())
        copies = []
        for i in range(n):
            src_me = ins[i].at[me] if scatter[i] else ins[i]
            local = pltpu.make_async_copy(src_me, outs[i].at[me], local_sems.at[i])
            local.start()
            copies.append(local)
        for k in range(1, N_DEV):
            peer = _peer(k)
            for i in range(n):
                src = ins[i].at[_flat_id(*peer)] if scatter[i] else ins[i]
                cp = pltpu.make_async_remote_copy(src_ref=src, dst_ref=outs[i].at[me], send_sem=send_sems.at[i * 7 + k - 1],
                                                  recv_sem=recv_sems.at[i * 7 + k - 1], device_id=peer, device_id_type=MESH_ID)
                cp.start()
                copies.append(cp)
        for cp in copies:
            cp.wait()

    any_spec = pl.BlockSpec(memory_space=pl.ANY)
    out_shape = [jax.ShapeDtypeStruct(a.shape if sc else (N_DEV,) + a.shape, a.dtype) for a, sc in zip(arrs, scatter)]
    return pl.pallas_call(
        body, name=name, in_specs=[any_spec] * (n + na), out_specs=[any_spec] * n, out_shape=out_shape,
        scratch_shapes=[pltpu.SemaphoreType.DMA((n * 7,)), pltpu.SemaphoreType.DMA((n * 7,)), pltpu.SemaphoreType.DMA((n,))],
        compiler_params=pltpu.CompilerParams(has_side_effects=True))(*arrs, *after)


def _gather_two_level(arrs, name):
    n = len(arrs)

    def body(*refs):
        ins, outs = refs[:n], refs[n:2 * n]
        send_sems, recv_sems, local_sems = refs[2 * n:]
        x, y, c = _my_coords()
        me, sibling = (x, y, c), (x, y, 1 - c)
        chips = [(1 - x, y), (x, 1 - y), (1 - x, 1 - y)]

        def copy(i, k, block, to, src=None):
            slot = outs[i].at[_flat_id(*block)]
            return pltpu.make_async_remote_copy(src_ref=slot if src is None else src, dst_ref=slot, send_sem=send_sems.at[i * 7 + k],
                                                recv_sem=recv_sems.at[i * 7 + k], device_id=to, device_id_type=MESH_ID)

        mine = [pltpu.make_async_copy(ins[i], outs[i].at[_flat_id(*me)], local_sems.at[i]) for i in range(n)]
        for cp in mine:
            cp.start()
        first = []
        for j, chip in enumerate(chips):
            first += [copy(i, 1 + j, me, (*chip, c), src=ins[i]) for i in range(n)]
        first += [copy(i, 0, me, sibling, src=ins[i]) for i in range(n)]
        for cp in first:
            cp.start()
        passed = []
        for j, chip in enumerate(chips):
            for i in range(n):
                copy(i, 1 + j, (*chip, c), me).wait_recv()
                passed.append(copy(i, 4 + j, (*chip, c), sibling))
                passed[-1].start()
        for i in range(n):
            copy(i, 0, sibling, me).wait_recv()
        for j, chip in enumerate(chips):
            for i in range(n):
                copy(i, 4 + j, (*chip, 1 - c), me).wait_recv()
        for cp in first + passed:
            cp.wait_send()
        for cp in mine:
            cp.wait()

    any_spec = pl.BlockSpec(memory_space=pl.ANY)
    return pl.pallas_call(
        body, name=name, in_specs=[any_spec] * n, out_specs=[any_spec] * n,
        out_shape=[jax.ShapeDtypeStruct((N_DEV,) + a.shape, a.dtype) for a in arrs],
        scratch_shapes=[pltpu.SemaphoreType.DMA((n * 7,)), pltpu.SemaphoreType.DMA((n * 7,)), pltpu.SemaphoreType.DMA((n,))],
        compiler_params=pltpu.CompilerParams(has_side_effects=True))(*arrs)


_HBM = pl.BlockSpec(memory_space=pltpu.HBM)
_SEM = pl.BlockSpec(memory_space=pltpu.SEMAPHORE)
_EFFECT = pltpu.SideEffectType.DATAFLOW_SIDE_EFFECTING


def _split_copies(ins, lands, send_sems, recv_sems, scatter):
    me = _flat_id(*_my_coords())
    out = []
    for k in range(1, N_DEV):
        peer = _peer(k)
        for i in range(len(ins)):
            src = ins[i].at[_flat_id(*peer)] if scatter else ins[i]
            out.append(pltpu.make_async_remote_copy(src_ref=src, dst_ref=lands[i].at[me], send_sem=send_sems.at[i * 7 + k - 1],
                                                    recv_sem=recv_sems.at[i * 7 + k - 1], device_id=peer, device_id_type=MESH_ID))
    return out


def _split_start(arrs, scatter, name, after=()):
    n, na = len(arrs), len(after)

    def body(*refs):
        for cp in _split_copies(refs[:n], refs[n:2 * n], refs[2 * n + na], refs[2 * n + na + 1], scatter):
            cp.start()
        refs[-1][...] = jnp.zeros_like(refs[-1])

    land_shapes = [a.shape if scatter else (N_DEV,) + a.shape for a in arrs]
    out_shape = ((pltpu.SemaphoreType.DMA((n * 7,)), pltpu.SemaphoreType.DMA((n * 7,)))
                 + tuple(pltpu.HBM(a.shape, a.dtype) for a in arrs) + tuple(pltpu.HBM(s, a.dtype) for s, a in zip(land_shapes, arrs))
                 + (jax.ShapeDtypeStruct((8, 128), F32),))
    operands = ([pltpu.with_memory_space_constraint(a, pltpu.HBM) for a in arrs]
                + [pltpu.with_memory_space_constraint(lax.empty(s, a.dtype), pltpu.HBM) for s, a in zip(land_shapes, arrs)])
    res = pl.pallas_call(
        body, name=name, out_shape=out_shape, in_specs=[_HBM] * (2 * n) + [pl.BlockSpec(memory_space=pl.ANY)] * na,
        out_specs=(_SEM, _SEM) + (_HBM,) * (2 * n) + (pl.BlockSpec(memory_space=pltpu.VMEM),),
        input_output_aliases={i: 2 + i for i in range(2 * n)},
        compiler_params=pltpu.CompilerParams(has_side_effects=_EFFECT))(*operands, *after)
    return dict(send=res[0], recv=res[1], ins=list(res[2:2 + n]), lands=list(res[2 + n:2 + 2 * n]), token=res[-1])


def _split_wait(st, scatter, after, name):
    n = len(st["ins"])

    def body(*refs):
        for cp in _split_copies(refs[:n], refs[n:2 * n], refs[2 * n], refs[2 * n + 1], scatter):
            cp.wait_send()
            cp.wait_recv()

    arrs = st["ins"] + st["lands"]
    res = pl.pallas_call(
        body, name=name, out_shape=tuple(pltpu.HBM(a.shape, a.dtype) for a in arrs),
        in_specs=[_HBM] * (2 * n) + [_SEM, _SEM] + [pl.BlockSpec(memory_space=pl.ANY)] * len(after), out_specs=(_HBM,) * (2 * n),
        input_output_aliases={i: i for i in range(2 * n)},
        compiler_params=pltpu.CompilerParams(has_side_effects=_EFFECT))(*arrs, st["send"], st["recv"], *after)
    me = _flat_id(*_my_coords())
    out = []
    for src, land in zip(res[:n], res[n:]):
        own = lax.dynamic_index_in_dim(src, me, 0, keepdims=True) if scatter else src[None]
        out.append(lax.dynamic_update_slice_in_dim(land, own, me, 0))
    return out


def _adam_step(p_ref, w_ref, m_ref, v_ref, g_ref, d_ref, nm_ref, nv_ref):
    g = p_ref[0].astype(F32)
    for j in range(1, p_ref.shape[0]):
        g = g + p_ref[j].astype(F32)
    g_ref[...] = g
    nm = ADAM_B1 * m_ref[...] + (1.0 - ADAM_B1) * g
    nv = ADAM_B2 * v_ref[...] + (1.0 - ADAM_B2) * (g * g)
    nm_ref[...] = nm
    nv_ref[...] = nv
    m_hat = nm / (1.0 - ADAM_B1 ** ADAM_STEP)
    v_hat = nv / (1.0 - ADAM_B2 ** ADAM_STEP)
    d_ref[...] = -ADAM_LR * (m_hat / (jnp.sqrt(v_hat) + ADAM_EPS) + ADAM_WD * w_ref[...])


def _adam_sum(parts, w, m, v, name):
    P, R, C = parts.shape
    tr = _pick(R, (256, 128, 64, 32, 8)) if C <= 1024 else _pick(R, (128, 64, 32, 8))
    blk = pl.BlockSpec((tr, C), lambda i: (i, 0))
    return pl.pallas_call(
        functools.partial(_adam_step), name=name, grid=(R // tr,),
        in_specs=[pl.BlockSpec((P, tr, C), lambda i: (0, i, 0)), blk, blk, blk],
        out_specs=[blk] * 4, out_shape=[jax.ShapeDtypeStruct((R, C), F32)] * 4, compiler_params=_cp(("parallel",)))(parts, w, m, v)


def _adam_small(loss_parts, parts, ws, ms, vs):
    n = len(parts)

    def body(*refs):
        loss_ref, ins, outs, total_ref = refs[0], refs[1:4 * n + 1], refs[4 * n + 1:-1], refs[-1]
        for i in range(n):
            _adam_step(ins[i], ins[n + i], ins[2 * n + i], ins[3 * n + i], *outs[4 * i:4 * i + 4])
        total = loss_ref[0]
        for d in range(1, N_DEV):
            total = total + loss_ref[d]
        total_ref[...] = total

    out_shape = [jax.ShapeDtypeStruct(w.shape, F32) for w in ws for _ in range(4)] + [jax.ShapeDtypeStruct(loss_parts.shape[1:], F32)]
    res = pl.pallas_call(body, name="adam_small", out_shape=out_shape)(loss_parts, *parts, *ws, *ms, *vs)
    return res[-1], [tuple(res[4 * i:4 * i + 4]) for i in range(n)]


_WEIGHTS = ["attn_norm_w", "w_in", "conv_w", "conv_b", "dt_bias", "a_log", "d_skip", "ssd_norm_w", "cmp_w1_k", "cmp_w2_k",
            "cmp_w1_v", "cmp_w2_v", "cmp_pe_k", "cmp_pe_v", "w_out", "ffn_norm_w", "w_gate", "w_up", "w_down", "final_norm_w"]
_BIG = ["w_in", "w_gate", "w_up", "w_down", "w_out", "cmp_w1_k", "cmp_w1_v"]
_COL_SHARDED = ("w_in", "w_gate", "w_up")
_REPLICATED = ["attn_norm_w", "conv_b", "dt_bias", "a_log", "d_skip", "ssd_norm_w", "cmp_pe_k", "cmp_pe_v", "ffn_norm_w",
               "final_norm_w"]
_SMALL_SHARDED = ["conv_w", "cmp_w2_k", "cmp_w2_v"]


def _cols_to_slabs(g):
    R = g.shape[0]
    return g.reshape(R, N_DEV, -1).transpose(1, 0, 2)


def _slabs_to_cols(s):
    return s.transpose(1, 0, 2).reshape(s.shape[1], -1)


def kernel(x, attn_norm_w, w_in, conv_w, conv_b, dt_bias, a_log, d_skip, ssd_norm_w, cmp_w1_k, cmp_w2_k, cmp_w1_v, cmp_w2_v, cmp_pe_k, cmp_pe_v, w_out, ffn_norm_w, w_gate, w_up, w_down, final_norm_w, loss_target, m_attn_norm_w, m_w_in, m_conv_w, m_conv_b, m_dt_bias, m_a_log, m_d_skip, m_ssd_norm_w, m_cmp_w1_k, m_cmp_w2_k, m_cmp_w1_v, m_cmp_w2_v, m_cmp_pe_k, m_cmp_pe_v, m_w_out, m_ffn_norm_w, m_w_gate, m_w_up, m_w_down, m_final_norm_w, v_attn_norm_w, v_w_in, v_conv_w, v_conv_b, v_dt_bias, v_a_log, v_d_skip, v_ssd_norm_w, v_cmp_w1_k, v_cmp_w2_k, v_cmp_w1_v, v_cmp_w2_v, v_cmp_pe_k, v_cmp_pe_v, v_w_out, v_ffn_norm_w, v_w_gate, v_w_up, v_w_down, v_final_norm_w):
    a = dict(locals())

    shard = {n: a[n][0].astype(_MXU) for n in _BIG}
    got = _gather_two_level([shard[n] for n in _EARLY] + [cmp_w2_k[0], cmp_w2_v[0], conv_w[0]], "gather_early")
    st_late = _split_start([shard[n] for n in _LATE], False, "gather_late_start", after=(got[0],))

    def assemble(n, t):
        return _cols_from_slabs(t) if n in _COL_SHARDED else t.reshape(-1, t.shape[-1])

    p = dict(attn_norm_w=attn_norm_w, conv_b=conv_b, dt_bias=dt_bias, a_log=a_log, d_skip=d_skip, ssd_norm_w=ssd_norm_w,
             cmp_pe_k=cmp_pe_k.reshape(1, -1), cmp_pe_v=cmp_pe_v.reshape(1, -1), ffn_norm_w=ffn_norm_w,
             final_norm_w=final_norm_w.reshape(1, -1))

    w_main, w_small = _w_in_from_slabs(got[0])
    p.update(before_in_proj=st_late["token"],
             w_main=w_main, w_small=w_small, cmp_w1_k=assemble("cmp_w1_k", got[1]), cmp_w1_v=assemble("cmp_w1_v", got[2]),
             cmp_w2_k=assemble("cmp_w2_k", got[3]).astype(_MXU), cmp_w2_v=assemble("cmp_w2_v", got[4]).astype(_MXU),
             conv_w=_slabs_to_cols(got[5]))

    def late_weights(after):
        got_late = _split_wait(st_late, False, (after,), "gather_late_wait")
        return {n: assemble(n, t) for n, t in zip(_LATE, got_late)}

    def slabs_of(g, n):
        if n == "w_in":
            return _w_in_to_slabs(g["w_main"], g["w_small"])
        return _slabs_from_cols(g[n]) if n in _COL_SHARDED else g[n].reshape(N_DEV, -1, g[n].shape[-1])

    started = []

    def grads_ready(names, g):
        started.append((names, _split_start([slabs_of(g, n) for n in names], True, "scatter_grads_start_%d" % len(started))))
        return started[-1][1]["token"]

    loss_part, grad_x, g = _local_step(x[0], loss_target[0], p, late_weights, grads_ready)

    out, after = {}, (started[-1][1]["token"],)
    for i, (names, st) in enumerate(started):
        if i == len(started) - 1:
            after = after + (grad_x,)
        received = _split_wait(st, True, after, "scatter_grads_wait_%d" % i)
        for n, parts in zip(names, received):
            out[n] = _adam_sum(parts, a[n][0], a["m_" + n][0], a["v_" + n][0], "adam_" + n)
        after = (out[names[-1]][0],)

    small_names = _REPLICATED + _SMALL_SHARDED
    partials = [g[n] for n in _REPLICATED] + [_cols_to_slabs(g["conv_w"])] + [
        g[n].reshape(N_DEV, -1, g[n].shape[-1]) for n in ("cmp_w2_k", "cmp_w2_v")]
    gathered = _exchange([loss_part] + partials, [False] * (1 + len(_REPLICATED)) + [True] * len(_SMALL_SHARDED),
                         "exchange_small_grads", after=(received[0],))
    shapes2d = [t.shape[1:] for t in gathered[1:]]
    loss, res_small = _adam_small(gathered[0], gathered[1:],
                                  *[[a[pre + n].reshape(s) for n, s in zip(small_names, shapes2d)] for pre in ("", "m_", "v_")])
    for n, r in zip(small_names, res_small):
        out[n] = r

    outs = [loss[0, 0], grad_x[None]]
    for j in range(4):
        for n in _WEIGHTS:
            outs.append(out[n][j].reshape(a[n].shape))
    return tuple(outs)
```

```python
import functools

import numpy as np
import jax
import jax.numpy as jnp
from jax import lax
from jax.experimental import pallas as pl
from jax.experimental.pallas import tpu as pltpu

F32 = jnp.float32
_MXU = jnp.bfloat16

N_DEV = 8
SSD_WIDTH = 1024
ATT_WIDTH = 1024
SSD_HEADS = 16
SSD_P = 64
SSD_N = 128
SSD_L = 128
SSD_G = 2
CONV_CH = 1536
CONV_K = 4
HD = 64
N_HEADS = 16
N_KV = 4
GRP = 4
CMP_HID = 256
SEL_BLOCK = 64
N_SELECT = 16
WINDOW = 512
ROPE_DIM = 16
ROPE_THETA = 500000.0
EPS = 1e-6
NEG = -1e30
FORCE = 1e4
SCALE = HD ** -0.5
D_IN = 5184
W_MAIN = 5120
W_SMALL = 128
VMEM_LIMIT = 52 * 1024 * 1024

ADAM_LR, ADAM_B1, ADAM_B2, ADAM_EPS, ADAM_WD, ADAM_STEP = 0.001, 0.9, 0.999, 1e-08, 0.01, 10


def _pick(n, cands):
    for c in cands:
        if n % c == 0:
            return c
    return n


def _cp(sem=None):
    return pltpu.CompilerParams(dimension_semantics=sem, vmem_limit_bytes=VMEM_LIMIT)


def _sigmoid(x):
    return 1.0 / (1.0 + jnp.exp(-x))


def _dot(a, b, dims, split=None):
    dn = {"nn": (((1,), (0,)), ((), ())), "nt": (((1,), (1,)), ((), ())), "tn": (((0,), (0,)), ((), ()))}[dims]
    mm = lambda x, y: lax.dot_general(x.astype(_MXU), y.astype(_MXU), dn, preferred_element_type=F32)
    if split is None:
        return mm(a, b)
    x = (a if split == "a" else b).astype(F32)
    hi = x.astype(_MXU)
    lo = x - hi.astype(F32)
    return mm(hi, b) + mm(lo, b) if split == "a" else mm(a, hi) + mm(a, lo)


LANE = 128
MM_TILE = 1024
MM_K_WHOLE = 2048
MM_K_STEP = 2816
TN_ACC_ELEMS = 3 * 2 ** 20
TN_K_STEP = 512


def _largest_tile(n, cap):
    if n <= cap:
        return n
    best = LANE
    for t in range(LANE, cap + 1, LANE):
        if n % t == 0:
            best = t
    return best


def _mm_tiles(mode, M, N, K):
    if mode == "tn":
        tm = _largest_tile(M, 2 * MM_TILE)
        return tm, _largest_tile(N, TN_ACC_ELEMS // tm), _largest_tile(K, TN_K_STEP)
    tk = K if K <= MM_K_WHOLE else _largest_tile(K, MM_K_STEP)
    return _largest_tile(M, MM_TILE), _largest_tile(N, MM_TILE), tk


def _mm(a, b, mode, out_dtype, name, res=None, after=None):
    if mode == "nn":
        (M, K), N = a.shape, b.shape[1]
    elif mode == "nt":
        (M, K), N = a.shape, b.shape[0]
    else:
        (K, M), N = a.shape, b.shape[1]
    tm, tn, tk = _mm_tiles(mode, M, N, K)
    nk = K // tk
    a_spec = pl.BlockSpec((tk, tm), lambda i, j, k: (k, i)) if mode == "tn" else pl.BlockSpec((tm, tk), lambda i, j, k: (i, k))
    b_spec = pl.BlockSpec((tn, tk), lambda i, j, k: (j, k)) if mode == "nt" else pl.BlockSpec((tk, tn), lambda i, j, k: (k, j))
    o_spec = pl.BlockSpec((tm, tn), lambda i, j, k: (i, j))

    def finish(r, r_ref, o_ref):
        if res is not None:
            r = r + r_ref[...].astype(F32)
        o_ref[...] = r.astype(out_dtype)

    def body_one_step(*refs):
        a_ref, b_ref, o_ref = refs[0], refs[1], refs[-1]
        finish(_dot(a_ref[...], b_ref[...], mode), refs[2], o_ref)

    def body(*refs):
        a_ref, b_ref, o_ref, acc = refs[0], refs[1], refs[-2], refs[-1]
        k = pl.program_id(2)

        @pl.when(k == 0)
        def _():
            acc[...] = jnp.zeros_like(acc)

        acc[...] += _dot(a_ref[...], b_ref[...], mode)

        @pl.when(k == nk - 1)
        def _():
            finish(acc[...], refs[2], o_ref)

    ins, specs = [a, b], [a_spec, b_spec]
    if res is not None:
        ins.append(res)
        specs.append(o_spec)
    if after is not None:
        ins.append(after)
        specs.append(pl.BlockSpec(memory_space=pl.ANY))
    return pl.pallas_call(
        body_one_step if nk == 1 else body, name=name, grid=(M // tm, N // tn, nk), in_specs=specs, out_specs=o_spec,
        out_shape=jax.ShapeDtypeStruct((M, N), out_dtype), scratch_shapes=[] if nk == 1 else [pltpu.VMEM((tm, tn), F32)],
        compiler_params=_cp(("parallel", "parallel", "arbitrary")))(*ins)


def _ffn_up(v, w_gate, w_up):
    S, D = v.shape
    F = w_gate.shape[1]
    tm, tn = _largest_tile(S, MM_TILE), _largest_tile(F, MM_TILE // 2)

    def body(v_ref, wg_ref, wu_ref, gt_ref, up_ref, act_ref):
        vv = v_ref[...]
        g = _dot(vv, wg_ref[...], "nn")
        u = _dot(vv, wu_ref[...], "nn")
        gt_ref[...] = g
        up_ref[...] = u
        act_ref[...] = (g * _sigmoid(g) * u).astype(act_ref.dtype)

    o_spec = pl.BlockSpec((tm, tn), lambda i, j: (i, j))
    w_spec = pl.BlockSpec((D, tn), lambda i, j: (0, j))
    return pl.pallas_call(
        body, name="ffn_up", grid=(S // tm, F // tn),
        in_specs=[pl.BlockSpec((tm, D), lambda i, j: (i, 0)), w_spec, w_spec], out_specs=[o_spec, o_spec, o_spec],
        out_shape=[jax.ShapeDtypeStruct((S, F), F32), jax.ShapeDtypeStruct((S, F), F32), jax.ShapeDtypeStruct((S, F), _MXU)],
        compiler_params=_cp(("parallel", "parallel")))(v, w_gate, w_up)


def _ffn_dv(dgt, dup, w_gate, w_up, after):
    S, F = dgt.shape
    D = w_gate.shape[0]
    tm, tn, _ = _mm_tiles("nt", S, D, F)
    tk = _largest_tile(F, MM_K_STEP // 2)
    nk = F // tk

    def body(g_ref, u_ref, wg_ref, wu_ref, *rest):
        o_ref, acc = rest[-2], rest[-1]
        k = pl.program_id(2)

        @pl.when(k == 0)
        def _():
            acc[...] = jnp.zeros_like(acc)

        acc[...] += _dot(g_ref[...], wg_ref[...], "nt") + _dot(u_ref[...], wu_ref[...], "nt")

        @pl.when(k == nk - 1)
        def _():
            o_ref[...] = acc[...]

    a_spec = pl.BlockSpec((tm, tk), lambda i, j, k: (i, k))
    w_spec = pl.BlockSpec((tn, tk), lambda i, j, k: (j, k))
    ins, specs = [dgt, dup, w_gate, w_up], [a_spec, a_spec, w_spec, w_spec]
    if after is not None:
        ins.append(after)
        specs.append(pl.BlockSpec(memory_space=pl.ANY))
    return pl.pallas_call(
        body, name="ffn_dv", grid=(S // tm, D // tn, nk), in_specs=specs, out_specs=pl.BlockSpec((tm, tn), lambda i, j, k: (i, j)),
        out_shape=jax.ShapeDtypeStruct((S, D), F32), scratch_shapes=[pltpu.VMEM((tm, tn), F32)],
        compiler_params=_cp(("parallel", "parallel", "arbitrary")))(*ins)


def _ffn_dact(dh2, w_down, gt, up):
    S, D = dh2.shape
    F = w_down.shape[0]
    tm, tn = _largest_tile(S, MM_TILE), _largest_tile(F, MM_TILE // 2)

    def body(d_ref, w_ref, gt_ref, up_ref, dg_ref, du_ref):
        da, g, u = _dot(d_ref[...], w_ref[...], "nt"), gt_ref[...], up_ref[...]
        s = _sigmoid(g)
        dg_ref[...] = (da * u * (s * (1.0 + g * (1.0 - s)))).astype(dg_ref.dtype)
        du_ref[...] = (da * (g * s)).astype(du_ref.dtype)

    o_spec = pl.BlockSpec((tm, tn), lambda i, j: (i, j))
    return pl.pallas_call(
        body, name="ffn_dact", grid=(S // tm, F // tn),
        in_specs=[pl.BlockSpec((tm, D), lambda i, j: (i, 0)), pl.BlockSpec((tn, D), lambda i, j: (j, 0)), o_spec, o_spec],
        out_specs=[o_spec, o_spec],
        out_shape=[jax.ShapeDtypeStruct((S, F), _MXU), jax.ShapeDtypeStruct((S, F), _MXU)],
        compiler_params=_cp(("parallel", "parallel")))(dh2, w_down, gt, up)


def _rms_fwd(x, w, name):
    S, D = x.shape
    tr = _pick(S, (256, 128))

    def body(x_ref, w_ref, xn_ref, rs_ref):
        xv = x_ref[...]
        rs = lax.rsqrt(jnp.mean(xv * xv, axis=-1, keepdims=True) + EPS)
        xn_ref[...] = ((xv * rs) * w_ref[...]).astype(xn_ref.dtype)
        rs_ref[...] = rs

    return pl.pallas_call(
        body, name=name, grid=(S // tr,),
        in_specs=[pl.BlockSpec((tr, D), lambda i: (i, 0)), pl.BlockSpec((1, D), lambda i: (0, 0))],
        out_specs=[pl.BlockSpec((tr, D), lambda i: (i, 0)), pl.BlockSpec((tr, 1), lambda i: (i, 0))],
        out_shape=[jax.ShapeDtypeStruct((S, D), _MXU), jax.ShapeDtypeStruct((S, 1), F32)],
        compiler_params=_cp(("parallel",)))(x, w)


def _rms_bwd(dyn, x, rs, w, res, name):
    S, D = x.shape
    tr = _pick(S, (256, 128))

    def body(dy_ref, x_ref, rs_ref, w_ref, res_ref, dx_ref, dxb_ref, dw_ref):
        @pl.when(pl.program_id(0) == 0)
        def _():
            dw_ref[...] = jnp.zeros_like(dw_ref)

        dy, r = dy_ref[...].astype(F32), rs_ref[...]
        xhat = x_ref[...] * r
        dw_ref[...] += jnp.sum(dy * xhat, axis=0, keepdims=True)
        dxhat = dy * w_ref[...]
        dx = res_ref[...] + r * (dxhat - xhat * jnp.mean(dxhat * xhat, axis=-1, keepdims=True))
        dx_ref[...] = dx
        dxb_ref[...] = dx.astype(dxb_ref.dtype)

    row = pl.BlockSpec((tr, D), lambda i: (i, 0))
    vec = pl.BlockSpec((1, D), lambda i: (0, 0))
    return pl.pallas_call(
        body, name=name, grid=(S // tr,),
        in_specs=[row, row, pl.BlockSpec((tr, 1), lambda i: (i, 0)), vec, row], out_specs=[row, row, vec],
        out_shape=[jax.ShapeDtypeStruct((S, D), F32), jax.ShapeDtypeStruct((S, D), _MXU), jax.ShapeDtypeStruct((1, D), F32)],
        compiler_params=_cp(("arbitrary",)))(dyn, x, rs, w, res)


def _final_loss(h2, w, tgt):
    S, D = h2.shape
    tr = _pick(S, (256, 128))

    def body(h_ref, w_ref, t_ref, loss_ref, dh_ref, dhb_ref, dw_ref):
        @pl.when(pl.program_id(0) == 0)
        def _():
            dw_ref[...] = jnp.zeros_like(dw_ref)
            loss_ref[...] = jnp.zeros_like(loss_ref)

        hv, wv = h_ref[...], w_ref[...]
        rs = lax.rsqrt(jnp.mean(hv * hv, axis=-1, keepdims=True) + EPS)
        xhat = hv * rs
        err = xhat * wv - t_ref[...]
        row = jnp.mean(err * err, axis=-1, keepdims=True)
        loss_ref[...] += jnp.broadcast_to(0.5 * jnp.sum(row, axis=0, keepdims=True), loss_ref.shape)
        dy = err * (1.0 / D)
        dw_ref[...] += jnp.sum(dy * xhat, axis=0, keepdims=True)
        dxhat = dy * wv
        dh = rs * (dxhat - xhat * jnp.mean(dxhat * xhat, axis=-1, keepdims=True))
        dh_ref[...] = dh
        dhb_ref[...] = dh.astype(dhb_ref.dtype)

    row = pl.BlockSpec((tr, D), lambda i: (i, 0))
    vec = pl.BlockSpec((1, D), lambda i: (0, 0))
    return pl.pallas_call(
        body, name="final_loss", grid=(S // tr,), in_specs=[row, vec, row],
        out_specs=[pl.BlockSpec((1, LANE), lambda i: (0, 0)), row, row, vec],
        out_shape=[jax.ShapeDtypeStruct((1, LANE), F32), jax.ShapeDtypeStruct((S, D), F32), jax.ShapeDtypeStruct((S, D), _MXU),
                   jax.ShapeDtypeStruct((1, D), F32)],
        compiler_params=_cp(("arbitrary",)))(h2, w, tgt)


def _shift_rows(x, k, rows):
    if k == 0:
        return x
    S = x.shape[0]
    r = pltpu.roll(x, k % S, axis=0)
    ok = (rows >= k) if k > 0 else (rows < S + k)
    return jnp.where(ok, r, 0.0)


XBC_COL0 = SSD_WIDTH // 128


def _conv_fwd(proj, conv_w, conv_b):
    S = proj.shape[0]
    nct = CONV_CH // 128

    def body(x_ref, w_ref, b_ref, o_ref):
        x = x_ref[...]
        rows = lax.broadcasted_iota(jnp.int32, x.shape, 0)
        c = b_ref[...] + w_ref[3:4, :] * x
        for k in range(1, CONV_K):
            c = c + w_ref[3 - k:4 - k, :] * _shift_rows(x, k, rows)
        o_ref[...] = c * _sigmoid(c)

    return pl.pallas_call(
        body, name="conv_fwd", grid=(nct,),
        in_specs=[pl.BlockSpec((S, 128), lambda j: (0, XBC_COL0 + j)), pl.BlockSpec((CONV_K, 128), lambda j: (0, j)),
                  pl.BlockSpec((1, 128), lambda j: (0, j))],
        out_specs=pl.BlockSpec((S, 128), lambda j: (0, j)),
        out_shape=jax.ShapeDtypeStruct((S, CONV_CH), F32), compiler_params=_cp(("parallel",)))(proj, conv_w, conv_b)


def _conv_bwd(proj, conv_w, conv_b, dxa):
    S = proj.shape[0]
    nct = CONV_CH // 128

    def body(x_ref, w_ref, b_ref, d_ref, dx_ref, dw_ref, db_ref):
        x = x_ref[...]
        rows = lax.broadcasted_iota(jnp.int32, x.shape, 0)
        xs = [_shift_rows(x, k, rows) for k in range(CONV_K)]
        c = b_ref[...] + w_ref[3:4, :] * x
        for k in range(1, CONV_K):
            c = c + w_ref[3 - k:4 - k, :] * xs[k]
        s = _sigmoid(c)
        dc = d_ref[...] * (s * (1.0 + c * (1.0 - s)))
        dx = w_ref[3:4, :] * dc
        for k in range(1, CONV_K):
            dx = dx + w_ref[3 - k:4 - k, :] * _shift_rows(dc, -k, rows)
        dx_ref[...] = dx.astype(dx_ref.dtype)
        for k in range(CONV_K):
            dw_ref[3 - k:4 - k, :] = jnp.sum(dc * xs[k], axis=0, keepdims=True)
        db_ref[...] = jnp.sum(dc, axis=0, keepdims=True)

    col = pl.BlockSpec((S, 128), lambda j: (0, j))
    return pl.pallas_call(
        body, name="conv_bwd", grid=(nct,),
        in_specs=[pl.BlockSpec((S, 128), lambda j: (0, XBC_COL0 + j)), pl.BlockSpec((CONV_K, 128), lambda j: (0, j)),
                  pl.BlockSpec((1, 128), lambda j: (0, j)), col],
        out_specs=[col, pl.BlockSpec((CONV_K, 128), lambda j: (0, j)), pl.BlockSpec((1, 128), lambda j: (0, j))],
        out_shape=[jax.ShapeDtypeStruct((S, CONV_CH), _MXU), jax.ShapeDtypeStruct((CONV_K, CONV_CH), F32),
                   jax.ShapeDtypeStruct((1, CONV_CH), F32)],
        compiler_params=_cp(("parallel",)))(proj, conv_w, conv_b, dxa)


def _ssd_consts():
    L = SSD_L
    r = lax.broadcasted_iota(jnp.int32, (L, L), 0)
    c = lax.broadcasted_iota(jnp.int32, (L, L), 1)
    causal = r >= c
    upper = (r <= c).astype(F32)
    hr = lax.broadcasted_iota(jnp.int32, (SSD_HEADS, SSD_WIDTH), 0)
    hc = lax.broadcasted_iota(jnp.int32, (SSD_HEADS, SSD_WIDTH), 1)
    expand = (lax.shift_right_logical(hc, 6) == hr).astype(F32)
    return causal, causal.astype(F32), upper, expand


def _softplus(x):
    return jnp.maximum(x, 0.0) + jnp.log(1.0 + jnp.exp(-jnp.abs(x)))


def _ssd_scalars(dtr, dt_bias, a_log, tri, upper, expand):
    dt = _softplus(dtr + dt_bias)
    A = -jnp.exp(a_log)
    adt = dt * A
    acum = _dot(tri, adt, "nn", split="b")
    acum_t = _dot(adt, upper, "tn", split="a")
    alast = acum[SSD_L - 1:SSD_L, :]
    e = jnp.exp(acum)
    wdec = jnp.exp(alast - acum)
    gam = jnp.exp(alast)
    ex = lambda t: _dot(t, expand, "nn", split="a")
    gam8 = jnp.broadcast_to(gam, (8, SSD_HEADS))
    return dt, A, acum, acum_t, e, wdec, gam, ex(dt), ex(e), ex(wdec), ex(gam8)[0:1, :]


def _ssd_fwd(proj, proj_small, xa, dt_bias, a_log, d_skip, norm_w):
    S = proj.shape[0]
    L, N, W = SSD_L, SSD_N, SSD_WIDTH
    nc = S // L

    def body(z_ref, xa_ref, dtr_ref, dtb_ref, al_ref, dsk_ref, nw_ref, yo_ref, y_ref, rs_ref, hs_ref, h_scr, y_scr):
        @pl.when(pl.program_id(0) == 0)
        def _():
            h_scr[...] = jnp.zeros_like(h_scr)

        causal, tri, upper, expand = _ssd_consts()
        dt, A, acum, acum_t, e, wdec, gam, dtE, eE, wE, gamE = _ssd_scalars(dtr_ref[:, 0:SSD_HEADS], dtb_ref[...], al_ref[...], tri, upper, expand)
        xs = xa_ref[:, 0:W]
        X = xs * dtE
        XW = X * wE
        hs_ref[0] = h_scr[...]
        for g in range(SSD_G):
            gs = slice(g * 512, (g + 1) * 512)
            Bg = xa_ref[:, W + g * N:W + (g + 1) * N]
            Cg = xa_ref[:, W + SSD_G * N + g * N:W + SSD_G * N + (g + 1) * N]
            Hg = h_scr[:, gs]
            CB = _dot(Cg, Bg, "nt")
            yoff = _dot(Cg, Hg, "nn") * eE[:, gs]
            st = _dot(Bg, XW[:, gs], "tn")
            for j in range(8):
                h = g * 8 + j
                hsl = slice(h * SSD_P, (h + 1) * SSD_P)
                lam = jnp.exp(jnp.where(causal, acum[:, h:h + 1] - acum_t[h:h + 1, :], -jnp.inf))
                y_scr[:, hsl] = _dot(CB * lam, X[:, hsl], "nn") + yoff[:, j * SSD_P:(j + 1) * SSD_P]
            h_scr[:, gs] = gamE[:, gs] * Hg + st
        dskE = _dot(jnp.broadcast_to(dsk_ref[...], (8, SSD_HEADS)), expand, "nn", split="a")[0:1, :]
        y = y_scr[...] + dskE * xs
        y_ref[...] = y
        zv = z_ref[...]
        yg = y * (zv * _sigmoid(zv))
        rs = lax.rsqrt(jnp.mean(yg * yg, axis=-1, keepdims=True) + EPS)
        rs_ref[...] = rs
        yo_ref[...] = ((yg * rs) * nw_ref[...]).astype(yo_ref.dtype)

    p16 = pl.BlockSpec((1, SSD_HEADS), lambda c: (0, 0))
    return pl.pallas_call(
        body, name="ssd_fwd", grid=(nc,),
        in_specs=[pl.BlockSpec((L, W), lambda c: (c, 0)), pl.BlockSpec((L, CONV_CH), lambda c: (c, 0)),
                  pl.BlockSpec((L, W_SMALL), lambda c: (c, 0)), p16, p16, p16, pl.BlockSpec((1, W), lambda c: (0, 0))],
        out_specs=[pl.BlockSpec((L, W), lambda c: (c, 0)), pl.BlockSpec((L, W), lambda c: (c, 0)),
                   pl.BlockSpec((L, 1), lambda c: (c, 0)), pl.BlockSpec((1, N, W), lambda c: (c, 0, 0))],
        out_shape=[jax.ShapeDtypeStruct((S, W), _MXU), jax.ShapeDtypeStruct((S, W), F32), jax.ShapeDtypeStruct((S, 1), F32),
                   jax.ShapeDtypeStruct((nc, N, W), F32)],
        scratch_shapes=[pltpu.VMEM((N, W), F32), pltpu.VMEM((L, W), F32)],
        compiler_params=_cp(("arbitrary",)))(proj, xa, proj_small, dt_bias, a_log, d_skip, norm_w)


def _ssd_bwd(dmixed, proj, proj_small, xa, y, rs2, hs, dt_bias, a_log, d_skip, norm_w):
    S = proj.shape[0]
    L, N, W, H = SSD_L, SSD_N, SSD_WIDTH, SSD_HEADS
    nc = S // L

    def body(dyo_ref, z_ref, xa_ref, dtr_ref, y_ref, rs_ref, hs_ref, dtb_ref, al_ref, dsk_ref, nw_ref,
             dz_ref, dxa_ref, ddtr_ref, ddtb_ref, dal_ref, ddsk_ref, dnw_ref, dh_scr, dx_scr):
        @pl.when(pl.program_id(0) == 0)
        def _():
            dh_scr[...] = jnp.zeros_like(dh_scr)
            ddtb_ref[...] = jnp.zeros_like(ddtb_ref)
            dal_ref[...] = jnp.zeros_like(dal_ref)
            ddsk_ref[...] = jnp.zeros_like(ddsk_ref)
            dnw_ref[...] = jnp.zeros_like(dnw_ref)

        causal, tri, upper, expand = _ssd_consts()
        heads = lambda t: _dot(t, expand, "nt", split="a")
        onehot = lambda h: (lax.broadcasted_iota(jnp.int32, (1, H), 1) == h).astype(F32)

        zv, yv, rs = z_ref[...], y_ref[...], rs_ref[...]
        sz = _sigmoid(zv)
        zs = zv * sz
        xhat = (yv * zs) * rs
        dyo = dyo_ref[...].astype(F32)
        dnw_ref[...] += jnp.sum(dyo * xhat, axis=0, keepdims=True)
        dxhat = dyo * nw_ref[...]
        dyg = rs * (dxhat - xhat * jnp.mean(dxhat * xhat, axis=-1, keepdims=True))
        dz_ref[...] = (dyg * yv * (sz * (1.0 + zv * (1.0 - sz)))).astype(dz_ref.dtype)
        dy = dyg * zs

        dtr = dtr_ref[:, 0:H]
        dt, A, acum, acum_t, e, wdec, gam, dtE, eE, wE, gamE = _ssd_scalars(dtr, dtb_ref[...], al_ref[...], tri, upper, expand)
        xs = xa_ref[:, 0:W]
        X = xs * dtE
        XW = X * wE
        dskE = _dot(jnp.broadcast_to(dsk_ref[...], (8, H)), expand, "nn", split="a")[0:1, :]
        ddsk_ref[...] += heads(jnp.broadcast_to(jnp.sum(dy * xs, axis=0, keepdims=True), (8, W)))[0:1, :]

        dYe = dy * eE
        dacum = jnp.zeros((L, H), F32)
        de_full = []
        dw_full = []
        dgam_full = []
        for g in range(SSD_G):
            gs = slice(g * 512, (g + 1) * 512)
            Bg = xa_ref[:, W + g * N:W + (g + 1) * N]
            Cg = xa_ref[:, W + SSD_G * N + g * N:W + SSD_G * N + (g + 1) * N]
            Hg = hs_ref[0, :, gs]
            dHn = dh_scr[:, gs]
            CH = _dot(Cg, Hg, "nn")
            de_full.append(dy[:, gs] * CH)
            dC = _dot(dYe[:, gs], Hg, "nt")
            dHs = gamE[:, gs] * dHn + _dot(Cg, dYe[:, gs], "tn")
            dgam_full.append(jnp.sum(dHn * Hg, axis=0, keepdims=True))
            BdS = _dot(Bg, dHn, "nn")
            dB = _dot(XW[:, gs], dHn, "nt")
            dx_scr[:, gs] = BdS * wE[:, gs]
            dw_full.append(BdS * X[:, gs])
            CB = _dot(Cg, Bg, "nt")
            dCB = jnp.zeros((L, L), F32)
            for j in range(8):
                h = g * 8 + j
                hsl = slice(h * SSD_P, (h + 1) * SSD_P)
                lam = jnp.exp(jnp.where(causal, acum[:, h:h + 1] - acum_t[h:h + 1, :], -jnp.inf))
                M = CB * lam
                dM = _dot(dy[:, hsl], X[:, hsl], "nt")
                dx_scr[:, hsl] += _dot(M, dy[:, hsl], "tn")
                dCB = dCB + dM * lam
                Q = dM * M
                rowsum = jnp.sum(Q, axis=1, keepdims=True)
                colsum = _dot(Q, jnp.ones((L, 8), F32), "tn", split="a")[:, 0:1]
                dacum = dacum + (rowsum - colsum) * onehot(h)
            dC = dC + _dot(dCB, Bg, "nn")
            dB = dB + _dot(dCB, Cg, "tn")
            dxa_ref[:, W + g * N:W + (g + 1) * N] = dB
            dxa_ref[:, W + SSD_G * N + g * N:W + SSD_G * N + (g + 1) * N] = dC
            dh_scr[:, gs] = dHs

        de16 = heads(jnp.concatenate(de_full, axis=1))
        dw16 = heads(jnp.concatenate(dw_full, axis=1))
        dgam16 = heads(jnp.broadcast_to(jnp.concatenate(dgam_full, axis=1), (8, W)))[0:1, :]
        dacum = dacum + de16 * e - dw16 * wdec
        dlast = jnp.sum(dw16 * wdec, axis=0, keepdims=True) + dgam16 * gam
        lastrow = (lax.broadcasted_iota(jnp.int32, (L, 1), 0) == L - 1).astype(F32)
        dacum = dacum + lastrow * dlast
        da = _dot(tri, dacum, "tn", split="b")
        dX = dx_scr[...]
        ddt = da * A + heads(dX * xs)
        dA = jnp.sum(da * dt, axis=0, keepdims=True)
        dal_ref[...] += dA * A
        ddtr = ddt * _sigmoid(dtr + dtb_ref[...])
        ddtb_ref[...] += jnp.sum(ddtr, axis=0, keepdims=True)
        ddtr_ref[...] = ddtr
        dxa_ref[:, 0:W] = dX * dtE + dy * dskE

    p16 = pl.BlockSpec((1, H), lambda c: (0, 0))
    rev = lambda c: (nc - 1 - c, 0)
    return pl.pallas_call(
        body, name="ssd_bwd", grid=(nc,),
        in_specs=[pl.BlockSpec((L, W), rev), pl.BlockSpec((L, W), rev), pl.BlockSpec((L, CONV_CH), rev),
                  pl.BlockSpec((L, W_SMALL), rev), pl.BlockSpec((L, W), rev), pl.BlockSpec((L, 1), rev),
                  pl.BlockSpec((1, N, W), lambda c: (nc - 1 - c, 0, 0)), p16, p16, p16, pl.BlockSpec((1, W), lambda c: (0, 0))],
        out_specs=[pl.BlockSpec((L, W), rev), pl.BlockSpec((L, CONV_CH), rev), pl.BlockSpec((L, H), rev),
                   p16, p16, p16, pl.BlockSpec((1, W), lambda c: (0, 0))],
        out_shape=[jax.ShapeDtypeStruct((S, W), _MXU), jax.ShapeDtypeStruct((S, CONV_CH), F32), jax.ShapeDtypeStruct((S, H), F32),
                   jax.ShapeDtypeStruct((1, H), F32), jax.ShapeDtypeStruct((1, H), F32), jax.ShapeDtypeStruct((1, H), F32),
                   jax.ShapeDtypeStruct((1, W), F32)],
        scratch_shapes=[pltpu.VMEM((N, W), F32), pltpu.VMEM((L, W), F32)],
        compiler_params=_cp(("arbitrary",)))(dmixed, proj, xa, proj_small, y, rs2, hs, dt_bias, a_log, d_skip, norm_w)


def _rope_tables(S):
    inv = 1.0 / (ROPE_THETA ** (jnp.arange(0, ROPE_DIM, 2, dtype=F32) / ROPE_DIM))
    ang = jnp.arange(S, dtype=F32)[:, None] * inv[None, :]
    cos, sin = jnp.cos(ang), jnp.sin(ang)
    half = ROPE_DIM // 2
    c64 = jnp.concatenate([cos, cos, jnp.ones((S, HD - ROPE_DIM), F32)], axis=1)
    s64 = jnp.concatenate([sin, sin, jnp.zeros((S, HD - ROPE_DIM), F32)], axis=1)
    del half
    return jnp.concatenate([c64, c64], axis=1), jnp.concatenate([s64, s64], axis=1)


def _rope(xs, blk0, width, cos, sin, sign, out_dtype, name, extra=None):
    S = xs[0].shape[0]
    tr = _pick(S, (512, 256, 128))
    nx = len(xs)

    def body(*refs):
        x_refs, c_ref, s_ref = refs[:nx], refs[nx], refs[nx + 1]
        e_ref = refs[nx + 2] if extra is not None else None
        o_ref = refs[-1]
        cv, sv = c_ref[...], s_ref[...] * sign
        lane = lax.broadcasted_iota(jnp.int32, (tr, 128), 1)
        first = (lane & (HD - 1)) < (ROPE_DIM // 2)
        for j in range(bw // 128):
            cs = slice(j * 128, (j + 1) * 128)
            xv = x_refs[0][:, cs].astype(F32)
            for r in x_refs[1:]:
                xv = xv + r[:, cs].astype(F32)
            out = _rotate128(xv, cv, sv, first)
            if extra is not None:
                out = out + e_ref[:, cs].astype(F32)
            o_ref[:, cs] = out.astype(out_dtype)

    bw = 512
    assert width % bw == 0 and (blk0 * 256) % bw == 0
    b0 = blk0 * 256 // bw
    t128 = pl.BlockSpec((tr, 128), lambda i, j: (i, 0))
    oblk = pl.BlockSpec((tr, bw), lambda i, j: (i, j))
    specs = [pl.BlockSpec((tr, bw), lambda i, j: (i, b0 + j))] * nx + [t128, t128]
    ins = list(xs) + [cos, sin]
    if extra is not None:
        assert (extra[1] * 256) % bw == 0
        ins.append(extra[0])
        eb = extra[1] * 256 // bw
        specs.append(pl.BlockSpec((tr, bw), lambda i, j: (i, eb + j)))
    return pl.pallas_call(
        body, name=name, grid=(S // tr, width // bw), in_specs=specs, out_specs=oblk,
        out_shape=jax.ShapeDtypeStruct((S, width), out_dtype), compiler_params=_cp(("parallel", "parallel")))(*ins)


def _rotate128(xv, cv, sv, first):
    rot = jnp.where(first, -pltpu.roll(xv, 128 - ROPE_DIM // 2, axis=1), pltpu.roll(xv, ROPE_DIM // 2, axis=1))
    return xv * cv + rot * sv


def _kv_prep(proj, cos, sin, tk):
    S = proj.shape[0]
    NB = S // SEL_BLOCK

    def body(ks_ref, vs_ref, kw_ref, vw_ref, c_ref, s_ref, *outs):
        cv, sv = c_ref[...], s_ref[...]
        lane = lax.broadcasted_iota(jnp.int32, (tk, 128), 1)
        first = (lane & (HD - 1)) < (ROPE_DIM // 2)
        key = pl.program_id(0) * tk + lax.broadcasted_iota(jnp.int32, (tk, NB), 0)
        onehot = (lax.shift_right_logical(key, 6) == lax.broadcasted_iota(jnp.int32, (tk, NB), 1)).astype(F32)
        for j, (ref, rotated) in enumerate(((ks_ref, True), (vs_ref, False), (kw_ref, True), (vw_ref, False))):
            nat, blk = outs[2 * j], outs[2 * j + 1]
            for half in range(2):
                xv = ref[:, half * 128:(half + 1) * 128]
                if rotated:
                    xv = _rotate128(xv, cv, sv, first)
                for e in range(2):
                    h = 2 * half + e
                    piece = xv[:, e * HD:(e + 1) * HD]
                    nat[h] = (jnp.concatenate([piece, onehot], axis=1) if j == 0 else piece).astype(nat.dtype)
                    blk[h, 0] = piece.T.astype(blk.dtype)

    col = lambda b: pl.BlockSpec((tk, 256), lambda i: (i, b))
    t128 = pl.BlockSpec((tk, 128), lambda i: (i, 0))
    nat_spec = lambda w: pl.BlockSpec((N_KV, tk, w), lambda i: (0, i, 0))
    blk_spec = pl.BlockSpec((N_KV, 1, HD, tk), lambda i: (0, i, 0, 0))
    nat_shape = lambda w: jax.ShapeDtypeStruct((N_KV, S, w), _MXU)
    blk_shape = jax.ShapeDtypeStruct((N_KV, S // tk, HD, tk), _MXU)
    widths = (HD + NB, HD, HD, HD)
    res = pl.pallas_call(
        body, name="kv_prep", grid=(S // tk,), in_specs=[col(KSB), col(VSB), col(KWB), col(VWB), t128, t128],
        out_specs=[s for w in widths for s in (nat_spec(w), blk_spec)],
        out_shape=[s for w in widths for s in (nat_shape(w), blk_shape)],
        compiler_params=_cp(("parallel",)))(proj, proj, proj, proj, cos, sin)
    return dict(ks_ext=res[0], ks_t=res[1], vs=res[2], vs_t=res[3], kw=res[4], kw_t=res[5], vw=res[6], vw_t=res[7])


def _dkv_post(dks, dvs, dkw, dvw, cos, sin):
    S = dks.shape[1]
    tr = _pick(S, (512, 256, 128))

    def body(dks_ref, dvs_ref, dkw_ref, dvw_ref, c_ref, s_ref, o_ref):
        cv, sv = c_ref[...], -s_ref[...]
        lane = lax.broadcasted_iota(jnp.int32, (tr, 128), 1)
        first = (lane & (HD - 1)) < (ROPE_DIM // 2)
        for j, (ref, rotated) in enumerate(((dks_ref, True), (dvs_ref, False), (dkw_ref, True), (dvw_ref, False))):
            for half in range(2):
                xv = jnp.concatenate([ref[2 * half], ref[2 * half + 1]], axis=1)
                if rotated:
                    xv = _rotate128(xv, cv, sv, first)
                o_ref[:, j * 256 + half * 128:j * 256 + (half + 1) * 128] = xv.astype(o_ref.dtype)

    hm = pl.BlockSpec((N_KV, tr, HD), lambda i: (0, i, 0))
    t128 = pl.BlockSpec((tr, 128), lambda i: (i, 0))
    return pl.pallas_call(
        body, name="dkv_post", grid=(S // tr,), in_specs=[hm, hm, hm, hm, t128, t128],
        out_specs=pl.BlockSpec((tr, 4 * 256), lambda i: (i, 0)), out_shape=jax.ShapeDtypeStruct((S, 4 * 256), _MXU),
        compiler_params=_cp(("parallel",)))(dks, dvs, dkw, dvw, cos, sin)


def _compress_fwd(R, pe, w1, w2):
    NC = R.shape[1]
    half = 16 * HD

    def body(r_ref, pe_ref, w1_ref, w2_ref, o_ref, hid_ref):
        r = r_ref[0]
        a = _dot(r + pe_ref[:, 0:half], w1_ref[0:half, :], "nn")
        b = _dot(r + pe_ref[:, half:2 * half], w1_ref[half:2 * half, :], "nn")
        hid = a + pltpu.roll(b, NC - 1, axis=0)
        hid_ref[0] = hid
        out = _dot(hid * _sigmoid(hid), w2_ref[...], "nn")
        rows = lax.broadcasted_iota(jnp.int32, out.shape, 0)
        o_ref[0] = jnp.where(rows < NC - 1, out, 0.0).astype(o_ref.dtype)

    return pl.pallas_call(
        body, name="compress_fwd", grid=(N_KV,),
        in_specs=[pl.BlockSpec((1, NC, half), lambda h: (h, 0, 0)), pl.BlockSpec((1, 2 * half), lambda h: (0, 0)),
                  pl.BlockSpec((2 * half, CMP_HID), lambda h: (0, 0)), pl.BlockSpec((CMP_HID, HD), lambda h: (0, 0))],
        out_specs=[pl.BlockSpec((1, NC, HD), lambda h: (h, 0, 0)), pl.BlockSpec((1, NC, CMP_HID), lambda h: (h, 0, 0))],
        out_shape=[jax.ShapeDtypeStruct((N_KV, NC, HD), _MXU), jax.ShapeDtypeStruct((N_KV, NC, CMP_HID), F32)],
        compiler_params=_cp(("parallel",)))(R, pe, w1, w2)


def _compress_bwd(R, pe, w1, w2, hid, dout):
    NC = R.shape[1]
    half = 16 * HD

    def body(r_ref, pe_ref, w1_ref, w2_ref, hid_ref, do_ref, dr_ref, dw1_ref, dw2_ref, dpe_ref):
        @pl.when(pl.program_id(0) == 0)
        def _():
            dw1_ref[...] = jnp.zeros_like(dw1_ref)
            dw2_ref[...] = jnp.zeros_like(dw2_ref)
            dpe_ref[...] = jnp.zeros_like(dpe_ref)

        r, hv, do = r_ref[0], hid_ref[0], do_ref[0]
        s = _sigmoid(hv)
        dw2_ref[...] += _dot(hv * s, do, "tn")
        dhid = _dot(do, w2_ref[...], "nt") * (s * (1.0 + hv * (1.0 - s)))
        rows = lax.broadcasted_iota(jnp.int32, dhid.shape, 0)
        dhid = jnp.where(rows < NC - 1, dhid, 0.0)
        dhid_dn = pltpu.roll(dhid, 1, axis=0)
        dw1_ref[0:half, :] += _dot(r + pe_ref[:, 0:half], dhid, "tn")
        dw1_ref[half:2 * half, :] += _dot(r + pe_ref[:, half:2 * half], dhid_dn, "tn")
        dxt = _dot(dhid, w1_ref[0:half, :], "nt")
        dxb = _dot(dhid_dn, w1_ref[half:2 * half, :], "nt")
        dr_ref[0] = dxt + dxb
        dpe_ref[:, 0:half] += jnp.sum(dxt, axis=0, keepdims=True)
        dpe_ref[:, half:2 * half] += jnp.sum(dxb, axis=0, keepdims=True)

    return pl.pallas_call(
        body, name="compress_bwd", grid=(N_KV,),
        in_specs=[pl.BlockSpec((1, NC, half), lambda h: (h, 0, 0)), pl.BlockSpec((1, 2 * half), lambda h: (0, 0)),
                  pl.BlockSpec((2 * half, CMP_HID), lambda h: (0, 0)), pl.BlockSpec((CMP_HID, HD), lambda h: (0, 0)),
                  pl.BlockSpec((1, NC, CMP_HID), lambda h: (h, 0, 0)), pl.BlockSpec((1, NC, HD), lambda h: (h, 0, 0))],
        out_specs=[pl.BlockSpec((1, NC, half), lambda h: (h, 0, 0)), pl.BlockSpec((2 * half, CMP_HID), lambda h: (0, 0)),
                   pl.BlockSpec((CMP_HID, HD), lambda h: (0, 0)), pl.BlockSpec((1, 2 * half), lambda h: (0, 0))],
        out_shape=[jax.ShapeDtypeStruct((N_KV, NC, half), F32), jax.ShapeDtypeStruct((2 * half, CMP_HID), F32),
                   jax.ShapeDtypeStruct((CMP_HID, HD), F32), jax.ShapeDtypeStruct((1, 2 * half), F32)],
        compiler_params=_cp(("arbitrary",)))(R, pe, w1, w2, hid, dout)


def _attn_cfg(S, Sk, mode):
    tk = _pick(Sk, (256, 128))
    if mode == "cmp":
        return _pick(S, (512, 256, 128)), Sk
    if mode == "sel" and S % (2 * tk) == 0:
        return 2 * tk, tk
    return tk, tk


def _block_start(kb, tk):
    return kb * tk if isinstance(kb, int) else pl.multiple_of(kb * tk, tk)


def _pipelined_key_blocks(mode, q0, tq, tk, produce, consume):
    if mode == "cmp":
        produce(0, True, 0)
        consume(0, 0)
        return
    if mode == "win":
        assert tq == tk and WINDOW == 2 * tk
        last = q0 // tk
        first = jnp.maximum(last - 2, 0)

        @pl.when(last == 0)
        def _():
            produce(last, True, 0)
            consume(last, 0)

        @pl.when(last == 1)
        def _():
            produce(first, True, 0)
            produce(last, True, 1)
            consume(first, 0)
            consume(last, 1)

        @pl.when(last >= 2)
        def _():
            produce(first, True, 0)
            produce(first + 1, False, 1)
            consume(first, 0)
            produce(last, True, 0)
            consume(first + 1, 1)
            consume(last, 0)

        return
    first, n_plain, plain_masked = 0, q0 // tk, False
    pairs = jnp.maximum(n_plain - 1, 0) // 2
    if tq == 2 * tk:
        @pl.when(n_plain >= 1)
        def _():
            produce(0, False, 0)

        def two_plain(j, carry):
            produce(2 * j + 1, False, 1)
            consume(2 * j, 0)
            produce(2 * j + 2, False, 0)
            consume(2 * j + 1, 1)
            return carry

        lax.fori_loop(0, pairs, two_plain, 0)
        kb = 2 * pairs

        @pl.when(n_plain >= 2)
        def _():
            produce(kb + 1, False, 1)
            consume(kb, 0)
            produce(n_plain, True, 0)
            consume(kb + 1, 1)
            produce(n_plain + 1, True, 1)
            consume(n_plain, 0)
            consume(n_plain + 1, 1)

        @pl.when(n_plain == 0)
        def _():
            produce(0, True, 0)
            produce(1, True, 1)
            consume(0, 0)
            consume(1, 1)

        return
    assert tq == tk
    last = first + n_plain

    @pl.when(n_plain >= 1)
    def _():
        produce(first, plain_masked, 0)

    def two(j, carry):
        kb = first + 2 * j
        produce(kb + 1, plain_masked, 1)
        consume(kb, 0)
        produce(kb + 2, plain_masked, 0)
        consume(kb + 1, 1)
        return carry

    lax.fori_loop(0, pairs, two, 0)
    kb = first + 2 * pairs
    left = n_plain - 2 * pairs

    @pl.when(left == 2)
    def _():
        produce(kb + 1, plain_masked, 1)
        consume(kb, 0)
        produce(last, True, 0)
        consume(kb + 1, 1)
        consume(last, 0)

    @pl.when(left == 1)
    def _():
        produce(last, True, 1)
        consume(kb, 0)
        consume(last, 1)

    @pl.when(left == 0)
    def _():
        produce(last, True, 0)
        consume(last, 0)


def _attn_bias(mode, q0, k0, tq, tk):
    k = k0 + lax.broadcasted_iota(jnp.int32, (tk, tq), 0)
    t = q0 + lax.broadcasted_iota(jnp.int32, (tk, tq), 1)
    if mode == "cmp":
        ok = (k * 16 + 31) <= t
    elif mode == "win":
        ok = (k <= t) & ((t - k) < WINDOW)
    else:
        ok = k <= t
    bias = jnp.where(ok, 0.0, NEG)
    return jnp.concatenate([bias] * GRP, axis=1), jnp.concatenate([ok.astype(F32)] * GRP, axis=1)


def _sel_operands(qs, selneg_ref):
    return jnp.concatenate([qs, jnp.concatenate([selneg_ref[0]] * GRP, axis=0)], axis=1)


def _stack_heads(ref, tq):
    return jnp.concatenate([ref[:, g * HD:(g + 1) * HD] for g in range(GRP)], axis=0)


def _scaled_queries(q_ref, tq):
    return (_stack_heads(q_ref, tq).astype(F32) * SCALE).astype(_MXU)


def _blocked_t(x, tk):
    n, Sk, d = x.shape
    return x.reshape(n, Sk // tk, tk, d).transpose(0, 1, 3, 2)


def _head_rows(ref):
    return jnp.concatenate([ref[0, g:g + 1, :] for g in range(GRP)], axis=1)


def _attn_fwd(q, qcol0, k, vt, mode, selneg, gate, y_prev, y_dtype, name):
    S, Sk = q.shape[0], k.shape[1]
    tq, tk = _attn_cfg(S, Sk, mode)
    R = GRP * tq

    def body(*refs):
        q_ref, k_ref, vt_ref = refs[:3]
        rest = list(refs[3:])
        sel_ref = rest.pop(0) if mode == "sel" else None
        gate_ref = rest.pop(0)
        yp_ref = rest.pop(0) if y_prev is not None else None
        o_ref, lse_ref, y_ref, m_scr, l_scr, acc, s_scr = rest
        q0 = pl.program_id(1) * tq
        qs = _scaled_queries(q_ref, tq)
        m_scr[...] = jnp.full_like(m_scr, NEG)
        l_scr[...] = jnp.zeros_like(l_scr)
        acc[...] = jnp.zeros_like(acc)
        qk = _sel_operands(qs, sel_ref) if mode == "sel" else qs

        def produce(kb, masked, slot):
            k0 = _block_start(kb, tk)
            s = _dot(k_ref[0, pl.ds(k0, tk), :], qk, "nt")
            if masked:
                s = s + _attn_bias(mode, q0, k0, tq, tk)[0]
            s_scr[slot] = s

        def consume(kb, slot):
            s = s_scr[slot]
            m_old = m_scr[...]
            m_new = jnp.maximum(m_old, jnp.max(s, axis=0, keepdims=True))
            p = jnp.exp(s - m_new)
            if mode == "cmp":
                p = p * _attn_bias(mode, q0, 0, tq, tk)[1]
            alpha = jnp.exp(m_old - m_new)
            l_scr[...] = alpha * l_scr[...] + jnp.sum(p, axis=0, keepdims=True)
            acc[...] = alpha * acc[...] + _dot(vt_ref[0, kb], p, "nn")
            m_scr[...] = m_new

        _pipelined_key_blocks(mode, q0, tq, tk, produce, consume)
        l = l_scr[...]
        good = l > 0.0
        o_t = acc[...] * jnp.where(good, 1.0 / jnp.where(good, l, 1.0), 0.0)
        lse = jnp.where(good, m_scr[...] + jnp.log(jnp.where(good, l, 1.0)), -NEG)
        y_t = o_t * _sigmoid(_head_rows(gate_ref))
        for g in range(GRP):
            hs, qs_ = slice(g * HD, (g + 1) * HD), slice(g * tq, (g + 1) * tq)
            o_ref[:, hs] = o_t[:, qs_].T
            lse_ref[0, g:g + 1, :] = lse[:, qs_]
            yg = y_t[:, qs_].T
            if y_prev is not None:
                yg = yg + yp_ref[:, hs]
            y_ref[:, hs] = yg.astype(y_ref.dtype)

    row_spec = pl.BlockSpec((1, GRP, tq), lambda h, i: (h, 0, i))
    qo_spec = pl.BlockSpec((tq, GRP * HD), lambda h, i: (i, h))
    ins = [q, k, vt]
    specs = [pl.BlockSpec((tq, GRP * HD), lambda h, i: (i, qcol0 + h)), pl.BlockSpec((1, Sk, k.shape[2]), lambda h, i: (h, 0, 0)),
             pl.BlockSpec((1, Sk // tk, HD, tk), lambda h, i: (h, 0, 0, 0))]
    if mode == "sel":
        ins.append(selneg)
        specs.append(pl.BlockSpec((1, tq, selneg.shape[2]), lambda h, i: (h, i, 0)))
    ins.append(gate)
    specs.append(row_spec)
    if y_prev is not None:
        ins.append(y_prev)
        specs.append(qo_spec)
    return pl.pallas_call(
        body, name=name, grid=(N_KV, S // tq), in_specs=specs, out_specs=[qo_spec, row_spec, qo_spec],
        out_shape=[jax.ShapeDtypeStruct((S, ATT_WIDTH), F32), jax.ShapeDtypeStruct((N_KV, GRP, S), F32),
                   jax.ShapeDtypeStruct((S, ATT_WIDTH), y_dtype)],
        scratch_shapes=[pltpu.VMEM((1, R), F32), pltpu.VMEM((1, R), F32), pltpu.VMEM((HD, R), F32), pltpu.VMEM((2, tk, R), F32)],
        compiler_params=_cp(("parallel", "arbitrary")))(*ins)


def _attn_bwd(q, qcol0, k, kt, v, o, lse, dy, dycol0, gate, mode, selneg, name):
    S, Sk = q.shape[0], k.shape[1]
    tq, tk = _attn_cfg(S, Sk, mode)
    R = GRP * tq

    def body(*refs):
        if mode == "sel":
            (q_ref, k_ref, kt_ref, v_ref, o_ref, lse_ref, dy_ref, gate_ref, sel_ref, dq_ref, dk_ref, dv_ref, dg_ref, dq_scr, s_scr,
             dp_scr) = refs
        else:
            q_ref, k_ref, kt_ref, v_ref, o_ref, lse_ref, dy_ref, gate_ref, dq_ref, dk_ref, dv_ref, dg_ref, dq_scr, s_scr, dp_scr = refs

        @pl.when(pl.program_id(1) == 0)
        def _():
            dk_ref[...] = jnp.zeros_like(dk_ref)
            dv_ref[...] = jnp.zeros_like(dv_ref)

        q0 = pl.program_id(1) * tq
        qs = _scaled_queries(q_ref, tq)
        dys = _stack_heads(dy_ref, tq)
        gv = _sigmoid(_head_rows(gate_ref))
        dy_o = _dot(jnp.ones((8, HD), F32), dys * _stack_heads(o_ref, tq), "nt", split="b")[0:1, :]
        delta = gv * dy_o
        dgate = dy_o * (gv * (1.0 - gv))
        for g in range(GRP):
            dg_ref[0, g:g + 1, :] = dgate[:, g * tq:(g + 1) * tq]
        lsev = _head_rows(lse_ref)
        dos = (dys * jnp.broadcast_to(gv, (8, R)).T[:, 0:1]).astype(_MXU)
        dq_scr[...] = jnp.zeros_like(dq_scr)
        qk = _sel_operands(qs, sel_ref) if mode == "sel" else qs

        def produce(kb, masked, slot):
            k0 = _block_start(kb, tk)
            s = _dot(k_ref[0, pl.ds(k0, tk), :], qk, "nt")
            if masked:
                s = s + _attn_bias(mode, q0, k0, tq, tk)[0]
            s_scr[slot] = s
            dp_scr[slot] = _dot(v_ref[0, pl.ds(k0, tk), :], dos, "nt")

        def consume(kb, slot):
            k0 = _block_start(kb, tk)
            p = jnp.exp(s_scr[slot] - lsev)
            if mode == "cmp":
                p = p * _attn_bias(mode, q0, 0, tq, tk)[1]
            ds = p * (dp_scr[slot] - delta)
            dq_scr[...] += _dot(kt_ref[0, kb], ds, "nn")
            dk_ref[0, pl.ds(k0, tk), :] += _dot(ds, qs, "nn")
            dv_ref[0, pl.ds(k0, tk), :] += _dot(p, dos, "nn")

        _pipelined_key_blocks(mode, q0, tq, tk, produce, consume)
        for g in range(GRP):
            dq_ref[:, g * HD:(g + 1) * HD] = (dq_scr[:, g * tq:(g + 1) * tq] * SCALE).T

    kv_spec = pl.BlockSpec((1, Sk, HD), lambda h, i: (h, 0, 0))
    qo_spec = pl.BlockSpec((tq, GRP * HD), lambda h, i: (i, h))
    row_spec = pl.BlockSpec((1, GRP, tq), lambda h, i: (h, 0, i))
    ins = [q, k, kt, v, o, lse, dy, gate]
    specs = [pl.BlockSpec((tq, GRP * HD), lambda h, i: (i, qcol0 + h)), pl.BlockSpec((1, Sk, k.shape[2]), lambda h, i: (h, 0, 0)),
             pl.BlockSpec((1, Sk // tk, HD, tk), lambda h, i: (h, 0, 0, 0)), kv_spec, qo_spec, row_spec,
             pl.BlockSpec((tq, GRP * HD), lambda h, i: (i, dycol0 + h)), row_spec]
    if mode == "sel":
        ins.append(selneg)
        specs.append(pl.BlockSpec((1, tq, selneg.shape[2]), lambda h, i: (h, i, 0)))
    return pl.pallas_call(
        body, name=name, grid=(N_KV, S // tq), in_specs=specs, out_specs=[qo_spec, kv_spec, kv_spec, row_spec],
        out_shape=[jax.ShapeDtypeStruct((S, ATT_WIDTH), F32), jax.ShapeDtypeStruct((N_KV, Sk, HD), F32),
                   jax.ShapeDtypeStruct((N_KV, Sk, HD), F32), jax.ShapeDtypeStruct((N_KV, GRP, S), F32)],
        scratch_shapes=[pltpu.VMEM((HD, R), F32), pltpu.VMEM((2, tk, R), F32), pltpu.VMEM((2, tk, R), F32)],
        compiler_params=_cp(("parallel", "arbitrary")))(*ins)


def _select(q, qcol0, k_cmp, lse):
    S, NC = q.shape[0], k_cmp.shape[1]
    NB = S // SEL_BLOCK
    tq = _attn_cfg(S, NC, "cmp")[0]
    ci = np.arange(NC)[None, :] * 16
    sj = np.arange(NB)[:, None] * SEL_BLOCK
    ov_t = np.clip(np.minimum(ci + 32, sj + SEL_BLOCK) - np.maximum(ci, sj), 0, None) / 32.0
    ov_t[:, NC - 1] = 0.0
    ov_t = jnp.asarray(ov_t, F32)

    def body(q_ref, k_ref, lse_ref, ov_ref, sel_ref):
        q0 = pl.program_id(1) * tq
        bias, okf = _attn_bias("cmp", q0, 0, tq, NC)
        lsev = _head_rows(lse_ref)
        p = jnp.exp(_dot(k_ref[0], _scaled_queries(q_ref, tq), "nt") + bias - lsev) * okf
        imp4 = _dot(ov_ref[...], p, "nn")
        imp = imp4[:, 0:tq] + imp4[:, tq:2 * tq] + imp4[:, 2 * tq:3 * tq] + imp4[:, 3 * tq:4 * tq]
        blk = lax.broadcasted_iota(jnp.int32, (NB, tq), 0)
        cur = lax.shift_right_logical(q0 + lax.broadcasted_iota(jnp.int32, (NB, tq), 1), 6)
        imp = jnp.where((blk == 0) | (blk == cur) | (blk == cur - 1), FORCE, imp)
        imp = jnp.where(blk <= cur, imp, -1.0)
        rank = jnp.zeros((NB, tq), F32)
        for j in range(NB):
            row = imp[j:j + 1, :]
            ahead = (row > imp) | ((row == imp) & (blk > j))
            rank = rank + ahead.astype(F32)
        chosen = (rank < float(N_SELECT)) & (imp >= 0.0)
        sel_ref[0] = jnp.where(chosen, 0.0, NEG).T.astype(sel_ref.dtype)

    return pl.pallas_call(
        body, name="select_blocks", grid=(N_KV, S // tq),
        in_specs=[pl.BlockSpec((tq, GRP * HD), lambda h, i: (i, qcol0 + h)), pl.BlockSpec((1, NC, HD), lambda h, i: (h, 0, 0)),
                  pl.BlockSpec((1, GRP, tq), lambda h, i: (h, 0, i)), pl.BlockSpec((NB, NC), lambda h, i: (0, 0))],
        out_specs=pl.BlockSpec((1, tq, NB), lambda h, i: (h, i, 0)),
        out_shape=jax.ShapeDtypeStruct((N_KV, S, NB), _MXU), compiler_params=_cp(("parallel", "parallel")))(q, k_cmp, lse, ov_t)


def _to_rows16(x):
    S = x.shape[0]
    return x.reshape(S // 16, 16, N_KV, HD).transpose(2, 0, 1, 3).reshape(N_KV, S // 16, 16 * HD)


def _from_rows16(r):
    NC = r.shape[1]
    return r.reshape(N_KV, NC, 16, HD).transpose(1, 2, 0, 3).reshape(NC * 16, N_KV * HD)


DT_COL0 = SSD_WIDTH + CONV_CH
GATE_IN_COL0 = D_IN - 3 * N_HEADS


SHARD_IN = D_IN // N_DEV


def _orig_cols(ref, c0, width):
    pieces, c = [], c0
    while c < c0 + width:
        d, off = divmod(c, SHARD_IN)
        w = min(SHARD_IN - off, c0 + width - c)
        pieces.append(ref[d, :, off:off + w])
        c += w
    return pieces[0] if len(pieces) == 1 else jnp.concatenate(pieces, axis=1)


def _cols_from_slabs(slabs):
    _, R, c = slabs.shape
    tr = _pick(R, (256, 128))

    def body(s_ref, o_ref):
        for t in range(N_DEV * c // LANE):
            pieces, col = [], t * LANE
            while col < (t + 1) * LANE:
                d, off = divmod(col, c)
                w = min(c - off, (t + 1) * LANE - col)
                pieces.append(s_ref[d, :, off:off + w])
                col += w
            o_ref[:, t * LANE:(t + 1) * LANE] = pieces[0] if len(pieces) == 1 else jnp.concatenate(pieces, axis=1)

    return pl.pallas_call(
        body, name="cols_from_slabs", grid=(R // tr,), in_specs=[pl.BlockSpec((N_DEV, tr, c), lambda i: (0, i, 0))],
        out_specs=pl.BlockSpec((tr, N_DEV * c), lambda i: (i, 0)), out_shape=jax.ShapeDtypeStruct((R, N_DEV * c), slabs.dtype),
        compiler_params=_cp(("parallel",)))(slabs)


def _slabs_from_cols(x):
    R, c = x.shape[0], x.shape[1] // N_DEV
    tr = _pick(R, (256, 128))

    def body(x_ref, o_ref):
        for d in range(N_DEV):
            o_ref[d] = x_ref[:, d * c:(d + 1) * c]

    return pl.pallas_call(
        body, name="slabs_from_cols", grid=(R // tr,), in_specs=[pl.BlockSpec((tr, N_DEV * c), lambda i: (i, 0))],
        out_specs=pl.BlockSpec((N_DEV, tr, c), lambda i: (0, i, 0)), out_shape=jax.ShapeDtypeStruct((N_DEV, R, c), x.dtype),
        compiler_params=_cp(("parallel",)))(x)


def _w_in_from_slabs(slabs):
    D = slabs.shape[1]
    tr = _pick(D, (256, 128))

    def body(s_ref, main_ref, small_ref):
        for t in range(W_MAIN // LANE):
            c = t * LANE
            main_ref[:, c:c + LANE] = _orig_cols(s_ref, c if c < DT_COL0 else c + SSD_HEADS, LANE)
        small_ref[...] = jnp.concatenate(
            [_orig_cols(s_ref, DT_COL0, SSD_HEADS), _orig_cols(s_ref, GATE_IN_COL0, 3 * N_HEADS),
             jnp.zeros((tr, W_SMALL - SSD_HEADS - 3 * N_HEADS), small_ref.dtype)], axis=1)

    return pl.pallas_call(
        body, name="w_in_layout", grid=(D // tr,), in_specs=[pl.BlockSpec((N_DEV, tr, SHARD_IN), lambda i: (0, i, 0))],
        out_specs=[pl.BlockSpec((tr, W_MAIN), lambda i: (i, 0)), pl.BlockSpec((tr, W_SMALL), lambda i: (i, 0))],
        out_shape=[jax.ShapeDtypeStruct((D, W_MAIN), slabs.dtype), jax.ShapeDtypeStruct((D, W_SMALL), slabs.dtype)],
        compiler_params=_cp(("parallel",)))(slabs)


def _w_in_to_slabs(main, small):
    D = main.shape[0]
    tr = _pick(D, (256, 128))
    ranges = [(0, DT_COL0, 0, 0), (DT_COL0, DT_COL0 + SSD_HEADS, 1, 0), (DT_COL0 + SSD_HEADS, GATE_IN_COL0, 0, DT_COL0),
              (GATE_IN_COL0, D_IN, 1, SSD_HEADS)]

    def body(main_ref, small_ref, o_ref):
        srcs = (main_ref, small_ref)
        for d in range(N_DEV):
            lo, hi = d * SHARD_IN, (d + 1) * SHARD_IN
            pieces = []
            for start, stop, which, s0 in ranges:
                a, b = max(lo, start), min(hi, stop)
                if a < b:
                    pieces.append(srcs[which][:, s0 + a - start:s0 + b - start].astype(o_ref.dtype))
            o_ref[d] = pieces[0] if len(pieces) == 1 else jnp.concatenate(pieces, axis=1)

    return pl.pallas_call(
        body, name="w_in_grad_layout", grid=(D // tr,),
        in_specs=[pl.BlockSpec((tr, W_MAIN), lambda i: (i, 0)), pl.BlockSpec((tr, W_SMALL), lambda i: (i, 0))],
        out_specs=pl.BlockSpec((N_DEV, tr, SHARD_IN), lambda i: (0, i, 0)),
        out_shape=jax.ShapeDtypeStruct((N_DEV, D, SHARD_IN), main.dtype), compiler_params=_cp(("parallel",)))(main, small)


QB, KCB, VCB, KSB, VSB, KWB, VWB = 10, 14, 15, 16, 17, 18, 19


def _col256(a, b):
    return a[:, b * 256:(b + 1) * 256]


_EARLY = ["w_in", "cmp_w1_k", "cmp_w1_v"]
_LATE = ["w_out", "w_gate", "w_up", "w_down"]
_FFN = ["w_down", "w_gate", "w_up"]
_MID = ["w_out"]
_LAST = ["cmp_w1_k", "cmp_w1_v", "w_in"]


def _local_step(x, tgt, p, late_weights=None, grads_ready=None):
    S = x.shape[0]
    cos, sin = _rope_tables(S)

    u, rs1 = _rms_fwd(x, p["attn_norm_w"], "attn_norm")
    proj = _mm(u, p["w_main"], "nn", F32, "in_proj", after=p.get("before_in_proj"))
    proj_small = _mm(u, p["w_small"], "nn", F32, "in_proj_small")
    xa = _conv_fwd(proj, p["conv_w"], p["conv_b"])
    y_ssd, y_pre, rs_ssd, hs = _ssd_fwd(proj, proj_small, xa, p["dt_bias"], p["a_log"], p["d_skip"], p["ssd_norm_w"])

    q_rot = _rope([proj], QB, ATT_WIDTH, cos, sin, 1.0, _MXU, "rope_q")
    kv = _kv_prep(proj, cos, sin, _attn_cfg(S, S, "sel")[1])
    rk, rv = _to_rows16(_col256(proj, KCB)), _to_rows16(_col256(proj, VCB))
    k_cmp, hid_k = _compress_fwd(rk, p["cmp_pe_k"], p["cmp_w1_k"], p["cmp_w2_k"])
    v_cmp, hid_v = _compress_fwd(rv, p["cmp_pe_v"], p["cmp_w1_v"], p["cmp_w2_v"])
    n_cmp = k_cmp.shape[1]

    gates = proj_small[:, SSD_HEADS:SSD_HEADS + 3 * N_HEADS].reshape(S, N_KV, GRP, 3).transpose(3, 1, 2, 0)
    o_cmp, lse_cmp, y_att = _attn_fwd(proj, QB, k_cmp, _blocked_t(v_cmp, n_cmp), "cmp", None, gates[0], None, F32, "attn_cmp_fwd")
    sel = _select(proj, QB, k_cmp, lse_cmp)
    o_sel, lse_sel, y_att = _attn_fwd(q_rot, 0, kv["ks_ext"], kv["vs_t"], "sel", sel, gates[1], y_att, F32, "attn_sel_fwd")
    o_win, lse_win, y_att = _attn_fwd(q_rot, 0, kv["kw"], kv["vw_t"], "win", None, gates[2], y_att, _MXU, "attn_win_fwd")

    if late_weights is not None:
        p = {**p, **late_weights(y_att)}
    mixed = jnp.concatenate([y_ssd, y_att], axis=1)
    h1 = _mm(mixed, p["w_out"], "nn", F32, "out_proj", res=x)
    v, rs_ffn = _rms_fwd(h1, p["ffn_norm_w"], "ffn_norm")
    gt, up, act = _ffn_up(v, p["w_gate"], p["w_up"])
    h2 = _mm(act, p["w_down"], "nn", F32, "ffn_down", res=h1)
    loss, dh2, dh2b, d_final_w = _final_loss(h2, p["final_norm_w"], tgt)

    def ready(names):
        return None if grads_ready is None else grads_ready(names, g)

    g = {"final_norm_w": d_final_w}
    g["w_down"] = _mm(act, dh2b, "tn", _MXU, "dw_down")
    dgt, dup = _ffn_dact(dh2b, p["w_down"], gt, up)
    g["w_gate"] = _mm(v, dgt, "tn", _MXU, "dw_gate")
    g["w_up"] = _mm(v, dup, "tn", _MXU, "dw_up")
    dv = _ffn_dv(dgt, dup, p["w_gate"], p["w_up"], ready(_FFN))
    dh1, dh1b, g["ffn_norm_w"] = _rms_bwd(dv, h1, rs_ffn, p["ffn_norm_w"], dh2, "ffn_norm_bwd")
    g["w_out"] = _mm(mixed, dh1b, "tn", _MXU, "dw_out")
    dmixed = _mm(dh1b, p["w_out"], "nt", F32, "dmixed", after=ready(_MID))

    dz, dxa, ddtr, g["dt_bias"], g["a_log"], g["d_skip"], g["ssd_norm_w"] = _ssd_bwd(
        dmixed, proj, proj_small, xa, y_pre, rs_ssd, hs, p["dt_bias"], p["a_log"], p["d_skip"], p["ssd_norm_w"])
    dxbc, g["conv_w"], g["conv_b"] = _conv_bwd(proj, p["conv_w"], p["conv_b"], dxa)

    dyb = SSD_WIDTH // (GRP * HD)
    dq_cmp, dk_cmp, dv_cmp, dg_cmp = _attn_bwd(proj, QB, k_cmp, _blocked_t(k_cmp, n_cmp), v_cmp, o_cmp, lse_cmp, dmixed, dyb,
                                               gates[0], "cmp", None, "attn_cmp_bwd")
    dq_sel, dks, dvs, dg_sel = _attn_bwd(q_rot, 0, kv["ks_ext"], kv["ks_t"], kv["vs"], o_sel, lse_sel, dmixed, dyb, gates[1], "sel",
                                         sel, "attn_sel_bwd")
    dq_win, dkw, dvw, dg_win = _attn_bwd(q_rot, 0, kv["kw"], kv["kw_t"], kv["vw"], o_win, lse_win, dmixed, dyb, gates[2], "win", None,
                                         "attn_win_bwd")
    dgate = jnp.stack([dg_cmp, dg_sel, dg_win]).transpose(3, 1, 2, 0).reshape(S, 3 * N_HEADS)
    drk, g["cmp_w1_k"], g["cmp_w2_k"], g["cmp_pe_k"] = _compress_bwd(rk, p["cmp_pe_k"], p["cmp_w1_k"], p["cmp_w2_k"], hid_k, dk_cmp)
    drv, g["cmp_w1_v"], g["cmp_w2_v"], g["cmp_pe_v"] = _compress_bwd(rv, p["cmp_pe_v"], p["cmp_w1_v"], p["cmp_w2_v"], hid_v, dv_cmp)
    dq = _rope([dq_sel, dq_win], 0, ATT_WIDTH, cos, sin, -1.0, _MXU, "rope_dq", extra=(dq_cmp, 0))
    dkv = _dkv_post(dks, dvs, dkw, dvw, cos, sin)
    dproj = jnp.concatenate([dz, dxbc, dq] + [t.astype(_MXU) for t in (_from_rows16(drk), _from_rows16(drv))] + [dkv], axis=1)
    dsmall = jnp.concatenate([ddtr, dgate, jnp.zeros((S, W_SMALL - SSD_HEADS - 3 * N_HEADS), F32)], axis=1).astype(_MXU)
    g["w_main"] = _mm(u, dproj, "tn", _MXU, "dw_in")
    g["w_small"] = _mm(u, dsmall, "tn", F32, "dw_in_small")
    du = _mm(dproj, p["w_main"], "nt", F32, "du_main", after=ready(_LAST))
    du = _mm(dsmall, p["w_small"], "nt", F32, "du_small", res=du)
    grad_x, _, g["attn_norm_w"] = _rms_bwd(du, x, rs1, p["attn_norm_w"], dh1, "attn_norm_bwd")
    return loss, grad_x, g


MESH_ID = pl.DeviceIdType.MESH


def _my_coords():
    return lax.axis_index("x"), lax.axis_index("y"), lax.axis_index("c")


def _flat_id(px, py, pc):
    return 4 * px + 2 * py + pc


def _peer(k):
    mx, my, mc = _my_coords()
    return (1 - mx if k & 4 else mx, 1 - my if k & 2 else my, 1 - mc if k & 1 else mc)


def _exchange(arrs, scatter, name, after=()):
    n, na = len(arrs), len(after)
    scatter = [scatter] * n if isinstance(scatter, bool) else list(scatter)

    def body(*refs):
        ins, outs = refs[:n], refs[n + na:2 * n + na]
        send_sems, recv_sems, local_sems = refs[2 * n + na:]
        me = _flat_id(*_my_coords())
        copies = []
        for i in range(n):
            src_me = ins[i].at[me] if scatter[i] else ins[i]
            local = pltpu.make_async_copy(src_me, outs[i].at[me], local_sems.at[i])
            local.start()
            copies.append(local)
        for k in range(1, N_DEV):
            peer = _peer(k)
            for i in range(n):
                src = ins[i].at[_flat_id(*peer)] if scatter[i] else ins[i]
                cp = pltpu.make_async_remote_copy(src_ref=src, dst_ref=outs[i].at[me], send_sem=send_sems.at[i * 7 + k - 1],
                                                  recv_sem=recv_sems.at[i * 7 + k - 1], device_id=peer, device_id_type=MESH_ID)
                cp.start()
                copies.append(cp)
        for cp in copies:
            cp.wait()

    any_spec = pl.BlockSpec(memory_space=pl.ANY)
    out_shape = [jax.ShapeDtypeStruct(a.shape if sc else (N_DEV,) + a.shape, a.dtype) for a, sc in zip(arrs, scatter)]
    return pl.pallas_call(
        body, name=name, in_specs=[any_spec] * (n + na), out_specs=[any_spec] * n, out_shape=out_shape,
        scratch_shapes=[pltpu.SemaphoreType.DMA((n * 7,)), pltpu.SemaphoreType.DMA((n * 7,)), pltpu.SemaphoreType.DMA((n,))],
        compiler_params=pltpu.CompilerParams(has_side_effects=True))(*arrs, *after)


def _gather_two_level(arrs, name):
    n = len(arrs)

    def body(*refs):
        ins, outs = refs[:n], refs[n:2 * n]
        send_sems, recv_sems, local_sems = refs[2 * n:]
        x, y, c = _my_coords()
        me, sibling = (x, y, c), (x, y, 1 - c)
        chips = [(1 - x, y), (x, 1 - y), (1 - x, 1 - y)]

        def copy(i, k, block, to, src=None):
            slot = outs[i].at[_flat_id(*block)]
            return pltpu.make_async_remote_copy(src_ref=slot if src is None else src, dst_ref=slot, send_sem=send_sems.at[i * 7 + k],
                                                recv_sem=recv_sems.at[i * 7 + k], device_id=to, device_id_type=MESH_ID)

        mine = [pltpu.make_async_copy(ins[i], outs[i].at[_flat_id(*me)], local_sems.at[i]) for i in range(n)]
        for cp in mine:
            cp.start()
        first = []
        for j, chip in enumerate(chips):
            first += [copy(i, 1 + j, me, (*chip, c), src=ins[i]) for i in range(n)]
        first += [copy(i, 0, me, sibling, src=ins[i]) for i in range(n)]
        for cp in first:
            cp.start()
        passed = []
        for j, chip in enumerate(chips):
            for i in range(n):
                copy(i, 1 + j, (*chip, c), me).wait_recv()
                passed.append(copy(i, 4 + j, (*chip, c), sibling))
                passed[-1].start()
        for i in range(n):
            copy(i, 0, sibling, me).wait_recv()
        for j, chip in enumerate(chips):
            for i in range(n):
                copy(i, 4 + j, (*chip, 1 - c), me).wait_recv()
        for cp in first + passed:
            cp.wait_send()
        for cp in mine:
            cp.wait()

    any_spec = pl.BlockSpec(memory_space=pl.ANY)
    return pl.pallas_call(
        body, name=name, in_specs=[any_spec] * n, out_specs=[any_spec] * n,
        out_shape=[jax.ShapeDtypeStruct((N_DEV,) + a.shape, a.dtype) for a in arrs],
        scratch_shapes=[pltpu.SemaphoreType.DMA((n * 7,)), pltpu.SemaphoreType.DMA((n * 7,)), pltpu.SemaphoreType.DMA((n,))],
        compiler_params=pltpu.CompilerParams(has_side_effects=True))(*arrs)


_HBM = pl.BlockSpec(memory_space=pltpu.HBM)
_SEM = pl.BlockSpec(memory_space=pltpu.SEMAPHORE)
_EFFECT = pltpu.SideEffectType.DATAFLOW_SIDE_EFFECTING


def _split_copies(ins, lands, send_sems, recv_sems, own_sems, scatter):
    me = _flat_id(*_my_coords())
    remote = []
    for k in range(1, N_DEV):
        peer = _peer(k)
        for i in range(len(ins)):
            src = ins[i].at[_flat_id(*peer)] if scatter else ins[i]
            remote.append(pltpu.make_async_remote_copy(src_ref=src, dst_ref=lands[i].at[me], send_sem=send_sems.at[i * 7 + k - 1],
                                                       recv_sem=recv_sems.at[i * 7 + k - 1], device_id=peer, device_id_type=MESH_ID))
    own = [pltpu.make_async_copy(ins[i].at[me] if scatter else ins[i], lands[i].at[me], own_sems.at[i]) for i in range(len(ins))]
    return remote, own


def _split_start(arrs, scatter, name, after=()):
    n, na = len(arrs), len(after)

    def body(*refs):
        remote, own = _split_copies(refs[:n], refs[n:2 * n], refs[2 * n + na], refs[2 * n + na + 1], refs[2 * n + na + 2], scatter)
        for cp in remote + own:
            cp.start()
        refs[-1][...] = jnp.zeros_like(refs[-1])

    land_shapes = [a.shape if scatter else (N_DEV,) + a.shape for a in arrs]
    out_shape = ((pltpu.SemaphoreType.DMA((n * 7,)), pltpu.SemaphoreType.DMA((n * 7,)), pltpu.SemaphoreType.DMA((n,)))
                 + tuple(pltpu.HBM(a.shape, a.dtype) for a in arrs) + tuple(pltpu.HBM(s, a.dtype) for s, a in zip(land_shapes, arrs))
                 + (jax.ShapeDtypeStruct((8, 128), F32),))
    operands = ([pltpu.with_memory_space_constraint(a, pltpu.HBM) for a in arrs]
                + [pltpu.with_memory_space_constraint(lax.empty(s, a.dtype), pltpu.HBM) for s, a in zip(land_shapes, arrs)])
    res = pl.pallas_call(
        body, name=name, out_shape=out_shape, in_specs=[_HBM] * (2 * n) + [pl.BlockSpec(memory_space=pl.ANY)] * na,
        out_specs=(_SEM, _SEM, _SEM) + (_HBM,) * (2 * n) + (pl.BlockSpec(memory_space=pltpu.VMEM),),
        input_output_aliases={i: 3 + i for i in range(2 * n)},
        compiler_params=pltpu.CompilerParams(has_side_effects=_EFFECT))(*operands, *after)
    return dict(send=res[0], recv=res[1], own=res[2], ins=list(res[3:3 + n]), lands=list(res[3 + n:3 + 2 * n]), token=res[-1])


def _split_wait(st, scatter, after, name):
    n = len(st["ins"])

    def body(*refs):
        remote, own = _split_copies(refs[:n], refs[n:2 * n], refs[2 * n], refs[2 * n + 1], refs[2 * n + 2], scatter)
        for cp in remote:
            cp.wait_send()
            cp.wait_recv()
        for cp in own:
            cp.wait()

    arrs = st["ins"] + st["lands"]
    res = pl.pallas_call(
        body, name=name, out_shape=tuple(pltpu.HBM(a.shape, a.dtype) for a in arrs),
        in_specs=[_HBM] * (2 * n) + [_SEM, _SEM, _SEM] + [pl.BlockSpec(memory_space=pl.ANY)] * len(after), out_specs=(_HBM,) * (2 * n),
        input_output_aliases={i: i for i in range(2 * n)},
        compiler_params=pltpu.CompilerParams(has_side_effects=_EFFECT))(*arrs, st["send"], st["recv"], st["own"], *after)
    return list(res[n:])


def _adam_step(p_ref, w_ref, m_ref, v_ref, g_ref, d_ref, nm_ref, nv_ref):
    g = p_ref[0].astype(F32)
    for j in range(1, p_ref.shape[0]):
        g = g + p_ref[j].astype(F32)
    g_ref[...] = g
    nm = ADAM_B1 * m_ref[...] + (1.0 - ADAM_B1) * g
    nv = ADAM_B2 * v_ref[...] + (1.0 - ADAM_B2) * (g * g)
    nm_ref[...] = nm
    nv_ref[...] = nv
    m_hat = nm / (1.0 - ADAM_B1 ** ADAM_STEP)
    v_hat = nv / (1.0 - ADAM_B2 ** ADAM_STEP)
    d_ref[...] = -ADAM_LR * (m_hat / (jnp.sqrt(v_hat) + ADAM_EPS) + ADAM_WD * w_ref[...])


def _adam_sum(parts, w, m, v, name):
    P, R, C = parts.shape
    tr = _pick(R, (256, 128, 64, 32, 8)) if C <= 1024 else _pick(R, (128, 64, 32, 8))
    blk = pl.BlockSpec((tr, C), lambda i: (i, 0))
    return pl.pallas_call(
        functools.partial(_adam_step), name=name, grid=(R // tr,),
        in_specs=[pl.BlockSpec((P, tr, C), lambda i: (0, i, 0)), blk, blk, blk],
        out_specs=[blk] * 4, out_shape=[jax.ShapeDtypeStruct((R, C), F32)] * 4, compiler_params=_cp(("parallel",)))(parts, w, m, v)


def _adam_small(loss_parts, parts, ws, ms, vs):
    n = len(parts)

    def body(*refs):
        loss_ref, ins, outs, total_ref = refs[0], refs[1:4 * n + 1], refs[4 * n + 1:-1], refs[-1]
        for i in range(n):
            _adam_step(ins[i], ins[n + i], ins[2 * n + i], ins[3 * n + i], *outs[4 * i:4 * i + 4])
        total = loss_ref[0]
        for d in range(1, N_DEV):
            total = total + loss_ref[d]
        total_ref[...] = total

    out_shape = [jax.ShapeDtypeStruct(w.shape, F32) for w in ws for _ in range(4)] + [jax.ShapeDtypeStruct(loss_parts.shape[1:], F32)]
    res = pl.pallas_call(body, name="adam_small", out_shape=out_shape)(loss_parts, *parts, *ws, *ms, *vs)
    return res[-1], [tuple(res[4 * i:4 * i + 4]) for i in range(n)]


_WEIGHTS = ["attn_norm_w", "w_in", "conv_w", "conv_b", "dt_bias", "a_log", "d_skip", "ssd_norm_w", "cmp_w1_k", "cmp_w2_k",
            "cmp_w1_v", "cmp_w2_v", "cmp_pe_k", "cmp_pe_v", "w_out", "ffn_norm_w", "w_gate", "w_up", "w_down", "final_norm_w"]
_BIG = ["w_in", "w_gate", "w_up", "w_down", "w_out", "cmp_w1_k", "cmp_w1_v"]
_COL_SHARDED = ("w_in", "w_gate", "w_up")
_REPLICATED = ["attn_norm_w", "conv_b", "dt_bias", "a_log", "d_skip", "ssd_norm_w", "cmp_pe_k", "cmp_pe_v", "ffn_norm_w",
               "final_norm_w"]
_SMALL_SHARDED = ["conv_w", "cmp_w2_k", "cmp_w2_v"]


def _cols_to_slabs(g):
    R = g.shape[0]
    return g.reshape(R, N_DEV, -1).transpose(1, 0, 2)


def _slabs_to_cols(s):
    return s.transpose(1, 0, 2).reshape(s.shape[1], -1)


def kernel(x, attn_norm_w, w_in, conv_w, conv_b, dt_bias, a_log, d_skip, ssd_norm_w, cmp_w1_k, cmp_w2_k, cmp_w1_v, cmp_w2_v, cmp_pe_k, cmp_pe_v, w_out, ffn_norm_w, w_gate, w_up, w_down, final_norm_w, loss_target, m_attn_norm_w, m_w_in, m_conv_w, m_conv_b, m_dt_bias, m_a_log, m_d_skip, m_ssd_norm_w, m_cmp_w1_k, m_cmp_w2_k, m_cmp_w1_v, m_cmp_w2_v, m_cmp_pe_k, m_cmp_pe_v, m_w_out, m_ffn_norm_w, m_w_gate, m_w_up, m_w_down, m_final_norm_w, v_attn_norm_w, v_w_in, v_conv_w, v_conv_b, v_dt_bias, v_a_log, v_d_skip, v_ssd_norm_w, v_cmp_w1_k, v_cmp_w2_k, v_cmp_w1_v, v_cmp_w2_v, v_cmp_pe_k, v_cmp_pe_v, v_w_out, v_ffn_norm_w, v_w_gate, v_w_up, v_w_down, v_final_norm_w):
    a = dict(locals())

    shard = {n: a[n][0].astype(_MXU) for n in _BIG}
    got = _gather_two_level([shard[n] for n in _EARLY] + [cmp_w2_k[0], cmp_w2_v[0], conv_w[0]], "gather_early")
    st_late = _split_start([shard[n] for n in _LATE], False, "gather_late_start", after=(got[0],))

    def assemble(n, t):
        return _cols_from_slabs(t) if n in _COL_SHARDED else t.reshape(-1, t.shape[-1])

    p = dict(attn_norm_w=attn_norm_w, conv_b=conv_b, dt_bias=dt_bias, a_log=a_log, d_skip=d_skip, ssd_norm_w=ssd_norm_w,
             cmp_pe_k=cmp_pe_k.reshape(1, -1), cmp_pe_v=cmp_pe_v.reshape(1, -1), ffn_norm_w=ffn_norm_w,
             final_norm_w=final_norm_w.reshape(1, -1))

    w_main, w_small = _w_in_from_slabs(got[0])
    p.update(before_in_proj=st_late["token"],
             w_main=w_main, w_small=w_small, cmp_w1_k=assemble("cmp_w1_k", got[1]), cmp_w1_v=assemble("cmp_w1_v", got[2]),
             cmp_w2_k=assemble("cmp_w2_k", got[3]).astype(_MXU), cmp_w2_v=assemble("cmp_w2_v", got[4]).astype(_MXU),
             conv_w=_slabs_to_cols(got[5]))

    def late_weights(after):
        got_late = _split_wait(st_late, False, (after,), "gather_late_wait")
        return {n: assemble(n, t) for n, t in zip(_LATE, got_late)}

    def slabs_of(g, n):
        if n == "w_in":
            return _w_in_to_slabs(g["w_main"], g["w_small"])
        return _slabs_from_cols(g[n]) if n in _COL_SHARDED else g[n].reshape(N_DEV, -1, g[n].shape[-1])

    started = []

    def grads_ready(names, g):
        started.append((names, _split_start([slabs_of(g, n) for n in names], True, "scatter_grads_start_%d" % len(started))))
        return started[-1][1]["token"]

    loss_part, grad_x, g = _local_step(x[0], loss_target[0], p, late_weights, grads_ready)

    out, after = {}, (started[-1][1]["token"],)
    for i, (names, st) in enumerate(started):
        if i == len(started) - 1:
            after = after + (grad_x,)
        received = _split_wait(st, True, after, "scatter_grads_wait_%d" % i)
        for n, parts in zip(names, received):
            out[n] = _adam_sum(parts, a[n][0], a["m_" + n][0], a["v_" + n][0], "adam_" + n)
        after = (out[names[-1]][0],)

    small_names = _REPLICATED + _SMALL_SHARDED
    partials = [g[n] for n in _REPLICATED] + [_cols_to_slabs(g["conv_w"])] + [
        g[n].reshape(N_DEV, -1, g[n].shape[-1]) for n in ("cmp_w2_k", "cmp_w2_v")]
    gathered = _exchange([loss_part] + partials, [False] * (1 + len(_REPLICATED)) + [True] * len(_SMALL_SHARDED),
                         "exchange_small_grads", after=(received[0],))
    shapes2d = [t.shape[1:] for t in gathered[1:]]
    loss, res_small = _adam_small(gathered[0], gathered[1:],
                                  *[[a[pre + n].reshape(s) for n, s in zip(small_names, shapes2d)] for pre in ("", "m_", "v_")])
    for n, r in zip(small_names, res_small):
        out[n] = r

    outs = [loss[0, 0], grad_x[None]]
    for j in range(4):
        for n in _WEIGHTS:
            outs.append(out[n][j].reshape(a[n].shape))
    return tuple(outs)
```

```python
import functools

import numpy as np
import jax
import jax.numpy as jnp
from jax import lax
from jax.experimental import pallas as pl
from jax.experimental.pallas import tpu as pltpu

F32 = jnp.float32
_MXU = jnp.bfloat16

N_DEV = 8
SSD_WIDTH = 1024
ATT_WIDTH = 1024
SSD_HEADS = 16
SSD_P = 64
SSD_N = 128
SSD_L = 128
SSD_G = 2
CONV_CH = 1536
CONV_K = 4
HD = 64
N_HEADS = 16
N_KV = 4
GRP = 4
CMP_HID = 256
SEL_BLOCK = 64
N_SELECT = 16
WINDOW = 512
ROPE_DIM = 16
ROPE_THETA = 500000.0
EPS = 1e-6
NEG = -1e30
FORCE = 1e4
SCALE = HD ** -0.5
D_IN = 5184
W_MAIN = 5120
W_SMALL = 128
VMEM_LIMIT = 52 * 1024 * 1024

ADAM_LR, ADAM_B1, ADAM_B2, ADAM_EPS, ADAM_WD, ADAM_STEP = 0.001, 0.9, 0.999, 1e-08, 0.01, 10


def _pick(n, cands):
    for c in cands:
        if n % c == 0:
            return c
    return n


def _cp(sem=None):
    return pltpu.CompilerParams(dimension_semantics=sem, vmem_limit_bytes=VMEM_LIMIT)


def _sigmoid(x):
    return 1.0 / (1.0 + jnp.exp(-x))


def _dot(a, b, dims, split=None):
    dn = {"nn": (((1,), (0,)), ((), ())), "nt": (((1,), (1,)), ((), ())), "tn": (((0,), (0,)), ((), ()))}[dims]
    mm = lambda x, y: lax.dot_general(x.astype(_MXU), y.astype(_MXU), dn, preferred_element_type=F32)
    if split is None:
        return mm(a, b)
    x = (a if split == "a" else b).astype(F32)
    hi = x.astype(_MXU)
    lo = x - hi.astype(F32)
    return mm(hi, b) + mm(lo, b) if split == "a" else mm(a, hi) + mm(a, lo)


LANE = 128
MM_TILE = 1024
MM_K_WHOLE = 2048
MM_K_STEP = 2816
TN_ACC_ELEMS = 3 * 2 ** 20
TN_K_STEP = 512


def _largest_tile(n, cap):
    if n <= cap:
        return n
    best = LANE
    for t in range(LANE, cap + 1, LANE):
        if n % t == 0:
            best = t
    return best


def _mm_tiles(mode, M, N, K):
    if mode == "tn":
        tm = _largest_tile(M, 2 * MM_TILE)
        return tm, _largest_tile(N, TN_ACC_ELEMS // tm), _largest_tile(K, TN_K_STEP)
    tk = K if K <= MM_K_WHOLE else _largest_tile(K, MM_K_STEP)
    return _largest_tile(M, MM_TILE), _largest_tile(N, MM_TILE), tk


def _mm(a, b, mode, out_dtype, name, res=None, after=None):
    if mode == "nn":
        (M, K), N = a.shape, b.shape[1]
    elif mode == "nt":
        (M, K), N = a.shape, b.shape[0]
    else:
        (K, M), N = a.shape, b.shape[1]
    tm, tn, tk = _mm_tiles(mode, M, N, K)
    nk = K // tk
    a_spec = pl.BlockSpec((tk, tm), lambda i, j, k: (k, i)) if mode == "tn" else pl.BlockSpec((tm, tk), lambda i, j, k: (i, k))
    b_spec = pl.BlockSpec((tn, tk), lambda i, j, k: (j, k)) if mode == "nt" else pl.BlockSpec((tk, tn), lambda i, j, k: (k, j))
    o_spec = pl.BlockSpec((tm, tn), lambda i, j, k: (i, j))

    def finish(r, r_ref, o_ref):
        if res is not None:
            r = r + r_ref[...].astype(F32)
        o_ref[...] = r.astype(out_dtype)

    def body_one_step(*refs):
        a_ref, b_ref, o_ref = refs[0], refs[1], refs[-1]
        finish(_dot(a_ref[...], b_ref[...], mode), refs[2], o_ref)

    def body(*refs):
        a_ref, b_ref, o_ref, acc = refs[0], refs[1], refs[-2], refs[-1]
        k = pl.program_id(2)

        @pl.when(k == 0)
        def _():
            acc[...] = jnp.zeros_like(acc)

        acc[...] += _dot(a_ref[...], b_ref[...], mode)

        @pl.when(k == nk - 1)
        def _():
            finish(acc[...], refs[2], o_ref)

    ins, specs = [a, b], [a_spec, b_spec]
    if res is not None:
        ins.append(res)
        specs.append(o_spec)
    if after is not None:
        ins.append(after)
        specs.append(pl.BlockSpec(memory_space=pl.ANY))
    return pl.pallas_call(
        body_one_step if nk == 1 else body, name=name, grid=(M // tm, N // tn, nk), in_specs=specs, out_specs=o_spec,
        out_shape=jax.ShapeDtypeStruct((M, N), out_dtype), scratch_shapes=[] if nk == 1 else [pltpu.VMEM((tm, tn), F32)],
        compiler_params=_cp(("parallel", "parallel", "arbitrary")))(*ins)


def _ffn_up(v, w_gate, w_up):
    S, D = v.shape
    F = w_gate.shape[1]
    tm, tn = _largest_tile(S, MM_TILE), _largest_tile(F, MM_TILE // 2)

    def body(v_ref, wg_ref, wu_ref, gt_ref, up_ref, act_ref):
        vv = v_ref[...]
        g = _dot(vv, wg_ref[...], "nn")
        u = _dot(vv, wu_ref[...], "nn")
        gt_ref[...] = g.astype(gt_ref.dtype)
        up_ref[...] = u.astype(up_ref.dtype)
        act_ref[...] = (g * _sigmoid(g) * u).astype(act_ref.dtype)

    o_spec = pl.BlockSpec((tm, tn), lambda i, j: (i, j))
    w_spec = pl.BlockSpec((D, tn), lambda i, j: (0, j))
    return pl.pallas_call(
        body, name="ffn_up", grid=(S // tm, F // tn),
        in_specs=[pl.BlockSpec((tm, D), lambda i, j: (i, 0)), w_spec, w_spec], out_specs=[o_spec, o_spec, o_spec],
        out_shape=[jax.ShapeDtypeStruct((S, F), _MXU)] * 3,
        compiler_params=_cp(("parallel", "parallel")))(v, w_gate, w_up)


def _ffn_dv(dgt, dup, w_gate, w_up, after):
    S, F = dgt.shape
    D = w_gate.shape[0]
    tm, tn, _ = _mm_tiles("nt", S, D, F)
    tk = _largest_tile(F, MM_K_STEP // 2)
    nk = F // tk

    def body(g_ref, u_ref, wg_ref, wu_ref, *rest):
        o_ref, acc = rest[-2], rest[-1]
        k = pl.program_id(2)

        @pl.when(k == 0)
        def _():
            acc[...] = jnp.zeros_like(acc)

        acc[...] += _dot(g_ref[...], wg_ref[...], "nt") + _dot(u_ref[...], wu_ref[...], "nt")

        @pl.when(k == nk - 1)
        def _():
            o_ref[...] = acc[...]

    a_spec = pl.BlockSpec((tm, tk), lambda i, j, k: (i, k))
    w_spec = pl.BlockSpec((tn, tk), lambda i, j, k: (j, k))
    ins, specs = [dgt, dup, w_gate, w_up], [a_spec, a_spec, w_spec, w_spec]
    if after is not None:
        ins.append(after)
        specs.append(pl.BlockSpec(memory_space=pl.ANY))
    return pl.pallas_call(
        body, name="ffn_dv", grid=(S // tm, D // tn, nk), in_specs=specs, out_specs=pl.BlockSpec((tm, tn), lambda i, j, k: (i, j)),
        out_shape=jax.ShapeDtypeStruct((S, D), F32), scratch_shapes=[pltpu.VMEM((tm, tn), F32)],
        compiler_params=_cp(("parallel", "parallel", "arbitrary")))(*ins)


def _ffn_dact(dh2, w_down, gt, up):
    S, D = dh2.shape
    F = w_down.shape[0]
    tm, tn = _largest_tile(S, MM_TILE), _largest_tile(F, MM_TILE // 2)

    def body(d_ref, w_ref, gt_ref, up_ref, dg_ref, du_ref):
        da, g, u = _dot(d_ref[...], w_ref[...], "nt"), gt_ref[...].astype(F32), up_ref[...].astype(F32)
        s = _sigmoid(g)
        dg_ref[...] = (da * u * (s * (1.0 + g * (1.0 - s)))).astype(dg_ref.dtype)
        du_ref[...] = (da * (g * s)).astype(du_ref.dtype)

    o_spec = pl.BlockSpec((tm, tn), lambda i, j: (i, j))
    return pl.pallas_call(
        body, name="ffn_dact", grid=(S // tm, F // tn),
        in_specs=[pl.BlockSpec((tm, D), lambda i, j: (i, 0)), pl.BlockSpec((tn, D), lambda i, j: (j, 0)), o_spec, o_spec],
        out_specs=[o_spec, o_spec],
        out_shape=[jax.ShapeDtypeStruct((S, F), _MXU), jax.ShapeDtypeStruct((S, F), _MXU)],
        compiler_params=_cp(("parallel", "parallel")))(dh2, w_down, gt, up)


def _rms_fwd(x, w, name):
    S, D = x.shape
    tr = _pick(S, (256, 128))

    def body(x_ref, w_ref, xn_ref, rs_ref):
        xv = x_ref[...]
        rs = lax.rsqrt(jnp.mean(xv * xv, axis=-1, keepdims=True) + EPS)
        xn_ref[...] = ((xv * rs) * w_ref[...]).astype(xn_ref.dtype)
        rs_ref[...] = rs

    return pl.pallas_call(
        body, name=name, grid=(S // tr,),
        in_specs=[pl.BlockSpec((tr, D), lambda i: (i, 0)), pl.BlockSpec((1, D), lambda i: (0, 0))],
        out_specs=[pl.BlockSpec((tr, D), lambda i: (i, 0)), pl.BlockSpec((tr, 1), lambda i: (i, 0))],
        out_shape=[jax.ShapeDtypeStruct((S, D), _MXU), jax.ShapeDtypeStruct((S, 1), F32)],
        compiler_params=_cp(("parallel",)))(x, w)


def _rms_bwd(dyn, x, rs, w, res, name):
    S, D = x.shape
    tr = _pick(S, (256, 128))

    def body(dy_ref, x_ref, rs_ref, w_ref, res_ref, dx_ref, dxb_ref, dw_ref):
        @pl.when(pl.program_id(0) == 0)
        def _():
            dw_ref[...] = jnp.zeros_like(dw_ref)

        dy, r = dy_ref[...].astype(F32), rs_ref[...]
        xhat = x_ref[...] * r
        dw_ref[...] += jnp.sum(dy * xhat, axis=0, keepdims=True)
        dxhat = dy * w_ref[...]
        dx = res_ref[...] + r * (dxhat - xhat * jnp.mean(dxhat * xhat, axis=-1, keepdims=True))
        dx_ref[...] = dx
        dxb_ref[...] = dx.astype(dxb_ref.dtype)

    row = pl.BlockSpec((tr, D), lambda i: (i, 0))
    vec = pl.BlockSpec((1, D), lambda i: (0, 0))
    return pl.pallas_call(
        body, name=name, grid=(S // tr,),
        in_specs=[row, row, pl.BlockSpec((tr, 1), lambda i: (i, 0)), vec, row], out_specs=[row, row, vec],
        out_shape=[jax.ShapeDtypeStruct((S, D), F32), jax.ShapeDtypeStruct((S, D), _MXU), jax.ShapeDtypeStruct((1, D), F32)],
        compiler_params=_cp(("arbitrary",)))(dyn, x, rs, w, res)


def _final_loss(h2, w, tgt):
    S, D = h2.shape
    tr = _pick(S, (256, 128))

    def body(h_ref, w_ref, t_ref, loss_ref, dh_ref, dhb_ref, dw_ref):
        @pl.when(pl.program_id(0) == 0)
        def _():
            dw_ref[...] = jnp.zeros_like(dw_ref)
            loss_ref[...] = jnp.zeros_like(loss_ref)

        hv, wv = h_ref[...], w_ref[...]
        rs = lax.rsqrt(jnp.mean(hv * hv, axis=-1, keepdims=True) + EPS)
        xhat = hv * rs
        err = xhat * wv - t_ref[...]
        row = jnp.mean(err * err, axis=-1, keepdims=True)
        loss_ref[...] += jnp.broadcast_to(0.5 * jnp.sum(row, axis=0, keepdims=True), loss_ref.shape)
        dy = err * (1.0 / D)
        dw_ref[...] += jnp.sum(dy * xhat, axis=0, keepdims=True)
        dxhat = dy * wv
        dh = rs * (dxhat - xhat * jnp.mean(dxhat * xhat, axis=-1, keepdims=True))
        dh_ref[...] = dh
        dhb_ref[...] = dh.astype(dhb_ref.dtype)

    row = pl.BlockSpec((tr, D), lambda i: (i, 0))
    vec = pl.BlockSpec((1, D), lambda i: (0, 0))
    return pl.pallas_call(
        body, name="final_loss", grid=(S // tr,), in_specs=[row, vec, row],
        out_specs=[pl.BlockSpec((1, LANE), lambda i: (0, 0)), row, row, vec],
        out_shape=[jax.ShapeDtypeStruct((1, LANE), F32), jax.ShapeDtypeStruct((S, D), F32), jax.ShapeDtypeStruct((S, D), _MXU),
                   jax.ShapeDtypeStruct((1, D), F32)],
        compiler_params=_cp(("arbitrary",)))(h2, w, tgt)


def _shift_rows(x, k, rows):
    if k == 0:
        return x
    S = x.shape[0]
    r = pltpu.roll(x, k % S, axis=0)
    ok = (rows >= k) if k > 0 else (rows < S + k)
    return jnp.where(ok, r, 0.0)


XBC_COL0 = SSD_WIDTH // 128


def _conv_fwd(proj, conv_w, conv_b):
    S = proj.shape[0]
    nct = CONV_CH // 128

    def body(x_ref, w_ref, b_ref, o_ref):
        x = x_ref[...]
        rows = lax.broadcasted_iota(jnp.int32, x.shape, 0)
        c = b_ref[...] + w_ref[3:4, :] * x
        for k in range(1, CONV_K):
            c = c + w_ref[3 - k:4 - k, :] * _shift_rows(x, k, rows)
        o_ref[...] = c * _sigmoid(c)

    return pl.pallas_call(
        body, name="conv_fwd", grid=(nct,),
        in_specs=[pl.BlockSpec((S, 128), lambda j: (0, XBC_COL0 + j)), pl.BlockSpec((CONV_K, 128), lambda j: (0, j)),
                  pl.BlockSpec((1, 128), lambda j: (0, j))],
        out_specs=pl.BlockSpec((S, 128), lambda j: (0, j)),
        out_shape=jax.ShapeDtypeStruct((S, CONV_CH), F32), compiler_params=_cp(("parallel",)))(proj, conv_w, conv_b)


def _conv_bwd(proj, conv_w, conv_b, dxa):
    S = proj.shape[0]
    nct = CONV_CH // 128

    def body(x_ref, w_ref, b_ref, d_ref, dx_ref, dw_ref, db_ref):
        x = x_ref[...]
        rows = lax.broadcasted_iota(jnp.int32, x.shape, 0)
        xs = [_shift_rows(x, k, rows) for k in range(CONV_K)]
        c = b_ref[...] + w_ref[3:4, :] * x
        for k in range(1, CONV_K):
            c = c + w_ref[3 - k:4 - k, :] * xs[k]
        s = _sigmoid(c)
        dc = d_ref[...] * (s * (1.0 + c * (1.0 - s)))
        dx = w_ref[3:4, :] * dc
        for k in range(1, CONV_K):
            dx = dx + w_ref[3 - k:4 - k, :] * _shift_rows(dc, -k, rows)
        dx_ref[...] = dx.astype(dx_ref.dtype)
        for k in range(CONV_K):
            dw_ref[3 - k:4 - k, :] = jnp.sum(dc * xs[k], axis=0, keepdims=True)
        db_ref[...] = jnp.sum(dc, axis=0, keepdims=True)

    col = pl.BlockSpec((S, 128), lambda j: (0, j))
    return pl.pallas_call(
        body, name="conv_bwd", grid=(nct,),
        in_specs=[pl.BlockSpec((S, 128), lambda j: (0, XBC_COL0 + j)), pl.BlockSpec((CONV_K, 128), lambda j: (0, j)),
                  pl.BlockSpec((1, 128), lambda j: (0, j)), col],
        out_specs=[col, pl.BlockSpec((CONV_K, 128), lambda j: (0, j)), pl.BlockSpec((1, 128), lambda j: (0, j))],
        out_shape=[jax.ShapeDtypeStruct((S, CONV_CH), _MXU), jax.ShapeDtypeStruct((CONV_K, CONV_CH), F32),
                   jax.ShapeDtypeStruct((1, CONV_CH), F32)],
        compiler_params=_cp(("parallel",)))(proj, conv_w, conv_b, dxa)


def _ssd_consts():
    L = SSD_L
    r = lax.broadcasted_iota(jnp.int32, (L, L), 0)
    c = lax.broadcasted_iota(jnp.int32, (L, L), 1)
    causal = r >= c
    upper = (r <= c).astype(F32)
    hr = lax.broadcasted_iota(jnp.int32, (SSD_HEADS, SSD_WIDTH), 0)
    hc = lax.broadcasted_iota(jnp.int32, (SSD_HEADS, SSD_WIDTH), 1)
    expand = (lax.shift_right_logical(hc, 6) == hr).astype(F32)
    return causal, causal.astype(F32), upper, expand


def _softplus(x):
    return jnp.maximum(x, 0.0) + jnp.log(1.0 + jnp.exp(-jnp.abs(x)))


def _ssd_scalars(dtr, dt_bias, a_log, tri, upper, expand):
    dt = _softplus(dtr + dt_bias)
    A = -jnp.exp(a_log)
    adt = dt * A
    acum = _dot(tri, adt, "nn", split="b")
    acum_t = _dot(adt, upper, "tn", split="a")
    alast = acum[SSD_L - 1:SSD_L, :]
    e = jnp.exp(acum)
    wdec = jnp.exp(alast - acum)
    gam = jnp.exp(alast)
    ex = lambda t: _dot(t, expand, "nn", split="a")
    gam8 = jnp.broadcast_to(gam, (8, SSD_HEADS))
    return dt, A, acum, acum_t, e, wdec, gam, ex(dt), ex(e), ex(wdec), ex(gam8)[0:1, :]


def _ssd_fwd(proj, proj_small, xa, dt_bias, a_log, d_skip, norm_w):
    S = proj.shape[0]
    L, N, W = SSD_L, SSD_N, SSD_WIDTH
    nc = S // L

    def body(z_ref, xa_ref, dtr_ref, dtb_ref, al_ref, dsk_ref, nw_ref, yo_ref, y_ref, rs_ref, hs_ref, h_scr, y_scr):
        @pl.when(pl.program_id(0) == 0)
        def _():
            h_scr[...] = jnp.zeros_like(h_scr)

        causal, tri, upper, expand = _ssd_consts()
        dt, A, acum, acum_t, e, wdec, gam, dtE, eE, wE, gamE = _ssd_scalars(dtr_ref[:, 0:SSD_HEADS], dtb_ref[...], al_ref[...], tri, upper, expand)
        xs = xa_ref[:, 0:W]
        X = xs * dtE
        XW = X * wE
        hs_ref[0] = h_scr[...]
        for g in range(SSD_G):
            gs = slice(g * 512, (g + 1) * 512)
            Bg = xa_ref[:, W + g * N:W + (g + 1) * N]
            Cg = xa_ref[:, W + SSD_G * N + g * N:W + SSD_G * N + (g + 1) * N]
            Hg = h_scr[:, gs]
            CB = _dot(Cg, Bg, "nt")
            yoff = _dot(Cg, Hg, "nn") * eE[:, gs]
            st = _dot(Bg, XW[:, gs], "tn")
            for j in range(8):
                h = g * 8 + j
                hsl = slice(h * SSD_P, (h + 1) * SSD_P)
                lam = jnp.exp(jnp.where(causal, acum[:, h:h + 1] - acum_t[h:h + 1, :], -jnp.inf))
                y_scr[:, hsl] = _dot(CB * lam, X[:, hsl], "nn") + yoff[:, j * SSD_P:(j + 1) * SSD_P]
            h_scr[:, gs] = gamE[:, gs] * Hg + st
        dskE = _dot(jnp.broadcast_to(dsk_ref[...], (8, SSD_HEADS)), expand, "nn", split="a")[0:1, :]
        y = y_scr[...] + dskE * xs
        y_ref[...] = y
        zv = z_ref[...]
        yg = y * (zv * _sigmoid(zv))
        rs = lax.rsqrt(jnp.mean(yg * yg, axis=-1, keepdims=True) + EPS)
        rs_ref[...] = rs
        yo_ref[...] = ((yg * rs) * nw_ref[...]).astype(yo_ref.dtype)

    p16 = pl.BlockSpec((1, SSD_HEADS), lambda c: (0, 0))
    return pl.pallas_call(
        body, name="ssd_fwd", grid=(nc,),
        in_specs=[pl.BlockSpec((L, W), lambda c: (c, 0)), pl.BlockSpec((L, CONV_CH), lambda c: (c, 0)),
                  pl.BlockSpec((L, W_SMALL), lambda c: (c, 0)), p16, p16, p16, pl.BlockSpec((1, W), lambda c: (0, 0))],
        out_specs=[pl.BlockSpec((L, W), lambda c: (c, 0)), pl.BlockSpec((L, W), lambda c: (c, 0)),
                   pl.BlockSpec((L, 1), lambda c: (c, 0)), pl.BlockSpec((1, N, W), lambda c: (c, 0, 0))],
        out_shape=[jax.ShapeDtypeStruct((S, W), _MXU), jax.ShapeDtypeStruct((S, W), F32), jax.ShapeDtypeStruct((S, 1), F32),
                   jax.ShapeDtypeStruct((nc, N, W), F32)],
        scratch_shapes=[pltpu.VMEM((N, W), F32), pltpu.VMEM((L, W), F32)],
        compiler_params=_cp(("arbitrary",)))(proj, xa, proj_small, dt_bias, a_log, d_skip, norm_w)


def _ssd_bwd(dmixed, proj, proj_small, xa, y, rs2, hs, dt_bias, a_log, d_skip, norm_w):
    S = proj.shape[0]
    L, N, W, H = SSD_L, SSD_N, SSD_WIDTH, SSD_HEADS
    nc = S // L

    def body(dyo_ref, z_ref, xa_ref, dtr_ref, y_ref, rs_ref, hs_ref, dtb_ref, al_ref, dsk_ref, nw_ref,
             dz_ref, dxa_ref, ddtr_ref, ddtb_ref, dal_ref, ddsk_ref, dnw_ref, dh_scr, dx_scr):
        @pl.when(pl.program_id(0) == 0)
        def _():
            dh_scr[...] = jnp.zeros_like(dh_scr)
            ddtb_ref[...] = jnp.zeros_like(ddtb_ref)
            dal_ref[...] = jnp.zeros_like(dal_ref)
            ddsk_ref[...] = jnp.zeros_like(ddsk_ref)
            dnw_ref[...] = jnp.zeros_like(dnw_ref)

        causal, tri, upper, expand = _ssd_consts()
        heads = lambda t: _dot(t, expand, "nt", split="a")
        onehot = lambda h: (lax.broadcasted_iota(jnp.int32, (1, H), 1) == h).astype(F32)

        zv, yv, rs = z_ref[...], y_ref[...], rs_ref[...]
        sz = _sigmoid(zv)
        zs = zv * sz
        xhat = (yv * zs) * rs
        dyo = dyo_ref[...].astype(F32)
        dnw_ref[...] += jnp.sum(dyo * xhat, axis=0, keepdims=True)
        dxhat = dyo * nw_ref[...]
        dyg = rs * (dxhat - xhat * jnp.mean(dxhat * xhat, axis=-1, keepdims=True))
        dz_ref[...] = (dyg * yv * (sz * (1.0 + zv * (1.0 - sz)))).astype(dz_ref.dtype)
        dy = dyg * zs

        dtr = dtr_ref[:, 0:H]
        dt, A, acum, acum_t, e, wdec, gam, dtE, eE, wE, gamE = _ssd_scalars(dtr, dtb_ref[...], al_ref[...], tri, upper, expand)
        xs = xa_ref[:, 0:W]
        X = xs * dtE
        XW = X * wE
        dskE = _dot(jnp.broadcast_to(dsk_ref[...], (8, H)), expand, "nn", split="a")[0:1, :]
        ddsk_ref[...] += heads(jnp.broadcast_to(jnp.sum(dy * xs, axis=0, keepdims=True), (8, W)))[0:1, :]

        dYe = dy * eE
        dacum = jnp.zeros((L, H), F32)
        de_full = []
        dw_full = []
        dgam_full = []
        for g in range(SSD_G):
            gs = slice(g * 512, (g + 1) * 512)
            Bg = xa_ref[:, W + g * N:W + (g + 1) * N]
            Cg = xa_ref[:, W + SSD_G * N + g * N:W + SSD_G * N + (g + 1) * N]
            Hg = hs_ref[0, :, gs]
            dHn = dh_scr[:, gs]
            CH = _dot(Cg, Hg, "nn")
            de_full.append(dy[:, gs] * CH)
            dC = _dot(dYe[:, gs], Hg, "nt")
            dHs = gamE[:, gs] * dHn + _dot(Cg, dYe[:, gs], "tn")
            dgam_full.append(jnp.sum(dHn * Hg, axis=0, keepdims=True))
            BdS = _dot(Bg, dHn, "nn")
            dB = _dot(XW[:, gs], dHn, "nt")
            dx_scr[:, gs] = BdS * wE[:, gs]
            dw_full.append(BdS * X[:, gs])
            CB = _dot(Cg, Bg, "nt")
            dCB = jnp.zeros((L, L), F32)
            for j in range(8):
                h = g * 8 + j
                hsl = slice(h * SSD_P, (h + 1) * SSD_P)
                lam = jnp.exp(jnp.where(causal, acum[:, h:h + 1] - acum_t[h:h + 1, :], -jnp.inf))
                M = CB * lam
                dM = _dot(dy[:, hsl], X[:, hsl], "nt")
                dx_scr[:, hsl] += _dot(M, dy[:, hsl], "tn")
                dCB = dCB + dM * lam
                Q = dM * M
                rowsum = jnp.sum(Q, axis=1, keepdims=True)
                colsum = _dot(Q, jnp.ones((L, 8), F32), "tn", split="a")[:, 0:1]
                dacum = dacum + (rowsum - colsum) * onehot(h)
            dC = dC + _dot(dCB, Bg, "nn")
            dB = dB + _dot(dCB, Cg, "tn")
            dxa_ref[:, W + g * N:W + (g + 1) * N] = dB
            dxa_ref[:, W + SSD_G * N + g * N:W + SSD_G * N + (g + 1) * N] = dC
            dh_scr[:, gs] = dHs

        de16 = heads(jnp.concatenate(de_full, axis=1))
        dw16 = heads(jnp.concatenate(dw_full, axis=1))
        dgam16 = heads(jnp.broadcast_to(jnp.concatenate(dgam_full, axis=1), (8, W)))[0:1, :]
        dacum = dacum + de16 * e - dw16 * wdec
        dlast = jnp.sum(dw16 * wdec, axis=0, keepdims=True) + dgam16 * gam
        lastrow = (lax.broadcasted_iota(jnp.int32, (L, 1), 0) == L - 1).astype(F32)
        dacum = dacum + lastrow * dlast
        da = _dot(tri, dacum, "tn", split="b")
        dX = dx_scr[...]
        ddt = da * A + heads(dX * xs)
        dA = jnp.sum(da * dt, axis=0, keepdims=True)
        dal_ref[...] += dA * A
        ddtr = ddt * _sigmoid(dtr + dtb_ref[...])
        ddtb_ref[...] += jnp.sum(ddtr, axis=0, keepdims=True)
        ddtr_ref[...] = ddtr
        dxa_ref[:, 0:W] = dX * dtE + dy * dskE

    p16 = pl.BlockSpec((1, H), lambda c: (0, 0))
    rev = lambda c: (nc - 1 - c, 0)
    return pl.pallas_call(
        body, name="ssd_bwd", grid=(nc,),
        in_specs=[pl.BlockSpec((L, W), rev), pl.BlockSpec((L, W), rev), pl.BlockSpec((L, CONV_CH), rev),
                  pl.BlockSpec((L, W_SMALL), rev), pl.BlockSpec((L, W), rev), pl.BlockSpec((L, 1), rev),
                  pl.BlockSpec((1, N, W), lambda c: (nc - 1 - c, 0, 0)), p16, p16, p16, pl.BlockSpec((1, W), lambda c: (0, 0))],
        out_specs=[pl.BlockSpec((L, W), rev), pl.BlockSpec((L, CONV_CH), rev), pl.BlockSpec((L, H), rev),
                   p16, p16, p16, pl.BlockSpec((1, W), lambda c: (0, 0))],
        out_shape=[jax.ShapeDtypeStruct((S, W), _MXU), jax.ShapeDtypeStruct((S, CONV_CH), F32), jax.ShapeDtypeStruct((S, H), F32),
                   jax.ShapeDtypeStruct((1, H), F32), jax.ShapeDtypeStruct((1, H), F32), jax.ShapeDtypeStruct((1, H), F32),
                   jax.ShapeDtypeStruct((1, W), F32)],
        scratch_shapes=[pltpu.VMEM((N, W), F32), pltpu.VMEM((L, W), F32)],
        compiler_params=_cp(("arbitrary",)))(dmixed, proj, xa, proj_small, y, rs2, hs, dt_bias, a_log, d_skip, norm_w)


def _rope_tables(S):
    inv = 1.0 / (ROPE_THETA ** (jnp.arange(0, ROPE_DIM, 2, dtype=F32) / ROPE_DIM))
    ang = jnp.arange(S, dtype=F32)[:, None] * inv[None, :]
    cos, sin = jnp.cos(ang), jnp.sin(ang)
    half = ROPE_DIM // 2
    c64 = jnp.concatenate([cos, cos, jnp.ones((S, HD - ROPE_DIM), F32)], axis=1)
    s64 = jnp.concatenate([sin, sin, jnp.zeros((S, HD - ROPE_DIM), F32)], axis=1)
    del half
    return jnp.concatenate([c64, c64], axis=1), jnp.concatenate([s64, s64], axis=1)


def _rope(xs, blk0, width, cos, sin, sign, out_dtype, name, extra=None):
    S = xs[0].shape[0]
    tr = _pick(S, (512, 256, 128))
    nx = len(xs)

    def body(*refs):
        x_refs, c_ref, s_ref = refs[:nx], refs[nx], refs[nx + 1]
        e_ref = refs[nx + 2] if extra is not None else None
        o_ref = refs[-1]
        cv, sv = c_ref[...], s_ref[...] * sign
        lane = lax.broadcasted_iota(jnp.int32, (tr, 128), 1)
        first = (lane & (HD - 1)) < (ROPE_DIM // 2)
        for j in range(bw // 128):
            cs = slice(j * 128, (j + 1) * 128)
            xv = x_refs[0][:, cs].astype(F32)
            for r in x_refs[1:]:
                xv = xv + r[:, cs].astype(F32)
            out = _rotate128(xv, cv, sv, first)
            if extra is not None:
                out = out + e_ref[:, cs].astype(F32)
            o_ref[:, cs] = out.astype(out_dtype)

    bw = 512
    assert width % bw == 0 and (blk0 * 256) % bw == 0
    b0 = blk0 * 256 // bw
    t128 = pl.BlockSpec((tr, 128), lambda i, j: (i, 0))
    oblk = pl.BlockSpec((tr, bw), lambda i, j: (i, j))
    specs = [pl.BlockSpec((tr, bw), lambda i, j: (i, b0 + j))] * nx + [t128, t128]
    ins = list(xs) + [cos, sin]
    if extra is not None:
        assert (extra[1] * 256) % bw == 0
        ins.append(extra[0])
        eb = extra[1] * 256 // bw
        specs.append(pl.BlockSpec((tr, bw), lambda i, j: (i, eb + j)))
    return pl.pallas_call(
        body, name=name, grid=(S // tr, width // bw), in_specs=specs, out_specs=oblk,
        out_shape=jax.ShapeDtypeStruct((S, width), out_dtype), compiler_params=_cp(("parallel", "parallel")))(*ins)


def _rotate128(xv, cv, sv, first):
    rot = jnp.where(first, -pltpu.roll(xv, 128 - ROPE_DIM // 2, axis=1), pltpu.roll(xv, ROPE_DIM // 2, axis=1))
    return xv * cv + rot * sv


def _kv_prep(proj, cos, sin, tk):
    S = proj.shape[0]
    NB = S // SEL_BLOCK

    def body(ks_ref, vs_ref, kw_ref, vw_ref, c_ref, s_ref, *outs):
        cv, sv = c_ref[...], s_ref[...]
        lane = lax.broadcasted_iota(jnp.int32, (tk, 128), 1)
        first = (lane & (HD - 1)) < (ROPE_DIM // 2)
        key = pl.program_id(0) * tk + lax.broadcasted_iota(jnp.int32, (tk, NB), 0)
        onehot = (lax.shift_right_logical(key, 6) == lax.broadcasted_iota(jnp.int32, (tk, NB), 1)).astype(F32)
        for j, (ref, rotated) in enumerate(((ks_ref, True), (vs_ref, False), (kw_ref, True), (vw_ref, False))):
            nat, blk = outs[2 * j], outs[2 * j + 1]
            for half in range(2):
                xv = ref[:, half * 128:(half + 1) * 128]
                if rotated:
                    xv = _rotate128(xv, cv, sv, first)
                for e in range(2):
                    h = 2 * half + e
                    piece = xv[:, e * HD:(e + 1) * HD]
                    nat[h] = (jnp.concatenate([piece, onehot], axis=1) if j == 0 else piece).astype(nat.dtype)
                    blk[h, 0] = piece.T.astype(blk.dtype)

    col = lambda b: pl.BlockSpec((tk, 256), lambda i: (i, b))
    t128 = pl.BlockSpec((tk, 128), lambda i: (i, 0))
    nat_spec = lambda w: pl.BlockSpec((N_KV, tk, w), lambda i: (0, i, 0))
    blk_spec = pl.BlockSpec((N_KV, 1, HD, tk), lambda i: (0, i, 0, 0))
    nat_shape = lambda w: jax.ShapeDtypeStruct((N_KV, S, w), _MXU)
    blk_shape = jax.ShapeDtypeStruct((N_KV, S // tk, HD, tk), _MXU)
    widths = (HD + NB, HD, HD, HD)
    res = pl.pallas_call(
        body, name="kv_prep", grid=(S // tk,), in_specs=[col(KSB), col(VSB), col(KWB), col(VWB), t128, t128],
        out_specs=[s for w in widths for s in (nat_spec(w), blk_spec)],
        out_shape=[s for w in widths for s in (nat_shape(w), blk_shape)],
        compiler_params=_cp(("parallel",)))(proj, proj, proj, proj, cos, sin)
    return dict(ks_ext=res[0], ks_t=res[1], vs=res[2], vs_t=res[3], kw=res[4], kw_t=res[5], vw=res[6], vw_t=res[7])


def _dkv_post(dks, dvs, dkw, dvw, cos, sin):
    S = dks.shape[1]
    tr = _pick(S, (512, 256, 128))

    def body(dks_ref, dvs_ref, dkw_ref, dvw_ref, c_ref, s_ref, o_ref):
        cv, sv = c_ref[...], -s_ref[...]
        lane = lax.broadcasted_iota(jnp.int32, (tr, 128), 1)
        first = (lane & (HD - 1)) < (ROPE_DIM // 2)
        for j, (ref, rotated) in enumerate(((dks_ref, True), (dvs_ref, False), (dkw_ref, True), (dvw_ref, False))):
            for half in range(2):
                xv = jnp.concatenate([ref[2 * half], ref[2 * half + 1]], axis=1)
                if rotated:
                    xv = _rotate128(xv, cv, sv, first)
                o_ref[:, j * 256 + half * 128:j * 256 + (half + 1) * 128] = xv.astype(o_ref.dtype)

    hm = pl.BlockSpec((N_KV, tr, HD), lambda i: (0, i, 0))
    t128 = pl.BlockSpec((tr, 128), lambda i: (i, 0))
    return pl.pallas_call(
        body, name="dkv_post", grid=(S // tr,), in_specs=[hm, hm, hm, hm, t128, t128],
        out_specs=pl.BlockSpec((tr, 4 * 256), lambda i: (i, 0)), out_shape=jax.ShapeDtypeStruct((S, 4 * 256), _MXU),
        compiler_params=_cp(("parallel",)))(dks, dvs, dkw, dvw, cos, sin)


def _compress_fwd(R, pe, w1, w2):
    NC = R.shape[1]
    half = 16 * HD

    def body(r_ref, pe_ref, w1_ref, w2_ref, o_ref, hid_ref):
        r = r_ref[0]
        a = _dot(r + pe_ref[:, 0:half], w1_ref[0:half, :], "nn")
        b = _dot(r + pe_ref[:, half:2 * half], w1_ref[half:2 * half, :], "nn")
        hid = a + pltpu.roll(b, NC - 1, axis=0)
        hid_ref[0] = hid
        out = _dot(hid * _sigmoid(hid), w2_ref[...], "nn")
        rows = lax.broadcasted_iota(jnp.int32, out.shape, 0)
        o_ref[0] = jnp.where(rows < NC - 1, out, 0.0).astype(o_ref.dtype)

    return pl.pallas_call(
        body, name="compress_fwd", grid=(N_KV,),
        in_specs=[pl.BlockSpec((1, NC, half), lambda h: (h, 0, 0)), pl.BlockSpec((1, 2 * half), lambda h: (0, 0)),
                  pl.BlockSpec((2 * half, CMP_HID), lambda h: (0, 0)), pl.BlockSpec((CMP_HID, HD), lambda h: (0, 0))],
        out_specs=[pl.BlockSpec((1, NC, HD), lambda h: (h, 0, 0)), pl.BlockSpec((1, NC, CMP_HID), lambda h: (h, 0, 0))],
        out_shape=[jax.ShapeDtypeStruct((N_KV, NC, HD), _MXU), jax.ShapeDtypeStruct((N_KV, NC, CMP_HID), F32)],
        compiler_params=_cp(("parallel",)))(R, pe, w1, w2)


def _compress_bwd(R, pe, w1, w2, hid, dout):
    NC = R.shape[1]
    half = 16 * HD

    def body(r_ref, pe_ref, w1_ref, w2_ref, hid_ref, do_ref, dr_ref, dw1_ref, dw2_ref, dpe_ref):
        @pl.when(pl.program_id(0) == 0)
        def _():
            dw1_ref[...] = jnp.zeros_like(dw1_ref)
            dw2_ref[...] = jnp.zeros_like(dw2_ref)
            dpe_ref[...] = jnp.zeros_like(dpe_ref)

        r, hv, do = r_ref[0], hid_ref[0], do_ref[0]
        s = _sigmoid(hv)
        dw2_ref[...] += _dot(hv * s, do, "tn")
        dhid = _dot(do, w2_ref[...], "nt") * (s * (1.0 + hv * (1.0 - s)))
        rows = lax.broadcasted_iota(jnp.int32, dhid.shape, 0)
        dhid = jnp.where(rows < NC - 1, dhid, 0.0)
        dhid_dn = pltpu.roll(dhid, 1, axis=0)
        dw1_ref[0:half, :] += _dot(r + pe_ref[:, 0:half], dhid, "tn")
        dw1_ref[half:2 * half, :] += _dot(r + pe_ref[:, half:2 * half], dhid_dn, "tn")
        dxt = _dot(dhid, w1_ref[0:half, :], "nt")
        dxb = _dot(dhid_dn, w1_ref[half:2 * half, :], "nt")
        dr_ref[0] = dxt + dxb
        dpe_ref[:, 0:half] += jnp.sum(dxt, axis=0, keepdims=True)
        dpe_ref[:, half:2 * half] += jnp.sum(dxb, axis=0, keepdims=True)

    return pl.pallas_call(
        body, name="compress_bwd", grid=(N_KV,),
        in_specs=[pl.BlockSpec((1, NC, half), lambda h: (h, 0, 0)), pl.BlockSpec((1, 2 * half), lambda h: (0, 0)),
                  pl.BlockSpec((2 * half, CMP_HID), lambda h: (0, 0)), pl.BlockSpec((CMP_HID, HD), lambda h: (0, 0)),
                  pl.BlockSpec((1, NC, CMP_HID), lambda h: (h, 0, 0)), pl.BlockSpec((1, NC, HD), lambda h: (h, 0, 0))],
        out_specs=[pl.BlockSpec((1, NC, half), lambda h: (h, 0, 0)), pl.BlockSpec((2 * half, CMP_HID), lambda h: (0, 0)),
                   pl.BlockSpec((CMP_HID, HD), lambda h: (0, 0)), pl.BlockSpec((1, 2 * half), lambda h: (0, 0))],
        out_shape=[jax.ShapeDtypeStruct((N_KV, NC, half), F32), jax.ShapeDtypeStruct((2 * half, CMP_HID), F32),
                   jax.ShapeDtypeStruct((CMP_HID, HD), F32), jax.ShapeDtypeStruct((1, 2 * half), F32)],
        compiler_params=_cp(("arbitrary",)))(R, pe, w1, w2, hid, dout)


def _attn_cfg(S, Sk, mode):
    tk = _pick(Sk, (256, 128))
    if mode == "cmp":
        return _pick(S, (512, 256, 128)), Sk
    if mode == "sel" and S % (2 * tk) == 0:
        return 2 * tk, tk
    return tk, tk


def _block_start(kb, tk):
    return kb * tk if isinstance(kb, int) else pl.multiple_of(kb * tk, tk)


def _pipelined_key_blocks(mode, q0, tq, tk, produce, consume):
    if mode == "cmp":
        produce(0, True, 0)
        consume(0, 0)
        return
    if mode == "win":
        assert tq == tk and WINDOW == 2 * tk
        last = q0 // tk
        first = jnp.maximum(last - 2, 0)

        @pl.when(last == 0)
        def _():
            produce(last, True, 0)
            consume(last, 0)

        @pl.when(last == 1)
        def _():
            produce(first, True, 0)
            produce(last, True, 1)
            consume(first, 0)
            consume(last, 1)

        @pl.when(last >= 2)
        def _():
            produce(first, True, 0)
            produce(first + 1, False, 1)
            consume(first, 0)
            produce(last, True, 0)
            consume(first + 1, 1)
            consume(last, 0)

        return
    first, n_plain, plain_masked = 0, q0 // tk, False
    pairs = jnp.maximum(n_plain - 1, 0) // 2
    if tq == 2 * tk:
        @pl.when(n_plain >= 1)
        def _():
            produce(0, False, 0)

        def two_plain(j, carry):
            produce(2 * j + 1, False, 1)
            consume(2 * j, 0)
            produce(2 * j + 2, False, 0)
            consume(2 * j + 1, 1)
            return carry

        lax.fori_loop(0, pairs, two_plain, 0)
        kb = 2 * pairs

        @pl.when(n_plain >= 2)
        def _():
            produce(kb + 1, False, 1)
            consume(kb, 0)
            produce(n_plain, True, 0)
            consume(kb + 1, 1)
            produce(n_plain + 1, True, 1)
            consume(n_plain, 0)
            consume(n_plain + 1, 1)

        @pl.when(n_plain == 0)
        def _():
            produce(0, True, 0)
            produce(1, True, 1)
            consume(0, 0)
            consume(1, 1)

        return
    assert tq == tk
    last = first + n_plain

    @pl.when(n_plain >= 1)
    def _():
        produce(first, plain_masked, 0)

    def two(j, carry):
        kb = first + 2 * j
        produce(kb + 1, plain_masked, 1)
        consume(kb, 0)
        produce(kb + 2, plain_masked, 0)
        consume(kb + 1, 1)
        return carry

    lax.fori_loop(0, pairs, two, 0)
    kb = first + 2 * pairs
    left = n_plain - 2 * pairs

    @pl.when(left == 2)
    def _():
        produce(kb + 1, plain_masked, 1)
        consume(kb, 0)
        produce(last, True, 0)
        consume(kb + 1, 1)
        consume(last, 0)

    @pl.when(left == 1)
    def _():
        produce(last, True, 1)
        consume(kb, 0)
        consume(last, 1)

    @pl.when(left == 0)
    def _():
        produce(last, True, 0)
        consume(last, 0)


def _attn_bias(mode, q0, k0, tq, tk):
    k = k0 + lax.broadcasted_iota(jnp.int32, (tk, tq), 0)
    t = q0 + lax.broadcasted_iota(jnp.int32, (tk, tq), 1)
    if mode == "cmp":
        ok = (k * 16 + 31) <= t
    elif mode == "win":
        ok = (k <= t) & ((t - k) < WINDOW)
    else:
        ok = k <= t
    bias = jnp.where(ok, 0.0, NEG)
    return jnp.concatenate([bias] * GRP, axis=1), jnp.concatenate([ok.astype(F32)] * GRP, axis=1)


def _sel_operands(qs, selneg_ref):
    return jnp.concatenate([qs, jnp.concatenate([selneg_ref[0]] * GRP, axis=0)], axis=1)


def _stack_heads(ref, tq):
    return jnp.concatenate([ref[:, g * HD:(g + 1) * HD] for g in range(GRP)], axis=0)


def _scaled_queries(q_ref, tq):
    return (_stack_heads(q_ref, tq).astype(F32) * SCALE).astype(_MXU)


def _blocked_t(x, tk):
    n, Sk, d = x.shape
    return x.reshape(n, Sk // tk, tk, d).transpose(0, 1, 3, 2)


def _head_rows(ref):
    return jnp.concatenate([ref[0, g:g + 1, :] for g in range(GRP)], axis=1)


def _attn_fwd(q, qcol0, k, vt, mode, selneg, gate, y_prev, y_dtype, name):
    S, Sk = q.shape[0], k.shape[1]
    tq, tk = _attn_cfg(S, Sk, mode)
    R = GRP * tq

    def body(*refs):
        q_ref, k_ref, vt_ref = refs[:3]
        rest = list(refs[3:])
        sel_ref = rest.pop(0) if mode == "sel" else None
        gate_ref = rest.pop(0)
        yp_ref = rest.pop(0) if y_prev is not None else None
        o_ref, lse_ref, y_ref, m_scr, l_scr, acc, s_scr = rest
        q0 = pl.program_id(1) * tq
        qs = _scaled_queries(q_ref, tq)
        m_scr[...] = jnp.full_like(m_scr, NEG)
        l_scr[...] = jnp.zeros_like(l_scr)
        acc[...] = jnp.zeros_like(acc)
        qk = _sel_operands(qs, sel_ref) if mode == "sel" else qs

        def produce(kb, masked, slot):
            k0 = _block_start(kb, tk)
            s = _dot(k_ref[0, pl.ds(k0, tk), :], qk, "nt")
            if masked:
                s = s + _attn_bias(mode, q0, k0, tq, tk)[0]
            s_scr[slot] = s

        def consume(kb, slot):
            s = s_scr[slot]
            m_old = m_scr[...]
            m_new = jnp.maximum(m_old, jnp.max(s, axis=0, keepdims=True))
            p = jnp.exp(s - m_new)
            if mode == "cmp":
                p = p * _attn_bias(mode, q0, 0, tq, tk)[1]
            alpha = jnp.exp(m_old - m_new)
            l_scr[...] = alpha * l_scr[...] + jnp.sum(p, axis=0, keepdims=True)
            acc[...] = alpha * acc[...] + _dot(vt_ref[0, kb], p, "nn")
            m_scr[...] = m_new

        _pipelined_key_blocks(mode, q0, tq, tk, produce, consume)
        l = l_scr[...]
        good = l > 0.0
        o_t = acc[...] * jnp.where(good, 1.0 / jnp.where(good, l, 1.0), 0.0)
        lse = jnp.where(good, m_scr[...] + jnp.log(jnp.where(good, l, 1.0)), -NEG)
        y_t = o_t * _sigmoid(_head_rows(gate_ref))
        for g in range(GRP):
            hs, qs_ = slice(g * HD, (g + 1) * HD), slice(g * tq, (g + 1) * tq)
            o_ref[:, hs] = o_t[:, qs_].T
            lse_ref[0, g:g + 1, :] = lse[:, qs_]
            yg = y_t[:, qs_].T
            if y_prev is not None:
                yg = yg + yp_ref[:, hs]
            y_ref[:, hs] = yg.astype(y_ref.dtype)

    row_spec = pl.BlockSpec((1, GRP, tq), lambda h, i: (h, 0, i))
    qo_spec = pl.BlockSpec((tq, GRP * HD), lambda h, i: (i, h))
    ins = [q, k, vt]
    specs = [pl.BlockSpec((tq, GRP * HD), lambda h, i: (i, qcol0 + h)), pl.BlockSpec((1, Sk, k.shape[2]), lambda h, i: (h, 0, 0)),
             pl.BlockSpec((1, Sk // tk, HD, tk), lambda h, i: (h, 0, 0, 0))]
    if mode == "sel":
        ins.append(selneg)
        specs.append(pl.BlockSpec((1, tq, selneg.shape[2]), lambda h, i: (h, i, 0)))
    ins.append(gate)
    specs.append(row_spec)
    if y_prev is not None:
        ins.append(y_prev)
        specs.append(qo_spec)
    return pl.pallas_call(
        body, name=name, grid=(N_KV, S // tq), in_specs=specs, out_specs=[qo_spec, row_spec, qo_spec],
        out_shape=[jax.ShapeDtypeStruct((S, ATT_WIDTH), F32), jax.ShapeDtypeStruct((N_KV, GRP, S), F32),
                   jax.ShapeDtypeStruct((S, ATT_WIDTH), y_dtype)],
        scratch_shapes=[pltpu.VMEM((1, R), F32), pltpu.VMEM((1, R), F32), pltpu.VMEM((HD, R), F32), pltpu.VMEM((2, tk, R), F32)],
        compiler_params=_cp(("parallel", "arbitrary")))(*ins)


def _attn_bwd(q, qcol0, k, kt, v, o, lse, dy, dycol0, gate, mode, selneg, name):
    S, Sk = q.shape[0], k.shape[1]
    tq, tk = _attn_cfg(S, Sk, mode)
    R = GRP * tq

    def body(*refs):
        if mode == "sel":
            (q_ref, k_ref, kt_ref, v_ref, o_ref, lse_ref, dy_ref, gate_ref, sel_ref, dq_ref, dk_ref, dv_ref, dg_ref, dq_scr, s_scr,
             dp_scr) = refs
        else:
            q_ref, k_ref, kt_ref, v_ref, o_ref, lse_ref, dy_ref, gate_ref, dq_ref, dk_ref, dv_ref, dg_ref, dq_scr, s_scr, dp_scr = refs

        @pl.when(pl.program_id(1) == 0)
        def _():
            dk_ref[...] = jnp.zeros_like(dk_ref)
            dv_ref[...] = jnp.zeros_like(dv_ref)

        q0 = pl.program_id(1) * tq
        qs = _scaled_queries(q_ref, tq)
        dys = _stack_heads(dy_ref, tq)
        gv = _sigmoid(_head_rows(gate_ref))
        dy_o = _dot(jnp.ones((8, HD), F32), dys * _stack_heads(o_ref, tq), "nt", split="b")[0:1, :]
        delta = gv * dy_o
        dgate = dy_o * (gv * (1.0 - gv))
        for g in range(GRP):
            dg_ref[0, g:g + 1, :] = dgate[:, g * tq:(g + 1) * tq]
        lsev = _head_rows(lse_ref)
        dos = (dys * jnp.broadcast_to(gv, (8, R)).T[:, 0:1]).astype(_MXU)
        dq_scr[...] = jnp.zeros_like(dq_scr)
        qk = _sel_operands(qs, sel_ref) if mode == "sel" else qs

        def produce(kb, masked, slot):
            k0 = _block_start(kb, tk)
            s = _dot(k_ref[0, pl.ds(k0, tk), :], qk, "nt")
            if masked:
                s = s + _attn_bias(mode, q0, k0, tq, tk)[0]
            s_scr[slot] = s
            dp_scr[slot] = _dot(v_ref[0, pl.ds(k0, tk), :], dos, "nt")

        def consume(kb, slot):
            k0 = _block_start(kb, tk)
            p = jnp.exp(s_scr[slot] - lsev)
            if mode == "cmp":
                p = p * _attn_bias(mode, q0, 0, tq, tk)[1]
            ds = p * (dp_scr[slot] - delta)
            dq_scr[...] += _dot(kt_ref[0, kb], ds, "nn")
            dk_ref[0, pl.ds(k0, tk), :] += _dot(ds, qs, "nn")
            dv_ref[0, pl.ds(k0, tk), :] += _dot(p, dos, "nn")

        _pipelined_key_blocks(mode, q0, tq, tk, produce, consume)
        for g in range(GRP):
            dq_ref[:, g * HD:(g + 1) * HD] = (dq_scr[:, g * tq:(g + 1) * tq] * SCALE).T

    kv_spec = pl.BlockSpec((1, Sk, HD), lambda h, i: (h, 0, 0))
    qo_spec = pl.BlockSpec((tq, GRP * HD), lambda h, i: (i, h))
    row_spec = pl.BlockSpec((1, GRP, tq), lambda h, i: (h, 0, i))
    ins = [q, k, kt, v, o, lse, dy, gate]
    specs = [pl.BlockSpec((tq, GRP * HD), lambda h, i: (i, qcol0 + h)), pl.BlockSpec((1, Sk, k.shape[2]), lambda h, i: (h, 0, 0)),
             pl.BlockSpec((1, Sk // tk, HD, tk), lambda h, i: (h, 0, 0, 0)), kv_spec, qo_spec, row_spec,
             pl.BlockSpec((tq, GRP * HD), lambda h, i: (i, dycol0 + h)), row_spec]
    if mode == "sel":
        ins.append(selneg)
        specs.append(pl.BlockSpec((1, tq, selneg.shape[2]), lambda h, i: (h, i, 0)))
    return pl.pallas_call(
        body, name=name, grid=(N_KV, S // tq), in_specs=specs, out_specs=[qo_spec, kv_spec, kv_spec, row_spec],
        out_shape=[jax.ShapeDtypeStruct((S, ATT_WIDTH), F32), jax.ShapeDtypeStruct((N_KV, Sk, HD), F32),
                   jax.ShapeDtypeStruct((N_KV, Sk, HD), F32), jax.ShapeDtypeStruct((N_KV, GRP, S), F32)],
        scratch_shapes=[pltpu.VMEM((HD, R), F32), pltpu.VMEM((2, tk, R), F32), pltpu.VMEM((2, tk, R), F32)],
        compiler_params=_cp(("parallel", "arbitrary")))(*ins)


def _select(q, qcol0, k_cmp, lse):
    S, NC = q.shape[0], k_cmp.shape[1]
    NB = S // SEL_BLOCK
    tq = _attn_cfg(S, NC, "cmp")[0]
    ci = np.arange(NC)[None, :] * 16
    sj = np.arange(NB)[:, None] * SEL_BLOCK
    ov_t = np.clip(np.minimum(ci + 32, sj + SEL_BLOCK) - np.maximum(ci, sj), 0, None) / 32.0
    ov_t[:, NC - 1] = 0.0
    ov_t = jnp.asarray(ov_t, F32)

    def body(q_ref, k_ref, lse_ref, ov_ref, sel_ref):
        q0 = pl.program_id(1) * tq
        bias, okf = _attn_bias("cmp", q0, 0, tq, NC)
        lsev = _head_rows(lse_ref)
        p = jnp.exp(_dot(k_ref[0], _scaled_queries(q_ref, tq), "nt") + bias - lsev) * okf
        imp4 = _dot(ov_ref[...], p, "nn")
        imp = imp4[:, 0:tq] + imp4[:, tq:2 * tq] + imp4[:, 2 * tq:3 * tq] + imp4[:, 3 * tq:4 * tq]
        blk = lax.broadcasted_iota(jnp.int32, (NB, tq), 0)
        cur = lax.shift_right_logical(q0 + lax.broadcasted_iota(jnp.int32, (NB, tq), 1), 6)
        imp = jnp.where((blk == 0) | (blk == cur) | (blk == cur - 1), FORCE, imp)
        imp = jnp.where(blk <= cur, imp, -1.0)
        rank = jnp.zeros((NB, tq), F32)
        for j in range(NB):
            row = imp[j:j + 1, :]
            ahead = (row > imp) | ((row == imp) & (blk > j))
            rank = rank + ahead.astype(F32)
        chosen = (rank < float(N_SELECT)) & (imp >= 0.0)
        sel_ref[0] = jnp.where(chosen, 0.0, NEG).T.astype(sel_ref.dtype)

    return pl.pallas_call(
        body, name="select_blocks", grid=(N_KV, S // tq),
        in_specs=[pl.BlockSpec((tq, GRP * HD), lambda h, i: (i, qcol0 + h)), pl.BlockSpec((1, NC, HD), lambda h, i: (h, 0, 0)),
                  pl.BlockSpec((1, GRP, tq), lambda h, i: (h, 0, i)), pl.BlockSpec((NB, NC), lambda h, i: (0, 0))],
        out_specs=pl.BlockSpec((1, tq, NB), lambda h, i: (h, i, 0)),
        out_shape=jax.ShapeDtypeStruct((N_KV, S, NB), _MXU), compiler_params=_cp(("parallel", "parallel")))(q, k_cmp, lse, ov_t)


def _to_rows16(x):
    S = x.shape[0]
    return x.reshape(S // 16, 16, N_KV, HD).transpose(2, 0, 1, 3).reshape(N_KV, S // 16, 16 * HD)


def _from_rows16(r):
    NC = r.shape[1]
    return r.reshape(N_KV, NC, 16, HD).transpose(1, 2, 0, 3).reshape(NC * 16, N_KV * HD)


DT_COL0 = SSD_WIDTH + CONV_CH
GATE_IN_COL0 = D_IN - 3 * N_HEADS


SHARD_IN = D_IN // N_DEV


def _orig_cols(ref, c0, width):
    pieces, c = [], c0
    while c < c0 + width:
        d, off = divmod(c, SHARD_IN)
        w = min(SHARD_IN - off, c0 + width - c)
        pieces.append(ref[d, :, off:off + w])
        c += w
    return pieces[0] if len(pieces) == 1 else jnp.concatenate(pieces, axis=1)


def _cols_from_slabs(slabs):
    _, R, c = slabs.shape
    tr = _pick(R, (256, 128))

    def body(s_ref, o_ref):
        for t in range(N_DEV * c // LANE):
            pieces, col = [], t * LANE
            while col < (t + 1) * LANE:
                d, off = divmod(col, c)
                w = min(c - off, (t + 1) * LANE - col)
                pieces.append(s_ref[d, :, off:off + w])
                col += w
            o_ref[:, t * LANE:(t + 1) * LANE] = pieces[0] if len(pieces) == 1 else jnp.concatenate(pieces, axis=1)

    return pl.pallas_call(
        body, name="cols_from_slabs", grid=(R // tr,), in_specs=[pl.BlockSpec((N_DEV, tr, c), lambda i: (0, i, 0))],
        out_specs=pl.BlockSpec((tr, N_DEV * c), lambda i: (i, 0)), out_shape=jax.ShapeDtypeStruct((R, N_DEV * c), slabs.dtype),
        compiler_params=_cp(("parallel",)))(slabs)


def _slabs_from_cols(x):
    R, c = x.shape[0], x.shape[1] // N_DEV
    tr = _pick(R, (256, 128))

    def body(x_ref, o_ref):
        for d in range(N_DEV):
            o_ref[d] = x_ref[:, d * c:(d + 1) * c]

    return pl.pallas_call(
        body, name="slabs_from_cols", grid=(R // tr,), in_specs=[pl.BlockSpec((tr, N_DEV * c), lambda i: (i, 0))],
        out_specs=pl.BlockSpec((N_DEV, tr, c), lambda i: (0, i, 0)), out_shape=jax.ShapeDtypeStruct((N_DEV, R, c), x.dtype),
        compiler_params=_cp(("parallel",)))(x)


def _w_in_from_slabs(slabs):
    D = slabs.shape[1]
    tr = _pick(D, (256, 128))

    def body(s_ref, main_ref, small_ref):
        for t in range(W_MAIN // LANE):
            c = t * LANE
            main_ref[:, c:c + LANE] = _orig_cols(s_ref, c if c < DT_COL0 else c + SSD_HEADS, LANE)
        small_ref[...] = jnp.concatenate(
            [_orig_cols(s_ref, DT_COL0, SSD_HEADS), _orig_cols(s_ref, GATE_IN_COL0, 3 * N_HEADS),
             jnp.zeros((tr, W_SMALL - SSD_HEADS - 3 * N_HEADS), small_ref.dtype)], axis=1)

    return pl.pallas_call(
        body, name="w_in_layout", grid=(D // tr,), in_specs=[pl.BlockSpec((N_DEV, tr, SHARD_IN), lambda i: (0, i, 0))],
        out_specs=[pl.BlockSpec((tr, W_MAIN), lambda i: (i, 0)), pl.BlockSpec((tr, W_SMALL), lambda i: (i, 0))],
        out_shape=[jax.ShapeDtypeStruct((D, W_MAIN), slabs.dtype), jax.ShapeDtypeStruct((D, W_SMALL), slabs.dtype)],
        compiler_params=_cp(("parallel",)))(slabs)


def _w_in_to_slabs(main, small):
    D = main.shape[0]
    tr = _pick(D, (256, 128))
    ranges = [(0, DT_COL0, 0, 0), (DT_COL0, DT_COL0 + SSD_HEADS, 1, 0), (DT_COL0 + SSD_HEADS, GATE_IN_COL0, 0, DT_COL0),
              (GATE_IN_COL0, D_IN, 1, SSD_HEADS)]

    def body(main_ref, small_ref, o_ref):
        srcs = (main_ref, small_ref)
        for d in range(N_DEV):
            lo, hi = d * SHARD_IN, (d + 1) * SHARD_IN
            pieces = []
            for start, stop, which, s0 in ranges:
                a, b = max(lo, start), min(hi, stop)
                if a < b:
                    pieces.append(srcs[which][:, s0 + a - start:s0 + b - start].astype(o_ref.dtype))
            o_ref[d] = pieces[0] if len(pieces) == 1 else jnp.concatenate(pieces, axis=1)

    return pl.pallas_call(
        body, name="w_in_grad_layout", grid=(D // tr,),
        in_specs=[pl.BlockSpec((tr, W_MAIN), lambda i: (i, 0)), pl.BlockSpec((tr, W_SMALL), lambda i: (i, 0))],
        out_specs=pl.BlockSpec((N_DEV, tr, SHARD_IN), lambda i: (0, i, 0)),
        out_shape=jax.ShapeDtypeStruct((N_DEV, D, SHARD_IN), main.dtype), compiler_params=_cp(("parallel",)))(main, small)


QB, KCB, VCB, KSB, VSB, KWB, VWB = 10, 14, 15, 16, 17, 18, 19


def _col256(a, b):
    return a[:, b * 256:(b + 1) * 256]


_EARLY = ["w_in", "cmp_w1_k", "cmp_w1_v"]
_LATE = ["w_out", "w_gate", "w_up", "w_down"]
_FFN = ["w_down", "w_gate", "w_up"]
_MID = ["w_out"]
_LAST = ["cmp_w1_k", "cmp_w1_v", "w_in"]


def _local_step(x, tgt, p, late_weights=None, grads_ready=None):
    S = x.shape[0]
    cos, sin = _rope_tables(S)

    u, rs1 = _rms_fwd(x, p["attn_norm_w"], "attn_norm")
    proj = _mm(u, p["w_main"], "nn", F32, "in_proj", after=p.get("before_in_proj"))
    proj_small = _mm(u, p["w_small"], "nn", F32, "in_proj_small")
    xa = _conv_fwd(proj, p["conv_w"], p["conv_b"])
    y_ssd, y_pre, rs_ssd, hs = _ssd_fwd(proj, proj_small, xa, p["dt_bias"], p["a_log"], p["d_skip"], p["ssd_norm_w"])

    q_rot = _rope([proj], QB, ATT_WIDTH, cos, sin, 1.0, _MXU, "rope_q")
    kv = _kv_prep(proj, cos, sin, _attn_cfg(S, S, "sel")[1])
    rk, rv = _to_rows16(_col256(proj, KCB)), _to_rows16(_col256(proj, VCB))
    k_cmp, hid_k = _compress_fwd(rk, p["cmp_pe_k"], p["cmp_w1_k"], p["cmp_w2_k"])
    v_cmp, hid_v = _compress_fwd(rv, p["cmp_pe_v"], p["cmp_w1_v"], p["cmp_w2_v"])
    n_cmp = k_cmp.shape[1]

    gates = proj_small[:, SSD_HEADS:SSD_HEADS + 3 * N_HEADS].reshape(S, N_KV, GRP, 3).transpose(3, 1, 2, 0)
    o_cmp, lse_cmp, y_att = _attn_fwd(proj, QB, k_cmp, _blocked_t(v_cmp, n_cmp), "cmp", None, gates[0], None, F32, "attn_cmp_fwd")
    sel = _select(proj, QB, k_cmp, lse_cmp)
    o_sel, lse_sel, y_att = _attn_fwd(q_rot, 0, kv["ks_ext"], kv["vs_t"], "sel", sel, gates[1], y_att, F32, "attn_sel_fwd")
    o_win, lse_win, y_att = _attn_fwd(q_rot, 0, kv["kw"], kv["vw_t"], "win", None, gates[2], y_att, _MXU, "attn_win_fwd")

    if late_weights is not None:
        p = {**p, **late_weights(y_att)}
    mixed = jnp.concatenate([y_ssd, y_att], axis=1)
    h1 = _mm(mixed, p["w_out"], "nn", F32, "out_proj", res=x)
    v, rs_ffn = _rms_fwd(h1, p["ffn_norm_w"], "ffn_norm")
    gt, up, act = _ffn_up(v, p["w_gate"], p["w_up"])
    h2 = _mm(act, p["w_down"], "nn", F32, "ffn_down", res=h1)
    loss, dh2, dh2b, d_final_w = _final_loss(h2, p["final_norm_w"], tgt)

    def ready(names):
        return None if grads_ready is None else grads_ready(names, g)

    g = {"final_norm_w": d_final_w}
    g["w_down"] = _mm(act, dh2b, "tn", _MXU, "dw_down")
    dgt, dup = _ffn_dact(dh2b, p["w_down"], gt, up)
    g["w_gate"] = _mm(v, dgt, "tn", _MXU, "dw_gate")
    g["w_up"] = _mm(v, dup, "tn", _MXU, "dw_up")
    dv = _ffn_dv(dgt, dup, p["w_gate"], p["w_up"], ready(_FFN))
    dh1, dh1b, g["ffn_norm_w"] = _rms_bwd(dv, h1, rs_ffn, p["ffn_norm_w"], dh2, "ffn_norm_bwd")
    g["w_out"] = _mm(mixed, dh1b, "tn", _MXU, "dw_out")
    dmixed = _mm(dh1b, p["w_out"], "nt", F32, "dmixed", after=ready(_MID))

    dz, dxa, ddtr, g["dt_bias"], g["a_log"], g["d_skip"], g["ssd_norm_w"] = _ssd_bwd(
        dmixed, proj, proj_small, xa, y_pre, rs_ssd, hs, p["dt_bias"], p["a_log"], p["d_skip"], p["ssd_norm_w"])
    dxbc, g["conv_w"], g["conv_b"] = _conv_bwd(proj, p["conv_w"], p["conv_b"], dxa)

    dyb = SSD_WIDTH // (GRP * HD)
    dq_cmp, dk_cmp, dv_cmp, dg_cmp = _attn_bwd(proj, QB, k_cmp, _blocked_t(k_cmp, n_cmp), v_cmp, o_cmp, lse_cmp, dmixed, dyb,
                                               gates[0], "cmp", None, "attn_cmp_bwd")
    dq_sel, dks, dvs, dg_sel = _attn_bwd(q_rot, 0, kv["ks_ext"], kv["ks_t"], kv["vs"], o_sel, lse_sel, dmixed, dyb, gates[1], "sel",
                                         sel, "attn_sel_bwd")
    dq_win, dkw, dvw, dg_win = _attn_bwd(q_rot, 0, kv["kw"], kv["kw_t"], kv["vw"], o_win, lse_win, dmixed, dyb, gates[2], "win", None,
                                         "attn_win_bwd")
    dgate = jnp.stack([dg_cmp, dg_sel, dg_win]).transpose(3, 1, 2, 0).reshape(S, 3 * N_HEADS)
    drk, g["cmp_w1_k"], g["cmp_w2_k"], g["cmp_pe_k"] = _compress_bwd(rk, p["cmp_pe_k"], p["cmp_w1_k"], p["cmp_w2_k"], hid_k, dk_cmp)
    drv, g["cmp_w1_v"], g["cmp_w2_v"], g["cmp_pe_v"] = _compress_bwd(rv, p["cmp_pe_v"], p["cmp_w1_v"], p["cmp_w2_v"], hid_v, dv_cmp)
    dq = _rope([dq_sel, dq_win], 0, ATT_WIDTH, cos, sin, -1.0, _MXU, "rope_dq", extra=(dq_cmp, 0))
    dkv = _dkv_post(dks, dvs, dkw, dvw, cos, sin)
    dproj = jnp.concatenate([dz, dxbc, dq] + [t.astype(_MXU) for t in (_from_rows16(drk), _from_rows16(drv))] + [dkv], axis=1)
    dsmall = jnp.concatenate([ddtr, dgate, jnp.zeros((S, W_SMALL - SSD_HEADS - 3 * N_HEADS), F32)], axis=1).astype(_MXU)
    g["w_main"] = _mm(u, dproj, "tn", _MXU, "dw_in")
    g["w_small"] = _mm(u, dsmall, "tn", F32, "dw_in_small")
    du = _mm(dproj, p["w_main"], "nt", F32, "du_main", after=ready(_LAST))
    du = _mm(dsmall, p["w_small"], "nt", F32, "du_small", res=du)
    grad_x, _, g["attn_norm_w"] = _rms_bwd(du, x, rs1, p["attn_norm_w"], dh1, "attn_norm_bwd")
    return loss, grad_x, g


MESH_ID = pl.DeviceIdType.MESH


def _my_coords():
    return lax.axis_index("x"), lax.axis_index("y"), lax.axis_index("c")


def _flat_id(px, py, pc):
    return 4 * px + 2 * py + pc


def _peer(k):
    mx, my, mc = _my_coords()
    return (1 - mx if k & 4 else mx, 1 - my if k & 2 else my, 1 - mc if k & 1 else mc)


def _exchange(arrs, scatter, name, after=()):
    n, na = len(arrs), len(after)
    scatter = [scatter] * n if isinstance(scatter, bool) else list(scatter)

    def body(*refs):
        ins, outs = refs[:n], refs[n + na:2 * n + na]
        send_sems, recv_sems, local_sems = refs[2 * n + na:]
        me = _flat_id(*_my_coords())
        copies = []
        for i in range(n):
            src_me = ins[i].at[me] if scatter[i] else ins[i]
            local = pltpu.make_async_copy(src_me, outs[i].at[me], local_sems.at[i])
            local.start()
            copies.append(local)
        for k in range(1, N_DEV):
            peer = _peer(k)
            for i in range(n):
                src = ins[i].at[_flat_id(*peer)] if scatter[i] else ins[i]
                cp = pltpu.make_async_remote_copy(src_ref=src, dst_ref=outs[i].at[me], send_sem=send_sems.at[i * 7 + k - 1],
                                                  recv_sem=recv_sems.at[i * 7 + k - 1], device_id=peer, device_id_type=MESH_ID)
                cp.start()
                copies.append(cp)
        for cp in copies:
            cp.wait()

    any_spec = pl.BlockSpec(memory_space=pl.ANY)
    out_shape = [jax.ShapeDtypeStruct(a.shape if sc else (N_DEV,) + a.shape, a.dtype) for a, sc in zip(arrs, scatter)]
    return pl.pallas_call(
        body, name=name, in_specs=[any_spec] * (n + na), out_specs=[any_spec] * n, out_shape=out_shape,
        scratch_shapes=[pltpu.SemaphoreType.DMA((n * 7,)), pltpu.SemaphoreType.DMA((n * 7,)), pltpu.SemaphoreType.DMA((n,))],
        compiler_params=pltpu.CompilerParams(has_side_effects=True))(*arrs, *after)


def _gather_two_level(arrs, name):
    n = len(arrs)

    def body(*refs):
        ins, outs = refs[:n], refs[n:2 * n]
        send_sems, recv_sems, local_sems = refs[2 * n:]
        x, y, c = _my_coords()
        me, sibling = (x, y, c), (x, y, 1 - c)
        chips = [(1 - x, y), (x, 1 - y), (1 - x, 1 - y)]

        def copy(i, k, block, to, src=None):
            slot = outs[i].at[_flat_id(*block)]
            return pltpu.make_async_remote_copy(src_ref=slot if src is None else src, dst_ref=slot, send_sem=send_sems.at[i * 7 + k],
                                                recv_sem=recv_sems.at[i * 7 + k], device_id=to, device_id_type=MESH_ID)

        mine = [pltpu.make_async_copy(ins[i], outs[i].at[_flat_id(*me)], local_sems.at[i]) for i in range(n)]
        for cp in mine:
            cp.start()
        first = []
        for j, chip in enumerate(chips):
            first += [copy(i, 1 + j, me, (*chip, c), src=ins[i]) for i in range(n)]
        first += [copy(i, 0, me, sibling, src=ins[i]) for i in range(n)]
        for cp in first:
            cp.start()
        passed = []
        for j, chip in enumerate(chips):
            for i in range(n):
                copy(i, 1 + j, (*chip, c), me).wait_recv()
                passed.append(copy(i, 4 + j, (*chip, c), sibling))
                passed[-1].start()
        for i in range(n):
            copy(i, 0, sibling, me).wait_recv()
        for j, chip in enumerate(chips):
            for i in range(n):
                copy(i, 4 + j, (*chip, 1 - c), me).wait_recv()
        for cp in first + passed:
            cp.wait_send()
        for cp in mine:
            cp.wait()

    any_spec = pl.BlockSpec(memory_space=pl.ANY)
    return pl.pallas_call(
        body, name=name, in_specs=[any_spec] * n, out_specs=[any_spec] * n,
        out_shape=[jax.ShapeDtypeStruct((N_DEV,) + a.shape, a.dtype) for a in arrs],
        scratch_shapes=[pltpu.SemaphoreType.DMA((n * 7,)), pltpu.SemaphoreType.DMA((n * 7,)), pltpu.SemaphoreType.DMA((n,))],
        compiler_params=pltpu.CompilerParams(has_side_effects=True))(*arrs)


_HBM = pl.BlockSpec(memory_space=pltpu.HBM)
_SEM = pl.BlockSpec(memory_space=pltpu.SEMAPHORE)
_EFFECT = pltpu.SideEffectType.DATAFLOW_SIDE_EFFECTING


def _split_copies(ins, lands, send_sems, recv_sems, own_sems, scatter):
    me = _flat_id(*_my_coords())
    remote = []
    for k in range(1, N_DEV):
        peer = _peer(k)
        for i in range(len(ins)):
            src = ins[i].at[_flat_id(*peer)] if scatter else ins[i]
            remote.append(pltpu.make_async_remote_copy(src_ref=src, dst_ref=lands[i].at[me], send_sem=send_sems.at[i * 7 + k - 1],
                                                       recv_sem=recv_sems.at[i * 7 + k - 1], device_id=peer, device_id_type=MESH_ID))
    own = [pltpu.make_async_copy(ins[i].at[me] if scatter else ins[i], lands[i].at[me], own_sems.at[i]) for i in range(len(ins))]
    return remote, own


def _split_start(arrs, scatter, name, after=()):
    n, na = len(arrs), len(after)

    def body(*refs):
        remote, own = _split_copies(refs[:n], refs[n:2 * n], refs[2 * n + na], refs[2 * n + na + 1], refs[2 * n + na + 2], scatter)
        for cp in remote + own:
            cp.start()
        refs[-1][...] = jnp.zeros_like(refs[-1])

    land_shapes = [a.shape if scatter else (N_DEV,) + a.shape for a in arrs]
    out_shape = ((pltpu.SemaphoreType.DMA((n * 7,)), pltpu.SemaphoreType.DMA((n * 7,)), pltpu.SemaphoreType.DMA((n,)))
                 + tuple(pltpu.HBM(a.shape, a.dtype) for a in arrs) + tuple(pltpu.HBM(s, a.dtype) for s, a in zip(land_shapes, arrs))
                 + (jax.ShapeDtypeStruct((8, 128), F32),))
    operands = ([pltpu.with_memory_space_constraint(a, pltpu.HBM) for a in arrs]
                + [pltpu.with_memory_space_constraint(lax.empty(s, a.dtype), pltpu.HBM) for s, a in zip(land_shapes, arrs)])
    res = pl.pallas_call(
        body, name=name, out_shape=out_shape, in_specs=[_HBM] * (2 * n) + [pl.BlockSpec(memory_space=pl.ANY)] * na,
        out_specs=(_SEM, _SEM, _SEM) + (_HBM,) * (2 * n) + (pl.BlockSpec(memory_space=pltpu.VMEM),),
        input_output_aliases={i: 3 + i for i in range(2 * n)},
        compiler_params=pltpu.CompilerParams(has_side_effects=_EFFECT))(*operands, *after)
    return dict(send=res[0], recv=res[1], own=res[2], ins=list(res[3:3 + n]), lands=list(res[3 + n:3 + 2 * n]), token=res[-1])


def _split_wait(st, scatter, after, name):
    n = len(st["ins"])

    def body(*refs):
        remote, own = _split_copies(refs[:n], refs[n:2 * n], refs[2 * n], refs[2 * n + 1], refs[2 * n + 2], scatter)
        for cp in remote:
            cp.wait_send()
            cp.wait_recv()
        for cp in own:
            cp.wait()

    arrs = st["ins"] + st["lands"]
    res = pl.pallas_call(
        body, name=name, out_shape=tuple(pltpu.HBM(a.shape, a.dtype) for a in arrs),
        in_specs=[_HBM] * (2 * n) + [_SEM, _SEM, _SEM] + [pl.BlockSpec(memory_space=pl.ANY)] * len(after), out_specs=(_HBM,) * (2 * n),
        input_output_aliases={i: i for i in range(2 * n)},
        compiler_params=pltpu.CompilerParams(has_side_effects=_EFFECT))(*arrs, st["send"], st["recv"], st["own"], *after)
    return list(res[n:])


def _adam_step(p_ref, w_ref, m_ref, v_ref, g_ref, d_ref, nm_ref, nv_ref):
    g = p_ref[0].astype(F32)
    for j in range(1, p_ref.shape[0]):
        g = g + p_ref[j].astype(F32)
    g_ref[...] = g
    nm = ADAM_B1 * m_ref[...] + (1.0 - ADAM_B1) * g
    nv = ADAM_B2 * v_ref[...] + (1.0 - ADAM_B2) * (g * g)
    nm_ref[...] = nm
    nv_ref[...] = nv
    m_hat = nm / (1.0 - ADAM_B1 ** ADAM_STEP)
    v_hat = nv / (1.0 - ADAM_B2 ** ADAM_STEP)
    d_ref[...] = -ADAM_LR * (m_hat / (jnp.sqrt(v_hat) + ADAM_EPS) + ADAM_WD * w_ref[...])


def _adam_sum(parts, w, m, v, name):
    P, R, C = parts.shape
    tr = _pick(R, (256, 128, 64, 32, 8)) if C <= 1024 else _pick(R, (128, 64, 32, 8))
    blk = pl.BlockSpec((tr, C), lambda i: (i, 0))
    return pl.pallas_call(
        functools.partial(_adam_step), name=name, grid=(R // tr,),
        in_specs=[pl.BlockSpec((P, tr, C), lambda i: (0, i, 0)), blk, blk, blk],
        out_specs=[blk] * 4, out_shape=[jax.ShapeDtypeStruct((R, C), F32)] * 4, compiler_params=_cp(("parallel",)))(parts, w, m, v)


def _adam_small(loss_parts, parts, ws, ms, vs):
    n = len(parts)

    def body(*refs):
        loss_ref, ins, outs, total_ref = refs[0], refs[1:4 * n + 1], refs[4 * n + 1:-1], refs[-1]
        for i in range(n):
            _adam_step(ins[i], ins[n + i], ins[2 * n + i], ins[3 * n + i], *outs[4 * i:4 * i + 4])
        total = loss_ref[0]
        for d in range(1, N_DEV):
            total = total + loss_ref[d]
        total_ref[...] = total

    out_shape = [jax.ShapeDtypeStruct(w.shape, F32) for w in ws for _ in range(4)] + [jax.ShapeDtypeStruct(loss_parts.shape[1:], F32)]
    res = pl.pallas_call(body, name="adam_small", out_shape=out_shape)(loss_parts, *parts, *ws, *ms, *vs)
    return res[-1], [tuple(res[4 * i:4 * i + 4]) for i in range(n)]


_WEIGHTS = ["attn_norm_w", "w_in", "conv_w", "conv_b", "dt_bias", "a_log", "d_skip", "ssd_norm_w", "cmp_w1_k", "cmp_w2_k",
            "cmp_w1_v", "cmp_w2_v", "cmp_pe_k", "cmp_pe_v", "w_out", "ffn_norm_w", "w_gate", "w_up", "w_down", "final_norm_w"]
_BIG = ["w_in", "w_gate", "w_up", "w_down", "w_out", "cmp_w1_k", "cmp_w1_v"]
_COL_SHARDED = ("w_in", "w_gate", "w_up")
_REPLICATED = ["attn_norm_w", "conv_b", "dt_bias", "a_log", "d_skip", "ssd_norm_w", "cmp_pe_k", "cmp_pe_v", "ffn_norm_w",
               "final_norm_w"]
_SMALL_SHARDED = ["conv_w", "cmp_w2_k", "cmp_w2_v"]


def _cols_to_slabs(g):
    R = g.shape[0]
    return g.reshape(R, N_DEV, -1).transpose(1, 0, 2)


def _slabs_to_cols(s):
    return s.transpose(1, 0, 2).reshape(s.shape[1], -1)


def kernel(x, attn_norm_w, w_in, conv_w, conv_b, dt_bias, a_log, d_skip, ssd_norm_w, cmp_w1_k, cmp_w2_k, cmp_w1_v, cmp_w2_v, cmp_pe_k, cmp_pe_v, w_out, ffn_norm_w, w_gate, w_up, w_down, final_norm_w, loss_target, m_attn_norm_w, m_w_in, m_conv_w, m_conv_b, m_dt_bias, m_a_log, m_d_skip, m_ssd_norm_w, m_cmp_w1_k, m_cmp_w2_k, m_cmp_w1_v, m_cmp_w2_v, m_cmp_pe_k, m_cmp_pe_v, m_w_out, m_ffn_norm_w, m_w_gate, m_w_up, m_w_down, m_final_norm_w, v_attn_norm_w, v_w_in, v_conv_w, v_conv_b, v_dt_bias, v_a_log, v_d_skip, v_ssd_norm_w, v_cmp_w1_k, v_cmp_w2_k, v_cmp_w1_v, v_cmp_w2_v, v_cmp_pe_k, v_cmp_pe_v, v_w_out, v_ffn_norm_w, v_w_gate, v_w_up, v_w_down, v_final_norm_w):
    a = dict(locals())

    shard = {n: a[n][0].astype(_MXU) for n in _BIG}
    got = _gather_two_level([shard[n] for n in _EARLY] + [cmp_w2_k[0], cmp_w2_v[0], conv_w[0]], "gather_early")
    st_late = _split_start([shard[n] for n in _LATE], False, "gather_late_start", after=(got[0],))

    def assemble(n, t):
        return _cols_from_slabs(t) if n in _COL_SHARDED else t.reshape(-1, t.shape[-1])

    p = dict(attn_norm_w=attn_norm_w, conv_b=conv_b, dt_bias=dt_bias, a_log=a_log, d_skip=d_skip, ssd_norm_w=ssd_norm_w,
             cmp_pe_k=cmp_pe_k.reshape(1, -1), cmp_pe_v=cmp_pe_v.reshape(1, -1), ffn_norm_w=ffn_norm_w,
             final_norm_w=final_norm_w.reshape(1, -1))

    w_main, w_small = _w_in_from_slabs(got[0])
    p.update(before_in_proj=st_late["token"],
             w_main=w_main, w_small=w_small, cmp_w1_k=assemble("cmp_w1_k", got[1]), cmp_w1_v=assemble("cmp_w1_v", got[2]),
             cmp_w2_k=assemble("cmp_w2_k", got[3]).astype(_MXU), cmp_w2_v=assemble("cmp_w2_v", got[4]).astype(_MXU),
             conv_w=_slabs_to_cols(got[5]))

    def late_weights(after):
        got_late = _split_wait(st_late, False, (after,), "gather_late_wait")
        return {n: assemble(n, t) for n, t in zip(_LATE, got_late)}

    def slabs_of(g, n):
        if n == "w_in":
            return _w_in_to_slabs(g["w_main"], g["w_small"])
        return _slabs_from_cols(g[n]) if n in _COL_SHARDED else g[n].reshape(N_DEV, -1, g[n].shape[-1])

    started = []

    def grads_ready(names, g):
        started.append((names, _split_start([slabs_of(g, n) for n in names], True, "scatter_grads_start_%d" % len(started))))
        return started[-1][1]["token"]

    loss_part, grad_x, g = _local_step(x[0], loss_target[0], p, late_weights, grads_ready)

    out, after = {}, (started[-1][1]["token"],)
    for i, (names, st) in enumerate(started):
        if i == len(started) - 1:
            after = after + (grad_x,)
        received = _split_wait(st, True, after, "scatter_grads_wait_%d" % i)
        for n, parts in zip(names, received):
            out[n] = _adam_sum(parts, a[n][0], a["m_" + n][0], a["v_" + n][0], "adam_" + n)
        after = (out[names[-1]][0],)

    small_names = _REPLICATED + _SMALL_SHARDED
    partials = [g[n] for n in _REPLICATED] + [_cols_to_slabs(g["conv_w"])] + [
        g[n].reshape(N_DEV, -1, g[n].shape[-1]) for n in ("cmp_w2_k", "cmp_w2_v")]
    gathered = _exchange([loss_part] + partials, [False] * (1 + len(_REPLICATED)) + [True] * len(_SMALL_SHARDED),
                         "exchange_small_grads", after=(received[0],))
    shapes2d = [t.shape[1:] for t in gathered[1:]]
    loss, res_small = _adam_small(gathered[0], gathered[1:],
                                  *[[a[pre + n].reshape(s) for n, s in zip(small_names, shapes2d)] for pre in ("", "m_", "v_")])
    for n, r in zip(small_names, res_small):
        out[n] = r

    outs = [loss[0, 0], grad_x[None]]
    for j in range(4):
        for n in _WEIGHTS:
            outs.append(out[n][j].reshape(a[n].shape))
    return tuple(outs)
```

```python
import functools

import numpy as np
import jax
import jax.numpy as jnp
from jax import lax
from jax.experimental import pallas as pl
from jax.experimental.pallas import tpu as pltpu

F32 = jnp.float32
_MXU = jnp.bfloat16

N_DEV = 8
SSD_WIDTH = 1024
ATT_WIDTH = 1024
SSD_HEADS = 16
SSD_P = 64
SSD_N = 128
SSD_L = 128
SSD_G = 2
CONV_CH = 1536
CONV_K = 4
HD = 64
N_HEADS = 16
N_KV = 4
GRP = 4
CMP_HID = 256
SEL_BLOCK = 64
N_SELECT = 16
WINDOW = 512
ROPE_DIM = 16
ROPE_THETA = 500000.0
EPS = 1e-6
NEG = -1e30
FORCE = 1e4
SCALE = HD ** -0.5
D_IN = 5184
W_MAIN = 5120
W_SMALL = 128
VMEM_LIMIT = 52 * 1024 * 1024

ADAM_LR, ADAM_B1, ADAM_B2, ADAM_EPS, ADAM_WD, ADAM_STEP = 0.001, 0.9, 0.999, 1e-08, 0.01, 10


def _pick(n, cands):
    for c in cands:
        if n % c == 0:
            return c
    return n


def _cp(sem=None):
    return pltpu.CompilerParams(dimension_semantics=sem, vmem_limit_bytes=VMEM_LIMIT)


def _sigmoid(x):
    return 1.0 / (1.0 + jnp.exp(-x))


def _dot(a, b, dims, split=None):
    dn = {"nn": (((1,), (0,)), ((), ())), "nt": (((1,), (1,)), ((), ())), "tn": (((0,), (0,)), ((), ()))}[dims]
    mm = lambda x, y: lax.dot_general(x.astype(_MXU), y.astype(_MXU), dn, preferred_element_type=F32)
    if split is None:
        return mm(a, b)
    x = (a if split == "a" else b).astype(F32)
    hi = x.astype(_MXU)
    lo = x - hi.astype(F32)
    return mm(hi, b) + mm(lo, b) if split == "a" else mm(a, hi) + mm(a, lo)


LANE = 128
MM_TILE = 1024
MM_K_WHOLE = 2048
MM_K_STEP = 2816
TN_ACC_ELEMS = 3 * 2 ** 20
TN_K_STEP = 512


def _largest_tile(n, cap):
    if n <= cap:
        return n
    best = LANE
    for t in range(LANE, cap + 1, LANE):
        if n % t == 0:
            best = t
    return best


def _mm_tiles(mode, M, N, K):
    if mode == "tn":
        tm = _largest_tile(M, 2 * MM_TILE)
        return tm, _largest_tile(N, TN_ACC_ELEMS // tm), _largest_tile(K, TN_K_STEP)
    tk = K if K <= MM_K_WHOLE else _largest_tile(K, MM_K_STEP)
    return _largest_tile(M, MM_TILE), _largest_tile(N, MM_TILE), tk


def _mm(a, b, mode, out_dtype, name, res=None, after=None):
    if mode == "nn":
        (M, K), N = a.shape, b.shape[1]
    elif mode == "nt":
        (M, K), N = a.shape, b.shape[0]
    else:
        (K, M), N = a.shape, b.shape[1]
    tm, tn, tk = _mm_tiles(mode, M, N, K)
    nk = K // tk
    a_spec = pl.BlockSpec((tk, tm), lambda i, j, k: (k, i)) if mode == "tn" else pl.BlockSpec((tm, tk), lambda i, j, k: (i, k))
    b_spec = pl.BlockSpec((tn, tk), lambda i, j, k: (j, k)) if mode == "nt" else pl.BlockSpec((tk, tn), lambda i, j, k: (k, j))
    o_spec = pl.BlockSpec((tm, tn), lambda i, j, k: (i, j))

    def finish(r, r_ref, o_ref):
        if res is not None:
            r = r + r_ref[...].astype(F32)
        o_ref[...] = r.astype(out_dtype)

    def body_one_step(*refs):
        a_ref, b_ref, o_ref = refs[0], refs[1], refs[-1]
        finish(_dot(a_ref[...], b_ref[...], mode), refs[2], o_ref)

    def body(*refs):
        a_ref, b_ref, o_ref, acc = refs[0], refs[1], refs[-2], refs[-1]
        k = pl.program_id(2)

        @pl.when(k == 0)
        def _():
            acc[...] = jnp.zeros_like(acc)

        acc[...] += _dot(a_ref[...], b_ref[...], mode)

        @pl.when(k == nk - 1)
        def _():
            finish(acc[...], refs[2], o_ref)

    ins, specs = [a, b], [a_spec, b_spec]
    if res is not None:
        ins.append(res)
        specs.append(o_spec)
    if after is not None:
        ins.append(after)
        specs.append(pl.BlockSpec(memory_space=pl.ANY))
    return pl.pallas_call(
        body_one_step if nk == 1 else body, name=name, grid=(M // tm, N // tn, nk), in_specs=specs, out_specs=o_spec,
        out_shape=jax.ShapeDtypeStruct((M, N), out_dtype), scratch_shapes=[] if nk == 1 else [pltpu.VMEM((tm, tn), F32)],
        compiler_params=_cp(("parallel", "parallel", "arbitrary")))(*ins)


def _ffn_up(v, w_gate, w_up):
    S, D = v.shape
    F = w_gate.shape[1]
    tm, tn = _largest_tile(S, MM_TILE), _largest_tile(F, MM_TILE // 2)

    def body(v_ref, wg_ref, wu_ref, gt_ref, up_ref, act_ref):
        vv = v_ref[...]
        g = _dot(vv, wg_ref[...], "nn")
        u = _dot(vv, wu_ref[...], "nn")
        gt_ref[...] = g.astype(gt_ref.dtype)
        up_ref[...] = u.astype(up_ref.dtype)
        act_ref[...] = (g * _sigmoid(g) * u).astype(act_ref.dtype)

    o_spec = pl.BlockSpec((tm, tn), lambda i, j: (i, j))
    w_spec = pl.BlockSpec((D, tn), lambda i, j: (0, j))
    return pl.pallas_call(
        body, name="ffn_up", grid=(S // tm, F // tn),
        in_specs=[pl.BlockSpec((tm, D), lambda i, j: (i, 0)), w_spec, w_spec], out_specs=[o_spec, o_spec, o_spec],
        out_shape=[jax.ShapeDtypeStruct((S, F), _MXU)] * 3,
        compiler_params=_cp(("parallel", "parallel")))(v, w_gate, w_up)


def _ffn_dv(dgt, dup, w_gate, w_up, after):
    S, F = dgt.shape
    D = w_gate.shape[0]
    tm, tn, _ = _mm_tiles("nt", S, D, F)
    tk = _largest_tile(F, MM_K_STEP // 2)
    nk = F // tk

    def body(g_ref, u_ref, wg_ref, wu_ref, *rest):
        o_ref, acc = rest[-2], rest[-1]
        k = pl.program_id(2)

        @pl.when(k == 0)
        def _():
            acc[...] = jnp.zeros_like(acc)

        acc[...] += _dot(g_ref[...], wg_ref[...], "nt") + _dot(u_ref[...], wu_ref[...], "nt")

        @pl.when(k == nk - 1)
        def _():
            o_ref[...] = acc[...]

    a_spec = pl.BlockSpec((tm, tk), lambda i, j, k: (i, k))
    w_spec = pl.BlockSpec((tn, tk), lambda i, j, k: (j, k))
    ins, specs = [dgt, dup, w_gate, w_up], [a_spec, a_spec, w_spec, w_spec]
    if after is not None:
        ins.append(after)
        specs.append(pl.BlockSpec(memory_space=pl.ANY))
    return pl.pallas_call(
        body, name="ffn_dv", grid=(S // tm, D // tn, nk), in_specs=specs, out_specs=pl.BlockSpec((tm, tn), lambda i, j, k: (i, j)),
        out_shape=jax.ShapeDtypeStruct((S, D), F32), scratch_shapes=[pltpu.VMEM((tm, tn), F32)],
        compiler_params=_cp(("parallel", "parallel", "arbitrary")))(*ins)


def _ffn_dact(dh2, w_down, gt, up):
    S, D = dh2.shape
    F = w_down.shape[0]
    tm, tn = _largest_tile(S, MM_TILE), _largest_tile(F, MM_TILE // 2)

    def body(d_ref, w_ref, gt_ref, up_ref, dg_ref, du_ref):
        da, g, u = _dot(d_ref[...], w_ref[...], "nt"), gt_ref[...].astype(F32), up_ref[...].astype(F32)
        s = _sigmoid(g)
        dg_ref[...] = (da * u * (s * (1.0 + g * (1.0 - s)))).astype(dg_ref.dtype)
        du_ref[...] = (da * (g * s)).astype(du_ref.dtype)

    o_spec = pl.BlockSpec((tm, tn), lambda i, j: (i, j))
    return pl.pallas_call(
        body, name="ffn_dact", grid=(S // tm, F // tn),
        in_specs=[pl.BlockSpec((tm, D), lambda i, j: (i, 0)), pl.BlockSpec((tn, D), lambda i, j: (j, 0)), o_spec, o_spec],
        out_specs=[o_spec, o_spec],
        out_shape=[jax.ShapeDtypeStruct((S, F), _MXU), jax.ShapeDtypeStruct((S, F), _MXU)],
        compiler_params=_cp(("parallel", "parallel")))(dh2, w_down, gt, up)


def _rms_fwd(x, w, name):
    S, D = x.shape
    tr = _pick(S, (256, 128))

    def body(x_ref, w_ref, xn_ref, rs_ref):
        xv = x_ref[...]
        rs = lax.rsqrt(jnp.mean(xv * xv, axis=-1, keepdims=True) + EPS)
        xn_ref[...] = ((xv * rs) * w_ref[...]).astype(xn_ref.dtype)
        rs_ref[...] = rs

    return pl.pallas_call(
        body, name=name, grid=(S // tr,),
        in_specs=[pl.BlockSpec((tr, D), lambda i: (i, 0)), pl.BlockSpec((1, D), lambda i: (0, 0))],
        out_specs=[pl.BlockSpec((tr, D), lambda i: (i, 0)), pl.BlockSpec((tr, 1), lambda i: (i, 0))],
        out_shape=[jax.ShapeDtypeStruct((S, D), _MXU), jax.ShapeDtypeStruct((S, 1), F32)],
        compiler_params=_cp(("parallel",)))(x, w)


def _rms_bwd(dyn, x, rs, w, res, name):
    S, D = x.shape
    tr = _pick(S, (256, 128))

    def body(dy_ref, x_ref, rs_ref, w_ref, res_ref, dx_ref, dxb_ref, dw_ref):
        @pl.when(pl.program_id(0) == 0)
        def _():
            dw_ref[...] = jnp.zeros_like(dw_ref)

        dy, r = dy_ref[...].astype(F32), rs_ref[...]
        xhat = x_ref[...] * r
        dw_ref[...] += jnp.sum(dy * xhat, axis=0, keepdims=True)
        dxhat = dy * w_ref[...]
        dx = res_ref[...] + r * (dxhat - xhat * jnp.mean(dxhat * xhat, axis=-1, keepdims=True))
        dx_ref[...] = dx
        dxb_ref[...] = dx.astype(dxb_ref.dtype)

    row = pl.BlockSpec((tr, D), lambda i: (i, 0))
    vec = pl.BlockSpec((1, D), lambda i: (0, 0))
    return pl.pallas_call(
        body, name=name, grid=(S // tr,),
        in_specs=[row, row, pl.BlockSpec((tr, 1), lambda i: (i, 0)), vec, row], out_specs=[row, row, vec],
        out_shape=[jax.ShapeDtypeStruct((S, D), F32), jax.ShapeDtypeStruct((S, D), _MXU), jax.ShapeDtypeStruct((1, D), F32)],
        compiler_params=_cp(("arbitrary",)))(dyn, x, rs, w, res)


def _final_loss(h2, w, tgt):
    S, D = h2.shape
    tr = _pick(S, (256, 128))

    def body(h_ref, w_ref, t_ref, loss_ref, dh_ref, dhb_ref, dw_ref):
        @pl.when(pl.program_id(0) == 0)
        def _():
            dw_ref[...] = jnp.zeros_like(dw_ref)
            loss_ref[...] = jnp.zeros_like(loss_ref)

        hv, wv = h_ref[...], w_ref[...]
        rs = lax.rsqrt(jnp.mean(hv * hv, axis=-1, keepdims=True) + EPS)
        xhat = hv * rs
        err = xhat * wv - t_ref[...]
        row = jnp.mean(err * err, axis=-1, keepdims=True)
        loss_ref[...] += jnp.broadcast_to(0.5 * jnp.sum(row, axis=0, keepdims=True), loss_ref.shape)
        dy = err * (1.0 / D)
        dw_ref[...] += jnp.sum(dy * xhat, axis=0, keepdims=True)
        dxhat = dy * wv
        dh = rs * (dxhat - xhat * jnp.mean(dxhat * xhat, axis=-1, keepdims=True))
        dh_ref[...] = dh
        dhb_ref[...] = dh.astype(dhb_ref.dtype)

    row = pl.BlockSpec((tr, D), lambda i: (i, 0))
    vec = pl.BlockSpec((1, D), lambda i: (0, 0))
    return pl.pallas_call(
        body, name="final_loss", grid=(S // tr,), in_specs=[row, vec, row],
        out_specs=[pl.BlockSpec((1, LANE), lambda i: (0, 0)), row, row, vec],
        out_shape=[jax.ShapeDtypeStruct((1, LANE), F32), jax.ShapeDtypeStruct((S, D), F32), jax.ShapeDtypeStruct((S, D), _MXU),
                   jax.ShapeDtypeStruct((1, D), F32)],
        compiler_params=_cp(("arbitrary",)))(h2, w, tgt)


def _shift_rows(x, k, rows):
    if k == 0:
        return x
    S = x.shape[0]
    r = pltpu.roll(x, k % S, axis=0)
    ok = (rows >= k) if k > 0 else (rows < S + k)
    return jnp.where(ok, r, 0.0)


XBC_COL0 = SSD_WIDTH // 128


def _conv_fwd(proj, conv_w, conv_b):
    S = proj.shape[0]
    nct = CONV_CH // 128

    def body(x_ref, w_ref, b_ref, o_ref):
        x = x_ref[...]
        rows = lax.broadcasted_iota(jnp.int32, x.shape, 0)
        c = b_ref[...] + w_ref[3:4, :] * x
        for k in range(1, CONV_K):
            c = c + w_ref[3 - k:4 - k, :] * _shift_rows(x, k, rows)
        o_ref[...] = c * _sigmoid(c)

    return pl.pallas_call(
        body, name="conv_fwd", grid=(nct,),
        in_specs=[pl.BlockSpec((S, 128), lambda j: (0, XBC_COL0 + j)), pl.BlockSpec((CONV_K, 128), lambda j: (0, j)),
                  pl.BlockSpec((1, 128), lambda j: (0, j))],
        out_specs=pl.BlockSpec((S, 128), lambda j: (0, j)),
        out_shape=jax.ShapeDtypeStruct((S, CONV_CH), F32), compiler_params=_cp(("parallel",)))(proj, conv_w, conv_b)


def _conv_bwd(proj, conv_w, conv_b, dxa):
    S = proj.shape[0]
    nct = CONV_CH // 128

    def body(x_ref, w_ref, b_ref, d_ref, dx_ref, dw_ref, db_ref):
        x = x_ref[...]
        rows = lax.broadcasted_iota(jnp.int32, x.shape, 0)
        xs = [_shift_rows(x, k, rows) for k in range(CONV_K)]
        c = b_ref[...] + w_ref[3:4, :] * x
        for k in range(1, CONV_K):
            c = c + w_ref[3 - k:4 - k, :] * xs[k]
        s = _sigmoid(c)
        dc = d_ref[...] * (s * (1.0 + c * (1.0 - s)))
        dx = w_ref[3:4, :] * dc
        for k in range(1, CONV_K):
            dx = dx + w_ref[3 - k:4 - k, :] * _shift_rows(dc, -k, rows)
        dx_ref[...] = dx.astype(dx_ref.dtype)
        for k in range(CONV_K):
            dw_ref[3 - k:4 - k, :] = jnp.sum(dc * xs[k], axis=0, keepdims=True)
        db_ref[...] = jnp.sum(dc, axis=0, keepdims=True)

    col = pl.BlockSpec((S, 128), lambda j: (0, j))
    return pl.pallas_call(
        body, name="conv_bwd", grid=(nct,),
        in_specs=[pl.BlockSpec((S, 128), lambda j: (0, XBC_COL0 + j)), pl.BlockSpec((CONV_K, 128), lambda j: (0, j)),
                  pl.BlockSpec((1, 128), lambda j: (0, j)), col],
        out_specs=[col, pl.BlockSpec((CONV_K, 128), lambda j: (0, j)), pl.BlockSpec((1, 128), lambda j: (0, j))],
        out_shape=[jax.ShapeDtypeStruct((S, CONV_CH), _MXU), jax.ShapeDtypeStruct((CONV_K, CONV_CH), F32),
                   jax.ShapeDtypeStruct((1, CONV_CH), F32)],
        compiler_params=_cp(("parallel",)))(proj, conv_w, conv_b, dxa)


def _ssd_consts():
    L = SSD_L
    r = lax.broadcasted_iota(jnp.int32, (L, L), 0)
    c = lax.broadcasted_iota(jnp.int32, (L, L), 1)
    causal = r >= c
    upper = (r <= c).astype(F32)
    hr = lax.broadcasted_iota(jnp.int32, (SSD_HEADS, SSD_WIDTH), 0)
    hc = lax.broadcasted_iota(jnp.int32, (SSD_HEADS, SSD_WIDTH), 1)
    expand = (lax.shift_right_logical(hc, 6) == hr).astype(F32)
    return causal, causal.astype(F32), upper, expand


def _softplus(x):
    return jnp.maximum(x, 0.0) + jnp.log(1.0 + jnp.exp(-jnp.abs(x)))


def _ssd_scalars(dtr, dt_bias, a_log, tri, upper, expand):
    dt = _softplus(dtr + dt_bias)
    A = -jnp.exp(a_log)
    adt = dt * A
    acum = _dot(tri, adt, "nn", split="b")
    acum_t = _dot(adt, upper, "tn", split="a")
    alast = acum[SSD_L - 1:SSD_L, :]
    e = jnp.exp(acum)
    wdec = jnp.exp(alast - acum)
    gam = jnp.exp(alast)
    ex = lambda t: _dot(t, expand, "nn", split="a")
    gam8 = jnp.broadcast_to(gam, (8, SSD_HEADS))
    return dt, A, acum, acum_t, e, wdec, gam, ex(dt), ex(e), ex(wdec), ex(gam8)[0:1, :]


def _ssd_fwd(proj, proj_small, xa, dt_bias, a_log, d_skip, norm_w):
    S = proj.shape[0]
    L, N, W = SSD_L, SSD_N, SSD_WIDTH
    nc = S // L

    def body(z_ref, xa_ref, dtr_ref, dtb_ref, al_ref, dsk_ref, nw_ref, yo_ref, y_ref, rs_ref, hs_ref, h_scr, y_scr):
        @pl.when(pl.program_id(0) == 0)
        def _():
            h_scr[...] = jnp.zeros_like(h_scr)

        causal, tri, upper, expand = _ssd_consts()
        dt, A, acum, acum_t, e, wdec, gam, dtE, eE, wE, gamE = _ssd_scalars(dtr_ref[:, 0:SSD_HEADS], dtb_ref[...], al_ref[...], tri, upper, expand)
        xs = xa_ref[:, 0:W]
        X = xs * dtE
        XW = X * wE
        hs_ref[0] = h_scr[...]
        for g in range(SSD_G):
            gs = slice(g * 512, (g + 1) * 512)
            Bg = xa_ref[:, W + g * N:W + (g + 1) * N]
            Cg = xa_ref[:, W + SSD_G * N + g * N:W + SSD_G * N + (g + 1) * N]
            Hg = h_scr[:, gs]
            CB = _dot(Cg, Bg, "nt")
            yoff = _dot(Cg, Hg, "nn") * eE[:, gs]
            st = _dot(Bg, XW[:, gs], "tn")
            for j in range(8):
                h = g * 8 + j
                hsl = slice(h * SSD_P, (h + 1) * SSD_P)
                lam = jnp.exp(jnp.where(causal, acum[:, h:h + 1] - acum_t[h:h + 1, :], -jnp.inf))
                y_scr[:, hsl] = _dot(CB * lam, X[:, hsl], "nn") + yoff[:, j * SSD_P:(j + 1) * SSD_P]
            h_scr[:, gs] = gamE[:, gs] * Hg + st
        dskE = _dot(jnp.broadcast_to(dsk_ref[...], (8, SSD_HEADS)), expand, "nn", split="a")[0:1, :]
        y = y_scr[...] + dskE * xs
        y_ref[...] = y
        zv = z_ref[...]
        yg = y * (zv * _sigmoid(zv))
        rs = lax.rsqrt(jnp.mean(yg * yg, axis=-1, keepdims=True) + EPS)
        rs_ref[...] = rs
        yo_ref[...] = ((yg * rs) * nw_ref[...]).astype(yo_ref.dtype)

    p16 = pl.BlockSpec((1, SSD_HEADS), lambda c: (0, 0))
    return pl.pallas_call(
        body, name="ssd_fwd", grid=(nc,),
        in_specs=[pl.BlockSpec((L, W), lambda c: (c, 0)), pl.BlockSpec((L, CONV_CH), lambda c: (c, 0)),
                  pl.BlockSpec((L, W_SMALL), lambda c: (c, 0)), p16, p16, p16, pl.BlockSpec((1, W), lambda c: (0, 0))],
        out_specs=[pl.BlockSpec((L, W), lambda c: (c, 0)), pl.BlockSpec((L, W), lambda c: (c, 0)),
                   pl.BlockSpec((L, 1), lambda c: (c, 0)), pl.BlockSpec((1, N, W), lambda c: (c, 0, 0))],
        out_shape=[jax.ShapeDtypeStruct((S, W), _MXU), jax.ShapeDtypeStruct((S, W), F32), jax.ShapeDtypeStruct((S, 1), F32),
                   jax.ShapeDtypeStruct((nc, N, W), F32)],
        scratch_shapes=[pltpu.VMEM((N, W), F32), pltpu.VMEM((L, W), F32)],
        compiler_params=_cp(("arbitrary",)))(proj, xa, proj_small, dt_bias, a_log, d_skip, norm_w)


def _ssd_bwd(dmixed, proj, proj_small, xa, y, rs2, hs, dt_bias, a_log, d_skip, norm_w):
    S = proj.shape[0]
    L, N, W, H = SSD_L, SSD_N, SSD_WIDTH, SSD_HEADS
    nc = S // L

    def body(dyo_ref, z_ref, xa_ref, dtr_ref, y_ref, rs_ref, hs_ref, dtb_ref, al_ref, dsk_ref, nw_ref,
             dz_ref, dxa_ref, ddtr_ref, ddtb_ref, dal_ref, ddsk_ref, dnw_ref, dh_scr, dx_scr):
        @pl.when(pl.program_id(0) == 0)
        def _():
            dh_scr[...] = jnp.zeros_like(dh_scr)
            ddtb_ref[...] = jnp.zeros_like(ddtb_ref)
            dal_ref[...] = jnp.zeros_like(dal_ref)
            ddsk_ref[...] = jnp.zeros_like(ddsk_ref)
            dnw_ref[...] = jnp.zeros_like(dnw_ref)

        causal, tri, upper, expand = _ssd_consts()
        heads = lambda t: _dot(t, expand, "nt", split="a")
        onehot = lambda h: (lax.broadcasted_iota(jnp.int32, (1, H), 1) == h).astype(F32)

        zv, yv, rs = z_ref[...], y_ref[...], rs_ref[...]
        sz = _sigmoid(zv)
        zs = zv * sz
        xhat = (yv * zs) * rs
        dyo = dyo_ref[...].astype(F32)
        dnw_ref[...] += jnp.sum(dyo * xhat, axis=0, keepdims=True)
        dxhat = dyo * nw_ref[...]
        dyg = rs * (dxhat - xhat * jnp.mean(dxhat * xhat, axis=-1, keepdims=True))
        dz_ref[...] = (dyg * yv * (sz * (1.0 + zv * (1.0 - sz)))).astype(dz_ref.dtype)
        dy = dyg * zs

        dtr = dtr_ref[:, 0:H]
        dt, A, acum, acum_t, e, wdec, gam, dtE, eE, wE, gamE = _ssd_scalars(dtr, dtb_ref[...], al_ref[...], tri, upper, expand)
        xs = xa_ref[:, 0:W]
        X = xs * dtE
        XW = X * wE
        dskE = _dot(jnp.broadcast_to(dsk_ref[...], (8, H)), expand, "nn", split="a")[0:1, :]
        ddsk_ref[...] += heads(jnp.broadcast_to(jnp.sum(dy * xs, axis=0, keepdims=True), (8, W)))[0:1, :]

        dYe = dy * eE
        dacum = jnp.zeros((L, H), F32)
        de_full = []
        dw_full = []
        dgam_full = []
        for g in range(SSD_G):
            gs = slice(g * 512, (g + 1) * 512)
            Bg = xa_ref[:, W + g * N:W + (g + 1) * N]
            Cg = xa_ref[:, W + SSD_G * N + g * N:W + SSD_G * N + (g + 1) * N]
            Hg = hs_ref[0, :, gs]
            dHn = dh_scr[:, gs]
            CH = _dot(Cg, Hg, "nn")
            de_full.append(dy[:, gs] * CH)
            dC = _dot(dYe[:, gs], Hg, "nt")
            dHs = gamE[:, gs] * dHn + _dot(Cg, dYe[:, gs], "tn")
            dgam_full.append(jnp.sum(dHn * Hg, axis=0, keepdims=True))
            BdS = _dot(Bg, dHn, "nn")
            dB = _dot(XW[:, gs], dHn, "nt")
            dx_scr[:, gs] = BdS * wE[:, gs]
            dw_full.append(BdS * X[:, gs])
            CB = _dot(Cg, Bg, "nt")
            dCB = jnp.zeros((L, L), F32)
            for j in range(8):
                h = g * 8 + j
                hsl = slice(h * SSD_P, (h + 1) * SSD_P)
                lam = jnp.exp(jnp.where(causal, acum[:, h:h + 1] - acum_t[h:h + 1, :], -jnp.inf))
                M = CB * lam
                dM = _dot(dy[:, hsl], X[:, hsl], "nt")
                dx_scr[:, hsl] += _dot(M, dy[:, hsl], "tn")
                dCB = dCB + dM * lam
                Q = dM * M
                rowsum = jnp.sum(Q, axis=1, keepdims=True)
                colsum = _dot(Q, jnp.ones((L, 8), F32), "tn", split="a")[:, 0:1]
                dacum = dacum + (rowsum - colsum) * onehot(h)
            dC = dC + _dot(dCB, Bg, "nn")
            dB = dB + _dot(dCB, Cg, "tn")
            dxa_ref[:, W + g * N:W + (g + 1) * N] = dB
            dxa_ref[:, W + SSD_G * N + g * N:W + SSD_G * N + (g + 1) * N] = dC
            dh_scr[:, gs] = dHs

        de16 = heads(jnp.concatenate(de_full, axis=1))
        dw16 = heads(jnp.concatenate(dw_full, axis=1))
        dgam16 = heads(jnp.broadcast_to(jnp.concatenate(dgam_full, axis=1), (8, W)))[0:1, :]
        dacum = dacum + de16 * e - dw16 * wdec
        dlast = jnp.sum(dw16 * wdec, axis=0, keepdims=True) + dgam16 * gam
        lastrow = (lax.broadcasted_iota(jnp.int32, (L, 1), 0) == L - 1).astype(F32)
        dacum = dacum + lastrow * dlast
        da = _dot(tri, dacum, "tn", split="b")
        dX = dx_scr[...]
        ddt = da * A + heads(dX * xs)
        dA = jnp.sum(da * dt, axis=0, keepdims=True)
        dal_ref[...] += dA * A
        ddtr = ddt * _sigmoid(dtr + dtb_ref[...])
        ddtb_ref[...] += jnp.sum(ddtr, axis=0, keepdims=True)
        ddtr_ref[...] = ddtr
        dxa_ref[:, 0:W] = dX * dtE + dy * dskE

    p16 = pl.BlockSpec((1, H), lambda c: (0, 0))
    rev = lambda c: (nc - 1 - c, 0)
    return pl.pallas_call(
        body, name="ssd_bwd", grid=(nc,),
        in_specs=[pl.BlockSpec((L, W), rev), pl.BlockSpec((L, W), rev), pl.BlockSpec((L, CONV_CH), rev),
                  pl.BlockSpec((L, W_SMALL), rev), pl.BlockSpec((L, W), rev), pl.BlockSpec((L, 1), rev),
                  pl.BlockSpec((1, N, W), lambda c: (nc - 1 - c, 0, 0)), p16, p16, p16, pl.BlockSpec((1, W), lambda c: (0, 0))],
        out_specs=[pl.BlockSpec((L, W), rev), pl.BlockSpec((L, CONV_CH), rev), pl.BlockSpec((L, H), rev),
                   p16, p16, p16, pl.BlockSpec((1, W), lambda c: (0, 0))],
        out_shape=[jax.ShapeDtypeStruct((S, W), _MXU), jax.ShapeDtypeStruct((S, CONV_CH), F32), jax.ShapeDtypeStruct((S, H), F32),
                   jax.ShapeDtypeStruct((1, H), F32), jax.ShapeDtypeStruct((1, H), F32), jax.ShapeDtypeStruct((1, H), F32),
                   jax.ShapeDtypeStruct((1, W), F32)],
        scratch_shapes=[pltpu.VMEM((N, W), F32), pltpu.VMEM((L, W), F32)],
        compiler_params=_cp(("arbitrary",)))(dmixed, proj, xa, proj_small, y, rs2, hs, dt_bias, a_log, d_skip, norm_w)


def _rope_tables(S):
    inv = 1.0 / (ROPE_THETA ** (jnp.arange(0, ROPE_DIM, 2, dtype=F32) / ROPE_DIM))
    ang = jnp.arange(S, dtype=F32)[:, None] * inv[None, :]
    cos, sin = jnp.cos(ang), jnp.sin(ang)
    half = ROPE_DIM // 2
    c64 = jnp.concatenate([cos, cos, jnp.ones((S, HD - ROPE_DIM), F32)], axis=1)
    s64 = jnp.concatenate([sin, sin, jnp.zeros((S, HD - ROPE_DIM), F32)], axis=1)
    del half
    return jnp.concatenate([c64, c64], axis=1), jnp.concatenate([s64, s64], axis=1)


def _rope(xs, blk0, width, cos, sin, sign, out_dtype, name, extra=None):
    S = xs[0].shape[0]
    tr = _pick(S, (512, 256, 128))
    nx = len(xs)

    def body(*refs):
        x_refs, c_ref, s_ref = refs[:nx], refs[nx], refs[nx + 1]
        e_ref = refs[nx + 2] if extra is not None else None
        o_ref = refs[-1]
        cv, sv = c_ref[...], s_ref[...] * sign
        lane = lax.broadcasted_iota(jnp.int32, (tr, 128), 1)
        first = (lane & (HD - 1)) < (ROPE_DIM // 2)
        for j in range(bw // 128):
            cs = slice(j * 128, (j + 1) * 128)
            xv = x_refs[0][:, cs].astype(F32)
            for r in x_refs[1:]:
                xv = xv + r[:, cs].astype(F32)
            out = _rotate128(xv, cv, sv, first)
            if extra is not None:
                out = out + e_ref[:, cs].astype(F32)
            o_ref[:, cs] = out.astype(out_dtype)

    bw = 512
    assert width % bw == 0 and (blk0 * 256) % bw == 0
    b0 = blk0 * 256 // bw
    t128 = pl.BlockSpec((tr, 128), lambda i, j: (i, 0))
    oblk = pl.BlockSpec((tr, bw), lambda i, j: (i, j))
    specs = [pl.BlockSpec((tr, bw), lambda i, j: (i, b0 + j))] * nx + [t128, t128]
    ins = list(xs) + [cos, sin]
    if extra is not None:
        assert (extra[1] * 256) % bw == 0
        ins.append(extra[0])
        eb = extra[1] * 256 // bw
        specs.append(pl.BlockSpec((tr, bw), lambda i, j: (i, eb + j)))
    return pl.pallas_call(
        body, name=name, grid=(S // tr, width // bw), in_specs=specs, out_specs=oblk,
        out_shape=jax.ShapeDtypeStruct((S, width), out_dtype), compiler_params=_cp(("parallel", "parallel")))(*ins)


def _rotate128(xv, cv, sv, first):
    rot = jnp.where(first, -pltpu.roll(xv, 128 - ROPE_DIM // 2, axis=1), pltpu.roll(xv, ROPE_DIM // 2, axis=1))
    return xv * cv + rot * sv


def _kv_prep(proj, cos, sin, tk):
    S = proj.shape[0]
    NB = S // SEL_BLOCK

    def body(ks_ref, vs_ref, kw_ref, vw_ref, c_ref, s_ref, *outs):
        cv, sv = c_ref[...], s_ref[...]
        lane = lax.broadcasted_iota(jnp.int32, (tk, 128), 1)
        first = (lane & (HD - 1)) < (ROPE_DIM // 2)
        key = pl.program_id(0) * tk + lax.broadcasted_iota(jnp.int32, (tk, NB), 0)
        onehot = (lax.shift_right_logical(key, 6) == lax.broadcasted_iota(jnp.int32, (tk, NB), 1)).astype(F32)
        for j, (ref, rotated) in enumerate(((ks_ref, True), (vs_ref, False), (kw_ref, True), (vw_ref, False))):
            nat, blk = outs[2 * j], outs[2 * j + 1]
            for half in range(2):
                xv = ref[:, half * 128:(half + 1) * 128]
                if rotated:
                    xv = _rotate128(xv, cv, sv, first)
                for e in range(2):
                    h = 2 * half + e
                    piece = xv[:, e * HD:(e + 1) * HD]
                    nat[h] = (jnp.concatenate([piece, onehot], axis=1) if j == 0 else piece).astype(nat.dtype)
                    blk[h, 0] = piece.T.astype(blk.dtype)

    col = lambda b: pl.BlockSpec((tk, 256), lambda i: (i, b))
    t128 = pl.BlockSpec((tk, 128), lambda i: (i, 0))
    nat_spec = lambda w: pl.BlockSpec((N_KV, tk, w), lambda i: (0, i, 0))
    blk_spec = pl.BlockSpec((N_KV, 1, HD, tk), lambda i: (0, i, 0, 0))
    nat_shape = lambda w: jax.ShapeDtypeStruct((N_KV, S, w), _MXU)
    blk_shape = jax.ShapeDtypeStruct((N_KV, S // tk, HD, tk), _MXU)
    widths = (HD + NB, HD, HD, HD)
    res = pl.pallas_call(
        body, name="kv_prep", grid=(S // tk,), in_specs=[col(KSB), col(VSB), col(KWB), col(VWB), t128, t128],
        out_specs=[s for w in widths for s in (nat_spec(w), blk_spec)],
        out_shape=[s for w in widths for s in (nat_shape(w), blk_shape)],
        compiler_params=_cp(("parallel",)))(proj, proj, proj, proj, cos, sin)
    return dict(ks_ext=res[0], ks_t=res[1], vs=res[2], vs_t=res[3], kw=res[4], kw_t=res[5], vw=res[6], vw_t=res[7])


def _dkv_post(dks, dvs, dkw, dvw, cos, sin):
    S = dks.shape[1]
    tr = _pick(S, (512, 256, 128))

    def body(dks_ref, dvs_ref, dkw_ref, dvw_ref, c_ref, s_ref, o_ref):
        cv, sv = c_ref[...], -s_ref[...]
        lane = lax.broadcasted_iota(jnp.int32, (tr, 128), 1)
        first = (lane & (HD - 1)) < (ROPE_DIM // 2)
        for j, (ref, rotated) in enumerate(((dks_ref, True), (dvs_ref, False), (dkw_ref, True), (dvw_ref, False))):
            for half in range(2):
                xv = jnp.concatenate([ref[2 * half], ref[2 * half + 1]], axis=1)
                if rotated:
                    xv = _rotate128(xv, cv, sv, first)
                o_ref[:, j * 256 + half * 128:j * 256 + (half + 1) * 128] = xv.astype(o_ref.dtype)

    hm = pl.BlockSpec((N_KV, tr, HD), lambda i: (0, i, 0))
    t128 = pl.BlockSpec((tr, 128), lambda i: (i, 0))
    return pl.pallas_call(
        body, name="dkv_post", grid=(S // tr,), in_specs=[hm, hm, hm, hm, t128, t128],
        out_specs=pl.BlockSpec((tr, 4 * 256), lambda i: (i, 0)), out_shape=jax.ShapeDtypeStruct((S, 4 * 256), _MXU),
        compiler_params=_cp(("parallel",)))(dks, dvs, dkw, dvw, cos, sin)


def _compress_fwd(R, pe, w1, w2):
    NC = R.shape[1]
    half = 16 * HD

    def body(r_ref, pe_ref, w1_ref, w2_ref, o_ref, hid_ref):
        r = r_ref[0]
        a = _dot(r + pe_ref[:, 0:half], w1_ref[0:half, :], "nn")
        b = _dot(r + pe_ref[:, half:2 * half], w1_ref[half:2 * half, :], "nn")
        hid = a + pltpu.roll(b, NC - 1, axis=0)
        hid_ref[0] = hid
        out = _dot(hid * _sigmoid(hid), w2_ref[...], "nn")
        rows = lax.broadcasted_iota(jnp.int32, out.shape, 0)
        o_ref[0] = jnp.where(rows < NC - 1, out, 0.0).astype(o_ref.dtype)

    return pl.pallas_call(
        body, name="compress_fwd", grid=(N_KV,),
        in_specs=[pl.BlockSpec((1, NC, half), lambda h: (h, 0, 0)), pl.BlockSpec((1, 2 * half), lambda h: (0, 0)),
                  pl.BlockSpec((2 * half, CMP_HID), lambda h: (0, 0)), pl.BlockSpec((CMP_HID, HD), lambda h: (0, 0))],
        out_specs=[pl.BlockSpec((1, NC, HD), lambda h: (h, 0, 0)), pl.BlockSpec((1, NC, CMP_HID), lambda h: (h, 0, 0))],
        out_shape=[jax.ShapeDtypeStruct((N_KV, NC, HD), _MXU), jax.ShapeDtypeStruct((N_KV, NC, CMP_HID), F32)],
        compiler_params=_cp(("parallel",)))(R, pe, w1, w2)


def _compress_bwd(R, pe, w1, w2, hid, dout):
    NC = R.shape[1]
    half = 16 * HD

    def body(r_ref, pe_ref, w1_ref, w2_ref, hid_ref, do_ref, dr_ref, dw1_ref, dw2_ref, dpe_ref):
        @pl.when(pl.program_id(0) == 0)
        def _():
            dw1_ref[...] = jnp.zeros_like(dw1_ref)
            dw2_ref[...] = jnp.zeros_like(dw2_ref)
            dpe_ref[...] = jnp.zeros_like(dpe_ref)

        r, hv, do = r_ref[0], hid_ref[0], do_ref[0]
        s = _sigmoid(hv)
        dw2_ref[...] += _dot(hv * s, do, "tn")
        dhid = _dot(do, w2_ref[...], "nt") * (s * (1.0 + hv * (1.0 - s)))
        rows = lax.broadcasted_iota(jnp.int32, dhid.shape, 0)
        dhid = jnp.where(rows < NC - 1, dhid, 0.0)
        dhid_dn = pltpu.roll(dhid, 1, axis=0)
        dw1_ref[0:half, :] += _dot(r + pe_ref[:, 0:half], dhid, "tn")
        dw1_ref[half:2 * half, :] += _dot(r + pe_ref[:, half:2 * half], dhid_dn, "tn")
        dxt = _dot(dhid, w1_ref[0:half, :], "nt")
        dxb = _dot(dhid_dn, w1_ref[half:2 * half, :], "nt")
        dr_ref[0] = dxt + dxb
        dpe_ref[:, 0:half] += jnp.sum(dxt, axis=0, keepdims=True)
        dpe_ref[:, half:2 * half] += jnp.sum(dxb, axis=0, keepdims=True)

    return pl.pallas_call(
        body, name="compress_bwd", grid=(N_KV,),
        in_specs=[pl.BlockSpec((1, NC, half), lambda h: (h, 0, 0)), pl.BlockSpec((1, 2 * half), lambda h: (0, 0)),
                  pl.BlockSpec((2 * half, CMP_HID), lambda h: (0, 0)), pl.BlockSpec((CMP_HID, HD), lambda h: (0, 0)),
                  pl.BlockSpec((1, NC, CMP_HID), lambda h: (h, 0, 0)), pl.BlockSpec((1, NC, HD), lambda h: (h, 0, 0))],
        out_specs=[pl.BlockSpec((1, NC, half), lambda h: (h, 0, 0)), pl.BlockSpec((2 * half, CMP_HID), lambda h: (0, 0)),
                   pl.BlockSpec((CMP_HID, HD), lambda h: (0, 0)), pl.BlockSpec((1, 2 * half), lambda h: (0, 0))],
        out_shape=[jax.ShapeDtypeStruct((N_KV, NC, half), F32), jax.ShapeDtypeStruct((2 * half, CMP_HID), F32),
                   jax.ShapeDtypeStruct((CMP_HID, HD), F32), jax.ShapeDtypeStruct((1, 2 * half), F32)],
        compiler_params=_cp(("arbitrary",)))(R, pe, w1, w2, hid, dout)


def _attn_cfg(S, Sk, mode):
    tk = _pick(Sk, (256, 128))
    if mode == "cmp":
        return _pick(S, (512, 256, 128)), Sk
    if mode == "sel" and S % (2 * tk) == 0:
        return 2 * tk, tk
    return tk, tk


def _block_start(kb, tk):
    return kb * tk if isinstance(kb, int) else pl.multiple_of(kb * tk, tk)


def _pipelined_key_blocks(mode, q0, tq, tk, produce, consume):
    if mode == "cmp":
        produce(0, True, 0)
        consume(0, 0)
        return
    if mode == "win":
        assert tq == tk and WINDOW == 2 * tk
        last = q0 // tk
        first = jnp.maximum(last - 2, 0)

        @pl.when(last == 0)
        def _():
            produce(last, True, 0)
            consume(last, 0)

        @pl.when(last == 1)
        def _():
            produce(first, True, 0)
            produce(last, True, 1)
            consume(first, 0)
            consume(last, 1)

        @pl.when(last >= 2)
        def _():
            produce(first, True, 0)
            produce(first + 1, False, 1)
            consume(first, 0)
            produce(last, True, 0)
            consume(first + 1, 1)
            consume(last, 0)

        return
    first, n_plain, plain_masked = 0, q0 // tk, False
    pairs = jnp.maximum(n_plain - 1, 0) // 2
    if tq == 2 * tk:
        @pl.when(n_plain >= 1)
        def _():
            produce(0, False, 0)

        def two_plain(j, carry):
            produce(2 * j + 1, False, 1)
            consume(2 * j, 0)
            produce(2 * j + 2, False, 0)
            consume(2 * j + 1, 1)
            return carry

        lax.fori_loop(0, pairs, two_plain, 0)
        kb = 2 * pairs

        @pl.when(n_plain >= 2)
        def _():
            produce(kb + 1, False, 1)
            consume(kb, 0)
            produce(n_plain, True, 0)
            consume(kb + 1, 1)
            produce(n_plain + 1, True, 1)
            consume(n_plain, 0)
            consume(n_plain + 1, 1)

        @pl.when(n_plain == 0)
        def _():
            produce(0, True, 0)
            produce(1, True, 1)
            consume(0, 0)
            consume(1, 1)

        return
    assert tq == tk
    last = first + n_plain

    @pl.when(n_plain >= 1)
    def _():
        produce(first, plain_masked, 0)

    def two(j, carry):
        kb = first + 2 * j
        produce(kb + 1, plain_masked, 1)
        consume(kb, 0)
        produce(kb + 2, plain_masked, 0)
        consume(kb + 1, 1)
        return carry

    lax.fori_loop(0, pairs, two, 0)
    kb = first + 2 * pairs
    left = n_plain - 2 * pairs

    @pl.when(left == 2)
    def _():
        produce(kb + 1, plain_masked, 1)
        consume(kb, 0)
        produce(last, True, 0)
        consume(kb + 1, 1)
        consume(last, 0)

    @pl.when(left == 1)
    def _():
        produce(last, True, 1)
        consume(kb, 0)
        consume(last, 1)

    @pl.when(left == 0)
    def _():
        produce(last, True, 0)
        consume(last, 0)


def _attn_bias(mode, q0, k0, tq, tk):
    k = k0 + lax.broadcasted_iota(jnp.int32, (tk, tq), 0)
    t = q0 + lax.broadcasted_iota(jnp.int32, (tk, tq), 1)
    if mode == "cmp":
        ok = (k * 16 + 31) <= t
    elif mode == "win":
        ok = (k <= t) & ((t - k) < WINDOW)
    else:
        ok = k <= t
    bias = jnp.where(ok, 0.0, NEG)
    return jnp.concatenate([bias] * GRP, axis=1), jnp.concatenate([ok.astype(F32)] * GRP, axis=1)


def _sel_operands(qs, selneg_ref):
    return jnp.concatenate([qs, jnp.concatenate([selneg_ref[0]] * GRP, axis=0)], axis=1)


def _stack_heads(ref, tq):
    return jnp.concatenate([ref[:, g * HD:(g + 1) * HD] for g in range(GRP)], axis=0)


def _scaled_queries(q_ref, tq):
    return (_stack_heads(q_ref, tq).astype(F32) * SCALE).astype(_MXU)


def _blocked_t(x, tk):
    n, Sk, d = x.shape
    return x.reshape(n, Sk // tk, tk, d).transpose(0, 1, 3, 2)


def _head_rows(ref):
    return jnp.concatenate([ref[0, g:g + 1, :] for g in range(GRP)], axis=1)


def _attn_fwd(q, qcol0, k, vt, mode, selneg, gate, y_prev, y_dtype, name):
    S, Sk = q.shape[0], k.shape[1]
    tq, tk = _attn_cfg(S, Sk, mode)
    R = GRP * tq
    NB = S // SEL_BLOCK

    def body(*refs):
        q_ref, k_ref, vt_ref = refs[:3]
        rest = list(refs[3:])
        sel_ref = rest.pop(0) if mode == "sel" else None
        ov_ref = rest.pop(0) if mode == "cmp" else None
        gate_ref = rest.pop(0)
        yp_ref = rest.pop(0) if y_prev is not None else None
        o_ref, lse_ref, y_ref = rest[:3]
        choice_ref = rest[3] if mode == "cmp" else None
        m_scr, l_scr, acc, s_scr = rest[-4:]
        q0 = pl.program_id(1) * tq
        qs = _scaled_queries(q_ref, tq)
        m_scr[...] = jnp.full_like(m_scr, NEG)
        l_scr[...] = jnp.zeros_like(l_scr)
        acc[...] = jnp.zeros_like(acc)
        qk = _sel_operands(qs, sel_ref) if mode == "sel" else qs

        def produce(kb, masked, slot):
            k0 = _block_start(kb, tk)
            s = _dot(k_ref[0, pl.ds(k0, tk), :], qk, "nt")
            if masked:
                s = s + _attn_bias(mode, q0, k0, tq, tk)[0]
            s_scr[slot] = s

        def consume(kb, slot):
            s = s_scr[slot]
            m_old = m_scr[...]
            m_new = jnp.maximum(m_old, jnp.max(s, axis=0, keepdims=True))
            p = jnp.exp(s - m_new)
            if mode == "cmp":
                p = p * _attn_bias(mode, q0, 0, tq, tk)[1]
            alpha = jnp.exp(m_old - m_new)
            l_scr[...] = alpha * l_scr[...] + jnp.sum(p, axis=0, keepdims=True)
            acc[...] = alpha * acc[...] + _dot(vt_ref[0, kb], p, "nn")
            m_scr[...] = m_new

        _pipelined_key_blocks(mode, q0, tq, tk, produce, consume)
        l = l_scr[...]
        good = l > 0.0
        o_t = acc[...] * jnp.where(good, 1.0 / jnp.where(good, l, 1.0), 0.0)
        lse = jnp.where(good, m_scr[...] + jnp.log(jnp.where(good, l, 1.0)), -NEG)
        y_t = o_t * _sigmoid(_head_rows(gate_ref))
        if mode == "cmp":
            p = jnp.exp(s_scr[0] - lse) * _attn_bias(mode, q0, 0, tq, tk)[1]
            choice_ref[0] = _chosen_blocks(p, ov_ref[...], q0, tq).astype(choice_ref.dtype)
        for g in range(GRP):
            hs, qs_ = slice(g * HD, (g + 1) * HD), slice(g * tq, (g + 1) * tq)
            o_ref[:, hs] = o_t[:, qs_].T
            lse_ref[0, g:g + 1, :] = lse[:, qs_]
            yg = y_t[:, qs_].T
            if y_prev is not None:
                yg = yg + yp_ref[:, hs]
            y_ref[:, hs] = yg.astype(y_ref.dtype)

    row_spec = pl.BlockSpec((1, GRP, tq), lambda h, i: (h, 0, i))
    qo_spec = pl.BlockSpec((tq, GRP * HD), lambda h, i: (i, h))
    ins = [q, k, vt]
    specs = [pl.BlockSpec((tq, GRP * HD), lambda h, i: (i, qcol0 + h)), pl.BlockSpec((1, Sk, k.shape[2]), lambda h, i: (h, 0, 0)),
             pl.BlockSpec((1, Sk // tk, HD, tk), lambda h, i: (h, 0, 0, 0))]
    if mode == "sel":
        ins.append(selneg)
        specs.append(pl.BlockSpec((1, tq, selneg.shape[2]), lambda h, i: (h, i, 0)))
    out_specs = [qo_spec, row_spec, qo_spec]
    out_shape = [jax.ShapeDtypeStruct((S, ATT_WIDTH), F32), jax.ShapeDtypeStruct((N_KV, GRP, S), F32),
                 jax.ShapeDtypeStruct((S, ATT_WIDTH), y_dtype)]
    if mode == "cmp":
        ins.append(_block_overlap(Sk, NB))
        specs.append(pl.BlockSpec((NB, Sk), lambda h, i: (0, 0)))
        out_specs.append(pl.BlockSpec((1, tq, NB), lambda h, i: (h, i, 0)))
        out_shape.append(jax.ShapeDtypeStruct((N_KV, S, NB), _MXU))
    ins.append(gate)
    specs.append(row_spec)
    if y_prev is not None:
        ins.append(y_prev)
        specs.append(qo_spec)
    return pl.pallas_call(
        body, name=name, grid=(N_KV, S // tq), in_specs=specs, out_specs=out_specs, out_shape=out_shape,
        scratch_shapes=[pltpu.VMEM((1, R), F32), pltpu.VMEM((1, R), F32), pltpu.VMEM((HD, R), F32), pltpu.VMEM((2, tk, R), F32)],
        compiler_params=_cp(("parallel", "arbitrary")))(*ins)


def _attn_bwd(q, qcol0, k, kt, v, o, lse, dy, dycol0, gate, mode, selneg, name):
    S, Sk = q.shape[0], k.shape[1]
    tq, tk = _attn_cfg(S, Sk, mode)
    R = GRP * tq

    def body(*refs):
        if mode == "sel":
            (q_ref, k_ref, kt_ref, v_ref, o_ref, lse_ref, dy_ref, gate_ref, sel_ref, dq_ref, dk_ref, dv_ref, dg_ref, dq_scr, s_scr,
             dp_scr) = refs
        else:
            q_ref, k_ref, kt_ref, v_ref, o_ref, lse_ref, dy_ref, gate_ref, dq_ref, dk_ref, dv_ref, dg_ref, dq_scr, s_scr, dp_scr = refs

        @pl.when(pl.program_id(1) == 0)
        def _():
            dk_ref[...] = jnp.zeros_like(dk_ref)
            dv_ref[...] = jnp.zeros_like(dv_ref)

        q0 = pl.program_id(1) * tq
        qs = _scaled_queries(q_ref, tq)
        dys = _stack_heads(dy_ref, tq)
        gv = _sigmoid(_head_rows(gate_ref))
        dy_o = _dot(jnp.ones((8, HD), F32), dys * _stack_heads(o_ref, tq), "nt", split="b")[0:1, :]
        delta = gv * dy_o
        dgate = dy_o * (gv * (1.0 - gv))
        for g in range(GRP):
            dg_ref[0, g:g + 1, :] = dgate[:, g * tq:(g + 1) * tq]
        lsev = _head_rows(lse_ref)
        dos = (dys * jnp.broadcast_to(gv, (8, R)).T[:, 0:1]).astype(_MXU)
        dq_scr[...] = jnp.zeros_like(dq_scr)
        qk = _sel_operands(qs, sel_ref) if mode == "sel" else qs

        def produce(kb, masked, slot):
            k0 = _block_start(kb, tk)
            s = _dot(k_ref[0, pl.ds(k0, tk), :], qk, "nt")
            if masked:
                s = s + _attn_bias(mode, q0, k0, tq, tk)[0]
            s_scr[slot] = s
            dp_scr[slot] = _dot(v_ref[0, pl.ds(k0, tk), :], dos, "nt")

        def consume(kb, slot):
            k0 = _block_start(kb, tk)
            p = jnp.exp(s_scr[slot] - lsev)
            if mode == "cmp":
                p = p * _attn_bias(mode, q0, 0, tq, tk)[1]
            ds = p * (dp_scr[slot] - delta)
            dq_scr[...] += _dot(kt_ref[0, kb], ds, "nn")
            dk_ref[0, pl.ds(k0, tk), :] += _dot(ds, qs, "nn")
            dv_ref[0, pl.ds(k0, tk), :] += _dot(p, dos, "nn")

        _pipelined_key_blocks(mode, q0, tq, tk, produce, consume)
        for g in range(GRP):
            dq_ref[:, g * HD:(g + 1) * HD] = (dq_scr[:, g * tq:(g + 1) * tq] * SCALE).T

    kv_spec = pl.BlockSpec((1, Sk, HD), lambda h, i: (h, 0, 0))
    qo_spec = pl.BlockSpec((tq, GRP * HD), lambda h, i: (i, h))
    row_spec = pl.BlockSpec((1, GRP, tq), lambda h, i: (h, 0, i))
    ins = [q, k, kt, v, o, lse, dy, gate]
    specs = [pl.BlockSpec((tq, GRP * HD), lambda h, i: (i, qcol0 + h)), pl.BlockSpec((1, Sk, k.shape[2]), lambda h, i: (h, 0, 0)),
             pl.BlockSpec((1, Sk // tk, HD, tk), lambda h, i: (h, 0, 0, 0)), kv_spec, qo_spec, row_spec,
             pl.BlockSpec((tq, GRP * HD), lambda h, i: (i, dycol0 + h)), row_spec]
    if mode == "sel":
        ins.append(selneg)
        specs.append(pl.BlockSpec((1, tq, selneg.shape[2]), lambda h, i: (h, i, 0)))
    return pl.pallas_call(
        body, name=name, grid=(N_KV, S // tq), in_specs=specs, out_specs=[qo_spec, kv_spec, kv_spec, row_spec],
        out_shape=[jax.ShapeDtypeStruct((S, ATT_WIDTH), F32), jax.ShapeDtypeStruct((N_KV, Sk, HD), F32),
                   jax.ShapeDtypeStruct((N_KV, Sk, HD), F32), jax.ShapeDtypeStruct((N_KV, GRP, S), F32)],
        scratch_shapes=[pltpu.VMEM((HD, R), F32), pltpu.VMEM((2, tk, R), F32), pltpu.VMEM((2, tk, R), F32)],
        compiler_params=_cp(("parallel", "arbitrary")))(*ins)


def _block_overlap(NC, NB):
    ci = np.arange(NC)[None, :] * 16
    sj = np.arange(NB)[:, None] * SEL_BLOCK
    ov_t = np.clip(np.minimum(ci + 32, sj + SEL_BLOCK) - np.maximum(ci, sj), 0, None) / 32.0
    ov_t[:, NC - 1] = 0.0
    return jnp.asarray(ov_t, F32)


def _chosen_blocks(p, ov_t, q0, tq):
    NB = ov_t.shape[0]
    imp4 = _dot(ov_t, p, "nn")
    imp = imp4[:, 0:tq] + imp4[:, tq:2 * tq] + imp4[:, 2 * tq:3 * tq] + imp4[:, 3 * tq:4 * tq]
    blk = lax.broadcasted_iota(jnp.int32, (NB, tq), 0)
    cur = lax.shift_right_logical(q0 + lax.broadcasted_iota(jnp.int32, (NB, tq), 1), 6)
    imp = jnp.where((blk == 0) | (blk == cur) | (blk == cur - 1), FORCE, imp)
    imp = jnp.where(blk <= cur, imp, -1.0)
    rank = jnp.zeros((NB, tq), F32)
    for j in range(NB):
        row = imp[j:j + 1, :]
        ahead = (row > imp) | ((row == imp) & (blk > j))
        rank = rank + ahead.astype(F32)
    chosen = (rank < float(N_SELECT)) & (imp >= 0.0)
    return jnp.where(chosen, 0.0, NEG).T


def _to_rows16(x):
    S = x.shape[0]
    return x.reshape(S // 16, 16, N_KV, HD).transpose(2, 0, 1, 3).reshape(N_KV, S // 16, 16 * HD)


def _from_rows16(r):
    NC = r.shape[1]
    return r.reshape(N_KV, NC, 16, HD).transpose(1, 2, 0, 3).reshape(NC * 16, N_KV * HD)


DT_COL0 = SSD_WIDTH + CONV_CH
GATE_IN_COL0 = D_IN - 3 * N_HEADS


SHARD_IN = D_IN // N_DEV


def _orig_cols(ref, c0, width):
    pieces, c = [], c0
    while c < c0 + width:
        d, off = divmod(c, SHARD_IN)
        w = min(SHARD_IN - off, c0 + width - c)
        pieces.append(ref[d, :, off:off + w])
        c += w
    return pieces[0] if len(pieces) == 1 else jnp.concatenate(pieces, axis=1)


def _cols_from_slabs(slabs):
    _, R, c = slabs.shape
    tr = _pick(R, (256, 128))

    def body(s_ref, o_ref):
        for t in range(N_DEV * c // LANE):
            pieces, col = [], t * LANE
            while col < (t + 1) * LANE:
                d, off = divmod(col, c)
                w = min(c - off, (t + 1) * LANE - col)
                pieces.append(s_ref[d, :, off:off + w])
                col += w
            o_ref[:, t * LANE:(t + 1) * LANE] = pieces[0] if len(pieces) == 1 else jnp.concatenate(pieces, axis=1)

    return pl.pallas_call(
        body, name="cols_from_slabs", grid=(R // tr,), in_specs=[pl.BlockSpec((N_DEV, tr, c), lambda i: (0, i, 0))],
        out_specs=pl.BlockSpec((tr, N_DEV * c), lambda i: (i, 0)), out_shape=jax.ShapeDtypeStruct((R, N_DEV * c), slabs.dtype),
        compiler_params=_cp(("parallel",)))(slabs)


def _slabs_from_cols(x):
    R, c = x.shape[0], x.shape[1] // N_DEV
    tr = _pick(R, (256, 128))

    def body(x_ref, o_ref):
        for d in range(N_DEV):
            o_ref[d] = x_ref[:, d * c:(d + 1) * c]

    return pl.pallas_call(
        body, name="slabs_from_cols", grid=(R // tr,), in_specs=[pl.BlockSpec((tr, N_DEV * c), lambda i: (i, 0))],
        out_specs=pl.BlockSpec((N_DEV, tr, c), lambda i: (0, i, 0)), out_shape=jax.ShapeDtypeStruct((N_DEV, R, c), x.dtype),
        compiler_params=_cp(("parallel",)))(x)


def _w_in_from_slabs(slabs):
    D = slabs.shape[1]
    tr = _pick(D, (256, 128))

    def body(s_ref, main_ref, small_ref):
        for t in range(W_MAIN // LANE):
            c = t * LANE
            main_ref[:, c:c + LANE] = _orig_cols(s_ref, c if c < DT_COL0 else c + SSD_HEADS, LANE)
        small_ref[...] = jnp.concatenate(
            [_orig_cols(s_ref, DT_COL0, SSD_HEADS), _orig_cols(s_ref, GATE_IN_COL0, 3 * N_HEADS),
             jnp.zeros((tr, W_SMALL - SSD_HEADS - 3 * N_HEADS), small_ref.dtype)], axis=1)

    return pl.pallas_call(
        body, name="w_in_layout", grid=(D // tr,), in_specs=[pl.BlockSpec((N_DEV, tr, SHARD_IN), lambda i: (0, i, 0))],
        out_specs=[pl.BlockSpec((tr, W_MAIN), lambda i: (i, 0)), pl.BlockSpec((tr, W_SMALL), lambda i: (i, 0))],
        out_shape=[jax.ShapeDtypeStruct((D, W_MAIN), slabs.dtype), jax.ShapeDtypeStruct((D, W_SMALL), slabs.dtype)],
        compiler_params=_cp(("parallel",)))(slabs)


def _w_in_to_slabs(main, small):
    D = main.shape[0]
    tr = _pick(D, (256, 128))
    ranges = [(0, DT_COL0, 0, 0), (DT_COL0, DT_COL0 + SSD_HEADS, 1, 0), (DT_COL0 + SSD_HEADS, GATE_IN_COL0, 0, DT_COL0),
              (GATE_IN_COL0, D_IN, 1, SSD_HEADS)]

    def body(main_ref, small_ref, o_ref):
        srcs = (main_ref, small_ref)
        for d in range(N_DEV):
            lo, hi = d * SHARD_IN, (d + 1) * SHARD_IN
            pieces = []
            for start, stop, which, s0 in ranges:
                a, b = max(lo, start), min(hi, stop)
                if a < b:
                    pieces.append(srcs[which][:, s0 + a - start:s0 + b - start].astype(o_ref.dtype))
            o_ref[d] = pieces[0] if len(pieces) == 1 else jnp.concatenate(pieces, axis=1)

    return pl.pallas_call(
        body, name="w_in_grad_layout", grid=(D // tr,),
        in_specs=[pl.BlockSpec((tr, W_MAIN), lambda i: (i, 0)), pl.BlockSpec((tr, W_SMALL), lambda i: (i, 0))],
        out_specs=pl.BlockSpec((N_DEV, tr, SHARD_IN), lambda i: (0, i, 0)),
        out_shape=jax.ShapeDtypeStruct((N_DEV, D, SHARD_IN), main.dtype), compiler_params=_cp(("parallel",)))(main, small)


QB, KCB, VCB, KSB, VSB, KWB, VWB = 10, 14, 15, 16, 17, 18, 19


def _col256(a, b):
    return a[:, b * 256:(b + 1) * 256]


_EARLY = ["w_in", "cmp_w1_k", "cmp_w1_v"]
_LATE = ["w_out", "w_gate", "w_up", "w_down"]
_FFN = ["w_down", "w_gate", "w_up"]
_MID = ["w_out"]
_LAST = ["cmp_w1_k", "cmp_w1_v", "w_in"]


def _local_step(x, tgt, p, late_weights=None, grads_ready=None):
    S = x.shape[0]
    cos, sin = _rope_tables(S)

    u, rs1 = _rms_fwd(x, p["attn_norm_w"], "attn_norm")
    proj = _mm(u, p["w_main"], "nn", F32, "in_proj", after=p.get("before_in_proj"))
    proj_small = _mm(u, p["w_small"], "nn", F32, "in_proj_small")
    xa = _conv_fwd(proj, p["conv_w"], p["conv_b"])
    y_ssd, y_pre, rs_ssd, hs = _ssd_fwd(proj, proj_small, xa, p["dt_bias"], p["a_log"], p["d_skip"], p["ssd_norm_w"])

    q_rot = _rope([proj], QB, ATT_WIDTH, cos, sin, 1.0, _MXU, "rope_q")
    kv = _kv_prep(proj, cos, sin, _attn_cfg(S, S, "sel")[1])
    rk, rv = _to_rows16(_col256(proj, KCB)), _to_rows16(_col256(proj, VCB))
    k_cmp, hid_k = _compress_fwd(rk, p["cmp_pe_k"], p["cmp_w1_k"], p["cmp_w2_k"])
    v_cmp, hid_v = _compress_fwd(rv, p["cmp_pe_v"], p["cmp_w1_v"], p["cmp_w2_v"])
    n_cmp = k_cmp.shape[1]

    gates = proj_small[:, SSD_HEADS:SSD_HEADS + 3 * N_HEADS].reshape(S, N_KV, GRP, 3).transpose(3, 1, 2, 0)
    o_cmp, lse_cmp, y_att, sel = _attn_fwd(proj, QB, k_cmp, _blocked_t(v_cmp, n_cmp), "cmp", None, gates[0], None, F32,
                                           "attn_cmp_fwd")
    o_sel, lse_sel, y_att = _attn_fwd(q_rot, 0, kv["ks_ext"], kv["vs_t"], "sel", sel, gates[1], y_att, F32, "attn_sel_fwd")
    o_win, lse_win, y_att = _attn_fwd(q_rot, 0, kv["kw"], kv["vw_t"], "win", None, gates[2], y_att, _MXU, "attn_win_fwd")

    if late_weights is not None:
        p = {**p, **late_weights(y_att)}
    mixed = jnp.concatenate([y_ssd, y_att], axis=1)
    h1 = _mm(mixed, p["w_out"], "nn", F32, "out_proj", res=x)
    v, rs_ffn = _rms_fwd(h1, p["ffn_norm_w"], "ffn_norm")
    gt, up, act = _ffn_up(v, p["w_gate"], p["w_up"])
    h2 = _mm(act, p["w_down"], "nn", F32, "ffn_down", res=h1)
    loss, dh2, dh2b, d_final_w = _final_loss(h2, p["final_norm_w"], tgt)

    def ready(names):
        return None if grads_ready is None else grads_ready(names, g)

    g = {"final_norm_w": d_final_w}
    g["w_down"] = _mm(act, dh2b, "tn", _MXU, "dw_down")
    dgt, dup = _ffn_dact(dh2b, p["w_down"], gt, up)
    g["w_gate"] = _mm(v, dgt, "tn", _MXU, "dw_gate")
    g["w_up"] = _mm(v, dup, "tn", _MXU, "dw_up")
    dv = _ffn_dv(dgt, dup, p["w_gate"], p["w_up"], ready(_FFN))
    dh1, dh1b, g["ffn_norm_w"] = _rms_bwd(dv, h1, rs_ffn, p["ffn_norm_w"], dh2, "ffn_norm_bwd")
    g["w_out"] = _mm(mixed, dh1b, "tn", _MXU, "dw_out")
    dmixed = _mm(dh1b, p["w_out"], "nt", F32, "dmixed", after=ready(_MID))

    dz, dxa, ddtr, g["dt_bias"], g["a_log"], g["d_skip"], g["ssd_norm_w"] = _ssd_bwd(
        dmixed, proj, proj_small, xa, y_pre, rs_ssd, hs, p["dt_bias"], p["a_log"], p["d_skip"], p["ssd_norm_w"])
    dxbc, g["conv_w"], g["conv_b"] = _conv_bwd(proj, p["conv_w"], p["conv_b"], dxa)

    dyb = SSD_WIDTH // (GRP * HD)
    dq_cmp, dk_cmp, dv_cmp, dg_cmp = _attn_bwd(proj, QB, k_cmp, _blocked_t(k_cmp, n_cmp), v_cmp, o_cmp, lse_cmp, dmixed, dyb,
                                               gates[0], "cmp", None, "attn_cmp_bwd")
    dq_sel, dks, dvs, dg_sel = _attn_bwd(q_rot, 0, kv["ks_ext"], kv["ks_t"], kv["vs"], o_sel, lse_sel, dmixed, dyb, gates[1], "sel",
                                         sel, "attn_sel_bwd")
    dq_win, dkw, dvw, dg_win = _attn_bwd(q_rot, 0, kv["kw"], kv["kw_t"], kv["vw"], o_win, lse_win, dmixed, dyb, gates[2], "win", None,
                                         "attn_win_bwd")
    dgate = jnp.stack([dg_cmp, dg_sel, dg_win]).transpose(3, 1, 2, 0).reshape(S, 3 * N_HEADS)
    drk, g["cmp_w1_k"], g["cmp_w2_k"], g["cmp_pe_k"] = _compress_bwd(rk, p["cmp_pe_k"], p["cmp_w1_k"], p["cmp_w2_k"], hid_k, dk_cmp)
    drv, g["cmp_w1_v"], g["cmp_w2_v"], g["cmp_pe_v"] = _compress_bwd(rv, p["cmp_pe_v"], p["cmp_w1_v"], p["cmp_w2_v"], hid_v, dv_cmp)
    dq = _rope([dq_sel, dq_win], 0, ATT_WIDTH, cos, sin, -1.0, _MXU, "rope_dq", extra=(dq_cmp, 0))
    dkv = _dkv_post(dks, dvs, dkw, dvw, cos, sin)
    dproj = jnp.concatenate([dz, dxbc, dq] + [t.astype(_MXU) for t in (_from_rows16(drk), _from_rows16(drv))] + [dkv], axis=1)
    dsmall = jnp.concatenate([ddtr, dgate, jnp.zeros((S, W_SMALL - SSD_HEADS - 3 * N_HEADS), F32)], axis=1).astype(_MXU)
    g["w_main"] = _mm(u, dproj, "tn", _MXU, "dw_in")
    g["w_small"] = _mm(u, dsmall, "tn", F32, "dw_in_small")
    du = _mm(dproj, p["w_main"], "nt", F32, "du_main", after=ready(_LAST))
    du = _mm(dsmall, p["w_small"], "nt", F32, "du_small", res=du)
    grad_x, _, g["attn_norm_w"] = _rms_bwd(du, x, rs1, p["attn_norm_w"], dh1, "attn_norm_bwd")
    return loss, grad_x, g


MESH_ID = pl.DeviceIdType.MESH


def _my_coords():
    return lax.axis_index("x"), lax.axis_index("y"), lax.axis_index("c")


def _flat_id(px, py, pc):
    return 4 * px + 2 * py + pc


def _peer(k):
    mx, my, mc = _my_coords()
    return (1 - mx if k & 4 else mx, 1 - my if k & 2 else my, 1 - mc if k & 1 else mc)


def _exchange(arrs, scatter, name, after=()):
    n, na = len(arrs), len(after)
    scatter = [scatter] * n if isinstance(scatter, bool) else list(scatter)

    def body(*refs):
        ins, outs = refs[:n], refs[n + na:2 * n + na]
        send_sems, recv_sems, local_sems = refs[2 * n + na:]
        me = _flat_id(*_my_coords())
        copies = []
        for i in range(n):
            src_me = ins[i].at[me] if scatter[i] else ins[i]
            local = pltpu.make_async_copy(src_me, outs[i].at[me], local_sems.at[i])
            local.start()
            copies.append(local)
        for k in range(1, N_DEV):
            peer = _peer(k)
            for i in range(n):
                src = ins[i].at[_flat_id(*peer)] if scatter[i] else ins[i]
                cp = pltpu.make_async_remote_copy(src_ref=src, dst_ref=outs[i].at[me], send_sem=send_sems.at[i * 7 + k - 1],
                                                  recv_sem=recv_sems.at[i * 7 + k - 1], device_id=peer, device_id_type=MESH_ID)
                cp.start()
                copies.append(cp)
        for cp in copies:
            cp.wait()

    any_spec = pl.BlockSpec(memory_space=pl.ANY)
    out_shape = [jax.ShapeDtypeStruct(a.shape if sc else (N_DEV,) + a.shape, a.dtype) for a, sc in zip(arrs, scatter)]
    return pl.pallas_call(
        body, name=name, in_specs=[any_spec] * (n + na), out_specs=[any_spec] * n, out_shape=out_shape,
        scratch_shapes=[pltpu.SemaphoreType.DMA((n * 7,)), pltpu.SemaphoreType.DMA((n * 7,)), pltpu.SemaphoreType.DMA((n,))],
        compiler_params=pltpu.CompilerParams(has_side_effects=True))(*arrs, *after)


def _gather_two_level(arrs, name):
    n = len(arrs)

    def body(*refs):
        ins, outs = refs[:n], refs[n:2 * n]
        send_sems, recv_sems, local_sems = refs[2 * n:]
        x, y, c = _my_coords()
        me, sibling = (x, y, c), (x, y, 1 - c)
        chips = [(1 - x, y), (x, 1 - y), (1 - x, 1 - y)]

        def copy(i, k, block, to, src=None):
            slot = outs[i].at[_flat_id(*block)]
            return pltpu.make_async_remote_copy(src_ref=slot if src is None else src, dst_ref=slot, send_sem=send_sems.at[i * 7 + k],
                                                recv_sem=recv_sems.at[i * 7 + k], device_id=to, device_id_type=MESH_ID)

        mine = [pltpu.make_async_copy(ins[i], outs[i].at[_flat_id(*me)], local_sems.at[i]) for i in range(n)]
        for cp in mine:
            cp.start()
        first = []
        for j, chip in enumerate(chips):
            first += [copy(i, 1 + j, me, (*chip, c), src=ins[i]) for i in range(n)]
        first += [copy(i, 0, me, sibling, src=ins[i]) for i in range(n)]
        for cp in first:
            cp.start()
        passed = []
        for j, chip in enumerate(chips):
            for i in range(n):
                copy(i, 1 + j, (*chip, c), me).wait_recv()
                passed.append(copy(i, 4 + j, (*chip, c), sibling))
                passed[-1].start()
        for i in range(n):
            copy(i, 0, sibling, me).wait_recv()
        for j, chip in enumerate(chips):
            for i in range(n):
                copy(i, 4 + j, (*chip, 1 - c), me).wait_recv()
        for cp in first + passed:
            cp.wait_send()
        for cp in mine:
            cp.wait()

    any_spec = pl.BlockSpec(memory_space=pl.ANY)
    return pl.pallas_call(
        body, name=name, in_specs=[any_spec] * n, out_specs=[any_spec] * n,
        out_shape=[jax.ShapeDtypeStruct((N_DEV,) + a.shape, a.dtype) for a in arrs],
        scratch_shapes=[pltpu.SemaphoreType.DMA((n * 7,)), pltpu.SemaphoreType.DMA((n * 7,)), pltpu.SemaphoreType.DMA((n,))],
        compiler_params=pltpu.CompilerParams(has_side_effects=True))(*arrs)


_HBM = pl.BlockSpec(memory_space=pltpu.HBM)
_SEM = pl.BlockSpec(memory_space=pltpu.SEMAPHORE)
_EFFECT = pltpu.SideEffectType.DATAFLOW_SIDE_EFFECTING


def _split_copies(ins, lands, send_sems, recv_sems, own_sems, scatter):
    me = _flat_id(*_my_coords())
    remote = []
    for k in range(1, N_DEV):
        peer = _peer(k)
        for i in range(len(ins)):
            src = ins[i].at[_flat_id(*peer)] if scatter else ins[i]
            remote.append(pltpu.make_async_remote_copy(src_ref=src, dst_ref=lands[i].at[me], send_sem=send_sems.at[i * 7 + k - 1],
                                                       recv_sem=recv_sems.at[i * 7 + k - 1], device_id=peer, device_id_type=MESH_ID))
    own = [pltpu.make_async_copy(ins[i].at[me] if scatter else ins[i], lands[i].at[me], own_sems.at[i]) for i in range(len(ins))]
    return remote, own


def _split_start(arrs, scatter, name, after=()):
    n, na = len(arrs), len(after)

    def body(*refs):
        remote, own = _split_copies(refs[:n], refs[n:2 * n], refs[2 * n + na], refs[2 * n + na + 1], refs[2 * n + na + 2], scatter)
        for cp in remote + own:
            cp.start()
        refs[-1][...] = jnp.zeros_like(refs[-1])

    land_shapes = [a.shape if scatter else (N_DEV,) + a.shape for a in arrs]
    out_shape = ((pltpu.SemaphoreType.DMA((n * 7,)), pltpu.SemaphoreType.DMA((n * 7,)), pltpu.SemaphoreType.DMA((n,)))
                 + tuple(pltpu.HBM(a.shape, a.dtype) for a in arrs) + tuple(pltpu.HBM(s, a.dtype) for s, a in zip(land_shapes, arrs))
                 + (jax.ShapeDtypeStruct((8, 128), F32),))
    operands = ([pltpu.with_memory_space_constraint(a, pltpu.HBM) for a in arrs]
                + [pltpu.with_memory_space_constraint(lax.empty(s, a.dtype), pltpu.HBM) for s, a in zip(land_shapes, arrs)])
    res = pl.pallas_call(
        body, name=name, out_shape=out_shape, in_specs=[_HBM] * (2 * n) + [pl.BlockSpec(memory_space=pl.ANY)] * na,
        out_specs=(_SEM, _SEM, _SEM) + (_HBM,) * (2 * n) + (pl.BlockSpec(memory_space=pltpu.VMEM),),
        input_output_aliases={i: 3 + i for i in range(2 * n)},
        compiler_params=pltpu.CompilerParams(has_side_effects=_EFFECT))(*operands, *after)
    return dict(send=res[0], recv=res[1], own=res[2], ins=list(res[3:3 + n]), lands=list(res[3 + n:3 + 2 * n]), token=res[-1])


def _split_wait(st, scatter, after, name):
    n = len(st["ins"])

    def body(*refs):
        remote, own = _split_copies(refs[:n], refs[n:2 * n], refs[2 * n], refs[2 * n + 1], refs[2 * n + 2], scatter)
        for cp in remote:
            cp.wait_send()
            cp.wait_recv()
        for cp in own:
            cp.wait()

    arrs = st["ins"] + st["lands"]
    res = pl.pallas_call(
        body, name=name, out_shape=tuple(pltpu.HBM(a.shape, a.dtype) for a in arrs),
        in_specs=[_HBM] * (2 * n) + [_SEM, _SEM, _SEM] + [pl.BlockSpec(memory_space=pl.ANY)] * len(after), out_specs=(_HBM,) * (2 * n),
        input_output_aliases={i: i for i in range(2 * n)},
        compiler_params=pltpu.CompilerParams(has_side_effects=_EFFECT))(*arrs, st["send"], st["recv"], st["own"], *after)
    return list(res[n:])


def _adam_step(p_ref, w_ref, m_ref, v_ref, g_ref, d_ref, nm_ref, nv_ref):
    g = p_ref[0].astype(F32)
    for j in range(1, p_ref.shape[0]):
        g = g + p_ref[j].astype(F32)
    g_ref[...] = g
    nm = ADAM_B1 * m_ref[...] + (1.0 - ADAM_B1) * g
    nv = ADAM_B2 * v_ref[...] + (1.0 - ADAM_B2) * (g * g)
    nm_ref[...] = nm
    nv_ref[...] = nv
    m_hat = nm / (1.0 - ADAM_B1 ** ADAM_STEP)
    v_hat = nv / (1.0 - ADAM_B2 ** ADAM_STEP)
    d_ref[...] = -ADAM_LR * (m_hat / (jnp.sqrt(v_hat) + ADAM_EPS) + ADAM_WD * w_ref[...])


def _adam_sum(parts, w, m, v, name):
    P, R, C = parts.shape
    tr = _pick(R, (256, 128, 64, 32, 8)) if C <= 1024 else _pick(R, (128, 64, 32, 8))
    blk = pl.BlockSpec((tr, C), lambda i: (i, 0))
    return pl.pallas_call(
        functools.partial(_adam_step), name=name, grid=(R // tr,),
        in_specs=[pl.BlockSpec((P, tr, C), lambda i: (0, i, 0)), blk, blk, blk],
        out_specs=[blk] * 4, out_shape=[jax.ShapeDtypeStruct((R, C), F32)] * 4, compiler_params=_cp(("parallel",)))(parts, w, m, v)


def _adam_small(loss_parts, parts, ws, ms, vs):
    n = len(parts)

    def body(*refs):
        loss_ref, ins, outs, total_ref = refs[0], refs[1:4 * n + 1], refs[4 * n + 1:-1], refs[-1]
        for i in range(n):
            _adam_step(ins[i], ins[n + i], ins[2 * n + i], ins[3 * n + i], *outs[4 * i:4 * i + 4])
        total = loss_ref[0]
        for d in range(1, N_DEV):
            total = total + loss_ref[d]
        total_ref[...] = total

    out_shape = [jax.ShapeDtypeStruct(w.shape, F32) for w in ws for _ in range(4)] + [jax.ShapeDtypeStruct(loss_parts.shape[1:], F32)]
    res = pl.pallas_call(body, name="adam_small", out_shape=out_shape)(loss_parts, *parts, *ws, *ms, *vs)
    return res[-1], [tuple(res[4 * i:4 * i + 4]) for i in range(n)]


_WEIGHTS = ["attn_norm_w", "w_in", "conv_w", "conv_b", "dt_bias", "a_log", "d_skip", "ssd_norm_w", "cmp_w1_k", "cmp_w2_k",
            "cmp_w1_v", "cmp_w2_v", "cmp_pe_k", "cmp_pe_v", "w_out", "ffn_norm_w", "w_gate", "w_up", "w_down", "final_norm_w"]
_BIG = ["w_in", "w_gate", "w_up", "w_down", "w_out", "cmp_w1_k", "cmp_w1_v"]
_COL_SHARDED = ("w_in", "w_gate", "w_up")
_REPLICATED = ["attn_norm_w", "conv_b", "dt_bias", "a_log", "d_skip", "ssd_norm_w", "cmp_pe_k", "cmp_pe_v", "ffn_norm_w",
               "final_norm_w"]
_SMALL_SHARDED = ["conv_w", "cmp_w2_k", "cmp_w2_v"]


def _cols_to_slabs(g):
    R = g.shape[0]
    return g.reshape(R, N_DEV, -1).transpose(1, 0, 2)


def _slabs_to_cols(s):
    return s.transpose(1, 0, 2).reshape(s.shape[1], -1)


def kernel(x, attn_norm_w, w_in, conv_w, conv_b, dt_bias, a_log, d_skip, ssd_norm_w, cmp_w1_k, cmp_w2_k, cmp_w1_v, cmp_w2_v, cmp_pe_k, cmp_pe_v, w_out, ffn_norm_w, w_gate, w_up, w_down, final_norm_w, loss_target, m_attn_norm_w, m_w_in, m_conv_w, m_conv_b, m_dt_bias, m_a_log, m_d_skip, m_ssd_norm_w, m_cmp_w1_k, m_cmp_w2_k, m_cmp_w1_v, m_cmp_w2_v, m_cmp_pe_k, m_cmp_pe_v, m_w_out, m_ffn_norm_w, m_w_gate, m_w_up, m_w_down, m_final_norm_w, v_attn_norm_w, v_w_in, v_conv_w, v_conv_b, v_dt_bias, v_a_log, v_d_skip, v_ssd_norm_w, v_cmp_w1_k, v_cmp_w2_k, v_cmp_w1_v, v_cmp_w2_v, v_cmp_pe_k, v_cmp_pe_v, v_w_out, v_ffn_norm_w, v_w_gate, v_w_up, v_w_down, v_final_norm_w):
    a = dict(locals())

    shard = {n: a[n][0].astype(_MXU) for n in _BIG}
    got = _gather_two_level([shard[n] for n in _EARLY] + [cmp_w2_k[0], cmp_w2_v[0], conv_w[0]], "gather_early")
    st_late = _split_start([shard[n] for n in _LATE], False, "gather_late_start", after=(got[0],))

    def assemble(n, t):
        return _cols_from_slabs(t) if n in _COL_SHARDED else t.reshape(-1, t.shape[-1])

    p = dict(attn_norm_w=attn_norm_w, conv_b=conv_b, dt_bias=dt_bias, a_log=a_log, d_skip=d_skip, ssd_norm_w=ssd_norm_w,
             cmp_pe_k=cmp_pe_k.reshape(1, -1), cmp_pe_v=cmp_pe_v.reshape(1, -1), ffn_norm_w=ffn_norm_w,
             final_norm_w=final_norm_w.reshape(1, -1))

    w_main, w_small = _w_in_from_slabs(got[0])
    p.update(before_in_proj=st_late["token"],
             w_main=w_main, w_small=w_small, cmp_w1_k=assemble("cmp_w1_k", got[1]), cmp_w1_v=assemble("cmp_w1_v", got[2]),
             cmp_w2_k=assemble("cmp_w2_k", got[3]).astype(_MXU), cmp_w2_v=assemble("cmp_w2_v", got[4]).astype(_MXU),
             conv_w=_slabs_to_cols(got[5]))

    def late_weights(after):
        got_late = _split_wait(st_late, False, (after,), "gather_late_wait")
        return {n: assemble(n, t) for n, t in zip(_LATE, got_late)}

    def slabs_of(g, n):
        if n == "w_in":
            return _w_in_to_slabs(g["w_main"], g["w_small"])
        return _slabs_from_cols(g[n]) if n in _COL_SHARDED else g[n].reshape(N_DEV, -1, g[n].shape[-1])

    started = []

    def grads_ready(names, g):
        started.append((names, _split_start([slabs_of(g, n) for n in names], True, "scatter_grads_start_%d" % len(started))))
        return started[-1][1]["token"]

    loss_part, grad_x, g = _local_step(x[0], loss_target[0], p, late_weights, grads_ready)

    out, after = {}, (started[-1][1]["token"],)
    for i, (names, st) in enumerate(started):
        if i == len(started) - 1:
            after = after + (grad_x,)
        received = _split_wait(st, True, after, "scatter_grads_wait_%d" % i)
        for n, parts in zip(names, received):
            out[n] = _adam_sum(parts, a[n][0], a["m_" + n][0], a["v_" + n][0], "adam_" + n)
        after = (out[names[-1]][0],)

    small_names = _REPLICATED + _SMALL_SHARDED
    partials = [g[n] for n in _REPLICATED] + [_cols_to_slabs(g["conv_w"])] + [
        g[n].reshape(N_DEV, -1, g[n].shape[-1]) for n in ("cmp_w2_k", "cmp_w2_v")]
    gathered = _exchange([loss_part] + partials, [False] * (1 + len(_REPLICATED)) + [True] * len(_SMALL_SHARDED),
                         "exchange_small_grads", after=(received[0],))
    shapes2d = [t.shape[1:] for t in gathered[1:]]
    loss, res_small = _adam_small(gathered[0], gathered[1:],
                                  *[[a[pre + n].reshape(s) for n, s in zip(small_names, shapes2d)] for pre in ("", "m_", "v_")])
    for n, r in zip(small_names, res_small):
        out[n] = r

    outs = [loss[0, 0], grad_x[None]]
    for j in range(4):
        for n in _WEIGHTS:
            outs.append(out[n][j].reshape(a[n].shape))
    return tuple(outs)
```

```python
import functools

import numpy as np
import jax
import jax.numpy as jnp
from jax import lax
from jax.experimental import pallas as pl
from jax.experimental.pallas import tpu as pltpu

F32 = jnp.float32
_MXU = jnp.bfloat16

N_DEV = 8
SSD_WIDTH = 1024
ATT_WIDTH = 1024
SSD_HEADS = 16
SSD_P = 64
SSD_N = 128
SSD_L = 128
SSD_G = 2
CONV_CH = 1536
CONV_K = 4
HD = 64
N_HEADS = 16
N_KV = 4
GRP = 4
CMP_HID = 256
SEL_BLOCK = 64
N_SELECT = 16
WINDOW = 512
ROPE_DIM = 16
ROPE_THETA = 500000.0
EPS = 1e-6
NEG = -1e30
FORCE = 1e4
SCALE = HD ** -0.5
D_IN = 5184
W_MAIN = 5120
W_SMALL = 128
VMEM_LIMIT = 52 * 1024 * 1024

ADAM_LR, ADAM_B1, ADAM_B2, ADAM_EPS, ADAM_WD, ADAM_STEP = 0.001, 0.9, 0.999, 1e-08, 0.01, 10


def _pick(n, cands):
    for c in cands:
        if n % c == 0:
            return c
    return n


def _cp(sem=None):
    return pltpu.CompilerParams(dimension_semantics=sem, vmem_limit_bytes=VMEM_LIMIT)


def _sigmoid(x):
    return 1.0 / (1.0 + jnp.exp(-x))


def _dot(a, b, dims, split=None):
    dn = {"nn": (((1,), (0,)), ((), ())), "nt": (((1,), (1,)), ((), ())), "tn": (((0,), (0,)), ((), ()))}[dims]
    mm = lambda x, y: lax.dot_general(x.astype(_MXU), y.astype(_MXU), dn, preferred_element_type=F32)
    if split is None:
        return mm(a, b)
    x = (a if split == "a" else b).astype(F32)
    hi = x.astype(_MXU)
    lo = x - hi.astype(F32)
    return mm(hi, b) + mm(lo, b) if split == "a" else mm(a, hi) + mm(a, lo)


LANE = 128
MM_TILE = 1024
MM_K_WHOLE = 2048
MM_K_STEP = 2816
TN_ACC_ELEMS = 3 * 2 ** 20
TN_K_STEP = 1024


def _largest_tile(n, cap):
    if n <= cap:
        return n
    best = LANE
    for t in range(LANE, cap + 1, LANE):
        if n % t == 0:
            best = t
    return best


def _mm_tiles(mode, M, N, K):
    if mode == "tn":
        tm = _largest_tile(M, 2 * MM_TILE)
        return tm, _largest_tile(N, TN_ACC_ELEMS // tm), _largest_tile(K, TN_K_STEP)
    tk = K if K <= MM_K_WHOLE else _largest_tile(K, MM_K_STEP)
    return _largest_tile(M, MM_TILE), _largest_tile(N, MM_TILE), tk


def _mm(a, b, mode, out_dtype, name, res=None, after=None):
    if mode == "nn":
        (M, K), N = a.shape, b.shape[1]
    elif mode == "nt":
        (M, K), N = a.shape, b.shape[0]
    else:
        (K, M), N = a.shape, b.shape[1]
    tm, tn, tk = _mm_tiles(mode, M, N, K)
    nk = K // tk
    a_spec = pl.BlockSpec((tk, tm), lambda i, j, k: (k, i)) if mode == "tn" else pl.BlockSpec((tm, tk), lambda i, j, k: (i, k))
    b_spec = pl.BlockSpec((tn, tk), lambda i, j, k: (j, k)) if mode == "nt" else pl.BlockSpec((tk, tn), lambda i, j, k: (k, j))
    o_spec = pl.BlockSpec((tm, tn), lambda i, j, k: (i, j))

    def finish(r, r_ref, o_ref):
        if res is not None:
            r = r + r_ref[...].astype(F32)
        o_ref[...] = r.astype(out_dtype)

    def body_one_step(*refs):
        a_ref, b_ref, o_ref = refs[0], refs[1], refs[-1]
        finish(_dot(a_ref[...], b_ref[...], mode), refs[2], o_ref)

    def body(*refs):
        a_ref, b_ref, o_ref, acc = refs[0], refs[1], refs[-2], refs[-1]
        k = pl.program_id(2)

        @pl.when(k == 0)
        def _():
            acc[...] = jnp.zeros_like(acc)

        acc[...] += _dot(a_ref[...], b_ref[...], mode)

        @pl.when(k == nk - 1)
        def _():
            finish(acc[...], refs[2], o_ref)

    ins, specs = [a, b], [a_spec, b_spec]
    if res is not None:
        ins.append(res)
        specs.append(o_spec)
    if after is not None:
        ins.append(after)
        specs.append(pl.BlockSpec(memory_space=pl.ANY))
    return pl.pallas_call(
        body_one_step if nk == 1 else body, name=name, grid=(M // tm, N // tn, nk), in_specs=specs, out_specs=o_spec,
        out_shape=jax.ShapeDtypeStruct((M, N), out_dtype), scratch_shapes=[] if nk == 1 else [pltpu.VMEM((tm, tn), F32)],
        compiler_params=_cp(("parallel", "parallel", "arbitrary")))(*ins)


def _ffn_up(v, w_gate, w_up):
    S, D = v.shape
    F = w_gate.shape[1]
    tm, tn = _largest_tile(S, MM_TILE), _largest_tile(F, MM_TILE // 2)

    def body(v_ref, wg_ref, wu_ref, gt_ref, up_ref, act_ref):
        vv = v_ref[...]
        g = _dot(vv, wg_ref[...], "nn")
        u = _dot(vv, wu_ref[...], "nn")
        gt_ref[...] = g.astype(gt_ref.dtype)
        up_ref[...] = u.astype(up_ref.dtype)
        act_ref[...] = (g * _sigmoid(g) * u).astype(act_ref.dtype)

    o_spec = pl.BlockSpec((tm, tn), lambda i, j: (i, j))
    w_spec = pl.BlockSpec((D, tn), lambda i, j: (0, j))
    return pl.pallas_call(
        body, name="ffn_up", grid=(S // tm, F // tn),
        in_specs=[pl.BlockSpec((tm, D), lambda i, j: (i, 0)), w_spec, w_spec], out_specs=[o_spec, o_spec, o_spec],
        out_shape=[jax.ShapeDtypeStruct((S, F), _MXU)] * 3,
        compiler_params=_cp(("parallel", "parallel")))(v, w_gate, w_up)


def _ffn_dv(dgt, dup, w_gate, w_up, after):
    S, F = dgt.shape
    D = w_gate.shape[0]
    tm, tn, _ = _mm_tiles("nt", S, D, F)
    tk = _largest_tile(F, MM_K_STEP // 2)
    nk = F // tk

    def body(g_ref, u_ref, wg_ref, wu_ref, *rest):
        o_ref, acc = rest[-2], rest[-1]
        k = pl.program_id(2)

        @pl.when(k == 0)
        def _():
            acc[...] = jnp.zeros_like(acc)

        acc[...] += _dot(g_ref[...], wg_ref[...], "nt") + _dot(u_ref[...], wu_ref[...], "nt")

        @pl.when(k == nk - 1)
        def _():
            o_ref[...] = acc[...]

    a_spec = pl.BlockSpec((tm, tk), lambda i, j, k: (i, k))
    w_spec = pl.BlockSpec((tn, tk), lambda i, j, k: (j, k))
    ins, specs = [dgt, dup, w_gate, w_up], [a_spec, a_spec, w_spec, w_spec]
    if after is not None:
        ins.append(after)
        specs.append(pl.BlockSpec(memory_space=pl.ANY))
    return pl.pallas_call(
        body, name="ffn_dv", grid=(S // tm, D // tn, nk), in_specs=specs, out_specs=pl.BlockSpec((tm, tn), lambda i, j, k: (i, j)),
        out_shape=jax.ShapeDtypeStruct((S, D), F32), scratch_shapes=[pltpu.VMEM((tm, tn), F32)],
        compiler_params=_cp(("parallel", "parallel", "arbitrary")))(*ins)


def _ffn_dact(dh2, w_down, gt, up):
    S, D = dh2.shape
    F = w_down.shape[0]
    tm, tn = _largest_tile(S, MM_TILE), _largest_tile(F, MM_TILE // 2)

    def body(d_ref, w_ref, gt_ref, up_ref, dg_ref, du_ref):
        da, g, u = _dot(d_ref[...], w_ref[...], "nt"), gt_ref[...].astype(F32), up_ref[...].astype(F32)
        s = _sigmoid(g)
        dg_ref[...] = (da * u * (s * (1.0 + g * (1.0 - s)))).astype(dg_ref.dtype)
        du_ref[...] = (da * (g * s)).astype(du_ref.dtype)

    o_spec = pl.BlockSpec((tm, tn), lambda i, j: (i, j))
    return pl.pallas_call(
        body, name="ffn_dact", grid=(S // tm, F // tn),
        in_specs=[pl.BlockSpec((tm, D), lambda i, j: (i, 0)), pl.BlockSpec((tn, D), lambda i, j: (j, 0)), o_spec, o_spec],
        out_specs=[o_spec, o_spec],
        out_shape=[jax.ShapeDtypeStruct((S, F), _MXU), jax.ShapeDtypeStruct((S, F), _MXU)],
        compiler_params=_cp(("parallel", "parallel")))(dh2, w_down, gt, up)


def _rms_fwd(x, w, name):
    S, D = x.shape
    tr = _pick(S, (256, 128))

    def body(x_ref, w_ref, xn_ref, rs_ref):
        xv = x_ref[...]
        rs = lax.rsqrt(jnp.mean(xv * xv, axis=-1, keepdims=True) + EPS)
        xn_ref[...] = ((xv * rs) * w_ref[...]).astype(xn_ref.dtype)
        rs_ref[...] = rs

    return pl.pallas_call(
        body, name=name, grid=(S // tr,),
        in_specs=[pl.BlockSpec((tr, D), lambda i: (i, 0)), pl.BlockSpec((1, D), lambda i: (0, 0))],
        out_specs=[pl.BlockSpec((tr, D), lambda i: (i, 0)), pl.BlockSpec((tr, 1), lambda i: (i, 0))],
        out_shape=[jax.ShapeDtypeStruct((S, D), _MXU), jax.ShapeDtypeStruct((S, 1), F32)],
        compiler_params=_cp(("parallel",)))(x, w)


def _rms_bwd(dyn, x, rs, w, res, name):
    S, D = x.shape
    tr = _pick(S, (256, 128))

    def body(dy_ref, x_ref, rs_ref, w_ref, res_ref, dx_ref, dxb_ref, dw_ref):
        @pl.when(pl.program_id(0) == 0)
        def _():
            dw_ref[...] = jnp.zeros_like(dw_ref)

        dy, r = dy_ref[...].astype(F32), rs_ref[...]
        xhat = x_ref[...] * r
        dw_ref[...] += jnp.sum(dy * xhat, axis=0, keepdims=True)
        dxhat = dy * w_ref[...]
        dx = res_ref[...] + r * (dxhat - xhat * jnp.mean(dxhat * xhat, axis=-1, keepdims=True))
        dx_ref[...] = dx
        dxb_ref[...] = dx.astype(dxb_ref.dtype)

    row = pl.BlockSpec((tr, D), lambda i: (i, 0))
    vec = pl.BlockSpec((1, D), lambda i: (0, 0))
    return pl.pallas_call(
        body, name=name, grid=(S // tr,),
        in_specs=[row, row, pl.BlockSpec((tr, 1), lambda i: (i, 0)), vec, row], out_specs=[row, row, vec],
        out_shape=[jax.ShapeDtypeStruct((S, D), F32), jax.ShapeDtypeStruct((S, D), _MXU), jax.ShapeDtypeStruct((1, D), F32)],
        compiler_params=_cp(("arbitrary",)))(dyn, x, rs, w, res)


def _final_loss(h2, w, tgt):
    S, D = h2.shape
    tr = _pick(S, (256, 128))

    def body(h_ref, w_ref, t_ref, loss_ref, dh_ref, dhb_ref, dw_ref):
        @pl.when(pl.program_id(0) == 0)
        def _():
            dw_ref[...] = jnp.zeros_like(dw_ref)
            loss_ref[...] = jnp.zeros_like(loss_ref)

        hv, wv = h_ref[...], w_ref[...]
        rs = lax.rsqrt(jnp.mean(hv * hv, axis=-1, keepdims=True) + EPS)
        xhat = hv * rs
        err = xhat * wv - t_ref[...]
        row = jnp.mean(err * err, axis=-1, keepdims=True)
        loss_ref[...] += jnp.broadcast_to(0.5 * jnp.sum(row, axis=0, keepdims=True), loss_ref.shape)
        dy = err * (1.0 / D)
        dw_ref[...] += jnp.sum(dy * xhat, axis=0, keepdims=True)
        dxhat = dy * wv
        dh = rs * (dxhat - xhat * jnp.mean(dxhat * xhat, axis=-1, keepdims=True))
        dh_ref[...] = dh
        dhb_ref[...] = dh.astype(dhb_ref.dtype)

    row = pl.BlockSpec((tr, D), lambda i: (i, 0))
    vec = pl.BlockSpec((1, D), lambda i: (0, 0))
    return pl.pallas_call(
        body, name="final_loss", grid=(S // tr,), in_specs=[row, vec, row],
        out_specs=[pl.BlockSpec((1, LANE), lambda i: (0, 0)), row, row, vec],
        out_shape=[jax.ShapeDtypeStruct((1, LANE), F32), jax.ShapeDtypeStruct((S, D), F32), jax.ShapeDtypeStruct((S, D), _MXU),
                   jax.ShapeDtypeStruct((1, D), F32)],
        compiler_params=_cp(("arbitrary",)))(h2, w, tgt)


def _shift_rows(x, k, rows):
    if k == 0:
        return x
    S = x.shape[0]
    r = pltpu.roll(x, k % S, axis=0)
    ok = (rows >= k) if k > 0 else (rows < S + k)
    return jnp.where(ok, r, 0.0)


XBC_COL0 = SSD_WIDTH // 128


def _conv_fwd(proj, conv_w, conv_b):
    S = proj.shape[0]
    nct = CONV_CH // 128

    def body(x_ref, w_ref, b_ref, o_ref):
        x = x_ref[...]
        rows = lax.broadcasted_iota(jnp.int32, x.shape, 0)
        c = b_ref[...] + w_ref[3:4, :] * x
        for k in range(1, CONV_K):
            c = c + w_ref[3 - k:4 - k, :] * _shift_rows(x, k, rows)
        o_ref[...] = c * _sigmoid(c)

    return pl.pallas_call(
        body, name="conv_fwd", grid=(nct,),
        in_specs=[pl.BlockSpec((S, 128), lambda j: (0, XBC_COL0 + j)), pl.BlockSpec((CONV_K, 128), lambda j: (0, j)),
                  pl.BlockSpec((1, 128), lambda j: (0, j))],
        out_specs=pl.BlockSpec((S, 128), lambda j: (0, j)),
        out_shape=jax.ShapeDtypeStruct((S, CONV_CH), F32), compiler_params=_cp(("parallel",)))(proj, conv_w, conv_b)


def _conv_bwd(proj, conv_w, conv_b, dxa):
    S = proj.shape[0]
    nct = CONV_CH // 128

    def body(x_ref, w_ref, b_ref, d_ref, dx_ref, dw_ref, db_ref):
        x = x_ref[...]
        rows = lax.broadcasted_iota(jnp.int32, x.shape, 0)
        xs = [_shift_rows(x, k, rows) for k in range(CONV_K)]
        c = b_ref[...] + w_ref[3:4, :] * x
        for k in range(1, CONV_K):
            c = c + w_ref[3 - k:4 - k, :] * xs[k]
        s = _sigmoid(c)
        dc = d_ref[...] * (s * (1.0 + c * (1.0 - s)))
        dx = w_ref[3:4, :] * dc
        for k in range(1, CONV_K):
            dx = dx + w_ref[3 - k:4 - k, :] * _shift_rows(dc, -k, rows)
        dx_ref[...] = dx.astype(dx_ref.dtype)
        for k in range(CONV_K):
            dw_ref[3 - k:4 - k, :] = jnp.sum(dc * xs[k], axis=0, keepdims=True)
        db_ref[...] = jnp.sum(dc, axis=0, keepdims=True)

    col = pl.BlockSpec((S, 128), lambda j: (0, j))
    return pl.pallas_call(
        body, name="conv_bwd", grid=(nct,),
        in_specs=[pl.BlockSpec((S, 128), lambda j: (0, XBC_COL0 + j)), pl.BlockSpec((CONV_K, 128), lambda j: (0, j)),
                  pl.BlockSpec((1, 128), lambda j: (0, j)), col],
        out_specs=[col, pl.BlockSpec((CONV_K, 128), lambda j: (0, j)), pl.BlockSpec((1, 128), lambda j: (0, j))],
        out_shape=[jax.ShapeDtypeStruct((S, CONV_CH), _MXU), jax.ShapeDtypeStruct((CONV_K, CONV_CH), F32),
                   jax.ShapeDtypeStruct((1, CONV_CH), F32)],
        compiler_params=_cp(("parallel",)))(proj, conv_w, conv_b, dxa)


def _ssd_consts():
    L = SSD_L
    r = lax.broadcasted_iota(jnp.int32, (L, L), 0)
    c = lax.broadcasted_iota(jnp.int32, (L, L), 1)
    causal = r >= c
    upper = (r <= c).astype(F32)
    hr = lax.broadcasted_iota(jnp.int32, (SSD_HEADS, SSD_WIDTH), 0)
    hc = lax.broadcasted_iota(jnp.int32, (SSD_HEADS, SSD_WIDTH), 1)
    expand = (lax.shift_right_logical(hc, 6) == hr).astype(F32)
    return causal, causal.astype(F32), upper, expand


def _softplus(x):
    return jnp.maximum(x, 0.0) + jnp.log(1.0 + jnp.exp(-jnp.abs(x)))


def _ssd_scalars(dtr, dt_bias, a_log, tri, upper, expand):
    dt = _softplus(dtr + dt_bias)
    A = -jnp.exp(a_log)
    adt = dt * A
    acum = _dot(tri, adt, "nn", split="b")
    acum_t = _dot(adt, upper, "tn", split="a")
    alast = acum[SSD_L - 1:SSD_L, :]
    e = jnp.exp(acum)
    wdec = jnp.exp(alast - acum)
    gam = jnp.exp(alast)
    ex = lambda t: _dot(t, expand, "nn", split="a")
    gam8 = jnp.broadcast_to(gam, (8, SSD_HEADS))
    return dt, A, acum, acum_t, e, wdec, gam, ex(dt), ex(e), ex(wdec), ex(gam8)[0:1, :]


def _ssd_fwd(proj, proj_small, xa, dt_bias, a_log, d_skip, norm_w):
    S = proj.shape[0]
    L, N, W = SSD_L, SSD_N, SSD_WIDTH
    nc = S // L

    def body(z_ref, xa_ref, dtr_ref, dtb_ref, al_ref, dsk_ref, nw_ref, yo_ref, y_ref, rs_ref, hs_ref, h_scr, y_scr):
        @pl.when(pl.program_id(0) == 0)
        def _():
            h_scr[...] = jnp.zeros_like(h_scr)

        causal, tri, upper, expand = _ssd_consts()
        dt, A, acum, acum_t, e, wdec, gam, dtE, eE, wE, gamE = _ssd_scalars(dtr_ref[:, 0:SSD_HEADS], dtb_ref[...], al_ref[...], tri, upper, expand)
        xs = xa_ref[:, 0:W]
        X = xs * dtE
        XW = X * wE
        hs_ref[0] = h_scr[...]
        for g in range(SSD_G):
            gs = slice(g * 512, (g + 1) * 512)
            Bg = xa_ref[:, W + g * N:W + (g + 1) * N]
            Cg = xa_ref[:, W + SSD_G * N + g * N:W + SSD_G * N + (g + 1) * N]
            Hg = h_scr[:, gs]
            CB = _dot(Cg, Bg, "nt")
            yoff = _dot(Cg, Hg, "nn") * eE[:, gs]
            st = _dot(Bg, XW[:, gs], "tn")
            for j in range(8):
                h = g * 8 + j
                hsl = slice(h * SSD_P, (h + 1) * SSD_P)
                lam = jnp.exp(jnp.where(causal, acum[:, h:h + 1] - acum_t[h:h + 1, :], -jnp.inf))
                y_scr[:, hsl] = _dot(CB * lam, X[:, hsl], "nn") + yoff[:, j * SSD_P:(j + 1) * SSD_P]
            h_scr[:, gs] = gamE[:, gs] * Hg + st
        dskE = _dot(jnp.broadcast_to(dsk_ref[...], (8, SSD_HEADS)), expand, "nn", split="a")[0:1, :]
        y = y_scr[...] + dskE * xs
        y_ref[...] = y
        zv = z_ref[...]
        yg = y * (zv * _sigmoid(zv))
        rs = lax.rsqrt(jnp.mean(yg * yg, axis=-1, keepdims=True) + EPS)
        rs_ref[...] = rs
        yo_ref[...] = ((yg * rs) * nw_ref[...]).astype(yo_ref.dtype)

    p16 = pl.BlockSpec((1, SSD_HEADS), lambda c: (0, 0))
    return pl.pallas_call(
        body, name="ssd_fwd", grid=(nc,),
        in_specs=[pl.BlockSpec((L, W), lambda c: (c, 0)), pl.BlockSpec((L, CONV_CH), lambda c: (c, 0)),
                  pl.BlockSpec((L, W_SMALL), lambda c: (c, 0)), p16, p16, p16, pl.BlockSpec((1, W), lambda c: (0, 0))],
        out_specs=[pl.BlockSpec((L, W), lambda c: (c, 0)), pl.BlockSpec((L, W), lambda c: (c, 0)),
                   pl.BlockSpec((L, 1), lambda c: (c, 0)), pl.BlockSpec((1, N, W), lambda c: (c, 0, 0))],
        out_shape=[jax.ShapeDtypeStruct((S, W), _MXU), jax.ShapeDtypeStruct((S, W), F32), jax.ShapeDtypeStruct((S, 1), F32),
                   jax.ShapeDtypeStruct((nc, N, W), F32)],
        scratch_shapes=[pltpu.VMEM((N, W), F32), pltpu.VMEM((L, W), F32)],
        compiler_params=_cp(("arbitrary",)))(proj, xa, proj_small, dt_bias, a_log, d_skip, norm_w)


def _ssd_bwd(dmixed, proj, proj_small, xa, y, rs2, hs, dt_bias, a_log, d_skip, norm_w):
    S = proj.shape[0]
    L, N, W, H = SSD_L, SSD_N, SSD_WIDTH, SSD_HEADS
    nc = S // L

    def body(dyo_ref, z_ref, xa_ref, dtr_ref, y_ref, rs_ref, hs_ref, dtb_ref, al_ref, dsk_ref, nw_ref,
             dz_ref, dxa_ref, ddtr_ref, ddtb_ref, dal_ref, ddsk_ref, dnw_ref, dh_scr, dx_scr):
        @pl.when(pl.program_id(0) == 0)
        def _():
            dh_scr[...] = jnp.zeros_like(dh_scr)
            ddtb_ref[...] = jnp.zeros_like(ddtb_ref)
            dal_ref[...] = jnp.zeros_like(dal_ref)
            ddsk_ref[...] = jnp.zeros_like(ddsk_ref)
            dnw_ref[...] = jnp.zeros_like(dnw_ref)

        causal, tri, upper, expand = _ssd_consts()
        heads = lambda t: _dot(t, expand, "nt", split="a")
        onehot = lambda h: (lax.broadcasted_iota(jnp.int32, (1, H), 1) == h).astype(F32)

        zv, yv, rs = z_ref[...], y_ref[...], rs_ref[...]
        sz = _sigmoid(zv)
        zs = zv * sz
        xhat = (yv * zs) * rs
        dyo = dyo_ref[...].astype(F32)
        dnw_ref[...] += jnp.sum(dyo * xhat, axis=0, keepdims=True)
        dxhat = dyo * nw_ref[...]
        dyg = rs * (dxhat - xhat * jnp.mean(dxhat * xhat, axis=-1, keepdims=True))
        dz_ref[...] = (dyg * yv * (sz * (1.0 + zv * (1.0 - sz)))).astype(dz_ref.dtype)
        dy = dyg * zs

        dtr = dtr_ref[:, 0:H]
        dt, A, acum, acum_t, e, wdec, gam, dtE, eE, wE, gamE = _ssd_scalars(dtr, dtb_ref[...], al_ref[...], tri, upper, expand)
        xs = xa_ref[:, 0:W]
        X = xs * dtE
        XW = X * wE
        dskE = _dot(jnp.broadcast_to(dsk_ref[...], (8, H)), expand, "nn", split="a")[0:1, :]
        ddsk_ref[...] += heads(jnp.broadcast_to(jnp.sum(dy * xs, axis=0, keepdims=True), (8, W)))[0:1, :]

        dYe = dy * eE
        dacum = jnp.zeros((L, H), F32)
        de_full = []
        dw_full = []
        dgam_full = []
        for g in range(SSD_G):
            gs = slice(g * 512, (g + 1) * 512)
            Bg = xa_ref[:, W + g * N:W + (g + 1) * N]
            Cg = xa_ref[:, W + SSD_G * N + g * N:W + SSD_G * N + (g + 1) * N]
            Hg = hs_ref[0, :, gs]
            dHn = dh_scr[:, gs]
            CH = _dot(Cg, Hg, "nn")
            de_full.append(dy[:, gs] * CH)
            dC = _dot(dYe[:, gs], Hg, "nt")
            dHs = gamE[:, gs] * dHn + _dot(Cg, dYe[:, gs], "tn")
            dgam_full.append(jnp.sum(dHn * Hg, axis=0, keepdims=True))
            BdS = _dot(Bg, dHn, "nn")
            dB = _dot(XW[:, gs], dHn, "nt")
            dx_scr[:, gs] = BdS * wE[:, gs]
            dw_full.append(BdS * X[:, gs])
            CB = _dot(Cg, Bg, "nt")
            dCB = jnp.zeros((L, L), F32)
            for j in range(8):
                h = g * 8 + j
                hsl = slice(h * SSD_P, (h + 1) * SSD_P)
                lam = jnp.exp(jnp.where(causal, acum[:, h:h + 1] - acum_t[h:h + 1, :], -jnp.inf))
                M = CB * lam
                dM = _dot(dy[:, hsl], X[:, hsl], "nt")
                dx_scr[:, hsl] += _dot(M, dy[:, hsl], "tn")
                dCB = dCB + dM * lam
                Q = dM * M
                rowsum = jnp.sum(Q, axis=1, keepdims=True)
                colsum = _dot(Q, jnp.ones((L, 8), F32), "tn", split="a")[:, 0:1]
                dacum = dacum + (rowsum - colsum) * onehot(h)
            dC = dC + _dot(dCB, Bg, "nn")
            dB = dB + _dot(dCB, Cg, "tn")
            dxa_ref[:, W + g * N:W + (g + 1) * N] = dB
            dxa_ref[:, W + SSD_G * N + g * N:W + SSD_G * N + (g + 1) * N] = dC
            dh_scr[:, gs] = dHs

        de16 = heads(jnp.concatenate(de_full, axis=1))
        dw16 = heads(jnp.concatenate(dw_full, axis=1))
        dgam16 = heads(jnp.broadcast_to(jnp.concatenate(dgam_full, axis=1), (8, W)))[0:1, :]
        dacum = dacum + de16 * e - dw16 * wdec
        dlast = jnp.sum(dw16 * wdec, axis=0, keepdims=True) + dgam16 * gam
        lastrow = (lax.broadcasted_iota(jnp.int32, (L, 1), 0) == L - 1).astype(F32)
        dacum = dacum + lastrow * dlast
        da = _dot(tri, dacum, "tn", split="b")
        dX = dx_scr[...]
        ddt = da * A + heads(dX * xs)
        dA = jnp.sum(da * dt, axis=0, keepdims=True)
        dal_ref[...] += dA * A
        ddtr = ddt * _sigmoid(dtr + dtb_ref[...])
        ddtb_ref[...] += jnp.sum(ddtr, axis=0, keepdims=True)
        ddtr_ref[...] = ddtr
        dxa_ref[:, 0:W] = dX * dtE + dy * dskE

    p16 = pl.BlockSpec((1, H), lambda c: (0, 0))
    rev = lambda c: (nc - 1 - c, 0)
    return pl.pallas_call(
        body, name="ssd_bwd", grid=(nc,),
        in_specs=[pl.BlockSpec((L, W), rev), pl.BlockSpec((L, W), rev), pl.BlockSpec((L, CONV_CH), rev),
                  pl.BlockSpec((L, W_SMALL), rev), pl.BlockSpec((L, W), rev), pl.BlockSpec((L, 1), rev),
                  pl.BlockSpec((1, N, W), lambda c: (nc - 1 - c, 0, 0)), p16, p16, p16, pl.BlockSpec((1, W), lambda c: (0, 0))],
        out_specs=[pl.BlockSpec((L, W), rev), pl.BlockSpec((L, CONV_CH), rev), pl.BlockSpec((L, H), rev),
                   p16, p16, p16, pl.BlockSpec((1, W), lambda c: (0, 0))],
        out_shape=[jax.ShapeDtypeStruct((S, W), _MXU), jax.ShapeDtypeStruct((S, CONV_CH), F32), jax.ShapeDtypeStruct((S, H), F32),
                   jax.ShapeDtypeStruct((1, H), F32), jax.ShapeDtypeStruct((1, H), F32), jax.ShapeDtypeStruct((1, H), F32),
                   jax.ShapeDtypeStruct((1, W), F32)],
        scratch_shapes=[pltpu.VMEM((N, W), F32), pltpu.VMEM((L, W), F32)],
        compiler_params=_cp(("arbitrary",)))(dmixed, proj, xa, proj_small, y, rs2, hs, dt_bias, a_log, d_skip, norm_w)


def _rope_tables(S):
    inv = 1.0 / (ROPE_THETA ** (jnp.arange(0, ROPE_DIM, 2, dtype=F32) / ROPE_DIM))
    ang = jnp.arange(S, dtype=F32)[:, None] * inv[None, :]
    cos, sin = jnp.cos(ang), jnp.sin(ang)
    half = ROPE_DIM // 2
    c64 = jnp.concatenate([cos, cos, jnp.ones((S, HD - ROPE_DIM), F32)], axis=1)
    s64 = jnp.concatenate([sin, sin, jnp.zeros((S, HD - ROPE_DIM), F32)], axis=1)
    del half
    return jnp.concatenate([c64, c64], axis=1), jnp.concatenate([s64, s64], axis=1)


def _rope(xs, blk0, width, cos, sin, sign, out_dtype, name, extra=None):
    S = xs[0].shape[0]
    tr = _pick(S, (512, 256, 128))
    nx = len(xs)

    def body(*refs):
        x_refs, c_ref, s_ref = refs[:nx], refs[nx], refs[nx + 1]
        e_ref = refs[nx + 2] if extra is not None else None
        o_ref = refs[-1]
        cv, sv = c_ref[...], s_ref[...] * sign
        lane = lax.broadcasted_iota(jnp.int32, (tr, 128), 1)
        first = (lane & (HD - 1)) < (ROPE_DIM // 2)
        for j in range(bw // 128):
            cs = slice(j * 128, (j + 1) * 128)
            xv = x_refs[0][:, cs].astype(F32)
            for r in x_refs[1:]:
                xv = xv + r[:, cs].astype(F32)
            out = _rotate128(xv, cv, sv, first)
            if extra is not None:
                out = out + e_ref[:, cs].astype(F32)
            o_ref[:, cs] = out.astype(out_dtype)

    bw = 512
    assert width % bw == 0 and (blk0 * 256) % bw == 0
    b0 = blk0 * 256 // bw
    t128 = pl.BlockSpec((tr, 128), lambda i, j: (i, 0))
    oblk = pl.BlockSpec((tr, bw), lambda i, j: (i, j))
    specs = [pl.BlockSpec((tr, bw), lambda i, j: (i, b0 + j))] * nx + [t128, t128]
    ins = list(xs) + [cos, sin]
    if extra is not None:
        assert (extra[1] * 256) % bw == 0
        ins.append(extra[0])
        eb = extra[1] * 256 // bw
        specs.append(pl.BlockSpec((tr, bw), lambda i, j: (i, eb + j)))
    return pl.pallas_call(
        body, name=name, grid=(S // tr, width // bw), in_specs=specs, out_specs=oblk,
        out_shape=jax.ShapeDtypeStruct((S, width), out_dtype), compiler_params=_cp(("parallel", "parallel")))(*ins)


def _rotate128(xv, cv, sv, first):
    rot = jnp.where(first, -pltpu.roll(xv, 128 - ROPE_DIM // 2, axis=1), pltpu.roll(xv, ROPE_DIM // 2, axis=1))
    return xv * cv + rot * sv


def _kv_prep(proj, cos, sin, tk):
    S = proj.shape[0]
    NB = S // SEL_BLOCK

    def body(ks_ref, vs_ref, kw_ref, vw_ref, c_ref, s_ref, *outs):
        cv, sv = c_ref[...], s_ref[...]
        lane = lax.broadcasted_iota(jnp.int32, (tk, 128), 1)
        first = (lane & (HD - 1)) < (ROPE_DIM // 2)
        key = pl.program_id(0) * tk + lax.broadcasted_iota(jnp.int32, (tk, NB), 0)
        onehot = (lax.shift_right_logical(key, 6) == lax.broadcasted_iota(jnp.int32, (tk, NB), 1)).astype(F32)
        for j, (ref, rotated) in enumerate(((ks_ref, True), (vs_ref, False), (kw_ref, True), (vw_ref, False))):
            nat, blk = outs[2 * j], outs[2 * j + 1]
            for half in range(2):
                xv = ref[:, half * 128:(half + 1) * 128]
                if rotated:
                    xv = _rotate128(xv, cv, sv, first)
                for e in range(2):
                    h = 2 * half + e
                    piece = xv[:, e * HD:(e + 1) * HD]
                    nat[h] = (jnp.concatenate([piece, onehot], axis=1) if j == 0 else piece).astype(nat.dtype)
                    blk[h, 0] = piece.T.astype(blk.dtype)

    col = lambda b: pl.BlockSpec((tk, 256), lambda i: (i, b))
    t128 = pl.BlockSpec((tk, 128), lambda i: (i, 0))
    nat_spec = lambda w: pl.BlockSpec((N_KV, tk, w), lambda i: (0, i, 0))
    blk_spec = pl.BlockSpec((N_KV, 1, HD, tk), lambda i: (0, i, 0, 0))
    nat_shape = lambda w: jax.ShapeDtypeStruct((N_KV, S, w), _MXU)
    blk_shape = jax.ShapeDtypeStruct((N_KV, S // tk, HD, tk), _MXU)
    widths = (HD + NB, HD, HD, HD)
    res = pl.pallas_call(
        body, name="kv_prep", grid=(S // tk,), in_specs=[col(KSB), col(VSB), col(KWB), col(VWB), t128, t128],
        out_specs=[s for w in widths for s in (nat_spec(w), blk_spec)],
        out_shape=[s for w in widths for s in (nat_shape(w), blk_shape)],
        compiler_params=_cp(("parallel",)))(proj, proj, proj, proj, cos, sin)
    return dict(ks_ext=res[0], ks_t=res[1], vs=res[2], vs_t=res[3], kw=res[4], kw_t=res[5], vw=res[6], vw_t=res[7])


def _dkv_post(dks, dvs, dkw, dvw, cos, sin):
    S = dks.shape[1]
    tr = _pick(S, (512, 256, 128))

    def body(dks_ref, dvs_ref, dkw_ref, dvw_ref, c_ref, s_ref, o_ref):
        cv, sv = c_ref[...], -s_ref[...]
        lane = lax.broadcasted_iota(jnp.int32, (tr, 128), 1)
        first = (lane & (HD - 1)) < (ROPE_DIM // 2)
        for j, (ref, rotated) in enumerate(((dks_ref, True), (dvs_ref, False), (dkw_ref, True), (dvw_ref, False))):
            for half in range(2):
                xv = jnp.concatenate([ref[2 * half], ref[2 * half + 1]], axis=1)
                if rotated:
                    xv = _rotate128(xv, cv, sv, first)
                o_ref[:, j * 256 + half * 128:j * 256 + (half + 1) * 128] = xv.astype(o_ref.dtype)

    hm = pl.BlockSpec((N_KV, tr, HD), lambda i: (0, i, 0))
    t128 = pl.BlockSpec((tr, 128), lambda i: (i, 0))
    return pl.pallas_call(
        body, name="dkv_post", grid=(S // tr,), in_specs=[hm, hm, hm, hm, t128, t128],
        out_specs=pl.BlockSpec((tr, 4 * 256), lambda i: (i, 0)), out_shape=jax.ShapeDtypeStruct((S, 4 * 256), _MXU),
        compiler_params=_cp(("parallel",)))(dks, dvs, dkw, dvw, cos, sin)


def _compress_fwd(R, pe, w1, w2):
    NC = R.shape[1]
    half = 16 * HD

    def body(r_ref, pe_ref, w1_ref, w2_ref, o_ref, hid_ref):
        r = r_ref[0]
        a = _dot(r + pe_ref[:, 0:half], w1_ref[0:half, :], "nn")
        b = _dot(r + pe_ref[:, half:2 * half], w1_ref[half:2 * half, :], "nn")
        hid = a + pltpu.roll(b, NC - 1, axis=0)
        hid_ref[0] = hid
        out = _dot(hid * _sigmoid(hid), w2_ref[...], "nn")
        rows = lax.broadcasted_iota(jnp.int32, out.shape, 0)
        o_ref[0] = jnp.where(rows < NC - 1, out, 0.0).astype(o_ref.dtype)

    return pl.pallas_call(
        body, name="compress_fwd", grid=(N_KV,),
        in_specs=[pl.BlockSpec((1, NC, half), lambda h: (h, 0, 0)), pl.BlockSpec((1, 2 * half), lambda h: (0, 0)),
                  pl.BlockSpec((2 * half, CMP_HID), lambda h: (0, 0)), pl.BlockSpec((CMP_HID, HD), lambda h: (0, 0))],
        out_specs=[pl.BlockSpec((1, NC, HD), lambda h: (h, 0, 0)), pl.BlockSpec((1, NC, CMP_HID), lambda h: (h, 0, 0))],
        out_shape=[jax.ShapeDtypeStruct((N_KV, NC, HD), _MXU), jax.ShapeDtypeStruct((N_KV, NC, CMP_HID), F32)],
        compiler_params=_cp(("parallel",)))(R, pe, w1, w2)


def _compress_bwd(R, pe, w1, w2, hid, dout):
    NC = R.shape[1]
    half = 16 * HD

    def body(r_ref, pe_ref, w1_ref, w2_ref, hid_ref, do_ref, dr_ref, dw1_ref, dw2_ref, dpe_ref):
        @pl.when(pl.program_id(0) == 0)
        def _():
            dw1_ref[...] = jnp.zeros_like(dw1_ref)
            dw2_ref[...] = jnp.zeros_like(dw2_ref)
            dpe_ref[...] = jnp.zeros_like(dpe_ref)

        r, hv, do = r_ref[0], hid_ref[0], do_ref[0]
        s = _sigmoid(hv)
        dw2_ref[...] += _dot(hv * s, do, "tn")
        dhid = _dot(do, w2_ref[...], "nt") * (s * (1.0 + hv * (1.0 - s)))
        rows = lax.broadcasted_iota(jnp.int32, dhid.shape, 0)
        dhid = jnp.where(rows < NC - 1, dhid, 0.0)
        dhid_dn = pltpu.roll(dhid, 1, axis=0)
        dw1_ref[0:half, :] += _dot(r + pe_ref[:, 0:half], dhid, "tn")
        dw1_ref[half:2 * half, :] += _dot(r + pe_ref[:, half:2 * half], dhid_dn, "tn")
        dxt = _dot(dhid, w1_ref[0:half, :], "nt")
        dxb = _dot(dhid_dn, w1_ref[half:2 * half, :], "nt")
        dr_ref[0] = dxt + dxb
        dpe_ref[:, 0:half] += jnp.sum(dxt, axis=0, keepdims=True)
        dpe_ref[:, half:2 * half] += jnp.sum(dxb, axis=0, keepdims=True)

    return pl.pallas_call(
        body, name="compress_bwd", grid=(N_KV,),
        in_specs=[pl.BlockSpec((1, NC, half), lambda h: (h, 0, 0)), pl.BlockSpec((1, 2 * half), lambda h: (0, 0)),
                  pl.BlockSpec((2 * half, CMP_HID), lambda h: (0, 0)), pl.BlockSpec((CMP_HID, HD), lambda h: (0, 0)),
                  pl.BlockSpec((1, NC, CMP_HID), lambda h: (h, 0, 0)), pl.BlockSpec((1, NC, HD), lambda h: (h, 0, 0))],
        out_specs=[pl.BlockSpec((1, NC, half), lambda h: (h, 0, 0)), pl.BlockSpec((2 * half, CMP_HID), lambda h: (0, 0)),
                   pl.BlockSpec((CMP_HID, HD), lambda h: (0, 0)), pl.BlockSpec((1, 2 * half), lambda h: (0, 0))],
        out_shape=[jax.ShapeDtypeStruct((N_KV, NC, half), F32), jax.ShapeDtypeStruct((2 * half, CMP_HID), F32),
                   jax.ShapeDtypeStruct((CMP_HID, HD), F32), jax.ShapeDtypeStruct((1, 2 * half), F32)],
        compiler_params=_cp(("arbitrary",)))(R, pe, w1, w2, hid, dout)


def _attn_cfg(S, Sk, mode):
    tk = _pick(Sk, (256, 128))
    if mode == "cmp":
        return _pick(S, (512, 256, 128)), Sk
    if mode == "sel" and S % (2 * tk) == 0:
        return 2 * tk, tk
    return tk, tk


def _block_start(kb, tk):
    return kb * tk if isinstance(kb, int) else pl.multiple_of(kb * tk, tk)


def _pipelined_key_blocks(mode, q0, tq, tk, produce, consume):
    if mode == "cmp":
        produce(0, True, 0)
        consume(0, 0)
        return
    if mode == "win":
        assert tq == tk and WINDOW == 2 * tk
        last = q0 // tk
        first = jnp.maximum(last - 2, 0)

        @pl.when(last == 0)
        def _():
            produce(last, True, 0)
            consume(last, 0)

        @pl.when(last == 1)
        def _():
            produce(first, True, 0)
            produce(last, True, 1)
            consume(first, 0)
            consume(last, 1)

        @pl.when(last >= 2)
        def _():
            produce(first, True, 0)
            produce(first + 1, False, 1)
            consume(first, 0)
            produce(last, True, 0)
            consume(first + 1, 1)
            consume(last, 0)

        return
    first, n_plain, plain_masked = 0, q0 // tk, False
    pairs = jnp.maximum(n_plain - 1, 0) // 2
    if tq == 2 * tk:
        @pl.when(n_plain >= 1)
        def _():
            produce(0, False, 0)

        def two_plain(j, carry):
            produce(2 * j + 1, False, 1)
            consume(2 * j, 0)
            produce(2 * j + 2, False, 0)
            consume(2 * j + 1, 1)
            return carry

        lax.fori_loop(0, pairs, two_plain, 0)
        kb = 2 * pairs

        @pl.when(n_plain >= 2)
        def _():
            produce(kb + 1, False, 1)
            consume(kb, 0)
            produce(n_plain, True, 0)
            consume(kb + 1, 1)
            produce(n_plain + 1, True, 1)
            consume(n_plain, 0)
            consume(n_plain + 1, 1)

        @pl.when(n_plain == 0)
        def _():
            produce(0, True, 0)
            produce(1, True, 1)
            consume(0, 0)
            consume(1, 1)

        return
    assert tq == tk
    last = first + n_plain

    @pl.when(n_plain >= 1)
    def _():
        produce(first, plain_masked, 0)

    def two(j, carry):
        kb = first + 2 * j
        produce(kb + 1, plain_masked, 1)
        consume(kb, 0)
        produce(kb + 2, plain_masked, 0)
        consume(kb + 1, 1)
        return carry

    lax.fori_loop(0, pairs, two, 0)
    kb = first + 2 * pairs
    left = n_plain - 2 * pairs

    @pl.when(left == 2)
    def _():
        produce(kb + 1, plain_masked, 1)
        consume(kb, 0)
        produce(last, True, 0)
        consume(kb + 1, 1)
        consume(last, 0)

    @pl.when(left == 1)
    def _():
        produce(last, True, 1)
        consume(kb, 0)
        consume(last, 1)

    @pl.when(left == 0)
    def _():
        produce(last, True, 0)
        consume(last, 0)


def _attn_bias(mode, q0, k0, tq, tk):
    k = k0 + lax.broadcasted_iota(jnp.int32, (tk, tq), 0)
    t = q0 + lax.broadcasted_iota(jnp.int32, (tk, tq), 1)
    if mode == "cmp":
        ok = (k * 16 + 31) <= t
    elif mode == "win":
        ok = (k <= t) & ((t - k) < WINDOW)
    else:
        ok = k <= t
    bias = jnp.where(ok, 0.0, NEG)
    return jnp.concatenate([bias] * GRP, axis=1), jnp.concatenate([ok.astype(F32)] * GRP, axis=1)


def _sel_operands(qs, selneg_ref):
    return jnp.concatenate([qs, jnp.concatenate([selneg_ref[0]] * GRP, axis=0)], axis=1)


def _stack_heads(ref, tq):
    return jnp.concatenate([ref[:, g * HD:(g + 1) * HD] for g in range(GRP)], axis=0)


def _scaled_queries(q_ref, tq):
    return (_stack_heads(q_ref, tq).astype(F32) * SCALE).astype(_MXU)


def _blocked_t(x, tk):
    n, Sk, d = x.shape
    return x.reshape(n, Sk // tk, tk, d).transpose(0, 1, 3, 2)


def _head_rows(ref):
    return jnp.concatenate([ref[0, g:g + 1, :] for g in range(GRP)], axis=1)


def _attn_fwd(q, qcol0, k, vt, mode, selneg, gate, y_prev, y_dtype, name):
    S, Sk = q.shape[0], k.shape[1]
    tq, tk = _attn_cfg(S, Sk, mode)
    R = GRP * tq
    NB = S // SEL_BLOCK

    def body(*refs):
        q_ref, k_ref, vt_ref = refs[:3]
        rest = list(refs[3:])
        sel_ref = rest.pop(0) if mode == "sel" else None
        ov_ref = rest.pop(0) if mode == "cmp" else None
        gate_ref = rest.pop(0)
        yp_ref = rest.pop(0) if y_prev is not None else None
        o_ref, lse_ref, y_ref = rest[:3]
        choice_ref = rest[3] if mode == "cmp" else None
        m_scr, l_scr, acc, s_scr = rest[-4:]
        q0 = pl.program_id(1) * tq
        qs = _scaled_queries(q_ref, tq)
        m_scr[...] = jnp.full_like(m_scr, NEG)
        l_scr[...] = jnp.zeros_like(l_scr)
        acc[...] = jnp.zeros_like(acc)
        qk = _sel_operands(qs, sel_ref) if mode == "sel" else qs

        def produce(kb, masked, slot):
            k0 = _block_start(kb, tk)
            s = _dot(k_ref[0, pl.ds(k0, tk), :], qk, "nt")
            if masked:
                s = s + _attn_bias(mode, q0, k0, tq, tk)[0]
            s_scr[slot] = s

        def consume(kb, slot):
            s = s_scr[slot]
            m_old = m_scr[...]
            m_new = jnp.maximum(m_old, jnp.max(s, axis=0, keepdims=True))
            p = jnp.exp(s - m_new)
            if mode == "cmp":
                p = p * _attn_bias(mode, q0, 0, tq, tk)[1]
            alpha = jnp.exp(m_old - m_new)
            l_scr[...] = alpha * l_scr[...] + jnp.sum(p, axis=0, keepdims=True)
            acc[...] = alpha * acc[...] + _dot(vt_ref[0, kb], p, "nn")
            m_scr[...] = m_new

        _pipelined_key_blocks(mode, q0, tq, tk, produce, consume)
        l = l_scr[...]
        good = l > 0.0
        o_t = acc[...] * jnp.where(good, 1.0 / jnp.where(good, l, 1.0), 0.0)
        lse = jnp.where(good, m_scr[...] + jnp.log(jnp.where(good, l, 1.0)), -NEG)
        y_t = o_t * _sigmoid(_head_rows(gate_ref))
        if mode == "cmp":
            p = jnp.exp(s_scr[0] - lse) * _attn_bias(mode, q0, 0, tq, tk)[1]
            choice_ref[0] = _chosen_blocks(p, ov_ref[...], q0, tq).astype(choice_ref.dtype)
        for g in range(GRP):
            hs, qs_ = slice(g * HD, (g + 1) * HD), slice(g * tq, (g + 1) * tq)
            o_ref[:, hs] = o_t[:, qs_].T
            lse_ref[0, g:g + 1, :] = lse[:, qs_]
            yg = y_t[:, qs_].T
            if y_prev is not None:
                yg = yg + yp_ref[:, hs]
            y_ref[:, hs] = yg.astype(y_ref.dtype)

    row_spec = pl.BlockSpec((1, GRP, tq), lambda h, i: (h, 0, i))
    qo_spec = pl.BlockSpec((tq, GRP * HD), lambda h, i: (i, h))
    ins = [q, k, vt]
    specs = [pl.BlockSpec((tq, GRP * HD), lambda h, i: (i, qcol0 + h)), pl.BlockSpec((1, Sk, k.shape[2]), lambda h, i: (h, 0, 0)),
             pl.BlockSpec((1, Sk // tk, HD, tk), lambda h, i: (h, 0, 0, 0))]
    if mode == "sel":
        ins.append(selneg)
        specs.append(pl.BlockSpec((1, tq, selneg.shape[2]), lambda h, i: (h, i, 0)))
    out_specs = [qo_spec, row_spec, qo_spec]
    out_shape = [jax.ShapeDtypeStruct((S, ATT_WIDTH), F32), jax.ShapeDtypeStruct((N_KV, GRP, S), F32),
                 jax.ShapeDtypeStruct((S, ATT_WIDTH), y_dtype)]
    if mode == "cmp":
        ins.append(_block_overlap(Sk, NB))
        specs.append(pl.BlockSpec((NB, Sk), lambda h, i: (0, 0)))
        out_specs.append(pl.BlockSpec((1, tq, NB), lambda h, i: (h, i, 0)))
        out_shape.append(jax.ShapeDtypeStruct((N_KV, S, NB), _MXU))
    ins.append(gate)
    specs.append(row_spec)
    if y_prev is not None:
        ins.append(y_prev)
        specs.append(qo_spec)
    return pl.pallas_call(
        body, name=name, grid=(N_KV, S // tq), in_specs=specs, out_specs=out_specs, out_shape=out_shape,
        scratch_shapes=[pltpu.VMEM((1, R), F32), pltpu.VMEM((1, R), F32), pltpu.VMEM((HD, R), F32), pltpu.VMEM((2, tk, R), F32)],
        compiler_params=_cp(("parallel", "arbitrary")))(*ins)


def _attn_bwd(q, qcol0, k, kt, v, o, lse, dy, dycol0, gate, mode, selneg, name):
    S, Sk = q.shape[0], k.shape[1]
    tq, tk = _attn_cfg(S, Sk, mode)
    R = GRP * tq

    def body(*refs):
        if mode == "sel":
            (q_ref, k_ref, kt_ref, v_ref, o_ref, lse_ref, dy_ref, gate_ref, sel_ref, dq_ref, dk_ref, dv_ref, dg_ref, dq_scr, s_scr,
             dp_scr) = refs
        else:
            q_ref, k_ref, kt_ref, v_ref, o_ref, lse_ref, dy_ref, gate_ref, dq_ref, dk_ref, dv_ref, dg_ref, dq_scr, s_scr, dp_scr = refs

        @pl.when(pl.program_id(1) == 0)
        def _():
            dk_ref[...] = jnp.zeros_like(dk_ref)
            dv_ref[...] = jnp.zeros_like(dv_ref)

        q0 = pl.program_id(1) * tq
        qs = _scaled_queries(q_ref, tq)
        dys = _stack_heads(dy_ref, tq)
        gv = _sigmoid(_head_rows(gate_ref))
        dy_o = _dot(jnp.ones((8, HD), F32), dys * _stack_heads(o_ref, tq), "nt", split="b")[0:1, :]
        delta = gv * dy_o
        dgate = dy_o * (gv * (1.0 - gv))
        for g in range(GRP):
            dg_ref[0, g:g + 1, :] = dgate[:, g * tq:(g + 1) * tq]
        lsev = _head_rows(lse_ref)
        dos = (dys * jnp.broadcast_to(gv, (8, R)).T[:, 0:1]).astype(_MXU)
        dq_scr[...] = jnp.zeros_like(dq_scr)
        qk = _sel_operands(qs, sel_ref) if mode == "sel" else qs

        def produce(kb, masked, slot):
            k0 = _block_start(kb, tk)
            s = _dot(k_ref[0, pl.ds(k0, tk), :], qk, "nt")
            if masked:
                s = s + _attn_bias(mode, q0, k0, tq, tk)[0]
            s_scr[slot] = s
            dp_scr[slot] = _dot(v_ref[0, pl.ds(k0, tk), :], dos, "nt")

        def consume(kb, slot):
            k0 = _block_start(kb, tk)
            p = jnp.exp(s_scr[slot] - lsev)
            if mode == "cmp":
                p = p * _attn_bias(mode, q0, 0, tq, tk)[1]
            ds = p * (dp_scr[slot] - delta)
            dq_scr[...] += _dot(kt_ref[0, kb], ds, "nn")
            dk_ref[0, pl.ds(k0, tk), :] += _dot(ds, qs, "nn")
            dv_ref[0, pl.ds(k0, tk), :] += _dot(p, dos, "nn")

        _pipelined_key_blocks(mode, q0, tq, tk, produce, consume)
        for g in range(GRP):
            dq_ref[:, g * HD:(g + 1) * HD] = (dq_scr[:, g * tq:(g + 1) * tq] * SCALE).T

    kv_spec = pl.BlockSpec((1, Sk, HD), lambda h, i: (h, 0, 0))
    qo_spec = pl.BlockSpec((tq, GRP * HD), lambda h, i: (i, h))
    row_spec = pl.BlockSpec((1, GRP, tq), lambda h, i: (h, 0, i))
    ins = [q, k, kt, v, o, lse, dy, gate]
    specs = [pl.BlockSpec((tq, GRP * HD), lambda h, i: (i, qcol0 + h)), pl.BlockSpec((1, Sk, k.shape[2]), lambda h, i: (h, 0, 0)),
             pl.BlockSpec((1, Sk // tk, HD, tk), lambda h, i: (h, 0, 0, 0)), kv_spec, qo_spec, row_spec,
             pl.BlockSpec((tq, GRP * HD), lambda h, i: (i, dycol0 + h)), row_spec]
    if mode == "sel":
        ins.append(selneg)
        specs.append(pl.BlockSpec((1, tq, selneg.shape[2]), lambda h, i: (h, i, 0)))
    return pl.pallas_call(
        body, name=name, grid=(N_KV, S // tq), in_specs=specs, out_specs=[qo_spec, kv_spec, kv_spec, row_spec],
        out_shape=[jax.ShapeDtypeStruct((S, ATT_WIDTH), F32), jax.ShapeDtypeStruct((N_KV, Sk, HD), F32),
                   jax.ShapeDtypeStruct((N_KV, Sk, HD), F32), jax.ShapeDtypeStruct((N_KV, GRP, S), F32)],
        scratch_shapes=[pltpu.VMEM((HD, R), F32), pltpu.VMEM((2, tk, R), F32), pltpu.VMEM((2, tk, R), F32)],
        compiler_params=_cp(("parallel", "arbitrary")))(*ins)


def _block_overlap(NC, NB):
    ci = np.arange(NC)[None, :] * 16
    sj = np.arange(NB)[:, None] * SEL_BLOCK
    ov_t = np.clip(np.minimum(ci + 32, sj + SEL_BLOCK) - np.maximum(ci, sj), 0, None) / 32.0
    ov_t[:, NC - 1] = 0.0
    return jnp.asarray(ov_t, F32)


def _chosen_blocks(p, ov_t, q0, tq):
    NB = ov_t.shape[0]
    imp4 = _dot(ov_t, p, "nn")
    imp = imp4[:, 0:tq] + imp4[:, tq:2 * tq] + imp4[:, 2 * tq:3 * tq] + imp4[:, 3 * tq:4 * tq]
    blk = lax.broadcasted_iota(jnp.int32, (NB, tq), 0)
    cur = lax.shift_right_logical(q0 + lax.broadcasted_iota(jnp.int32, (NB, tq), 1), 6)
    imp = jnp.where((blk == 0) | (blk == cur) | (blk == cur - 1), FORCE, imp)
    imp = jnp.where(blk <= cur, imp, -1.0)
    rank = jnp.zeros((NB, tq), F32)
    for j in range(NB):
        row = imp[j:j + 1, :]
        ahead = (row > imp) | ((row == imp) & (blk > j))
        rank = rank + ahead.astype(F32)
    chosen = (rank < float(N_SELECT)) & (imp >= 0.0)
    return jnp.where(chosen, 0.0, NEG).T


def _to_rows16(x):
    S = x.shape[0]
    return x.reshape(S // 16, 16, N_KV, HD).transpose(2, 0, 1, 3).reshape(N_KV, S // 16, 16 * HD)


def _from_rows16(r):
    NC = r.shape[1]
    return r.reshape(N_KV, NC, 16, HD).transpose(1, 2, 0, 3).reshape(NC * 16, N_KV * HD)


DT_COL0 = SSD_WIDTH + CONV_CH
GATE_IN_COL0 = D_IN - 3 * N_HEADS


SHARD_IN = D_IN // N_DEV


def _orig_cols(ref, c0, width):
    pieces, c = [], c0
    while c < c0 + width:
        d, off = divmod(c, SHARD_IN)
        w = min(SHARD_IN - off, c0 + width - c)
        pieces.append(ref[d, :, off:off + w])
        c += w
    return pieces[0] if len(pieces) == 1 else jnp.concatenate(pieces, axis=1)


def _cols_from_slabs(slabs):
    _, R, c = slabs.shape
    tr = _pick(R, (256, 128))

    def body(s_ref, o_ref):
        for t in range(N_DEV * c // LANE):
            pieces, col = [], t * LANE
            while col < (t + 1) * LANE:
                d, off = divmod(col, c)
                w = min(c - off, (t + 1) * LANE - col)
                pieces.append(s_ref[d, :, off:off + w])
                col += w
            o_ref[:, t * LANE:(t + 1) * LANE] = pieces[0] if len(pieces) == 1 else jnp.concatenate(pieces, axis=1)

    return pl.pallas_call(
        body, name="cols_from_slabs", grid=(R // tr,), in_specs=[pl.BlockSpec((N_DEV, tr, c), lambda i: (0, i, 0))],
        out_specs=pl.BlockSpec((tr, N_DEV * c), lambda i: (i, 0)), out_shape=jax.ShapeDtypeStruct((R, N_DEV * c), slabs.dtype),
        compiler_params=_cp(("parallel",)))(slabs)


def _slabs_from_cols(x):
    R, c = x.shape[0], x.shape[1] // N_DEV
    tr = _pick(R, (256, 128))

    def body(x_ref, o_ref):
        for d in range(N_DEV):
            o_ref[d] = x_ref[:, d * c:(d + 1) * c]

    return pl.pallas_call(
        body, name="slabs_from_cols", grid=(R // tr,), in_specs=[pl.BlockSpec((tr, N_DEV * c), lambda i: (i, 0))],
        out_specs=pl.BlockSpec((N_DEV, tr, c), lambda i: (0, i, 0)), out_shape=jax.ShapeDtypeStruct((N_DEV, R, c), x.dtype),
        compiler_params=_cp(("parallel",)))(x)


def _w_in_from_slabs(slabs):
    D = slabs.shape[1]
    tr = _pick(D, (256, 128))

    def body(s_ref, main_ref, small_ref):
        for t in range(W_MAIN // LANE):
            c = t * LANE
            main_ref[:, c:c + LANE] = _orig_cols(s_ref, c if c < DT_COL0 else c + SSD_HEADS, LANE)
        small_ref[...] = jnp.concatenate(
            [_orig_cols(s_ref, DT_COL0, SSD_HEADS), _orig_cols(s_ref, GATE_IN_COL0, 3 * N_HEADS),
             jnp.zeros((tr, W_SMALL - SSD_HEADS - 3 * N_HEADS), small_ref.dtype)], axis=1)

    return pl.pallas_call(
        body, name="w_in_layout", grid=(D // tr,), in_specs=[pl.BlockSpec((N_DEV, tr, SHARD_IN), lambda i: (0, i, 0))],
        out_specs=[pl.BlockSpec((tr, W_MAIN), lambda i: (i, 0)), pl.BlockSpec((tr, W_SMALL), lambda i: (i, 0))],
        out_shape=[jax.ShapeDtypeStruct((D, W_MAIN), slabs.dtype), jax.ShapeDtypeStruct((D, W_SMALL), slabs.dtype)],
        compiler_params=_cp(("parallel",)))(slabs)


def _w_in_to_slabs(main, small):
    D = main.shape[0]
    tr = _pick(D, (256, 128))
    ranges = [(0, DT_COL0, 0, 0), (DT_COL0, DT_COL0 + SSD_HEADS, 1, 0), (DT_COL0 + SSD_HEADS, GATE_IN_COL0, 0, DT_COL0),
              (GATE_IN_COL0, D_IN, 1, SSD_HEADS)]

    def body(main_ref, small_ref, o_ref):
        srcs = (main_ref, small_ref)
        for d in range(N_DEV):
            lo, hi = d * SHARD_IN, (d + 1) * SHARD_IN
            pieces = []
            for start, stop, which, s0 in ranges:
                a, b = max(lo, start), min(hi, stop)
                if a < b:
                    pieces.append(srcs[which][:, s0 + a - start:s0 + b - start].astype(o_ref.dtype))
            o_ref[d] = pieces[0] if len(pieces) == 1 else jnp.concatenate(pieces, axis=1)

    return pl.pallas_call(
        body, name="w_in_grad_layout", grid=(D // tr,),
        in_specs=[pl.BlockSpec((tr, W_MAIN), lambda i: (i, 0)), pl.BlockSpec((tr, W_SMALL), lambda i: (i, 0))],
        out_specs=pl.BlockSpec((N_DEV, tr, SHARD_IN), lambda i: (0, i, 0)),
        out_shape=jax.ShapeDtypeStruct((N_DEV, D, SHARD_IN), main.dtype), compiler_params=_cp(("parallel",)))(main, small)


QB, KCB, VCB, KSB, VSB, KWB, VWB = 10, 14, 15, 16, 17, 18, 19


def _col256(a, b):
    return a[:, b * 256:(b + 1) * 256]


_EARLY = ["w_in", "cmp_w1_k", "cmp_w1_v"]
_LATE = ["w_out", "w_gate", "w_up", "w_down"]
_FFN = ["w_down", "w_gate", "w_up"]
_MID = ["w_out"]
_LAST = ["cmp_w1_k", "cmp_w1_v", "w_in"]


def _local_step(x, tgt, p, late_weights=None, grads_ready=None):
    S = x.shape[0]
    cos, sin = _rope_tables(S)

    u, rs1 = _rms_fwd(x, p["attn_norm_w"], "attn_norm")
    proj = _mm(u, p["w_main"], "nn", F32, "in_proj", after=p.get("before_in_proj"))
    proj_small = _mm(u, p["w_small"], "nn", F32, "in_proj_small")
    xa = _conv_fwd(proj, p["conv_w"], p["conv_b"])
    y_ssd, y_pre, rs_ssd, hs = _ssd_fwd(proj, proj_small, xa, p["dt_bias"], p["a_log"], p["d_skip"], p["ssd_norm_w"])

    q_rot = _rope([proj], QB, ATT_WIDTH, cos, sin, 1.0, _MXU, "rope_q")
    kv = _kv_prep(proj, cos, sin, _attn_cfg(S, S, "sel")[1])
    rk, rv = _to_rows16(_col256(proj, KCB)), _to_rows16(_col256(proj, VCB))
    k_cmp, hid_k = _compress_fwd(rk, p["cmp_pe_k"], p["cmp_w1_k"], p["cmp_w2_k"])
    v_cmp, hid_v = _compress_fwd(rv, p["cmp_pe_v"], p["cmp_w1_v"], p["cmp_w2_v"])
    n_cmp = k_cmp.shape[1]

    gates = proj_small[:, SSD_HEADS:SSD_HEADS + 3 * N_HEADS].reshape(S, N_KV, GRP, 3).transpose(3, 1, 2, 0)
    o_cmp, lse_cmp, y_att, sel = _attn_fwd(proj, QB, k_cmp, _blocked_t(v_cmp, n_cmp), "cmp", None, gates[0], None, F32,
                                           "attn_cmp_fwd")
    o_sel, lse_sel, y_att = _attn_fwd(q_rot, 0, kv["ks_ext"], kv["vs_t"], "sel", sel, gates[1], y_att, F32, "attn_sel_fwd")
    o_win, lse_win, y_att = _attn_fwd(q_rot, 0, kv["kw"], kv["vw_t"], "win", None, gates[2], y_att, _MXU, "attn_win_fwd")

    if late_weights is not None:
        p = {**p, **late_weights(y_att)}
    mixed = jnp.concatenate([y_ssd, y_att], axis=1)
    h1 = _mm(mixed, p["w_out"], "nn", F32, "out_proj", res=x)
    v, rs_ffn = _rms_fwd(h1, p["ffn_norm_w"], "ffn_norm")
    gt, up, act = _ffn_up(v, p["w_gate"], p["w_up"])
    h2 = _mm(act, p["w_down"], "nn", F32, "ffn_down", res=h1)
    loss, dh2, dh2b, d_final_w = _final_loss(h2, p["final_norm_w"], tgt)

    def ready(names):
        return None if grads_ready is None else grads_ready(names, g)

    g = {"final_norm_w": d_final_w}
    g["w_down"] = _mm(act, dh2b, "tn", _MXU, "dw_down")
    dgt, dup = _ffn_dact(dh2b, p["w_down"], gt, up)
    g["w_gate"] = _mm(v, dgt, "tn", _MXU, "dw_gate")
    g["w_up"] = _mm(v, dup, "tn", _MXU, "dw_up")
    dv = _ffn_dv(dgt, dup, p["w_gate"], p["w_up"], ready(_FFN))
    dh1, dh1b, g["ffn_norm_w"] = _rms_bwd(dv, h1, rs_ffn, p["ffn_norm_w"], dh2, "ffn_norm_bwd")
    g["w_out"] = _mm(mixed, dh1b, "tn", _MXU, "dw_out")
    dmixed = _mm(dh1b, p["w_out"], "nt", F32, "dmixed", after=ready(_MID))

    dz, dxa, ddtr, g["dt_bias"], g["a_log"], g["d_skip"], g["ssd_norm_w"] = _ssd_bwd(
        dmixed, proj, proj_small, xa, y_pre, rs_ssd, hs, p["dt_bias"], p["a_log"], p["d_skip"], p["ssd_norm_w"])
    dxbc, g["conv_w"], g["conv_b"] = _conv_bwd(proj, p["conv_w"], p["conv_b"], dxa)

    dyb = SSD_WIDTH // (GRP * HD)
    dq_cmp, dk_cmp, dv_cmp, dg_cmp = _attn_bwd(proj, QB, k_cmp, _blocked_t(k_cmp, n_cmp), v_cmp, o_cmp, lse_cmp, dmixed, dyb,
                                               gates[0], "cmp", None, "attn_cmp_bwd")
    dq_sel, dks, dvs, dg_sel = _attn_bwd(q_rot, 0, kv["ks_ext"], kv["ks_t"], kv["vs"], o_sel, lse_sel, dmixed, dyb, gates[1], "sel",
                                         sel, "attn_sel_bwd")
    dq_win, dkw, dvw, dg_win = _attn_bwd(q_rot, 0, kv["kw"], kv["kw_t"], kv["vw"], o_win, lse_win, dmixed, dyb, gates[2], "win", None,
                                         "attn_win_bwd")
    dgate = jnp.stack([dg_cmp, dg_sel, dg_win]).transpose(3, 1, 2, 0).reshape(S, 3 * N_HEADS)
    drk, g["cmp_w1_k"], g["cmp_w2_k"], g["cmp_pe_k"] = _compress_bwd(rk, p["cmp_pe_k"], p["cmp_w1_k"], p["cmp_w2_k"], hid_k, dk_cmp)
    drv, g["cmp_w1_v"], g["cmp_w2_v"], g["cmp_pe_v"] = _compress_bwd(rv, p["cmp_pe_v"], p["cmp_w1_v"], p["cmp_w2_v"], hid_v, dv_cmp)
    dq = _rope([dq_sel, dq_win], 0, ATT_WIDTH, cos, sin, -1.0, _MXU, "rope_dq", extra=(dq_cmp, 0))
    dkv = _dkv_post(dks, dvs, dkw, dvw, cos, sin)
    dproj = jnp.concatenate([dz, dxbc, dq] + [t.astype(_MXU) for t in (_from_rows16(drk), _from_rows16(drv))] + [dkv], axis=1)
    dsmall = jnp.concatenate([ddtr, dgate, jnp.zeros((S, W_SMALL - SSD_HEADS - 3 * N_HEADS), F32)], axis=1).astype(_MXU)
    g["w_main"] = _mm(u, dproj, "tn", _MXU, "dw_in")
    g["w_small"] = _mm(u, dsmall, "tn", F32, "dw_in_small")
    du = _mm(dproj, p["w_main"], "nt", F32, "du_main", after=ready(_LAST))
    du = _mm(dsmall, p["w_small"], "nt", F32, "du_small", res=du)
    grad_x, _, g["attn_norm_w"] = _rms_bwd(du, x, rs1, p["attn_norm_w"], dh1, "attn_norm_bwd")
    return loss, grad_x, g


MESH_ID = pl.DeviceIdType.MESH


def _my_coords():
    return lax.axis_index("x"), lax.axis_index("y"), lax.axis_index("c")


def _flat_id(px, py, pc):
    return 4 * px + 2 * py + pc


def _peer(k):
    mx, my, mc = _my_coords()
    return (1 - mx if k & 4 else mx, 1 - my if k & 2 else my, 1 - mc if k & 1 else mc)


def _exchange(arrs, scatter, name, after=()):
    n, na = len(arrs), len(after)
    scatter = [scatter] * n if isinstance(scatter, bool) else list(scatter)

    def body(*refs):
        ins, outs = refs[:n], refs[n + na:2 * n + na]
        send_sems, recv_sems, local_sems = refs[2 * n + na:]
        me = _flat_id(*_my_coords())
        copies = []
        for i in range(n):
            src_me = ins[i].at[me] if scatter[i] else ins[i]
            local = pltpu.make_async_copy(src_me, outs[i].at[me], local_sems.at[i])
            local.start()
            copies.append(local)
        for k in range(1, N_DEV):
            peer = _peer(k)
            for i in range(n):
                src = ins[i].at[_flat_id(*peer)] if scatter[i] else ins[i]
                cp = pltpu.make_async_remote_copy(src_ref=src, dst_ref=outs[i].at[me], send_sem=send_sems.at[i * 7 + k - 1],
                                                  recv_sem=recv_sems.at[i * 7 + k - 1], device_id=peer, device_id_type=MESH_ID)
                cp.start()
                copies.append(cp)
        for cp in copies:
            cp.wait()

    any_spec = pl.BlockSpec(memory_space=pl.ANY)
    out_shape = [jax.ShapeDtypeStruct(a.shape if sc else (N_DEV,) + a.shape, a.dtype) for a, sc in zip(arrs, scatter)]
    return pl.pallas_call(
        body, name=name, in_specs=[any_spec] * (n + na), out_specs=[any_spec] * n, out_shape=out_shape,
        scratch_shapes=[pltpu.SemaphoreType.DMA((n * 7,)), pltpu.SemaphoreType.DMA((n * 7,)), pltpu.SemaphoreType.DMA((n,))],
        compiler_params=pltpu.CompilerParams(has_side_effects=True))(*arrs, *after)


def _gather_two_level(arrs, name):
    n = len(arrs)

    def body(*refs):
        ins, outs = refs[:n], refs[n:2 * n]
        send_sems, recv_sems, local_sems = refs[2 * n:]
        x, y, c = _my_coords()
        me, sibling = (x, y, c), (x, y, 1 - c)
        chips = [(1 - x, y), (x, 1 - y), (1 - x, 1 - y)]

        def copy(i, k, block, to, src=None):
            slot = outs[i].at[_flat_id(*block)]
            return pltpu.make_async_remote_copy(src_ref=slot if src is None else src, dst_ref=slot, send_sem=send_sems.at[i * 7 + k],
                                                recv_sem=recv_sems.at[i * 7 + k], device_id=to, device_id_type=MESH_ID)

        mine = [pltpu.make_async_copy(ins[i], outs[i].at[_flat_id(*me)], local_sems.at[i]) for i in range(n)]
        for cp in mine:
            cp.start()
        first = []
        for j, chip in enumerate(chips):
            first += [copy(i, 1 + j, me, (*chip, c), src=ins[i]) for i in range(n)]
        first += [copy(i, 0, me, sibling, src=ins[i]) for i in range(n)]
        for cp in first:
            cp.start()
        passed = []
        for j, chip in enumerate(chips):
            for i in range(n):
                copy(i, 1 + j, (*chip, c), me).wait_recv()
                passed.append(copy(i, 4 + j, (*chip, c), sibling))
                passed[-1].start()
        for i in range(n):
            copy(i, 0, sibling, me).wait_recv()
        for j, chip in enumerate(chips):
            for i in range(n):
                copy(i, 4 + j, (*chip, 1 - c), me).wait_recv()
        for cp in first + passed:
            cp.wait_send()
        for cp in mine:
            cp.wait()

    any_spec = pl.BlockSpec(memory_space=pl.ANY)
    return pl.pallas_call(
        body, name=name, in_specs=[any_spec] * n, out_specs=[any_spec] * n,
        out_shape=[jax.ShapeDtypeStruct((N_DEV,) + a.shape, a.dtype) for a in arrs],
        scratch_shapes=[pltpu.SemaphoreType.DMA((n * 7,)), pltpu.SemaphoreType.DMA((n * 7,)), pltpu.SemaphoreType.DMA((n,))],
        compiler_params=pltpu.CompilerParams(has_side_effects=True))(*arrs)


_HBM = pl.BlockSpec(memory_space=pltpu.HBM)
_SEM = pl.BlockSpec(memory_space=pltpu.SEMAPHORE)
_EFFECT = pltpu.SideEffectType.DATAFLOW_SIDE_EFFECTING


def _split_copies(ins, lands, send_sems, recv_sems, own_sems, scatter):
    me = _flat_id(*_my_coords())
    remote = []
    for k in range(1, N_DEV):
        peer = _peer(k)
        for i in range(len(ins)):
            src = ins[i].at[_flat_id(*peer)] if scatter else ins[i]
            remote.append(pltpu.make_async_remote_copy(src_ref=src, dst_ref=lands[i].at[me], send_sem=send_sems.at[i * 7 + k - 1],
                                                       recv_sem=recv_sems.at[i * 7 + k - 1], device_id=peer, device_id_type=MESH_ID))
    own = [pltpu.make_async_copy(ins[i].at[me] if scatter else ins[i], lands[i].at[me], own_sems.at[i]) for i in range(len(ins))]
    return remote, own


def _split_start(arrs, scatter, name, after=()):
    n, na = len(arrs), len(after)

    def body(*refs):
        remote, own = _split_copies(refs[:n], refs[n:2 * n], refs[2 * n + na], refs[2 * n + na + 1], refs[2 * n + na + 2], scatter)
        for cp in remote + own:
            cp.start()
        refs[-1][...] = jnp.zeros_like(refs[-1])

    land_shapes = [a.shape if scatter else (N_DEV,) + a.shape for a in arrs]
    out_shape = ((pltpu.SemaphoreType.DMA((n * 7,)), pltpu.SemaphoreType.DMA((n * 7,)), pltpu.SemaphoreType.DMA((n,)))
                 + tuple(pltpu.HBM(a.shape, a.dtype) for a in arrs) + tuple(pltpu.HBM(s, a.dtype) for s, a in zip(land_shapes, arrs))
                 + (jax.ShapeDtypeStruct((8, 128), F32),))
    operands = ([pltpu.with_memory_space_constraint(a, pltpu.HBM) for a in arrs]
                + [pltpu.with_memory_space_constraint(lax.empty(s, a.dtype), pltpu.HBM) for s, a in zip(land_shapes, arrs)])
    res = pl.pallas_call(
        body, name=name, out_shape=out_shape, in_specs=[_HBM] * (2 * n) + [pl.BlockSpec(memory_space=pl.ANY)] * na,
        out_specs=(_SEM, _SEM, _SEM) + (_HBM,) * (2 * n) + (pl.BlockSpec(memory_space=pltpu.VMEM),),
        input_output_aliases={i: 3 + i for i in range(2 * n)},
        compiler_params=pltpu.CompilerParams(has_side_effects=_EFFECT))(*operands, *after)
    return dict(send=res[0], recv=res[1], own=res[2], ins=list(res[3:3 + n]), lands=list(res[3 + n:3 + 2 * n]), token=res[-1])


def _split_wait(st, scatter, after, name):
    n = len(st["ins"])

    def body(*refs):
        remote, own = _split_copies(refs[:n], refs[n:2 * n], refs[2 * n], refs[2 * n + 1], refs[2 * n + 2], scatter)
        for cp in remote:
            cp.wait_send()
            cp.wait_recv()
        for cp in own:
            cp.wait()

    arrs = st["ins"] + st["lands"]
    res = pl.pallas_call(
        body, name=name, out_shape=tuple(pltpu.HBM(a.shape, a.dtype) for a in arrs),
        in_specs=[_HBM] * (2 * n) + [_SEM, _SEM, _SEM] + [pl.BlockSpec(memory_space=pl.ANY)] * len(after), out_specs=(_HBM,) * (2 * n),
        input_output_aliases={i: i for i in range(2 * n)},
        compiler_params=pltpu.CompilerParams(has_side_effects=_EFFECT))(*arrs, st["send"], st["recv"], st["own"], *after)
    return list(res[n:])


def _adam_step(p_ref, w_ref, m_ref, v_ref, g_ref, d_ref, nm_ref, nv_ref):
    g = p_ref[0].astype(F32)
    for j in range(1, p_ref.shape[0]):
        g = g + p_ref[j].astype(F32)
    g_ref[...] = g
    nm = ADAM_B1 * m_ref[...] + (1.0 - ADAM_B1) * g
    nv = ADAM_B2 * v_ref[...] + (1.0 - ADAM_B2) * (g * g)
    nm_ref[...] = nm
    nv_ref[...] = nv
    m_hat = nm / (1.0 - ADAM_B1 ** ADAM_STEP)
    v_hat = nv / (1.0 - ADAM_B2 ** ADAM_STEP)
    d_ref[...] = -ADAM_LR * (m_hat / (jnp.sqrt(v_hat) + ADAM_EPS) + ADAM_WD * w_ref[...])


def _adam_sum(parts, w, m, v, name):
    P, R, C = parts.shape
    tr = _pick(R, (256, 128, 64, 32, 8)) if C <= 1024 else _pick(R, (128, 64, 32, 8))
    blk = pl.BlockSpec((tr, C), lambda i: (i, 0))
    return pl.pallas_call(
        functools.partial(_adam_step), name=name, grid=(R // tr,),
        in_specs=[pl.BlockSpec((P, tr, C), lambda i: (0, i, 0)), blk, blk, blk],
        out_specs=[blk] * 4, out_shape=[jax.ShapeDtypeStruct((R, C), F32)] * 4, compiler_params=_cp(("parallel",)))(parts, w, m, v)


def _adam_small(loss_parts, parts, ws, ms, vs):
    n = len(parts)

    def body(*refs):
        loss_ref, ins, outs, total_ref = refs[0], refs[1:4 * n + 1], refs[4 * n + 1:-1], refs[-1]
        for i in range(n):
            _adam_step(ins[i], ins[n + i], ins[2 * n + i], ins[3 * n + i], *outs[4 * i:4 * i + 4])
        total = loss_ref[0]
        for d in range(1, N_DEV):
            total = total + loss_ref[d]
        total_ref[...] = total

    out_shape = [jax.ShapeDtypeStruct(w.shape, F32) for w in ws for _ in range(4)] + [jax.ShapeDtypeStruct(loss_parts.shape[1:], F32)]
    res = pl.pallas_call(body, name="adam_small", out_shape=out_shape)(loss_parts, *parts, *ws, *ms, *vs)
    return res[-1], [tuple(res[4 * i:4 * i + 4]) for i in range(n)]


_WEIGHTS = ["attn_norm_w", "w_in", "conv_w", "conv_b", "dt_bias", "a_log", "d_skip", "ssd_norm_w", "cmp_w1_k", "cmp_w2_k",
            "cmp_w1_v", "cmp_w2_v", "cmp_pe_k", "cmp_pe_v", "w_out", "ffn_norm_w", "w_gate", "w_up", "w_down", "final_norm_w"]
_BIG = ["w_in", "w_gate", "w_up", "w_down", "w_out", "cmp_w1_k", "cmp_w1_v"]
_COL_SHARDED = ("w_in", "w_gate", "w_up")
_REPLICATED = ["attn_norm_w", "conv_b", "dt_bias", "a_log", "d_skip", "ssd_norm_w", "cmp_pe_k", "cmp_pe_v", "ffn_norm_w",
               "final_norm_w"]
_SMALL_SHARDED = ["conv_w", "cmp_w2_k", "cmp_w2_v"]


def _cols_to_slabs(g):
    R = g.shape[0]
    return g.reshape(R, N_DEV, -1).transpose(1, 0, 2)


def _slabs_to_cols(s):
    return s.transpose(1, 0, 2).reshape(s.shape[1], -1)


def kernel(x, attn_norm_w, w_in, conv_w, conv_b, dt_bias, a_log, d_skip, ssd_norm_w, cmp_w1_k, cmp_w2_k, cmp_w1_v, cmp_w2_v, cmp_pe_k, cmp_pe_v, w_out, ffn_norm_w, w_gate, w_up, w_down, final_norm_w, loss_target, m_attn_norm_w, m_w_in, m_conv_w, m_conv_b, m_dt_bias, m_a_log, m_d_skip, m_ssd_norm_w, m_cmp_w1_k, m_cmp_w2_k, m_cmp_w1_v, m_cmp_w2_v, m_cmp_pe_k, m_cmp_pe_v, m_w_out, m_ffn_norm_w, m_w_gate, m_w_up, m_w_down, m_final_norm_w, v_attn_norm_w, v_w_in, v_conv_w, v_conv_b, v_dt_bias, v_a_log, v_d_skip, v_ssd_norm_w, v_cmp_w1_k, v_cmp_w2_k, v_cmp_w1_v, v_cmp_w2_v, v_cmp_pe_k, v_cmp_pe_v, v_w_out, v_ffn_norm_w, v_w_gate, v_w_up, v_w_down, v_final_norm_w):
    a = dict(locals())

    shard = {n: a[n][0].astype(_MXU) for n in _BIG}
    got = _gather_two_level([shard[n] for n in _EARLY] + [cmp_w2_k[0], cmp_w2_v[0], conv_w[0]], "gather_early")
    st_late = _split_start([shard[n] for n in _LATE], False, "gather_late_start", after=(got[0],))

    def assemble(n, t):
        return _cols_from_slabs(t) if n in _COL_SHARDED else t.reshape(-1, t.shape[-1])

    p = dict(attn_norm_w=attn_norm_w, conv_b=conv_b, dt_bias=dt_bias, a_log=a_log, d_skip=d_skip, ssd_norm_w=ssd_norm_w,
             cmp_pe_k=cmp_pe_k.reshape(1, -1), cmp_pe_v=cmp_pe_v.reshape(1, -1), ffn_norm_w=ffn_norm_w,
             final_norm_w=final_norm_w.reshape(1, -1))

    w_main, w_small = _w_in_from_slabs(got[0])
    p.update(before_in_proj=st_late["token"],
             w_main=w_main, w_small=w_small, cmp_w1_k=assemble("cmp_w1_k", got[1]), cmp_w1_v=assemble("cmp_w1_v", got[2]),
             cmp_w2_k=assemble("cmp_w2_k", got[3]).astype(_MXU), cmp_w2_v=assemble("cmp_w2_v", got[4]).astype(_MXU),
             conv_w=_slabs_to_cols(got[5]))

    def late_weights(after):
        got_late = _split_wait(st_late, False, (after,), "gather_late_wait")
        return {n: assemble(n, t) for n, t in zip(_LATE, got_late)}

    def slabs_of(g, n):
        if n == "w_in":
            return _w_in_to_slabs(g["w_main"], g["w_small"])
        return _slabs_from_cols(g[n]) if n in _COL_SHARDED else g[n].reshape(N_DEV, -1, g[n].shape[-1])

    started = []

    def grads_ready(names, g):
        started.append((names, _split_start([slabs_of(g, n) for n in names], True, "scatter_grads_start_%d" % len(started))))
        return started[-1][1]["token"]

    loss_part, grad_x, g = _local_step(x[0], loss_target[0], p, late_weights, grads_ready)

    out, after = {}, (started[-1][1]["token"],)
    for i, (names, st) in enumerate(started):
        if i == len(started) - 1:
            after = after + (grad_x,)
        received = _split_wait(st, True, after, "scatter_grads_wait_%d" % i)
        for n, parts in zip(names, received):
            out[n] = _adam_sum(parts, a[n][0], a["m_" + n][0], a["v_" + n][0], "adam_" + n)
        after = (out[names[-1]][0],)

    small_names = _REPLICATED + _SMALL_SHARDED
    partials = [g[n] for n in _REPLICATED] + [_cols_to_slabs(g["conv_w"])] + [
        g[n].reshape(N_DEV, -1, g[n].shape[-1]) for n in ("cmp_w2_k", "cmp_w2_v")]
    gathered = _exchange([loss_part] + partials, [False] * (1 + len(_REPLICATED)) + [True] * len(_SMALL_SHARDED),
                         "exchange_small_grads", after=(received[0],))
    shapes2d = [t.shape[1:] for t in gathered[1:]]
    loss, res_small = _adam_small(gathered[0], gathered[1:],
                                  *[[a[pre + n].reshape(s) for n, s in zip(small_names, shapes2d)] for pre in ("", "m_", "v_")])
    for n, r in zip(small_names, res_small):
        out[n] = r

    outs = [loss[0, 0], grad_x[None]]
    for j in range(4):
        for n in _WEIGHTS:
            outs.append(out[n][j].reshape(a[n].shape))
    return tuple(outs)
```

```python
import functools

import numpy as np
import jax
import jax.numpy as jnp
from jax import lax
from jax.experimental import pallas as pl
from jax.experimental.pallas import tpu as pltpu

F32 = jnp.float32
_MXU = jnp.bfloat16

N_DEV = 8
SSD_WIDTH = 1024
ATT_WIDTH = 1024
SSD_HEADS = 16
SSD_P = 64
SSD_N = 128
SSD_L = 128
SSD_G = 2
CONV_CH = 1536
CONV_K = 4
HD = 64
N_HEADS = 16
N_KV = 4
GRP = 4
CMP_HID = 256
SEL_BLOCK = 64
N_SELECT = 16
WINDOW = 512
ROPE_DIM = 16
ROPE_THETA = 500000.0
EPS = 1e-6
NEG = -1e30
FORCE = 1e4
SCALE = HD ** -0.5
D_IN = 5184
W_MAIN = 5120
W_SMALL = 128
VMEM_LIMIT = 52 * 1024 * 1024

ADAM_LR, ADAM_B1, ADAM_B2, ADAM_EPS, ADAM_WD, ADAM_STEP = 0.001, 0.9, 0.999, 1e-08, 0.01, 10


def _pick(n, cands):
    for c in cands:
        if n % c == 0:
            return c
    return n


def _cp(sem=None):
    return pltpu.CompilerParams(dimension_semantics=sem, vmem_limit_bytes=VMEM_LIMIT)


def _sigmoid(x):
    return 1.0 / (1.0 + jnp.exp(-x))


def _dot(a, b, dims, split=None):
    dn = {"nn": (((1,), (0,)), ((), ())), "nt": (((1,), (1,)), ((), ())), "tn": (((0,), (0,)), ((), ()))}[dims]
    mm = lambda x, y: lax.dot_general(x.astype(_MXU), y.astype(_MXU), dn, preferred_element_type=F32)
    if split is None:
        return mm(a, b)
    x = (a if split == "a" else b).astype(F32)
    hi = x.astype(_MXU)
    lo = x - hi.astype(F32)
    return mm(hi, b) + mm(lo, b) if split == "a" else mm(a, hi) + mm(a, lo)


LANE = 128
MM_TILE = 1024
MM_K_WHOLE = 2048
MM_K_STEP = 2816
TN_ACC_ELEMS = 3 * 2 ** 20
TN_K_STEP = 1024
MM_TALL_BYTES = 44 * 1024 * 1024


def _largest_tile(n, cap):
    if n <= cap:
        return n
    best = LANE
    for t in range(LANE, cap + 1, LANE):
        if n % t == 0:
            best = t
    return best


def _mm_tiles(mode, M, N, K):
    if mode == "tn":
        tm = _largest_tile(M, 2 * MM_TILE)
        return tm, _largest_tile(N, TN_ACC_ELEMS // tm), _largest_tile(K, TN_K_STEP)
    tk = K if K <= MM_K_WHOLE else _largest_tile(K, MM_K_STEP)
    return _largest_tile(M, MM_TILE), _largest_tile(N, MM_TILE), tk


def _mm(a, b, mode, out_dtype, name, res=None, after=None):
    if mode == "nn":
        (M, K), N = a.shape, b.shape[1]
    elif mode == "nt":
        (M, K), N = a.shape, b.shape[0]
    else:
        (K, M), N = a.shape, b.shape[1]
    tm, tn, tk = _mm_tiles(mode, M, N, K)
    nk = K // tk
    if mode != "tn" and res is None and nk == 1 and M % (2 * tm) == 0:
        tile_bytes = 2 * tm * tk * a.dtype.itemsize + tn * tk * b.dtype.itemsize + 2 * tm * tn * jnp.dtype(out_dtype).itemsize
        if 2 * tile_bytes <= MM_TALL_BYTES:
            tm *= 2
    a_spec = pl.BlockSpec((tk, tm), lambda i, j, k: (k, i)) if mode == "tn" else pl.BlockSpec((tm, tk), lambda i, j, k: (i, k))
    b_spec = pl.BlockSpec((tn, tk), lambda i, j, k: (j, k)) if mode == "nt" else pl.BlockSpec((tk, tn), lambda i, j, k: (k, j))
    o_spec = pl.BlockSpec((tm, tn), lambda i, j, k: (i, j))

    def finish(r, r_ref, o_ref):
        if res is not None:
            r = r + r_ref[...].astype(F32)
        o_ref[...] = r.astype(out_dtype)

    def body_one_step(*refs):
        a_ref, b_ref, o_ref = refs[0], refs[1], refs[-1]
        finish(_dot(a_ref[...], b_ref[...], mode), refs[2], o_ref)

    def body(*refs):
        a_ref, b_ref, o_ref, acc = refs[0], refs[1], refs[-2], refs[-1]
        k = pl.program_id(2)

        @pl.when(k == 0)
        def _():
            acc[...] = jnp.zeros_like(acc)

        acc[...] += _dot(a_ref[...], b_ref[...], mode)

        @pl.when(k == nk - 1)
        def _():
            finish(acc[...], refs[2], o_ref)

    ins, specs = [a, b], [a_spec, b_spec]
    if res is not None:
        ins.append(res)
        specs.append(o_spec)
    if after is not None:
        ins.append(after)
        specs.append(pl.BlockSpec(memory_space=pl.ANY))
    return pl.pallas_call(
        body_one_step if nk == 1 else body, name=name, grid=(M // tm, N // tn, nk), in_specs=specs, out_specs=o_spec,
        out_shape=jax.ShapeDtypeStruct((M, N), out_dtype), scratch_shapes=[] if nk == 1 else [pltpu.VMEM((tm, tn), F32)],
        compiler_params=_cp(("parallel", "parallel", "arbitrary")))(*ins)


def _ffn_up(v, w_gate, w_up):
    S, D = v.shape
    F = w_gate.shape[1]
    tm, tn = _largest_tile(S, MM_TILE), _largest_tile(F, MM_TILE // 2)

    def body(v_ref, wg_ref, wu_ref, gt_ref, up_ref, act_ref):
        vv = v_ref[...]
        g = _dot(vv, wg_ref[...], "nn")
        u = _dot(vv, wu_ref[...], "nn")
        gt_ref[...] = g.astype(gt_ref.dtype)
        up_ref[...] = u.astype(up_ref.dtype)
        act_ref[...] = (g * _sigmoid(g) * u).astype(act_ref.dtype)

    o_spec = pl.BlockSpec((tm, tn), lambda i, j: (i, j))
    w_spec = pl.BlockSpec((D, tn), lambda i, j: (0, j))
    return pl.pallas_call(
        body, name="ffn_up", grid=(S // tm, F // tn),
        in_specs=[pl.BlockSpec((tm, D), lambda i, j: (i, 0)), w_spec, w_spec], out_specs=[o_spec, o_spec, o_spec],
        out_shape=[jax.ShapeDtypeStruct((S, F), _MXU)] * 3,
        compiler_params=_cp(("parallel", "parallel")))(v, w_gate, w_up)


def _ffn_dv(dgt, dup, w_gate, w_up, after):
    S, F = dgt.shape
    D = w_gate.shape[0]
    tm, tn, _ = _mm_tiles("nt", S, D, F)
    tk = _largest_tile(F, MM_K_STEP // 2)
    nk = F // tk

    def body(g_ref, u_ref, wg_ref, wu_ref, *rest):
        o_ref, acc = rest[-2], rest[-1]
        k = pl.program_id(2)

        @pl.when(k == 0)
        def _():
            acc[...] = jnp.zeros_like(acc)

        acc[...] += _dot(g_ref[...], wg_ref[...], "nt") + _dot(u_ref[...], wu_ref[...], "nt")

        @pl.when(k == nk - 1)
        def _():
            o_ref[...] = acc[...]

    a_spec = pl.BlockSpec((tm, tk), lambda i, j, k: (i, k))
    w_spec = pl.BlockSpec((tn, tk), lambda i, j, k: (j, k))
    ins, specs = [dgt, dup, w_gate, w_up], [a_spec, a_spec, w_spec, w_spec]
    if after is not None:
        ins.append(after)
        specs.append(pl.BlockSpec(memory_space=pl.ANY))
    return pl.pallas_call(
        body, name="ffn_dv", grid=(S // tm, D // tn, nk), in_specs=specs, out_specs=pl.BlockSpec((tm, tn), lambda i, j, k: (i, j)),
        out_shape=jax.ShapeDtypeStruct((S, D), F32), scratch_shapes=[pltpu.VMEM((tm, tn), F32)],
        compiler_params=_cp(("parallel", "parallel", "arbitrary")))(*ins)


def _ffn_dact(dh2, w_down, gt, up):
    S, D = dh2.shape
    F = w_down.shape[0]
    tm, tn = _largest_tile(S, MM_TILE), _largest_tile(F, MM_TILE // 2)

    def body(d_ref, w_ref, gt_ref, up_ref, dg_ref, du_ref):
        da, g, u = _dot(d_ref[...], w_ref[...], "nt"), gt_ref[...].astype(F32), up_ref[...].astype(F32)
        s = _sigmoid(g)
        dg_ref[...] = (da * u * (s * (1.0 + g * (1.0 - s)))).astype(dg_ref.dtype)
        du_ref[...] = (da * (g * s)).astype(du_ref.dtype)

    o_spec = pl.BlockSpec((tm, tn), lambda i, j: (i, j))
    return pl.pallas_call(
        body, name="ffn_dact", grid=(S // tm, F // tn),
        in_specs=[pl.BlockSpec((tm, D), lambda i, j: (i, 0)), pl.BlockSpec((tn, D), lambda i, j: (j, 0)), o_spec, o_spec],
        out_specs=[o_spec, o_spec],
        out_shape=[jax.ShapeDtypeStruct((S, F), _MXU), jax.ShapeDtypeStruct((S, F), _MXU)],
        compiler_params=_cp(("parallel", "parallel")))(dh2, w_down, gt, up)


def _rms_fwd(x, w, name):
    S, D = x.shape
    tr = _pick(S, (256, 128))

    def body(x_ref, w_ref, xn_ref, rs_ref):
        xv = x_ref[...]
        rs = lax.rsqrt(jnp.mean(xv * xv, axis=-1, keepdims=True) + EPS)
        xn_ref[...] = ((xv * rs) * w_ref[...]).astype(xn_ref.dtype)
        rs_ref[...] = rs

    return pl.pallas_call(
        body, name=name, grid=(S // tr,),
        in_specs=[pl.BlockSpec((tr, D), lambda i: (i, 0)), pl.BlockSpec((1, D), lambda i: (0, 0))],
        out_specs=[pl.BlockSpec((tr, D), lambda i: (i, 0)), pl.BlockSpec((tr, 1), lambda i: (i, 0))],
        out_shape=[jax.ShapeDtypeStruct((S, D), _MXU), jax.ShapeDtypeStruct((S, 1), F32)],
        compiler_params=_cp(("parallel",)))(x, w)


def _rms_bwd(dyn, x, rs, w, res, name):
    S, D = x.shape
    tr = _pick(S, (256, 128))

    def body(dy_ref, x_ref, rs_ref, w_ref, res_ref, dx_ref, dxb_ref, dw_ref):
        @pl.when(pl.program_id(0) == 0)
        def _():
            dw_ref[...] = jnp.zeros_like(dw_ref)

        dy, r = dy_ref[...].astype(F32), rs_ref[...]
        xhat = x_ref[...] * r
        dw_ref[...] += jnp.sum(dy * xhat, axis=0, keepdims=True)
        dxhat = dy * w_ref[...]
        dx = res_ref[...] + r * (dxhat - xhat * jnp.mean(dxhat * xhat, axis=-1, keepdims=True))
        dx_ref[...] = dx
        dxb_ref[...] = dx.astype(dxb_ref.dtype)

    row = pl.BlockSpec((tr, D), lambda i: (i, 0))
    vec = pl.BlockSpec((1, D), lambda i: (0, 0))
    return pl.pallas_call(
        body, name=name, grid=(S // tr,),
        in_specs=[row, row, pl.BlockSpec((tr, 1), lambda i: (i, 0)), vec, row], out_specs=[row, row, vec],
        out_shape=[jax.ShapeDtypeStruct((S, D), F32), jax.ShapeDtypeStruct((S, D), _MXU), jax.ShapeDtypeStruct((1, D), F32)],
        compiler_params=_cp(("arbitrary",)))(dyn, x, rs, w, res)


def _final_loss(h2, w, tgt):
    S, D = h2.shape
    tr = _pick(S, (256, 128))

    def body(h_ref, w_ref, t_ref, loss_ref, dh_ref, dhb_ref, dw_ref):
        @pl.when(pl.program_id(0) == 0)
        def _():
            dw_ref[...] = jnp.zeros_like(dw_ref)
            loss_ref[...] = jnp.zeros_like(loss_ref)

        hv, wv = h_ref[...], w_ref[...]
        rs = lax.rsqrt(jnp.mean(hv * hv, axis=-1, keepdims=True) + EPS)
        xhat = hv * rs
        err = xhat * wv - t_ref[...]
        row = jnp.mean(err * err, axis=-1, keepdims=True)
        loss_ref[...] += jnp.broadcast_to(0.5 * jnp.sum(row, axis=0, keepdims=True), loss_ref.shape)
        dy = err * (1.0 / D)
        dw_ref[...] += jnp.sum(dy * xhat, axis=0, keepdims=True)
        dxhat = dy * wv
        dh = rs * (dxhat - xhat * jnp.mean(dxhat * xhat, axis=-1, keepdims=True))
        dh_ref[...] = dh
        dhb_ref[...] = dh.astype(dhb_ref.dtype)

    row = pl.BlockSpec((tr, D), lambda i: (i, 0))
    vec = pl.BlockSpec((1, D), lambda i: (0, 0))
    return pl.pallas_call(
        body, name="final_loss", grid=(S // tr,), in_specs=[row, vec, row],
        out_specs=[pl.BlockSpec((1, LANE), lambda i: (0, 0)), row, row, vec],
        out_shape=[jax.ShapeDtypeStruct((1, LANE), F32), jax.ShapeDtypeStruct((S, D), F32), jax.ShapeDtypeStruct((S, D), _MXU),
                   jax.ShapeDtypeStruct((1, D), F32)],
        compiler_params=_cp(("arbitrary",)))(h2, w, tgt)


def _shift_rows(x, k, rows):
    if k == 0:
        return x
    S = x.shape[0]
    r = pltpu.roll(x, k % S, axis=0)
    ok = (rows >= k) if k > 0 else (rows < S + k)
    return jnp.where(ok, r, 0.0)


XBC_COL0 = SSD_WIDTH // 128


def _conv_fwd(proj, conv_w, conv_b):
    S = proj.shape[0]
    nct = CONV_CH // 128

    def body(x_ref, w_ref, b_ref, o_ref):
        x = x_ref[...]
        rows = lax.broadcasted_iota(jnp.int32, x.shape, 0)
        c = b_ref[...] + w_ref[3:4, :] * x
        for k in range(1, CONV_K):
            c = c + w_ref[3 - k:4 - k, :] * _shift_rows(x, k, rows)
        o_ref[...] = c * _sigmoid(c)

    return pl.pallas_call(
        body, name="conv_fwd", grid=(nct,),
        in_specs=[pl.BlockSpec((S, 128), lambda j: (0, XBC_COL0 + j)), pl.BlockSpec((CONV_K, 128), lambda j: (0, j)),
                  pl.BlockSpec((1, 128), lambda j: (0, j))],
        out_specs=pl.BlockSpec((S, 128), lambda j: (0, j)),
        out_shape=jax.ShapeDtypeStruct((S, CONV_CH), F32), compiler_params=_cp(("parallel",)))(proj, conv_w, conv_b)


def _conv_bwd(proj, conv_w, conv_b, dxa):
    S = proj.shape[0]
    nct = CONV_CH // 128

    def body(x_ref, w_ref, b_ref, d_ref, dx_ref, dw_ref, db_ref):
        x = x_ref[...]
        rows = lax.broadcasted_iota(jnp.int32, x.shape, 0)
        xs = [_shift_rows(x, k, rows) for k in range(CONV_K)]
        c = b_ref[...] + w_ref[3:4, :] * x
        for k in range(1, CONV_K):
            c = c + w_ref[3 - k:4 - k, :] * xs[k]
        s = _sigmoid(c)
        dc = d_ref[...] * (s * (1.0 + c * (1.0 - s)))
        dx = w_ref[3:4, :] * dc
        for k in range(1, CONV_K):
            dx = dx + w_ref[3 - k:4 - k, :] * _shift_rows(dc, -k, rows)
        dx_ref[...] = dx.astype(dx_ref.dtype)
        for k in range(CONV_K):
            dw_ref[3 - k:4 - k, :] = jnp.sum(dc * xs[k], axis=0, keepdims=True)
        db_ref[...] = jnp.sum(dc, axis=0, keepdims=True)

    col = pl.BlockSpec((S, 128), lambda j: (0, j))
    return pl.pallas_call(
        body, name="conv_bwd", grid=(nct,),
        in_specs=[pl.BlockSpec((S, 128), lambda j: (0, XBC_COL0 + j)), pl.BlockSpec((CONV_K, 128), lambda j: (0, j)),
                  pl.BlockSpec((1, 128), lambda j: (0, j)), col],
        out_specs=[col, pl.BlockSpec((CONV_K, 128), lambda j: (0, j)), pl.BlockSpec((1, 128), lambda j: (0, j))],
        out_shape=[jax.ShapeDtypeStruct((S, CONV_CH), _MXU), jax.ShapeDtypeStruct((CONV_K, CONV_CH), F32),
                   jax.ShapeDtypeStruct((1, CONV_CH), F32)],
        compiler_params=_cp(("parallel",)))(proj, conv_w, conv_b, dxa)


def _ssd_consts():
    L = SSD_L
    r = lax.broadcasted_iota(jnp.int32, (L, L), 0)
    c = lax.broadcasted_iota(jnp.int32, (L, L), 1)
    causal = r >= c
    upper = (r <= c).astype(F32)
    hr = lax.broadcasted_iota(jnp.int32, (SSD_HEADS, SSD_WIDTH), 0)
    hc = lax.broadcasted_iota(jnp.int32, (SSD_HEADS, SSD_WIDTH), 1)
    expand = (lax.shift_right_logical(hc, 6) == hr).astype(F32)
    return causal, causal.astype(F32), upper, expand


def _softplus(x):
    return jnp.maximum(x, 0.0) + jnp.log(1.0 + jnp.exp(-jnp.abs(x)))


def _ssd_scalars(dtr, dt_bias, a_log, tri, upper, expand):
    dt = _softplus(dtr + dt_bias)
    A = -jnp.exp(a_log)
    adt = dt * A
    acum = _dot(tri, adt, "nn", split="b")
    acum_t = _dot(adt, upper, "tn", split="a")
    alast = acum[SSD_L - 1:SSD_L, :]
    e = jnp.exp(acum)
    wdec = jnp.exp(alast - acum)
    gam = jnp.exp(alast)
    ex = lambda t: _dot(t, expand, "nn", split="a")
    gam8 = jnp.broadcast_to(gam, (8, SSD_HEADS))
    return dt, A, acum, acum_t, e, wdec, gam, ex(dt), ex(e), ex(wdec), ex(gam8)[0:1, :]


def _ssd_fwd(proj, proj_small, xa, dt_bias, a_log, d_skip, norm_w):
    S = proj.shape[0]
    L, N, W = SSD_L, SSD_N, SSD_WIDTH
    nc = S // L

    def body(z_ref, xa_ref, dtr_ref, dtb_ref, al_ref, dsk_ref, nw_ref, yo_ref, y_ref, rs_ref, hs_ref, h_scr, y_scr):
        @pl.when(pl.program_id(0) == 0)
        def _():
            h_scr[...] = jnp.zeros_like(h_scr)

        causal, tri, upper, expand = _ssd_consts()
        dt, A, acum, acum_t, e, wdec, gam, dtE, eE, wE, gamE = _ssd_scalars(dtr_ref[:, 0:SSD_HEADS], dtb_ref[...], al_ref[...], tri, upper, expand)
        xs = xa_ref[:, 0:W]
        X = xs * dtE
        XW = X * wE
        hs_ref[0] = h_scr[...]
        for g in range(SSD_G):
            gs = slice(g * 512, (g + 1) * 512)
            Bg = xa_ref[:, W + g * N:W + (g + 1) * N]
            Cg = xa_ref[:, W + SSD_G * N + g * N:W + SSD_G * N + (g + 1) * N]
            Hg = h_scr[:, gs]
            CB = _dot(Cg, Bg, "nt")
            yoff = _dot(Cg, Hg, "nn") * eE[:, gs]
            st = _dot(Bg, XW[:, gs], "tn")
            for j in range(8):
                h = g * 8 + j
                hsl = slice(h * SSD_P, (h + 1) * SSD_P)
                lam = jnp.exp(jnp.where(causal, acum[:, h:h + 1] - acum_t[h:h + 1, :], -jnp.inf))
                y_scr[:, hsl] = _dot(CB * lam, X[:, hsl], "nn") + yoff[:, j * SSD_P:(j + 1) * SSD_P]
            h_scr[:, gs] = gamE[:, gs] * Hg + st
        dskE = _dot(jnp.broadcast_to(dsk_ref[...], (8, SSD_HEADS)), expand, "nn", split="a")[0:1, :]
        y = y_scr[...] + dskE * xs
        y_ref[...] = y
        zv = z_ref[...]
        yg = y * (zv * _sigmoid(zv))
        rs = lax.rsqrt(jnp.mean(yg * yg, axis=-1, keepdims=True) + EPS)
        rs_ref[...] = rs
        yo_ref[...] = ((yg * rs) * nw_ref[...]).astype(yo_ref.dtype)

    p16 = pl.BlockSpec((1, SSD_HEADS), lambda c: (0, 0))
    return pl.pallas_call(
        body, name="ssd_fwd", grid=(nc,),
        in_specs=[pl.BlockSpec((L, W), lambda c: (c, 0)), pl.BlockSpec((L, CONV_CH), lambda c: (c, 0)),
                  pl.BlockSpec((L, W_SMALL), lambda c: (c, 0)), p16, p16, p16, pl.BlockSpec((1, W), lambda c: (0, 0))],
        out_specs=[pl.BlockSpec((L, W), lambda c: (c, 0)), pl.BlockSpec((L, W), lambda c: (c, 0)),
                   pl.BlockSpec((L, 1), lambda c: (c, 0)), pl.BlockSpec((1, N, W), lambda c: (c, 0, 0))],
        out_shape=[jax.ShapeDtypeStruct((S, W), _MXU), jax.ShapeDtypeStruct((S, W), F32), jax.ShapeDtypeStruct((S, 1), F32),
                   jax.ShapeDtypeStruct((nc, N, W), F32)],
        scratch_shapes=[pltpu.VMEM((N, W), F32), pltpu.VMEM((L, W), F32)],
        compiler_params=_cp(("arbitrary",)))(proj, xa, proj_small, dt_bias, a_log, d_skip, norm_w)


def _ssd_bwd(dmixed, proj, proj_small, xa, y, rs2, hs, dt_bias, a_log, d_skip, norm_w):
    S = proj.shape[0]
    L, N, W, H = SSD_L, SSD_N, SSD_WIDTH, SSD_HEADS
    nc = S // L

    def body(dyo_ref, z_ref, xa_ref, dtr_ref, y_ref, rs_ref, hs_ref, dtb_ref, al_ref, dsk_ref, nw_ref,
             dz_ref, dxa_ref, ddtr_ref, ddtb_ref, dal_ref, ddsk_ref, dnw_ref, dh_scr, dx_scr):
        @pl.when(pl.program_id(0) == 0)
        def _():
            dh_scr[...] = jnp.zeros_like(dh_scr)
            ddtb_ref[...] = jnp.zeros_like(ddtb_ref)
            dal_ref[...] = jnp.zeros_like(dal_ref)
            ddsk_ref[...] = jnp.zeros_like(ddsk_ref)
            dnw_ref[...] = jnp.zeros_like(dnw_ref)

        causal, tri, upper, expand = _ssd_consts()
        heads = lambda t: _dot(t, expand, "nt", split="a")
        onehot = lambda h: (lax.broadcasted_iota(jnp.int32, (1, H), 1) == h).astype(F32)

        zv, yv, rs = z_ref[...], y_ref[...], rs_ref[...]
        sz = _sigmoid(zv)
        zs = zv * sz
        xhat = (yv * zs) * rs
        dyo = dyo_ref[...].astype(F32)
        dnw_ref[...] += jnp.sum(dyo * xhat, axis=0, keepdims=True)
        dxhat = dyo * nw_ref[...]
        dyg = rs * (dxhat - xhat * jnp.mean(dxhat * xhat, axis=-1, keepdims=True))
        dz_ref[...] = (dyg * yv * (sz * (1.0 + zv * (1.0 - sz)))).astype(dz_ref.dtype)
        dy = dyg * zs

        dtr = dtr_ref[:, 0:H]
        dt, A, acum, acum_t, e, wdec, gam, dtE, eE, wE, gamE = _ssd_scalars(dtr, dtb_ref[...], al_ref[...], tri, upper, expand)
        xs = xa_ref[:, 0:W]
        X = xs * dtE
        XW = X * wE
        dskE = _dot(jnp.broadcast_to(dsk_ref[...], (8, H)), expand, "nn", split="a")[0:1, :]
        ddsk_ref[...] += heads(jnp.broadcast_to(jnp.sum(dy * xs, axis=0, keepdims=True), (8, W)))[0:1, :]

        dYe = dy * eE
        dacum = jnp.zeros((L, H), F32)
        de_full = []
        dw_full = []
        dgam_full = []
        for g in range(SSD_G):
            gs = slice(g * 512, (g + 1) * 512)
            Bg = xa_ref[:, W + g * N:W + (g + 1) * N]
            Cg = xa_ref[:, W + SSD_G * N + g * N:W + SSD_G * N + (g + 1) * N]
            Hg = hs_ref[0, :, gs]
            dHn = dh_scr[:, gs]
            CH = _dot(Cg, Hg, "nn")
            de_full.append(dy[:, gs] * CH)
            dC = _dot(dYe[:, gs], Hg, "nt")
            dHs = gamE[:, gs] * dHn + _dot(Cg, dYe[:, gs], "tn")
            dgam_full.append(jnp.sum(dHn * Hg, axis=0, keepdims=True))
            BdS = _dot(Bg, dHn, "nn")
            dB = _dot(XW[:, gs], dHn, "nt")
            dx_scr[:, gs] = BdS * wE[:, gs]
            dw_full.append(BdS * X[:, gs])
            CB = _dot(Cg, Bg, "nt")
            dCB = jnp.zeros((L, L), F32)
            for j in range(8):
                h = g * 8 + j
                hsl = slice(h * SSD_P, (h + 1) * SSD_P)
                lam = jnp.exp(jnp.where(causal, acum[:, h:h + 1] - acum_t[h:h + 1, :], -jnp.inf))
                M = CB * lam
                dM = _dot(dy[:, hsl], X[:, hsl], "nt")
                dx_scr[:, hsl] += _dot(M, dy[:, hsl], "tn")
                dCB = dCB + dM * lam
                Q = dM * M
                rowsum = jnp.sum(Q, axis=1, keepdims=True)
                colsum = _dot(Q, jnp.ones((L, 8), F32), "tn", split="a")[:, 0:1]
                dacum = dacum + (rowsum - colsum) * onehot(h)
            dC = dC + _dot(dCB, Bg, "nn")
            dB = dB + _dot(dCB, Cg, "tn")
            dxa_ref[:, W + g * N:W + (g + 1) * N] = dB
            dxa_ref[:, W + SSD_G * N + g * N:W + SSD_G * N + (g + 1) * N] = dC
            dh_scr[:, gs] = dHs

        de16 = heads(jnp.concatenate(de_full, axis=1))
        dw16 = heads(jnp.concatenate(dw_full, axis=1))
        dgam16 = heads(jnp.broadcast_to(jnp.concatenate(dgam_full, axis=1), (8, W)))[0:1, :]
        dacum = dacum + de16 * e - dw16 * wdec
        dlast = jnp.sum(dw16 * wdec, axis=0, keepdims=True) + dgam16 * gam
        lastrow = (lax.broadcasted_iota(jnp.int32, (L, 1), 0) == L - 1).astype(F32)
        dacum = dacum + lastrow * dlast
        da = _dot(tri, dacum, "tn", split="b")
        dX = dx_scr[...]
        ddt = da * A + heads(dX * xs)
        dA = jnp.sum(da * dt, axis=0, keepdims=True)
        dal_ref[...] += dA * A
        ddtr = ddt * _sigmoid(dtr + dtb_ref[...])
        ddtb_ref[...] += jnp.sum(ddtr, axis=0, keepdims=True)
        ddtr_ref[...] = ddtr
        dxa_ref[:, 0:W] = dX * dtE + dy * dskE

    p16 = pl.BlockSpec((1, H), lambda c: (0, 0))
    rev = lambda c: (nc - 1 - c, 0)
    return pl.pallas_call(
        body, name="ssd_bwd", grid=(nc,),
        in_specs=[pl.BlockSpec((L, W), rev), pl.BlockSpec((L, W), rev), pl.BlockSpec((L, CONV_CH), rev),
                  pl.BlockSpec((L, W_SMALL), rev), pl.BlockSpec((L, W), rev), pl.BlockSpec((L, 1), rev),
                  pl.BlockSpec((1, N, W), lambda c: (nc - 1 - c, 0, 0)), p16, p16, p16, pl.BlockSpec((1, W), lambda c: (0, 0))],
        out_specs=[pl.BlockSpec((L, W), rev), pl.BlockSpec((L, CONV_CH), rev), pl.BlockSpec((L, H), rev),
                   p16, p16, p16, pl.BlockSpec((1, W), lambda c: (0, 0))],
        out_shape=[jax.ShapeDtypeStruct((S, W), _MXU), jax.ShapeDtypeStruct((S, CONV_CH), F32), jax.ShapeDtypeStruct((S, H), F32),
                   jax.ShapeDtypeStruct((1, H), F32), jax.ShapeDtypeStruct((1, H), F32), jax.ShapeDtypeStruct((1, H), F32),
                   jax.ShapeDtypeStruct((1, W), F32)],
        scratch_shapes=[pltpu.VMEM((N, W), F32), pltpu.VMEM((L, W), F32)],
        compiler_params=_cp(("arbitrary",)))(dmixed, proj, xa, proj_small, y, rs2, hs, dt_bias, a_log, d_skip, norm_w)


def _rope_tables(S):
    inv = 1.0 / (ROPE_THETA ** (jnp.arange(0, ROPE_DIM, 2, dtype=F32) / ROPE_DIM))
    ang = jnp.arange(S, dtype=F32)[:, None] * inv[None, :]
    cos, sin = jnp.cos(ang), jnp.sin(ang)
    half = ROPE_DIM // 2
    c64 = jnp.concatenate([cos, cos, jnp.ones((S, HD - ROPE_DIM), F32)], axis=1)
    s64 = jnp.concatenate([sin, sin, jnp.zeros((S, HD - ROPE_DIM), F32)], axis=1)
    del half
    return jnp.concatenate([c64, c64], axis=1), jnp.concatenate([s64, s64], axis=1)


def _rope(xs, blk0, width, cos, sin, sign, out_dtype, name, extra=None):
    S = xs[0].shape[0]
    tr = _pick(S, (512, 256, 128))
    nx = len(xs)

    def body(*refs):
        x_refs, c_ref, s_ref = refs[:nx], refs[nx], refs[nx + 1]
        e_ref = refs[nx + 2] if extra is not None else None
        o_ref = refs[-1]
        cv, sv = c_ref[...], s_ref[...] * sign
        lane = lax.broadcasted_iota(jnp.int32, (tr, 128), 1)
        first = (lane & (HD - 1)) < (ROPE_DIM // 2)
        for j in range(bw // 128):
            cs = slice(j * 128, (j + 1) * 128)
            xv = x_refs[0][:, cs].astype(F32)
            for r in x_refs[1:]:
                xv = xv + r[:, cs].astype(F32)
            out = _rotate128(xv, cv, sv, first)
            if extra is not None:
                out = out + e_ref[:, cs].astype(F32)
            o_ref[:, cs] = out.astype(out_dtype)

    bw = 512
    assert width % bw == 0 and (blk0 * 256) % bw == 0
    b0 = blk0 * 256 // bw
    t128 = pl.BlockSpec((tr, 128), lambda i, j: (i, 0))
    oblk = pl.BlockSpec((tr, bw), lambda i, j: (i, j))
    specs = [pl.BlockSpec((tr, bw), lambda i, j: (i, b0 + j))] * nx + [t128, t128]
    ins = list(xs) + [cos, sin]
    if extra is not None:
        assert (extra[1] * 256) % bw == 0
        ins.append(extra[0])
        eb = extra[1] * 256 // bw
        specs.append(pl.BlockSpec((tr, bw), lambda i, j: (i, eb + j)))
    return pl.pallas_call(
        body, name=name, grid=(S // tr, width // bw), in_specs=specs, out_specs=oblk,
        out_shape=jax.ShapeDtypeStruct((S, width), out_dtype), compiler_params=_cp(("parallel", "parallel")))(*ins)


def _rotate128(xv, cv, sv, first):
    rot = jnp.where(first, -pltpu.roll(xv, 128 - ROPE_DIM // 2, axis=1), pltpu.roll(xv, ROPE_DIM // 2, axis=1))
    return xv * cv + rot * sv


def _kv_prep(proj, cos, sin, tk):
    S = proj.shape[0]
    NB = S // SEL_BLOCK

    def body(ks_ref, vs_ref, kw_ref, vw_ref, c_ref, s_ref, *outs):
        cv, sv = c_ref[...], s_ref[...]
        lane = lax.broadcasted_iota(jnp.int32, (tk, 128), 1)
        first = (lane & (HD - 1)) < (ROPE_DIM // 2)
        key = pl.program_id(0) * tk + lax.broadcasted_iota(jnp.int32, (tk, NB), 0)
        onehot = (lax.shift_right_logical(key, 6) == lax.broadcasted_iota(jnp.int32, (tk, NB), 1)).astype(F32)
        for j, (ref, rotated) in enumerate(((ks_ref, True), (vs_ref, False), (kw_ref, True), (vw_ref, False))):
            nat, blk = outs[2 * j], outs[2 * j + 1]
            for half in range(2):
                xv = ref[:, half * 128:(half + 1) * 128]
                if rotated:
                    xv = _rotate128(xv, cv, sv, first)
                for e in range(2):
                    h = 2 * half + e
                    piece = xv[:, e * HD:(e + 1) * HD]
                    nat[h] = (jnp.concatenate([piece, onehot], axis=1) if j == 0 else piece).astype(nat.dtype)
                    blk[h, 0] = piece.T.astype(blk.dtype)

    col = lambda b: pl.BlockSpec((tk, 256), lambda i: (i, b))
    t128 = pl.BlockSpec((tk, 128), lambda i: (i, 0))
    nat_spec = lambda w: pl.BlockSpec((N_KV, tk, w), lambda i: (0, i, 0))
    blk_spec = pl.BlockSpec((N_KV, 1, HD, tk), lambda i: (0, i, 0, 0))
    nat_shape = lambda w: jax.ShapeDtypeStruct((N_KV, S, w), _MXU)
    blk_shape = jax.ShapeDtypeStruct((N_KV, S // tk, HD, tk), _MXU)
    widths = (HD + NB, HD, HD, HD)
    res = pl.pallas_call(
        body, name="kv_prep", grid=(S // tk,), in_specs=[col(KSB), col(VSB), col(KWB), col(VWB), t128, t128],
        out_specs=[s for w in widths for s in (nat_spec(w), blk_spec)],
        out_shape=[s for w in widths for s in (nat_shape(w), blk_shape)],
        compiler_params=_cp(("parallel",)))(proj, proj, proj, proj, cos, sin)
    return dict(ks_ext=res[0], ks_t=res[1], vs=res[2], vs_t=res[3], kw=res[4], kw_t=res[5], vw=res[6], vw_t=res[7])


def _dkv_post(dks, dvs, dkw, dvw, cos, sin):
    S = dks.shape[1]
    tr = _pick(S, (512, 256, 128))

    def body(dks_ref, dvs_ref, dkw_ref, dvw_ref, c_ref, s_ref, o_ref):
        cv, sv = c_ref[...], -s_ref[...]
        lane = lax.broadcasted_iota(jnp.int32, (tr, 128), 1)
        first = (lane & (HD - 1)) < (ROPE_DIM // 2)
        for j, (ref, rotated) in enumerate(((dks_ref, True), (dvs_ref, False), (dkw_ref, True), (dvw_ref, False))):
            for half in range(2):
                xv = jnp.concatenate([ref[2 * half], ref[2 * half + 1]], axis=1)
                if rotated:
                    xv = _rotate128(xv, cv, sv, first)
                o_ref[:, j * 256 + half * 128:j * 256 + (half + 1) * 128] = xv.astype(o_ref.dtype)

    hm = pl.BlockSpec((N_KV, tr, HD), lambda i: (0, i, 0))
    t128 = pl.BlockSpec((tr, 128), lambda i: (i, 0))
    return pl.pallas_call(
        body, name="dkv_post", grid=(S // tr,), in_specs=[hm, hm, hm, hm, t128, t128],
        out_specs=pl.BlockSpec((tr, 4 * 256), lambda i: (i, 0)), out_shape=jax.ShapeDtypeStruct((S, 4 * 256), _MXU),
        compiler_params=_cp(("parallel",)))(dks, dvs, dkw, dvw, cos, sin)


def _compress_fwd(R, pe, w1, w2):
    NC = R.shape[1]
    half = 16 * HD

    def body(r_ref, pe_ref, w1_ref, w2_ref, o_ref, hid_ref):
        r = r_ref[0]
        a = _dot(r + pe_ref[:, 0:half], w1_ref[0:half, :], "nn")
        b = _dot(r + pe_ref[:, half:2 * half], w1_ref[half:2 * half, :], "nn")
        hid = a + pltpu.roll(b, NC - 1, axis=0)
        hid_ref[0] = hid
        out = _dot(hid * _sigmoid(hid), w2_ref[...], "nn")
        rows = lax.broadcasted_iota(jnp.int32, out.shape, 0)
        o_ref[0] = jnp.where(rows < NC - 1, out, 0.0).astype(o_ref.dtype)

    return pl.pallas_call(
        body, name="compress_fwd", grid=(N_KV,),
        in_specs=[pl.BlockSpec((1, NC, half), lambda h: (h, 0, 0)), pl.BlockSpec((1, 2 * half), lambda h: (0, 0)),
                  pl.BlockSpec((2 * half, CMP_HID), lambda h: (0, 0)), pl.BlockSpec((CMP_HID, HD), lambda h: (0, 0))],
        out_specs=[pl.BlockSpec((1, NC, HD), lambda h: (h, 0, 0)), pl.BlockSpec((1, NC, CMP_HID), lambda h: (h, 0, 0))],
        out_shape=[jax.ShapeDtypeStruct((N_KV, NC, HD), _MXU), jax.ShapeDtypeStruct((N_KV, NC, CMP_HID), F32)],
        compiler_params=_cp(("parallel",)))(R, pe, w1, w2)


def _compress_bwd(R, pe, w1, w2, hid, dout):
    NC = R.shape[1]
    half = 16 * HD

    def body(r_ref, pe_ref, w1_ref, w2_ref, hid_ref, do_ref, dr_ref, dw1_ref, dw2_ref, dpe_ref):
        @pl.when(pl.program_id(0) == 0)
        def _():
            dw1_ref[...] = jnp.zeros_like(dw1_ref)
            dw2_ref[...] = jnp.zeros_like(dw2_ref)
            dpe_ref[...] = jnp.zeros_like(dpe_ref)

        r, hv, do = r_ref[0], hid_ref[0], do_ref[0]
        s = _sigmoid(hv)
        dw2_ref[...] += _dot(hv * s, do, "tn")
        dhid = _dot(do, w2_ref[...], "nt") * (s * (1.0 + hv * (1.0 - s)))
        rows = lax.broadcasted_iota(jnp.int32, dhid.shape, 0)
        dhid = jnp.where(rows < NC - 1, dhid, 0.0)
        dhid_dn = pltpu.roll(dhid, 1, axis=0)
        dw1_ref[0:half, :] += _dot(r + pe_ref[:, 0:half], dhid, "tn")
        dw1_ref[half:2 * half, :] += _dot(r + pe_ref[:, half:2 * half], dhid_dn, "tn")
        dxt = _dot(dhid, w1_ref[0:half, :], "nt")
        dxb = _dot(dhid_dn, w1_ref[half:2 * half, :], "nt")
        dr_ref[0] = dxt + dxb
        dpe_ref[:, 0:half] += jnp.sum(dxt, axis=0, keepdims=True)
        dpe_ref[:, half:2 * half] += jnp.sum(dxb, axis=0, keepdims=True)

    return pl.pallas_call(
        body, name="compress_bwd", grid=(N_KV,),
        in_specs=[pl.BlockSpec((1, NC, half), lambda h: (h, 0, 0)), pl.BlockSpec((1, 2 * half), lambda h: (0, 0)),
                  pl.BlockSpec((2 * half, CMP_HID), lambda h: (0, 0)), pl.BlockSpec((CMP_HID, HD), lambda h: (0, 0)),
                  pl.BlockSpec((1, NC, CMP_HID), lambda h: (h, 0, 0)), pl.BlockSpec((1, NC, HD), lambda h: (h, 0, 0))],
        out_specs=[pl.BlockSpec((1, NC, half), lambda h: (h, 0, 0)), pl.BlockSpec((2 * half, CMP_HID), lambda h: (0, 0)),
                   pl.BlockSpec((CMP_HID, HD), lambda h: (0, 0)), pl.BlockSpec((1, 2 * half), lambda h: (0, 0))],
        out_shape=[jax.ShapeDtypeStruct((N_KV, NC, half), F32), jax.ShapeDtypeStruct((2 * half, CMP_HID), F32),
                   jax.ShapeDtypeStruct((CMP_HID, HD), F32), jax.ShapeDtypeStruct((1, 2 * half), F32)],
        compiler_params=_cp(("arbitrary",)))(R, pe, w1, w2, hid, dout)


def _attn_cfg(S, Sk, mode):
    tk = _pick(Sk, (256, 128))
    if mode == "cmp":
        return _pick(S, (512, 256, 128)), Sk
    if mode == "sel" and S % (2 * tk) == 0:
        return 2 * tk, tk
    return tk, tk


def _block_start(kb, tk):
    return kb * tk if isinstance(kb, int) else pl.multiple_of(kb * tk, tk)


def _pipelined_key_blocks(mode, q0, tq, tk, produce, consume):
    if mode == "cmp":
        produce(0, True, 0)
        consume(0, 0)
        return
    if mode == "win":
        assert tq == tk and WINDOW == 2 * tk
        last = q0 // tk
        first = jnp.maximum(last - 2, 0)

        @pl.when(last == 0)
        def _():
            produce(last, True, 0)
            consume(last, 0)

        @pl.when(last == 1)
        def _():
            produce(first, True, 0)
            produce(last, True, 1)
            consume(first, 0)
            consume(last, 1)

        @pl.when(last >= 2)
        def _():
            produce(first, True, 0)
            produce(first + 1, False, 1)
            consume(first, 0)
            produce(last, True, 0)
            consume(first + 1, 1)
            consume(last, 0)

        return
    first, n_plain, plain_masked = 0, q0 // tk, False
    pairs = jnp.maximum(n_plain - 1, 0) // 2
    if tq == 2 * tk:
        @pl.when(n_plain >= 1)
        def _():
            produce(0, False, 0)

        def two_plain(j, carry):
            produce(2 * j + 1, False, 1)
            consume(2 * j, 0)
            produce(2 * j + 2, False, 0)
            consume(2 * j + 1, 1)
            return carry

        lax.fori_loop(0, pairs, two_plain, 0)
        kb = 2 * pairs

        @pl.when(n_plain >= 2)
        def _():
            produce(kb + 1, False, 1)
            consume(kb, 0)
            produce(n_plain, True, 0)
            consume(kb + 1, 1)
            produce(n_plain + 1, True, 1)
            consume(n_plain, 0)
            consume(n_plain + 1, 1)

        @pl.when(n_plain == 0)
        def _():
            produce(0, True, 0)
            produce(1, True, 1)
            consume(0, 0)
            consume(1, 1)

        return
    assert tq == tk
    last = first + n_plain

    @pl.when(n_plain >= 1)
    def _():
        produce(first, plain_masked, 0)

    def two(j, carry):
        kb = first + 2 * j
        produce(kb + 1, plain_masked, 1)
        consume(kb, 0)
        produce(kb + 2, plain_masked, 0)
        consume(kb + 1, 1)
        return carry

    lax.fori_loop(0, pairs, two, 0)
    kb = first + 2 * pairs
    left = n_plain - 2 * pairs

    @pl.when(left == 2)
    def _():
        produce(kb + 1, plain_masked, 1)
        consume(kb, 0)
        produce(last, True, 0)
        consume(kb + 1, 1)
        consume(last, 0)

    @pl.when(left == 1)
    def _():
        produce(last, True, 1)
        consume(kb, 0)
        consume(last, 1)

    @pl.when(left == 0)
    def _():
        produce(last, True, 0)
        consume(last, 0)


def _attn_bias(mode, q0, k0, tq, tk):
    k = k0 + lax.broadcasted_iota(jnp.int32, (tk, tq), 0)
    t = q0 + lax.broadcasted_iota(jnp.int32, (tk, tq), 1)
    if mode == "cmp":
        ok = (k * 16 + 31) <= t
    elif mode == "win":
        ok = (k <= t) & ((t - k) < WINDOW)
    else:
        ok = k <= t
    bias = jnp.where(ok, 0.0, NEG)
    return jnp.concatenate([bias] * GRP, axis=1), jnp.concatenate([ok.astype(F32)] * GRP, axis=1)


def _sel_operands(qs, selneg_ref):
    return jnp.concatenate([qs, jnp.concatenate([selneg_ref[0]] * GRP, axis=0)], axis=1)


def _stack_heads(ref, tq):
    return jnp.concatenate([ref[:, g * HD:(g + 1) * HD] for g in range(GRP)], axis=0)


def _scaled_queries(q_ref, tq):
    return (_stack_heads(q_ref, tq).astype(F32) * SCALE).astype(_MXU)


def _blocked_t(x, tk):
    n, Sk, d = x.shape
    return x.reshape(n, Sk // tk, tk, d).transpose(0, 1, 3, 2)


def _head_rows(ref):
    return jnp.concatenate([ref[0, g:g + 1, :] for g in range(GRP)], axis=1)


def _attn_fwd(q, qcol0, k, vt, mode, selneg, gate, y_prev, y_dtype, name):
    S, Sk = q.shape[0], k.shape[1]
    tq, tk = _attn_cfg(S, Sk, mode)
    R = GRP * tq
    NB = S // SEL_BLOCK

    def body(*refs):
        q_ref, k_ref, vt_ref = refs[:3]
        rest = list(refs[3:])
        sel_ref = rest.pop(0) if mode == "sel" else None
        ov_ref = rest.pop(0) if mode == "cmp" else None
        gate_ref = rest.pop(0)
        yp_ref = rest.pop(0) if y_prev is not None else None
        o_ref, lse_ref, y_ref = rest[:3]
        choice_ref = rest[3] if mode == "cmp" else None
        m_scr, l_scr, acc, s_scr = rest[-4:]
        q0 = pl.program_id(1) * tq
        qs = _scaled_queries(q_ref, tq)
        m_scr[...] = jnp.full_like(m_scr, NEG)
        l_scr[...] = jnp.zeros_like(l_scr)
        acc[...] = jnp.zeros_like(acc)
        qk = _sel_operands(qs, sel_ref) if mode == "sel" else qs

        def produce(kb, masked, slot):
            k0 = _block_start(kb, tk)
            s = _dot(k_ref[0, pl.ds(k0, tk), :], qk, "nt")
            if masked:
                s = s + _attn_bias(mode, q0, k0, tq, tk)[0]
            s_scr[slot] = s

        def consume(kb, slot):
            s = s_scr[slot]
            m_old = m_scr[...]
            m_new = jnp.maximum(m_old, jnp.max(s, axis=0, keepdims=True))
            p = jnp.exp(s - m_new)
            if mode == "cmp":
                p = p * _attn_bias(mode, q0, 0, tq, tk)[1]
            alpha = jnp.exp(m_old - m_new)
            l_scr[...] = alpha * l_scr[...] + jnp.sum(p, axis=0, keepdims=True)
            acc[...] = alpha * acc[...] + _dot(vt_ref[0, kb], p, "nn")
            m_scr[...] = m_new

        _pipelined_key_blocks(mode, q0, tq, tk, produce, consume)
        l = l_scr[...]
        good = l > 0.0
        o_t = acc[...] * jnp.where(good, 1.0 / jnp.where(good, l, 1.0), 0.0)
        lse = jnp.where(good, m_scr[...] + jnp.log(jnp.where(good, l, 1.0)), -NEG)
        y_t = o_t * _sigmoid(_head_rows(gate_ref))
        if mode == "cmp":
            p = jnp.exp(s_scr[0] - lse) * _attn_bias(mode, q0, 0, tq, tk)[1]
            choice_ref[0] = _chosen_blocks(p, ov_ref[...], q0, tq).astype(choice_ref.dtype)
        for g in range(GRP):
            hs, qs_ = slice(g * HD, (g + 1) * HD), slice(g * tq, (g + 1) * tq)
            o_ref[:, hs] = o_t[:, qs_].T
            lse_ref[0, g:g + 1, :] = lse[:, qs_]
            yg = y_t[:, qs_].T
            if y_prev is not None:
                yg = yg + yp_ref[:, hs]
            y_ref[:, hs] = yg.astype(y_ref.dtype)

    row_spec = pl.BlockSpec((1, GRP, tq), lambda h, i: (h, 0, i))
    qo_spec = pl.BlockSpec((tq, GRP * HD), lambda h, i: (i, h))
    ins = [q, k, vt]
    specs = [pl.BlockSpec((tq, GRP * HD), lambda h, i: (i, qcol0 + h)), pl.BlockSpec((1, Sk, k.shape[2]), lambda h, i: (h, 0, 0)),
             pl.BlockSpec((1, Sk // tk, HD, tk), lambda h, i: (h, 0, 0, 0))]
    if mode == "sel":
        ins.append(selneg)
        specs.append(pl.BlockSpec((1, tq, selneg.shape[2]), lambda h, i: (h, i, 0)))
    out_specs = [qo_spec, row_spec, qo_spec]
    out_shape = [jax.ShapeDtypeStruct((S, ATT_WIDTH), F32), jax.ShapeDtypeStruct((N_KV, GRP, S), F32),
                 jax.ShapeDtypeStruct((S, ATT_WIDTH), y_dtype)]
    if mode == "cmp":
        ins.append(_block_overlap(Sk, NB))
        specs.append(pl.BlockSpec((NB, Sk), lambda h, i: (0, 0)))
        out_specs.append(pl.BlockSpec((1, tq, NB), lambda h, i: (h, i, 0)))
        out_shape.append(jax.ShapeDtypeStruct((N_KV, S, NB), _MXU))
    ins.append(gate)
    specs.append(row_spec)
    if y_prev is not None:
        ins.append(y_prev)
        specs.append(qo_spec)
    return pl.pallas_call(
        body, name=name, grid=(N_KV, S // tq), in_specs=specs, out_specs=out_specs, out_shape=out_shape,
        scratch_shapes=[pltpu.VMEM((1, R), F32), pltpu.VMEM((1, R), F32), pltpu.VMEM((HD, R), F32), pltpu.VMEM((2, tk, R), F32)],
        compiler_params=_cp(("parallel", "arbitrary")))(*ins)


def _attn_bwd(q, qcol0, k, kt, v, o, lse, dy, dycol0, gate, mode, selneg, name):
    S, Sk = q.shape[0], k.shape[1]
    tq, tk = _attn_cfg(S, Sk, mode)
    R = GRP * tq

    def body(*refs):
        if mode == "sel":
            (q_ref, k_ref, kt_ref, v_ref, o_ref, lse_ref, dy_ref, gate_ref, sel_ref, dq_ref, dk_ref, dv_ref, dg_ref, dq_scr, s_scr,
             dp_scr) = refs
        else:
            q_ref, k_ref, kt_ref, v_ref, o_ref, lse_ref, dy_ref, gate_ref, dq_ref, dk_ref, dv_ref, dg_ref, dq_scr, s_scr, dp_scr = refs

        @pl.when(pl.program_id(1) == 0)
        def _():
            dk_ref[...] = jnp.zeros_like(dk_ref)
            dv_ref[...] = jnp.zeros_like(dv_ref)

        q0 = pl.program_id(1) * tq
        qs = _scaled_queries(q_ref, tq)
        dys = _stack_heads(dy_ref, tq)
        gv = _sigmoid(_head_rows(gate_ref))
        dy_o = _dot(jnp.ones((8, HD), F32), dys * _stack_heads(o_ref, tq), "nt", split="b")[0:1, :]
        delta = gv * dy_o
        dgate = dy_o * (gv * (1.0 - gv))
        for g in range(GRP):
            dg_ref[0, g:g + 1, :] = dgate[:, g * tq:(g + 1) * tq]
        lsev = _head_rows(lse_ref)
        dos = (dys * jnp.broadcast_to(gv, (8, R)).T[:, 0:1]).astype(_MXU)
        dq_scr[...] = jnp.zeros_like(dq_scr)
        qk = _sel_operands(qs, sel_ref) if mode == "sel" else qs

        def produce(kb, masked, slot):
            k0 = _block_start(kb, tk)
            s = _dot(k_ref[0, pl.ds(k0, tk), :], qk, "nt")
            if masked:
                s = s + _attn_bias(mode, q0, k0, tq, tk)[0]
            s_scr[slot] = s
            dp_scr[slot] = _dot(v_ref[0, pl.ds(k0, tk), :], dos, "nt")

        def consume(kb, slot):
            k0 = _block_start(kb, tk)
            p = jnp.exp(s_scr[slot] - lsev)
            if mode == "cmp":
                p = p * _attn_bias(mode, q0, 0, tq, tk)[1]
            ds = p * (dp_scr[slot] - delta)
            dq_scr[...] += _dot(kt_ref[0, kb], ds, "nn")
            dk_ref[0, pl.ds(k0, tk), :] += _dot(ds, qs, "nn")
            dv_ref[0, pl.ds(k0, tk), :] += _dot(p, dos, "nn")

        _pipelined_key_blocks(mode, q0, tq, tk, produce, consume)
        for g in range(GRP):
            dq_ref[:, g * HD:(g + 1) * HD] = (dq_scr[:, g * tq:(g + 1) * tq] * SCALE).T

    kv_spec = pl.BlockSpec((1, Sk, HD), lambda h, i: (h, 0, 0))
    qo_spec = pl.BlockSpec((tq, GRP * HD), lambda h, i: (i, h))
    row_spec = pl.BlockSpec((1, GRP, tq), lambda h, i: (h, 0, i))
    ins = [q, k, kt, v, o, lse, dy, gate]
    specs = [pl.BlockSpec((tq, GRP * HD), lambda h, i: (i, qcol0 + h)), pl.BlockSpec((1, Sk, k.shape[2]), lambda h, i: (h, 0, 0)),
             pl.BlockSpec((1, Sk // tk, HD, tk), lambda h, i: (h, 0, 0, 0)), kv_spec, qo_spec, row_spec,
             pl.BlockSpec((tq, GRP * HD), lambda h, i: (i, dycol0 + h)), row_spec]
    if mode == "sel":
        ins.append(selneg)
        specs.append(pl.BlockSpec((1, tq, selneg.shape[2]), lambda h, i: (h, i, 0)))
    return pl.pallas_call(
        body, name=name, grid=(N_KV, S // tq), in_specs=specs, out_specs=[qo_spec, kv_spec, kv_spec, row_spec],
        out_shape=[jax.ShapeDtypeStruct((S, ATT_WIDTH), F32), jax.ShapeDtypeStruct((N_KV, Sk, HD), F32),
                   jax.ShapeDtypeStruct((N_KV, Sk, HD), F32), jax.ShapeDtypeStruct((N_KV, GRP, S), F32)],
        scratch_shapes=[pltpu.VMEM((HD, R), F32), pltpu.VMEM((2, tk, R), F32), pltpu.VMEM((2, tk, R), F32)],
        compiler_params=_cp(("parallel", "arbitrary")))(*ins)


def _block_overlap(NC, NB):
    ci = np.arange(NC)[None, :] * 16
    sj = np.arange(NB)[:, None] * SEL_BLOCK
    ov_t = np.clip(np.minimum(ci + 32, sj + SEL_BLOCK) - np.maximum(ci, sj), 0, None) / 32.0
    ov_t[:, NC - 1] = 0.0
    return jnp.asarray(ov_t, F32)


def _chosen_blocks(p, ov_t, q0, tq):
    NB = ov_t.shape[0]
    imp4 = _dot(ov_t, p, "nn")
    imp = imp4[:, 0:tq] + imp4[:, tq:2 * tq] + imp4[:, 2 * tq:3 * tq] + imp4[:, 3 * tq:4 * tq]
    blk = lax.broadcasted_iota(jnp.int32, (NB, tq), 0)
    cur = lax.shift_right_logical(q0 + lax.broadcasted_iota(jnp.int32, (NB, tq), 1), 6)
    imp = jnp.where((blk == 0) | (blk == cur) | (blk == cur - 1), FORCE, imp)
    imp = jnp.where(blk <= cur, imp, -1.0)
    rank = jnp.zeros((NB, tq), F32)
    for j in range(NB):
        row = imp[j:j + 1, :]
        ahead = (row > imp) | ((row == imp) & (blk > j))
        rank = rank + ahead.astype(F32)
    chosen = (rank < float(N_SELECT)) & (imp >= 0.0)
    return jnp.where(chosen, 0.0, NEG).T


def _to_rows16(x):
    S = x.shape[0]
    return x.reshape(S // 16, 16, N_KV, HD).transpose(2, 0, 1, 3).reshape(N_KV, S // 16, 16 * HD)


def _from_rows16(r):
    NC = r.shape[1]
    return r.reshape(N_KV, NC, 16, HD).transpose(1, 2, 0, 3).reshape(NC * 16, N_KV * HD)


DT_COL0 = SSD_WIDTH + CONV_CH
GATE_IN_COL0 = D_IN - 3 * N_HEADS


SHARD_IN = D_IN // N_DEV


def _orig_cols(ref, c0, width):
    pieces, c = [], c0
    while c < c0 + width:
        d, off = divmod(c, SHARD_IN)
        w = min(SHARD_IN - off, c0 + width - c)
        pieces.append(ref[d, :, off:off + w])
        c += w
    return pieces[0] if len(pieces) == 1 else jnp.concatenate(pieces, axis=1)


def _cols_from_slabs(slabs):
    _, R, c = slabs.shape
    tr = _pick(R, (256, 128))

    def body(s_ref, o_ref):
        for t in range(N_DEV * c // LANE):
            pieces, col = [], t * LANE
            while col < (t + 1) * LANE:
                d, off = divmod(col, c)
                w = min(c - off, (t + 1) * LANE - col)
                pieces.append(s_ref[d, :, off:off + w])
                col += w
            o_ref[:, t * LANE:(t + 1) * LANE] = pieces[0] if len(pieces) == 1 else jnp.concatenate(pieces, axis=1)

    return pl.pallas_call(
        body, name="cols_from_slabs", grid=(R // tr,), in_specs=[pl.BlockSpec((N_DEV, tr, c), lambda i: (0, i, 0))],
        out_specs=pl.BlockSpec((tr, N_DEV * c), lambda i: (i, 0)), out_shape=jax.ShapeDtypeStruct((R, N_DEV * c), slabs.dtype),
        compiler_params=_cp(("parallel",)))(slabs)


def _slabs_from_cols(x):
    R, c = x.shape[0], x.shape[1] // N_DEV
    tr = _pick(R, (256, 128))

    def body(x_ref, o_ref):
        for d in range(N_DEV):
            o_ref[d] = x_ref[:, d * c:(d + 1) * c]

    return pl.pallas_call(
        body, name="slabs_from_cols", grid=(R // tr,), in_specs=[pl.BlockSpec((tr, N_DEV * c), lambda i: (i, 0))],
        out_specs=pl.BlockSpec((N_DEV, tr, c), lambda i: (0, i, 0)), out_shape=jax.ShapeDtypeStruct((N_DEV, R, c), x.dtype),
        compiler_params=_cp(("parallel",)))(x)


def _w_in_from_slabs(slabs):
    D = slabs.shape[1]
    tr = _pick(D, (256, 128))

    def body(s_ref, main_ref, small_ref):
        for t in range(W_MAIN // LANE):
            c = t * LANE
            main_ref[:, c:c + LANE] = _orig_cols(s_ref, c if c < DT_COL0 else c + SSD_HEADS, LANE)
        small_ref[...] = jnp.concatenate(
            [_orig_cols(s_ref, DT_COL0, SSD_HEADS), _orig_cols(s_ref, GATE_IN_COL0, 3 * N_HEADS),
             jnp.zeros((tr, W_SMALL - SSD_HEADS - 3 * N_HEADS), small_ref.dtype)], axis=1)

    return pl.pallas_call(
        body, name="w_in_layout", grid=(D // tr,), in_specs=[pl.BlockSpec((N_DEV, tr, SHARD_IN), lambda i: (0, i, 0))],
        out_specs=[pl.BlockSpec((tr, W_MAIN), lambda i: (i, 0)), pl.BlockSpec((tr, W_SMALL), lambda i: (i, 0))],
        out_shape=[jax.ShapeDtypeStruct((D, W_MAIN), slabs.dtype), jax.ShapeDtypeStruct((D, W_SMALL), slabs.dtype)],
        compiler_params=_cp(("parallel",)))(slabs)


def _w_in_to_slabs(main, small):
    D = main.shape[0]
    tr = _pick(D, (256, 128))
    ranges = [(0, DT_COL0, 0, 0), (DT_COL0, DT_COL0 + SSD_HEADS, 1, 0), (DT_COL0 + SSD_HEADS, GATE_IN_COL0, 0, DT_COL0),
              (GATE_IN_COL0, D_IN, 1, SSD_HEADS)]

    def body(main_ref, small_ref, o_ref):
        srcs = (main_ref, small_ref)
        for d in range(N_DEV):
            lo, hi = d * SHARD_IN, (d + 1) * SHARD_IN
            pieces = []
            for start, stop, which, s0 in ranges:
                a, b = max(lo, start), min(hi, stop)
                if a < b:
                    pieces.append(srcs[which][:, s0 + a - start:s0 + b - start].astype(o_ref.dtype))
            o_ref[d] = pieces[0] if len(pieces) == 1 else jnp.concatenate(pieces, axis=1)

    return pl.pallas_call(
        body, name="w_in_grad_layout", grid=(D // tr,),
        in_specs=[pl.BlockSpec((tr, W_MAIN), lambda i: (i, 0)), pl.BlockSpec((tr, W_SMALL), lambda i: (i, 0))],
        out_specs=pl.BlockSpec((N_DEV, tr, SHARD_IN), lambda i: (0, i, 0)),
        out_shape=jax.ShapeDtypeStruct((N_DEV, D, SHARD_IN), main.dtype), compiler_params=_cp(("parallel",)))(main, small)


QB, KCB, VCB, KSB, VSB, KWB, VWB = 10, 14, 15, 16, 17, 18, 19


def _col256(a, b):
    return a[:, b * 256:(b + 1) * 256]


_EARLY = ["w_in", "cmp_w1_k", "cmp_w1_v"]
_LATE = ["w_out", "w_gate", "w_up", "w_down"]
_FFN = ["w_down", "w_gate", "w_up"]
_MID = ["w_out"]
_LAST = ["cmp_w1_k", "cmp_w1_v", "w_in"]


def _local_step(x, tgt, p, late_weights=None, grads_ready=None):
    S = x.shape[0]
    cos, sin = _rope_tables(S)

    u, rs1 = _rms_fwd(x, p["attn_norm_w"], "attn_norm")
    proj = _mm(u, p["w_main"], "nn", F32, "in_proj", after=p.get("before_in_proj"))
    proj_small = _mm(u, p["w_small"], "nn", F32, "in_proj_small")
    xa = _conv_fwd(proj, p["conv_w"], p["conv_b"])
    y_ssd, y_pre, rs_ssd, hs = _ssd_fwd(proj, proj_small, xa, p["dt_bias"], p["a_log"], p["d_skip"], p["ssd_norm_w"])

    q_rot = _rope([proj], QB, ATT_WIDTH, cos, sin, 1.0, _MXU, "rope_q")
    kv = _kv_prep(proj, cos, sin, _attn_cfg(S, S, "sel")[1])
    rk, rv = _to_rows16(_col256(proj, KCB)), _to_rows16(_col256(proj, VCB))
    k_cmp, hid_k = _compress_fwd(rk, p["cmp_pe_k"], p["cmp_w1_k"], p["cmp_w2_k"])
    v_cmp, hid_v = _compress_fwd(rv, p["cmp_pe_v"], p["cmp_w1_v"], p["cmp_w2_v"])
    n_cmp = k_cmp.shape[1]

    gates = proj_small[:, SSD_HEADS:SSD_HEADS + 3 * N_HEADS].reshape(S, N_KV, GRP, 3).transpose(3, 1, 2, 0)
    o_cmp, lse_cmp, y_att, sel = _attn_fwd(proj, QB, k_cmp, _blocked_t(v_cmp, n_cmp), "cmp", None, gates[0], None, F32,
                                           "attn_cmp_fwd")
    o_sel, lse_sel, y_att = _attn_fwd(q_rot, 0, kv["ks_ext"], kv["vs_t"], "sel", sel, gates[1], y_att, F32, "attn_sel_fwd")
    o_win, lse_win, y_att = _attn_fwd(q_rot, 0, kv["kw"], kv["vw_t"], "win", None, gates[2], y_att, _MXU, "attn_win_fwd")

    if late_weights is not None:
        p = {**p, **late_weights(y_att)}
    mixed = jnp.concatenate([y_ssd, y_att], axis=1)
    h1 = _mm(mixed, p["w_out"], "nn", F32, "out_proj", res=x)
    v, rs_ffn = _rms_fwd(h1, p["ffn_norm_w"], "ffn_norm")
    gt, up, act = _ffn_up(v, p["w_gate"], p["w_up"])
    h2 = _mm(act, p["w_down"], "nn", F32, "ffn_down", res=h1)
    loss, dh2, dh2b, d_final_w = _final_loss(h2, p["final_norm_w"], tgt)

    def ready(names):
        return None if grads_ready is None else grads_ready(names, g)

    g = {"final_norm_w": d_final_w}
    g["w_down"] = _mm(act, dh2b, "tn", _MXU, "dw_down")
    dgt, dup = _ffn_dact(dh2b, p["w_down"], gt, up)
    g["w_gate"] = _mm(v, dgt, "tn", _MXU, "dw_gate")
    g["w_up"] = _mm(v, dup, "tn", _MXU, "dw_up")
    dv = _ffn_dv(dgt, dup, p["w_gate"], p["w_up"], ready(_FFN))
    dh1, dh1b, g["ffn_norm_w"] = _rms_bwd(dv, h1, rs_ffn, p["ffn_norm_w"], dh2, "ffn_norm_bwd")
    g["w_out"] = _mm(mixed, dh1b, "tn", _MXU, "dw_out")
    dmixed = _mm(dh1b, p["w_out"], "nt", F32, "dmixed", after=ready(_MID))

    dz, dxa, ddtr, g["dt_bias"], g["a_log"], g["d_skip"], g["ssd_norm_w"] = _ssd_bwd(
        dmixed, proj, proj_small, xa, y_pre, rs_ssd, hs, p["dt_bias"], p["a_log"], p["d_skip"], p["ssd_norm_w"])
    dxbc, g["conv_w"], g["conv_b"] = _conv_bwd(proj, p["conv_w"], p["conv_b"], dxa)

    dyb = SSD_WIDTH // (GRP * HD)
    dq_cmp, dk_cmp, dv_cmp, dg_cmp = _attn_bwd(proj, QB, k_cmp, _blocked_t(k_cmp, n_cmp), v_cmp, o_cmp, lse_cmp, dmixed, dyb,
                                               gates[0], "cmp", None, "attn_cmp_bwd")
    dq_sel, dks, dvs, dg_sel = _attn_bwd(q_rot, 0, kv["ks_ext"], kv["ks_t"], kv["vs"], o_sel, lse_sel, dmixed, dyb, gates[1], "sel",
                                         sel, "attn_sel_bwd")
    dq_win, dkw, dvw, dg_win = _attn_bwd(q_rot, 0, kv["kw"], kv["kw_t"], kv["vw"], o_win, lse_win, dmixed, dyb, gates[2], "win", None,
                                         "attn_win_bwd")
    dgate = jnp.stack([dg_cmp, dg_sel, dg_win]).transpose(3, 1, 2, 0).reshape(S, 3 * N_HEADS)
    drk, g["cmp_w1_k"], g["cmp_w2_k"], g["cmp_pe_k"] = _compress_bwd(rk, p["cmp_pe_k"], p["cmp_w1_k"], p["cmp_w2_k"], hid_k, dk_cmp)
    drv, g["cmp_w1_v"], g["cmp_w2_v"], g["cmp_pe_v"] = _compress_bwd(rv, p["cmp_pe_v"], p["cmp_w1_v"], p["cmp_w2_v"], hid_v, dv_cmp)
    dq = _rope([dq_sel, dq_win], 0, ATT_WIDTH, cos, sin, -1.0, _MXU, "rope_dq", extra=(dq_cmp, 0))
    dkv = _dkv_post(dks, dvs, dkw, dvw, cos, sin)
    dproj = jnp.concatenate([dz, dxbc, dq] + [t.astype(_MXU) for t in (_from_rows16(drk), _from_rows16(drv))] + [dkv], axis=1)
    dsmall = jnp.concatenate([ddtr, dgate, jnp.zeros((S, W_SMALL - SSD_HEADS - 3 * N_HEADS), F32)], axis=1).astype(_MXU)
    g["w_main"] = _mm(u, dproj, "tn", _MXU, "dw_in")
    g["w_small"] = _mm(u, dsmall, "tn", F32, "dw_in_small")
    du = _mm(dproj, p["w_main"], "nt", F32, "du_main", after=ready(_LAST))
    du = _mm(dsmall, p["w_small"], "nt", F32, "du_small", res=du)
    grad_x, _, g["attn_norm_w"] = _rms_bwd(du, x, rs1, p["attn_norm_w"], dh1, "attn_norm_bwd")
    return loss, grad_x, g


MESH_ID = pl.DeviceIdType.MESH


def _my_coords():
    return lax.axis_index("x"), lax.axis_index("y"), lax.axis_index("c")


def _flat_id(px, py, pc):
    return 4 * px + 2 * py + pc


def _peer(k):
    mx, my, mc = _my_coords()
    return (1 - mx if k & 4 else mx, 1 - my if k & 2 else my, 1 - mc if k & 1 else mc)


def _exchange(arrs, scatter, name, after=()):
    n, na = len(arrs), len(after)
    scatter = [scatter] * n if isinstance(scatter, bool) else list(scatter)

    def body(*refs):
        ins, outs = refs[:n], refs[n + na:2 * n + na]
        send_sems, recv_sems, local_sems = refs[2 * n + na:]
        me = _flat_id(*_my_coords())
        copies = []
        for i in range(n):
            src_me = ins[i].at[me] if scatter[i] else ins[i]
            local = pltpu.make_async_copy(src_me, outs[i].at[me], local_sems.at[i])
            local.start()
            copies.append(local)
        for k in range(1, N_DEV):
            peer = _peer(k)
            for i in range(n):
                src = ins[i].at[_flat_id(*peer)] if scatter[i] else ins[i]
                cp = pltpu.make_async_remote_copy(src_ref=src, dst_ref=outs[i].at[me], send_sem=send_sems.at[i * 7 + k - 1],
                                                  recv_sem=recv_sems.at[i * 7 + k - 1], device_id=peer, device_id_type=MESH_ID)
                cp.start()
                copies.append(cp)
        for cp in copies:
            cp.wait()

    any_spec = pl.BlockSpec(memory_space=pl.ANY)
    out_shape = [jax.ShapeDtypeStruct(a.shape if sc else (N_DEV,) + a.shape, a.dtype) for a, sc in zip(arrs, scatter)]
    return pl.pallas_call(
        body, name=name, in_specs=[any_spec] * (n + na), out_specs=[any_spec] * n, out_shape=out_shape,
        scratch_shapes=[pltpu.SemaphoreType.DMA((n * 7,)), pltpu.SemaphoreType.DMA((n * 7,)), pltpu.SemaphoreType.DMA((n,))],
        compiler_params=pltpu.CompilerParams(has_side_effects=True))(*arrs, *after)


def _gather_two_level(arrs, name):
    n = len(arrs)

    def body(*refs):
        ins, outs = refs[:n], refs[n:2 * n]
        send_sems, recv_sems, local_sems = refs[2 * n:]
        x, y, c = _my_coords()
        me, sibling = (x, y, c), (x, y, 1 - c)
        chips = [(1 - x, y), (x, 1 - y), (1 - x, 1 - y)]

        def copy(i, k, block, to, src=None):
            slot = outs[i].at[_flat_id(*block)]
            return pltpu.make_async_remote_copy(src_ref=slot if src is None else src, dst_ref=slot, send_sem=send_sems.at[i * 7 + k],
                                                recv_sem=recv_sems.at[i * 7 + k], device_id=to, device_id_type=MESH_ID)

        mine = [pltpu.make_async_copy(ins[i], outs[i].at[_flat_id(*me)], local_sems.at[i]) for i in range(n)]
        for cp in mine:
            cp.start()
        first = []
        for j, chip in enumerate(chips):
            first += [copy(i, 1 + j, me, (*chip, c), src=ins[i]) for i in range(n)]
        first += [copy(i, 0, me, sibling, src=ins[i]) for i in range(n)]
        for cp in first:
            cp.start()
        passed = []
        for j, chip in enumerate(chips):
            for i in range(n):
                copy(i, 1 + j, (*chip, c), me).wait_recv()
                passed.append(copy(i, 4 + j, (*chip, c), sibling))
                passed[-1].start()
        for i in range(n):
            copy(i, 0, sibling, me).wait_recv()
        for j, chip in enumerate(chips):
            for i in range(n):
                copy(i, 4 + j, (*chip, 1 - c), me).wait_recv()
        for cp in first + passed:
            cp.wait_send()
        for cp in mine:
            cp.wait()

    any_spec = pl.BlockSpec(memory_space=pl.ANY)
    return pl.pallas_call(
        body, name=name, in_specs=[any_spec] * n, out_specs=[any_spec] * n,
        out_shape=[jax.ShapeDtypeStruct((N_DEV,) + a.shape, a.dtype) for a in arrs],
        scratch_shapes=[pltpu.SemaphoreType.DMA((n * 7,)), pltpu.SemaphoreType.DMA((n * 7,)), pltpu.SemaphoreType.DMA((n,))],
        compiler_params=pltpu.CompilerParams(has_side_effects=True))(*arrs)


_HBM = pl.BlockSpec(memory_space=pltpu.HBM)
_SEM = pl.BlockSpec(memory_space=pltpu.SEMAPHORE)
_EFFECT = pltpu.SideEffectType.DATAFLOW_SIDE_EFFECTING


def _split_copies(ins, lands, send_sems, recv_sems, own_sems, scatter):
    me = _flat_id(*_my_coords())
    remote = []
    for k in range(1, N_DEV):
        peer = _peer(k)
        for i in range(len(ins)):
            src = ins[i].at[_flat_id(*peer)] if scatter else ins[i]
            remote.append(pltpu.make_async_remote_copy(src_ref=src, dst_ref=lands[i].at[me], send_sem=send_sems.at[i * 7 + k - 1],
                                                       recv_sem=recv_sems.at[i * 7 + k - 1], device_id=peer, device_id_type=MESH_ID))
    own = [pltpu.make_async_copy(ins[i].at[me] if scatter else ins[i], lands[i].at[me], own_sems.at[i]) for i in range(len(ins))]
    return remote, own


def _split_start(arrs, scatter, name, after=()):
    n, na = len(arrs), len(after)

    def body(*refs):
        remote, own = _split_copies(refs[:n], refs[n:2 * n], refs[2 * n + na], refs[2 * n + na + 1], refs[2 * n + na + 2], scatter)
        for cp in remote + own:
            cp.start()
        refs[-1][...] = jnp.zeros_like(refs[-1])

    land_shapes = [a.shape if scatter else (N_DEV,) + a.shape for a in arrs]
    out_shape = ((pltpu.SemaphoreType.DMA((n * 7,)), pltpu.SemaphoreType.DMA((n * 7,)), pltpu.SemaphoreType.DMA((n,)))
                 + tuple(pltpu.HBM(a.shape, a.dtype) for a in arrs) + tuple(pltpu.HBM(s, a.dtype) for s, a in zip(land_shapes, arrs))
                 + (jax.ShapeDtypeStruct((8, 128), F32),))
    operands = ([pltpu.with_memory_space_constraint(a, pltpu.HBM) for a in arrs]
                + [pltpu.with_memory_space_constraint(lax.empty(s, a.dtype), pltpu.HBM) for s, a in zip(land_shapes, arrs)])
    res = pl.pallas_call(
        body, name=name, out_shape=out_shape, in_specs=[_HBM] * (2 * n) + [pl.BlockSpec(memory_space=pl.ANY)] * na,
        out_specs=(_SEM, _SEM, _SEM) + (_HBM,) * (2 * n) + (pl.BlockSpec(memory_space=pltpu.VMEM),),
        input_output_aliases={i: 3 + i for i in range(2 * n)},
        compiler_params=pltpu.CompilerParams(has_side_effects=_EFFECT))(*operands, *after)
    return dict(send=res[0], recv=res[1], own=res[2], ins=list(res[3:3 + n]), lands=list(res[3 + n:3 + 2 * n]), token=res[-1])


def _split_wait(st, scatter, after, name):
    n = len(st["ins"])

    def body(*refs):
        remote, own = _split_copies(refs[:n], refs[n:2 * n], refs[2 * n], refs[2 * n + 1], refs[2 * n + 2], scatter)
        for cp in remote:
            cp.wait_send()
            cp.wait_recv()
        for cp in own:
            cp.wait()

    arrs = st["ins"] + st["lands"]
    res = pl.pallas_call(
        body, name=name, out_shape=tuple(pltpu.HBM(a.shape, a.dtype) for a in arrs),
        in_specs=[_HBM] * (2 * n) + [_SEM, _SEM, _SEM] + [pl.BlockSpec(memory_space=pl.ANY)] * len(after), out_specs=(_HBM,) * (2 * n),
        input_output_aliases={i: i for i in range(2 * n)},
        compiler_params=pltpu.CompilerParams(has_side_effects=_EFFECT))(*arrs, st["send"], st["recv"], st["own"], *after)
    return list(res[n:])


def _adam_step(p_ref, w_ref, m_ref, v_ref, g_ref, d_ref, nm_ref, nv_ref):
    g = p_ref[0].astype(F32)
    for j in range(1, p_ref.shape[0]):
        g = g + p_ref[j].astype(F32)
    g_ref[...] = g
    nm = ADAM_B1 * m_ref[...] + (1.0 - ADAM_B1) * g
    nv = ADAM_B2 * v_ref[...] + (1.0 - ADAM_B2) * (g * g)
    nm_ref[...] = nm
    nv_ref[...] = nv
    m_hat = nm / (1.0 - ADAM_B1 ** ADAM_STEP)
    v_hat = nv / (1.0 - ADAM_B2 ** ADAM_STEP)
    d_ref[...] = -ADAM_LR * (m_hat / (jnp.sqrt(v_hat) + ADAM_EPS) + ADAM_WD * w_ref[...])


def _adam_sum(parts, w, m, v, name):
    P, R, C = parts.shape
    tr = _pick(R, (256, 128, 64, 32, 8)) if C <= 1024 else _pick(R, (128, 64, 32, 8))
    blk = pl.BlockSpec((tr, C), lambda i: (i, 0))
    return pl.pallas_call(
        functools.partial(_adam_step), name=name, grid=(R // tr,),
        in_specs=[pl.BlockSpec((P, tr, C), lambda i: (0, i, 0)), blk, blk, blk],
        out_specs=[blk] * 4, out_shape=[jax.ShapeDtypeStruct((R, C), F32)] * 4, compiler_params=_cp(("parallel",)))(parts, w, m, v)


def _adam_small(loss_parts, parts, ws, ms, vs):
    n = len(parts)

    def body(*refs):
        loss_ref, ins, outs, total_ref = refs[0], refs[1:4 * n + 1], refs[4 * n + 1:-1], refs[-1]
        for i in range(n):
            _adam_step(ins[i], ins[n + i], ins[2 * n + i], ins[3 * n + i], *outs[4 * i:4 * i + 4])
        total = loss_ref[0]
        for d in range(1, N_DEV):
            total = total + loss_ref[d]
        total_ref[...] = total

    out_shape = [jax.ShapeDtypeStruct(w.shape, F32) for w in ws for _ in range(4)] + [jax.ShapeDtypeStruct(loss_parts.shape[1:], F32)]
    res = pl.pallas_call(body, name="adam_small", out_shape=out_shape)(loss_parts, *parts, *ws, *ms, *vs)
    return res[-1], [tuple(res[4 * i:4 * i + 4]) for i in range(n)]


_WEIGHTS = ["attn_norm_w", "w_in", "conv_w", "conv_b", "dt_bias", "a_log", "d_skip", "ssd_norm_w", "cmp_w1_k", "cmp_w2_k",
            "cmp_w1_v", "cmp_w2_v", "cmp_pe_k", "cmp_pe_v", "w_out", "ffn_norm_w", "w_gate", "w_up", "w_down", "final_norm_w"]
_BIG = ["w_in", "w_gate", "w_up", "w_down", "w_out", "cmp_w1_k", "cmp_w1_v"]
_COL_SHARDED = ("w_in", "w_gate", "w_up")
_REPLICATED = ["attn_norm_w", "conv_b", "dt_bias", "a_log", "d_skip", "ssd_norm_w", "cmp_pe_k", "cmp_pe_v", "ffn_norm_w",
               "final_norm_w"]
_SMALL_SHARDED = ["conv_w", "cmp_w2_k", "cmp_w2_v"]


def _cols_to_slabs(g):
    R = g.shape[0]
    return g.reshape(R, N_DEV, -1).transpose(1, 0, 2)


def _slabs_to_cols(s):
    return s.transpose(1, 0, 2).reshape(s.shape[1], -1)


def kernel(x, attn_norm_w, w_in, conv_w, conv_b, dt_bias, a_log, d_skip, ssd_norm_w, cmp_w1_k, cmp_w2_k, cmp_w1_v, cmp_w2_v, cmp_pe_k, cmp_pe_v, w_out, ffn_norm_w, w_gate, w_up, w_down, final_norm_w, loss_target, m_attn_norm_w, m_w_in, m_conv_w, m_conv_b, m_dt_bias, m_a_log, m_d_skip, m_ssd_norm_w, m_cmp_w1_k, m_cmp_w2_k, m_cmp_w1_v, m_cmp_w2_v, m_cmp_pe_k, m_cmp_pe_v, m_w_out, m_ffn_norm_w, m_w_gate, m_w_up, m_w_down, m_final_norm_w, v_attn_norm_w, v_w_in, v_conv_w, v_conv_b, v_dt_bias, v_a_log, v_d_skip, v_ssd_norm_w, v_cmp_w1_k, v_cmp_w2_k, v_cmp_w1_v, v_cmp_w2_v, v_cmp_pe_k, v_cmp_pe_v, v_w_out, v_ffn_norm_w, v_w_gate, v_w_up, v_w_down, v_final_norm_w):
    a = dict(locals())

    shard = {n: a[n][0].astype(_MXU) for n in _BIG}
    got = _gather_two_level([shard[n] for n in _EARLY] + [cmp_w2_k[0], cmp_w2_v[0], conv_w[0]], "gather_early")
    st_late = _split_start([shard[n] for n in _LATE], False, "gather_late_start", after=(got[0],))

    def assemble(n, t):
        return _cols_from_slabs(t) if n in _COL_SHARDED else t.reshape(-1, t.shape[-1])

    p = dict(attn_norm_w=attn_norm_w, conv_b=conv_b, dt_bias=dt_bias, a_log=a_log, d_skip=d_skip, ssd_norm_w=ssd_norm_w,
             cmp_pe_k=cmp_pe_k.reshape(1, -1), cmp_pe_v=cmp_pe_v.reshape(1, -1), ffn_norm_w=ffn_norm_w,
             final_norm_w=final_norm_w.reshape(1, -1))

    w_main, w_small = _w_in_from_slabs(got[0])
    p.update(before_in_proj=st_late["token"],
             w_main=w_main, w_small=w_small, cmp_w1_k=assemble("cmp_w1_k", got[1]), cmp_w1_v=assemble("cmp_w1_v", got[2]),
             cmp_w2_k=assemble("cmp_w2_k", got[3]).astype(_MXU), cmp_w2_v=assemble("cmp_w2_v", got[4]).astype(_MXU),
             conv_w=_slabs_to_cols(got[5]))

    def late_weights(after):
        got_late = _split_wait(st_late, False, (after,), "gather_late_wait")
        return {n: assemble(n, t) for n, t in zip(_LATE, got_late)}

    def slabs_of(g, n):
        if n == "w_in":
            return _w_in_to_slabs(g["w_main"], g["w_small"])
        return _slabs_from_cols(g[n]) if n in _COL_SHARDED else g[n].reshape(N_DEV, -1, g[n].shape[-1])

    started = []

    def grads_ready(names, g):
        started.append((names, _split_start([slabs_of(g, n) for n in names], True, "scatter_grads_start_%d" % len(started))))
        return started[-1][1]["token"]

    loss_part, grad_x, g = _local_step(x[0], loss_target[0], p, late_weights, grads_ready)

    out, after = {}, (started[-1][1]["token"],)
    for i, (names, st) in enumerate(started):
        if i == len(started) - 1:
            after = after + (grad_x,)
        received = _split_wait(st, True, after, "scatter_grads_wait_%d" % i)
        for n, parts in zip(names, received):
            out[n] = _adam_sum(parts, a[n][0], a["m_" + n][0], a["v_" + n][0], "adam_" + n)
        after = (out[names[-1]][0],)

    small_names = _REPLICATED + _SMALL_SHARDED
    partials = [g[n] for n in _REPLICATED] + [_cols_to_slabs(g["conv_w"])] + [
        g[n].reshape(N_DEV, -1, g[n].shape[-1]) for n in ("cmp_w2_k", "cmp_w2_v")]
    gathered = _exchange([loss_part] + partials, [False] * (1 + len(_REPLICATED)) + [True] * len(_SMALL_SHARDED),
                         "exchange_small_grads", after=(received[0],))
    shapes2d = [t.shape[1:] for t in gathered[1:]]
    loss, res_small = _adam_small(gathered[0], gathered[1:],
                                  *[[a[pre + n].reshape(s) for n, s in zip(small_names, shapes2d)] for pre in ("", "m_", "v_")])
    for n, r in zip(small_names, res_small):
        out[n] = r

    outs = [loss[0, 0], grad_x[None]]
    for j in range(4):
        for n in _WEIGHTS:
            outs.append(out[n][j].reshape(a[n].shape))
    return tuple(outs)
```

```python
import functools

import numpy as np
import jax
import jax.numpy as jnp
from jax import lax
from jax.experimental import pallas as pl
from jax.experimental.pallas import tpu as pltpu

F32 = jnp.float32
_MXU = jnp.bfloat16

N_DEV = 8
SSD_WIDTH = 1024
ATT_WIDTH = 1024
SSD_HEADS = 16
SSD_P = 64
SSD_N = 128
SSD_L = 128
SSD_G = 2
CONV_CH = 1536
CONV_K = 4
HD = 64
N_HEADS = 16
N_KV = 4
GRP = 4
CMP_HID = 256
SEL_BLOCK = 64
N_SELECT = 16
WINDOW = 512
ROPE_DIM = 16
ROPE_THETA = 500000.0
EPS = 1e-6
NEG = -1e30
FORCE = 1e4
SCALE = HD ** -0.5
D_IN = 5184
W_MAIN = 5120
W_SMALL = 128
VMEM_LIMIT = 52 * 1024 * 1024

ADAM_LR, ADAM_B1, ADAM_B2, ADAM_EPS, ADAM_WD, ADAM_STEP = 0.001, 0.9, 0.999, 1e-08, 0.01, 10


def _pick(n, cands):
    for c in cands:
        if n % c == 0:
            return c
    return n


def _cp(sem=None):
    return pltpu.CompilerParams(dimension_semantics=sem, vmem_limit_bytes=VMEM_LIMIT)


def _sigmoid(x):
    return 1.0 / (1.0 + jnp.exp(-x))


def _dot(a, b, dims, split=None):
    dn = {"nn": (((1,), (0,)), ((), ())), "nt": (((1,), (1,)), ((), ())), "tn": (((0,), (0,)), ((), ()))}[dims]
    mm = lambda x, y: lax.dot_general(x.astype(_MXU), y.astype(_MXU), dn, preferred_element_type=F32)
    if split is None:
        return mm(a, b)
    x = (a if split == "a" else b).astype(F32)
    hi = x.astype(_MXU)
    lo = x - hi.astype(F32)
    return mm(hi, b) + mm(lo, b) if split == "a" else mm(a, hi) + mm(a, lo)


LANE = 128
MM_TILE = 1024
MM_K_WHOLE = 2048
MM_K_STEP = 2816
TN_ACC_ELEMS = 3 * 2 ** 20
TN_K_STEP = 1024


def _largest_tile(n, cap):
    if n <= cap:
        return n
    best = LANE
    for t in range(LANE, cap + 1, LANE):
        if n % t == 0:
            best = t
    return best


def _mm_tiles(mode, M, N, K):
    if mode == "tn":
        tm = _largest_tile(M, 2 * MM_TILE)
        return tm, _largest_tile(N, TN_ACC_ELEMS // tm), _largest_tile(K, TN_K_STEP)
    tk = K if K <= MM_K_WHOLE else _largest_tile(K, MM_K_STEP)
    return _largest_tile(M, MM_TILE), _largest_tile(N, MM_TILE), tk


def _mm(a, b, mode, out_dtype, name, res=None, after=None):
    if mode == "nn":
        (M, K), N = a.shape, b.shape[1]
    elif mode == "nt":
        (M, K), N = a.shape, b.shape[0]
    else:
        (K, M), N = a.shape, b.shape[1]
    tm, tn, tk = _mm_tiles(mode, M, N, K)
    nk = K // tk
    a_spec = pl.BlockSpec((tk, tm), lambda i, j, k: (k, i)) if mode == "tn" else pl.BlockSpec((tm, tk), lambda i, j, k: (i, k))
    b_spec = pl.BlockSpec((tn, tk), lambda i, j, k: (j, k)) if mode == "nt" else pl.BlockSpec((tk, tn), lambda i, j, k: (k, j))
    o_spec = pl.BlockSpec((tm, tn), lambda i, j, k: (i, j))

    def finish(r, r_ref, o_ref):
        if res is not None:
            r = r + r_ref[...].astype(F32)
        o_ref[...] = r.astype(out_dtype)

    def body_one_step(*refs):
        a_ref, b_ref, o_ref = refs[0], refs[1], refs[-1]
        finish(_dot(a_ref[...], b_ref[...], mode), refs[2], o_ref)

    def body(*refs):
        a_ref, b_ref, o_ref, acc = refs[0], refs[1], refs[-2], refs[-1]
        k = pl.program_id(2)

        @pl.when(k == 0)
        def _():
            acc[...] = jnp.zeros_like(acc)

        acc[...] += _dot(a_ref[...], b_ref[...], mode)

        @pl.when(k == nk - 1)
        def _():
            finish(acc[...], refs[2], o_ref)

    ins, specs = [a, b], [a_spec, b_spec]
    if res is not None:
        ins.append(res)
        specs.append(o_spec)
    if after is not None:
        ins.append(after)
        specs.append(pl.BlockSpec(memory_space=pl.ANY))
    return pl.pallas_call(
        body_one_step if nk == 1 else body, name=name, grid=(M // tm, N // tn, nk), in_specs=specs, out_specs=o_spec,
        out_shape=jax.ShapeDtypeStruct((M, N), out_dtype), scratch_shapes=[] if nk == 1 else [pltpu.VMEM((tm, tn), F32)],
        compiler_params=_cp(("parallel", "parallel", "arbitrary")))(*ins)


def _ffn_up(v, w_gate, w_up):
    S, D = v.shape
    F = w_gate.shape[1]
    tm, tn = _largest_tile(S, MM_TILE), _largest_tile(F, MM_TILE // 2)

    def body(v_ref, wg_ref, wu_ref, gt_ref, up_ref, act_ref):
        vv = v_ref[...]
        g = _dot(vv, wg_ref[...], "nn")
        u = _dot(vv, wu_ref[...], "nn")
        gt_ref[...] = g.astype(gt_ref.dtype)
        up_ref[...] = u.astype(up_ref.dtype)
        act_ref[...] = (g * _sigmoid(g) * u).astype(act_ref.dtype)

    o_spec = pl.BlockSpec((tm, tn), lambda i, j: (i, j))
    w_spec = pl.BlockSpec((D, tn), lambda i, j: (0, j))
    return pl.pallas_call(
        body, name="ffn_up", grid=(S // tm, F // tn),
        in_specs=[pl.BlockSpec((tm, D), lambda i, j: (i, 0)), w_spec, w_spec], out_specs=[o_spec, o_spec, o_spec],
        out_shape=[jax.ShapeDtypeStruct((S, F), _MXU)] * 3,
        compiler_params=_cp(("parallel", "parallel")))(v, w_gate, w_up)


def _ffn_dv(dgt, dup, w_gate, w_up, after):
    S, F = dgt.shape
    D = w_gate.shape[0]
    tm, tn, _ = _mm_tiles("nt", S, D, F)
    tk = _largest_tile(F, MM_K_STEP // 2)
    nk = F // tk

    def body(g_ref, u_ref, wg_ref, wu_ref, *rest):
        o_ref, acc = rest[-2], rest[-1]
        k = pl.program_id(2)

        @pl.when(k == 0)
        def _():
            acc[...] = jnp.zeros_like(acc)

        acc[...] += _dot(g_ref[...], wg_ref[...], "nt") + _dot(u_ref[...], wu_ref[...], "nt")

        @pl.when(k == nk - 1)
        def _():
            o_ref[...] = acc[...]

    a_spec = pl.BlockSpec((tm, tk), lambda i, j, k: (i, k))
    w_spec = pl.BlockSpec((tn, tk), lambda i, j, k: (j, k))
    ins, specs = [dgt, dup, w_gate, w_up], [a_spec, a_spec, w_spec, w_spec]
    if after is not None:
        ins.append(after)
        specs.append(pl.BlockSpec(memory_space=pl.ANY))
    return pl.pallas_call(
        body, name="ffn_dv", grid=(S // tm, D // tn, nk), in_specs=specs, out_specs=pl.BlockSpec((tm, tn), lambda i, j, k: (i, j)),
        out_shape=jax.ShapeDtypeStruct((S, D), F32), scratch_shapes=[pltpu.VMEM((tm, tn), F32)],
        compiler_params=_cp(("parallel", "parallel", "arbitrary")))(*ins)


def _ffn_dact(dh2, w_down, gt, up):
    S, D = dh2.shape
    F = w_down.shape[0]
    tm, tn = _largest_tile(S, MM_TILE), _largest_tile(F, MM_TILE // 2)

    def body(d_ref, w_ref, gt_ref, up_ref, dg_ref, du_ref):
        da, g, u = _dot(d_ref[...], w_ref[...], "nt"), gt_ref[...].astype(F32), up_ref[...].astype(F32)
        s = _sigmoid(g)
        dg_ref[...] = (da * u * (s * (1.0 + g * (1.0 - s)))).astype(dg_ref.dtype)
        du_ref[...] = (da * (g * s)).astype(du_ref.dtype)

    o_spec = pl.BlockSpec((tm, tn), lambda i, j: (i, j))
    return pl.pallas_call(
        body, name="ffn_dact", grid=(S // tm, F // tn),
        in_specs=[pl.BlockSpec((tm, D), lambda i, j: (i, 0)), pl.BlockSpec((tn, D), lambda i, j: (j, 0)), o_spec, o_spec],
        out_specs=[o_spec, o_spec],
        out_shape=[jax.ShapeDtypeStruct((S, F), _MXU), jax.ShapeDtypeStruct((S, F), _MXU)],
        compiler_params=_cp(("parallel", "parallel")))(dh2, w_down, gt, up)


def _rms_fwd(x, w, name):
    S, D = x.shape
    tr = _pick(S, (256, 128))

    def body(x_ref, w_ref, xn_ref, rs_ref):
        xv = x_ref[...]
        rs = lax.rsqrt(jnp.mean(xv * xv, axis=-1, keepdims=True) + EPS)
        xn_ref[...] = ((xv * rs) * w_ref[...]).astype(xn_ref.dtype)
        rs_ref[...] = rs

    return pl.pallas_call(
        body, name=name, grid=(S // tr,),
        in_specs=[pl.BlockSpec((tr, D), lambda i: (i, 0)), pl.BlockSpec((1, D), lambda i: (0, 0))],
        out_specs=[pl.BlockSpec((tr, D), lambda i: (i, 0)), pl.BlockSpec((tr, 1), lambda i: (i, 0))],
        out_shape=[jax.ShapeDtypeStruct((S, D), _MXU), jax.ShapeDtypeStruct((S, 1), F32)],
        compiler_params=_cp(("parallel",)))(x, w)


def _rms_bwd(dyn, x, rs, w, res, name):
    S, D = x.shape
    tr = _pick(S, (256, 128))

    def body(dy_ref, x_ref, rs_ref, w_ref, res_ref, dx_ref, dxb_ref, dw_ref):
        @pl.when(pl.program_id(0) == 0)
        def _():
            dw_ref[...] = jnp.zeros_like(dw_ref)

        dy, r = dy_ref[...].astype(F32), rs_ref[...]
        xhat = x_ref[...] * r
        dw_ref[...] += jnp.sum(dy * xhat, axis=0, keepdims=True)
        dxhat = dy * w_ref[...]
        dx = res_ref[...] + r * (dxhat - xhat * jnp.mean(dxhat * xhat, axis=-1, keepdims=True))
        dx_ref[...] = dx
        dxb_ref[...] = dx.astype(dxb_ref.dtype)

    row = pl.BlockSpec((tr, D), lambda i: (i, 0))
    vec = pl.BlockSpec((1, D), lambda i: (0, 0))
    return pl.pallas_call(
        body, name=name, grid=(S // tr,),
        in_specs=[row, row, pl.BlockSpec((tr, 1), lambda i: (i, 0)), vec, row], out_specs=[row, row, vec],
        out_shape=[jax.ShapeDtypeStruct((S, D), F32), jax.ShapeDtypeStruct((S, D), _MXU), jax.ShapeDtypeStruct((1, D), F32)],
        compiler_params=_cp(("arbitrary",)))(dyn, x, rs, w, res)


def _final_loss(h2, w, tgt):
    S, D = h2.shape
    tr = _pick(S, (256, 128))

    def body(h_ref, w_ref, t_ref, loss_ref, dh_ref, dhb_ref, dw_ref):
        @pl.when(pl.program_id(0) == 0)
        def _():
            dw_ref[...] = jnp.zeros_like(dw_ref)
            loss_ref[...] = jnp.zeros_like(loss_ref)

        hv, wv = h_ref[...], w_ref[...]
        rs = lax.rsqrt(jnp.mean(hv * hv, axis=-1, keepdims=True) + EPS)
        xhat = hv * rs
        err = xhat * wv - t_ref[...]
        row = jnp.mean(err * err, axis=-1, keepdims=True)
        loss_ref[...] += jnp.broadcast_to(0.5 * jnp.sum(row, axis=0, keepdims=True), loss_ref.shape)
        dy = err * (1.0 / D)
        dw_ref[...] += jnp.sum(dy * xhat, axis=0, keepdims=True)
        dxhat = dy * wv
        dh = rs * (dxhat - xhat * jnp.mean(dxhat * xhat, axis=-1, keepdims=True))
        dh_ref[...] = dh
        dhb_ref[...] = dh.astype(dhb_ref.dtype)

    row = pl.BlockSpec((tr, D), lambda i: (i, 0))
    vec = pl.BlockSpec((1, D), lambda i: (0, 0))
    return pl.pallas_call(
        body, name="final_loss", grid=(S // tr,), in_specs=[row, vec, row],
        out_specs=[pl.BlockSpec((1, LANE), lambda i: (0, 0)), row, row, vec],
        out_shape=[jax.ShapeDtypeStruct((1, LANE), F32), jax.ShapeDtypeStruct((S, D), F32), jax.ShapeDtypeStruct((S, D), _MXU),
                   jax.ShapeDtypeStruct((1, D), F32)],
        compiler_params=_cp(("arbitrary",)))(h2, w, tgt)


def _shift_rows(x, k, rows):
    if k == 0:
        return x
    S = x.shape[0]
    r = pltpu.roll(x, k % S, axis=0)
    ok = (rows >= k) if k > 0 else (rows < S + k)
    return jnp.where(ok, r, 0.0)


XBC_COL0 = SSD_WIDTH // 128


def _conv_fwd(proj, conv_w, conv_b):
    S = proj.shape[0]
    nct = CONV_CH // 128

    def body(x_ref, w_ref, b_ref, o_ref):
        x = x_ref[...]
        rows = lax.broadcasted_iota(jnp.int32, x.shape, 0)
        c = b_ref[...] + w_ref[3:4, :] * x
        for k in range(1, CONV_K):
            c = c + w_ref[3 - k:4 - k, :] * _shift_rows(x, k, rows)
        o_ref[...] = c * _sigmoid(c)

    return pl.pallas_call(
        body, name="conv_fwd", grid=(nct,),
        in_specs=[pl.BlockSpec((S, 128), lambda j: (0, XBC_COL0 + j)), pl.BlockSpec((CONV_K, 128), lambda j: (0, j)),
                  pl.BlockSpec((1, 128), lambda j: (0, j))],
        out_specs=pl.BlockSpec((S, 128), lambda j: (0, j)),
        out_shape=jax.ShapeDtypeStruct((S, CONV_CH), F32), compiler_params=_cp(("parallel",)))(proj, conv_w, conv_b)


def _conv_bwd(proj, conv_w, conv_b, dxa):
    S = proj.shape[0]
    nct = CONV_CH // 128

    def body(x_ref, w_ref, b_ref, d_ref, dx_ref, dw_ref, db_ref):
        x = x_ref[...]
        rows = lax.broadcasted_iota(jnp.int32, x.shape, 0)
        xs = [_shift_rows(x, k, rows) for k in range(CONV_K)]
        c = b_ref[...] + w_ref[3:4, :] * x
        for k in range(1, CONV_K):
            c = c + w_ref[3 - k:4 - k, :] * xs[k]
        s = _sigmoid(c)
        dc = d_ref[...] * (s * (1.0 + c * (1.0 - s)))
        dx = w_ref[3:4, :] * dc
        for k in range(1, CONV_K):
            dx = dx + w_ref[3 - k:4 - k, :] * _shift_rows(dc, -k, rows)
        dx_ref[...] = dx.astype(dx_ref.dtype)
        for k in range(CONV_K):
            dw_ref[3 - k:4 - k, :] = jnp.sum(dc * xs[k], axis=0, keepdims=True)
        db_ref[...] = jnp.sum(dc, axis=0, keepdims=True)

    col = pl.BlockSpec((S, 128), lambda j: (0, j))
    return pl.pallas_call(
        body, name="conv_bwd", grid=(nct,),
        in_specs=[pl.BlockSpec((S, 128), lambda j: (0, XBC_COL0 + j)), pl.BlockSpec((CONV_K, 128), lambda j: (0, j)),
                  pl.BlockSpec((1, 128), lambda j: (0, j)), col],
        out_specs=[col, pl.BlockSpec((CONV_K, 128), lambda j: (0, j)), pl.BlockSpec((1, 128), lambda j: (0, j))],
        out_shape=[jax.ShapeDtypeStruct((S, CONV_CH), _MXU), jax.ShapeDtypeStruct((CONV_K, CONV_CH), F32),
                   jax.ShapeDtypeStruct((1, CONV_CH), F32)],
        compiler_params=_cp(("parallel",)))(proj, conv_w, conv_b, dxa)


def _ssd_consts():
    L = SSD_L
    r = lax.broadcasted_iota(jnp.int32, (L, L), 0)
    c = lax.broadcasted_iota(jnp.int32, (L, L), 1)
    causal = r >= c
    upper = (r <= c).astype(F32)
    hr = lax.broadcasted_iota(jnp.int32, (SSD_HEADS, SSD_WIDTH), 0)
    hc = lax.broadcasted_iota(jnp.int32, (SSD_HEADS, SSD_WIDTH), 1)
    expand = (lax.shift_right_logical(hc, 6) == hr).astype(F32)
    return causal, causal.astype(F32), upper, expand


def _softplus(x):
    return jnp.maximum(x, 0.0) + jnp.log(1.0 + jnp.exp(-jnp.abs(x)))


def _ssd_scalars(dtr, dt_bias, a_log, tri, upper, expand):
    dt = _softplus(dtr + dt_bias)
    A = -jnp.exp(a_log)
    adt = dt * A
    acum = _dot(tri, adt, "nn", split="b")
    acum_t = _dot(adt, upper, "tn", split="a")
    alast = acum[SSD_L - 1:SSD_L, :]
    e = jnp.exp(acum)
    wdec = jnp.exp(alast - acum)
    gam = jnp.exp(alast)
    ex = lambda t: _dot(t, expand, "nn", split="a")
    gam8 = jnp.broadcast_to(gam, (8, SSD_HEADS))
    return dt, A, acum, acum_t, e, wdec, gam, ex(dt), ex(e), ex(wdec), ex(gam8)[0:1, :]


def _ssd_fwd(proj, proj_small, xa, dt_bias, a_log, d_skip, norm_w):
    S = proj.shape[0]
    L, N, W = SSD_L, SSD_N, SSD_WIDTH
    nc = S // L

    def body(z_ref, xa_ref, dtr_ref, dtb_ref, al_ref, dsk_ref, nw_ref, yo_ref, y_ref, rs_ref, hs_ref, h_scr, y_scr):
        @pl.when(pl.program_id(0) == 0)
        def _():
            h_scr[...] = jnp.zeros_like(h_scr)

        causal, tri, upper, expand = _ssd_consts()
        dt, A, acum, acum_t, e, wdec, gam, dtE, eE, wE, gamE = _ssd_scalars(dtr_ref[:, 0:SSD_HEADS], dtb_ref[...], al_ref[...], tri, upper, expand)
        xs = xa_ref[:, 0:W]
        X = xs * dtE
        XW = X * wE
        hs_ref[0] = h_scr[...]
        for g in range(SSD_G):
            gs = slice(g * 512, (g + 1) * 512)
            Bg = xa_ref[:, W + g * N:W + (g + 1) * N]
            Cg = xa_ref[:, W + SSD_G * N + g * N:W + SSD_G * N + (g + 1) * N]
            Hg = h_scr[:, gs]
            CB = _dot(Cg, Bg, "nt")
            yoff = _dot(Cg, Hg, "nn") * eE[:, gs]
            st = _dot(Bg, XW[:, gs], "tn")
            for j in range(8):
                h = g * 8 + j
                hsl = slice(h * SSD_P, (h + 1) * SSD_P)
                lam = jnp.exp(jnp.where(causal, acum[:, h:h + 1] - acum_t[h:h + 1, :], -jnp.inf))
                y_scr[:, hsl] = _dot(CB * lam, X[:, hsl], "nn") + yoff[:, j * SSD_P:(j + 1) * SSD_P]
            h_scr[:, gs] = gamE[:, gs] * Hg + st
        dskE = _dot(jnp.broadcast_to(dsk_ref[...], (8, SSD_HEADS)), expand, "nn", split="a")[0:1, :]
        y = y_scr[...] + dskE * xs
        y_ref[...] = y
        zv = z_ref[...]
        yg = y * (zv * _sigmoid(zv))
        rs = lax.rsqrt(jnp.mean(yg * yg, axis=-1, keepdims=True) + EPS)
        rs_ref[...] = rs
        yo_ref[...] = ((yg * rs) * nw_ref[...]).astype(yo_ref.dtype)

    p16 = pl.BlockSpec((1, SSD_HEADS), lambda c: (0, 0))
    return pl.pallas_call(
        body, name="ssd_fwd", grid=(nc,),
        in_specs=[pl.BlockSpec((L, W), lambda c: (c, 0)), pl.BlockSpec((L, CONV_CH), lambda c: (c, 0)),
                  pl.BlockSpec((L, W_SMALL), lambda c: (c, 0)), p16, p16, p16, pl.BlockSpec((1, W), lambda c: (0, 0))],
        out_specs=[pl.BlockSpec((L, W), lambda c: (c, 0)), pl.BlockSpec((L, W), lambda c: (c, 0)),
                   pl.BlockSpec((L, 1), lambda c: (c, 0)), pl.BlockSpec((1, N, W), lambda c: (c, 0, 0))],
        out_shape=[jax.ShapeDtypeStruct((S, W), _MXU), jax.ShapeDtypeStruct((S, W), F32), jax.ShapeDtypeStruct((S, 1), F32),
                   jax.ShapeDtypeStruct((nc, N, W), F32)],
        scratch_shapes=[pltpu.VMEM((N, W), F32), pltpu.VMEM((L, W), F32)],
        compiler_params=_cp(("arbitrary",)))(proj, xa, proj_small, dt_bias, a_log, d_skip, norm_w)


def _ssd_bwd(dmixed, proj, proj_small, xa, y, rs2, hs, dt_bias, a_log, d_skip, norm_w):
    S = proj.shape[0]
    L, N, W, H = SSD_L, SSD_N, SSD_WIDTH, SSD_HEADS
    nc = S // L

    def body(dyo_ref, z_ref, xa_ref, dtr_ref, y_ref, rs_ref, hs_ref, dtb_ref, al_ref, dsk_ref, nw_ref,
             dz_ref, dxa_ref, ddtr_ref, ddtb_ref, dal_ref, ddsk_ref, dnw_ref, dh_scr, dx_scr):
        @pl.when(pl.program_id(0) == 0)
        def _():
            dh_scr[...] = jnp.zeros_like(dh_scr)
            ddtb_ref[...] = jnp.zeros_like(ddtb_ref)
            dal_ref[...] = jnp.zeros_like(dal_ref)
            ddsk_ref[...] = jnp.zeros_like(ddsk_ref)
            dnw_ref[...] = jnp.zeros_like(dnw_ref)

        causal, tri, upper, expand = _ssd_consts()
        heads = lambda t: _dot(t, expand, "nt", split="a")
        onehot = lambda h: (lax.broadcasted_iota(jnp.int32, (1, H), 1) == h).astype(F32)

        zv, yv, rs = z_ref[...], y_ref[...], rs_ref[...]
        sz = _sigmoid(zv)
        zs = zv * sz
        xhat = (yv * zs) * rs
        dyo = dyo_ref[...].astype(F32)
        dnw_ref[...] += jnp.sum(dyo * xhat, axis=0, keepdims=True)
        dxhat = dyo * nw_ref[...]
        dyg = rs * (dxhat - xhat * jnp.mean(dxhat * xhat, axis=-1, keepdims=True))
        dz_ref[...] = (dyg * yv * (sz * (1.0 + zv * (1.0 - sz)))).astype(dz_ref.dtype)
        dy = dyg * zs

        dtr = dtr_ref[:, 0:H]
        dt, A, acum, acum_t, e, wdec, gam, dtE, eE, wE, gamE = _ssd_scalars(dtr, dtb_ref[...], al_ref[...], tri, upper, expand)
        xs = xa_ref[:, 0:W]
        X = xs * dtE
        XW = X * wE
        dskE = _dot(jnp.broadcast_to(dsk_ref[...], (8, H)), expand, "nn", split="a")[0:1, :]
        ddsk_ref[...] += heads(jnp.broadcast_to(jnp.sum(dy * xs, axis=0, keepdims=True), (8, W)))[0:1, :]

        dYe = dy * eE
        dacum = jnp.zeros((L, H), F32)
        de_full = []
        dw_full = []
        dgam_full = []
        for g in range(SSD_G):
            gs = slice(g * 512, (g + 1) * 512)
            Bg = xa_ref[:, W + g * N:W + (g + 1) * N]
            Cg = xa_ref[:, W + SSD_G * N + g * N:W + SSD_G * N + (g + 1) * N]
            Hg = hs_ref[0, :, gs]
            dHn = dh_scr[:, gs]
            CH = _dot(Cg, Hg, "nn")
            de_full.append(dy[:, gs] * CH)
            dC = _dot(dYe[:, gs], Hg, "nt")
            dHs = gamE[:, gs] * dHn + _dot(Cg, dYe[:, gs], "tn")
            dgam_full.append(jnp.sum(dHn * Hg, axis=0, keepdims=True))
            BdS = _dot(Bg, dHn, "nn")
            dB = _dot(XW[:, gs], dHn, "nt")
            dx_scr[:, gs] = BdS * wE[:, gs]
            dw_full.append(BdS * X[:, gs])
            CB = _dot(Cg, Bg, "nt")
            dCB = jnp.zeros((L, L), F32)
            for j in range(8):
                h = g * 8 + j
                hsl = slice(h * SSD_P, (h + 1) * SSD_P)
                lam = jnp.exp(jnp.where(causal, acum[:, h:h + 1] - acum_t[h:h + 1, :], -jnp.inf))
                M = CB * lam
                dM = _dot(dy[:, hsl], X[:, hsl], "nt")
                dx_scr[:, hsl] += _dot(M, dy[:, hsl], "tn")
                dCB = dCB + dM * lam
                Q = dM * M
                rowsum = jnp.sum(Q, axis=1, keepdims=True)
                colsum = _dot(Q, jnp.ones((L, 8), F32), "tn", split="a")[:, 0:1]
                dacum = dacum + (rowsum - colsum) * onehot(h)
            dC = dC + _dot(dCB, Bg, "nn")
            dB = dB + _dot(dCB, Cg, "tn")
            dxa_ref[:, W + g * N:W + (g + 1) * N] = dB
            dxa_ref[:, W + SSD_G * N + g * N:W + SSD_G * N + (g + 1) * N] = dC
            dh_scr[:, gs] = dHs

        de16 = heads(jnp.concatenate(de_full, axis=1))
        dw16 = heads(jnp.concatenate(dw_full, axis=1))
        dgam16 = heads(jnp.broadcast_to(jnp.concatenate(dgam_full, axis=1), (8, W)))[0:1, :]
        dacum = dacum + de16 * e - dw16 * wdec
        dlast = jnp.sum(dw16 * wdec, axis=0, keepdims=True) + dgam16 * gam
        lastrow = (lax.broadcasted_iota(jnp.int32, (L, 1), 0) == L - 1).astype(F32)
        dacum = dacum + lastrow * dlast
        da = _dot(tri, dacum, "tn", split="b")
        dX = dx_scr[...]
        ddt = da * A + heads(dX * xs)
        dA = jnp.sum(da * dt, axis=0, keepdims=True)
        dal_ref[...] += dA * A
        ddtr = ddt * _sigmoid(dtr + dtb_ref[...])
        ddtb_ref[...] += jnp.sum(ddtr, axis=0, keepdims=True)
        ddtr_ref[...] = ddtr
        dxa_ref[:, 0:W] = dX * dtE + dy * dskE

    p16 = pl.BlockSpec((1, H), lambda c: (0, 0))
    rev = lambda c: (nc - 1 - c, 0)
    return pl.pallas_call(
        body, name="ssd_bwd", grid=(nc,),
        in_specs=[pl.BlockSpec((L, W), rev), pl.BlockSpec((L, W), rev), pl.BlockSpec((L, CONV_CH), rev),
                  pl.BlockSpec((L, W_SMALL), rev), pl.BlockSpec((L, W), rev), pl.BlockSpec((L, 1), rev),
                  pl.BlockSpec((1, N, W), lambda c: (nc - 1 - c, 0, 0)), p16, p16, p16, pl.BlockSpec((1, W), lambda c: (0, 0))],
        out_specs=[pl.BlockSpec((L, W), rev), pl.BlockSpec((L, CONV_CH), rev), pl.BlockSpec((L, H), rev),
                   p16, p16, p16, pl.BlockSpec((1, W), lambda c: (0, 0))],
        out_shape=[jax.ShapeDtypeStruct((S, W), _MXU), jax.ShapeDtypeStruct((S, CONV_CH), F32), jax.ShapeDtypeStruct((S, H), F32),
                   jax.ShapeDtypeStruct((1, H), F32), jax.ShapeDtypeStruct((1, H), F32), jax.ShapeDtypeStruct((1, H), F32),
                   jax.ShapeDtypeStruct((1, W), F32)],
        scratch_shapes=[pltpu.VMEM((N, W), F32), pltpu.VMEM((L, W), F32)],
        compiler_params=_cp(("arbitrary",)))(dmixed, proj, xa, proj_small, y, rs2, hs, dt_bias, a_log, d_skip, norm_w)


def _rope_tables(S):
    inv = 1.0 / (ROPE_THETA ** (jnp.arange(0, ROPE_DIM, 2, dtype=F32) / ROPE_DIM))
    ang = jnp.arange(S, dtype=F32)[:, None] * inv[None, :]
    cos, sin = jnp.cos(ang), jnp.sin(ang)
    half = ROPE_DIM // 2
    c64 = jnp.concatenate([cos, cos, jnp.ones((S, HD - ROPE_DIM), F32)], axis=1)
    s64 = jnp.concatenate([sin, sin, jnp.zeros((S, HD - ROPE_DIM), F32)], axis=1)
    del half
    return jnp.concatenate([c64, c64], axis=1), jnp.concatenate([s64, s64], axis=1)


def _rope(xs, blk0, width, cos, sin, sign, out_dtype, name, extra=None):
    S = xs[0].shape[0]
    tr = _pick(S, (512, 256, 128))
    nx = len(xs)

    def body(*refs):
        x_refs, c_ref, s_ref = refs[:nx], refs[nx], refs[nx + 1]
        e_ref = refs[nx + 2] if extra is not None else None
        o_ref = refs[-1]
        cv, sv = c_ref[...], s_ref[...] * sign
        lane = lax.broadcasted_iota(jnp.int32, (tr, 128), 1)
        first = (lane & (HD - 1)) < (ROPE_DIM // 2)
        for j in range(bw // 128):
            cs = slice(j * 128, (j + 1) * 128)
            xv = x_refs[0][:, cs].astype(F32)
            for r in x_refs[1:]:
                xv = xv + r[:, cs].astype(F32)
            out = _rotate128(xv, cv, sv, first)
            if extra is not None:
                out = out + e_ref[:, cs].astype(F32)
            o_ref[:, cs] = out.astype(out_dtype)

    bw = 512
    assert width % bw == 0 and (blk0 * 256) % bw == 0
    b0 = blk0 * 256 // bw
    t128 = pl.BlockSpec((tr, 128), lambda i, j: (i, 0))
    oblk = pl.BlockSpec((tr, bw), lambda i, j: (i, j))
    specs = [pl.BlockSpec((tr, bw), lambda i, j: (i, b0 + j))] * nx + [t128, t128]
    ins = list(xs) + [cos, sin]
    if extra is not None:
        assert (extra[1] * 256) % bw == 0
        ins.append(extra[0])
        eb = extra[1] * 256 // bw
        specs.append(pl.BlockSpec((tr, bw), lambda i, j: (i, eb + j)))
    return pl.pallas_call(
        body, name=name, grid=(S // tr, width // bw), in_specs=specs, out_specs=oblk,
        out_shape=jax.ShapeDtypeStruct((S, width), out_dtype), compiler_params=_cp(("parallel", "parallel")))(*ins)


def _rotate128(xv, cv, sv, first):
    rot = jnp.where(first, -pltpu.roll(xv, 128 - ROPE_DIM // 2, axis=1), pltpu.roll(xv, ROPE_DIM // 2, axis=1))
    return xv * cv + rot * sv


def _kv_prep(proj, cos, sin, tk):
    S = proj.shape[0]
    NB = S // SEL_BLOCK

    def body(ks_ref, vs_ref, kw_ref, vw_ref, c_ref, s_ref, *outs):
        cv, sv = c_ref[...], s_ref[...]
        lane = lax.broadcasted_iota(jnp.int32, (tk, 128), 1)
        first = (lane & (HD - 1)) < (ROPE_DIM // 2)
        key = pl.program_id(0) * tk + lax.broadcasted_iota(jnp.int32, (tk, NB), 0)
        onehot = (lax.shift_right_logical(key, 6) == lax.broadcasted_iota(jnp.int32, (tk, NB), 1)).astype(F32)
        for j, (ref, rotated) in enumerate(((ks_ref, True), (vs_ref, False), (kw_ref, True), (vw_ref, False))):
            nat, blk = outs[2 * j], outs[2 * j + 1]
            for half in range(2):
                xv = ref[:, half * 128:(half + 1) * 128]
                if rotated:
                    xv = _rotate128(xv, cv, sv, first)
                for e in range(2):
                    h = 2 * half + e
                    piece = xv[:, e * HD:(e + 1) * HD]
                    nat[h] = (jnp.concatenate([piece, onehot], axis=1) if j == 0 else piece).astype(nat.dtype)
                    blk[h, 0] = piece.T.astype(blk.dtype)

    col = lambda b: pl.BlockSpec((tk, 256), lambda i: (i, b))
    t128 = pl.BlockSpec((tk, 128), lambda i: (i, 0))
    nat_spec = lambda w: pl.BlockSpec((N_KV, tk, w), lambda i: (0, i, 0))
    blk_spec = pl.BlockSpec((N_KV, 1, HD, tk), lambda i: (0, i, 0, 0))
    nat_shape = lambda w: jax.ShapeDtypeStruct((N_KV, S, w), _MXU)
    blk_shape = jax.ShapeDtypeStruct((N_KV, S // tk, HD, tk), _MXU)
    widths = (HD + NB, HD, HD, HD)
    res = pl.pallas_call(
        body, name="kv_prep", grid=(S // tk,), in_specs=[col(KSB), col(VSB), col(KWB), col(VWB), t128, t128],
        out_specs=[s for w in widths for s in (nat_spec(w), blk_spec)],
        out_shape=[s for w in widths for s in (nat_shape(w), blk_shape)],
        compiler_params=_cp(("parallel",)))(proj, proj, proj, proj, cos, sin)
    return dict(ks_ext=res[0], ks_t=res[1], vs=res[2], vs_t=res[3], kw=res[4], kw_t=res[5], vw=res[6], vw_t=res[7])


def _dkv_post(dks, dvs, dkw, dvw, cos, sin):
    S = dks.shape[1]
    tr = _pick(S, (512, 256, 128))

    def body(dks_ref, dvs_ref, dkw_ref, dvw_ref, c_ref, s_ref, o_ref):
        cv, sv = c_ref[...], -s_ref[...]
        lane = lax.broadcasted_iota(jnp.int32, (tr, 128), 1)
        first = (lane & (HD - 1)) < (ROPE_DIM // 2)
        for j, (ref, rotated) in enumerate(((dks_ref, True), (dvs_ref, False), (dkw_ref, True), (dvw_ref, False))):
            for half in range(2):
                xv = jnp.concatenate([ref[2 * half], ref[2 * half + 1]], axis=1)
                if rotated:
                    xv = _rotate128(xv, cv, sv, first)
                o_ref[:, j * 256 + half * 128:j * 256 + (half + 1) * 128] = xv.astype(o_ref.dtype)

    hm = pl.BlockSpec((N_KV, tr, HD), lambda i: (0, i, 0))
    t128 = pl.BlockSpec((tr, 128), lambda i: (i, 0))
    return pl.pallas_call(
        body, name="dkv_post", grid=(S // tr,), in_specs=[hm, hm, hm, hm, t128, t128],
        out_specs=pl.BlockSpec((tr, 4 * 256), lambda i: (i, 0)), out_shape=jax.ShapeDtypeStruct((S, 4 * 256), _MXU),
        compiler_params=_cp(("parallel",)))(dks, dvs, dkw, dvw, cos, sin)


def _compress_fwd(R, pe, w1, w2):
    NC = R.shape[1]
    half = 16 * HD

    def body(r_ref, pe_ref, w1_ref, w2_ref, o_ref, hid_ref):
        r = r_ref[0]
        a = _dot(r + pe_ref[:, 0:half], w1_ref[0:half, :], "nn")
        b = _dot(r + pe_ref[:, half:2 * half], w1_ref[half:2 * half, :], "nn")
        hid = a + pltpu.roll(b, NC - 1, axis=0)
        hid_ref[0] = hid
        out = _dot(hid * _sigmoid(hid), w2_ref[...], "nn")
        rows = lax.broadcasted_iota(jnp.int32, out.shape, 0)
        o_ref[0] = jnp.where(rows < NC - 1, out, 0.0).astype(o_ref.dtype)

    return pl.pallas_call(
        body, name="compress_fwd", grid=(N_KV,),
        in_specs=[pl.BlockSpec((1, NC, half), lambda h: (h, 0, 0)), pl.BlockSpec((1, 2 * half), lambda h: (0, 0)),
                  pl.BlockSpec((2 * half, CMP_HID), lambda h: (0, 0)), pl.BlockSpec((CMP_HID, HD), lambda h: (0, 0))],
        out_specs=[pl.BlockSpec((1, NC, HD), lambda h: (h, 0, 0)), pl.BlockSpec((1, NC, CMP_HID), lambda h: (h, 0, 0))],
        out_shape=[jax.ShapeDtypeStruct((N_KV, NC, HD), _MXU), jax.ShapeDtypeStruct((N_KV, NC, CMP_HID), F32)],
        compiler_params=_cp(("parallel",)))(R, pe, w1, w2)


def _compress_bwd(R, pe, w1, w2, hid, dout):
    NC = R.shape[1]
    half = 16 * HD

    def body(r_ref, pe_ref, w1_ref, w2_ref, hid_ref, do_ref, dr_ref, dw1_ref, dw2_ref, dpe_ref):
        @pl.when(pl.program_id(0) == 0)
        def _():
            dw1_ref[...] = jnp.zeros_like(dw1_ref)
            dw2_ref[...] = jnp.zeros_like(dw2_ref)
            dpe_ref[...] = jnp.zeros_like(dpe_ref)

        r, hv, do = r_ref[0], hid_ref[0], do_ref[0]
        s = _sigmoid(hv)
        dw2_ref[...] += _dot(hv * s, do, "tn")
        dhid = _dot(do, w2_ref[...], "nt") * (s * (1.0 + hv * (1.0 - s)))
        rows = lax.broadcasted_iota(jnp.int32, dhid.shape, 0)
        dhid = jnp.where(rows < NC - 1, dhid, 0.0)
        dhid_dn = pltpu.roll(dhid, 1, axis=0)
        dw1_ref[0:half, :] += _dot(r + pe_ref[:, 0:half], dhid, "tn")
        dw1_ref[half:2 * half, :] += _dot(r + pe_ref[:, half:2 * half], dhid_dn, "tn")
        dxt = _dot(dhid, w1_ref[0:half, :], "nt")
        dxb = _dot(dhid_dn, w1_ref[half:2 * half, :], "nt")
        dr_ref[0] = dxt + dxb
        dpe_ref[:, 0:half] += jnp.sum(dxt, axis=0, keepdims=True)
        dpe_ref[:, half:2 * half] += jnp.sum(dxb, axis=0, keepdims=True)

    return pl.pallas_call(
        body, name="compress_bwd", grid=(N_KV,),
        in_specs=[pl.BlockSpec((1, NC, half), lambda h: (h, 0, 0)), pl.BlockSpec((1, 2 * half), lambda h: (0, 0)),
                  pl.BlockSpec((2 * half, CMP_HID), lambda h: (0, 0)), pl.BlockSpec((CMP_HID, HD), lambda h: (0, 0)),
                  pl.BlockSpec((1, NC, CMP_HID), lambda h: (h, 0, 0)), pl.BlockSpec((1, NC, HD), lambda h: (h, 0, 0))],
        out_specs=[pl.BlockSpec((1, NC, half), lambda h: (h, 0, 0)), pl.BlockSpec((2 * half, CMP_HID), lambda h: (0, 0)),
                   pl.BlockSpec((CMP_HID, HD), lambda h: (0, 0)), pl.BlockSpec((1, 2 * half), lambda h: (0, 0))],
        out_shape=[jax.ShapeDtypeStruct((N_KV, NC, half), F32), jax.ShapeDtypeStruct((2 * half, CMP_HID), F32),
                   jax.ShapeDtypeStruct((CMP_HID, HD), F32), jax.ShapeDtypeStruct((1, 2 * half), F32)],
        compiler_params=_cp(("arbitrary",)))(R, pe, w1, w2, hid, dout)


def _attn_cfg(S, Sk, mode):
    tk = _pick(Sk, (256, 128))
    if mode == "cmp":
        return _pick(S, (512, 256, 128)), Sk
    if mode == "sel" and S % (2 * tk) == 0:
        return 2 * tk, tk
    return tk, tk


def _block_start(kb, tk):
    return kb * tk if isinstance(kb, int) else pl.multiple_of(kb * tk, tk)


def _pipelined_key_blocks(mode, q0, tq, tk, produce, consume):
    if mode == "cmp":
        produce(0, True, 0)
        consume(0, 0)
        return
    if mode == "win":
        assert tq == tk and WINDOW == 2 * tk
        last = q0 // tk
        first = jnp.maximum(last - 2, 0)

        @pl.when(last == 0)
        def _():
            produce(last, True, 0)
            consume(last, 0)

        @pl.when(last == 1)
        def _():
            produce(first, True, 0)
            produce(last, True, 1)
            consume(first, 0)
            consume(last, 1)

        @pl.when(last >= 2)
        def _():
            produce(first, True, 0)
            produce(first + 1, False, 1)
            consume(first, 0)
            produce(last, True, 0)
            consume(first + 1, 1)
            consume(last, 0)

        return
    first, n_plain, plain_masked = 0, q0 // tk, False
    pairs = jnp.maximum(n_plain - 1, 0) // 2
    if tq == 2 * tk:
        @pl.when(n_plain >= 1)
        def _():
            produce(0, False, 0)

        def two_plain(j, carry):
            produce(2 * j + 1, False, 1)
            consume(2 * j, 0)
            produce(2 * j + 2, False, 0)
            consume(2 * j + 1, 1)
            return carry

        lax.fori_loop(0, pairs, two_plain, 0)
        kb = 2 * pairs

        @pl.when(n_plain >= 2)
        def _():
            produce(kb + 1, False, 1)
            consume(kb, 0)
            produce(n_plain, True, 0)
            consume(kb + 1, 1)
            produce(n_plain + 1, True, 1)
            consume(n_plain, 0)
            consume(n_plain + 1, 1)

        @pl.when(n_plain == 0)
        def _():
            produce(0, True, 0)
            produce(1, True, 1)
            consume(0, 0)
            consume(1, 1)

        return
    assert tq == tk
    last = first + n_plain

    @pl.when(n_plain >= 1)
    def _():
        produce(first, plain_masked, 0)

    def two(j, carry):
        kb = first + 2 * j
        produce(kb + 1, plain_masked, 1)
        consume(kb, 0)
        produce(kb + 2, plain_masked, 0)
        consume(kb + 1, 1)
        return carry

    lax.fori_loop(0, pairs, two, 0)
    kb = first + 2 * pairs
    left = n_plain - 2 * pairs

    @pl.when(left == 2)
    def _():
        produce(kb + 1, plain_masked, 1)
        consume(kb, 0)
        produce(last, True, 0)
        consume(kb + 1, 1)
        consume(last, 0)

    @pl.when(left == 1)
    def _():
        produce(last, True, 1)
        consume(kb, 0)
        consume(last, 1)

    @pl.when(left == 0)
    def _():
        produce(last, True, 0)
        consume(last, 0)


def _attn_bias(mode, q0, k0, tq, tk):
    k = k0 + lax.broadcasted_iota(jnp.int32, (tk, tq), 0)
    t = q0 + lax.broadcasted_iota(jnp.int32, (tk, tq), 1)
    if mode == "cmp":
        ok = (k * 16 + 31) <= t
    elif mode == "win":
        ok = (k <= t) & ((t - k) < WINDOW)
    else:
        ok = k <= t
    bias = jnp.where(ok, 0.0, NEG)
    return jnp.concatenate([bias] * GRP, axis=1), jnp.concatenate([ok.astype(F32)] * GRP, axis=1)


def _sel_operands(qs, selneg_ref):
    return jnp.concatenate([qs, jnp.concatenate([selneg_ref[0]] * GRP, axis=0)], axis=1)


def _stack_heads(ref, tq):
    return jnp.concatenate([ref[:, g * HD:(g + 1) * HD] for g in range(GRP)], axis=0)


def _scaled_queries(q_ref, tq):
    return (_stack_heads(q_ref, tq).astype(F32) * SCALE).astype(_MXU)


def _blocked_t(x, tk):
    n, Sk, d = x.shape
    return x.reshape(n, Sk // tk, tk, d).transpose(0, 1, 3, 2)


def _head_rows(ref):
    return jnp.concatenate([ref[0, g:g + 1, :] for g in range(GRP)], axis=1)


def _attn_fwd(q, qcol0, k, vt, mode, selneg, gate, y_prev, y_dtype, name):
    S, Sk = q.shape[0], k.shape[1]
    tq, tk = _attn_cfg(S, Sk, mode)
    R = GRP * tq
    NB = S // SEL_BLOCK

    def body(*refs):
        q_ref, k_ref, vt_ref = refs[:3]
        rest = list(refs[3:])
        sel_ref = rest.pop(0) if mode == "sel" else None
        ov_ref = rest.pop(0) if mode == "cmp" else None
        gate_ref = rest.pop(0)
        yp_ref = rest.pop(0) if y_prev is not None else None
        o_ref, lse_ref, y_ref = rest[:3]
        choice_ref = rest[3] if mode == "cmp" else None
        m_scr, l_scr, acc, s_scr = rest[-4:]
        q0 = pl.program_id(1) * tq
        qs = _scaled_queries(q_ref, tq)
        m_scr[...] = jnp.full_like(m_scr, NEG)
        l_scr[...] = jnp.zeros_like(l_scr)
        acc[...] = jnp.zeros_like(acc)
        qk = _sel_operands(qs, sel_ref) if mode == "sel" else qs

        def produce(kb, masked, slot):
            k0 = _block_start(kb, tk)
            s = _dot(k_ref[0, pl.ds(k0, tk), :], qk, "nt")
            if masked:
                s = s + _attn_bias(mode, q0, k0, tq, tk)[0]
            s_scr[slot] = s

        def consume(kb, slot):
            s = s_scr[slot]
            m_old = m_scr[...]
            m_new = jnp.maximum(m_old, jnp.max(s, axis=0, keepdims=True))
            p = jnp.exp(s - m_new)
            if mode == "cmp":
                p = p * _attn_bias(mode, q0, 0, tq, tk)[1]
            alpha = jnp.exp(m_old - m_new)
            l_scr[...] = alpha * l_scr[...] + jnp.sum(p, axis=0, keepdims=True)
            acc[...] = alpha * acc[...] + _dot(vt_ref[0, kb], p, "nn")
            m_scr[...] = m_new

        _pipelined_key_blocks(mode, q0, tq, tk, produce, consume)
        l = l_scr[...]
        good = l > 0.0
        o_t = acc[...] * jnp.where(good, 1.0 / jnp.where(good, l, 1.0), 0.0)
        lse = jnp.where(good, m_scr[...] + jnp.log(jnp.where(good, l, 1.0)), -NEG)
        y_t = o_t * _sigmoid(_head_rows(gate_ref))
        if mode == "cmp":
            p = jnp.exp(s_scr[0] - lse) * _attn_bias(mode, q0, 0, tq, tk)[1]
            choice_ref[0] = _chosen_blocks(p, ov_ref[...], q0, tq).astype(choice_ref.dtype)
        for g in range(GRP):
            hs, qs_ = slice(g * HD, (g + 1) * HD), slice(g * tq, (g + 1) * tq)
            o_ref[:, hs] = o_t[:, qs_].T
            lse_ref[0, g:g + 1, :] = lse[:, qs_]
            yg = y_t[:, qs_].T
            if y_prev is not None:
                yg = yg + yp_ref[:, hs]
            y_ref[:, hs] = yg.astype(y_ref.dtype)

    row_spec = pl.BlockSpec((1, GRP, tq), lambda h, i: (h, 0, i))
    qo_spec = pl.BlockSpec((tq, GRP * HD), lambda h, i: (i, h))
    ins = [q, k, vt]
    specs = [pl.BlockSpec((tq, GRP * HD), lambda h, i: (i, qcol0 + h)), pl.BlockSpec((1, Sk, k.shape[2]), lambda h, i: (h, 0, 0)),
             pl.BlockSpec((1, Sk // tk, HD, tk), lambda h, i: (h, 0, 0, 0))]
    if mode == "sel":
        ins.append(selneg)
        specs.append(pl.BlockSpec((1, tq, selneg.shape[2]), lambda h, i: (h, i, 0)))
    out_specs = [qo_spec, row_spec, qo_spec]
    out_shape = [jax.ShapeDtypeStruct((S, ATT_WIDTH), F32), jax.ShapeDtypeStruct((N_KV, GRP, S), F32),
                 jax.ShapeDtypeStruct((S, ATT_WIDTH), y_dtype)]
    if mode == "cmp":
        ins.append(_block_overlap(Sk, NB))
        specs.append(pl.BlockSpec((NB, Sk), lambda h, i: (0, 0)))
        out_specs.append(pl.BlockSpec((1, tq, NB), lambda h, i: (h, i, 0)))
        out_shape.append(jax.ShapeDtypeStruct((N_KV, S, NB), _MXU))
    ins.append(gate)
    specs.append(row_spec)
    if y_prev is not None:
        ins.append(y_prev)
        specs.append(qo_spec)
    return pl.pallas_call(
        body, name=name, grid=(N_KV, S // tq), in_specs=specs, out_specs=out_specs, out_shape=out_shape,
        scratch_shapes=[pltpu.VMEM((1, R), F32), pltpu.VMEM((1, R), F32), pltpu.VMEM((HD, R), F32), pltpu.VMEM((2, tk, R), F32)],
        compiler_params=_cp(("parallel", "arbitrary")))(*ins)


def _attn_bwd(q, qcol0, k, kt, v, o, lse, dy, dycol0, gate, mode, selneg, name):
    S, Sk = q.shape[0], k.shape[1]
    tq, tk = _attn_cfg(S, Sk, mode)
    R = GRP * tq

    def body(*refs):
        if mode == "sel":
            (q_ref, k_ref, kt_ref, v_ref, o_ref, lse_ref, dy_ref, gate_ref, sel_ref, dq_ref, dk_ref, dv_ref, dg_ref, dq_scr, s_scr,
             dp_scr) = refs
        else:
            q_ref, k_ref, kt_ref, v_ref, o_ref, lse_ref, dy_ref, gate_ref, dq_ref, dk_ref, dv_ref, dg_ref, dq_scr, s_scr, dp_scr = refs

        @pl.when(pl.program_id(1) == 0)
        def _():
            dk_ref[...] = jnp.zeros_like(dk_ref)
            dv_ref[...] = jnp.zeros_like(dv_ref)

        q0 = pl.program_id(1) * tq
        qs = _scaled_queries(q_ref, tq)
        dys = _stack_heads(dy_ref, tq)
        gv = _sigmoid(_head_rows(gate_ref))
        dy_o = _dot(jnp.ones((8, HD), F32), dys * _stack_heads(o_ref, tq), "nt", split="b")[0:1, :]
        delta = gv * dy_o
        dgate = dy_o * (gv * (1.0 - gv))
        for g in range(GRP):
            dg_ref[0, g:g + 1, :] = dgate[:, g * tq:(g + 1) * tq]
        lsev = _head_rows(lse_ref)
        dos = (dys * jnp.broadcast_to(gv, (8, R)).T[:, 0:1]).astype(_MXU)
        dq_scr[...] = jnp.zeros_like(dq_scr)
        qk = _sel_operands(qs, sel_ref) if mode == "sel" else qs

        def produce(kb, masked, slot):
            k0 = _block_start(kb, tk)
            s = _dot(k_ref[0, pl.ds(k0, tk), :], qk, "nt")
            if masked:
                s = s + _attn_bias(mode, q0, k0, tq, tk)[0]
            s_scr[slot] = s
            dp_scr[slot] = _dot(v_ref[0, pl.ds(k0, tk), :], dos, "nt")

        def consume(kb, slot):
            k0 = _block_start(kb, tk)
            p = jnp.exp(s_scr[slot] - lsev)
            if mode == "cmp":
                p = p * _attn_bias(mode, q0, 0, tq, tk)[1]
            ds = p * (dp_scr[slot] - delta)
            dq_scr[...] += _dot(kt_ref[0, kb], ds, "nn")
            dk_ref[0, pl.ds(k0, tk), :] += _dot(ds, qs, "nn")
            dv_ref[0, pl.ds(k0, tk), :] += _dot(p, dos, "nn")

        _pipelined_key_blocks(mode, q0, tq, tk, produce, consume)
        for g in range(GRP):
            dq_ref[:, g * HD:(g + 1) * HD] = (dq_scr[:, g * tq:(g + 1) * tq] * SCALE).T

    kv_spec = pl.BlockSpec((1, Sk, HD), lambda h, i: (h, 0, 0))
    qo_spec = pl.BlockSpec((tq, GRP * HD), lambda h, i: (i, h))
    row_spec = pl.BlockSpec((1, GRP, tq), lambda h, i: (h, 0, i))
    ins = [q, k, kt, v, o, lse, dy, gate]
    specs = [pl.BlockSpec((tq, GRP * HD), lambda h, i: (i, qcol0 + h)), pl.BlockSpec((1, Sk, k.shape[2]), lambda h, i: (h, 0, 0)),
             pl.BlockSpec((1, Sk // tk, HD, tk), lambda h, i: (h, 0, 0, 0)), kv_spec, qo_spec, row_spec,
             pl.BlockSpec((tq, GRP * HD), lambda h, i: (i, dycol0 + h)), row_spec]
    if mode == "sel":
        ins.append(selneg)
        specs.append(pl.BlockSpec((1, tq, selneg.shape[2]), lambda h, i: (h, i, 0)))
    return pl.pallas_call(
        body, name=name, grid=(N_KV, S // tq), in_specs=specs, out_specs=[qo_spec, kv_spec, kv_spec, row_spec],
        out_shape=[jax.ShapeDtypeStruct((S, ATT_WIDTH), F32), jax.ShapeDtypeStruct((N_KV, Sk, HD), F32),
                   jax.ShapeDtypeStruct((N_KV, Sk, HD), F32), jax.ShapeDtypeStruct((N_KV, GRP, S), F32)],
        scratch_shapes=[pltpu.VMEM((HD, R), F32), pltpu.VMEM((2, tk, R), F32), pltpu.VMEM((2, tk, R), F32)],
        compiler_params=_cp(("parallel", "arbitrary")))(*ins)


def _block_overlap(NC, NB):
    ci = np.arange(NC)[None, :] * 16
    sj = np.arange(NB)[:, None] * SEL_BLOCK
    ov_t = np.clip(np.minimum(ci + 32, sj + SEL_BLOCK) - np.maximum(ci, sj), 0, None) / 32.0
    ov_t[:, NC - 1] = 0.0
    return jnp.asarray(ov_t, F32)


def _chosen_blocks(p, ov_t, q0, tq):
    NB = ov_t.shape[0]
    imp4 = _dot(ov_t, p, "nn")
    imp = imp4[:, 0:tq] + imp4[:, tq:2 * tq] + imp4[:, 2 * tq:3 * tq] + imp4[:, 3 * tq:4 * tq]
    blk = lax.broadcasted_iota(jnp.int32, (NB, tq), 0)
    cur = lax.shift_right_logical(q0 + lax.broadcasted_iota(jnp.int32, (NB, tq), 1), 6)
    imp = jnp.where((blk == 0) | (blk == cur) | (blk == cur - 1), FORCE, imp)
    imp = jnp.where(blk <= cur, imp, -1.0)
    rank = jnp.zeros((NB, tq), F32)
    for j in range(NB):
        row = imp[j:j + 1, :]
        ahead = (row > imp) | ((row == imp) & (blk > j))
        rank = rank + ahead.astype(F32)
    chosen = (rank < float(N_SELECT)) & (imp >= 0.0)
    return jnp.where(chosen, 0.0, NEG).T


def _to_rows16(x):
    S = x.shape[0]
    return x.reshape(S // 16, 16, N_KV, HD).transpose(2, 0, 1, 3).reshape(N_KV, S // 16, 16 * HD)


def _from_rows16(r):
    NC = r.shape[1]
    return r.reshape(N_KV, NC, 16, HD).transpose(1, 2, 0, 3).reshape(NC * 16, N_KV * HD)


DT_COL0 = SSD_WIDTH + CONV_CH
GATE_IN_COL0 = D_IN - 3 * N_HEADS


SHARD_IN = D_IN // N_DEV


def _orig_cols(ref, c0, width):
    pieces, c = [], c0
    while c < c0 + width:
        d, off = divmod(c, SHARD_IN)
        w = min(SHARD_IN - off, c0 + width - c)
        pieces.append(ref[d, :, off:off + w])
        c += w
    return pieces[0] if len(pieces) == 1 else jnp.concatenate(pieces, axis=1)


def _cols_from_slabs(slabs):
    _, R, c = slabs.shape
    tr = _pick(R, (256, 128))

    def body(s_ref, o_ref):
        for t in range(N_DEV * c // LANE):
            pieces, col = [], t * LANE
            while col < (t + 1) * LANE:
                d, off = divmod(col, c)
                w = min(c - off, (t + 1) * LANE - col)
                pieces.append(s_ref[d, :, off:off + w])
                col += w
            o_ref[:, t * LANE:(t + 1) * LANE] = pieces[0] if len(pieces) == 1 else jnp.concatenate(pieces, axis=1)

    return pl.pallas_call(
        body, name="cols_from_slabs", grid=(R // tr,), in_specs=[pl.BlockSpec((N_DEV, tr, c), lambda i: (0, i, 0))],
        out_specs=pl.BlockSpec((tr, N_DEV * c), lambda i: (i, 0)), out_shape=jax.ShapeDtypeStruct((R, N_DEV * c), slabs.dtype),
        compiler_params=_cp(("parallel",)))(slabs)


def _slabs_from_cols(x):
    R, c = x.shape[0], x.shape[1] // N_DEV
    tr = _pick(R, (256, 128))

    def body(x_ref, o_ref):
        for d in range(N_DEV):
            o_ref[d] = x_ref[:, d * c:(d + 1) * c]

    return pl.pallas_call(
        body, name="slabs_from_cols", grid=(R // tr,), in_specs=[pl.BlockSpec((tr, N_DEV * c), lambda i: (i, 0))],
        out_specs=pl.BlockSpec((N_DEV, tr, c), lambda i: (0, i, 0)), out_shape=jax.ShapeDtypeStruct((N_DEV, R, c), x.dtype),
        compiler_params=_cp(("parallel",)))(x)


def _w_in_from_slabs(slabs):
    D = slabs.shape[1]
    tr = _pick(D, (256, 128))

    def body(s_ref, main_ref, small_ref):
        for t in range(W_MAIN // LANE):
            c = t * LANE
            main_ref[:, c:c + LANE] = _orig_cols(s_ref, c if c < DT_COL0 else c + SSD_HEADS, LANE)
        small_ref[...] = jnp.concatenate(
            [_orig_cols(s_ref, DT_COL0, SSD_HEADS), _orig_cols(s_ref, GATE_IN_COL0, 3 * N_HEADS),
             jnp.zeros((tr, W_SMALL - SSD_HEADS - 3 * N_HEADS), small_ref.dtype)], axis=1)

    return pl.pallas_call(
        body, name="w_in_layout", grid=(D // tr,), in_specs=[pl.BlockSpec((N_DEV, tr, SHARD_IN), lambda i: (0, i, 0))],
        out_specs=[pl.BlockSpec((tr, W_MAIN), lambda i: (i, 0)), pl.BlockSpec((tr, W_SMALL), lambda i: (i, 0))],
        out_shape=[jax.ShapeDtypeStruct((D, W_MAIN), slabs.dtype), jax.ShapeDtypeStruct((D, W_SMALL), slabs.dtype)],
        compiler_params=_cp(("parallel",)))(slabs)


def _w_in_to_slabs(main, small):
    D = main.shape[0]
    tr = _pick(D, (256, 128))
    ranges = [(0, DT_COL0, 0, 0), (DT_COL0, DT_COL0 + SSD_HEADS, 1, 0), (DT_COL0 + SSD_HEADS, GATE_IN_COL0, 0, DT_COL0),
              (GATE_IN_COL0, D_IN, 1, SSD_HEADS)]

    def body(main_ref, small_ref, o_ref):
        srcs = (main_ref, small_ref)
        for d in range(N_DEV):
            lo, hi = d * SHARD_IN, (d + 1) * SHARD_IN
            pieces = []
            for start, stop, which, s0 in ranges:
                a, b = max(lo, start), min(hi, stop)
                if a < b:
                    pieces.append(srcs[which][:, s0 + a - start:s0 + b - start].astype(o_ref.dtype))
            o_ref[d] = pieces[0] if len(pieces) == 1 else jnp.concatenate(pieces, axis=1)

    return pl.pallas_call(
        body, name="w_in_grad_layout", grid=(D // tr,),
        in_specs=[pl.BlockSpec((tr, W_MAIN), lambda i: (i, 0)), pl.BlockSpec((tr, W_SMALL), lambda i: (i, 0))],
        out_specs=pl.BlockSpec((N_DEV, tr, SHARD_IN), lambda i: (0, i, 0)),
        out_shape=jax.ShapeDtypeStruct((N_DEV, D, SHARD_IN), main.dtype), compiler_params=_cp(("parallel",)))(main, small)


QB, KCB, VCB, KSB, VSB, KWB, VWB = 10, 14, 15, 16, 17, 18, 19


def _col256(a, b):
    return a[:, b * 256:(b + 1) * 256]


_EARLY = ["w_in", "cmp_w1_k", "cmp_w1_v"]
_LATE = ["w_out", "w_gate", "w_up", "w_down"]
_FFN = ["w_down", "w_gate", "w_up"]
_MID = ["w_out"]
_LAST = ["cmp_w1_k", "cmp_w1_v", "w_in"]


def _local_step(x, tgt, p, late_weights=None, grads_ready=None):
    S = x.shape[0]
    cos, sin = _rope_tables(S)

    u, rs1 = p["normed_x"] if "normed_x" in p else _rms_fwd(x, p["attn_norm_w"], "attn_norm")
    proj = _mm(u, p["w_main"], "nn", F32, "in_proj", after=p.get("before_in_proj"))
    proj_small = _mm(u, p["w_small"], "nn", F32, "in_proj_small")
    xa = _conv_fwd(proj, p["conv_w"], p["conv_b"])
    y_ssd, y_pre, rs_ssd, hs = _ssd_fwd(proj, proj_small, xa, p["dt_bias"], p["a_log"], p["d_skip"], p["ssd_norm_w"])

    q_rot = _rope([proj], QB, ATT_WIDTH, cos, sin, 1.0, _MXU, "rope_q")
    kv = _kv_prep(proj, cos, sin, _attn_cfg(S, S, "sel")[1])
    rk, rv = _to_rows16(_col256(proj, KCB)), _to_rows16(_col256(proj, VCB))
    k_cmp, hid_k = _compress_fwd(rk, p["cmp_pe_k"], p["cmp_w1_k"], p["cmp_w2_k"])
    v_cmp, hid_v = _compress_fwd(rv, p["cmp_pe_v"], p["cmp_w1_v"], p["cmp_w2_v"])
    n_cmp = k_cmp.shape[1]

    gates = proj_small[:, SSD_HEADS:SSD_HEADS + 3 * N_HEADS].reshape(S, N_KV, GRP, 3).transpose(3, 1, 2, 0)
    o_cmp, lse_cmp, y_att, sel = _attn_fwd(proj, QB, k_cmp, _blocked_t(v_cmp, n_cmp), "cmp", None, gates[0], None, F32,
                                           "attn_cmp_fwd")
    o_sel, lse_sel, y_att = _attn_fwd(q_rot, 0, kv["ks_ext"], kv["vs_t"], "sel", sel, gates[1], y_att, F32, "attn_sel_fwd")
    o_win, lse_win, y_att = _attn_fwd(q_rot, 0, kv["kw"], kv["vw_t"], "win", None, gates[2], y_att, _MXU, "attn_win_fwd")

    if late_weights is not None:
        p = {**p, **late_weights(y_att)}
    mixed = jnp.concatenate([y_ssd, y_att], axis=1)
    h1 = _mm(mixed, p["w_out"], "nn", F32, "out_proj", res=x)
    v, rs_ffn = _rms_fwd(h1, p["ffn_norm_w"], "ffn_norm")
    gt, up, act = _ffn_up(v, p["w_gate"], p["w_up"])
    h2 = _mm(act, p["w_down"], "nn", F32, "ffn_down", res=h1)
    loss, dh2, dh2b, d_final_w = _final_loss(h2, p["final_norm_w"], tgt)

    def ready(names):
        return None if grads_ready is None else grads_ready(names, g)

    g = {"final_norm_w": d_final_w}
    g["w_down"] = _mm(act, dh2b, "tn", _MXU, "dw_down")
    dgt, dup = _ffn_dact(dh2b, p["w_down"], gt, up)
    g["w_gate"] = _mm(v, dgt, "tn", _MXU, "dw_gate")
    g["w_up"] = _mm(v, dup, "tn", _MXU, "dw_up")
    dv = _ffn_dv(dgt, dup, p["w_gate"], p["w_up"], ready(_FFN))
    dh1, dh1b, g["ffn_norm_w"] = _rms_bwd(dv, h1, rs_ffn, p["ffn_norm_w"], dh2, "ffn_norm_bwd")
    g["w_out"] = _mm(mixed, dh1b, "tn", _MXU, "dw_out")
    dmixed = _mm(dh1b, p["w_out"], "nt", F32, "dmixed", after=ready(_MID))

    dz, dxa, ddtr, g["dt_bias"], g["a_log"], g["d_skip"], g["ssd_norm_w"] = _ssd_bwd(
        dmixed, proj, proj_small, xa, y_pre, rs_ssd, hs, p["dt_bias"], p["a_log"], p["d_skip"], p["ssd_norm_w"])
    dxbc, g["conv_w"], g["conv_b"] = _conv_bwd(proj, p["conv_w"], p["conv_b"], dxa)

    dyb = SSD_WIDTH // (GRP * HD)
    dq_cmp, dk_cmp, dv_cmp, dg_cmp = _attn_bwd(proj, QB, k_cmp, _blocked_t(k_cmp, n_cmp), v_cmp, o_cmp, lse_cmp, dmixed, dyb,
                                               gates[0], "cmp", None, "attn_cmp_bwd")
    dq_sel, dks, dvs, dg_sel = _attn_bwd(q_rot, 0, kv["ks_ext"], kv["ks_t"], kv["vs"], o_sel, lse_sel, dmixed, dyb, gates[1], "sel",
                                         sel, "attn_sel_bwd")
    dq_win, dkw, dvw, dg_win = _attn_bwd(q_rot, 0, kv["kw"], kv["kw_t"], kv["vw"], o_win, lse_win, dmixed, dyb, gates[2], "win", None,
                                         "attn_win_bwd")
    dgate = jnp.stack([dg_cmp, dg_sel, dg_win]).transpose(3, 1, 2, 0).reshape(S, 3 * N_HEADS)
    drk, g["cmp_w1_k"], g["cmp_w2_k"], g["cmp_pe_k"] = _compress_bwd(rk, p["cmp_pe_k"], p["cmp_w1_k"], p["cmp_w2_k"], hid_k, dk_cmp)
    drv, g["cmp_w1_v"], g["cmp_w2_v"], g["cmp_pe_v"] = _compress_bwd(rv, p["cmp_pe_v"], p["cmp_w1_v"], p["cmp_w2_v"], hid_v, dv_cmp)
    dq = _rope([dq_sel, dq_win], 0, ATT_WIDTH, cos, sin, -1.0, _MXU, "rope_dq", extra=(dq_cmp, 0))
    dkv = _dkv_post(dks, dvs, dkw, dvw, cos, sin)
    dproj = jnp.concatenate([dz, dxbc, dq] + [t.astype(_MXU) for t in (_from_rows16(drk), _from_rows16(drv))] + [dkv], axis=1)
    dsmall = jnp.concatenate([ddtr, dgate, jnp.zeros((S, W_SMALL - SSD_HEADS - 3 * N_HEADS), F32)], axis=1).astype(_MXU)
    g["w_main"] = _mm(u, dproj, "tn", _MXU, "dw_in")
    g["w_small"] = _mm(u, dsmall, "tn", F32, "dw_in_small")
    du = _mm(dproj, p["w_main"], "nt", F32, "du_main", after=ready(_LAST))
    du = _mm(dsmall, p["w_small"], "nt", F32, "du_small", res=du)
    grad_x, _, g["attn_norm_w"] = _rms_bwd(du, x, rs1, p["attn_norm_w"], dh1, "attn_norm_bwd")
    return loss, grad_x, g


MESH_ID = pl.DeviceIdType.MESH


def _my_coords():
    return lax.axis_index("x"), lax.axis_index("y"), lax.axis_index("c")


def _flat_id(px, py, pc):
    return 4 * px + 2 * py + pc


def _peer(k):
    mx, my, mc = _my_coords()
    return (1 - mx if k & 4 else mx, 1 - my if k & 2 else my, 1 - mc if k & 1 else mc)


def _exchange(arrs, scatter, name, after=()):
    n, na = len(arrs), len(after)
    scatter = [scatter] * n if isinstance(scatter, bool) else list(scatter)

    def body(*refs):
        ins, outs = refs[:n], refs[n + na:2 * n + na]
        send_sems, recv_sems, local_sems = refs[2 * n + na:]
        me = _flat_id(*_my_coords())
        copies = []
        for i in range(n):
            src_me = ins[i].at[me] if scatter[i] else ins[i]
            local = pltpu.make_async_copy(src_me, outs[i].at[me], local_sems.at[i])
            local.start()
            copies.append(local)
        for k in range(1, N_DEV):
            peer = _peer(k)
            for i in range(n):
                src = ins[i].at[_flat_id(*peer)] if scatter[i] else ins[i]
                cp = pltpu.make_async_remote_copy(src_ref=src, dst_ref=outs[i].at[me], send_sem=send_sems.at[i * 7 + k - 1],
                                                  recv_sem=recv_sems.at[i * 7 + k - 1], device_id=peer, device_id_type=MESH_ID)
                cp.start()
                copies.append(cp)
        for cp in copies:
            cp.wait()

    any_spec = pl.BlockSpec(memory_space=pl.ANY)
    out_shape = [jax.ShapeDtypeStruct(a.shape if sc else (N_DEV,) + a.shape, a.dtype) for a, sc in zip(arrs, scatter)]
    return pl.pallas_call(
        body, name=name, in_specs=[any_spec] * (n + na), out_specs=[any_spec] * n, out_shape=out_shape,
        scratch_shapes=[pltpu.SemaphoreType.DMA((n * 7,)), pltpu.SemaphoreType.DMA((n * 7,)), pltpu.SemaphoreType.DMA((n,))],
        compiler_params=pltpu.CompilerParams(has_side_effects=True))(*arrs, *after)


def _gather_two_level(arrs, name, xin, norm_w):
    n = len(arrs)
    S, D = xin.shape
    tr = _pick(S, (512, 256, 128))

    def rms_rows(x_hbm, w_ref, u_hbm, rs_hbm, xbuf, ubuf, rsbuf, sem):
        def move(src, dst):
            cp = pltpu.make_async_copy(src, dst, sem.at[0])
            cp.start()
            cp.wait()

        def tile(i, carry):
            rows = pl.ds(pl.multiple_of(i * tr, tr), tr)
            move(x_hbm.at[rows], xbuf)
            xv = xbuf[...]
            rs = lax.rsqrt(jnp.mean(xv * xv, axis=-1, keepdims=True) + EPS)
            ubuf[...] = ((xv * rs) * w_ref[...]).astype(ubuf.dtype)
            rsbuf[...] = rs
            move(ubuf, u_hbm.at[rows])
            move(rsbuf, rs_hbm.at[rows])
            return carry

        lax.fori_loop(0, S // tr, tile, 0)

    def body(*refs):
        ins, x_hbm, w_ref = refs[:n], refs[n], refs[n + 1]
        outs, u_hbm, rs_hbm = refs[n + 2:2 * n + 2], refs[2 * n + 2], refs[2 * n + 3]
        send_sems, recv_sems, local_sems, xbuf, ubuf, rsbuf, norm_sem = refs[2 * n + 4:]
        x, y, c = _my_coords()
        me, sibling = (x, y, c), (x, y, 1 - c)
        chips = [(1 - x, y), (x, 1 - y), (1 - x, 1 - y)]

        def copy(i, k, block, to, src=None):
            slot = outs[i].at[_flat_id(*block)]
            return pltpu.make_async_remote_copy(src_ref=slot if src is None else src, dst_ref=slot, send_sem=send_sems.at[i * 7 + k],
                                                recv_sem=recv_sems.at[i * 7 + k], device_id=to, device_id_type=MESH_ID)

        mine = [pltpu.make_async_copy(ins[i], outs[i].at[_flat_id(*me)], local_sems.at[i]) for i in range(n)]
        for cp in mine:
            cp.start()
        first = []
        for j, chip in enumerate(chips):
            first += [copy(i, 1 + j, me, (*chip, c), src=ins[i]) for i in range(n)]
        first += [copy(i, 0, me, sibling, src=ins[i]) for i in range(n)]
        for cp in first:
            cp.start()
        rms_rows(x_hbm, w_ref, u_hbm, rs_hbm, xbuf, ubuf, rsbuf, norm_sem)
        passed = []
        for j, chip in enumerate(chips):
            for i in range(n):
                copy(i, 1 + j, (*chip, c), me).wait_recv()
                passed.append(copy(i, 4 + j, (*chip, c), sibling))
                passed[-1].start()
        for i in range(n):
            copy(i, 0, sibling, me).wait_recv()
        for j, chip in enumerate(chips):
            for i in range(n):
                copy(i, 4 + j, (*chip, 1 - c), me).wait_recv()
        for cp in first + passed:
            cp.wait_send()
        for cp in mine:
            cp.wait()

    any_spec = pl.BlockSpec(memory_space=pl.ANY)
    res = pl.pallas_call(
        body, name=name, in_specs=[any_spec] * (n + 1) + [pl.BlockSpec(memory_space=pltpu.VMEM)], out_specs=[any_spec] * (n + 2),
        out_shape=[jax.ShapeDtypeStruct((N_DEV,) + a.shape, a.dtype) for a in arrs]
        + [jax.ShapeDtypeStruct((S, D), _MXU), jax.ShapeDtypeStruct((S, 1), F32)],
        scratch_shapes=[pltpu.SemaphoreType.DMA((n * 7,)), pltpu.SemaphoreType.DMA((n * 7,)), pltpu.SemaphoreType.DMA((n,)),
                        pltpu.VMEM((tr, D), F32), pltpu.VMEM((tr, D), _MXU), pltpu.VMEM((tr, 1), F32), pltpu.SemaphoreType.DMA((1,))],
        compiler_params=pltpu.CompilerParams(has_side_effects=True))(*arrs, xin, norm_w)
    return res[:n], res[n], res[n + 1]


_HBM = pl.BlockSpec(memory_space=pltpu.HBM)
_SEM = pl.BlockSpec(memory_space=pltpu.SEMAPHORE)
_EFFECT = pltpu.SideEffectType.DATAFLOW_SIDE_EFFECTING


def _split_copies(ins, lands, send_sems, recv_sems, own_sems, scatter):
    me = _flat_id(*_my_coords())
    remote = []
    for k in range(1, N_DEV):
        peer = _peer(k)
        for i in range(len(ins)):
            src = ins[i].at[_flat_id(*peer)] if scatter else ins[i]
            remote.append(pltpu.make_async_remote_copy(src_ref=src, dst_ref=lands[i].at[me], send_sem=send_sems.at[i * 7 + k - 1],
                                                       recv_sem=recv_sems.at[i * 7 + k - 1], device_id=peer, device_id_type=MESH_ID))
    own = [pltpu.make_async_copy(ins[i].at[me] if scatter else ins[i], lands[i].at[me], own_sems.at[i]) for i in range(len(ins))]
    return remote, own


def _split_start(arrs, scatter, name, after=()):
    n, na = len(arrs), len(after)

    def body(*refs):
        remote, own = _split_copies(refs[:n], refs[n:2 * n], refs[2 * n + na], refs[2 * n + na + 1], refs[2 * n + na + 2], scatter)
        for cp in remote + own:
            cp.start()
        refs[-1][...] = jnp.zeros_like(refs[-1])

    land_shapes = [a.shape if scatter else (N_DEV,) + a.shape for a in arrs]
    out_shape = ((pltpu.SemaphoreType.DMA((n * 7,)), pltpu.SemaphoreType.DMA((n * 7,)), pltpu.SemaphoreType.DMA((n,)))
                 + tuple(pltpu.HBM(a.shape, a.dtype) for a in arrs) + tuple(pltpu.HBM(s, a.dtype) for s, a in zip(land_shapes, arrs))
                 + (jax.ShapeDtypeStruct((8, 128), F32),))
    operands = ([pltpu.with_memory_space_constraint(a, pltpu.HBM) for a in arrs]
                + [pltpu.with_memory_space_constraint(lax.empty(s, a.dtype), pltpu.HBM) for s, a in zip(land_shapes, arrs)])
    res = pl.pallas_call(
        body, name=name, out_shape=out_shape, in_specs=[_HBM] * (2 * n) + [pl.BlockSpec(memory_space=pl.ANY)] * na,
        out_specs=(_SEM, _SEM, _SEM) + (_HBM,) * (2 * n) + (pl.BlockSpec(memory_space=pltpu.VMEM),),
        input_output_aliases={i: 3 + i for i in range(2 * n)},
        compiler_params=pltpu.CompilerParams(has_side_effects=_EFFECT))(*operands, *after)
    return dict(send=res[0], recv=res[1], own=res[2], ins=list(res[3:3 + n]), lands=list(res[3 + n:3 + 2 * n]), token=res[-1])


def _split_wait(st, scatter, after, name):
    n = len(st["ins"])

    def body(*refs):
        remote, own = _split_copies(refs[:n], refs[n:2 * n], refs[2 * n], refs[2 * n + 1], refs[2 * n + 2], scatter)
        for cp in remote:
            cp.wait_send()
            cp.wait_recv()
        for cp in own:
            cp.wait()

    arrs = st["ins"] + st["lands"]
    res = pl.pallas_call(
        body, name=name, out_shape=tuple(pltpu.HBM(a.shape, a.dtype) for a in arrs),
        in_specs=[_HBM] * (2 * n) + [_SEM, _SEM, _SEM] + [pl.BlockSpec(memory_space=pl.ANY)] * len(after), out_specs=(_HBM,) * (2 * n),
        input_output_aliases={i: i for i in range(2 * n)},
        compiler_params=pltpu.CompilerParams(has_side_effects=_EFFECT))(*arrs, st["send"], st["recv"], st["own"], *after)
    return list(res[n:])


def _adam_step(p_ref, w_ref, m_ref, v_ref, g_ref, d_ref, nm_ref, nv_ref):
    g = p_ref[0].astype(F32)
    for j in range(1, p_ref.shape[0]):
        g = g + p_ref[j].astype(F32)
    g_ref[...] = g
    nm = ADAM_B1 * m_ref[...] + (1.0 - ADAM_B1) * g
    nv = ADAM_B2 * v_ref[...] + (1.0 - ADAM_B2) * (g * g)
    nm_ref[...] = nm
    nv_ref[...] = nv
    m_hat = nm / (1.0 - ADAM_B1 ** ADAM_STEP)
    v_hat = nv / (1.0 - ADAM_B2 ** ADAM_STEP)
    d_ref[...] = -ADAM_LR * (m_hat / (jnp.sqrt(v_hat) + ADAM_EPS) + ADAM_WD * w_ref[...])


def _adam_sum(parts, w, m, v, name):
    P, R, C = parts.shape
    tr = _pick(R, (256, 128, 64, 32, 8)) if C <= 1024 else _pick(R, (128, 64, 32, 8))
    blk = pl.BlockSpec((tr, C), lambda i: (i, 0))
    return pl.pallas_call(
        functools.partial(_adam_step), name=name, grid=(R // tr,),
        in_specs=[pl.BlockSpec((P, tr, C), lambda i: (0, i, 0)), blk, blk, blk],
        out_specs=[blk] * 4, out_shape=[jax.ShapeDtypeStruct((R, C), F32)] * 4, compiler_params=_cp(("parallel",)))(parts, w, m, v)


def _adam_small(loss_parts, parts, ws, ms, vs):
    n = len(parts)

    def body(*refs):
        loss_ref, ins, outs, total_ref = refs[0], refs[1:4 * n + 1], refs[4 * n + 1:-1], refs[-1]
        for i in range(n):
            _adam_step(ins[i], ins[n + i], ins[2 * n + i], ins[3 * n + i], *outs[4 * i:4 * i + 4])
        total = loss_ref[0]
        for d in range(1, N_DEV):
            total = total + loss_ref[d]
        total_ref[...] = total

    out_shape = [jax.ShapeDtypeStruct(w.shape, F32) for w in ws for _ in range(4)] + [jax.ShapeDtypeStruct(loss_parts.shape[1:], F32)]
    res = pl.pallas_call(body, name="adam_small", out_shape=out_shape)(loss_parts, *parts, *ws, *ms, *vs)
    return res[-1], [tuple(res[4 * i:4 * i + 4]) for i in range(n)]


_WEIGHTS = ["attn_norm_w", "w_in", "conv_w", "conv_b", "dt_bias", "a_log", "d_skip", "ssd_norm_w", "cmp_w1_k", "cmp_w2_k",
            "cmp_w1_v", "cmp_w2_v", "cmp_pe_k", "cmp_pe_v", "w_out", "ffn_norm_w", "w_gate", "w_up", "w_down", "final_norm_w"]
_BIG = ["w_in", "w_gate", "w_up", "w_down", "w_out", "cmp_w1_k", "cmp_w1_v"]
_COL_SHARDED = ("w_in", "w_gate", "w_up")
_REPLICATED = ["attn_norm_w", "conv_b", "dt_bias", "a_log", "d_skip", "ssd_norm_w", "cmp_pe_k", "cmp_pe_v", "ffn_norm_w",
               "final_norm_w"]
_SMALL_SHARDED = ["conv_w", "cmp_w2_k", "cmp_w2_v"]


def _cols_to_slabs(g):
    R = g.shape[0]
    return g.reshape(R, N_DEV, -1).transpose(1, 0, 2)


def _slabs_to_cols(s):
    return s.transpose(1, 0, 2).reshape(s.shape[1], -1)


def kernel(x, attn_norm_w, w_in, conv_w, conv_b, dt_bias, a_log, d_skip, ssd_norm_w, cmp_w1_k, cmp_w2_k, cmp_w1_v, cmp_w2_v, cmp_pe_k, cmp_pe_v, w_out, ffn_norm_w, w_gate, w_up, w_down, final_norm_w, loss_target, m_attn_norm_w, m_w_in, m_conv_w, m_conv_b, m_dt_bias, m_a_log, m_d_skip, m_ssd_norm_w, m_cmp_w1_k, m_cmp_w2_k, m_cmp_w1_v, m_cmp_w2_v, m_cmp_pe_k, m_cmp_pe_v, m_w_out, m_ffn_norm_w, m_w_gate, m_w_up, m_w_down, m_final_norm_w, v_attn_norm_w, v_w_in, v_conv_w, v_conv_b, v_dt_bias, v_a_log, v_d_skip, v_ssd_norm_w, v_cmp_w1_k, v_cmp_w2_k, v_cmp_w1_v, v_cmp_w2_v, v_cmp_pe_k, v_cmp_pe_v, v_w_out, v_ffn_norm_w, v_w_gate, v_w_up, v_w_down, v_final_norm_w):
    a = dict(locals())

    shard = {n: a[n][0].astype(_MXU) for n in _BIG}
    got, normed_x, rs_x = _gather_two_level([shard[n] for n in _EARLY] + [cmp_w2_k[0], cmp_w2_v[0], conv_w[0]], "gather_early",
                                            x[0], attn_norm_w)
    st_late = _split_start([shard[n] for n in _LATE], False, "gather_late_start", after=(got[0],))

    def assemble(n, t):
        return _cols_from_slabs(t) if n in _COL_SHARDED else t.reshape(-1, t.shape[-1])

    p = dict(attn_norm_w=attn_norm_w, conv_b=conv_b, dt_bias=dt_bias, a_log=a_log, d_skip=d_skip, ssd_norm_w=ssd_norm_w,
             cmp_pe_k=cmp_pe_k.reshape(1, -1), cmp_pe_v=cmp_pe_v.reshape(1, -1), ffn_norm_w=ffn_norm_w,
             final_norm_w=final_norm_w.reshape(1, -1))

    w_main, w_small = _w_in_from_slabs(got[0])
    p.update(normed_x=(normed_x, rs_x), before_in_proj=st_late["token"],
             w_main=w_main, w_small=w_small, cmp_w1_k=assemble("cmp_w1_k", got[1]), cmp_w1_v=assemble("cmp_w1_v", got[2]),
             cmp_w2_k=assemble("cmp_w2_k", got[3]).astype(_MXU), cmp_w2_v=assemble("cmp_w2_v", got[4]).astype(_MXU),
             conv_w=_slabs_to_cols(got[5]))

    def late_weights(after):
        got_late = _split_wait(st_late, False, (after,), "gather_late_wait")
        return {n: assemble(n, t) for n, t in zip(_LATE, got_late)}

    def slabs_of(g, n):
        if n == "w_in":
            return _w_in_to_slabs(g["w_main"], g["w_small"])
        return _slabs_from_cols(g[n]) if n in _COL_SHARDED else g[n].reshape(N_DEV, -1, g[n].shape[-1])

    started = []

    def grads_ready(names, g):
        started.append((names, _split_start([slabs_of(g, n) for n in names], True, "scatter_grads_start_%d" % len(started))))
        return started[-1][1]["token"]

    loss_part, grad_x, g = _local_step(x[0], loss_target[0], p, late_weights, grads_ready)

    out, after = {}, (started[-1][1]["token"],)
    for i, (names, st) in enumerate(started):
        if i == len(started) - 1:
            after = after + (grad_x,)
        received = _split_wait(st, True, after, "scatter_grads_wait_%d" % i)
        for n, parts in zip(names, received):
            out[n] = _adam_sum(parts, a[n][0], a["m_" + n][0], a["v_" + n][0], "adam_" + n)
        after = (out[names[-1]][0],)

    small_names = _REPLICATED + _SMALL_SHARDED
    partials = [g[n] for n in _REPLICATED] + [_cols_to_slabs(g["conv_w"])] + [
        g[n].reshape(N_DEV, -1, g[n].shape[-1]) for n in ("cmp_w2_k", "cmp_w2_v")]
    gathered = _exchange([loss_part] + partials, [False] * (1 + len(_REPLICATED)) + [True] * len(_SMALL_SHARDED),
                         "exchange_small_grads", after=(received[0],))
    shapes2d = [t.shape[1:] for t in gathered[1:]]
    loss, res_small = _adam_small(gathered[0], gathered[1:],
                                  *[[a[pre + n].reshape(s) for n, s in zip(small_names, shapes2d)] for pre in ("", "m_", "v_")])
    for n, r in zip(small_names, res_small):
        out[n] = r

    outs = [loss[0, 0], grad_x[None]]
    for j in range(4):
        for n in _WEIGHTS:
            outs.append(out[n][j].reshape(a[n].shape))
    return tuple(outs)
```

```python
import functools

import numpy as np
import jax
import jax.numpy as jnp
from jax import lax
from jax.experimental import pallas as pl
from jax.experimental.pallas import tpu as pltpu

F32 = jnp.float32
_MXU = jnp.bfloat16

N_DEV = 8
SSD_WIDTH = 1024
ATT_WIDTH = 1024
SSD_HEADS = 16
SSD_P = 64
SSD_N = 128
SSD_L = 128
SSD_G = 2
CONV_CH = 1536
CONV_K = 4
HD = 64
N_HEADS = 16
N_KV = 4
GRP = 4
CMP_HID = 256
SEL_BLOCK = 64
N_SELECT = 16
WINDOW = 512
ROPE_DIM = 16
ROPE_THETA = 500000.0
EPS = 1e-6
NEG = -1e30
FORCE = 1e4
SCALE = HD ** -0.5
D_IN = 5184
W_MAIN = 5120
W_SMALL = 128
VMEM_LIMIT = 52 * 1024 * 1024

ADAM_LR, ADAM_B1, ADAM_B2, ADAM_EPS, ADAM_WD, ADAM_STEP = 0.001, 0.9, 0.999, 1e-08, 0.01, 10


def _pick(n, cands):
    for c in cands:
        if n % c == 0:
            return c
    return n


def _cp(sem=None):
    return pltpu.CompilerParams(dimension_semantics=sem, vmem_limit_bytes=VMEM_LIMIT)


def _sigmoid(x):
    return 1.0 / (1.0 + jnp.exp(-x))


def _dot(a, b, dims, split=None):
    dn = {"nn": (((1,), (0,)), ((), ())), "nt": (((1,), (1,)), ((), ())), "tn": (((0,), (0,)), ((), ()))}[dims]
    mm = lambda x, y: lax.dot_general(x.astype(_MXU), y.astype(_MXU), dn, preferred_element_type=F32)
    if split is None:
        return mm(a, b)
    x = (a if split == "a" else b).astype(F32)
    hi = x.astype(_MXU)
    lo = x - hi.astype(F32)
    return mm(hi, b) + mm(lo, b) if split == "a" else mm(a, hi) + mm(a, lo)


LANE = 128
MM_TILE = 1024
MM_K_WHOLE = 2048
MM_K_STEP = 2816
TN_ACC_ELEMS = 3 * 2 ** 20
TN_K_STEP = 1024


def _largest_tile(n, cap):
    if n <= cap:
        return n
    best = LANE
    for t in range(LANE, cap + 1, LANE):
        if n % t == 0:
            best = t
    return best


def _mm_tiles(mode, M, N, K):
    if mode == "tn":
        tm = _largest_tile(M, 2 * MM_TILE)
        return tm, _largest_tile(N, TN_ACC_ELEMS // tm), _largest_tile(K, TN_K_STEP)
    tk = K if K <= MM_K_WHOLE else _largest_tile(K, MM_K_STEP)
    return _largest_tile(M, MM_TILE), _largest_tile(N, MM_TILE), tk


def _mm(a, b, mode, out_dtype, name, res=None, after=None):
    if mode == "nn":
        (M, K), N = a.shape, b.shape[1]
    elif mode == "nt":
        (M, K), N = a.shape, b.shape[0]
    else:
        (K, M), N = a.shape, b.shape[1]
    tm, tn, tk = _mm_tiles(mode, M, N, K)
    nk = K // tk
    a_spec = pl.BlockSpec((tk, tm), lambda i, j, k: (k, i)) if mode == "tn" else pl.BlockSpec((tm, tk), lambda i, j, k: (i, k))
    b_spec = pl.BlockSpec((tn, tk), lambda i, j, k: (j, k)) if mode == "nt" else pl.BlockSpec((tk, tn), lambda i, j, k: (k, j))
    o_spec = pl.BlockSpec((tm, tn), lambda i, j, k: (i, j))

    def finish(r, r_ref, o_ref):
        if res is not None:
            r = r + r_ref[...].astype(F32)
        o_ref[...] = r.astype(out_dtype)

    def body_one_step(*refs):
        a_ref, b_ref, o_ref = refs[0], refs[1], refs[-1]
        finish(_dot(a_ref[...], b_ref[...], mode), refs[2], o_ref)

    def body(*refs):
        a_ref, b_ref, o_ref, acc = refs[0], refs[1], refs[-2], refs[-1]
        k = pl.program_id(2)

        @pl.when(k == 0)
        def _():
            acc[...] = jnp.zeros_like(acc)

        acc[...] += _dot(a_ref[...], b_ref[...], mode)

        @pl.when(k == nk - 1)
        def _():
            finish(acc[...], refs[2], o_ref)

    ins, specs = [a, b], [a_spec, b_spec]
    if res is not None:
        ins.append(res)
        specs.append(o_spec)
    if after is not None:
        ins.append(after)
        specs.append(pl.BlockSpec(memory_space=pl.ANY))
    return pl.pallas_call(
        body_one_step if nk == 1 else body, name=name, grid=(M // tm, N // tn, nk), in_specs=specs, out_specs=o_spec,
        out_shape=jax.ShapeDtypeStruct((M, N), out_dtype), scratch_shapes=[] if nk == 1 else [pltpu.VMEM((tm, tn), F32)],
        compiler_params=_cp(("parallel", "parallel", "arbitrary")))(*ins)


def _ffn_up(v, w_gate, w_up):
    S, D = v.shape
    F = w_gate.shape[1]
    tm, tn = _largest_tile(S, MM_TILE), _largest_tile(F, MM_TILE // 2)

    def body(v_ref, wg_ref, wu_ref, gt_ref, up_ref, act_ref):
        vv = v_ref[...]
        g = _dot(vv, wg_ref[...], "nn")
        u = _dot(vv, wu_ref[...], "nn")
        gt_ref[...] = g.astype(gt_ref.dtype)
        up_ref[...] = u.astype(up_ref.dtype)
        act_ref[...] = (g * _sigmoid(g) * u).astype(act_ref.dtype)

    o_spec = pl.BlockSpec((tm, tn), lambda i, j: (i, j))
    w_spec = pl.BlockSpec((D, tn), lambda i, j: (0, j))
    return pl.pallas_call(
        body, name="ffn_up", grid=(S // tm, F // tn),
        in_specs=[pl.BlockSpec((tm, D), lambda i, j: (i, 0)), w_spec, w_spec], out_specs=[o_spec, o_spec, o_spec],
        out_shape=[jax.ShapeDtypeStruct((S, F), _MXU)] * 3,
        compiler_params=_cp(("parallel", "parallel")))(v, w_gate, w_up)


def _ffn_dv(dgt, dup, w_gate, w_up, after):
    S, F = dgt.shape
    D = w_gate.shape[0]
    tm, tn, _ = _mm_tiles("nt", S, D, F)
    tk = _largest_tile(F, MM_K_STEP // 2)
    nk = F // tk

    def body(g_ref, u_ref, wg_ref, wu_ref, *rest):
        o_ref, acc = rest[-2], rest[-1]
        k = pl.program_id(2)

        @pl.when(k == 0)
        def _():
            acc[...] = jnp.zeros_like(acc)

        acc[...] += _dot(g_ref[...], wg_ref[...], "nt") + _dot(u_ref[...], wu_ref[...], "nt")

        @pl.when(k == nk - 1)
        def _():
            o_ref[...] = acc[...]

    a_spec = pl.BlockSpec((tm, tk), lambda i, j, k: (i, k))
    w_spec = pl.BlockSpec((tn, tk), lambda i, j, k: (j, k))
    ins, specs = [dgt, dup, w_gate, w_up], [a_spec, a_spec, w_spec, w_spec]
    if after is not None:
        ins.append(after)
        specs.append(pl.BlockSpec(memory_space=pl.ANY))
    return pl.pallas_call(
        body, name="ffn_dv", grid=(S // tm, D // tn, nk), in_specs=specs, out_specs=pl.BlockSpec((tm, tn), lambda i, j, k: (i, j)),
        out_shape=jax.ShapeDtypeStruct((S, D), F32), scratch_shapes=[pltpu.VMEM((tm, tn), F32)],
        compiler_params=_cp(("parallel", "parallel", "arbitrary")))(*ins)


def _ffn_dact(dh2, w_down, gt, up):
    S, D = dh2.shape
    F = w_down.shape[0]
    tm, tn = _largest_tile(S, MM_TILE), _largest_tile(F, MM_TILE // 2)

    def body(d_ref, w_ref, gt_ref, up_ref, dg_ref, du_ref):
        da, g, u = _dot(d_ref[...], w_ref[...], "nt"), gt_ref[...].astype(F32), up_ref[...].astype(F32)
        s = _sigmoid(g)
        dg_ref[...] = (da * u * (s * (1.0 + g * (1.0 - s)))).astype(dg_ref.dtype)
        du_ref[...] = (da * (g * s)).astype(du_ref.dtype)

    o_spec = pl.BlockSpec((tm, tn), lambda i, j: (i, j))
    return pl.pallas_call(
        body, name="ffn_dact", grid=(S // tm, F // tn),
        in_specs=[pl.BlockSpec((tm, D), lambda i, j: (i, 0)), pl.BlockSpec((tn, D), lambda i, j: (j, 0)), o_spec, o_spec],
        out_specs=[o_spec, o_spec],
        out_shape=[jax.ShapeDtypeStruct((S, F), _MXU), jax.ShapeDtypeStruct((S, F), _MXU)],
        compiler_params=_cp(("parallel", "parallel")))(dh2, w_down, gt, up)


def _rms_fwd(x, w, name):
    S, D = x.shape
    tr = _pick(S, (256, 128))

    def body(x_ref, w_ref, xn_ref, rs_ref):
        xv = x_ref[...]
        rs = lax.rsqrt(jnp.mean(xv * xv, axis=-1, keepdims=True) + EPS)
        xn_ref[...] = ((xv * rs) * w_ref[...]).astype(xn_ref.dtype)
        rs_ref[...] = rs

    return pl.pallas_call(
        body, name=name, grid=(S // tr,),
        in_specs=[pl.BlockSpec((tr, D), lambda i: (i, 0)), pl.BlockSpec((1, D), lambda i: (0, 0))],
        out_specs=[pl.BlockSpec((tr, D), lambda i: (i, 0)), pl.BlockSpec((tr, 1), lambda i: (i, 0))],
        out_shape=[jax.ShapeDtypeStruct((S, D), _MXU), jax.ShapeDtypeStruct((S, 1), F32)],
        compiler_params=_cp(("parallel",)))(x, w)


def _rms_bwd(dyn, x, rs, w, res, name):
    S, D = x.shape
    tr = _pick(S, (256, 128))

    def body(dy_ref, x_ref, rs_ref, w_ref, res_ref, dx_ref, dxb_ref, dw_ref):
        @pl.when(pl.program_id(0) == 0)
        def _():
            dw_ref[...] = jnp.zeros_like(dw_ref)

        dy, r = dy_ref[...].astype(F32), rs_ref[...]
        xhat = x_ref[...] * r
        dw_ref[...] += jnp.sum(dy * xhat, axis=0, keepdims=True)
        dxhat = dy * w_ref[...]
        dx = res_ref[...] + r * (dxhat - xhat * jnp.mean(dxhat * xhat, axis=-1, keepdims=True))
        dx_ref[...] = dx
        dxb_ref[...] = dx.astype(dxb_ref.dtype)

    row = pl.BlockSpec((tr, D), lambda i: (i, 0))
    vec = pl.BlockSpec((1, D), lambda i: (0, 0))
    return pl.pallas_call(
        body, name=name, grid=(S // tr,),
        in_specs=[row, row, pl.BlockSpec((tr, 1), lambda i: (i, 0)), vec, row], out_specs=[row, row, vec],
        out_shape=[jax.ShapeDtypeStruct((S, D), F32), jax.ShapeDtypeStruct((S, D), _MXU), jax.ShapeDtypeStruct((1, D), F32)],
        compiler_params=_cp(("arbitrary",)))(dyn, x, rs, w, res)


def _final_loss(h2, w, tgt):
    S, D = h2.shape
    tr = _pick(S, (256, 128))

    def body(h_ref, w_ref, t_ref, loss_ref, dh_ref, dhb_ref, dw_ref):
        @pl.when(pl.program_id(0) == 0)
        def _():
            dw_ref[...] = jnp.zeros_like(dw_ref)
            loss_ref[...] = jnp.zeros_like(loss_ref)

        hv, wv = h_ref[...], w_ref[...]
        rs = lax.rsqrt(jnp.mean(hv * hv, axis=-1, keepdims=True) + EPS)
        xhat = hv * rs
        err = xhat * wv - t_ref[...]
        row = jnp.mean(err * err, axis=-1, keepdims=True)
        loss_ref[...] += jnp.broadcast_to(0.5 * jnp.sum(row, axis=0, keepdims=True), loss_ref.shape)
        dy = err * (1.0 / D)
        dw_ref[...] += jnp.sum(dy * xhat, axis=0, keepdims=True)
        dxhat = dy * wv
        dh = rs * (dxhat - xhat * jnp.mean(dxhat * xhat, axis=-1, keepdims=True))
        dh_ref[...] = dh
        dhb_ref[...] = dh.astype(dhb_ref.dtype)

    row = pl.BlockSpec((tr, D), lambda i: (i, 0))
    vec = pl.BlockSpec((1, D), lambda i: (0, 0))
    return pl.pallas_call(
        body, name="final_loss", grid=(S // tr,), in_specs=[row, vec, row],
        out_specs=[pl.BlockSpec((1, LANE), lambda i: (0, 0)), row, row, vec],
        out_shape=[jax.ShapeDtypeStruct((1, LANE), F32), jax.ShapeDtypeStruct((S, D), F32), jax.ShapeDtypeStruct((S, D), _MXU),
                   jax.ShapeDtypeStruct((1, D), F32)],
        compiler_params=_cp(("arbitrary",)))(h2, w, tgt)


def _shift_rows(x, k, rows):
    if k == 0:
        return x
    S = x.shape[0]
    r = pltpu.roll(x, k % S, axis=0)
    ok = (rows >= k) if k > 0 else (rows < S + k)
    return jnp.where(ok, r, 0.0)


XBC_COL0 = SSD_WIDTH // 128


def _conv_fwd(proj, conv_w, conv_b):
    S = proj.shape[0]
    nct = CONV_CH // 128

    def body(x_ref, w_ref, b_ref, o_ref):
        x = x_ref[...]
        rows = lax.broadcasted_iota(jnp.int32, x.shape, 0)
        c = b_ref[...] + w_ref[3:4, :] * x
        for k in range(1, CONV_K):
            c = c + w_ref[3 - k:4 - k, :] * _shift_rows(x, k, rows)
        o_ref[...] = c * _sigmoid(c)

    return pl.pallas_call(
        body, name="conv_fwd", grid=(nct,),
        in_specs=[pl.BlockSpec((S, 128), lambda j: (0, XBC_COL0 + j)), pl.BlockSpec((CONV_K, 128), lambda j: (0, j)),
                  pl.BlockSpec((1, 128), lambda j: (0, j))],
        out_specs=pl.BlockSpec((S, 128), lambda j: (0, j)),
        out_shape=jax.ShapeDtypeStruct((S, CONV_CH), F32), compiler_params=_cp(("parallel",)))(proj, conv_w, conv_b)


def _conv_bwd(proj, conv_w, conv_b, dxa):
    S = proj.shape[0]
    nct = CONV_CH // 128

    def body(x_ref, w_ref, b_ref, d_ref, dx_ref, dw_ref, db_ref):
        x = x_ref[...]
        rows = lax.broadcasted_iota(jnp.int32, x.shape, 0)
        xs = [_shift_rows(x, k, rows) for k in range(CONV_K)]
        c = b_ref[...] + w_ref[3:4, :] * x
        for k in range(1, CONV_K):
            c = c + w_ref[3 - k:4 - k, :] * xs[k]
        s = _sigmoid(c)
        dc = d_ref[...] * (s * (1.0 + c * (1.0 - s)))
        dx = w_ref[3:4, :] * dc
        for k in range(1, CONV_K):
            dx = dx + w_ref[3 - k:4 - k, :] * _shift_rows(dc, -k, rows)
        dx_ref[...] = dx.astype(dx_ref.dtype)
        for k in range(CONV_K):
            dw_ref[3 - k:4 - k, :] = jnp.sum(dc * xs[k], axis=0, keepdims=True)
        db_ref[...] = jnp.sum(dc, axis=0, keepdims=True)

    col = pl.BlockSpec((S, 128), lambda j: (0, j))
    return pl.pallas_call(
        body, name="conv_bwd", grid=(nct,),
        in_specs=[pl.BlockSpec((S, 128), lambda j: (0, XBC_COL0 + j)), pl.BlockSpec((CONV_K, 128), lambda j: (0, j)),
                  pl.BlockSpec((1, 128), lambda j: (0, j)), col],
        out_specs=[col, pl.BlockSpec((CONV_K, 128), lambda j: (0, j)), pl.BlockSpec((1, 128), lambda j: (0, j))],
        out_shape=[jax.ShapeDtypeStruct((S, CONV_CH), _MXU), jax.ShapeDtypeStruct((CONV_K, CONV_CH), F32),
                   jax.ShapeDtypeStruct((1, CONV_CH), F32)],
        compiler_params=_cp(("parallel",)))(proj, conv_w, conv_b, dxa)


def _ssd_consts():
    L = SSD_L
    r = lax.broadcasted_iota(jnp.int32, (L, L), 0)
    c = lax.broadcasted_iota(jnp.int32, (L, L), 1)
    causal = r >= c
    upper = (r <= c).astype(F32)
    hr = lax.broadcasted_iota(jnp.int32, (SSD_HEADS, SSD_WIDTH), 0)
    hc = lax.broadcasted_iota(jnp.int32, (SSD_HEADS, SSD_WIDTH), 1)
    expand = (lax.shift_right_logical(hc, 6) == hr).astype(F32)
    return causal, causal.astype(F32), upper, expand


def _softplus(x):
    return jnp.maximum(x, 0.0) + jnp.log(1.0 + jnp.exp(-jnp.abs(x)))


def _ssd_scalars(dtr, dt_bias, a_log, tri, upper, expand):
    dt = _softplus(dtr + dt_bias)
    A = -jnp.exp(a_log)
    adt = dt * A
    acum = _dot(tri, adt, "nn", split="b")
    acum_t = _dot(adt, upper, "tn", split="a")
    alast = acum[SSD_L - 1:SSD_L, :]
    e = jnp.exp(acum)
    wdec = jnp.exp(alast - acum)
    gam = jnp.exp(alast)
    ex = lambda t: _dot(t, expand, "nn", split="a")
    gam8 = jnp.broadcast_to(gam, (8, SSD_HEADS))
    return dt, A, acum, acum_t, e, wdec, gam, ex(dt), ex(e), ex(wdec), ex(gam8)[0:1, :]


def _ssd_fwd(proj, proj_small, xa, dt_bias, a_log, d_skip, norm_w):
    S = proj.shape[0]
    L, N, W = SSD_L, SSD_N, SSD_WIDTH
    nc = S // L

    def body(z_ref, xa_ref, dtr_ref, dtb_ref, al_ref, dsk_ref, nw_ref, yo_ref, y_ref, rs_ref, hs_ref, h_scr, y_scr):
        @pl.when(pl.program_id(0) == 0)
        def _():
            h_scr[...] = jnp.zeros_like(h_scr)

        causal, tri, upper, expand = _ssd_consts()
        dt, A, acum, acum_t, e, wdec, gam, dtE, eE, wE, gamE = _ssd_scalars(dtr_ref[:, 0:SSD_HEADS], dtb_ref[...], al_ref[...], tri, upper, expand)
        xs = xa_ref[:, 0:W]
        X = xs * dtE
        XW = X * wE
        hs_ref[0] = h_scr[...]
        for g in range(SSD_G):
            gs = slice(g * 512, (g + 1) * 512)
            Bg = xa_ref[:, W + g * N:W + (g + 1) * N]
            Cg = xa_ref[:, W + SSD_G * N + g * N:W + SSD_G * N + (g + 1) * N]
            Hg = h_scr[:, gs]
            CB = _dot(Cg, Bg, "nt")
            yoff = _dot(Cg, Hg, "nn") * eE[:, gs]
            st = _dot(Bg, XW[:, gs], "tn")
            for j in range(8):
                h = g * 8 + j
                hsl = slice(h * SSD_P, (h + 1) * SSD_P)
                lam = jnp.exp(jnp.where(causal, acum[:, h:h + 1] - acum_t[h:h + 1, :], -jnp.inf))
                y_scr[:, hsl] = _dot(CB * lam, X[:, hsl], "nn") + yoff[:, j * SSD_P:(j + 1) * SSD_P]
            h_scr[:, gs] = gamE[:, gs] * Hg + st
        dskE = _dot(jnp.broadcast_to(dsk_ref[...], (8, SSD_HEADS)), expand, "nn", split="a")[0:1, :]
        y = y_scr[...] + dskE * xs
        y_ref[...] = y
        zv = z_ref[...]
        yg = y * (zv * _sigmoid(zv))
        rs = lax.rsqrt(jnp.mean(yg * yg, axis=-1, keepdims=True) + EPS)
        rs_ref[...] = rs
        yo_ref[...] = ((yg * rs) * nw_ref[...]).astype(yo_ref.dtype)

    p16 = pl.BlockSpec((1, SSD_HEADS), lambda c: (0, 0))
    return pl.pallas_call(
        body, name="ssd_fwd", grid=(nc,),
        in_specs=[pl.BlockSpec((L, W), lambda c: (c, 0)), pl.BlockSpec((L, CONV_CH), lambda c: (c, 0)),
                  pl.BlockSpec((L, W_SMALL), lambda c: (c, 0)), p16, p16, p16, pl.BlockSpec((1, W), lambda c: (0, 0))],
        out_specs=[pl.BlockSpec((L, W), lambda c: (c, 0)), pl.BlockSpec((L, W), lambda c: (c, 0)),
                   pl.BlockSpec((L, 1), lambda c: (c, 0)), pl.BlockSpec((1, N, W), lambda c: (c, 0, 0))],
        out_shape=[jax.ShapeDtypeStruct((S, W), _MXU), jax.ShapeDtypeStruct((S, W), F32), jax.ShapeDtypeStruct((S, 1), F32),
                   jax.ShapeDtypeStruct((nc, N, W), F32)],
        scratch_shapes=[pltpu.VMEM((N, W), F32), pltpu.VMEM((L, W), F32)],
        compiler_params=_cp(("arbitrary",)))(proj, xa, proj_small, dt_bias, a_log, d_skip, norm_w)


def _ssd_bwd(dmixed, proj, proj_small, xa, y, rs2, hs, dt_bias, a_log, d_skip, norm_w):
    S = proj.shape[0]
    L, N, W, H = SSD_L, SSD_N, SSD_WIDTH, SSD_HEADS
    nc = S // L

    def body(dyo_ref, z_ref, xa_ref, dtr_ref, y_ref, rs_ref, hs_ref, dtb_ref, al_ref, dsk_ref, nw_ref,
             dz_ref, dxa_ref, ddtr_ref, ddtb_ref, dal_ref, ddsk_ref, dnw_ref, dh_scr, dx_scr):
        @pl.when(pl.program_id(0) == 0)
        def _():
            dh_scr[...] = jnp.zeros_like(dh_scr)
            ddtb_ref[...] = jnp.zeros_like(ddtb_ref)
            dal_ref[...] = jnp.zeros_like(dal_ref)
            ddsk_ref[...] = jnp.zeros_like(ddsk_ref)
            dnw_ref[...] = jnp.zeros_like(dnw_ref)

        causal, tri, upper, expand = _ssd_consts()
        heads = lambda t: _dot(t, expand, "nt", split="a")
        onehot = lambda h: (lax.broadcasted_iota(jnp.int32, (1, H), 1) == h).astype(F32)

        zv, yv, rs = z_ref[...], y_ref[...], rs_ref[...]
        sz = _sigmoid(zv)
        zs = zv * sz
        xhat = (yv * zs) * rs
        dyo = dyo_ref[...].astype(F32)
        dnw_ref[...] += jnp.sum(dyo * xhat, axis=0, keepdims=True)
        dxhat = dyo * nw_ref[...]
        dyg = rs * (dxhat - xhat * jnp.mean(dxhat * xhat, axis=-1, keepdims=True))
        dz_ref[...] = (dyg * yv * (sz * (1.0 + zv * (1.0 - sz)))).astype(dz_ref.dtype)
        dy = dyg * zs

        dtr = dtr_ref[:, 0:H]
        dt, A, acum, acum_t, e, wdec, gam, dtE, eE, wE, gamE = _ssd_scalars(dtr, dtb_ref[...], al_ref[...], tri, upper, expand)
        xs = xa_ref[:, 0:W]
        X = xs * dtE
        XW = X * wE
        dskE = _dot(jnp.broadcast_to(dsk_ref[...], (8, H)), expand, "nn", split="a")[0:1, :]
        ddsk_ref[...] += heads(jnp.broadcast_to(jnp.sum(dy * xs, axis=0, keepdims=True), (8, W)))[0:1, :]

        dYe = dy * eE
        dacum = jnp.zeros((L, H), F32)
        de_full = []
        dw_full = []
        dgam_full = []
        for g in range(SSD_G):
            gs = slice(g * 512, (g + 1) * 512)
            Bg = xa_ref[:, W + g * N:W + (g + 1) * N]
            Cg = xa_ref[:, W + SSD_G * N + g * N:W + SSD_G * N + (g + 1) * N]
            Hg = hs_ref[0, :, gs]
            dHn = dh_scr[:, gs]
            CH = _dot(Cg, Hg, "nn")
            de_full.append(dy[:, gs] * CH)
            dC = _dot(dYe[:, gs], Hg, "nt")
            dHs = gamE[:, gs] * dHn + _dot(Cg, dYe[:, gs], "tn")
            dgam_full.append(jnp.sum(dHn * Hg, axis=0, keepdims=True))
            BdS = _dot(Bg, dHn, "nn")
            dB = _dot(XW[:, gs], dHn, "nt")
            dx_scr[:, gs] = BdS * wE[:, gs]
            dw_full.append(BdS * X[:, gs])
            CB = _dot(Cg, Bg, "nt")
            dCB = jnp.zeros((L, L), F32)
            for j in range(8):
                h = g * 8 + j
                hsl = slice(h * SSD_P, (h + 1) * SSD_P)
                lam = jnp.exp(jnp.where(causal, acum[:, h:h + 1] - acum_t[h:h + 1, :], -jnp.inf))
                M = CB * lam
                dM = _dot(dy[:, hsl], X[:, hsl], "nt")
                dx_scr[:, hsl] += _dot(M, dy[:, hsl], "tn")
                dCB = dCB + dM * lam
                Q = dM * M
                rowsum = jnp.sum(Q, axis=1, keepdims=True)
                colsum = _dot(Q, jnp.ones((L, 8), F32), "tn", split="a")[:, 0:1]
                dacum = dacum + (rowsum - colsum) * onehot(h)
            dC = dC + _dot(dCB, Bg, "nn")
            dB = dB + _dot(dCB, Cg, "tn")
            dxa_ref[:, W + g * N:W + (g + 1) * N] = dB
            dxa_ref[:, W + SSD_G * N + g * N:W + SSD_G * N + (g + 1) * N] = dC
            dh_scr[:, gs] = dHs

        de16 = heads(jnp.concatenate(de_full, axis=1))
        dw16 = heads(jnp.concatenate(dw_full, axis=1))
        dgam16 = heads(jnp.broadcast_to(jnp.concatenate(dgam_full, axis=1), (8, W)))[0:1, :]
        dacum = dacum + de16 * e - dw16 * wdec
        dlast = jnp.sum(dw16 * wdec, axis=0, keepdims=True) + dgam16 * gam
        lastrow = (lax.broadcasted_iota(jnp.int32, (L, 1), 0) == L - 1).astype(F32)
        dacum = dacum + lastrow * dlast
        da = _dot(tri, dacum, "tn", split="b")
        dX = dx_scr[...]
        ddt = da * A + heads(dX * xs)
        dA = jnp.sum(da * dt, axis=0, keepdims=True)
        dal_ref[...] += dA * A
        ddtr = ddt * _sigmoid(dtr + dtb_ref[...])
        ddtb_ref[...] += jnp.sum(ddtr, axis=0, keepdims=True)
        ddtr_ref[...] = ddtr
        dxa_ref[:, 0:W] = dX * dtE + dy * dskE

    p16 = pl.BlockSpec((1, H), lambda c: (0, 0))
    rev = lambda c: (nc - 1 - c, 0)
    return pl.pallas_call(
        body, name="ssd_bwd", grid=(nc,),
        in_specs=[pl.BlockSpec((L, W), rev), pl.BlockSpec((L, W), rev), pl.BlockSpec((L, CONV_CH), rev),
                  pl.BlockSpec((L, W_SMALL), rev), pl.BlockSpec((L, W), rev), pl.BlockSpec((L, 1), rev),
                  pl.BlockSpec((1, N, W), lambda c: (nc - 1 - c, 0, 0)), p16, p16, p16, pl.BlockSpec((1, W), lambda c: (0, 0))],
        out_specs=[pl.BlockSpec((L, W), rev), pl.BlockSpec((L, CONV_CH), rev), pl.BlockSpec((L, H), rev),
                   p16, p16, p16, pl.BlockSpec((1, W), lambda c: (0, 0))],
        out_shape=[jax.ShapeDtypeStruct((S, W), _MXU), jax.ShapeDtypeStruct((S, CONV_CH), F32), jax.ShapeDtypeStruct((S, H), F32),
                   jax.ShapeDtypeStruct((1, H), F32), jax.ShapeDtypeStruct((1, H), F32), jax.ShapeDtypeStruct((1, H), F32),
                   jax.ShapeDtypeStruct((1, W), F32)],
        scratch_shapes=[pltpu.VMEM((N, W), F32), pltpu.VMEM((L, W), F32)],
        compiler_params=_cp(("arbitrary",)))(dmixed, proj, xa, proj_small, y, rs2, hs, dt_bias, a_log, d_skip, norm_w)


def _rope_tables(S):
    inv = 1.0 / (ROPE_THETA ** (jnp.arange(0, ROPE_DIM, 2, dtype=F32) / ROPE_DIM))
    ang = jnp.arange(S, dtype=F32)[:, None] * inv[None, :]
    cos, sin = jnp.cos(ang), jnp.sin(ang)
    half = ROPE_DIM // 2
    c64 = jnp.concatenate([cos, cos, jnp.ones((S, HD - ROPE_DIM), F32)], axis=1)
    s64 = jnp.concatenate([sin, sin, jnp.zeros((S, HD - ROPE_DIM), F32)], axis=1)
    del half
    return jnp.concatenate([c64, c64], axis=1), jnp.concatenate([s64, s64], axis=1)


def _rope(xs, blk0, width, cos, sin, sign, out_dtype, name, extra=None):
    S = xs[0].shape[0]
    tr = _pick(S, (512, 256, 128))
    nx = len(xs)

    def body(*refs):
        x_refs, c_ref, s_ref = refs[:nx], refs[nx], refs[nx + 1]
        e_ref = refs[nx + 2] if extra is not None else None
        o_ref = refs[-1]
        cv, sv = c_ref[...], s_ref[...] * sign
        lane = lax.broadcasted_iota(jnp.int32, (tr, 128), 1)
        first = (lane & (HD - 1)) < (ROPE_DIM // 2)
        for j in range(bw // 128):
            cs = slice(j * 128, (j + 1) * 128)
            xv = x_refs[0][:, cs].astype(F32)
            for r in x_refs[1:]:
                xv = xv + r[:, cs].astype(F32)
            out = _rotate128(xv, cv, sv, first)
            if extra is not None:
                out = out + e_ref[:, cs].astype(F32)
            o_ref[:, cs] = out.astype(out_dtype)

    bw = 512
    assert width % bw == 0 and (blk0 * 256) % bw == 0
    b0 = blk0 * 256 // bw
    t128 = pl.BlockSpec((tr, 128), lambda i, j: (i, 0))
    oblk = pl.BlockSpec((tr, bw), lambda i, j: (i, j))
    specs = [pl.BlockSpec((tr, bw), lambda i, j: (i, b0 + j))] * nx + [t128, t128]
    ins = list(xs) + [cos, sin]
    if extra is not None:
        assert (extra[1] * 256) % bw == 0
        ins.append(extra[0])
        eb = extra[1] * 256 // bw
        specs.append(pl.BlockSpec((tr, bw), lambda i, j: (i, eb + j)))
    return pl.pallas_call(
        body, name=name, grid=(S // tr, width // bw), in_specs=specs, out_specs=oblk,
        out_shape=jax.ShapeDtypeStruct((S, width), out_dtype), compiler_params=_cp(("parallel", "parallel")))(*ins)


def _rotate128(xv, cv, sv, first):
    rot = jnp.where(first, -pltpu.roll(xv, 128 - ROPE_DIM // 2, axis=1), pltpu.roll(xv, ROPE_DIM // 2, axis=1))
    return xv * cv + rot * sv


def _kv_prep(proj, cos, sin, tk):
    S = proj.shape[0]
    NB = S // SEL_BLOCK

    def body(ks_ref, vs_ref, kw_ref, vw_ref, c_ref, s_ref, *outs):
        cv, sv = c_ref[...], s_ref[...]
        lane = lax.broadcasted_iota(jnp.int32, (tk, 128), 1)
        first = (lane & (HD - 1)) < (ROPE_DIM // 2)
        key = pl.program_id(0) * tk + lax.broadcasted_iota(jnp.int32, (tk, NB), 0)
        onehot = (lax.shift_right_logical(key, 6) == lax.broadcasted_iota(jnp.int32, (tk, NB), 1)).astype(F32)
        for j, (ref, rotated) in enumerate(((ks_ref, True), (vs_ref, False), (kw_ref, True), (vw_ref, False))):
            nat, blk = outs[2 * j], outs[2 * j + 1]
            for half in range(2):
                xv = ref[:, half * 128:(half + 1) * 128]
                if rotated:
                    xv = _rotate128(xv, cv, sv, first)
                for e in range(2):
                    h = 2 * half + e
                    piece = xv[:, e * HD:(e + 1) * HD]
                    nat[h] = (jnp.concatenate([piece, onehot], axis=1) if j == 0 else piece).astype(nat.dtype)
                    blk[h, 0] = piece.T.astype(blk.dtype)

    col = lambda b: pl.BlockSpec((tk, 256), lambda i: (i, b))
    t128 = pl.BlockSpec((tk, 128), lambda i: (i, 0))
    nat_spec = lambda w: pl.BlockSpec((N_KV, tk, w), lambda i: (0, i, 0))
    blk_spec = pl.BlockSpec((N_KV, 1, HD, tk), lambda i: (0, i, 0, 0))
    nat_shape = lambda w: jax.ShapeDtypeStruct((N_KV, S, w), _MXU)
    blk_shape = jax.ShapeDtypeStruct((N_KV, S // tk, HD, tk), _MXU)
    widths = (HD + NB, HD, HD, HD)
    res = pl.pallas_call(
        body, name="kv_prep", grid=(S // tk,), in_specs=[col(KSB), col(VSB), col(KWB), col(VWB), t128, t128],
        out_specs=[s for w in widths for s in (nat_spec(w), blk_spec)],
        out_shape=[s for w in widths for s in (nat_shape(w), blk_shape)],
        compiler_params=_cp(("parallel",)))(proj, proj, proj, proj, cos, sin)
    return dict(ks_ext=res[0], ks_t=res[1], vs=res[2], vs_t=res[3], kw=res[4], kw_t=res[5], vw=res[6], vw_t=res[7])


def _dkv_post(dks, dvs, dkw, dvw, cos, sin):
    S = dks.shape[1]
    tr = _pick(S, (512, 256, 128))

    def body(dks_ref, dvs_ref, dkw_ref, dvw_ref, c_ref, s_ref, o_ref):
        cv, sv = c_ref[...], -s_ref[...]
        lane = lax.broadcasted_iota(jnp.int32, (tr, 128), 1)
        first = (lane & (HD - 1)) < (ROPE_DIM // 2)
        for j, (ref, rotated) in enumerate(((dks_ref, True), (dvs_ref, False), (dkw_ref, True), (dvw_ref, False))):
            for half in range(2):
                xv = jnp.concatenate([ref[2 * half], ref[2 * half + 1]], axis=1)
                if rotated:
                    xv = _rotate128(xv, cv, sv, first)
                o_ref[:, j * 256 + half * 128:j * 256 + (half + 1) * 128] = xv.astype(o_ref.dtype)

    hm = pl.BlockSpec((N_KV, tr, HD), lambda i: (0, i, 0))
    t128 = pl.BlockSpec((tr, 128), lambda i: (i, 0))
    return pl.pallas_call(
        body, name="dkv_post", grid=(S // tr,), in_specs=[hm, hm, hm, hm, t128, t128],
        out_specs=pl.BlockSpec((tr, 4 * 256), lambda i: (i, 0)), out_shape=jax.ShapeDtypeStruct((S, 4 * 256), _MXU),
        compiler_params=_cp(("parallel",)))(dks, dvs, dkw, dvw, cos, sin)


def _compress_fwd(R, pe, w1, w2):
    NC = R.shape[1]
    half = 16 * HD

    def body(r_ref, pe_ref, w1_ref, w2_ref, o_ref, hid_ref):
        r = r_ref[0]
        a = _dot(r + pe_ref[:, 0:half], w1_ref[0:half, :], "nn")
        b = _dot(r + pe_ref[:, half:2 * half], w1_ref[half:2 * half, :], "nn")
        hid = a + pltpu.roll(b, NC - 1, axis=0)
        hid_ref[0] = hid
        out = _dot(hid * _sigmoid(hid), w2_ref[...], "nn")
        rows = lax.broadcasted_iota(jnp.int32, out.shape, 0)
        o_ref[0] = jnp.where(rows < NC - 1, out, 0.0).astype(o_ref.dtype)

    return pl.pallas_call(
        body, name="compress_fwd", grid=(N_KV,),
        in_specs=[pl.BlockSpec((1, NC, half), lambda h: (h, 0, 0)), pl.BlockSpec((1, 2 * half), lambda h: (0, 0)),
                  pl.BlockSpec((2 * half, CMP_HID), lambda h: (0, 0)), pl.BlockSpec((CMP_HID, HD), lambda h: (0, 0))],
        out_specs=[pl.BlockSpec((1, NC, HD), lambda h: (h, 0, 0)), pl.BlockSpec((1, NC, CMP_HID), lambda h: (h, 0, 0))],
        out_shape=[jax.ShapeDtypeStruct((N_KV, NC, HD), _MXU), jax.ShapeDtypeStruct((N_KV, NC, CMP_HID), F32)],
        compiler_params=_cp(("parallel",)))(R, pe, w1, w2)


def _compress_bwd(R, pe, w1, w2, hid, dout):
    NC = R.shape[1]
    half = 16 * HD

    def body(r_ref, pe_ref, w1_ref, w2_ref, hid_ref, do_ref, dr_ref, dw1_ref, dw2_ref, dpe_ref):
        @pl.when(pl.program_id(0) == 0)
        def _():
            dw1_ref[...] = jnp.zeros_like(dw1_ref)
            dw2_ref[...] = jnp.zeros_like(dw2_ref)
            dpe_ref[...] = jnp.zeros_like(dpe_ref)

        r, hv, do = r_ref[0], hid_ref[0], do_ref[0]
        s = _sigmoid(hv)
        dw2_ref[...] += _dot(hv * s, do, "tn")
        dhid = _dot(do, w2_ref[...], "nt") * (s * (1.0 + hv * (1.0 - s)))
        rows = lax.broadcasted_iota(jnp.int32, dhid.shape, 0)
        dhid = jnp.where(rows < NC - 1, dhid, 0.0)
        dhid_dn = pltpu.roll(dhid, 1, axis=0)
        dw1_ref[0:half, :] += _dot(r + pe_ref[:, 0:half], dhid, "tn")
        dw1_ref[half:2 * half, :] += _dot(r + pe_ref[:, half:2 * half], dhid_dn, "tn")
        dxt = _dot(dhid, w1_ref[0:half, :], "nt")
        dxb = _dot(dhid_dn, w1_ref[half:2 * half, :], "nt")
        dr_ref[0] = dxt + dxb
        dpe_ref[:, 0:half] += jnp.sum(dxt, axis=0, keepdims=True)
        dpe_ref[:, half:2 * half] += jnp.sum(dxb, axis=0, keepdims=True)

    return pl.pallas_call(
        body, name="compress_bwd", grid=(N_KV,),
        in_specs=[pl.BlockSpec((1, NC, half), lambda h: (h, 0, 0)), pl.BlockSpec((1, 2 * half), lambda h: (0, 0)),
                  pl.BlockSpec((2 * half, CMP_HID), lambda h: (0, 0)), pl.BlockSpec((CMP_HID, HD), lambda h: (0, 0)),
                  pl.BlockSpec((1, NC, CMP_HID), lambda h: (h, 0, 0)), pl.BlockSpec((1, NC, HD), lambda h: (h, 0, 0))],
        out_specs=[pl.BlockSpec((1, NC, half), lambda h: (h, 0, 0)), pl.BlockSpec((2 * half, CMP_HID), lambda h: (0, 0)),
                   pl.BlockSpec((CMP_HID, HD), lambda h: (0, 0)), pl.BlockSpec((1, 2 * half), lambda h: (0, 0))],
        out_shape=[jax.ShapeDtypeStruct((N_KV, NC, half), F32), jax.ShapeDtypeStruct((2 * half, CMP_HID), F32),
                   jax.ShapeDtypeStruct((CMP_HID, HD), F32), jax.ShapeDtypeStruct((1, 2 * half), F32)],
        compiler_params=_cp(("arbitrary",)))(R, pe, w1, w2, hid, dout)


def _attn_cfg(S, Sk, mode):
    tk = _pick(Sk, (256, 128))
    if mode == "cmp":
        return _pick(S, (512, 256, 128)), Sk
    if mode == "sel" and S % (2 * tk) == 0:
        return 2 * tk, tk
    return tk, tk


def _block_start(kb, tk):
    return kb * tk if isinstance(kb, int) else pl.multiple_of(kb * tk, tk)


def _pipelined_key_blocks(mode, q0, tq, tk, produce, consume):
    if mode == "cmp":
        produce(0, True, 0)
        consume(0, 0)
        return
    if mode == "win":
        assert tq == tk and WINDOW == 2 * tk
        last = q0 // tk
        first = jnp.maximum(last - 2, 0)

        @pl.when(last == 0)
        def _():
            produce(last, True, 0)
            consume(last, 0)

        @pl.when(last == 1)
        def _():
            produce(first, True, 0)
            produce(last, True, 1)
            consume(first, 0)
            consume(last, 1)

        @pl.when(last >= 2)
        def _():
            produce(first, True, 0)
            produce(first + 1, False, 1)
            consume(first, 0)
            produce(last, True, 0)
            consume(first + 1, 1)
            consume(last, 0)

        return
    first, n_plain, plain_masked = 0, q0 // tk, False
    pairs = jnp.maximum(n_plain - 1, 0) // 2
    if tq == 2 * tk:
        @pl.when(n_plain >= 1)
        def _():
            produce(0, False, 0)

        def two_plain(j, carry):
            produce(2 * j + 1, False, 1)
            consume(2 * j, 0)
            produce(2 * j + 2, False, 0)
            consume(2 * j + 1, 1)
            return carry

        lax.fori_loop(0, pairs, two_plain, 0)
        kb = 2 * pairs

        @pl.when(n_plain >= 2)
        def _():
            produce(kb + 1, False, 1)
            consume(kb, 0)
            produce(n_plain, True, 0)
            consume(kb + 1, 1)
            produce(n_plain + 1, True, 1)
            consume(n_plain, 0)
            consume(n_plain + 1, 1)

        @pl.when(n_plain == 0)
        def _():
            produce(0, True, 0)
            produce(1, True, 1)
            consume(0, 0)
            consume(1, 1)

        return
    assert tq == tk
    last = first + n_plain

    @pl.when(n_plain >= 1)
    def _():
        produce(first, plain_masked, 0)

    def two(j, carry):
        kb = first + 2 * j
        produce(kb + 1, plain_masked, 1)
        consume(kb, 0)
        produce(kb + 2, plain_masked, 0)
        consume(kb + 1, 1)
        return carry

    lax.fori_loop(0, pairs, two, 0)
    kb = first + 2 * pairs
    left = n_plain - 2 * pairs

    @pl.when(left == 2)
    def _():
        produce(kb + 1, plain_masked, 1)
        consume(kb, 0)
        produce(last, True, 0)
        consume(kb + 1, 1)
        consume(last, 0)

    @pl.when(left == 1)
    def _():
        produce(last, True, 1)
        consume(kb, 0)
        consume(last, 1)

    @pl.when(left == 0)
    def _():
        produce(last, True, 0)
        consume(last, 0)


def _attn_bias(mode, q0, k0, tq, tk):
    k = k0 + lax.broadcasted_iota(jnp.int32, (tk, tq), 0)
    t = q0 + lax.broadcasted_iota(jnp.int32, (tk, tq), 1)
    if mode == "cmp":
        ok = (k * 16 + 31) <= t
    elif mode == "win":
        ok = (k <= t) & ((t - k) < WINDOW)
    else:
        ok = k <= t
    bias = jnp.where(ok, 0.0, NEG)
    return jnp.concatenate([bias] * GRP, axis=1), jnp.concatenate([ok.astype(F32)] * GRP, axis=1)


def _sel_operands(qs, selneg_ref):
    return jnp.concatenate([qs, jnp.concatenate([selneg_ref[0]] * GRP, axis=0)], axis=1)


def _stack_heads(ref, tq):
    return jnp.concatenate([ref[:, g * HD:(g + 1) * HD] for g in range(GRP)], axis=0)


def _scaled_queries(q_ref, tq):
    return (_stack_heads(q_ref, tq).astype(F32) * SCALE).astype(_MXU)


def _blocked_t(x, tk):
    n, Sk, d = x.shape
    return x.reshape(n, Sk // tk, tk, d).transpose(0, 1, 3, 2)


def _head_rows(ref):
    return jnp.concatenate([ref[0, g:g + 1, :] for g in range(GRP)], axis=1)


def _attn_fwd(q, qcol0, k, vt, mode, selneg, gate, y_prev, y_dtype, name):
    S, Sk = q.shape[0], k.shape[1]
    tq, tk = _attn_cfg(S, Sk, mode)
    R = GRP * tq
    NB = S // SEL_BLOCK

    def body(*refs):
        q_ref, k_ref, vt_ref = refs[:3]
        rest = list(refs[3:])
        sel_ref = rest.pop(0) if mode == "sel" else None
        ov_ref = rest.pop(0) if mode == "cmp" else None
        gate_ref = rest.pop(0)
        yp_ref = rest.pop(0) if y_prev is not None else None
        o_ref, lse_ref, y_ref = rest[:3]
        choice_ref = rest[3] if mode == "cmp" else None
        m_scr, l_scr, acc, s_scr = rest[-4:]
        q0 = pl.program_id(1) * tq
        qs = _scaled_queries(q_ref, tq)
        m_scr[...] = jnp.full_like(m_scr, NEG)
        l_scr[...] = jnp.zeros_like(l_scr)
        acc[...] = jnp.zeros_like(acc)
        qk = _sel_operands(qs, sel_ref) if mode == "sel" else qs

        def produce(kb, masked, slot):
            k0 = _block_start(kb, tk)
            s = _dot(k_ref[0, pl.ds(k0, tk), :], qk, "nt")
            if masked:
                s = s + _attn_bias(mode, q0, k0, tq, tk)[0]
            s_scr[slot] = s

        def consume(kb, slot):
            s = s_scr[slot]
            m_old = m_scr[...]
            m_new = jnp.maximum(m_old, jnp.max(s, axis=0, keepdims=True))
            p = jnp.exp(s - m_new)
            if mode == "cmp":
                p = p * _attn_bias(mode, q0, 0, tq, tk)[1]
            alpha = jnp.exp(m_old - m_new)
            l_scr[...] = alpha * l_scr[...] + jnp.sum(p, axis=0, keepdims=True)
            acc[...] = alpha * acc[...] + _dot(vt_ref[0, kb], p, "nn")
            m_scr[...] = m_new

        _pipelined_key_blocks(mode, q0, tq, tk, produce, consume)
        l = l_scr[...]
        good = l > 0.0
        o_t = acc[...] * jnp.where(good, 1.0 / jnp.where(good, l, 1.0), 0.0)
        lse = jnp.where(good, m_scr[...] + jnp.log(jnp.where(good, l, 1.0)), -NEG)
        y_t = o_t * _sigmoid(_head_rows(gate_ref))
        if mode == "cmp":
            p = jnp.exp(s_scr[0] - lse) * _attn_bias(mode, q0, 0, tq, tk)[1]
            choice_ref[0] = _chosen_blocks(p, ov_ref[...], q0, tq).astype(choice_ref.dtype)
        for g in range(GRP):
            hs, qs_ = slice(g * HD, (g + 1) * HD), slice(g * tq, (g + 1) * tq)
            o_ref[:, hs] = o_t[:, qs_].T
            lse_ref[0, g:g + 1, :] = lse[:, qs_]
            yg = y_t[:, qs_].T
            if y_prev is not None:
                yg = yg + yp_ref[:, hs]
            y_ref[:, hs] = yg.astype(y_ref.dtype)

    row_spec = pl.BlockSpec((1, GRP, tq), lambda h, i: (h, 0, i))
    qo_spec = pl.BlockSpec((tq, GRP * HD), lambda h, i: (i, h))
    ins = [q, k, vt]
    specs = [pl.BlockSpec((tq, GRP * HD), lambda h, i: (i, qcol0 + h)), pl.BlockSpec((1, Sk, k.shape[2]), lambda h, i: (h, 0, 0)),
             pl.BlockSpec((1, Sk // tk, HD, tk), lambda h, i: (h, 0, 0, 0))]
    if mode == "sel":
        ins.append(selneg)
        specs.append(pl.BlockSpec((1, tq, selneg.shape[2]), lambda h, i: (h, i, 0)))
    out_specs = [qo_spec, row_spec, qo_spec]
    out_shape = [jax.ShapeDtypeStruct((S, ATT_WIDTH), F32), jax.ShapeDtypeStruct((N_KV, GRP, S), F32),
                 jax.ShapeDtypeStruct((S, ATT_WIDTH), y_dtype)]
    if mode == "cmp":
        ins.append(_block_overlap(Sk, NB))
        specs.append(pl.BlockSpec((NB, Sk), lambda h, i: (0, 0)))
        out_specs.append(pl.BlockSpec((1, tq, NB), lambda h, i: (h, i, 0)))
        out_shape.append(jax.ShapeDtypeStruct((N_KV, S, NB), _MXU))
    ins.append(gate)
    specs.append(row_spec)
    if y_prev is not None:
        ins.append(y_prev)
        specs.append(qo_spec)
    return pl.pallas_call(
        body, name=name, grid=(N_KV, S // tq), in_specs=specs, out_specs=out_specs, out_shape=out_shape,
        scratch_shapes=[pltpu.VMEM((1, R), F32), pltpu.VMEM((1, R), F32), pltpu.VMEM((HD, R), F32), pltpu.VMEM((2, tk, R), F32)],
        compiler_params=_cp(("parallel", "arbitrary")))(*ins)


def _attn_bwd(q, qcol0, k, kt, v, o, lse, dy, dycol0, gate, mode, selneg, name):
    S, Sk = q.shape[0], k.shape[1]
    tq, tk = _attn_cfg(S, Sk, mode)
    R = GRP * tq

    def body(*refs):
        if mode == "sel":
            (q_ref, k_ref, kt_ref, v_ref, o_ref, lse_ref, dy_ref, gate_ref, sel_ref, dq_ref, dk_ref, dv_ref, dg_ref, dq_scr, s_scr,
             dp_scr) = refs
        else:
            q_ref, k_ref, kt_ref, v_ref, o_ref, lse_ref, dy_ref, gate_ref, dq_ref, dk_ref, dv_ref, dg_ref, dq_scr, s_scr, dp_scr = refs

        @pl.when(pl.program_id(1) == 0)
        def _():
            dk_ref[...] = jnp.zeros_like(dk_ref)
            dv_ref[...] = jnp.zeros_like(dv_ref)

        q0 = pl.program_id(1) * tq
        qs = _scaled_queries(q_ref, tq)
        dys = _stack_heads(dy_ref, tq)
        gv = _sigmoid(_head_rows(gate_ref))
        dy_o = _dot(jnp.ones((8, HD), F32), dys * _stack_heads(o_ref, tq), "nt", split="b")[0:1, :]
        delta = gv * dy_o
        dgate = dy_o * (gv * (1.0 - gv))
        for g in range(GRP):
            dg_ref[0, g:g + 1, :] = dgate[:, g * tq:(g + 1) * tq]
        lsev = _head_rows(lse_ref)
        dos = (dys * jnp.broadcast_to(gv, (8, R)).T[:, 0:1]).astype(_MXU)
        dq_scr[...] = jnp.zeros_like(dq_scr)
        qk = _sel_operands(qs, sel_ref) if mode == "sel" else qs

        def produce(kb, masked, slot):
            k0 = _block_start(kb, tk)
            s = _dot(k_ref[0, pl.ds(k0, tk), :], qk, "nt")
            if masked:
                s = s + _attn_bias(mode, q0, k0, tq, tk)[0]
            s_scr[slot] = s
            dp_scr[slot] = _dot(v_ref[0, pl.ds(k0, tk), :], dos, "nt")

        def consume(kb, slot):
            k0 = _block_start(kb, tk)
            p = jnp.exp(s_scr[slot] - lsev)
            if mode == "cmp":
                p = p * _attn_bias(mode, q0, 0, tq, tk)[1]
            ds = p * (dp_scr[slot] - delta)
            dq_scr[...] += _dot(kt_ref[0, kb], ds, "nn")
            dk_ref[0, pl.ds(k0, tk), :] += _dot(ds, qs, "nn")
            dv_ref[0, pl.ds(k0, tk), :] += _dot(p, dos, "nn")

        _pipelined_key_blocks(mode, q0, tq, tk, produce, consume)
        for g in range(GRP):
            dq_ref[:, g * HD:(g + 1) * HD] = (dq_scr[:, g * tq:(g + 1) * tq] * SCALE).T

    kv_spec = pl.BlockSpec((1, Sk, HD), lambda h, i: (h, 0, 0))
    qo_spec = pl.BlockSpec((tq, GRP * HD), lambda h, i: (i, h))
    row_spec = pl.BlockSpec((1, GRP, tq), lambda h, i: (h, 0, i))
    ins = [q, k, kt, v, o, lse, dy, gate]
    specs = [pl.BlockSpec((tq, GRP * HD), lambda h, i: (i, qcol0 + h)), pl.BlockSpec((1, Sk, k.shape[2]), lambda h, i: (h, 0, 0)),
             pl.BlockSpec((1, Sk // tk, HD, tk), lambda h, i: (h, 0, 0, 0)), kv_spec, qo_spec, row_spec,
             pl.BlockSpec((tq, GRP * HD), lambda h, i: (i, dycol0 + h)), row_spec]
    if mode == "sel":
        ins.append(selneg)
        specs.append(pl.BlockSpec((1, tq, selneg.shape[2]), lambda h, i: (h, i, 0)))
    return pl.pallas_call(
        body, name=name, grid=(N_KV, S // tq), in_specs=specs, out_specs=[qo_spec, kv_spec, kv_spec, row_spec],
        out_shape=[jax.ShapeDtypeStruct((S, ATT_WIDTH), F32), jax.ShapeDtypeStruct((N_KV, Sk, HD), F32),
                   jax.ShapeDtypeStruct((N_KV, Sk, HD), F32), jax.ShapeDtypeStruct((N_KV, GRP, S), F32)],
        scratch_shapes=[pltpu.VMEM((HD, R), F32), pltpu.VMEM((2, tk, R), F32), pltpu.VMEM((2, tk, R), F32)],
        compiler_params=_cp(("parallel", "arbitrary")))(*ins)


def _block_overlap(NC, NB):
    ci = np.arange(NC)[None, :] * 16
    sj = np.arange(NB)[:, None] * SEL_BLOCK
    ov_t = np.clip(np.minimum(ci + 32, sj + SEL_BLOCK) - np.maximum(ci, sj), 0, None) / 32.0
    ov_t[:, NC - 1] = 0.0
    return jnp.asarray(ov_t, F32)


def _chosen_blocks(p, ov_t, q0, tq):
    NB = ov_t.shape[0]
    imp4 = _dot(ov_t, p, "nn")
    imp = imp4[:, 0:tq] + imp4[:, tq:2 * tq] + imp4[:, 2 * tq:3 * tq] + imp4[:, 3 * tq:4 * tq]
    blk = lax.broadcasted_iota(jnp.int32, (NB, tq), 0)
    cur = lax.shift_right_logical(q0 + lax.broadcasted_iota(jnp.int32, (NB, tq), 1), 6)
    imp = jnp.where((blk == 0) | (blk == cur) | (blk == cur - 1), FORCE, imp)
    imp = jnp.where(blk <= cur, imp, -1.0)
    rank = jnp.zeros((NB, tq), F32)
    for j in range(NB):
        row = imp[j:j + 1, :]
        ahead = (row > imp) | ((row == imp) & (blk > j))
        rank = rank + ahead.astype(F32)
    chosen = (rank < float(N_SELECT)) & (imp >= 0.0)
    return jnp.where(chosen, 0.0, NEG).T


def _to_rows16(x):
    S = x.shape[0]
    return x.reshape(S // 16, 16, N_KV, HD).transpose(2, 0, 1, 3).reshape(N_KV, S // 16, 16 * HD)


def _from_rows16(r):
    NC = r.shape[1]
    return r.reshape(N_KV, NC, 16, HD).transpose(1, 2, 0, 3).reshape(NC * 16, N_KV * HD)


DT_COL0 = SSD_WIDTH + CONV_CH
GATE_IN_COL0 = D_IN - 3 * N_HEADS


SHARD_IN = D_IN // N_DEV


def _orig_cols(ref, c0, width):
    pieces, c = [], c0
    while c < c0 + width:
        d, off = divmod(c, SHARD_IN)
        w = min(SHARD_IN - off, c0 + width - c)
        pieces.append(ref[d, :, off:off + w])
        c += w
    return pieces[0] if len(pieces) == 1 else jnp.concatenate(pieces, axis=1)


def _cols_from_slabs(slabs):
    _, R, c = slabs.shape
    tr = _pick(R, (256, 128))

    def body(s_ref, o_ref):
        for t in range(N_DEV * c // LANE):
            pieces, col = [], t * LANE
            while col < (t + 1) * LANE:
                d, off = divmod(col, c)
                w = min(c - off, (t + 1) * LANE - col)
                pieces.append(s_ref[d, :, off:off + w])
                col += w
            o_ref[:, t * LANE:(t + 1) * LANE] = pieces[0] if len(pieces) == 1 else jnp.concatenate(pieces, axis=1)

    return pl.pallas_call(
        body, name="cols_from_slabs", grid=(R // tr,), in_specs=[pl.BlockSpec((N_DEV, tr, c), lambda i: (0, i, 0))],
        out_specs=pl.BlockSpec((tr, N_DEV * c), lambda i: (i, 0)), out_shape=jax.ShapeDtypeStruct((R, N_DEV * c), slabs.dtype),
        compiler_params=_cp(("parallel",)))(slabs)


def _slabs_from_cols(x):
    R, c = x.shape[0], x.shape[1] // N_DEV
    tr = _pick(R, (256, 128))

    def body(x_ref, o_ref):
        for d in range(N_DEV):
            o_ref[d] = x_ref[:, d * c:(d + 1) * c]

    return pl.pallas_call(
        body, name="slabs_from_cols", grid=(R // tr,), in_specs=[pl.BlockSpec((tr, N_DEV * c), lambda i: (i, 0))],
        out_specs=pl.BlockSpec((N_DEV, tr, c), lambda i: (0, i, 0)), out_shape=jax.ShapeDtypeStruct((N_DEV, R, c), x.dtype),
        compiler_params=_cp(("parallel",)))(x)


def _w_in_from_slabs(slabs):
    D = slabs.shape[1]
    tr = _pick(D, (256, 128))

    def body(s_ref, main_ref, small_ref):
        for t in range(W_MAIN // LANE):
            c = t * LANE
            main_ref[:, c:c + LANE] = _orig_cols(s_ref, c if c < DT_COL0 else c + SSD_HEADS, LANE)
        small_ref[...] = jnp.concatenate(
            [_orig_cols(s_ref, DT_COL0, SSD_HEADS), _orig_cols(s_ref, GATE_IN_COL0, 3 * N_HEADS),
             jnp.zeros((tr, W_SMALL - SSD_HEADS - 3 * N_HEADS), small_ref.dtype)], axis=1)

    return pl.pallas_call(
        body, name="w_in_layout", grid=(D // tr,), in_specs=[pl.BlockSpec((N_DEV, tr, SHARD_IN), lambda i: (0, i, 0))],
        out_specs=[pl.BlockSpec((tr, W_MAIN), lambda i: (i, 0)), pl.BlockSpec((tr, W_SMALL), lambda i: (i, 0))],
        out_shape=[jax.ShapeDtypeStruct((D, W_MAIN), slabs.dtype), jax.ShapeDtypeStruct((D, W_SMALL), slabs.dtype)],
        compiler_params=_cp(("parallel",)))(slabs)


def _w_in_to_slabs(main, small):
    D = main.shape[0]
    tr = _pick(D, (256, 128))
    ranges = [(0, DT_COL0, 0, 0), (DT_COL0, DT_COL0 + SSD_HEADS, 1, 0), (DT_COL0 + SSD_HEADS, GATE_IN_COL0, 0, DT_COL0),
              (GATE_IN_COL0, D_IN, 1, SSD_HEADS)]

    def body(main_ref, small_ref, o_ref):
        srcs = (main_ref, small_ref)
        for d in range(N_DEV):
            lo, hi = d * SHARD_IN, (d + 1) * SHARD_IN
            pieces = []
            for start, stop, which, s0 in ranges:
                a, b = max(lo, start), min(hi, stop)
                if a < b:
                    pieces.append(srcs[which][:, s0 + a - start:s0 + b - start].astype(o_ref.dtype))
            o_ref[d] = pieces[0] if len(pieces) == 1 else jnp.concatenate(pieces, axis=1)

    return pl.pallas_call(
        body, name="w_in_grad_layout", grid=(D // tr,),
        in_specs=[pl.BlockSpec((tr, W_MAIN), lambda i: (i, 0)), pl.BlockSpec((tr, W_SMALL), lambda i: (i, 0))],
        out_specs=pl.BlockSpec((N_DEV, tr, SHARD_IN), lambda i: (0, i, 0)),
        out_shape=jax.ShapeDtypeStruct((N_DEV, D, SHARD_IN), main.dtype), compiler_params=_cp(("parallel",)))(main, small)


QB, KCB, VCB, KSB, VSB, KWB, VWB = 10, 14, 15, 16, 17, 18, 19


def _col256(a, b):
    return a[:, b * 256:(b + 1) * 256]


_EARLY = ["w_in", "cmp_w1_k", "cmp_w1_v"]
_LATE = ["w_out", "w_gate", "w_up", "w_down"]
_FFN = ["w_down", "w_gate", "w_up"]
_MID = ["w_out"]
_LAST = ["cmp_w1_k", "cmp_w1_v", "w_in"]


def _local_step(x, tgt, p, late_weights=None, grads_ready=None):
    S = x.shape[0]
    cos, sin = _rope_tables(S)

    u, rs1 = p["normed_x"] if "normed_x" in p else _rms_fwd(x, p["attn_norm_w"], "attn_norm")
    proj = _mm(u, p["w_main"], "nn", F32, "in_proj", after=p.get("before_in_proj"))
    proj_small = _mm(u, p["w_small"], "nn", F32, "in_proj_small")
    xa = _conv_fwd(proj, p["conv_w"], p["conv_b"])
    y_ssd, y_pre, rs_ssd, hs = _ssd_fwd(proj, proj_small, xa, p["dt_bias"], p["a_log"], p["d_skip"], p["ssd_norm_w"])

    q_rot = _rope([proj], QB, ATT_WIDTH, cos, sin, 1.0, _MXU, "rope_q")
    kv = _kv_prep(proj, cos, sin, _attn_cfg(S, S, "sel")[1])
    rk, rv = _to_rows16(_col256(proj, KCB)), _to_rows16(_col256(proj, VCB))
    k_cmp, hid_k = _compress_fwd(rk, p["cmp_pe_k"], p["cmp_w1_k"], p["cmp_w2_k"])
    v_cmp, hid_v = _compress_fwd(rv, p["cmp_pe_v"], p["cmp_w1_v"], p["cmp_w2_v"])
    n_cmp = k_cmp.shape[1]

    gates = proj_small[:, SSD_HEADS:SSD_HEADS + 3 * N_HEADS].reshape(S, N_KV, GRP, 3).transpose(3, 1, 2, 0)
    o_cmp, lse_cmp, y_att, sel = _attn_fwd(proj, QB, k_cmp, _blocked_t(v_cmp, n_cmp), "cmp", None, gates[0], None, F32,
                                           "attn_cmp_fwd")
    o_sel, lse_sel, y_att = _attn_fwd(q_rot, 0, kv["ks_ext"], kv["vs_t"], "sel", sel, gates[1], y_att, F32, "attn_sel_fwd")
    o_win, lse_win, y_att = _attn_fwd(q_rot, 0, kv["kw"], kv["vw_t"], "win", None, gates[2], y_att, _MXU, "attn_win_fwd")

    if late_weights is not None:
        p = {**p, **late_weights(y_att)}
    mixed = jnp.concatenate([y_ssd, y_att], axis=1)
    h1 = _mm(mixed, p["w_out"], "nn", F32, "out_proj", res=x)
    v, rs_ffn = _rms_fwd(h1, p["ffn_norm_w"], "ffn_norm")
    gt, up, act = _ffn_up(v, p["w_gate"], p["w_up"])
    h2 = _mm(act, p["w_down"], "nn", F32, "ffn_down", res=h1)
    loss, dh2, dh2b, d_final_w = _final_loss(h2, p["final_norm_w"], tgt)

    def ready(names):
        return None if grads_ready is None else grads_ready(names, g)

    g = {"final_norm_w": d_final_w}
    g["w_down"] = _mm(act, dh2b, "tn", _MXU, "dw_down")
    dgt, dup = _ffn_dact(dh2b, p["w_down"], gt, up)
    g["w_gate"] = _mm(v, dgt, "tn", _MXU, "dw_gate")
    g["w_up"] = _mm(v, dup, "tn", _MXU, "dw_up")
    dv = _ffn_dv(dgt, dup, p["w_gate"], p["w_up"], ready(_FFN))
    dh1, dh1b, g["ffn_norm_w"] = _rms_bwd(dv, h1, rs_ffn, p["ffn_norm_w"], dh2, "ffn_norm_bwd")
    g["w_out"] = _mm(mixed, dh1b, "tn", _MXU, "dw_out")
    dmixed = _mm(dh1b, p["w_out"], "nt", F32, "dmixed", after=ready(_MID))

    dz, dxa, ddtr, g["dt_bias"], g["a_log"], g["d_skip"], g["ssd_norm_w"] = _ssd_bwd(
        dmixed, proj, proj_small, xa, y_pre, rs_ssd, hs, p["dt_bias"], p["a_log"], p["d_skip"], p["ssd_norm_w"])
    dxbc, g["conv_w"], g["conv_b"] = _conv_bwd(proj, p["conv_w"], p["conv_b"], dxa)

    dyb = SSD_WIDTH // (GRP * HD)
    dq_cmp, dk_cmp, dv_cmp, dg_cmp = _attn_bwd(proj, QB, k_cmp, _blocked_t(k_cmp, n_cmp), v_cmp, o_cmp, lse_cmp, dmixed, dyb,
                                               gates[0], "cmp", None, "attn_cmp_bwd")
    dq_sel, dks, dvs, dg_sel = _attn_bwd(q_rot, 0, kv["ks_ext"], kv["ks_t"], kv["vs"], o_sel, lse_sel, dmixed, dyb, gates[1], "sel",
                                         sel, "attn_sel_bwd")
    dq_win, dkw, dvw, dg_win = _attn_bwd(q_rot, 0, kv["kw"], kv["kw_t"], kv["vw"], o_win, lse_win, dmixed, dyb, gates[2], "win", None,
                                         "attn_win_bwd")
    dgate = jnp.stack([dg_cmp, dg_sel, dg_win]).transpose(3, 1, 2, 0).reshape(S, 3 * N_HEADS)
    drk, g["cmp_w1_k"], g["cmp_w2_k"], g["cmp_pe_k"] = _compress_bwd(rk, p["cmp_pe_k"], p["cmp_w1_k"], p["cmp_w2_k"], hid_k, dk_cmp)
    drv, g["cmp_w1_v"], g["cmp_w2_v"], g["cmp_pe_v"] = _compress_bwd(rv, p["cmp_pe_v"], p["cmp_w1_v"], p["cmp_w2_v"], hid_v, dv_cmp)
    dq = _rope([dq_sel, dq_win], 0, ATT_WIDTH, cos, sin, -1.0, _MXU, "rope_dq", extra=(dq_cmp, 0))
    dkv = _dkv_post(dks, dvs, dkw, dvw, cos, sin)
    dproj = jnp.concatenate([dz, dxbc, dq] + [t.astype(_MXU) for t in (_from_rows16(drk), _from_rows16(drv))] + [dkv], axis=1)
    dsmall = jnp.concatenate([ddtr, dgate, jnp.zeros((S, W_SMALL - SSD_HEADS - 3 * N_HEADS), F32)], axis=1).astype(_MXU)
    g["w_main"] = _mm(u, dproj, "tn", _MXU, "dw_in")
    g["w_small"] = _mm(u, dsmall, "tn", F32, "dw_in_small")
    du = _mm(dproj, p["w_main"], "nt", F32, "du_main", after=ready(_LAST))
    du = _mm(dsmall, p["w_small"], "nt", F32, "du_small", res=du)
    grad_x, _, g["attn_norm_w"] = _rms_bwd(du, x, rs1, p["attn_norm_w"], dh1, "attn_norm_bwd")
    return loss, grad_x, g


MESH_ID = pl.DeviceIdType.MESH


def _my_coords():
    return lax.axis_index("x"), lax.axis_index("y"), lax.axis_index("c")


def _flat_id(px, py, pc):
    return 4 * px + 2 * py + pc


def _peer(k):
    mx, my, mc = _my_coords()
    return (1 - mx if k & 4 else mx, 1 - my if k & 2 else my, 1 - mc if k & 1 else mc)


def _exchange(arrs, scatter, name, after=()):
    n, na = len(arrs), len(after)
    scatter = [scatter] * n if isinstance(scatter, bool) else list(scatter)

    def body(*refs):
        ins, outs = refs[:n], refs[n + na:2 * n + na]
        send_sems, recv_sems, local_sems = refs[2 * n + na:]
        me = _flat_id(*_my_coords())
        copies = []
        for i in range(n):
            src_me = ins[i].at[me] if scatter[i] else ins[i]
            local = pltpu.make_async_copy(src_me, outs[i].at[me], local_sems.at[i])
            local.start()
            copies.append(local)
        for k in range(1, N_DEV):
            peer = _peer(k)
            for i in range(n):
                src = ins[i].at[_flat_id(*peer)] if scatter[i] else ins[i]
                cp = pltpu.make_async_remote_copy(src_ref=src, dst_ref=outs[i].at[me], send_sem=send_sems.at[i * 7 + k - 1],
                                                  recv_sem=recv_sems.at[i * 7 + k - 1], device_id=peer, device_id_type=MESH_ID)
                cp.start()
                copies.append(cp)
        for cp in copies:
            cp.wait()

    any_spec = pl.BlockSpec(memory_space=pl.ANY)
    out_shape = [jax.ShapeDtypeStruct(a.shape if sc else (N_DEV,) + a.shape, a.dtype) for a, sc in zip(arrs, scatter)]
    return pl.pallas_call(
        body, name=name, in_specs=[any_spec] * (n + na), out_specs=[any_spec] * n, out_shape=out_shape,
        scratch_shapes=[pltpu.SemaphoreType.DMA((n * 7,)), pltpu.SemaphoreType.DMA((n * 7,)), pltpu.SemaphoreType.DMA((n,))],
        compiler_params=pltpu.CompilerParams(has_side_effects=True))(*arrs, *after)


def _gather_two_level(arrs, name, xin, norm_w, to_round):
    n, m = len(arrs), len(to_round)
    S, D = xin.shape
    tr = _pick(S, (512, 256, 128))
    round_rows = [_largest_tile(t.shape[0], 256) if t.shape[0] % 128 == 0 else t.shape[0] // 4 for t in to_round]

    def move(src, dst, sem):
        cp = pltpu.make_async_copy(src, dst, sem.at[0])
        cp.start()
        cp.wait()

    def round_array(src_hbm, dst_hbm, fbuf, bbuf, rows_per, sem):
        def tile(i, carry):
            rows = pl.ds(pl.multiple_of(i * rows_per, 16), rows_per)
            move(src_hbm.at[rows], fbuf, sem)
            bbuf[...] = fbuf[...].astype(bbuf.dtype)
            move(bbuf, dst_hbm.at[rows], sem)
            return carry

        lax.fori_loop(0, src_hbm.shape[0] // rows_per, tile, 0)

    def rms_rows(x_hbm, w_ref, u_hbm, rs_hbm, xbuf, ubuf, rsbuf, sem):
        def tile(i, carry):
            rows = pl.ds(pl.multiple_of(i * tr, tr), tr)
            move(x_hbm.at[rows], xbuf, sem)
            xv = xbuf[...]
            rs = lax.rsqrt(jnp.mean(xv * xv, axis=-1, keepdims=True) + EPS)
            ubuf[...] = ((xv * rs) * w_ref[...]).astype(ubuf.dtype)
            rsbuf[...] = rs
            move(ubuf, u_hbm.at[rows], sem)
            move(rsbuf, rs_hbm.at[rows], sem)
            return carry

        lax.fori_loop(0, S // tr, tile, 0)

    def body(*refs):
        ins, x_hbm, w_ref, f32_srcs = refs[:n], refs[n], refs[n + 1], refs[n + 2:n + 2 + m]
        o0 = n + 2 + m
        outs, u_hbm, rs_hbm, rounded = refs[o0:o0 + n], refs[o0 + n], refs[o0 + n + 1], refs[o0 + n + 2:o0 + n + 2 + m]
        s0 = o0 + n + 2 + m
        send_sems, recv_sems, local_sems, xbuf, ubuf, rsbuf, norm_sem = refs[s0:s0 + 7]
        round_bufs = refs[s0 + 7:]
        x, y, c = _my_coords()
        me, sibling = (x, y, c), (x, y, 1 - c)
        chips = [(1 - x, y), (x, 1 - y), (1 - x, 1 - y)]

        def copy(i, k, block, to, src=None):
            slot = outs[i].at[_flat_id(*block)]
            return pltpu.make_async_remote_copy(src_ref=slot if src is None else src, dst_ref=slot, send_sem=send_sems.at[i * 7 + k],
                                                recv_sem=recv_sems.at[i * 7 + k], device_id=to, device_id_type=MESH_ID)

        mine = [pltpu.make_async_copy(ins[i], outs[i].at[_flat_id(*me)], local_sems.at[i]) for i in range(n)]
        for cp in mine:
            cp.start()
        first = []
        for j, chip in enumerate(chips):
            first += [copy(i, 1 + j, me, (*chip, c), src=ins[i]) for i in range(n)]
        first += [copy(i, 0, me, sibling, src=ins[i]) for i in range(n)]
        for cp in first:
            cp.start()
        rms_rows(x_hbm, w_ref, u_hbm, rs_hbm, xbuf, ubuf, rsbuf, norm_sem)
        for j in range(m):
            round_array(f32_srcs[j], rounded[j], round_bufs[2 * j], round_bufs[2 * j + 1], round_rows[j], norm_sem)
        passed = []
        for j, chip in enumerate(chips):
            for i in range(n):
                copy(i, 1 + j, (*chip, c), me).wait_recv()
                passed.append(copy(i, 4 + j, (*chip, c), sibling))
                passed[-1].start()
        for i in range(n):
            copy(i, 0, sibling, me).wait_recv()
        for j, chip in enumerate(chips):
            for i in range(n):
                copy(i, 4 + j, (*chip, 1 - c), me).wait_recv()
        for cp in first + passed:
            cp.wait_send()
        for cp in mine:
            cp.wait()

    any_spec = pl.BlockSpec(memory_space=pl.ANY)
    round_scratch = [s for t, r in zip(to_round, round_rows) for s in (pltpu.VMEM((r, t.shape[1]), F32), pltpu.VMEM((r, t.shape[1]), _MXU))]
    res = pl.pallas_call(
        body, name=name, in_specs=[any_spec] * (n + 1) + [pl.BlockSpec(memory_space=pltpu.VMEM)] + [any_spec] * m,
        out_specs=[any_spec] * (n + 2 + m),
        out_shape=[jax.ShapeDtypeStruct((N_DEV,) + a.shape, a.dtype) for a in arrs]
        + [jax.ShapeDtypeStruct((S, D), _MXU), jax.ShapeDtypeStruct((S, 1), F32)] + [jax.ShapeDtypeStruct(t.shape, _MXU) for t in to_round],
        scratch_shapes=[pltpu.SemaphoreType.DMA((n * 7,)), pltpu.SemaphoreType.DMA((n * 7,)), pltpu.SemaphoreType.DMA((n,)),
                        pltpu.VMEM((tr, D), F32), pltpu.VMEM((tr, D), _MXU), pltpu.VMEM((tr, 1), F32), pltpu.SemaphoreType.DMA((1,))]
        + round_scratch,
        compiler_params=pltpu.CompilerParams(has_side_effects=True, vmem_limit_bytes=VMEM_LIMIT))(*arrs, xin, norm_w, *to_round)
    return res[:n], res[n], res[n + 1], res[n + 2:]


_HBM = pl.BlockSpec(memory_space=pltpu.HBM)
_SEM = pl.BlockSpec(memory_space=pltpu.SEMAPHORE)
_EFFECT = pltpu.SideEffectType.DATAFLOW_SIDE_EFFECTING


def _split_copies(ins, lands, send_sems, recv_sems, own_sems, scatter):
    me = _flat_id(*_my_coords())
    remote = []
    for k in range(1, N_DEV):
        peer = _peer(k)
        for i in range(len(ins)):
            src = ins[i].at[_flat_id(*peer)] if scatter else ins[i]
            remote.append(pltpu.make_async_remote_copy(src_ref=src, dst_ref=lands[i].at[me], send_sem=send_sems.at[i * 7 + k - 1],
                                                       recv_sem=recv_sems.at[i * 7 + k - 1], device_id=peer, device_id_type=MESH_ID))
    own = [pltpu.make_async_copy(ins[i].at[me] if scatter else ins[i], lands[i].at[me], own_sems.at[i]) for i in range(len(ins))]
    return remote, own


def _split_start(arrs, scatter, name, after=()):
    n, na = len(arrs), len(after)

    def body(*refs):
        remote, own = _split_copies(refs[:n], refs[n:2 * n], refs[2 * n + na], refs[2 * n + na + 1], refs[2 * n + na + 2], scatter)
        for cp in remote + own:
            cp.start()
        refs[-1][...] = jnp.zeros_like(refs[-1])

    land_shapes = [a.shape if scatter else (N_DEV,) + a.shape for a in arrs]
    out_shape = ((pltpu.SemaphoreType.DMA((n * 7,)), pltpu.SemaphoreType.DMA((n * 7,)), pltpu.SemaphoreType.DMA((n,)))
                 + tuple(pltpu.HBM(a.shape, a.dtype) for a in arrs) + tuple(pltpu.HBM(s, a.dtype) for s, a in zip(land_shapes, arrs))
                 + (jax.ShapeDtypeStruct((8, 128), F32),))
    operands = ([pltpu.with_memory_space_constraint(a, pltpu.HBM) for a in arrs]
                + [pltpu.with_memory_space_constraint(lax.empty(s, a.dtype), pltpu.HBM) for s, a in zip(land_shapes, arrs)])
    res = pl.pallas_call(
        body, name=name, out_shape=out_shape, in_specs=[_HBM] * (2 * n) + [pl.BlockSpec(memory_space=pl.ANY)] * na,
        out_specs=(_SEM, _SEM, _SEM) + (_HBM,) * (2 * n) + (pl.BlockSpec(memory_space=pltpu.VMEM),),
        input_output_aliases={i: 3 + i for i in range(2 * n)},
        compiler_params=pltpu.CompilerParams(has_side_effects=_EFFECT))(*operands, *after)
    return dict(send=res[0], recv=res[1], own=res[2], ins=list(res[3:3 + n]), lands=list(res[3 + n:3 + 2 * n]), token=res[-1])


def _split_wait(st, scatter, after, name):
    n = len(st["ins"])

    def body(*refs):
        remote, own = _split_copies(refs[:n], refs[n:2 * n], refs[2 * n], refs[2 * n + 1], refs[2 * n + 2], scatter)
        for cp in remote:
            cp.wait_send()
            cp.wait_recv()
        for cp in own:
            cp.wait()

    arrs = st["ins"] + st["lands"]
    res = pl.pallas_call(
        body, name=name, out_shape=tuple(pltpu.HBM(a.shape, a.dtype) for a in arrs),
        in_specs=[_HBM] * (2 * n) + [_SEM, _SEM, _SEM] + [pl.BlockSpec(memory_space=pl.ANY)] * len(after), out_specs=(_HBM,) * (2 * n),
        input_output_aliases={i: i for i in range(2 * n)},
        compiler_params=pltpu.CompilerParams(has_side_effects=_EFFECT))(*arrs, st["send"], st["recv"], st["own"], *after)
    return list(res[n:])


def _adam_step(p_ref, w_ref, m_ref, v_ref, g_ref, d_ref, nm_ref, nv_ref):
    g = p_ref[0].astype(F32)
    for j in range(1, p_ref.shape[0]):
        g = g + p_ref[j].astype(F32)
    g_ref[...] = g
    nm = ADAM_B1 * m_ref[...] + (1.0 - ADAM_B1) * g
    nv = ADAM_B2 * v_ref[...] + (1.0 - ADAM_B2) * (g * g)
    nm_ref[...] = nm
    nv_ref[...] = nv
    m_hat = nm / (1.0 - ADAM_B1 ** ADAM_STEP)
    v_hat = nv / (1.0 - ADAM_B2 ** ADAM_STEP)
    d_ref[...] = -ADAM_LR * (m_hat / (jnp.sqrt(v_hat) + ADAM_EPS) + ADAM_WD * w_ref[...])


def _adam_sum(parts, w, m, v, name):
    P, R, C = parts.shape
    tr = _pick(R, (256, 128, 64, 32, 8)) if C <= 1024 else _pick(R, (128, 64, 32, 8))
    blk = pl.BlockSpec((tr, C), lambda i: (i, 0))
    return pl.pallas_call(
        functools.partial(_adam_step), name=name, grid=(R // tr,),
        in_specs=[pl.BlockSpec((P, tr, C), lambda i: (0, i, 0)), blk, blk, blk],
        out_specs=[blk] * 4, out_shape=[jax.ShapeDtypeStruct((R, C), F32)] * 4, compiler_params=_cp(("parallel",)))(parts, w, m, v)


def _adam_small(loss_parts, parts, ws, ms, vs):
    n = len(parts)

    def body(*refs):
        loss_ref, ins, outs, total_ref = refs[0], refs[1:4 * n + 1], refs[4 * n + 1:-1], refs[-1]
        for i in range(n):
            _adam_step(ins[i], ins[n + i], ins[2 * n + i], ins[3 * n + i], *outs[4 * i:4 * i + 4])
        total = loss_ref[0]
        for d in range(1, N_DEV):
            total = total + loss_ref[d]
        total_ref[...] = total

    out_shape = [jax.ShapeDtypeStruct(w.shape, F32) for w in ws for _ in range(4)] + [jax.ShapeDtypeStruct(loss_parts.shape[1:], F32)]
    res = pl.pallas_call(body, name="adam_small", out_shape=out_shape)(loss_parts, *parts, *ws, *ms, *vs)
    return res[-1], [tuple(res[4 * i:4 * i + 4]) for i in range(n)]


_WEIGHTS = ["attn_norm_w", "w_in", "conv_w", "conv_b", "dt_bias", "a_log", "d_skip", "ssd_norm_w", "cmp_w1_k", "cmp_w2_k",
            "cmp_w1_v", "cmp_w2_v", "cmp_pe_k", "cmp_pe_v", "w_out", "ffn_norm_w", "w_gate", "w_up", "w_down", "final_norm_w"]
_BIG = ["w_in", "w_gate", "w_up", "w_down", "w_out", "cmp_w1_k", "cmp_w1_v"]
_COL_SHARDED = ("w_in", "w_gate", "w_up")
_REPLICATED = ["attn_norm_w", "conv_b", "dt_bias", "a_log", "d_skip", "ssd_norm_w", "cmp_pe_k", "cmp_pe_v", "ffn_norm_w",
               "final_norm_w"]
_SMALL_SHARDED = ["conv_w", "cmp_w2_k", "cmp_w2_v"]


def _cols_to_slabs(g):
    R = g.shape[0]
    return g.reshape(R, N_DEV, -1).transpose(1, 0, 2)


def _slabs_to_cols(s):
    return s.transpose(1, 0, 2).reshape(s.shape[1], -1)


def kernel(x, attn_norm_w, w_in, conv_w, conv_b, dt_bias, a_log, d_skip, ssd_norm_w, cmp_w1_k, cmp_w2_k, cmp_w1_v, cmp_w2_v, cmp_pe_k, cmp_pe_v, w_out, ffn_norm_w, w_gate, w_up, w_down, final_norm_w, loss_target, m_attn_norm_w, m_w_in, m_conv_w, m_conv_b, m_dt_bias, m_a_log, m_d_skip, m_ssd_norm_w, m_cmp_w1_k, m_cmp_w2_k, m_cmp_w1_v, m_cmp_w2_v, m_cmp_pe_k, m_cmp_pe_v, m_w_out, m_ffn_norm_w, m_w_gate, m_w_up, m_w_down, m_final_norm_w, v_attn_norm_w, v_w_in, v_conv_w, v_conv_b, v_dt_bias, v_a_log, v_d_skip, v_ssd_norm_w, v_cmp_w1_k, v_cmp_w2_k, v_cmp_w1_v, v_cmp_w2_v, v_cmp_pe_k, v_cmp_pe_v, v_w_out, v_ffn_norm_w, v_w_gate, v_w_up, v_w_down, v_final_norm_w):
    a = dict(locals())

    early = [a[n][0].astype(_MXU) for n in _EARLY] + [cmp_w2_k[0], cmp_w2_v[0], conv_w[0]]
    got, normed_x, rs_x, late = _gather_two_level(early, "gather_early", x[0], attn_norm_w, [a[n][0] for n in _LATE])
    st_late = _split_start(list(late), False, "gather_late_start", after=(got[0],))

    def assemble(n, t):
        return _cols_from_slabs(t) if n in _COL_SHARDED else t.reshape(-1, t.shape[-1])

    p = dict(attn_norm_w=attn_norm_w, conv_b=conv_b, dt_bias=dt_bias, a_log=a_log, d_skip=d_skip, ssd_norm_w=ssd_norm_w,
             cmp_pe_k=cmp_pe_k.reshape(1, -1), cmp_pe_v=cmp_pe_v.reshape(1, -1), ffn_norm_w=ffn_norm_w,
             final_norm_w=final_norm_w.reshape(1, -1))

    w_main, w_small = _w_in_from_slabs(got[0])
    p.update(normed_x=(normed_x, rs_x), before_in_proj=st_late["token"],
             w_main=w_main, w_small=w_small, cmp_w1_k=assemble("cmp_w1_k", got[1]), cmp_w1_v=assemble("cmp_w1_v", got[2]),
             cmp_w2_k=assemble("cmp_w2_k", got[3]).astype(_MXU), cmp_w2_v=assemble("cmp_w2_v", got[4]).astype(_MXU),
             conv_w=_slabs_to_cols(got[5]))

    def late_weights(after):
        got_late = _split_wait(st_late, False, (after,), "gather_late_wait")
        return {n: assemble(n, t) for n, t in zip(_LATE, got_late)}

    def slabs_of(g, n):
        if n == "w_in":
            return _w_in_to_slabs(g["w_main"], g["w_small"])
        return _slabs_from_cols(g[n]) if n in _COL_SHARDED else g[n].reshape(N_DEV, -1, g[n].shape[-1])

    started = []

    def grads_ready(names, g):
        started.append((names, _split_start([slabs_of(g, n) for n in names], True, "scatter_grads_start_%d" % len(started))))
        return started[-1][1]["token"]

    loss_part, grad_x, g = _local_step(x[0], loss_target[0], p, late_weights, grads_ready)

    out, after = {}, (started[-1][1]["token"],)
    for i, (names, st) in enumerate(started):
        if i == len(started) - 1:
            after = after + (grad_x,)
        received = _split_wait(st, True, after, "scatter_grads_wait_%d" % i)
        for n, parts in zip(names, received):
            out[n] = _adam_sum(parts, a[n][0], a["m_" + n][0], a["v_" + n][0], "adam_" + n)
        after = (out[names[-1]][0],)

    small_names = _REPLICATED + _SMALL_SHARDED
    partials = [g[n] for n in _REPLICATED] + [_cols_to_slabs(g["conv_w"])] + [
        g[n].reshape(N_DEV, -1, g[n].shape[-1]) for n in ("cmp_w2_k", "cmp_w2_v")]
    gathered = _exchange([loss_part] + partials, [False] * (1 + len(_REPLICATED)) + [True] * len(_SMALL_SHARDED),
                         "exchange_small_grads", after=(received[0],))
    shapes2d = [t.shape[1:] for t in gathered[1:]]
    loss, res_small = _adam_small(gathered[0], gathered[1:],
                                  *[[a[pre + n].reshape(s) for n, s in zip(small_names, shapes2d)] for pre in ("", "m_", "v_")])
    for n, r in zip(small_names, res_small):
        out[n] = r

    outs = [loss[0, 0], grad_x[None]]
    for j in range(4):
        for n in _WEIGHTS:
            outs.append(out[n][j].reshape(a[n].shape))
    return tuple(outs)
```

```python
import functools

import numpy as np
import jax
import jax.numpy as jnp
from jax import lax
from jax.experimental import pallas as pl
from jax.experimental.pallas import tpu as pltpu

F32 = jnp.float32
_MXU = jnp.bfloat16

N_DEV = 8
SSD_WIDTH = 1024
ATT_WIDTH = 1024
SSD_HEADS = 16
SSD_P = 64
SSD_N = 128
SSD_L = 128
SSD_G = 2
CONV_CH = 1536
CONV_K = 4
HD = 64
N_HEADS = 16
N_KV = 4
GRP = 4
CMP_HID = 256
SEL_BLOCK = 64
N_SELECT = 16
WINDOW = 512
ROPE_DIM = 16
ROPE_THETA = 500000.0
EPS = 1e-6
NEG = -1e30
FORCE = 1e4
SCALE = HD ** -0.5
D_IN = 5184
W_MAIN = 5120
W_SMALL = 128
VMEM_LIMIT = 52 * 1024 * 1024

ADAM_LR, ADAM_B1, ADAM_B2, ADAM_EPS, ADAM_WD, ADAM_STEP = 0.001, 0.9, 0.999, 1e-08, 0.01, 10


def _pick(n, cands):
    for c in cands:
        if n % c == 0:
            return c
    return n


def _cp(sem=None):
    return pltpu.CompilerParams(dimension_semantics=sem, vmem_limit_bytes=VMEM_LIMIT)


def _sigmoid(x):
    return 1.0 / (1.0 + jnp.exp(-x))


def _dot(a, b, dims, split=None):
    dn = {"nn": (((1,), (0,)), ((), ())), "nt": (((1,), (1,)), ((), ())), "tn": (((0,), (0,)), ((), ()))}[dims]
    mm = lambda x, y: lax.dot_general(x.astype(_MXU), y.astype(_MXU), dn, preferred_element_type=F32)
    if split is None:
        return mm(a, b)
    x = (a if split == "a" else b).astype(F32)
    hi = x.astype(_MXU)
    lo = x - hi.astype(F32)
    return mm(hi, b) + mm(lo, b) if split == "a" else mm(a, hi) + mm(a, lo)


LANE = 128
MM_TILE = 1024
MM_K_WHOLE = 2048
MM_K_STEP = 2816
TN_ACC_ELEMS = 3 * 2 ** 20
TN_K_STEP = 1024


def _largest_tile(n, cap):
    if n <= cap:
        return n
    best = LANE
    for t in range(LANE, cap + 1, LANE):
        if n % t == 0:
            best = t
    return best


def _mm_tiles(mode, M, N, K):
    if mode == "tn":
        tm = _largest_tile(M, 2 * MM_TILE)
        return tm, _largest_tile(N, TN_ACC_ELEMS // tm), _largest_tile(K, TN_K_STEP)
    tk = K if K <= MM_K_WHOLE else _largest_tile(K, MM_K_STEP)
    return _largest_tile(M, MM_TILE), _largest_tile(N, MM_TILE), tk


def _mm(a, b, mode, out_dtype, name, res=None, after=None):
    if mode == "nn":
        (M, K), N = a.shape, b.shape[1]
    elif mode == "nt":
        (M, K), N = a.shape, b.shape[0]
    else:
        (K, M), N = a.shape, b.shape[1]
    tm, tn, tk = _mm_tiles(mode, M, N, K)
    nk = K // tk
    a_spec = pl.BlockSpec((tk, tm), lambda i, j, k: (k, i)) if mode == "tn" else pl.BlockSpec((tm, tk), lambda i, j, k: (i, k))
    b_spec = pl.BlockSpec((tn, tk), lambda i, j, k: (j, k)) if mode == "nt" else pl.BlockSpec((tk, tn), lambda i, j, k: (k, j))
    o_spec = pl.BlockSpec((tm, tn), lambda i, j, k: (i, j))

    def finish(r, r_ref, o_ref):
        if res is not None:
            r = r + r_ref[...].astype(F32)
        o_ref[...] = r.astype(out_dtype)

    def body_one_step(*refs):
        a_ref, b_ref, o_ref = refs[0], refs[1], refs[-1]
        finish(_dot(a_ref[...], b_ref[...], mode), refs[2], o_ref)

    def body(*refs):
        a_ref, b_ref, o_ref, acc = refs[0], refs[1], refs[-2], refs[-1]
        k = pl.program_id(2)

        @pl.when(k == 0)
        def _():
            acc[...] = jnp.zeros_like(acc)

        acc[...] += _dot(a_ref[...], b_ref[...], mode)

        @pl.when(k == nk - 1)
        def _():
            finish(acc[...], refs[2], o_ref)

    ins, specs = [a, b], [a_spec, b_spec]
    if res is not None:
        ins.append(res)
        specs.append(o_spec)
    if after is not None:
        ins.append(after)
        specs.append(pl.BlockSpec(memory_space=pl.ANY))
    return pl.pallas_call(
        body_one_step if nk == 1 else body, name=name, grid=(M // tm, N // tn, nk), in_specs=specs, out_specs=o_spec,
        out_shape=jax.ShapeDtypeStruct((M, N), out_dtype), scratch_shapes=[] if nk == 1 else [pltpu.VMEM((tm, tn), F32)],
        compiler_params=_cp(("parallel", "parallel", "arbitrary")))(*ins)


def _ffn_up(v, w_gate, w_up):
    S, D = v.shape
    F = w_gate.shape[1]
    tm, tn = _largest_tile(S, MM_TILE), _largest_tile(F, MM_TILE // 2)

    def body(v_ref, wg_ref, wu_ref, gt_ref, up_ref, act_ref):
        vv = v_ref[...]
        g = _dot(vv, wg_ref[...], "nn")
        u = _dot(vv, wu_ref[...], "nn")
        gt_ref[...] = g.astype(gt_ref.dtype)
        up_ref[...] = u.astype(up_ref.dtype)
        act_ref[...] = (g * _sigmoid(g) * u).astype(act_ref.dtype)

    o_spec = pl.BlockSpec((tm, tn), lambda i, j: (i, j))
    w_spec = pl.BlockSpec((D, tn), lambda i, j: (0, j))
    return pl.pallas_call(
        body, name="ffn_up", grid=(S // tm, F // tn),
        in_specs=[pl.BlockSpec((tm, D), lambda i, j: (i, 0)), w_spec, w_spec], out_specs=[o_spec, o_spec, o_spec],
        out_shape=[jax.ShapeDtypeStruct((S, F), _MXU)] * 3,
        compiler_params=_cp(("parallel", "parallel")))(v, w_gate, w_up)


def _ffn_dv(dgt, dup, w_gate, w_up, after):
    S, F = dgt.shape
    D = w_gate.shape[0]
    tm, tn, _ = _mm_tiles("nt", S, D, F)
    tk = _largest_tile(F, MM_K_STEP // 2)
    nk = F // tk

    def body(g_ref, u_ref, wg_ref, wu_ref, *rest):
        o_ref, acc = rest[-2], rest[-1]
        k = pl.program_id(2)

        @pl.when(k == 0)
        def _():
            acc[...] = jnp.zeros_like(acc)

        acc[...] += _dot(g_ref[...], wg_ref[...], "nt") + _dot(u_ref[...], wu_ref[...], "nt")

        @pl.when(k == nk - 1)
        def _():
            o_ref[...] = acc[...]

    a_spec = pl.BlockSpec((tm, tk), lambda i, j, k: (i, k))
    w_spec = pl.BlockSpec((tn, tk), lambda i, j, k: (j, k))
    ins, specs = [dgt, dup, w_gate, w_up], [a_spec, a_spec, w_spec, w_spec]
    if after is not None:
        ins.append(after)
        specs.append(pl.BlockSpec(memory_space=pl.ANY))
    return pl.pallas_call(
        body, name="ffn_dv", grid=(S // tm, D // tn, nk), in_specs=specs, out_specs=pl.BlockSpec((tm, tn), lambda i, j, k: (i, j)),
        out_shape=jax.ShapeDtypeStruct((S, D), F32), scratch_shapes=[pltpu.VMEM((tm, tn), F32)],
        compiler_params=_cp(("parallel", "parallel", "arbitrary")))(*ins)


def _ffn_dact(dh2, w_down, gt, up):
    S, D = dh2.shape
    F = w_down.shape[0]
    tm, tn = _largest_tile(S, MM_TILE), _largest_tile(F, MM_TILE // 2)

    def body(d_ref, w_ref, gt_ref, up_ref, dg_ref, du_ref):
        da, g, u = _dot(d_ref[...], w_ref[...], "nt"), gt_ref[...].astype(F32), up_ref[...].astype(F32)
        s = _sigmoid(g)
        dg_ref[...] = (da * u * (s * (1.0 + g * (1.0 - s)))).astype(dg_ref.dtype)
        du_ref[...] = (da * (g * s)).astype(du_ref.dtype)

    o_spec = pl.BlockSpec((tm, tn), lambda i, j: (i, j))
    return pl.pallas_call(
        body, name="ffn_dact", grid=(S // tm, F // tn),
        in_specs=[pl.BlockSpec((tm, D), lambda i, j: (i, 0)), pl.BlockSpec((tn, D), lambda i, j: (j, 0)), o_spec, o_spec],
        out_specs=[o_spec, o_spec],
        out_shape=[jax.ShapeDtypeStruct((S, F), _MXU), jax.ShapeDtypeStruct((S, F), _MXU)],
        compiler_params=_cp(("parallel", "parallel")))(dh2, w_down, gt, up)


def _rms_fwd(x, w, name):
    S, D = x.shape
    tr = _pick(S, (256, 128))

    def body(x_ref, w_ref, xn_ref, rs_ref):
        xv = x_ref[...]
        rs = lax.rsqrt(jnp.mean(xv * xv, axis=-1, keepdims=True) + EPS)
        xn_ref[...] = ((xv * rs) * w_ref[...]).astype(xn_ref.dtype)
        rs_ref[...] = rs

    return pl.pallas_call(
        body, name=name, grid=(S // tr,),
        in_specs=[pl.BlockSpec((tr, D), lambda i: (i, 0)), pl.BlockSpec((1, D), lambda i: (0, 0))],
        out_specs=[pl.BlockSpec((tr, D), lambda i: (i, 0)), pl.BlockSpec((tr, 1), lambda i: (i, 0))],
        out_shape=[jax.ShapeDtypeStruct((S, D), _MXU), jax.ShapeDtypeStruct((S, 1), F32)],
        compiler_params=_cp(("parallel",)))(x, w)


def _rms_bwd(dyn, x, rs, w, res, name):
    S, D = x.shape
    tr = _pick(S, (256, 128))

    def body(dy_ref, x_ref, rs_ref, w_ref, res_ref, dx_ref, dxb_ref, dw_ref):
        @pl.when(pl.program_id(0) == 0)
        def _():
            dw_ref[...] = jnp.zeros_like(dw_ref)

        dy, r = dy_ref[...].astype(F32), rs_ref[...]
        xhat = x_ref[...] * r
        dw_ref[...] += jnp.sum(dy * xhat, axis=0, keepdims=True)
        dxhat = dy * w_ref[...]
        dx = res_ref[...] + r * (dxhat - xhat * jnp.mean(dxhat * xhat, axis=-1, keepdims=True))
        dx_ref[...] = dx
        dxb_ref[...] = dx.astype(dxb_ref.dtype)

    row = pl.BlockSpec((tr, D), lambda i: (i, 0))
    vec = pl.BlockSpec((1, D), lambda i: (0, 0))
    return pl.pallas_call(
        body, name=name, grid=(S // tr,),
        in_specs=[row, row, pl.BlockSpec((tr, 1), lambda i: (i, 0)), vec, row], out_specs=[row, row, vec],
        out_shape=[jax.ShapeDtypeStruct((S, D), F32), jax.ShapeDtypeStruct((S, D), _MXU), jax.ShapeDtypeStruct((1, D), F32)],
        compiler_params=_cp(("arbitrary",)))(dyn, x, rs, w, res)


def _final_loss(h2, w, tgt):
    S, D = h2.shape
    tr = _pick(S, (256, 128))

    def body(h_ref, w_ref, t_ref, loss_ref, dh_ref, dhb_ref, dw_ref):
        @pl.when(pl.program_id(0) == 0)
        def _():
            dw_ref[...] = jnp.zeros_like(dw_ref)
            loss_ref[...] = jnp.zeros_like(loss_ref)

        hv, wv = h_ref[...], w_ref[...]
        rs = lax.rsqrt(jnp.mean(hv * hv, axis=-1, keepdims=True) + EPS)
        xhat = hv * rs
        err = xhat * wv - t_ref[...]
        row = jnp.mean(err * err, axis=-1, keepdims=True)
        loss_ref[...] += jnp.broadcast_to(0.5 * jnp.sum(row, axis=0, keepdims=True), loss_ref.shape)
        dy = err * (1.0 / D)
        dw_ref[...] += jnp.sum(dy * xhat, axis=0, keepdims=True)
        dxhat = dy * wv
        dh = rs * (dxhat - xhat * jnp.mean(dxhat * xhat, axis=-1, keepdims=True))
        dh_ref[...] = dh
        dhb_ref[...] = dh.astype(dhb_ref.dtype)

    row = pl.BlockSpec((tr, D), lambda i: (i, 0))
    vec = pl.BlockSpec((1, D), lambda i: (0, 0))
    return pl.pallas_call(
        body, name="final_loss", grid=(S // tr,), in_specs=[row, vec, row],
        out_specs=[pl.BlockSpec((1, LANE), lambda i: (0, 0)), row, row, vec],
        out_shape=[jax.ShapeDtypeStruct((1, LANE), F32), jax.ShapeDtypeStruct((S, D), F32), jax.ShapeDtypeStruct((S, D), _MXU),
                   jax.ShapeDtypeStruct((1, D), F32)],
        compiler_params=_cp(("arbitrary",)))(h2, w, tgt)


def _shift_rows(x, k, rows):
    if k == 0:
        return x
    S = x.shape[0]
    r = pltpu.roll(x, k % S, axis=0)
    ok = (rows >= k) if k > 0 else (rows < S + k)
    return jnp.where(ok, r, 0.0)


XBC_COL0 = SSD_WIDTH // 128


def _conv_fwd(proj, conv_w, conv_b):
    S = proj.shape[0]
    nct = CONV_CH // 128

    def body(x_ref, w_ref, b_ref, o_ref):
        x = x_ref[...]
        rows = lax.broadcasted_iota(jnp.int32, x.shape, 0)
        c = b_ref[...] + w_ref[3:4, :] * x
        for k in range(1, CONV_K):
            c = c + w_ref[3 - k:4 - k, :] * _shift_rows(x, k, rows)
        o_ref[...] = c * _sigmoid(c)

    return pl.pallas_call(
        body, name="conv_fwd", grid=(nct,),
        in_specs=[pl.BlockSpec((S, 128), lambda j: (0, XBC_COL0 + j)), pl.BlockSpec((CONV_K, 128), lambda j: (0, j)),
                  pl.BlockSpec((1, 128), lambda j: (0, j))],
        out_specs=pl.BlockSpec((S, 128), lambda j: (0, j)),
        out_shape=jax.ShapeDtypeStruct((S, CONV_CH), F32), compiler_params=_cp(("parallel",)))(proj, conv_w, conv_b)


def _conv_bwd(proj, conv_w, conv_b, dxa):
    S = proj.shape[0]
    nct = CONV_CH // 128

    def body(x_ref, w_ref, b_ref, d_ref, dx_ref, dw_ref, db_ref):
        x = x_ref[...]
        rows = lax.broadcasted_iota(jnp.int32, x.shape, 0)
        xs = [_shift_rows(x, k, rows) for k in range(CONV_K)]
        c = b_ref[...] + w_ref[3:4, :] * x
        for k in range(1, CONV_K):
            c = c + w_ref[3 - k:4 - k, :] * xs[k]
        s = _sigmoid(c)
        dc = d_ref[...] * (s * (1.0 + c * (1.0 - s)))
        dx = w_ref[3:4, :] * dc
        for k in range(1, CONV_K):
            dx = dx + w_ref[3 - k:4 - k, :] * _shift_rows(dc, -k, rows)
        dx_ref[...] = dx.astype(dx_ref.dtype)
        for k in range(CONV_K):
            dw_ref[3 - k:4 - k, :] = jnp.sum(dc * xs[k], axis=0, keepdims=True)
        db_ref[...] = jnp.sum(dc, axis=0, keepdims=True)

    col = pl.BlockSpec((S, 128), lambda j: (0, j))
    return pl.pallas_call(
        body, name="conv_bwd", grid=(nct,),
        in_specs=[pl.BlockSpec((S, 128), lambda j: (0, XBC_COL0 + j)), pl.BlockSpec((CONV_K, 128), lambda j: (0, j)),
                  pl.BlockSpec((1, 128), lambda j: (0, j)), col],
        out_specs=[col, pl.BlockSpec((CONV_K, 128), lambda j: (0, j)), pl.BlockSpec((1, 128), lambda j: (0, j))],
        out_shape=[jax.ShapeDtypeStruct((S, CONV_CH), _MXU), jax.ShapeDtypeStruct((CONV_K, CONV_CH), F32),
                   jax.ShapeDtypeStruct((1, CONV_CH), F32)],
        compiler_params=_cp(("parallel",)))(proj, conv_w, conv_b, dxa)


def _ssd_consts():
    L = SSD_L
    r = lax.broadcasted_iota(jnp.int32, (L, L), 0)
    c = lax.broadcasted_iota(jnp.int32, (L, L), 1)
    causal = r >= c
    upper = (r <= c).astype(F32)
    hr = lax.broadcasted_iota(jnp.int32, (SSD_HEADS, SSD_WIDTH), 0)
    hc = lax.broadcasted_iota(jnp.int32, (SSD_HEADS, SSD_WIDTH), 1)
    expand = (lax.shift_right_logical(hc, 6) == hr).astype(F32)
    return causal, causal.astype(F32), upper, expand


def _softplus(x):
    return jnp.maximum(x, 0.0) + jnp.log(1.0 + jnp.exp(-jnp.abs(x)))


def _ssd_scalars(dtr, dt_bias, a_log, tri, upper, expand):
    dt = _softplus(dtr + dt_bias)
    A = -jnp.exp(a_log)
    adt = dt * A
    acum = _dot(tri, adt, "nn", split="b")
    acum_t = _dot(adt, upper, "tn", split="a")
    alast = acum[SSD_L - 1:SSD_L, :]
    e = jnp.exp(acum)
    wdec = jnp.exp(alast - acum)
    gam = jnp.exp(alast)
    ex = lambda t: _dot(t, expand, "nn", split="a")
    gam8 = jnp.broadcast_to(gam, (8, SSD_HEADS))
    return dt, A, acum, acum_t, e, wdec, gam, ex(dt), ex(e), ex(wdec), ex(gam8)[0:1, :]


def _ssd_fwd(proj, proj_small, xa, dt_bias, a_log, d_skip, norm_w):
    S = proj.shape[0]
    L, N, W = SSD_L, SSD_N, SSD_WIDTH
    nc = S // L

    def body(z_ref, xa_ref, dtr_ref, dtb_ref, al_ref, dsk_ref, nw_ref, yo_ref, y_ref, rs_ref, hs_ref, h_scr, y_scr):
        @pl.when(pl.program_id(0) == 0)
        def _():
            h_scr[...] = jnp.zeros_like(h_scr)

        causal, tri, upper, expand = _ssd_consts()
        dt, A, acum, acum_t, e, wdec, gam, dtE, eE, wE, gamE = _ssd_scalars(dtr_ref[:, 0:SSD_HEADS], dtb_ref[...], al_ref[...], tri, upper, expand)
        xs = xa_ref[:, 0:W]
        X = xs * dtE
        XW = X * wE
        hs_ref[0] = h_scr[...]
        for g in range(SSD_G):
            gs = slice(g * 512, (g + 1) * 512)
            Bg = xa_ref[:, W + g * N:W + (g + 1) * N]
            Cg = xa_ref[:, W + SSD_G * N + g * N:W + SSD_G * N + (g + 1) * N]
            Hg = h_scr[:, gs]
            CB = _dot(Cg, Bg, "nt")
            yoff = _dot(Cg, Hg, "nn") * eE[:, gs]
            st = _dot(Bg, XW[:, gs], "tn")
            for j in range(8):
                h = g * 8 + j
                hsl = slice(h * SSD_P, (h + 1) * SSD_P)
                lam = jnp.exp(jnp.where(causal, acum[:, h:h + 1] - acum_t[h:h + 1, :], -jnp.inf))
                y_scr[:, hsl] = _dot(CB * lam, X[:, hsl], "nn") + yoff[:, j * SSD_P:(j + 1) * SSD_P]
            h_scr[:, gs] = gamE[:, gs] * Hg + st
        dskE = _dot(jnp.broadcast_to(dsk_ref[...], (8, SSD_HEADS)), expand, "nn", split="a")[0:1, :]
        y = y_scr[...] + dskE * xs
        y_ref[...] = y
        zv = z_ref[...]
        yg = y * (zv * _sigmoid(zv))
        rs = lax.rsqrt(jnp.mean(yg * yg, axis=-1, keepdims=True) + EPS)
        rs_ref[...] = rs
        yo_ref[...] = ((yg * rs) * nw_ref[...]).astype(yo_ref.dtype)

    p16 = pl.BlockSpec((1, SSD_HEADS), lambda c: (0, 0))
    return pl.pallas_call(
        body, name="ssd_fwd", grid=(nc,),
        in_specs=[pl.BlockSpec((L, W), lambda c: (c, 0)), pl.BlockSpec((L, CONV_CH), lambda c: (c, 0)),
                  pl.BlockSpec((L, W_SMALL), lambda c: (c, 0)), p16, p16, p16, pl.BlockSpec((1, W), lambda c: (0, 0))],
        out_specs=[pl.BlockSpec((L, W), lambda c: (c, 0)), pl.BlockSpec((L, W), lambda c: (c, 0)),
                   pl.BlockSpec((L, 1), lambda c: (c, 0)), pl.BlockSpec((1, N, W), lambda c: (c, 0, 0))],
        out_shape=[jax.ShapeDtypeStruct((S, W), _MXU), jax.ShapeDtypeStruct((S, W), F32), jax.ShapeDtypeStruct((S, 1), F32),
                   jax.ShapeDtypeStruct((nc, N, W), F32)],
        scratch_shapes=[pltpu.VMEM((N, W), F32), pltpu.VMEM((L, W), F32)],
        compiler_params=_cp(("arbitrary",)))(proj, xa, proj_small, dt_bias, a_log, d_skip, norm_w)


def _ssd_bwd(dmixed, proj, proj_small, xa, y, rs2, hs, dt_bias, a_log, d_skip, norm_w):
    S = proj.shape[0]
    L, N, W, H = SSD_L, SSD_N, SSD_WIDTH, SSD_HEADS
    nc = S // L

    def body(dyo_ref, z_ref, xa_ref, dtr_ref, y_ref, rs_ref, hs_ref, dtb_ref, al_ref, dsk_ref, nw_ref,
             dz_ref, dxa_ref, ddtr_ref, ddtb_ref, dal_ref, ddsk_ref, dnw_ref, dh_scr, dx_scr):
        @pl.when(pl.program_id(0) == 0)
        def _():
            dh_scr[...] = jnp.zeros_like(dh_scr)
            ddtb_ref[...] = jnp.zeros_like(ddtb_ref)
            dal_ref[...] = jnp.zeros_like(dal_ref)
            ddsk_ref[...] = jnp.zeros_like(ddsk_ref)
            dnw_ref[...] = jnp.zeros_like(dnw_ref)

        causal, tri, upper, expand = _ssd_consts()
        heads = lambda t: _dot(t, expand, "nt", split="a")
        onehot = lambda h: (lax.broadcasted_iota(jnp.int32, (1, H), 1) == h).astype(F32)

        zv, yv, rs = z_ref[...], y_ref[...], rs_ref[...]
        sz = _sigmoid(zv)
        zs = zv * sz
        xhat = (yv * zs) * rs
        dyo = dyo_ref[...].astype(F32)
        dnw_ref[...] += jnp.sum(dyo * xhat, axis=0, keepdims=True)
        dxhat = dyo * nw_ref[...]
        dyg = rs * (dxhat - xhat * jnp.mean(dxhat * xhat, axis=-1, keepdims=True))
        dz_ref[...] = (dyg * yv * (sz * (1.0 + zv * (1.0 - sz)))).astype(dz_ref.dtype)
        dy = dyg * zs

        dtr = dtr_ref[:, 0:H]
        dt, A, acum, acum_t, e, wdec, gam, dtE, eE, wE, gamE = _ssd_scalars(dtr, dtb_ref[...], al_ref[...], tri, upper, expand)
        xs = xa_ref[:, 0:W]
        X = xs * dtE
        XW = X * wE
        dskE = _dot(jnp.broadcast_to(dsk_ref[...], (8, H)), expand, "nn", split="a")[0:1, :]
        ddsk_ref[...] += heads(jnp.broadcast_to(jnp.sum(dy * xs, axis=0, keepdims=True), (8, W)))[0:1, :]

        dYe = dy * eE
        dacum = jnp.zeros((L, H), F32)
        de_full = []
        dw_full = []
        dgam_full = []
        for g in range(SSD_G):
            gs = slice(g * 512, (g + 1) * 512)
            Bg = xa_ref[:, W + g * N:W + (g + 1) * N]
            Cg = xa_ref[:, W + SSD_G * N + g * N:W + SSD_G * N + (g + 1) * N]
            Hg = hs_ref[0, :, gs]
            dHn = dh_scr[:, gs]
            CH = _dot(Cg, Hg, "nn")
            de_full.append(dy[:, gs] * CH)
            dC = _dot(dYe[:, gs], Hg, "nt")
            dHs = gamE[:, gs] * dHn + _dot(Cg, dYe[:, gs], "tn")
            dgam_full.append(jnp.sum(dHn * Hg, axis=0, keepdims=True))
            BdS = _dot(Bg, dHn, "nn")
            dB = _dot(XW[:, gs], dHn, "nt")
            dx_scr[:, gs] = BdS * wE[:, gs]
            dw_full.append(BdS * X[:, gs])
            CB = _dot(Cg, Bg, "nt")
            dCB = jnp.zeros((L, L), F32)
            for j in range(8):
                h = g * 8 + j
                hsl = slice(h * SSD_P, (h + 1) * SSD_P)
                lam = jnp.exp(jnp.where(causal, acum[:, h:h + 1] - acum_t[h:h + 1, :], -jnp.inf))
                M = CB * lam
                dM = _dot(dy[:, hsl], X[:, hsl], "nt")
                dx_scr[:, hsl] += _dot(M, dy[:, hsl], "tn")
                dCB = dCB + dM * lam
                Q = dM * M
                rowsum = jnp.sum(Q, axis=1, keepdims=True)
                colsum = _dot(Q, jnp.ones((L, 8), F32), "tn", split="a")[:, 0:1]
                dacum = dacum + (rowsum - colsum) * onehot(h)
            dC = dC + _dot(dCB, Bg, "nn")
            dB = dB + _dot(dCB, Cg, "tn")
            dxa_ref[:, W + g * N:W + (g + 1) * N] = dB
            dxa_ref[:, W + SSD_G * N + g * N:W + SSD_G * N + (g + 1) * N] = dC
            dh_scr[:, gs] = dHs

        de16 = heads(jnp.concatenate(de_full, axis=1))
        dw16 = heads(jnp.concatenate(dw_full, axis=1))
        dgam16 = heads(jnp.broadcast_to(jnp.concatenate(dgam_full, axis=1), (8, W)))[0:1, :]
        dacum = dacum + de16 * e - dw16 * wdec
        dlast = jnp.sum(dw16 * wdec, axis=0, keepdims=True) + dgam16 * gam
        lastrow = (lax.broadcasted_iota(jnp.int32, (L, 1), 0) == L - 1).astype(F32)
        dacum = dacum + lastrow * dlast
        da = _dot(tri, dacum, "tn", split="b")
        dX = dx_scr[...]
        ddt = da * A + heads(dX * xs)
        dA = jnp.sum(da * dt, axis=0, keepdims=True)
        dal_ref[...] += dA * A
        ddtr = ddt * _sigmoid(dtr + dtb_ref[...])
        ddtb_ref[...] += jnp.sum(ddtr, axis=0, keepdims=True)
        ddtr_ref[...] = ddtr
        dxa_ref[:, 0:W] = dX * dtE + dy * dskE

    p16 = pl.BlockSpec((1, H), lambda c: (0, 0))
    rev = lambda c: (nc - 1 - c, 0)
    return pl.pallas_call(
        body, name="ssd_bwd", grid=(nc,),
        in_specs=[pl.BlockSpec((L, W), rev), pl.BlockSpec((L, W), rev), pl.BlockSpec((L, CONV_CH), rev),
                  pl.BlockSpec((L, W_SMALL), rev), pl.BlockSpec((L, W), rev), pl.BlockSpec((L, 1), rev),
                  pl.BlockSpec((1, N, W), lambda c: (nc - 1 - c, 0, 0)), p16, p16, p16, pl.BlockSpec((1, W), lambda c: (0, 0))],
        out_specs=[pl.BlockSpec((L, W), rev), pl.BlockSpec((L, CONV_CH), rev), pl.BlockSpec((L, H), rev),
                   p16, p16, p16, pl.BlockSpec((1, W), lambda c: (0, 0))],
        out_shape=[jax.ShapeDtypeStruct((S, W), _MXU), jax.ShapeDtypeStruct((S, CONV_CH), F32), jax.ShapeDtypeStruct((S, H), F32),
                   jax.ShapeDtypeStruct((1, H), F32), jax.ShapeDtypeStruct((1, H), F32), jax.ShapeDtypeStruct((1, H), F32),
                   jax.ShapeDtypeStruct((1, W), F32)],
        scratch_shapes=[pltpu.VMEM((N, W), F32), pltpu.VMEM((L, W), F32)],
        compiler_params=_cp(("arbitrary",)))(dmixed, proj, xa, proj_small, y, rs2, hs, dt_bias, a_log, d_skip, norm_w)


def _rope_tables(S):
    inv = 1.0 / (ROPE_THETA ** (jnp.arange(0, ROPE_DIM, 2, dtype=F32) / ROPE_DIM))
    ang = jnp.arange(S, dtype=F32)[:, None] * inv[None, :]
    cos, sin = jnp.cos(ang), jnp.sin(ang)
    half = ROPE_DIM // 2
    c64 = jnp.concatenate([cos, cos, jnp.ones((S, HD - ROPE_DIM), F32)], axis=1)
    s64 = jnp.concatenate([sin, sin, jnp.zeros((S, HD - ROPE_DIM), F32)], axis=1)
    del half
    return jnp.concatenate([c64, c64], axis=1), jnp.concatenate([s64, s64], axis=1)


def _rope(xs, blk0, width, cos, sin, sign, out_dtype, name, extra=None):
    S = xs[0].shape[0]
    tr = _pick(S, (512, 256, 128))
    nx = len(xs)

    def body(*refs):
        x_refs, c_ref, s_ref = refs[:nx], refs[nx], refs[nx + 1]
        e_ref = refs[nx + 2] if extra is not None else None
        o_ref = refs[-1]
        cv, sv = c_ref[...], s_ref[...] * sign
        lane = lax.broadcasted_iota(jnp.int32, (tr, 128), 1)
        first = (lane & (HD - 1)) < (ROPE_DIM // 2)
        for j in range(bw // 128):
            cs = slice(j * 128, (j + 1) * 128)
            xv = x_refs[0][:, cs].astype(F32)
            for r in x_refs[1:]:
                xv = xv + r[:, cs].astype(F32)
            out = _rotate128(xv, cv, sv, first)
            if extra is not None:
                out = out + e_ref[:, cs].astype(F32)
            o_ref[:, cs] = out.astype(out_dtype)

    bw = 512
    assert width % bw == 0 and (blk0 * 256) % bw == 0
    b0 = blk0 * 256 // bw
    t128 = pl.BlockSpec((tr, 128), lambda i, j: (i, 0))
    oblk = pl.BlockSpec((tr, bw), lambda i, j: (i, j))
    specs = [pl.BlockSpec((tr, bw), lambda i, j: (i, b0 + j))] * nx + [t128, t128]
    ins = list(xs) + [cos, sin]
    if extra is not None:
        assert (extra[1] * 256) % bw == 0
        ins.append(extra[0])
        eb = extra[1] * 256 // bw
        specs.append(pl.BlockSpec((tr, bw), lambda i, j: (i, eb + j)))
    return pl.pallas_call(
        body, name=name, grid=(S // tr, width // bw), in_specs=specs, out_specs=oblk,
        out_shape=jax.ShapeDtypeStruct((S, width), out_dtype), compiler_params=_cp(("parallel", "parallel")))(*ins)


def _rotate128(xv, cv, sv, first):
    rot = jnp.where(first, -pltpu.roll(xv, 128 - ROPE_DIM // 2, axis=1), pltpu.roll(xv, ROPE_DIM // 2, axis=1))
    return xv * cv + rot * sv


def _kv_prep(proj, cos, sin, tk):
    S = proj.shape[0]
    NB = S // SEL_BLOCK

    def body(ks_ref, vs_ref, kw_ref, vw_ref, c_ref, s_ref, *outs):
        cv, sv = c_ref[...], s_ref[...]
        lane = lax.broadcasted_iota(jnp.int32, (tk, 128), 1)
        first = (lane & (HD - 1)) < (ROPE_DIM // 2)
        key = pl.program_id(0) * tk + lax.broadcasted_iota(jnp.int32, (tk, NB), 0)
        onehot = (lax.shift_right_logical(key, 6) == lax.broadcasted_iota(jnp.int32, (tk, NB), 1)).astype(F32)
        for j, (ref, rotated) in enumerate(((ks_ref, True), (vs_ref, False), (kw_ref, True), (vw_ref, False))):
            nat, blk = outs[2 * j], outs[2 * j + 1]
            for half in range(2):
                xv = ref[:, half * 128:(half + 1) * 128]
                if rotated:
                    xv = _rotate128(xv, cv, sv, first)
                for e in range(2):
                    h = 2 * half + e
                    piece = xv[:, e * HD:(e + 1) * HD]
                    nat[h] = (jnp.concatenate([piece, onehot], axis=1) if j == 0 else piece).astype(nat.dtype)
                    blk[h, 0] = piece.T.astype(blk.dtype)

    col = lambda b: pl.BlockSpec((tk, 256), lambda i: (i, b))
    t128 = pl.BlockSpec((tk, 128), lambda i: (i, 0))
    nat_spec = lambda w: pl.BlockSpec((N_KV, tk, w), lambda i: (0, i, 0))
    blk_spec = pl.BlockSpec((N_KV, 1, HD, tk), lambda i: (0, i, 0, 0))
    nat_shape = lambda w: jax.ShapeDtypeStruct((N_KV, S, w), _MXU)
    blk_shape = jax.ShapeDtypeStruct((N_KV, S // tk, HD, tk), _MXU)
    widths = (HD + NB, HD, HD, HD)
    res = pl.pallas_call(
        body, name="kv_prep", grid=(S // tk,), in_specs=[col(KSB), col(VSB), col(KWB), col(VWB), t128, t128],
        out_specs=[s for w in widths for s in (nat_spec(w), blk_spec)],
        out_shape=[s for w in widths for s in (nat_shape(w), blk_shape)],
        compiler_params=_cp(("parallel",)))(proj, proj, proj, proj, cos, sin)
    return dict(ks_ext=res[0], ks_t=res[1], vs=res[2], vs_t=res[3], kw=res[4], kw_t=res[5], vw=res[6], vw_t=res[7])


def _dkv_post(dks, dvs, dkw, dvw, cos, sin):
    S = dks.shape[1]
    tr = _pick(S, (512, 256, 128))

    def body(dks_ref, dvs_ref, dkw_ref, dvw_ref, c_ref, s_ref, o_ref):
        cv, sv = c_ref[...], -s_ref[...]
        lane = lax.broadcasted_iota(jnp.int32, (tr, 128), 1)
        first = (lane & (HD - 1)) < (ROPE_DIM // 2)
        for j, (ref, rotated) in enumerate(((dks_ref, True), (dvs_ref, False), (dkw_ref, True), (dvw_ref, False))):
            for half in range(2):
                xv = jnp.concatenate([ref[2 * half], ref[2 * half + 1]], axis=1)
                if rotated:
                    xv = _rotate128(xv, cv, sv, first)
                o_ref[:, j * 256 + half * 128:j * 256 + (half + 1) * 128] = xv.astype(o_ref.dtype)

    hm = pl.BlockSpec((N_KV, tr, HD), lambda i: (0, i, 0))
    t128 = pl.BlockSpec((tr, 128), lambda i: (i, 0))
    return pl.pallas_call(
        body, name="dkv_post", grid=(S // tr,), in_specs=[hm, hm, hm, hm, t128, t128],
        out_specs=pl.BlockSpec((tr, 4 * 256), lambda i: (i, 0)), out_shape=jax.ShapeDtypeStruct((S, 4 * 256), _MXU),
        compiler_params=_cp(("parallel",)))(dks, dvs, dkw, dvw, cos, sin)


def _compress_fwd(R, pe, w1, w2):
    NC = R.shape[1]
    half = 16 * HD

    def body(r_ref, pe_ref, w1_ref, w2_ref, o_ref, hid_ref):
        r = r_ref[0]
        a = _dot(r + pe_ref[:, 0:half], w1_ref[0:half, :], "nn")
        b = _dot(r + pe_ref[:, half:2 * half], w1_ref[half:2 * half, :], "nn")
        hid = a + pltpu.roll(b, NC - 1, axis=0)
        hid_ref[0] = hid
        out = _dot(hid * _sigmoid(hid), w2_ref[...], "nn")
        rows = lax.broadcasted_iota(jnp.int32, out.shape, 0)
        o_ref[0] = jnp.where(rows < NC - 1, out, 0.0).astype(o_ref.dtype)

    return pl.pallas_call(
        body, name="compress_fwd", grid=(N_KV,),
        in_specs=[pl.BlockSpec((1, NC, half), lambda h: (h, 0, 0)), pl.BlockSpec((1, 2 * half), lambda h: (0, 0)),
                  pl.BlockSpec((2 * half, CMP_HID), lambda h: (0, 0)), pl.BlockSpec((CMP_HID, HD), lambda h: (0, 0))],
        out_specs=[pl.BlockSpec((1, NC, HD), lambda h: (h, 0, 0)), pl.BlockSpec((1, NC, CMP_HID), lambda h: (h, 0, 0))],
        out_shape=[jax.ShapeDtypeStruct((N_KV, NC, HD), _MXU), jax.ShapeDtypeStruct((N_KV, NC, CMP_HID), F32)],
        compiler_params=_cp(("parallel",)))(R, pe, w1, w2)


def _compress_bwd(R, pe, w1, w2, hid, dout):
    NC = R.shape[1]
    half = 16 * HD

    def body(r_ref, pe_ref, w1_ref, w2_ref, hid_ref, do_ref, dr_ref, dw1_ref, dw2_ref, dpe_ref):
        @pl.when(pl.program_id(0) == 0)
        def _():
            dw1_ref[...] = jnp.zeros_like(dw1_ref)
            dw2_ref[...] = jnp.zeros_like(dw2_ref)
            dpe_ref[...] = jnp.zeros_like(dpe_ref)

        r, hv, do = r_ref[0], hid_ref[0], do_ref[0]
        s = _sigmoid(hv)
        dw2_ref[...] += _dot(hv * s, do, "tn")
        dhid = _dot(do, w2_ref[...], "nt") * (s * (1.0 + hv * (1.0 - s)))
        rows = lax.broadcasted_iota(jnp.int32, dhid.shape, 0)
        dhid = jnp.where(rows < NC - 1, dhid, 0.0)
        dhid_dn = pltpu.roll(dhid, 1, axis=0)
        dw1_ref[0:half, :] += _dot(r + pe_ref[:, 0:half], dhid, "tn")
        dw1_ref[half:2 * half, :] += _dot(r + pe_ref[:, half:2 * half], dhid_dn, "tn")
        dxt = _dot(dhid, w1_ref[0:half, :], "nt")
        dxb = _dot(dhid_dn, w1_ref[half:2 * half, :], "nt")
        dr_ref[0] = dxt + dxb
        dpe_ref[:, 0:half] += jnp.sum(dxt, axis=0, keepdims=True)
        dpe_ref[:, half:2 * half] += jnp.sum(dxb, axis=0, keepdims=True)

    return pl.pallas_call(
        body, name="compress_bwd", grid=(N_KV,),
        in_specs=[pl.BlockSpec((1, NC, half), lambda h: (h, 0, 0)), pl.BlockSpec((1, 2 * half), lambda h: (0, 0)),
                  pl.BlockSpec((2 * half, CMP_HID), lambda h: (0, 0)), pl.BlockSpec((CMP_HID, HD), lambda h: (0, 0)),
                  pl.BlockSpec((1, NC, CMP_HID), lambda h: (h, 0, 0)), pl.BlockSpec((1, NC, HD), lambda h: (h, 0, 0))],
        out_specs=[pl.BlockSpec((1, NC, half), lambda h: (h, 0, 0)), pl.BlockSpec((2 * half, CMP_HID), lambda h: (0, 0)),
                   pl.BlockSpec((CMP_HID, HD), lambda h: (0, 0)), pl.BlockSpec((1, 2 * half), lambda h: (0, 0))],
        out_shape=[jax.ShapeDtypeStruct((N_KV, NC, half), F32), jax.ShapeDtypeStruct((2 * half, CMP_HID), F32),
                   jax.ShapeDtypeStruct((CMP_HID, HD), F32), jax.ShapeDtypeStruct((1, 2 * half), F32)],
        compiler_params=_cp(("arbitrary",)))(R, pe, w1, w2, hid, dout)


def _attn_cfg(S, Sk, mode):
    tk = _pick(Sk, (256, 128))
    if mode == "cmp":
        return _pick(S, (512, 256, 128)), Sk
    if mode == "sel" and S % (2 * tk) == 0:
        return 2 * tk, tk
    return tk, tk


def _block_start(kb, tk):
    return kb * tk if isinstance(kb, int) else pl.multiple_of(kb * tk, tk)


def _pipelined_key_blocks(mode, q0, tq, tk, produce, consume):
    if mode == "cmp":
        produce(0, True, 0)
        consume(0, 0)
        return
    if mode == "win":
        assert tq == tk and WINDOW == 2 * tk
        last = q0 // tk
        first = jnp.maximum(last - 2, 0)

        @pl.when(last == 0)
        def _():
            produce(last, True, 0)
            consume(last, 0)

        @pl.when(last == 1)
        def _():
            produce(first, True, 0)
            produce(last, True, 1)
            consume(first, 0)
            consume(last, 1)

        @pl.when(last >= 2)
        def _():
            produce(first, True, 0)
            produce(first + 1, False, 1)
            consume(first, 0)
            produce(last, True, 0)
            consume(first + 1, 1)
            consume(last, 0)

        return
    first, n_plain, plain_masked = 0, q0 // tk, False
    pairs = jnp.maximum(n_plain - 1, 0) // 2
    if tq == 2 * tk:
        @pl.when(n_plain >= 1)
        def _():
            produce(0, False, 0)

        def two_plain(j, carry):
            produce(2 * j + 1, False, 1)
            consume(2 * j, 0)
            produce(2 * j + 2, False, 0)
            consume(2 * j + 1, 1)
            return carry

        lax.fori_loop(0, pairs, two_plain, 0)
        kb = 2 * pairs

        @pl.when(n_plain >= 2)
        def _():
            produce(kb + 1, False, 1)
            consume(kb, 0)
            produce(n_plain, True, 0)
            consume(kb + 1, 1)
            produce(n_plain + 1, True, 1)
            consume(n_plain, 0)
            consume(n_plain + 1, 1)

        @pl.when(n_plain == 0)
        def _():
            produce(0, True, 0)
            produce(1, True, 1)
            consume(0, 0)
            consume(1, 1)

        return
    assert tq == tk
    last = first + n_plain

    @pl.when(n_plain >= 1)
    def _():
        produce(first, plain_masked, 0)

    def two(j, carry):
        kb = first + 2 * j
        produce(kb + 1, plain_masked, 1)
        consume(kb, 0)
        produce(kb + 2, plain_masked, 0)
        consume(kb + 1, 1)
        return carry

    lax.fori_loop(0, pairs, two, 0)
    kb = first + 2 * pairs
    left = n_plain - 2 * pairs

    @pl.when(left == 2)
    def _():
        produce(kb + 1, plain_masked, 1)
        consume(kb, 0)
        produce(last, True, 0)
        consume(kb + 1, 1)
        consume(last, 0)

    @pl.when(left == 1)
    def _():
        produce(last, True, 1)
        consume(kb, 0)
        consume(last, 1)

    @pl.when(left == 0)
    def _():
        produce(last, True, 0)
        consume(last, 0)


def _attn_bias(mode, q0, k0, tq, tk):
    k = k0 + lax.broadcasted_iota(jnp.int32, (tk, tq), 0)
    t = q0 + lax.broadcasted_iota(jnp.int32, (tk, tq), 1)
    if mode == "cmp":
        ok = (k * 16 + 31) <= t
    elif mode == "win":
        ok = (k <= t) & ((t - k) < WINDOW)
    else:
        ok = k <= t
    bias = jnp.where(ok, 0.0, NEG)
    return jnp.concatenate([bias] * GRP, axis=1), jnp.concatenate([ok.astype(F32)] * GRP, axis=1)


def _sel_operands(qs, selneg_ref):
    return jnp.concatenate([qs, jnp.concatenate([selneg_ref[0]] * GRP, axis=0)], axis=1)


def _stack_heads(ref, tq):
    return jnp.concatenate([ref[:, g * HD:(g + 1) * HD] for g in range(GRP)], axis=0)


def _scaled_queries(q_ref, tq):
    return (_stack_heads(q_ref, tq).astype(F32) * SCALE).astype(_MXU)


def _blocked_t(x, tk):
    n, Sk, d = x.shape
    return x.reshape(n, Sk // tk, tk, d).transpose(0, 1, 3, 2)


def _head_rows(ref):
    return jnp.concatenate([ref[0, g:g + 1, :] for g in range(GRP)], axis=1)


def _attn_fwd(q, qcol0, k, vt, mode, selneg, gate, y_prev, y_dtype, name):
    S, Sk = q.shape[0], k.shape[1]
    tq, tk = _attn_cfg(S, Sk, mode)
    R = GRP * tq
    NB = S // SEL_BLOCK

    def body(*refs):
        q_ref, k_ref, vt_ref = refs[:3]
        rest = list(refs[3:])
        sel_ref = rest.pop(0) if mode == "sel" else None
        ov_ref = rest.pop(0) if mode == "cmp" else None
        gate_ref = rest.pop(0)
        yp_ref = rest.pop(0) if y_prev is not None else None
        o_ref, lse_ref, y_ref = rest[:3]
        choice_ref = rest[3] if mode == "cmp" else None
        m_scr, l_scr, acc, s_scr = rest[-4:]
        q0 = pl.program_id(1) * tq
        qs = _scaled_queries(q_ref, tq)
        m_scr[...] = jnp.full_like(m_scr, NEG)
        l_scr[...] = jnp.zeros_like(l_scr)
        acc[...] = jnp.zeros_like(acc)
        qk = _sel_operands(qs, sel_ref) if mode == "sel" else qs

        def produce(kb, masked, slot):
            k0 = _block_start(kb, tk)
            s = _dot(k_ref[0, pl.ds(k0, tk), :], qk, "nt")
            if masked:
                s = s + _attn_bias(mode, q0, k0, tq, tk)[0]
            s_scr[slot] = s

        def consume(kb, slot):
            s = s_scr[slot]
            m_old = m_scr[...]
            m_new = jnp.maximum(m_old, jnp.max(s, axis=0, keepdims=True))
            p = jnp.exp(s - m_new)
            if mode == "cmp":
                p = p * _attn_bias(mode, q0, 0, tq, tk)[1]
            alpha = jnp.exp(m_old - m_new)
            l_scr[...] = alpha * l_scr[...] + jnp.sum(p, axis=0, keepdims=True)
            acc[...] = alpha * acc[...] + _dot(vt_ref[0, kb], p, "nn")
            m_scr[...] = m_new

        _pipelined_key_blocks(mode, q0, tq, tk, produce, consume)
        l = l_scr[...]
        good = l > 0.0
        o_t = acc[...] * jnp.where(good, 1.0 / jnp.where(good, l, 1.0), 0.0)
        lse = jnp.where(good, m_scr[...] + jnp.log(jnp.where(good, l, 1.0)), -NEG)
        y_t = o_t * _sigmoid(_head_rows(gate_ref))
        if mode == "cmp":
            p = jnp.exp(s_scr[0] - lse) * _attn_bias(mode, q0, 0, tq, tk)[1]
            choice_ref[0] = _chosen_blocks(p, ov_ref[...], q0, tq).astype(choice_ref.dtype)
        for g in range(GRP):
            hs, qs_ = slice(g * HD, (g + 1) * HD), slice(g * tq, (g + 1) * tq)
            o_ref[:, hs] = o_t[:, qs_].T
            lse_ref[0, g:g + 1, :] = lse[:, qs_]
            yg = y_t[:, qs_].T
            if y_prev is not None:
                yg = yg + yp_ref[:, hs]
            y_ref[:, hs] = yg.astype(y_ref.dtype)

    row_spec = pl.BlockSpec((1, GRP, tq), lambda h, i: (h, 0, i))
    qo_spec = pl.BlockSpec((tq, GRP * HD), lambda h, i: (i, h))
    ins = [q, k, vt]
    specs = [pl.BlockSpec((tq, GRP * HD), lambda h, i: (i, qcol0 + h)), pl.BlockSpec((1, Sk, k.shape[2]), lambda h, i: (h, 0, 0)),
             pl.BlockSpec((1, Sk // tk, HD, tk), lambda h, i: (h, 0, 0, 0))]
    if mode == "sel":
        ins.append(selneg)
        specs.append(pl.BlockSpec((1, tq, selneg.shape[2]), lambda h, i: (h, i, 0)))
    out_specs = [qo_spec, row_spec, qo_spec]
    out_shape = [jax.ShapeDtypeStruct((S, ATT_WIDTH), F32), jax.ShapeDtypeStruct((N_KV, GRP, S), F32),
                 jax.ShapeDtypeStruct((S, ATT_WIDTH), y_dtype)]
    if mode == "cmp":
        ins.append(_block_overlap(Sk, NB))
        specs.append(pl.BlockSpec((NB, Sk), lambda h, i: (0, 0)))
        out_specs.append(pl.BlockSpec((1, tq, NB), lambda h, i: (h, i, 0)))
        out_shape.append(jax.ShapeDtypeStruct((N_KV, S, NB), _MXU))
    ins.append(gate)
    specs.append(row_spec)
    if y_prev is not None:
        ins.append(y_prev)
        specs.append(qo_spec)
    return pl.pallas_call(
        body, name=name, grid=(N_KV, S // tq), in_specs=specs, out_specs=out_specs, out_shape=out_shape,
        scratch_shapes=[pltpu.VMEM((1, R), F32), pltpu.VMEM((1, R), F32), pltpu.VMEM((HD, R), F32), pltpu.VMEM((2, tk, R), F32)],
        compiler_params=_cp(("parallel", "arbitrary")))(*ins)


def _attn_bwd(q, qcol0, k, kt, v, o, lse, dy, dycol0, gate, mode, selneg, name):
    S, Sk = q.shape[0], k.shape[1]
    tq, tk = _attn_cfg(S, Sk, mode)
    R = GRP * tq

    def body(*refs):
        if mode == "sel":
            (q_ref, k_ref, kt_ref, v_ref, o_ref, lse_ref, dy_ref, gate_ref, sel_ref, dq_ref, dk_ref, dv_ref, dg_ref, dq_scr, s_scr,
             dp_scr) = refs
        else:
            q_ref, k_ref, kt_ref, v_ref, o_ref, lse_ref, dy_ref, gate_ref, dq_ref, dk_ref, dv_ref, dg_ref, dq_scr, s_scr, dp_scr = refs

        @pl.when(pl.program_id(1) == 0)
        def _():
            dk_ref[...] = jnp.zeros_like(dk_ref)
            dv_ref[...] = jnp.zeros_like(dv_ref)

        q0 = pl.program_id(1) * tq
        qs = _scaled_queries(q_ref, tq)
        dys = _stack_heads(dy_ref, tq)
        gv = _sigmoid(_head_rows(gate_ref))
        dy_o = _dot(jnp.ones((8, HD), F32), dys * _stack_heads(o_ref, tq), "nt", split="b")[0:1, :]
        delta = gv * dy_o
        dgate = dy_o * (gv * (1.0 - gv))
        for g in range(GRP):
            dg_ref[0, g:g + 1, :] = dgate[:, g * tq:(g + 1) * tq]
        lsev = _head_rows(lse_ref)
        dos = (dys * jnp.broadcast_to(gv, (8, R)).T[:, 0:1]).astype(_MXU)
        dq_scr[...] = jnp.zeros_like(dq_scr)
        qk = _sel_operands(qs, sel_ref) if mode == "sel" else qs

        def produce(kb, masked, slot):
            k0 = _block_start(kb, tk)
            s = _dot(k_ref[0, pl.ds(k0, tk), :], qk, "nt")
            if masked:
                s = s + _attn_bias(mode, q0, k0, tq, tk)[0]
            s_scr[slot] = s
            dp_scr[slot] = _dot(v_ref[0, pl.ds(k0, tk), :], dos, "nt")

        def consume(kb, slot):
            k0 = _block_start(kb, tk)
            p = jnp.exp(s_scr[slot] - lsev)
            if mode == "cmp":
                p = p * _attn_bias(mode, q0, 0, tq, tk)[1]
            ds = p * (dp_scr[slot] - delta)
            dq_scr[...] += _dot(kt_ref[0, kb], ds, "nn")
            dk_ref[0, pl.ds(k0, tk), :] += _dot(ds, qs, "nn")
            dv_ref[0, pl.ds(k0, tk), :] += _dot(p, dos, "nn")

        _pipelined_key_blocks(mode, q0, tq, tk, produce, consume)
        for g in range(GRP):
            dq_ref[:, g * HD:(g + 1) * HD] = (dq_scr[:, g * tq:(g + 1) * tq] * SCALE).T

    kv_spec = pl.BlockSpec((1, Sk, HD), lambda h, i: (h, 0, 0))
    qo_spec = pl.BlockSpec((tq, GRP * HD), lambda h, i: (i, h))
    row_spec = pl.BlockSpec((1, GRP, tq), lambda h, i: (h, 0, i))
    ins = [q, k, kt, v, o, lse, dy, gate]
    specs = [pl.BlockSpec((tq, GRP * HD), lambda h, i: (i, qcol0 + h)), pl.BlockSpec((1, Sk, k.shape[2]), lambda h, i: (h, 0, 0)),
             pl.BlockSpec((1, Sk // tk, HD, tk), lambda h, i: (h, 0, 0, 0)), kv_spec, qo_spec, row_spec,
             pl.BlockSpec((tq, GRP * HD), lambda h, i: (i, dycol0 + h)), row_spec]
    if mode == "sel":
        ins.append(selneg)
        specs.append(pl.BlockSpec((1, tq, selneg.shape[2]), lambda h, i: (h, i, 0)))
    return pl.pallas_call(
        body, name=name, grid=(N_KV, S // tq), in_specs=specs, out_specs=[qo_spec, kv_spec, kv_spec, row_spec],
        out_shape=[jax.ShapeDtypeStruct((S, ATT_WIDTH), F32), jax.ShapeDtypeStruct((N_KV, Sk, HD), F32),
                   jax.ShapeDtypeStruct((N_KV, Sk, HD), F32), jax.ShapeDtypeStruct((N_KV, GRP, S), F32)],
        scratch_shapes=[pltpu.VMEM((HD, R), F32), pltpu.VMEM((2, tk, R), F32), pltpu.VMEM((2, tk, R), F32)],
        compiler_params=_cp(("parallel", "arbitrary")))(*ins)


def _block_overlap(NC, NB):
    ci = np.arange(NC)[None, :] * 16
    sj = np.arange(NB)[:, None] * SEL_BLOCK
    ov_t = np.clip(np.minimum(ci + 32, sj + SEL_BLOCK) - np.maximum(ci, sj), 0, None) / 32.0
    ov_t[:, NC - 1] = 0.0
    return jnp.asarray(ov_t, F32)


def _chosen_blocks(p, ov_t, q0, tq):
    NB = ov_t.shape[0]
    imp4 = _dot(ov_t, p, "nn")
    imp = imp4[:, 0:tq] + imp4[:, tq:2 * tq] + imp4[:, 2 * tq:3 * tq] + imp4[:, 3 * tq:4 * tq]
    blk = lax.broadcasted_iota(jnp.int32, (NB, tq), 0)
    cur = lax.shift_right_logical(q0 + lax.broadcasted_iota(jnp.int32, (NB, tq), 1), 6)
    imp = jnp.where((blk == 0) | (blk == cur) | (blk == cur - 1), FORCE, imp)
    imp = jnp.where(blk <= cur, imp, -1.0)
    rank = jnp.zeros((NB, tq), F32)
    for j in range(NB):
        row = imp[j:j + 1, :]
        ahead = (row > imp) | ((row == imp) & (blk > j))
        rank = rank + ahead.astype(F32)
    chosen = (rank < float(N_SELECT)) & (imp >= 0.0)
    return jnp.where(chosen, 0.0, NEG).T


def _to_rows16(x):
    S = x.shape[0]
    return x.reshape(S // 16, 16, N_KV, HD).transpose(2, 0, 1, 3).reshape(N_KV, S // 16, 16 * HD)


def _from_rows16(r):
    NC = r.shape[1]
    return r.reshape(N_KV, NC, 16, HD).transpose(1, 2, 0, 3).reshape(NC * 16, N_KV * HD)


DT_COL0 = SSD_WIDTH + CONV_CH
GATE_IN_COL0 = D_IN - 3 * N_HEADS


SHARD_IN = D_IN // N_DEV


def _orig_cols(ref, c0, width):
    pieces, c = [], c0
    while c < c0 + width:
        d, off = divmod(c, SHARD_IN)
        w = min(SHARD_IN - off, c0 + width - c)
        pieces.append(ref[d, :, off:off + w])
        c += w
    return pieces[0] if len(pieces) == 1 else jnp.concatenate(pieces, axis=1)


def _cols_from_slabs(slabs):
    _, R, c = slabs.shape
    tr = _pick(R, (256, 128))

    def body(s_ref, o_ref):
        for t in range(N_DEV * c // LANE):
            pieces, col = [], t * LANE
            while col < (t + 1) * LANE:
                d, off = divmod(col, c)
                w = min(c - off, (t + 1) * LANE - col)
                pieces.append(s_ref[d, :, off:off + w])
                col += w
            o_ref[:, t * LANE:(t + 1) * LANE] = pieces[0] if len(pieces) == 1 else jnp.concatenate(pieces, axis=1)

    return pl.pallas_call(
        body, name="cols_from_slabs", grid=(R // tr,), in_specs=[pl.BlockSpec((N_DEV, tr, c), lambda i: (0, i, 0))],
        out_specs=pl.BlockSpec((tr, N_DEV * c), lambda i: (i, 0)), out_shape=jax.ShapeDtypeStruct((R, N_DEV * c), slabs.dtype),
        compiler_params=_cp(("parallel",)))(slabs)


def _slabs_from_cols(x):
    R, c = x.shape[0], x.shape[1] // N_DEV
    tr = _pick(R, (256, 128))

    def body(x_ref, o_ref):
        for d in range(N_DEV):
            o_ref[d] = x_ref[:, d * c:(d + 1) * c]

    return pl.pallas_call(
        body, name="slabs_from_cols", grid=(R // tr,), in_specs=[pl.BlockSpec((tr, N_DEV * c), lambda i: (i, 0))],
        out_specs=pl.BlockSpec((N_DEV, tr, c), lambda i: (0, i, 0)), out_shape=jax.ShapeDtypeStruct((N_DEV, R, c), x.dtype),
        compiler_params=_cp(("parallel",)))(x)


def _w_in_from_slabs(slabs):
    D = slabs.shape[1]
    tr = _pick(D, (256, 128))

    def body(s_ref, main_ref, small_ref):
        for t in range(W_MAIN // LANE):
            c = t * LANE
            main_ref[:, c:c + LANE] = _orig_cols(s_ref, c if c < DT_COL0 else c + SSD_HEADS, LANE)
        small_ref[...] = jnp.concatenate(
            [_orig_cols(s_ref, DT_COL0, SSD_HEADS), _orig_cols(s_ref, GATE_IN_COL0, 3 * N_HEADS),
             jnp.zeros((tr, W_SMALL - SSD_HEADS - 3 * N_HEADS), small_ref.dtype)], axis=1)

    return pl.pallas_call(
        body, name="w_in_layout", grid=(D // tr,), in_specs=[pl.BlockSpec((N_DEV, tr, SHARD_IN), lambda i: (0, i, 0))],
        out_specs=[pl.BlockSpec((tr, W_MAIN), lambda i: (i, 0)), pl.BlockSpec((tr, W_SMALL), lambda i: (i, 0))],
        out_shape=[jax.ShapeDtypeStruct((D, W_MAIN), slabs.dtype), jax.ShapeDtypeStruct((D, W_SMALL), slabs.dtype)],
        compiler_params=_cp(("parallel",)))(slabs)


def _w_in_to_slabs(main, small):
    D = main.shape[0]
    tr = _pick(D, (256, 128))
    ranges = [(0, DT_COL0, 0, 0), (DT_COL0, DT_COL0 + SSD_HEADS, 1, 0), (DT_COL0 + SSD_HEADS, GATE_IN_COL0, 0, DT_COL0),
              (GATE_IN_COL0, D_IN, 1, SSD_HEADS)]

    def body(main_ref, small_ref, o_ref):
        srcs = (main_ref, small_ref)
        for d in range(N_DEV):
            lo, hi = d * SHARD_IN, (d + 1) * SHARD_IN
            pieces = []
            for start, stop, which, s0 in ranges:
                a, b = max(lo, start), min(hi, stop)
                if a < b:
                    pieces.append(srcs[which][:, s0 + a - start:s0 + b - start].astype(o_ref.dtype))
            o_ref[d] = pieces[0] if len(pieces) == 1 else jnp.concatenate(pieces, axis=1)

    return pl.pallas_call(
        body, name="w_in_grad_layout", grid=(D // tr,),
        in_specs=[pl.BlockSpec((tr, W_MAIN), lambda i: (i, 0)), pl.BlockSpec((tr, W_SMALL), lambda i: (i, 0))],
        out_specs=pl.BlockSpec((N_DEV, tr, SHARD_IN), lambda i: (0, i, 0)),
        out_shape=jax.ShapeDtypeStruct((N_DEV, D, SHARD_IN), main.dtype), compiler_params=_cp(("parallel",)))(main, small)


QB, KCB, VCB, KSB, VSB, KWB, VWB = 10, 14, 15, 16, 17, 18, 19


def _col256(a, b):
    return a[:, b * 256:(b + 1) * 256]


_EARLY = ["w_in", "cmp_w1_k", "cmp_w1_v"]
_LATE = ["w_out", "w_gate", "w_up", "w_down"]
_FFN = ["w_down", "w_gate", "w_up"]
_MID = ["w_out"]
_LAST = ["cmp_w1_k", "cmp_w1_v", "w_in"]


def _local_step(x, tgt, p, late_weights=None, grads_ready=None):
    S = x.shape[0]
    cos, sin = _rope_tables(S)

    u, rs1 = p["normed_x"] if "normed_x" in p else _rms_fwd(x, p["attn_norm_w"], "attn_norm")
    proj = _mm(u, p["w_main"], "nn", F32, "in_proj", after=p.get("before_in_proj"))
    proj_small = _mm(u, p["w_small"], "nn", F32, "in_proj_small")
    xa = _conv_fwd(proj, p["conv_w"], p["conv_b"])
    y_ssd, y_pre, rs_ssd, hs = _ssd_fwd(proj, proj_small, xa, p["dt_bias"], p["a_log"], p["d_skip"], p["ssd_norm_w"])

    q_rot = _rope([proj], QB, ATT_WIDTH, cos, sin, 1.0, _MXU, "rope_q")
    kv = _kv_prep(proj, cos, sin, _attn_cfg(S, S, "sel")[1])
    rk, rv = _to_rows16(_col256(proj, KCB)), _to_rows16(_col256(proj, VCB))
    k_cmp, hid_k = _compress_fwd(rk, p["cmp_pe_k"], p["cmp_w1_k"], p["cmp_w2_k"])
    v_cmp, hid_v = _compress_fwd(rv, p["cmp_pe_v"], p["cmp_w1_v"], p["cmp_w2_v"])
    n_cmp = k_cmp.shape[1]

    gates = proj_small[:, SSD_HEADS:SSD_HEADS + 3 * N_HEADS].reshape(S, N_KV, GRP, 3).transpose(3, 1, 2, 0)
    o_cmp, lse_cmp, y_att, sel = _attn_fwd(proj, QB, k_cmp, _blocked_t(v_cmp, n_cmp), "cmp", None, gates[0], None, F32,
                                           "attn_cmp_fwd")
    o_sel, lse_sel, y_att = _attn_fwd(q_rot, 0, kv["ks_ext"], kv["vs_t"], "sel", sel, gates[1], y_att, F32, "attn_sel_fwd")
    o_win, lse_win, y_att = _attn_fwd(q_rot, 0, kv["kw"], kv["vw_t"], "win", None, gates[2], y_att, _MXU, "attn_win_fwd")

    if late_weights is not None:
        p = {**p, **late_weights(y_att)}
    mixed = jnp.concatenate([y_ssd, y_att], axis=1)
    h1 = _mm(mixed, p["w_out"], "nn", F32, "out_proj", res=x)
    v, rs_ffn = _rms_fwd(h1, p["ffn_norm_w"], "ffn_norm")
    gt, up, act = _ffn_up(v, p["w_gate"], p["w_up"])
    h2 = _mm(act, p["w_down"], "nn", F32, "ffn_down", res=h1)
    loss, dh2, dh2b, d_final_w = _final_loss(h2, p["final_norm_w"], tgt)

    def ready(names):
        return None if grads_ready is None else grads_ready(names, g)

    g = {"final_norm_w": d_final_w}
    g["w_down"] = _mm(act, dh2b, "tn", _MXU, "dw_down")
    dgt, dup = _ffn_dact(dh2b, p["w_down"], gt, up)
    g["w_gate"] = _mm(v, dgt, "tn", _MXU, "dw_gate")
    g["w_up"] = _mm(v, dup, "tn", _MXU, "dw_up")
    dv = _ffn_dv(dgt, dup, p["w_gate"], p["w_up"], ready(_FFN))
    dh1, dh1b, g["ffn_norm_w"] = _rms_bwd(dv, h1, rs_ffn, p["ffn_norm_w"], dh2, "ffn_norm_bwd")
    g["w_out"] = _mm(mixed, dh1b, "tn", _MXU, "dw_out")
    dmixed = _mm(dh1b, p["w_out"], "nt", F32, "dmixed", after=ready(_MID))

    dz, dxa, ddtr, g["dt_bias"], g["a_log"], g["d_skip"], g["ssd_norm_w"] = _ssd_bwd(
        dmixed, proj, proj_small, xa, y_pre, rs_ssd, hs, p["dt_bias"], p["a_log"], p["d_skip"], p["ssd_norm_w"])
    dxbc, g["conv_w"], g["conv_b"] = _conv_bwd(proj, p["conv_w"], p["conv_b"], dxa)

    dyb = SSD_WIDTH // (GRP * HD)
    dq_cmp, dk_cmp, dv_cmp, dg_cmp = _attn_bwd(proj, QB, k_cmp, _blocked_t(k_cmp, n_cmp), v_cmp, o_cmp, lse_cmp, dmixed, dyb,
                                               gates[0], "cmp", None, "attn_cmp_bwd")
    dq_sel, dks, dvs, dg_sel = _attn_bwd(q_rot, 0, kv["ks_ext"], kv["ks_t"], kv["vs"], o_sel, lse_sel, dmixed, dyb, gates[1], "sel",
                                         sel, "attn_sel_bwd")
    dq_win, dkw, dvw, dg_win = _attn_bwd(q_rot, 0, kv["kw"], kv["kw_t"], kv["vw"], o_win, lse_win, dmixed, dyb, gates[2], "win", None,
                                         "attn_win_bwd")
    dgate = jnp.stack([dg_cmp, dg_sel, dg_win]).transpose(3, 1, 2, 0).reshape(S, 3 * N_HEADS)
    drk, g["cmp_w1_k"], g["cmp_w2_k"], g["cmp_pe_k"] = _compress_bwd(rk, p["cmp_pe_k"], p["cmp_w1_k"], p["cmp_w2_k"], hid_k, dk_cmp)
    drv, g["cmp_w1_v"], g["cmp_w2_v"], g["cmp_pe_v"] = _compress_bwd(rv, p["cmp_pe_v"], p["cmp_w1_v"], p["cmp_w2_v"], hid_v, dv_cmp)
    dq = _rope([dq_sel, dq_win], 0, ATT_WIDTH, cos, sin, -1.0, _MXU, "rope_dq", extra=(dq_cmp, 0))
    dkv = _dkv_post(dks, dvs, dkw, dvw, cos, sin)
    dproj = jnp.concatenate([dz, dxbc, dq] + [t.astype(_MXU) for t in (_from_rows16(drk), _from_rows16(drv))] + [dkv], axis=1)
    dsmall = jnp.concatenate([ddtr, dgate, jnp.zeros((S, W_SMALL - SSD_HEADS - 3 * N_HEADS), F32)], axis=1).astype(_MXU)
    g["w_main"] = _mm(u, dproj, "tn", _MXU, "dw_in")
    g["w_small"] = _mm(u, dsmall, "tn", F32, "dw_in_small")
    du = _mm(dproj, p["w_main"], "nt", F32, "du_main", after=ready(_LAST))
    du = _mm(dsmall, p["w_small"], "nt", F32, "du_small", res=du)
    grad_x, _, g["attn_norm_w"] = _rms_bwd(du, x, rs1, p["attn_norm_w"], dh1, "attn_norm_bwd")
    return loss, grad_x, g


MESH_ID = pl.DeviceIdType.MESH


def _my_coords():
    return lax.axis_index("x"), lax.axis_index("y"), lax.axis_index("c")


def _flat_id(px, py, pc):
    return 4 * px + 2 * py + pc


def _peer(k):
    mx, my, mc = _my_coords()
    return (1 - mx if k & 4 else mx, 1 - my if k & 2 else my, 1 - mc if k & 1 else mc)


def _exchange(arrs, scatter, name, after=()):
    n, na = len(arrs), len(after)
    scatter = [scatter] * n if isinstance(scatter, bool) else list(scatter)

    def body(*refs):
        ins, outs = refs[:n], refs[n + na:2 * n + na]
        send_sems, recv_sems, local_sems = refs[2 * n + na:]
        me = _flat_id(*_my_coords())
        copies = []
        for i in range(n):
            src_me = ins[i].at[me] if scatter[i] else ins[i]
            local = pltpu.make_async_copy(src_me, outs[i].at[me], local_sems.at[i])
            local.start()
            copies.append(local)
        for k in range(1, N_DEV):
            peer = _peer(k)
            for i in range(n):
                src = ins[i].at[_flat_id(*peer)] if scatter[i] else ins[i]
                cp = pltpu.make_async_remote_copy(src_ref=src, dst_ref=outs[i].at[me], send_sem=send_sems.at[i * 7 + k - 1],
                                                  recv_sem=recv_sems.at[i * 7 + k - 1], device_id=peer, device_id_type=MESH_ID)
                cp.start()
                copies.append(cp)
        for cp in copies:
            cp.wait()

    any_spec = pl.BlockSpec(memory_space=pl.ANY)
    out_shape = [jax.ShapeDtypeStruct(a.shape if sc else (N_DEV,) + a.shape, a.dtype) for a, sc in zip(arrs, scatter)]
    return pl.pallas_call(
        body, name=name, in_specs=[any_spec] * (n + na), out_specs=[any_spec] * n, out_shape=out_shape,
        scratch_shapes=[pltpu.SemaphoreType.DMA((n * 7,)), pltpu.SemaphoreType.DMA((n * 7,)), pltpu.SemaphoreType.DMA((n,))],
        compiler_params=pltpu.CompilerParams(has_side_effects=True))(*arrs, *after)


def _gather_two_level(arrs, name, xin, norm_w, to_round):
    n, m = len(arrs), len(to_round)
    S, D = xin.shape
    tr = _pick(S, (512, 256, 128))
    round_shapes = [t.shape[1:] for t in to_round]
    round_rows = [_largest_tile(r, 256) if r % 128 == 0 else r // 4 for r, _ in round_shapes]

    def move(src, dst, sem):
        cp = pltpu.make_async_copy(src, dst, sem.at[0])
        cp.start()
        cp.wait()

    def round_array(src_hbm, dst_hbm, fbuf, bbuf, rows_per, sem):
        def tile(i, carry):
            rows = pl.ds(pl.multiple_of(i * rows_per, 16), rows_per)
            move(src_hbm.at[rows], fbuf, sem)
            bbuf[...] = fbuf[...].astype(bbuf.dtype)
            move(bbuf, dst_hbm.at[rows], sem)
            return carry

        lax.fori_loop(0, src_hbm.shape[0] // rows_per, tile, 0)

    def rms_rows(x_hbm, w_ref, u_hbm, rs_hbm, xbuf, ubuf, rsbuf, sem):
        def tile(i, carry):
            rows = pl.ds(pl.multiple_of(i * tr, tr), tr)
            move(x_hbm.at[rows], xbuf, sem)
            xv = xbuf[...]
            rs = lax.rsqrt(jnp.mean(xv * xv, axis=-1, keepdims=True) + EPS)
            ubuf[...] = ((xv * rs) * w_ref[...]).astype(ubuf.dtype)
            rsbuf[...] = rs
            move(ubuf, u_hbm.at[rows], sem)
            move(rsbuf, rs_hbm.at[rows], sem)
            return carry

        lax.fori_loop(0, S // tr, tile, 0)

    def body(*refs):
        ins, x_hbm, w_ref, f32_srcs = refs[:n], refs[n], refs[n + 1], refs[n + 2:n + 2 + m]
        o0 = n + 2 + m
        outs, u_hbm, rs_hbm, rounded = refs[o0:o0 + n], refs[o0 + n], refs[o0 + n + 1], refs[o0 + n + 2:o0 + n + 2 + m]
        s0 = o0 + n + 2 + m
        send_sems, recv_sems, local_sems, xbuf, ubuf, rsbuf, norm_sem = refs[s0:s0 + 7]
        round_bufs = refs[s0 + 7:]
        x, y, c = _my_coords()
        me, sibling = (x, y, c), (x, y, 1 - c)
        chips = [(1 - x, y), (x, 1 - y), (1 - x, 1 - y)]

        def copy(i, k, block, to, src=None):
            slot = outs[i].at[_flat_id(*block)]
            return pltpu.make_async_remote_copy(src_ref=slot if src is None else src, dst_ref=slot, send_sem=send_sems.at[i * 7 + k],
                                                recv_sem=recv_sems.at[i * 7 + k], device_id=to, device_id_type=MESH_ID)

        mine = [pltpu.make_async_copy(ins[i], outs[i].at[_flat_id(*me)], local_sems.at[i]) for i in range(n)]
        for cp in mine:
            cp.start()
        first = []
        for j, chip in enumerate(chips):
            first += [copy(i, 1 + j, me, (*chip, c), src=ins[i]) for i in range(n)]
        first += [copy(i, 0, me, sibling, src=ins[i]) for i in range(n)]
        for cp in first:
            cp.start()
        rms_rows(x_hbm, w_ref, u_hbm, rs_hbm, xbuf, ubuf, rsbuf, norm_sem)
        for j in range(m):
            round_array(f32_srcs[j].at[0], rounded[j], round_bufs[2 * j], round_bufs[2 * j + 1], round_rows[j], norm_sem)
        passed = []
        for j, chip in enumerate(chips):
            for i in range(n):
                copy(i, 1 + j, (*chip, c), me).wait_recv()
                passed.append(copy(i, 4 + j, (*chip, c), sibling))
                passed[-1].start()
        for i in range(n):
            copy(i, 0, sibling, me).wait_recv()
        for j, chip in enumerate(chips):
            for i in range(n):
                copy(i, 4 + j, (*chip, 1 - c), me).wait_recv()
        for cp in first + passed:
            cp.wait_send()
        for cp in mine:
            cp.wait()

    any_spec = pl.BlockSpec(memory_space=pl.ANY)
    round_scratch = [s for (_, c), r in zip(round_shapes, round_rows) for s in (pltpu.VMEM((r, c), F32), pltpu.VMEM((r, c), _MXU))]
    res = pl.pallas_call(
        body, name=name, in_specs=[any_spec] * (n + 1) + [pl.BlockSpec(memory_space=pltpu.VMEM)] + [any_spec] * m,
        out_specs=[any_spec] * (n + 2 + m),
        out_shape=[jax.ShapeDtypeStruct((N_DEV,) + a.shape, a.dtype) for a in arrs]
        + [jax.ShapeDtypeStruct((S, D), _MXU), jax.ShapeDtypeStruct((S, 1), F32)] + [jax.ShapeDtypeStruct(s, _MXU) for s in round_shapes],
        scratch_shapes=[pltpu.SemaphoreType.DMA((n * 7,)), pltpu.SemaphoreType.DMA((n * 7,)), pltpu.SemaphoreType.DMA((n,)),
                        pltpu.VMEM((tr, D), F32), pltpu.VMEM((tr, D), _MXU), pltpu.VMEM((tr, 1), F32), pltpu.SemaphoreType.DMA((1,))]
        + round_scratch,
        compiler_params=pltpu.CompilerParams(has_side_effects=True, vmem_limit_bytes=VMEM_LIMIT))(*arrs, xin, norm_w, *to_round)
    return res[:n], res[n], res[n + 1], res[n + 2:]


_HBM = pl.BlockSpec(memory_space=pltpu.HBM)
_SEM = pl.BlockSpec(memory_space=pltpu.SEMAPHORE)
_EFFECT = pltpu.SideEffectType.DATAFLOW_SIDE_EFFECTING


def _split_copies(ins, lands, send_sems, recv_sems, own_sems, scatter):
    me = _flat_id(*_my_coords())
    remote = []
    for k in range(1, N_DEV):
        peer = _peer(k)
        for i in range(len(ins)):
            src = ins[i].at[_flat_id(*peer)] if scatter else ins[i]
            remote.append(pltpu.make_async_remote_copy(src_ref=src, dst_ref=lands[i].at[me], send_sem=send_sems.at[i * 7 + k - 1],
                                                       recv_sem=recv_sems.at[i * 7 + k - 1], device_id=peer, device_id_type=MESH_ID))
    own = [pltpu.make_async_copy(ins[i].at[me] if scatter else ins[i], lands[i].at[me], own_sems.at[i]) for i in range(len(ins))]
    return remote, own


def _split_start(arrs, scatter, name, after=()):
    n, na = len(arrs), len(after)

    def body(*refs):
        remote, own = _split_copies(refs[:n], refs[n:2 * n], refs[2 * n + na], refs[2 * n + na + 1], refs[2 * n + na + 2], scatter)
        for cp in remote + own:
            cp.start()
        refs[-1][...] = jnp.zeros_like(refs[-1])

    land_shapes = [a.shape if scatter else (N_DEV,) + a.shape for a in arrs]
    out_shape = ((pltpu.SemaphoreType.DMA((n * 7,)), pltpu.SemaphoreType.DMA((n * 7,)), pltpu.SemaphoreType.DMA((n,)))
                 + tuple(pltpu.HBM(a.shape, a.dtype) for a in arrs) + tuple(pltpu.HBM(s, a.dtype) for s, a in zip(land_shapes, arrs))
                 + (jax.ShapeDtypeStruct((8, 128), F32),))
    operands = ([pltpu.with_memory_space_constraint(a, pltpu.HBM) for a in arrs]
                + [pltpu.with_memory_space_constraint(lax.empty(s, a.dtype), pltpu.HBM) for s, a in zip(land_shapes, arrs)])
    res = pl.pallas_call(
        body, name=name, out_shape=out_shape, in_specs=[_HBM] * (2 * n) + [pl.BlockSpec(memory_space=pl.ANY)] * na,
        out_specs=(_SEM, _SEM, _SEM) + (_HBM,) * (2 * n) + (pl.BlockSpec(memory_space=pltpu.VMEM),),
        input_output_aliases={i: 3 + i for i in range(2 * n)},
        compiler_params=pltpu.CompilerParams(has_side_effects=_EFFECT))(*operands, *after)
    return dict(send=res[0], recv=res[1], own=res[2], ins=list(res[3:3 + n]), lands=list(res[3 + n:3 + 2 * n]), token=res[-1])


def _split_wait(st, scatter, after, name):
    n = len(st["ins"])

    def body(*refs):
        remote, own = _split_copies(refs[:n], refs[n:2 * n], refs[2 * n], refs[2 * n + 1], refs[2 * n + 2], scatter)
        for cp in remote:
            cp.wait_send()
            cp.wait_recv()
        for cp in own:
            cp.wait()

    arrs = st["ins"] + st["lands"]
    res = pl.pallas_call(
        body, name=name, out_shape=tuple(pltpu.HBM(a.shape, a.dtype) for a in arrs),
        in_specs=[_HBM] * (2 * n) + [_SEM, _SEM, _SEM] + [pl.BlockSpec(memory_space=pl.ANY)] * len(after), out_specs=(_HBM,) * (2 * n),
        input_output_aliases={i: i for i in range(2 * n)},
        compiler_params=pltpu.CompilerParams(has_side_effects=_EFFECT))(*arrs, st["send"], st["recv"], st["own"], *after)
    return list(res[n:])


def _adam_step(p_ref, w_ref, m_ref, v_ref, g_ref, d_ref, nm_ref, nv_ref):
    g = p_ref[0].astype(F32)
    for j in range(1, p_ref.shape[0]):
        g = g + p_ref[j].astype(F32)
    g_ref[...] = g
    nm = ADAM_B1 * m_ref[...] + (1.0 - ADAM_B1) * g
    nv = ADAM_B2 * v_ref[...] + (1.0 - ADAM_B2) * (g * g)
    nm_ref[...] = nm
    nv_ref[...] = nv
    m_hat = nm / (1.0 - ADAM_B1 ** ADAM_STEP)
    v_hat = nv / (1.0 - ADAM_B2 ** ADAM_STEP)
    d_ref[...] = -ADAM_LR * (m_hat / (jnp.sqrt(v_hat) + ADAM_EPS) + ADAM_WD * w_ref[...])


def _adam_sum(parts, w, m, v, name):
    P, R, C = parts.shape
    tr = _pick(R, (256, 128, 64, 32, 8)) if C <= 1024 else _pick(R, (128, 64, 32, 8))
    blk = pl.BlockSpec((tr, C), lambda i: (i, 0))
    return pl.pallas_call(
        functools.partial(_adam_step), name=name, grid=(R // tr,),
        in_specs=[pl.BlockSpec((P, tr, C), lambda i: (0, i, 0)), blk, blk, blk],
        out_specs=[blk] * 4, out_shape=[jax.ShapeDtypeStruct((R, C), F32)] * 4, compiler_params=_cp(("parallel",)))(parts, w, m, v)


def _adam_small(loss_parts, parts, ws, ms, vs):
    n = len(parts)

    def body(*refs):
        loss_ref, ins, outs, total_ref = refs[0], refs[1:4 * n + 1], refs[4 * n + 1:-1], refs[-1]
        for i in range(n):
            _adam_step(ins[i], ins[n + i], ins[2 * n + i], ins[3 * n + i], *outs[4 * i:4 * i + 4])
        total = loss_ref[0]
        for d in range(1, N_DEV):
            total = total + loss_ref[d]
        total_ref[...] = total

    out_shape = [jax.ShapeDtypeStruct(w.shape, F32) for w in ws for _ in range(4)] + [jax.ShapeDtypeStruct(loss_parts.shape[1:], F32)]
    res = pl.pallas_call(body, name="adam_small", out_shape=out_shape)(loss_parts, *parts, *ws, *ms, *vs)
    return res[-1], [tuple(res[4 * i:4 * i + 4]) for i in range(n)]


_WEIGHTS = ["attn_norm_w", "w_in", "conv_w", "conv_b", "dt_bias", "a_log", "d_skip", "ssd_norm_w", "cmp_w1_k", "cmp_w2_k",
            "cmp_w1_v", "cmp_w2_v", "cmp_pe_k", "cmp_pe_v", "w_out", "ffn_norm_w", "w_gate", "w_up", "w_down", "final_norm_w"]
_BIG = ["w_in", "w_gate", "w_up", "w_down", "w_out", "cmp_w1_k", "cmp_w1_v"]
_COL_SHARDED = ("w_in", "w_gate", "w_up")
_REPLICATED = ["attn_norm_w", "conv_b", "dt_bias", "a_log", "d_skip", "ssd_norm_w", "cmp_pe_k", "cmp_pe_v", "ffn_norm_w",
               "final_norm_w"]
_SMALL_SHARDED = ["conv_w", "cmp_w2_k", "cmp_w2_v"]


def _cols_to_slabs(g):
    R = g.shape[0]
    return g.reshape(R, N_DEV, -1).transpose(1, 0, 2)


def _slabs_to_cols(s):
    return s.transpose(1, 0, 2).reshape(s.shape[1], -1)


def kernel(x, attn_norm_w, w_in, conv_w, conv_b, dt_bias, a_log, d_skip, ssd_norm_w, cmp_w1_k, cmp_w2_k, cmp_w1_v, cmp_w2_v, cmp_pe_k, cmp_pe_v, w_out, ffn_norm_w, w_gate, w_up, w_down, final_norm_w, loss_target, m_attn_norm_w, m_w_in, m_conv_w, m_conv_b, m_dt_bias, m_a_log, m_d_skip, m_ssd_norm_w, m_cmp_w1_k, m_cmp_w2_k, m_cmp_w1_v, m_cmp_w2_v, m_cmp_pe_k, m_cmp_pe_v, m_w_out, m_ffn_norm_w, m_w_gate, m_w_up, m_w_down, m_final_norm_w, v_attn_norm_w, v_w_in, v_conv_w, v_conv_b, v_dt_bias, v_a_log, v_d_skip, v_ssd_norm_w, v_cmp_w1_k, v_cmp_w2_k, v_cmp_w1_v, v_cmp_w2_v, v_cmp_pe_k, v_cmp_pe_v, v_w_out, v_ffn_norm_w, v_w_gate, v_w_up, v_w_down, v_final_norm_w):
    a = dict(locals())

    early = [a[n][0].astype(_MXU) for n in _EARLY] + [cmp_w2_k[0], cmp_w2_v[0], conv_w[0]]
    got, normed_x, rs_x, late = _gather_two_level(early, "gather_early", x[0], attn_norm_w, [a[n] for n in _LATE])
    st_late = _split_start(list(late), False, "gather_late_start", after=(got[0],))

    def assemble(n, t):
        return _cols_from_slabs(t) if n in _COL_SHARDED else t.reshape(-1, t.shape[-1])

    p = dict(attn_norm_w=attn_norm_w, conv_b=conv_b, dt_bias=dt_bias, a_log=a_log, d_skip=d_skip, ssd_norm_w=ssd_norm_w,
             cmp_pe_k=cmp_pe_k.reshape(1, -1), cmp_pe_v=cmp_pe_v.reshape(1, -1), ffn_norm_w=ffn_norm_w,
             final_norm_w=final_norm_w.reshape(1, -1))

    w_main, w_small = _w_in_from_slabs(got[0])
    p.update(normed_x=(normed_x, rs_x), before_in_proj=st_late["token"],
             w_main=w_main, w_small=w_small, cmp_w1_k=assemble("cmp_w1_k", got[1]), cmp_w1_v=assemble("cmp_w1_v", got[2]),
             cmp_w2_k=assemble("cmp_w2_k", got[3]).astype(_MXU), cmp_w2_v=assemble("cmp_w2_v", got[4]).astype(_MXU),
             conv_w=_slabs_to_cols(got[5]))

    def late_weights(after):
        got_late = _split_wait(st_late, False, (after,), "gather_late_wait")
        return {n: assemble(n, t) for n, t in zip(_LATE, got_late)}

    def slabs_of(g, n):
        if n == "w_in":
            return _w_in_to_slabs(g["w_main"], g["w_small"])
        return _slabs_from_cols(g[n]) if n in _COL_SHARDED else g[n].reshape(N_DEV, -1, g[n].shape[-1])

    started = []

    def grads_ready(names, g):
        started.append((names, _split_start([slabs_of(g, n) for n in names], True, "scatter_grads_start_%d" % len(started))))
        return started[-1][1]["token"]

    loss_part, grad_x, g = _local_step(x[0], loss_target[0], p, late_weights, grads_ready)

    out, after = {}, (started[-1][1]["token"],)
    for i, (names, st) in enumerate(started):
        if i == len(started) - 1:
            after = after + (grad_x,)
        received = _split_wait(st, True, after, "scatter_grads_wait_%d" % i)
        for n, parts in zip(names, received):
            out[n] = _adam_sum(parts, a[n][0], a["m_" + n][0], a["v_" + n][0], "adam_" + n)
        after = (out[names[-1]][0],)

    small_names = _REPLICATED + _SMALL_SHARDED
    partials = [g[n] for n in _REPLICATED] + [_cols_to_slabs(g["conv_w"])] + [
        g[n].reshape(N_DEV, -1, g[n].shape[-1]) for n in ("cmp_w2_k", "cmp_w2_v")]
    gathered = _exchange([loss_part] + partials, [False] * (1 + len(_REPLICATED)) + [True] * len(_SMALL_SHARDED),
                         "exchange_small_grads", after=(received[0],))
    shapes2d = [t.shape[1:] for t in gathered[1:]]
    loss, res_small = _adam_small(gathered[0], gathered[1:],
                                  *[[a[pre + n].reshape(s) for n, s in zip(small_names, shapes2d)] for pre in ("", "m_", "v_")])
    for n, r in zip(small_names, res_small):
        out[n] = r

    outs = [loss[0, 0], grad_x[None]]
    for j in range(4):
        for n in _WEIGHTS:
            outs.append(out[n][j].reshape(a[n].shape))
    return tuple(outs)
```
